```python
import math
import jax, jax.numpy as jnp
from jax import lax
import numpy as np

D_MODEL = 1024
BATCH = 8
SEQ = 4096
DEPTH = 1

HEAD_DIM = 64
HEADS_PER_GROUP = 8
ATTN_GROUPS = ((128, 1), (512, 4), (2048, 16))
N_GROUPS = len(ATTN_GROUPS)
N_ATTN_HEADS = N_GROUPS * HEADS_PER_GROUP
ATTN_WIDTH = HEADS_PER_GROUP * HEAD_DIM
QKV_WIDTH = N_GROUPS * 3 * ATTN_WIDTH
BLOCK = 128
POOL_WINDOWS = (2, 4, 8, 16)
POOL_GROUPS = len(POOL_WINDOWS)
POOL_WIDTH = D_MODEL // 2
PGW = POOL_WIDTH // POOL_GROUPS
NUM_BUCKETS = 32
MAX_DISTANCE = 2048
EPS = 1e-6
SPLIT_SIZES = (QKV_WIDTH, ATTN_WIDTH, POOL_WIDTH, POOL_WIDTH, D_MODEL, D_MODEL)
SPLIT_POINTS = (QKV_WIDTH,
                QKV_WIDTH + ATTN_WIDTH,
                QKV_WIDTH + ATTN_WIDTH + POOL_WIDTH,
                QKV_WIDTH + ATTN_WIDTH + 2 * POOL_WIDTH,
                QKV_WIDTH + ATTN_WIDTH + 2 * POOL_WIDTH + D_MODEL)
IN_WIDTH = QKV_WIDTH + ATTN_WIDTH + 2 * POOL_WIDTH + 2 * D_MODEL

kernel_name = "hybrid_dilated_attn_pool_gated_block"


def rmsnorm(x, g):
    xf = x.astype(jnp.float32)
    y = xf * lax.rsqrt(jnp.mean(xf * xf, axis=-1, keepdims=True) + EPS)
    return (y * g.astype(jnp.float32)).astype(x.dtype)


def t5_bucket(n):
    max_exact = NUM_BUCKETS // 2
    nf = jnp.maximum(n, 1).astype(jnp.float32)
    large = max_exact + (jnp.log(nf / max_exact) / math.log(MAX_DISTANCE / max_exact)
                         * (NUM_BUCKETS - max_exact)).astype(jnp.int32)
    large = jnp.minimum(large, NUM_BUCKETS - 1)
    return jnp.where(n < max_exact, n, large)


def to_sub(t, dil):
    B, S = t.shape[:2]
    L = S // dil
    t = t.reshape((B, L, dil) + t.shape[2:])
    t = jnp.moveaxis(t, 2, 1)
    return t.reshape((B * dil, L) + t.shape[3:])


def from_sub(t, B, dil):
    L = t.shape[1]
    t = t.reshape((B, dil, L) + t.shape[2:])
    t = jnp.moveaxis(t, 1, 2)
    return t.reshape((B, L * dil) + t.shape[3:])


def dilated_window_attention(q, k, v, dil, n_back, bias_g):
    B, S, H, Dh = q.shape
    L = S // dil
    nb = -(-L // BLOCK)
    pad = nb * BLOCK - L
    Bd = B * dil

    def sub(t):
        return jnp.pad(to_sub(t, dil), ((0, 0), (0, pad), (0, 0), (0, 0)))

    def band(t):
        tp = jnp.pad(t, ((0, 0), (BLOCK, 0), (0, 0), (0, 0)))
        prev = tp[:, :-BLOCK].reshape(Bd, nb, BLOCK, H, Dh)
        cur = t.reshape(Bd, nb, BLOCK, H, Dh)
        return jnp.concatenate([prev, cur], axis=2)

    qb = sub(q).reshape(Bd, nb, BLOCK, H, Dh)
    kb = band(sub(k))
    vb = band(sub(v))

    i = jnp.arange(BLOCK)[:, None]
    j = jnp.arange(2 * BLOCK)[None, :]
    dist = BLOCK + i - j
    band_ok = (dist >= 0) & (dist <= n_back)
    key_ok = (jnp.arange(nb)[:, None, None] * BLOCK - BLOCK + j[None]) >= 0
    mask = band_ok[None] & key_ok
    bucket = t5_bucket(jnp.clip(dist, 0, n_back) * dil)
    bias = jnp.transpose(bias_g[bucket].astype(jnp.float32), (2, 0, 1))

    logits = jnp.einsum('znqhd,znkhd->znhqk', qb, kb).astype(jnp.float32) * (HEAD_DIM ** -0.5)
    logits = jnp.where(mask[None, :, None], logits + bias[None, None], -jnp.inf)
    m = jnp.max(logits, axis=-1, keepdims=True)
    p = jnp.exp(logits - m)
    denom = jnp.sum(p, axis=-1)
    o = jnp.einsum('znhqk,znkhd->znqhd', p.astype(vb.dtype), vb).astype(jnp.float32)
    o = o / jnp.moveaxis(denom, 2, 3)[..., None]
    lse = jnp.moveaxis(m[..., 0] + jnp.log(denom), 2, 3)

    o = o.reshape(Bd, nb * BLOCK, H, Dh)[:, :L]
    lse = lse.reshape(Bd, nb * BLOCK, H)[:, :L]
    return from_sub(o, B, dil), from_sub(lse, B, dil)


def multiscale_pool(u):
    B, S, C = u.shape
    uf = u.astype(jnp.float32)
    csp = jnp.pad(jnp.cumsum(uf, axis=1), ((0, 0), (1, 0), (0, 0)))
    t = jnp.arange(S)
    outs = []
    for g, win in enumerate(POOL_WINDOWS):
        cg = csp[:, :, g * PGW:(g + 1) * PGW]
        lo = jnp.maximum(t + 1 - win, 0)
        s = cg[:, 1:] - cg[:, lo]
        cnt = jnp.minimum(t + 1, win).astype(jnp.float32)
        outs.append(s / cnt[None, :, None])
    return jnp.concatenate(outs, axis=-1) - uf


def _fwd_setup_inputs(seed: int = 0) -> dict:
    key = jax.random.key(seed)
    ks = jax.random.split(key, 14)
    f32 = jnp.float32
    nrm = lambda k, shape, s: (jax.random.normal(k, shape, f32) * s).astype(f32)
    return {
        "x": nrm(ks[0], (BATCH, SEQ, D_MODEL), 1.0),
        "c": nrm(ks[1], (BATCH, D_MODEL), 1.0),
        "norm_g": 1.0 + nrm(ks[2], (DEPTH, D_MODEL), 0.05),
        "w_ada": nrm(ks[3], (DEPTH, D_MODEL, 3 * D_MODEL), 0.5 * D_MODEL ** -0.5),
        "b_ada": nrm(ks[4], (DEPTH, 3 * D_MODEL), 0.01),
        "w_in": nrm(ks[5], (DEPTH, D_MODEL, IN_WIDTH), D_MODEL ** -0.5),
        "pool_w": nrm(ks[6], (DEPTH, POOL_GROUPS, PGW, PGW), PGW ** -0.5),
        "pool_scale": 1.0 + nrm(ks[7], (DEPTH, POOL_WIDTH), 0.1),
        "w_attn_br": nrm(ks[8], (DEPTH, ATTN_WIDTH, D_MODEL), ATTN_WIDTH ** -0.5),
        "w_pool_br": nrm(ks[9], (DEPTH, POOL_WIDTH, D_MODEL), POOL_WIDTH ** -0.5),
        "w_out": nrm(ks[10], (DEPTH, D_MODEL, D_MODEL), D_MODEL ** -0.5),
        "rel_bias": nrm(ks[11], (NUM_BUCKETS, N_ATTN_HEADS), 0.5),
        "final_g": 1.0 + nrm(ks[12], (D_MODEL,), 0.05),
    }


def _fwd_reference(x, c, norm_g, w_ada, b_ada, w_in, pool_w, pool_scale, w_attn_br, w_pool_br, w_out, rel_bias, final_g):
    B, S, D = x.shape
    for l in range(DEPTH):
        mod = c @ w_ada[l] + b_ada[l]
        shift, scale, gate = jnp.split(mod, 3, axis=-1)
        h = rmsnorm(x, norm_g[l]) * (1.0 + scale[:, None]) + shift[:, None]

        proj = h @ w_in[l]
        qkv, z_attn, u_pool, z_pool, g_attn, g_pool = jnp.split(proj, SPLIT_POINTS, axis=-1)
        qkv = qkv.reshape(B, S, N_GROUPS, 3, HEADS_PER_GROUP, HEAD_DIM)

        outs, lses = [], []
        for gi, (win, dil) in enumerate(ATTN_GROUPS):
            bias_g = rel_bias[:, gi * HEADS_PER_GROUP:(gi + 1) * HEADS_PER_GROUP]
            o, lse = dilated_window_attention(qkv[:, :, gi, 0], qkv[:, :, gi, 1], qkv[:, :, gi, 2],
                                              dil, win // dil, bias_g)
            outs.append(o)
            lses.append(lse)
        wts = jax.nn.softmax(jnp.stack(lses, axis=0), axis=0)
        attn = jnp.sum(wts[..., None] * jnp.stack(outs, axis=0), axis=0)
        attn = attn.reshape(B, S, ATTN_WIDTH).astype(x.dtype)
        y_attn = (attn * jax.nn.silu(z_attn)) @ w_attn_br[l]

        pooled = multiscale_pool(u_pool).reshape(B, S, POOL_GROUPS, PGW)
        mixed = jnp.einsum('bsgc,gce->bsge', pooled, pool_w[l].astype(jnp.float32))
        mixed = (mixed.reshape(B, S, POOL_WIDTH) * pool_scale[l]).astype(x.dtype)
        y_pool = (mixed * jax.nn.silu(z_pool)) @ w_pool_br[l]

        merged = jax.nn.sigmoid(g_attn) * y_attn + jax.nn.sigmoid(g_pool) * y_pool
        x = x + gate[:, None] * (merged @ w_out[l])
    return rmsnorm(x, final_g)


import jax as _jax
import jax.numpy as _jnp

TWIN_FORMAT = 'train_step'
FWD_PARAMS = ['x', 'c', 'norm_g', 'w_ada', 'b_ada', 'w_in', 'pool_w', 'pool_scale', 'w_attn_br', 'w_pool_br', 'w_out', 'rel_bias', 'final_g']
TWIN_WEIGHTS = ['norm_g', 'w_ada', 'b_ada', 'w_in', 'pool_w', 'pool_scale', 'w_attn_br', 'w_pool_br', 'w_out', 'rel_bias', 'final_g']
TWIN_DIFF_INPUT = 'x'
TWIN_INPUTS = ['x', 'c', 'norm_g', 'w_ada', 'b_ada', 'w_in', 'pool_w', 'pool_scale', 'w_attn_br', 'w_pool_br', 'w_out', 'rel_bias', 'final_g', 'loss_target', 'm_norm_g', 'm_w_ada', 'm_b_ada', 'm_w_in', 'm_pool_w', 'm_pool_scale', 'm_w_attn_br', 'm_w_pool_br', 'm_w_out', 'm_rel_bias', 'm_final_g', 'v_norm_g', 'v_w_ada', 'v_b_ada', 'v_w_in', 'v_pool_w', 'v_pool_scale', 'v_w_attn_br', 'v_w_pool_br', 'v_w_out', 'v_rel_bias', 'v_final_g']
TWIN_OUTPUTS = ['loss', 'grad_x', 'grad_norm_g', 'grad_w_ada', 'grad_b_ada', 'grad_w_in', 'grad_pool_w', 'grad_pool_scale', 'grad_w_attn_br', 'grad_w_pool_br', 'grad_w_out', 'grad_rel_bias', 'grad_final_g', 'delta_norm_g', 'delta_w_ada', 'delta_b_ada', 'delta_w_in', 'delta_pool_w', 'delta_pool_scale', 'delta_w_attn_br', 'delta_w_pool_br', 'delta_w_out', 'delta_rel_bias', 'delta_final_g', 'new_m_norm_g', 'new_m_w_ada', 'new_m_b_ada', 'new_m_w_in', 'new_m_pool_w', 'new_m_pool_scale', 'new_m_w_attn_br', 'new_m_w_pool_br', 'new_m_w_out', 'new_m_rel_bias', 'new_m_final_g', 'new_v_norm_g', 'new_v_w_ada', 'new_v_b_ada', 'new_v_w_in', 'new_v_pool_w', 'new_v_pool_scale', 'new_v_w_attn_br', 'new_v_w_pool_br', 'new_v_w_out', 'new_v_rel_bias', 'new_v_final_g']
TWIN_LEAF_KINDS = {'loss': 'loss', 'grad_x': 'grad_x', 'grad_norm_g': 'grad_w', 'grad_w_ada': 'grad_w', 'grad_b_ada': 'grad_w', 'grad_w_in': 'grad_w', 'grad_pool_w': 'grad_w', 'grad_pool_scale': 'grad_w', 'grad_w_attn_br': 'grad_w', 'grad_w_pool_br': 'grad_w', 'grad_w_out': 'grad_w', 'grad_rel_bias': 'grad_w', 'grad_final_g': 'grad_w', 'delta_norm_g': 'delta_w', 'delta_w_ada': 'delta_w', 'delta_b_ada': 'delta_w', 'delta_w_in': 'delta_w', 'delta_pool_w': 'delta_w', 'delta_pool_scale': 'delta_w', 'delta_w_attn_br': 'delta_w', 'delta_w_pool_br': 'delta_w', 'delta_w_out': 'delta_w', 'delta_rel_bias': 'delta_w', 'delta_final_g': 'delta_w', 'new_m_norm_g': 'new_m', 'new_m_w_ada': 'new_m', 'new_m_b_ada': 'new_m', 'new_m_w_in': 'new_m', 'new_m_pool_w': 'new_m', 'new_m_pool_scale': 'new_m', 'new_m_w_attn_br': 'new_m', 'new_m_w_pool_br': 'new_m', 'new_m_w_out': 'new_m', 'new_m_rel_bias': 'new_m', 'new_m_final_g': 'new_m', 'new_v_norm_g': 'new_v', 'new_v_w_ada': 'new_v', 'new_v_b_ada': 'new_v', 'new_v_w_in': 'new_v', 'new_v_pool_w': 'new_v', 'new_v_pool_scale': 'new_v', 'new_v_w_attn_br': 'new_v', 'new_v_w_pool_br': 'new_v', 'new_v_w_out': 'new_v', 'new_v_rel_bias': 'new_v', 'new_v_final_g': 'new_v'}


def _forward(args):
    return _fwd_reference(*[args[k] for k in FWD_PARAMS])


def _output_shape():
    out = _jax.eval_shape(lambda: _forward(_fwd_setup_inputs(0)))
    return out.shape, out.dtype

N_MICROBATCH = 1
ADAM_LR = 0.001
ADAM_B1 = 0.9
ADAM_B2 = 0.999
ADAM_EPS = 1e-08
ADAM_WD = 0.01
ADAM_STEP = 10
PER_EXAMPLE_BATCH_AXIS = {'x': 0, 'c': 0, 'loss_target': 0}
SHARED_INPUTS = []
_WEIGHT_DTYPES = {'norm_g': _jnp.float32, 'w_ada': _jnp.float32, 'b_ada': _jnp.float32, 'w_in': _jnp.float32, 'pool_w': _jnp.float32, 'pool_scale': _jnp.float32, 'w_attn_br': _jnp.float32, 'w_pool_br': _jnp.float32, 'w_out': _jnp.float32, 'rel_bias': _jnp.float32, 'final_g': _jnp.float32}
MOMENT_SCALE = {'norm_g': 6.277985e-02, 'w_ada': 7.448304e-02, 'b_ada': 7.333678e-02, 'w_in': 2.281004e-02, 'pool_w': 5.284179e-02, 'pool_scale': 6.571819e-02, 'w_attn_br': 1.868252e-02, 'w_pool_br': 3.669263e-02, 'w_out': 4.055896e-02, 'rel_bias': 1.087183e-02, 'final_g': 3.208491e+01}


def _to_microbatches(a, axis):
    t = _jnp.moveaxis(a, axis, 0)
    t = t.reshape((N_MICROBATCH, t.shape[0] // N_MICROBATCH) + t.shape[1:])
    return _jnp.moveaxis(t, 1, axis + 1)


def setup_inputs(seed: int = 0) -> dict:
    inp = _fwd_setup_inputs(seed)
    key = _jax.random.fold_in(_jax.random.key(seed), 7919)
    shape, _ = _output_shape()
    out = dict(inp)
    out["loss_target"] = _jax.random.normal(_jax.random.fold_in(key, 0), shape, _jnp.float32)
    for i, name in enumerate(TWIN_WEIGHTS):
        w = inp[name].astype(_jnp.float32)
        if MOMENT_SCALE is None:
            s = _jnp.sqrt(_jnp.mean(_jnp.square(w)) + 1e-30)
        else:
            s = MOMENT_SCALE[name]
        km, kv = _jax.random.split(_jax.random.fold_in(key, i + 1))
        out[name] = w
        out["m_" + name] = s * _jax.random.normal(km, w.shape, _jnp.float32)
        out["v_" + name] = (s * s) * _jax.random.uniform(kv, w.shape, _jnp.float32, 0.5, 1.5)
    if N_MICROBATCH > 1:
        for name, axis in PER_EXAMPLE_BATCH_AXIS.items():
            out[name] = _to_microbatches(out[name], axis)
    return {'x': out['x'], 'c': out['c'], 'norm_g': out['norm_g'], 'w_ada': out['w_ada'], 'b_ada': out['b_ada'], 'w_in': out['w_in'], 'pool_w': out['pool_w'], 'pool_scale': out['pool_scale'], 'w_attn_br': out['w_attn_br'], 'w_pool_br': out['w_pool_br'], 'w_out': out['w_out'], 'rel_bias': out['rel_bias'], 'final_g': out['final_g'], 'loss_target': out['loss_target'], 'm_norm_g': out['m_norm_g'], 'm_w_ada': out['m_w_ada'], 'm_b_ada': out['m_b_ada'], 'm_w_in': out['m_w_in'], 'm_pool_w': out['m_pool_w'], 'm_pool_scale': out['m_pool_scale'], 'm_w_attn_br': out['m_w_attn_br'], 'm_w_pool_br': out['m_w_pool_br'], 'm_w_out': out['m_w_out'], 'm_rel_bias': out['m_rel_bias'], 'm_final_g': out['m_final_g'], 'v_norm_g': out['v_norm_g'], 'v_w_ada': out['v_w_ada'], 'v_b_ada': out['v_b_ada'], 'v_w_in': out['v_w_in'], 'v_pool_w': out['v_pool_w'], 'v_pool_scale': out['v_pool_scale'], 'v_w_attn_br': out['v_w_attn_br'], 'v_w_pool_br': out['v_w_pool_br'], 'v_w_out': out['v_w_out'], 'v_rel_bias': out['v_rel_bias'], 'v_final_g': out['v_final_g']}


def _loss(weights, diff, rest, loss_target):
    with _jax.named_scope("forward"):
        args = {**rest, TWIN_DIFF_INPUT: diff, **{k: w.astype(_WEIGHT_DTYPES[k]) for k, w in weights.items()}}
        y = _forward(args)
    with _jax.named_scope("loss_head"):
        err = _jnp.square(y.astype(_jnp.float32) - loss_target)
        return 0.5 * _jnp.sum(_jnp.mean(err, axis=-1)) if err.ndim else 0.5 * err


def _adamw(w, g, m, v):
    m = ADAM_B1 * m + (1.0 - ADAM_B1) * g
    v = ADAM_B2 * v + (1.0 - ADAM_B2) * _jnp.square(g)
    m_hat = m / (1.0 - ADAM_B1 ** ADAM_STEP)
    v_hat = v / (1.0 - ADAM_B2 ** ADAM_STEP)
    delta = -ADAM_LR * (m_hat / (_jnp.sqrt(v_hat) + ADAM_EPS) + ADAM_WD * w)
    return delta, m, v


def reference(x, c, norm_g, w_ada, b_ada, w_in, pool_w, pool_scale, w_attn_br, w_pool_br, w_out, rel_bias, final_g, loss_target, m_norm_g, m_w_ada, m_b_ada, m_w_in, m_pool_w, m_pool_scale, m_w_attn_br, m_w_pool_br, m_w_out, m_rel_bias, m_final_g, v_norm_g, v_w_ada, v_b_ada, v_w_in, v_pool_w, v_pool_scale, v_w_attn_br, v_w_pool_br, v_w_out, v_rel_bias, v_final_g):
    given = dict(x=x, c=c, norm_g=norm_g, w_ada=w_ada, b_ada=b_ada, w_in=w_in, pool_w=pool_w, pool_scale=pool_scale, w_attn_br=w_attn_br, w_pool_br=w_pool_br, w_out=w_out, rel_bias=rel_bias, final_g=final_g, loss_target=loss_target, m_norm_g=m_norm_g, m_w_ada=m_w_ada, m_b_ada=m_b_ada, m_w_in=m_w_in, m_pool_w=m_pool_w, m_pool_scale=m_pool_scale, m_w_attn_br=m_w_attn_br, m_w_pool_br=m_w_pool_br, m_w_out=m_w_out, m_rel_bias=m_rel_bias, m_final_g=m_final_g, v_norm_g=v_norm_g, v_w_ada=v_w_ada, v_b_ada=v_b_ada, v_w_in=v_w_in, v_pool_w=v_pool_w, v_pool_scale=v_pool_scale, v_w_attn_br=v_w_attn_br, v_w_pool_br=v_w_pool_br, v_w_out=v_w_out, v_rel_bias=v_rel_bias, v_final_g=v_final_g)
    weights = {n: given[n] for n in TWIN_WEIGHTS}
    shared = {n: given[n] for n in SHARED_INPUTS}
    per_example = {n: given[n] for n in ['x', 'c']}
    grad_fn = _jax.value_and_grad(_loss, argnums=(0, 1))

    def one_microbatch(ex, loss_target):
        ex = dict(ex)
        diff = ex.pop(TWIN_DIFF_INPUT)
        return grad_fn(weights, diff, {**shared, **ex}, loss_target)

    if N_MICROBATCH == 1:
        loss, (grad_w, grad_x) = one_microbatch(per_example, given["loss_target"])
    else:
        def body(carry, xs):
            loss_sum, grad_sum = carry
            l_k, (gw_k, gx_k) = one_microbatch(xs[0], xs[1])
            with _jax.named_scope("update"):
                return (loss_sum + l_k, _jax.tree.map(_jnp.add, grad_sum, gw_k)), gx_k

        init = (_jnp.zeros((), _jnp.float32), _jax.tree.map(_jnp.zeros_like, weights))
        (loss, grad_w), grad_x = _jax.lax.scan(body, init, (per_example, given["loss_target"]))
    with _jax.named_scope("update"):
        delta_w, new_m, new_v = {}, {}, {}
        for n in TWIN_WEIGHTS:
            delta_w[n], new_m[n], new_v[n] = _adamw(weights[n], grad_w[n], given["m_" + n], given["v_" + n])
    return (loss, grad_x, *[grad_w[n] for n in TWIN_WEIGHTS], *[delta_w[n] for n in TWIN_WEIGHTS],
            *[new_m[n] for n in TWIN_WEIGHTS], *[new_v[n] for n in TWIN_WEIGHTS])
```

```python
import functools
import math

import numpy as np
import jax
import jax.numpy as jnp
from jax import lax
from jax.experimental import pallas as pl
from jax.experimental.pallas import tpu as pltpu

F32 = jnp.float32
CDT = jnp.bfloat16
D = 1024
HD = 64
NH = 8
AW = 512
QKV_W = 4608
REST_W = 3584
R16 = 16
EPS = 1e-6
NEG = -1e30
N_BACK = 128
ADAM_LR, ADAM_B1, ADAM_B2, ADAM_EPS, ADAM_WD, ADAM_STEP = 0.001, 0.9, 0.999, 1e-08, 0.01, 10
MESH = pl.DeviceIdType.MESH
VMEM_MB = 1024 * 1024


def _cp(sem=None, vmem=48, **kw):
    if sem is not None:
        kw["dimension_semantics"] = sem
    return pltpu.CompilerParams(vmem_limit_bytes=vmem * VMEM_MB, **kw)


def _sds(shape, dtype):
    return jax.ShapeDtypeStruct(tuple(shape), dtype)


def _rest_blk(jj):
    return jnp.where(jj == 0, 9, jnp.where(jj == 6, 10, jj + 10))


def _sigmoid(z):
    return 1.0 / (1.0 + jnp.exp(-z))


def _dot(a, b, dims):
    return lax.dot_general(a, b, (dims, ((), ())), preferred_element_type=F32)


NN = ((1,), (0,))
NT = ((1,), (1,))
TN = ((0,), (0,))


def _ld(ref):
    if len(ref.shape) == 2:
        return ref[...]
    return jnp.concatenate([ref[a] for a in range(ref.shape[0])], axis=0)


def _st(ref, val, cols=None):
    val = val.astype(ref.dtype)
    if len(ref.shape) == 2:
        if cols is None:
            ref[...] = val
        else:
            ref[:, cols] = val
        return
    rows = ref.shape[1]
    for a in range(ref.shape[0]):
        if cols is None:
            ref[a] = val[a * rows:(a + 1) * rows]
        else:
            ref[a, :, cols] = val[a * rows:(a + 1) * rows]


def _mm_nn(a, b, *, nb, b_blk, out_dtype, name, tm=512, tn=512):
    M, K = a.shape

    def body(a_ref, b_ref, o_ref):
        o_ref[...] = _dot(a_ref[...], b_ref[...], NN).astype(o_ref.dtype)

    return pl.pallas_call(
        body, name=name, grid=(M // tm, nb),
        in_specs=[pl.BlockSpec((tm, K), lambda i, j: (i, 0)),
                  pl.BlockSpec((K, tn), lambda i, j: (0, b_blk(j)))],
        out_specs=pl.BlockSpec((tm, tn), lambda i, j: (i, j)),
        out_shape=_sds((M, nb * tn), out_dtype),
        compiler_params=_cp(("parallel", "arbitrary")),
    )(a, b)


def _mm_nt_acc(a, b, *, k_blk, c_in, name, tm=512, tk=512):
    M, K = a.shape
    N = b.shape[0]
    nk = K // tk
    has_c = c_in is not None

    def body(*refs):
        if has_c:
            a_ref, b_ref, c_ref, o_ref, acc = refs
        else:
            a_ref, b_ref, o_ref, acc = refs
        k = pl.program_id(1)

        @pl.when(k == 0)
        def _():
            acc[...] = c_ref[...] if has_c else jnp.zeros_like(acc)

        acc[...] += _dot(a_ref[...], b_ref[...], NT)

        @pl.when(k == nk - 1)
        def _():
            o_ref[...] = acc[...]

    in_specs = [pl.BlockSpec((tm, tk), lambda i, k: (i, k)),
                pl.BlockSpec((N, tk), lambda i, k: (0, k_blk(k)))]
    args = [a, b]
    if has_c:
        in_specs.append(pl.BlockSpec((tm, N), lambda i, k: (i, 0)))
        args.append(c_in)
    return pl.pallas_call(
        body, name=name, grid=(M // tm, nk), in_specs=in_specs,
        out_specs=pl.BlockSpec((tm, N), lambda i, k: (i, 0)),
        out_shape=_sds((M, N), F32),
        scratch_shapes=[pltpu.VMEM((tm, N), F32)],
        compiler_params=_cp(("parallel", "arbitrary")),
    )(*args)


def _mm_tn(at, b, *, n_total, out_blk, name, prev=None, tn=512, ts=512):
    R, S = at.shape
    nb = b.shape[1] // tn
    ns = S // ts

    def body(*refs):
        at_ref, b_ref = refs[0], refs[1]
        o_ref = refs[-1]
        s = pl.program_id(1)

        @pl.when(s == 0)
        def _():
            o_ref[...] = jnp.zeros_like(o_ref)

        o_ref[...] += _dot(at_ref[...], b_ref[...], NN)

    in_specs = [pl.BlockSpec((R, ts), lambda j, s: (0, s)),
                pl.BlockSpec((ts, tn), lambda j, s: (s, j))]
    args = [at, b]
    aliases = {}
    if prev is not None:
        in_specs.append(pl.BlockSpec(memory_space=pl.ANY))
        args.append(prev)
        aliases = {2: 0}
    return pl.pallas_call(
        body, name=name, grid=(nb, ns), in_specs=in_specs,
        out_specs=pl.BlockSpec((R, tn), lambda j, s: (0, out_blk(j))),
        out_shape=_sds((R, n_total), F32),
        input_output_aliases=aliases,
        compiler_params=_cp(("parallel", "arbitrary")),
    )(*args)


def _tcast(a, name):
    M, C = a.shape
    tm = 512

    def body(a_ref, o_ref):
        o_ref[...] = a_ref[...].astype(F32).T.astype(o_ref.dtype)

    return pl.pallas_call(
        body, name=name, grid=(M // tm,),
        in_specs=[pl.BlockSpec((tm, C), lambda i: (i, 0))],
        out_specs=pl.BlockSpec((C, tm), lambda i: (0, i)),
        out_shape=_sds((C, M), CDT), compiler_params=_cp(("parallel",)),
    )(a)


def _norm_mod(x3, norm_g, shift, scale):
    LS = x3.shape[0]
    tl = 16

    def body(x_ref, g_ref, sh_ref, sc_ref, h_ref):
        gg = g_ref[...] * (1.0 + sc_ref[...])
        for r in range(R16):
            xr = x_ref[:, r, :]
            xn = xr * lax.rsqrt(jnp.mean(xr * xr, axis=-1, keepdims=True) + EPS)
            h_ref[r] = (xn * gg + sh_ref[...]).astype(h_ref.dtype)

    vec = pl.BlockSpec((1, D), lambda i: (0, 0))
    return pl.pallas_call(
        body, name="norm_mod", grid=(LS // tl,),
        in_specs=[pl.BlockSpec((tl, R16, D), lambda i: (i, 0, 0)), vec, vec, vec],
        out_specs=pl.BlockSpec((R16, tl, D), lambda i: (0, i, 0)),
        out_shape=_sds((R16, LS, D), CDT), compiler_params=_cp(("parallel",)),
    )(x3, norm_g, shift, scale)


def _t5_bucket_np(n):
    nf = np.maximum(n, 1).astype(np.float32)
    large = 16 + (np.log(nf / np.float32(16)) / np.float32(math.log(2048 / 16)) * np.float32(16)).astype(np.int32)
    large = np.minimum(large, 31)
    return np.where(n < 16, n, large).astype(np.int32)


class _Geom:
    def __init__(self, gi, LS):
        self.gi, self.LS = gi, LS
        self.dil = (1, 4, 16)[gi]
        if gi == 0:
            self.lead, self.rows, self.n_tiles, self.has_prev = (R16,), 16, LS // 16, True
            a = np.arange(R16)[:, None]
            ll = np.arange(16)[None, :]
            self.pos = (16 * ll + a).reshape(-1)
        elif gi == 1:
            self.lead, self.rows, self.n_tiles, self.has_prev = (4,), 64, LS // 64, True
            a = np.arange(4)[:, None]
            ll = np.arange(64)[None, :]
            self.pos = (4 * ll + a).reshape(-1)
        else:
            self.lead, self.rows, self.n_tiles, self.has_prev = (), LS, 1, False
            self.pos = np.arange(LS)
        self.T = self.pos.shape[0]
        self.Tk = 2 * self.T if self.has_prev else self.T
        self.outer = {0: (), 1: (4,), 2: (R16,)}[gi]

    def view(self, arr):
        S, C = arr.shape
        if self.gi == 1:
            return arr.reshape(4, 4, self.LS, C)
        return arr.reshape(R16, self.LS, C)

    def view_shape(self, C):
        return (4, 4, self.LS, C) if self.gi == 1 else (R16, self.LS, C)

    def unview(self, arr):
        return arr.reshape(R16 * self.LS, arr.shape[-1])

    def spec(self, W, cblk, tile):
        if self.gi == 0:
            return pl.BlockSpec((R16, 16, W), lambda n: (0, tile(n), cblk))
        if self.gi == 1:
            return pl.BlockSpec((4, None, 64, W), lambda r, n: (0, r, tile(n), cblk))
        return pl.BlockSpec((None, self.LS, W), lambda r: (r, 0, cblk))

    def bucket_index(self):
        pq = self.pos[:, None]
        pk = self.pos[None, :]
        if self.has_prev:
            pk = np.concatenate([pk - self.T, pk], axis=1)
        dist = pq - pk
        ok = (dist >= 0) & (dist <= N_BACK)
        bucket = _t5_bucket_np(np.clip(dist, 0, N_BACK) * self.dil)
        return np.where(ok, bucket, -1).astype(np.int32)


def _bias_table(idx, rel_bias, gi):
    T, Tk = idx.shape
    tr = 32

    def body(idx_ref, rb_ref, o_ref):
        h = pl.program_id(0)
        ix = idx_ref[...]
        acc = jnp.full(ix.shape, NEG, F32)
        for b in range(32):
            acc = jnp.where(ix == b, rb_ref[b, gi * NH + h], acc)
        o_ref[...] = acc

    return pl.pallas_call(
        body, name=f"bias_table{gi}", grid=(NH, T // tr),
        in_specs=[pl.BlockSpec((tr, Tk), lambda h, i: (i, 0)), pl.BlockSpec(memory_space=pltpu.SMEM)],
        out_specs=pl.BlockSpec((None, tr, Tk), lambda h, i: (h, i, 0)),
        out_shape=_sds((NH, T, Tk), F32), compiler_params=_cp(("parallel", "parallel")),
    )(idx, rel_bias)


def _bias_grad(dsum, idx, gi):
    _, T, Tk = dsum.shape

    def body(ds_ref, idx_ref, o_ref, r_ref):
        def step(b, c):
            r_ref[pl.ds(b, 1), :] = jnp.sum(jnp.where(idx_ref[...] == b, ds_ref[...], 0.0), axis=0, keepdims=True)
            return c

        lax.fori_loop(0, 32, step, 0)
        o_ref[...] = jnp.broadcast_to(jnp.sum(r_ref[...], axis=1, keepdims=True), (32, 128))

    out = pl.pallas_call(
        body, name=f"bias_grad{gi}", grid=(NH,),
        in_specs=[pl.BlockSpec((None, T, Tk), lambda h: (h, 0, 0)), pl.BlockSpec((T, Tk), lambda h: (0, 0))],
        out_specs=pl.BlockSpec((None, 32, 128), lambda h: (h, 0, 0)),
        out_shape=_sds((NH, 32, 128), F32), scratch_shapes=[pltpu.VMEM((32, Tk), F32)],
        compiler_params=_cp(("parallel",)),
    )(dsum, idx)
    return out[:, :, 0].T


def _prev_mask(n, T, Tk):
    col = lax.broadcasted_iota(jnp.int32, (1, Tk), 1)
    return jnp.where((col < T) & (n == 0), NEG, 0.0).astype(F32)


def _attn_fwd(qkv, bias, geom):
    S = qkv.shape[0]
    gi, T, Tk, has_prev = geom.gi, geom.T, geom.Tk, geom.has_prev
    qv = geom.view(qkv)
    cq, ck, cv = 3 * gi, 3 * gi + 1, 3 * gi + 2

    def body(*refs):
        if has_prev:
            q_ref, kp_ref, kc_ref, vp_ref, vc_ref, b_ref, o_ref, l_ref = refs
        else:
            q_ref, kc_ref, vc_ref, b_ref, o_ref, l_ref = refs
        q = _ld(q_ref)
        if has_prev:
            n = pl.program_id(len(geom.outer))
            k = jnp.concatenate([_ld(kp_ref), _ld(kc_ref)], axis=0)
            v = jnp.concatenate([_ld(vp_ref), _ld(vc_ref)], axis=0)
            pm = _prev_mask(n, T, Tk)
        else:
            k, v = _ld(kc_ref), _ld(vc_ref)
        for h in range(NH):
            sl = slice(HD * h, HD * h + HD)
            qs = (q[:, sl].astype(F32) * 0.125).astype(q.dtype)
            s = _dot(qs, k[:, sl], NT) + b_ref[h]
            if has_prev:
                s = s + pm
            m = jnp.max(s, axis=1, keepdims=True)
            p = jnp.exp(s - m)
            l = jnp.sum(p, axis=1, keepdims=True)
            o = _dot(p.astype(v.dtype), v[:, sl], NN) / l
            _st(o_ref, o, sl)
            _st(l_ref, jnp.broadcast_to(m + jnp.log(l), (T, HD)), sl)

    cur = lambda n: n
    prev = lambda n: jnp.maximum(n - 1, 0)
    if has_prev:
        in_specs = [geom.spec(AW, cq, cur), geom.spec(AW, ck, prev), geom.spec(AW, ck, cur),
                    geom.spec(AW, cv, prev), geom.spec(AW, cv, cur)]
        args = [qv] * 5
    else:
        in_specs = [geom.spec(AW, cq, cur), geom.spec(AW, ck, cur), geom.spec(AW, cv, cur)]
        args = [qv] * 3
    nd = len(geom.outer) + (1 if has_prev else 0)
    in_specs.append(pl.BlockSpec((NH, T, Tk), lambda *ids: (0, 0, 0)))
    grid = geom.outer + ((geom.n_tiles,) if has_prev else ())
    oshape = geom.view_shape(AW)
    o, l = pl.pallas_call(
        body, name=f"attn_fwd{gi}", grid=grid, in_specs=in_specs,
        out_specs=[geom.spec(AW, 0, cur), geom.spec(AW, 0, cur)],
        out_shape=[_sds(oshape, F32), _sds(oshape, F32)],
        compiler_params=_cp(("arbitrary",) * nd),
    )(*args, bias)
    return geom.unview(o), geom.unview(l)


def _attn_combine(os_, ls_, rest):
    S = rest.shape[0]
    tm = 512

    def body(o0, o1, o2, l0, l1, l2, z_ref, attn_ref, L_ref, ag_ref, agt_ref):
        la, lb, lc = l0[...], l1[...], l2[...]
        m = jnp.maximum(jnp.maximum(la, lb), lc)
        ea, eb, ec = jnp.exp(la - m), jnp.exp(lb - m), jnp.exp(lc - m)
        den = ea + eb + ec
        attn = (ea * o0[...] + eb * o1[...] + ec * o2[...]) / den
        attn_ref[...] = attn
        L_ref[...] = m + jnp.log(den)
        z = z_ref[...]
        ag = attn * (z * _sigmoid(z))
        ag_ref[...] = ag.astype(ag_ref.dtype)
        agt_ref[...] = ag.T.astype(agt_ref.dtype)

    t = pl.BlockSpec((tm, AW), lambda i: (i, 0))
    return pl.pallas_call(
        body, name="attn_combine", grid=(S // tm,),
        in_specs=[t] * 6 + [pl.BlockSpec((tm, AW), lambda i: (i, 0))],
        out_specs=[t, t, t, pl.BlockSpec((AW, tm), lambda i: (0, i))],
        out_shape=[_sds((S, AW), F32), _sds((S, AW), F32), _sds((S, AW), CDT), _sds((AW, S), CDT)],
        compiler_params=_cp(("parallel",)),
    )(*os_, *ls_, rest)


def _pool_counts(r, LS, win):
    l = lax.broadcasted_iota(jnp.int32, (LS, 1), 0)
    return jnp.minimum(16 * l + (r + 1), win).astype(F32)


def _pool_fwd(rest, pool_w, pool_scale):
    S = rest.shape[0]
    LS = S // R16
    r3 = rest.reshape(R16, LS, REST_W)
    PG = 128

    def body(u_ref, z_ref, pw_ref, ps_ref, pt_ref, mp_ref, pg_ref, pgt_ref, d_ref):
        g = pl.program_id(0)
        win = jnp.left_shift(2, g)
        row = lax.broadcasted_iota(jnp.int32, (LS, PG), 0)
        for r in range(1, R16):
            d_ref[r] = jnp.where(row == 0, 0.0, pltpu.roll(u_ref[r], 1, 0))
        pw = pw_ref[...].astype(CDT)
        for r in range(R16):
            u = u_ref[r]
            acc = u
            for j in range(1, R16):
                coef = jnp.where(j < win, 1.0, 0.0).astype(F32)
                src = u_ref[r - j] if r >= j else d_ref[r - j + R16]
                acc = acc + coef * src
            pooled = acc / _pool_counts(r, LS, win) - u
            pc = pooled.astype(CDT)
            mp = _dot(pc, pw, NN)
            z = z_ref[r]
            pg = (mp * ps_ref[...]) * (z * _sigmoid(z))
            mp_ref[r] = mp
            pg_ref[r] = pg.astype(pg_ref.dtype)
            pt_ref[:, r * LS:(r + 1) * LS] = pooled.T.astype(pt_ref.dtype)
            pgt_ref[:, r * LS:(r + 1) * LS] = pg.T.astype(pgt_ref.dtype)

    col = lambda off: pl.BlockSpec((R16, LS, PG), lambda g: (0, 0, off + g))
    tsp = pl.BlockSpec((PG, S), lambda g: (g, 0))
    pt, mp, pg, pgt = pl.pallas_call(
        body, name="pool_fwd", grid=(4,),
        in_specs=[col(24), col(4), pl.BlockSpec((None, PG, PG), lambda g: (g, 0, 0)),
                  pl.BlockSpec((1, PG), lambda g: (0, g))],
        out_specs=[tsp, col(0), col(0), tsp],
        out_shape=[_sds((AW, S), CDT), _sds((R16, LS, AW), F32), _sds((R16, LS, AW), CDT), _sds((AW, S), CDT)],
        scratch_shapes=[pltpu.VMEM((R16, LS, PG), F32)],
        compiler_params=_cp(("parallel",)),
    )(r3, r3, pool_w, pool_scale)
    return pt, mp.reshape(S, AW), pg.reshape(S, AW), pgt


def _branch_merge(ag, pg, wab, wpb, rest):
    S = ag.shape[0]
    tm, tn = 512, 512

    def body(ag_ref, pg_ref, wa_ref, wp_ref, ga_ref, gp_ref, ya_ref, yp_ref, m_ref, mt_ref):
        ya = _dot(ag_ref[...], wa_ref[...], NN)
        yp = _dot(pg_ref[...], wp_ref[...], NN)
        ya_ref[...] = ya
        yp_ref[...] = yp
        mg = _sigmoid(ga_ref[...]) * ya + _sigmoid(gp_ref[...]) * yp
        m_ref[...] = mg.astype(m_ref.dtype)
        mt_ref[...] = mg.T.astype(mt_ref.dtype)

    a = pl.BlockSpec((tm, AW), lambda i, j: (i, 0))
    w = pl.BlockSpec((AW, tn), lambda i, j: (0, j))
    o = pl.BlockSpec((tm, tn), lambda i, j: (i, j))
    return pl.pallas_call(
        body, name="branch_merge", grid=(S // tm, D // tn),
        in_specs=[a, a, w, w, pl.BlockSpec((tm, tn), lambda i, j: (i, 2 + j)),
                  pl.BlockSpec((tm, tn), lambda i, j: (i, 4 + j))],
        out_specs=[o, o, o, pl.BlockSpec((tn, tm), lambda i, j: (j, i))],
        out_shape=[_sds((S, D), F32), _sds((S, D), F32), _sds((S, D), CDT), _sds((D, S), CDT)],
        compiler_params=_cp(("parallel", "parallel")),
    )(ag, pg, wab, wpb, rest, rest)


def _out_loss(merged3, wout, x3, t3, gate, final_g):
    LS = x3.shape[0]
    tl = 32

    def body(m_ref, w_ref, x_ref, t_ref, gate_ref, fg_ref, dx2_ref, dmo_ref, loss_ref, gfg_ref, dgate_ref):
        i = pl.program_id(0)

        @pl.when(i == 0)
        def _():
            loss_ref[...] = jnp.zeros_like(loss_ref)
            gfg_ref[...] = jnp.zeros_like(gfg_ref)
            dgate_ref[...] = jnp.zeros_like(dgate_ref)

        mo = _dot(_ld(m_ref), w_ref[...], NN)
        gate, fg = gate_ref[...], fg_ref[...]
        loss = jnp.zeros((1, 1), F32)
        gfg = jnp.zeros((1, D), F32)
        dgate = jnp.zeros((1, D), F32)
        for r in range(R16):
            mo_r = mo[r * tl:(r + 1) * tl]
            x2 = x_ref[:, r, :] + gate * mo_r
            r2 = lax.rsqrt(jnp.mean(x2 * x2, axis=-1, keepdims=True) + EPS)
            xn2 = x2 * r2
            diff = xn2 * fg - t_ref[:, r, :]
            loss = loss + jnp.sum(jnp.sum(diff * diff, axis=-1, keepdims=True), axis=0, keepdims=True)
            dy = diff * (1.0 / D)
            gfg = gfg + jnp.sum(dy * xn2, axis=0, keepdims=True)
            dxn2 = dy * fg
            dx2 = r2 * (dxn2 - xn2 * jnp.mean(dxn2 * xn2, axis=-1, keepdims=True))
            dgate = dgate + jnp.sum(dx2 * mo_r, axis=0, keepdims=True)
            dx2_ref[r] = dx2
            dmo_ref[r] = (dx2 * gate).astype(dmo_ref.dtype)
        loss_ref[...] += jnp.broadcast_to(loss * (0.5 / D), loss_ref.shape)
        gfg_ref[...] += gfg
        dgate_ref[...] += dgate

    nat = pl.BlockSpec((tl, R16, D), lambda i: (i, 0, 0))
    slab = pl.BlockSpec((R16, tl, D), lambda i: (0, i, 0))
    vec = pl.BlockSpec((1, D), lambda i: (0, 0))
    return pl.pallas_call(
        body, name="out_loss", grid=(LS // tl,),
        in_specs=[slab, pl.BlockSpec((D, D), lambda i: (0, 0)), nat, nat, vec, vec],
        out_specs=[slab, slab, pl.BlockSpec((1, 128), lambda i: (0, 0)), vec, vec],
        out_shape=[_sds((R16, LS, D), F32), _sds((R16, LS, D), CDT), _sds((1, 128), F32), _sds((1, D), F32),
                   _sds((1, D), F32)],
        compiler_params=_cp(("arbitrary",)),
    )(merged3, wout, x3, t3, gate, final_g)


def _bwd_gates(dmo, wout, wab, wpb, rest, ya, yp, attn, mp, pool_scale):
    S = dmo.shape[0]
    tm = 256

    def body(dmo_ref, wo_ref, wa_ref, wp_ref, za_ref, zp_ref, ga0, ga1, gp0, gp1, ya_ref, yp_ref, at_ref, mp_ref,
             ps_ref, dya_ref, dyp_ref, dr_ref, dat_ref, dl_ref, dmp_ref, dps_ref):
        i = pl.program_id(0)

        @pl.when(i == 0)
        def _():
            dps_ref[...] = jnp.zeros_like(dps_ref)

        dm = _dot(dmo_ref[...], wo_ref[...], NT)
        sa = _sigmoid(jnp.concatenate([ga0[...], ga1[...]], axis=1))
        sp = _sigmoid(jnp.concatenate([gp0[...], gp1[...]], axis=1))
        dya = (dm * sa).astype(CDT)
        dyp = (dm * sp).astype(CDT)
        dya_ref[...] = dya
        dyp_ref[...] = dyp
        dr_ref[:, 1024:2048] = (dm * ya_ref[...] * sa * (1.0 - sa)).astype(dr_ref.dtype)
        dr_ref[:, 2048:3072] = (dm * yp_ref[...] * sp * (1.0 - sp)).astype(dr_ref.dtype)
        da = _dot(dya, wa_ref[...], NT)
        dp = _dot(dyp, wp_ref[...], NT)
        za = za_ref[...]
        sga = _sigmoid(za)
        attn = at_ref[...]
        dattn = da * (za * sga)
        dr_ref[:, 0:512] = (da * attn * (sga * (1.0 + za * (1.0 - sga)))).astype(dr_ref.dtype)
        dat_ref[...] = dattn.astype(dat_ref.dtype)
        prod = dattn * attn
        for h in range(NH):
            sl = slice(HD * h, HD * h + HD)
            dl_ref[:, sl] = jnp.broadcast_to(jnp.sum(prod[:, sl], axis=1, keepdims=True), (tm, HD))
        zp = zp_ref[...]
        sgp = _sigmoid(zp)
        mpre = mp_ref[...]
        ps = ps_ref[...]
        dmixed = dp * (zp * sgp)
        dr_ref[:, 512:1024] = (dp * (mpre * ps) * (sgp * (1.0 + zp * (1.0 - sgp)))).astype(dr_ref.dtype)
        dps_ref[...] += jnp.sum(dmixed * mpre, axis=0, keepdims=True)
        dmp_ref[...] = (dmixed * ps).astype(dmp_ref.dtype)

    full = lambda shape: pl.BlockSpec(shape, lambda i: (0, 0))
    tD = pl.BlockSpec((tm, D), lambda i: (i, 0))
    tA = pl.BlockSpec((tm, AW), lambda i: (i, 0))
    rb = lambda c: pl.BlockSpec((tm, AW), lambda i: (i, c))
    return pl.pallas_call(
        body, name="bwd_gates", grid=(S // tm,),
        in_specs=[tD, full((D, D)), full((AW, D)), full((AW, D)), rb(0), rb(1), rb(2), rb(3), rb(4), rb(5),
                  tD, tD, tA, tA, full((1, AW))],
        out_specs=[tD, tD, pl.BlockSpec((tm, 3072), lambda i: (i, 0)), tA, tA, tA, full((1, AW))],
        out_shape=[_sds((S, D), CDT), _sds((S, D), CDT), _sds((S, REST_W), CDT), _sds((S, AW), CDT),
                   _sds((S, AW), F32), _sds((S, AW), CDT), _sds((1, AW), F32)],
        compiler_params=_cp(("arbitrary",), vmem=56),
    )(dmo, wout, wab, wpb, rest, rest, rest, rest, rest, rest, ya, yp, attn, mp, pool_scale)


def _pool_bwd(dmp, pt, pool_w, drest):
    S = dmp.shape[0]
    LS = S // R16
    PG = 128
    d3 = dmp.reshape(R16, LS, AW)
    dr3 = drest.reshape(R16, LS, REST_W)

    def body(d_ref, pt_ref, pw_ref, dr_in, du_ref, dpw_ref, g_ref, s_ref):
        del dr_in
        g = pl.program_id(0)
        win = jnp.left_shift(2, g)
        row = lax.broadcasted_iota(jnp.int32, (LS, PG), 0)
        pw = pw_ref[...].astype(CDT)
        dpw = jnp.zeros((PG, PG), F32)
        for r in range(R16):
            dm = d_ref[r]
            dpw = dpw + _dot(pt_ref[:, r * LS:(r + 1) * LS], dm, NN)
            dpooled = _dot(dm, pw, NT)
            g_ref[r] = dpooled
            s_ref[r] = dpooled / _pool_counts(r, LS, win)
        dpw_ref[...] = dpw
        for r in range(R16):
            acc = s_ref[r]
            for j in range(1, R16):
                coef = jnp.where(j < win, 1.0, 0.0).astype(F32)
                if r + j < R16:
                    src = s_ref[r + j]
                else:
                    src = jnp.where(row == LS - 1, 0.0, pltpu.roll(s_ref[r + j - R16], LS - 1, 0))
                acc = acc + coef * src
            du_ref[r] = (acc - g_ref[r]).astype(du_ref.dtype)

    du, dpw = pl.pallas_call(
        body, name="pool_bwd", grid=(4,),
        in_specs=[pl.BlockSpec((R16, LS, PG), lambda g: (0, 0, g)), pl.BlockSpec((PG, S), lambda g: (g, 0)),
                  pl.BlockSpec((None, PG, PG), lambda g: (g, 0, 0)), pl.BlockSpec(memory_space=pl.ANY)],
        out_specs=[pl.BlockSpec((R16, LS, PG), lambda g: (0, 0, 24 + g)),
                   pl.BlockSpec((None, PG, PG), lambda g: (g, 0, 0))],
        out_shape=[_sds((R16, LS, REST_W), CDT), _sds((4, PG, PG), F32)],
        scratch_shapes=[pltpu.VMEM((R16, LS, PG), F32), pltpu.VMEM((R16, LS, PG), F32)],
        input_output_aliases={3: 0},
        compiler_params=_cp(("parallel",)),
    )(d3, pt, pool_w, dr3)
    return du.reshape(S, REST_W), dpw


def _attn_bwd(qkv, dattn, Lb, dlb, bias, geom, dqkv_prev):
    S = qkv.shape[0]
    gi, T, Tk, has_prev, N = geom.gi, geom.T, geom.Tk, geom.has_prev, geom.n_tiles
    nd_outer = len(geom.outer)
    qv = geom.view(qkv)
    cq, ck, cv = 3 * gi, 3 * gi + 1, 3 * gi + 2
    W3 = 3 * AW

    def heads(q, k, v, do, L, dl, b_ref, pm, dsum_ref, emit):
        for h in range(NH):
            sl = slice(HD * h, HD * h + HD)
            qs = (q[:, sl].astype(F32) * 0.125).astype(q.dtype)
            kh, vh, doh = k[:, sl], v[:, sl], do[:, sl]
            s = _dot(qs, kh, NT) + b_ref[h]
            if pm is not None:
                s = s + pm
            p = jnp.exp(s - L[:, HD * h:HD * h + 1])
            dp = _dot(doh, vh, NT)
            ds = p * (dp - dl[:, HD * h:HD * h + 1])
            dsum_ref[h] += ds
            dsc = ds.astype(q.dtype)
            dq = _dot(dsc, kh, NN) * 0.125
            dk = _dot(dsc, qs, TN)
            dv = _dot(p.astype(q.dtype), doh, TN)
            emit(sl, dq, dk, dv)

    def first_step():
        ok = pl.program_id(nd_outer) == 0 if has_prev else pl.program_id(0) == 0
        for a in range(nd_outer if has_prev else 0):
            ok = ok & (pl.program_id(a) == 0)
        return ok

    if has_prev:
        def body(q_ref, kp_ref, kc_ref, vp_ref, vc_ref, do_ref, L_ref, dl_ref, b_ref, prev_in, out_ref, dsum_ref,
                 hold, new, pp):
            del prev_in
            s_id = pl.program_id(nd_outer)

            @pl.when(first_step())
            def _():
                dsum_ref[...] = jnp.zeros_like(dsum_ref)

            @pl.when(s_id < N)
            def _():
                q = _ld(q_ref)
                k = jnp.concatenate([_ld(kp_ref), _ld(kc_ref)], axis=0)
                v = jnp.concatenate([_ld(vp_ref), _ld(vc_ref)], axis=0)
                pm = _prev_mask(s_id, T, Tk)

                def emit(sl, dq, dk, dv):
                    new[:, sl] = dq
                    new[:, slice(AW + sl.start, AW + sl.stop)] = dk[T:]
                    new[:, slice(2 * AW + sl.start, 2 * AW + sl.stop)] = dv[T:]
                    pp[:, sl] = dk[:T]
                    pp[:, slice(AW + sl.start, AW + sl.stop)] = dv[:T]

                heads(q, k, v, _ld(do_ref), _ld(L_ref), _ld(dl_ref), b_ref, pm, dsum_ref, emit)

            @pl.when((s_id >= 1) & (s_id < N))
            def _():
                _st(out_ref, hold[:, 0:AW], slice(0, AW))
                _st(out_ref, hold[:, AW:W3] + pp[...], slice(AW, W3))

            @pl.when(s_id == N)
            def _():
                _st(out_ref, hold[...])

            @pl.when(s_id < N)
            def _():
                hold[...] = new[...]

        cur = lambda n: jnp.minimum(n, N - 1)
        prev = lambda n: jnp.clip(n - 1, 0, N - 1)
        delayed = lambda n: jnp.maximum(n - 1, 0)
        in_specs = [geom.spec(AW, cq, cur), geom.spec(AW, ck, prev), geom.spec(AW, ck, cur),
                    geom.spec(AW, cv, prev), geom.spec(AW, cv, cur),
                    geom.spec(AW, 0, cur), geom.spec(AW, 0, cur), geom.spec(AW, 0, cur)]
        args = [qv] * 5 + [geom.view(dattn), geom.view(Lb), geom.view(dlb)]
        out_spec = geom.spec(W3, gi, delayed)
        grid = geom.outer + (N + 1,)
        scratch = [pltpu.VMEM((T, W3), F32), pltpu.VMEM((T, W3), F32), pltpu.VMEM((T, 2 * AW), F32)]
    else:
        def body(q_ref, kc_ref, vc_ref, do_ref, L_ref, dl_ref, b_ref, prev_in, out_ref, dsum_ref):
            del prev_in

            @pl.when(first_step())
            def _():
                dsum_ref[...] = jnp.zeros_like(dsum_ref)

            def emit(sl, dq, dk, dv):
                _st(out_ref, dq, sl)
                _st(out_ref, dk, slice(AW + sl.start, AW + sl.stop))
                _st(out_ref, dv, slice(2 * AW + sl.start, 2 * AW + sl.stop))

            heads(_ld(q_ref), _ld(kc_ref), _ld(vc_ref), _ld(do_ref), _ld(L_ref), _ld(dl_ref), b_ref, None,
                  dsum_ref, emit)

        cur = lambda n: n
        in_specs = [geom.spec(AW, cq, cur), geom.spec(AW, ck, cur), geom.spec(AW, cv, cur),
                    geom.spec(AW, 0, cur), geom.spec(AW, 0, cur), geom.spec(AW, 0, cur)]
        args = [qv] * 3 + [geom.view(dattn), geom.view(Lb), geom.view(dlb)]
        out_spec = geom.spec(W3, gi, cur)
        grid = geom.outer
        scratch = []

    in_specs.append(pl.BlockSpec((NH, T, Tk), lambda *ids: (0, 0, 0)))
    in_specs.append(pl.BlockSpec(memory_space=pl.ANY))
    dq, dsum = pl.pallas_call(
        body, name=f"attn_bwd{gi}", grid=grid, in_specs=in_specs,
        out_specs=[out_spec, pl.BlockSpec((NH, T, Tk), lambda *ids: (0, 0, 0))],
        out_shape=[_sds(geom.view_shape(QKV_W), CDT), _sds((NH, T, Tk), F32)],
        scratch_shapes=scratch,
        input_output_aliases={len(in_specs) - 1: 0},
        compiler_params=_cp(("arbitrary",) * len(grid), vmem=56),
    )(*args, bias, geom.view(dqkv_prev))
    return geom.unview(dq), dsum


def _h_bwd(dh3, dx2_3, x3, norm_g, scale):
    LS = x3.shape[0]
    tl = 32

    def body(dh_ref, dx2_ref, x_ref, g_ref, sc_ref, gx_ref, dsh_ref, dsc_ref, dng_ref):
        i = pl.program_id(0)

        @pl.when(i == 0)
        def _():
            dsh_ref[...] = jnp.zeros_like(dsh_ref)
            dsc_ref[...] = jnp.zeros_like(dsc_ref)
            dng_ref[...] = jnp.zeros_like(dng_ref)

        g = g_ref[...]
        one_sc = 1.0 + sc_ref[...]
        dsh = jnp.zeros((1, D), F32)
        dsc = jnp.zeros((1, D), F32)
        dng = jnp.zeros((1, D), F32)
        for r in range(R16):
            xr = x_ref[:, r, :]
            r1 = lax.rsqrt(jnp.mean(xr * xr, axis=-1, keepdims=True) + EPS)
            xn = xr * r1
            dh = dh_ref[r]
            dsh = dsh + jnp.sum(dh, axis=0, keepdims=True)
            dhx = dh * xn
            dsc = dsc + jnp.sum(dhx * g, axis=0, keepdims=True)
            dng = dng + jnp.sum(dhx * one_sc, axis=0, keepdims=True)
            dxn = dh * (g * one_sc)
            dx1 = r1 * (dxn - xn * jnp.mean(dxn * xn, axis=-1, keepdims=True))
            gx_ref[:, r, :] = dx2_ref[r] + dx1
        dsh_ref[...] += dsh
        dsc_ref[...] += dsc
        dng_ref[...] += dng

    nat = pl.BlockSpec((tl, R16, D), lambda i: (i, 0, 0))
    slab = pl.BlockSpec((R16, tl, D), lambda i: (0, i, 0))
    vec = pl.BlockSpec((1, D), lambda i: (0, 0))
    return pl.pallas_call(
        body, name="h_bwd", grid=(LS // tl,),
        in_specs=[slab, slab, nat, vec, vec],
        out_specs=[nat, vec, vec, vec],
        out_shape=[_sds((LS, R16, D), F32), _sds((1, D), F32), _sds((1, D), F32), _sds((1, D), F32)],
        compiler_params=_cp(("arbitrary",)),
    )(dh3, dx2_3, x3, norm_g, scale)


def _local_step(x, target, shift, scale, gate, norm_g, w_in_full, pool_w, pool_scale, wab, wpb, wout, rel_bias,
                final_g):
    S = x.shape[0]
    LS = S // R16
    x3 = x.reshape(LS, R16, D)
    t3 = target.reshape(LS, R16, D)
    geoms = [_Geom(gi, LS) for gi in range(3)]
    idxs = [jnp.asarray(g.bucket_index()) for g in geoms]
    biases = [_bias_table(idxs[gi], rel_bias, gi) for gi in range(3)]

    h3 = _norm_mod(x3, norm_g, shift, scale)
    h = h3.reshape(S, D)
    ht = _tcast(h, "h_transpose")
    qkv = _mm_nn(h, w_in_full, nb=9, b_blk=lambda j: j, out_dtype=CDT, name="proj_qkv")
    rest = _mm_nn(h, w_in_full, nb=7, b_blk=_rest_blk, out_dtype=F32, name="proj_rest")

    os_, ls_ = [], []
    for gi in range(3):
        o, l = _attn_fwd(qkv, biases[gi], geoms[gi])
        os_.append(o)
        ls_.append(l)
    attn, Lb, ag, agt = _attn_combine(os_, ls_, rest)
    pt, mp, pg, pgt = _pool_fwd(rest, pool_w, pool_scale)
    ya, yp, merged, mt = _branch_merge(ag, pg, wab, wpb, rest)
    dx2_3, dmo3, loss, gfg, dgate = _out_loss(merged.reshape(R16, LS, D), wout, x3, t3, gate, final_g)
    dmo = dmo3.reshape(S, D)

    dya, dyp, drest, dattn, dlb, dmp, dps = _bwd_gates(dmo, wout, wab, wpb, rest, ya, yp, attn, mp, pool_scale)
    dw_out = _mm_tn(mt, dmo, n_total=D, out_blk=lambda j: j, name="dw_out")
    dw_ab = _mm_tn(agt, dya, n_total=D, out_blk=lambda j: j, name="dw_attn_br")
    dw_pb = _mm_tn(pgt, dyp, n_total=D, out_blk=lambda j: j, name="dw_pool_br")
    drest, dpw = _pool_bwd(dmp, pt, pool_w, drest)

    dqkv = jnp.zeros((S, QKV_W), CDT)
    drb = []
    for gi in range(3):
        dqkv, dsum = _attn_bwd(qkv, dattn, Lb, dlb, biases[gi], geoms[gi], dqkv)
        drb.append(_bias_grad(dsum, idxs[gi], gi))
    d_rel_bias = jnp.concatenate(drb, axis=1)

    dh = _mm_nt_acc(dqkv, w_in_full, k_blk=lambda k: k, c_in=None, name="dh_qkv")
    dh = _mm_nt_acc(drest, w_in_full, k_blk=_rest_blk, c_in=dh, name="dh_rest")
    dw_in = _mm_tn(ht, dqkv, n_total=8192, out_blk=lambda j: j, name="dw_in_qkv")
    dw_in = _mm_tn(ht, drest, n_total=8192, out_blk=_rest_blk, name="dw_in_rest", prev=dw_in)

    gx3, dsh, dsc, dng = _h_bwd(dh.reshape(R16, LS, D), dx2_3, x3, norm_g, scale)
    return dict(loss=loss[0, 0], grad_x=gx3.reshape(S, D), dw_in=dw_in, dw_ab=dw_ab, dw_pb=dw_pb, dw_out=dw_out,
                d_norm_g=dng, d_pool_w=dpw, d_pool_scale=dps, d_rel_bias=d_rel_bias, d_final_g=gfg,
                dmod=jnp.concatenate([dsh, dsc, dgate], axis=1))


_HBM = pl.BlockSpec(memory_space=pltpu.HBM)


def _my_place():
    return lax.axis_index("x"), lax.axis_index("y"), lax.axis_index("c")


def _block_of(ref, axis, idx, width):
    if axis is None:
        return ref.at[idx]
    start = pl.multiple_of(idx * width, width)
    if axis == 0:
        return ref.at[pl.ds(start, width)]
    return ref.at[:, pl.ds(start, width)]


def _all_gather(srcs, out_shapes, layout, name):
    n = len(srcs)

    def body(*refs):
        src, out = refs[:n], refs[n:2 * n]
        send_sems, recv_sems, local_sems = refs[2 * n:]
        x, y, c = _my_place()
        me, sibling = (x, y, c), (x, y, 1 - c)
        chips = [(1 - x, y), (x, 1 - y), (1 - x, 1 - y)]

        def region(p, dev):
            px, py, pc = dev
            return _block_of(out[p], layout[p][0], 4 * px + 2 * py + pc, layout[p][1])

        def copy(p, k, block, to, src_ref=None):
            return pltpu.make_async_remote_copy(
                src_ref=region(p, block) if src_ref is None else src_ref, dst_ref=region(p, block),
                send_sem=send_sems.at[p * 7 + k], recv_sem=recv_sems.at[p * 7 + k],
                device_id=to, device_id_type=MESH)

        mine = [pltpu.make_async_copy(src[p], region(p, me), local_sems.at[p]) for p in range(n)]
        for cp in mine:
            cp.start()
        first = []
        for p in range(n):
            first.append(copy(p, 0, me, sibling, src[p]))
            first += [copy(p, 1 + j, me, (*chip, c), src[p]) for j, chip in enumerate(chips)]
        for cp in first:
            cp.start()
        passed = []
        for j, chip in enumerate(chips):
            for p in range(n):
                copy(p, 1 + j, (*chip, c), me).wait_recv()
                fwd = copy(p, 4 + j, (*chip, c), sibling)
                fwd.start()
                passed.append(fwd)
        for p in range(n):
            copy(p, 0, sibling, me).wait_recv()
            for j, chip in enumerate(chips):
                copy(p, 4 + j, (*chip, 1 - c), me).wait_recv()
        for cp in first + passed:
            cp.wait_send()
        for cp in mine:
            cp.wait()

    return pl.pallas_call(
        body, name=name, in_specs=[_HBM] * n, out_specs=[_HBM] * n,
        out_shape=[_sds(s, a.dtype) for s, a in zip(out_shapes, srcs)],
        scratch_shapes=[pltpu.SemaphoreType.DMA((7 * n,)), pltpu.SemaphoreType.DMA((7 * n,)),
                        pltpu.SemaphoreType.DMA((n,))],
    )(*srcs)


def _reduce_pair_exchange(dws, layout, smalls):
    n = len(dws)

    def body(*refs):
        src, sm = refs[:n], refs[n]
        out, sm_out = refs[n + 1:2 * n + 1], refs[2 * n + 1]
        send_sems, recv_sems = refs[2 * n + 2:]
        x, y, c = _my_place()
        sibling = (x, y, 1 - c)
        cps = []
        for p in range(n):
            for q in range(4):
                cps.append(pltpu.make_async_remote_copy(
                    src_ref=_block_of(src[p], layout[p][0], 2 * q + (1 - c), layout[p][1]), dst_ref=out[p].at[q],
                    send_sem=send_sems.at[4 * p + q], recv_sem=recv_sems.at[4 * p + q],
                    device_id=sibling, device_id_type=MESH))
        cps.append(pltpu.make_async_remote_copy(src_ref=sm, dst_ref=sm_out, send_sem=send_sems.at[4 * n],
                                                recv_sem=recv_sems.at[4 * n], device_id=sibling, device_id_type=MESH))
        for cp in cps:
            cp.start()
        for cp in cps:
            cp.wait()

    def shard_shape(a, lay):
        axis, w = lay
        return (4, w, a.shape[1]) if axis == 0 else (4, a.shape[0], w)

    return pl.pallas_call(
        body, name="reduce_pair_exchange", in_specs=[_HBM] * (n + 1), out_specs=[_HBM] * (n + 1),
        out_shape=[_sds(shard_shape(a, l), F32) for a, l in zip(dws, layout)] + [_sds(smalls.shape, F32)],
        scratch_shapes=[pltpu.SemaphoreType.DMA((4 * n + 1,)), pltpu.SemaphoreType.DMA((4 * n + 1,))],
    )(*dws, smalls)


def _pair_add(dw, recv, axis, width, cidx, name):
    if axis == 0:
        tr = min(width, 256)
        C = dw.shape[1]
        grid = (4, width // tr)
        nb = width // tr
        own = pl.BlockSpec((tr, C), lambda q, i, c_ref: ((2 * q + c_ref[0]) * nb + i, 0))
        oth = pl.BlockSpec((None, tr, C), lambda q, i, c_ref: (q, i, 0))
    else:
        R = dw.shape[0]
        tr = min(R, 256)
        grid = (4, R // tr)
        own = pl.BlockSpec((tr, width), lambda q, i, c_ref: (i, 2 * q + c_ref[0]))
        oth = pl.BlockSpec((None, tr, width), lambda q, i, c_ref: (q, i, 0))

    def body(c_ref, a_ref, b_ref, o_ref):
        o_ref[...] = a_ref[...] + b_ref[...]

    return pl.pallas_call(
        body, name=name,
        grid_spec=pltpu.PrefetchScalarGridSpec(num_scalar_prefetch=1, grid=grid, in_specs=[own, oth], out_specs=oth),
        out_shape=_sds(recv.shape, F32), compiler_params=_cp(("parallel", "parallel")),
    )(cidx, dw, recv)


def _add2(a, b, name):
    def body(a_ref, b_ref, o_ref):
        o_ref[...] = a_ref[...] + b_ref[...]

    return pl.pallas_call(body, name=name, out_shape=_sds(a.shape, F32))(a, b)


def _reduce_chip_exchange(parts):
    n = len(parts)

    def body(*refs):
        src, out = refs[:n], refs[n:2 * n]
        send_sems, recv_sems, local_sems = refs[2 * n:]
        x, y, c = _my_place()
        my_slot = 2 * x + y
        chips = [(1 - x, y), (x, 1 - y), (1 - x, 1 - y)]
        cps = []
        for p in range(n):
            keep = pltpu.make_async_copy(src[p].at[my_slot], out[p].at[my_slot], local_sems.at[p])
            keep.start()
            cps.append(keep)
            for k, (cx, cy) in enumerate(chips):
                cps.append(pltpu.make_async_remote_copy(
                    src_ref=src[p].at[2 * cx + cy], dst_ref=out[p].at[my_slot],
                    send_sem=send_sems.at[3 * p + k], recv_sem=recv_sems.at[3 * p + k],
                    device_id=(cx, cy, c), device_id_type=MESH))
                cps[-1].start()
        for cp in cps:
            cp.wait()

    return pl.pallas_call(
        body, name="reduce_chip_exchange", in_specs=[_HBM] * n, out_specs=[_HBM] * n,
        out_shape=[_sds(a.shape, F32) for a in parts],
        scratch_shapes=[pltpu.SemaphoreType.DMA((3 * n,)), pltpu.SemaphoreType.DMA((3 * n,)),
                        pltpu.SemaphoreType.DMA((n,))],
    )(*parts)


def _adaln_part(c_all, w_ada, b_part):
    def body(c_ref, w_ref, b_ref, o_ref):
        o_ref[...] = _dot(c_ref[...].astype(CDT), w_ref[...].astype(CDT), NN) + b_ref[...]

    return pl.pallas_call(body, name="adaln_part", out_shape=_sds((8, w_ada.shape[1]), F32),
                          compiler_params=_cp())(c_all, w_ada, b_part)


def _adam_math(w, g, m, v):
    m = ADAM_B1 * m + (1.0 - ADAM_B1) * g
    v = ADAM_B2 * v + (1.0 - ADAM_B2) * (g * g)
    m_hat = m / (1.0 - ADAM_B1 ** ADAM_STEP)
    v_hat = v / (1.0 - ADAM_B2 ** ADAM_STEP)
    delta = -ADAM_LR * (m_hat / (jnp.sqrt(v_hat) + ADAM_EPS) + ADAM_WD * w)
    return delta, m, v


def _adam(gparts, w, m, v, name):
    P, R, C = gparts.shape
    tr = R if R <= 256 else 256

    def body(g_ref, w_ref, m_ref, v_ref, go_ref, d_ref, mo_ref, vo_ref):
        g = g_ref[0]
        for i in range(1, P):
            g = g + g_ref[i]
        d, mn, vn = _adam_math(w_ref[...], g, m_ref[...], v_ref[...])
        go_ref[...] = g
        d_ref[...] = d
        mo_ref[...] = mn
        vo_ref[...] = vn

    t = pl.BlockSpec((tr, C), lambda i: (i, 0))
    return pl.pallas_call(
        body, name=name, grid=(R // tr,),
        in_specs=[pl.BlockSpec((P, tr, C), lambda i: (0, i, 0)), t, t, t], out_specs=[t, t, t, t],
        out_shape=[_sds((R, C), F32)] * 4, compiler_params=_cp(("parallel",)),
    )(gparts, w, m, v)


def _sum_parts(gparts, name):
    P, R, C = gparts.shape

    def body(g_ref, o_ref):
        g = g_ref[0]
        for i in range(1, P):
            g = g + g_ref[i]
        o_ref[...] = g

    return pl.pallas_call(body, name=name, out_shape=_sds((R, C), F32))(gparts)


def _adam_w_ada(c_t, dm, w, m, v):
    R, C = w.shape

    def body(c_ref, dm_ref, w_ref, m_ref, v_ref, go_ref, d_ref, mo_ref, vo_ref):
        g = c_ref[:, 0:1] * dm_ref[0:1, :]
        for b in range(1, 8):
            g = g + c_ref[:, b:b + 1] * dm_ref[b:b + 1, :]
        d, mn, vn = _adam_math(w_ref[...], g, m_ref[...], v_ref[...])
        go_ref[...] = g
        d_ref[...] = d
        mo_ref[...] = mn
        vo_ref[...] = vn

    return pl.pallas_call(body, name="adam_w_ada", out_shape=[_sds((R, C), F32)] * 4,
                          compiler_params=_cp())(c_t, dm, w, m, v)


_SM = dict(norm_g=(0, 8), pool_w=(8, 512), pool_scale=(520, 8), rel_bias=(528, 8), final_g=(536, 8),
           dmod=(544, 192), loss=(736, 8))
_SM_ROWS = 744


def _pack_rows(a, rows):
    flat = a.reshape(-1).astype(F32)
    return jnp.pad(flat, (0, rows * 128 - flat.shape[0])).reshape(rows, 128)


def kernel(x, c, norm_g, w_ada, b_ada, w_in, pool_w, pool_scale, w_attn_br, w_pool_br, w_out, rel_bias, final_g, loss_target, m_norm_g, m_w_ada, m_b_ada, m_w_in, m_pool_w, m_pool_scale, m_w_attn_br, m_w_pool_br, m_w_out, m_rel_bias, m_final_g, v_norm_g, v_w_ada, v_b_ada, v_w_in, v_pool_w, v_pool_scale, v_w_attn_br, v_w_pool_br, v_w_out, v_rel_bias, v_final_g):
    S = x.shape[1]
    px, py, pc = _my_place()
    me = 4 * px + 2 * py + pc
    cidx = jnp.reshape(pc, (1,)).astype(jnp.int32)

    c8 = jnp.broadcast_to(c, (8, D))
    w_in_full, wab, wpb, wout, c_rows = _all_gather(
        [w_in[0].astype(CDT), w_attn_br[0].astype(CDT), w_pool_br[0].astype(CDT), w_out[0].astype(CDT), c8],
        [(D, 8192), (AW, D), (AW, D), (D, D), (64, D)],
        [(1, 1024), (1, 128), (1, 128), (0, 128), (0, 8)], "gather_weights")
    c_all = c_rows[::8]

    b_part = lax.dynamic_slice(b_ada, (0, me * 384), (1, 384))
    mod_part = _adaln_part(c_all, w_ada[0], b_part)
    (mod_all,) = _all_gather([mod_part], [(8, 8, 384)], [(None, 1)], "gather_mod")
    mod = lax.dynamic_index_in_dim(mod_all, me, axis=1, keepdims=False).reshape(1, 3 * D)
    shift, scale, gate = mod[:, :D], mod[:, D:2 * D], mod[:, 2 * D:]

    loc = _local_step(x[0], loss_target[0], shift, scale, gate, norm_g, w_in_full, pool_w[0], pool_scale, wab, wpb,
                      wout, rel_bias, final_g.reshape(1, D))

    dmod_blk = lax.dynamic_update_slice(jnp.zeros((8, 3 * D), F32), loc["dmod"], (me, 0))
    smalls = jnp.concatenate([
        _pack_rows(loc["d_norm_g"], 8), _pack_rows(loc["d_pool_w"], 512), _pack_rows(loc["d_pool_scale"], 8),
        _pack_rows(loc["d_rel_bias"], 8), _pack_rows(loc["d_final_g"], 8), _pack_rows(dmod_blk, 192),
        _pack_rows(loc["loss"], 8)], axis=0)

    dws = [loc["dw_in"], loc["dw_ab"], loc["dw_pb"], loc["dw_out"]]
    layout = [(1, 1024), (1, 128), (1, 128), (0, 128)]
    *recv, sm_recv = _reduce_pair_exchange(dws, layout, smalls)
    parts = [_pair_add(dw, r, ax, w, cidx, f"pair_add{i}") for i, (dw, r, (ax, w)) in enumerate(zip(dws, recv, layout))]
    parts.append(_add2(smalls, sm_recv, "pair_add_smalls")[None])
    g_in, g_ab, g_pb, g_out, sm4 = _reduce_chip_exchange(
        parts[:4] + [jnp.broadcast_to(parts[4], (4, _SM_ROWS, 128))])

    def sm(name):
        lo, n = _SM[name]
        return sm4[:, lo:lo + n]

    def small_adam(name, w, m, v, nvalid):
        rows = _SM[name][1]
        outs = _adam(sm(name), _pack_rows(w, rows), _pack_rows(m, rows), _pack_rows(v, rows), f"adam_{name}")
        return [o.reshape(-1)[:nvalid].reshape(w.shape) for o in outs]

    o_norm_g = small_adam("norm_g", norm_g, m_norm_g, v_norm_g, D)
    o_pool_w = small_adam("pool_w", pool_w, m_pool_w, v_pool_w, 65536)
    o_pool_scale = small_adam("pool_scale", pool_scale, m_pool_scale, v_pool_scale, AW)
    o_rel_bias = small_adam("rel_bias", rel_bias, m_rel_bias, v_rel_bias, 768)
    o_final_g = small_adam("final_g", final_g, m_final_g, v_final_g, D)

    dmod_all = _sum_parts(sm("dmod"), "sum_dmod").reshape(8, 3 * D)
    o_b_ada = [o.reshape(1, 3 * D) for o in _adam(dmod_all.reshape(8, 24, 128), b_ada.reshape(24, 128),
                                                   m_b_ada.reshape(24, 128), v_b_ada.reshape(24, 128), "adam_b_ada")]
    dm_mine = lax.dynamic_slice(dmod_all, (0, me * 384), (8, 384))
    o_w_ada = [o[None] for o in _adam_w_ada(c_all.T, dm_mine, w_ada[0], m_w_ada[0], v_w_ada[0])]

    o_w_in = [o[None] for o in _adam(g_in, w_in[0], m_w_in[0], v_w_in[0], "adam_w_in")]
    o_w_ab = [o[None] for o in _adam(g_ab, w_attn_br[0], m_w_attn_br[0], v_w_attn_br[0], "adam_w_attn_br")]
    o_w_pb = [o[None] for o in _adam(g_pb, w_pool_br[0], m_w_pool_br[0], v_w_pool_br[0], "adam_w_pool_br")]
    o_w_out = [o[None] for o in _adam(g_out, w_out[0], m_w_out[0], v_w_out[0], "adam_w_out")]

    loss = _sum_parts(sm("loss"), "sum_loss")[0, 0]

    per_w = [o_norm_g, o_w_ada, o_b_ada, o_w_in, o_pool_w, o_pool_scale, o_w_ab, o_w_pb, o_w_out, o_rel_bias, o_final_g]
    outs = [loss, loc["grad_x"][None]]
    for k in range(4):
        outs += [o[k] for o in per_w]
    return tuple(outs)
```

```python
import functools
import math

import numpy as np
import jax
import jax.numpy as jnp
from jax import lax
from jax.experimental import pallas as pl
from jax.experimental.pallas import tpu as pltpu

F32 = jnp.float32
CDT = jnp.bfloat16
D = 1024
HD = 64
NH = 8
AW = 512
QKV_W = 4608
REST_W = 3584
R16 = 16
EPS = 1e-6
NEG = -1e30
N_BACK = 128
ADAM_LR, ADAM_B1, ADAM_B2, ADAM_EPS, ADAM_WD, ADAM_STEP = 0.001, 0.9, 0.999, 1e-08, 0.01, 10
MESH = pl.DeviceIdType.MESH
VMEM_MB = 1024 * 1024


def _cp(sem=None, vmem=48, **kw):
    if sem is not None:
        kw["dimension_semantics"] = sem
    return pltpu.CompilerParams(vmem_limit_bytes=vmem * VMEM_MB, **kw)


def _sds(shape, dtype):
    return jax.ShapeDtypeStruct(tuple(shape), dtype)


def _rest_blk(jj):
    return jnp.where(jj == 0, 9, jnp.where(jj == 6, 10, jj + 10))


def _sigmoid(z):
    return 1.0 / (1.0 + jnp.exp(-z))


def _dot(a, b, dims):
    return lax.dot_general(a, b, (dims, ((), ())), preferred_element_type=F32)


NN = ((1,), (0,))
NT = ((1,), (1,))
TN = ((0,), (0,))


def _ld(ref):
    if len(ref.shape) == 2:
        return ref[...]
    return jnp.concatenate([ref[a] for a in range(ref.shape[0])], axis=0)


def _st(ref, val, cols=None):
    val = val.astype(ref.dtype)
    if len(ref.shape) == 2:
        if cols is None:
            ref[...] = val
        else:
            ref[:, cols] = val
        return
    rows = ref.shape[1]
    for a in range(ref.shape[0]):
        if cols is None:
            ref[a] = val[a * rows:(a + 1) * rows]
        else:
            ref[a, :, cols] = val[a * rows:(a + 1) * rows]


REST_COLS = ((9, 10), (11, 16), (10, 11))
_HBM = pl.BlockSpec(memory_space=pltpu.HBM)


def _resident(shape):
    return pl.BlockSpec(shape, lambda *ids: (0,) * len(shape), pipeline_mode=pl.Buffered(1))


def _proj(h, w):
    S = h.shape[0]
    tm = 256

    def body(a_ref, w_ref, q_ref, r_ref):
        a = a_ref[...]
        for j in range(9):
            q_ref[:, 512 * j:512 * (j + 1)] = _dot(a, w_ref[:, 512 * j:512 * (j + 1)], NN).astype(q_ref.dtype)
        off = 0
        for lo, hi in REST_COLS:
            for j in range(lo, hi):
                r_ref[:, off:off + 512] = _dot(a, w_ref[:, 512 * j:512 * (j + 1)], NN)
                off += 512

    return pl.pallas_call(
        body, name="proj", grid=(S // tm,),
        in_specs=[pl.BlockSpec((tm, D), lambda i: (i, 0)), _resident((D, 8192))],
        out_specs=[pl.BlockSpec((tm, QKV_W), lambda i: (i, 0)), pl.BlockSpec((tm, REST_W), lambda i: (i, 0))],
        out_shape=[_sds((S, QKV_W), CDT), _sds((S, REST_W), F32)],
        compiler_params=_cp(("parallel",), vmem=56),
    )(h, w)


def _dh(dqkv, drest, w):
    S = dqkv.shape[0]
    tm = 256

    def body(a_ref, b_ref, w_ref, o_ref):
        acc = _dot(a_ref[...], w_ref[:, 0:QKV_W], NT)
        off = 0
        for lo, hi in REST_COLS:
            n = 512 * (hi - lo)
            acc = acc + _dot(b_ref[:, off:off + n], w_ref[:, 512 * lo:512 * hi], NT)
            off += n
        o_ref[...] = acc

    return pl.pallas_call(
        body, name="dh", grid=(S // tm,),
        in_specs=[pl.BlockSpec((tm, QKV_W), lambda i: (i, 0)), pl.BlockSpec((tm, REST_W), lambda i: (i, 0)),
                  _resident((D, 8192))],
        out_specs=pl.BlockSpec((tm, D), lambda i: (i, 0)),
        out_shape=_sds((S, D), F32),
        compiler_params=_cp(("parallel",), vmem=56),
    )(dqkv, drest, w)


def _mm_tn(at, b, *, n_total, out_blk, name, prev=None, tn=512):
    R, S = at.shape
    nb = b.shape[1] // tn

    def body(*refs):
        refs[-1][...] = _dot(refs[0][...], refs[1][...], NN)

    in_specs = [_resident((R, S)), pl.BlockSpec((S, tn), lambda j: (0, j))]
    args = [at, b]
    aliases = {}
    if prev is not None:
        in_specs.append(pl.BlockSpec(memory_space=pl.ANY))
        args.append(prev)
        aliases = {2: 0}
    return pl.pallas_call(
        body, name=name, grid=(nb,), in_specs=in_specs,
        out_specs=pl.BlockSpec((R, tn), lambda j: (0, out_blk(j))),
        out_shape=_sds((R, n_total), F32),
        input_output_aliases=aliases,
        compiler_params=_cp(("parallel",)),
    )(*args)


def _to_slabs(x3):
    LS = x3.shape[0]

    def body(x_hbm, o_hbm, sems):
        cps = [pltpu.make_async_copy(x_hbm.at[:, r, :], o_hbm.at[r], sems.at[r]) for r in range(R16)]
        for cp in cps:
            cp.start()
        for cp in cps:
            cp.wait()

    return pl.pallas_call(body, name="to_slabs", in_specs=[_HBM], out_specs=_HBM,
                          out_shape=_sds((R16, LS, x3.shape[2]), x3.dtype),
                          scratch_shapes=[pltpu.SemaphoreType.DMA((R16,))])(x3)


def _from_slabs(xs):
    LS = xs.shape[1]

    def body(x_hbm, o_hbm, sems):
        cps = [pltpu.make_async_copy(x_hbm.at[r], o_hbm.at[:, r, :], sems.at[r]) for r in range(R16)]
        for cp in cps:
            cp.start()
        for cp in cps:
            cp.wait()

    return pl.pallas_call(body, name="from_slabs", in_specs=[_HBM], out_specs=_HBM,
                          out_shape=_sds((LS, R16, xs.shape[2]), xs.dtype),
                          scratch_shapes=[pltpu.SemaphoreType.DMA((R16,))])(xs)


def _norm_mod(xs, norm_g, shift, scale):
    S = xs.shape[0]
    tm = 512

    def body(x_ref, g_ref, sh_ref, sc_ref, h_ref, ht_ref):
        x = x_ref[...]
        xn = x * lax.rsqrt(jnp.mean(x * x, axis=-1, keepdims=True) + EPS)
        h = xn * (g_ref[...] * (1.0 + sc_ref[...])) + sh_ref[...]
        h_ref[...] = h.astype(h_ref.dtype)
        ht_ref[...] = h.T.astype(ht_ref.dtype)

    vec = pl.BlockSpec((1, D), lambda i: (0, 0))
    return pl.pallas_call(
        body, name="norm_mod", grid=(S // tm,),
        in_specs=[pl.BlockSpec((tm, D), lambda i: (i, 0)), vec, vec, vec],
        out_specs=[pl.BlockSpec((tm, D), lambda i: (i, 0)), pl.BlockSpec((D, tm), lambda i: (0, i))],
        out_shape=[_sds((S, D), CDT), _sds((D, S), CDT)], compiler_params=_cp(("parallel",)),
    )(xs, norm_g, shift, scale)


def _t5_bucket_np(n):
    nf = np.maximum(n, 1).astype(np.float32)
    large = 16 + (np.log(nf / np.float32(16)) / np.float32(math.log(2048 / 16)) * np.float32(16)).astype(np.int32)
    large = np.minimum(large, 31)
    return np.where(n < 16, n, large).astype(np.int32)


class _Geom:
    def __init__(self, gi, LS):
        self.gi, self.LS = gi, LS
        self.dil = (1, 4, 16)[gi]
        if gi == 0:
            self.lead, self.rows, self.n_tiles, self.has_prev = (R16,), 16, LS // 16, True
            a = np.arange(R16)[:, None]
            ll = np.arange(16)[None, :]
            self.pos = (16 * ll + a).reshape(-1)
        elif gi == 1:
            self.lead, self.rows, self.n_tiles, self.has_prev = (4,), 64, LS // 64, True
            a = np.arange(4)[:, None]
            ll = np.arange(64)[None, :]
            self.pos = (4 * ll + a).reshape(-1)
        else:
            self.lead, self.rows, self.n_tiles, self.has_prev = (), LS, 1, False
            self.pos = np.arange(LS)
        self.T = self.pos.shape[0]
        self.Tk = 2 * self.T if self.has_prev else self.T
        self.outer = {0: (), 1: (4,), 2: (R16,)}[gi]

    def view(self, arr):
        S, C = arr.shape
        if self.gi == 1:
            return arr.reshape(4, 4, self.LS, C)
        return arr.reshape(R16, self.LS, C)

    def view_shape(self, C):
        return (4, 4, self.LS, C) if self.gi == 1 else (R16, self.LS, C)

    def unview(self, arr):
        return arr.reshape(R16 * self.LS, arr.shape[-1])

    def spec(self, W, cblk, tile):
        if self.gi == 0:
            return pl.BlockSpec((R16, 16, W), lambda n: (0, tile(n), cblk))
        if self.gi == 1:
            return pl.BlockSpec((4, None, 64, W), lambda r, n: (0, r, tile(n), cblk))
        return pl.BlockSpec((None, self.LS, W), lambda r: (r, 0, cblk))

    def bucket_index(self):
        pq = self.pos[:, None]
        pk = self.pos[None, :]
        if self.has_prev:
            pk = np.concatenate([pk - self.T, pk], axis=1)
        dist = pq - pk
        ok = (dist >= 0) & (dist <= N_BACK)
        bucket = _t5_bucket_np(np.clip(dist, 0, N_BACK) * self.dil)
        return np.where(ok, bucket, -1).astype(np.int32)


def _bias_table(idx, rel_bias, gi):
    T, Tk = idx.shape
    tr = 32

    def body(idx_ref, rb_ref, o_ref):
        h = pl.program_id(0)
        ix = idx_ref[...]
        acc = jnp.full(ix.shape, NEG, F32)
        for b in range(32):
            acc = jnp.where(ix == b, rb_ref[b, gi * NH + h], acc)
        o_ref[...] = acc

    return pl.pallas_call(
        body, name=f"bias_table{gi}", grid=(NH, T // tr),
        in_specs=[pl.BlockSpec((tr, Tk), lambda h, i: (i, 0)), pl.BlockSpec(memory_space=pltpu.SMEM)],
        out_specs=pl.BlockSpec((None, tr, Tk), lambda h, i: (h, i, 0)),
        out_shape=_sds((NH, T, Tk), F32), compiler_params=_cp(("parallel", "parallel")),
    )(idx, rel_bias)


def _bias_grad(dsum, idx, gi):
    _, T, Tk = dsum.shape

    def body(ds_ref, idx_ref, o_ref, r_ref):
        def step(b, c):
            r_ref[pl.ds(b, 1), :] = jnp.sum(jnp.where(idx_ref[...] == b, ds_ref[...], 0.0), axis=0, keepdims=True)
            return c

        lax.fori_loop(0, 32, step, 0)
        o_ref[...] = jnp.broadcast_to(jnp.sum(r_ref[...], axis=1, keepdims=True), (32, 128))

    out = pl.pallas_call(
        body, name=f"bias_grad{gi}", grid=(NH,),
        in_specs=[pl.BlockSpec((None, T, Tk), lambda h: (h, 0, 0)), pl.BlockSpec((T, Tk), lambda h: (0, 0))],
        out_specs=pl.BlockSpec((None, 32, 128), lambda h: (h, 0, 0)),
        out_shape=_sds((NH, 32, 128), F32), scratch_shapes=[pltpu.VMEM((32, Tk), F32)],
        compiler_params=_cp(("parallel",)),
    )(dsum, idx)
    return out[:, :, 0].T


def _prev_mask(n, T, Tk):
    col = lax.broadcasted_iota(jnp.int32, (1, Tk), 1)
    return jnp.where((col < T) & (n == 0), NEG, 0.0).astype(F32)


def _attn_fwd(qkv, bias, geom):
    S = qkv.shape[0]
    gi, T, Tk, has_prev = geom.gi, geom.T, geom.Tk, geom.has_prev
    qv = geom.view(qkv)
    cq, ck, cv = 3 * gi, 3 * gi + 1, 3 * gi + 2

    def body(*refs):
        if has_prev:
            q_ref, kp_ref, kc_ref, vp_ref, vc_ref, b_ref, o_ref, l_ref = refs
        else:
            q_ref, kc_ref, vc_ref, b_ref, o_ref, l_ref = refs
        q = _ld(q_ref)
        if has_prev:
            n = pl.program_id(len(geom.outer))
            k = jnp.concatenate([_ld(kp_ref), _ld(kc_ref)], axis=0)
            v = jnp.concatenate([_ld(vp_ref), _ld(vc_ref)], axis=0)
            pm = _prev_mask(n, T, Tk)
        else:
            k, v = _ld(kc_ref), _ld(vc_ref)
        for h in range(NH):
            sl = slice(HD * h, HD * h + HD)
            qs = (q[:, sl].astype(F32) * 0.125).astype(q.dtype)
            s = _dot(qs, k[:, sl], NT) + b_ref[h]
            if has_prev:
                s = s + pm
            m = jnp.max(s, axis=1, keepdims=True)
            p = jnp.exp(s - m)
            l = jnp.sum(p, axis=1, keepdims=True)
            o = _dot(p.astype(v.dtype), v[:, sl], NN) / l
            _st(o_ref, o, sl)
            _st(l_ref, jnp.broadcast_to(m + jnp.log(l), (T, HD)), sl)

    cur = lambda n: n
    prev = lambda n: jnp.maximum(n - 1, 0)
    if has_prev:
        in_specs = [geom.spec(AW, cq, cur), geom.spec(AW, ck, prev), geom.spec(AW, ck, cur),
                    geom.spec(AW, cv, prev), geom.spec(AW, cv, cur)]
        args = [qv] * 5
    else:
        in_specs = [geom.spec(AW, cq, cur), geom.spec(AW, ck, cur), geom.spec(AW, cv, cur)]
        args = [qv] * 3
    nd = len(geom.outer) + (1 if has_prev else 0)
    in_specs.append(pl.BlockSpec((NH, T, Tk), lambda *ids: (0, 0, 0)))
    grid = geom.outer + ((geom.n_tiles,) if has_prev else ())
    oshape = geom.view_shape(AW)
    o, l = pl.pallas_call(
        body, name=f"attn_fwd{gi}", grid=grid, in_specs=in_specs,
        out_specs=[geom.spec(AW, 0, cur), geom.spec(AW, 0, cur)],
        out_shape=[_sds(oshape, F32), _sds(oshape, F32)],
        compiler_params=_cp(("arbitrary",) * nd),
    )(*args, bias)
    return geom.unview(o), geom.unview(l)


def _attn_combine(os_, ls_, rest):
    S = rest.shape[0]
    tm = 512

    def body(o0, o1, o2, l0, l1, l2, z_ref, attn_ref, L_ref, ag_ref, agt_ref):
        la, lb, lc = l0[...], l1[...], l2[...]
        m = jnp.maximum(jnp.maximum(la, lb), lc)
        ea, eb, ec = jnp.exp(la - m), jnp.exp(lb - m), jnp.exp(lc - m)
        den = ea + eb + ec
        attn = (ea * o0[...] + eb * o1[...] + ec * o2[...]) / den
        attn_ref[...] = attn
        L_ref[...] = m + jnp.log(den)
        z = z_ref[...]
        ag = attn * (z * _sigmoid(z))
        ag_ref[...] = ag.astype(ag_ref.dtype)
        agt_ref[...] = ag.T.astype(agt_ref.dtype)

    t = pl.BlockSpec((tm, AW), lambda i: (i, 0))
    return pl.pallas_call(
        body, name="attn_combine", grid=(S // tm,),
        in_specs=[t] * 6 + [pl.BlockSpec((tm, AW), lambda i: (i, 0))],
        out_specs=[t, t, t, pl.BlockSpec((AW, tm), lambda i: (0, i))],
        out_shape=[_sds((S, AW), F32), _sds((S, AW), F32), _sds((S, AW), CDT), _sds((AW, S), CDT)],
        compiler_params=_cp(("parallel",)),
    )(*os_, *ls_, rest)


def _pool_counts(r, LS, win):
    l = lax.broadcasted_iota(jnp.int32, (LS, 1), 0)
    return jnp.minimum(16 * l + (r + 1), win).astype(F32)


def _pool_fwd(rest, pool_w, pool_scale):
    S = rest.shape[0]
    LS = S // R16
    r3 = rest.reshape(R16, LS, REST_W)
    PG = 128

    def body(u_ref, z_ref, pw_ref, ps_ref, pt_ref, mp_ref, pg_ref, pgt_ref, d_ref):
        g = pl.program_id(0)
        win = jnp.left_shift(2, g)
        row = lax.broadcasted_iota(jnp.int32, (LS, PG), 0)
        for r in range(1, R16):
            d_ref[r] = jnp.where(row == 0, 0.0, pltpu.roll(u_ref[r], 1, 0))
        pw = pw_ref[...].astype(CDT)
        for r in range(R16):
            u = u_ref[r]
            acc = u
            for j in range(1, R16):
                coef = jnp.where(j < win, 1.0, 0.0).astype(F32)
                src = u_ref[r - j] if r >= j else d_ref[r - j + R16]
                acc = acc + coef * src
            pooled = acc / _pool_counts(r, LS, win) - u
            pc = pooled.astype(CDT)
            mp = _dot(pc, pw, NN)
            z = z_ref[r]
            pg = (mp * ps_ref[...]) * (z * _sigmoid(z))
            mp_ref[r] = mp
            pg_ref[r] = pg.astype(pg_ref.dtype)
            pt_ref[:, r * LS:(r + 1) * LS] = pooled.T.astype(pt_ref.dtype)
            pgt_ref[:, r * LS:(r + 1) * LS] = pg.T.astype(pgt_ref.dtype)

    col = lambda off: pl.BlockSpec((R16, LS, PG), lambda g: (0, 0, off + g))
    tsp = pl.BlockSpec((PG, S), lambda g: (g, 0))
    pt, mp, pg, pgt = pl.pallas_call(
        body, name="pool_fwd", grid=(4,),
        in_specs=[col(24), col(4), pl.BlockSpec((None, PG, PG), lambda g: (g, 0, 0)),
                  pl.BlockSpec((1, PG), lambda g: (0, g))],
        out_specs=[tsp, col(0), col(0), tsp],
        out_shape=[_sds((AW, S), CDT), _sds((R16, LS, AW), F32), _sds((R16, LS, AW), CDT), _sds((AW, S), CDT)],
        scratch_shapes=[pltpu.VMEM((R16, LS, PG), F32)],
        compiler_params=_cp(("parallel",)),
    )(r3, r3, pool_w, pool_scale)
    return pt, mp.reshape(S, AW), pg.reshape(S, AW), pgt


def _branch_merge(ag, pg, wab, wpb, rest):
    S = ag.shape[0]
    tm, tn = 512, 512

    def body(ag_ref, pg_ref, wa_ref, wp_ref, ga_ref, gp_ref, ya_ref, yp_ref, m_ref, mt_ref):
        ya = _dot(ag_ref[...], wa_ref[...], NN)
        yp = _dot(pg_ref[...], wp_ref[...], NN)
        ya_ref[...] = ya
        yp_ref[...] = yp
        mg = _sigmoid(ga_ref[...]) * ya + _sigmoid(gp_ref[...]) * yp
        m_ref[...] = mg.astype(m_ref.dtype)
        mt_ref[...] = mg.T.astype(mt_ref.dtype)

    a = pl.BlockSpec((tm, AW), lambda i, j: (i, 0))
    w = pl.BlockSpec((AW, tn), lambda i, j: (0, j))
    o = pl.BlockSpec((tm, tn), lambda i, j: (i, j))
    return pl.pallas_call(
        body, name="branch_merge", grid=(S // tm, D // tn),
        in_specs=[a, a, w, w, pl.BlockSpec((tm, tn), lambda i, j: (i, 2 + j)),
                  pl.BlockSpec((tm, tn), lambda i, j: (i, 4 + j))],
        out_specs=[o, o, o, pl.BlockSpec((tn, tm), lambda i, j: (j, i))],
        out_shape=[_sds((S, D), F32), _sds((S, D), F32), _sds((S, D), CDT), _sds((D, S), CDT)],
        compiler_params=_cp(("parallel", "parallel")),
    )(ag, pg, wab, wpb, rest, rest)


def _out_loss(merged, wout, xs, ts, gate, final_g):
    S = xs.shape[0]
    tm = 256

    def body(m_ref, w_ref, x_ref, t_ref, gate_ref, fg_ref, dx2_ref, dmo_ref, loss_ref, gfg_ref, dgate_ref):
        i = pl.program_id(0)

        @pl.when(i == 0)
        def _():
            loss_ref[...] = jnp.zeros_like(loss_ref)
            gfg_ref[...] = jnp.zeros_like(gfg_ref)
            dgate_ref[...] = jnp.zeros_like(dgate_ref)

        mo = _dot(m_ref[...], w_ref[...], NN)
        gate, fg = gate_ref[...], fg_ref[...]
        x2 = x_ref[...] + gate * mo
        r2 = lax.rsqrt(jnp.mean(x2 * x2, axis=-1, keepdims=True) + EPS)
        xn2 = x2 * r2
        diff = xn2 * fg - t_ref[...]
        loss = jnp.sum(jnp.sum(diff * diff, axis=-1, keepdims=True), axis=0, keepdims=True)
        dy = diff * (1.0 / D)
        dxn2 = dy * fg
        dx2 = r2 * (dxn2 - xn2 * jnp.mean(dxn2 * xn2, axis=-1, keepdims=True))
        dx2_ref[...] = dx2
        dmo_ref[...] = (dx2 * gate).astype(dmo_ref.dtype)
        loss_ref[...] += jnp.broadcast_to(loss * (0.5 / D), loss_ref.shape)
        gfg_ref[...] += jnp.sum(dy * xn2, axis=0, keepdims=True)
        dgate_ref[...] += jnp.sum(dx2 * mo, axis=0, keepdims=True)

    t = pl.BlockSpec((tm, D), lambda i: (i, 0))
    vec = pl.BlockSpec((1, D), lambda i: (0, 0))
    return pl.pallas_call(
        body, name="out_loss", grid=(S // tm,),
        in_specs=[t, _resident((D, D)), t, t, vec, vec],
        out_specs=[t, t, pl.BlockSpec((1, 128), lambda i: (0, 0)), vec, vec],
        out_shape=[_sds((S, D), F32), _sds((S, D), CDT), _sds((1, 128), F32), _sds((1, D), F32), _sds((1, D), F32)],
        compiler_params=_cp(("arbitrary",)),
    )(merged, wout, xs, ts, gate, final_g)


def _bwd_gates(dmo, wout, wab, wpb, rest, ya, yp, attn, mp, pool_scale):
    S = dmo.shape[0]
    tm = 256

    def body(dmo_ref, wo_ref, wa_ref, wp_ref, za_ref, zp_ref, ga0, ga1, gp0, gp1, ya_ref, yp_ref, at_ref, mp_ref,
             ps_ref, dya_ref, dyp_ref, dr_ref, dat_ref, dl_ref, dmp_ref, dps_ref):
        i = pl.program_id(0)

        @pl.when(i == 0)
        def _():
            dps_ref[...] = jnp.zeros_like(dps_ref)

        dm = _dot(dmo_ref[...], wo_ref[...], NT)
        sa = _sigmoid(jnp.concatenate([ga0[...], ga1[...]], axis=1))
        sp = _sigmoid(jnp.concatenate([gp0[...], gp1[...]], axis=1))
        dya = (dm * sa).astype(CDT)
        dyp = (dm * sp).astype(CDT)
        dya_ref[...] = dya
        dyp_ref[...] = dyp
        dr_ref[:, 1024:2048] = (dm * ya_ref[...] * sa * (1.0 - sa)).astype(dr_ref.dtype)
        dr_ref[:, 2048:3072] = (dm * yp_ref[...] * sp * (1.0 - sp)).astype(dr_ref.dtype)
        da = _dot(dya, wa_ref[...], NT)
        dp = _dot(dyp, wp_ref[...], NT)
        za = za_ref[...]
        sga = _sigmoid(za)
        attn = at_ref[...]
        dattn = da * (za * sga)
        dr_ref[:, 0:512] = (da * attn * (sga * (1.0 + za * (1.0 - sga)))).astype(dr_ref.dtype)
        dat_ref[...] = dattn.astype(dat_ref.dtype)
        prod = dattn * attn
        for h in range(NH):
            sl = slice(HD * h, HD * h + HD)
            dl_ref[:, sl] = jnp.broadcast_to(jnp.sum(prod[:, sl], axis=1, keepdims=True), (tm, HD))
        zp = zp_ref[...]
        sgp = _sigmoid(zp)
        mpre = mp_ref[...]
        ps = ps_ref[...]
        dmixed = dp * (zp * sgp)
        dr_ref[:, 512:1024] = (dp * (mpre * ps) * (sgp * (1.0 + zp * (1.0 - sgp)))).astype(dr_ref.dtype)
        dps_ref[...] += jnp.sum(dmixed * mpre, axis=0, keepdims=True)
        dmp_ref[...] = (dmixed * ps).astype(dmp_ref.dtype)

    full = lambda shape: pl.BlockSpec(shape, lambda i: (0, 0))
    tD = pl.BlockSpec((tm, D), lambda i: (i, 0))
    tA = pl.BlockSpec((tm, AW), lambda i: (i, 0))
    rb = lambda c: pl.BlockSpec((tm, AW), lambda i: (i, c))
    return pl.pallas_call(
        body, name="bwd_gates", grid=(S // tm,),
        in_specs=[tD, full((D, D)), full((AW, D)), full((AW, D)), rb(0), rb(1), rb(2), rb(3), rb(4), rb(5),
                  tD, tD, tA, tA, full((1, AW))],
        out_specs=[tD, tD, pl.BlockSpec((tm, 3072), lambda i: (i, 0)), tA, tA, tA, full((1, AW))],
        out_shape=[_sds((S, D), CDT), _sds((S, D), CDT), _sds((S, REST_W), CDT), _sds((S, AW), CDT),
                   _sds((S, AW), F32), _sds((S, AW), CDT), _sds((1, AW), F32)],
        compiler_params=_cp(("arbitrary",), vmem=56),
    )(dmo, wout, wab, wpb, rest, rest, rest, rest, rest, rest, ya, yp, attn, mp, pool_scale)


def _pool_bwd(dmp, pt, pool_w, drest):
    S = dmp.shape[0]
    LS = S // R16
    PG = 128
    d3 = dmp.reshape(R16, LS, AW)
    dr3 = drest.reshape(R16, LS, REST_W)

    def body(d_ref, pt_ref, pw_ref, dr_in, du_ref, dpw_ref, g_ref, s_ref):
        del dr_in
        g = pl.program_id(0)
        win = jnp.left_shift(2, g)
        row = lax.broadcasted_iota(jnp.int32, (LS, PG), 0)
        pw = pw_ref[...].astype(CDT)
        dpw = jnp.zeros((PG, PG), F32)
        for r in range(R16):
            dm = d_ref[r]
            dpw = dpw + _dot(pt_ref[:, r * LS:(r + 1) * LS], dm, NN)
            dpooled = _dot(dm, pw, NT)
            g_ref[r] = dpooled
            s_ref[r] = dpooled / _pool_counts(r, LS, win)
        dpw_ref[...] = dpw
        for r in range(R16):
            acc = s_ref[r]
            for j in range(1, R16):
                coef = jnp.where(j < win, 1.0, 0.0).astype(F32)
                if r + j < R16:
                    src = s_ref[r + j]
                else:
                    src = jnp.where(row == LS - 1, 0.0, pltpu.roll(s_ref[r + j - R16], LS - 1, 0))
                acc = acc + coef * src
            du_ref[r] = (acc - g_ref[r]).astype(du_ref.dtype)

    du, dpw = pl.pallas_call(
        body, name="pool_bwd", grid=(4,),
        in_specs=[pl.BlockSpec((R16, LS, PG), lambda g: (0, 0, g)), pl.BlockSpec((PG, S), lambda g: (g, 0)),
                  pl.BlockSpec((None, PG, PG), lambda g: (g, 0, 0)), pl.BlockSpec(memory_space=pl.ANY)],
        out_specs=[pl.BlockSpec((R16, LS, PG), lambda g: (0, 0, 24 + g)),
                   pl.BlockSpec((None, PG, PG), lambda g: (g, 0, 0))],
        out_shape=[_sds((R16, LS, REST_W), CDT), _sds((4, PG, PG), F32)],
        scratch_shapes=[pltpu.VMEM((R16, LS, PG), F32), pltpu.VMEM((R16, LS, PG), F32)],
        input_output_aliases={3: 0},
        compiler_params=_cp(("parallel",)),
    )(d3, pt, pool_w, dr3)
    return du.reshape(S, REST_W), dpw


def _attn_bwd(qkv, dattn, Lb, dlb, bias, geom, dqkv_prev):
    S = qkv.shape[0]
    gi, T, Tk, has_prev, N = geom.gi, geom.T, geom.Tk, geom.has_prev, geom.n_tiles
    nd_outer = len(geom.outer)
    qv = geom.view(qkv)
    cq, ck, cv = 3 * gi, 3 * gi + 1, 3 * gi + 2
    W3 = 3 * AW

    def heads(q, k, v, do, L, dl, b_ref, pm, dsum_ref, emit):
        for h in range(NH):
            sl = slice(HD * h, HD * h + HD)
            qs = (q[:, sl].astype(F32) * 0.125).astype(q.dtype)
            kh, vh, doh = k[:, sl], v[:, sl], do[:, sl]
            s = _dot(qs, kh, NT) + b_ref[h]
            if pm is not None:
                s = s + pm
            p = jnp.exp(s - L[:, HD * h:HD * h + 1])
            dp = _dot(doh, vh, NT)
            ds = p * (dp - dl[:, HD * h:HD * h + 1])
            dsum_ref[h] += ds
            dsc = ds.astype(q.dtype)
            dq = _dot(dsc, kh, NN) * 0.125
            dk = _dot(dsc, qs, TN)
            dv = _dot(p.astype(q.dtype), doh, TN)
            emit(sl, dq, dk, dv)

    def first_step():
        ok = pl.program_id(nd_outer) == 0 if has_prev else pl.program_id(0) == 0
        for a in range(nd_outer if has_prev else 0):
            ok = ok & (pl.program_id(a) == 0)
        return ok

    if has_prev:
        def body(q_ref, kp_ref, kc_ref, vp_ref, vc_ref, do_ref, L_ref, dl_ref, b_ref, prev_in, out_ref, dsum_ref,
                 hold, new, pp):
            del prev_in
            s_id = pl.program_id(nd_outer)

            @pl.when(first_step())
            def _():
                dsum_ref[...] = jnp.zeros_like(dsum_ref)

            @pl.when(s_id < N)
            def _():
                q = _ld(q_ref)
                k = jnp.concatenate([_ld(kp_ref), _ld(kc_ref)], axis=0)
                v = jnp.concatenate([_ld(vp_ref), _ld(vc_ref)], axis=0)
                pm = _prev_mask(s_id, T, Tk)

                def emit(sl, dq, dk, dv):
                    new[:, sl] = dq
                    new[:, slice(AW + sl.start, AW + sl.stop)] = dk[T:]
                    new[:, slice(2 * AW + sl.start, 2 * AW + sl.stop)] = dv[T:]
                    pp[:, sl] = dk[:T]
                    pp[:, slice(AW + sl.start, AW + sl.stop)] = dv[:T]

                heads(q, k, v, _ld(do_ref), _ld(L_ref), _ld(dl_ref), b_ref, pm, dsum_ref, emit)

            @pl.when((s_id >= 1) & (s_id < N))
            def _():
                _st(out_ref, hold[:, 0:AW], slice(0, AW))
                _st(out_ref, hold[:, AW:W3] + pp[...], slice(AW, W3))

            @pl.when(s_id == N)
            def _():
                _st(out_ref, hold[...])

            @pl.when(s_id < N)
            def _():
                hold[...] = new[...]

        cur = lambda n: jnp.minimum(n, N - 1)
        prev = lambda n: jnp.clip(n - 1, 0, N - 1)
        delayed = lambda n: jnp.maximum(n - 1, 0)
        in_specs = [geom.spec(AW, cq, cur), geom.spec(AW, ck, prev), geom.spec(AW, ck, cur),
                    geom.spec(AW, cv, prev), geom.spec(AW, cv, cur),
                    geom.spec(AW, 0, cur), geom.spec(AW, 0, cur), geom.spec(AW, 0, cur)]
        args = [qv] * 5 + [geom.view(dattn), geom.view(Lb), geom.view(dlb)]
        out_spec = geom.spec(W3, gi, delayed)
        grid = geom.outer + (N + 1,)
        scratch = [pltpu.VMEM((T, W3), F32), pltpu.VMEM((T, W3), F32), pltpu.VMEM((T, 2 * AW), F32)]
    else:
        def body(q_ref, kc_ref, vc_ref, do_ref, L_ref, dl_ref, b_ref, prev_in, out_ref, dsum_ref):
            del prev_in

            @pl.when(first_step())
            def _():
                dsum_ref[...] = jnp.zeros_like(dsum_ref)

            def emit(sl, dq, dk, dv):
                _st(out_ref, dq, sl)
                _st(out_ref, dk, slice(AW + sl.start, AW + sl.stop))
                _st(out_ref, dv, slice(2 * AW + sl.start, 2 * AW + sl.stop))

            heads(_ld(q_ref), _ld(kc_ref), _ld(vc_ref), _ld(do_ref), _ld(L_ref), _ld(dl_ref), b_ref, None,
                  dsum_ref, emit)

        cur = lambda n: n
        in_specs = [geom.spec(AW, cq, cur), geom.spec(AW, ck, cur), geom.spec(AW, cv, cur),
                    geom.spec(AW, 0, cur), geom.spec(AW, 0, cur), geom.spec(AW, 0, cur)]
        args = [qv] * 3 + [geom.view(dattn), geom.view(Lb), geom.view(dlb)]
        out_spec = geom.spec(W3, gi, cur)
        grid = geom.outer
        scratch = []

    in_specs.append(pl.BlockSpec((NH, T, Tk), lambda *ids: (0, 0, 0)))
    in_specs.append(pl.BlockSpec(memory_space=pl.ANY))
    if dqkv_prev is None:
        last, aliases = bias, {}
    else:
        last, aliases = geom.view(dqkv_prev), {len(in_specs) - 1: 0}
    dq, dsum = pl.pallas_call(
        body, name=f"attn_bwd{gi}", grid=grid, in_specs=in_specs,
        out_specs=[out_spec, pl.BlockSpec((NH, T, Tk), lambda *ids: (0, 0, 0))],
        out_shape=[_sds(geom.view_shape(QKV_W), CDT), _sds((NH, T, Tk), F32)],
        scratch_shapes=scratch,
        input_output_aliases=aliases,
        compiler_params=_cp(("arbitrary",) * len(grid), vmem=56),
    )(*args, bias, last)
    return geom.unview(dq), dsum


def _h_bwd(dh, dx2, xs, norm_g, scale):
    S = xs.shape[0]
    tm = 256

    def body(dh_ref, dx2_ref, x_ref, g_ref, sc_ref, gx_ref, dsh_ref, dsc_ref, dng_ref):
        i = pl.program_id(0)

        @pl.when(i == 0)
        def _():
            dsh_ref[...] = jnp.zeros_like(dsh_ref)
            dsc_ref[...] = jnp.zeros_like(dsc_ref)
            dng_ref[...] = jnp.zeros_like(dng_ref)

        g = g_ref[...]
        one_sc = 1.0 + sc_ref[...]
        x = x_ref[...]
        r1 = lax.rsqrt(jnp.mean(x * x, axis=-1, keepdims=True) + EPS)
        xn = x * r1
        dh = dh_ref[...]
        dhx = dh * xn
        dxn = dh * (g * one_sc)
        dx1 = r1 * (dxn - xn * jnp.mean(dxn * xn, axis=-1, keepdims=True))
        gx_ref[...] = dx2_ref[...] + dx1
        dsh_ref[...] += jnp.sum(dh, axis=0, keepdims=True)
        dsc_ref[...] += jnp.sum(dhx * g, axis=0, keepdims=True)
        dng_ref[...] += jnp.sum(dhx * one_sc, axis=0, keepdims=True)

    t = pl.BlockSpec((tm, D), lambda i: (i, 0))
    vec = pl.BlockSpec((1, D), lambda i: (0, 0))
    return pl.pallas_call(
        body, name="h_bwd", grid=(S // tm,),
        in_specs=[t, t, t, vec, vec], out_specs=[t, vec, vec, vec],
        out_shape=[_sds((S, D), F32), _sds((1, D), F32), _sds((1, D), F32), _sds((1, D), F32)],
        compiler_params=_cp(("arbitrary",)),
    )(dh, dx2, xs, norm_g, scale)


def _local_step(x, target, shift, scale, gate, norm_g, w_in_full, pool_w, pool_scale, wab, wpb, wout, rel_bias,
                final_g):
    S = x.shape[0]
    LS = S // R16
    xs = _to_slabs(x.reshape(LS, R16, D)).reshape(S, D)
    ts = _to_slabs(target.reshape(LS, R16, D)).reshape(S, D)
    geoms = [_Geom(gi, LS) for gi in range(3)]
    idxs = [jnp.asarray(g.bucket_index()) for g in geoms]
    biases = [_bias_table(idxs[gi], rel_bias, gi) for gi in range(3)]

    h, ht = _norm_mod(xs, norm_g, shift, scale)
    qkv, rest = _proj(h, w_in_full)

    os_, ls_ = [], []
    for gi in range(3):
        o, l = _attn_fwd(qkv, biases[gi], geoms[gi])
        os_.append(o)
        ls_.append(l)
    attn, Lb, ag, agt = _attn_combine(os_, ls_, rest)
    pt, mp, pg, pgt = _pool_fwd(rest, pool_w, pool_scale)
    ya, yp, merged, mt = _branch_merge(ag, pg, wab, wpb, rest)
    dx2, dmo, loss, gfg, dgate = _out_loss(merged, wout, xs, ts, gate, final_g)

    dya, dyp, drest, dattn, dlb, dmp, dps = _bwd_gates(dmo, wout, wab, wpb, rest, ya, yp, attn, mp, pool_scale)
    dw_out = _mm_tn(mt, dmo, n_total=D, out_blk=lambda j: j, name="dw_out")
    dw_ab = _mm_tn(agt, dya, n_total=D, out_blk=lambda j: j, name="dw_attn_br")
    dw_pb = _mm_tn(pgt, dyp, n_total=D, out_blk=lambda j: j, name="dw_pool_br")
    drest, dpw = _pool_bwd(dmp, pt, pool_w, drest)

    dqkv = None
    drb = []
    for gi in range(3):
        dqkv, dsum = _attn_bwd(qkv, dattn, Lb, dlb, biases[gi], geoms[gi], dqkv)
        drb.append(_bias_grad(dsum, idxs[gi], gi))
    d_rel_bias = jnp.concatenate(drb, axis=1)

    dw_in = _mm_tn(ht, dqkv, n_total=8192, out_blk=lambda j: j, name="dw_in_qkv")
    dw_in = _mm_tn(ht, drest, n_total=8192, out_blk=_rest_blk, name="dw_in_rest", prev=dw_in)
    dh = _dh(dqkv, drest, w_in_full)

    gxs, dsh, dsc, dng = _h_bwd(dh, dx2, xs, norm_g, scale)
    gx = _from_slabs(gxs.reshape(R16, LS, D)).reshape(S, D)
    return dict(loss=loss[0, 0], grad_x=gx, dw_in=dw_in, dw_ab=dw_ab, dw_pb=dw_pb, dw_out=dw_out,
                d_norm_g=dng, d_pool_w=dpw, d_pool_scale=dps, d_rel_bias=d_rel_bias, d_final_g=gfg,
                dmod=jnp.concatenate([dsh, dsc, dgate], axis=1))


def _my_place():
    return lax.axis_index("x"), lax.axis_index("y"), lax.axis_index("c")


def _block_of(ref, axis, idx, width):
    if axis is None:
        return ref.at[idx]
    start = pl.multiple_of(idx * width, width)
    if axis == 0:
        return ref.at[pl.ds(start, width)]
    return ref.at[:, pl.ds(start, width)]


def _all_gather(srcs, out_shapes, layout, name):
    n = len(srcs)

    def body(*refs):
        src, out = refs[:n], refs[n:2 * n]
        send_sems, recv_sems, local_sems = refs[2 * n:]
        x, y, c = _my_place()
        me, sibling = (x, y, c), (x, y, 1 - c)
        chips = [(1 - x, y), (x, 1 - y), (1 - x, 1 - y)]

        def region(p, dev):
            px, py, pc = dev
            return _block_of(out[p], layout[p][0], 4 * px + 2 * py + pc, layout[p][1])

        def copy(p, k, block, to, src_ref=None):
            return pltpu.make_async_remote_copy(
                src_ref=region(p, block) if src_ref is None else src_ref, dst_ref=region(p, block),
                send_sem=send_sems.at[p * 7 + k], recv_sem=recv_sems.at[p * 7 + k],
                device_id=to, device_id_type=MESH)

        mine = [pltpu.make_async_copy(src[p], region(p, me), local_sems.at[p]) for p in range(n)]
        for cp in mine:
            cp.start()
        first = []
        for p in range(n):
            first.append(copy(p, 0, me, sibling, src[p]))
            first += [copy(p, 1 + j, me, (*chip, c), src[p]) for j, chip in enumerate(chips)]
        for cp in first:
            cp.start()
        passed = []
        for j, chip in enumerate(chips):
            for p in range(n):
                copy(p, 1 + j, (*chip, c), me).wait_recv()
                fwd = copy(p, 4 + j, (*chip, c), sibling)
                fwd.start()
                passed.append(fwd)
        for p in range(n):
            copy(p, 0, sibling, me).wait_recv()
            for j, chip in enumerate(chips):
                copy(p, 4 + j, (*chip, 1 - c), me).wait_recv()
        for cp in first + passed:
            cp.wait_send()
        for cp in mine:
            cp.wait()

    return pl.pallas_call(
        body, name=name, in_specs=[_HBM] * n, out_specs=[_HBM] * n,
        out_shape=[_sds(s, a.dtype) for s, a in zip(out_shapes, srcs)],
        scratch_shapes=[pltpu.SemaphoreType.DMA((7 * n,)), pltpu.SemaphoreType.DMA((7 * n,)),
                        pltpu.SemaphoreType.DMA((n,))],
    )(*srcs)


def _reduce_pair_exchange(dws, layout, smalls):
    n = len(dws)

    def body(*refs):
        src, sm = refs[:n], refs[n]
        out, sm_out = refs[n + 1:2 * n + 1], refs[2 * n + 1]
        send_sems, recv_sems = refs[2 * n + 2:]
        x, y, c = _my_place()
        sibling = (x, y, 1 - c)
        cps = []
        for p in range(n):
            for q in range(4):
                cps.append(pltpu.make_async_remote_copy(
                    src_ref=_block_of(src[p], layout[p][0], 2 * q + (1 - c), layout[p][1]), dst_ref=out[p].at[q],
                    send_sem=send_sems.at[4 * p + q], recv_sem=recv_sems.at[4 * p + q],
                    device_id=sibling, device_id_type=MESH))
        cps.append(pltpu.make_async_remote_copy(src_ref=sm, dst_ref=sm_out, send_sem=send_sems.at[4 * n],
                                                recv_sem=recv_sems.at[4 * n], device_id=sibling, device_id_type=MESH))
        for cp in cps:
            cp.start()
        for cp in cps:
            cp.wait()

    def shard_shape(a, lay):
        axis, w = lay
        return (4, w, a.shape[1]) if axis == 0 else (4, a.shape[0], w)

    return pl.pallas_call(
        body, name="reduce_pair_exchange", in_specs=[_HBM] * (n + 1), out_specs=[_HBM] * (n + 1),
        out_shape=[_sds(shard_shape(a, l), F32) for a, l in zip(dws, layout)] + [_sds(smalls.shape, F32)],
        scratch_shapes=[pltpu.SemaphoreType.DMA((4 * n + 1,)), pltpu.SemaphoreType.DMA((4 * n + 1,))],
    )(*dws, smalls)


def _pair_add(dw, recv, axis, width, cidx, name):
    if axis == 0:
        tr = min(width, 256)
        C = dw.shape[1]
        grid = (4, width // tr)
        nb = width // tr
        own = pl.BlockSpec((tr, C), lambda q, i, c_ref: ((2 * q + c_ref[0]) * nb + i, 0))
        oth = pl.BlockSpec((None, tr, C), lambda q, i, c_ref: (q, i, 0))
    else:
        R = dw.shape[0]
        tr = min(R, 256)
        grid = (4, R // tr)
        own = pl.BlockSpec((tr, width), lambda q, i, c_ref: (i, 2 * q + c_ref[0]))
        oth = pl.BlockSpec((None, tr, width), lambda q, i, c_ref: (q, i, 0))

    def body(c_ref, a_ref, b_ref, o_ref):
        o_ref[...] = (a_ref[...] + b_ref[...]).astype(o_ref.dtype)

    return pl.pallas_call(
        body, name=name,
        grid_spec=pltpu.PrefetchScalarGridSpec(num_scalar_prefetch=1, grid=grid, in_specs=[own, oth], out_specs=oth),
        out_shape=_sds(recv.shape, CDT), compiler_params=_cp(("parallel", "parallel")),
    )(cidx, dw, recv)


def _add2(a, b, name):
    def body(a_ref, b_ref, o_ref):
        o_ref[...] = a_ref[...] + b_ref[...]

    return pl.pallas_call(body, name=name, out_shape=_sds(a.shape, F32))(a, b)


def _reduce_chip_exchange(parts):
    n = len(parts)

    def body(*refs):
        src, out = refs[:n], refs[n:2 * n]
        send_sems, recv_sems, local_sems = refs[2 * n:]
        x, y, c = _my_place()
        my_slot = 2 * x + y
        chips = [(1 - x, y), (x, 1 - y), (1 - x, 1 - y)]
        cps = []
        for p in range(n):
            keep = pltpu.make_async_copy(src[p].at[my_slot], out[p].at[my_slot], local_sems.at[p])
            keep.start()
            cps.append(keep)
            for k, (cx, cy) in enumerate(chips):
                cps.append(pltpu.make_async_remote_copy(
                    src_ref=src[p].at[2 * cx + cy], dst_ref=out[p].at[my_slot],
                    send_sem=send_sems.at[3 * p + k], recv_sem=recv_sems.at[3 * p + k],
                    device_id=(cx, cy, c), device_id_type=MESH))
                cps[-1].start()
        for cp in cps:
            cp.wait()

    return pl.pallas_call(
        body, name="reduce_chip_exchange", in_specs=[_HBM] * n, out_specs=[_HBM] * n,
        out_shape=[_sds(a.shape, a.dtype) for a in parts],
        scratch_shapes=[pltpu.SemaphoreType.DMA((3 * n,)), pltpu.SemaphoreType.DMA((3 * n,)),
                        pltpu.SemaphoreType.DMA((n,))],
    )(*parts)


def _adaln_part(c_all, w_ada, b_part):
    def body(c_ref, w_ref, b_ref, o_ref):
        o_ref[...] = _dot(c_ref[...].astype(CDT), w_ref[...].astype(CDT), NN) + b_ref[...]

    return pl.pallas_call(body, name="adaln_part", out_shape=_sds((8, w_ada.shape[1]), F32),
                          compiler_params=_cp())(c_all, w_ada, b_part)


def _adam_math(w, g, m, v):
    m = ADAM_B1 * m + (1.0 - ADAM_B1) * g
    v = ADAM_B2 * v + (1.0 - ADAM_B2) * (g * g)
    m_hat = m / (1.0 - ADAM_B1 ** ADAM_STEP)
    v_hat = v / (1.0 - ADAM_B2 ** ADAM_STEP)
    delta = -ADAM_LR * (m_hat / (jnp.sqrt(v_hat) + ADAM_EPS) + ADAM_WD * w)
    return delta, m, v


def _adam(gparts, w, m, v, name):
    P, R, C = gparts.shape
    tr = R if R <= 256 else 256

    def body(g_ref, w_ref, m_ref, v_ref, go_ref, d_ref, mo_ref, vo_ref):
        g = g_ref[0].astype(F32)
        for i in range(1, P):
            g = g + g_ref[i].astype(F32)
        d, mn, vn = _adam_math(w_ref[...], g, m_ref[...], v_ref[...])
        go_ref[...] = g
        d_ref[...] = d
        mo_ref[...] = mn
        vo_ref[...] = vn

    t = pl.BlockSpec((tr, C), lambda i: (i, 0))
    return pl.pallas_call(
        body, name=name, grid=(R // tr,),
        in_specs=[pl.BlockSpec((P, tr, C), lambda i: (0, i, 0)), t, t, t], out_specs=[t, t, t, t],
        out_shape=[_sds((R, C), F32)] * 4, compiler_params=_cp(("parallel",)),
    )(gparts, w, m, v)


def _sum_parts(gparts, name):
    P, R, C = gparts.shape

    def body(g_ref, o_ref):
        g = g_ref[0]
        for i in range(1, P):
            g = g + g_ref[i]
        o_ref[...] = g

    return pl.pallas_call(body, name=name, out_shape=_sds((R, C), F32))(gparts)


def _adam_w_ada(c_t, dm, w, m, v):
    R, C = w.shape

    def body(c_ref, dm_ref, w_ref, m_ref, v_ref, go_ref, d_ref, mo_ref, vo_ref):
        g = c_ref[:, 0:1] * dm_ref[0:1, :]
        for b in range(1, 8):
            g = g + c_ref[:, b:b + 1] * dm_ref[b:b + 1, :]
        d, mn, vn = _adam_math(w_ref[...], g, m_ref[...], v_ref[...])
        go_ref[...] = g
        d_ref[...] = d
        mo_ref[...] = mn
        vo_ref[...] = vn

    return pl.pallas_call(body, name="adam_w_ada", out_shape=[_sds((R, C), F32)] * 4,
                          compiler_params=_cp())(c_t, dm, w, m, v)


_SM = dict(norm_g=(0, 8), pool_w=(8, 512), pool_scale=(520, 8), rel_bias=(528, 8), final_g=(536, 8),
           dmod=(544, 192), loss=(736, 8))
_SM_ROWS = 744


def _pack_rows(a, rows):
    flat = a.reshape(-1).astype(F32)
    return jnp.pad(flat, (0, rows * 128 - flat.shape[0])).reshape(rows, 128)


def kernel(x, c, norm_g, w_ada, b_ada, w_in, pool_w, pool_scale, w_attn_br, w_pool_br, w_out, rel_bias, final_g, loss_target, m_norm_g, m_w_ada, m_b_ada, m_w_in, m_pool_w, m_pool_scale, m_w_attn_br, m_w_pool_br, m_w_out, m_rel_bias, m_final_g, v_norm_g, v_w_ada, v_b_ada, v_w_in, v_pool_w, v_pool_scale, v_w_attn_br, v_w_pool_br, v_w_out, v_rel_bias, v_final_g):
    S = x.shape[1]
    px, py, pc = _my_place()
    me = 4 * px + 2 * py + pc
    cidx = jnp.reshape(pc, (1,)).astype(jnp.int32)

    c8 = jnp.broadcast_to(c, (8, D))
    w_in_full, wab, wpb, wout, c_rows = _all_gather(
        [w_in[0].astype(CDT), w_attn_br[0].astype(CDT), w_pool_br[0].astype(CDT), w_out[0].astype(CDT), c8],
        [(D, 8192), (AW, D), (AW, D), (D, D), (64, D)],
        [(1, 1024), (1, 128), (1, 128), (0, 128), (0, 8)], "gather_weights")
    c_all = c_rows[::8]

    b_part = lax.dynamic_slice(b_ada, (0, me * 384), (1, 384))
    mod_part = _adaln_part(c_all, w_ada[0], b_part)
    (mod_all,) = _all_gather([mod_part], [(8, 8, 384)], [(None, 1)], "gather_mod")
    mod = lax.dynamic_index_in_dim(mod_all, me, axis=1, keepdims=False).reshape(1, 3 * D)
    shift, scale, gate = mod[:, :D], mod[:, D:2 * D], mod[:, 2 * D:]

    loc = _local_step(x[0], loss_target[0], shift, scale, gate, norm_g, w_in_full, pool_w[0], pool_scale, wab, wpb,
                      wout, rel_bias, final_g.reshape(1, D))

    dmod_blk = lax.dynamic_update_slice(jnp.zeros((8, 3 * D), F32), loc["dmod"], (me, 0))
    smalls = jnp.concatenate([
        _pack_rows(loc["d_norm_g"], 8), _pack_rows(loc["d_pool_w"], 512), _pack_rows(loc["d_pool_scale"], 8),
        _pack_rows(loc["d_rel_bias"], 8), _pack_rows(loc["d_final_g"], 8), _pack_rows(dmod_blk, 192),
        _pack_rows(loc["loss"], 8)], axis=0)

    dws = [loc["dw_in"], loc["dw_ab"], loc["dw_pb"], loc["dw_out"]]
    layout = [(1, 1024), (1, 128), (1, 128), (0, 128)]
    *recv, sm_recv = _reduce_pair_exchange(dws, layout, smalls)
    parts = [_pair_add(dw, r, ax, w, cidx, f"pair_add{i}") for i, (dw, r, (ax, w)) in enumerate(zip(dws, recv, layout))]
    parts.append(_add2(smalls, sm_recv, "pair_add_smalls")[None])
    g_in, g_ab, g_pb, g_out, sm4 = _reduce_chip_exchange(
        parts[:4] + [jnp.broadcast_to(parts[4], (4, _SM_ROWS, 128))])

    def sm(name):
        lo, n = _SM[name]
        return sm4[:, lo:lo + n]

    def small_adam(name, w, m, v, nvalid):
        rows = _SM[name][1]
        outs = _adam(sm(name), _pack_rows(w, rows), _pack_rows(m, rows), _pack_rows(v, rows), f"adam_{name}")
        return [o.reshape(-1)[:nvalid].reshape(w.shape) for o in outs]

    o_norm_g = small_adam("norm_g", norm_g, m_norm_g, v_norm_g, D)
    o_pool_w = small_adam("pool_w", pool_w, m_pool_w, v_pool_w, 65536)
    o_pool_scale = small_adam("pool_scale", pool_scale, m_pool_scale, v_pool_scale, AW)
    o_rel_bias = small_adam("rel_bias", rel_bias, m_rel_bias, v_rel_bias, 768)
    o_final_g = small_adam("final_g", final_g, m_final_g, v_final_g, D)

    dmod_all = _sum_parts(sm("dmod"), "sum_dmod").reshape(8, 3 * D)
    o_b_ada = [o.reshape(1, 3 * D) for o in _adam(dmod_all.reshape(8, 24, 128), b_ada.reshape(24, 128),
                                                   m_b_ada.reshape(24, 128), v_b_ada.reshape(24, 128), "adam_b_ada")]
    dm_mine = lax.dynamic_slice(dmod_all, (0, me * 384), (8, 384))
    o_w_ada = [o[None] for o in _adam_w_ada(c_all.T, dm_mine, w_ada[0], m_w_ada[0], v_w_ada[0])]

    o_w_in = [o[None] for o in _adam(g_in, w_in[0], m_w_in[0], v_w_in[0], "adam_w_in")]
    o_w_ab = [o[None] for o in _adam(g_ab, w_attn_br[0], m_w_attn_br[0], v_w_attn_br[0], "adam_w_attn_br")]
    o_w_pb = [o[None] for o in _adam(g_pb, w_pool_br[0], m_w_pool_br[0], v_w_pool_br[0], "adam_w_pool_br")]
    o_w_out = [o[None] for o in _adam(g_out, w_out[0], m_w_out[0], v_w_out[0], "adam_w_out")]

    loss = _sum_parts(sm("loss"), "sum_loss")[0, 0]

    per_w = [o_norm_g, o_w_ada, o_b_ada, o_w_in, o_pool_w, o_pool_scale, o_w_ab, o_w_pb, o_w_out, o_rel_bias, o_final_g]
    outs = [loss, loc["grad_x"][None]]
    for k in range(4):
        outs += [o[k] for o in per_w]
    return tuple(outs)
```

```python
import functools
import math

import numpy as np
import jax
import jax.numpy as jnp
from jax import lax
from jax.experimental import pallas as pl
from jax.experimental.pallas import tpu as pltpu

F32 = jnp.float32
CDT = jnp.bfloat16
D = 1024
HD = 64
NH = 8
AW = 512
QKV_W = 4608
REST_W = 3584
R16 = 16
EPS = 1e-6
NEG = -1e30
N_BACK = 128
ADAM_LR, ADAM_B1, ADAM_B2, ADAM_EPS, ADAM_WD, ADAM_STEP = 0.001, 0.9, 0.999, 1e-08, 0.01, 10
MESH = pl.DeviceIdType.MESH
VMEM_MB = 1024 * 1024


def _cp(sem=None, vmem=48, **kw):
    if sem is not None:
        kw["dimension_semantics"] = sem
    return pltpu.CompilerParams(vmem_limit_bytes=vmem * VMEM_MB, **kw)


def _sds(shape, dtype):
    return jax.ShapeDtypeStruct(tuple(shape), dtype)


def _rest_blk(jj):
    return jnp.where(jj == 0, 9, jnp.where(jj == 6, 10, jj + 10))


def _sigmoid(z):
    return 1.0 / (1.0 + jnp.exp(-z))


def _dot(a, b, dims):
    return lax.dot_general(a, b, (dims, ((), ())), preferred_element_type=F32)


NN = ((1,), (0,))
NT = ((1,), (1,))
TN = ((0,), (0,))


def _ld(ref):
    if len(ref.shape) == 2:
        return ref[...]
    return jnp.concatenate([ref[a] for a in range(ref.shape[0])], axis=0)


def _st(ref, val, cols=None):
    val = val.astype(ref.dtype)
    if len(ref.shape) == 2:
        if cols is None:
            ref[...] = val
        else:
            ref[:, cols] = val
        return
    rows = ref.shape[1]
    for a in range(ref.shape[0]):
        if cols is None:
            ref[a] = val[a * rows:(a + 1) * rows]
        else:
            ref[a, :, cols] = val[a * rows:(a + 1) * rows]


REST_COLS = ((9, 10), (11, 16), (10, 11))
_HBM = pl.BlockSpec(memory_space=pltpu.HBM)


def _resident(shape):
    return pl.BlockSpec(shape, lambda *ids: (0,) * len(shape), pipeline_mode=pl.Buffered(1))


def _proj(h, w):
    S = h.shape[0]
    tm = 256

    def body(a_ref, w_ref, q_ref, r_ref):
        a = a_ref[...]
        for j in range(9):
            q_ref[:, 512 * j:512 * (j + 1)] = _dot(a, w_ref[:, 512 * j:512 * (j + 1)], NN).astype(q_ref.dtype)
        off = 0
        for lo, hi in REST_COLS:
            for j in range(lo, hi):
                r_ref[:, off:off + 512] = _dot(a, w_ref[:, 512 * j:512 * (j + 1)], NN)
                off += 512

    return pl.pallas_call(
        body, name="proj", grid=(S // tm,),
        in_specs=[pl.BlockSpec((tm, D), lambda i: (i, 0)), _resident((D, 8192))],
        out_specs=[pl.BlockSpec((tm, QKV_W), lambda i: (i, 0)), pl.BlockSpec((tm, REST_W), lambda i: (i, 0))],
        out_shape=[_sds((S, QKV_W), CDT), _sds((S, REST_W), F32)],
        compiler_params=_cp(("parallel",), vmem=56),
    )(h, w)


def _dh(dqkv, drest, w):
    S = dqkv.shape[0]
    tm = 256

    def body(a_ref, b_ref, w_ref, o_ref):
        acc = _dot(a_ref[...], w_ref[:, 0:QKV_W], NT)
        off = 0
        for lo, hi in REST_COLS:
            n = 512 * (hi - lo)
            acc = acc + _dot(b_ref[:, off:off + n], w_ref[:, 512 * lo:512 * hi], NT)
            off += n
        o_ref[...] = acc

    return pl.pallas_call(
        body, name="dh", grid=(S // tm,),
        in_specs=[pl.BlockSpec((tm, QKV_W), lambda i: (i, 0)), pl.BlockSpec((tm, REST_W), lambda i: (i, 0)),
                  _resident((D, 8192))],
        out_specs=pl.BlockSpec((tm, D), lambda i: (i, 0)),
        out_shape=_sds((S, D), F32),
        compiler_params=_cp(("parallel",), vmem=56),
    )(dqkv, drest, w)


def _mm_tn(at, b, *, n_total, out_blk, name, prev=None, tn=512):
    R, S = at.shape
    nb = b.shape[1] // tn

    def body(*refs):
        refs[-1][...] = _dot(refs[0][...], refs[1][...], NN)

    in_specs = [_resident((R, S)), pl.BlockSpec((S, tn), lambda j: (0, j))]
    args = [at, b]
    aliases = {}
    if prev is not None:
        in_specs.append(pl.BlockSpec(memory_space=pl.ANY))
        args.append(prev)
        aliases = {2: 0}
    return pl.pallas_call(
        body, name=name, grid=(nb,), in_specs=in_specs,
        out_specs=pl.BlockSpec((R, tn), lambda j: (0, out_blk(j))),
        out_shape=_sds((R, n_total), F32),
        input_output_aliases=aliases,
        compiler_params=_cp(("parallel",)),
    )(*args)


def _tcast(a, name):
    M, C = a.shape
    tm = 512

    def body(a_ref, o_ref):
        o_ref[...] = a_ref[...].astype(F32).T.astype(o_ref.dtype)

    return pl.pallas_call(
        body, name=name, grid=(M // tm,),
        in_specs=[pl.BlockSpec((tm, C), lambda i: (i, 0))],
        out_specs=pl.BlockSpec((C, tm), lambda i: (0, i)),
        out_shape=_sds((C, M), CDT), compiler_params=_cp(("parallel",)),
    )(a)


def _norm_mod(x3, norm_g, shift, scale):
    LS = x3.shape[0]
    tl = 16

    def body(x_ref, g_ref, sh_ref, sc_ref, h_ref, xs_ref):
        gg = g_ref[...] * (1.0 + sc_ref[...])
        for r in range(R16):
            xr = x_ref[:, r, :]
            xs_ref[r] = xr
            xn = xr * lax.rsqrt(jnp.mean(xr * xr, axis=-1, keepdims=True) + EPS)
            h_ref[r] = (xn * gg + sh_ref[...]).astype(h_ref.dtype)

    vec = pl.BlockSpec((1, D), lambda i: (0, 0))
    slab = pl.BlockSpec((R16, tl, D), lambda i: (0, i, 0))
    return pl.pallas_call(
        body, name="norm_mod", grid=(LS // tl,),
        in_specs=[pl.BlockSpec((tl, R16, D), lambda i: (i, 0, 0)), vec, vec, vec],
        out_specs=[slab, slab],
        out_shape=[_sds((R16, LS, D), CDT), _sds((R16, LS, D), F32)], compiler_params=_cp(("parallel",)),
    )(x3, norm_g, shift, scale)


def _t5_bucket_np(n):
    nf = np.maximum(n, 1).astype(np.float32)
    large = 16 + (np.log(nf / np.float32(16)) / np.float32(math.log(2048 / 16)) * np.float32(16)).astype(np.int32)
    large = np.minimum(large, 31)
    return np.where(n < 16, n, large).astype(np.int32)


class _Geom:
    def __init__(self, gi, LS):
        self.gi, self.LS = gi, LS
        self.dil = (1, 4, 16)[gi]
        if gi == 0:
            self.lead, self.rows, self.n_tiles, self.has_prev = (R16,), 16, LS // 16, True
            a = np.arange(R16)[:, None]
            ll = np.arange(16)[None, :]
            self.pos = (16 * ll + a).reshape(-1)
        elif gi == 1:
            self.lead, self.rows, self.n_tiles, self.has_prev = (4,), 64, LS // 64, True
            a = np.arange(4)[:, None]
            ll = np.arange(64)[None, :]
            self.pos = (4 * ll + a).reshape(-1)
        else:
            self.lead, self.rows, self.n_tiles, self.has_prev = (), LS, 1, False
            self.pos = np.arange(LS)
        self.T = self.pos.shape[0]
        self.Tk = 2 * self.T if self.has_prev else self.T
        self.outer = {0: (), 1: (4,), 2: (R16,)}[gi]

    def view(self, arr):
        S, C = arr.shape
        if self.gi == 1:
            return arr.reshape(4, 4, self.LS, C)
        return arr.reshape(R16, self.LS, C)

    def view_shape(self, C):
        return (4, 4, self.LS, C) if self.gi == 1 else (R16, self.LS, C)

    def unview(self, arr):
        return arr.reshape(R16 * self.LS, arr.shape[-1])

    def spec(self, W, cblk, tile):
        if self.gi == 0:
            return pl.BlockSpec((R16, 16, W), lambda n: (0, tile(n), cblk))
        if self.gi == 1:
            return pl.BlockSpec((4, None, 64, W), lambda r, n: (0, r, tile(n), cblk))
        return pl.BlockSpec((None, self.LS, W), lambda r: (r, 0, cblk))

    def bucket_index(self):
        pq = self.pos[:, None]
        pk = self.pos[None, :]
        if self.has_prev:
            pk = np.concatenate([pk - self.T, pk], axis=1)
        dist = pq - pk
        ok = (dist >= 0) & (dist <= N_BACK)
        bucket = _t5_bucket_np(np.clip(dist, 0, N_BACK) * self.dil)
        return np.where(ok, bucket, -1).astype(np.int32)


def _bias_table(idx, rel_bias, gi):
    T, Tk = idx.shape
    tr = 32

    def body(idx_ref, rb_ref, o_ref):
        h = pl.program_id(0)
        ix = idx_ref[...]
        acc = jnp.full(ix.shape, NEG, F32)
        for b in range(32):
            acc = jnp.where(ix == b, rb_ref[b, gi * NH + h], acc)
        o_ref[...] = acc

    return pl.pallas_call(
        body, name=f"bias_table{gi}", grid=(NH, T // tr),
        in_specs=[pl.BlockSpec((tr, Tk), lambda h, i: (i, 0)), pl.BlockSpec(memory_space=pltpu.SMEM)],
        out_specs=pl.BlockSpec((None, tr, Tk), lambda h, i: (h, i, 0)),
        out_shape=_sds((NH, T, Tk), F32), compiler_params=_cp(("parallel", "parallel")),
    )(idx, rel_bias)


def _bias_grad(dsum, idx, gi):
    _, T, Tk = dsum.shape

    def body(ds_ref, idx_ref, o_ref, r_ref):
        def step(b, c):
            r_ref[pl.ds(b, 1), :] = jnp.sum(jnp.where(idx_ref[...] == b, ds_ref[...], 0.0), axis=0, keepdims=True)
            return c

        lax.fori_loop(0, 32, step, 0)
        o_ref[...] = jnp.broadcast_to(jnp.sum(r_ref[...], axis=1, keepdims=True), (32, 128))

    out = pl.pallas_call(
        body, name=f"bias_grad{gi}", grid=(NH,),
        in_specs=[pl.BlockSpec((None, T, Tk), lambda h: (h, 0, 0)), pl.BlockSpec((T, Tk), lambda h: (0, 0))],
        out_specs=pl.BlockSpec((None, 32, 128), lambda h: (h, 0, 0)),
        out_shape=_sds((NH, 32, 128), F32), scratch_shapes=[pltpu.VMEM((32, Tk), F32)],
        compiler_params=_cp(("parallel",)),
    )(dsum, idx)
    return out[:, :, 0].T


def _prev_mask(n, T, Tk):
    col = lax.broadcasted_iota(jnp.int32, (1, Tk), 1)
    return jnp.where((col < T) & (n == 0), NEG, 0.0).astype(F32)


def _attn_fwd(qkv, bias, geom):
    S = qkv.shape[0]
    gi, T, Tk, has_prev = geom.gi, geom.T, geom.Tk, geom.has_prev
    qv = geom.view(qkv)
    cq, ck, cv = 3 * gi, 3 * gi + 1, 3 * gi + 2

    def body(*refs):
        if has_prev:
            q_ref, kp_ref, kc_ref, vp_ref, vc_ref, b_ref, o_ref, l_ref = refs
        else:
            q_ref, kc_ref, vc_ref, b_ref, o_ref, l_ref = refs
        q = _ld(q_ref)
        if has_prev:
            n = pl.program_id(len(geom.outer))
            k = jnp.concatenate([_ld(kp_ref), _ld(kc_ref)], axis=0)
            v = jnp.concatenate([_ld(vp_ref), _ld(vc_ref)], axis=0)
            pm = _prev_mask(n, T, Tk)
        else:
            k, v = _ld(kc_ref), _ld(vc_ref)
        for h in range(NH):
            sl = slice(HD * h, HD * h + HD)
            qs = (q[:, sl].astype(F32) * 0.125).astype(q.dtype)
            s = _dot(qs, k[:, sl], NT) + b_ref[h]
            if has_prev:
                s = s + pm
            m = jnp.max(s, axis=1, keepdims=True)
            p = jnp.exp(s - m)
            l = jnp.sum(p, axis=1, keepdims=True)
            o = _dot(p.astype(v.dtype), v[:, sl], NN) / l
            _st(o_ref, o, sl)
            _st(l_ref, jnp.broadcast_to(m + jnp.log(l), (T, HD)), sl)

    cur = lambda n: n
    prev = lambda n: jnp.maximum(n - 1, 0)
    if has_prev:
        in_specs = [geom.spec(AW, cq, cur), geom.spec(AW, ck, prev), geom.spec(AW, ck, cur),
                    geom.spec(AW, cv, prev), geom.spec(AW, cv, cur)]
        args = [qv] * 5
    else:
        in_specs = [geom.spec(AW, cq, cur), geom.spec(AW, ck, cur), geom.spec(AW, cv, cur)]
        args = [qv] * 3
    nd = len(geom.outer) + (1 if has_prev else 0)
    in_specs.append(pl.BlockSpec((NH, T, Tk), lambda *ids: (0, 0, 0)))
    grid = geom.outer + ((geom.n_tiles,) if has_prev else ())
    oshape = geom.view_shape(AW)
    o, l = pl.pallas_call(
        body, name=f"attn_fwd{gi}", grid=grid, in_specs=in_specs,
        out_specs=[geom.spec(AW, 0, cur), geom.spec(AW, 0, cur)],
        out_shape=[_sds(oshape, F32), _sds(oshape, F32)],
        compiler_params=_cp(("arbitrary",) * nd),
    )(*args, bias)
    return geom.unview(o), geom.unview(l)


def _attn_combine(os_, ls_, rest):
    S = rest.shape[0]
    tm = 512

    def body(o0, o1, o2, l0, l1, l2, z_ref, attn_ref, L_ref, ag_ref, agt_ref):
        la, lb, lc = l0[...], l1[...], l2[...]
        m = jnp.maximum(jnp.maximum(la, lb), lc)
        ea, eb, ec = jnp.exp(la - m), jnp.exp(lb - m), jnp.exp(lc - m)
        den = ea + eb + ec
        attn = (ea * o0[...] + eb * o1[...] + ec * o2[...]) / den
        attn_ref[...] = attn
        L_ref[...] = m + jnp.log(den)
        z = z_ref[...]
        ag = attn * (z * _sigmoid(z))
        ag_ref[...] = ag.astype(ag_ref.dtype)
        agt_ref[...] = ag.T.astype(agt_ref.dtype)

    t = pl.BlockSpec((tm, AW), lambda i: (i, 0))
    return pl.pallas_call(
        body, name="attn_combine", grid=(S // tm,),
        in_specs=[t] * 6 + [pl.BlockSpec((tm, AW), lambda i: (i, 0))],
        out_specs=[t, t, t, pl.BlockSpec((AW, tm), lambda i: (0, i))],
        out_shape=[_sds((S, AW), F32), _sds((S, AW), F32), _sds((S, AW), CDT), _sds((AW, S), CDT)],
        compiler_params=_cp(("parallel",)),
    )(*os_, *ls_, rest)


def _pool_counts(r, LS, win):
    l = lax.broadcasted_iota(jnp.int32, (LS, 1), 0)
    return jnp.minimum(16 * l + (r + 1), win).astype(F32)


def _pool_fwd(rest, pool_w, pool_scale):
    S = rest.shape[0]
    LS = S // R16
    r3 = rest.reshape(R16, LS, REST_W)
    PG = 128

    def body(u_ref, z_ref, pw_ref, ps_ref, pt_ref, mp_ref, pg_ref, pgt_ref, d_ref):
        g = pl.program_id(0)
        win = jnp.left_shift(2, g)
        row = lax.broadcasted_iota(jnp.int32, (LS, PG), 0)
        for r in range(1, R16):
            d_ref[r] = jnp.where(row == 0, 0.0, pltpu.roll(u_ref[r], 1, 0))
        pw = pw_ref[...].astype(CDT)
        for r in range(R16):
            u = u_ref[r]
            acc = u
            for j in range(1, R16):
                coef = jnp.where(j < win, 1.0, 0.0).astype(F32)
                src = u_ref[r - j] if r >= j else d_ref[r - j + R16]
                acc = acc + coef * src
            pooled = acc / _pool_counts(r, LS, win) - u
            pc = pooled.astype(CDT)
            mp = _dot(pc, pw, NN)
            z = z_ref[r]
            pg = (mp * ps_ref[...]) * (z * _sigmoid(z))
            mp_ref[r] = mp
            pg_ref[r] = pg.astype(pg_ref.dtype)
            pt_ref[:, r * LS:(r + 1) * LS] = pooled.T.astype(pt_ref.dtype)
            pgt_ref[:, r * LS:(r + 1) * LS] = pg.T.astype(pgt_ref.dtype)

    col = lambda off: pl.BlockSpec((R16, LS, PG), lambda g: (0, 0, off + g))
    tsp = pl.BlockSpec((PG, S), lambda g: (g, 0))
    pt, mp, pg, pgt = pl.pallas_call(
        body, name="pool_fwd", grid=(4,),
        in_specs=[col(24), col(4), pl.BlockSpec((None, PG, PG), lambda g: (g, 0, 0)),
                  pl.BlockSpec((1, PG), lambda g: (0, g))],
        out_specs=[tsp, col(0), col(0), tsp],
        out_shape=[_sds((AW, S), CDT), _sds((R16, LS, AW), F32), _sds((R16, LS, AW), CDT), _sds((AW, S), CDT)],
        scratch_shapes=[pltpu.VMEM((R16, LS, PG), F32)],
        compiler_params=_cp(("parallel",)),
    )(r3, r3, pool_w, pool_scale)
    return pt, mp.reshape(S, AW), pg.reshape(S, AW), pgt


def _branch_merge(ag, pg, wab, wpb, rest):
    S = ag.shape[0]
    tm, tn = 512, 512

    def body(ag_ref, pg_ref, wa_ref, wp_ref, ga_ref, gp_ref, ya_ref, yp_ref, m_ref, mt_ref):
        ya = _dot(ag_ref[...], wa_ref[...], NN)
        yp = _dot(pg_ref[...], wp_ref[...], NN)
        ya_ref[...] = ya
        yp_ref[...] = yp
        mg = _sigmoid(ga_ref[...]) * ya + _sigmoid(gp_ref[...]) * yp
        m_ref[...] = mg.astype(m_ref.dtype)
        mt_ref[...] = mg.T.astype(mt_ref.dtype)

    a = pl.BlockSpec((tm, AW), lambda i, j: (i, 0))
    w = pl.BlockSpec((AW, tn), lambda i, j: (0, j))
    o = pl.BlockSpec((tm, tn), lambda i, j: (i, j))
    return pl.pallas_call(
        body, name="branch_merge", grid=(S // tm, D // tn),
        in_specs=[a, a, w, w, pl.BlockSpec((tm, tn), lambda i, j: (i, 2 + j)),
                  pl.BlockSpec((tm, tn), lambda i, j: (i, 4 + j))],
        out_specs=[o, o, o, pl.BlockSpec((tn, tm), lambda i, j: (j, i))],
        out_shape=[_sds((S, D), F32), _sds((S, D), F32), _sds((S, D), CDT), _sds((D, S), CDT)],
        compiler_params=_cp(("parallel", "parallel")),
    )(ag, pg, wab, wpb, rest, rest)


def _out_loss(merged, wout, xs, ts, gate, final_g):
    LS = xs.shape[1]
    tl = 16

    def body(m_ref, w_ref, x_ref, t_ref, gate_ref, fg_ref, dx2_ref, dmo_ref, loss_ref, gfg_ref, dgate_ref):
        i = pl.program_id(0)

        @pl.when(i == 0)
        def _():
            loss_ref[...] = jnp.zeros_like(loss_ref)
            gfg_ref[...] = jnp.zeros_like(gfg_ref)
            dgate_ref[...] = jnp.zeros_like(dgate_ref)

        mo = _dot(_ld(m_ref), w_ref[...], NN)
        gate, fg = gate_ref[...], fg_ref[...]
        tgt = jnp.concatenate([t_ref[:, r, :] for r in range(R16)], axis=0)
        x2 = _ld(x_ref) + gate * mo
        r2 = lax.rsqrt(jnp.mean(x2 * x2, axis=-1, keepdims=True) + EPS)
        xn2 = x2 * r2
        diff = xn2 * fg - tgt
        loss = jnp.sum(jnp.sum(diff * diff, axis=-1, keepdims=True), axis=0, keepdims=True)
        dy = diff * (1.0 / D)
        dxn2 = dy * fg
        dx2 = r2 * (dxn2 - xn2 * jnp.mean(dxn2 * xn2, axis=-1, keepdims=True))
        _st(dx2_ref, dx2)
        _st(dmo_ref, dx2 * gate)
        loss_ref[...] += jnp.broadcast_to(loss * (0.5 / D), loss_ref.shape)
        gfg_ref[...] += jnp.sum(dy * xn2, axis=0, keepdims=True)
        dgate_ref[...] += jnp.sum(dx2 * mo, axis=0, keepdims=True)

    slab = pl.BlockSpec((R16, tl, D), lambda i: (0, i, 0))
    vec = pl.BlockSpec((1, D), lambda i: (0, 0))
    return pl.pallas_call(
        body, name="out_loss", grid=(LS // tl,),
        in_specs=[slab, _resident((D, D)), slab, pl.BlockSpec((tl, R16, D), lambda i: (i, 0, 0)), vec, vec],
        out_specs=[slab, slab, pl.BlockSpec((1, 128), lambda i: (0, 0)), vec, vec],
        out_shape=[_sds((R16, LS, D), F32), _sds((R16, LS, D), CDT), _sds((1, 128), F32), _sds((1, D), F32),
                   _sds((1, D), F32)],
        compiler_params=_cp(("arbitrary",)),
    )(merged, wout, xs, ts, gate, final_g)


def _bwd_gates(dmo, wout, wab, wpb, rest, ya, yp, attn, mp, pool_scale):
    S = dmo.shape[0]
    tm = 256

    def body(dmo_ref, wo_ref, wa_ref, wp_ref, za_ref, zp_ref, ga0, ga1, gp0, gp1, ya_ref, yp_ref, at_ref, mp_ref,
             ps_ref, dya_ref, dyp_ref, dr_ref, dat_ref, dl_ref, dmp_ref, dps_ref):
        i = pl.program_id(0)

        @pl.when(i == 0)
        def _():
            dps_ref[...] = jnp.zeros_like(dps_ref)

        dm = _dot(dmo_ref[...], wo_ref[...], NT)
        sa = _sigmoid(jnp.concatenate([ga0[...], ga1[...]], axis=1))
        sp = _sigmoid(jnp.concatenate([gp0[...], gp1[...]], axis=1))
        dya = (dm * sa).astype(CDT)
        dyp = (dm * sp).astype(CDT)
        dya_ref[...] = dya
        dyp_ref[...] = dyp
        dr_ref[:, 1024:2048] = (dm * ya_ref[...] * sa * (1.0 - sa)).astype(dr_ref.dtype)
        dr_ref[:, 2048:3072] = (dm * yp_ref[...] * sp * (1.0 - sp)).astype(dr_ref.dtype)
        da = _dot(dya, wa_ref[...], NT)
        dp = _dot(dyp, wp_ref[...], NT)
        za = za_ref[...]
        sga = _sigmoid(za)
        attn = at_ref[...]
        dattn = da * (za * sga)
        dr_ref[:, 0:512] = (da * attn * (sga * (1.0 + za * (1.0 - sga)))).astype(dr_ref.dtype)
        dat_ref[...] = dattn.astype(dat_ref.dtype)
        prod = dattn * attn
        for h in range(NH):
            sl = slice(HD * h, HD * h + HD)
            dl_ref[:, sl] = jnp.broadcast_to(jnp.sum(prod[:, sl], axis=1, keepdims=True), (tm, HD))
        zp = zp_ref[...]
        sgp = _sigmoid(zp)
        mpre = mp_ref[...]
        ps = ps_ref[...]
        dmixed = dp * (zp * sgp)
        dr_ref[:, 512:1024] = (dp * (mpre * ps) * (sgp * (1.0 + zp * (1.0 - sgp)))).astype(dr_ref.dtype)
        dps_ref[...] += jnp.sum(dmixed * mpre, axis=0, keepdims=True)
        dmp_ref[...] = (dmixed * ps).astype(dmp_ref.dtype)

    full = lambda shape: pl.BlockSpec(shape, lambda i: (0, 0))
    tD = pl.BlockSpec((tm, D), lambda i: (i, 0))
    tA = pl.BlockSpec((tm, AW), lambda i: (i, 0))
    rb = lambda c: pl.BlockSpec((tm, AW), lambda i: (i, c))
    return pl.pallas_call(
        body, name="bwd_gates", grid=(S // tm,),
        in_specs=[tD, full((D, D)), full((AW, D)), full((AW, D)), rb(0), rb(1), rb(2), rb(3), rb(4), rb(5),
                  tD, tD, tA, tA, full((1, AW))],
        out_specs=[tD, tD, pl.BlockSpec((tm, 3072), lambda i: (i, 0)), tA, tA, tA, full((1, AW))],
        out_shape=[_sds((S, D), CDT), _sds((S, D), CDT), _sds((S, REST_W), CDT), _sds((S, AW), CDT),
                   _sds((S, AW), F32), _sds((S, AW), CDT), _sds((1, AW), F32)],
        compiler_params=_cp(("arbitrary",), vmem=56),
    )(dmo, wout, wab, wpb, rest, rest, rest, rest, rest, rest, ya, yp, attn, mp, pool_scale)


def _pool_bwd(dmp, pt, pool_w, drest):
    S = dmp.shape[0]
    LS = S // R16
    PG = 128
    d3 = dmp.reshape(R16, LS, AW)
    dr3 = drest.reshape(R16, LS, REST_W)

    def body(d_ref, pt_ref, pw_ref, dr_in, du_ref, dpw_ref, g_ref, s_ref):
        del dr_in
        g = pl.program_id(0)
        win = jnp.left_shift(2, g)
        row = lax.broadcasted_iota(jnp.int32, (LS, PG), 0)
        pw = pw_ref[...].astype(CDT)
        dpw = jnp.zeros((PG, PG), F32)
        for r in range(R16):
            dm = d_ref[r]
            dpw = dpw + _dot(pt_ref[:, r * LS:(r + 1) * LS], dm, NN)
            dpooled = _dot(dm, pw, NT)
            g_ref[r] = dpooled
            s_ref[r] = dpooled / _pool_counts(r, LS, win)
        dpw_ref[...] = dpw
        for r in range(R16):
            acc = s_ref[r]
            for j in range(1, R16):
                coef = jnp.where(j < win, 1.0, 0.0).astype(F32)
                if r + j < R16:
                    src = s_ref[r + j]
                else:
                    src = jnp.where(row == LS - 1, 0.0, pltpu.roll(s_ref[r + j - R16], LS - 1, 0))
                acc = acc + coef * src
            du_ref[r] = (acc - g_ref[r]).astype(du_ref.dtype)

    du, dpw = pl.pallas_call(
        body, name="pool_bwd", grid=(4,),
        in_specs=[pl.BlockSpec((R16, LS, PG), lambda g: (0, 0, g)), pl.BlockSpec((PG, S), lambda g: (g, 0)),
                  pl.BlockSpec((None, PG, PG), lambda g: (g, 0, 0)), pl.BlockSpec(memory_space=pl.ANY)],
        out_specs=[pl.BlockSpec((R16, LS, PG), lambda g: (0, 0, 24 + g)),
                   pl.BlockSpec((None, PG, PG), lambda g: (g, 0, 0))],
        out_shape=[_sds((R16, LS, REST_W), CDT), _sds((4, PG, PG), F32)],
        scratch_shapes=[pltpu.VMEM((R16, LS, PG), F32), pltpu.VMEM((R16, LS, PG), F32)],
        input_output_aliases={3: 0},
        compiler_params=_cp(("parallel",)),
    )(d3, pt, pool_w, dr3)
    return du.reshape(S, REST_W), dpw


def _attn_bwd(qkv, dattn, Lb, dlb, bias, geom, dqkv_prev):
    S = qkv.shape[0]
    gi, T, Tk, has_prev, N = geom.gi, geom.T, geom.Tk, geom.has_prev, geom.n_tiles
    nd_outer = len(geom.outer)
    qv = geom.view(qkv)
    cq, ck, cv = 3 * gi, 3 * gi + 1, 3 * gi + 2
    W3 = 3 * AW

    def heads(q, k, v, do, L, dl, b_ref, pm, dsum_ref, emit):
        for h in range(NH):
            sl = slice(HD * h, HD * h + HD)
            qs = (q[:, sl].astype(F32) * 0.125).astype(q.dtype)
            kh, vh, doh = k[:, sl], v[:, sl], do[:, sl]
            s = _dot(qs, kh, NT) + b_ref[h]
            if pm is not None:
                s = s + pm
            p = jnp.exp(s - L[:, HD * h:HD * h + 1])
            dp = _dot(doh, vh, NT)
            ds = p * (dp - dl[:, HD * h:HD * h + 1])
            dsum_ref[h] += ds
            dsc = ds.astype(q.dtype)
            dq = _dot(dsc, kh, NN) * 0.125
            dk = _dot(dsc, qs, TN)
            dv = _dot(p.astype(q.dtype), doh, TN)
            emit(sl, dq, dk, dv)

    def first_step():
        ok = pl.program_id(nd_outer) == 0 if has_prev else pl.program_id(0) == 0
        for a in range(nd_outer if has_prev else 0):
            ok = ok & (pl.program_id(a) == 0)
        return ok

    if has_prev:
        def body(q_ref, kp_ref, kc_ref, vp_ref, vc_ref, do_ref, L_ref, dl_ref, b_ref, prev_in, out_ref, dsum_ref,
                 hold, new, pp):
            del prev_in
            s_id = pl.program_id(nd_outer)

            @pl.when(first_step())
            def _():
                dsum_ref[...] = jnp.zeros_like(dsum_ref)

            @pl.when(s_id < N)
            def _():
                q = _ld(q_ref)
                k = jnp.concatenate([_ld(kp_ref), _ld(kc_ref)], axis=0)
                v = jnp.concatenate([_ld(vp_ref), _ld(vc_ref)], axis=0)
                pm = _prev_mask(s_id, T, Tk)

                def emit(sl, dq, dk, dv):
                    new[:, sl] = dq
                    new[:, slice(AW + sl.start, AW + sl.stop)] = dk[T:]
                    new[:, slice(2 * AW + sl.start, 2 * AW + sl.stop)] = dv[T:]
                    pp[:, sl] = dk[:T]
                    pp[:, slice(AW + sl.start, AW + sl.stop)] = dv[:T]

                heads(q, k, v, _ld(do_ref), _ld(L_ref), _ld(dl_ref), b_ref, pm, dsum_ref, emit)

            @pl.when((s_id >= 1) & (s_id < N))
            def _():
                _st(out_ref, hold[:, 0:AW], slice(0, AW))
                _st(out_ref, hold[:, AW:W3] + pp[...], slice(AW, W3))

            @pl.when(s_id == N)
            def _():
                _st(out_ref, hold[...])

            @pl.when(s_id < N)
            def _():
                hold[...] = new[...]

        cur = lambda n: jnp.minimum(n, N - 1)
        prev = lambda n: jnp.clip(n - 1, 0, N - 1)
        delayed = lambda n: jnp.maximum(n - 1, 0)
        in_specs = [geom.spec(AW, cq, cur), geom.spec(AW, ck, prev), geom.spec(AW, ck, cur),
                    geom.spec(AW, cv, prev), geom.spec(AW, cv, cur),
                    geom.spec(AW, 0, cur), geom.spec(AW, 0, cur), geom.spec(AW, 0, cur)]
        args = [qv] * 5 + [geom.view(dattn), geom.view(Lb), geom.view(dlb)]
        out_spec = geom.spec(W3, gi, delayed)
        grid = geom.outer + (N + 1,)
        scratch = [pltpu.VMEM((T, W3), F32), pltpu.VMEM((T, W3), F32), pltpu.VMEM((T, 2 * AW), F32)]
    else:
        def body(q_ref, kc_ref, vc_ref, do_ref, L_ref, dl_ref, b_ref, prev_in, out_ref, dsum_ref):
            del prev_in

            @pl.when(first_step())
            def _():
                dsum_ref[...] = jnp.zeros_like(dsum_ref)

            def emit(sl, dq, dk, dv):
                _st(out_ref, dq, sl)
                _st(out_ref, dk, slice(AW + sl.start, AW + sl.stop))
                _st(out_ref, dv, slice(2 * AW + sl.start, 2 * AW + sl.stop))

            heads(_ld(q_ref), _ld(kc_ref), _ld(vc_ref), _ld(do_ref), _ld(L_ref), _ld(dl_ref), b_ref, None,
                  dsum_ref, emit)

        cur = lambda n: n
        in_specs = [geom.spec(AW, cq, cur), geom.spec(AW, ck, cur), geom.spec(AW, cv, cur),
                    geom.spec(AW, 0, cur), geom.spec(AW, 0, cur), geom.spec(AW, 0, cur)]
        args = [qv] * 3 + [geom.view(dattn), geom.view(Lb), geom.view(dlb)]
        out_spec = geom.spec(W3, gi, cur)
        grid = geom.outer
        scratch = []

    in_specs.append(pl.BlockSpec((NH, T, Tk), lambda *ids: (0, 0, 0)))
    in_specs.append(pl.BlockSpec(memory_space=pl.ANY))
    if dqkv_prev is None:
        last, aliases = bias, {}
    else:
        last, aliases = geom.view(dqkv_prev), {len(in_specs) - 1: 0}
    dq, dsum = pl.pallas_call(
        body, name=f"attn_bwd{gi}", grid=grid, in_specs=in_specs,
        out_specs=[out_spec, pl.BlockSpec((NH, T, Tk), lambda *ids: (0, 0, 0))],
        out_shape=[_sds(geom.view_shape(QKV_W), CDT), _sds((NH, T, Tk), F32)],
        scratch_shapes=scratch,
        input_output_aliases=aliases,
        compiler_params=_cp(("arbitrary",) * len(grid), vmem=56),
    )(*args, bias, last)
    return geom.unview(dq), dsum


def _h_bwd(dh, dx2, xs, norm_g, scale):
    LS = xs.shape[1]
    tl = 16

    def body(dh_ref, dx2_ref, x_ref, g_ref, sc_ref, gx_ref, dsh_ref, dsc_ref, dng_ref):
        i = pl.program_id(0)

        @pl.when(i == 0)
        def _():
            dsh_ref[...] = jnp.zeros_like(dsh_ref)
            dsc_ref[...] = jnp.zeros_like(dsc_ref)
            dng_ref[...] = jnp.zeros_like(dng_ref)

        g = g_ref[...]
        one_sc = 1.0 + sc_ref[...]
        x = _ld(x_ref)
        r1 = lax.rsqrt(jnp.mean(x * x, axis=-1, keepdims=True) + EPS)
        xn = x * r1
        dh = _ld(dh_ref)
        dhx = dh * xn
        dxn = dh * (g * one_sc)
        dx1 = r1 * (dxn - xn * jnp.mean(dxn * xn, axis=-1, keepdims=True))
        gx = _ld(dx2_ref) + dx1
        for r in range(R16):
            gx_ref[:, r, :] = gx[r * tl:(r + 1) * tl]
        dsh_ref[...] += jnp.sum(dh, axis=0, keepdims=True)
        dsc_ref[...] += jnp.sum(dhx * g, axis=0, keepdims=True)
        dng_ref[...] += jnp.sum(dhx * one_sc, axis=0, keepdims=True)

    slab = pl.BlockSpec((R16, tl, D), lambda i: (0, i, 0))
    vec = pl.BlockSpec((1, D), lambda i: (0, 0))
    return pl.pallas_call(
        body, name="h_bwd", grid=(LS // tl,),
        in_specs=[slab, slab, slab, vec, vec],
        out_specs=[pl.BlockSpec((tl, R16, D), lambda i: (i, 0, 0)), vec, vec, vec],
        out_shape=[_sds((LS, R16, D), F32), _sds((1, D), F32), _sds((1, D), F32), _sds((1, D), F32)],
        compiler_params=_cp(("arbitrary",)),
    )(dh, dx2, xs, norm_g, scale)


def _local_step(x, target, shift, scale, gate, norm_g, w_in_full, pool_w, pool_scale, wab, wpb, wout, rel_bias,
                final_g):
    S = x.shape[0]
    LS = S // R16
    geoms = [_Geom(gi, LS) for gi in range(3)]
    idxs = [jnp.asarray(g.bucket_index()) for g in geoms]
    biases = [_bias_table(idxs[gi], rel_bias, gi) for gi in range(3)]

    h3, xs3 = _norm_mod(x.reshape(LS, R16, D), norm_g, shift, scale)
    h = h3.reshape(S, D)
    ht = _tcast(h, "h_transpose")
    qkv, rest = _proj(h, w_in_full)

    os_, ls_ = [], []
    for gi in range(3):
        o, l = _attn_fwd(qkv, biases[gi], geoms[gi])
        os_.append(o)
        ls_.append(l)
    attn, Lb, ag, agt = _attn_combine(os_, ls_, rest)
    pt, mp, pg, pgt = _pool_fwd(rest, pool_w, pool_scale)
    ya, yp, merged, mt = _branch_merge(ag, pg, wab, wpb, rest)
    dx2_3, dmo3, loss, gfg, dgate = _out_loss(merged.reshape(R16, LS, D), wout, xs3, target.reshape(LS, R16, D),
                                              gate, final_g)
    dmo = dmo3.reshape(S, D)

    dya, dyp, drest, dattn, dlb, dmp, dps = _bwd_gates(dmo, wout, wab, wpb, rest, ya, yp, attn, mp, pool_scale)
    dw_out = _mm_tn(mt, dmo, n_total=D, out_blk=lambda j: j, name="dw_out")
    dw_ab = _mm_tn(agt, dya, n_total=D, out_blk=lambda j: j, name="dw_attn_br")
    dw_pb = _mm_tn(pgt, dyp, n_total=D, out_blk=lambda j: j, name="dw_pool_br")
    drest, dpw = _pool_bwd(dmp, pt, pool_w, drest)

    dqkv = None
    drb = []
    for gi in range(3):
        dqkv, dsum = _attn_bwd(qkv, dattn, Lb, dlb, biases[gi], geoms[gi], dqkv)
        drb.append(_bias_grad(dsum, idxs[gi], gi))
    d_rel_bias = jnp.concatenate(drb, axis=1)

    dw_in = _mm_tn(ht, dqkv, n_total=8192, out_blk=lambda j: j, name="dw_in_qkv")
    dw_in = _mm_tn(ht, drest, n_total=8192, out_blk=_rest_blk, name="dw_in_rest", prev=dw_in)
    dh = _dh(dqkv, drest, w_in_full)

    gx3, dsh, dsc, dng = _h_bwd(dh.reshape(R16, LS, D), dx2_3, xs3, norm_g, scale)
    return dict(loss=loss[0, 0], grad_x=gx3.reshape(S, D), dw_in=dw_in, dw_ab=dw_ab, dw_pb=dw_pb, dw_out=dw_out,
                d_norm_g=dng, d_pool_w=dpw, d_pool_scale=dps, d_rel_bias=d_rel_bias, d_final_g=gfg,
                dmod=jnp.concatenate([dsh, dsc, dgate], axis=1))


def _my_place():
    return lax.axis_index("x"), lax.axis_index("y"), lax.axis_index("c")


def _block_of(ref, axis, idx, width):
    if axis is None:
        return ref.at[idx]
    start = pl.multiple_of(idx * width, width)
    if axis == 0:
        return ref.at[pl.ds(start, width)]
    return ref.at[:, pl.ds(start, width)]


def _all_gather(srcs, out_shapes, layout, name):
    n = len(srcs)

    def body(*refs):
        src, out = refs[:n], refs[n:2 * n]
        send_sems, recv_sems, local_sems = refs[2 * n:]
        x, y, c = _my_place()
        me, sibling = (x, y, c), (x, y, 1 - c)
        chips = [(1 - x, y), (x, 1 - y), (1 - x, 1 - y)]

        def region(p, dev):
            px, py, pc = dev
            return _block_of(out[p], layout[p][0], 4 * px + 2 * py + pc, layout[p][1])

        def copy(p, k, block, to, src_ref=None):
            return pltpu.make_async_remote_copy(
                src_ref=region(p, block) if src_ref is None else src_ref, dst_ref=region(p, block),
                send_sem=send_sems.at[p * 7 + k], recv_sem=recv_sems.at[p * 7 + k],
                device_id=to, device_id_type=MESH)

        mine = [pltpu.make_async_copy(src[p], region(p, me), local_sems.at[p]) for p in range(n)]
        for cp in mine:
            cp.start()
        first = []
        for p in range(n):
            first.append(copy(p, 0, me, sibling, src[p]))
            first += [copy(p, 1 + j, me, (*chip, c), src[p]) for j, chip in enumerate(chips)]
        for cp in first:
            cp.start()
        passed = []
        for j, chip in enumerate(chips):
            for p in range(n):
                copy(p, 1 + j, (*chip, c), me).wait_recv()
                fwd = copy(p, 4 + j, (*chip, c), sibling)
                fwd.start()
                passed.append(fwd)
        for p in range(n):
            copy(p, 0, sibling, me).wait_recv()
            for j, chip in enumerate(chips):
                copy(p, 4 + j, (*chip, 1 - c), me).wait_recv()
        for cp in first + passed:
            cp.wait_send()
        for cp in mine:
            cp.wait()

    return pl.pallas_call(
        body, name=name, in_specs=[_HBM] * n, out_specs=[_HBM] * n,
        out_shape=[_sds(s, a.dtype) for s, a in zip(out_shapes, srcs)],
        scratch_shapes=[pltpu.SemaphoreType.DMA((7 * n,)), pltpu.SemaphoreType.DMA((7 * n,)),
                        pltpu.SemaphoreType.DMA((n,))],
    )(*srcs)


def _reduce_pair_exchange(dws, layout, smalls):
    n = len(dws)

    def body(*refs):
        src, sm = refs[:n], refs[n]
        out, sm_out = refs[n + 1:2 * n + 1], refs[2 * n + 1]
        send_sems, recv_sems = refs[2 * n + 2:]
        x, y, c = _my_place()
        sibling = (x, y, 1 - c)
        cps = []
        for p in range(n):
            for q in range(4):
                cps.append(pltpu.make_async_remote_copy(
                    src_ref=_block_of(src[p], layout[p][0], 2 * q + (1 - c), layout[p][1]), dst_ref=out[p].at[q],
                    send_sem=send_sems.at[4 * p + q], recv_sem=recv_sems.at[4 * p + q],
                    device_id=sibling, device_id_type=MESH))
        cps.append(pltpu.make_async_remote_copy(src_ref=sm, dst_ref=sm_out, send_sem=send_sems.at[4 * n],
                                                recv_sem=recv_sems.at[4 * n], device_id=sibling, device_id_type=MESH))
        for cp in cps:
            cp.start()
        for cp in cps:
            cp.wait()

    def shard_shape(a, lay):
        axis, w = lay
        return (4, w, a.shape[1]) if axis == 0 else (4, a.shape[0], w)

    return pl.pallas_call(
        body, name="reduce_pair_exchange", in_specs=[_HBM] * (n + 1), out_specs=[_HBM] * (n + 1),
        out_shape=[_sds(shard_shape(a, l), F32) for a, l in zip(dws, layout)] + [_sds(smalls.shape, F32)],
        scratch_shapes=[pltpu.SemaphoreType.DMA((4 * n + 1,)), pltpu.SemaphoreType.DMA((4 * n + 1,))],
    )(*dws, smalls)


def _pair_add(dw, recv, axis, width, cidx, name):
    if axis == 0:
        tr = min(width, 256)
        C = dw.shape[1]
        grid = (4, width // tr)
        nb = width // tr
        own = pl.BlockSpec((tr, C), lambda q, i, c_ref: ((2 * q + c_ref[0]) * nb + i, 0))
        oth = pl.BlockSpec((None, tr, C), lambda q, i, c_ref: (q, i, 0))
    else:
        R = dw.shape[0]
        tr = min(R, 256)
        grid = (4, R // tr)
        own = pl.BlockSpec((tr, width), lambda q, i, c_ref: (i, 2 * q + c_ref[0]))
        oth = pl.BlockSpec((None, tr, width), lambda q, i, c_ref: (q, i, 0))

    def body(c_ref, a_ref, b_ref, o_ref):
        o_ref[...] = (a_ref[...] + b_ref[...]).astype(o_ref.dtype)

    return pl.pallas_call(
        body, name=name,
        grid_spec=pltpu.PrefetchScalarGridSpec(num_scalar_prefetch=1, grid=grid, in_specs=[own, oth], out_specs=oth),
        out_shape=_sds(recv.shape, CDT), compiler_params=_cp(("parallel", "parallel")),
    )(cidx, dw, recv)


def _add2(a, b, name):
    def body(a_ref, b_ref, o_ref):
        o_ref[...] = a_ref[...] + b_ref[...]

    return pl.pallas_call(body, name=name, out_shape=_sds(a.shape, F32))(a, b)


def _reduce_chip_exchange(parts):
    n = len(parts)

    def body(*refs):
        src, out = refs[:n], refs[n:2 * n]
        send_sems, recv_sems, local_sems = refs[2 * n:]
        x, y, c = _my_place()
        my_slot = 2 * x + y
        chips = [(1 - x, y), (x, 1 - y), (1 - x, 1 - y)]
        cps = []
        for p in range(n):
            keep = pltpu.make_async_copy(src[p].at[my_slot], out[p].at[my_slot], local_sems.at[p])
            keep.start()
            cps.append(keep)
            for k, (cx, cy) in enumerate(chips):
                cps.append(pltpu.make_async_remote_copy(
                    src_ref=src[p].at[2 * cx + cy], dst_ref=out[p].at[my_slot],
                    send_sem=send_sems.at[3 * p + k], recv_sem=recv_sems.at[3 * p + k],
                    device_id=(cx, cy, c), device_id_type=MESH))
                cps[-1].start()
        for cp in cps:
            cp.wait()

    return pl.pallas_call(
        body, name="reduce_chip_exchange", in_specs=[_HBM] * n, out_specs=[_HBM] * n,
        out_shape=[_sds(a.shape, a.dtype) for a in parts],
        scratch_shapes=[pltpu.SemaphoreType.DMA((3 * n,)), pltpu.SemaphoreType.DMA((3 * n,)),
                        pltpu.SemaphoreType.DMA((n,))],
    )(*parts)


def _adaln_part(c_all, w_ada, b_part):
    def body(c_ref, w_ref, b_ref, o_ref):
        o_ref[...] = _dot(c_ref[...].astype(CDT), w_ref[...].astype(CDT), NN) + b_ref[...]

    return pl.pallas_call(body, name="adaln_part", out_shape=_sds((8, w_ada.shape[1]), F32),
                          compiler_params=_cp())(c_all, w_ada, b_part)


def _adam_math(w, g, m, v):
    m = ADAM_B1 * m + (1.0 - ADAM_B1) * g
    v = ADAM_B2 * v + (1.0 - ADAM_B2) * (g * g)
    m_hat = m / (1.0 - ADAM_B1 ** ADAM_STEP)
    v_hat = v / (1.0 - ADAM_B2 ** ADAM_STEP)
    delta = -ADAM_LR * (m_hat / (jnp.sqrt(v_hat) + ADAM_EPS) + ADAM_WD * w)
    return delta, m, v


def _adam(gparts, w, m, v, name):
    P, R, C = gparts.shape
    tr = R if R <= 256 else 256

    def body(g_ref, w_ref, m_ref, v_ref, go_ref, d_ref, mo_ref, vo_ref):
        g = g_ref[0].astype(F32)
        for i in range(1, P):
            g = g + g_ref[i].astype(F32)
        d, mn, vn = _adam_math(w_ref[...], g, m_ref[...], v_ref[...])
        go_ref[...] = g
        d_ref[...] = d
        mo_ref[...] = mn
        vo_ref[...] = vn

    t = pl.BlockSpec((tr, C), lambda i: (i, 0))
    return pl.pallas_call(
        body, name=name, grid=(R // tr,),
        in_specs=[pl.BlockSpec((P, tr, C), lambda i: (0, i, 0)), t, t, t], out_specs=[t, t, t, t],
        out_shape=[_sds((R, C), F32)] * 4, compiler_params=_cp(("parallel",)),
    )(gparts, w, m, v)


def _sum_parts(gparts, name):
    P, R, C = gparts.shape

    def body(g_ref, o_ref):
        g = g_ref[0]
        for i in range(1, P):
            g = g + g_ref[i]
        o_ref[...] = g

    return pl.pallas_call(body, name=name, out_shape=_sds((R, C), F32))(gparts)


def _adam_w_ada(c_t, dm, w, m, v):
    R, C = w.shape

    def body(c_ref, dm_ref, w_ref, m_ref, v_ref, go_ref, d_ref, mo_ref, vo_ref):
        g = c_ref[:, 0:1] * dm_ref[0:1, :]
        for b in range(1, 8):
            g = g + c_ref[:, b:b + 1] * dm_ref[b:b + 1, :]
        d, mn, vn = _adam_math(w_ref[...], g, m_ref[...], v_ref[...])
        go_ref[...] = g
        d_ref[...] = d
        mo_ref[...] = mn
        vo_ref[...] = vn

    return pl.pallas_call(body, name="adam_w_ada", out_shape=[_sds((R, C), F32)] * 4,
                          compiler_params=_cp())(c_t, dm, w, m, v)


_SM = dict(norm_g=(0, 8), pool_w=(8, 512), pool_scale=(520, 8), rel_bias=(528, 8), final_g=(536, 8),
           dmod=(544, 192), loss=(736, 8))
_SM_ROWS = 744


def _pack_rows(a, rows):
    flat = a.reshape(-1).astype(F32)
    return jnp.pad(flat, (0, rows * 128 - flat.shape[0])).reshape(rows, 128)


def kernel(x, c, norm_g, w_ada, b_ada, w_in, pool_w, pool_scale, w_attn_br, w_pool_br, w_out, rel_bias, final_g, loss_target, m_norm_g, m_w_ada, m_b_ada, m_w_in, m_pool_w, m_pool_scale, m_w_attn_br, m_w_pool_br, m_w_out, m_rel_bias, m_final_g, v_norm_g, v_w_ada, v_b_ada, v_w_in, v_pool_w, v_pool_scale, v_w_attn_br, v_w_pool_br, v_w_out, v_rel_bias, v_final_g):
    S = x.shape[1]
    px, py, pc = _my_place()
    me = 4 * px + 2 * py + pc
    cidx = jnp.reshape(pc, (1,)).astype(jnp.int32)

    c8 = jnp.broadcast_to(c, (8, D))
    w_in_full, wab, wpb, wout, c_rows = _all_gather(
        [w_in[0].astype(CDT), w_attn_br[0].astype(CDT), w_pool_br[0].astype(CDT), w_out[0].astype(CDT), c8],
        [(D, 8192), (AW, D), (AW, D), (D, D), (64, D)],
        [(1, 1024), (1, 128), (1, 128), (0, 128), (0, 8)], "gather_weights")
    c_all = c_rows[::8]

    b_part = lax.dynamic_slice(b_ada, (0, me * 384), (1, 384))
    mod_part = _adaln_part(c_all, w_ada[0], b_part)
    (mod_all,) = _all_gather([mod_part], [(8, 8, 384)], [(None, 1)], "gather_mod")
    mod = lax.dynamic_index_in_dim(mod_all, me, axis=1, keepdims=False).reshape(1, 3 * D)
    shift, scale, gate = mod[:, :D], mod[:, D:2 * D], mod[:, 2 * D:]

    loc = _local_step(x[0], loss_target[0], shift, scale, gate, norm_g, w_in_full, pool_w[0], pool_scale, wab, wpb,
                      wout, rel_bias, final_g.reshape(1, D))

    dmod_blk = lax.dynamic_update_slice(jnp.zeros((8, 3 * D), F32), loc["dmod"], (me, 0))
    smalls = jnp.concatenate([
        _pack_rows(loc["d_norm_g"], 8), _pack_rows(loc["d_pool_w"], 512), _pack_rows(loc["d_pool_scale"], 8),
        _pack_rows(loc["d_rel_bias"], 8), _pack_rows(loc["d_final_g"], 8), _pack_rows(dmod_blk, 192),
        _pack_rows(loc["loss"], 8)], axis=0)

    dws = [loc["dw_in"], loc["dw_ab"], loc["dw_pb"], loc["dw_out"]]
    layout = [(1, 1024), (1, 128), (1, 128), (0, 128)]
    *recv, sm_recv = _reduce_pair_exchange(dws, layout, smalls)
    parts = [_pair_add(dw, r, ax, w, cidx, f"pair_add{i}") for i, (dw, r, (ax, w)) in enumerate(zip(dws, recv, layout))]
    parts.append(_add2(smalls, sm_recv, "pair_add_smalls")[None])
    g_in, g_ab, g_pb, g_out, sm4 = _reduce_chip_exchange(
        parts[:4] + [jnp.broadcast_to(parts[4], (4, _SM_ROWS, 128))])

    def sm(name):
        lo, n = _SM[name]
        return sm4[:, lo:lo + n]

    def small_adam(name, w, m, v, nvalid):
        rows = _SM[name][1]
        outs = _adam(sm(name), _pack_rows(w, rows), _pack_rows(m, rows), _pack_rows(v, rows), f"adam_{name}")
        return [o.reshape(-1)[:nvalid].reshape(w.shape) for o in outs]

    o_norm_g = small_adam("norm_g", norm_g, m_norm_g, v_norm_g, D)
    o_pool_w = small_adam("pool_w", pool_w, m_pool_w, v_pool_w, 65536)
    o_pool_scale = small_adam("pool_scale", pool_scale, m_pool_scale, v_pool_scale, AW)
    o_rel_bias = small_adam("rel_bias", rel_bias, m_rel_bias, v_rel_bias, 768)
    o_final_g = small_adam("final_g", final_g, m_final_g, v_final_g, D)

    dmod_all = _sum_parts(sm("dmod"), "sum_dmod").reshape(8, 3 * D)
    o_b_ada = [o.reshape(1, 3 * D) for o in _adam(dmod_all.reshape(8, 24, 128), b_ada.reshape(24, 128),
                                                   m_b_ada.reshape(24, 128), v_b_ada.reshape(24, 128), "adam_b_ada")]
    dm_mine = lax.dynamic_slice(dmod_all, (0, me * 384), (8, 384))
    o_w_ada = [o[None] for o in _adam_w_ada(c_all.T, dm_mine, w_ada[0], m_w_ada[0], v_w_ada[0])]

    o_w_in = [o[None] for o in _adam(g_in, w_in[0], m_w_in[0], v_w_in[0], "adam_w_in")]
    o_w_ab = [o[None] for o in _adam(g_ab, w_attn_br[0], m_w_attn_br[0], v_w_attn_br[0], "adam_w_attn_br")]
    o_w_pb = [o[None] for o in _adam(g_pb, w_pool_br[0], m_w_pool_br[0], v_w_pool_br[0], "adam_w_pool_br")]
    o_w_out = [o[None] for o in _adam(g_out, w_out[0], m_w_out[0], v_w_out[0], "adam_w_out")]

    loss = _sum_parts(sm("loss"), "sum_loss")[0, 0]

    per_w = [o_norm_g, o_w_ada, o_b_ada, o_w_in, o_pool_w, o_pool_scale, o_w_ab, o_w_pb, o_w_out, o_rel_bias, o_final_g]
    outs = [loss, loc["grad_x"][None]]
    for k in range(4):
        outs += [o[k] for o in per_w]
    return tuple(outs)
```

```python
import functools
import math

import numpy as np
import jax
import jax.numpy as jnp
from jax import lax
from jax.experimental import pallas as pl
from jax.experimental.pallas import tpu as pltpu

F32 = jnp.float32
CDT = jnp.bfloat16
D = 1024
HD = 64
NH = 8
AW = 512
QKV_W = 4608
REST_W = 3584
R16 = 16
EPS = 1e-6
NEG = -1e30
N_BACK = 128
ADAM_LR, ADAM_B1, ADAM_B2, ADAM_EPS, ADAM_WD, ADAM_STEP = 0.001, 0.9, 0.999, 1e-08, 0.01, 10
MESH = pl.DeviceIdType.MESH
VMEM_MB = 1024 * 1024


def _cp(sem=None, vmem=48, **kw):
    if sem is not None:
        kw["dimension_semantics"] = sem
    return pltpu.CompilerParams(vmem_limit_bytes=vmem * VMEM_MB, **kw)


def _sds(shape, dtype):
    return jax.ShapeDtypeStruct(tuple(shape), dtype)


def _rest_blk(jj):
    return jnp.where(jj == 0, 9, jnp.where(jj == 6, 10, jj + 10))


def _sigmoid(z):
    return 1.0 / (1.0 + jnp.exp(-z))


def _dot(a, b, dims):
    return lax.dot_general(a, b, (dims, ((), ())), preferred_element_type=F32)


NN = ((1,), (0,))
NT = ((1,), (1,))
TN = ((0,), (0,))


def _ld(ref):
    if len(ref.shape) == 2:
        return ref[...]
    return jnp.concatenate([ref[a] for a in range(ref.shape[0])], axis=0)


def _st(ref, val, cols=None):
    val = val.astype(ref.dtype)
    if len(ref.shape) == 2:
        if cols is None:
            ref[...] = val
        else:
            ref[:, cols] = val
        return
    rows = ref.shape[1]
    for a in range(ref.shape[0]):
        if cols is None:
            ref[a] = val[a * rows:(a + 1) * rows]
        else:
            ref[a, :, cols] = val[a * rows:(a + 1) * rows]


REST_COLS = ((9, 10), (11, 16), (10, 11))
_HBM = pl.BlockSpec(memory_space=pltpu.HBM)


def _resident(shape):
    return pl.BlockSpec(shape, lambda *ids: (0,) * len(shape), pipeline_mode=pl.Buffered(1))


def _proj(h, w):
    S = h.shape[0]
    tm = 256

    def body(a_ref, w_ref, q_ref, r_ref):
        a = a_ref[...]
        for j in range(9):
            q_ref[:, 512 * j:512 * (j + 1)] = _dot(a, w_ref[:, 512 * j:512 * (j + 1)], NN).astype(q_ref.dtype)
        off = 0
        for lo, hi in REST_COLS:
            for j in range(lo, hi):
                r_ref[:, off:off + 512] = _dot(a, w_ref[:, 512 * j:512 * (j + 1)], NN)
                off += 512

    return pl.pallas_call(
        body, name="proj", grid=(S // tm,),
        in_specs=[pl.BlockSpec((tm, D), lambda i: (i, 0)), _resident((D, 8192))],
        out_specs=[pl.BlockSpec((tm, QKV_W), lambda i: (i, 0)), pl.BlockSpec((tm, REST_W), lambda i: (i, 0))],
        out_shape=[_sds((S, QKV_W), CDT), _sds((S, REST_W), F32)],
        compiler_params=_cp(("parallel",), vmem=56),
    )(h, w)


def _after(dep):
    return ([], []) if dep is None else ([pl.BlockSpec(memory_space=pl.ANY)], [dep])


def _dh(dqkv, drest, w, dep=None):
    S = dqkv.shape[0]
    tm = 256
    dep_specs, dep_args = _after(dep)

    def body(a_ref, b_ref, w_ref, *rest):
        o_ref = rest[-1]
        acc = _dot(a_ref[...], w_ref[:, 0:QKV_W], NT)
        off = 0
        for lo, hi in REST_COLS:
            n = 512 * (hi - lo)
            acc = acc + _dot(b_ref[:, off:off + n], w_ref[:, 512 * lo:512 * hi], NT)
            off += n
        o_ref[...] = acc

    return pl.pallas_call(
        body, name="dh", grid=(S // tm,),
        in_specs=[pl.BlockSpec((tm, QKV_W), lambda i: (i, 0)), pl.BlockSpec((tm, REST_W), lambda i: (i, 0)),
                  _resident((D, 8192))] + dep_specs,
        out_specs=pl.BlockSpec((tm, D), lambda i: (i, 0)),
        out_shape=_sds((S, D), F32),
        compiler_params=_cp(("parallel",), vmem=56),
    )(dqkv, drest, w, *dep_args)


def _mm_tn(at, b, *, n_total, out_blk, name, prev=None, tn=512):
    R, S = at.shape
    nb = b.shape[1] // tn

    def body(*refs):
        refs[-1][...] = _dot(refs[0][...], refs[1][...], NN)

    in_specs = [_resident((R, S)), pl.BlockSpec((S, tn), lambda j: (0, j))]
    args = [at, b]
    aliases = {}
    if prev is not None:
        in_specs.append(pl.BlockSpec(memory_space=pl.ANY))
        args.append(prev)
        aliases = {2: 0}
    return pl.pallas_call(
        body, name=name, grid=(nb,), in_specs=in_specs,
        out_specs=pl.BlockSpec((R, tn), lambda j: (0, out_blk(j))),
        out_shape=_sds((R, n_total), F32),
        input_output_aliases=aliases,
        compiler_params=_cp(("parallel",)),
    )(*args)


def _tcast(a, name):
    M, C = a.shape
    tm = 512

    def body(a_ref, o_ref):
        o_ref[...] = a_ref[...].astype(F32).T.astype(o_ref.dtype)

    return pl.pallas_call(
        body, name=name, grid=(M // tm,),
        in_specs=[pl.BlockSpec((tm, C), lambda i: (i, 0))],
        out_specs=pl.BlockSpec((C, tm), lambda i: (0, i)),
        out_shape=_sds((C, M), CDT), compiler_params=_cp(("parallel",)),
    )(a)


def _norm_mod(x3, norm_g, shift, scale):
    LS = x3.shape[0]
    tl = 16

    def body(x_ref, g_ref, sh_ref, sc_ref, h_ref, xs_ref):
        gg = g_ref[...] * (1.0 + sc_ref[...])
        for r in range(R16):
            xr = x_ref[:, r, :]
            xs_ref[r] = xr
            xn = xr * lax.rsqrt(jnp.mean(xr * xr, axis=-1, keepdims=True) + EPS)
            h_ref[r] = (xn * gg + sh_ref[...]).astype(h_ref.dtype)

    vec = pl.BlockSpec((1, D), lambda i: (0, 0))
    slab = pl.BlockSpec((R16, tl, D), lambda i: (0, i, 0))
    return pl.pallas_call(
        body, name="norm_mod", grid=(LS // tl,),
        in_specs=[pl.BlockSpec((tl, R16, D), lambda i: (i, 0, 0)), vec, vec, vec],
        out_specs=[slab, slab],
        out_shape=[_sds((R16, LS, D), CDT), _sds((R16, LS, D), F32)], compiler_params=_cp(("parallel",)),
    )(x3, norm_g, shift, scale)


def _t5_bucket_np(n):
    nf = np.maximum(n, 1).astype(np.float32)
    large = 16 + (np.log(nf / np.float32(16)) / np.float32(math.log(2048 / 16)) * np.float32(16)).astype(np.int32)
    large = np.minimum(large, 31)
    return np.where(n < 16, n, large).astype(np.int32)


class _Geom:
    def __init__(self, gi, LS):
        self.gi, self.LS = gi, LS
        self.dil = (1, 4, 16)[gi]
        if gi == 0:
            self.lead, self.rows, self.n_tiles, self.has_prev = (R16,), 16, LS // 16, True
            a = np.arange(R16)[:, None]
            ll = np.arange(16)[None, :]
            self.pos = (16 * ll + a).reshape(-1)
        elif gi == 1:
            self.lead, self.rows, self.n_tiles, self.has_prev = (4,), 64, LS // 64, True
            a = np.arange(4)[:, None]
            ll = np.arange(64)[None, :]
            self.pos = (4 * ll + a).reshape(-1)
        else:
            self.lead, self.rows, self.n_tiles, self.has_prev = (), LS, 1, False
            self.pos = np.arange(LS)
        self.T = self.pos.shape[0]
        self.Tk = 2 * self.T if self.has_prev else self.T
        self.outer = {0: (), 1: (4,), 2: (R16,)}[gi]

    def view(self, arr):
        S, C = arr.shape
        if self.gi == 1:
            return arr.reshape(4, 4, self.LS, C)
        return arr.reshape(R16, self.LS, C)

    def view_shape(self, C):
        return (4, 4, self.LS, C) if self.gi == 1 else (R16, self.LS, C)

    def unview(self, arr):
        return arr.reshape(R16 * self.LS, arr.shape[-1])

    def spec(self, W, cblk, tile):
        if self.gi == 0:
            return pl.BlockSpec((R16, 16, W), lambda n: (0, tile(n), cblk))
        if self.gi == 1:
            return pl.BlockSpec((4, None, 64, W), lambda r, n: (0, r, tile(n), cblk))
        return pl.BlockSpec((None, self.LS, W), lambda r: (r, 0, cblk))

    def bucket_index(self):
        pq = self.pos[:, None]
        pk = self.pos[None, :]
        if self.has_prev:
            pk = np.concatenate([pk - self.T, pk], axis=1)
        dist = pq - pk
        ok = (dist >= 0) & (dist <= N_BACK)
        bucket = _t5_bucket_np(np.clip(dist, 0, N_BACK) * self.dil)
        return np.where(ok, bucket, -1).astype(np.int32)


def _bias_table(idx, rel_bias, gi):
    T, Tk = idx.shape
    tr = 32

    def body(idx_ref, rb_ref, o_ref):
        h = pl.program_id(0)
        ix = idx_ref[...]
        acc = jnp.full(ix.shape, NEG, F32)
        for b in range(32):
            acc = jnp.where(ix == b, rb_ref[b, gi * NH + h], acc)
        o_ref[...] = acc

    return pl.pallas_call(
        body, name=f"bias_table{gi}", grid=(NH, T // tr),
        in_specs=[pl.BlockSpec((tr, Tk), lambda h, i: (i, 0)), pl.BlockSpec(memory_space=pltpu.SMEM)],
        out_specs=pl.BlockSpec((None, tr, Tk), lambda h, i: (h, i, 0)),
        out_shape=_sds((NH, T, Tk), F32), compiler_params=_cp(("parallel", "parallel")),
    )(idx, rel_bias)


def _bias_grad(dsum, idx, gi, dep=None):
    _, T, Tk = dsum.shape
    dep_specs, dep_args = _after(dep)

    def body(ds_ref, idx_ref, *rest):
        o_ref, r_ref = rest[-2], rest[-1]

        def step(b, c):
            r_ref[pl.ds(b, 1), :] = jnp.sum(jnp.where(idx_ref[...] == b, ds_ref[...], 0.0), axis=0, keepdims=True)
            return c

        lax.fori_loop(0, 32, step, 0)
        o_ref[...] = jnp.broadcast_to(jnp.sum(r_ref[...], axis=1, keepdims=True), (32, 128))

    out = pl.pallas_call(
        body, name=f"bias_grad{gi}", grid=(NH,),
        in_specs=[pl.BlockSpec((None, T, Tk), lambda h: (h, 0, 0)), pl.BlockSpec((T, Tk), lambda h: (0, 0))] + dep_specs,
        out_specs=pl.BlockSpec((None, 32, 128), lambda h: (h, 0, 0)),
        out_shape=_sds((NH, 32, 128), F32), scratch_shapes=[pltpu.VMEM((32, Tk), F32)],
        compiler_params=_cp(("parallel",)),
    )(dsum, idx, *dep_args)
    return out[:, :, 0].T


def _prev_mask(n, T, Tk):
    col = lax.broadcasted_iota(jnp.int32, (1, Tk), 1)
    return jnp.where((col < T) & (n == 0), NEG, 0.0).astype(F32)


def _attn_fwd(qkv, bias, geom):
    S = qkv.shape[0]
    gi, T, Tk, has_prev = geom.gi, geom.T, geom.Tk, geom.has_prev
    qv = geom.view(qkv)
    cq, ck, cv = 3 * gi, 3 * gi + 1, 3 * gi + 2

    def body(*refs):
        if has_prev:
            q_ref, kp_ref, kc_ref, vp_ref, vc_ref, b_ref, o_ref, l_ref = refs
        else:
            q_ref, kc_ref, vc_ref, b_ref, o_ref, l_ref = refs
        q = _ld(q_ref)
        if has_prev:
            n = pl.program_id(len(geom.outer))
            k = jnp.concatenate([_ld(kp_ref), _ld(kc_ref)], axis=0)
            v = jnp.concatenate([_ld(vp_ref), _ld(vc_ref)], axis=0)
            pm = _prev_mask(n, T, Tk)
        else:
            k, v = _ld(kc_ref), _ld(vc_ref)
        for h in range(NH):
            sl = slice(HD * h, HD * h + HD)
            qs = (q[:, sl].astype(F32) * 0.125).astype(q.dtype)
            s = _dot(qs, k[:, sl], NT) + b_ref[h]
            if has_prev:
                s = s + pm
            m = jnp.max(s, axis=1, keepdims=True)
            p = jnp.exp(s - m)
            l = jnp.sum(p, axis=1, keepdims=True)
            o = _dot(p.astype(v.dtype), v[:, sl], NN) / l
            _st(o_ref, o, sl)
            _st(l_ref, jnp.broadcast_to(m + jnp.log(l), (T, HD)), sl)

    cur = lambda n: n
    prev = lambda n: jnp.maximum(n - 1, 0)
    if has_prev:
        in_specs = [geom.spec(AW, cq, cur), geom.spec(AW, ck, prev), geom.spec(AW, ck, cur),
                    geom.spec(AW, cv, prev), geom.spec(AW, cv, cur)]
        args = [qv] * 5
    else:
        in_specs = [geom.spec(AW, cq, cur), geom.spec(AW, ck, cur), geom.spec(AW, cv, cur)]
        args = [qv] * 3
    nd = len(geom.outer) + (1 if has_prev else 0)
    in_specs.append(pl.BlockSpec((NH, T, Tk), lambda *ids: (0, 0, 0)))
    grid = geom.outer + ((geom.n_tiles,) if has_prev else ())
    oshape = geom.view_shape(AW)
    o, l = pl.pallas_call(
        body, name=f"attn_fwd{gi}", grid=grid, in_specs=in_specs,
        out_specs=[geom.spec(AW, 0, cur), geom.spec(AW, 0, cur)],
        out_shape=[_sds(oshape, F32), _sds(oshape, F32)],
        compiler_params=_cp(("arbitrary",) * nd),
    )(*args, bias)
    return geom.unview(o), geom.unview(l)


def _attn_combine(os_, ls_, rest):
    S = rest.shape[0]
    tm = 512

    def body(o0, o1, o2, l0, l1, l2, z_ref, attn_ref, L_ref, ag_ref, agt_ref):
        la, lb, lc = l0[...], l1[...], l2[...]
        m = jnp.maximum(jnp.maximum(la, lb), lc)
        ea, eb, ec = jnp.exp(la - m), jnp.exp(lb - m), jnp.exp(lc - m)
        den = ea + eb + ec
        attn = (ea * o0[...] + eb * o1[...] + ec * o2[...]) / den
        attn_ref[...] = attn
        L_ref[...] = m + jnp.log(den)
        z = z_ref[...]
        ag = attn * (z * _sigmoid(z))
        ag_ref[...] = ag.astype(ag_ref.dtype)
        agt_ref[...] = ag.T.astype(agt_ref.dtype)

    t = pl.BlockSpec((tm, AW), lambda i: (i, 0))
    return pl.pallas_call(
        body, name="attn_combine", grid=(S // tm,),
        in_specs=[t] * 6 + [pl.BlockSpec((tm, AW), lambda i: (i, 0))],
        out_specs=[t, t, t, pl.BlockSpec((AW, tm), lambda i: (0, i))],
        out_shape=[_sds((S, AW), F32), _sds((S, AW), F32), _sds((S, AW), CDT), _sds((AW, S), CDT)],
        compiler_params=_cp(("parallel",)),
    )(*os_, *ls_, rest)


def _pool_counts(r, LS, win):
    l = lax.broadcasted_iota(jnp.int32, (LS, 1), 0)
    return jnp.minimum(16 * l + (r + 1), win).astype(F32)


def _pool_fwd(rest, pool_w, pool_scale):
    S = rest.shape[0]
    LS = S // R16
    r3 = rest.reshape(R16, LS, REST_W)
    PG = 128

    def body(u_ref, z_ref, pw_ref, ps_ref, pt_ref, mp_ref, pg_ref, pgt_ref, d_ref):
        g = pl.program_id(0)
        win = jnp.left_shift(2, g)
        row = lax.broadcasted_iota(jnp.int32, (LS, PG), 0)
        for r in range(1, R16):
            d_ref[r] = jnp.where(row == 0, 0.0, pltpu.roll(u_ref[r], 1, 0))
        pw = pw_ref[...].astype(CDT)
        for r in range(R16):
            u = u_ref[r]
            acc = u
            for j in range(1, R16):
                coef = jnp.where(j < win, 1.0, 0.0).astype(F32)
                src = u_ref[r - j] if r >= j else d_ref[r - j + R16]
                acc = acc + coef * src
            pooled = acc / _pool_counts(r, LS, win) - u
            pc = pooled.astype(CDT)
            mp = _dot(pc, pw, NN)
            z = z_ref[r]
            pg = (mp * ps_ref[...]) * (z * _sigmoid(z))
            mp_ref[r] = mp
            pg_ref[r] = pg.astype(pg_ref.dtype)
            pt_ref[:, r * LS:(r + 1) * LS] = pooled.T.astype(pt_ref.dtype)
            pgt_ref[:, r * LS:(r + 1) * LS] = pg.T.astype(pgt_ref.dtype)

    col = lambda off: pl.BlockSpec((R16, LS, PG), lambda g: (0, 0, off + g))
    tsp = pl.BlockSpec((PG, S), lambda g: (g, 0))
    pt, mp, pg, pgt = pl.pallas_call(
        body, name="pool_fwd", grid=(4,),
        in_specs=[col(24), col(4), pl.BlockSpec((None, PG, PG), lambda g: (g, 0, 0)),
                  pl.BlockSpec((1, PG), lambda g: (0, g))],
        out_specs=[tsp, col(0), col(0), tsp],
        out_shape=[_sds((AW, S), CDT), _sds((R16, LS, AW), F32), _sds((R16, LS, AW), CDT), _sds((AW, S), CDT)],
        scratch_shapes=[pltpu.VMEM((R16, LS, PG), F32)],
        compiler_params=_cp(("parallel",)),
    )(r3, r3, pool_w, pool_scale)
    return pt, mp.reshape(S, AW), pg.reshape(S, AW), pgt


def _branch_merge(ag, pg, wab, wpb, rest):
    S = ag.shape[0]
    tm, tn = 512, 512

    def body(ag_ref, pg_ref, wa_ref, wp_ref, ga_ref, gp_ref, ya_ref, yp_ref, m_ref, mt_ref):
        ya = _dot(ag_ref[...], wa_ref[...], NN)
        yp = _dot(pg_ref[...], wp_ref[...], NN)
        ya_ref[...] = ya
        yp_ref[...] = yp
        mg = _sigmoid(ga_ref[...]) * ya + _sigmoid(gp_ref[...]) * yp
        m_ref[...] = mg.astype(m_ref.dtype)
        mt_ref[...] = mg.T.astype(mt_ref.dtype)

    a = pl.BlockSpec((tm, AW), lambda i, j: (i, 0))
    w = pl.BlockSpec((AW, tn), lambda i, j: (0, j))
    o = pl.BlockSpec((tm, tn), lambda i, j: (i, j))
    return pl.pallas_call(
        body, name="branch_merge", grid=(S // tm, D // tn),
        in_specs=[a, a, w, w, pl.BlockSpec((tm, tn), lambda i, j: (i, 2 + j)),
                  pl.BlockSpec((tm, tn), lambda i, j: (i, 4 + j))],
        out_specs=[o, o, o, pl.BlockSpec((tn, tm), lambda i, j: (j, i))],
        out_shape=[_sds((S, D), F32), _sds((S, D), F32), _sds((S, D), CDT), _sds((D, S), CDT)],
        compiler_params=_cp(("parallel", "parallel")),
    )(ag, pg, wab, wpb, rest, rest)


def _out_loss(merged, wout, xs, ts, gate, final_g):
    LS = xs.shape[1]
    tl = 16

    def body(m_ref, w_ref, x_ref, t_ref, gate_ref, fg_ref, dx2_ref, dmo_ref, loss_ref, gfg_ref, dgate_ref):
        i = pl.program_id(0)

        @pl.when(i == 0)
        def _():
            loss_ref[...] = jnp.zeros_like(loss_ref)
            gfg_ref[...] = jnp.zeros_like(gfg_ref)
            dgate_ref[...] = jnp.zeros_like(dgate_ref)

        mo = _dot(_ld(m_ref), w_ref[...], NN)
        gate, fg = gate_ref[...], fg_ref[...]
        tgt = jnp.concatenate([t_ref[:, r, :] for r in range(R16)], axis=0)
        x2 = _ld(x_ref) + gate * mo
        r2 = lax.rsqrt(jnp.mean(x2 * x2, axis=-1, keepdims=True) + EPS)
        xn2 = x2 * r2
        diff = xn2 * fg - tgt
        loss = jnp.sum(jnp.sum(diff * diff, axis=-1, keepdims=True), axis=0, keepdims=True)
        dy = diff * (1.0 / D)
        dxn2 = dy * fg
        dx2 = r2 * (dxn2 - xn2 * jnp.mean(dxn2 * xn2, axis=-1, keepdims=True))
        _st(dx2_ref, dx2)
        _st(dmo_ref, dx2 * gate)
        loss_ref[...] += jnp.broadcast_to(loss * (0.5 / D), loss_ref.shape)
        gfg_ref[...] += jnp.sum(dy * xn2, axis=0, keepdims=True)
        dgate_ref[...] += jnp.sum(dx2 * mo, axis=0, keepdims=True)

    slab = pl.BlockSpec((R16, tl, D), lambda i: (0, i, 0))
    vec = pl.BlockSpec((1, D), lambda i: (0, 0))
    return pl.pallas_call(
        body, name="out_loss", grid=(LS // tl,),
        in_specs=[slab, _resident((D, D)), slab, pl.BlockSpec((tl, R16, D), lambda i: (i, 0, 0)), vec, vec],
        out_specs=[slab, slab, pl.BlockSpec((1, 128), lambda i: (0, 0)), vec, vec],
        out_shape=[_sds((R16, LS, D), F32), _sds((R16, LS, D), CDT), _sds((1, 128), F32), _sds((1, D), F32),
                   _sds((1, D), F32)],
        compiler_params=_cp(("arbitrary",)),
    )(merged, wout, xs, ts, gate, final_g)


def _bwd_gates(dmo, wout, wab, wpb, rest, ya, yp, attn, mp, pool_scale):
    S = dmo.shape[0]
    tm = 256

    def body(dmo_ref, wo_ref, wa_ref, wp_ref, za_ref, zp_ref, ga0, ga1, gp0, gp1, ya_ref, yp_ref, at_ref, mp_ref,
             ps_ref, dya_ref, dyp_ref, dr_ref, dat_ref, dl_ref, dmp_ref, dps_ref):
        i = pl.program_id(0)

        @pl.when(i == 0)
        def _():
            dps_ref[...] = jnp.zeros_like(dps_ref)

        dm = _dot(dmo_ref[...], wo_ref[...], NT)
        sa = _sigmoid(jnp.concatenate([ga0[...], ga1[...]], axis=1))
        sp = _sigmoid(jnp.concatenate([gp0[...], gp1[...]], axis=1))
        dya = (dm * sa).astype(CDT)
        dyp = (dm * sp).astype(CDT)
        dya_ref[...] = dya
        dyp_ref[...] = dyp
        dr_ref[:, 1024:2048] = (dm * ya_ref[...] * sa * (1.0 - sa)).astype(dr_ref.dtype)
        dr_ref[:, 2048:3072] = (dm * yp_ref[...] * sp * (1.0 - sp)).astype(dr_ref.dtype)
        da = _dot(dya, wa_ref[...], NT)
        dp = _dot(dyp, wp_ref[...], NT)
        za = za_ref[...]
        sga = _sigmoid(za)
        attn = at_ref[...]
        dattn = da * (za * sga)
        dr_ref[:, 0:512] = (da * attn * (sga * (1.0 + za * (1.0 - sga)))).astype(dr_ref.dtype)
        dat_ref[...] = dattn.astype(dat_ref.dtype)
        prod = dattn * attn
        for h in range(NH):
            sl = slice(HD * h, HD * h + HD)
            dl_ref[:, sl] = jnp.broadcast_to(jnp.sum(prod[:, sl], axis=1, keepdims=True), (tm, HD))
        zp = zp_ref[...]
        sgp = _sigmoid(zp)
        mpre = mp_ref[...]
        ps = ps_ref[...]
        dmixed = dp * (zp * sgp)
        dr_ref[:, 512:1024] = (dp * (mpre * ps) * (sgp * (1.0 + zp * (1.0 - sgp)))).astype(dr_ref.dtype)
        dps_ref[...] += jnp.sum(dmixed * mpre, axis=0, keepdims=True)
        dmp_ref[...] = (dmixed * ps).astype(dmp_ref.dtype)

    full = lambda shape: pl.BlockSpec(shape, lambda i: (0, 0))
    tD = pl.BlockSpec((tm, D), lambda i: (i, 0))
    tA = pl.BlockSpec((tm, AW), lambda i: (i, 0))
    rb = lambda c: pl.BlockSpec((tm, AW), lambda i: (i, c))
    return pl.pallas_call(
        body, name="bwd_gates", grid=(S // tm,),
        in_specs=[tD, full((D, D)), full((AW, D)), full((AW, D)), rb(0), rb(1), rb(2), rb(3), rb(4), rb(5),
                  tD, tD, tA, tA, full((1, AW))],
        out_specs=[tD, tD, pl.BlockSpec((tm, 3072), lambda i: (i, 0)), tA, tA, tA, full((1, AW))],
        out_shape=[_sds((S, D), CDT), _sds((S, D), CDT), _sds((S, REST_W), CDT), _sds((S, AW), CDT),
                   _sds((S, AW), F32), _sds((S, AW), CDT), _sds((1, AW), F32)],
        compiler_params=_cp(("arbitrary",), vmem=56),
    )(dmo, wout, wab, wpb, rest, rest, rest, rest, rest, rest, ya, yp, attn, mp, pool_scale)


def _pool_bwd(dmp, pt, pool_w, drest):
    S = dmp.shape[0]
    LS = S // R16
    PG = 128
    d3 = dmp.reshape(R16, LS, AW)
    dr3 = drest.reshape(R16, LS, REST_W)

    def body(d_ref, pt_ref, pw_ref, dr_in, du_ref, dpw_ref, g_ref, s_ref):
        del dr_in
        g = pl.program_id(0)
        win = jnp.left_shift(2, g)
        row = lax.broadcasted_iota(jnp.int32, (LS, PG), 0)
        pw = pw_ref[...].astype(CDT)
        dpw = jnp.zeros((PG, PG), F32)
        for r in range(R16):
            dm = d_ref[r]
            dpw = dpw + _dot(pt_ref[:, r * LS:(r + 1) * LS], dm, NN)
            dpooled = _dot(dm, pw, NT)
            g_ref[r] = dpooled
            s_ref[r] = dpooled / _pool_counts(r, LS, win)
        dpw_ref[...] = dpw
        for r in range(R16):
            acc = s_ref[r]
            for j in range(1, R16):
                coef = jnp.where(j < win, 1.0, 0.0).astype(F32)
                if r + j < R16:
                    src = s_ref[r + j]
                else:
                    src = jnp.where(row == LS - 1, 0.0, pltpu.roll(s_ref[r + j - R16], LS - 1, 0))
                acc = acc + coef * src
            du_ref[r] = (acc - g_ref[r]).astype(du_ref.dtype)

    du, dpw = pl.pallas_call(
        body, name="pool_bwd", grid=(4,),
        in_specs=[pl.BlockSpec((R16, LS, PG), lambda g: (0, 0, g)), pl.BlockSpec((PG, S), lambda g: (g, 0)),
                  pl.BlockSpec((None, PG, PG), lambda g: (g, 0, 0)), pl.BlockSpec(memory_space=pl.ANY)],
        out_specs=[pl.BlockSpec((R16, LS, PG), lambda g: (0, 0, 24 + g)),
                   pl.BlockSpec((None, PG, PG), lambda g: (g, 0, 0))],
        out_shape=[_sds((R16, LS, REST_W), CDT), _sds((4, PG, PG), F32)],
        scratch_shapes=[pltpu.VMEM((R16, LS, PG), F32), pltpu.VMEM((R16, LS, PG), F32)],
        input_output_aliases={3: 0},
        compiler_params=_cp(("parallel",)),
    )(d3, pt, pool_w, dr3)
    return du.reshape(S, REST_W), dpw


def _attn_bwd(qkv, dattn, Lb, dlb, bias, geom, dqkv_prev):
    S = qkv.shape[0]
    gi, T, Tk, has_prev, N = geom.gi, geom.T, geom.Tk, geom.has_prev, geom.n_tiles
    nd_outer = len(geom.outer)
    qv = geom.view(qkv)
    cq, ck, cv = 3 * gi, 3 * gi + 1, 3 * gi + 2
    W3 = 3 * AW

    def heads(q, k, v, do, L, dl, b_ref, pm, dsum_ref, emit):
        for h in range(NH):
            sl = slice(HD * h, HD * h + HD)
            qs = (q[:, sl].astype(F32) * 0.125).astype(q.dtype)
            kh, vh, doh = k[:, sl], v[:, sl], do[:, sl]
            s = _dot(qs, kh, NT) + b_ref[h]
            if pm is not None:
                s = s + pm
            p = jnp.exp(s - L[:, HD * h:HD * h + 1])
            dp = _dot(doh, vh, NT)
            ds = p * (dp - dl[:, HD * h:HD * h + 1])
            dsum_ref[h] += ds
            dsc = ds.astype(q.dtype)
            dq = _dot(dsc, kh, NN) * 0.125
            dk = _dot(dsc, qs, TN)
            dv = _dot(p.astype(q.dtype), doh, TN)
            emit(sl, dq, dk, dv)

    def first_step():
        ok = pl.program_id(nd_outer) == 0 if has_prev else pl.program_id(0) == 0
        for a in range(nd_outer if has_prev else 0):
            ok = ok & (pl.program_id(a) == 0)
        return ok

    if has_prev:
        def body(q_ref, kp_ref, kc_ref, vp_ref, vc_ref, do_ref, L_ref, dl_ref, b_ref, prev_in, out_ref, dsum_ref,
                 hold, new, pp):
            del prev_in
            s_id = pl.program_id(nd_outer)

            @pl.when(first_step())
            def _():
                dsum_ref[...] = jnp.zeros_like(dsum_ref)

            @pl.when(s_id < N)
            def _():
                q = _ld(q_ref)
                k = jnp.concatenate([_ld(kp_ref), _ld(kc_ref)], axis=0)
                v = jnp.concatenate([_ld(vp_ref), _ld(vc_ref)], axis=0)
                pm = _prev_mask(s_id, T, Tk)

                def emit(sl, dq, dk, dv):
                    new[:, sl] = dq
                    new[:, slice(AW + sl.start, AW + sl.stop)] = dk[T:]
                    new[:, slice(2 * AW + sl.start, 2 * AW + sl.stop)] = dv[T:]
                    pp[:, sl] = dk[:T]
                    pp[:, slice(AW + sl.start, AW + sl.stop)] = dv[:T]

                heads(q, k, v, _ld(do_ref), _ld(L_ref), _ld(dl_ref), b_ref, pm, dsum_ref, emit)

            @pl.when((s_id >= 1) & (s_id < N))
            def _():
                _st(out_ref, hold[:, 0:AW], slice(0, AW))
                _st(out_ref, hold[:, AW:W3] + pp[...], slice(AW, W3))

            @pl.when(s_id == N)
            def _():
                _st(out_ref, hold[...])

            @pl.when(s_id < N)
            def _():
                hold[...] = new[...]

        cur = lambda n: jnp.minimum(n, N - 1)
        prev = lambda n: jnp.clip(n - 1, 0, N - 1)
        delayed = lambda n: jnp.maximum(n - 1, 0)
        in_specs = [geom.spec(AW, cq, cur), geom.spec(AW, ck, prev), geom.spec(AW, ck, cur),
                    geom.spec(AW, cv, prev), geom.spec(AW, cv, cur),
                    geom.spec(AW, 0, cur), geom.spec(AW, 0, cur), geom.spec(AW, 0, cur)]
        args = [qv] * 5 + [geom.view(dattn), geom.view(Lb), geom.view(dlb)]
        out_spec = geom.spec(W3, gi, delayed)
        grid = geom.outer + (N + 1,)
        scratch = [pltpu.VMEM((T, W3), F32), pltpu.VMEM((T, W3), F32), pltpu.VMEM((T, 2 * AW), F32)]
    else:
        def body(q_ref, kc_ref, vc_ref, do_ref, L_ref, dl_ref, b_ref, prev_in, out_ref, dsum_ref):
            del prev_in

            @pl.when(first_step())
            def _():
                dsum_ref[...] = jnp.zeros_like(dsum_ref)

            def emit(sl, dq, dk, dv):
                _st(out_ref, dq, sl)
                _st(out_ref, dk, slice(AW + sl.start, AW + sl.stop))
                _st(out_ref, dv, slice(2 * AW + sl.start, 2 * AW + sl.stop))

            heads(_ld(q_ref), _ld(kc_ref), _ld(vc_ref), _ld(do_ref), _ld(L_ref), _ld(dl_ref), b_ref, None,
                  dsum_ref, emit)

        cur = lambda n: n
        in_specs = [geom.spec(AW, cq, cur), geom.spec(AW, ck, cur), geom.spec(AW, cv, cur),
                    geom.spec(AW, 0, cur), geom.spec(AW, 0, cur), geom.spec(AW, 0, cur)]
        args = [qv] * 3 + [geom.view(dattn), geom.view(Lb), geom.view(dlb)]
        out_spec = geom.spec(W3, gi, cur)
        grid = geom.outer
        scratch = []

    in_specs.append(pl.BlockSpec((NH, T, Tk), lambda *ids: (0, 0, 0)))
    in_specs.append(pl.BlockSpec(memory_space=pl.ANY))
    if dqkv_prev is None:
        last, aliases = bias, {}
    else:
        last, aliases = geom.view(dqkv_prev), {len(in_specs) - 1: 0}
    dq, dsum = pl.pallas_call(
        body, name=f"attn_bwd{gi}", grid=grid, in_specs=in_specs,
        out_specs=[out_spec, pl.BlockSpec((NH, T, Tk), lambda *ids: (0, 0, 0))],
        out_shape=[_sds(geom.view_shape(QKV_W), CDT), _sds((NH, T, Tk), F32)],
        scratch_shapes=scratch,
        input_output_aliases=aliases,
        compiler_params=_cp(("arbitrary",) * len(grid), vmem=56),
    )(*args, bias, last)
    return geom.unview(dq), dsum


def _h_bwd(dh, dx2, xs, norm_g, scale):
    LS = xs.shape[1]
    tl = 16

    def body(dh_ref, dx2_ref, x_ref, g_ref, sc_ref, gx_ref, dsh_ref, dsc_ref, dng_ref):
        i = pl.program_id(0)

        @pl.when(i == 0)
        def _():
            dsh_ref[...] = jnp.zeros_like(dsh_ref)
            dsc_ref[...] = jnp.zeros_like(dsc_ref)
            dng_ref[...] = jnp.zeros_like(dng_ref)

        g = g_ref[...]
        one_sc = 1.0 + sc_ref[...]
        x = _ld(x_ref)
        r1 = lax.rsqrt(jnp.mean(x * x, axis=-1, keepdims=True) + EPS)
        xn = x * r1
        dh = _ld(dh_ref)
        dhx = dh * xn
        dxn = dh * (g * one_sc)
        dx1 = r1 * (dxn - xn * jnp.mean(dxn * xn, axis=-1, keepdims=True))
        gx = _ld(dx2_ref) + dx1
        for r in range(R16):
            gx_ref[:, r, :] = gx[r * tl:(r + 1) * tl]
        dsh_ref[...] += jnp.sum(dh, axis=0, keepdims=True)
        dsc_ref[...] += jnp.sum(dhx * g, axis=0, keepdims=True)
        dng_ref[...] += jnp.sum(dhx * one_sc, axis=0, keepdims=True)

    slab = pl.BlockSpec((R16, tl, D), lambda i: (0, i, 0))
    vec = pl.BlockSpec((1, D), lambda i: (0, 0))
    return pl.pallas_call(
        body, name="h_bwd", grid=(LS // tl,),
        in_specs=[slab, slab, slab, vec, vec],
        out_specs=[pl.BlockSpec((tl, R16, D), lambda i: (i, 0, 0)), vec, vec, vec],
        out_shape=[_sds((LS, R16, D), F32), _sds((1, D), F32), _sds((1, D), F32), _sds((1, D), F32)],
        compiler_params=_cp(("arbitrary",)),
    )(dh, dx2, xs, norm_g, scale)


def _local_step(x, target, shift, scale, gate, norm_g, w_in_full, pool_w, pool_scale, wab, wpb, wout, rel_bias,
                final_g):
    st = _local_pre(x, shift, scale, norm_g, rel_bias)
    st = _local_grads(st, target, gate, w_in_full, pool_w, pool_scale, wab, wpb, wout, final_g)
    d_rel_bias = _local_bias_grads(st)
    gx, dmod, dng = _local_input_grads(st, w_in_full, norm_g, scale)
    return dict(loss=st["loss"], grad_x=gx, dw_in=st["dw_in"], dw_ab=st["dw_ab"], dw_pb=st["dw_pb"],
                dw_out=st["dw_out"], d_norm_g=dng, d_pool_w=st["dpw"], d_pool_scale=st["dps"],
                d_rel_bias=d_rel_bias, d_final_g=st["gfg"], dmod=dmod)


def _local_pre(x, shift, scale, norm_g, rel_bias):
    S = x.shape[0]
    LS = S // R16
    geoms = [_Geom(gi, LS) for gi in range(3)]
    idxs = [jnp.asarray(g.bucket_index()) for g in geoms]
    biases = [_bias_table(idxs[gi], rel_bias, gi) for gi in range(3)]
    h3, xs3 = _norm_mod(x.reshape(LS, R16, D), norm_g, shift, scale)
    h = h3.reshape(S, D)
    return dict(geoms=geoms, idxs=idxs, biases=biases, xs3=xs3, h=h, ht=_tcast(h, "h_transpose"))


def _local_grads(st, target, gate, w_in_full, pool_w, pool_scale, wab, wpb, wout, final_g):
    geoms, idxs, biases, xs3, h, ht = (st[k] for k in ("geoms", "idxs", "biases", "xs3", "h", "ht"))
    S = h.shape[0]
    LS = S // R16
    qkv, rest = _proj(h, w_in_full)

    os_, ls_ = [], []
    for gi in range(3):
        o, l = _attn_fwd(qkv, biases[gi], geoms[gi])
        os_.append(o)
        ls_.append(l)
    attn, Lb, ag, agt = _attn_combine(os_, ls_, rest)
    pt, mp, pg, pgt = _pool_fwd(rest, pool_w, pool_scale)
    ya, yp, merged, mt = _branch_merge(ag, pg, wab, wpb, rest)
    dx2_3, dmo3, loss, gfg, dgate = _out_loss(merged.reshape(R16, LS, D), wout, xs3, target.reshape(LS, R16, D),
                                              gate, final_g)
    dmo = dmo3.reshape(S, D)

    dya, dyp, drest, dattn, dlb, dmp, dps = _bwd_gates(dmo, wout, wab, wpb, rest, ya, yp, attn, mp, pool_scale)
    dw_out = _mm_tn(mt, dmo, n_total=D, out_blk=lambda j: j, name="dw_out")
    dw_ab = _mm_tn(agt, dya, n_total=D, out_blk=lambda j: j, name="dw_attn_br")
    dw_pb = _mm_tn(pgt, dyp, n_total=D, out_blk=lambda j: j, name="dw_pool_br")
    drest, dpw = _pool_bwd(dmp, pt, pool_w, drest)

    dqkv = None
    dsums = []
    for gi in range(3):
        dqkv, dsum = _attn_bwd(qkv, dattn, Lb, dlb, biases[gi], geoms[gi], dqkv)
        dsums.append(dsum)

    dw_in = _mm_tn(ht, dqkv, n_total=8192, out_blk=lambda j: j, name="dw_in_qkv")
    dw_in = _mm_tn(ht, drest, n_total=8192, out_blk=_rest_blk, name="dw_in_rest", prev=dw_in)
    return dict(st, loss=loss[0, 0], dw_in=dw_in, dw_ab=dw_ab, dw_pb=dw_pb, dw_out=dw_out, dpw=dpw, dps=dps, gfg=gfg,
                dgate=dgate, dsums=dsums, dqkv=dqkv, drest=drest, dx2_3=dx2_3)


def _local_bias_grads(st, dep=None):
    drb = [_bias_grad(st["dsums"][gi], st["idxs"][gi], gi, dep if gi == 0 else None) for gi in range(3)]
    return jnp.concatenate(drb, axis=1)


def _local_input_grads(st, w_in_full, norm_g, scale, dep=None):
    S = st["h"].shape[0]
    LS = S // R16
    dh = _dh(st["dqkv"], st["drest"], w_in_full, dep)
    gx3, dsh, dsc, dng = _h_bwd(dh.reshape(R16, LS, D), st["dx2_3"], st["xs3"], norm_g, scale)
    return gx3.reshape(S, D), jnp.concatenate([dsh, dsc, st["dgate"]], axis=1), dng


def _my_place():
    return lax.axis_index("x"), lax.axis_index("y"), lax.axis_index("c")


def _block_of(ref, axis, idx, width):
    if axis is None:
        return ref.at[idx]
    start = pl.multiple_of(idx * width, width)
    if axis == 0:
        return ref.at[pl.ds(start, width)]
    return ref.at[:, pl.ds(start, width)]


_SEM = pl.BlockSpec(memory_space=pltpu.SEMAPHORE)
_EFFECT = pltpu.SideEffectType.DATAFLOW_SIDE_EFFECTING


def _split_start(name, srcs, lands, ncopy, plan):
    arrs = list(srcs) + list(lands)
    na, ns = len(arrs), len(srcs)

    def body(*refs):
        send_sems, recv_sems, token = refs[na], refs[na + 1], refs[-1]
        for k, (s, d, dev) in enumerate(plan(refs[:ns], refs[ns:na])):
            pltpu.make_async_remote_copy(src_ref=s, dst_ref=d, send_sem=send_sems.at[k], recv_sem=recv_sems.at[k],
                                         device_id=dev, device_id_type=MESH).start()
        token[...] = jnp.zeros_like(token)

    outs = pl.pallas_call(
        body, name=name,
        out_shape=(pltpu.SemaphoreType.DMA((ncopy,)), pltpu.SemaphoreType.DMA((ncopy,)),
                   *[pltpu.HBM(a.shape, a.dtype) for a in arrs], _sds((8, 128), F32)),
        in_specs=[_HBM] * na,
        out_specs=(_SEM, _SEM, *[_HBM] * na, pl.BlockSpec(memory_space=pltpu.VMEM)),
        input_output_aliases={i: 2 + i for i in range(na)},
        compiler_params=pltpu.CompilerParams(has_side_effects=_EFFECT),
    )(*[pltpu.with_memory_space_constraint(a, pltpu.HBM) for a in arrs])
    return (outs[0], outs[1]), list(outs[2:2 + ns]), list(outs[2 + ns:2 + na]), outs[-1]


def _split_wait(name, sems, srcs, lands, plan, after):
    arrs = list(srcs) + list(lands)
    na, ns = len(arrs), len(srcs)

    def body(*refs):
        send_sems, recv_sems = refs[na], refs[na + 1]
        for k, (s, d, dev) in enumerate(plan(refs[:ns], refs[ns:na])):
            cp = pltpu.make_async_remote_copy(src_ref=s, dst_ref=d, send_sem=send_sems.at[k], recv_sem=recv_sems.at[k],
                                              device_id=dev, device_id_type=MESH)
            cp.wait_send()
            cp.wait_recv()

    outs = pl.pallas_call(
        body, name=name, out_shape=[pltpu.HBM(a.shape, a.dtype) for a in arrs],
        in_specs=[_HBM] * na + [_SEM, _SEM, pl.BlockSpec(memory_space=pl.ANY)], out_specs=[_HBM] * na,
        input_output_aliases={i: i for i in range(na)},
        compiler_params=pltpu.CompilerParams(has_side_effects=_EFFECT),
    )(*arrs, sems[0], sems[1], after)
    return list(outs[:ns]), list(outs[ns:])


def _gather_chips_plan(layout):
    def plan(src, land):
        x, y, c = _my_place()
        me = 4 * x + 2 * y + c
        return [(src[p], _block_of(land[p], layout[p][0], me, layout[p][1]), (cx, cy, c))
                for p in range(len(src)) for cx, cy in [(1 - x, y), (x, 1 - y), (1 - x, 1 - y)]]
    return plan


def _pair_plan(layout):
    def plan(src, land):
        x, y, c = _my_place()
        return [(_block_of(src[p], layout[p][0], 2 * q + (1 - c), layout[p][1]), land[p].at[q], (x, y, 1 - c))
                for p in range(len(src)) for q in range(4)]
    return plan


def _chips_plan(src, land):
    x, y, c = _my_place()
    return [(src[p].at[2 * cx + cy], land[p].at[2 * x + y], (cx, cy, c))
            for p in range(len(src)) for cx, cy in [(1 - x, y), (x, 1 - y), (1 - x, 1 - y)]]


def _pair_share(srcs, gathered, layout):
    n = len(srcs)

    def body(*refs):
        src, land = refs[:n], refs[2 * n:3 * n]
        send_sems, recv_sems, local_sems = refs[3 * n:]
        x, y, c = _my_place()
        sibling = (x, y, 1 - c)
        chips = [(x, y), (1 - x, y), (x, 1 - y), (1 - x, 1 - y)]

        def region(p, cx, cy, pc):
            return _block_of(land[p], layout[p][0], 4 * cx + 2 * cy + pc, layout[p][1])

        cps = []
        for p in range(n):
            cps.append(pltpu.make_async_copy(src[p], region(p, x, y, c), local_sems.at[p]))
            for k, (cx, cy) in enumerate(chips):
                cps.append(pltpu.make_async_remote_copy(
                    src_ref=src[p] if k == 0 else region(p, cx, cy, c), dst_ref=region(p, cx, cy, c),
                    send_sem=send_sems.at[4 * p + k], recv_sem=recv_sems.at[4 * p + k],
                    device_id=sibling, device_id_type=MESH))
        for cp in cps:
            cp.start()
        for cp in cps:
            cp.wait()

    return pl.pallas_call(
        body, name="gather_pair_share", in_specs=[_HBM] * (2 * n), out_specs=[_HBM] * n,
        out_shape=[_sds(g.shape, g.dtype) for g in gathered],
        input_output_aliases={n + p: p for p in range(n)},
        scratch_shapes=[pltpu.SemaphoreType.DMA((4 * n,)), pltpu.SemaphoreType.DMA((4 * n,)),
                        pltpu.SemaphoreType.DMA((n,))],
    )(*srcs, *gathered)


def _keep_own(parts):
    n = len(parts)

    def body(*refs):
        src, out, sems = refs[:n], refs[n:2 * n], refs[2 * n]
        x, y, _ = _my_place()
        cps = [pltpu.make_async_copy(src[p].at[2 * x + y], out[p].at[2 * x + y], sems.at[p]) for p in range(n)]
        for cp in cps:
            cp.start()
        for cp in cps:
            cp.wait()

    return pl.pallas_call(body, name="keep_own", in_specs=[_HBM] * n, out_specs=[_HBM] * n,
                          out_shape=[_sds(a.shape, a.dtype) for a in parts],
                          scratch_shapes=[pltpu.SemaphoreType.DMA((n,))])(*parts)


def _all_gather(srcs, out_shapes, layout, name):
    n = len(srcs)

    def body(*refs):
        src, out = refs[:n], refs[n:2 * n]
        send_sems, recv_sems, local_sems = refs[2 * n:]
        x, y, c = _my_place()
        me, sibling = (x, y, c), (x, y, 1 - c)
        chips = [(1 - x, y), (x, 1 - y), (1 - x, 1 - y)]

        def region(p, dev):
            px, py, pc = dev
            return _block_of(out[p], layout[p][0], 4 * px + 2 * py + pc, layout[p][1])

        def copy(p, k, block, to, src_ref=None):
            return pltpu.make_async_remote_copy(
                src_ref=region(p, block) if src_ref is None else src_ref, dst_ref=region(p, block),
                send_sem=send_sems.at[p * 7 + k], recv_sem=recv_sems.at[p * 7 + k],
                device_id=to, device_id_type=MESH)

        mine = [pltpu.make_async_copy(src[p], region(p, me), local_sems.at[p]) for p in range(n)]
        for cp in mine:
            cp.start()
        first = []
        for p in range(n):
            first.append(copy(p, 0, me, sibling, src[p]))
            first += [copy(p, 1 + j, me, (*chip, c), src[p]) for j, chip in enumerate(chips)]
        for cp in first:
            cp.start()
        passed = []
        for j, chip in enumerate(chips):
            for p in range(n):
                copy(p, 1 + j, (*chip, c), me).wait_recv()
                fwd = copy(p, 4 + j, (*chip, c), sibling)
                fwd.start()
                passed.append(fwd)
        for p in range(n):
            copy(p, 0, sibling, me).wait_recv()
            for j, chip in enumerate(chips):
                copy(p, 4 + j, (*chip, 1 - c), me).wait_recv()
        for cp in first + passed:
            cp.wait_send()
        for cp in mine:
            cp.wait()

    return pl.pallas_call(
        body, name=name, in_specs=[_HBM] * n, out_specs=[_HBM] * n,
        out_shape=[_sds(s, a.dtype) for s, a in zip(out_shapes, srcs)],
        scratch_shapes=[pltpu.SemaphoreType.DMA((7 * n,)), pltpu.SemaphoreType.DMA((7 * n,)),
                        pltpu.SemaphoreType.DMA((n,))],
    )(*srcs)


def _reduce_pair_exchange(dws, layout, smalls):
    n = len(dws)

    def body(*refs):
        src, sm = refs[:n], refs[n]
        out, sm_out = refs[n + 1:2 * n + 1], refs[2 * n + 1]
        send_sems, recv_sems = refs[2 * n + 2:]
        x, y, c = _my_place()
        sibling = (x, y, 1 - c)
        cps = []
        for p in range(n):
            for q in range(4):
                cps.append(pltpu.make_async_remote_copy(
                    src_ref=_block_of(src[p], layout[p][0], 2 * q + (1 - c), layout[p][1]), dst_ref=out[p].at[q],
                    send_sem=send_sems.at[4 * p + q], recv_sem=recv_sems.at[4 * p + q],
                    device_id=sibling, device_id_type=MESH))
        cps.append(pltpu.make_async_remote_copy(src_ref=sm, dst_ref=sm_out, send_sem=send_sems.at[4 * n],
                                                recv_sem=recv_sems.at[4 * n], device_id=sibling, device_id_type=MESH))
        for cp in cps:
            cp.start()
        for cp in cps:
            cp.wait()

    def shard_shape(a, lay):
        axis, w = lay
        return (4, w, a.shape[1]) if axis == 0 else (4, a.shape[0], w)

    return pl.pallas_call(
        body, name="reduce_pair_exchange", in_specs=[_HBM] * (n + 1), out_specs=[_HBM] * (n + 1),
        out_shape=[_sds(shard_shape(a, l), F32) for a, l in zip(dws, layout)] + [_sds(smalls.shape, F32)],
        scratch_shapes=[pltpu.SemaphoreType.DMA((4 * n + 1,)), pltpu.SemaphoreType.DMA((4 * n + 1,))],
    )(*dws, smalls)


def _pair_add(dw, recv, axis, width, cidx, name):
    if axis == 0:
        tr = min(width, 256)
        C = dw.shape[1]
        grid = (4, width // tr)
        nb = width // tr
        own = pl.BlockSpec((tr, C), lambda q, i, c_ref: ((2 * q + c_ref[0]) * nb + i, 0))
        oth = pl.BlockSpec((None, tr, C), lambda q, i, c_ref: (q, i, 0))
    else:
        R = dw.shape[0]
        tr = min(R, 256)
        grid = (4, R // tr)
        own = pl.BlockSpec((tr, width), lambda q, i, c_ref: (i, 2 * q + c_ref[0]))
        oth = pl.BlockSpec((None, tr, width), lambda q, i, c_ref: (q, i, 0))

    def body(c_ref, a_ref, b_ref, o_ref):
        o_ref[...] = (a_ref[...] + b_ref[...]).astype(o_ref.dtype)

    return pl.pallas_call(
        body, name=name,
        grid_spec=pltpu.PrefetchScalarGridSpec(num_scalar_prefetch=1, grid=grid, in_specs=[own, oth], out_specs=oth),
        out_shape=_sds(recv.shape, CDT), compiler_params=_cp(("parallel", "parallel")),
    )(cidx, dw, recv)


def _add2(a, b, name):
    def body(a_ref, b_ref, o_ref):
        o_ref[...] = a_ref[...] + b_ref[...]

    return pl.pallas_call(body, name=name, out_shape=_sds(a.shape, F32))(a, b)


def _reduce_chip_exchange(parts):
    n = len(parts)

    def body(*refs):
        src, out = refs[:n], refs[n:2 * n]
        send_sems, recv_sems, local_sems = refs[2 * n:]
        x, y, c = _my_place()
        my_slot = 2 * x + y
        chips = [(1 - x, y), (x, 1 - y), (1 - x, 1 - y)]
        cps = []
        for p in range(n):
            keep = pltpu.make_async_copy(src[p].at[my_slot], out[p].at[my_slot], local_sems.at[p])
            keep.start()
            cps.append(keep)
            for k, (cx, cy) in enumerate(chips):
                cps.append(pltpu.make_async_remote_copy(
                    src_ref=src[p].at[2 * cx + cy], dst_ref=out[p].at[my_slot],
                    send_sem=send_sems.at[3 * p + k], recv_sem=recv_sems.at[3 * p + k],
                    device_id=(cx, cy, c), device_id_type=MESH))
                cps[-1].start()
        for cp in cps:
            cp.wait()

    return pl.pallas_call(
        body, name="reduce_chip_exchange", in_specs=[_HBM] * n, out_specs=[_HBM] * n,
        out_shape=[_sds(a.shape, a.dtype) for a in parts],
        scratch_shapes=[pltpu.SemaphoreType.DMA((3 * n,)), pltpu.SemaphoreType.DMA((3 * n,)),
                        pltpu.SemaphoreType.DMA((n,))],
    )(*parts)


def _adaln_part(c_all, w_ada, b_part, dep=None):
    dep_specs, dep_args = _after(dep)
    vmem = pl.BlockSpec(memory_space=pltpu.VMEM)

    def body(c_ref, w_ref, b_ref, *rest):
        rest[-1][...] = _dot(c_ref[...].astype(CDT), w_ref[...].astype(CDT), NN) + b_ref[...]

    return pl.pallas_call(body, name="adaln_part", in_specs=[vmem, vmem, vmem] + dep_specs, out_specs=vmem,
                          out_shape=_sds((8, w_ada.shape[1]), F32),
                          compiler_params=_cp())(c_all, w_ada, b_part, *dep_args)


def _adam_math(w, g, m, v):
    m = ADAM_B1 * m + (1.0 - ADAM_B1) * g
    v = ADAM_B2 * v + (1.0 - ADAM_B2) * (g * g)
    m_hat = m / (1.0 - ADAM_B1 ** ADAM_STEP)
    v_hat = v / (1.0 - ADAM_B2 ** ADAM_STEP)
    delta = -ADAM_LR * (m_hat / (jnp.sqrt(v_hat) + ADAM_EPS) + ADAM_WD * w)
    return delta, m, v


def _adam(gparts, w, m, v, name):
    P, R, C = gparts.shape
    tr = R if R <= 256 else 256

    def body(g_ref, w_ref, m_ref, v_ref, go_ref, d_ref, mo_ref, vo_ref):
        g = g_ref[0].astype(F32)
        for i in range(1, P):
            g = g + g_ref[i].astype(F32)
        d, mn, vn = _adam_math(w_ref[...], g, m_ref[...], v_ref[...])
        go_ref[...] = g
        d_ref[...] = d
        mo_ref[...] = mn
        vo_ref[...] = vn

    t = pl.BlockSpec((tr, C), lambda i: (i, 0))
    return pl.pallas_call(
        body, name=name, grid=(R // tr,),
        in_specs=[pl.BlockSpec((P, tr, C), lambda i: (0, i, 0)), t, t, t], out_specs=[t, t, t, t],
        out_shape=[_sds((R, C), F32)] * 4, compiler_params=_cp(("parallel",)),
    )(gparts, w, m, v)


def _sum_parts(gparts, name):
    P, R, C = gparts.shape

    def body(g_ref, o_ref):
        g = g_ref[0]
        for i in range(1, P):
            g = g + g_ref[i]
        o_ref[...] = g

    return pl.pallas_call(body, name=name, out_shape=_sds((R, C), F32))(gparts)


def _adam_w_ada(c_t, dm, w, m, v):
    R, C = w.shape

    def body(c_ref, dm_ref, w_ref, m_ref, v_ref, go_ref, d_ref, mo_ref, vo_ref):
        g = c_ref[:, 0:1] * dm_ref[0:1, :]
        for b in range(1, 8):
            g = g + c_ref[:, b:b + 1] * dm_ref[b:b + 1, :]
        d, mn, vn = _adam_math(w_ref[...], g, m_ref[...], v_ref[...])
        go_ref[...] = g
        d_ref[...] = d
        mo_ref[...] = mn
        vo_ref[...] = vn

    return pl.pallas_call(body, name="adam_w_ada", out_shape=[_sds((R, C), F32)] * 4,
                          compiler_params=_cp())(c_t, dm, w, m, v)


_SM = dict(norm_g=(0, 8), pool_w=(8, 512), pool_scale=(520, 8), rel_bias=(528, 8), final_g=(536, 8),
           dmod=(544, 192), loss=(736, 8))
_SM_ROWS = 744


def _pack_rows(a, rows):
    flat = a.reshape(-1).astype(F32)
    return jnp.pad(flat, (0, rows * 128 - flat.shape[0])).reshape(rows, 128)


def kernel(x, c, norm_g, w_ada, b_ada, w_in, pool_w, pool_scale, w_attn_br, w_pool_br, w_out, rel_bias, final_g, loss_target, m_norm_g, m_w_ada, m_b_ada, m_w_in, m_pool_w, m_pool_scale, m_w_attn_br, m_w_pool_br, m_w_out, m_rel_bias, m_final_g, v_norm_g, v_w_ada, v_b_ada, v_w_in, v_pool_w, v_pool_scale, v_w_attn_br, v_w_pool_br, v_w_out, v_rel_bias, v_final_g):
    S = x.shape[1]
    px, py, pc = _my_place()
    me = 4 * px + 2 * py + pc
    cidx = jnp.reshape(pc, (1,)).astype(jnp.int32)

    layout = [(1, 1024), (1, 128), (1, 128), (0, 128)]
    full_shapes = [(D, 8192), (AW, D), (AW, D), (D, D)]
    final_g2 = final_g.reshape(1, D)

    (c_rows,) = _all_gather([jnp.broadcast_to(c, (8, D))], [(64, D)], [(0, 8)], "gather_c")
    c_all = c_rows[::8]

    shards = [w_in[0].astype(CDT), w_attn_br[0].astype(CDT), w_pool_br[0].astype(CDT), w_out[0].astype(CDT)]
    gplan = _gather_chips_plan(layout)
    g_sems, shards, lands, tok = _split_start("gather_chips_start", shards,
                                              [lax.empty(s, CDT) for s in full_shapes], 12, gplan)

    b_part = lax.dynamic_slice(b_ada, (0, me * 384), (1, 384))
    mod_part = _adaln_part(c_all, w_ada[0], b_part, dep=tok)
    (mod_all,) = _all_gather([mod_part], [(8, 8, 384)], [(None, 1)], "gather_mod")
    mod = lax.dynamic_index_in_dim(mod_all, me, axis=1, keepdims=False).reshape(1, 3 * D)
    shift, scale, gate = mod[:, :D], mod[:, D:2 * D], mod[:, 2 * D:]
    st = _local_pre(x[0], shift, scale, norm_g, rel_bias)

    shards, lands = _split_wait("gather_chips_wait", g_sems, shards, lands, gplan, st["ht"])
    w_in_full, wab, wpb, wout = _pair_share(shards, lands, layout)

    st = _local_grads(st, loss_target[0], gate, w_in_full, pool_w[0], pool_scale, wab, wpb, wout, final_g2)

    dws = [st["dw_in"], st["dw_ab"], st["dw_pb"], st["dw_out"]]
    pplan = _pair_plan(layout)
    pair_shapes = [(4, w, a.shape[1]) if ax == 0 else (4, a.shape[0], w) for a, (ax, w) in zip(dws, layout)]
    p_sems, dws, recv, tok = _split_start("reduce_pair_start", dws, [lax.empty(s, F32) for s in pair_shapes], 16,
                                          pplan)
    d_rel_bias = _local_bias_grads(st, dep=tok)
    dws, recv = _split_wait("reduce_pair_wait", p_sems, dws, recv, pplan, d_rel_bias)
    parts = [_pair_add(dw, r, ax, w, cidx, f"pair_add{i}") for i, (dw, r, (ax, w)) in enumerate(zip(dws, recv, layout))]

    c_sems, parts, lands, tok = _split_start("reduce_chips_start", parts, _keep_own(parts), 12, _chips_plan)
    grad_x, dmod, d_norm_g = _local_input_grads(st, w_in_full, norm_g, scale, dep=tok)

    dmod_blk = lax.dynamic_update_slice(jnp.zeros((8, 3 * D), F32), dmod, (me, 0))
    smalls = jnp.concatenate([
        _pack_rows(d_norm_g, 8), _pack_rows(st["dpw"], 512), _pack_rows(st["dps"], 8),
        _pack_rows(d_rel_bias, 8), _pack_rows(st["gfg"], 8), _pack_rows(dmod_blk, 192),
        _pack_rows(st["loss"], 8)], axis=0)
    (sm_recv,) = _reduce_pair_exchange([], [], smalls)
    sm_pair = _add2(smalls, sm_recv, "pair_add_smalls")
    (sm4,) = _reduce_chip_exchange([jnp.broadcast_to(sm_pair[None], (4, _SM_ROWS, 128))])

    def sm(name):
        lo, n = _SM[name]
        return sm4[:, lo:lo + n]

    def small_adam(name, w, m, v, nvalid):
        rows = _SM[name][1]
        outs = _adam(sm(name), _pack_rows(w, rows), _pack_rows(m, rows), _pack_rows(v, rows), f"adam_{name}")
        return [o.reshape(-1)[:nvalid].reshape(w.shape) for o in outs]

    o_norm_g = small_adam("norm_g", norm_g, m_norm_g, v_norm_g, D)
    o_pool_w = small_adam("pool_w", pool_w, m_pool_w, v_pool_w, 65536)
    o_pool_scale = small_adam("pool_scale", pool_scale, m_pool_scale, v_pool_scale, AW)
    o_rel_bias = small_adam("rel_bias", rel_bias, m_rel_bias, v_rel_bias, 768)
    o_final_g = small_adam("final_g", final_g, m_final_g, v_final_g, D)

    dmod_all = _sum_parts(sm("dmod"), "sum_dmod").reshape(8, 3 * D)
    o_b_ada = [o.reshape(1, 3 * D) for o in _adam(dmod_all.reshape(8, 24, 128), b_ada.reshape(24, 128),
                                                   m_b_ada.reshape(24, 128), v_b_ada.reshape(24, 128), "adam_b_ada")]
    dm_mine = lax.dynamic_slice(dmod_all, (0, me * 384), (8, 384))
    o_w_ada = [o[None] for o in _adam_w_ada(c_all.T, dm_mine, w_ada[0], m_w_ada[0], v_w_ada[0])]

    _, (g_in, g_ab, g_pb, g_out) = _split_wait("reduce_chips_wait", c_sems, parts, lands, _chips_plan, o_w_ada[0])
    o_w_in = [o[None] for o in _adam(g_in, w_in[0], m_w_in[0], v_w_in[0], "adam_w_in")]
    o_w_ab = [o[None] for o in _adam(g_ab, w_attn_br[0], m_w_attn_br[0], v_w_attn_br[0], "adam_w_attn_br")]
    o_w_pb = [o[None] for o in _adam(g_pb, w_pool_br[0], m_w_pool_br[0], v_w_pool_br[0], "adam_w_pool_br")]
    o_w_out = [o[None] for o in _adam(g_out, w_out[0], m_w_out[0], v_w_out[0], "adam_w_out")]

    loss = _sum_parts(sm("loss"), "sum_loss")[0, 0]

    per_w = [o_norm_g, o_w_ada, o_b_ada, o_w_in, o_pool_w, o_pool_scale, o_w_ab, o_w_pb, o_w_out, o_rel_bias, o_final_g]
    outs = [loss, grad_x[None]]
    for k in range(4):
        outs += [o[k] for o in per_w]
    return tuple(outs)
```

```python
import functools
import math

import numpy as np
import jax
import jax.numpy as jnp
from jax import lax
from jax.experimental import pallas as pl
from jax.experimental.pallas import tpu as pltpu

F32 = jnp.float32
CDT = jnp.bfloat16
D = 1024
HD = 64
NH = 8
AW = 512
QKV_W = 4608
REST_W = 3584
R16 = 16
EPS = 1e-6
NEG = -1e30
N_BACK = 128
ADAM_LR, ADAM_B1, ADAM_B2, ADAM_EPS, ADAM_WD, ADAM_STEP = 0.001, 0.9, 0.999, 1e-08, 0.01, 10
MESH = pl.DeviceIdType.MESH
VMEM_MB = 1024 * 1024


def _cp(sem=None, vmem=48, **kw):
    if sem is not None:
        kw["dimension_semantics"] = sem
    return pltpu.CompilerParams(vmem_limit_bytes=vmem * VMEM_MB, **kw)


def _sds(shape, dtype):
    return jax.ShapeDtypeStruct(tuple(shape), dtype)


def _rest_blk(jj):
    return jnp.where(jj == 0, 9, jnp.where(jj == 6, 10, jj + 10))


def _sigmoid(z):
    return 1.0 / (1.0 + jnp.exp(-z))


def _dot(a, b, dims):
    return lax.dot_general(a, b, (dims, ((), ())), preferred_element_type=F32)


NN = ((1,), (0,))
NT = ((1,), (1,))
TN = ((0,), (0,))


def _ld(ref):
    if len(ref.shape) == 2:
        return ref[...]
    return jnp.concatenate([ref[a] for a in range(ref.shape[0])], axis=0)


def _st(ref, val, cols=None):
    val = val.astype(ref.dtype)
    if len(ref.shape) == 2:
        if cols is None:
            ref[...] = val
        else:
            ref[:, cols] = val
        return
    rows = ref.shape[1]
    for a in range(ref.shape[0]):
        if cols is None:
            ref[a] = val[a * rows:(a + 1) * rows]
        else:
            ref[a, :, cols] = val[a * rows:(a + 1) * rows]


REST_COLS = ((9, 10), (11, 16), (10, 11))
_HBM = pl.BlockSpec(memory_space=pltpu.HBM)


def _resident(shape):
    return pl.BlockSpec(shape, lambda *ids: (0,) * len(shape), pipeline_mode=pl.Buffered(1))


def _w512(w_ref, j):
    return w_ref[j // 2, :, 512 * (j % 2):512 * (j % 2 + 1)]


def _proj(h, w):
    S = h.shape[0]
    tm = 256

    def body(a_ref, w_ref, q_ref, r_ref):
        a = a_ref[...]
        for j in range(9):
            q_ref[:, 512 * j:512 * (j + 1)] = _dot(a, _w512(w_ref, j), NN).astype(q_ref.dtype)
        off = 0
        for lo, hi in REST_COLS:
            for j in range(lo, hi):
                r_ref[:, off:off + 512] = _dot(a, _w512(w_ref, j), NN)
                off += 512

    return pl.pallas_call(
        body, name="proj", grid=(S // tm,),
        in_specs=[pl.BlockSpec((tm, D), lambda i: (i, 0)), _resident((8, D, 1024))],
        out_specs=[pl.BlockSpec((tm, QKV_W), lambda i: (i, 0)), pl.BlockSpec((tm, REST_W), lambda i: (i, 0))],
        out_shape=[_sds((S, QKV_W), CDT), _sds((S, REST_W), F32)],
        compiler_params=_cp(("parallel",), vmem=56),
    )(h, w)


def _after(dep):
    return ([], []) if dep is None else ([pl.BlockSpec(memory_space=pl.ANY)], [dep])


def _dh(dqkv, drest, w, dep=None):
    S = dqkv.shape[0]
    tm = 256
    dep_specs, dep_args = _after(dep)

    def body(a_ref, b_ref, w_ref, *rest):
        o_ref = rest[-1]
        acc = _dot(a_ref[:, 0:512], _w512(w_ref, 0), NT)
        for j in range(1, 9):
            acc = acc + _dot(a_ref[:, 512 * j:512 * (j + 1)], _w512(w_ref, j), NT)
        off = 0
        for lo, hi in REST_COLS:
            for j in range(lo, hi):
                acc = acc + _dot(b_ref[:, off:off + 512], _w512(w_ref, j), NT)
                off += 512
        o_ref[...] = acc

    return pl.pallas_call(
        body, name="dh", grid=(S // tm,),
        in_specs=[pl.BlockSpec((tm, QKV_W), lambda i: (i, 0)), pl.BlockSpec((tm, REST_W), lambda i: (i, 0)),
                  _resident((8, D, 1024))] + dep_specs,
        out_specs=pl.BlockSpec((tm, D), lambda i: (i, 0)),
        out_shape=_sds((S, D), F32),
        compiler_params=_cp(("parallel",), vmem=56),
    )(dqkv, drest, w, *dep_args)


def _mm_tn(at, b, *, n_total, out_blk, name, prev=None, tn=512):
    R, S = at.shape
    nb = b.shape[1] // tn

    def body(*refs):
        refs[-1][...] = _dot(refs[0][...], refs[1][...], NN)

    in_specs = [_resident((R, S)), pl.BlockSpec((S, tn), lambda j: (0, j))]
    args = [at, b]
    aliases = {}
    if prev is not None:
        in_specs.append(pl.BlockSpec(memory_space=pl.ANY))
        args.append(prev)
        aliases = {2: 0}
    return pl.pallas_call(
        body, name=name, grid=(nb,), in_specs=in_specs,
        out_specs=pl.BlockSpec((R, tn), lambda j: (0, out_blk(j))),
        out_shape=_sds((R, n_total), F32),
        input_output_aliases=aliases,
        compiler_params=_cp(("parallel",)),
    )(*args)


def _tcast(a, name):
    M, C = a.shape
    tm = 512

    def body(a_ref, o_ref):
        o_ref[...] = a_ref[...].astype(F32).T.astype(o_ref.dtype)

    return pl.pallas_call(
        body, name=name, grid=(M // tm,),
        in_specs=[pl.BlockSpec((tm, C), lambda i: (i, 0))],
        out_specs=pl.BlockSpec((C, tm), lambda i: (0, i)),
        out_shape=_sds((C, M), CDT), compiler_params=_cp(("parallel",)),
    )(a)


def _norm_mod(x3, norm_g, shift, scale, dep=None):
    LS = x3.shape[0]
    tl = 16
    dep_specs, dep_args = _after(dep)

    def body(x_ref, g_ref, sh_ref, sc_ref, *rest):
        h_ref, xs_ref = rest[-2], rest[-1]
        gg = g_ref[...] * (1.0 + sc_ref[...])
        for r in range(R16):
            xr = x_ref[:, r, :]
            xs_ref[r] = xr
            xn = xr * lax.rsqrt(jnp.mean(xr * xr, axis=-1, keepdims=True) + EPS)
            h_ref[r] = (xn * gg + sh_ref[...]).astype(h_ref.dtype)

    vec = pl.BlockSpec((1, D), lambda i: (0, 0))
    slab = pl.BlockSpec((R16, tl, D), lambda i: (0, i, 0))
    return pl.pallas_call(
        body, name="norm_mod", grid=(LS // tl,),
        in_specs=[pl.BlockSpec((tl, R16, D), lambda i: (i, 0, 0)), vec, vec, vec] + dep_specs,
        out_specs=[slab, slab],
        out_shape=[_sds((R16, LS, D), CDT), _sds((R16, LS, D), F32)], compiler_params=_cp(("parallel",)),
    )(x3, norm_g, shift, scale, *dep_args)


def _t5_bucket_np(n):
    nf = np.maximum(n, 1).astype(np.float32)
    large = 16 + (np.log(nf / np.float32(16)) / np.float32(math.log(2048 / 16)) * np.float32(16)).astype(np.int32)
    large = np.minimum(large, 31)
    return np.where(n < 16, n, large).astype(np.int32)


class _Geom:
    def __init__(self, gi, LS):
        self.gi, self.LS = gi, LS
        self.dil = (1, 4, 16)[gi]
        if gi == 0:
            self.lead, self.rows, self.n_tiles, self.has_prev = (R16,), 16, LS // 16, True
            a = np.arange(R16)[:, None]
            ll = np.arange(16)[None, :]
            self.pos = (16 * ll + a).reshape(-1)
        elif gi == 1:
            self.lead, self.rows, self.n_tiles, self.has_prev = (4,), 64, LS // 64, True
            a = np.arange(4)[:, None]
            ll = np.arange(64)[None, :]
            self.pos = (4 * ll + a).reshape(-1)
        else:
            self.lead, self.rows, self.n_tiles, self.has_prev = (), LS, 1, False
            self.pos = np.arange(LS)
        self.T = self.pos.shape[0]
        self.Tk = 2 * self.T if self.has_prev else self.T
        self.outer = {0: (), 1: (4,), 2: (R16,)}[gi]

    def view(self, arr):
        S, C = arr.shape
        if self.gi == 1:
            return arr.reshape(4, 4, self.LS, C)
        return arr.reshape(R16, self.LS, C)

    def view_shape(self, C):
        return (4, 4, self.LS, C) if self.gi == 1 else (R16, self.LS, C)

    def unview(self, arr):
        return arr.reshape(R16 * self.LS, arr.shape[-1])

    def spec(self, W, cblk, tile):
        if self.gi == 0:
            return pl.BlockSpec((R16, 16, W), lambda n: (0, tile(n), cblk))
        if self.gi == 1:
            return pl.BlockSpec((4, None, 64, W), lambda r, n: (0, r, tile(n), cblk))
        return pl.BlockSpec((None, self.LS, W), lambda r: (r, 0, cblk))

    def bucket_index(self):
        pq = self.pos[:, None]
        pk = self.pos[None, :]
        if self.has_prev:
            pk = np.concatenate([pk - self.T, pk], axis=1)
        dist = pq - pk
        ok = (dist >= 0) & (dist <= N_BACK)
        bucket = _t5_bucket_np(np.clip(dist, 0, N_BACK) * self.dil)
        return np.where(ok, bucket, -1).astype(np.int32)


def _bias_table(idx, rel_bias, gi, dep=None):
    T, Tk = idx.shape
    tr = 32
    dep_specs, dep_args = _after(dep)

    def body(idx_ref, rb_ref, *rest):
        h = pl.program_id(0)
        ix = idx_ref[...]
        acc = jnp.full(ix.shape, NEG, F32)
        for b in range(32):
            acc = jnp.where(ix == b, rb_ref[b, gi * NH + h], acc)
        rest[-1][...] = acc

    return pl.pallas_call(
        body, name=f"bias_table{gi}", grid=(NH, T // tr),
        in_specs=[pl.BlockSpec((tr, Tk), lambda h, i: (i, 0)), pl.BlockSpec(memory_space=pltpu.SMEM)] + dep_specs,
        out_specs=pl.BlockSpec((None, tr, Tk), lambda h, i: (h, i, 0)),
        out_shape=_sds((NH, T, Tk), F32), compiler_params=_cp(("parallel", "parallel")),
    )(idx, rel_bias, *dep_args)


def _bias_grad(dsum, idx, gi, dep=None):
    _, T, Tk = dsum.shape
    dep_specs, dep_args = _after(dep)

    def body(ds_ref, idx_ref, *rest):
        o_ref, r_ref = rest[-2], rest[-1]

        def step(b, c):
            r_ref[pl.ds(b, 1), :] = jnp.sum(jnp.where(idx_ref[...] == b, ds_ref[...], 0.0), axis=0, keepdims=True)
            return c

        lax.fori_loop(0, 32, step, 0)
        o_ref[...] = jnp.broadcast_to(jnp.sum(r_ref[...], axis=1, keepdims=True), (32, 128))

    out = pl.pallas_call(
        body, name=f"bias_grad{gi}", grid=(NH,),
        in_specs=[pl.BlockSpec((None, T, Tk), lambda h: (h, 0, 0)), pl.BlockSpec((T, Tk), lambda h: (0, 0))] + dep_specs,
        out_specs=pl.BlockSpec((None, 32, 128), lambda h: (h, 0, 0)),
        out_shape=_sds((NH, 32, 128), F32), scratch_shapes=[pltpu.VMEM((32, Tk), F32)],
        compiler_params=_cp(("parallel",)),
    )(dsum, idx, *dep_args)
    return out[:, :, 0].T


def _prev_mask(n, T, Tk):
    col = lax.broadcasted_iota(jnp.int32, (1, Tk), 1)
    return jnp.where((col < T) & (n == 0), NEG, 0.0).astype(F32)


def _attn_fwd(qkv, bias, geom):
    S = qkv.shape[0]
    gi, T, Tk, has_prev = geom.gi, geom.T, geom.Tk, geom.has_prev
    qv = geom.view(qkv)
    cq, ck, cv = 3 * gi, 3 * gi + 1, 3 * gi + 2

    def body(*refs):
        if has_prev:
            q_ref, kp_ref, kc_ref, vp_ref, vc_ref, b_ref, o_ref, l_ref = refs
        else:
            q_ref, kc_ref, vc_ref, b_ref, o_ref, l_ref = refs
        q = _ld(q_ref)
        if has_prev:
            n = pl.program_id(len(geom.outer))
            k = jnp.concatenate([_ld(kp_ref), _ld(kc_ref)], axis=0)
            v = jnp.concatenate([_ld(vp_ref), _ld(vc_ref)], axis=0)
            pm = _prev_mask(n, T, Tk)
        else:
            k, v = _ld(kc_ref), _ld(vc_ref)
        for h in range(NH):
            sl = slice(HD * h, HD * h + HD)
            qs = (q[:, sl].astype(F32) * 0.125).astype(q.dtype)
            s = _dot(qs, k[:, sl], NT) + b_ref[h]
            if has_prev:
                s = s + pm
            m = jnp.max(s, axis=1, keepdims=True)
            p = jnp.exp(s - m)
            l = jnp.sum(p, axis=1, keepdims=True)
            o = _dot(p.astype(v.dtype), v[:, sl], NN) / l
            _st(o_ref, o, sl)
            _st(l_ref, jnp.broadcast_to(m + jnp.log(l), (T, HD)), sl)

    cur = lambda n: n
    prev = lambda n: jnp.maximum(n - 1, 0)
    if has_prev:
        in_specs = [geom.spec(AW, cq, cur), geom.spec(AW, ck, prev), geom.spec(AW, ck, cur),
                    geom.spec(AW, cv, prev), geom.spec(AW, cv, cur)]
        args = [qv] * 5
    else:
        in_specs = [geom.spec(AW, cq, cur), geom.spec(AW, ck, cur), geom.spec(AW, cv, cur)]
        args = [qv] * 3
    nd = len(geom.outer) + (1 if has_prev else 0)
    in_specs.append(pl.BlockSpec((NH, T, Tk), lambda *ids: (0, 0, 0)))
    grid = geom.outer + ((geom.n_tiles,) if has_prev else ())
    oshape = geom.view_shape(AW)
    o, l = pl.pallas_call(
        body, name=f"attn_fwd{gi}", grid=grid, in_specs=in_specs,
        out_specs=[geom.spec(AW, 0, cur), geom.spec(AW, 0, cur)],
        out_shape=[_sds(oshape, F32), _sds(oshape, F32)],
        compiler_params=_cp(("arbitrary",) * nd),
    )(*args, bias)
    return geom.unview(o), geom.unview(l)


def _attn_combine(os_, ls_, rest):
    S = rest.shape[0]
    tm = 512

    def body(o0, o1, o2, l0, l1, l2, z_ref, attn_ref, L_ref, ag_ref, agt_ref):
        la, lb, lc = l0[...], l1[...], l2[...]
        m = jnp.maximum(jnp.maximum(la, lb), lc)
        ea, eb, ec = jnp.exp(la - m), jnp.exp(lb - m), jnp.exp(lc - m)
        den = ea + eb + ec
        attn = (ea * o0[...] + eb * o1[...] + ec * o2[...]) / den
        attn_ref[...] = attn
        L_ref[...] = m + jnp.log(den)
        z = z_ref[...]
        ag = attn * (z * _sigmoid(z))
        ag_ref[...] = ag.astype(ag_ref.dtype)
        agt_ref[...] = ag.T.astype(agt_ref.dtype)

    t = pl.BlockSpec((tm, AW), lambda i: (i, 0))
    return pl.pallas_call(
        body, name="attn_combine", grid=(S // tm,),
        in_specs=[t] * 6 + [pl.BlockSpec((tm, AW), lambda i: (i, 0))],
        out_specs=[t, t, t, pl.BlockSpec((AW, tm), lambda i: (0, i))],
        out_shape=[_sds((S, AW), F32), _sds((S, AW), F32), _sds((S, AW), CDT), _sds((AW, S), CDT)],
        compiler_params=_cp(("parallel",)),
    )(*os_, *ls_, rest)


def _pool_counts(r, LS, win):
    l = lax.broadcasted_iota(jnp.int32, (LS, 1), 0)
    return jnp.minimum(16 * l + (r + 1), win).astype(F32)


def _pool_fwd(rest, pool_w, pool_scale):
    S = rest.shape[0]
    LS = S // R16
    r3 = rest.reshape(R16, LS, REST_W)
    PG = 128

    def body(u_ref, z_ref, pw_ref, ps_ref, pt_ref, mp_ref, pg_ref, pgt_ref, d_ref):
        g = pl.program_id(0)
        win = jnp.left_shift(2, g)
        row = lax.broadcasted_iota(jnp.int32, (LS, PG), 0)
        for r in range(1, R16):
            d_ref[r] = jnp.where(row == 0, 0.0, pltpu.roll(u_ref[r], 1, 0))
        pw = pw_ref[...].astype(CDT)
        for r in range(R16):
            u = u_ref[r]
            acc = u
            for j in range(1, R16):
                coef = jnp.where(j < win, 1.0, 0.0).astype(F32)
                src = u_ref[r - j] if r >= j else d_ref[r - j + R16]
                acc = acc + coef * src
            pooled = acc / _pool_counts(r, LS, win) - u
            pc = pooled.astype(CDT)
            mp = _dot(pc, pw, NN)
            z = z_ref[r]
            pg = (mp * ps_ref[...]) * (z * _sigmoid(z))
            mp_ref[r] = mp
            pg_ref[r] = pg.astype(pg_ref.dtype)
            pt_ref[:, r * LS:(r + 1) * LS] = pooled.T.astype(pt_ref.dtype)
            pgt_ref[:, r * LS:(r + 1) * LS] = pg.T.astype(pgt_ref.dtype)

    col = lambda off: pl.BlockSpec((R16, LS, PG), lambda g: (0, 0, off + g))
    tsp = pl.BlockSpec((PG, S), lambda g: (g, 0))
    pt, mp, pg, pgt = pl.pallas_call(
        body, name="pool_fwd", grid=(4,),
        in_specs=[col(24), col(4), pl.BlockSpec((None, PG, PG), lambda g: (g, 0, 0)),
                  pl.BlockSpec((1, PG), lambda g: (0, g))],
        out_specs=[tsp, col(0), col(0), tsp],
        out_shape=[_sds((AW, S), CDT), _sds((R16, LS, AW), F32), _sds((R16, LS, AW), CDT), _sds((AW, S), CDT)],
        scratch_shapes=[pltpu.VMEM((R16, LS, PG), F32)],
        compiler_params=_cp(("parallel",)),
    )(r3, r3, pool_w, pool_scale)
    return pt, mp.reshape(S, AW), pg.reshape(S, AW), pgt


def _branch_merge(ag, pg, wab, wpb, rest):
    S = ag.shape[0]
    tm, tn = 512, 512

    def body(ag_ref, pg_ref, wa_ref, wp_ref, ga_ref, gp_ref, ya_ref, yp_ref, m_ref, mt_ref):
        ya = _dot(ag_ref[...], wa_ref[...], NN)
        yp = _dot(pg_ref[...], wp_ref[...], NN)
        ya_ref[...] = ya
        yp_ref[...] = yp
        mg = _sigmoid(ga_ref[...]) * ya + _sigmoid(gp_ref[...]) * yp
        m_ref[...] = mg.astype(m_ref.dtype)
        mt_ref[...] = mg.T.astype(mt_ref.dtype)

    a = pl.BlockSpec((tm, AW), lambda i, j: (i, 0))
    w = pl.BlockSpec((AW, tn), lambda i, j: (0, j))
    o = pl.BlockSpec((tm, tn), lambda i, j: (i, j))
    return pl.pallas_call(
        body, name="branch_merge", grid=(S // tm, D // tn),
        in_specs=[a, a, w, w, pl.BlockSpec((tm, tn), lambda i, j: (i, 2 + j)),
                  pl.BlockSpec((tm, tn), lambda i, j: (i, 4 + j))],
        out_specs=[o, o, o, pl.BlockSpec((tn, tm), lambda i, j: (j, i))],
        out_shape=[_sds((S, D), F32), _sds((S, D), F32), _sds((S, D), CDT), _sds((D, S), CDT)],
        compiler_params=_cp(("parallel", "parallel")),
    )(ag, pg, wab, wpb, rest, rest)


def _out_loss(merged, wout, xs, ts, gate, final_g):
    LS = xs.shape[1]
    tl = 16

    def body(m_ref, w_ref, x_ref, t_ref, gate_ref, fg_ref, dx2_ref, dmo_ref, loss_ref, gfg_ref, dgate_ref):
        i = pl.program_id(0)

        @pl.when(i == 0)
        def _():
            loss_ref[...] = jnp.zeros_like(loss_ref)
            gfg_ref[...] = jnp.zeros_like(gfg_ref)
            dgate_ref[...] = jnp.zeros_like(dgate_ref)

        mo = _dot(_ld(m_ref), w_ref[...], NN)
        gate, fg = gate_ref[...], fg_ref[...]
        tgt = jnp.concatenate([t_ref[:, r, :] for r in range(R16)], axis=0)
        x2 = _ld(x_ref) + gate * mo
        r2 = lax.rsqrt(jnp.mean(x2 * x2, axis=-1, keepdims=True) + EPS)
        xn2 = x2 * r2
        diff = xn2 * fg - tgt
        loss = jnp.sum(jnp.sum(diff * diff, axis=-1, keepdims=True), axis=0, keepdims=True)
        dy = diff * (1.0 / D)
        dxn2 = dy * fg
        dx2 = r2 * (dxn2 - xn2 * jnp.mean(dxn2 * xn2, axis=-1, keepdims=True))
        _st(dx2_ref, dx2)
        _st(dmo_ref, dx2 * gate)
        loss_ref[...] += jnp.broadcast_to(loss * (0.5 / D), loss_ref.shape)
        gfg_ref[...] += jnp.sum(dy * xn2, axis=0, keepdims=True)
        dgate_ref[...] += jnp.sum(dx2 * mo, axis=0, keepdims=True)

    slab = pl.BlockSpec((R16, tl, D), lambda i: (0, i, 0))
    vec = pl.BlockSpec((1, D), lambda i: (0, 0))
    return pl.pallas_call(
        body, name="out_loss", grid=(LS // tl,),
        in_specs=[slab, _resident((D, D)), slab, pl.BlockSpec((tl, R16, D), lambda i: (i, 0, 0)), vec, vec],
        out_specs=[slab, slab, pl.BlockSpec((1, 128), lambda i: (0, 0)), vec, vec],
        out_shape=[_sds((R16, LS, D), F32), _sds((R16, LS, D), CDT), _sds((1, 128), F32), _sds((1, D), F32),
                   _sds((1, D), F32)],
        compiler_params=_cp(("arbitrary",)),
    )(merged, wout, xs, ts, gate, final_g)


def _bwd_gates(dmo, wout, wab, wpb, rest, ya, yp, attn, mp, pool_scale):
    S = dmo.shape[0]
    tm = 256

    def body(dmo_ref, wo_ref, wa_ref, wp_ref, za_ref, zp_ref, ga0, ga1, gp0, gp1, ya_ref, yp_ref, at_ref, mp_ref,
             ps_ref, dya_ref, dyp_ref, dr_ref, dat_ref, dl_ref, dmp_ref, dps_ref):
        i = pl.program_id(0)

        @pl.when(i == 0)
        def _():
            dps_ref[...] = jnp.zeros_like(dps_ref)

        dm = _dot(dmo_ref[...], wo_ref[...], NT)
        sa = _sigmoid(jnp.concatenate([ga0[...], ga1[...]], axis=1))
        sp = _sigmoid(jnp.concatenate([gp0[...], gp1[...]], axis=1))
        dya = (dm * sa).astype(CDT)
        dyp = (dm * sp).astype(CDT)
        dya_ref[...] = dya
        dyp_ref[...] = dyp
        dr_ref[:, 1024:2048] = (dm * ya_ref[...] * sa * (1.0 - sa)).astype(dr_ref.dtype)
        dr_ref[:, 2048:3072] = (dm * yp_ref[...] * sp * (1.0 - sp)).astype(dr_ref.dtype)
        da = _dot(dya, wa_ref[...], NT)
        dp = _dot(dyp, wp_ref[...], NT)
        za = za_ref[...]
        sga = _sigmoid(za)
        attn = at_ref[...]
        dattn = da * (za * sga)
        dr_ref[:, 0:512] = (da * attn * (sga * (1.0 + za * (1.0 - sga)))).astype(dr_ref.dtype)
        dat_ref[...] = dattn.astype(dat_ref.dtype)
        prod = dattn * attn
        for h in range(NH):
            sl = slice(HD * h, HD * h + HD)
            dl_ref[:, sl] = jnp.broadcast_to(jnp.sum(prod[:, sl], axis=1, keepdims=True), (tm, HD))
        zp = zp_ref[...]
        sgp = _sigmoid(zp)
        mpre = mp_ref[...]
        ps = ps_ref[...]
        dmixed = dp * (zp * sgp)
        dr_ref[:, 512:1024] = (dp * (mpre * ps) * (sgp * (1.0 + zp * (1.0 - sgp)))).astype(dr_ref.dtype)
        dps_ref[...] += jnp.sum(dmixed * mpre, axis=0, keepdims=True)
        dmp_ref[...] = (dmixed * ps).astype(dmp_ref.dtype)

    full = lambda shape: pl.BlockSpec(shape, lambda i: (0, 0))
    tD = pl.BlockSpec((tm, D), lambda i: (i, 0))
    tA = pl.BlockSpec((tm, AW), lambda i: (i, 0))
    rb = lambda c: pl.BlockSpec((tm, AW), lambda i: (i, c))
    return pl.pallas_call(
        body, name="bwd_gates", grid=(S // tm,),
        in_specs=[tD, full((D, D)), full((AW, D)), full((AW, D)), rb(0), rb(1), rb(2), rb(3), rb(4), rb(5),
                  tD, tD, tA, tA, full((1, AW))],
        out_specs=[tD, tD, pl.BlockSpec((tm, 3072), lambda i: (i, 0)), tA, tA, tA, full((1, AW))],
        out_shape=[_sds((S, D), CDT), _sds((S, D), CDT), _sds((S, REST_W), CDT), _sds((S, AW), CDT),
                   _sds((S, AW), F32), _sds((S, AW), CDT), _sds((1, AW), F32)],
        compiler_params=_cp(("arbitrary",), vmem=56),
    )(dmo, wout, wab, wpb, rest, rest, rest, rest, rest, rest, ya, yp, attn, mp, pool_scale)


def _pool_bwd(dmp, pt, pool_w, drest):
    S = dmp.shape[0]
    LS = S // R16
    PG = 128
    d3 = dmp.reshape(R16, LS, AW)
    dr3 = drest.reshape(R16, LS, REST_W)

    def body(d_ref, pt_ref, pw_ref, dr_in, du_ref, dpw_ref, g_ref, s_ref):
        del dr_in
        g = pl.program_id(0)
        win = jnp.left_shift(2, g)
        row = lax.broadcasted_iota(jnp.int32, (LS, PG), 0)
        pw = pw_ref[...].astype(CDT)
        dpw = jnp.zeros((PG, PG), F32)
        for r in range(R16):
            dm = d_ref[r]
            dpw = dpw + _dot(pt_ref[:, r * LS:(r + 1) * LS], dm, NN)
            dpooled = _dot(dm, pw, NT)
            g_ref[r] = dpooled
            s_ref[r] = dpooled / _pool_counts(r, LS, win)
        dpw_ref[...] = dpw
        for r in range(R16):
            acc = s_ref[r]
            for j in range(1, R16):
                coef = jnp.where(j < win, 1.0, 0.0).astype(F32)
                if r + j < R16:
                    src = s_ref[r + j]
                else:
                    src = jnp.where(row == LS - 1, 0.0, pltpu.roll(s_ref[r + j - R16], LS - 1, 0))
                acc = acc + coef * src
            du_ref[r] = (acc - g_ref[r]).astype(du_ref.dtype)

    du, dpw = pl.pallas_call(
        body, name="pool_bwd", grid=(4,),
        in_specs=[pl.BlockSpec((R16, LS, PG), lambda g: (0, 0, g)), pl.BlockSpec((PG, S), lambda g: (g, 0)),
                  pl.BlockSpec((None, PG, PG), lambda g: (g, 0, 0)), pl.BlockSpec(memory_space=pl.ANY)],
        out_specs=[pl.BlockSpec((R16, LS, PG), lambda g: (0, 0, 24 + g)),
                   pl.BlockSpec((None, PG, PG), lambda g: (g, 0, 0))],
        out_shape=[_sds((R16, LS, REST_W), CDT), _sds((4, PG, PG), F32)],
        scratch_shapes=[pltpu.VMEM((R16, LS, PG), F32), pltpu.VMEM((R16, LS, PG), F32)],
        input_output_aliases={3: 0},
        compiler_params=_cp(("parallel",)),
    )(d3, pt, pool_w, dr3)
    return du.reshape(S, REST_W), dpw


def _attn_bwd(qkv, dattn, Lb, dlb, bias, geom, dqkv_prev):
    S = qkv.shape[0]
    gi, T, Tk, has_prev, N = geom.gi, geom.T, geom.Tk, geom.has_prev, geom.n_tiles
    nd_outer = len(geom.outer)
    qv = geom.view(qkv)
    cq, ck, cv = 3 * gi, 3 * gi + 1, 3 * gi + 2
    W3 = 3 * AW

    def heads(q, k, v, do, L, dl, b_ref, pm, dsum_ref, emit):
        for h in range(NH):
            sl = slice(HD * h, HD * h + HD)
            qs = (q[:, sl].astype(F32) * 0.125).astype(q.dtype)
            kh, vh, doh = k[:, sl], v[:, sl], do[:, sl]
            s = _dot(qs, kh, NT) + b_ref[h]
            if pm is not None:
                s = s + pm
            p = jnp.exp(s - L[:, HD * h:HD * h + 1])
            dp = _dot(doh, vh, NT)
            ds = p * (dp - dl[:, HD * h:HD * h + 1])
            dsum_ref[h] += ds
            dsc = ds.astype(q.dtype)
            dq = _dot(dsc, kh, NN) * 0.125
            dk = _dot(dsc, qs, TN)
            dv = _dot(p.astype(q.dtype), doh, TN)
            emit(sl, dq, dk, dv)

    def first_step():
        ok = pl.program_id(nd_outer) == 0 if has_prev else pl.program_id(0) == 0
        for a in range(nd_outer if has_prev else 0):
            ok = ok & (pl.program_id(a) == 0)
        return ok

    if has_prev:
        def body(q_ref, kp_ref, kc_ref, vp_ref, vc_ref, do_ref, L_ref, dl_ref, b_ref, prev_in, out_ref, dsum_ref,
                 hold, new, pp):
            del prev_in
            s_id = pl.program_id(nd_outer)

            @pl.when(first_step())
            def _():
                dsum_ref[...] = jnp.zeros_like(dsum_ref)

            @pl.when(s_id < N)
            def _():
                q = _ld(q_ref)
                k = jnp.concatenate([_ld(kp_ref), _ld(kc_ref)], axis=0)
                v = jnp.concatenate([_ld(vp_ref), _ld(vc_ref)], axis=0)
                pm = _prev_mask(s_id, T, Tk)

                def emit(sl, dq, dk, dv):
                    new[:, sl] = dq
                    new[:, slice(AW + sl.start, AW + sl.stop)] = dk[T:]
                    new[:, slice(2 * AW + sl.start, 2 * AW + sl.stop)] = dv[T:]
                    pp[:, sl] = dk[:T]
                    pp[:, slice(AW + sl.start, AW + sl.stop)] = dv[:T]

                heads(q, k, v, _ld(do_ref), _ld(L_ref), _ld(dl_ref), b_ref, pm, dsum_ref, emit)

            @pl.when((s_id >= 1) & (s_id < N))
            def _():
                _st(out_ref, hold[:, 0:AW], slice(0, AW))
                _st(out_ref, hold[:, AW:W3] + pp[...], slice(AW, W3))

            @pl.when(s_id == N)
            def _():
                _st(out_ref, hold[...])

            @pl.when(s_id < N)
            def _():
                hold[...] = new[...]

        cur = lambda n: jnp.minimum(n, N - 1)
        prev = lambda n: jnp.clip(n - 1, 0, N - 1)
        delayed = lambda n: jnp.maximum(n - 1, 0)
        in_specs = [geom.spec(AW, cq, cur), geom.spec(AW, ck, prev), geom.spec(AW, ck, cur),
                    geom.spec(AW, cv, prev), geom.spec(AW, cv, cur),
                    geom.spec(AW, 0, cur), geom.spec(AW, 0, cur), geom.spec(AW, 0, cur)]
        args = [qv] * 5 + [geom.view(dattn), geom.view(Lb), geom.view(dlb)]
        out_spec = geom.spec(W3, gi, delayed)
        grid = geom.outer + (N + 1,)
        scratch = [pltpu.VMEM((T, W3), F32), pltpu.VMEM((T, W3), F32), pltpu.VMEM((T, 2 * AW), F32)]
    else:
        def body(q_ref, kc_ref, vc_ref, do_ref, L_ref, dl_ref, b_ref, prev_in, out_ref, dsum_ref):
            del prev_in

            @pl.when(first_step())
            def _():
                dsum_ref[...] = jnp.zeros_like(dsum_ref)

            def emit(sl, dq, dk, dv):
                _st(out_ref, dq, sl)
                _st(out_ref, dk, slice(AW + sl.start, AW + sl.stop))
                _st(out_ref, dv, slice(2 * AW + sl.start, 2 * AW + sl.stop))

            heads(_ld(q_ref), _ld(kc_ref), _ld(vc_ref), _ld(do_ref), _ld(L_ref), _ld(dl_ref), b_ref, None,
                  dsum_ref, emit)

        cur = lambda n: n
        in_specs = [geom.spec(AW, cq, cur), geom.spec(AW, ck, cur), geom.spec(AW, cv, cur),
                    geom.spec(AW, 0, cur), geom.spec(AW, 0, cur), geom.spec(AW, 0, cur)]
        args = [qv] * 3 + [geom.view(dattn), geom.view(Lb), geom.view(dlb)]
        out_spec = geom.spec(W3, gi, cur)
        grid = geom.outer
        scratch = []

    in_specs.append(pl.BlockSpec((NH, T, Tk), lambda *ids: (0, 0, 0)))
    in_specs.append(pl.BlockSpec(memory_space=pl.ANY))
    if dqkv_prev is None:
        last, aliases = bias, {}
    else:
        last, aliases = geom.view(dqkv_prev), {len(in_specs) - 1: 0}
    dq, dsum = pl.pallas_call(
        body, name=f"attn_bwd{gi}", grid=grid, in_specs=in_specs,
        out_specs=[out_spec, pl.BlockSpec((NH, T, Tk), lambda *ids: (0, 0, 0))],
        out_shape=[_sds(geom.view_shape(QKV_W), CDT), _sds((NH, T, Tk), F32)],
        scratch_shapes=scratch,
        input_output_aliases=aliases,
        compiler_params=_cp(("arbitrary",) * len(grid), vmem=56),
    )(*args, bias, last)
    return geom.unview(dq), dsum


def _h_bwd(dh, dx2, xs, norm_g, scale):
    LS = xs.shape[1]
    tl = 16

    def body(dh_ref, dx2_ref, x_ref, g_ref, sc_ref, gx_ref, dsh_ref, dsc_ref, dng_ref):
        i = pl.program_id(0)

        @pl.when(i == 0)
        def _():
            dsh_ref[...] = jnp.zeros_like(dsh_ref)
            dsc_ref[...] = jnp.zeros_like(dsc_ref)
            dng_ref[...] = jnp.zeros_like(dng_ref)

        g = g_ref[...]
        one_sc = 1.0 + sc_ref[...]
        x = _ld(x_ref)
        r1 = lax.rsqrt(jnp.mean(x * x, axis=-1, keepdims=True) + EPS)
        xn = x * r1
        dh = _ld(dh_ref)
        dhx = dh * xn
        dxn = dh * (g * one_sc)
        dx1 = r1 * (dxn - xn * jnp.mean(dxn * xn, axis=-1, keepdims=True))
        gx = _ld(dx2_ref) + dx1
        for r in range(R16):
            gx_ref[:, r, :] = gx[r * tl:(r + 1) * tl]
        dsh_ref[...] += jnp.sum(dh, axis=0, keepdims=True)
        dsc_ref[...] += jnp.sum(dhx * g, axis=0, keepdims=True)
        dng_ref[...] += jnp.sum(dhx * one_sc, axis=0, keepdims=True)

    slab = pl.BlockSpec((R16, tl, D), lambda i: (0, i, 0))
    vec = pl.BlockSpec((1, D), lambda i: (0, 0))
    return pl.pallas_call(
        body, name="h_bwd", grid=(LS // tl,),
        in_specs=[slab, slab, slab, vec, vec],
        out_specs=[pl.BlockSpec((tl, R16, D), lambda i: (i, 0, 0)), vec, vec, vec],
        out_shape=[_sds((LS, R16, D), F32), _sds((1, D), F32), _sds((1, D), F32), _sds((1, D), F32)],
        compiler_params=_cp(("arbitrary",)),
    )(dh, dx2, xs, norm_g, scale)


def _local_step(x, target, shift, scale, gate, norm_g, w_in_full, pool_w, pool_scale, wab, wpb, wout, rel_bias,
                final_g):
    st = _local_pre(x, shift, scale, norm_g, rel_bias)
    st = _local_grads(st, target, gate, w_in_full, pool_w, pool_scale, wab, wpb, wout, final_g)
    d_rel_bias = _local_bias_grads(st)
    gx, dmod, dng = _local_input_grads(st, w_in_full, norm_g, scale)
    return dict(loss=st["loss"], grad_x=gx, dw_in=st["dw_in"], dw_ab=st["dw_ab"], dw_pb=st["dw_pb"],
                dw_out=st["dw_out"], d_norm_g=dng, d_pool_w=st["dpw"], d_pool_scale=st["dps"],
                d_rel_bias=d_rel_bias, d_final_g=st["gfg"], dmod=dmod)


def _local_pre(x, shift, scale, norm_g, rel_bias, dep=None):
    S = x.shape[0]
    LS = S // R16
    geoms = [_Geom(gi, LS) for gi in range(3)]
    idxs = [jnp.asarray(g.bucket_index()) for g in geoms]
    biases = [_bias_table(idxs[gi], rel_bias, gi, dep) for gi in range(3)]
    h3, xs3 = _norm_mod(x.reshape(LS, R16, D), norm_g, shift, scale, dep)
    h = h3.reshape(S, D)
    return dict(geoms=geoms, idxs=idxs, biases=biases, xs3=xs3, h=h, ht=_tcast(h, "h_transpose"))


def _local_grads(st, target, gate, w_in_full, pool_w, pool_scale, wab, wpb, wout, final_g):
    geoms, idxs, biases, xs3, h, ht = (st[k] for k in ("geoms", "idxs", "biases", "xs3", "h", "ht"))
    S = h.shape[0]
    LS = S // R16
    qkv, rest = _proj(h, w_in_full)

    os_, ls_ = [], []
    for gi in range(3):
        o, l = _attn_fwd(qkv, biases[gi], geoms[gi])
        os_.append(o)
        ls_.append(l)
    attn, Lb, ag, agt = _attn_combine(os_, ls_, rest)
    pt, mp, pg, pgt = _pool_fwd(rest, pool_w, pool_scale)
    ya, yp, merged, mt = _branch_merge(ag, pg, wab, wpb, rest)
    dx2_3, dmo3, loss, gfg, dgate = _out_loss(merged.reshape(R16, LS, D), wout, xs3, target.reshape(LS, R16, D),
                                              gate, final_g)
    dmo = dmo3.reshape(S, D)

    dya, dyp, drest, dattn, dlb, dmp, dps = _bwd_gates(dmo, wout, wab, wpb, rest, ya, yp, attn, mp, pool_scale)
    dw_out = _mm_tn(mt, dmo, n_total=D, out_blk=lambda j: j, name="dw_out")
    dw_ab = _mm_tn(agt, dya, n_total=D, out_blk=lambda j: j, name="dw_attn_br")
    dw_pb = _mm_tn(pgt, dyp, n_total=D, out_blk=lambda j: j, name="dw_pool_br")
    drest, dpw = _pool_bwd(dmp, pt, pool_w, drest)

    dqkv = None
    dsums = []
    for gi in range(3):
        dqkv, dsum = _attn_bwd(qkv, dattn, Lb, dlb, biases[gi], geoms[gi], dqkv)
        dsums.append(dsum)

    dw_in = _mm_tn(ht, dqkv, n_total=8192, out_blk=lambda j: j, name="dw_in_qkv")
    dw_in = _mm_tn(ht, drest, n_total=8192, out_blk=_rest_blk, name="dw_in_rest", prev=dw_in)
    return dict(st, loss=loss[0, 0], dw_in=dw_in, dw_ab=dw_ab, dw_pb=dw_pb, dw_out=dw_out, dpw=dpw, dps=dps, gfg=gfg,
                dgate=dgate, dsums=dsums, dqkv=dqkv, drest=drest, dx2_3=dx2_3)


def _local_bias_grads(st, dep=None):
    drb = [_bias_grad(st["dsums"][gi], st["idxs"][gi], gi, dep if gi == 0 else None) for gi in range(3)]
    return jnp.concatenate(drb, axis=1)


def _local_input_grads(st, w_in_full, norm_g, scale, dep=None):
    S = st["h"].shape[0]
    LS = S // R16
    dh = _dh(st["dqkv"], st["drest"], w_in_full, dep)
    gx3, dsh, dsc, dng = _h_bwd(dh.reshape(R16, LS, D), st["dx2_3"], st["xs3"], norm_g, scale)
    return gx3.reshape(S, D), jnp.concatenate([dsh, dsc, st["dgate"]], axis=1), dng


def _my_place():
    return lax.axis_index("x"), lax.axis_index("y"), lax.axis_index("c")


def _block_of(ref, axis, idx, width):
    if axis is None:
        return ref.at[idx]
    start = pl.multiple_of(idx * width, width)
    if axis == 0:
        return ref.at[pl.ds(start, width)]
    return ref.at[:, pl.ds(start, width)]


_SEM = pl.BlockSpec(memory_space=pltpu.SEMAPHORE)
_EFFECT = pltpu.SideEffectType.DATAFLOW_SIDE_EFFECTING


def _split_start(name, srcs, lands, ncopy, plan, dep=None):
    arrs = list(srcs) + list(lands)
    na, ns = len(arrs), len(srcs)
    dep_specs, dep_args = _after(dep)

    def body(*refs):
        send_sems, recv_sems, token = refs[na + len(dep_args)], refs[na + len(dep_args) + 1], refs[-1]
        for k, (s, d, dev) in enumerate(plan(refs[:ns], refs[ns:na])):
            pltpu.make_async_remote_copy(src_ref=s, dst_ref=d, send_sem=send_sems.at[k], recv_sem=recv_sems.at[k],
                                         device_id=dev, device_id_type=MESH).start()
        token[...] = jnp.zeros_like(token)

    outs = pl.pallas_call(
        body, name=name,
        out_shape=(pltpu.SemaphoreType.DMA((ncopy,)), pltpu.SemaphoreType.DMA((ncopy,)),
                   *[pltpu.HBM(a.shape, a.dtype) for a in arrs], _sds((8, 128), F32)),
        in_specs=[_HBM] * na + dep_specs,
        out_specs=(_SEM, _SEM, *[_HBM] * na, pl.BlockSpec(memory_space=pltpu.VMEM)),
        input_output_aliases={i: 2 + i for i in range(na)},
        compiler_params=pltpu.CompilerParams(has_side_effects=_EFFECT),
    )(*[pltpu.with_memory_space_constraint(a, pltpu.HBM) for a in arrs], *dep_args)
    return (outs[0], outs[1]), list(outs[2:2 + ns]), list(outs[2 + ns:2 + na]), outs[-1]


def _split_wait(name, sems, srcs, lands, plan, after):
    arrs = list(srcs) + list(lands)
    na, ns = len(arrs), len(srcs)

    def body(*refs):
        send_sems, recv_sems = refs[na], refs[na + 1]
        for k, (s, d, dev) in enumerate(plan(refs[:ns], refs[ns:na])):
            cp = pltpu.make_async_remote_copy(src_ref=s, dst_ref=d, send_sem=send_sems.at[k], recv_sem=recv_sems.at[k],
                                              device_id=dev, device_id_type=MESH)
            cp.wait_send()
            cp.wait_recv()

    outs = pl.pallas_call(
        body, name=name, out_shape=[pltpu.HBM(a.shape, a.dtype) for a in arrs],
        in_specs=[_HBM] * na + [_SEM, _SEM, pl.BlockSpec(memory_space=pl.ANY)], out_specs=[_HBM] * na,
        input_output_aliases={i: i for i in range(na)},
        compiler_params=pltpu.CompilerParams(has_side_effects=_EFFECT),
    )(*arrs, sems[0], sems[1], after)
    return list(outs[:ns]), list(outs[ns:])


def _gather_chips_plan(layout):
    def plan(src, land):
        x, y, c = _my_place()
        me = 4 * x + 2 * y + c
        return [(src[p], _block_of(land[p], layout[p][0], me, layout[p][1]), (cx, cy, c))
                for p in range(len(src)) for cx, cy in [(1 - x, y), (x, 1 - y), (1 - x, 1 - y)]]
    return plan


def _pair_plan(layout):
    def plan(src, land):
        x, y, c = _my_place()
        return [(_block_of(src[p], layout[p][0], 2 * q + (1 - c), layout[p][1]), land[p].at[q], (x, y, 1 - c))
                for p in range(len(src)) for q in range(4)]
    return plan


def _chips_plan(src, land):
    x, y, c = _my_place()
    return [(src[p].at[2 * cx + cy], land[p].at[2 * x + y], (cx, cy, c))
            for p in range(len(src)) for cx, cy in [(1 - x, y), (x, 1 - y), (1 - x, 1 - y)]]


def _pair_share(gathered):
    n = len(gathered)

    def body(*refs):
        land = refs[n:2 * n]
        send_sems, recv_sems = refs[2 * n:]
        x, y, c = _my_place()
        cps = []
        for p in range(n):
            for k in range(4):
                blk = land[p].at[2 * k + c]
                cps.append(pltpu.make_async_remote_copy(
                    src_ref=blk, dst_ref=blk, send_sem=send_sems.at[4 * p + k], recv_sem=recv_sems.at[4 * p + k],
                    device_id=(x, y, 1 - c), device_id_type=MESH))
        for cp in cps:
            cp.start()
        for cp in cps:
            cp.wait_send()
        for p in range(n):
            for k in range(4):
                blk = land[p].at[2 * k + 1 - c]
                pltpu.make_async_remote_copy(
                    src_ref=blk, dst_ref=blk, send_sem=send_sems.at[4 * p + k], recv_sem=recv_sems.at[4 * p + k],
                    device_id=(x, y, 1 - c), device_id_type=MESH).wait_recv()

    return pl.pallas_call(
        body, name="gather_pair_share", in_specs=[_HBM] * n, out_specs=[_HBM] * n,
        out_shape=[_sds(g.shape, g.dtype) for g in gathered],
        input_output_aliases={p: p for p in range(n)},
        scratch_shapes=[pltpu.SemaphoreType.DMA((4 * n,)), pltpu.SemaphoreType.DMA((4 * n,))],
    )(*gathered)


def _place_slot(arrs, slot, n_slots, name):
    n = len(arrs)

    def body(s_ref, *refs):
        for p in range(n):
            refs[n + p][...] = refs[p][...]

    def spec(a):
        tail = a.shape[-2:]
        return pl.BlockSpec((None,) + tail, lambda i, s_ref: (s_ref[0], 0, 0))

    in_specs = [spec(a) if a.ndim == 3 else pl.BlockSpec(a.shape, lambda i, s_ref: (0, 0)) for a in arrs]
    return pl.pallas_call(
        body, name=name,
        grid_spec=pltpu.PrefetchScalarGridSpec(num_scalar_prefetch=1, grid=(1,), in_specs=in_specs,
                                               out_specs=[spec(a) for a in arrs]),
        out_shape=[_sds((n_slots,) + a.shape[-2:], a.dtype) for a in arrs],
        compiler_params=_cp(("arbitrary",)),
    )(slot, *arrs)


def _cols_from_blocks(w8, name):
    _, R, C = w8.shape

    def body(w_ref, o_ref):
        for e in range(8):
            o_ref[:, C * e:C * (e + 1)] = w_ref[e]

    return pl.pallas_call(body, name=name, out_shape=_sds((R, 8 * C), w8.dtype), compiler_params=_cp())(w8)


def _all_gather(srcs, out_shapes, layout, name):
    n = len(srcs)

    def body(*refs):
        src, out = refs[:n], refs[n:2 * n]
        send_sems, recv_sems, local_sems = refs[2 * n:]
        x, y, c = _my_place()
        me, sibling = (x, y, c), (x, y, 1 - c)
        chips = [(1 - x, y), (x, 1 - y), (1 - x, 1 - y)]

        def region(p, dev):
            px, py, pc = dev
            return _block_of(out[p], layout[p][0], 4 * px + 2 * py + pc, layout[p][1])

        def copy(p, k, block, to, src_ref=None):
            return pltpu.make_async_remote_copy(
                src_ref=region(p, block) if src_ref is None else src_ref, dst_ref=region(p, block),
                send_sem=send_sems.at[p * 7 + k], recv_sem=recv_sems.at[p * 7 + k],
                device_id=to, device_id_type=MESH)

        mine = [pltpu.make_async_copy(src[p], region(p, me), local_sems.at[p]) for p in range(n)]
        for cp in mine:
            cp.start()
        first = []
        for p in range(n):
            first.append(copy(p, 0, me, sibling, src[p]))
            first += [copy(p, 1 + j, me, (*chip, c), src[p]) for j, chip in enumerate(chips)]
        for cp in first:
            cp.start()
        passed = []
        for j, chip in enumerate(chips):
            for p in range(n):
                copy(p, 1 + j, (*chip, c), me).wait_recv()
                fwd = copy(p, 4 + j, (*chip, c), sibling)
                fwd.start()
                passed.append(fwd)
        for p in range(n):
            copy(p, 0, sibling, me).wait_recv()
            for j, chip in enumerate(chips):
                copy(p, 4 + j, (*chip, 1 - c), me).wait_recv()
        for cp in first + passed:
            cp.wait_send()
        for cp in mine:
            cp.wait()

    return pl.pallas_call(
        body, name=name, in_specs=[_HBM] * n, out_specs=[_HBM] * n,
        out_shape=[_sds(s, a.dtype) for s, a in zip(out_shapes, srcs)],
        scratch_shapes=[pltpu.SemaphoreType.DMA((7 * n,)), pltpu.SemaphoreType.DMA((7 * n,)),
                        pltpu.SemaphoreType.DMA((n,))],
    )(*srcs)


def _reduce_pair_exchange(dws, layout, smalls):
    n = len(dws)

    def body(*refs):
        src, sm = refs[:n], refs[n]
        out, sm_out = refs[n + 1:2 * n + 1], refs[2 * n + 1]
        send_sems, recv_sems = refs[2 * n + 2:]
        x, y, c = _my_place()
        sibling = (x, y, 1 - c)
        cps = []
        for p in range(n):
            for q in range(4):
                cps.append(pltpu.make_async_remote_copy(
                    src_ref=_block_of(src[p], layout[p][0], 2 * q + (1 - c), layout[p][1]), dst_ref=out[p].at[q],
                    send_sem=send_sems.at[4 * p + q], recv_sem=recv_sems.at[4 * p + q],
                    device_id=sibling, device_id_type=MESH))
        cps.append(pltpu.make_async_remote_copy(src_ref=sm, dst_ref=sm_out, send_sem=send_sems.at[4 * n],
                                                recv_sem=recv_sems.at[4 * n], device_id=sibling, device_id_type=MESH))
        for cp in cps:
            cp.start()
        for cp in cps:
            cp.wait()

    def shard_shape(a, lay):
        axis, w = lay
        return (4, w, a.shape[1]) if axis == 0 else (4, a.shape[0], w)

    return pl.pallas_call(
        body, name="reduce_pair_exchange", in_specs=[_HBM] * (n + 1), out_specs=[_HBM] * (n + 1),
        out_shape=[_sds(shard_shape(a, l), F32) for a, l in zip(dws, layout)] + [_sds(smalls.shape, F32)],
        scratch_shapes=[pltpu.SemaphoreType.DMA((4 * n + 1,)), pltpu.SemaphoreType.DMA((4 * n + 1,))],
    )(*dws, smalls)


def _pair_add(dw, recv, axis, width, cidx, name):
    if axis == 0:
        tr = min(width, 256)
        C = dw.shape[1]
        grid = (4, width // tr)
        nb = width // tr
        own = pl.BlockSpec((tr, C), lambda q, i, c_ref: ((2 * q + c_ref[0]) * nb + i, 0))
        oth = pl.BlockSpec((None, tr, C), lambda q, i, c_ref: (q, i, 0))
    else:
        R = dw.shape[0]
        tr = min(R, 256)
        grid = (4, R // tr)
        own = pl.BlockSpec((tr, width), lambda q, i, c_ref: (i, 2 * q + c_ref[0]))
        oth = pl.BlockSpec((None, tr, width), lambda q, i, c_ref: (q, i, 0))

    def body(c_ref, a_ref, b_ref, o_ref):
        o_ref[...] = (a_ref[...] + b_ref[...]).astype(o_ref.dtype)

    return pl.pallas_call(
        body, name=name,
        grid_spec=pltpu.PrefetchScalarGridSpec(num_scalar_prefetch=1, grid=grid, in_specs=[own, oth], out_specs=oth),
        out_shape=_sds(recv.shape, CDT), compiler_params=_cp(("parallel", "parallel")),
    )(cidx, dw, recv)


def _add2(a, b, name):
    def body(a_ref, b_ref, o_ref):
        o_ref[...] = a_ref[...] + b_ref[...]

    return pl.pallas_call(body, name=name, out_shape=_sds(a.shape, F32))(a, b)


def _reduce_chip_exchange(parts):
    n = len(parts)

    def body(*refs):
        src, out = refs[:n], refs[n:2 * n]
        send_sems, recv_sems, local_sems = refs[2 * n:]
        x, y, c = _my_place()
        my_slot = 2 * x + y
        chips = [(1 - x, y), (x, 1 - y), (1 - x, 1 - y)]
        cps = []
        for p in range(n):
            keep = pltpu.make_async_copy(src[p].at[my_slot], out[p].at[my_slot], local_sems.at[p])
            keep.start()
            cps.append(keep)
            for k, (cx, cy) in enumerate(chips):
                cps.append(pltpu.make_async_remote_copy(
                    src_ref=src[p].at[2 * cx + cy], dst_ref=out[p].at[my_slot],
                    send_sem=send_sems.at[3 * p + k], recv_sem=recv_sems.at[3 * p + k],
                    device_id=(cx, cy, c), device_id_type=MESH))
                cps[-1].start()
        for cp in cps:
            cp.wait()

    return pl.pallas_call(
        body, name="reduce_chip_exchange", in_specs=[_HBM] * n, out_specs=[_HBM] * n,
        out_shape=[_sds(a.shape, a.dtype) for a in parts],
        scratch_shapes=[pltpu.SemaphoreType.DMA((3 * n,)), pltpu.SemaphoreType.DMA((3 * n,)),
                        pltpu.SemaphoreType.DMA((n,))],
    )(*parts)


def _adaln_part(c_all, w_ada, b_part, dep=None):
    dep_specs, dep_args = _after(dep)
    vmem = pl.BlockSpec(memory_space=pltpu.VMEM)

    def body(c_ref, w_ref, b_ref, *rest):
        rest[-1][...] = _dot(c_ref[...].astype(CDT), w_ref[...].astype(CDT), NN) + b_ref[...]

    return pl.pallas_call(body, name="adaln_part", in_specs=[vmem, vmem, vmem] + dep_specs, out_specs=vmem,
                          out_shape=_sds((8, w_ada.shape[1]), F32),
                          compiler_params=_cp())(c_all, w_ada, b_part, *dep_args)


def _adam_math(w, g, m, v):
    m = ADAM_B1 * m + (1.0 - ADAM_B1) * g
    v = ADAM_B2 * v + (1.0 - ADAM_B2) * (g * g)
    m_hat = m / (1.0 - ADAM_B1 ** ADAM_STEP)
    v_hat = v / (1.0 - ADAM_B2 ** ADAM_STEP)
    delta = -ADAM_LR * (m_hat / (jnp.sqrt(v_hat) + ADAM_EPS) + ADAM_WD * w)
    return delta, m, v


def _adam(gparts, w, m, v, name):
    P, R, C = gparts.shape
    tr = R if R <= 256 else 256

    def body(g_ref, w_ref, m_ref, v_ref, go_ref, d_ref, mo_ref, vo_ref):
        g = g_ref[0].astype(F32)
        for i in range(1, P):
            g = g + g_ref[i].astype(F32)
        d, mn, vn = _adam_math(w_ref[...], g, m_ref[...], v_ref[...])
        go_ref[...] = g
        d_ref[...] = d
        mo_ref[...] = mn
        vo_ref[...] = vn

    t = pl.BlockSpec((tr, C), lambda i: (i, 0))
    return pl.pallas_call(
        body, name=name, grid=(R // tr,),
        in_specs=[pl.BlockSpec((P, tr, C), lambda i: (0, i, 0)), t, t, t], out_specs=[t, t, t, t],
        out_shape=[_sds((R, C), F32)] * 4, compiler_params=_cp(("parallel",)),
    )(gparts, w, m, v)


def _sum_parts(gparts, name):
    P, R, C = gparts.shape

    def body(g_ref, o_ref):
        g = g_ref[0]
        for i in range(1, P):
            g = g + g_ref[i]
        o_ref[...] = g

    return pl.pallas_call(body, name=name, out_shape=_sds((R, C), F32))(gparts)


def _adam_w_ada(c_t, dm, w, m, v):
    R, C = w.shape

    def body(c_ref, dm_ref, w_ref, m_ref, v_ref, go_ref, d_ref, mo_ref, vo_ref):
        g = c_ref[:, 0:1] * dm_ref[0:1, :]
        for b in range(1, 8):
            g = g + c_ref[:, b:b + 1] * dm_ref[b:b + 1, :]
        d, mn, vn = _adam_math(w_ref[...], g, m_ref[...], v_ref[...])
        go_ref[...] = g
        d_ref[...] = d
        mo_ref[...] = mn
        vo_ref[...] = vn

    return pl.pallas_call(body, name="adam_w_ada", out_shape=[_sds((R, C), F32)] * 4,
                          compiler_params=_cp())(c_t, dm, w, m, v)


_SM = dict(norm_g=(0, 8), pool_w=(8, 512), pool_scale=(520, 8), rel_bias=(528, 8), final_g=(536, 8),
           dmod=(544, 192), loss=(736, 8))
_SM_ROWS = 744


def _pack_rows(a, rows):
    flat = a.reshape(-1).astype(F32)
    return jnp.pad(flat, (0, rows * 128 - flat.shape[0])).reshape(rows, 128)


def kernel(x, c, norm_g, w_ada, b_ada, w_in, pool_w, pool_scale, w_attn_br, w_pool_br, w_out, rel_bias, final_g, loss_target, m_norm_g, m_w_ada, m_b_ada, m_w_in, m_pool_w, m_pool_scale, m_w_attn_br, m_w_pool_br, m_w_out, m_rel_bias, m_final_g, v_norm_g, v_w_ada, v_b_ada, v_w_in, v_pool_w, v_pool_scale, v_w_attn_br, v_w_pool_br, v_w_out, v_rel_bias, v_final_g):
    S = x.shape[1]
    px, py, pc = _my_place()
    me = 4 * px + 2 * py + pc
    cidx = jnp.reshape(pc, (1,)).astype(jnp.int32)

    layout = [(1, 1024), (1, 128), (1, 128), (0, 128)]
    final_g2 = final_g.reshape(1, D)
    me1 = jnp.reshape(me, (1,)).astype(jnp.int32)

    (c_rows,) = _all_gather([jnp.broadcast_to(c, (8, D))], [(64, D)], [(0, 8)], "gather_c")
    c_all = c_rows[::8]

    b_part = lax.dynamic_slice(b_ada, (0, me * 384), (1, 384))
    mod_part = _adaln_part(c_all, w_ada[0], b_part)
    (mod_all,) = _all_gather([mod_part], [(8, 8, 384)], [(None, 1)], "gather_mod")
    mod = lax.dynamic_index_in_dim(mod_all, me, axis=1, keepdims=False).reshape(1, 3 * D)
    shift, scale, gate = mod[:, :D], mod[:, D:2 * D], mod[:, 2 * D:]

    shards = [w_in[0].astype(CDT), w_attn_br[0].astype(CDT), w_pool_br[0].astype(CDT), w_out[0].astype(CDT)]
    gplan = _gather_chips_plan([(None, 1)] * 4)
    g_sems, shards, lands, tok = _split_start("gather_chips_start", shards,
                                              _place_slot(shards, me1, 8, "place_own_shards"), 12, gplan, dep=mod_all)
    st = _local_pre(x[0], shift, scale, norm_g, rel_bias, dep=tok)
    _, lands = _split_wait("gather_chips_wait", g_sems, shards, lands, gplan, st["ht"])
    w_in_full, wab8, wpb8, wout8 = _pair_share(lands)
    wab, wpb = _cols_from_blocks(wab8, "w_attn_br_cols"), _cols_from_blocks(wpb8, "w_pool_br_cols")
    wout = wout8.reshape(D, D)

    st = _local_grads(st, loss_target[0], gate, w_in_full, pool_w[0], pool_scale, wab, wpb, wout, final_g2)

    dws = [st["dw_in"], st["dw_ab"], st["dw_pb"], st["dw_out"]]
    pplan = _pair_plan(layout)
    pair_shapes = [(4, w, a.shape[1]) if ax == 0 else (4, a.shape[0], w) for a, (ax, w) in zip(dws, layout)]
    p_sems, dws, recv, tok = _split_start("reduce_pair_start", dws, [lax.empty(s, F32) for s in pair_shapes], 16,
                                          pplan)
    d_rel_bias = _local_bias_grads(st, dep=tok)
    dws, recv = _split_wait("reduce_pair_wait", p_sems, dws, recv, pplan, d_rel_bias)
    parts = [_pair_add(dw, r, ax, w, cidx, f"pair_add{i}") for i, (dw, r, (ax, w)) in enumerate(zip(dws, recv, layout))]

    slot1 = jnp.reshape(2 * px + py, (1,)).astype(jnp.int32)
    c_sems, parts, lands, tok = _split_start("reduce_chips_start", parts,
                                             _place_slot(parts, slot1, 4, "place_own_partials"), 12, _chips_plan)
    grad_x, dmod, d_norm_g = _local_input_grads(st, w_in_full, norm_g, scale, dep=tok)

    dmod_blk = lax.dynamic_update_slice(jnp.zeros((8, 3 * D), F32), dmod, (me, 0))
    smalls = jnp.concatenate([
        _pack_rows(d_norm_g, 8), _pack_rows(st["dpw"], 512), _pack_rows(st["dps"], 8),
        _pack_rows(d_rel_bias, 8), _pack_rows(st["gfg"], 8), _pack_rows(dmod_blk, 192),
        _pack_rows(st["loss"], 8)], axis=0)
    (sm_recv,) = _reduce_pair_exchange([], [], smalls)
    sm_pair = _add2(smalls, sm_recv, "pair_add_smalls")
    (sm4,) = _reduce_chip_exchange([jnp.broadcast_to(sm_pair[None], (4, _SM_ROWS, 128))])

    def sm(name):
        lo, n = _SM[name]
        return sm4[:, lo:lo + n]

    def small_adam(name, w, m, v, nvalid):
        rows = _SM[name][1]
        outs = _adam(sm(name), _pack_rows(w, rows), _pack_rows(m, rows), _pack_rows(v, rows), f"adam_{name}")
        return [o.reshape(-1)[:nvalid].reshape(w.shape) for o in outs]

    o_norm_g = small_adam("norm_g", norm_g, m_norm_g, v_norm_g, D)
    o_pool_w = small_adam("pool_w", pool_w, m_pool_w, v_pool_w, 65536)
    o_pool_scale = small_adam("pool_scale", pool_scale, m_pool_scale, v_pool_scale, AW)
    o_rel_bias = small_adam("rel_bias", rel_bias, m_rel_bias, v_rel_bias, 768)
    o_final_g = small_adam("final_g", final_g, m_final_g, v_final_g, D)

    dmod_all = _sum_parts(sm("dmod"), "sum_dmod").reshape(8, 3 * D)
    o_b_ada = [o.reshape(1, 3 * D) for o in _adam(dmod_all.reshape(8, 24, 128), b_ada.reshape(24, 128),
                                                   m_b_ada.reshape(24, 128), v_b_ada.reshape(24, 128), "adam_b_ada")]
    dm_mine = lax.dynamic_slice(dmod_all, (0, me * 384), (8, 384))
    o_w_ada = [o[None] for o in _adam_w_ada(c_all.T, dm_mine, w_ada[0], m_w_ada[0], v_w_ada[0])]

    _, (g_in, g_ab, g_pb, g_out) = _split_wait("reduce_chips_wait", c_sems, parts, lands, _chips_plan, o_w_ada[0])
    o_w_in = [o[None] for o in _adam(g_in, w_in[0], m_w_in[0], v_w_in[0], "adam_w_in")]
    o_w_ab = [o[None] for o in _adam(g_ab, w_attn_br[0], m_w_attn_br[0], v_w_attn_br[0], "adam_w_attn_br")]
    o_w_pb = [o[None] for o in _adam(g_pb, w_pool_br[0], m_w_pool_br[0], v_w_pool_br[0], "adam_w_pool_br")]
    o_w_out = [o[None] for o in _adam(g_out, w_out[0], m_w_out[0], v_w_out[0], "adam_w_out")]

    loss = _sum_parts(sm("loss"), "sum_loss")[0, 0]

    per_w = [o_norm_g, o_w_ada, o_b_ada, o_w_in, o_pool_w, o_pool_scale, o_w_ab, o_w_pb, o_w_out, o_rel_bias, o_final_g]
    outs = [loss, grad_x[None]]
    for k in range(4):
        outs += [o[k] for o in per_w]
    return tuple(outs)
```

```python
import functools
import math

import numpy as np
import jax
import jax.numpy as jnp
from jax import lax
from jax.experimental import pallas as pl
from jax.experimental.pallas import tpu as pltpu

F32 = jnp.float32
CDT = jnp.bfloat16
D = 1024
HD = 64
NH = 8
AW = 512
QKV_W = 4608
REST_W = 3584
R16 = 16
EPS = 1e-6
NEG = -1e30
N_BACK = 128
ADAM_LR, ADAM_B1, ADAM_B2, ADAM_EPS, ADAM_WD, ADAM_STEP = 0.001, 0.9, 0.999, 1e-08, 0.01, 10
MESH = pl.DeviceIdType.MESH
VMEM_MB = 1024 * 1024


def _cp(sem=None, vmem=48, **kw):
    if sem is not None:
        kw["dimension_semantics"] = sem
    return pltpu.CompilerParams(vmem_limit_bytes=vmem * VMEM_MB, **kw)


def _sds(shape, dtype):
    return jax.ShapeDtypeStruct(tuple(shape), dtype)


def _rest_blk(jj):
    return jnp.where(jj == 0, 9, jnp.where(jj == 6, 10, jj + 10))


def _sigmoid(z):
    return 1.0 / (1.0 + jnp.exp(-z))


def _dot(a, b, dims):
    return lax.dot_general(a, b, (dims, ((), ())), preferred_element_type=F32)


NN = ((1,), (0,))
NT = ((1,), (1,))
TN = ((0,), (0,))


def _ld(ref):
    if len(ref.shape) == 2:
        return ref[...]
    return jnp.concatenate([ref[a] for a in range(ref.shape[0])], axis=0)


def _st(ref, val, cols=None):
    val = val.astype(ref.dtype)
    if len(ref.shape) == 2:
        if cols is None:
            ref[...] = val
        else:
            ref[:, cols] = val
        return
    rows = ref.shape[1]
    for a in range(ref.shape[0]):
        if cols is None:
            ref[a] = val[a * rows:(a + 1) * rows]
        else:
            ref[a, :, cols] = val[a * rows:(a + 1) * rows]


REST_COLS = ((9, 10), (11, 16), (10, 11))
_HBM = pl.BlockSpec(memory_space=pltpu.HBM)


def _resident(shape):
    return pl.BlockSpec(shape, lambda *ids: (0,) * len(shape), pipeline_mode=pl.Buffered(1))


def _w512(w_ref, j):
    return w_ref[j // 2, :, 512 * (j % 2):512 * (j % 2 + 1)]


def _proj(h, w):
    S = h.shape[0]
    tm = 256

    def body(a_ref, w_ref, q_ref, r_ref):
        a = a_ref[...]
        for j in range(9):
            q_ref[:, 512 * j:512 * (j + 1)] = _dot(a, _w512(w_ref, j), NN).astype(q_ref.dtype)
        off = 0
        for lo, hi in REST_COLS:
            for j in range(lo, hi):
                r_ref[:, off:off + 512] = _dot(a, _w512(w_ref, j), NN)
                off += 512

    return pl.pallas_call(
        body, name="proj", grid=(S // tm,),
        in_specs=[pl.BlockSpec((tm, D), lambda i: (i, 0)), _resident((8, D, 1024))],
        out_specs=[pl.BlockSpec((tm, QKV_W), lambda i: (i, 0)), pl.BlockSpec((tm, REST_W), lambda i: (i, 0))],
        out_shape=[_sds((S, QKV_W), CDT), _sds((S, REST_W), F32)],
        compiler_params=_cp(("parallel",), vmem=56),
    )(h, w)


def _after(dep):
    return ([], []) if dep is None else ([pl.BlockSpec(memory_space=pl.ANY)], [dep])


def _dh(dqkv, drest, w, dep=None):
    S = dqkv.shape[0]
    tm = 256
    dep_specs, dep_args = _after(dep)

    def body(a_ref, b_ref, w_ref, *rest):
        o_ref = rest[-1]
        acc = _dot(a_ref[:, 0:512], _w512(w_ref, 0), NT)
        for j in range(1, 9):
            acc = acc + _dot(a_ref[:, 512 * j:512 * (j + 1)], _w512(w_ref, j), NT)
        off = 0
        for lo, hi in REST_COLS:
            for j in range(lo, hi):
                acc = acc + _dot(b_ref[:, off:off + 512], _w512(w_ref, j), NT)
                off += 512
        o_ref[...] = acc

    return pl.pallas_call(
        body, name="dh", grid=(S // tm,),
        in_specs=[pl.BlockSpec((tm, QKV_W), lambda i: (i, 0)), pl.BlockSpec((tm, REST_W), lambda i: (i, 0)),
                  _resident((8, D, 1024))] + dep_specs,
        out_specs=pl.BlockSpec((tm, D), lambda i: (i, 0)),
        out_shape=_sds((S, D), F32),
        compiler_params=_cp(("parallel",), vmem=56),
    )(dqkv, drest, w, *dep_args)


def _mm_tn(at, b, *, n_total, out_blk, name, prev=None, tn=512):
    R, S = at.shape
    nb = b.shape[1] // tn

    def body(*refs):
        refs[-1][...] = _dot(refs[0][...], refs[1][...], NN)

    in_specs = [_resident((R, S)), pl.BlockSpec((S, tn), lambda j: (0, j))]
    args = [at, b]
    aliases = {}
    if prev is not None:
        in_specs.append(pl.BlockSpec(memory_space=pl.ANY))
        args.append(prev)
        aliases = {2: 0}
    return pl.pallas_call(
        body, name=name, grid=(nb,), in_specs=in_specs,
        out_specs=pl.BlockSpec((R, tn), lambda j: (0, out_blk(j))),
        out_shape=_sds((R, n_total), F32),
        input_output_aliases=aliases,
        compiler_params=_cp(("parallel",)),
    )(*args)


def _tcast(a, name):
    M, C = a.shape
    tm = 512

    def body(a_ref, o_ref):
        o_ref[...] = a_ref[...].astype(F32).T.astype(o_ref.dtype)

    return pl.pallas_call(
        body, name=name, grid=(M // tm,),
        in_specs=[pl.BlockSpec((tm, C), lambda i: (i, 0))],
        out_specs=pl.BlockSpec((C, tm), lambda i: (0, i)),
        out_shape=_sds((C, M), CDT), compiler_params=_cp(("parallel",)),
    )(a)


def _norm_mod(x3, norm_g, shift, scale, dep=None):
    LS = x3.shape[0]
    tl = 16
    dep_specs, dep_args = _after(dep)

    def body(x_ref, g_ref, sh_ref, sc_ref, *rest):
        h_ref, xs_ref = rest[-2], rest[-1]
        gg = g_ref[...] * (1.0 + sc_ref[...])
        for r in range(R16):
            xr = x_ref[:, r, :]
            xs_ref[r] = xr
            xn = xr * lax.rsqrt(jnp.mean(xr * xr, axis=-1, keepdims=True) + EPS)
            h_ref[r] = (xn * gg + sh_ref[...]).astype(h_ref.dtype)

    vec = pl.BlockSpec((1, D), lambda i: (0, 0))
    slab = pl.BlockSpec((R16, tl, D), lambda i: (0, i, 0))
    return pl.pallas_call(
        body, name="norm_mod", grid=(LS // tl,),
        in_specs=[pl.BlockSpec((tl, R16, D), lambda i: (i, 0, 0)), vec, vec, vec] + dep_specs,
        out_specs=[slab, slab],
        out_shape=[_sds((R16, LS, D), CDT), _sds((R16, LS, D), F32)], compiler_params=_cp(("parallel",)),
    )(x3, norm_g, shift, scale, *dep_args)


def _t5_bucket_np(n):
    nf = np.maximum(n, 1).astype(np.float32)
    large = 16 + (np.log(nf / np.float32(16)) / np.float32(math.log(2048 / 16)) * np.float32(16)).astype(np.int32)
    large = np.minimum(large, 31)
    return np.where(n < 16, n, large).astype(np.int32)


class _Geom:
    def __init__(self, gi, LS):
        self.gi, self.LS = gi, LS
        self.dil = (1, 4, 16)[gi]
        if gi == 0:
            self.lead, self.rows, self.n_tiles, self.has_prev = (R16,), 16, LS // 16, True
            a = np.arange(R16)[:, None]
            ll = np.arange(16)[None, :]
            self.pos = (16 * ll + a).reshape(-1)
        elif gi == 1:
            self.lead, self.rows, self.n_tiles, self.has_prev = (4,), 64, LS // 64, True
            a = np.arange(4)[:, None]
            ll = np.arange(64)[None, :]
            self.pos = (4 * ll + a).reshape(-1)
        else:
            self.lead, self.rows, self.n_tiles, self.has_prev = (), LS, 1, False
            self.pos = np.arange(LS)
        self.T = self.pos.shape[0]
        self.Tk = 2 * self.T if self.has_prev else self.T
        self.outer = {0: (), 1: (4,), 2: (R16,)}[gi]

    def view(self, arr):
        S, C = arr.shape
        if self.gi == 1:
            return arr.reshape(4, 4, self.LS, C)
        return arr.reshape(R16, self.LS, C)

    def view_shape(self, C):
        return (4, 4, self.LS, C) if self.gi == 1 else (R16, self.LS, C)

    def unview(self, arr):
        return arr.reshape(R16 * self.LS, arr.shape[-1])

    def spec(self, W, cblk, tile):
        if self.gi == 0:
            return pl.BlockSpec((R16, 16, W), lambda n: (0, tile(n), cblk))
        if self.gi == 1:
            return pl.BlockSpec((4, None, 64, W), lambda r, n: (0, r, tile(n), cblk))
        return pl.BlockSpec((None, self.LS, W), lambda r: (r, 0, cblk))

    def bucket_index(self):
        pq = self.pos[:, None]
        pk = self.pos[None, :]
        if self.has_prev:
            pk = np.concatenate([pk - self.T, pk], axis=1)
        dist = pq - pk
        ok = (dist >= 0) & (dist <= N_BACK)
        bucket = _t5_bucket_np(np.clip(dist, 0, N_BACK) * self.dil)
        return np.where(ok, bucket, -1).astype(np.int32)


def _bias_table(idx, rel_bias, gi, dep=None):
    T, Tk = idx.shape
    tr = 32
    dep_specs, dep_args = _after(dep)

    def body(idx_ref, rb_ref, *rest):
        h = pl.program_id(0)
        ix = idx_ref[...]
        acc = jnp.full(ix.shape, NEG, F32)
        for b in range(32):
            acc = jnp.where(ix == b, rb_ref[b, gi * NH + h], acc)
        rest[-1][...] = acc

    return pl.pallas_call(
        body, name=f"bias_table{gi}", grid=(NH, T // tr),
        in_specs=[pl.BlockSpec((tr, Tk), lambda h, i: (i, 0)), pl.BlockSpec(memory_space=pltpu.SMEM)] + dep_specs,
        out_specs=pl.BlockSpec((None, tr, Tk), lambda h, i: (h, i, 0)),
        out_shape=_sds((NH, T, Tk), F32), compiler_params=_cp(("parallel", "parallel")),
    )(idx, rel_bias, *dep_args)


def _bias_grad(dsum, idx, gi, dep=None):
    _, T, Tk = dsum.shape
    dep_specs, dep_args = _after(dep)

    def body(ds_ref, idx_ref, *rest):
        o_ref, r_ref = rest[-2], rest[-1]

        def step(b, c):
            r_ref[pl.ds(b, 1), :] = jnp.sum(jnp.where(idx_ref[...] == b, ds_ref[...], 0.0), axis=0, keepdims=True)
            return c

        lax.fori_loop(0, 32, step, 0)
        o_ref[...] = jnp.broadcast_to(jnp.sum(r_ref[...], axis=1, keepdims=True), (32, 128))

    out = pl.pallas_call(
        body, name=f"bias_grad{gi}", grid=(NH,),
        in_specs=[pl.BlockSpec((None, T, Tk), lambda h: (h, 0, 0)), pl.BlockSpec((T, Tk), lambda h: (0, 0))] + dep_specs,
        out_specs=pl.BlockSpec((None, 32, 128), lambda h: (h, 0, 0)),
        out_shape=_sds((NH, 32, 128), F32), scratch_shapes=[pltpu.VMEM((32, Tk), F32)],
        compiler_params=_cp(("parallel",)),
    )(dsum, idx, *dep_args)
    return out[:, :, 0].T


def _prev_mask(n, T, Tk):
    col = lax.broadcasted_iota(jnp.int32, (1, Tk), 1)
    return jnp.where((col < T) & (n == 0), NEG, 0.0).astype(F32)


def _attn_fwd(qkv, bias, geom):
    S = qkv.shape[0]
    gi, T, Tk, has_prev = geom.gi, geom.T, geom.Tk, geom.has_prev
    qv = geom.view(qkv)
    cq, ck, cv = 3 * gi, 3 * gi + 1, 3 * gi + 2

    def body(*refs):
        if has_prev:
            q_ref, kp_ref, kc_ref, vp_ref, vc_ref, b_ref, o_ref, l_ref = refs
        else:
            q_ref, kc_ref, vc_ref, b_ref, o_ref, l_ref = refs
        q = _ld(q_ref)
        if has_prev:
            n = pl.program_id(len(geom.outer))
            k = jnp.concatenate([_ld(kp_ref), _ld(kc_ref)], axis=0)
            v = jnp.concatenate([_ld(vp_ref), _ld(vc_ref)], axis=0)
            pm = _prev_mask(n, T, Tk)
        else:
            k, v = _ld(kc_ref), _ld(vc_ref)
        for h in range(NH):
            sl = slice(HD * h, HD * h + HD)
            qs = (q[:, sl].astype(F32) * 0.125).astype(q.dtype)
            s = _dot(qs, k[:, sl], NT) + b_ref[h]
            if has_prev:
                s = s + pm
            m = jnp.max(s, axis=1, keepdims=True)
            p = jnp.exp(s - m)
            l = jnp.sum(p, axis=1, keepdims=True)
            o = _dot(p.astype(v.dtype), v[:, sl], NN) / l
            _st(o_ref, o, sl)
            _st(l_ref, jnp.broadcast_to(m + jnp.log(l), (T, HD)), sl)

    cur = lambda n: n
    prev = lambda n: jnp.maximum(n - 1, 0)
    if has_prev:
        in_specs = [geom.spec(AW, cq, cur), geom.spec(AW, ck, prev), geom.spec(AW, ck, cur),
                    geom.spec(AW, cv, prev), geom.spec(AW, cv, cur)]
        args = [qv] * 5
    else:
        in_specs = [geom.spec(AW, cq, cur), geom.spec(AW, ck, cur), geom.spec(AW, cv, cur)]
        args = [qv] * 3
    nd = len(geom.outer) + (1 if has_prev else 0)
    in_specs.append(pl.BlockSpec((NH, T, Tk), lambda *ids: (0, 0, 0)))
    grid = geom.outer + ((geom.n_tiles,) if has_prev else ())
    oshape = geom.view_shape(AW)
    o, l = pl.pallas_call(
        body, name=f"attn_fwd{gi}", grid=grid, in_specs=in_specs,
        out_specs=[geom.spec(AW, 0, cur), geom.spec(AW, 0, cur)],
        out_shape=[_sds(oshape, F32), _sds(oshape, F32)],
        compiler_params=_cp(("arbitrary",) * nd),
    )(*args, bias)
    return geom.unview(o), geom.unview(l)


def _attn_combine(os_, ls_, rest):
    S = rest.shape[0]
    tm = 512

    def body(o0, o1, o2, l0, l1, l2, z_ref, attn_ref, L_ref, ag_ref, agt_ref):
        la, lb, lc = l0[...], l1[...], l2[...]
        m = jnp.maximum(jnp.maximum(la, lb), lc)
        ea, eb, ec = jnp.exp(la - m), jnp.exp(lb - m), jnp.exp(lc - m)
        den = ea + eb + ec
        attn = (ea * o0[...] + eb * o1[...] + ec * o2[...]) / den
        attn_ref[...] = attn
        L_ref[...] = m + jnp.log(den)
        z = z_ref[...]
        ag = attn * (z * _sigmoid(z))
        ag_ref[...] = ag.astype(ag_ref.dtype)
        agt_ref[...] = ag.T.astype(agt_ref.dtype)

    t = pl.BlockSpec((tm, AW), lambda i: (i, 0))
    return pl.pallas_call(
        body, name="attn_combine", grid=(S // tm,),
        in_specs=[t] * 6 + [pl.BlockSpec((tm, AW), lambda i: (i, 0))],
        out_specs=[t, t, t, pl.BlockSpec((AW, tm), lambda i: (0, i))],
        out_shape=[_sds((S, AW), F32), _sds((S, AW), F32), _sds((S, AW), CDT), _sds((AW, S), CDT)],
        compiler_params=_cp(("parallel",)),
    )(*os_, *ls_, rest)


def _pool_counts(r, LS, win):
    l = lax.broadcasted_iota(jnp.int32, (LS, 1), 0)
    return jnp.minimum(16 * l + (r + 1), win).astype(F32)


def _pool_fwd(rest, pool_w, pool_scale):
    S = rest.shape[0]
    LS = S // R16
    r3 = rest.reshape(R16, LS, REST_W)
    PG = 128

    def body(u_ref, z_ref, pw_ref, ps_ref, pt_ref, mp_ref, pg_ref, pgt_ref, d_ref):
        g = pl.program_id(0)
        win = jnp.left_shift(2, g)
        row = lax.broadcasted_iota(jnp.int32, (LS, PG), 0)
        for r in range(1, R16):
            d_ref[r] = jnp.where(row == 0, 0.0, pltpu.roll(u_ref[r], 1, 0))
        pw = pw_ref[...].astype(CDT)
        for r in range(R16):
            u = u_ref[r]
            acc = u
            for j in range(1, R16):
                coef = jnp.where(j < win, 1.0, 0.0).astype(F32)
                src = u_ref[r - j] if r >= j else d_ref[r - j + R16]
                acc = acc + coef * src
            pooled = acc / _pool_counts(r, LS, win) - u
            pc = pooled.astype(CDT)
            mp = _dot(pc, pw, NN)
            z = z_ref[r]
            pg = (mp * ps_ref[...]) * (z * _sigmoid(z))
            mp_ref[r] = mp
            pg_ref[r] = pg.astype(pg_ref.dtype)
            pt_ref[:, r * LS:(r + 1) * LS] = pooled.T.astype(pt_ref.dtype)
            pgt_ref[:, r * LS:(r + 1) * LS] = pg.T.astype(pgt_ref.dtype)

    col = lambda off: pl.BlockSpec((R16, LS, PG), lambda g: (0, 0, off + g))
    tsp = pl.BlockSpec((PG, S), lambda g: (g, 0))
    pt, mp, pg, pgt = pl.pallas_call(
        body, name="pool_fwd", grid=(4,),
        in_specs=[col(24), col(4), pl.BlockSpec((None, PG, PG), lambda g: (g, 0, 0)),
                  pl.BlockSpec((1, PG), lambda g: (0, g))],
        out_specs=[tsp, col(0), col(0), tsp],
        out_shape=[_sds((AW, S), CDT), _sds((R16, LS, AW), F32), _sds((R16, LS, AW), CDT), _sds((AW, S), CDT)],
        scratch_shapes=[pltpu.VMEM((R16, LS, PG), F32)],
        compiler_params=_cp(("parallel",)),
    )(r3, r3, pool_w, pool_scale)
    return pt, mp.reshape(S, AW), pg.reshape(S, AW), pgt


def _branch_merge(ag, pg, wab, wpb, rest):
    S = ag.shape[0]
    tm, tn = 512, 512

    def body(ag_ref, pg_ref, wa_ref, wp_ref, ga_ref, gp_ref, ya_ref, yp_ref, m_ref, mt_ref):
        ya = _dot(ag_ref[...], wa_ref[...], NN)
        yp = _dot(pg_ref[...], wp_ref[...], NN)
        ya_ref[...] = ya
        yp_ref[...] = yp
        mg = _sigmoid(ga_ref[...]) * ya + _sigmoid(gp_ref[...]) * yp
        m_ref[...] = mg.astype(m_ref.dtype)
        mt_ref[...] = mg.T.astype(mt_ref.dtype)

    a = pl.BlockSpec((tm, AW), lambda i, j: (i, 0))
    w = pl.BlockSpec((AW, tn), lambda i, j: (0, j))
    o = pl.BlockSpec((tm, tn), lambda i, j: (i, j))
    return pl.pallas_call(
        body, name="branch_merge", grid=(S // tm, D // tn),
        in_specs=[a, a, w, w, pl.BlockSpec((tm, tn), lambda i, j: (i, 2 + j)),
                  pl.BlockSpec((tm, tn), lambda i, j: (i, 4 + j))],
        out_specs=[o, o, o, pl.BlockSpec((tn, tm), lambda i, j: (j, i))],
        out_shape=[_sds((S, D), F32), _sds((S, D), F32), _sds((S, D), CDT), _sds((D, S), CDT)],
        compiler_params=_cp(("parallel", "parallel")),
    )(ag, pg, wab, wpb, rest, rest)


def _out_loss(merged, wout, xs, ts, gate, final_g):
    LS = xs.shape[1]
    tl = 16

    def body(m_ref, w_ref, x_ref, t_ref, gate_ref, fg_ref, dx2_ref, dmo_ref, loss_ref, gfg_ref, dgate_ref):
        i = pl.program_id(0)

        @pl.when(i == 0)
        def _():
            loss_ref[...] = jnp.zeros_like(loss_ref)
            gfg_ref[...] = jnp.zeros_like(gfg_ref)
            dgate_ref[...] = jnp.zeros_like(dgate_ref)

        mo = _dot(_ld(m_ref), w_ref[...], NN)
        gate, fg = gate_ref[...], fg_ref[...]
        tgt = jnp.concatenate([t_ref[:, r, :] for r in range(R16)], axis=0)
        x2 = _ld(x_ref) + gate * mo
        r2 = lax.rsqrt(jnp.mean(x2 * x2, axis=-1, keepdims=True) + EPS)
        xn2 = x2 * r2
        diff = xn2 * fg - tgt
        loss = jnp.sum(jnp.sum(diff * diff, axis=-1, keepdims=True), axis=0, keepdims=True)
        dy = diff * (1.0 / D)
        dxn2 = dy * fg
        dx2 = r2 * (dxn2 - xn2 * jnp.mean(dxn2 * xn2, axis=-1, keepdims=True))
        _st(dx2_ref, dx2)
        _st(dmo_ref, dx2 * gate)
        loss_ref[...] += jnp.broadcast_to(loss * (0.5 / D), loss_ref.shape)
        gfg_ref[...] += jnp.sum(dy * xn2, axis=0, keepdims=True)
        dgate_ref[...] += jnp.sum(dx2 * mo, axis=0, keepdims=True)

    slab = pl.BlockSpec((R16, tl, D), lambda i: (0, i, 0))
    vec = pl.BlockSpec((1, D), lambda i: (0, 0))
    return pl.pallas_call(
        body, name="out_loss", grid=(LS // tl,),
        in_specs=[slab, _resident((D, D)), slab, pl.BlockSpec((tl, R16, D), lambda i: (i, 0, 0)), vec, vec],
        out_specs=[slab, slab, pl.BlockSpec((1, 128), lambda i: (0, 0)), vec, vec],
        out_shape=[_sds((R16, LS, D), F32), _sds((R16, LS, D), CDT), _sds((1, 128), F32), _sds((1, D), F32),
                   _sds((1, D), F32)],
        compiler_params=_cp(("arbitrary",)),
    )(merged, wout, xs, ts, gate, final_g)


def _bwd_gates(dmo, wout, wab, wpb, rest, ya, yp, attn, mp, pool_scale):
    S = dmo.shape[0]
    tm = 256

    def body(dmo_ref, wo_ref, wa_ref, wp_ref, za_ref, zp_ref, ga0, ga1, gp0, gp1, ya_ref, yp_ref, at_ref, mp_ref,
             ps_ref, dya_ref, dyp_ref, dr_ref, dat_ref, dl_ref, dmp_ref, dps_ref):
        i = pl.program_id(0)

        @pl.when(i == 0)
        def _():
            dps_ref[...] = jnp.zeros_like(dps_ref)

        dm = _dot(dmo_ref[...], wo_ref[...], NT)
        sa = _sigmoid(jnp.concatenate([ga0[...], ga1[...]], axis=1))
        sp = _sigmoid(jnp.concatenate([gp0[...], gp1[...]], axis=1))
        dya = (dm * sa).astype(CDT)
        dyp = (dm * sp).astype(CDT)
        dya_ref[...] = dya
        dyp_ref[...] = dyp
        dr_ref[:, 1024:2048] = (dm * ya_ref[...] * sa * (1.0 - sa)).astype(dr_ref.dtype)
        dr_ref[:, 2048:3072] = (dm * yp_ref[...] * sp * (1.0 - sp)).astype(dr_ref.dtype)
        da = _dot(dya, wa_ref[...], NT)
        dp = _dot(dyp, wp_ref[...], NT)
        za = za_ref[...]
        sga = _sigmoid(za)
        attn = at_ref[...]
        dattn = da * (za * sga)
        dr_ref[:, 0:512] = (da * attn * (sga * (1.0 + za * (1.0 - sga)))).astype(dr_ref.dtype)
        dat_ref[...] = dattn.astype(dat_ref.dtype)
        prod = dattn * attn
        for h in range(NH):
            sl = slice(HD * h, HD * h + HD)
            dl_ref[:, sl] = jnp.broadcast_to(jnp.sum(prod[:, sl], axis=1, keepdims=True), (tm, HD))
        zp = zp_ref[...]
        sgp = _sigmoid(zp)
        mpre = mp_ref[...]
        ps = ps_ref[...]
        dmixed = dp * (zp * sgp)
        dr_ref[:, 512:1024] = (dp * (mpre * ps) * (sgp * (1.0 + zp * (1.0 - sgp)))).astype(dr_ref.dtype)
        dps_ref[...] += jnp.sum(dmixed * mpre, axis=0, keepdims=True)
        dmp_ref[...] = (dmixed * ps).astype(dmp_ref.dtype)

    full = lambda shape: pl.BlockSpec(shape, lambda i: (0, 0))
    tD = pl.BlockSpec((tm, D), lambda i: (i, 0))
    tA = pl.BlockSpec((tm, AW), lambda i: (i, 0))
    rb = lambda c: pl.BlockSpec((tm, AW), lambda i: (i, c))
    return pl.pallas_call(
        body, name="bwd_gates", grid=(S // tm,),
        in_specs=[tD, full((D, D)), full((AW, D)), full((AW, D)), rb(0), rb(1), rb(2), rb(3), rb(4), rb(5),
                  tD, tD, tA, tA, full((1, AW))],
        out_specs=[tD, tD, pl.BlockSpec((tm, 3072), lambda i: (i, 0)), tA, tA, tA, full((1, AW))],
        out_shape=[_sds((S, D), CDT), _sds((S, D), CDT), _sds((S, REST_W), CDT), _sds((S, AW), CDT),
                   _sds((S, AW), F32), _sds((S, AW), CDT), _sds((1, AW), F32)],
        compiler_params=_cp(("arbitrary",), vmem=56),
    )(dmo, wout, wab, wpb, rest, rest, rest, rest, rest, rest, ya, yp, attn, mp, pool_scale)


def _pool_bwd(dmp, pt, pool_w, drest):
    S = dmp.shape[0]
    LS = S // R16
    PG = 128
    d3 = dmp.reshape(R16, LS, AW)
    dr3 = drest.reshape(R16, LS, REST_W)

    def body(d_ref, pt_ref, pw_ref, dr_in, du_ref, dpw_ref, g_ref, s_ref):
        del dr_in
        g = pl.program_id(0)
        win = jnp.left_shift(2, g)
        row = lax.broadcasted_iota(jnp.int32, (LS, PG), 0)
        pw = pw_ref[...].astype(CDT)
        dpw = jnp.zeros((PG, PG), F32)
        for r in range(R16):
            dm = d_ref[r]
            dpw = dpw + _dot(pt_ref[:, r * LS:(r + 1) * LS], dm, NN)
            dpooled = _dot(dm, pw, NT)
            g_ref[r] = dpooled
            s_ref[r] = dpooled / _pool_counts(r, LS, win)
        dpw_ref[...] = dpw
        for r in range(R16):
            acc = s_ref[r]
            for j in range(1, R16):
                coef = jnp.where(j < win, 1.0, 0.0).astype(F32)
                if r + j < R16:
                    src = s_ref[r + j]
                else:
                    src = jnp.where(row == LS - 1, 0.0, pltpu.roll(s_ref[r + j - R16], LS - 1, 0))
                acc = acc + coef * src
            du_ref[r] = (acc - g_ref[r]).astype(du_ref.dtype)

    du, dpw = pl.pallas_call(
        body, name="pool_bwd", grid=(4,),
        in_specs=[pl.BlockSpec((R16, LS, PG), lambda g: (0, 0, g)), pl.BlockSpec((PG, S), lambda g: (g, 0)),
                  pl.BlockSpec((None, PG, PG), lambda g: (g, 0, 0)), pl.BlockSpec(memory_space=pl.ANY)],
        out_specs=[pl.BlockSpec((R16, LS, PG), lambda g: (0, 0, 24 + g)),
                   pl.BlockSpec((None, PG, PG), lambda g: (g, 0, 0))],
        out_shape=[_sds((R16, LS, REST_W), CDT), _sds((4, PG, PG), F32)],
        scratch_shapes=[pltpu.VMEM((R16, LS, PG), F32), pltpu.VMEM((R16, LS, PG), F32)],
        input_output_aliases={3: 0},
        compiler_params=_cp(("parallel",)),
    )(d3, pt, pool_w, dr3)
    return du.reshape(S, REST_W), dpw


def _attn_bwd(qkv, dattn, Lb, dlb, bias, geom, dqkv_prev):
    S = qkv.shape[0]
    gi, T, Tk, has_prev, N = geom.gi, geom.T, geom.Tk, geom.has_prev, geom.n_tiles
    nd_outer = len(geom.outer)
    qv = geom.view(qkv)
    cq, ck, cv = 3 * gi, 3 * gi + 1, 3 * gi + 2
    W3 = 3 * AW

    def heads(q, k, v, do, L, dl, b_ref, pm, dsum_ref, emit):
        for h in range(NH):
            sl = slice(HD * h, HD * h + HD)
            qs = (q[:, sl].astype(F32) * 0.125).astype(q.dtype)
            kh, vh, doh = k[:, sl], v[:, sl], do[:, sl]
            s = _dot(qs, kh, NT) + b_ref[h]
            if pm is not None:
                s = s + pm
            p = jnp.exp(s - L[:, HD * h:HD * h + 1])
            dp = _dot(doh, vh, NT)
            ds = p * (dp - dl[:, HD * h:HD * h + 1])
            dsum_ref[h] += ds
            dsc = ds.astype(q.dtype)
            dq = _dot(dsc, kh, NN) * 0.125
            dk = _dot(dsc, qs, TN)
            dv = _dot(p.astype(q.dtype), doh, TN)
            emit(sl, dq, dk, dv)

    def first_step():
        ok = pl.program_id(nd_outer) == 0 if has_prev else pl.program_id(0) == 0
        for a in range(nd_outer if has_prev else 0):
            ok = ok & (pl.program_id(a) == 0)
        return ok

    if has_prev:
        def body(q_ref, kp_ref, kc_ref, vp_ref, vc_ref, do_ref, L_ref, dl_ref, b_ref, prev_in, out_ref, dsum_ref,
                 hold, new, pp):
            del prev_in
            s_id = pl.program_id(nd_outer)

            @pl.when(first_step())
            def _():
                dsum_ref[...] = jnp.zeros_like(dsum_ref)

            @pl.when(s_id < N)
            def _():
                q = _ld(q_ref)
                k = jnp.concatenate([_ld(kp_ref), _ld(kc_ref)], axis=0)
                v = jnp.concatenate([_ld(vp_ref), _ld(vc_ref)], axis=0)
                pm = _prev_mask(s_id, T, Tk)

                def emit(sl, dq, dk, dv):
                    new[:, sl] = dq
                    new[:, slice(AW + sl.start, AW + sl.stop)] = dk[T:]
                    new[:, slice(2 * AW + sl.start, 2 * AW + sl.stop)] = dv[T:]
                    pp[:, sl] = dk[:T]
                    pp[:, slice(AW + sl.start, AW + sl.stop)] = dv[:T]

                heads(q, k, v, _ld(do_ref), _ld(L_ref), _ld(dl_ref), b_ref, pm, dsum_ref, emit)

            @pl.when((s_id >= 1) & (s_id < N))
            def _():
                _st(out_ref, hold[:, 0:AW], slice(0, AW))
                _st(out_ref, hold[:, AW:W3] + pp[...], slice(AW, W3))

            @pl.when(s_id == N)
            def _():
                _st(out_ref, hold[...])

            @pl.when(s_id < N)
            def _():
                hold[...] = new[...]

        cur = lambda n: jnp.minimum(n, N - 1)
        prev = lambda n: jnp.clip(n - 1, 0, N - 1)
        delayed = lambda n: jnp.maximum(n - 1, 0)
        in_specs = [geom.spec(AW, cq, cur), geom.spec(AW, ck, prev), geom.spec(AW, ck, cur),
                    geom.spec(AW, cv, prev), geom.spec(AW, cv, cur),
                    geom.spec(AW, 0, cur), geom.spec(AW, 0, cur), geom.spec(AW, 0, cur)]
        args = [qv] * 5 + [geom.view(dattn), geom.view(Lb), geom.view(dlb)]
        out_spec = geom.spec(W3, gi, delayed)
        grid = geom.outer + (N + 1,)
        scratch = [pltpu.VMEM((T, W3), F32), pltpu.VMEM((T, W3), F32), pltpu.VMEM((T, 2 * AW), F32)]
    else:
        def body(q_ref, kc_ref, vc_ref, do_ref, L_ref, dl_ref, b_ref, prev_in, out_ref, dsum_ref):
            del prev_in

            @pl.when(first_step())
            def _():
                dsum_ref[...] = jnp.zeros_like(dsum_ref)

            def emit(sl, dq, dk, dv):
                _st(out_ref, dq, sl)
                _st(out_ref, dk, slice(AW + sl.start, AW + sl.stop))
                _st(out_ref, dv, slice(2 * AW + sl.start, 2 * AW + sl.stop))

            heads(_ld(q_ref), _ld(kc_ref), _ld(vc_ref), _ld(do_ref), _ld(L_ref), _ld(dl_ref), b_ref, None,
                  dsum_ref, emit)

        cur = lambda n: n
        in_specs = [geom.spec(AW, cq, cur), geom.spec(AW, ck, cur), geom.spec(AW, cv, cur),
                    geom.spec(AW, 0, cur), geom.spec(AW, 0, cur), geom.spec(AW, 0, cur)]
        args = [qv] * 3 + [geom.view(dattn), geom.view(Lb), geom.view(dlb)]
        out_spec = geom.spec(W3, gi, cur)
        grid = geom.outer
        scratch = []

    in_specs.append(pl.BlockSpec((NH, T, Tk), lambda *ids: (0, 0, 0)))
    in_specs.append(pl.BlockSpec(memory_space=pl.ANY))
    if dqkv_prev is None:
        last, aliases = bias, {}
    else:
        last, aliases = geom.view(dqkv_prev), {len(in_specs) - 1: 0}
    dq, dsum = pl.pallas_call(
        body, name=f"attn_bwd{gi}", grid=grid, in_specs=in_specs,
        out_specs=[out_spec, pl.BlockSpec((NH, T, Tk), lambda *ids: (0, 0, 0))],
        out_shape=[_sds(geom.view_shape(QKV_W), CDT), _sds((NH, T, Tk), F32)],
        scratch_shapes=scratch,
        input_output_aliases=aliases,
        compiler_params=_cp(("arbitrary",) * len(grid), vmem=56),
    )(*args, bias, last)
    return geom.unview(dq), dsum


def _h_bwd(dh, dx2, xs, norm_g, scale):
    LS = xs.shape[1]
    tl = 16

    def body(dh_ref, dx2_ref, x_ref, g_ref, sc_ref, gx_ref, dsh_ref, dsc_ref, dng_ref):
        i = pl.program_id(0)

        @pl.when(i == 0)
        def _():
            dsh_ref[...] = jnp.zeros_like(dsh_ref)
            dsc_ref[...] = jnp.zeros_like(dsc_ref)
            dng_ref[...] = jnp.zeros_like(dng_ref)

        g = g_ref[...]
        one_sc = 1.0 + sc_ref[...]
        x = _ld(x_ref)
        r1 = lax.rsqrt(jnp.mean(x * x, axis=-1, keepdims=True) + EPS)
        xn = x * r1
        dh = _ld(dh_ref)
        dhx = dh * xn
        dxn = dh * (g * one_sc)
        dx1 = r1 * (dxn - xn * jnp.mean(dxn * xn, axis=-1, keepdims=True))
        gx = _ld(dx2_ref) + dx1
        for r in range(R16):
            gx_ref[:, r, :] = gx[r * tl:(r + 1) * tl]
        dsh_ref[...] += jnp.sum(dh, axis=0, keepdims=True)
        dsc_ref[...] += jnp.sum(dhx * g, axis=0, keepdims=True)
        dng_ref[...] += jnp.sum(dhx * one_sc, axis=0, keepdims=True)

    slab = pl.BlockSpec((R16, tl, D), lambda i: (0, i, 0))
    vec = pl.BlockSpec((1, D), lambda i: (0, 0))
    return pl.pallas_call(
        body, name="h_bwd", grid=(LS // tl,),
        in_specs=[slab, slab, slab, vec, vec],
        out_specs=[pl.BlockSpec((tl, R16, D), lambda i: (i, 0, 0)), vec, vec, vec],
        out_shape=[_sds((LS, R16, D), F32), _sds((1, D), F32), _sds((1, D), F32), _sds((1, D), F32)],
        compiler_params=_cp(("arbitrary",)),
    )(dh, dx2, xs, norm_g, scale)


def _local_step(x, target, shift, scale, gate, norm_g, w_in_full, pool_w, pool_scale, wab, wpb, wout, rel_bias,
                final_g):
    st = _local_pre(x, shift, scale, norm_g, rel_bias)
    st = _local_grads(st, target, gate, w_in_full, pool_w, pool_scale, wab, wpb, wout, final_g)
    d_rel_bias = _local_bias_grads(st)
    gx, dmod, dng = _local_input_grads(st, w_in_full, norm_g, scale)
    return dict(loss=st["loss"], grad_x=gx, dw_in=st["dw_in"], dw_ab=st["dw_ab"], dw_pb=st["dw_pb"],
                dw_out=st["dw_out"], d_norm_g=dng, d_pool_w=st["dpw"], d_pool_scale=st["dps"],
                d_rel_bias=d_rel_bias, d_final_g=st["gfg"], dmod=dmod)


def _local_pre(x, shift, scale, norm_g, rel_bias, dep=None):
    S = x.shape[0]
    LS = S // R16
    geoms = [_Geom(gi, LS) for gi in range(3)]
    idxs = [jnp.asarray(g.bucket_index()) for g in geoms]
    biases = [_bias_table(idxs[gi], rel_bias, gi, dep) for gi in range(3)]
    h3, xs3 = _norm_mod(x.reshape(LS, R16, D), norm_g, shift, scale, dep)
    h = h3.reshape(S, D)
    return dict(geoms=geoms, idxs=idxs, biases=biases, xs3=xs3, h=h, ht=_tcast(h, "h_transpose"))


def _local_grads(st, target, gate, w_in_full, pool_w, pool_scale, wab, wpb, wout, final_g):
    geoms, idxs, biases, xs3, h, ht = (st[k] for k in ("geoms", "idxs", "biases", "xs3", "h", "ht"))
    S = h.shape[0]
    LS = S // R16
    qkv, rest = _proj(h, w_in_full)

    os_, ls_ = [], []
    for gi in range(3):
        o, l = _attn_fwd(qkv, biases[gi], geoms[gi])
        os_.append(o)
        ls_.append(l)
    attn, Lb, ag, agt = _attn_combine(os_, ls_, rest)
    pt, mp, pg, pgt = _pool_fwd(rest, pool_w, pool_scale)
    ya, yp, merged, mt = _branch_merge(ag, pg, wab, wpb, rest)
    dx2_3, dmo3, loss, gfg, dgate = _out_loss(merged.reshape(R16, LS, D), wout, xs3, target.reshape(LS, R16, D),
                                              gate, final_g)
    dmo = dmo3.reshape(S, D)

    dya, dyp, drest, dattn, dlb, dmp, dps = _bwd_gates(dmo, wout, wab, wpb, rest, ya, yp, attn, mp, pool_scale)
    dw_out = _mm_tn(mt, dmo, n_total=D, out_blk=lambda j: j, name="dw_out")
    dw_ab = _mm_tn(agt, dya, n_total=D, out_blk=lambda j: j, name="dw_attn_br")
    dw_pb = _mm_tn(pgt, dyp, n_total=D, out_blk=lambda j: j, name="dw_pool_br")
    drest, dpw = _pool_bwd(dmp, pt, pool_w, drest)

    dqkv = None
    dsums = []
    for gi in range(3):
        dqkv, dsum = _attn_bwd(qkv, dattn, Lb, dlb, biases[gi], geoms[gi], dqkv)
        dsums.append(dsum)

    dw_in = _mm_tn(ht, dqkv, n_total=8192, out_blk=lambda j: j, name="dw_in_qkv")
    dw_in = _mm_tn(ht, drest, n_total=8192, out_blk=_rest_blk, name="dw_in_rest", prev=dw_in)
    return dict(st, loss=loss[0, 0], dw_in=dw_in, dw_ab=dw_ab, dw_pb=dw_pb, dw_out=dw_out, dpw=dpw, dps=dps, gfg=gfg,
                dgate=dgate, dsums=dsums, dqkv=dqkv, drest=drest, dx2_3=dx2_3)


def _local_bias_grads(st, dep=None):
    drb = [_bias_grad(st["dsums"][gi], st["idxs"][gi], gi, dep if gi == 0 else None) for gi in range(3)]
    return jnp.concatenate(drb, axis=1)


def _local_input_grads(st, w_in_full, norm_g, scale, dep=None):
    S = st["h"].shape[0]
    LS = S // R16
    dh = _dh(st["dqkv"], st["drest"], w_in_full, dep)
    gx3, dsh, dsc, dng = _h_bwd(dh.reshape(R16, LS, D), st["dx2_3"], st["xs3"], norm_g, scale)
    return gx3.reshape(S, D), jnp.concatenate([dsh, dsc, st["dgate"]], axis=1), dng


def _my_place():
    return lax.axis_index("x"), lax.axis_index("y"), lax.axis_index("c")


def _block_of(ref, axis, idx, width):
    if axis is None:
        return ref.at[idx]
    start = pl.multiple_of(idx * width, width)
    if axis == 0:
        return ref.at[pl.ds(start, width)]
    return ref.at[:, pl.ds(start, width)]


_SEM = pl.BlockSpec(memory_space=pltpu.SEMAPHORE)
_EFFECT = pltpu.SideEffectType.DATAFLOW_SIDE_EFFECTING


def _split_start(name, srcs, lands, ncopy, plan, dep=None):
    arrs = list(srcs) + list(lands)
    na, ns = len(arrs), len(srcs)
    dep_specs, dep_args = _after(dep)

    def body(*refs):
        send_sems, recv_sems, token = refs[na + len(dep_args)], refs[na + len(dep_args) + 1], refs[-1]
        for k, (s, d, dev) in enumerate(plan(refs[:ns], refs[ns:na])):
            pltpu.make_async_remote_copy(src_ref=s, dst_ref=d, send_sem=send_sems.at[k], recv_sem=recv_sems.at[k],
                                         device_id=dev, device_id_type=MESH).start()
        token[...] = jnp.zeros_like(token)

    outs = pl.pallas_call(
        body, name=name,
        out_shape=(pltpu.SemaphoreType.DMA((ncopy,)), pltpu.SemaphoreType.DMA((ncopy,)),
                   *[pltpu.HBM(a.shape, a.dtype) for a in arrs], _sds((8, 128), F32)),
        in_specs=[_HBM] * na + dep_specs,
        out_specs=(_SEM, _SEM, *[_HBM] * na, pl.BlockSpec(memory_space=pltpu.VMEM)),
        input_output_aliases={i: 2 + i for i in range(na)},
        compiler_params=pltpu.CompilerParams(has_side_effects=_EFFECT),
    )(*[pltpu.with_memory_space_constraint(a, pltpu.HBM) for a in arrs], *dep_args)
    return (outs[0], outs[1]), list(outs[2:2 + ns]), list(outs[2 + ns:2 + na]), outs[-1]


def _split_wait(name, sems, srcs, lands, plan, after):
    arrs = list(srcs) + list(lands)
    na, ns = len(arrs), len(srcs)
    after = list(after) if isinstance(after, (list, tuple)) else [after]

    def body(*refs):
        send_sems, recv_sems = refs[na], refs[na + 1]
        for k, (s, d, dev) in enumerate(plan(refs[:ns], refs[ns:na])):
            cp = pltpu.make_async_remote_copy(src_ref=s, dst_ref=d, send_sem=send_sems.at[k], recv_sem=recv_sems.at[k],
                                              device_id=dev, device_id_type=MESH)
            cp.wait_send()
            cp.wait_recv()

    outs = pl.pallas_call(
        body, name=name, out_shape=[pltpu.HBM(a.shape, a.dtype) for a in arrs],
        in_specs=[_HBM] * na + [_SEM, _SEM] + [pl.BlockSpec(memory_space=pl.ANY)] * len(after),
        out_specs=[_HBM] * na,
        input_output_aliases={i: i for i in range(na)},
        compiler_params=pltpu.CompilerParams(has_side_effects=_EFFECT),
    )(*arrs, sems[0], sems[1], *after)
    return list(outs[:ns]), list(outs[ns:])


def _gather_chips_plan(layout):
    def plan(src, land):
        x, y, c = _my_place()
        me = 4 * x + 2 * y + c
        return [(src[p], _block_of(land[p], layout[p][0], me, layout[p][1]), (cx, cy, c))
                for p in range(len(src)) for cx, cy in [(1 - x, y), (x, 1 - y), (1 - x, 1 - y)]]
    return plan


def _pair_plan(layout):
    def plan(src, land):
        x, y, c = _my_place()
        n = len(layout)
        return [(_block_of(src[p], layout[p][0], 2 * q + (1 - c), layout[p][1]), land[p].at[q], (x, y, 1 - c))
                for p in range(n) for q in range(4)] + [(src[n], land[n], (x, y, 1 - c))]
    return plan


def _chips_plan(src, land):
    x, y, c = _my_place()
    n = len(src) - 1
    chips = [(1 - x, y), (x, 1 - y), (1 - x, 1 - y)]
    return [(src[p].at[2 * cx + cy], land[p].at[2 * x + y], (cx, cy, c)) for p in range(n) for cx, cy in chips] + \
           [(src[n], land[n].at[2 * x + y], (cx, cy, c)) for cx, cy in chips]


def _pair_share(gathered):
    n = len(gathered)

    def body(*refs):
        land = refs[n:2 * n]
        send_sems, recv_sems = refs[2 * n:]
        x, y, c = _my_place()
        cps = []
        for p in range(n):
            for k in range(4):
                blk = land[p].at[2 * k + c]
                cps.append(pltpu.make_async_remote_copy(
                    src_ref=blk, dst_ref=blk, send_sem=send_sems.at[4 * p + k], recv_sem=recv_sems.at[4 * p + k],
                    device_id=(x, y, 1 - c), device_id_type=MESH))
        for cp in cps:
            cp.start()
        for cp in cps:
            cp.wait_send()
        for p in range(n):
            for k in range(4):
                blk = land[p].at[2 * k + 1 - c]
                pltpu.make_async_remote_copy(
                    src_ref=blk, dst_ref=blk, send_sem=send_sems.at[4 * p + k], recv_sem=recv_sems.at[4 * p + k],
                    device_id=(x, y, 1 - c), device_id_type=MESH).wait_recv()

    return pl.pallas_call(
        body, name="gather_pair_share", in_specs=[_HBM] * n, out_specs=[_HBM] * n,
        out_shape=[_sds(g.shape, g.dtype) for g in gathered],
        input_output_aliases={p: p for p in range(n)},
        scratch_shapes=[pltpu.SemaphoreType.DMA((4 * n,)), pltpu.SemaphoreType.DMA((4 * n,))],
    )(*gathered)


def _place_slot(arrs, slot, n_slots, name):
    n = len(arrs)

    def body(s_ref, *refs):
        for p in range(n):
            refs[n + p][...] = refs[p][...]

    def spec(a):
        tail = a.shape[-2:]
        return pl.BlockSpec((None,) + tail, lambda i, s_ref: (s_ref[0], 0, 0))

    in_specs = [spec(a) if a.ndim == 3 else pl.BlockSpec(a.shape, lambda i, s_ref: (0, 0)) for a in arrs]
    return pl.pallas_call(
        body, name=name,
        grid_spec=pltpu.PrefetchScalarGridSpec(num_scalar_prefetch=1, grid=(1,), in_specs=in_specs,
                                               out_specs=[spec(a) for a in arrs]),
        out_shape=[_sds((n_slots,) + a.shape[-2:], a.dtype) for a in arrs],
        compiler_params=_cp(("arbitrary",)),
    )(slot, *arrs)


def _cols_from_blocks(w8, name):
    _, R, C = w8.shape

    def body(w_ref, o_ref):
        for e in range(8):
            o_ref[:, C * e:C * (e + 1)] = w_ref[e]

    return pl.pallas_call(body, name=name, out_shape=_sds((R, 8 * C), w8.dtype), compiler_params=_cp())(w8)


def _all_gather(srcs, out_shapes, layout, name):
    n = len(srcs)

    def body(*refs):
        src, out = refs[:n], refs[n:2 * n]
        send_sems, recv_sems, local_sems = refs[2 * n:]
        x, y, c = _my_place()
        me, sibling = (x, y, c), (x, y, 1 - c)
        chips = [(1 - x, y), (x, 1 - y), (1 - x, 1 - y)]

        def region(p, dev):
            px, py, pc = dev
            return _block_of(out[p], layout[p][0], 4 * px + 2 * py + pc, layout[p][1])

        def copy(p, k, block, to, src_ref=None):
            return pltpu.make_async_remote_copy(
                src_ref=region(p, block) if src_ref is None else src_ref, dst_ref=region(p, block),
                send_sem=send_sems.at[p * 7 + k], recv_sem=recv_sems.at[p * 7 + k],
                device_id=to, device_id_type=MESH)

        mine = [pltpu.make_async_copy(src[p], region(p, me), local_sems.at[p]) for p in range(n)]
        for cp in mine:
            cp.start()
        first = []
        for p in range(n):
            first.append(copy(p, 0, me, sibling, src[p]))
            first += [copy(p, 1 + j, me, (*chip, c), src[p]) for j, chip in enumerate(chips)]
        for cp in first:
            cp.start()
        passed = []
        for j, chip in enumerate(chips):
            for p in range(n):
                copy(p, 1 + j, (*chip, c), me).wait_recv()
                fwd = copy(p, 4 + j, (*chip, c), sibling)
                fwd.start()
                passed.append(fwd)
        for p in range(n):
            copy(p, 0, sibling, me).wait_recv()
            for j, chip in enumerate(chips):
                copy(p, 4 + j, (*chip, 1 - c), me).wait_recv()
        for cp in first + passed:
            cp.wait_send()
        for cp in mine:
            cp.wait()

    return pl.pallas_call(
        body, name=name, in_specs=[_HBM] * n, out_specs=[_HBM] * n,
        out_shape=[_sds(s, a.dtype) for s, a in zip(out_shapes, srcs)],
        scratch_shapes=[pltpu.SemaphoreType.DMA((7 * n,)), pltpu.SemaphoreType.DMA((7 * n,)),
                        pltpu.SemaphoreType.DMA((n,))],
    )(*srcs)


def _reduce_pair_exchange(dws, layout, smalls):
    n = len(dws)

    def body(*refs):
        src, sm = refs[:n], refs[n]
        out, sm_out = refs[n + 1:2 * n + 1], refs[2 * n + 1]
        send_sems, recv_sems = refs[2 * n + 2:]
        x, y, c = _my_place()
        sibling = (x, y, 1 - c)
        cps = []
        for p in range(n):
            for q in range(4):
                cps.append(pltpu.make_async_remote_copy(
                    src_ref=_block_of(src[p], layout[p][0], 2 * q + (1 - c), layout[p][1]), dst_ref=out[p].at[q],
                    send_sem=send_sems.at[4 * p + q], recv_sem=recv_sems.at[4 * p + q],
                    device_id=sibling, device_id_type=MESH))
        cps.append(pltpu.make_async_remote_copy(src_ref=sm, dst_ref=sm_out, send_sem=send_sems.at[4 * n],
                                                recv_sem=recv_sems.at[4 * n], device_id=sibling, device_id_type=MESH))
        for cp in cps:
            cp.start()
        for cp in cps:
            cp.wait()

    def shard_shape(a, lay):
        axis, w = lay
        return (4, w, a.shape[1]) if axis == 0 else (4, a.shape[0], w)

    return pl.pallas_call(
        body, name="reduce_pair_exchange", in_specs=[_HBM] * (n + 1), out_specs=[_HBM] * (n + 1),
        out_shape=[_sds(shard_shape(a, l), F32) for a, l in zip(dws, layout)] + [_sds(smalls.shape, F32)],
        scratch_shapes=[pltpu.SemaphoreType.DMA((4 * n + 1,)), pltpu.SemaphoreType.DMA((4 * n + 1,))],
    )(*dws, smalls)


def _pair_add(dw, recv, axis, width, cidx, name):
    if axis == 0:
        tr = min(width, 256)
        C = dw.shape[1]
        grid = (4, width // tr)
        nb = width // tr
        own = pl.BlockSpec((tr, C), lambda q, i, c_ref: ((2 * q + c_ref[0]) * nb + i, 0))
        oth = pl.BlockSpec((None, tr, C), lambda q, i, c_ref: (q, i, 0))
    else:
        R = dw.shape[0]
        tr = min(R, 256)
        grid = (4, R // tr)
        own = pl.BlockSpec((tr, width), lambda q, i, c_ref: (i, 2 * q + c_ref[0]))
        oth = pl.BlockSpec((None, tr, width), lambda q, i, c_ref: (q, i, 0))

    def body(c_ref, a_ref, b_ref, o_ref):
        o_ref[...] = (a_ref[...] + b_ref[...]).astype(o_ref.dtype)

    return pl.pallas_call(
        body, name=name,
        grid_spec=pltpu.PrefetchScalarGridSpec(num_scalar_prefetch=1, grid=grid, in_specs=[own, oth], out_specs=oth),
        out_shape=_sds(recv.shape, CDT), compiler_params=_cp(("parallel", "parallel")),
    )(cidx, dw, recv)


def _add2(a, b, name):
    def body(a_ref, b_ref, o_ref):
        o_ref[...] = a_ref[...] + b_ref[...]

    return pl.pallas_call(body, name=name, out_shape=_sds(a.shape, F32))(a, b)


def _reduce_chip_exchange(parts):
    n = len(parts)

    def body(*refs):
        src, out = refs[:n], refs[n:2 * n]
        send_sems, recv_sems, local_sems = refs[2 * n:]
        x, y, c = _my_place()
        my_slot = 2 * x + y
        chips = [(1 - x, y), (x, 1 - y), (1 - x, 1 - y)]
        cps = []
        for p in range(n):
            keep = pltpu.make_async_copy(src[p].at[my_slot], out[p].at[my_slot], local_sems.at[p])
            keep.start()
            cps.append(keep)
            for k, (cx, cy) in enumerate(chips):
                cps.append(pltpu.make_async_remote_copy(
                    src_ref=src[p].at[2 * cx + cy], dst_ref=out[p].at[my_slot],
                    send_sem=send_sems.at[3 * p + k], recv_sem=recv_sems.at[3 * p + k],
                    device_id=(cx, cy, c), device_id_type=MESH))
                cps[-1].start()
        for cp in cps:
            cp.wait()

    return pl.pallas_call(
        body, name="reduce_chip_exchange", in_specs=[_HBM] * n, out_specs=[_HBM] * n,
        out_shape=[_sds(a.shape, a.dtype) for a in parts],
        scratch_shapes=[pltpu.SemaphoreType.DMA((3 * n,)), pltpu.SemaphoreType.DMA((3 * n,)),
                        pltpu.SemaphoreType.DMA((n,))],
    )(*parts)


def _adaln_part(c_all, w_ada, b_part, dep=None):
    dep_specs, dep_args = _after(dep)
    vmem = pl.BlockSpec(memory_space=pltpu.VMEM)

    def body(c_ref, w_ref, b_ref, *rest):
        rest[-1][...] = _dot(c_ref[...].astype(CDT), w_ref[...].astype(CDT), NN) + b_ref[...]

    return pl.pallas_call(body, name="adaln_part", in_specs=[vmem, vmem, vmem] + dep_specs, out_specs=vmem,
                          out_shape=_sds((8, w_ada.shape[1]), F32),
                          compiler_params=_cp())(c_all, w_ada, b_part, *dep_args)


def _adam_math(w, g, m, v):
    m = ADAM_B1 * m + (1.0 - ADAM_B1) * g
    v = ADAM_B2 * v + (1.0 - ADAM_B2) * (g * g)
    m_hat = m / (1.0 - ADAM_B1 ** ADAM_STEP)
    v_hat = v / (1.0 - ADAM_B2 ** ADAM_STEP)
    delta = -ADAM_LR * (m_hat / (jnp.sqrt(v_hat) + ADAM_EPS) + ADAM_WD * w)
    return delta, m, v


def _adam(gparts, w, m, v, name):
    P, R, C = gparts.shape
    tr = R if R <= 256 else 256

    def body(g_ref, w_ref, m_ref, v_ref, go_ref, d_ref, mo_ref, vo_ref):
        g = g_ref[0].astype(F32)
        for i in range(1, P):
            g = g + g_ref[i].astype(F32)
        d, mn, vn = _adam_math(w_ref[...], g, m_ref[...], v_ref[...])
        go_ref[...] = g
        d_ref[...] = d
        mo_ref[...] = mn
        vo_ref[...] = vn

    t = pl.BlockSpec((tr, C), lambda i: (i, 0))
    return pl.pallas_call(
        body, name=name, grid=(R // tr,),
        in_specs=[pl.BlockSpec((P, tr, C), lambda i: (0, i, 0)), t, t, t], out_specs=[t, t, t, t],
        out_shape=[_sds((R, C), F32)] * 4, compiler_params=_cp(("parallel",)),
    )(gparts, w, m, v)


def _sum_parts(gparts, name):
    P, R, C = gparts.shape

    def body(g_ref, o_ref):
        g = g_ref[0]
        for i in range(1, P):
            g = g + g_ref[i]
        o_ref[...] = g

    return pl.pallas_call(body, name=name, out_shape=_sds((R, C), F32))(gparts)


def _adam_w_ada(c_t, dm, w, m, v):
    R, C = w.shape

    def body(c_ref, dm_ref, w_ref, m_ref, v_ref, go_ref, d_ref, mo_ref, vo_ref):
        g = c_ref[:, 0:1] * dm_ref[0:1, :]
        for b in range(1, 8):
            g = g + c_ref[:, b:b + 1] * dm_ref[b:b + 1, :]
        d, mn, vn = _adam_math(w_ref[...], g, m_ref[...], v_ref[...])
        go_ref[...] = g
        d_ref[...] = d
        mo_ref[...] = mn
        vo_ref[...] = vn

    return pl.pallas_call(body, name="adam_w_ada", out_shape=[_sds((R, C), F32)] * 4,
                          compiler_params=_cp())(c_t, dm, w, m, v)


_SM = dict(norm_g=(0, 8), pool_w=(8, 512), pool_scale=(520, 8), rel_bias=(528, 8), final_g=(536, 8),
           dmod=(544, 192), loss=(736, 8))
_SM_ROWS = 744


def _pack_rows(a, rows):
    flat = a.reshape(-1).astype(F32)
    return jnp.pad(flat, (0, rows * 128 - flat.shape[0])).reshape(rows, 128)


def kernel(x, c, norm_g, w_ada, b_ada, w_in, pool_w, pool_scale, w_attn_br, w_pool_br, w_out, rel_bias, final_g, loss_target, m_norm_g, m_w_ada, m_b_ada, m_w_in, m_pool_w, m_pool_scale, m_w_attn_br, m_w_pool_br, m_w_out, m_rel_bias, m_final_g, v_norm_g, v_w_ada, v_b_ada, v_w_in, v_pool_w, v_pool_scale, v_w_attn_br, v_w_pool_br, v_w_out, v_rel_bias, v_final_g):
    S = x.shape[1]
    px, py, pc = _my_place()
    me = 4 * px + 2 * py + pc
    cidx = jnp.reshape(pc, (1,)).astype(jnp.int32)

    layout = [(1, 1024), (1, 128), (1, 128), (0, 128)]
    final_g2 = final_g.reshape(1, D)
    me1 = jnp.reshape(me, (1,)).astype(jnp.int32)

    (c_rows,) = _all_gather([jnp.broadcast_to(c, (8, D))], [(64, D)], [(0, 8)], "gather_c")
    c_all = c_rows[::8]

    b_part = lax.dynamic_slice(b_ada, (0, me * 384), (1, 384))
    mod_part = _adaln_part(c_all, w_ada[0], b_part)
    (mod_all,) = _all_gather([mod_part], [(8, 8, 384)], [(None, 1)], "gather_mod")
    mod = lax.dynamic_index_in_dim(mod_all, me, axis=1, keepdims=False).reshape(1, 3 * D)
    shift, scale, gate = mod[:, :D], mod[:, D:2 * D], mod[:, 2 * D:]

    shards = [w_in[0].astype(CDT), w_attn_br[0].astype(CDT), w_pool_br[0].astype(CDT), w_out[0].astype(CDT)]
    gplan = _gather_chips_plan([(None, 1)] * 4)
    g_sems, shards, lands, tok = _split_start("gather_chips_start", shards,
                                              _place_slot(shards, me1, 8, "place_own_shards"), 12, gplan, dep=mod_all)
    st = _local_pre(x[0], shift, scale, norm_g, rel_bias, dep=tok)
    _, lands = _split_wait("gather_chips_wait", g_sems, shards, lands, gplan, [st["ht"]] + st["biases"])
    w_in_full, wab8, wpb8, wout8 = _pair_share(lands)
    wab, wpb = _cols_from_blocks(wab8, "w_attn_br_cols"), _cols_from_blocks(wpb8, "w_pool_br_cols")
    wout = wout8.reshape(D, D)

    st = _local_grads(st, loss_target[0], gate, w_in_full, pool_w[0], pool_scale, wab, wpb, wout, final_g2)

    early = jnp.concatenate([_pack_rows(st["dpw"], 512), _pack_rows(st["dps"], 8), _pack_rows(st["gfg"], 8),
                             _pack_rows(st["loss"], 8)], axis=0)
    dws = [st["dw_in"], st["dw_ab"], st["dw_pb"], st["dw_out"]]
    pplan = _pair_plan(layout)
    pair_shapes = [(4, w, a.shape[1]) if ax == 0 else (4, a.shape[0], w) for a, (ax, w) in zip(dws, layout)]
    p_sems, srcs, recv, tok = _split_start("reduce_pair_start", dws + [early],
                                           [lax.empty(s, F32) for s in pair_shapes + [early.shape]], 17, pplan)
    d_rel_bias = _local_bias_grads(st, dep=tok)
    srcs, recv = _split_wait("reduce_pair_wait", p_sems, srcs, recv, pplan, d_rel_bias)
    parts = [_pair_add(dw, r, ax, w, cidx, f"pair_add{i}")
             for i, (dw, r, (ax, w)) in enumerate(zip(srcs[:4], recv[:4], layout))]
    parts.append(_add2(srcs[4], recv[4], "pair_add_smalls"))

    slot1 = jnp.reshape(2 * px + py, (1,)).astype(jnp.int32)
    c_sems, parts, lands, tok = _split_start("reduce_chips_start", parts,
                                             _place_slot(parts, slot1, 4, "place_own_partials"), 15, _chips_plan)
    grad_x, dmod, d_norm_g = _local_input_grads(st, w_in_full, norm_g, scale, dep=tok)

    late = jnp.concatenate([_pack_rows(dmod, 24), _pack_rows(d_norm_g, 8), _pack_rows(d_rel_bias, 8)], axis=0)
    (late_all,) = _all_gather([late], [(8, 40, 128)], [(None, 1)], "gather_late")

    def small_adam(gparts, w, m, v, nvalid, name):
        rows = gparts.shape[1]
        outs = _adam(gparts, _pack_rows(w, rows), _pack_rows(m, rows), _pack_rows(v, rows), f"adam_{name}")
        return [o.reshape(-1)[:nvalid].reshape(w.shape) for o in outs]

    o_norm_g = small_adam(late_all[:, 24:32], norm_g, m_norm_g, v_norm_g, D, "norm_g")
    o_rel_bias = small_adam(late_all[:, 32:40], rel_bias, m_rel_bias, v_rel_bias, 768, "rel_bias")
    dmod_all = late_all[:, 0:24].reshape(8, 3 * D)
    o_b_ada = [o.reshape(1, 3 * D) for o in _adam(dmod_all.reshape(8, 24, 128), b_ada.reshape(24, 128),
                                                   m_b_ada.reshape(24, 128), v_b_ada.reshape(24, 128), "adam_b_ada")]
    dm_mine = lax.dynamic_slice(dmod_all, (0, me * 384), (8, 384))
    o_w_ada = [o[None] for o in _adam_w_ada(c_all.T, dm_mine, w_ada[0], m_w_ada[0], v_w_ada[0])]

    _, (g_in, g_ab, g_pb, g_out, sm4) = _split_wait("reduce_chips_wait", c_sems, parts, lands, _chips_plan,
                                                    [o_w_ada[0], o_b_ada[0], o_norm_g[0], o_rel_bias[0]])
    o_w_in = [o[None] for o in _adam(g_in, w_in[0], m_w_in[0], v_w_in[0], "adam_w_in")]
    o_w_ab = [o[None] for o in _adam(g_ab, w_attn_br[0], m_w_attn_br[0], v_w_attn_br[0], "adam_w_attn_br")]
    o_w_pb = [o[None] for o in _adam(g_pb, w_pool_br[0], m_w_pool_br[0], v_w_pool_br[0], "adam_w_pool_br")]
    o_w_out = [o[None] for o in _adam(g_out, w_out[0], m_w_out[0], v_w_out[0], "adam_w_out")]
    o_pool_w = small_adam(sm4[:, 0:512], pool_w, m_pool_w, v_pool_w, 65536, "pool_w")
    o_pool_scale = small_adam(sm4[:, 512:520], pool_scale, m_pool_scale, v_pool_scale, AW, "pool_scale")
    o_final_g = small_adam(sm4[:, 520:528], final_g, m_final_g, v_final_g, D, "final_g")
    loss = _sum_parts(sm4[:, 528:536], "sum_loss")[0, 0]

    per_w = [o_norm_g, o_w_ada, o_b_ada, o_w_in, o_pool_w, o_pool_scale, o_w_ab, o_w_pb, o_w_out, o_rel_bias, o_final_g]
    outs = [loss, grad_x[None]]
    for k in range(4):
        outs += [o[k] for o in per_w]
    return tuple(outs)
```

```python
import functools
import math

import numpy as np
import jax
import jax.numpy as jnp
from jax import lax
from jax.experimental import pallas as pl
from jax.experimental.pallas import tpu as pltpu

F32 = jnp.float32
CDT = jnp.bfloat16
D = 1024
HD = 64
NH = 8
AW = 512
QKV_W = 4608
REST_W = 3584
R16 = 16
EPS = 1e-6
NEG = -1e30
N_BACK = 128
ADAM_LR, ADAM_B1, ADAM_B2, ADAM_EPS, ADAM_WD, ADAM_STEP = 0.001, 0.9, 0.999, 1e-08, 0.01, 10
MESH = pl.DeviceIdType.MESH
VMEM_MB = 1024 * 1024


def _cp(sem=None, vmem=48, **kw):
    if sem is not None:
        kw["dimension_semantics"] = sem
    return pltpu.CompilerParams(vmem_limit_bytes=vmem * VMEM_MB, **kw)


def _sds(shape, dtype):
    return jax.ShapeDtypeStruct(tuple(shape), dtype)


def _rest_blk(jj):
    return jnp.where(jj == 0, 9, jnp.where(jj == 6, 10, jj + 10))


def _sigmoid(z):
    return 1.0 / (1.0 + jnp.exp(-z))


def _dot(a, b, dims):
    return lax.dot_general(a, b, (dims, ((), ())), preferred_element_type=F32)


NN = ((1,), (0,))
NT = ((1,), (1,))
TN = ((0,), (0,))


def _ld(ref):
    if len(ref.shape) == 2:
        return ref[...]
    return jnp.concatenate([ref[a] for a in range(ref.shape[0])], axis=0)


def _st(ref, val, cols=None):
    val = val.astype(ref.dtype)
    if len(ref.shape) == 2:
        if cols is None:
            ref[...] = val
        else:
            ref[:, cols] = val
        return
    rows = ref.shape[1]
    for a in range(ref.shape[0]):
        if cols is None:
            ref[a] = val[a * rows:(a + 1) * rows]
        else:
            ref[a, :, cols] = val[a * rows:(a + 1) * rows]


REST_COLS = ((9, 10), (11, 16), (10, 11))
_HBM = pl.BlockSpec(memory_space=pltpu.HBM)


def _resident(shape):
    return pl.BlockSpec(shape, lambda *ids: (0,) * len(shape), pipeline_mode=pl.Buffered(1))


def _w512(w_ref, j):
    return w_ref[j // 2, :, 512 * (j % 2):512 * (j % 2 + 1)]


def _proj(h, w):
    S = h.shape[0]
    tm = 256

    def body(a_ref, w_ref, q_ref, r_ref):
        a = a_ref[...]
        for j in range(9):
            q_ref[:, 512 * j:512 * (j + 1)] = _dot(a, _w512(w_ref, j), NN).astype(q_ref.dtype)
        off = 0
        for lo, hi in REST_COLS:
            for j in range(lo, hi):
                r_ref[:, off:off + 512] = _dot(a, _w512(w_ref, j), NN)
                off += 512

    return pl.pallas_call(
        body, name="proj", grid=(S // tm,),
        in_specs=[pl.BlockSpec((tm, D), lambda i: (i, 0)), _resident((8, D, 1024))],
        out_specs=[pl.BlockSpec((tm, QKV_W), lambda i: (i, 0)), pl.BlockSpec((tm, REST_W), lambda i: (i, 0))],
        out_shape=[_sds((S, QKV_W), CDT), _sds((S, REST_W), F32)],
        compiler_params=_cp(("parallel",), vmem=56),
    )(h, w)


def _after(dep):
    return ([], []) if dep is None else ([pl.BlockSpec(memory_space=pl.ANY)], [dep])


def _dh(dqkv, drest, w, dep=None):
    S = dqkv.shape[0]
    tm = 256
    dep_specs, dep_args = _after(dep)

    def body(a_ref, b_ref, w_ref, *rest):
        o_ref = rest[-1]
        acc = _dot(a_ref[:, 0:512], _w512(w_ref, 0), NT)
        for j in range(1, 9):
            acc = acc + _dot(a_ref[:, 512 * j:512 * (j + 1)], _w512(w_ref, j), NT)
        off = 0
        for lo, hi in REST_COLS:
            for j in range(lo, hi):
                acc = acc + _dot(b_ref[:, off:off + 512], _w512(w_ref, j), NT)
                off += 512
        o_ref[...] = acc

    return pl.pallas_call(
        body, name="dh", grid=(S // tm,),
        in_specs=[pl.BlockSpec((tm, QKV_W), lambda i: (i, 0)), pl.BlockSpec((tm, REST_W), lambda i: (i, 0)),
                  _resident((8, D, 1024))] + dep_specs,
        out_specs=pl.BlockSpec((tm, D), lambda i: (i, 0)),
        out_shape=_sds((S, D), F32),
        compiler_params=_cp(("parallel",), vmem=56),
    )(dqkv, drest, w, *dep_args)


def _mm_tn(at, b, *, n_total, out_blk, name, prev=None, tn=512):
    R, S = at.shape
    nb = b.shape[1] // tn

    def body(*refs):
        refs[-1][...] = _dot(refs[0][...], refs[1][...], NN)

    in_specs = [_resident((R, S)), pl.BlockSpec((S, tn), lambda j: (0, j))]
    args = [at, b]
    aliases = {}
    if prev is not None:
        in_specs.append(pl.BlockSpec(memory_space=pl.ANY))
        args.append(prev)
        aliases = {2: 0}
    return pl.pallas_call(
        body, name=name, grid=(nb,), in_specs=in_specs,
        out_specs=pl.BlockSpec((R, tn), lambda j: (0, out_blk(j))),
        out_shape=_sds((R, n_total), F32),
        input_output_aliases=aliases,
        compiler_params=_cp(("parallel",)),
    )(*args)


def _tcast(a, name):
    M, C = a.shape
    tm = 512

    def body(a_ref, o_ref):
        o_ref[...] = a_ref[...].astype(F32).T.astype(o_ref.dtype)

    return pl.pallas_call(
        body, name=name, grid=(M // tm,),
        in_specs=[pl.BlockSpec((tm, C), lambda i: (i, 0))],
        out_specs=pl.BlockSpec((C, tm), lambda i: (0, i)),
        out_shape=_sds((C, M), CDT), compiler_params=_cp(("parallel",)),
    )(a)


def _norm_mod(x3, norm_g, shift, scale, dep=None):
    LS = x3.shape[0]
    tl = 16
    dep_specs, dep_args = _after(dep)

    def body(x_ref, g_ref, sh_ref, sc_ref, *rest):
        h_ref, xs_ref = rest[-2], rest[-1]
        gg = g_ref[...] * (1.0 + sc_ref[...])
        for r in range(R16):
            xr = x_ref[:, r, :]
            xs_ref[r] = xr
            xn = xr * lax.rsqrt(jnp.mean(xr * xr, axis=-1, keepdims=True) + EPS)
            h_ref[r] = (xn * gg + sh_ref[...]).astype(h_ref.dtype)

    vec = pl.BlockSpec((1, D), lambda i: (0, 0))
    slab = pl.BlockSpec((R16, tl, D), lambda i: (0, i, 0))
    return pl.pallas_call(
        body, name="norm_mod", grid=(LS // tl,),
        in_specs=[pl.BlockSpec((tl, R16, D), lambda i: (i, 0, 0)), vec, vec, vec] + dep_specs,
        out_specs=[slab, slab],
        out_shape=[_sds((R16, LS, D), CDT), _sds((R16, LS, D), F32)], compiler_params=_cp(("parallel",)),
    )(x3, norm_g, shift, scale, *dep_args)


def _t5_bucket_np(n):
    nf = np.maximum(n, 1).astype(np.float32)
    large = 16 + (np.log(nf / np.float32(16)) / np.float32(math.log(2048 / 16)) * np.float32(16)).astype(np.int32)
    large = np.minimum(large, 31)
    return np.where(n < 16, n, large).astype(np.int32)


class _Geom:
    def __init__(self, gi, LS):
        self.gi, self.LS = gi, LS
        self.dil = (1, 4, 16)[gi]
        if gi == 0:
            self.lead, self.rows, self.n_tiles, self.has_prev = (R16,), 16, LS // 16, True
            a = np.arange(R16)[:, None]
            ll = np.arange(16)[None, :]
            self.pos = (16 * ll + a).reshape(-1)
        elif gi == 1:
            self.lead, self.rows, self.n_tiles, self.has_prev = (4,), 64, LS // 64, True
            a = np.arange(4)[:, None]
            ll = np.arange(64)[None, :]
            self.pos = (4 * ll + a).reshape(-1)
        else:
            self.lead, self.rows, self.n_tiles, self.has_prev = (), LS, 1, False
            self.pos = np.arange(LS)
        self.T = self.pos.shape[0]
        self.Tk = 2 * self.T if self.has_prev else self.T
        self.outer = {0: (), 1: (4,), 2: (R16,)}[gi]

    def view(self, arr):
        S, C = arr.shape
        if self.gi == 1:
            return arr.reshape(4, 4, self.LS, C)
        return arr.reshape(R16, self.LS, C)

    def view_shape(self, C):
        return (4, 4, self.LS, C) if self.gi == 1 else (R16, self.LS, C)

    def unview(self, arr):
        return arr.reshape(R16 * self.LS, arr.shape[-1])

    def spec(self, W, cblk, tile):
        if self.gi == 0:
            return pl.BlockSpec((R16, 16, W), lambda n: (0, tile(n), cblk))
        if self.gi == 1:
            return pl.BlockSpec((4, None, 64, W), lambda r, n: (0, r, tile(n), cblk))
        return pl.BlockSpec((None, self.LS, W), lambda r: (r, 0, cblk))

    def halves(self):
        R = self.rows
        idx = np.arange(self.T).reshape(-1, R)
        return idx[:, :R // 2].reshape(-1), idx[:, R // 2:].reshape(-1)

    def bucket_index(self, sub):
        lo, hi = self.halves()
        pos = self.pos
        if sub == 0:
            pq = pos[lo]
            pk = np.concatenate([pos[hi] - self.T, pos[lo]]) if self.has_prev else pos[lo]
        else:
            pq = pos[hi]
            pk = np.concatenate([pos[lo], pos[hi]])
        dist = pq[:, None] - pk[None, :]
        ok = (dist >= 0) & (dist <= N_BACK)
        bucket = _t5_bucket_np(np.clip(dist, 0, N_BACK) * self.dil)
        return np.where(ok, bucket, -1).astype(np.int32)


def _bias_table(idx, rel_bias, gi, sub, dep=None):
    T, Tk = idx.shape
    tr = 32
    dep_specs, dep_args = _after(dep)

    def body(idx_ref, rb_ref, *rest):
        h = pl.program_id(0)
        ix = idx_ref[...]
        acc = jnp.full(ix.shape, NEG, F32)
        for b in range(32):
            acc = jnp.where(ix == b, rb_ref[b, gi * NH + h], acc)
        rest[-1][...] = acc

    return pl.pallas_call(
        body, name=f"bias_table{gi}{'ab'[sub]}", grid=(NH, T // tr),
        in_specs=[pl.BlockSpec((tr, Tk), lambda h, i: (i, 0)), pl.BlockSpec(memory_space=pltpu.SMEM)] + dep_specs,
        out_specs=pl.BlockSpec((None, tr, Tk), lambda h, i: (h, i, 0)),
        out_shape=_sds((NH, T, Tk), F32), compiler_params=_cp(("parallel", "parallel")),
    )(idx, rel_bias, *dep_args)


def _bias_grad(dsum, idx, gi, sub, dep=None):
    _, T, Tk = dsum.shape
    dep_specs, dep_args = _after(dep)

    def body(ds_ref, idx_ref, *rest):
        o_ref, r_ref = rest[-2], rest[-1]

        def step(b, c):
            r_ref[pl.ds(b, 1), :] = jnp.sum(jnp.where(idx_ref[...] == b, ds_ref[...], 0.0), axis=0, keepdims=True)
            return c

        lax.fori_loop(0, 32, step, 0)
        o_ref[...] = jnp.broadcast_to(jnp.sum(r_ref[...], axis=1, keepdims=True), (32, 128))

    out = pl.pallas_call(
        body, name=f"bias_grad{gi}{'ab'[sub]}", grid=(NH,),
        in_specs=[pl.BlockSpec((None, T, Tk), lambda h: (h, 0, 0)), pl.BlockSpec((T, Tk), lambda h: (0, 0))] + dep_specs,
        out_specs=pl.BlockSpec((None, 32, 128), lambda h: (h, 0, 0)),
        out_shape=_sds((NH, 32, 128), F32), scratch_shapes=[pltpu.VMEM((32, Tk), F32)],
        compiler_params=_cp(("parallel",)),
    )(dsum, idx, *dep_args)
    return out[:, :, 0].T


def _prev_mask(n, T, Tk):
    col = lax.broadcasted_iota(jnp.int32, (1, Tk), 1)
    return jnp.where((col < T) & (n == 0), NEG, 0.0).astype(F32)


def _split_rows(v, R):
    A, half = v.shape[0] // R, R // 2
    via_f32 = v.dtype != F32 and half % 16 != 0
    w = v.astype(F32) if via_f32 else v
    lo = jnp.concatenate([w[a * R:a * R + half] for a in range(A)], axis=0)
    hi = jnp.concatenate([w[a * R + half:(a + 1) * R] for a in range(A)], axis=0)
    return (lo.astype(v.dtype), hi.astype(v.dtype)) if via_f32 else (lo, hi)


def _merge_rows(lo, hi, R):
    half = R // 2
    A = lo.shape[0] // half
    return jnp.concatenate([x[a * half:(a + 1) * half] for a in range(A) for x in (lo, hi)], axis=0)


def _attn_fwd(qkv, biases, geom):
    S = qkv.shape[0]
    gi, T, R, has_prev = geom.gi, geom.T, geom.rows, geom.has_prev
    H2 = T // 2
    qv = geom.view(qkv)
    cq, ck, cv = 3 * gi, 3 * gi + 1, 3 * gi + 2

    def body(*refs):
        if has_prev:
            q_ref, kp_ref, kc_ref, vp_ref, vc_ref, ba_ref, bb_ref, o_ref, l_ref = refs
        else:
            q_ref, kc_ref, vc_ref, ba_ref, bb_ref, o_ref, l_ref = refs
        q_lo, q_hi = _split_rows(_ld(q_ref), R)
        kc_lo, kc_hi = _split_rows(_ld(kc_ref), R)
        vc_lo, vc_hi = _split_rows(_ld(vc_ref), R)
        k_b = jnp.concatenate([kc_lo, kc_hi], axis=0)
        v_b = jnp.concatenate([vc_lo, vc_hi], axis=0)
        if has_prev:
            n = pl.program_id(len(geom.outer))
            k_a = jnp.concatenate([_split_rows(_ld(kp_ref), R)[1], kc_lo], axis=0)
            v_a = jnp.concatenate([_split_rows(_ld(vp_ref), R)[1], vc_lo], axis=0)
            pm = _prev_mask(n, H2, T)
        else:
            k_a, v_a, pm = kc_lo, vc_lo, None
        for h in range(NH):
            sl = slice(HD * h, HD * h + HD)
            halves = []
            for q, k, v, b_ref, mask in ((q_lo, k_a, v_a, ba_ref, pm), (q_hi, k_b, v_b, bb_ref, None)):
                qs = (q[:, sl].astype(F32) * 0.125).astype(q.dtype)
                s = _dot(qs, k[:, sl], NT) + b_ref[h]
                if mask is not None:
                    s = s + mask
                m = jnp.max(s, axis=1, keepdims=True)
                p = jnp.exp(s - m)
                l = jnp.sum(p, axis=1, keepdims=True)
                halves.append((_dot(p.astype(v.dtype), v[:, sl], NN) / l, m + jnp.log(l)))
            _st(o_ref, _merge_rows(halves[0][0], halves[1][0], R), sl)
            _st(l_ref, jnp.broadcast_to(_merge_rows(halves[0][1], halves[1][1], R), (T, HD)), sl)

    cur = lambda n: n
    prev = lambda n: jnp.maximum(n - 1, 0)
    if has_prev:
        in_specs = [geom.spec(AW, cq, cur), geom.spec(AW, ck, prev), geom.spec(AW, ck, cur),
                    geom.spec(AW, cv, prev), geom.spec(AW, cv, cur)]
        args = [qv] * 5
    else:
        in_specs = [geom.spec(AW, cq, cur), geom.spec(AW, ck, cur), geom.spec(AW, cv, cur)]
        args = [qv] * 3
    nd = len(geom.outer) + (1 if has_prev else 0)
    in_specs += [pl.BlockSpec(b.shape, lambda *ids: (0, 0, 0)) for b in biases]
    grid = geom.outer + ((geom.n_tiles,) if has_prev else ())
    oshape = geom.view_shape(AW)
    o, l = pl.pallas_call(
        body, name=f"attn_fwd{gi}", grid=grid, in_specs=in_specs,
        out_specs=[geom.spec(AW, 0, cur), geom.spec(AW, 0, cur)],
        out_shape=[_sds(oshape, F32), _sds(oshape, F32)],
        compiler_params=_cp(("arbitrary",) * nd),
    )(*args, *biases)
    return geom.unview(o), geom.unview(l)


def _attn_combine(os_, ls_, rest):
    S = rest.shape[0]
    tm = 512

    def body(o0, o1, o2, l0, l1, l2, z_ref, attn_ref, L_ref, ag_ref, agt_ref):
        la, lb, lc = l0[...], l1[...], l2[...]
        m = jnp.maximum(jnp.maximum(la, lb), lc)
        ea, eb, ec = jnp.exp(la - m), jnp.exp(lb - m), jnp.exp(lc - m)
        den = ea + eb + ec
        attn = (ea * o0[...] + eb * o1[...] + ec * o2[...]) / den
        attn_ref[...] = attn
        L_ref[...] = m + jnp.log(den)
        z = z_ref[...]
        ag = attn * (z * _sigmoid(z))
        ag_ref[...] = ag.astype(ag_ref.dtype)
        agt_ref[...] = ag.T.astype(agt_ref.dtype)

    t = pl.BlockSpec((tm, AW), lambda i: (i, 0))
    return pl.pallas_call(
        body, name="attn_combine", grid=(S // tm,),
        in_specs=[t] * 6 + [pl.BlockSpec((tm, AW), lambda i: (i, 0))],
        out_specs=[t, t, t, pl.BlockSpec((AW, tm), lambda i: (0, i))],
        out_shape=[_sds((S, AW), F32), _sds((S, AW), F32), _sds((S, AW), CDT), _sds((AW, S), CDT)],
        compiler_params=_cp(("parallel",)),
    )(*os_, *ls_, rest)


def _pool_counts(r, LS, win):
    l = lax.broadcasted_iota(jnp.int32, (LS, 1), 0)
    return jnp.minimum(16 * l + (r + 1), win).astype(F32)


def _pool_fwd(rest, pool_w, pool_scale):
    S = rest.shape[0]
    LS = S // R16
    r3 = rest.reshape(R16, LS, REST_W)
    PG = 128

    def body(u_ref, z_ref, pw_ref, ps_ref, pt_ref, mp_ref, pg_ref, pgt_ref, d_ref):
        g = pl.program_id(0)
        win = jnp.left_shift(2, g)
        row = lax.broadcasted_iota(jnp.int32, (LS, PG), 0)
        for r in range(1, R16):
            d_ref[r] = jnp.where(row == 0, 0.0, pltpu.roll(u_ref[r], 1, 0))
        pw = pw_ref[...].astype(CDT)
        for r in range(R16):
            u = u_ref[r]
            acc = u
            for j in range(1, R16):
                coef = jnp.where(j < win, 1.0, 0.0).astype(F32)
                src = u_ref[r - j] if r >= j else d_ref[r - j + R16]
                acc = acc + coef * src
            pooled = acc / _pool_counts(r, LS, win) - u
            pc = pooled.astype(CDT)
            mp = _dot(pc, pw, NN)
            z = z_ref[r]
            pg = (mp * ps_ref[...]) * (z * _sigmoid(z))
            mp_ref[r] = mp
            pg_ref[r] = pg.astype(pg_ref.dtype)
            pt_ref[:, r * LS:(r + 1) * LS] = pooled.T.astype(pt_ref.dtype)
            pgt_ref[:, r * LS:(r + 1) * LS] = pg.T.astype(pgt_ref.dtype)

    col = lambda off: pl.BlockSpec((R16, LS, PG), lambda g: (0, 0, off + g))
    tsp = pl.BlockSpec((PG, S), lambda g: (g, 0))
    pt, mp, pg, pgt = pl.pallas_call(
        body, name="pool_fwd", grid=(4,),
        in_specs=[col(24), col(4), pl.BlockSpec((None, PG, PG), lambda g: (g, 0, 0)),
                  pl.BlockSpec((1, PG), lambda g: (0, g))],
        out_specs=[tsp, col(0), col(0), tsp],
        out_shape=[_sds((AW, S), CDT), _sds((R16, LS, AW), F32), _sds((R16, LS, AW), CDT), _sds((AW, S), CDT)],
        scratch_shapes=[pltpu.VMEM((R16, LS, PG), F32)],
        compiler_params=_cp(("parallel",)),
    )(r3, r3, pool_w, pool_scale)
    return pt, mp.reshape(S, AW), pg.reshape(S, AW), pgt


def _branch_merge(ag, pg, wab, wpb, rest):
    S = ag.shape[0]
    tm, tn = 512, 512

    def body(ag_ref, pg_ref, wa_ref, wp_ref, ga_ref, gp_ref, ya_ref, yp_ref, m_ref, mt_ref):
        ya = _dot(ag_ref[...], wa_ref[...], NN)
        yp = _dot(pg_ref[...], wp_ref[...], NN)
        ya_ref[...] = ya
        yp_ref[...] = yp
        mg = _sigmoid(ga_ref[...]) * ya + _sigmoid(gp_ref[...]) * yp
        m_ref[...] = mg.astype(m_ref.dtype)
        mt_ref[...] = mg.T.astype(mt_ref.dtype)

    a = pl.BlockSpec((tm, AW), lambda i, j: (i, 0))
    w = pl.BlockSpec((AW, tn), lambda i, j: (0, j))
    o = pl.BlockSpec((tm, tn), lambda i, j: (i, j))
    return pl.pallas_call(
        body, name="branch_merge", grid=(S // tm, D // tn),
        in_specs=[a, a, w, w, pl.BlockSpec((tm, tn), lambda i, j: (i, 2 + j)),
                  pl.BlockSpec((tm, tn), lambda i, j: (i, 4 + j))],
        out_specs=[o, o, o, pl.BlockSpec((tn, tm), lambda i, j: (j, i))],
        out_shape=[_sds((S, D), F32), _sds((S, D), F32), _sds((S, D), CDT), _sds((D, S), CDT)],
        compiler_params=_cp(("parallel", "parallel")),
    )(ag, pg, wab, wpb, rest, rest)


def _out_loss(merged, wout, xs, ts, gate, final_g):
    LS = xs.shape[1]
    tl = 16

    def body(m_ref, w_ref, x_ref, t_ref, gate_ref, fg_ref, dx2_ref, dmo_ref, loss_ref, gfg_ref, dgate_ref):
        i = pl.program_id(0)

        @pl.when(i == 0)
        def _():
            loss_ref[...] = jnp.zeros_like(loss_ref)
            gfg_ref[...] = jnp.zeros_like(gfg_ref)
            dgate_ref[...] = jnp.zeros_like(dgate_ref)

        mo = _dot(_ld(m_ref), w_ref[...], NN)
        gate, fg = gate_ref[...], fg_ref[...]
        tgt = jnp.concatenate([t_ref[:, r, :] for r in range(R16)], axis=0)
        x2 = _ld(x_ref) + gate * mo
        r2 = lax.rsqrt(jnp.mean(x2 * x2, axis=-1, keepdims=True) + EPS)
        xn2 = x2 * r2
        diff = xn2 * fg - tgt
        loss = jnp.sum(jnp.sum(diff * diff, axis=-1, keepdims=True), axis=0, keepdims=True)
        dy = diff * (1.0 / D)
        dxn2 = dy * fg
        dx2 = r2 * (dxn2 - xn2 * jnp.mean(dxn2 * xn2, axis=-1, keepdims=True))
        _st(dx2_ref, dx2)
        _st(dmo_ref, dx2 * gate)
        loss_ref[...] += jnp.broadcast_to(loss * (0.5 / D), loss_ref.shape)
        gfg_ref[...] += jnp.sum(dy * xn2, axis=0, keepdims=True)
        dgate_ref[...] += jnp.sum(dx2 * mo, axis=0, keepdims=True)

    slab = pl.BlockSpec((R16, tl, D), lambda i: (0, i, 0))
    vec = pl.BlockSpec((1, D), lambda i: (0, 0))
    return pl.pallas_call(
        body, name="out_loss", grid=(LS // tl,),
        in_specs=[slab, _resident((D, D)), slab, pl.BlockSpec((tl, R16, D), lambda i: (i, 0, 0)), vec, vec],
        out_specs=[slab, slab, pl.BlockSpec((1, 128), lambda i: (0, 0)), vec, vec],
        out_shape=[_sds((R16, LS, D), F32), _sds((R16, LS, D), CDT), _sds((1, 128), F32), _sds((1, D), F32),
                   _sds((1, D), F32)],
        compiler_params=_cp(("arbitrary",)),
    )(merged, wout, xs, ts, gate, final_g)


def _bwd_gates(dmo, wout, wab, wpb, rest, ya, yp, attn, mp, pool_scale):
    S = dmo.shape[0]
    tm = 256

    def body(dmo_ref, wo_ref, wa_ref, wp_ref, za_ref, zp_ref, ga0, ga1, gp0, gp1, ya_ref, yp_ref, at_ref, mp_ref,
             ps_ref, dya_ref, dyp_ref, dr_ref, dat_ref, dl_ref, dmp_ref, dps_ref):
        i = pl.program_id(0)

        @pl.when(i == 0)
        def _():
            dps_ref[...] = jnp.zeros_like(dps_ref)

        dm = _dot(dmo_ref[...], wo_ref[...], NT)
        sa = _sigmoid(jnp.concatenate([ga0[...], ga1[...]], axis=1))
        sp = _sigmoid(jnp.concatenate([gp0[...], gp1[...]], axis=1))
        dya = (dm * sa).astype(CDT)
        dyp = (dm * sp).astype(CDT)
        dya_ref[...] = dya
        dyp_ref[...] = dyp
        dr_ref[:, 1024:2048] = (dm * ya_ref[...] * sa * (1.0 - sa)).astype(dr_ref.dtype)
        dr_ref[:, 2048:3072] = (dm * yp_ref[...] * sp * (1.0 - sp)).astype(dr_ref.dtype)
        da = _dot(dya, wa_ref[...], NT)
        dp = _dot(dyp, wp_ref[...], NT)
        za = za_ref[...]
        sga = _sigmoid(za)
        attn = at_ref[...]
        dattn = da * (za * sga)
        dr_ref[:, 0:512] = (da * attn * (sga * (1.0 + za * (1.0 - sga)))).astype(dr_ref.dtype)
        dat_ref[...] = dattn.astype(dat_ref.dtype)
        prod = dattn * attn
        for h in range(NH):
            sl = slice(HD * h, HD * h + HD)
            dl_ref[:, sl] = jnp.broadcast_to(jnp.sum(prod[:, sl], axis=1, keepdims=True), (tm, HD))
        zp = zp_ref[...]
        sgp = _sigmoid(zp)
        mpre = mp_ref[...]
        ps = ps_ref[...]
        dmixed = dp * (zp * sgp)
        dr_ref[:, 512:1024] = (dp * (mpre * ps) * (sgp * (1.0 + zp * (1.0 - sgp)))).astype(dr_ref.dtype)
        dps_ref[...] += jnp.sum(dmixed * mpre, axis=0, keepdims=True)
        dmp_ref[...] = (dmixed * ps).astype(dmp_ref.dtype)

    full = lambda shape: pl.BlockSpec(shape, lambda i: (0, 0))
    tD = pl.BlockSpec((tm, D), lambda i: (i, 0))
    tA = pl.BlockSpec((tm, AW), lambda i: (i, 0))
    rb = lambda c: pl.BlockSpec((tm, AW), lambda i: (i, c))
    return pl.pallas_call(
        body, name="bwd_gates", grid=(S // tm,),
        in_specs=[tD, full((D, D)), full((AW, D)), full((AW, D)), rb(0), rb(1), rb(2), rb(3), rb(4), rb(5),
                  tD, tD, tA, tA, full((1, AW))],
        out_specs=[tD, tD, pl.BlockSpec((tm, 3072), lambda i: (i, 0)), tA, tA, tA, full((1, AW))],
        out_shape=[_sds((S, D), CDT), _sds((S, D), CDT), _sds((S, REST_W), CDT), _sds((S, AW), CDT),
                   _sds((S, AW), F32), _sds((S, AW), CDT), _sds((1, AW), F32)],
        compiler_params=_cp(("arbitrary",), vmem=56),
    )(dmo, wout, wab, wpb, rest, rest, rest, rest, rest, rest, ya, yp, attn, mp, pool_scale)


def _pool_bwd(dmp, pt, pool_w, drest):
    S = dmp.shape[0]
    LS = S // R16
    PG = 128
    d3 = dmp.reshape(R16, LS, AW)
    dr3 = drest.reshape(R16, LS, REST_W)

    def body(d_ref, pt_ref, pw_ref, dr_in, du_ref, dpw_ref, g_ref, s_ref):
        del dr_in
        g = pl.program_id(0)
        win = jnp.left_shift(2, g)
        row = lax.broadcasted_iota(jnp.int32, (LS, PG), 0)
        pw = pw_ref[...].astype(CDT)
        dpw = jnp.zeros((PG, PG), F32)
        for r in range(R16):
            dm = d_ref[r]
            dpw = dpw + _dot(pt_ref[:, r * LS:(r + 1) * LS], dm, NN)
            dpooled = _dot(dm, pw, NT)
            g_ref[r] = dpooled
            s_ref[r] = dpooled / _pool_counts(r, LS, win)
        dpw_ref[...] = dpw
        for r in range(R16):
            acc = s_ref[r]
            for j in range(1, R16):
                coef = jnp.where(j < win, 1.0, 0.0).astype(F32)
                if r + j < R16:
                    src = s_ref[r + j]
                else:
                    src = jnp.where(row == LS - 1, 0.0, pltpu.roll(s_ref[r + j - R16], LS - 1, 0))
                acc = acc + coef * src
            du_ref[r] = (acc - g_ref[r]).astype(du_ref.dtype)

    du, dpw = pl.pallas_call(
        body, name="pool_bwd", grid=(4,),
        in_specs=[pl.BlockSpec((R16, LS, PG), lambda g: (0, 0, g)), pl.BlockSpec((PG, S), lambda g: (g, 0)),
                  pl.BlockSpec((None, PG, PG), lambda g: (g, 0, 0)), pl.BlockSpec(memory_space=pl.ANY)],
        out_specs=[pl.BlockSpec((R16, LS, PG), lambda g: (0, 0, 24 + g)),
                   pl.BlockSpec((None, PG, PG), lambda g: (g, 0, 0))],
        out_shape=[_sds((R16, LS, REST_W), CDT), _sds((4, PG, PG), F32)],
        scratch_shapes=[pltpu.VMEM((R16, LS, PG), F32), pltpu.VMEM((R16, LS, PG), F32)],
        input_output_aliases={3: 0},
        compiler_params=_cp(("parallel",)),
    )(d3, pt, pool_w, dr3)
    return du.reshape(S, REST_W), dpw


def _attn_bwd(qkv, dattn, Lb, dlb, biases, geom, dqkv_prev):
    S = qkv.shape[0]
    gi, T, R, has_prev, N = geom.gi, geom.T, geom.rows, geom.has_prev, geom.n_tiles
    H2 = T // 2
    nd_outer = len(geom.outer)
    qv = geom.view(qkv)
    cq, ck, cv = 3 * gi, 3 * gi + 1, 3 * gi + 2
    W3 = 3 * AW

    def heads(q_ref, kp_ref, kc_ref, vp_ref, vc_ref, do_ref, L_ref, dl_ref, ba_ref, bb_ref, pm, dsa_ref, dsb_ref, emit):
        q_lo, q_hi = _split_rows(_ld(q_ref), R)
        do_lo, do_hi = _split_rows(_ld(do_ref), R)
        L_lo, L_hi = _split_rows(_ld(L_ref), R)
        dl_lo, dl_hi = _split_rows(_ld(dl_ref), R)
        kc_lo, kc_hi = _split_rows(_ld(kc_ref), R)
        vc_lo, vc_hi = _split_rows(_ld(vc_ref), R)
        k_b = jnp.concatenate([kc_lo, kc_hi], axis=0)
        v_b = jnp.concatenate([vc_lo, vc_hi], axis=0)
        if has_prev:
            k_a = jnp.concatenate([_split_rows(_ld(kp_ref), R)[1], kc_lo], axis=0)
            v_a = jnp.concatenate([_split_rows(_ld(vp_ref), R)[1], vc_lo], axis=0)
        else:
            k_a, v_a = kc_lo, vc_lo
        for h in range(NH):
            sl = slice(HD * h, HD * h + HD)
            col = slice(HD * h, HD * h + 1)
            res = []
            for q, k, v, do, L, dl, b_ref, mask, ds_ref in (
                    (q_lo, k_a, v_a, do_lo, L_lo, dl_lo, ba_ref, pm, dsa_ref),
                    (q_hi, k_b, v_b, do_hi, L_hi, dl_hi, bb_ref, None, dsb_ref)):
                qs = (q[:, sl].astype(F32) * 0.125).astype(q.dtype)
                kh, vh, doh = k[:, sl], v[:, sl], do[:, sl]
                s = _dot(qs, kh, NT) + b_ref[h]
                if mask is not None:
                    s = s + mask
                p = jnp.exp(s - L[:, col])
                ds = p * (_dot(doh, vh, NT) - dl[:, col])
                ds_ref[h] += ds
                dsc = ds.astype(q.dtype)
                res.append((_dot(dsc, kh, NN) * 0.125, _dot(dsc, qs, TN), _dot(p.astype(q.dtype), doh, TN)))
            (dq_lo, dk_a, dv_a), (dq_hi, dk_b, dv_b) = res
            dq = _merge_rows(dq_lo, dq_hi, R)
            if has_prev:
                emit(sl, dq, _merge_rows(dk_a[H2:] + dk_b[:H2], dk_b[H2:], R),
                     _merge_rows(dv_a[H2:] + dv_b[:H2], dv_b[H2:], R), dk_a[:H2], dv_a[:H2])
            else:
                emit(sl, dq, _merge_rows(dk_a + dk_b[:H2], dk_b[H2:], R),
                     _merge_rows(dv_a + dv_b[:H2], dv_b[H2:], R), None, None)

    def first_step():
        ok = pl.program_id(nd_outer) == 0 if has_prev else pl.program_id(0) == 0
        for a in range(nd_outer if has_prev else 0):
            ok = ok & (pl.program_id(a) == 0)
        return ok

    if has_prev:
        def body(q_ref, kp_ref, kc_ref, vp_ref, vc_ref, do_ref, L_ref, dl_ref, ba_ref, bb_ref, prev_in, out_ref,
                 dsa_ref, dsb_ref, hold, new, pp):
            del prev_in
            s_id = pl.program_id(nd_outer)

            @pl.when(first_step())
            def _():
                dsa_ref[...] = jnp.zeros_like(dsa_ref)
                dsb_ref[...] = jnp.zeros_like(dsb_ref)

            @pl.when(s_id < N)
            def _():
                def emit(sl, dq, dk, dv, dk_prev, dv_prev):
                    new[:, sl] = dq
                    new[:, slice(AW + sl.start, AW + sl.stop)] = dk
                    new[:, slice(2 * AW + sl.start, 2 * AW + sl.stop)] = dv
                    pp[:, sl] = dk_prev
                    pp[:, slice(AW + sl.start, AW + sl.stop)] = dv_prev

                heads(q_ref, kp_ref, kc_ref, vp_ref, vc_ref, do_ref, L_ref, dl_ref, ba_ref, bb_ref,
                      _prev_mask(s_id, H2, T), dsa_ref, dsb_ref, emit)

            @pl.when((s_id >= 1) & (s_id < N))
            def _():
                _st(out_ref, hold[:, 0:AW], slice(0, AW))
                late = _merge_rows(jnp.zeros((H2, 2 * AW), F32), pp[...], R)
                _st(out_ref, hold[:, AW:W3] + late, slice(AW, W3))

            @pl.when(s_id == N)
            def _():
                _st(out_ref, hold[...])

            @pl.when(s_id < N)
            def _():
                hold[...] = new[...]

        cur = lambda n: jnp.minimum(n, N - 1)
        prev = lambda n: jnp.clip(n - 1, 0, N - 1)
        delayed = lambda n: jnp.maximum(n - 1, 0)
        in_specs = [geom.spec(AW, cq, cur), geom.spec(AW, ck, prev), geom.spec(AW, ck, cur),
                    geom.spec(AW, cv, prev), geom.spec(AW, cv, cur),
                    geom.spec(AW, 0, cur), geom.spec(AW, 0, cur), geom.spec(AW, 0, cur)]
        args = [qv] * 5 + [geom.view(dattn), geom.view(Lb), geom.view(dlb)]
        out_spec = geom.spec(W3, gi, delayed)
        grid = geom.outer + (N + 1,)
        scratch = [pltpu.VMEM((T, W3), F32), pltpu.VMEM((T, W3), F32), pltpu.VMEM((H2, 2 * AW), F32)]
    else:
        def body(q_ref, kc_ref, vc_ref, do_ref, L_ref, dl_ref, ba_ref, bb_ref, prev_in, out_ref, dsa_ref, dsb_ref):
            del prev_in

            @pl.when(first_step())
            def _():
                dsa_ref[...] = jnp.zeros_like(dsa_ref)
                dsb_ref[...] = jnp.zeros_like(dsb_ref)

            def emit(sl, dq, dk, dv, dk_prev, dv_prev):
                _st(out_ref, dq, sl)
                _st(out_ref, dk, slice(AW + sl.start, AW + sl.stop))
                _st(out_ref, dv, slice(2 * AW + sl.start, 2 * AW + sl.stop))

            heads(q_ref, None, kc_ref, None, vc_ref, do_ref, L_ref, dl_ref, ba_ref, bb_ref, None, dsa_ref, dsb_ref,
                  emit)

        cur = lambda n: n
        in_specs = [geom.spec(AW, cq, cur), geom.spec(AW, ck, cur), geom.spec(AW, cv, cur),
                    geom.spec(AW, 0, cur), geom.spec(AW, 0, cur), geom.spec(AW, 0, cur)]
        args = [qv] * 3 + [geom.view(dattn), geom.view(Lb), geom.view(dlb)]
        out_spec = geom.spec(W3, gi, cur)
        grid = geom.outer
        scratch = []

    whole = lambda b: pl.BlockSpec(b.shape, lambda *ids: (0, 0, 0))
    in_specs += [whole(b) for b in biases]
    in_specs.append(pl.BlockSpec(memory_space=pl.ANY))
    if dqkv_prev is None:
        last, aliases = biases[0], {}
    else:
        last, aliases = geom.view(dqkv_prev), {len(in_specs) - 1: 0}
    dq, dsa, dsb = pl.pallas_call(
        body, name=f"attn_bwd{gi}", grid=grid, in_specs=in_specs,
        out_specs=[out_spec] + [whole(b) for b in biases],
        out_shape=[_sds(geom.view_shape(QKV_W), CDT)] + [_sds(b.shape, F32) for b in biases],
        scratch_shapes=scratch,
        input_output_aliases=aliases,
        compiler_params=_cp(("arbitrary",) * len(grid), vmem=56),
    )(*args, *biases, last)
    return geom.unview(dq), [dsa, dsb]


def _h_bwd(dh, dx2, xs, norm_g, scale):
    LS = xs.shape[1]
    tl = 16

    def body(dh_ref, dx2_ref, x_ref, g_ref, sc_ref, gx_ref, dsh_ref, dsc_ref, dng_ref):
        i = pl.program_id(0)

        @pl.when(i == 0)
        def _():
            dsh_ref[...] = jnp.zeros_like(dsh_ref)
            dsc_ref[...] = jnp.zeros_like(dsc_ref)
            dng_ref[...] = jnp.zeros_like(dng_ref)

        g = g_ref[...]
        one_sc = 1.0 + sc_ref[...]
        x = _ld(x_ref)
        r1 = lax.rsqrt(jnp.mean(x * x, axis=-1, keepdims=True) + EPS)
        xn = x * r1
        dh = _ld(dh_ref)
        dhx = dh * xn
        dxn = dh * (g * one_sc)
        dx1 = r1 * (dxn - xn * jnp.mean(dxn * xn, axis=-1, keepdims=True))
        gx = _ld(dx2_ref) + dx1
        for r in range(R16):
            gx_ref[:, r, :] = gx[r * tl:(r + 1) * tl]
        dsh_ref[...] += jnp.sum(dh, axis=0, keepdims=True)
        dsc_ref[...] += jnp.sum(dhx * g, axis=0, keepdims=True)
        dng_ref[...] += jnp.sum(dhx * one_sc, axis=0, keepdims=True)

    slab = pl.BlockSpec((R16, tl, D), lambda i: (0, i, 0))
    vec = pl.BlockSpec((1, D), lambda i: (0, 0))
    return pl.pallas_call(
        body, name="h_bwd", grid=(LS // tl,),
        in_specs=[slab, slab, slab, vec, vec],
        out_specs=[pl.BlockSpec((tl, R16, D), lambda i: (i, 0, 0)), vec, vec, vec],
        out_shape=[_sds((LS, R16, D), F32), _sds((1, D), F32), _sds((1, D), F32), _sds((1, D), F32)],
        compiler_params=_cp(("arbitrary",)),
    )(dh, dx2, xs, norm_g, scale)


def _local_step(x, target, shift, scale, gate, norm_g, w_in_full, pool_w, pool_scale, wab, wpb, wout, rel_bias,
                final_g):
    st = _local_pre(x, shift, scale, norm_g, rel_bias)
    st = _local_grads(st, target, gate, w_in_full, pool_w, pool_scale, wab, wpb, wout, final_g)
    d_rel_bias = _local_bias_grads(st)
    gx, dmod, dng = _local_input_grads(st, w_in_full, norm_g, scale)
    return dict(loss=st["loss"], grad_x=gx, dw_in=st["dw_in"], dw_ab=st["dw_ab"], dw_pb=st["dw_pb"],
                dw_out=st["dw_out"], d_norm_g=dng, d_pool_w=st["dpw"], d_pool_scale=st["dps"],
                d_rel_bias=d_rel_bias, d_final_g=st["gfg"], dmod=dmod)


def _local_pre(x, shift, scale, norm_g, rel_bias, dep=None):
    S = x.shape[0]
    LS = S // R16
    geoms = [_Geom(gi, LS) for gi in range(3)]
    idxs = [[jnp.asarray(g.bucket_index(sub)) for sub in range(2)] for g in geoms]
    biases = [[_bias_table(idxs[gi][sub], rel_bias, gi, sub, dep) for sub in range(2)] for gi in range(3)]
    h3, xs3 = _norm_mod(x.reshape(LS, R16, D), norm_g, shift, scale, dep)
    h = h3.reshape(S, D)
    return dict(geoms=geoms, idxs=idxs, biases=biases, xs3=xs3, h=h, ht=_tcast(h, "h_transpose"))


def _local_grads(st, target, gate, w_in_full, pool_w, pool_scale, wab, wpb, wout, final_g):
    geoms, idxs, biases, xs3, h, ht = (st[k] for k in ("geoms", "idxs", "biases", "xs3", "h", "ht"))
    S = h.shape[0]
    LS = S // R16
    qkv, rest = _proj(h, w_in_full)

    os_, ls_ = [], []
    for gi in range(3):
        o, l = _attn_fwd(qkv, biases[gi], geoms[gi])
        os_.append(o)
        ls_.append(l)
    attn, Lb, ag, agt = _attn_combine(os_, ls_, rest)
    pt, mp, pg, pgt = _pool_fwd(rest, pool_w, pool_scale)
    ya, yp, merged, mt = _branch_merge(ag, pg, wab, wpb, rest)
    dx2_3, dmo3, loss, gfg, dgate = _out_loss(merged.reshape(R16, LS, D), wout, xs3, target.reshape(LS, R16, D),
                                              gate, final_g)
    dmo = dmo3.reshape(S, D)

    dya, dyp, drest, dattn, dlb, dmp, dps = _bwd_gates(dmo, wout, wab, wpb, rest, ya, yp, attn, mp, pool_scale)
    dw_out = _mm_tn(mt, dmo, n_total=D, out_blk=lambda j: j, name="dw_out")
    dw_ab = _mm_tn(agt, dya, n_total=D, out_blk=lambda j: j, name="dw_attn_br")
    dw_pb = _mm_tn(pgt, dyp, n_total=D, out_blk=lambda j: j, name="dw_pool_br")
    drest, dpw = _pool_bwd(dmp, pt, pool_w, drest)

    dqkv = None
    dsums = []
    for gi in range(3):
        dqkv, dsum = _attn_bwd(qkv, dattn, Lb, dlb, biases[gi], geoms[gi], dqkv)
        dsums.append(dsum)

    dw_in = _mm_tn(ht, dqkv, n_total=8192, out_blk=lambda j: j, name="dw_in_qkv")
    dw_in = _mm_tn(ht, drest, n_total=8192, out_blk=_rest_blk, name="dw_in_rest", prev=dw_in)
    return dict(st, loss=loss[0, 0], dw_in=dw_in, dw_ab=dw_ab, dw_pb=dw_pb, dw_out=dw_out, dpw=dpw, dps=dps, gfg=gfg,
                dgate=dgate, dsums=dsums, dqkv=dqkv, drest=drest, dx2_3=dx2_3)


def _local_bias_grads(st, dep=None):
    drb = [_bias_grad(st["dsums"][gi][0], st["idxs"][gi][0], gi, 0, dep if gi == 0 else None) +
           _bias_grad(st["dsums"][gi][1], st["idxs"][gi][1], gi, 1) for gi in range(3)]
    return jnp.concatenate(drb, axis=1)


def _local_input_grads(st, w_in_full, norm_g, scale, dep=None):
    S = st["h"].shape[0]
    LS = S // R16
    dh = _dh(st["dqkv"], st["drest"], w_in_full, dep)
    gx3, dsh, dsc, dng = _h_bwd(dh.reshape(R16, LS, D), st["dx2_3"], st["xs3"], norm_g, scale)
    return gx3.reshape(S, D), jnp.concatenate([dsh, dsc, st["dgate"]], axis=1), dng


def _my_place():
    return lax.axis_index("x"), lax.axis_index("y"), lax.axis_index("c")


def _block_of(ref, axis, idx, width):
    if axis is None:
        return ref.at[idx]
    start = pl.multiple_of(idx * width, width)
    if axis == 0:
        return ref.at[pl.ds(start, width)]
    return ref.at[:, pl.ds(start, width)]


_SEM = pl.BlockSpec(memory_space=pltpu.SEMAPHORE)
_EFFECT = pltpu.SideEffectType.DATAFLOW_SIDE_EFFECTING


def _split_start(name, srcs, lands, ncopy, plan, dep=None):
    arrs = list(srcs) + list(lands)
    na, ns = len(arrs), len(srcs)
    dep_specs, dep_args = _after(dep)

    def body(*refs):
        send_sems, recv_sems, token = refs[na + len(dep_args)], refs[na + len(dep_args) + 1], refs[-1]
        for k, (s, d, dev) in enumerate(plan(refs[:ns], refs[ns:na])):
            pltpu.make_async_remote_copy(src_ref=s, dst_ref=d, send_sem=send_sems.at[k], recv_sem=recv_sems.at[k],
                                         device_id=dev, device_id_type=MESH).start()
        token[...] = jnp.zeros_like(token)

    outs = pl.pallas_call(
        body, name=name,
        out_shape=(pltpu.SemaphoreType.DMA((ncopy,)), pltpu.SemaphoreType.DMA((ncopy,)),
                   *[pltpu.HBM(a.shape, a.dtype) for a in arrs], _sds((8, 128), F32)),
        in_specs=[_HBM] * na + dep_specs,
        out_specs=(_SEM, _SEM, *[_HBM] * na, pl.BlockSpec(memory_space=pltpu.VMEM)),
        input_output_aliases={i: 2 + i for i in range(na)},
        compiler_params=pltpu.CompilerParams(has_side_effects=_EFFECT),
    )(*[pltpu.with_memory_space_constraint(a, pltpu.HBM) for a in arrs], *dep_args)
    return (outs[0], outs[1]), list(outs[2:2 + ns]), list(outs[2 + ns:2 + na]), outs[-1]


def _split_wait(name, sems, srcs, lands, plan, after):
    arrs = list(srcs) + list(lands)
    na, ns = len(arrs), len(srcs)
    after = list(after) if isinstance(after, (list, tuple)) else [after]

    def body(*refs):
        send_sems, recv_sems = refs[na], refs[na + 1]
        for k, (s, d, dev) in enumerate(plan(refs[:ns], refs[ns:na])):
            cp = pltpu.make_async_remote_copy(src_ref=s, dst_ref=d, send_sem=send_sems.at[k], recv_sem=recv_sems.at[k],
                                              device_id=dev, device_id_type=MESH)
            cp.wait_send()
            cp.wait_recv()

    outs = pl.pallas_call(
        body, name=name, out_shape=[pltpu.HBM(a.shape, a.dtype) for a in arrs],
        in_specs=[_HBM] * na + [_SEM, _SEM] + [pl.BlockSpec(memory_space=pl.ANY)] * len(after),
        out_specs=[_HBM] * na,
        input_output_aliases={i: i for i in range(na)},
        compiler_params=pltpu.CompilerParams(has_side_effects=_EFFECT),
    )(*arrs, sems[0], sems[1], *after)
    return list(outs[:ns]), list(outs[ns:])


def _gather_chips_plan(layout):
    def plan(src, land):
        x, y, c = _my_place()
        me = 4 * x + 2 * y + c
        return [(src[p], _block_of(land[p], layout[p][0], me, layout[p][1]), (cx, cy, c))
                for p in range(len(src)) for cx, cy in [(1 - x, y), (x, 1 - y), (1 - x, 1 - y)]]
    return plan


def _pair_plan(layout):
    def plan(src, land):
        x, y, c = _my_place()
        n = len(layout)
        return [(_block_of(src[p], layout[p][0], 2 * q + (1 - c), layout[p][1]), land[p].at[q], (x, y, 1 - c))
                for p in range(n) for q in range(4)] + [(src[n], land[n], (x, y, 1 - c))]
    return plan


def _chips_plan(src, land):
    x, y, c = _my_place()
    n = len(src) - 1
    chips = [(1 - x, y), (x, 1 - y), (1 - x, 1 - y)]
    return [(src[p].at[2 * cx + cy], land[p].at[2 * x + y], (cx, cy, c)) for p in range(n) for cx, cy in chips] + \
           [(src[n], land[n].at[2 * x + y], (cx, cy, c)) for cx, cy in chips]


def _pair_share(gathered):
    n = len(gathered)

    def body(*refs):
        land = refs[n:2 * n]
        send_sems, recv_sems = refs[2 * n:]
        x, y, c = _my_place()
        cps = []
        for p in range(n):
            for k in range(4):
                blk = land[p].at[2 * k + c]
                cps.append(pltpu.make_async_remote_copy(
                    src_ref=blk, dst_ref=blk, send_sem=send_sems.at[4 * p + k], recv_sem=recv_sems.at[4 * p + k],
                    device_id=(x, y, 1 - c), device_id_type=MESH))
        for cp in cps:
            cp.start()
        for cp in cps:
            cp.wait_send()
        for p in range(n):
            for k in range(4):
                blk = land[p].at[2 * k + 1 - c]
                pltpu.make_async_remote_copy(
                    src_ref=blk, dst_ref=blk, send_sem=send_sems.at[4 * p + k], recv_sem=recv_sems.at[4 * p + k],
                    device_id=(x, y, 1 - c), device_id_type=MESH).wait_recv()

    return pl.pallas_call(
        body, name="gather_pair_share", in_specs=[_HBM] * n, out_specs=[_HBM] * n,
        out_shape=[_sds(g.shape, g.dtype) for g in gathered],
        input_output_aliases={p: p for p in range(n)},
        scratch_shapes=[pltpu.SemaphoreType.DMA((4 * n,)), pltpu.SemaphoreType.DMA((4 * n,))],
    )(*gathered)


def _place_slot(arrs, slot, n_slots, name):
    n = len(arrs)

    def body(s_ref, *refs):
        for p in range(n):
            refs[n + p][...] = refs[p][...]

    def spec(a):
        tail = a.shape[-2:]
        return pl.BlockSpec((None,) + tail, lambda i, s_ref: (s_ref[0], 0, 0))

    in_specs = [spec(a) if a.ndim == 3 else pl.BlockSpec(a.shape, lambda i, s_ref: (0, 0)) for a in arrs]
    return pl.pallas_call(
        body, name=name,
        grid_spec=pltpu.PrefetchScalarGridSpec(num_scalar_prefetch=1, grid=(1,), in_specs=in_specs,
                                               out_specs=[spec(a) for a in arrs]),
        out_shape=[_sds((n_slots,) + a.shape[-2:], a.dtype) for a in arrs],
        compiler_params=_cp(("arbitrary",)),
    )(slot, *arrs)


def _cols_from_blocks(w8, name):
    _, R, C = w8.shape

    def body(w_ref, o_ref):
        for e in range(8):
            o_ref[:, C * e:C * (e + 1)] = w_ref[e]

    return pl.pallas_call(body, name=name, out_shape=_sds((R, 8 * C), w8.dtype), compiler_params=_cp())(w8)


def _all_gather(srcs, out_shapes, layout, name):
    n = len(srcs)

    def body(*refs):
        src, out = refs[:n], refs[n:2 * n]
        send_sems, recv_sems, local_sems = refs[2 * n:]
        x, y, c = _my_place()
        me, sibling = (x, y, c), (x, y, 1 - c)
        chips = [(1 - x, y), (x, 1 - y), (1 - x, 1 - y)]

        def region(p, dev):
            px, py, pc = dev
            return _block_of(out[p], layout[p][0], 4 * px + 2 * py + pc, layout[p][1])

        def copy(p, k, block, to, src_ref=None):
            return pltpu.make_async_remote_copy(
                src_ref=region(p, block) if src_ref is None else src_ref, dst_ref=region(p, block),
                send_sem=send_sems.at[p * 7 + k], recv_sem=recv_sems.at[p * 7 + k],
                device_id=to, device_id_type=MESH)

        mine = [pltpu.make_async_copy(src[p], region(p, me), local_sems.at[p]) for p in range(n)]
        for cp in mine:
            cp.start()
        first = []
        for p in range(n):
            first.append(copy(p, 0, me, sibling, src[p]))
            first += [copy(p, 1 + j, me, (*chip, c), src[p]) for j, chip in enumerate(chips)]
        for cp in first:
            cp.start()
        passed = []
        for j, chip in enumerate(chips):
            for p in range(n):
                copy(p, 1 + j, (*chip, c), me).wait_recv()
                fwd = copy(p, 4 + j, (*chip, c), sibling)
                fwd.start()
                passed.append(fwd)
        for p in range(n):
            copy(p, 0, sibling, me).wait_recv()
            for j, chip in enumerate(chips):
                copy(p, 4 + j, (*chip, 1 - c), me).wait_recv()
        for cp in first + passed:
            cp.wait_send()
        for cp in mine:
            cp.wait()

    return pl.pallas_call(
        body, name=name, in_specs=[_HBM] * n, out_specs=[_HBM] * n,
        out_shape=[_sds(s, a.dtype) for s, a in zip(out_shapes, srcs)],
        scratch_shapes=[pltpu.SemaphoreType.DMA((7 * n,)), pltpu.SemaphoreType.DMA((7 * n,)),
                        pltpu.SemaphoreType.DMA((n,))],
    )(*srcs)


def _reduce_pair_exchange(dws, layout, smalls):
    n = len(dws)

    def body(*refs):
        src, sm = refs[:n], refs[n]
        out, sm_out = refs[n + 1:2 * n + 1], refs[2 * n + 1]
        send_sems, recv_sems = refs[2 * n + 2:]
        x, y, c = _my_place()
        sibling = (x, y, 1 - c)
        cps = []
        for p in range(n):
            for q in range(4):
                cps.append(pltpu.make_async_remote_copy(
                    src_ref=_block_of(src[p], layout[p][0], 2 * q + (1 - c), layout[p][1]), dst_ref=out[p].at[q],
                    send_sem=send_sems.at[4 * p + q], recv_sem=recv_sems.at[4 * p + q],
                    device_id=sibling, device_id_type=MESH))
        cps.append(pltpu.make_async_remote_copy(src_ref=sm, dst_ref=sm_out, send_sem=send_sems.at[4 * n],
                                                recv_sem=recv_sems.at[4 * n], device_id=sibling, device_id_type=MESH))
        for cp in cps:
            cp.start()
        for cp in cps:
            cp.wait()

    def shard_shape(a, lay):
        axis, w = lay
        return (4, w, a.shape[1]) if axis == 0 else (4, a.shape[0], w)

    return pl.pallas_call(
        body, name="reduce_pair_exchange", in_specs=[_HBM] * (n + 1), out_specs=[_HBM] * (n + 1),
        out_shape=[_sds(shard_shape(a, l), F32) for a, l in zip(dws, layout)] + [_sds(smalls.shape, F32)],
        scratch_shapes=[pltpu.SemaphoreType.DMA((4 * n + 1,)), pltpu.SemaphoreType.DMA((4 * n + 1,))],
    )(*dws, smalls)


def _pair_add(dw, recv, axis, width, cidx, name):
    if axis == 0:
        tr = min(width, 256)
        C = dw.shape[1]
        grid = (4, width // tr)
        nb = width // tr
        own = pl.BlockSpec((tr, C), lambda q, i, c_ref: ((2 * q + c_ref[0]) * nb + i, 0))
        oth = pl.BlockSpec((None, tr, C), lambda q, i, c_ref: (q, i, 0))
    else:
        R = dw.shape[0]
        tr = min(R, 256)
        grid = (4, R // tr)
        own = pl.BlockSpec((tr, width), lambda q, i, c_ref: (i, 2 * q + c_ref[0]))
        oth = pl.BlockSpec((None, tr, width), lambda q, i, c_ref: (q, i, 0))

    def body(c_ref, a_ref, b_ref, o_ref):
        o_ref[...] = (a_ref[...] + b_ref[...]).astype(o_ref.dtype)

    return pl.pallas_call(
        body, name=name,
        grid_spec=pltpu.PrefetchScalarGridSpec(num_scalar_prefetch=1, grid=grid, in_specs=[own, oth], out_specs=oth),
        out_shape=_sds(recv.shape, CDT), compiler_params=_cp(("parallel", "parallel")),
    )(cidx, dw, recv)


def _add2(a, b, name):
    def body(a_ref, b_ref, o_ref):
        o_ref[...] = a_ref[...] + b_ref[...]

    return pl.pallas_call(body, name=name, out_shape=_sds(a.shape, F32))(a, b)


def _reduce_chip_exchange(parts):
    n = len(parts)

    def body(*refs):
        src, out = refs[:n], refs[n:2 * n]
        send_sems, recv_sems, local_sems = refs[2 * n:]
        x, y, c = _my_place()
        my_slot = 2 * x + y
        chips = [(1 - x, y), (x, 1 - y), (1 - x, 1 - y)]
        cps = []
        for p in range(n):
            keep = pltpu.make_async_copy(src[p].at[my_slot], out[p].at[my_slot], local_sems.at[p])
            keep.start()
            cps.append(keep)
            for k, (cx, cy) in enumerate(chips):
                cps.append(pltpu.make_async_remote_copy(
                    src_ref=src[p].at[2 * cx + cy], dst_ref=out[p].at[my_slot],
                    send_sem=send_sems.at[3 * p + k], recv_sem=recv_sems.at[3 * p + k],
                    device_id=(cx, cy, c), device_id_type=MESH))
                cps[-1].start()
        for cp in cps:
            cp.wait()

    return pl.pallas_call(
        body, name="reduce_chip_exchange", in_specs=[_HBM] * n, out_specs=[_HBM] * n,
        out_shape=[_sds(a.shape, a.dtype) for a in parts],
        scratch_shapes=[pltpu.SemaphoreType.DMA((3 * n,)), pltpu.SemaphoreType.DMA((3 * n,)),
                        pltpu.SemaphoreType.DMA((n,))],
    )(*parts)


def _adaln_part(c_all, w_ada, b_part, dep=None):
    dep_specs, dep_args = _after(dep)
    vmem = pl.BlockSpec(memory_space=pltpu.VMEM)

    def body(c_ref, w_ref, b_ref, *rest):
        rest[-1][...] = _dot(c_ref[...].astype(CDT), w_ref[...].astype(CDT), NN) + b_ref[...]

    return pl.pallas_call(body, name="adaln_part", in_specs=[vmem, vmem, vmem] + dep_specs, out_specs=vmem,
                          out_shape=_sds((8, w_ada.shape[1]), F32),
                          compiler_params=_cp())(c_all, w_ada, b_part, *dep_args)


def _adam_math(w, g, m, v):
    m = ADAM_B1 * m + (1.0 - ADAM_B1) * g
    v = ADAM_B2 * v + (1.0 - ADAM_B2) * (g * g)
    m_hat = m / (1.0 - ADAM_B1 ** ADAM_STEP)
    v_hat = v / (1.0 - ADAM_B2 ** ADAM_STEP)
    delta = -ADAM_LR * (m_hat / (jnp.sqrt(v_hat) + ADAM_EPS) + ADAM_WD * w)
    return delta, m, v


def _adam(gparts, w, m, v, name):
    P, R, C = gparts.shape
    tr = R if R <= 256 else 256

    def body(g_ref, w_ref, m_ref, v_ref, go_ref, d_ref, mo_ref, vo_ref):
        g = g_ref[0].astype(F32)
        for i in range(1, P):
            g = g + g_ref[i].astype(F32)
        d, mn, vn = _adam_math(w_ref[...], g, m_ref[...], v_ref[...])
        go_ref[...] = g
        d_ref[...] = d
        mo_ref[...] = mn
        vo_ref[...] = vn

    t = pl.BlockSpec((tr, C), lambda i: (i, 0))
    return pl.pallas_call(
        body, name=name, grid=(R // tr,),
        in_specs=[pl.BlockSpec((P, tr, C), lambda i: (0, i, 0)), t, t, t], out_specs=[t, t, t, t],
        out_shape=[_sds((R, C), F32)] * 4, compiler_params=_cp(("parallel",)),
    )(gparts, w, m, v)


def _sum_parts(gparts, name):
    P, R, C = gparts.shape

    def body(g_ref, o_ref):
        g = g_ref[0]
        for i in range(1, P):
            g = g + g_ref[i]
        o_ref[...] = g

    return pl.pallas_call(body, name=name, out_shape=_sds((R, C), F32))(gparts)


def _adam_w_ada(c_t, dm, w, m, v):
    R, C = w.shape

    def body(c_ref, dm_ref, w_ref, m_ref, v_ref, go_ref, d_ref, mo_ref, vo_ref):
        g = c_ref[:, 0:1] * dm_ref[0:1, :]
        for b in range(1, 8):
            g = g + c_ref[:, b:b + 1] * dm_ref[b:b + 1, :]
        d, mn, vn = _adam_math(w_ref[...], g, m_ref[...], v_ref[...])
        go_ref[...] = g
        d_ref[...] = d
        mo_ref[...] = mn
        vo_ref[...] = vn

    return pl.pallas_call(body, name="adam_w_ada", out_shape=[_sds((R, C), F32)] * 4,
                          compiler_params=_cp())(c_t, dm, w, m, v)


_SM = dict(norm_g=(0, 8), pool_w=(8, 512), pool_scale=(520, 8), rel_bias=(528, 8), final_g=(536, 8),
           dmod=(544, 192), loss=(736, 8))
_SM_ROWS = 744


def _pack_rows(a, rows):
    flat = a.reshape(-1).astype(F32)
    return jnp.pad(flat, (0, rows * 128 - flat.shape[0])).reshape(rows, 128)


def kernel(x, c, norm_g, w_ada, b_ada, w_in, pool_w, pool_scale, w_attn_br, w_pool_br, w_out, rel_bias, final_g, loss_target, m_norm_g, m_w_ada, m_b_ada, m_w_in, m_pool_w, m_pool_scale, m_w_attn_br, m_w_pool_br, m_w_out, m_rel_bias, m_final_g, v_norm_g, v_w_ada, v_b_ada, v_w_in, v_pool_w, v_pool_scale, v_w_attn_br, v_w_pool_br, v_w_out, v_rel_bias, v_final_g):
    S = x.shape[1]
    px, py, pc = _my_place()
    me = 4 * px + 2 * py + pc
    cidx = jnp.reshape(pc, (1,)).astype(jnp.int32)

    layout = [(1, 1024), (1, 128), (1, 128), (0, 128)]
    final_g2 = final_g.reshape(1, D)
    me1 = jnp.reshape(me, (1,)).astype(jnp.int32)

    (c_rows,) = _all_gather([jnp.broadcast_to(c, (8, D))], [(64, D)], [(0, 8)], "gather_c")
    c_all = c_rows[::8]

    b_part = lax.dynamic_slice(b_ada, (0, me * 384), (1, 384))
    mod_part = _adaln_part(c_all, w_ada[0], b_part)
    (mod_all,) = _all_gather([mod_part], [(8, 8, 384)], [(None, 1)], "gather_mod")
    mod = lax.dynamic_index_in_dim(mod_all, me, axis=1, keepdims=False).reshape(1, 3 * D)
    shift, scale, gate = mod[:, :D], mod[:, D:2 * D], mod[:, 2 * D:]

    shards = [w_in[0].astype(CDT), w_attn_br[0].astype(CDT), w_pool_br[0].astype(CDT), w_out[0].astype(CDT)]
    gplan = _gather_chips_plan([(None, 1)] * 4)
    g_sems, shards, lands, tok = _split_start("gather_chips_start", shards,
                                              _place_slot(shards, me1, 8, "place_own_shards"), 12, gplan, dep=mod_all)
    st = _local_pre(x[0], shift, scale, norm_g, rel_bias, dep=tok)
    _, lands = _split_wait("gather_chips_wait", g_sems, shards, lands, gplan,
                           [st["ht"]] + [b for pair in st["biases"] for b in pair])
    w_in_full, wab8, wpb8, wout8 = _pair_share(lands)
    wab, wpb = _cols_from_blocks(wab8, "w_attn_br_cols"), _cols_from_blocks(wpb8, "w_pool_br_cols")
    wout = wout8.reshape(D, D)

    st = _local_grads(st, loss_target[0], gate, w_in_full, pool_w[0], pool_scale, wab, wpb, wout, final_g2)

    early = jnp.concatenate([_pack_rows(st["dpw"], 512), _pack_rows(st["dps"], 8), _pack_rows(st["gfg"], 8),
                             _pack_rows(st["loss"], 8)], axis=0)
    dws = [st["dw_in"], st["dw_ab"], st["dw_pb"], st["dw_out"]]
    pplan = _pair_plan(layout)
    pair_shapes = [(4, w, a.shape[1]) if ax == 0 else (4, a.shape[0], w) for a, (ax, w) in zip(dws, layout)]
    p_sems, srcs, recv, tok = _split_start("reduce_pair_start", dws + [early],
                                           [lax.empty(s, F32) for s in pair_shapes + [early.shape]], 17, pplan)
    d_rel_bias = _local_bias_grads(st, dep=tok)
    srcs, recv = _split_wait("reduce_pair_wait", p_sems, srcs, recv, pplan, d_rel_bias)
    parts = [_pair_add(dw, r, ax, w, cidx, f"pair_add{i}")
             for i, (dw, r, (ax, w)) in enumerate(zip(srcs[:4], recv[:4], layout))]
    parts.append(_add2(srcs[4], recv[4], "pair_add_smalls"))

    slot1 = jnp.reshape(2 * px + py, (1,)).astype(jnp.int32)
    c_sems, parts, lands, tok = _split_start("reduce_chips_start", parts,
                                             _place_slot(parts, slot1, 4, "place_own_partials"), 15, _chips_plan)
    grad_x, dmod, d_norm_g = _local_input_grads(st, w_in_full, norm_g, scale, dep=tok)

    late = jnp.concatenate([_pack_rows(dmod, 24), _pack_rows(d_norm_g, 8), _pack_rows(d_rel_bias, 8)], axis=0)
    (late_all,) = _all_gather([late], [(8, 40, 128)], [(None, 1)], "gather_late")

    def small_adam(gparts, w, m, v, nvalid, name):
        rows = gparts.shape[1]
        outs = _adam(gparts, _pack_rows(w, rows), _pack_rows(m, rows), _pack_rows(v, rows), f"adam_{name}")
        return [o.reshape(-1)[:nvalid].reshape(w.shape) for o in outs]

    o_norm_g = small_adam(late_all[:, 24:32], norm_g, m_norm_g, v_norm_g, D, "norm_g")
    o_rel_bias = small_adam(late_all[:, 32:40], rel_bias, m_rel_bias, v_rel_bias, 768, "rel_bias")
    dmod_all = late_all[:, 0:24].reshape(8, 3 * D)
    o_b_ada = [o.reshape(1, 3 * D) for o in _adam(dmod_all.reshape(8, 24, 128), b_ada.reshape(24, 128),
                                                   m_b_ada.reshape(24, 128), v_b_ada.reshape(24, 128), "adam_b_ada")]
    dm_mine = lax.dynamic_slice(dmod_all, (0, me * 384), (8, 384))
    o_w_ada = [o[None] for o in _adam_w_ada(c_all.T, dm_mine, w_ada[0], m_w_ada[0], v_w_ada[0])]

    _, (g_in, g_ab, g_pb, g_out, sm4) = _split_wait("reduce_chips_wait", c_sems, parts, lands, _chips_plan,
                                                    [o_w_ada[0], o_b_ada[0], o_norm_g[0], o_rel_bias[0]])
    o_w_in = [o[None] for o in _adam(g_in, w_in[0], m_w_in[0], v_w_in[0], "adam_w_in")]
    o_w_ab = [o[None] for o in _adam(g_ab, w_attn_br[0], m_w_attn_br[0], v_w_attn_br[0], "adam_w_attn_br")]
    o_w_pb = [o[None] for o in _adam(g_pb, w_pool_br[0], m_w_pool_br[0], v_w_pool_br[0], "adam_w_pool_br")]
    o_w_out = [o[None] for o in _adam(g_out, w_out[0], m_w_out[0], v_w_out[0], "adam_w_out")]
    o_pool_w = small_adam(sm4[:, 0:512], pool_w, m_pool_w, v_pool_w, 65536, "pool_w")
    o_pool_scale = small_adam(sm4[:, 512:520], pool_scale, m_pool_scale, v_pool_scale, AW, "pool_scale")
    o_final_g = small_adam(sm4[:, 520:528], final_g, m_final_g, v_final_g, D, "final_g")
    loss = _sum_parts(sm4[:, 528:536], "sum_loss")[0, 0]

    per_w = [o_norm_g, o_w_ada, o_b_ada, o_w_in, o_pool_w, o_pool_scale, o_w_ab, o_w_pb, o_w_out, o_rel_bias, o_final_g]
    outs = [loss, grad_x[None]]
    for k in range(4):
        outs += [o[k] for o in per_w]
    return tuple(outs)
```

```python
import functools
import math

import numpy as np
import jax
import jax.numpy as jnp
from jax import lax
from jax.experimental import pallas as pl
from jax.experimental.pallas import tpu as pltpu

F32 = jnp.float32
CDT = jnp.bfloat16
D = 1024
HD = 64
NH = 8
AW = 512
QKV_W = 4608
REST_W = 3584
R16 = 16
EPS = 1e-6
NEG = -1e30
N_BACK = 128
ADAM_LR, ADAM_B1, ADAM_B2, ADAM_EPS, ADAM_WD, ADAM_STEP = 0.001, 0.9, 0.999, 1e-08, 0.01, 10
MESH = pl.DeviceIdType.MESH
VMEM_MB = 1024 * 1024


def _cp(sem=None, vmem=48, **kw):
    if sem is not None:
        kw["dimension_semantics"] = sem
    return pltpu.CompilerParams(vmem_limit_bytes=vmem * VMEM_MB, **kw)


def _sds(shape, dtype):
    return jax.ShapeDtypeStruct(tuple(shape), dtype)


def _rest_blk(jj):
    return jnp.where(jj == 0, 9, jnp.where(jj == 6, 10, jj + 10))


def _sigmoid(z):
    return 1.0 / (1.0 + jnp.exp(-z))


def _dot(a, b, dims):
    return lax.dot_general(a, b, (dims, ((), ())), preferred_element_type=F32)


NN = ((1,), (0,))
NT = ((1,), (1,))
TN = ((0,), (0,))


def _ld(ref):
    if len(ref.shape) == 2:
        return ref[...]
    return jnp.concatenate([ref[a] for a in range(ref.shape[0])], axis=0)


def _st(ref, val, cols=None):
    val = val.astype(ref.dtype)
    if len(ref.shape) == 2:
        if cols is None:
            ref[...] = val
        else:
            ref[:, cols] = val
        return
    rows = ref.shape[1]
    for a in range(ref.shape[0]):
        if cols is None:
            ref[a] = val[a * rows:(a + 1) * rows]
        else:
            ref[a, :, cols] = val[a * rows:(a + 1) * rows]


REST_COLS = ((9, 10), (11, 16), (10, 11))
_HBM = pl.BlockSpec(memory_space=pltpu.HBM)


def _resident(shape):
    return pl.BlockSpec(shape, lambda *ids: (0,) * len(shape), pipeline_mode=pl.Buffered(1))


def _w512(w_ref, j):
    return w_ref[j // 2, :, 512 * (j % 2):512 * (j % 2 + 1)]


def _proj(h, w):
    S = h.shape[0]
    tm = 256

    def body(a_ref, w_ref, q_ref, r_ref):
        a = a_ref[...]
        for j in range(9):
            q_ref[:, 512 * j:512 * (j + 1)] = _dot(a, _w512(w_ref, j), NN).astype(q_ref.dtype)
        off = 0
        for lo, hi in REST_COLS:
            for j in range(lo, hi):
                r_ref[:, off:off + 512] = _dot(a, _w512(w_ref, j), NN)
                off += 512

    return pl.pallas_call(
        body, name="proj", grid=(S // tm,),
        in_specs=[pl.BlockSpec((tm, D), lambda i: (i, 0)), _resident((8, D, 1024))],
        out_specs=[pl.BlockSpec((tm, QKV_W), lambda i: (i, 0)), pl.BlockSpec((tm, REST_W), lambda i: (i, 0))],
        out_shape=[_sds((S, QKV_W), CDT), _sds((S, REST_W), F32)],
        compiler_params=_cp(("parallel",), vmem=56),
    )(h, w)


def _after(dep):
    return ([], []) if dep is None else ([pl.BlockSpec(memory_space=pl.ANY)], [dep])


def _dh(dqkv, drest, w, dep=None):
    S = dqkv.shape[0]
    tm = 256
    dep_specs, dep_args = _after(dep)

    def body(a_ref, b_ref, w_ref, *rest):
        o_ref = rest[-1]
        acc = _dot(a_ref[:, 0:512], _w512(w_ref, 0), NT)
        for j in range(1, 9):
            acc = acc + _dot(a_ref[:, 512 * j:512 * (j + 1)], _w512(w_ref, j), NT)
        off = 0
        for lo, hi in REST_COLS:
            for j in range(lo, hi):
                acc = acc + _dot(b_ref[:, off:off + 512], _w512(w_ref, j), NT)
                off += 512
        o_ref[...] = acc

    return pl.pallas_call(
        body, name="dh", grid=(S // tm,),
        in_specs=[pl.BlockSpec((tm, QKV_W), lambda i: (i, 0)), pl.BlockSpec((tm, REST_W), lambda i: (i, 0)),
                  _resident((8, D, 1024))] + dep_specs,
        out_specs=pl.BlockSpec((tm, D), lambda i: (i, 0)),
        out_shape=_sds((S, D), F32),
        compiler_params=_cp(("parallel",), vmem=56),
    )(dqkv, drest, w, *dep_args)


def _mm_tn(at, b, *, n_total, out_blk, name, prev=None, tn=512):
    R, S = at.shape
    nb = b.shape[1] // tn

    def body(*refs):
        refs[-1][...] = _dot(refs[0][...], refs[1][...], NN)

    in_specs = [_resident((R, S)), pl.BlockSpec((S, tn), lambda j: (0, j))]
    args = [at, b]
    aliases = {}
    if prev is not None:
        in_specs.append(pl.BlockSpec(memory_space=pl.ANY))
        args.append(prev)
        aliases = {2: 0}
    return pl.pallas_call(
        body, name=name, grid=(nb,), in_specs=in_specs,
        out_specs=pl.BlockSpec((R, tn), lambda j: (0, out_blk(j))),
        out_shape=_sds((R, n_total), F32),
        input_output_aliases=aliases,
        compiler_params=_cp(("parallel",)),
    )(*args)


def _tcast(a, name):
    M, C = a.shape
    tm = 512

    def body(a_ref, o_ref):
        o_ref[...] = a_ref[...].astype(F32).T.astype(o_ref.dtype)

    return pl.pallas_call(
        body, name=name, grid=(M // tm,),
        in_specs=[pl.BlockSpec((tm, C), lambda i: (i, 0))],
        out_specs=pl.BlockSpec((C, tm), lambda i: (0, i)),
        out_shape=_sds((C, M), CDT), compiler_params=_cp(("parallel",)),
    )(a)


TOK = R16 * R16


def _swap16():
    i = lax.broadcasted_iota(jnp.int32, (TOK, TOK), 0)
    j = lax.broadcasted_iota(jnp.int32, (TOK, TOK), 1)
    return jnp.where(j == jnp.bitwise_and(i, 15) * 16 + jnp.right_shift(i, 4), 1.0, 0.0).astype(CDT)


def _norm_mod(x, norm_g, shift, scale, dep=None):
    S = x.shape[0]
    dep_specs, dep_args = _after(dep)

    def body(x_ref, g_ref, sh_ref, sc_ref, *rest):
        xv = x_ref[...]
        xn = xv * lax.rsqrt(jnp.mean(xv * xv, axis=-1, keepdims=True) + EPS)
        h = (xn * (g_ref[...] * (1.0 + sc_ref[...])) + sh_ref[...]).astype(CDT)
        _st(rest[-1], _dot(_swap16(), h, NN))

    vec = pl.BlockSpec((1, D), lambda i: (0, 0))
    return pl.pallas_call(
        body, name="norm_mod", grid=(S // TOK,),
        in_specs=[pl.BlockSpec((TOK, D), lambda i: (i, 0)), vec, vec, vec] + dep_specs,
        out_specs=pl.BlockSpec((R16, R16, D), lambda i: (0, i, 0)),
        out_shape=_sds((R16, S // R16, D), CDT), compiler_params=_cp(("parallel",)),
    )(x, norm_g, shift, scale, *dep_args)


def _t5_bucket_np(n):
    nf = np.maximum(n, 1).astype(np.float32)
    large = 16 + (np.log(nf / np.float32(16)) / np.float32(math.log(2048 / 16)) * np.float32(16)).astype(np.int32)
    large = np.minimum(large, 31)
    return np.where(n < 16, n, large).astype(np.int32)


class _Geom:
    def __init__(self, gi, LS):
        self.gi, self.LS = gi, LS
        self.dil = (1, 4, 16)[gi]
        if gi == 0:
            self.lead, self.rows, self.n_tiles, self.has_prev = (R16,), 16, LS // 16, True
            a = np.arange(R16)[:, None]
            ll = np.arange(16)[None, :]
            self.pos = (16 * ll + a).reshape(-1)
        elif gi == 1:
            self.lead, self.rows, self.n_tiles, self.has_prev = (4,), 64, LS // 64, True
            a = np.arange(4)[:, None]
            ll = np.arange(64)[None, :]
            self.pos = (4 * ll + a).reshape(-1)
        else:
            self.lead, self.rows, self.n_tiles, self.has_prev = (), LS, 1, False
            self.pos = np.arange(LS)
        self.T = self.pos.shape[0]
        self.Tk = 2 * self.T if self.has_prev else self.T
        self.outer = {0: (), 1: (4,), 2: (R16,)}[gi]

    def view(self, arr):
        S, C = arr.shape
        if self.gi == 1:
            return arr.reshape(4, 4, self.LS, C)
        return arr.reshape(R16, self.LS, C)

    def view_shape(self, C):
        return (4, 4, self.LS, C) if self.gi == 1 else (R16, self.LS, C)

    def unview(self, arr):
        return arr.reshape(R16 * self.LS, arr.shape[-1])

    def spec(self, W, cblk, tile):
        if self.gi == 0:
            return pl.BlockSpec((R16, 16, W), lambda n: (0, tile(n), cblk))
        if self.gi == 1:
            return pl.BlockSpec((4, None, 64, W), lambda r, n: (0, r, tile(n), cblk))
        return pl.BlockSpec((None, self.LS, W), lambda r: (r, 0, cblk))

    def halves(self):
        R = self.rows
        idx = np.arange(self.T).reshape(-1, R)
        return idx[:, :R // 2].reshape(-1), idx[:, R // 2:].reshape(-1)

    def bucket_index(self, sub):
        lo, hi = self.halves()
        pos = self.pos
        if sub == 0:
            pq = pos[lo]
            pk = np.concatenate([pos[hi] - self.T, pos[lo]]) if self.has_prev else pos[lo]
        else:
            pq = pos[hi]
            pk = np.concatenate([pos[lo], pos[hi]])
        dist = pq[:, None] - pk[None, :]
        ok = (dist >= 0) & (dist <= N_BACK)
        bucket = _t5_bucket_np(np.clip(dist, 0, N_BACK) * self.dil)
        return np.where(ok, bucket, -1).astype(np.int32)


def _bias_table(idx, rel_bias, gi, sub, dep=None):
    T, Tk = idx.shape
    tr = 32
    dep_specs, dep_args = _after(dep)

    def body(idx_ref, rb_ref, *rest):
        h = pl.program_id(0)
        ix = idx_ref[...]
        acc = jnp.full(ix.shape, NEG, F32)
        for b in range(32):
            acc = jnp.where(ix == b, rb_ref[b, gi * NH + h], acc)
        rest[-1][...] = acc

    return pl.pallas_call(
        body, name=f"bias_table{gi}{'ab'[sub]}", grid=(NH, T // tr),
        in_specs=[pl.BlockSpec((tr, Tk), lambda h, i: (i, 0)), pl.BlockSpec(memory_space=pltpu.SMEM)] + dep_specs,
        out_specs=pl.BlockSpec((None, tr, Tk), lambda h, i: (h, i, 0)),
        out_shape=_sds((NH, T, Tk), F32), compiler_params=_cp(("parallel", "parallel")),
    )(idx, rel_bias, *dep_args)


def _bias_grad(dsum, idx, gi, sub, dep=None):
    _, T, Tk = dsum.shape
    dep_specs, dep_args = _after(dep)

    def body(ds_ref, idx_ref, *rest):
        o_ref, r_ref = rest[-2], rest[-1]

        def step(b, c):
            r_ref[pl.ds(b, 1), :] = jnp.sum(jnp.where(idx_ref[...] == b, ds_ref[...], 0.0), axis=0, keepdims=True)
            return c

        lax.fori_loop(0, 32, step, 0)
        o_ref[...] = jnp.broadcast_to(jnp.sum(r_ref[...], axis=1, keepdims=True), (32, 128))

    out = pl.pallas_call(
        body, name=f"bias_grad{gi}{'ab'[sub]}", grid=(NH,),
        in_specs=[pl.BlockSpec((None, T, Tk), lambda h: (h, 0, 0)), pl.BlockSpec((T, Tk), lambda h: (0, 0))] + dep_specs,
        out_specs=pl.BlockSpec((None, 32, 128), lambda h: (h, 0, 0)),
        out_shape=_sds((NH, 32, 128), F32), scratch_shapes=[pltpu.VMEM((32, Tk), F32)],
        compiler_params=_cp(("parallel",)),
    )(dsum, idx, *dep_args)
    return out[:, :, 0].T


def _prev_mask(n, T, Tk):
    col = lax.broadcasted_iota(jnp.int32, (1, Tk), 1)
    return jnp.where((col < T) & (n == 0), NEG, 0.0).astype(F32)


def _split_rows(v, R):
    A, half = v.shape[0] // R, R // 2
    via_f32 = v.dtype != F32 and half % 16 != 0
    w = v.astype(F32) if via_f32 else v
    lo = jnp.concatenate([w[a * R:a * R + half] for a in range(A)], axis=0)
    hi = jnp.concatenate([w[a * R + half:(a + 1) * R] for a in range(A)], axis=0)
    return (lo.astype(v.dtype), hi.astype(v.dtype)) if via_f32 else (lo, hi)


def _merge_rows(lo, hi, R):
    half = R // 2
    A = lo.shape[0] // half
    return jnp.concatenate([x[a * half:(a + 1) * half] for a in range(A) for x in (lo, hi)], axis=0)


def _attn_fwd(qkv, biases, geom):
    S = qkv.shape[0]
    gi, T, R, has_prev = geom.gi, geom.T, geom.rows, geom.has_prev
    H2 = T // 2
    qv = geom.view(qkv)
    cq, ck, cv = 3 * gi, 3 * gi + 1, 3 * gi + 2

    def body(*refs):
        if has_prev:
            q_ref, kp_ref, kc_ref, vp_ref, vc_ref, ba_ref, bb_ref, o_ref, l_ref = refs
        else:
            q_ref, kc_ref, vc_ref, ba_ref, bb_ref, o_ref, l_ref = refs
        q_lo, q_hi = _split_rows(_ld(q_ref), R)
        kc_lo, kc_hi = _split_rows(_ld(kc_ref), R)
        vc_lo, vc_hi = _split_rows(_ld(vc_ref), R)
        k_b = jnp.concatenate([kc_lo, kc_hi], axis=0)
        v_b = jnp.concatenate([vc_lo, vc_hi], axis=0)
        if has_prev:
            n = pl.program_id(len(geom.outer))
            k_a = jnp.concatenate([_split_rows(_ld(kp_ref), R)[1], kc_lo], axis=0)
            v_a = jnp.concatenate([_split_rows(_ld(vp_ref), R)[1], vc_lo], axis=0)
            pm = _prev_mask(n, H2, T)
        else:
            k_a, v_a, pm = kc_lo, vc_lo, None
        for h in range(NH):
            sl = slice(HD * h, HD * h + HD)
            halves = []
            for q, k, v, b_ref, mask in ((q_lo, k_a, v_a, ba_ref, pm), (q_hi, k_b, v_b, bb_ref, None)):
                qs = (q[:, sl].astype(F32) * 0.125).astype(q.dtype)
                s = _dot(qs, k[:, sl], NT) + b_ref[h]
                if mask is not None:
                    s = s + mask
                m = jnp.max(s, axis=1, keepdims=True)
                p = jnp.exp(s - m)
                l = jnp.sum(p, axis=1, keepdims=True)
                halves.append((_dot(p.astype(v.dtype), v[:, sl], NN) / l, m + jnp.log(l)))
            _st(o_ref, _merge_rows(halves[0][0], halves[1][0], R), sl)
            _st(l_ref, jnp.broadcast_to(_merge_rows(halves[0][1], halves[1][1], R), (T, HD)), sl)

    cur = lambda n: n
    prev = lambda n: jnp.maximum(n - 1, 0)
    if has_prev:
        in_specs = [geom.spec(AW, cq, cur), geom.spec(AW, ck, prev), geom.spec(AW, ck, cur),
                    geom.spec(AW, cv, prev), geom.spec(AW, cv, cur)]
        args = [qv] * 5
    else:
        in_specs = [geom.spec(AW, cq, cur), geom.spec(AW, ck, cur), geom.spec(AW, cv, cur)]
        args = [qv] * 3
    nd = len(geom.outer) + (1 if has_prev else 0)
    in_specs += [pl.BlockSpec(b.shape, lambda *ids: (0, 0, 0)) for b in biases]
    grid = geom.outer + ((geom.n_tiles,) if has_prev else ())
    oshape = geom.view_shape(AW)
    o, l = pl.pallas_call(
        body, name=f"attn_fwd{gi}", grid=grid, in_specs=in_specs,
        out_specs=[geom.spec(AW, 0, cur), geom.spec(AW, 0, cur)],
        out_shape=[_sds(oshape, F32), _sds(oshape, F32)],
        compiler_params=_cp(("arbitrary",) * nd),
    )(*args, *biases)
    return geom.unview(o), geom.unview(l)


def _attn_combine(os_, ls_, rest):
    S = rest.shape[0]
    tm = 512

    def body(o0, o1, o2, l0, l1, l2, z_ref, attn_ref, L_ref, ag_ref, agt_ref):
        la, lb, lc = l0[...], l1[...], l2[...]
        m = jnp.maximum(jnp.maximum(la, lb), lc)
        ea, eb, ec = jnp.exp(la - m), jnp.exp(lb - m), jnp.exp(lc - m)
        den = ea + eb + ec
        attn = (ea * o0[...] + eb * o1[...] + ec * o2[...]) / den
        attn_ref[...] = attn
        L_ref[...] = m + jnp.log(den)
        z = z_ref[...]
        ag = attn * (z * _sigmoid(z))
        ag_ref[...] = ag.astype(ag_ref.dtype)
        agt_ref[...] = ag.T.astype(agt_ref.dtype)

    t = pl.BlockSpec((tm, AW), lambda i: (i, 0))
    return pl.pallas_call(
        body, name="attn_combine", grid=(S // tm,),
        in_specs=[t] * 6 + [pl.BlockSpec((tm, AW), lambda i: (i, 0))],
        out_specs=[t, t, t, pl.BlockSpec((AW, tm), lambda i: (0, i))],
        out_shape=[_sds((S, AW), F32), _sds((S, AW), F32), _sds((S, AW), CDT), _sds((AW, S), CDT)],
        compiler_params=_cp(("parallel",)),
    )(*os_, *ls_, rest)


def _pool_counts(r, LS, win):
    l = lax.broadcasted_iota(jnp.int32, (LS, 1), 0)
    return jnp.minimum(16 * l + (r + 1), win).astype(F32)


def _pool_fwd(rest, pool_w, pool_scale):
    S = rest.shape[0]
    LS = S // R16
    r3 = rest.reshape(R16, LS, REST_W)
    PG = 128

    def body(u_ref, z_ref, pw_ref, ps_ref, pt_ref, mp_ref, pg_ref, pgt_ref, d_ref):
        g = pl.program_id(0)
        win = jnp.left_shift(2, g)
        row = lax.broadcasted_iota(jnp.int32, (LS, PG), 0)
        for r in range(1, R16):
            d_ref[r] = jnp.where(row == 0, 0.0, pltpu.roll(u_ref[r], 1, 0))
        pw = pw_ref[...].astype(CDT)
        for r in range(R16):
            u = u_ref[r]
            acc = u
            for j in range(1, R16):
                coef = jnp.where(j < win, 1.0, 0.0).astype(F32)
                src = u_ref[r - j] if r >= j else d_ref[r - j + R16]
                acc = acc + coef * src
            pooled = acc / _pool_counts(r, LS, win) - u
            pc = pooled.astype(CDT)
            mp = _dot(pc, pw, NN)
            z = z_ref[r]
            pg = (mp * ps_ref[...]) * (z * _sigmoid(z))
            mp_ref[r] = mp
            pg_ref[r] = pg.astype(pg_ref.dtype)
            pt_ref[:, r * LS:(r + 1) * LS] = pooled.T.astype(pt_ref.dtype)
            pgt_ref[:, r * LS:(r + 1) * LS] = pg.T.astype(pgt_ref.dtype)

    col = lambda off: pl.BlockSpec((R16, LS, PG), lambda g: (0, 0, off + g))
    tsp = pl.BlockSpec((PG, S), lambda g: (g, 0))
    pt, mp, pg, pgt = pl.pallas_call(
        body, name="pool_fwd", grid=(4,),
        in_specs=[col(24), col(4), pl.BlockSpec((None, PG, PG), lambda g: (g, 0, 0)),
                  pl.BlockSpec((1, PG), lambda g: (0, g))],
        out_specs=[tsp, col(0), col(0), tsp],
        out_shape=[_sds((AW, S), CDT), _sds((R16, LS, AW), F32), _sds((R16, LS, AW), CDT), _sds((AW, S), CDT)],
        scratch_shapes=[pltpu.VMEM((R16, LS, PG), F32)],
        compiler_params=_cp(("parallel",)),
    )(r3, r3, pool_w, pool_scale)
    return pt, mp.reshape(S, AW), pg.reshape(S, AW), pgt


def _branch_merge(ag, pg, wab, wpb, rest):
    S = ag.shape[0]
    tm, tn = 512, 512

    def body(ag_ref, pg_ref, wa_ref, wp_ref, ga_ref, gp_ref, m_ref, mt_ref):
        ya = _dot(ag_ref[...], wa_ref[...], NN)
        yp = _dot(pg_ref[...], wp_ref[...], NN)
        mg = _sigmoid(ga_ref[...]) * ya + _sigmoid(gp_ref[...]) * yp
        m_ref[...] = mg.astype(m_ref.dtype)
        mt_ref[...] = mg.T.astype(mt_ref.dtype)

    a = pl.BlockSpec((tm, AW), lambda i, j: (i, 0))
    w = pl.BlockSpec((AW, tn), lambda i, j: (0, j))
    o = pl.BlockSpec((tm, tn), lambda i, j: (i, j))
    return pl.pallas_call(
        body, name="branch_merge", grid=(S // tm, D // tn),
        in_specs=[a, a, w, w, pl.BlockSpec((tm, tn), lambda i, j: (i, 2 + j)),
                  pl.BlockSpec((tm, tn), lambda i, j: (i, 4 + j))],
        out_specs=[o, pl.BlockSpec((tn, tm), lambda i, j: (j, i))],
        out_shape=[_sds((S, D), CDT), _sds((D, S), CDT)],
        compiler_params=_cp(("parallel", "parallel")),
    )(ag, pg, wab, wpb, rest, rest)


def _out_loss(merged, wout, x, target, gate, final_g):
    S = x.shape[0]

    def body(m_ref, w_ref, x_ref, t_ref, gate_ref, fg_ref, dx2_ref, dmo_ref, loss_ref, gfg_ref, dgate_ref):
        i = pl.program_id(0)

        @pl.when(i == 0)
        def _():
            loss_ref[...] = jnp.zeros_like(loss_ref)
            gfg_ref[...] = jnp.zeros_like(gfg_ref)
            dgate_ref[...] = jnp.zeros_like(dgate_ref)

        swap = _swap16()
        mo = _dot(_dot(swap, _ld(m_ref), NN).astype(CDT), w_ref[...], NN)
        gate, fg = gate_ref[...], fg_ref[...]
        x2 = x_ref[...] + gate * mo
        r2 = lax.rsqrt(jnp.mean(x2 * x2, axis=-1, keepdims=True) + EPS)
        xn2 = x2 * r2
        diff = xn2 * fg - t_ref[...]
        loss = jnp.sum(jnp.sum(diff * diff, axis=-1, keepdims=True), axis=0, keepdims=True)
        dy = diff * (1.0 / D)
        dxn2 = dy * fg
        dx2 = r2 * (dxn2 - xn2 * jnp.mean(dxn2 * xn2, axis=-1, keepdims=True))
        dx2_ref[...] = dx2
        _st(dmo_ref, _dot(swap, (dx2 * gate).astype(CDT), NN))
        loss_ref[...] += jnp.broadcast_to(loss * (0.5 / D), loss_ref.shape)
        gfg_ref[...] += jnp.sum(dy * xn2, axis=0, keepdims=True)
        dgate_ref[...] += jnp.sum(dx2 * mo, axis=0, keepdims=True)

    slab = pl.BlockSpec((R16, R16, D), lambda i: (0, i, 0))
    nat = pl.BlockSpec((TOK, D), lambda i: (i, 0))
    vec = pl.BlockSpec((1, D), lambda i: (0, 0))
    return pl.pallas_call(
        body, name="out_loss", grid=(S // TOK,),
        in_specs=[slab, _resident((D, D)), nat, nat, vec, vec],
        out_specs=[nat, slab, pl.BlockSpec((1, 128), lambda i: (0, 0)), vec, vec],
        out_shape=[_sds((S, D), F32), _sds((R16, S // R16, D), CDT), _sds((1, 128), F32), _sds((1, D), F32),
                   _sds((1, D), F32)],
        compiler_params=_cp(("arbitrary",)),
    )(merged, wout, x, target, gate, final_g)


def _bwd_gates(dmo, wout, wab, wpb, rest, ag, pg, attn, mp, pool_scale):
    S = dmo.shape[0]
    tm = 256

    def body(dmo_ref, wo_ref, wa_ref, wp_ref, za_ref, zp_ref, ga0, ga1, gp0, gp1, ag_ref, pg_ref, at_ref, mp_ref,
             ps_ref, dya_ref, dyp_ref, dr_ref, dat_ref, dl_ref, dmp_ref, dps_ref):
        i = pl.program_id(0)

        @pl.when(i == 0)
        def _():
            dps_ref[...] = jnp.zeros_like(dps_ref)

        dm = _dot(dmo_ref[...], wo_ref[...], NT)
        sa = _sigmoid(jnp.concatenate([ga0[...], ga1[...]], axis=1))
        sp = _sigmoid(jnp.concatenate([gp0[...], gp1[...]], axis=1))
        dya = (dm * sa).astype(CDT)
        dyp = (dm * sp).astype(CDT)
        dya_ref[...] = dya
        dyp_ref[...] = dyp
        ya = _dot(ag_ref[...], wa_ref[...], NN)
        yp = _dot(pg_ref[...], wp_ref[...], NN)
        dr_ref[:, 1024:2048] = (dm * ya * sa * (1.0 - sa)).astype(dr_ref.dtype)
        dr_ref[:, 2048:3072] = (dm * yp * sp * (1.0 - sp)).astype(dr_ref.dtype)
        da = _dot(dya, wa_ref[...], NT)
        dp = _dot(dyp, wp_ref[...], NT)
        za = za_ref[...]
        sga = _sigmoid(za)
        attn = at_ref[...]
        dattn = da * (za * sga)
        dr_ref[:, 0:512] = (da * attn * (sga * (1.0 + za * (1.0 - sga)))).astype(dr_ref.dtype)
        dat_ref[...] = dattn.astype(dat_ref.dtype)
        prod = dattn * attn
        for h in range(NH):
            sl = slice(HD * h, HD * h + HD)
            dl_ref[:, sl] = jnp.broadcast_to(jnp.sum(prod[:, sl], axis=1, keepdims=True), (tm, HD))
        zp = zp_ref[...]
        sgp = _sigmoid(zp)
        mpre = mp_ref[...]
        ps = ps_ref[...]
        dmixed = dp * (zp * sgp)
        dr_ref[:, 512:1024] = (dp * (mpre * ps) * (sgp * (1.0 + zp * (1.0 - sgp)))).astype(dr_ref.dtype)
        dps_ref[...] += jnp.sum(dmixed * mpre, axis=0, keepdims=True)
        dmp_ref[...] = (dmixed * ps).astype(dmp_ref.dtype)

    full = lambda shape: pl.BlockSpec(shape, lambda i: (0, 0))
    tD = pl.BlockSpec((tm, D), lambda i: (i, 0))
    tA = pl.BlockSpec((tm, AW), lambda i: (i, 0))
    rb = lambda c: pl.BlockSpec((tm, AW), lambda i: (i, c))
    return pl.pallas_call(
        body, name="bwd_gates", grid=(S // tm,),
        in_specs=[tD, full((D, D)), full((AW, D)), full((AW, D)), rb(0), rb(1), rb(2), rb(3), rb(4), rb(5),
                  tA, tA, tA, tA, full((1, AW))],
        out_specs=[tD, tD, pl.BlockSpec((tm, 3072), lambda i: (i, 0)), tA, tA, tA, full((1, AW))],
        out_shape=[_sds((S, D), CDT), _sds((S, D), CDT), _sds((S, REST_W), CDT), _sds((S, AW), CDT),
                   _sds((S, AW), F32), _sds((S, AW), CDT), _sds((1, AW), F32)],
        compiler_params=_cp(("arbitrary",), vmem=56),
    )(dmo, wout, wab, wpb, rest, rest, rest, rest, rest, rest, ag, pg, attn, mp, pool_scale)


def _pool_bwd(dmp, pt, pool_w, drest):
    S = dmp.shape[0]
    LS = S // R16
    PG = 128
    d3 = dmp.reshape(R16, LS, AW)
    dr3 = drest.reshape(R16, LS, REST_W)

    def body(d_ref, pt_ref, pw_ref, dr_in, du_ref, dpw_ref, g_ref, s_ref):
        del dr_in
        g = pl.program_id(0)
        win = jnp.left_shift(2, g)
        row = lax.broadcasted_iota(jnp.int32, (LS, PG), 0)
        pw = pw_ref[...].astype(CDT)
        dpw = jnp.zeros((PG, PG), F32)
        for r in range(R16):
            dm = d_ref[r]
            dpw = dpw + _dot(pt_ref[:, r * LS:(r + 1) * LS], dm, NN)
            dpooled = _dot(dm, pw, NT)
            g_ref[r] = dpooled
            s_ref[r] = dpooled / _pool_counts(r, LS, win)
        dpw_ref[...] = dpw
        for r in range(R16):
            acc = s_ref[r]
            for j in range(1, R16):
                coef = jnp.where(j < win, 1.0, 0.0).astype(F32)
                if r + j < R16:
                    src = s_ref[r + j]
                else:
                    src = jnp.where(row == LS - 1, 0.0, pltpu.roll(s_ref[r + j - R16], LS - 1, 0))
                acc = acc + coef * src
            du_ref[r] = (acc - g_ref[r]).astype(du_ref.dtype)

    du, dpw = pl.pallas_call(
        body, name="pool_bwd", grid=(4,),
        in_specs=[pl.BlockSpec((R16, LS, PG), lambda g: (0, 0, g)), pl.BlockSpec((PG, S), lambda g: (g, 0)),
                  pl.BlockSpec((None, PG, PG), lambda g: (g, 0, 0)), pl.BlockSpec(memory_space=pl.ANY)],
        out_specs=[pl.BlockSpec((R16, LS, PG), lambda g: (0, 0, 24 + g)),
                   pl.BlockSpec((None, PG, PG), lambda g: (g, 0, 0))],
        out_shape=[_sds((R16, LS, REST_W), CDT), _sds((4, PG, PG), F32)],
        scratch_shapes=[pltpu.VMEM((R16, LS, PG), F32), pltpu.VMEM((R16, LS, PG), F32)],
        input_output_aliases={3: 0},
        compiler_params=_cp(("parallel",)),
    )(d3, pt, pool_w, dr3)
    return du.reshape(S, REST_W), dpw


def _attn_bwd(qkv, dattn, Lb, dlb, biases, geom, dqkv_prev):
    S = qkv.shape[0]
    gi, T, R, has_prev, N = geom.gi, geom.T, geom.rows, geom.has_prev, geom.n_tiles
    H2 = T // 2
    nd_outer = len(geom.outer)
    qv = geom.view(qkv)
    cq, ck, cv = 3 * gi, 3 * gi + 1, 3 * gi + 2
    W3 = 3 * AW

    def heads(q_ref, kp_ref, kc_ref, vp_ref, vc_ref, do_ref, L_ref, dl_ref, ba_ref, bb_ref, pm, dsa_ref, dsb_ref, emit):
        q_lo, q_hi = _split_rows(_ld(q_ref), R)
        do_lo, do_hi = _split_rows(_ld(do_ref), R)
        L_lo, L_hi = _split_rows(_ld(L_ref), R)
        dl_lo, dl_hi = _split_rows(_ld(dl_ref), R)
        kc_lo, kc_hi = _split_rows(_ld(kc_ref), R)
        vc_lo, vc_hi = _split_rows(_ld(vc_ref), R)
        k_b = jnp.concatenate([kc_lo, kc_hi], axis=0)
        v_b = jnp.concatenate([vc_lo, vc_hi], axis=0)
        if has_prev:
            k_a = jnp.concatenate([_split_rows(_ld(kp_ref), R)[1], kc_lo], axis=0)
            v_a = jnp.concatenate([_split_rows(_ld(vp_ref), R)[1], vc_lo], axis=0)
        else:
            k_a, v_a = kc_lo, vc_lo
        for h in range(NH):
            sl = slice(HD * h, HD * h + HD)
            col = slice(HD * h, HD * h + 1)
            res = []
            for q, k, v, do, L, dl, b_ref, mask, ds_ref in (
                    (q_lo, k_a, v_a, do_lo, L_lo, dl_lo, ba_ref, pm, dsa_ref),
                    (q_hi, k_b, v_b, do_hi, L_hi, dl_hi, bb_ref, None, dsb_ref)):
                qs = (q[:, sl].astype(F32) * 0.125).astype(q.dtype)
                kh, vh, doh = k[:, sl], v[:, sl], do[:, sl]
                s = _dot(qs, kh, NT) + b_ref[h]
                if mask is not None:
                    s = s + mask
                p = jnp.exp(s - L[:, col])
                ds = p * (_dot(doh, vh, NT) - dl[:, col])
                ds_ref[h] += ds
                dsc = ds.astype(q.dtype)
                res.append((_dot(dsc, kh, NN) * 0.125, _dot(dsc, qs, TN), _dot(p.astype(q.dtype), doh, TN)))
            (dq_lo, dk_a, dv_a), (dq_hi, dk_b, dv_b) = res
            dq = _merge_rows(dq_lo, dq_hi, R)
            if has_prev:
                emit(sl, dq, _merge_rows(dk_a[H2:] + dk_b[:H2], dk_b[H2:], R),
                     _merge_rows(dv_a[H2:] + dv_b[:H2], dv_b[H2:], R), dk_a[:H2], dv_a[:H2])
            else:
                emit(sl, dq, _merge_rows(dk_a + dk_b[:H2], dk_b[H2:], R),
                     _merge_rows(dv_a + dv_b[:H2], dv_b[H2:], R), None, None)

    def first_step():
        ok = pl.program_id(nd_outer) == 0 if has_prev else pl.program_id(0) == 0
        for a in range(nd_outer if has_prev else 0):
            ok = ok & (pl.program_id(a) == 0)
        return ok

    if has_prev:
        def body(q_ref, kp_ref, kc_ref, vp_ref, vc_ref, do_ref, L_ref, dl_ref, ba_ref, bb_ref, prev_in, out_ref,
                 dsa_ref, dsb_ref, hold, new, pp):
            del prev_in
            s_id = pl.program_id(nd_outer)

            @pl.when(first_step())
            def _():
                dsa_ref[...] = jnp.zeros_like(dsa_ref)
                dsb_ref[...] = jnp.zeros_like(dsb_ref)

            @pl.when(s_id < N)
            def _():
                def emit(sl, dq, dk, dv, dk_prev, dv_prev):
                    new[:, sl] = dq
                    new[:, slice(AW + sl.start, AW + sl.stop)] = dk
                    new[:, slice(2 * AW + sl.start, 2 * AW + sl.stop)] = dv
                    pp[:, sl] = dk_prev
                    pp[:, slice(AW + sl.start, AW + sl.stop)] = dv_prev

                heads(q_ref, kp_ref, kc_ref, vp_ref, vc_ref, do_ref, L_ref, dl_ref, ba_ref, bb_ref,
                      _prev_mask(s_id, H2, T), dsa_ref, dsb_ref, emit)

            @pl.when((s_id >= 1) & (s_id < N))
            def _():
                _st(out_ref, hold[:, 0:AW], slice(0, AW))
                late = _merge_rows(jnp.zeros((H2, 2 * AW), F32), pp[...], R)
                _st(out_ref, hold[:, AW:W3] + late, slice(AW, W3))

            @pl.when(s_id == N)
            def _():
                _st(out_ref, hold[...])

            @pl.when(s_id < N)
            def _():
                hold[...] = new[...]

        cur = lambda n: jnp.minimum(n, N - 1)
        prev = lambda n: jnp.clip(n - 1, 0, N - 1)
        delayed = lambda n: jnp.maximum(n - 1, 0)
        in_specs = [geom.spec(AW, cq, cur), geom.spec(AW, ck, prev), geom.spec(AW, ck, cur),
                    geom.spec(AW, cv, prev), geom.spec(AW, cv, cur),
                    geom.spec(AW, 0, cur), geom.spec(AW, 0, cur), geom.spec(AW, 0, cur)]
        args = [qv] * 5 + [geom.view(dattn), geom.view(Lb), geom.view(dlb)]
        out_spec = geom.spec(W3, gi, delayed)
        grid = geom.outer + (N + 1,)
        scratch = [pltpu.VMEM((T, W3), F32), pltpu.VMEM((T, W3), F32), pltpu.VMEM((H2, 2 * AW), F32)]
    else:
        def body(q_ref, kc_ref, vc_ref, do_ref, L_ref, dl_ref, ba_ref, bb_ref, prev_in, out_ref, dsa_ref, dsb_ref):
            del prev_in

            @pl.when(first_step())
            def _():
                dsa_ref[...] = jnp.zeros_like(dsa_ref)
                dsb_ref[...] = jnp.zeros_like(dsb_ref)

            def emit(sl, dq, dk, dv, dk_prev, dv_prev):
                _st(out_ref, dq, sl)
                _st(out_ref, dk, slice(AW + sl.start, AW + sl.stop))
                _st(out_ref, dv, slice(2 * AW + sl.start, 2 * AW + sl.stop))

            heads(q_ref, None, kc_ref, None, vc_ref, do_ref, L_ref, dl_ref, ba_ref, bb_ref, None, dsa_ref, dsb_ref,
                  emit)

        cur = lambda n: n
        in_specs = [geom.spec(AW, cq, cur), geom.spec(AW, ck, cur), geom.spec(AW, cv, cur),
                    geom.spec(AW, 0, cur), geom.spec(AW, 0, cur), geom.spec(AW, 0, cur)]
        args = [qv] * 3 + [geom.view(dattn), geom.view(Lb), geom.view(dlb)]
        out_spec = geom.spec(W3, gi, cur)
        grid = geom.outer
        scratch = []

    whole = lambda b: pl.BlockSpec(b.shape, lambda *ids: (0, 0, 0))
    in_specs += [whole(b) for b in biases]
    in_specs.append(pl.BlockSpec(memory_space=pl.ANY))
    if dqkv_prev is None:
        last, aliases = biases[0], {}
    else:
        last, aliases = geom.view(dqkv_prev), {len(in_specs) - 1: 0}
    dq, dsa, dsb = pl.pallas_call(
        body, name=f"attn_bwd{gi}", grid=grid, in_specs=in_specs,
        out_specs=[out_spec] + [whole(b) for b in biases],
        out_shape=[_sds(geom.view_shape(QKV_W), CDT)] + [_sds(b.shape, F32) for b in biases],
        scratch_shapes=scratch,
        input_output_aliases=aliases,
        compiler_params=_cp(("arbitrary",) * len(grid), vmem=56),
    )(*args, *biases, last)
    return geom.unview(dq), [dsa, dsb]


def _h_bwd(dh, dx2, x, norm_g, scale):
    S = x.shape[0]

    def body(dh_ref, dx2_ref, x_ref, g_ref, sc_ref, gx_ref, dsh_ref, dsc_ref, dng_ref, nat_ref):
        i = pl.program_id(0)

        @pl.when(i == 0)
        def _():
            dsh_ref[...] = jnp.zeros_like(dsh_ref)
            dsc_ref[...] = jnp.zeros_like(dsc_ref)
            dng_ref[...] = jnp.zeros_like(dng_ref)

        for r in range(R16):
            nat_ref[:, r, :] = dh_ref[r]
        dh = jnp.concatenate([nat_ref[l] for l in range(R16)], axis=0)
        g = g_ref[...]
        one_sc = 1.0 + sc_ref[...]
        xv = x_ref[...]
        r1 = lax.rsqrt(jnp.mean(xv * xv, axis=-1, keepdims=True) + EPS)
        xn = xv * r1
        dhx = dh * xn
        dxn = dh * (g * one_sc)
        dx1 = r1 * (dxn - xn * jnp.mean(dxn * xn, axis=-1, keepdims=True))
        gx_ref[...] = dx2_ref[...] + dx1
        dsh_ref[...] += jnp.sum(dh, axis=0, keepdims=True)
        dsc_ref[...] += jnp.sum(dhx * g, axis=0, keepdims=True)
        dng_ref[...] += jnp.sum(dhx * one_sc, axis=0, keepdims=True)

    nat = pl.BlockSpec((TOK, D), lambda i: (i, 0))
    vec = pl.BlockSpec((1, D), lambda i: (0, 0))
    return pl.pallas_call(
        body, name="h_bwd", grid=(S // TOK,),
        in_specs=[pl.BlockSpec((R16, R16, D), lambda i: (0, i, 0)), nat, nat, vec, vec],
        out_specs=[nat, vec, vec, vec],
        out_shape=[_sds((S, D), F32), _sds((1, D), F32), _sds((1, D), F32), _sds((1, D), F32)],
        scratch_shapes=[pltpu.VMEM((R16, R16, D), F32)],
        compiler_params=_cp(("arbitrary",)),
    )(dh, dx2, x, norm_g, scale)


def _local_step(x, target, shift, scale, gate, norm_g, w_in_full, pool_w, pool_scale, wab, wpb, wout, rel_bias,
                final_g):
    st = _local_pre(x, shift, scale, norm_g, rel_bias)
    st = _local_grads(st, target, gate, w_in_full, pool_w, pool_scale, wab, wpb, wout, final_g)
    d_rel_bias = _local_bias_grads(st)
    gx, dmod, dng = _local_input_grads(st, w_in_full, norm_g, scale)
    return dict(loss=st["loss"], grad_x=gx, dw_in=st["dw_in"], dw_ab=st["dw_ab"], dw_pb=st["dw_pb"],
                dw_out=st["dw_out"], d_norm_g=dng, d_pool_w=st["dpw"], d_pool_scale=st["dps"],
                d_rel_bias=d_rel_bias, d_final_g=st["gfg"], dmod=dmod)


def _local_pre(x, shift, scale, norm_g, rel_bias, dep=None):
    S = x.shape[0]
    LS = S // R16
    geoms = [_Geom(gi, LS) for gi in range(3)]
    idxs = [[jnp.asarray(g.bucket_index(sub)) for sub in range(2)] for g in geoms]
    biases = [[_bias_table(idxs[gi][sub], rel_bias, gi, sub, dep) for sub in range(2)] for gi in range(3)]
    h = _norm_mod(x, norm_g, shift, scale, dep).reshape(S, D)
    return dict(geoms=geoms, idxs=idxs, biases=biases, x=x, h=h, ht=_tcast(h, "h_transpose"))


def _local_grads(st, target, gate, w_in_full, pool_w, pool_scale, wab, wpb, wout, final_g):
    geoms, idxs, biases, x, h, ht = (st[k] for k in ("geoms", "idxs", "biases", "x", "h", "ht"))
    S = h.shape[0]
    LS = S // R16
    qkv, rest = _proj(h, w_in_full)

    os_, ls_ = [], []
    for gi in range(3):
        o, l = _attn_fwd(qkv, biases[gi], geoms[gi])
        os_.append(o)
        ls_.append(l)
    attn, Lb, ag, agt = _attn_combine(os_, ls_, rest)
    pt, mp, pg, pgt = _pool_fwd(rest, pool_w, pool_scale)
    merged, mt = _branch_merge(ag, pg, wab, wpb, rest)
    dx2, dmo3, loss, gfg, dgate = _out_loss(merged.reshape(R16, LS, D), wout, x, target, gate, final_g)
    dmo = dmo3.reshape(S, D)

    dya, dyp, drest, dattn, dlb, dmp, dps = _bwd_gates(dmo, wout, wab, wpb, rest, ag, pg, attn, mp, pool_scale)
    dw_out = _mm_tn(mt, dmo, n_total=D, out_blk=lambda j: j, name="dw_out")
    dw_ab = _mm_tn(agt, dya, n_total=D, out_blk=lambda j: j, name="dw_attn_br")
    dw_pb = _mm_tn(pgt, dyp, n_total=D, out_blk=lambda j: j, name="dw_pool_br")
    drest, dpw = _pool_bwd(dmp, pt, pool_w, drest)

    dqkv = None
    dsums = []
    for gi in range(3):
        dqkv, dsum = _attn_bwd(qkv, dattn, Lb, dlb, biases[gi], geoms[gi], dqkv)
        dsums.append(dsum)

    dw_in = _mm_tn(ht, dqkv, n_total=8192, out_blk=lambda j: j, name="dw_in_qkv")
    dw_in = _mm_tn(ht, drest, n_total=8192, out_blk=_rest_blk, name="dw_in_rest", prev=dw_in)
    return dict(st, loss=loss[0, 0], dw_in=dw_in, dw_ab=dw_ab, dw_pb=dw_pb, dw_out=dw_out, dpw=dpw, dps=dps, gfg=gfg,
                dgate=dgate, dsums=dsums, dqkv=dqkv, drest=drest, dx2=dx2)


def _local_bias_grads(st, dep=None):
    drb = [_bias_grad(st["dsums"][gi][0], st["idxs"][gi][0], gi, 0, dep if gi == 0 else None) +
           _bias_grad(st["dsums"][gi][1], st["idxs"][gi][1], gi, 1) for gi in range(3)]
    return jnp.concatenate(drb, axis=1)


def _local_input_grads(st, w_in_full, norm_g, scale, dep=None):
    S = st["h"].shape[0]
    LS = S // R16
    dh = _dh(st["dqkv"], st["drest"], w_in_full, dep)
    gx, dsh, dsc, dng = _h_bwd(dh.reshape(R16, LS, D), st["dx2"], st["x"], norm_g, scale)
    return gx, jnp.concatenate([dsh, dsc, st["dgate"]], axis=1), dng


def _my_place():
    return lax.axis_index("x"), lax.axis_index("y"), lax.axis_index("c")


def _block_of(ref, axis, idx, width):
    if axis is None:
        return ref.at[idx]
    start = pl.multiple_of(idx * width, width)
    if axis == 0:
        return ref.at[pl.ds(start, width)]
    return ref.at[:, pl.ds(start, width)]


_SEM = pl.BlockSpec(memory_space=pltpu.SEMAPHORE)
_EFFECT = pltpu.SideEffectType.DATAFLOW_SIDE_EFFECTING


def _split_start(name, srcs, lands, ncopy, plan, dep=None):
    arrs = list(srcs) + list(lands)
    na, ns = len(arrs), len(srcs)
    dep_specs, dep_args = _after(dep)

    def body(*refs):
        send_sems, recv_sems, token = refs[na + len(dep_args)], refs[na + len(dep_args) + 1], refs[-1]
        for k, (s, d, dev) in enumerate(plan(refs[:ns], refs[ns:na])):
            pltpu.make_async_remote_copy(src_ref=s, dst_ref=d, send_sem=send_sems.at[k], recv_sem=recv_sems.at[k],
                                         device_id=dev, device_id_type=MESH).start()
        token[...] = jnp.zeros_like(token)

    outs = pl.pallas_call(
        body, name=name,
        out_shape=(pltpu.SemaphoreType.DMA((ncopy,)), pltpu.SemaphoreType.DMA((ncopy,)),
                   *[pltpu.HBM(a.shape, a.dtype) for a in arrs], _sds((8, 128), F32)),
        in_specs=[_HBM] * na + dep_specs,
        out_specs=(_SEM, _SEM, *[_HBM] * na, pl.BlockSpec(memory_space=pltpu.VMEM)),
        input_output_aliases={i: 2 + i for i in range(na)},
        compiler_params=pltpu.CompilerParams(has_side_effects=_EFFECT),
    )(*[pltpu.with_memory_space_constraint(a, pltpu.HBM) for a in arrs], *dep_args)
    return (outs[0], outs[1]), list(outs[2:2 + ns]), list(outs[2 + ns:2 + na]), outs[-1]


def _split_wait(name, sems, srcs, lands, plan, after):
    arrs = list(srcs) + list(lands)
    na, ns = len(arrs), len(srcs)
    after = list(after) if isinstance(after, (list, tuple)) else [after]

    def body(*refs):
        send_sems, recv_sems = refs[na], refs[na + 1]
        for k, (s, d, dev) in enumerate(plan(refs[:ns], refs[ns:na])):
            cp = pltpu.make_async_remote_copy(src_ref=s, dst_ref=d, send_sem=send_sems.at[k], recv_sem=recv_sems.at[k],
                                              device_id=dev, device_id_type=MESH)
            cp.wait_send()
            cp.wait_recv()

    outs = pl.pallas_call(
        body, name=name, out_shape=[pltpu.HBM(a.shape, a.dtype) for a in arrs],
        in_specs=[_HBM] * na + [_SEM, _SEM] + [pl.BlockSpec(memory_space=pl.ANY)] * len(after),
        out_specs=[_HBM] * na,
        input_output_aliases={i: i for i in range(na)},
        compiler_params=pltpu.CompilerParams(has_side_effects=_EFFECT),
    )(*arrs, sems[0], sems[1], *after)
    return list(outs[:ns]), list(outs[ns:])


def _gather_chips_plan(layout):
    def plan(src, land):
        x, y, c = _my_place()
        me = 4 * x + 2 * y + c
        return [(src[p], _block_of(land[p], layout[p][0], me, layout[p][1]), (cx, cy, c))
                for p in range(len(src)) for cx, cy in [(1 - x, y), (x, 1 - y), (1 - x, 1 - y)]]
    return plan


def _pair_plan(layout):
    def plan(src, land):
        x, y, c = _my_place()
        n = len(layout)
        return [(_block_of(src[p], layout[p][0], 2 * q + (1 - c), layout[p][1]), land[p].at[q], (x, y, 1 - c))
                for p in range(n) for q in range(4)] + [(src[n], land[n], (x, y, 1 - c))]
    return plan


def _chips_plan(src, land):
    x, y, c = _my_place()
    n = len(src) - 1
    chips = [(1 - x, y), (x, 1 - y), (1 - x, 1 - y)]
    return [(src[p].at[2 * cx + cy], land[p].at[2 * x + y], (cx, cy, c)) for p in range(n) for cx, cy in chips] + \
           [(src[n], land[n].at[2 * x + y], (cx, cy, c)) for cx, cy in chips]


def _pair_share(gathered):
    n = len(gathered)

    def body(*refs):
        land = refs[n:2 * n]
        send_sems, recv_sems = refs[2 * n:]
        x, y, c = _my_place()
        cps = []
        for p in range(n):
            for k in range(4):
                blk = land[p].at[2 * k + c]
                cps.append(pltpu.make_async_remote_copy(
                    src_ref=blk, dst_ref=blk, send_sem=send_sems.at[4 * p + k], recv_sem=recv_sems.at[4 * p + k],
                    device_id=(x, y, 1 - c), device_id_type=MESH))
        for cp in cps:
            cp.start()
        for cp in cps:
            cp.wait_send()
        for p in range(n):
            for k in range(4):
                blk = land[p].at[2 * k + 1 - c]
                pltpu.make_async_remote_copy(
                    src_ref=blk, dst_ref=blk, send_sem=send_sems.at[4 * p + k], recv_sem=recv_sems.at[4 * p + k],
                    device_id=(x, y, 1 - c), device_id_type=MESH).wait_recv()

    return pl.pallas_call(
        body, name="gather_pair_share", in_specs=[_HBM] * n, out_specs=[_HBM] * n,
        out_shape=[_sds(g.shape, g.dtype) for g in gathered],
        input_output_aliases={p: p for p in range(n)},
        scratch_shapes=[pltpu.SemaphoreType.DMA((4 * n,)), pltpu.SemaphoreType.DMA((4 * n,))],
    )(*gathered)


def _place_slot(arrs, slot, n_slots, name):
    n = len(arrs)

    def body(s_ref, *refs):
        for p in range(n):
            refs[n + p][...] = refs[p][...]

    def spec(a):
        tail = a.shape[-2:]
        return pl.BlockSpec((None,) + tail, lambda i, s_ref: (s_ref[0], 0, 0))

    in_specs = [spec(a) if a.ndim == 3 else pl.BlockSpec(a.shape, lambda i, s_ref: (0, 0)) for a in arrs]
    return pl.pallas_call(
        body, name=name,
        grid_spec=pltpu.PrefetchScalarGridSpec(num_scalar_prefetch=1, grid=(1,), in_specs=in_specs,
                                               out_specs=[spec(a) for a in arrs]),
        out_shape=[_sds((n_slots,) + a.shape[-2:], a.dtype) for a in arrs],
        compiler_params=_cp(("arbitrary",)),
    )(slot, *arrs)


def _cols_from_blocks(w8, name):
    _, R, C = w8.shape

    def body(w_ref, o_ref):
        for e in range(8):
            o_ref[:, C * e:C * (e + 1)] = w_ref[e]

    return pl.pallas_call(body, name=name, out_shape=_sds((R, 8 * C), w8.dtype), compiler_params=_cp())(w8)


def _all_gather(srcs, out_shapes, layout, name):
    n = len(srcs)

    def body(*refs):
        src, out = refs[:n], refs[n:2 * n]
        send_sems, recv_sems, local_sems = refs[2 * n:]
        x, y, c = _my_place()
        me, sibling = (x, y, c), (x, y, 1 - c)
        chips = [(1 - x, y), (x, 1 - y), (1 - x, 1 - y)]

        def region(p, dev):
            px, py, pc = dev
            return _block_of(out[p], layout[p][0], 4 * px + 2 * py + pc, layout[p][1])

        def copy(p, k, block, to, src_ref=None):
            return pltpu.make_async_remote_copy(
                src_ref=region(p, block) if src_ref is None else src_ref, dst_ref=region(p, block),
                send_sem=send_sems.at[p * 7 + k], recv_sem=recv_sems.at[p * 7 + k],
                device_id=to, device_id_type=MESH)

        mine = [pltpu.make_async_copy(src[p], region(p, me), local_sems.at[p]) for p in range(n)]
        for cp in mine:
            cp.start()
        first = []
        for p in range(n):
            first.append(copy(p, 0, me, sibling, src[p]))
            first += [copy(p, 1 + j, me, (*chip, c), src[p]) for j, chip in enumerate(chips)]
        for cp in first:
            cp.start()
        passed = []
        for j, chip in enumerate(chips):
            for p in range(n):
                copy(p, 1 + j, (*chip, c), me).wait_recv()
                fwd = copy(p, 4 + j, (*chip, c), sibling)
                fwd.start()
                passed.append(fwd)
        for p in range(n):
            copy(p, 0, sibling, me).wait_recv()
            for j, chip in enumerate(chips):
                copy(p, 4 + j, (*chip, 1 - c), me).wait_recv()
        for cp in first + passed:
            cp.wait_send()
        for cp in mine:
            cp.wait()

    return pl.pallas_call(
        body, name=name, in_specs=[_HBM] * n, out_specs=[_HBM] * n,
        out_shape=[_sds(s, a.dtype) for s, a in zip(out_shapes, srcs)],
        scratch_shapes=[pltpu.SemaphoreType.DMA((7 * n,)), pltpu.SemaphoreType.DMA((7 * n,)),
                        pltpu.SemaphoreType.DMA((n,))],
    )(*srcs)


def _reduce_pair_exchange(dws, layout, smalls):
    n = len(dws)

    def body(*refs):
        src, sm = refs[:n], refs[n]
        out, sm_out = refs[n + 1:2 * n + 1], refs[2 * n + 1]
        send_sems, recv_sems = refs[2 * n + 2:]
        x, y, c = _my_place()
        sibling = (x, y, 1 - c)
        cps = []
        for p in range(n):
            for q in range(4):
                cps.append(pltpu.make_async_remote_copy(
                    src_ref=_block_of(src[p], layout[p][0], 2 * q + (1 - c), layout[p][1]), dst_ref=out[p].at[q],
                    send_sem=send_sems.at[4 * p + q], recv_sem=recv_sems.at[4 * p + q],
                    device_id=sibling, device_id_type=MESH))
        cps.append(pltpu.make_async_remote_copy(src_ref=sm, dst_ref=sm_out, send_sem=send_sems.at[4 * n],
                                                recv_sem=recv_sems.at[4 * n], device_id=sibling, device_id_type=MESH))
        for cp in cps:
            cp.start()
        for cp in cps:
            cp.wait()

    def shard_shape(a, lay):
        axis, w = lay
        return (4, w, a.shape[1]) if axis == 0 else (4, a.shape[0], w)

    return pl.pallas_call(
        body, name="reduce_pair_exchange", in_specs=[_HBM] * (n + 1), out_specs=[_HBM] * (n + 1),
        out_shape=[_sds(shard_shape(a, l), F32) for a, l in zip(dws, layout)] + [_sds(smalls.shape, F32)],
        scratch_shapes=[pltpu.SemaphoreType.DMA((4 * n + 1,)), pltpu.SemaphoreType.DMA((4 * n + 1,))],
    )(*dws, smalls)


def _pair_add(dw, recv, axis, width, cidx, name):
    if axis == 0:
        tr = min(width, 256)
        C = dw.shape[1]
        grid = (4, width // tr)
        nb = width // tr
        own = pl.BlockSpec((tr, C), lambda q, i, c_ref: ((2 * q + c_ref[0]) * nb + i, 0))
        oth = pl.BlockSpec((None, tr, C), lambda q, i, c_ref: (q, i, 0))
    else:
        R = dw.shape[0]
        tr = min(R, 256)
        grid = (4, R // tr)
        own = pl.BlockSpec((tr, width), lambda q, i, c_ref: (i, 2 * q + c_ref[0]))
        oth = pl.BlockSpec((None, tr, width), lambda q, i, c_ref: (q, i, 0))

    def body(c_ref, a_ref, b_ref, o_ref):
        o_ref[...] = (a_ref[...] + b_ref[...]).astype(o_ref.dtype)

    return pl.pallas_call(
        body, name=name,
        grid_spec=pltpu.PrefetchScalarGridSpec(num_scalar_prefetch=1, grid=grid, in_specs=[own, oth], out_specs=oth),
        out_shape=_sds(recv.shape, CDT), compiler_params=_cp(("parallel", "parallel")),
    )(cidx, dw, recv)


def _add2(a, b, name):
    def body(a_ref, b_ref, o_ref):
        o_ref[...] = a_ref[...] + b_ref[...]

    return pl.pallas_call(body, name=name, out_shape=_sds(a.shape, F32))(a, b)


def _reduce_chip_exchange(parts):
    n = len(parts)

    def body(*refs):
        src, out = refs[:n], refs[n:2 * n]
        send_sems, recv_sems, local_sems = refs[2 * n:]
        x, y, c = _my_place()
        my_slot = 2 * x + y
        chips = [(1 - x, y), (x, 1 - y), (1 - x, 1 - y)]
        cps = []
        for p in range(n):
            keep = pltpu.make_async_copy(src[p].at[my_slot], out[p].at[my_slot], local_sems.at[p])
            keep.start()
            cps.append(keep)
            for k, (cx, cy) in enumerate(chips):
                cps.append(pltpu.make_async_remote_copy(
                    src_ref=src[p].at[2 * cx + cy], dst_ref=out[p].at[my_slot],
                    send_sem=send_sems.at[3 * p + k], recv_sem=recv_sems.at[3 * p + k],
                    device_id=(cx, cy, c), device_id_type=MESH))
                cps[-1].start()
        for cp in cps:
            cp.wait()

    return pl.pallas_call(
        body, name="reduce_chip_exchange", in_specs=[_HBM] * n, out_specs=[_HBM] * n,
        out_shape=[_sds(a.shape, a.dtype) for a in parts],
        scratch_shapes=[pltpu.SemaphoreType.DMA((3 * n,)), pltpu.SemaphoreType.DMA((3 * n,)),
                        pltpu.SemaphoreType.DMA((n,))],
    )(*parts)


def _adaln_part(c_all, w_ada, b_part, dep=None):
    dep_specs, dep_args = _after(dep)
    vmem = pl.BlockSpec(memory_space=pltpu.VMEM)

    def body(c_ref, w_ref, b_ref, *rest):
        rest[-1][...] = _dot(c_ref[...].astype(CDT), w_ref[...].astype(CDT), NN) + b_ref[...]

    return pl.pallas_call(body, name="adaln_part", in_specs=[vmem, vmem, vmem] + dep_specs, out_specs=vmem,
                          out_shape=_sds((8, w_ada.shape[1]), F32),
                          compiler_params=_cp())(c_all, w_ada, b_part, *dep_args)


def _adam_math(w, g, m, v):
    m = ADAM_B1 * m + (1.0 - ADAM_B1) * g
    v = ADAM_B2 * v + (1.0 - ADAM_B2) * (g * g)
    m_hat = m / (1.0 - ADAM_B1 ** ADAM_STEP)
    v_hat = v / (1.0 - ADAM_B2 ** ADAM_STEP)
    delta = -ADAM_LR * (m_hat / (jnp.sqrt(v_hat) + ADAM_EPS) + ADAM_WD * w)
    return delta, m, v


def _adam(gparts, w, m, v, name):
    P, R, C = gparts.shape
    tr = R if R <= 256 else 256

    def body(g_ref, w_ref, m_ref, v_ref, go_ref, d_ref, mo_ref, vo_ref):
        g = g_ref[0].astype(F32)
        for i in range(1, P):
            g = g + g_ref[i].astype(F32)
        d, mn, vn = _adam_math(w_ref[...], g, m_ref[...], v_ref[...])
        go_ref[...] = g
        d_ref[...] = d
        mo_ref[...] = mn
        vo_ref[...] = vn

    t = pl.BlockSpec((tr, C), lambda i: (i, 0))
    return pl.pallas_call(
        body, name=name, grid=(R // tr,),
        in_specs=[pl.BlockSpec((P, tr, C), lambda i: (0, i, 0)), t, t, t], out_specs=[t, t, t, t],
        out_shape=[_sds((R, C), F32)] * 4, compiler_params=_cp(("parallel",)),
    )(gparts, w, m, v)


def _sum_parts(gparts, name):
    P, R, C = gparts.shape

    def body(g_ref, o_ref):
        g = g_ref[0]
        for i in range(1, P):
            g = g + g_ref[i]
        o_ref[...] = g

    return pl.pallas_call(body, name=name, out_shape=_sds((R, C), F32))(gparts)


def _adam_w_ada(c_t, dm, w, m, v):
    R, C = w.shape

    def body(c_ref, dm_ref, w_ref, m_ref, v_ref, go_ref, d_ref, mo_ref, vo_ref):
        g = c_ref[:, 0:1] * dm_ref[0:1, :]
        for b in range(1, 8):
            g = g + c_ref[:, b:b + 1] * dm_ref[b:b + 1, :]
        d, mn, vn = _adam_math(w_ref[...], g, m_ref[...], v_ref[...])
        go_ref[...] = g
        d_ref[...] = d
        mo_ref[...] = mn
        vo_ref[...] = vn

    return pl.pallas_call(body, name="adam_w_ada", out_shape=[_sds((R, C), F32)] * 4,
                          compiler_params=_cp())(c_t, dm, w, m, v)


_SM = dict(norm_g=(0, 8), pool_w=(8, 512), pool_scale=(520, 8), rel_bias=(528, 8), final_g=(536, 8),
           dmod=(544, 192), loss=(736, 8))
_SM_ROWS = 744


def _pack_rows(a, rows):
    flat = a.reshape(-1).astype(F32)
    return jnp.pad(flat, (0, rows * 128 - flat.shape[0])).reshape(rows, 128)


def kernel(x, c, norm_g, w_ada, b_ada, w_in, pool_w, pool_scale, w_attn_br, w_pool_br, w_out, rel_bias, final_g, loss_target, m_norm_g, m_w_ada, m_b_ada, m_w_in, m_pool_w, m_pool_scale, m_w_attn_br, m_w_pool_br, m_w_out, m_rel_bias, m_final_g, v_norm_g, v_w_ada, v_b_ada, v_w_in, v_pool_w, v_pool_scale, v_w_attn_br, v_w_pool_br, v_w_out, v_rel_bias, v_final_g):
    S = x.shape[1]
    px, py, pc = _my_place()
    me = 4 * px + 2 * py + pc
    cidx = jnp.reshape(pc, (1,)).astype(jnp.int32)

    layout = [(1, 1024), (1, 128), (1, 128), (0, 128)]
    final_g2 = final_g.reshape(1, D)
    me1 = jnp.reshape(me, (1,)).astype(jnp.int32)

    (c_rows,) = _all_gather([jnp.broadcast_to(c, (8, D))], [(64, D)], [(0, 8)], "gather_c")
    c_all = c_rows[::8]

    b_part = lax.dynamic_slice(b_ada, (0, me * 384), (1, 384))
    mod_part = _adaln_part(c_all, w_ada[0], b_part)
    (mod_all,) = _all_gather([mod_part], [(8, 8, 384)], [(None, 1)], "gather_mod")
    mod = lax.dynamic_index_in_dim(mod_all, me, axis=1, keepdims=False).reshape(1, 3 * D)
    shift, scale, gate = mod[:, :D], mod[:, D:2 * D], mod[:, 2 * D:]

    shards = [w_in[0].astype(CDT), w_attn_br[0].astype(CDT), w_pool_br[0].astype(CDT), w_out[0].astype(CDT)]
    gplan = _gather_chips_plan([(None, 1)] * 4)
    g_sems, shards, lands, tok = _split_start("gather_chips_start", shards,
                                              _place_slot(shards, me1, 8, "place_own_shards"), 12, gplan, dep=mod_all)
    st = _local_pre(x[0], shift, scale, norm_g, rel_bias, dep=tok)
    _, lands = _split_wait("gather_chips_wait", g_sems, shards, lands, gplan,
                           [st["ht"]] + [b for pair in st["biases"] for b in pair])
    w_in_full, wab8, wpb8, wout8 = _pair_share(lands)
    wab, wpb = _cols_from_blocks(wab8, "w_attn_br_cols"), _cols_from_blocks(wpb8, "w_pool_br_cols")
    wout = wout8.reshape(D, D)

    st = _local_grads(st, loss_target[0], gate, w_in_full, pool_w[0], pool_scale, wab, wpb, wout, final_g2)

    early = jnp.concatenate([_pack_rows(st["dpw"], 512), _pack_rows(st["dps"], 8), _pack_rows(st["gfg"], 8),
                             _pack_rows(st["loss"], 8)], axis=0)
    dws = [st["dw_in"], st["dw_ab"], st["dw_pb"], st["dw_out"]]
    pplan = _pair_plan(layout)
    pair_shapes = [(4, w, a.shape[1]) if ax == 0 else (4, a.shape[0], w) for a, (ax, w) in zip(dws, layout)]
    p_sems, srcs, recv, tok = _split_start("reduce_pair_start", dws + [early],
                                           [lax.empty(s, F32) for s in pair_shapes + [early.shape]], 17, pplan)
    d_rel_bias = _local_bias_grads(st, dep=tok)
    srcs, recv = _split_wait("reduce_pair_wait", p_sems, srcs, recv, pplan, d_rel_bias)
    parts = [_pair_add(dw, r, ax, w, cidx, f"pair_add{i}")
             for i, (dw, r, (ax, w)) in enumerate(zip(srcs[:4], recv[:4], layout))]
    parts.append(_add2(srcs[4], recv[4], "pair_add_smalls"))

    slot1 = jnp.reshape(2 * px + py, (1,)).astype(jnp.int32)
    c_sems, parts, lands, tok = _split_start("reduce_chips_start", parts,
                                             _place_slot(parts, slot1, 4, "place_own_partials"), 15, _chips_plan)
    grad_x, dmod, d_norm_g = _local_input_grads(st, w_in_full, norm_g, scale, dep=tok)

    late = jnp.concatenate([_pack_rows(dmod, 24), _pack_rows(d_norm_g, 8), _pack_rows(d_rel_bias, 8)], axis=0)
    (late_all,) = _all_gather([late], [(8, 40, 128)], [(None, 1)], "gather_late")

    def small_adam(gparts, w, m, v, nvalid, name):
        rows = gparts.shape[1]
        outs = _adam(gparts, _pack_rows(w, rows), _pack_rows(m, rows), _pack_rows(v, rows), f"adam_{name}")
        return [o.reshape(-1)[:nvalid].reshape(w.shape) for o in outs]

    o_norm_g = small_adam(late_all[:, 24:32], norm_g, m_norm_g, v_norm_g, D, "norm_g")
    o_rel_bias = small_adam(late_all[:, 32:40], rel_bias, m_rel_bias, v_rel_bias, 768, "rel_bias")
    dmod_all = late_all[:, 0:24].reshape(8, 3 * D)
    o_b_ada = [o.reshape(1, 3 * D) for o in _adam(dmod_all.reshape(8, 24, 128), b_ada.reshape(24, 128),
                                                   m_b_ada.reshape(24, 128), v_b_ada.reshape(24, 128), "adam_b_ada")]
    dm_mine = lax.dynamic_slice(dmod_all, (0, me * 384), (8, 384))
    o_w_ada = [o[None] for o in _adam_w_ada(c_all.T, dm_mine, w_ada[0], m_w_ada[0], v_w_ada[0])]

    _, (g_in, g_ab, g_pb, g_out, sm4) = _split_wait("reduce_chips_wait", c_sems, parts, lands, _chips_plan,
                                                    [o_w_ada[0], o_b_ada[0], o_norm_g[0], o_rel_bias[0]])
    o_w_in = [o[None] for o in _adam(g_in, w_in[0], m_w_in[0], v_w_in[0], "adam_w_in")]
    o_w_ab = [o[None] for o in _adam(g_ab, w_attn_br[0], m_w_attn_br[0], v_w_attn_br[0], "adam_w_attn_br")]
    o_w_pb = [o[None] for o in _adam(g_pb, w_pool_br[0], m_w_pool_br[0], v_w_pool_br[0], "adam_w_pool_br")]
    o_w_out = [o[None] for o in _adam(g_out, w_out[0], m_w_out[0], v_w_out[0], "adam_w_out")]
    o_pool_w = small_adam(sm4[:, 0:512], pool_w, m_pool_w, v_pool_w, 65536, "pool_w")
    o_pool_scale = small_adam(sm4[:, 512:520], pool_scale, m_pool_scale, v_pool_scale, AW, "pool_scale")
    o_final_g = small_adam(sm4[:, 520:528], final_g, m_final_g, v_final_g, D, "final_g")
    loss = _sum_parts(sm4[:, 528:536], "sum_loss")[0, 0]

    per_w = [o_norm_g, o_w_ada, o_b_ada, o_w_in, o_pool_w, o_pool_scale, o_w_ab, o_w_pb, o_w_out, o_rel_bias, o_final_g]
    outs = [loss, grad_x[None]]
    for k in range(4):
        outs += [o[k] for o in per_w]
    return tuple(outs)
```

```python
import functools
import math

import numpy as np
import jax
import jax.numpy as jnp
from jax import lax
from jax.experimental import pallas as pl
from jax.experimental.pallas import tpu as pltpu

F32 = jnp.float32
CDT = jnp.bfloat16
D = 1024
HD = 64
NH = 8
AW = 512
QKV_W = 4608
REST_W = 3584
R16 = 16
EPS = 1e-6
NEG = -1e30
N_BACK = 128
ADAM_LR, ADAM_B1, ADAM_B2, ADAM_EPS, ADAM_WD, ADAM_STEP = 0.001, 0.9, 0.999, 1e-08, 0.01, 10
MESH = pl.DeviceIdType.MESH
VMEM_MB = 1024 * 1024


def _cp(sem=None, vmem=48, **kw):
    if sem is not None:
        kw["dimension_semantics"] = sem
    return pltpu.CompilerParams(vmem_limit_bytes=vmem * VMEM_MB, **kw)


def _sds(shape, dtype):
    return jax.ShapeDtypeStruct(tuple(shape), dtype)


def _rest_blk(jj):
    return jnp.where(jj == 0, 9, jnp.where(jj == 6, 10, jj + 10))


def _sigmoid(z):
    return 1.0 / (1.0 + jnp.exp(-z))


def _dot(a, b, dims):
    return lax.dot_general(a, b, (dims, ((), ())), preferred_element_type=F32)


NN = ((1,), (0,))
NT = ((1,), (1,))
TN = ((0,), (0,))


def _ld(ref):
    if len(ref.shape) == 2:
        return ref[...]
    return jnp.concatenate([ref[a] for a in range(ref.shape[0])], axis=0)


def _st(ref, val, cols=None):
    val = val.astype(ref.dtype)
    if len(ref.shape) == 2:
        if cols is None:
            ref[...] = val
        else:
            ref[:, cols] = val
        return
    rows = ref.shape[1]
    for a in range(ref.shape[0]):
        if cols is None:
            ref[a] = val[a * rows:(a + 1) * rows]
        else:
            ref[a, :, cols] = val[a * rows:(a + 1) * rows]


REST_COLS = ((9, 10), (11, 16), (10, 11))
_HBM = pl.BlockSpec(memory_space=pltpu.HBM)


def _resident(shape):
    return pl.BlockSpec(shape, lambda *ids: (0,) * len(shape), pipeline_mode=pl.Buffered(1))


def _w512(w_ref, j):
    return w_ref[j // 2, :, 512 * (j % 2):512 * (j % 2 + 1)]


def _proj(h, w):
    S = h.shape[0]
    tm = 256

    def body(a_ref, w_ref, q_ref, r_ref):
        a = a_ref[...]
        for j in range(9):
            q_ref[:, 512 * j:512 * (j + 1)] = _dot(a, _w512(w_ref, j), NN).astype(q_ref.dtype)
        off = 0
        for lo, hi in REST_COLS:
            for j in range(lo, hi):
                r_ref[:, off:off + 512] = _dot(a, _w512(w_ref, j), NN)
                off += 512

    return pl.pallas_call(
        body, name="proj", grid=(S // tm,),
        in_specs=[pl.BlockSpec((tm, D), lambda i: (i, 0)), _resident((8, D, 1024))],
        out_specs=[pl.BlockSpec((tm, QKV_W), lambda i: (i, 0)), pl.BlockSpec((tm, REST_W), lambda i: (i, 0))],
        out_shape=[_sds((S, QKV_W), CDT), _sds((S, REST_W), F32)],
        compiler_params=_cp(("parallel",), vmem=56),
    )(h, w)


def _after(dep):
    return ([], []) if dep is None else ([pl.BlockSpec(memory_space=pl.ANY)], [dep])


def _dh(dqkv, drest, w, dep=None):
    S = dqkv.shape[0]
    tm = 256
    dep_specs, dep_args = _after(dep)

    def body(a_ref, b_ref, w_ref, *rest):
        o_ref = rest[-1]
        acc = _dot(a_ref[:, 0:512], _w512(w_ref, 0), NT)
        for j in range(1, 9):
            acc = acc + _dot(a_ref[:, 512 * j:512 * (j + 1)], _w512(w_ref, j), NT)
        off = 0
        for lo, hi in REST_COLS:
            for j in range(lo, hi):
                acc = acc + _dot(b_ref[:, off:off + 512], _w512(w_ref, j), NT)
                off += 512
        o_ref[...] = acc

    return pl.pallas_call(
        body, name="dh", grid=(S // tm,),
        in_specs=[pl.BlockSpec((tm, QKV_W), lambda i: (i, 0)), pl.BlockSpec((tm, REST_W), lambda i: (i, 0)),
                  _resident((8, D, 1024))] + dep_specs,
        out_specs=pl.BlockSpec((tm, D), lambda i: (i, 0)),
        out_shape=_sds((S, D), F32),
        compiler_params=_cp(("parallel",), vmem=56),
    )(dqkv, drest, w, *dep_args)


def _mm_tn(at, b, *, n_total, out_blk, name, prev=None, tn=512):
    R, S = at.shape
    nb = b.shape[1] // tn

    def body(*refs):
        refs[-1][...] = _dot(refs[0][...], refs[1][...], NN)

    in_specs = [_resident((R, S)), pl.BlockSpec((S, tn), lambda j: (0, j))]
    args = [at, b]
    aliases = {}
    if prev is not None:
        in_specs.append(pl.BlockSpec(memory_space=pl.ANY))
        args.append(prev)
        aliases = {2: 0}
    return pl.pallas_call(
        body, name=name, grid=(nb,), in_specs=in_specs,
        out_specs=pl.BlockSpec((R, tn), lambda j: (0, out_blk(j))),
        out_shape=_sds((R, n_total), F32),
        input_output_aliases=aliases,
        compiler_params=_cp(("parallel",)),
    )(*args)


def _tcast(a, name):
    M, C = a.shape
    tm = 512

    def body(a_ref, o_ref):
        o_ref[...] = a_ref[...].astype(F32).T.astype(o_ref.dtype)

    return pl.pallas_call(
        body, name=name, grid=(M // tm,),
        in_specs=[pl.BlockSpec((tm, C), lambda i: (i, 0))],
        out_specs=pl.BlockSpec((C, tm), lambda i: (0, i)),
        out_shape=_sds((C, M), CDT), compiler_params=_cp(("parallel",)),
    )(a)


TOK = R16 * R16


def _swap16():
    i = lax.broadcasted_iota(jnp.int32, (TOK, TOK), 0)
    j = lax.broadcasted_iota(jnp.int32, (TOK, TOK), 1)
    return jnp.where(j == jnp.bitwise_and(i, 15) * 16 + jnp.right_shift(i, 4), 1.0, 0.0).astype(CDT)


def _norm_mod(x, norm_g, shift, scale, dep=None):
    S = x.shape[0]
    dep_specs, dep_args = _after(dep)

    def body(x_ref, g_ref, sh_ref, sc_ref, *rest):
        xv = x_ref[...]
        xn = xv * lax.rsqrt(jnp.mean(xv * xv, axis=-1, keepdims=True) + EPS)
        h = (xn * (g_ref[...] * (1.0 + sc_ref[...])) + sh_ref[...]).astype(CDT)
        _st(rest[-1], _dot(_swap16(), h, NN))

    vec = pl.BlockSpec((1, D), lambda i: (0, 0))
    return pl.pallas_call(
        body, name="norm_mod", grid=(S // TOK,),
        in_specs=[pl.BlockSpec((TOK, D), lambda i: (i, 0)), vec, vec, vec] + dep_specs,
        out_specs=pl.BlockSpec((R16, R16, D), lambda i: (0, i, 0)),
        out_shape=_sds((R16, S // R16, D), CDT), compiler_params=_cp(("parallel",)),
    )(x, norm_g, shift, scale, *dep_args)


def _t5_bucket_np(n):
    nf = np.maximum(n, 1).astype(np.float32)
    large = 16 + (np.log(nf / np.float32(16)) / np.float32(math.log(2048 / 16)) * np.float32(16)).astype(np.int32)
    large = np.minimum(large, 31)
    return np.where(n < 16, n, large).astype(np.int32)


class _Geom:
    def __init__(self, gi, LS):
        self.gi, self.LS = gi, LS
        self.dil = (1, 4, 16)[gi]
        if gi == 0:
            self.lead, self.rows, self.n_tiles, self.has_prev = (R16,), 16, LS // 16, True
            a = np.arange(R16)[:, None]
            ll = np.arange(16)[None, :]
            self.pos = (16 * ll + a).reshape(-1)
        elif gi == 1:
            self.lead, self.rows, self.n_tiles, self.has_prev = (4,), 64, LS // 64, True
            a = np.arange(4)[:, None]
            ll = np.arange(64)[None, :]
            self.pos = (4 * ll + a).reshape(-1)
        else:
            self.lead, self.rows, self.n_tiles, self.has_prev = (), LS, 1, False
            self.pos = np.arange(LS)
        self.T = self.pos.shape[0]
        self.Tk = 2 * self.T if self.has_prev else self.T
        self.outer = {0: (), 1: (4,), 2: (R16,)}[gi]

    def view(self, arr):
        S, C = arr.shape
        if self.gi == 1:
            return arr.reshape(4, 4, self.LS, C)
        return arr.reshape(R16, self.LS, C)

    def view_shape(self, C):
        return (4, 4, self.LS, C) if self.gi == 1 else (R16, self.LS, C)

    def unview(self, arr):
        return arr.reshape(R16 * self.LS, arr.shape[-1])

    def spec(self, W, cblk, tile):
        if self.gi == 0:
            return pl.BlockSpec((R16, 16, W), lambda n: (0, tile(n), cblk))
        if self.gi == 1:
            return pl.BlockSpec((4, None, 64, W), lambda r, n: (0, r, tile(n), cblk))
        return pl.BlockSpec((None, self.LS, W), lambda r: (r, 0, cblk))

    def halves(self):
        R = self.rows
        idx = np.arange(self.T).reshape(-1, R)
        return idx[:, :R // 2].reshape(-1), idx[:, R // 2:].reshape(-1)

    def bucket_index(self, sub):
        lo, hi = self.halves()
        pos = self.pos
        if sub == 0:
            pq = pos[lo]
            pk = np.concatenate([pos[hi] - self.T, pos[lo]]) if self.has_prev else pos[lo]
        else:
            pq = pos[hi]
            pk = np.concatenate([pos[lo], pos[hi]])
        dist = pq[:, None] - pk[None, :]
        ok = (dist >= 0) & (dist <= N_BACK)
        bucket = _t5_bucket_np(np.clip(dist, 0, N_BACK) * self.dil)
        return np.where(ok, bucket, -1).astype(np.int32)


def _bias_table(idx, rel_bias, gi, sub, dep=None):
    T, Tk = idx.shape
    tr = 32
    dep_specs, dep_args = _after(dep)

    def body(idx_ref, rb_ref, *rest):
        h = pl.program_id(0)
        ix = idx_ref[...]
        acc = jnp.full(ix.shape, NEG, F32)
        for b in range(32):
            acc = jnp.where(ix == b, rb_ref[b, gi * NH + h], acc)
        rest[-1][...] = acc

    return pl.pallas_call(
        body, name=f"bias_table{gi}{'ab'[sub]}", grid=(NH, T // tr),
        in_specs=[pl.BlockSpec((tr, Tk), lambda h, i: (i, 0)), pl.BlockSpec(memory_space=pltpu.SMEM)] + dep_specs,
        out_specs=pl.BlockSpec((None, tr, Tk), lambda h, i: (h, i, 0)),
        out_shape=_sds((NH, T, Tk), F32), compiler_params=_cp(("parallel", "parallel")),
    )(idx, rel_bias, *dep_args)


def _bias_grad(dsum, idx, gi, sub, dep=None):
    _, T, Tk = dsum.shape
    dep_specs, dep_args = _after(dep)

    def body(ds_ref, idx_ref, *rest):
        o_ref, r_ref = rest[-2], rest[-1]

        def step(b, c):
            r_ref[pl.ds(b, 1), :] = jnp.sum(jnp.where(idx_ref[...] == b, ds_ref[...], 0.0), axis=0, keepdims=True)
            return c

        lax.fori_loop(0, 32, step, 0)
        o_ref[...] = jnp.broadcast_to(jnp.sum(r_ref[...], axis=1, keepdims=True), (32, 128))

    out = pl.pallas_call(
        body, name=f"bias_grad{gi}{'ab'[sub]}", grid=(NH,),
        in_specs=[pl.BlockSpec((None, T, Tk), lambda h: (h, 0, 0)), pl.BlockSpec((T, Tk), lambda h: (0, 0))] + dep_specs,
        out_specs=pl.BlockSpec((None, 32, 128), lambda h: (h, 0, 0)),
        out_shape=_sds((NH, 32, 128), F32), scratch_shapes=[pltpu.VMEM((32, Tk), F32)],
        compiler_params=_cp(("parallel",)),
    )(dsum, idx, *dep_args)
    return out[:, :, 0].T


def _prev_mask(n, T, Tk):
    col = lax.broadcasted_iota(jnp.int32, (1, Tk), 1)
    return jnp.where((col < T) & (n == 0), NEG, 0.0).astype(F32)


def _split_rows(v, R):
    A, half = v.shape[0] // R, R // 2
    via_f32 = v.dtype != F32 and half % 16 != 0
    w = v.astype(F32) if via_f32 else v
    lo = jnp.concatenate([w[a * R:a * R + half] for a in range(A)], axis=0)
    hi = jnp.concatenate([w[a * R + half:(a + 1) * R] for a in range(A)], axis=0)
    return (lo.astype(v.dtype), hi.astype(v.dtype)) if via_f32 else (lo, hi)


def _merge_rows(lo, hi, R):
    half = R // 2
    A = lo.shape[0] // half
    return jnp.concatenate([x[a * half:(a + 1) * half] for a in range(A) for x in (lo, hi)], axis=0)


def _attn_fwd(qkv, biases, geom):
    S = qkv.shape[0]
    gi, T, R, has_prev = geom.gi, geom.T, geom.rows, geom.has_prev
    H2 = T // 2
    qv = geom.view(qkv)
    cq, ck, cv = 3 * gi, 3 * gi + 1, 3 * gi + 2

    def body(*refs):
        if has_prev:
            q_ref, kp_ref, kc_ref, vp_ref, vc_ref, ba_ref, bb_ref, o_ref, l_ref = refs
        else:
            q_ref, kc_ref, vc_ref, ba_ref, bb_ref, o_ref, l_ref = refs
        q_lo, q_hi = _split_rows(_ld(q_ref), R)
        kc_lo, kc_hi = _split_rows(_ld(kc_ref), R)
        vc_lo, vc_hi = _split_rows(_ld(vc_ref), R)
        k_b = jnp.concatenate([kc_lo, kc_hi], axis=0)
        v_b = jnp.concatenate([vc_lo, vc_hi], axis=0)
        if has_prev:
            n = pl.program_id(len(geom.outer))
            k_a = jnp.concatenate([_split_rows(_ld(kp_ref), R)[1], kc_lo], axis=0)
            v_a = jnp.concatenate([_split_rows(_ld(vp_ref), R)[1], vc_lo], axis=0)
            pm = _prev_mask(n, H2, T)
        else:
            k_a, v_a, pm = kc_lo, vc_lo, None
        low = lax.broadcasted_iota(jnp.int32, (1, 2 * HD), 1) < HD
        for j in range(NH // 2):
            cols = slice(2 * HD * j, 2 * HD * (j + 1))
            o_half, l_half = [], []
            for q, k, v, b_ref, mask in ((q_lo, k_a, v_a, ba_ref, pm), (q_hi, k_b, v_b, bb_ref, None)):
                q2 = q[:, cols].astype(F32) * 0.125
                k2, v2 = k[:, cols], v[:, cols]
                res = []
                for hh in range(2):
                    qs = jnp.where(low if hh == 0 else ~low, q2, 0.0).astype(q.dtype)
                    s = _dot(qs, k2, NT) + b_ref[2 * j + hh]
                    if mask is not None:
                        s = s + mask
                    m = jnp.max(s, axis=1, keepdims=True)
                    p = jnp.exp(s - m)
                    l = jnp.sum(p, axis=1, keepdims=True)
                    res.append((_dot(p.astype(v.dtype), v2, NN) / l, m + jnp.log(l)))
                o_half.append(jnp.where(low, res[0][0], res[1][0]))
                l_half.append(jnp.where(low, res[0][1], res[1][1]))
            _st(o_ref, _merge_rows(o_half[0], o_half[1], R), cols)
            _st(l_ref, _merge_rows(l_half[0], l_half[1], R), cols)

    cur = lambda n: n
    prev = lambda n: jnp.maximum(n - 1, 0)
    if has_prev:
        in_specs = [geom.spec(AW, cq, cur), geom.spec(AW, ck, prev), geom.spec(AW, ck, cur),
                    geom.spec(AW, cv, prev), geom.spec(AW, cv, cur)]
        args = [qv] * 5
    else:
        in_specs = [geom.spec(AW, cq, cur), geom.spec(AW, ck, cur), geom.spec(AW, cv, cur)]
        args = [qv] * 3
    nd = len(geom.outer) + (1 if has_prev else 0)
    in_specs += [pl.BlockSpec(b.shape, lambda *ids: (0, 0, 0)) for b in biases]
    grid = geom.outer + ((geom.n_tiles,) if has_prev else ())
    oshape = geom.view_shape(AW)
    o, l = pl.pallas_call(
        body, name=f"attn_fwd{gi}", grid=grid, in_specs=in_specs,
        out_specs=[geom.spec(AW, 0, cur), geom.spec(AW, 0, cur)],
        out_shape=[_sds(oshape, F32), _sds(oshape, F32)],
        compiler_params=_cp(("arbitrary",) * nd),
    )(*args, *biases)
    return geom.unview(o), geom.unview(l)


def _attn_combine(os_, ls_, rest):
    S = rest.shape[0]
    tm = 512

    def body(o0, o1, o2, l0, l1, l2, z_ref, attn_ref, L_ref, ag_ref, agt_ref):
        la, lb, lc = l0[...], l1[...], l2[...]
        m = jnp.maximum(jnp.maximum(la, lb), lc)
        ea, eb, ec = jnp.exp(la - m), jnp.exp(lb - m), jnp.exp(lc - m)
        den = ea + eb + ec
        attn = (ea * o0[...] + eb * o1[...] + ec * o2[...]) / den
        attn_ref[...] = attn
        L_ref[...] = m + jnp.log(den)
        z = z_ref[...]
        ag = attn * (z * _sigmoid(z))
        ag_ref[...] = ag.astype(ag_ref.dtype)
        agt_ref[...] = ag.T.astype(agt_ref.dtype)

    t = pl.BlockSpec((tm, AW), lambda i: (i, 0))
    return pl.pallas_call(
        body, name="attn_combine", grid=(S // tm,),
        in_specs=[t] * 6 + [pl.BlockSpec((tm, AW), lambda i: (i, 0))],
        out_specs=[t, t, t, pl.BlockSpec((AW, tm), lambda i: (0, i))],
        out_shape=[_sds((S, AW), F32), _sds((S, AW), F32), _sds((S, AW), CDT), _sds((AW, S), CDT)],
        compiler_params=_cp(("parallel",)),
    )(*os_, *ls_, rest)


def _pool_counts(r, LS, win):
    l = lax.broadcasted_iota(jnp.int32, (LS, 1), 0)
    return jnp.minimum(16 * l + (r + 1), win).astype(F32)


def _pool_fwd(rest, pool_w, pool_scale):
    S = rest.shape[0]
    LS = S // R16
    r3 = rest.reshape(R16, LS, REST_W)
    PG = 128

    def body(u_ref, z_ref, pw_ref, ps_ref, pt_ref, mp_ref, pg_ref, pgt_ref, d_ref):
        g = pl.program_id(0)
        win = jnp.left_shift(2, g)
        row = lax.broadcasted_iota(jnp.int32, (LS, PG), 0)
        for r in range(1, R16):
            d_ref[r] = jnp.where(row == 0, 0.0, pltpu.roll(u_ref[r], 1, 0))
        pw = pw_ref[...].astype(CDT)
        for r in range(R16):
            u = u_ref[r]
            acc = u
            for j in range(1, R16):
                coef = jnp.where(j < win, 1.0, 0.0).astype(F32)
                src = u_ref[r - j] if r >= j else d_ref[r - j + R16]
                acc = acc + coef * src
            pooled = acc / _pool_counts(r, LS, win) - u
            pc = pooled.astype(CDT)
            mp = _dot(pc, pw, NN)
            z = z_ref[r]
            pg = (mp * ps_ref[...]) * (z * _sigmoid(z))
            mp_ref[r] = mp
            pg_ref[r] = pg.astype(pg_ref.dtype)
            pt_ref[:, r * LS:(r + 1) * LS] = pooled.T.astype(pt_ref.dtype)
            pgt_ref[:, r * LS:(r + 1) * LS] = pg.T.astype(pgt_ref.dtype)

    col = lambda off: pl.BlockSpec((R16, LS, PG), lambda g: (0, 0, off + g))
    tsp = pl.BlockSpec((PG, S), lambda g: (g, 0))
    pt, mp, pg, pgt = pl.pallas_call(
        body, name="pool_fwd", grid=(4,),
        in_specs=[col(24), col(4), pl.BlockSpec((None, PG, PG), lambda g: (g, 0, 0)),
                  pl.BlockSpec((1, PG), lambda g: (0, g))],
        out_specs=[tsp, col(0), col(0), tsp],
        out_shape=[_sds((AW, S), CDT), _sds((R16, LS, AW), F32), _sds((R16, LS, AW), CDT), _sds((AW, S), CDT)],
        scratch_shapes=[pltpu.VMEM((R16, LS, PG), F32)],
        compiler_params=_cp(("parallel",)),
    )(r3, r3, pool_w, pool_scale)
    return pt, mp.reshape(S, AW), pg.reshape(S, AW), pgt


def _branch_merge(ag, pg, wab, wpb, rest):
    S = ag.shape[0]
    tm, tn = 512, 512

    def body(ag_ref, pg_ref, wa_ref, wp_ref, ga_ref, gp_ref, m_ref, mt_ref):
        ya = _dot(ag_ref[...], wa_ref[...], NN)
        yp = _dot(pg_ref[...], wp_ref[...], NN)
        mg = _sigmoid(ga_ref[...]) * ya + _sigmoid(gp_ref[...]) * yp
        m_ref[...] = mg.astype(m_ref.dtype)
        mt_ref[...] = mg.T.astype(mt_ref.dtype)

    a = pl.BlockSpec((tm, AW), lambda i, j: (i, 0))
    w = pl.BlockSpec((AW, tn), lambda i, j: (0, j))
    o = pl.BlockSpec((tm, tn), lambda i, j: (i, j))
    return pl.pallas_call(
        body, name="branch_merge", grid=(S // tm, D // tn),
        in_specs=[a, a, w, w, pl.BlockSpec((tm, tn), lambda i, j: (i, 2 + j)),
                  pl.BlockSpec((tm, tn), lambda i, j: (i, 4 + j))],
        out_specs=[o, pl.BlockSpec((tn, tm), lambda i, j: (j, i))],
        out_shape=[_sds((S, D), CDT), _sds((D, S), CDT)],
        compiler_params=_cp(("parallel", "parallel")),
    )(ag, pg, wab, wpb, rest, rest)


def _out_loss(merged, wout, x, target, gate, final_g):
    S = x.shape[0]

    def body(m_ref, w_ref, x_ref, t_ref, gate_ref, fg_ref, dx2_ref, dmo_ref, loss_ref, gfg_ref, dgate_ref):
        i = pl.program_id(0)

        @pl.when(i == 0)
        def _():
            loss_ref[...] = jnp.zeros_like(loss_ref)
            gfg_ref[...] = jnp.zeros_like(gfg_ref)
            dgate_ref[...] = jnp.zeros_like(dgate_ref)

        swap = _swap16()
        mo = _dot(_dot(swap, _ld(m_ref), NN).astype(CDT), w_ref[...], NN)
        gate, fg = gate_ref[...], fg_ref[...]
        x2 = x_ref[...] + gate * mo
        r2 = lax.rsqrt(jnp.mean(x2 * x2, axis=-1, keepdims=True) + EPS)
        xn2 = x2 * r2
        diff = xn2 * fg - t_ref[...]
        loss = jnp.sum(jnp.sum(diff * diff, axis=-1, keepdims=True), axis=0, keepdims=True)
        dy = diff * (1.0 / D)
        dxn2 = dy * fg
        dx2 = r2 * (dxn2 - xn2 * jnp.mean(dxn2 * xn2, axis=-1, keepdims=True))
        dx2_ref[...] = dx2
        _st(dmo_ref, _dot(swap, (dx2 * gate).astype(CDT), NN))
        loss_ref[...] += jnp.broadcast_to(loss * (0.5 / D), loss_ref.shape)
        gfg_ref[...] += jnp.sum(dy * xn2, axis=0, keepdims=True)
        dgate_ref[...] += jnp.sum(dx2 * mo, axis=0, keepdims=True)

    slab = pl.BlockSpec((R16, R16, D), lambda i: (0, i, 0))
    nat = pl.BlockSpec((TOK, D), lambda i: (i, 0))
    vec = pl.BlockSpec((1, D), lambda i: (0, 0))
    return pl.pallas_call(
        body, name="out_loss", grid=(S // TOK,),
        in_specs=[slab, _resident((D, D)), nat, nat, vec, vec],
        out_specs=[nat, slab, pl.BlockSpec((1, 128), lambda i: (0, 0)), vec, vec],
        out_shape=[_sds((S, D), F32), _sds((R16, S // R16, D), CDT), _sds((1, 128), F32), _sds((1, D), F32),
                   _sds((1, D), F32)],
        compiler_params=_cp(("arbitrary",)),
    )(merged, wout, x, target, gate, final_g)


def _bwd_gates(dmo, wout, wab, wpb, rest, ag, pg, attn, mp, pool_scale):
    S = dmo.shape[0]
    tm = 256

    def body(dmo_ref, wo_ref, wa_ref, wp_ref, za_ref, zp_ref, ga0, ga1, gp0, gp1, ag_ref, pg_ref, at_ref, mp_ref,
             ps_ref, dya_ref, dyp_ref, dr_ref, dat_ref, dl_ref, dmp_ref, dps_ref):
        i = pl.program_id(0)

        @pl.when(i == 0)
        def _():
            dps_ref[...] = jnp.zeros_like(dps_ref)

        dm = _dot(dmo_ref[...], wo_ref[...], NT)
        sa = _sigmoid(jnp.concatenate([ga0[...], ga1[...]], axis=1))
        sp = _sigmoid(jnp.concatenate([gp0[...], gp1[...]], axis=1))
        dya = (dm * sa).astype(CDT)
        dyp = (dm * sp).astype(CDT)
        dya_ref[...] = dya
        dyp_ref[...] = dyp
        ya = _dot(ag_ref[...], wa_ref[...], NN)
        yp = _dot(pg_ref[...], wp_ref[...], NN)
        dr_ref[:, 1024:2048] = (dm * ya * sa * (1.0 - sa)).astype(dr_ref.dtype)
        dr_ref[:, 2048:3072] = (dm * yp * sp * (1.0 - sp)).astype(dr_ref.dtype)
        da = _dot(dya, wa_ref[...], NT)
        dp = _dot(dyp, wp_ref[...], NT)
        za = za_ref[...]
        sga = _sigmoid(za)
        attn = at_ref[...]
        dattn = da * (za * sga)
        dr_ref[:, 0:512] = (da * attn * (sga * (1.0 + za * (1.0 - sga)))).astype(dr_ref.dtype)
        dat_ref[...] = dattn.astype(dat_ref.dtype)
        prod = dattn * attn
        for h in range(NH):
            sl = slice(HD * h, HD * h + HD)
            dl_ref[:, sl] = jnp.broadcast_to(jnp.sum(prod[:, sl], axis=1, keepdims=True), (tm, HD))
        zp = zp_ref[...]
        sgp = _sigmoid(zp)
        mpre = mp_ref[...]
        ps = ps_ref[...]
        dmixed = dp * (zp * sgp)
        dr_ref[:, 512:1024] = (dp * (mpre * ps) * (sgp * (1.0 + zp * (1.0 - sgp)))).astype(dr_ref.dtype)
        dps_ref[...] += jnp.sum(dmixed * mpre, axis=0, keepdims=True)
        dmp_ref[...] = (dmixed * ps).astype(dmp_ref.dtype)

    full = lambda shape: pl.BlockSpec(shape, lambda i: (0, 0))
    tD = pl.BlockSpec((tm, D), lambda i: (i, 0))
    tA = pl.BlockSpec((tm, AW), lambda i: (i, 0))
    rb = lambda c: pl.BlockSpec((tm, AW), lambda i: (i, c))
    return pl.pallas_call(
        body, name="bwd_gates", grid=(S // tm,),
        in_specs=[tD, full((D, D)), full((AW, D)), full((AW, D)), rb(0), rb(1), rb(2), rb(3), rb(4), rb(5),
                  tA, tA, tA, tA, full((1, AW))],
        out_specs=[tD, tD, pl.BlockSpec((tm, 3072), lambda i: (i, 0)), tA, tA, tA, full((1, AW))],
        out_shape=[_sds((S, D), CDT), _sds((S, D), CDT), _sds((S, REST_W), CDT), _sds((S, AW), CDT),
                   _sds((S, AW), F32), _sds((S, AW), CDT), _sds((1, AW), F32)],
        compiler_params=_cp(("arbitrary",), vmem=56),
    )(dmo, wout, wab, wpb, rest, rest, rest, rest, rest, rest, ag, pg, attn, mp, pool_scale)


def _pool_bwd(dmp, pt, pool_w, drest):
    S = dmp.shape[0]
    LS = S // R16
    PG = 128
    d3 = dmp.reshape(R16, LS, AW)
    dr3 = drest.reshape(R16, LS, REST_W)

    def body(d_ref, pt_ref, pw_ref, dr_in, du_ref, dpw_ref, g_ref, s_ref):
        del dr_in
        g = pl.program_id(0)
        win = jnp.left_shift(2, g)
        row = lax.broadcasted_iota(jnp.int32, (LS, PG), 0)
        pw = pw_ref[...].astype(CDT)
        dpw = jnp.zeros((PG, PG), F32)
        for r in range(R16):
            dm = d_ref[r]
            dpw = dpw + _dot(pt_ref[:, r * LS:(r + 1) * LS], dm, NN)
            dpooled = _dot(dm, pw, NT)
            g_ref[r] = dpooled
            s_ref[r] = dpooled / _pool_counts(r, LS, win)
        dpw_ref[...] = dpw
        for r in range(R16):
            acc = s_ref[r]
            for j in range(1, R16):
                coef = jnp.where(j < win, 1.0, 0.0).astype(F32)
                if r + j < R16:
                    src = s_ref[r + j]
                else:
                    src = jnp.where(row == LS - 1, 0.0, pltpu.roll(s_ref[r + j - R16], LS - 1, 0))
                acc = acc + coef * src
            du_ref[r] = (acc - g_ref[r]).astype(du_ref.dtype)

    du, dpw = pl.pallas_call(
        body, name="pool_bwd", grid=(4,),
        in_specs=[pl.BlockSpec((R16, LS, PG), lambda g: (0, 0, g)), pl.BlockSpec((PG, S), lambda g: (g, 0)),
                  pl.BlockSpec((None, PG, PG), lambda g: (g, 0, 0)), pl.BlockSpec(memory_space=pl.ANY)],
        out_specs=[pl.BlockSpec((R16, LS, PG), lambda g: (0, 0, 24 + g)),
                   pl.BlockSpec((None, PG, PG), lambda g: (g, 0, 0))],
        out_shape=[_sds((R16, LS, REST_W), CDT), _sds((4, PG, PG), F32)],
        scratch_shapes=[pltpu.VMEM((R16, LS, PG), F32), pltpu.VMEM((R16, LS, PG), F32)],
        input_output_aliases={3: 0},
        compiler_params=_cp(("parallel",)),
    )(d3, pt, pool_w, dr3)
    return du.reshape(S, REST_W), dpw


def _attn_bwd(qkv, dattn, Lb, dlb, biases, geom, dqkv_prev):
    S = qkv.shape[0]
    gi, T, R, has_prev, N = geom.gi, geom.T, geom.rows, geom.has_prev, geom.n_tiles
    H2 = T // 2
    nd_outer = len(geom.outer)
    qv = geom.view(qkv)
    cq, ck, cv = 3 * gi, 3 * gi + 1, 3 * gi + 2
    W3 = 3 * AW

    def heads(q_ref, kp_ref, kc_ref, vp_ref, vc_ref, do_ref, L_ref, dl_ref, ba_ref, bb_ref, pm, dsa_ref, dsb_ref, emit):
        q_lo, q_hi = _split_rows(_ld(q_ref), R)
        do_lo, do_hi = _split_rows(_ld(do_ref), R)
        L_lo, L_hi = _split_rows(_ld(L_ref), R)
        dl_lo, dl_hi = _split_rows(_ld(dl_ref), R)
        kc_lo, kc_hi = _split_rows(_ld(kc_ref), R)
        vc_lo, vc_hi = _split_rows(_ld(vc_ref), R)
        k_b = jnp.concatenate([kc_lo, kc_hi], axis=0)
        v_b = jnp.concatenate([vc_lo, vc_hi], axis=0)
        if has_prev:
            k_a = jnp.concatenate([_split_rows(_ld(kp_ref), R)[1], kc_lo], axis=0)
            v_a = jnp.concatenate([_split_rows(_ld(vp_ref), R)[1], vc_lo], axis=0)
        else:
            k_a, v_a = kc_lo, vc_lo
        low = lax.broadcasted_iota(jnp.int32, (1, 2 * HD), 1) < HD
        for j in range(NH // 2):
            sl = slice(2 * HD * j, 2 * HD * (j + 1))
            res = []
            for q, k, v, do, L, dl, b_ref, mask, ds_ref in (
                    (q_lo, k_a, v_a, do_lo, L_lo, dl_lo, ba_ref, pm, dsa_ref),
                    (q_hi, k_b, v_b, do_hi, L_hi, dl_hi, bb_ref, None, dsb_ref)):
                q2 = q[:, sl].astype(F32) * 0.125
                q2c = q2.astype(q.dtype)
                k2, v2, do2 = k[:, sl], v[:, sl], do[:, sl]
                do2f = do2.astype(F32)
                L2, dl2 = L[:, sl], dl[:, sl]
                parts = []
                for hh in range(2):
                    keep = low if hh == 0 else ~low
                    col = slice(HD * hh, HD * hh + 1)
                    qs = jnp.where(keep, q2, 0.0).astype(q.dtype)
                    dom = jnp.where(keep, do2f, 0.0).astype(q.dtype)
                    s = _dot(qs, k2, NT) + b_ref[2 * j + hh]
                    if mask is not None:
                        s = s + mask
                    p = jnp.exp(s - L2[:, col])
                    ds = p * (_dot(dom, v2, NT) - dl2[:, col])
                    ds_ref[2 * j + hh] += ds
                    dsc = ds.astype(q.dtype)
                    parts.append((_dot(dsc, k2, NN), _dot(dsc, q2c, TN), _dot(p.astype(q.dtype), do2, TN)))
                res.append((jnp.where(low, parts[0][0], parts[1][0]) * 0.125,
                            jnp.where(low, parts[0][1], parts[1][1]), jnp.where(low, parts[0][2], parts[1][2])))
            (dq_lo, dk_a, dv_a), (dq_hi, dk_b, dv_b) = res
            dq = _merge_rows(dq_lo, dq_hi, R)
            if has_prev:
                emit(sl, dq, _merge_rows(dk_a[H2:] + dk_b[:H2], dk_b[H2:], R),
                     _merge_rows(dv_a[H2:] + dv_b[:H2], dv_b[H2:], R), dk_a[:H2], dv_a[:H2])
            else:
                emit(sl, dq, _merge_rows(dk_a + dk_b[:H2], dk_b[H2:], R),
                     _merge_rows(dv_a + dv_b[:H2], dv_b[H2:], R), None, None)

    def first_step():
        ok = pl.program_id(nd_outer) == 0 if has_prev else pl.program_id(0) == 0
        for a in range(nd_outer if has_prev else 0):
            ok = ok & (pl.program_id(a) == 0)
        return ok

    if has_prev:
        def body(q_ref, kp_ref, kc_ref, vp_ref, vc_ref, do_ref, L_ref, dl_ref, ba_ref, bb_ref, prev_in, out_ref,
                 dsa_ref, dsb_ref, hold, new, pp):
            del prev_in
            s_id = pl.program_id(nd_outer)

            @pl.when(first_step())
            def _():
                dsa_ref[...] = jnp.zeros_like(dsa_ref)
                dsb_ref[...] = jnp.zeros_like(dsb_ref)

            @pl.when(s_id < N)
            def _():
                def emit(sl, dq, dk, dv, dk_prev, dv_prev):
                    new[:, sl] = dq
                    new[:, slice(AW + sl.start, AW + sl.stop)] = dk
                    new[:, slice(2 * AW + sl.start, 2 * AW + sl.stop)] = dv
                    pp[:, sl] = dk_prev
                    pp[:, slice(AW + sl.start, AW + sl.stop)] = dv_prev

                heads(q_ref, kp_ref, kc_ref, vp_ref, vc_ref, do_ref, L_ref, dl_ref, ba_ref, bb_ref,
                      _prev_mask(s_id, H2, T), dsa_ref, dsb_ref, emit)

            @pl.when((s_id >= 1) & (s_id < N))
            def _():
                _st(out_ref, hold[:, 0:AW], slice(0, AW))
                late = _merge_rows(jnp.zeros((H2, 2 * AW), F32), pp[...], R)
                _st(out_ref, hold[:, AW:W3] + late, slice(AW, W3))

            @pl.when(s_id == N)
            def _():
                _st(out_ref, hold[...])

            @pl.when(s_id < N)
            def _():
                hold[...] = new[...]

        cur = lambda n: jnp.minimum(n, N - 1)
        prev = lambda n: jnp.clip(n - 1, 0, N - 1)
        delayed = lambda n: jnp.maximum(n - 1, 0)
        in_specs = [geom.spec(AW, cq, cur), geom.spec(AW, ck, prev), geom.spec(AW, ck, cur),
                    geom.spec(AW, cv, prev), geom.spec(AW, cv, cur),
                    geom.spec(AW, 0, cur), geom.spec(AW, 0, cur), geom.spec(AW, 0, cur)]
        args = [qv] * 5 + [geom.view(dattn), geom.view(Lb), geom.view(dlb)]
        out_spec = geom.spec(W3, gi, delayed)
        grid = geom.outer + (N + 1,)
        scratch = [pltpu.VMEM((T, W3), F32), pltpu.VMEM((T, W3), F32), pltpu.VMEM((H2, 2 * AW), F32)]
    else:
        def body(q_ref, kc_ref, vc_ref, do_ref, L_ref, dl_ref, ba_ref, bb_ref, prev_in, out_ref, dsa_ref, dsb_ref):
            del prev_in

            @pl.when(first_step())
            def _():
                dsa_ref[...] = jnp.zeros_like(dsa_ref)
                dsb_ref[...] = jnp.zeros_like(dsb_ref)

            def emit(sl, dq, dk, dv, dk_prev, dv_prev):
                _st(out_ref, dq, sl)
                _st(out_ref, dk, slice(AW + sl.start, AW + sl.stop))
                _st(out_ref, dv, slice(2 * AW + sl.start, 2 * AW + sl.stop))

            heads(q_ref, None, kc_ref, None, vc_ref, do_ref, L_ref, dl_ref, ba_ref, bb_ref, None, dsa_ref, dsb_ref,
                  emit)

        cur = lambda n: n
        in_specs = [geom.spec(AW, cq, cur), geom.spec(AW, ck, cur), geom.spec(AW, cv, cur),
                    geom.spec(AW, 0, cur), geom.spec(AW, 0, cur), geom.spec(AW, 0, cur)]
        args = [qv] * 3 + [geom.view(dattn), geom.view(Lb), geom.view(dlb)]
        out_spec = geom.spec(W3, gi, cur)
        grid = geom.outer
        scratch = []

    whole = lambda b: pl.BlockSpec(b.shape, lambda *ids: (0, 0, 0))
    in_specs += [whole(b) for b in biases]
    in_specs.append(pl.BlockSpec(memory_space=pl.ANY))
    if dqkv_prev is None:
        last, aliases = biases[0], {}
    else:
        last, aliases = geom.view(dqkv_prev), {len(in_specs) - 1: 0}
    dq, dsa, dsb = pl.pallas_call(
        body, name=f"attn_bwd{gi}", grid=grid, in_specs=in_specs,
        out_specs=[out_spec] + [whole(b) for b in biases],
        out_shape=[_sds(geom.view_shape(QKV_W), CDT)] + [_sds(b.shape, F32) for b in biases],
        scratch_shapes=scratch,
        input_output_aliases=aliases,
        compiler_params=_cp(("arbitrary",) * len(grid), vmem=56),
    )(*args, *biases, last)
    return geom.unview(dq), [dsa, dsb]


def _h_bwd(dh, dx2, x, norm_g, scale):
    S = x.shape[0]

    def body(dh_ref, dx2_ref, x_ref, g_ref, sc_ref, gx_ref, dsh_ref, dsc_ref, dng_ref, nat_ref):
        i = pl.program_id(0)

        @pl.when(i == 0)
        def _():
            dsh_ref[...] = jnp.zeros_like(dsh_ref)
            dsc_ref[...] = jnp.zeros_like(dsc_ref)
            dng_ref[...] = jnp.zeros_like(dng_ref)

        for r in range(R16):
            nat_ref[:, r, :] = dh_ref[r]
        dh = jnp.concatenate([nat_ref[l] for l in range(R16)], axis=0)
        g = g_ref[...]
        one_sc = 1.0 + sc_ref[...]
        xv = x_ref[...]
        r1 = lax.rsqrt(jnp.mean(xv * xv, axis=-1, keepdims=True) + EPS)
        xn = xv * r1
        dhx = dh * xn
        dxn = dh * (g * one_sc)
        dx1 = r1 * (dxn - xn * jnp.mean(dxn * xn, axis=-1, keepdims=True))
        gx_ref[...] = dx2_ref[...] + dx1
        dsh_ref[...] += jnp.sum(dh, axis=0, keepdims=True)
        dsc_ref[...] += jnp.sum(dhx * g, axis=0, keepdims=True)
        dng_ref[...] += jnp.sum(dhx * one_sc, axis=0, keepdims=True)

    nat = pl.BlockSpec((TOK, D), lambda i: (i, 0))
    vec = pl.BlockSpec((1, D), lambda i: (0, 0))
    return pl.pallas_call(
        body, name="h_bwd", grid=(S // TOK,),
        in_specs=[pl.BlockSpec((R16, R16, D), lambda i: (0, i, 0)), nat, nat, vec, vec],
        out_specs=[nat, vec, vec, vec],
        out_shape=[_sds((S, D), F32), _sds((1, D), F32), _sds((1, D), F32), _sds((1, D), F32)],
        scratch_shapes=[pltpu.VMEM((R16, R16, D), F32)],
        compiler_params=_cp(("arbitrary",)),
    )(dh, dx2, x, norm_g, scale)


def _local_step(x, target, shift, scale, gate, norm_g, w_in_full, pool_w, pool_scale, wab, wpb, wout, rel_bias,
                final_g):
    st = _local_pre(x, shift, scale, norm_g, rel_bias)
    st = _local_grads(st, target, gate, w_in_full, pool_w, pool_scale, wab, wpb, wout, final_g)
    d_rel_bias = _local_bias_grads(st)
    gx, dmod, dng = _local_input_grads(st, w_in_full, norm_g, scale)
    return dict(loss=st["loss"], grad_x=gx, dw_in=st["dw_in"], dw_ab=st["dw_ab"], dw_pb=st["dw_pb"],
                dw_out=st["dw_out"], d_norm_g=dng, d_pool_w=st["dpw"], d_pool_scale=st["dps"],
                d_rel_bias=d_rel_bias, d_final_g=st["gfg"], dmod=dmod)


def _local_pre(x, shift, scale, norm_g, rel_bias, dep=None):
    S = x.shape[0]
    LS = S // R16
    geoms = [_Geom(gi, LS) for gi in range(3)]
    idxs = [[jnp.asarray(g.bucket_index(sub)) for sub in range(2)] for g in geoms]
    biases = [[_bias_table(idxs[gi][sub], rel_bias, gi, sub, dep) for sub in range(2)] for gi in range(3)]
    h = _norm_mod(x, norm_g, shift, scale, dep).reshape(S, D)
    return dict(geoms=geoms, idxs=idxs, biases=biases, x=x, h=h, ht=_tcast(h, "h_transpose"))


def _local_grads(st, target, gate, w_in_full, pool_w, pool_scale, wab, wpb, wout, final_g):
    geoms, idxs, biases, x, h, ht = (st[k] for k in ("geoms", "idxs", "biases", "x", "h", "ht"))
    S = h.shape[0]
    LS = S // R16
    qkv, rest = _proj(h, w_in_full)

    os_, ls_ = [], []
    for gi in range(3):
        o, l = _attn_fwd(qkv, biases[gi], geoms[gi])
        os_.append(o)
        ls_.append(l)
    attn, Lb, ag, agt = _attn_combine(os_, ls_, rest)
    pt, mp, pg, pgt = _pool_fwd(rest, pool_w, pool_scale)
    merged, mt = _branch_merge(ag, pg, wab, wpb, rest)
    dx2, dmo3, loss, gfg, dgate = _out_loss(merged.reshape(R16, LS, D), wout, x, target, gate, final_g)
    dmo = dmo3.reshape(S, D)

    dya, dyp, drest, dattn, dlb, dmp, dps = _bwd_gates(dmo, wout, wab, wpb, rest, ag, pg, attn, mp, pool_scale)
    dw_out = _mm_tn(mt, dmo, n_total=D, out_blk=lambda j: j, name="dw_out")
    dw_ab = _mm_tn(agt, dya, n_total=D, out_blk=lambda j: j, name="dw_attn_br")
    dw_pb = _mm_tn(pgt, dyp, n_total=D, out_blk=lambda j: j, name="dw_pool_br")
    drest, dpw = _pool_bwd(dmp, pt, pool_w, drest)

    dqkv = None
    dsums = []
    for gi in range(3):
        dqkv, dsum = _attn_bwd(qkv, dattn, Lb, dlb, biases[gi], geoms[gi], dqkv)
        dsums.append(dsum)

    dw_in = _mm_tn(ht, dqkv, n_total=8192, out_blk=lambda j: j, name="dw_in_qkv")
    dw_in = _mm_tn(ht, drest, n_total=8192, out_blk=_rest_blk, name="dw_in_rest", prev=dw_in)
    return dict(st, loss=loss[0, 0], dw_in=dw_in, dw_ab=dw_ab, dw_pb=dw_pb, dw_out=dw_out, dpw=dpw, dps=dps, gfg=gfg,
                dgate=dgate, dsums=dsums, dqkv=dqkv, drest=drest, dx2=dx2)


def _local_bias_grads(st, dep=None):
    drb = [_bias_grad(st["dsums"][gi][0], st["idxs"][gi][0], gi, 0, dep if gi == 0 else None) +
           _bias_grad(st["dsums"][gi][1], st["idxs"][gi][1], gi, 1) for gi in range(3)]
    return jnp.concatenate(drb, axis=1)


def _local_input_grads(st, w_in_full, norm_g, scale, dep=None):
    S = st["h"].shape[0]
    LS = S // R16
    dh = _dh(st["dqkv"], st["drest"], w_in_full, dep)
    gx, dsh, dsc, dng = _h_bwd(dh.reshape(R16, LS, D), st["dx2"], st["x"], norm_g, scale)
    return gx, jnp.concatenate([dsh, dsc, st["dgate"]], axis=1), dng


def _my_place():
    return lax.axis_index("x"), lax.axis_index("y"), lax.axis_index("c")


def _block_of(ref, axis, idx, width):
    if axis is None:
        return ref.at[idx]
    start = pl.multiple_of(idx * width, width)
    if axis == 0:
        return ref.at[pl.ds(start, width)]
    return ref.at[:, pl.ds(start, width)]


_SEM = pl.BlockSpec(memory_space=pltpu.SEMAPHORE)
_EFFECT = pltpu.SideEffectType.DATAFLOW_SIDE_EFFECTING


def _split_start(name, srcs, lands, ncopy, plan, dep=None):
    arrs = list(srcs) + list(lands)
    na, ns = len(arrs), len(srcs)
    dep_specs, dep_args = _after(dep)

    def body(*refs):
        send_sems, recv_sems, token = refs[na + len(dep_args)], refs[na + len(dep_args) + 1], refs[-1]
        for k, (s, d, dev) in enumerate(plan(refs[:ns], refs[ns:na])):
            pltpu.make_async_remote_copy(src_ref=s, dst_ref=d, send_sem=send_sems.at[k], recv_sem=recv_sems.at[k],
                                         device_id=dev, device_id_type=MESH).start()
        token[...] = jnp.zeros_like(token)

    outs = pl.pallas_call(
        body, name=name,
        out_shape=(pltpu.SemaphoreType.DMA((ncopy,)), pltpu.SemaphoreType.DMA((ncopy,)),
                   *[pltpu.HBM(a.shape, a.dtype) for a in arrs], _sds((8, 128), F32)),
        in_specs=[_HBM] * na + dep_specs,
        out_specs=(_SEM, _SEM, *[_HBM] * na, pl.BlockSpec(memory_space=pltpu.VMEM)),
        input_output_aliases={i: 2 + i for i in range(na)},
        compiler_params=pltpu.CompilerParams(has_side_effects=_EFFECT),
    )(*[pltpu.with_memory_space_constraint(a, pltpu.HBM) for a in arrs], *dep_args)
    return (outs[0], outs[1]), list(outs[2:2 + ns]), list(outs[2 + ns:2 + na]), outs[-1]


def _split_wait(name, sems, srcs, lands, plan, after):
    arrs = list(srcs) + list(lands)
    na, ns = len(arrs), len(srcs)
    after = list(after) if isinstance(after, (list, tuple)) else [after]

    def body(*refs):
        send_sems, recv_sems = refs[na], refs[na + 1]
        for k, (s, d, dev) in enumerate(plan(refs[:ns], refs[ns:na])):
            cp = pltpu.make_async_remote_copy(src_ref=s, dst_ref=d, send_sem=send_sems.at[k], recv_sem=recv_sems.at[k],
                                              device_id=dev, device_id_type=MESH)
            cp.wait_send()
            cp.wait_recv()

    outs = pl.pallas_call(
        body, name=name, out_shape=[pltpu.HBM(a.shape, a.dtype) for a in arrs],
        in_specs=[_HBM] * na + [_SEM, _SEM] + [pl.BlockSpec(memory_space=pl.ANY)] * len(after),
        out_specs=[_HBM] * na,
        input_output_aliases={i: i for i in range(na)},
        compiler_params=pltpu.CompilerParams(has_side_effects=_EFFECT),
    )(*arrs, sems[0], sems[1], *after)
    return list(outs[:ns]), list(outs[ns:])


def _gather_chips_plan(layout):
    def plan(src, land):
        x, y, c = _my_place()
        me = 4 * x + 2 * y + c
        return [(src[p], _block_of(land[p], layout[p][0], me, layout[p][1]), (cx, cy, c))
                for p in range(len(src)) for cx, cy in [(1 - x, y), (x, 1 - y), (1 - x, 1 - y)]]
    return plan


def _pair_plan(layout):
    def plan(src, land):
        x, y, c = _my_place()
        n = len(layout)
        return [(_block_of(src[p], layout[p][0], 2 * q + (1 - c), layout[p][1]), land[p].at[q], (x, y, 1 - c))
                for p in range(n) for q in range(4)] + [(src[n], land[n], (x, y, 1 - c))]
    return plan


def _chips_plan(src, land):
    x, y, c = _my_place()
    n = len(src) - 1
    chips = [(1 - x, y), (x, 1 - y), (1 - x, 1 - y)]
    return [(src[p].at[2 * cx + cy], land[p].at[2 * x + y], (cx, cy, c)) for p in range(n) for cx, cy in chips] + \
           [(src[n], land[n].at[2 * x + y], (cx, cy, c)) for cx, cy in chips]


def _pair_share(gathered):
    n = len(gathered)

    def body(*refs):
        land = refs[n:2 * n]
        send_sems, recv_sems = refs[2 * n:]
        x, y, c = _my_place()
        cps = []
        for p in range(n):
            for k in range(4):
                blk = land[p].at[2 * k + c]
                cps.append(pltpu.make_async_remote_copy(
                    src_ref=blk, dst_ref=blk, send_sem=send_sems.at[4 * p + k], recv_sem=recv_sems.at[4 * p + k],
                    device_id=(x, y, 1 - c), device_id_type=MESH))
        for cp in cps:
            cp.start()
        for cp in cps:
            cp.wait_send()
        for p in range(n):
            for k in range(4):
                blk = land[p].at[2 * k + 1 - c]
                pltpu.make_async_remote_copy(
                    src_ref=blk, dst_ref=blk, send_sem=send_sems.at[4 * p + k], recv_sem=recv_sems.at[4 * p + k],
                    device_id=(x, y, 1 - c), device_id_type=MESH).wait_recv()

    return pl.pallas_call(
        body, name="gather_pair_share", in_specs=[_HBM] * n, out_specs=[_HBM] * n,
        out_shape=[_sds(g.shape, g.dtype) for g in gathered],
        input_output_aliases={p: p for p in range(n)},
        scratch_shapes=[pltpu.SemaphoreType.DMA((4 * n,)), pltpu.SemaphoreType.DMA((4 * n,))],
    )(*gathered)


def _place_slot(arrs, slot, n_slots, name):
    n = len(arrs)

    def body(s_ref, *refs):
        for p in range(n):
            refs[n + p][...] = refs[p][...]

    def spec(a):
        tail = a.shape[-2:]
        return pl.BlockSpec((None,) + tail, lambda i, s_ref: (s_ref[0], 0, 0))

    in_specs = [spec(a) if a.ndim == 3 else pl.BlockSpec(a.shape, lambda i, s_ref: (0, 0)) for a in arrs]
    return pl.pallas_call(
        body, name=name,
        grid_spec=pltpu.PrefetchScalarGridSpec(num_scalar_prefetch=1, grid=(1,), in_specs=in_specs,
                                               out_specs=[spec(a) for a in arrs]),
        out_shape=[_sds((n_slots,) + a.shape[-2:], a.dtype) for a in arrs],
        compiler_params=_cp(("arbitrary",)),
    )(slot, *arrs)


def _cols_from_blocks(w8, name):
    _, R, C = w8.shape

    def body(w_ref, o_ref):
        for e in range(8):
            o_ref[:, C * e:C * (e + 1)] = w_ref[e]

    return pl.pallas_call(body, name=name, out_shape=_sds((R, 8 * C), w8.dtype), compiler_params=_cp())(w8)


def _all_gather(srcs, out_shapes, layout, name):
    n = len(srcs)

    def body(*refs):
        src, out = refs[:n], refs[n:2 * n]
        send_sems, recv_sems, local_sems = refs[2 * n:]
        x, y, c = _my_place()
        me, sibling = (x, y, c), (x, y, 1 - c)
        chips = [(1 - x, y), (x, 1 - y), (1 - x, 1 - y)]

        def region(p, dev):
            px, py, pc = dev
            return _block_of(out[p], layout[p][0], 4 * px + 2 * py + pc, layout[p][1])

        def copy(p, k, block, to, src_ref=None):
            return pltpu.make_async_remote_copy(
                src_ref=region(p, block) if src_ref is None else src_ref, dst_ref=region(p, block),
                send_sem=send_sems.at[p * 7 + k], recv_sem=recv_sems.at[p * 7 + k],
                device_id=to, device_id_type=MESH)

        mine = [pltpu.make_async_copy(src[p], region(p, me), local_sems.at[p]) for p in range(n)]
        for cp in mine:
            cp.start()
        first = []
        for p in range(n):
            first.append(copy(p, 0, me, sibling, src[p]))
            first += [copy(p, 1 + j, me, (*chip, c), src[p]) for j, chip in enumerate(chips)]
        for cp in first:
            cp.start()
        passed = []
        for j, chip in enumerate(chips):
            for p in range(n):
                copy(p, 1 + j, (*chip, c), me).wait_recv()
                fwd = copy(p, 4 + j, (*chip, c), sibling)
                fwd.start()
                passed.append(fwd)
        for p in range(n):
            copy(p, 0, sibling, me).wait_recv()
            for j, chip in enumerate(chips):
                copy(p, 4 + j, (*chip, 1 - c), me).wait_recv()
        for cp in first + passed:
            cp.wait_send()
        for cp in mine:
            cp.wait()

    return pl.pallas_call(
        body, name=name, in_specs=[_HBM] * n, out_specs=[_HBM] * n,
        out_shape=[_sds(s, a.dtype) for s, a in zip(out_shapes, srcs)],
        scratch_shapes=[pltpu.SemaphoreType.DMA((7 * n,)), pltpu.SemaphoreType.DMA((7 * n,)),
                        pltpu.SemaphoreType.DMA((n,))],
    )(*srcs)


def _reduce_pair_exchange(dws, layout, smalls):
    n = len(dws)

    def body(*refs):
        src, sm = refs[:n], refs[n]
        out, sm_out = refs[n + 1:2 * n + 1], refs[2 * n + 1]
        send_sems, recv_sems = refs[2 * n + 2:]
        x, y, c = _my_place()
        sibling = (x, y, 1 - c)
        cps = []
        for p in range(n):
            for q in range(4):
                cps.append(pltpu.make_async_remote_copy(
                    src_ref=_block_of(src[p], layout[p][0], 2 * q + (1 - c), layout[p][1]), dst_ref=out[p].at[q],
                    send_sem=send_sems.at[4 * p + q], recv_sem=recv_sems.at[4 * p + q],
                    device_id=sibling, device_id_type=MESH))
        cps.append(pltpu.make_async_remote_copy(src_ref=sm, dst_ref=sm_out, send_sem=send_sems.at[4 * n],
                                                recv_sem=recv_sems.at[4 * n], device_id=sibling, device_id_type=MESH))
        for cp in cps:
            cp.start()
        for cp in cps:
            cp.wait()

    def shard_shape(a, lay):
        axis, w = lay
        return (4, w, a.shape[1]) if axis == 0 else (4, a.shape[0], w)

    return pl.pallas_call(
        body, name="reduce_pair_exchange", in_specs=[_HBM] * (n + 1), out_specs=[_HBM] * (n + 1),
        out_shape=[_sds(shard_shape(a, l), F32) for a, l in zip(dws, layout)] + [_sds(smalls.shape, F32)],
        scratch_shapes=[pltpu.SemaphoreType.DMA((4 * n + 1,)), pltpu.SemaphoreType.DMA((4 * n + 1,))],
    )(*dws, smalls)


def _pair_add(dw, recv, axis, width, cidx, name):
    if axis == 0:
        tr = min(width, 256)
        C = dw.shape[1]
        grid = (4, width // tr)
        nb = width // tr
        own = pl.BlockSpec((tr, C), lambda q, i, c_ref: ((2 * q + c_ref[0]) * nb + i, 0))
        oth = pl.BlockSpec((None, tr, C), lambda q, i, c_ref: (q, i, 0))
    else:
        R = dw.shape[0]
        tr = min(R, 256)
        grid = (4, R // tr)
        own = pl.BlockSpec((tr, width), lambda q, i, c_ref: (i, 2 * q + c_ref[0]))
        oth = pl.BlockSpec((None, tr, width), lambda q, i, c_ref: (q, i, 0))

    def body(c_ref, a_ref, b_ref, o_ref):
        o_ref[...] = (a_ref[...] + b_ref[...]).astype(o_ref.dtype)

    return pl.pallas_call(
        body, name=name,
        grid_spec=pltpu.PrefetchScalarGridSpec(num_scalar_prefetch=1, grid=grid, in_specs=[own, oth], out_specs=oth),
        out_shape=_sds(recv.shape, CDT), compiler_params=_cp(("parallel", "parallel")),
    )(cidx, dw, recv)


def _add2(a, b, name):
    def body(a_ref, b_ref, o_ref):
        o_ref[...] = a_ref[...] + b_ref[...]

    return pl.pallas_call(body, name=name, out_shape=_sds(a.shape, F32))(a, b)


def _reduce_chip_exchange(parts):
    n = len(parts)

    def body(*refs):
        src, out = refs[:n], refs[n:2 * n]
        send_sems, recv_sems, local_sems = refs[2 * n:]
        x, y, c = _my_place()
        my_slot = 2 * x + y
        chips = [(1 - x, y), (x, 1 - y), (1 - x, 1 - y)]
        cps = []
        for p in range(n):
            keep = pltpu.make_async_copy(src[p].at[my_slot], out[p].at[my_slot], local_sems.at[p])
            keep.start()
            cps.append(keep)
            for k, (cx, cy) in enumerate(chips):
                cps.append(pltpu.make_async_remote_copy(
                    src_ref=src[p].at[2 * cx + cy], dst_ref=out[p].at[my_slot],
                    send_sem=send_sems.at[3 * p + k], recv_sem=recv_sems.at[3 * p + k],
                    device_id=(cx, cy, c), device_id_type=MESH))
                cps[-1].start()
        for cp in cps:
            cp.wait()

    return pl.pallas_call(
        body, name="reduce_chip_exchange", in_specs=[_HBM] * n, out_specs=[_HBM] * n,
        out_shape=[_sds(a.shape, a.dtype) for a in parts],
        scratch_shapes=[pltpu.SemaphoreType.DMA((3 * n,)), pltpu.SemaphoreType.DMA((3 * n,)),
                        pltpu.SemaphoreType.DMA((n,))],
    )(*parts)


def _adaln_part(c_all, w_ada, b_part, dep=None):
    dep_specs, dep_args = _after(dep)
    vmem = pl.BlockSpec(memory_space=pltpu.VMEM)

    def body(c_ref, w_ref, b_ref, *rest):
        rest[-1][...] = _dot(c_ref[...].astype(CDT), w_ref[...].astype(CDT), NN) + b_ref[...]

    return pl.pallas_call(body, name="adaln_part", in_specs=[vmem, vmem, vmem] + dep_specs, out_specs=vmem,
                          out_shape=_sds((8, w_ada.shape[1]), F32),
                          compiler_params=_cp())(c_all, w_ada, b_part, *dep_args)


def _adam_math(w, g, m, v):
    m = ADAM_B1 * m + (1.0 - ADAM_B1) * g
    v = ADAM_B2 * v + (1.0 - ADAM_B2) * (g * g)
    m_hat = m / (1.0 - ADAM_B1 ** ADAM_STEP)
    v_hat = v / (1.0 - ADAM_B2 ** ADAM_STEP)
    delta = -ADAM_LR * (m_hat / (jnp.sqrt(v_hat) + ADAM_EPS) + ADAM_WD * w)
    return delta, m, v


def _adam(gparts, w, m, v, name):
    P, R, C = gparts.shape
    tr = R if R <= 256 else 256

    def body(g_ref, w_ref, m_ref, v_ref, go_ref, d_ref, mo_ref, vo_ref):
        g = g_ref[0].astype(F32)
        for i in range(1, P):
            g = g + g_ref[i].astype(F32)
        d, mn, vn = _adam_math(w_ref[...], g, m_ref[...], v_ref[...])
        go_ref[...] = g
        d_ref[...] = d
        mo_ref[...] = mn
        vo_ref[...] = vn

    t = pl.BlockSpec((tr, C), lambda i: (i, 0))
    return pl.pallas_call(
        body, name=name, grid=(R // tr,),
        in_specs=[pl.BlockSpec((P, tr, C), lambda i: (0, i, 0)), t, t, t], out_specs=[t, t, t, t],
        out_shape=[_sds((R, C), F32)] * 4, compiler_params=_cp(("parallel",)),
    )(gparts, w, m, v)


def _sum_parts(gparts, name):
    P, R, C = gparts.shape

    def body(g_ref, o_ref):
        g = g_ref[0]
        for i in range(1, P):
            g = g + g_ref[i]
        o_ref[...] = g

    return pl.pallas_call(body, name=name, out_shape=_sds((R, C), F32))(gparts)


def _adam_w_ada(c_t, dm, w, m, v):
    R, C = w.shape

    def body(c_ref, dm_ref, w_ref, m_ref, v_ref, go_ref, d_ref, mo_ref, vo_ref):
        g = c_ref[:, 0:1] * dm_ref[0:1, :]
        for b in range(1, 8):
            g = g + c_ref[:, b:b + 1] * dm_ref[b:b + 1, :]
        d, mn, vn = _adam_math(w_ref[...], g, m_ref[...], v_ref[...])
        go_ref[...] = g
        d_ref[...] = d
        mo_ref[...] = mn
        vo_ref[...] = vn

    return pl.pallas_call(body, name="adam_w_ada", out_shape=[_sds((R, C), F32)] * 4,
                          compiler_params=_cp())(c_t, dm, w, m, v)


_SM = dict(norm_g=(0, 8), pool_w=(8, 512), pool_scale=(520, 8), rel_bias=(528, 8), final_g=(536, 8),
           dmod=(544, 192), loss=(736, 8))
_SM_ROWS = 744


def _pack_rows(a, rows):
    flat = a.reshape(-1).astype(F32)
    return jnp.pad(flat, (0, rows * 128 - flat.shape[0])).reshape(rows, 128)


def kernel(x, c, norm_g, w_ada, b_ada, w_in, pool_w, pool_scale, w_attn_br, w_pool_br, w_out, rel_bias, final_g, loss_target, m_norm_g, m_w_ada, m_b_ada, m_w_in, m_pool_w, m_pool_scale, m_w_attn_br, m_w_pool_br, m_w_out, m_rel_bias, m_final_g, v_norm_g, v_w_ada, v_b_ada, v_w_in, v_pool_w, v_pool_scale, v_w_attn_br, v_w_pool_br, v_w_out, v_rel_bias, v_final_g):
    S = x.shape[1]
    px, py, pc = _my_place()
    me = 4 * px + 2 * py + pc
    cidx = jnp.reshape(pc, (1,)).astype(jnp.int32)

    layout = [(1, 1024), (1, 128), (1, 128), (0, 128)]
    final_g2 = final_g.reshape(1, D)
    me1 = jnp.reshape(me, (1,)).astype(jnp.int32)

    (c_rows,) = _all_gather([jnp.broadcast_to(c, (8, D))], [(64, D)], [(0, 8)], "gather_c")
    c_all = c_rows[::8]

    b_part = lax.dynamic_slice(b_ada, (0, me * 384), (1, 384))
    mod_part = _adaln_part(c_all, w_ada[0], b_part)
    (mod_all,) = _all_gather([mod_part], [(8, 8, 384)], [(None, 1)], "gather_mod")
    mod = lax.dynamic_index_in_dim(mod_all, me, axis=1, keepdims=False).reshape(1, 3 * D)
    shift, scale, gate = mod[:, :D], mod[:, D:2 * D], mod[:, 2 * D:]

    shards = [w_in[0].astype(CDT), w_attn_br[0].astype(CDT), w_pool_br[0].astype(CDT), w_out[0].astype(CDT)]
    gplan = _gather_chips_plan([(None, 1)] * 4)
    g_sems, shards, lands, tok = _split_start("gather_chips_start", shards,
                                              _place_slot(shards, me1, 8, "place_own_shards"), 12, gplan, dep=mod_all)
    st = _local_pre(x[0], shift, scale, norm_g, rel_bias, dep=tok)
    _, lands = _split_wait("gather_chips_wait", g_sems, shards, lands, gplan,
                           [st["ht"]] + [b for pair in st["biases"] for b in pair])
    w_in_full, wab8, wpb8, wout8 = _pair_share(lands)
    wab, wpb = _cols_from_blocks(wab8, "w_attn_br_cols"), _cols_from_blocks(wpb8, "w_pool_br_cols")
    wout = wout8.reshape(D, D)

    st = _local_grads(st, loss_target[0], gate, w_in_full, pool_w[0], pool_scale, wab, wpb, wout, final_g2)

    early = jnp.concatenate([_pack_rows(st["dpw"], 512), _pack_rows(st["dps"], 8), _pack_rows(st["gfg"], 8),
                             _pack_rows(st["loss"], 8)], axis=0)
    dws = [st["dw_in"], st["dw_ab"], st["dw_pb"], st["dw_out"]]
    pplan = _pair_plan(layout)
    pair_shapes = [(4, w, a.shape[1]) if ax == 0 else (4, a.shape[0], w) for a, (ax, w) in zip(dws, layout)]
    p_sems, srcs, recv, tok = _split_start("reduce_pair_start", dws + [early],
                                           [lax.empty(s, F32) for s in pair_shapes + [early.shape]], 17, pplan)
    d_rel_bias = _local_bias_grads(st, dep=tok)
    srcs, recv = _split_wait("reduce_pair_wait", p_sems, srcs, recv, pplan, d_rel_bias)
    parts = [_pair_add(dw, r, ax, w, cidx, f"pair_add{i}")
             for i, (dw, r, (ax, w)) in enumerate(zip(srcs[:4], recv[:4], layout))]
    parts.append(_add2(srcs[4], recv[4], "pair_add_smalls"))

    slot1 = jnp.reshape(2 * px + py, (1,)).astype(jnp.int32)
    c_sems, parts, lands, tok = _split_start("reduce_chips_start", parts,
                                             _place_slot(parts, slot1, 4, "place_own_partials"), 15, _chips_plan)
    grad_x, dmod, d_norm_g = _local_input_grads(st, w_in_full, norm_g, scale, dep=tok)

    late = jnp.concatenate([_pack_rows(dmod, 24), _pack_rows(d_norm_g, 8), _pack_rows(d_rel_bias, 8)], axis=0)
    (late_all,) = _all_gather([late], [(8, 40, 128)], [(None, 1)], "gather_late")

    def small_adam(gparts, w, m, v, nvalid, name):
        rows = gparts.shape[1]
        outs = _adam(gparts, _pack_rows(w, rows), _pack_rows(m, rows), _pack_rows(v, rows), f"adam_{name}")
        return [o.reshape(-1)[:nvalid].reshape(w.shape) for o in outs]

    o_norm_g = small_adam(late_all[:, 24:32], norm_g, m_norm_g, v_norm_g, D, "norm_g")
    o_rel_bias = small_adam(late_all[:, 32:40], rel_bias, m_rel_bias, v_rel_bias, 768, "rel_bias")
    dmod_all = late_all[:, 0:24].reshape(8, 3 * D)
    o_b_ada = [o.reshape(1, 3 * D) for o in _adam(dmod_all.reshape(8, 24, 128), b_ada.reshape(24, 128),
                                                   m_b_ada.reshape(24, 128), v_b_ada.reshape(24, 128), "adam_b_ada")]
    dm_mine = lax.dynamic_slice(dmod_all, (0, me * 384), (8, 384))
    o_w_ada = [o[None] for o in _adam_w_ada(c_all.T, dm_mine, w_ada[0], m_w_ada[0], v_w_ada[0])]

    _, (g_in, g_ab, g_pb, g_out, sm4) = _split_wait("reduce_chips_wait", c_sems, parts, lands, _chips_plan,
                                                    [o_w_ada[0], o_b_ada[0], o_norm_g[0], o_rel_bias[0]])
    o_w_in = [o[None] for o in _adam(g_in, w_in[0], m_w_in[0], v_w_in[0], "adam_w_in")]
    o_w_ab = [o[None] for o in _adam(g_ab, w_attn_br[0], m_w_attn_br[0], v_w_attn_br[0], "adam_w_attn_br")]
    o_w_pb = [o[None] for o in _adam(g_pb, w_pool_br[0], m_w_pool_br[0], v_w_pool_br[0], "adam_w_pool_br")]
    o_w_out = [o[None] for o in _adam(g_out, w_out[0], m_w_out[0], v_w_out[0], "adam_w_out")]
    o_pool_w = small_adam(sm4[:, 0:512], pool_w, m_pool_w, v_pool_w, 65536, "pool_w")
    o_pool_scale = small_adam(sm4[:, 512:520], pool_scale, m_pool_scale, v_pool_scale, AW, "pool_scale")
    o_final_g = small_adam(sm4[:, 520:528], final_g, m_final_g, v_final_g, D, "final_g")
    loss = _sum_parts(sm4[:, 528:536], "sum_loss")[0, 0]

    per_w = [o_norm_g, o_w_ada, o_b_ada, o_w_in, o_pool_w, o_pool_scale, o_w_ab, o_w_pb, o_w_out, o_rel_bias, o_final_g]
    outs = [loss, grad_x[None]]
    for k in range(4):
        outs += [o[k] for o in per_w]
    return tuple(outs)
```

```python
import functools
import math

import numpy as np
import jax
import jax.numpy as jnp
from jax import lax
from jax.experimental import pallas as pl
from jax.experimental.pallas import tpu as pltpu

F32 = jnp.float32
CDT = jnp.bfloat16
D = 1024
HD = 64
NH = 8
AW = 512
QKV_W = 4608
REST_W = 3584
R16 = 16
EPS = 1e-6
NEG = -1e30
N_BACK = 128
ADAM_LR, ADAM_B1, ADAM_B2, ADAM_EPS, ADAM_WD, ADAM_STEP = 0.001, 0.9, 0.999, 1e-08, 0.01, 10
MESH = pl.DeviceIdType.MESH
VMEM_MB = 1024 * 1024


def _cp(sem=None, vmem=48, **kw):
    if sem is not None:
        kw["dimension_semantics"] = sem
    return pltpu.CompilerParams(vmem_limit_bytes=vmem * VMEM_MB, **kw)


def _sds(shape, dtype):
    return jax.ShapeDtypeStruct(tuple(shape), dtype)


def _rest_blk(jj):
    return jnp.where(jj == 0, 9, jnp.where(jj == 6, 10, jj + 10))


def _sigmoid(z):
    return 1.0 / (1.0 + jnp.exp(-z))


def _dot(a, b, dims):
    return lax.dot_general(a, b, (dims, ((), ())), preferred_element_type=F32)


NN = ((1,), (0,))
NT = ((1,), (1,))
TN = ((0,), (0,))


def _ld(ref):
    if len(ref.shape) == 2:
        return ref[...]
    return jnp.concatenate([ref[a] for a in range(ref.shape[0])], axis=0)


def _st(ref, val, cols=None):
    val = val.astype(ref.dtype)
    if len(ref.shape) == 2:
        if cols is None:
            ref[...] = val
        else:
            ref[:, cols] = val
        return
    rows = ref.shape[1]
    for a in range(ref.shape[0]):
        if cols is None:
            ref[a] = val[a * rows:(a + 1) * rows]
        else:
            ref[a, :, cols] = val[a * rows:(a + 1) * rows]


REST_COLS = ((9, 10), (11, 16), (10, 11))
_HBM = pl.BlockSpec(memory_space=pltpu.HBM)


def _resident(shape):
    return pl.BlockSpec(shape, lambda *ids: (0,) * len(shape), pipeline_mode=pl.Buffered(1))


def _w512(w_ref, j):
    return w_ref[j // 2, :, 512 * (j % 2):512 * (j % 2 + 1)]


def _proj(h, w):
    S = h.shape[0]
    tm = 256

    def body(a_ref, w_ref, q_ref, r_ref):
        a = a_ref[...]
        for j in range(9):
            q_ref[:, 512 * j:512 * (j + 1)] = _dot(a, _w512(w_ref, j), NN).astype(q_ref.dtype)
        off = 0
        for lo, hi in REST_COLS:
            for j in range(lo, hi):
                r_ref[:, off:off + 512] = _dot(a, _w512(w_ref, j), NN)
                off += 512

    return pl.pallas_call(
        body, name="proj", grid=(S // tm,),
        in_specs=[pl.BlockSpec((tm, D), lambda i: (i, 0)), _resident((8, D, 1024))],
        out_specs=[pl.BlockSpec((tm, QKV_W), lambda i: (i, 0)), pl.BlockSpec((tm, REST_W), lambda i: (i, 0))],
        out_shape=[_sds((S, QKV_W), CDT), _sds((S, REST_W), F32)],
        compiler_params=_cp(("parallel",), vmem=56),
    )(h, w)


def _after(dep):
    return ([], []) if dep is None else ([pl.BlockSpec(memory_space=pl.ANY)], [dep])


def _dh(dqkv, drest, w, dep=None):
    S = dqkv.shape[0]
    tm = 256
    dep_specs, dep_args = _after(dep)

    def body(a_ref, b_ref, w_ref, *rest):
        o_ref = rest[-1]
        acc = _dot(a_ref[:, 0:512], _w512(w_ref, 0), NT)
        for j in range(1, 9):
            acc = acc + _dot(a_ref[:, 512 * j:512 * (j + 1)], _w512(w_ref, j), NT)
        off = 0
        for lo, hi in REST_COLS:
            for j in range(lo, hi):
                acc = acc + _dot(b_ref[:, off:off + 512], _w512(w_ref, j), NT)
                off += 512
        o_ref[...] = acc

    return pl.pallas_call(
        body, name="dh", grid=(S // tm,),
        in_specs=[pl.BlockSpec((tm, QKV_W), lambda i: (i, 0)), pl.BlockSpec((tm, REST_W), lambda i: (i, 0)),
                  _resident((8, D, 1024))] + dep_specs,
        out_specs=pl.BlockSpec((tm, D), lambda i: (i, 0)),
        out_shape=_sds((S, D), F32),
        compiler_params=_cp(("parallel",), vmem=56),
    )(dqkv, drest, w, *dep_args)


def _mm_tn(at, b, *, n_total, out_blk, name, prev=None, tn=512):
    R, S = at.shape
    nb = b.shape[1] // tn

    def body(*refs):
        refs[-1][...] = _dot(refs[0][...], refs[1][...], NN)

    in_specs = [_resident((R, S)), pl.BlockSpec((S, tn), lambda j: (0, j))]
    args = [at, b]
    aliases = {}
    if prev is not None:
        in_specs.append(pl.BlockSpec(memory_space=pl.ANY))
        args.append(prev)
        aliases = {2: 0}
    return pl.pallas_call(
        body, name=name, grid=(nb,), in_specs=in_specs,
        out_specs=pl.BlockSpec((R, tn), lambda j: (0, out_blk(j))),
        out_shape=_sds((R, n_total), F32),
        input_output_aliases=aliases,
        compiler_params=_cp(("parallel",)),
    )(*args)


def _tcast(a, name):
    M, C = a.shape
    tm = 512

    def body(a_ref, o_ref):
        o_ref[...] = a_ref[...].astype(F32).T.astype(o_ref.dtype)

    return pl.pallas_call(
        body, name=name, grid=(M // tm,),
        in_specs=[pl.BlockSpec((tm, C), lambda i: (i, 0))],
        out_specs=pl.BlockSpec((C, tm), lambda i: (0, i)),
        out_shape=_sds((C, M), CDT), compiler_params=_cp(("parallel",)),
    )(a)


TOK = R16 * R16


def _swap16():
    i = lax.broadcasted_iota(jnp.int32, (TOK, TOK), 0)
    j = lax.broadcasted_iota(jnp.int32, (TOK, TOK), 1)
    return jnp.where(j == jnp.bitwise_and(i, 15) * 16 + jnp.right_shift(i, 4), 1.0, 0.0).astype(CDT)


def _norm_mod(x, norm_g, shift, scale, dep=None):
    S = x.shape[0]
    dep_specs, dep_args = _after(dep)

    def body(x_ref, g_ref, sh_ref, sc_ref, *rest):
        xv = x_ref[...]
        xn = xv * lax.rsqrt(jnp.mean(xv * xv, axis=-1, keepdims=True) + EPS)
        h = (xn * (g_ref[...] * (1.0 + sc_ref[...])) + sh_ref[...]).astype(CDT)
        _st(rest[-1], _dot(_swap16(), h, NN))

    vec = pl.BlockSpec((1, D), lambda i: (0, 0))
    return pl.pallas_call(
        body, name="norm_mod", grid=(S // TOK,),
        in_specs=[pl.BlockSpec((TOK, D), lambda i: (i, 0)), vec, vec, vec] + dep_specs,
        out_specs=pl.BlockSpec((R16, R16, D), lambda i: (0, i, 0)),
        out_shape=_sds((R16, S // R16, D), CDT), compiler_params=_cp(("parallel",)),
    )(x, norm_g, shift, scale, *dep_args)


def _t5_bucket_np(n):
    nf = np.maximum(n, 1).astype(np.float32)
    large = 16 + (np.log(nf / np.float32(16)) / np.float32(math.log(2048 / 16)) * np.float32(16)).astype(np.int32)
    large = np.minimum(large, 31)
    return np.where(n < 16, n, large).astype(np.int32)


class _Geom:
    def __init__(self, gi, LS):
        self.gi, self.LS = gi, LS
        self.dil = (1, 4, 16)[gi]
        if gi == 0:
            self.lead, self.rows, self.n_tiles, self.has_prev = (R16,), 16, LS // 16, True
            a = np.arange(R16)[:, None]
            ll = np.arange(16)[None, :]
            self.pos = (16 * ll + a).reshape(-1)
        elif gi == 1:
            self.lead, self.rows, self.n_tiles, self.has_prev = (4,), 64, LS // 64, True
            a = np.arange(4)[:, None]
            ll = np.arange(64)[None, :]
            self.pos = (4 * ll + a).reshape(-1)
        else:
            self.lead, self.rows, self.n_tiles, self.has_prev = (), LS, 1, False
            self.pos = np.arange(LS)
        self.T = self.pos.shape[0]
        self.Tk = 2 * self.T if self.has_prev else self.T
        self.outer = {0: (), 1: (4,), 2: (R16,)}[gi]

    def view(self, arr):
        S, C = arr.shape
        if self.gi == 1:
            return arr.reshape(4, 4, self.LS, C)
        return arr.reshape(R16, self.LS, C)

    def view_shape(self, C):
        return (4, 4, self.LS, C) if self.gi == 1 else (R16, self.LS, C)

    def unview(self, arr):
        return arr.reshape(R16 * self.LS, arr.shape[-1])

    def spec(self, W, cblk, tile):
        if self.gi == 0:
            return pl.BlockSpec((R16, 16, W), lambda n: (0, tile(n), cblk))
        if self.gi == 1:
            return pl.BlockSpec((4, None, 64, W), lambda r, n: (0, r, tile(n), cblk))
        return pl.BlockSpec((None, self.LS, W), lambda r: (r, 0, cblk))

    def halves(self):
        R = self.rows
        idx = np.arange(self.T).reshape(-1, R)
        return idx[:, :R // 2].reshape(-1), idx[:, R // 2:].reshape(-1)

    def bucket_index(self, sub):
        lo, hi = self.halves()
        pos = self.pos
        if sub == 0:
            pq = pos[lo]
            pk = np.concatenate([pos[hi] - self.T, pos[lo]]) if self.has_prev else np.concatenate([pos[lo], pos[hi]])
        else:
            pq = pos[hi]
            pk = np.concatenate([pos[lo], pos[hi]])
        dist = pq[:, None] - pk[None, :]
        ok = (dist >= 0) & (dist <= N_BACK)
        bucket = _t5_bucket_np(np.clip(dist, 0, N_BACK) * self.dil)
        return np.where(ok, bucket, -1).astype(np.int32)


def _bias_table(idx, rel_bias, gi, sub, dep=None):
    T, Tk = idx.shape
    tr = 32
    dep_specs, dep_args = _after(dep)

    def body(idx_ref, rb_ref, *rest):
        h = pl.program_id(0)
        ix = idx_ref[...]
        acc = jnp.full(ix.shape, NEG, F32)
        for b in range(32):
            acc = jnp.where(ix == b, rb_ref[b, gi * NH + h], acc)
        rest[-1][...] = acc

    return pl.pallas_call(
        body, name=f"bias_table{gi}{'ab'[sub]}", grid=(NH, T // tr),
        in_specs=[pl.BlockSpec((tr, Tk), lambda h, i: (i, 0)), pl.BlockSpec(memory_space=pltpu.SMEM)] + dep_specs,
        out_specs=pl.BlockSpec((None, tr, Tk), lambda h, i: (h, i, 0)),
        out_shape=_sds((NH, T, Tk), F32), compiler_params=_cp(("parallel", "parallel")),
    )(idx, rel_bias, *dep_args)


def _bias_grad(dsum, idx, gi, sub, dep=None):
    _, T, Tk = dsum.shape
    dep_specs, dep_args = _after(dep)

    def body(ds_ref, idx_ref, *rest):
        o_ref, r_ref = rest[-2], rest[-1]

        def step(b, c):
            r_ref[pl.ds(b, 1), :] = jnp.sum(jnp.where(idx_ref[...] == b, ds_ref[...], 0.0), axis=0, keepdims=True)
            return c

        lax.fori_loop(0, 32, step, 0)
        o_ref[...] = jnp.broadcast_to(jnp.sum(r_ref[...], axis=1, keepdims=True), (32, 128))

    out = pl.pallas_call(
        body, name=f"bias_grad{gi}{'ab'[sub]}", grid=(NH,),
        in_specs=[pl.BlockSpec((None, T, Tk), lambda h: (h, 0, 0)), pl.BlockSpec((T, Tk), lambda h: (0, 0))] + dep_specs,
        out_specs=pl.BlockSpec((None, 32, 128), lambda h: (h, 0, 0)),
        out_shape=_sds((NH, 32, 128), F32), scratch_shapes=[pltpu.VMEM((32, Tk), F32)],
        compiler_params=_cp(("parallel",)),
    )(dsum, idx, *dep_args)
    return out[:, :, 0].T


def _prev_mask(n, T, Tk):
    col = lax.broadcasted_iota(jnp.int32, (1, Tk), 1)
    return jnp.where((col < T) & (n == 0), NEG, 0.0).astype(F32)


def _split_rows(v, R):
    A, half = v.shape[0] // R, R // 2
    via_f32 = v.dtype != F32 and half % 16 != 0
    w = v.astype(F32) if via_f32 else v
    lo = jnp.concatenate([w[a * R:a * R + half] for a in range(A)], axis=0)
    hi = jnp.concatenate([w[a * R + half:(a + 1) * R] for a in range(A)], axis=0)
    return (lo.astype(v.dtype), hi.astype(v.dtype)) if via_f32 else (lo, hi)


def _merge_rows(lo, hi, R):
    half = R // 2
    A = lo.shape[0] // half
    return jnp.concatenate([x[a * half:(a + 1) * half] for a in range(A) for x in (lo, hi)], axis=0)


def _attn_fwd(qkv, biases, geom):
    S = qkv.shape[0]
    gi, T, R, has_prev = geom.gi, geom.T, geom.rows, geom.has_prev
    H2 = T // 2
    qv = geom.view(qkv)
    cq, ck, cv = 3 * gi, 3 * gi + 1, 3 * gi + 2

    def body(*refs):
        if has_prev:
            q_ref, kp_ref, kc_ref, vp_ref, vc_ref, ba_ref, bb_ref, o_ref, l_ref = refs
        else:
            q_ref, kc_ref, vc_ref, ba_ref, bb_ref, o_ref, l_ref = refs
        q_lo, q_hi = _split_rows(_ld(q_ref), R)
        kc_lo, kc_hi = _split_rows(_ld(kc_ref), R)
        vc_lo, vc_hi = _split_rows(_ld(vc_ref), R)
        k_b = jnp.concatenate([kc_lo, kc_hi], axis=0)
        v_b = jnp.concatenate([vc_lo, vc_hi], axis=0)
        if has_prev:
            n = pl.program_id(len(geom.outer))
            k_a = jnp.concatenate([_split_rows(_ld(kp_ref), R)[1], kc_lo], axis=0)
            v_a = jnp.concatenate([_split_rows(_ld(vp_ref), R)[1], vc_lo], axis=0)
            pm = _prev_mask(n, H2, T)
        else:
            k_a, v_a, pm = k_b, v_b, None
        low = lax.broadcasted_iota(jnp.int32, (1, 2 * HD), 1) < HD
        for j in range(NH // 2):
            cols = slice(2 * HD * j, 2 * HD * (j + 1))
            o_half, l_half = [], []
            for q, k, v, b_ref, mask in ((q_lo, k_a, v_a, ba_ref, pm), (q_hi, k_b, v_b, bb_ref, None)):
                q2 = q[:, cols].astype(F32) * 0.125
                k2, v2 = k[:, cols], v[:, cols]
                qs = jnp.concatenate([jnp.where(low, q2, 0.0), jnp.where(low, 0.0, q2)], axis=0).astype(q.dtype)
                s = _dot(qs, k2, NT) + b_ref[j]
                if mask is not None:
                    s = s + mask
                m = jnp.max(s, axis=1, keepdims=True)
                p = jnp.exp(s - m)
                l = jnp.sum(p, axis=1, keepdims=True)
                pv = _dot(p.astype(v.dtype), v2, NN) / l
                lse = m + jnp.log(l)
                o_half.append(jnp.where(low, pv[:H2], pv[H2:]))
                l_half.append(jnp.where(low, lse[:H2], lse[H2:]))
            _st(o_ref, _merge_rows(o_half[0], o_half[1], R), cols)
            _st(l_ref, _merge_rows(l_half[0], l_half[1], R), cols)

    cur = lambda n: n
    prev = lambda n: jnp.maximum(n - 1, 0)
    if has_prev:
        in_specs = [geom.spec(AW, cq, cur), geom.spec(AW, ck, prev), geom.spec(AW, ck, cur),
                    geom.spec(AW, cv, prev), geom.spec(AW, cv, cur)]
        args = [qv] * 5
    else:
        in_specs = [geom.spec(AW, cq, cur), geom.spec(AW, ck, cur), geom.spec(AW, cv, cur)]
        args = [qv] * 3
    nd = len(geom.outer) + (1 if has_prev else 0)
    biases = [b.reshape(NH // 2, 2 * H2, b.shape[2]) for b in biases]
    in_specs += [pl.BlockSpec(b.shape, lambda *ids: (0, 0, 0)) for b in biases]
    grid = geom.outer + ((geom.n_tiles,) if has_prev else ())
    oshape = geom.view_shape(AW)
    o, l = pl.pallas_call(
        body, name=f"attn_fwd{gi}", grid=grid, in_specs=in_specs,
        out_specs=[geom.spec(AW, 0, cur), geom.spec(AW, 0, cur)],
        out_shape=[_sds(oshape, F32), _sds(oshape, F32)],
        compiler_params=_cp(("arbitrary",) * nd),
    )(*args, *biases)
    return geom.unview(o), geom.unview(l)


def _attn_combine(os_, ls_, rest):
    S = rest.shape[0]
    tm = 512

    def body(o0, o1, o2, l0, l1, l2, z_ref, attn_ref, L_ref, ag_ref, agt_ref):
        la, lb, lc = l0[...], l1[...], l2[...]
        m = jnp.maximum(jnp.maximum(la, lb), lc)
        ea, eb, ec = jnp.exp(la - m), jnp.exp(lb - m), jnp.exp(lc - m)
        den = ea + eb + ec
        attn = (ea * o0[...] + eb * o1[...] + ec * o2[...]) / den
        attn_ref[...] = attn
        L_ref[...] = m + jnp.log(den)
        z = z_ref[...]
        ag = attn * (z * _sigmoid(z))
        ag_ref[...] = ag.astype(ag_ref.dtype)
        agt_ref[...] = ag.T.astype(agt_ref.dtype)

    t = pl.BlockSpec((tm, AW), lambda i: (i, 0))
    return pl.pallas_call(
        body, name="attn_combine", grid=(S // tm,),
        in_specs=[t] * 6 + [pl.BlockSpec((tm, AW), lambda i: (i, 0))],
        out_specs=[t, t, t, pl.BlockSpec((AW, tm), lambda i: (0, i))],
        out_shape=[_sds((S, AW), F32), _sds((S, AW), F32), _sds((S, AW), CDT), _sds((AW, S), CDT)],
        compiler_params=_cp(("parallel",)),
    )(*os_, *ls_, rest)


def _pool_counts(r, LS, win):
    l = lax.broadcasted_iota(jnp.int32, (LS, 1), 0)
    return jnp.minimum(16 * l + (r + 1), win).astype(F32)


def _pool_fwd(rest, pool_w, pool_scale):
    S = rest.shape[0]
    LS = S // R16
    r3 = rest.reshape(R16, LS, REST_W)
    PG = 128

    def body(u_ref, z_ref, pw_ref, ps_ref, pt_ref, mp_ref, pg_ref, pgt_ref, d_ref):
        g = pl.program_id(0)
        win = jnp.left_shift(2, g)
        row = lax.broadcasted_iota(jnp.int32, (LS, PG), 0)
        for r in range(1, R16):
            d_ref[r] = jnp.where(row == 0, 0.0, pltpu.roll(u_ref[r], 1, 0))
        pw = pw_ref[...].astype(CDT)
        for r in range(R16):
            u = u_ref[r]
            acc = u
            for j in range(1, R16):
                coef = jnp.where(j < win, 1.0, 0.0).astype(F32)
                src = u_ref[r - j] if r >= j else d_ref[r - j + R16]
                acc = acc + coef * src
            pooled = acc / _pool_counts(r, LS, win) - u
            pc = pooled.astype(CDT)
            mp = _dot(pc, pw, NN)
            z = z_ref[r]
            pg = (mp * ps_ref[...]) * (z * _sigmoid(z))
            mp_ref[r] = mp
            pg_ref[r] = pg.astype(pg_ref.dtype)
            pt_ref[:, r * LS:(r + 1) * LS] = pooled.T.astype(pt_ref.dtype)
            pgt_ref[:, r * LS:(r + 1) * LS] = pg.T.astype(pgt_ref.dtype)

    col = lambda off: pl.BlockSpec((R16, LS, PG), lambda g: (0, 0, off + g))
    tsp = pl.BlockSpec((PG, S), lambda g: (g, 0))
    pt, mp, pg, pgt = pl.pallas_call(
        body, name="pool_fwd", grid=(4,),
        in_specs=[col(24), col(4), pl.BlockSpec((None, PG, PG), lambda g: (g, 0, 0)),
                  pl.BlockSpec((1, PG), lambda g: (0, g))],
        out_specs=[tsp, col(0), col(0), tsp],
        out_shape=[_sds((AW, S), CDT), _sds((R16, LS, AW), F32), _sds((R16, LS, AW), CDT), _sds((AW, S), CDT)],
        scratch_shapes=[pltpu.VMEM((R16, LS, PG), F32)],
        compiler_params=_cp(("parallel",)),
    )(r3, r3, pool_w, pool_scale)
    return pt, mp.reshape(S, AW), pg.reshape(S, AW), pgt


def _branch_merge(ag, pg, wab, wpb, rest):
    S = ag.shape[0]
    tm, tn = 512, 512

    def body(ag_ref, pg_ref, wa_ref, wp_ref, ga_ref, gp_ref, m_ref, mt_ref):
        ya = _dot(ag_ref[...], wa_ref[...], NN)
        yp = _dot(pg_ref[...], wp_ref[...], NN)
        mg = _sigmoid(ga_ref[...]) * ya + _sigmoid(gp_ref[...]) * yp
        m_ref[...] = mg.astype(m_ref.dtype)
        mt_ref[...] = mg.T.astype(mt_ref.dtype)

    a = pl.BlockSpec((tm, AW), lambda i, j: (i, 0))
    w = pl.BlockSpec((AW, tn), lambda i, j: (0, j))
    o = pl.BlockSpec((tm, tn), lambda i, j: (i, j))
    return pl.pallas_call(
        body, name="branch_merge", grid=(S // tm, D // tn),
        in_specs=[a, a, w, w, pl.BlockSpec((tm, tn), lambda i, j: (i, 2 + j)),
                  pl.BlockSpec((tm, tn), lambda i, j: (i, 4 + j))],
        out_specs=[o, pl.BlockSpec((tn, tm), lambda i, j: (j, i))],
        out_shape=[_sds((S, D), CDT), _sds((D, S), CDT)],
        compiler_params=_cp(("parallel", "parallel")),
    )(ag, pg, wab, wpb, rest, rest)


def _out_loss(merged, wout, x, target, gate, final_g):
    S = x.shape[0]

    def body(m_ref, w_ref, x_ref, t_ref, gate_ref, fg_ref, dx2_ref, dmo_ref, loss_ref, gfg_ref, dgate_ref):
        i = pl.program_id(0)

        @pl.when(i == 0)
        def _():
            loss_ref[...] = jnp.zeros_like(loss_ref)
            gfg_ref[...] = jnp.zeros_like(gfg_ref)
            dgate_ref[...] = jnp.zeros_like(dgate_ref)

        swap = _swap16()
        mo = _dot(_dot(swap, _ld(m_ref), NN).astype(CDT), w_ref[...], NN)
        gate, fg = gate_ref[...], fg_ref[...]
        x2 = x_ref[...] + gate * mo
        r2 = lax.rsqrt(jnp.mean(x2 * x2, axis=-1, keepdims=True) + EPS)
        xn2 = x2 * r2
        diff = xn2 * fg - t_ref[...]
        loss = jnp.sum(jnp.sum(diff * diff, axis=-1, keepdims=True), axis=0, keepdims=True)
        dy = diff * (1.0 / D)
        dxn2 = dy * fg
        dx2 = r2 * (dxn2 - xn2 * jnp.mean(dxn2 * xn2, axis=-1, keepdims=True))
        dx2_ref[...] = dx2
        _st(dmo_ref, _dot(swap, (dx2 * gate).astype(CDT), NN))
        loss_ref[...] += jnp.broadcast_to(loss * (0.5 / D), loss_ref.shape)
        gfg_ref[...] += jnp.sum(dy * xn2, axis=0, keepdims=True)
        dgate_ref[...] += jnp.sum(dx2 * mo, axis=0, keepdims=True)

    slab = pl.BlockSpec((R16, R16, D), lambda i: (0, i, 0))
    nat = pl.BlockSpec((TOK, D), lambda i: (i, 0))
    vec = pl.BlockSpec((1, D), lambda i: (0, 0))
    return pl.pallas_call(
        body, name="out_loss", grid=(S // TOK,),
        in_specs=[slab, _resident((D, D)), nat, nat, vec, vec],
        out_specs=[nat, slab, pl.BlockSpec((1, 128), lambda i: (0, 0)), vec, vec],
        out_shape=[_sds((S, D), F32), _sds((R16, S // R16, D), CDT), _sds((1, 128), F32), _sds((1, D), F32),
                   _sds((1, D), F32)],
        compiler_params=_cp(("arbitrary",)),
    )(merged, wout, x, target, gate, final_g)


def _bwd_gates(dmo, wout, wab, wpb, rest, ag, pg, attn, mp, pool_scale):
    S = dmo.shape[0]
    tm = 256

    def body(dmo_ref, wo_ref, wa_ref, wp_ref, za_ref, zp_ref, ga0, ga1, gp0, gp1, ag_ref, pg_ref, at_ref, mp_ref,
             ps_ref, dya_ref, dyp_ref, dr_ref, dat_ref, dl_ref, dmp_ref, dps_ref):
        i = pl.program_id(0)

        @pl.when(i == 0)
        def _():
            dps_ref[...] = jnp.zeros_like(dps_ref)

        dm = _dot(dmo_ref[...], wo_ref[...], NT)
        sa = _sigmoid(jnp.concatenate([ga0[...], ga1[...]], axis=1))
        sp = _sigmoid(jnp.concatenate([gp0[...], gp1[...]], axis=1))
        dya = (dm * sa).astype(CDT)
        dyp = (dm * sp).astype(CDT)
        dya_ref[...] = dya
        dyp_ref[...] = dyp
        ya = _dot(ag_ref[...], wa_ref[...], NN)
        yp = _dot(pg_ref[...], wp_ref[...], NN)
        dr_ref[:, 1024:2048] = (dm * ya * sa * (1.0 - sa)).astype(dr_ref.dtype)
        dr_ref[:, 2048:3072] = (dm * yp * sp * (1.0 - sp)).astype(dr_ref.dtype)
        da = _dot(dya, wa_ref[...], NT)
        dp = _dot(dyp, wp_ref[...], NT)
        za = za_ref[...]
        sga = _sigmoid(za)
        attn = at_ref[...]
        dattn = da * (za * sga)
        dr_ref[:, 0:512] = (da * attn * (sga * (1.0 + za * (1.0 - sga)))).astype(dr_ref.dtype)
        dat_ref[...] = dattn.astype(dat_ref.dtype)
        prod = dattn * attn
        for h in range(NH):
            sl = slice(HD * h, HD * h + HD)
            dl_ref[:, sl] = jnp.broadcast_to(jnp.sum(prod[:, sl], axis=1, keepdims=True), (tm, HD))
        zp = zp_ref[...]
        sgp = _sigmoid(zp)
        mpre = mp_ref[...]
        ps = ps_ref[...]
        dmixed = dp * (zp * sgp)
        dr_ref[:, 512:1024] = (dp * (mpre * ps) * (sgp * (1.0 + zp * (1.0 - sgp)))).astype(dr_ref.dtype)
        dps_ref[...] += jnp.sum(dmixed * mpre, axis=0, keepdims=True)
        dmp_ref[...] = (dmixed * ps).astype(dmp_ref.dtype)

    full = lambda shape: pl.BlockSpec(shape, lambda i: (0, 0))
    tD = pl.BlockSpec((tm, D), lambda i: (i, 0))
    tA = pl.BlockSpec((tm, AW), lambda i: (i, 0))
    rb = lambda c: pl.BlockSpec((tm, AW), lambda i: (i, c))
    return pl.pallas_call(
        body, name="bwd_gates", grid=(S // tm,),
        in_specs=[tD, full((D, D)), full((AW, D)), full((AW, D)), rb(0), rb(1), rb(2), rb(3), rb(4), rb(5),
                  tA, tA, tA, tA, full((1, AW))],
        out_specs=[tD, tD, pl.BlockSpec((tm, 3072), lambda i: (i, 0)), tA, tA, tA, full((1, AW))],
        out_shape=[_sds((S, D), CDT), _sds((S, D), CDT), _sds((S, REST_W), CDT), _sds((S, AW), CDT),
                   _sds((S, AW), F32), _sds((S, AW), CDT), _sds((1, AW), F32)],
        compiler_params=_cp(("arbitrary",), vmem=56),
    )(dmo, wout, wab, wpb, rest, rest, rest, rest, rest, rest, ag, pg, attn, mp, pool_scale)


def _pool_bwd(dmp, pt, pool_w, drest):
    S = dmp.shape[0]
    LS = S // R16
    PG = 128
    d3 = dmp.reshape(R16, LS, AW)
    dr3 = drest.reshape(R16, LS, REST_W)

    def body(d_ref, pt_ref, pw_ref, dr_in, du_ref, dpw_ref, g_ref, s_ref):
        del dr_in
        g = pl.program_id(0)
        win = jnp.left_shift(2, g)
        row = lax.broadcasted_iota(jnp.int32, (LS, PG), 0)
        pw = pw_ref[...].astype(CDT)
        dpw = jnp.zeros((PG, PG), F32)
        for r in range(R16):
            dm = d_ref[r]
            dpw = dpw + _dot(pt_ref[:, r * LS:(r + 1) * LS], dm, NN)
            dpooled = _dot(dm, pw, NT)
            g_ref[r] = dpooled
            s_ref[r] = dpooled / _pool_counts(r, LS, win)
        dpw_ref[...] = dpw
        for r in range(R16):
            acc = s_ref[r]
            for j in range(1, R16):
                coef = jnp.where(j < win, 1.0, 0.0).astype(F32)
                if r + j < R16:
                    src = s_ref[r + j]
                else:
                    src = jnp.where(row == LS - 1, 0.0, pltpu.roll(s_ref[r + j - R16], LS - 1, 0))
                acc = acc + coef * src
            du_ref[r] = (acc - g_ref[r]).astype(du_ref.dtype)

    du, dpw = pl.pallas_call(
        body, name="pool_bwd", grid=(4,),
        in_specs=[pl.BlockSpec((R16, LS, PG), lambda g: (0, 0, g)), pl.BlockSpec((PG, S), lambda g: (g, 0)),
                  pl.BlockSpec((None, PG, PG), lambda g: (g, 0, 0)), pl.BlockSpec(memory_space=pl.ANY)],
        out_specs=[pl.BlockSpec((R16, LS, PG), lambda g: (0, 0, 24 + g)),
                   pl.BlockSpec((None, PG, PG), lambda g: (g, 0, 0))],
        out_shape=[_sds((R16, LS, REST_W), CDT), _sds((4, PG, PG), F32)],
        scratch_shapes=[pltpu.VMEM((R16, LS, PG), F32), pltpu.VMEM((R16, LS, PG), F32)],
        input_output_aliases={3: 0},
        compiler_params=_cp(("parallel",)),
    )(d3, pt, pool_w, dr3)
    return du.reshape(S, REST_W), dpw


def _attn_bwd(qkv, dattn, Lb, dlb, biases, geom, dqkv_prev):
    S = qkv.shape[0]
    gi, T, R, has_prev, N = geom.gi, geom.T, geom.rows, geom.has_prev, geom.n_tiles
    H2 = T // 2
    nd_outer = len(geom.outer)
    qv = geom.view(qkv)
    cq, ck, cv = 3 * gi, 3 * gi + 1, 3 * gi + 2
    W3 = 3 * AW

    def heads(q_ref, kp_ref, kc_ref, vp_ref, vc_ref, do_ref, L_ref, dl_ref, ba_ref, bb_ref, pm, dsa_ref, dsb_ref, emit):
        q_lo, q_hi = _split_rows(_ld(q_ref), R)
        do_lo, do_hi = _split_rows(_ld(do_ref), R)
        L_lo, L_hi = _split_rows(_ld(L_ref), R)
        dl_lo, dl_hi = _split_rows(_ld(dl_ref), R)
        kc_lo, kc_hi = _split_rows(_ld(kc_ref), R)
        vc_lo, vc_hi = _split_rows(_ld(vc_ref), R)
        k_b = jnp.concatenate([kc_lo, kc_hi], axis=0)
        v_b = jnp.concatenate([vc_lo, vc_hi], axis=0)
        if has_prev:
            k_a = jnp.concatenate([_split_rows(_ld(kp_ref), R)[1], kc_lo], axis=0)
            v_a = jnp.concatenate([_split_rows(_ld(vp_ref), R)[1], vc_lo], axis=0)
        else:
            k_a, v_a = k_b, v_b
        low = lax.broadcasted_iota(jnp.int32, (1, 2 * HD), 1) < HD
        for j in range(NH // 2):
            sl = slice(2 * HD * j, 2 * HD * (j + 1))
            res = []
            for q, k, v, do, L, dl, b_ref, mask, ds_ref in (
                    (q_lo, k_a, v_a, do_lo, L_lo, dl_lo, ba_ref, pm, dsa_ref),
                    (q_hi, k_b, v_b, do_hi, L_hi, dl_hi, bb_ref, None, dsb_ref)):
                q2 = q[:, sl].astype(F32) * 0.125
                do2 = do[:, sl].astype(F32)
                k2, v2 = k[:, sl], v[:, sl]
                L2, dl2 = L[:, sl], dl[:, sl]
                qs = jnp.concatenate([jnp.where(low, q2, 0.0), jnp.where(low, 0.0, q2)], axis=0).astype(q.dtype)
                dom = jnp.concatenate([jnp.where(low, do2, 0.0), jnp.where(low, 0.0, do2)], axis=0).astype(q.dtype)
                Lc = jnp.concatenate([L2[:, 0:1], L2[:, HD:HD + 1]], axis=0)
                dlc = jnp.concatenate([dl2[:, 0:1], dl2[:, HD:HD + 1]], axis=0)
                s = _dot(qs, k2, NT) + b_ref[j]
                if mask is not None:
                    s = s + mask
                p = jnp.exp(s - Lc)
                ds = p * (_dot(dom, v2, NT) - dlc)
                ds_ref[j] += ds
                dsc = ds.astype(q.dtype)
                dq_st = _dot(dsc, k2, NN)
                res.append((jnp.where(low, dq_st[:H2], dq_st[H2:]) * 0.125, _dot(dsc, qs, TN),
                            _dot(p.astype(q.dtype), dom, TN)))
            (dq_lo, dk_a, dv_a), (dq_hi, dk_b, dv_b) = res
            dq = _merge_rows(dq_lo, dq_hi, R)
            if has_prev:
                emit(sl, dq, _merge_rows(dk_a[H2:] + dk_b[:H2], dk_b[H2:], R),
                     _merge_rows(dv_a[H2:] + dv_b[:H2], dv_b[H2:], R), dk_a[:H2], dv_a[:H2])
            else:
                emit(sl, dq, _merge_rows(dk_a[:H2] + dk_b[:H2], dk_a[H2:] + dk_b[H2:], R),
                     _merge_rows(dv_a[:H2] + dv_b[:H2], dv_a[H2:] + dv_b[H2:], R), None, None)

    def first_step():
        ok = pl.program_id(nd_outer) == 0 if has_prev else pl.program_id(0) == 0
        for a in range(nd_outer if has_prev else 0):
            ok = ok & (pl.program_id(a) == 0)
        return ok

    if has_prev:
        def body(q_ref, kp_ref, kc_ref, vp_ref, vc_ref, do_ref, L_ref, dl_ref, ba_ref, bb_ref, prev_in, out_ref,
                 dsa_ref, dsb_ref, hold, new, pp):
            del prev_in
            s_id = pl.program_id(nd_outer)

            @pl.when(first_step())
            def _():
                dsa_ref[...] = jnp.zeros_like(dsa_ref)
                dsb_ref[...] = jnp.zeros_like(dsb_ref)

            @pl.when(s_id < N)
            def _():
                def emit(sl, dq, dk, dv, dk_prev, dv_prev):
                    new[:, sl] = dq
                    new[:, slice(AW + sl.start, AW + sl.stop)] = dk
                    new[:, slice(2 * AW + sl.start, 2 * AW + sl.stop)] = dv
                    pp[:, sl] = dk_prev
                    pp[:, slice(AW + sl.start, AW + sl.stop)] = dv_prev

                heads(q_ref, kp_ref, kc_ref, vp_ref, vc_ref, do_ref, L_ref, dl_ref, ba_ref, bb_ref,
                      _prev_mask(s_id, H2, T), dsa_ref, dsb_ref, emit)

            @pl.when((s_id >= 1) & (s_id < N))
            def _():
                _st(out_ref, hold[:, 0:AW], slice(0, AW))
                late = _merge_rows(jnp.zeros((H2, 2 * AW), F32), pp[...], R)
                _st(out_ref, hold[:, AW:W3] + late, slice(AW, W3))

            @pl.when(s_id == N)
            def _():
                _st(out_ref, hold[...])

            @pl.when(s_id < N)
            def _():
                hold[...] = new[...]

        cur = lambda n: jnp.minimum(n, N - 1)
        prev = lambda n: jnp.clip(n - 1, 0, N - 1)
        delayed = lambda n: jnp.maximum(n - 1, 0)
        in_specs = [geom.spec(AW, cq, cur), geom.spec(AW, ck, prev), geom.spec(AW, ck, cur),
                    geom.spec(AW, cv, prev), geom.spec(AW, cv, cur),
                    geom.spec(AW, 0, cur), geom.spec(AW, 0, cur), geom.spec(AW, 0, cur)]
        args = [qv] * 5 + [geom.view(dattn), geom.view(Lb), geom.view(dlb)]
        out_spec = geom.spec(W3, gi, delayed)
        grid = geom.outer + (N + 1,)
        scratch = [pltpu.VMEM((T, W3), F32), pltpu.VMEM((T, W3), F32), pltpu.VMEM((H2, 2 * AW), F32)]
    else:
        def body(q_ref, kc_ref, vc_ref, do_ref, L_ref, dl_ref, ba_ref, bb_ref, prev_in, out_ref, dsa_ref, dsb_ref):
            del prev_in

            @pl.when(first_step())
            def _():
                dsa_ref[...] = jnp.zeros_like(dsa_ref)
                dsb_ref[...] = jnp.zeros_like(dsb_ref)

            def emit(sl, dq, dk, dv, dk_prev, dv_prev):
                _st(out_ref, dq, sl)
                _st(out_ref, dk, slice(AW + sl.start, AW + sl.stop))
                _st(out_ref, dv, slice(2 * AW + sl.start, 2 * AW + sl.stop))

            heads(q_ref, None, kc_ref, None, vc_ref, do_ref, L_ref, dl_ref, ba_ref, bb_ref, None, dsa_ref, dsb_ref,
                  emit)

        cur = lambda n: n
        in_specs = [geom.spec(AW, cq, cur), geom.spec(AW, ck, cur), geom.spec(AW, cv, cur),
                    geom.spec(AW, 0, cur), geom.spec(AW, 0, cur), geom.spec(AW, 0, cur)]
        args = [qv] * 3 + [geom.view(dattn), geom.view(Lb), geom.view(dlb)]
        out_spec = geom.spec(W3, gi, cur)
        grid = geom.outer
        scratch = []

    whole = lambda b: pl.BlockSpec(b.shape, lambda *ids: (0, 0, 0))
    biases = [b.reshape(NH // 2, 2 * H2, b.shape[2]) for b in biases]
    in_specs += [whole(b) for b in biases]
    in_specs.append(pl.BlockSpec(memory_space=pl.ANY))
    if dqkv_prev is None:
        last, aliases = biases[0], {}
    else:
        last, aliases = geom.view(dqkv_prev), {len(in_specs) - 1: 0}
    dq, dsa, dsb = pl.pallas_call(
        body, name=f"attn_bwd{gi}", grid=grid, in_specs=in_specs,
        out_specs=[out_spec] + [whole(b) for b in biases],
        out_shape=[_sds(geom.view_shape(QKV_W), CDT)] + [_sds(b.shape, F32) for b in biases],
        scratch_shapes=scratch,
        input_output_aliases=aliases,
        compiler_params=_cp(("arbitrary",) * len(grid), vmem=56),
    )(*args, *biases, last)
    return geom.unview(dq), [d.reshape(NH, H2, d.shape[2]) for d in (dsa, dsb)]


def _h_bwd(dh, dx2, x, norm_g, scale):
    S = x.shape[0]

    def body(dh_ref, dx2_ref, x_ref, g_ref, sc_ref, gx_ref, dsh_ref, dsc_ref, dng_ref, nat_ref):
        i = pl.program_id(0)

        @pl.when(i == 0)
        def _():
            dsh_ref[...] = jnp.zeros_like(dsh_ref)
            dsc_ref[...] = jnp.zeros_like(dsc_ref)
            dng_ref[...] = jnp.zeros_like(dng_ref)

        for r in range(R16):
            nat_ref[:, r, :] = dh_ref[r]
        dh = jnp.concatenate([nat_ref[l] for l in range(R16)], axis=0)
        g = g_ref[...]
        one_sc = 1.0 + sc_ref[...]
        xv = x_ref[...]
        r1 = lax.rsqrt(jnp.mean(xv * xv, axis=-1, keepdims=True) + EPS)
        xn = xv * r1
        dhx = dh * xn
        dxn = dh * (g * one_sc)
        dx1 = r1 * (dxn - xn * jnp.mean(dxn * xn, axis=-1, keepdims=True))
        gx_ref[...] = dx2_ref[...] + dx1
        dsh_ref[...] += jnp.sum(dh, axis=0, keepdims=True)
        dsc_ref[...] += jnp.sum(dhx * g, axis=0, keepdims=True)
        dng_ref[...] += jnp.sum(dhx * one_sc, axis=0, keepdims=True)

    nat = pl.BlockSpec((TOK, D), lambda i: (i, 0))
    vec = pl.BlockSpec((1, D), lambda i: (0, 0))
    return pl.pallas_call(
        body, name="h_bwd", grid=(S // TOK,),
        in_specs=[pl.BlockSpec((R16, R16, D), lambda i: (0, i, 0)), nat, nat, vec, vec],
        out_specs=[nat, vec, vec, vec],
        out_shape=[_sds((S, D), F32), _sds((1, D), F32), _sds((1, D), F32), _sds((1, D), F32)],
        scratch_shapes=[pltpu.VMEM((R16, R16, D), F32)],
        compiler_params=_cp(("arbitrary",)),
    )(dh, dx2, x, norm_g, scale)


def _local_step(x, target, shift, scale, gate, norm_g, w_in_full, pool_w, pool_scale, wab, wpb, wout, rel_bias,
                final_g):
    st = _local_pre(x, shift, scale, norm_g, rel_bias)
    st = _local_grads(st, target, gate, w_in_full, pool_w, pool_scale, wab, wpb, wout, final_g)
    d_rel_bias = _local_bias_grads(st)
    gx, dmod, dng = _local_input_grads(st, w_in_full, norm_g, scale)
    return dict(loss=st["loss"], grad_x=gx, dw_in=st["dw_in"], dw_ab=st["dw_ab"], dw_pb=st["dw_pb"],
                dw_out=st["dw_out"], d_norm_g=dng, d_pool_w=st["dpw"], d_pool_scale=st["dps"],
                d_rel_bias=d_rel_bias, d_final_g=st["gfg"], dmod=dmod)


def _local_pre(x, shift, scale, norm_g, rel_bias, dep=None):
    S = x.shape[0]
    LS = S // R16
    geoms = [_Geom(gi, LS) for gi in range(3)]
    idxs = [[jnp.asarray(g.bucket_index(sub)) for sub in range(2)] for g in geoms]
    biases = [[_bias_table(idxs[gi][sub], rel_bias, gi, sub, dep) for sub in range(2)] for gi in range(3)]
    h = _norm_mod(x, norm_g, shift, scale, dep).reshape(S, D)
    return dict(geoms=geoms, idxs=idxs, biases=biases, x=x, h=h, ht=_tcast(h, "h_transpose"))


def _local_grads(st, target, gate, w_in_full, pool_w, pool_scale, wab, wpb, wout, final_g):
    geoms, idxs, biases, x, h, ht = (st[k] for k in ("geoms", "idxs", "biases", "x", "h", "ht"))
    S = h.shape[0]
    LS = S // R16
    qkv, rest = _proj(h, w_in_full)

    os_, ls_ = [], []
    for gi in range(3):
        o, l = _attn_fwd(qkv, biases[gi], geoms[gi])
        os_.append(o)
        ls_.append(l)
    attn, Lb, ag, agt = _attn_combine(os_, ls_, rest)
    pt, mp, pg, pgt = _pool_fwd(rest, pool_w, pool_scale)
    merged, mt = _branch_merge(ag, pg, wab, wpb, rest)
    dx2, dmo3, loss, gfg, dgate = _out_loss(merged.reshape(R16, LS, D), wout, x, target, gate, final_g)
    dmo = dmo3.reshape(S, D)

    dya, dyp, drest, dattn, dlb, dmp, dps = _bwd_gates(dmo, wout, wab, wpb, rest, ag, pg, attn, mp, pool_scale)
    dw_out = _mm_tn(mt, dmo, n_total=D, out_blk=lambda j: j, name="dw_out")
    dw_ab = _mm_tn(agt, dya, n_total=D, out_blk=lambda j: j, name="dw_attn_br")
    dw_pb = _mm_tn(pgt, dyp, n_total=D, out_blk=lambda j: j, name="dw_pool_br")
    drest, dpw = _pool_bwd(dmp, pt, pool_w, drest)

    dqkv = None
    dsums = []
    for gi in range(3):
        dqkv, dsum = _attn_bwd(qkv, dattn, Lb, dlb, biases[gi], geoms[gi], dqkv)
        dsums.append(dsum)

    dw_in = _mm_tn(ht, dqkv, n_total=8192, out_blk=lambda j: j, name="dw_in_qkv")
    dw_in = _mm_tn(ht, drest, n_total=8192, out_blk=_rest_blk, name="dw_in_rest", prev=dw_in)
    return dict(st, loss=loss[0, 0], dw_in=dw_in, dw_ab=dw_ab, dw_pb=dw_pb, dw_out=dw_out, dpw=dpw, dps=dps, gfg=gfg,
                dgate=dgate, dsums=dsums, dqkv=dqkv, drest=drest, dx2=dx2)


def _local_bias_grads(st, dep=None):
    drb = [_bias_grad(st["dsums"][gi][0], st["idxs"][gi][0], gi, 0, dep if gi == 0 else None) +
           _bias_grad(st["dsums"][gi][1], st["idxs"][gi][1], gi, 1) for gi in range(3)]
    return jnp.concatenate(drb, axis=1)


def _local_input_grads(st, w_in_full, norm_g, scale, dep=None):
    S = st["h"].shape[0]
    LS = S // R16
    dh = _dh(st["dqkv"], st["drest"], w_in_full, dep)
    gx, dsh, dsc, dng = _h_bwd(dh.reshape(R16, LS, D), st["dx2"], st["x"], norm_g, scale)
    return gx, jnp.concatenate([dsh, dsc, st["dgate"]], axis=1), dng


def _my_place():
    return lax.axis_index("x"), lax.axis_index("y"), lax.axis_index("c")


def _block_of(ref, axis, idx, width):
    if axis is None:
        return ref.at[idx]
    start = pl.multiple_of(idx * width, width)
    if axis == 0:
        return ref.at[pl.ds(start, width)]
    return ref.at[:, pl.ds(start, width)]


_SEM = pl.BlockSpec(memory_space=pltpu.SEMAPHORE)
_EFFECT = pltpu.SideEffectType.DATAFLOW_SIDE_EFFECTING


def _split_start(name, srcs, lands, ncopy, plan, dep=None):
    arrs = list(srcs) + list(lands)
    na, ns = len(arrs), len(srcs)
    dep_specs, dep_args = _after(dep)

    def body(*refs):
        send_sems, recv_sems, token = refs[na + len(dep_args)], refs[na + len(dep_args) + 1], refs[-1]
        for k, (s, d, dev) in enumerate(plan(refs[:ns], refs[ns:na])):
            pltpu.make_async_remote_copy(src_ref=s, dst_ref=d, send_sem=send_sems.at[k], recv_sem=recv_sems.at[k],
                                         device_id=dev, device_id_type=MESH).start()
        token[...] = jnp.zeros_like(token)

    outs = pl.pallas_call(
        body, name=name,
        out_shape=(pltpu.SemaphoreType.DMA((ncopy,)), pltpu.SemaphoreType.DMA((ncopy,)),
                   *[pltpu.HBM(a.shape, a.dtype) for a in arrs], _sds((8, 128), F32)),
        in_specs=[_HBM] * na + dep_specs,
        out_specs=(_SEM, _SEM, *[_HBM] * na, pl.BlockSpec(memory_space=pltpu.VMEM)),
        input_output_aliases={i: 2 + i for i in range(na)},
        compiler_params=pltpu.CompilerParams(has_side_effects=_EFFECT),
    )(*[pltpu.with_memory_space_constraint(a, pltpu.HBM) for a in arrs], *dep_args)
    return (outs[0], outs[1]), list(outs[2:2 + ns]), list(outs[2 + ns:2 + na]), outs[-1]


def _split_wait(name, sems, srcs, lands, plan, after):
    arrs = list(srcs) + list(lands)
    na, ns = len(arrs), len(srcs)
    after = list(after) if isinstance(after, (list, tuple)) else [after]

    def body(*refs):
        send_sems, recv_sems = refs[na], refs[na + 1]
        for k, (s, d, dev) in enumerate(plan(refs[:ns], refs[ns:na])):
            cp = pltpu.make_async_remote_copy(src_ref=s, dst_ref=d, send_sem=send_sems.at[k], recv_sem=recv_sems.at[k],
                                              device_id=dev, device_id_type=MESH)
            cp.wait_send()
            cp.wait_recv()

    outs = pl.pallas_call(
        body, name=name, out_shape=[pltpu.HBM(a.shape, a.dtype) for a in arrs],
        in_specs=[_HBM] * na + [_SEM, _SEM] + [pl.BlockSpec(memory_space=pl.ANY)] * len(after),
        out_specs=[_HBM] * na,
        input_output_aliases={i: i for i in range(na)},
        compiler_params=pltpu.CompilerParams(has_side_effects=_EFFECT),
    )(*arrs, sems[0], sems[1], *after)
    return list(outs[:ns]), list(outs[ns:])


def _gather_chips_plan(layout):
    def plan(src, land):
        x, y, c = _my_place()
        me = 4 * x + 2 * y + c
        return [(src[p], _block_of(land[p], layout[p][0], me, layout[p][1]), (cx, cy, c))
                for p in range(len(src)) for cx, cy in [(1 - x, y), (x, 1 - y), (1 - x, 1 - y)]]
    return plan


def _pair_plan(layout):
    def plan(src, land):
        x, y, c = _my_place()
        n = len(layout)
        return [(_block_of(src[p], layout[p][0], 2 * q + (1 - c), layout[p][1]), land[p].at[q], (x, y, 1 - c))
                for p in range(n) for q in range(4)] + [(src[n], land[n], (x, y, 1 - c))]
    return plan


def _chips_plan(src, land):
    x, y, c = _my_place()
    n = len(src) - 1
    chips = [(1 - x, y), (x, 1 - y), (1 - x, 1 - y)]
    return [(src[p].at[2 * cx + cy], land[p].at[2 * x + y], (cx, cy, c)) for p in range(n) for cx, cy in chips] + \
           [(src[n], land[n].at[2 * x + y], (cx, cy, c)) for cx, cy in chips]


def _pair_share(gathered):
    n = len(gathered)

    def body(*refs):
        land = refs[n:2 * n]
        send_sems, recv_sems = refs[2 * n:]
        x, y, c = _my_place()
        cps = []
        for p in range(n):
            for k in range(4):
                blk = land[p].at[2 * k + c]
                cps.append(pltpu.make_async_remote_copy(
                    src_ref=blk, dst_ref=blk, send_sem=send_sems.at[4 * p + k], recv_sem=recv_sems.at[4 * p + k],
                    device_id=(x, y, 1 - c), device_id_type=MESH))
        for cp in cps:
            cp.start()
        for cp in cps:
            cp.wait_send()
        for p in range(n):
            for k in range(4):
                blk = land[p].at[2 * k + 1 - c]
                pltpu.make_async_remote_copy(
                    src_ref=blk, dst_ref=blk, send_sem=send_sems.at[4 * p + k], recv_sem=recv_sems.at[4 * p + k],
                    device_id=(x, y, 1 - c), device_id_type=MESH).wait_recv()

    return pl.pallas_call(
        body, name="gather_pair_share", in_specs=[_HBM] * n, out_specs=[_HBM] * n,
        out_shape=[_sds(g.shape, g.dtype) for g in gathered],
        input_output_aliases={p: p for p in range(n)},
        scratch_shapes=[pltpu.SemaphoreType.DMA((4 * n,)), pltpu.SemaphoreType.DMA((4 * n,))],
    )(*gathered)


def _place_slot(arrs, slot, n_slots, name):
    n = len(arrs)

    def body(s_ref, *refs):
        for p in range(n):
            refs[n + p][...] = refs[p][...]

    def spec(a):
        tail = a.shape[-2:]
        return pl.BlockSpec((None,) + tail, lambda i, s_ref: (s_ref[0], 0, 0))

    in_specs = [spec(a) if a.ndim == 3 else pl.BlockSpec(a.shape, lambda i, s_ref: (0, 0)) for a in arrs]
    return pl.pallas_call(
        body, name=name,
        grid_spec=pltpu.PrefetchScalarGridSpec(num_scalar_prefetch=1, grid=(1,), in_specs=in_specs,
                                               out_specs=[spec(a) for a in arrs]),
        out_shape=[_sds((n_slots,) + a.shape[-2:], a.dtype) for a in arrs],
        compiler_params=_cp(("arbitrary",)),
    )(slot, *arrs)


def _cols_from_blocks(w8, name):
    _, R, C = w8.shape

    def body(w_ref, o_ref):
        for e in range(8):
            o_ref[:, C * e:C * (e + 1)] = w_ref[e]

    return pl.pallas_call(body, name=name, out_shape=_sds((R, 8 * C), w8.dtype), compiler_params=_cp())(w8)


def _all_gather(srcs, out_shapes, layout, name):
    n = len(srcs)

    def body(*refs):
        src, out = refs[:n], refs[n:2 * n]
        send_sems, recv_sems, local_sems = refs[2 * n:]
        x, y, c = _my_place()
        me, sibling = (x, y, c), (x, y, 1 - c)
        chips = [(1 - x, y), (x, 1 - y), (1 - x, 1 - y)]

        def region(p, dev):
            px, py, pc = dev
            return _block_of(out[p], layout[p][0], 4 * px + 2 * py + pc, layout[p][1])

        def copy(p, k, block, to, src_ref=None):
            return pltpu.make_async_remote_copy(
                src_ref=region(p, block) if src_ref is None else src_ref, dst_ref=region(p, block),
                send_sem=send_sems.at[p * 7 + k], recv_sem=recv_sems.at[p * 7 + k],
                device_id=to, device_id_type=MESH)

        mine = [pltpu.make_async_copy(src[p], region(p, me), local_sems.at[p]) for p in range(n)]
        for cp in mine:
            cp.start()
        first = []
        for p in range(n):
            first.append(copy(p, 0, me, sibling, src[p]))
            first += [copy(p, 1 + j, me, (*chip, c), src[p]) for j, chip in enumerate(chips)]
        for cp in first:
            cp.start()
        passed = []
        for j, chip in enumerate(chips):
            for p in range(n):
                copy(p, 1 + j, (*chip, c), me).wait_recv()
                fwd = copy(p, 4 + j, (*chip, c), sibling)
                fwd.start()
                passed.append(fwd)
        for p in range(n):
            copy(p, 0, sibling, me).wait_recv()
            for j, chip in enumerate(chips):
                copy(p, 4 + j, (*chip, 1 - c), me).wait_recv()
        for cp in first + passed:
            cp.wait_send()
        for cp in mine:
            cp.wait()

    return pl.pallas_call(
        body, name=name, in_specs=[_HBM] * n, out_specs=[_HBM] * n,
        out_shape=[_sds(s, a.dtype) for s, a in zip(out_shapes, srcs)],
        scratch_shapes=[pltpu.SemaphoreType.DMA((7 * n,)), pltpu.SemaphoreType.DMA((7 * n,)),
                        pltpu.SemaphoreType.DMA((n,))],
    )(*srcs)


def _reduce_pair_exchange(dws, layout, smalls):
    n = len(dws)

    def body(*refs):
        src, sm = refs[:n], refs[n]
        out, sm_out = refs[n + 1:2 * n + 1], refs[2 * n + 1]
        send_sems, recv_sems = refs[2 * n + 2:]
        x, y, c = _my_place()
        sibling = (x, y, 1 - c)
        cps = []
        for p in range(n):
            for q in range(4):
                cps.append(pltpu.make_async_remote_copy(
                    src_ref=_block_of(src[p], layout[p][0], 2 * q + (1 - c), layout[p][1]), dst_ref=out[p].at[q],
                    send_sem=send_sems.at[4 * p + q], recv_sem=recv_sems.at[4 * p + q],
                    device_id=sibling, device_id_type=MESH))
        cps.append(pltpu.make_async_remote_copy(src_ref=sm, dst_ref=sm_out, send_sem=send_sems.at[4 * n],
                                                recv_sem=recv_sems.at[4 * n], device_id=sibling, device_id_type=MESH))
        for cp in cps:
            cp.start()
        for cp in cps:
            cp.wait()

    def shard_shape(a, lay):
        axis, w = lay
        return (4, w, a.shape[1]) if axis == 0 else (4, a.shape[0], w)

    return pl.pallas_call(
        body, name="reduce_pair_exchange", in_specs=[_HBM] * (n + 1), out_specs=[_HBM] * (n + 1),
        out_shape=[_sds(shard_shape(a, l), F32) for a, l in zip(dws, layout)] + [_sds(smalls.shape, F32)],
        scratch_shapes=[pltpu.SemaphoreType.DMA((4 * n + 1,)), pltpu.SemaphoreType.DMA((4 * n + 1,))],
    )(*dws, smalls)


def _pair_add(dw, recv, axis, width, cidx, name):
    if axis == 0:
        tr = min(width, 256)
        C = dw.shape[1]
        grid = (4, width // tr)
        nb = width // tr
        own = pl.BlockSpec((tr, C), lambda q, i, c_ref: ((2 * q + c_ref[0]) * nb + i, 0))
        oth = pl.BlockSpec((None, tr, C), lambda q, i, c_ref: (q, i, 0))
    else:
        R = dw.shape[0]
        tr = min(R, 256)
        grid = (4, R // tr)
        own = pl.BlockSpec((tr, width), lambda q, i, c_ref: (i, 2 * q + c_ref[0]))
        oth = pl.BlockSpec((None, tr, width), lambda q, i, c_ref: (q, i, 0))

    def body(c_ref, a_ref, b_ref, o_ref):
        o_ref[...] = (a_ref[...] + b_ref[...]).astype(o_ref.dtype)

    return pl.pallas_call(
        body, name=name,
        grid_spec=pltpu.PrefetchScalarGridSpec(num_scalar_prefetch=1, grid=grid, in_specs=[own, oth], out_specs=oth),
        out_shape=_sds(recv.shape, CDT), compiler_params=_cp(("parallel", "parallel")),
    )(cidx, dw, recv)


def _add2(a, b, name):
    def body(a_ref, b_ref, o_ref):
        o_ref[...] = a_ref[...] + b_ref[...]

    return pl.pallas_call(body, name=name, out_shape=_sds(a.shape, F32))(a, b)


def _reduce_chip_exchange(parts):
    n = len(parts)

    def body(*refs):
        src, out = refs[:n], refs[n:2 * n]
        send_sems, recv_sems, local_sems = refs[2 * n:]
        x, y, c = _my_place()
        my_slot = 2 * x + y
        chips = [(1 - x, y), (x, 1 - y), (1 - x, 1 - y)]
        cps = []
        for p in range(n):
            keep = pltpu.make_async_copy(src[p].at[my_slot], out[p].at[my_slot], local_sems.at[p])
            keep.start()
            cps.append(keep)
            for k, (cx, cy) in enumerate(chips):
                cps.append(pltpu.make_async_remote_copy(
                    src_ref=src[p].at[2 * cx + cy], dst_ref=out[p].at[my_slot],
                    send_sem=send_sems.at[3 * p + k], recv_sem=recv_sems.at[3 * p + k],
                    device_id=(cx, cy, c), device_id_type=MESH))
                cps[-1].start()
        for cp in cps:
            cp.wait()

    return pl.pallas_call(
        body, name="reduce_chip_exchange", in_specs=[_HBM] * n, out_specs=[_HBM] * n,
        out_shape=[_sds(a.shape, a.dtype) for a in parts],
        scratch_shapes=[pltpu.SemaphoreType.DMA((3 * n,)), pltpu.SemaphoreType.DMA((3 * n,)),
                        pltpu.SemaphoreType.DMA((n,))],
    )(*parts)


def _adaln_part(c_all, w_ada, b_part, dep=None):
    dep_specs, dep_args = _after(dep)
    vmem = pl.BlockSpec(memory_space=pltpu.VMEM)

    def body(c_ref, w_ref, b_ref, *rest):
        rest[-1][...] = _dot(c_ref[...].astype(CDT), w_ref[...].astype(CDT), NN) + b_ref[...]

    return pl.pallas_call(body, name="adaln_part", in_specs=[vmem, vmem, vmem] + dep_specs, out_specs=vmem,
                          out_shape=_sds((8, w_ada.shape[1]), F32),
                          compiler_params=_cp())(c_all, w_ada, b_part, *dep_args)


def _adam_math(w, g, m, v):
    m = ADAM_B1 * m + (1.0 - ADAM_B1) * g
    v = ADAM_B2 * v + (1.0 - ADAM_B2) * (g * g)
    m_hat = m / (1.0 - ADAM_B1 ** ADAM_STEP)
    v_hat = v / (1.0 - ADAM_B2 ** ADAM_STEP)
    delta = -ADAM_LR * (m_hat / (jnp.sqrt(v_hat) + ADAM_EPS) + ADAM_WD * w)
    return delta, m, v


def _adam(gparts, w, m, v, name):
    P, R, C = gparts.shape
    tr = R if R <= 256 else 256

    def body(g_ref, w_ref, m_ref, v_ref, go_ref, d_ref, mo_ref, vo_ref):
        g = g_ref[0].astype(F32)
        for i in range(1, P):
            g = g + g_ref[i].astype(F32)
        d, mn, vn = _adam_math(w_ref[...], g, m_ref[...], v_ref[...])
        go_ref[...] = g
        d_ref[...] = d
        mo_ref[...] = mn
        vo_ref[...] = vn

    t = pl.BlockSpec((tr, C), lambda i: (i, 0))
    return pl.pallas_call(
        body, name=name, grid=(R // tr,),
        in_specs=[pl.BlockSpec((P, tr, C), lambda i: (0, i, 0)), t, t, t], out_specs=[t, t, t, t],
        out_shape=[_sds((R, C), F32)] * 4, compiler_params=_cp(("parallel",)),
    )(gparts, w, m, v)


def _sum_parts(gparts, name):
    P, R, C = gparts.shape

    def body(g_ref, o_ref):
        g = g_ref[0]
        for i in range(1, P):
            g = g + g_ref[i]
        o_ref[...] = g

    return pl.pallas_call(body, name=name, out_shape=_sds((R, C), F32))(gparts)


def _adam_w_ada(c_t, dm, w, m, v):
    R, C = w.shape

    def body(c_ref, dm_ref, w_ref, m_ref, v_ref, go_ref, d_ref, mo_ref, vo_ref):
        g = c_ref[:, 0:1] * dm_ref[0:1, :]
        for b in range(1, 8):
            g = g + c_ref[:, b:b + 1] * dm_ref[b:b + 1, :]
        d, mn, vn = _adam_math(w_ref[...], g, m_ref[...], v_ref[...])
        go_ref[...] = g
        d_ref[...] = d
        mo_ref[...] = mn
        vo_ref[...] = vn

    return pl.pallas_call(body, name="adam_w_ada", out_shape=[_sds((R, C), F32)] * 4,
                          compiler_params=_cp())(c_t, dm, w, m, v)


_SM = dict(norm_g=(0, 8), pool_w=(8, 512), pool_scale=(520, 8), rel_bias=(528, 8), final_g=(536, 8),
           dmod=(544, 192), loss=(736, 8))
_SM_ROWS = 744


def _pack_rows(a, rows):
    flat = a.reshape(-1).astype(F32)
    return jnp.pad(flat, (0, rows * 128 - flat.shape[0])).reshape(rows, 128)


def kernel(x, c, norm_g, w_ada, b_ada, w_in, pool_w, pool_scale, w_attn_br, w_pool_br, w_out, rel_bias, final_g, loss_target, m_norm_g, m_w_ada, m_b_ada, m_w_in, m_pool_w, m_pool_scale, m_w_attn_br, m_w_pool_br, m_w_out, m_rel_bias, m_final_g, v_norm_g, v_w_ada, v_b_ada, v_w_in, v_pool_w, v_pool_scale, v_w_attn_br, v_w_pool_br, v_w_out, v_rel_bias, v_final_g):
    S = x.shape[1]
    px, py, pc = _my_place()
    me = 4 * px + 2 * py + pc
    cidx = jnp.reshape(pc, (1,)).astype(jnp.int32)

    layout = [(1, 1024), (1, 128), (1, 128), (0, 128)]
    final_g2 = final_g.reshape(1, D)
    me1 = jnp.reshape(me, (1,)).astype(jnp.int32)

    (c_rows,) = _all_gather([jnp.broadcast_to(c, (8, D))], [(64, D)], [(0, 8)], "gather_c")
    c_all = c_rows[::8]

    b_part = lax.dynamic_slice(b_ada, (0, me * 384), (1, 384))
    mod_part = _adaln_part(c_all, w_ada[0], b_part)
    (mod_all,) = _all_gather([mod_part], [(8, 8, 384)], [(None, 1)], "gather_mod")
    mod = lax.dynamic_index_in_dim(mod_all, me, axis=1, keepdims=False).reshape(1, 3 * D)
    shift, scale, gate = mod[:, :D], mod[:, D:2 * D], mod[:, 2 * D:]

    shards = [w_in[0].astype(CDT), w_attn_br[0].astype(CDT), w_pool_br[0].astype(CDT), w_out[0].astype(CDT)]
    gplan = _gather_chips_plan([(None, 1)] * 4)
    g_sems, shards, lands, tok = _split_start("gather_chips_start", shards,
                                              _place_slot(shards, me1, 8, "place_own_shards"), 12, gplan, dep=mod_all)
    st = _local_pre(x[0], shift, scale, norm_g, rel_bias, dep=tok)
    _, lands = _split_wait("gather_chips_wait", g_sems, shards, lands, gplan,
                           [st["ht"]] + [b for pair in st["biases"] for b in pair])
    w_in_full, wab8, wpb8, wout8 = _pair_share(lands)
    wab, wpb = _cols_from_blocks(wab8, "w_attn_br_cols"), _cols_from_blocks(wpb8, "w_pool_br_cols")
    wout = wout8.reshape(D, D)

    st = _local_grads(st, loss_target[0], gate, w_in_full, pool_w[0], pool_scale, wab, wpb, wout, final_g2)

    early = jnp.concatenate([_pack_rows(st["dpw"], 512), _pack_rows(st["dps"], 8), _pack_rows(st["gfg"], 8),
                             _pack_rows(st["loss"], 8)], axis=0)
    dws = [st["dw_in"], st["dw_ab"], st["dw_pb"], st["dw_out"]]
    pplan = _pair_plan(layout)
    pair_shapes = [(4, w, a.shape[1]) if ax == 0 else (4, a.shape[0], w) for a, (ax, w) in zip(dws, layout)]
    p_sems, srcs, recv, tok = _split_start("reduce_pair_start", dws + [early],
                                           [lax.empty(s, F32) for s in pair_shapes + [early.shape]], 17, pplan)
    d_rel_bias = _local_bias_grads(st, dep=tok)
    srcs, recv = _split_wait("reduce_pair_wait", p_sems, srcs, recv, pplan, d_rel_bias)
    parts = [_pair_add(dw, r, ax, w, cidx, f"pair_add{i}")
             for i, (dw, r, (ax, w)) in enumerate(zip(srcs[:4], recv[:4], layout))]
    parts.append(_add2(srcs[4], recv[4], "pair_add_smalls"))

    slot1 = jnp.reshape(2 * px + py, (1,)).astype(jnp.int32)
    c_sems, parts, lands, tok = _split_start("reduce_chips_start", parts,
                                             _place_slot(parts, slot1, 4, "place_own_partials"), 15, _chips_plan)
    grad_x, dmod, d_norm_g = _local_input_grads(st, w_in_full, norm_g, scale, dep=tok)

    late = jnp.concatenate([_pack_rows(dmod, 24), _pack_rows(d_norm_g, 8), _pack_rows(d_rel_bias, 8)], axis=0)
    (late_all,) = _all_gather([late], [(8, 40, 128)], [(None, 1)], "gather_late")

    def small_adam(gparts, w, m, v, nvalid, name):
        rows = gparts.shape[1]
        outs = _adam(gparts, _pack_rows(w, rows), _pack_rows(m, rows), _pack_rows(v, rows), f"adam_{name}")
        return [o.reshape(-1)[:nvalid].reshape(w.shape) for o in outs]

    o_norm_g = small_adam(late_all[:, 24:32], norm_g, m_norm_g, v_norm_g, D, "norm_g")
    o_rel_bias = small_adam(late_all[:, 32:40], rel_bias, m_rel_bias, v_rel_bias, 768, "rel_bias")
    dmod_all = late_all[:, 0:24].reshape(8, 3 * D)
    o_b_ada = [o.reshape(1, 3 * D) for o in _adam(dmod_all.reshape(8, 24, 128), b_ada.reshape(24, 128),
                                                   m_b_ada.reshape(24, 128), v_b_ada.reshape(24, 128), "adam_b_ada")]
    dm_mine = lax.dynamic_slice(dmod_all, (0, me * 384), (8, 384))
    o_w_ada = [o[None] for o in _adam_w_ada(c_all.T, dm_mine, w_ada[0], m_w_ada[0], v_w_ada[0])]

    _, (g_in, g_ab, g_pb, g_out, sm4) = _split_wait("reduce_chips_wait", c_sems, parts, lands, _chips_plan,
                                                    [o_w_ada[0], o_b_ada[0], o_norm_g[0], o_rel_bias[0]])
    o_w_in = [o[None] for o in _adam(g_in, w_in[0], m_w_in[0], v_w_in[0], "adam_w_in")]
    o_w_ab = [o[None] for o in _adam(g_ab, w_attn_br[0], m_w_attn_br[0], v_w_attn_br[0], "adam_w_attn_br")]
    o_w_pb = [o[None] for o in _adam(g_pb, w_pool_br[0], m_w_pool_br[0], v_w_pool_br[0], "adam_w_pool_br")]
    o_w_out = [o[None] for o in _adam(g_out, w_out[0], m_w_out[0], v_w_out[0], "adam_w_out")]
    o_pool_w = small_adam(sm4[:, 0:512], pool_w, m_pool_w, v_pool_w, 65536, "pool_w")
    o_pool_scale = small_adam(sm4[:, 512:520], pool_scale, m_pool_scale, v_pool_scale, AW, "pool_scale")
    o_final_g = small_adam(sm4[:, 520:528], final_g, m_final_g, v_final_g, D, "final_g")
    loss = _sum_parts(sm4[:, 528:536], "sum_loss")[0, 0]

    per_w = [o_norm_g, o_w_ada, o_b_ada, o_w_in, o_pool_w, o_pool_scale, o_w_ab, o_w_pb, o_w_out, o_rel_bias, o_final_g]
    outs = [loss, grad_x[None]]
    for k in range(4):
        outs += [o[k] for o in per_w]
    return tuple(outs)
```

```python
import functools
import math

import numpy as np
import jax
import jax.numpy as jnp
from jax import lax
from jax.experimental import pallas as pl
from jax.experimental.pallas import tpu as pltpu

F32 = jnp.float32
CDT = jnp.bfloat16
D = 1024
HD = 64
NH = 8
AW = 512
QKV_W = 4608
REST_W = 3584
R16 = 16
EPS = 1e-6
NEG = -1e30
N_BACK = 128
ADAM_LR, ADAM_B1, ADAM_B2, ADAM_EPS, ADAM_WD, ADAM_STEP = 0.001, 0.9, 0.999, 1e-08, 0.01, 10
MESH = pl.DeviceIdType.MESH
VMEM_MB = 1024 * 1024


def _cp(sem=None, vmem=48, **kw):
    if sem is not None:
        kw["dimension_semantics"] = sem
    return pltpu.CompilerParams(vmem_limit_bytes=vmem * VMEM_MB, **kw)


def _sds(shape, dtype):
    return jax.ShapeDtypeStruct(tuple(shape), dtype)


def _rest_blk(jj):
    return jnp.where(jj == 0, 9, jnp.where(jj == 6, 10, jj + 10))


def _sigmoid(z):
    return 1.0 / (1.0 + jnp.exp(-z))


def _dot(a, b, dims):
    return lax.dot_general(a, b, (dims, ((), ())), preferred_element_type=F32)


NN = ((1,), (0,))
NT = ((1,), (1,))
TN = ((0,), (0,))


def _ld(ref):
    if len(ref.shape) == 2:
        return ref[...]
    return jnp.concatenate([ref[a] for a in range(ref.shape[0])], axis=0)


def _st(ref, val, cols=None):
    val = val.astype(ref.dtype)
    if len(ref.shape) == 2:
        if cols is None:
            ref[...] = val
        else:
            ref[:, cols] = val
        return
    rows = ref.shape[1]
    for a in range(ref.shape[0]):
        if cols is None:
            ref[a] = val[a * rows:(a + 1) * rows]
        else:
            ref[a, :, cols] = val[a * rows:(a + 1) * rows]


REST_COLS = ((9, 10), (11, 16), (10, 11))
_HBM = pl.BlockSpec(memory_space=pltpu.HBM)


def _resident(shape):
    return pl.BlockSpec(shape, lambda *ids: (0,) * len(shape), pipeline_mode=pl.Buffered(1))


def _w512(w_ref, j):
    return w_ref[j // 2, :, 512 * (j % 2):512 * (j % 2 + 1)]


def _proj(h, w):
    S = h.shape[0]
    tm = 256

    def body(a_ref, w_ref, q_ref, r_ref):
        a = a_ref[...]
        for j in range(9):
            q_ref[:, 512 * j:512 * (j + 1)] = _dot(a, _w512(w_ref, j), NN).astype(q_ref.dtype)
        off = 0
        for lo, hi in REST_COLS:
            for j in range(lo, hi):
                r_ref[:, off:off + 512] = _dot(a, _w512(w_ref, j), NN)
                off += 512

    return pl.pallas_call(
        body, name="proj", grid=(S // tm,),
        in_specs=[pl.BlockSpec((tm, D), lambda i: (i, 0)), _resident((8, D, 1024))],
        out_specs=[pl.BlockSpec((tm, QKV_W), lambda i: (i, 0)), pl.BlockSpec((tm, REST_W), lambda i: (i, 0))],
        out_shape=[_sds((S, QKV_W), CDT), _sds((S, REST_W), F32)],
        compiler_params=_cp(("parallel",), vmem=56),
    )(h, w)


def _after(dep):
    return ([], []) if dep is None else ([pl.BlockSpec(memory_space=pl.ANY)], [dep])


def _dh(dqkv, drest, w, dep=None):
    S = dqkv.shape[0]
    tm = 256
    dep_specs, dep_args = _after(dep)

    def body(a_ref, b_ref, w_ref, *rest):
        o_ref = rest[-1]
        acc = _dot(a_ref[:, 0:512], _w512(w_ref, 0), NT)
        for j in range(1, 9):
            acc = acc + _dot(a_ref[:, 512 * j:512 * (j + 1)], _w512(w_ref, j), NT)
        off = 0
        for lo, hi in REST_COLS:
            for j in range(lo, hi):
                acc = acc + _dot(b_ref[:, off:off + 512], _w512(w_ref, j), NT)
                off += 512
        o_ref[...] = acc

    return pl.pallas_call(
        body, name="dh", grid=(S // tm,),
        in_specs=[pl.BlockSpec((tm, QKV_W), lambda i: (i, 0)), pl.BlockSpec((tm, REST_W), lambda i: (i, 0)),
                  _resident((8, D, 1024))] + dep_specs,
        out_specs=pl.BlockSpec((tm, D), lambda i: (i, 0)),
        out_shape=_sds((S, D), F32),
        compiler_params=_cp(("parallel",), vmem=56),
    )(dqkv, drest, w, *dep_args)


def _mm_tn(at, b, *, n_total, out_blk, name, prev=None, tn=512):
    R, S = at.shape
    nb = b.shape[1] // tn

    def body(*refs):
        refs[-1][...] = _dot(refs[0][...], refs[1][...], NN)

    in_specs = [_resident((R, S)), pl.BlockSpec((S, tn), lambda j: (0, j))]
    args = [at, b]
    aliases = {}
    if prev is not None:
        in_specs.append(pl.BlockSpec(memory_space=pl.ANY))
        args.append(prev)
        aliases = {2: 0}
    return pl.pallas_call(
        body, name=name, grid=(nb,), in_specs=in_specs,
        out_specs=pl.BlockSpec((R, tn), lambda j: (0, out_blk(j))),
        out_shape=_sds((R, n_total), F32),
        input_output_aliases=aliases,
        compiler_params=_cp(("parallel",)),
    )(*args)


def _tcast(a, name):
    M, C = a.shape
    tm = 512

    def body(a_ref, o_ref):
        o_ref[...] = a_ref[...].astype(F32).T.astype(o_ref.dtype)

    return pl.pallas_call(
        body, name=name, grid=(M // tm,),
        in_specs=[pl.BlockSpec((tm, C), lambda i: (i, 0))],
        out_specs=pl.BlockSpec((C, tm), lambda i: (0, i)),
        out_shape=_sds((C, M), CDT), compiler_params=_cp(("parallel",)),
    )(a)


TOK = R16 * R16


def _swap16():
    i = lax.broadcasted_iota(jnp.int32, (TOK, TOK), 0)
    j = lax.broadcasted_iota(jnp.int32, (TOK, TOK), 1)
    return jnp.where(j == jnp.bitwise_and(i, 15) * 16 + jnp.right_shift(i, 4), 1.0, 0.0).astype(CDT)


def _norm_mod(x, norm_g, shift, scale, dep=None):
    S = x.shape[0]
    dep_specs, dep_args = _after(dep)

    def body(x_ref, g_ref, sh_ref, sc_ref, *rest):
        xv = x_ref[...]
        xn = xv * lax.rsqrt(jnp.mean(xv * xv, axis=-1, keepdims=True) + EPS)
        h = (xn * (g_ref[...] * (1.0 + sc_ref[...])) + sh_ref[...]).astype(CDT)
        _st(rest[-1], _dot(_swap16(), h, NN))

    vec = pl.BlockSpec((1, D), lambda i: (0, 0))
    return pl.pallas_call(
        body, name="norm_mod", grid=(S // TOK,),
        in_specs=[pl.BlockSpec((TOK, D), lambda i: (i, 0)), vec, vec, vec] + dep_specs,
        out_specs=pl.BlockSpec((R16, R16, D), lambda i: (0, i, 0)),
        out_shape=_sds((R16, S // R16, D), CDT), compiler_params=_cp(("parallel",)),
    )(x, norm_g, shift, scale, *dep_args)


def _t5_bucket_np(n):
    nf = np.maximum(n, 1).astype(np.float32)
    large = 16 + (np.log(nf / np.float32(16)) / np.float32(math.log(2048 / 16)) * np.float32(16)).astype(np.int32)
    large = np.minimum(large, 31)
    return np.where(n < 16, n, large).astype(np.int32)


class _Geom:
    def __init__(self, gi, LS):
        self.gi, self.LS = gi, LS
        self.dil = (1, 4, 16)[gi]
        if gi == 0:
            self.lead, self.rows, self.n_tiles, self.has_prev = (R16,), 16, LS // 16, True
            a = np.arange(R16)[:, None]
            ll = np.arange(16)[None, :]
            self.pos = (16 * ll + a).reshape(-1)
        elif gi == 1:
            self.lead, self.rows, self.n_tiles, self.has_prev = (4,), 64, LS // 64, True
            a = np.arange(4)[:, None]
            ll = np.arange(64)[None, :]
            self.pos = (4 * ll + a).reshape(-1)
        else:
            self.lead, self.rows, self.n_tiles, self.has_prev = (), LS, 1, False
            self.pos = np.arange(LS)
        self.T = self.pos.shape[0]
        self.Tk = 2 * self.T if self.has_prev else self.T
        self.outer = {0: (), 1: (4,), 2: (R16,)}[gi]

    def view(self, arr):
        S, C = arr.shape
        if self.gi == 1:
            return arr.reshape(4, 4, self.LS, C)
        return arr.reshape(R16, self.LS, C)

    def view_shape(self, C):
        return (4, 4, self.LS, C) if self.gi == 1 else (R16, self.LS, C)

    def unview(self, arr):
        return arr.reshape(R16 * self.LS, arr.shape[-1])

    def spec(self, W, cblk, tile):
        if self.gi == 0:
            return pl.BlockSpec((R16, 16, W), lambda n: (0, tile(n), cblk))
        if self.gi == 1:
            return pl.BlockSpec((4, None, 64, W), lambda r, n: (0, r, tile(n), cblk))
        return pl.BlockSpec((None, self.LS, W), lambda r: (r, 0, cblk))

    def halves(self):
        R = self.rows
        idx = np.arange(self.T).reshape(-1, R)
        return idx[:, :R // 2].reshape(-1), idx[:, R // 2:].reshape(-1)

    def bucket_index(self, sub):
        lo, hi = self.halves()
        pos = self.pos
        if sub == 0:
            pq = pos[lo]
            pk = np.concatenate([pos[hi] - self.T, pos[lo]]) if self.has_prev else np.concatenate([pos[lo], pos[hi]])
        else:
            pq = pos[hi]
            pk = np.concatenate([pos[lo], pos[hi]])
        dist = pq[:, None] - pk[None, :]
        ok = (dist >= 0) & (dist <= N_BACK)
        bucket = _t5_bucket_np(np.clip(dist, 0, N_BACK) * self.dil)
        return np.where(ok, bucket, -1).astype(np.int32)


def _bias_table(idx, buckets, rel_bias, gi, sub, dep=None):
    T, Tk = idx.shape
    tr = 32
    dep_specs, dep_args = _after(dep)

    def body(idx_ref, rb_ref, *rest):
        h = pl.program_id(0)
        ix = idx_ref[...]
        acc = jnp.full(ix.shape, NEG, F32)
        for b in buckets:
            acc = jnp.where(ix == b, rb_ref[b, gi * NH + h], acc)
        rest[-1][...] = acc

    return pl.pallas_call(
        body, name=f"bias_table{gi}{'ab'[sub]}", grid=(NH, T // tr),
        in_specs=[pl.BlockSpec((tr, Tk), lambda h, i: (i, 0)), pl.BlockSpec(memory_space=pltpu.SMEM)] + dep_specs,
        out_specs=pl.BlockSpec((None, tr, Tk), lambda h, i: (h, i, 0)),
        out_shape=_sds((NH, T, Tk), F32), compiler_params=_cp(("parallel", "parallel")),
    )(idx, rel_bias, *dep_args)


def _bias_grad(dsum, idx, buckets, gi, sub, dep=None):
    _, T, Tk = dsum.shape
    dep_specs, dep_args = _after(dep)

    def body(ds_ref, idx_ref, *rest):
        o_ref, r_ref = rest[-2], rest[-1]
        r_ref[...] = jnp.zeros_like(r_ref)
        ix, ds = idx_ref[...], ds_ref[...]
        for b in buckets:
            r_ref[b:b + 1, :] = jnp.sum(jnp.where(ix == b, ds, 0.0), axis=0, keepdims=True)
        o_ref[...] = jnp.broadcast_to(jnp.sum(r_ref[...], axis=1, keepdims=True), (32, 128))

    out = pl.pallas_call(
        body, name=f"bias_grad{gi}{'ab'[sub]}", grid=(NH,),
        in_specs=[pl.BlockSpec((None, T, Tk), lambda h: (h, 0, 0)), pl.BlockSpec((T, Tk), lambda h: (0, 0))] + dep_specs,
        out_specs=pl.BlockSpec((None, 32, 128), lambda h: (h, 0, 0)),
        out_shape=_sds((NH, 32, 128), F32), scratch_shapes=[pltpu.VMEM((32, Tk), F32)],
        compiler_params=_cp(("parallel",)),
    )(dsum, idx, *dep_args)
    return out[:, :, 0].T


def _prev_mask(n, T, Tk):
    col = lax.broadcasted_iota(jnp.int32, (1, Tk), 1)
    return jnp.where((col < T) & (n == 0), NEG, 0.0).astype(F32)


def _split_rows(v, R):
    A, half = v.shape[0] // R, R // 2
    via_f32 = v.dtype != F32 and half % 16 != 0
    w = v.astype(F32) if via_f32 else v
    lo = jnp.concatenate([w[a * R:a * R + half] for a in range(A)], axis=0)
    hi = jnp.concatenate([w[a * R + half:(a + 1) * R] for a in range(A)], axis=0)
    return (lo.astype(v.dtype), hi.astype(v.dtype)) if via_f32 else (lo, hi)


def _merge_rows(lo, hi, R):
    half = R // 2
    A = lo.shape[0] // half
    return jnp.concatenate([x[a * half:(a + 1) * half] for a in range(A) for x in (lo, hi)], axis=0)


def _attn_fwd(qkv, biases, geom):
    S = qkv.shape[0]
    gi, T, R, has_prev = geom.gi, geom.T, geom.rows, geom.has_prev
    H2 = T // 2
    qv = geom.view(qkv)
    cq, ck, cv = 3 * gi, 3 * gi + 1, 3 * gi + 2

    def body(*refs):
        if has_prev:
            q_ref, kp_ref, kc_ref, vp_ref, vc_ref, ba_ref, bb_ref, o_ref, l_ref = refs
        else:
            q_ref, kc_ref, vc_ref, ba_ref, bb_ref, o_ref, l_ref = refs
        q_lo, q_hi = _split_rows(_ld(q_ref), R)
        kc_lo, kc_hi = _split_rows(_ld(kc_ref), R)
        vc_lo, vc_hi = _split_rows(_ld(vc_ref), R)
        k_b = jnp.concatenate([kc_lo, kc_hi], axis=0)
        v_b = jnp.concatenate([vc_lo, vc_hi], axis=0)
        if has_prev:
            n = pl.program_id(len(geom.outer))
            k_a = jnp.concatenate([_split_rows(_ld(kp_ref), R)[1], kc_lo], axis=0)
            v_a = jnp.concatenate([_split_rows(_ld(vp_ref), R)[1], vc_lo], axis=0)
            pm = _prev_mask(n, H2, T)
        else:
            k_a, v_a, pm = k_b, v_b, None
        low = lax.broadcasted_iota(jnp.int32, (1, 2 * HD), 1) < HD
        for j in range(NH // 2):
            cols = slice(2 * HD * j, 2 * HD * (j + 1))
            o_half, l_half = [], []
            for q, k, v, b_ref, mask in ((q_lo, k_a, v_a, ba_ref, pm), (q_hi, k_b, v_b, bb_ref, None)):
                q2 = q[:, cols].astype(F32) * 0.125
                k2, v2 = k[:, cols], v[:, cols]
                qs = jnp.concatenate([jnp.where(low, q2, 0.0), jnp.where(low, 0.0, q2)], axis=0).astype(q.dtype)
                s = _dot(qs, k2, NT) + b_ref[j]
                if mask is not None:
                    s = s + mask
                m = jnp.max(s, axis=1, keepdims=True)
                p = jnp.exp(s - m)
                l = jnp.sum(p, axis=1, keepdims=True)
                pv = _dot(p.astype(v.dtype), v2, NN) / l
                lse = m + jnp.log(l)
                o_half.append(jnp.where(low, pv[:H2], pv[H2:]))
                l_half.append(jnp.where(low, lse[:H2], lse[H2:]))
            _st(o_ref, _merge_rows(o_half[0], o_half[1], R), cols)
            _st(l_ref, _merge_rows(l_half[0], l_half[1], R), cols)

    cur = lambda n: n
    prev = lambda n: jnp.maximum(n - 1, 0)
    if has_prev:
        in_specs = [geom.spec(AW, cq, cur), geom.spec(AW, ck, prev), geom.spec(AW, ck, cur),
                    geom.spec(AW, cv, prev), geom.spec(AW, cv, cur)]
        args = [qv] * 5
    else:
        in_specs = [geom.spec(AW, cq, cur), geom.spec(AW, ck, cur), geom.spec(AW, cv, cur)]
        args = [qv] * 3
    nd = len(geom.outer) + (1 if has_prev else 0)
    biases = [b.reshape(NH // 2, 2 * H2, b.shape[2]) for b in biases]
    in_specs += [pl.BlockSpec(b.shape, lambda *ids: (0, 0, 0)) for b in biases]
    grid = geom.outer + ((geom.n_tiles,) if has_prev else ())
    oshape = geom.view_shape(AW)
    o, l = pl.pallas_call(
        body, name=f"attn_fwd{gi}", grid=grid, in_specs=in_specs,
        out_specs=[geom.spec(AW, 0, cur), geom.spec(AW, 0, cur)],
        out_shape=[_sds(oshape, F32), _sds(oshape, F32)],
        compiler_params=_cp(("arbitrary",) * nd),
    )(*args, *biases)
    return geom.unview(o), geom.unview(l)


def _attn_combine(os_, ls_, rest):
    S = rest.shape[0]
    tm = 512

    def body(o0, o1, o2, l0, l1, l2, z_ref, attn_ref, L_ref, ag_ref, agt_ref):
        la, lb, lc = l0[...], l1[...], l2[...]
        m = jnp.maximum(jnp.maximum(la, lb), lc)
        ea, eb, ec = jnp.exp(la - m), jnp.exp(lb - m), jnp.exp(lc - m)
        den = ea + eb + ec
        attn = (ea * o0[...] + eb * o1[...] + ec * o2[...]) / den
        attn_ref[...] = attn
        L_ref[...] = m + jnp.log(den)
        z = z_ref[...]
        ag = attn * (z * _sigmoid(z))
        ag_ref[...] = ag.astype(ag_ref.dtype)
        agt_ref[...] = ag.T.astype(agt_ref.dtype)

    t = pl.BlockSpec((tm, AW), lambda i: (i, 0))
    return pl.pallas_call(
        body, name="attn_combine", grid=(S // tm,),
        in_specs=[t] * 6 + [pl.BlockSpec((tm, AW), lambda i: (i, 0))],
        out_specs=[t, t, t, pl.BlockSpec((AW, tm), lambda i: (0, i))],
        out_shape=[_sds((S, AW), F32), _sds((S, AW), F32), _sds((S, AW), CDT), _sds((AW, S), CDT)],
        compiler_params=_cp(("parallel",)),
    )(*os_, *ls_, rest)


def _pool_counts(r, LS, win):
    l = lax.broadcasted_iota(jnp.int32, (LS, 1), 0)
    return jnp.minimum(16 * l + (r + 1), win).astype(F32)


def _pool_fwd(rest, pool_w, pool_scale):
    S = rest.shape[0]
    LS = S // R16
    r3 = rest.reshape(R16, LS, REST_W)
    PG = 128

    def body(u_ref, z_ref, pw_ref, ps_ref, pt_ref, mp_ref, pg_ref, pgt_ref, d_ref):
        g = pl.program_id(0)
        win = jnp.left_shift(2, g)
        row = lax.broadcasted_iota(jnp.int32, (LS, PG), 0)
        for r in range(1, R16):
            d_ref[r] = jnp.where(row == 0, 0.0, pltpu.roll(u_ref[r], 1, 0))
        pw = pw_ref[...].astype(CDT)
        for r in range(R16):
            u = u_ref[r]
            acc = u
            for j in range(1, R16):
                coef = jnp.where(j < win, 1.0, 0.0).astype(F32)
                src = u_ref[r - j] if r >= j else d_ref[r - j + R16]
                acc = acc + coef * src
            pooled = acc / _pool_counts(r, LS, win) - u
            pc = pooled.astype(CDT)
            mp = _dot(pc, pw, NN)
            z = z_ref[r]
            pg = (mp * ps_ref[...]) * (z * _sigmoid(z))
            mp_ref[r] = mp
            pg_ref[r] = pg.astype(pg_ref.dtype)
            pt_ref[:, r * LS:(r + 1) * LS] = pooled.T.astype(pt_ref.dtype)
            pgt_ref[:, r * LS:(r + 1) * LS] = pg.T.astype(pgt_ref.dtype)

    col = lambda off: pl.BlockSpec((R16, LS, PG), lambda g: (0, 0, off + g))
    tsp = pl.BlockSpec((PG, S), lambda g: (g, 0))
    pt, mp, pg, pgt = pl.pallas_call(
        body, name="pool_fwd", grid=(4,),
        in_specs=[col(24), col(4), pl.BlockSpec((None, PG, PG), lambda g: (g, 0, 0)),
                  pl.BlockSpec((1, PG), lambda g: (0, g))],
        out_specs=[tsp, col(0), col(0), tsp],
        out_shape=[_sds((AW, S), CDT), _sds((R16, LS, AW), F32), _sds((R16, LS, AW), CDT), _sds((AW, S), CDT)],
        scratch_shapes=[pltpu.VMEM((R16, LS, PG), F32)],
        compiler_params=_cp(("parallel",)),
    )(r3, r3, pool_w, pool_scale)
    return pt, mp.reshape(S, AW), pg.reshape(S, AW), pgt


def _branch_merge(ag, pg, wab, wpb, rest):
    S = ag.shape[0]
    tm, tn = 512, 512

    def body(ag_ref, pg_ref, wa_ref, wp_ref, ga_ref, gp_ref, m_ref, mt_ref):
        ya = _dot(ag_ref[...], wa_ref[...], NN)
        yp = _dot(pg_ref[...], wp_ref[...], NN)
        mg = _sigmoid(ga_ref[...]) * ya + _sigmoid(gp_ref[...]) * yp
        m_ref[...] = mg.astype(m_ref.dtype)
        mt_ref[...] = mg.T.astype(mt_ref.dtype)

    a = pl.BlockSpec((tm, AW), lambda i, j: (i, 0))
    w = pl.BlockSpec((AW, tn), lambda i, j: (0, j))
    o = pl.BlockSpec((tm, tn), lambda i, j: (i, j))
    return pl.pallas_call(
        body, name="branch_merge", grid=(S // tm, D // tn),
        in_specs=[a, a, w, w, pl.BlockSpec((tm, tn), lambda i, j: (i, 2 + j)),
                  pl.BlockSpec((tm, tn), lambda i, j: (i, 4 + j))],
        out_specs=[o, pl.BlockSpec((tn, tm), lambda i, j: (j, i))],
        out_shape=[_sds((S, D), CDT), _sds((D, S), CDT)],
        compiler_params=_cp(("parallel", "parallel")),
    )(ag, pg, wab, wpb, rest, rest)


def _out_loss(merged, wout, x, target, gate, final_g):
    S = x.shape[0]

    def body(m_ref, w_ref, x_ref, t_ref, gate_ref, fg_ref, dx2_ref, dmo_ref, loss_ref, gfg_ref, dgate_ref):
        i = pl.program_id(0)

        @pl.when(i == 0)
        def _():
            loss_ref[...] = jnp.zeros_like(loss_ref)
            gfg_ref[...] = jnp.zeros_like(gfg_ref)
            dgate_ref[...] = jnp.zeros_like(dgate_ref)

        swap = _swap16()
        mo = _dot(_dot(swap, _ld(m_ref), NN).astype(CDT), w_ref[...], NN)
        gate, fg = gate_ref[...], fg_ref[...]
        x2 = x_ref[...] + gate * mo
        r2 = lax.rsqrt(jnp.mean(x2 * x2, axis=-1, keepdims=True) + EPS)
        xn2 = x2 * r2
        diff = xn2 * fg - t_ref[...]
        loss = jnp.sum(jnp.sum(diff * diff, axis=-1, keepdims=True), axis=0, keepdims=True)
        dy = diff * (1.0 / D)
        dxn2 = dy * fg
        dx2 = r2 * (dxn2 - xn2 * jnp.mean(dxn2 * xn2, axis=-1, keepdims=True))
        dx2_ref[...] = dx2
        _st(dmo_ref, _dot(swap, (dx2 * gate).astype(CDT), NN))
        loss_ref[...] += jnp.broadcast_to(loss * (0.5 / D), loss_ref.shape)
        gfg_ref[...] += jnp.sum(dy * xn2, axis=0, keepdims=True)
        dgate_ref[...] += jnp.sum(dx2 * mo, axis=0, keepdims=True)

    slab = pl.BlockSpec((R16, R16, D), lambda i: (0, i, 0))
    nat = pl.BlockSpec((TOK, D), lambda i: (i, 0))
    vec = pl.BlockSpec((1, D), lambda i: (0, 0))
    return pl.pallas_call(
        body, name="out_loss", grid=(S // TOK,),
        in_specs=[slab, _resident((D, D)), nat, nat, vec, vec],
        out_specs=[nat, slab, pl.BlockSpec((1, 128), lambda i: (0, 0)), vec, vec],
        out_shape=[_sds((S, D), F32), _sds((R16, S // R16, D), CDT), _sds((1, 128), F32), _sds((1, D), F32),
                   _sds((1, D), F32)],
        compiler_params=_cp(("arbitrary",)),
    )(merged, wout, x, target, gate, final_g)


def _bwd_gates(dmo, wout, wab, wpb, rest, ag, pg, attn, mp, pool_scale):
    S = dmo.shape[0]
    tm = 256

    def body(dmo_ref, wo_ref, wa_ref, wp_ref, za_ref, zp_ref, ga0, ga1, gp0, gp1, ag_ref, pg_ref, at_ref, mp_ref,
             ps_ref, dya_ref, dyp_ref, dr_ref, dat_ref, dl_ref, dmp_ref, dps_ref):
        i = pl.program_id(0)

        @pl.when(i == 0)
        def _():
            dps_ref[...] = jnp.zeros_like(dps_ref)

        dm = _dot(dmo_ref[...], wo_ref[...], NT)
        sa = _sigmoid(jnp.concatenate([ga0[...], ga1[...]], axis=1))
        sp = _sigmoid(jnp.concatenate([gp0[...], gp1[...]], axis=1))
        dya = (dm * sa).astype(CDT)
        dyp = (dm * sp).astype(CDT)
        dya_ref[...] = dya
        dyp_ref[...] = dyp
        ya = _dot(ag_ref[...], wa_ref[...], NN)
        yp = _dot(pg_ref[...], wp_ref[...], NN)
        dr_ref[:, 1024:2048] = (dm * ya * sa * (1.0 - sa)).astype(dr_ref.dtype)
        dr_ref[:, 2048:3072] = (dm * yp * sp * (1.0 - sp)).astype(dr_ref.dtype)
        da = _dot(dya, wa_ref[...], NT)
        dp = _dot(dyp, wp_ref[...], NT)
        za = za_ref[...]
        sga = _sigmoid(za)
        attn = at_ref[...]
        dattn = da * (za * sga)
        dr_ref[:, 0:512] = (da * attn * (sga * (1.0 + za * (1.0 - sga)))).astype(dr_ref.dtype)
        dat_ref[...] = dattn.astype(dat_ref.dtype)
        prod = dattn * attn
        for h in range(NH):
            sl = slice(HD * h, HD * h + HD)
            dl_ref[:, sl] = jnp.broadcast_to(jnp.sum(prod[:, sl], axis=1, keepdims=True), (tm, HD))
        zp = zp_ref[...]
        sgp = _sigmoid(zp)
        mpre = mp_ref[...]
        ps = ps_ref[...]
        dmixed = dp * (zp * sgp)
        dr_ref[:, 512:1024] = (dp * (mpre * ps) * (sgp * (1.0 + zp * (1.0 - sgp)))).astype(dr_ref.dtype)
        dps_ref[...] += jnp.sum(dmixed * mpre, axis=0, keepdims=True)
        dmp_ref[...] = (dmixed * ps).astype(dmp_ref.dtype)

    full = lambda shape: pl.BlockSpec(shape, lambda i: (0, 0))
    tD = pl.BlockSpec((tm, D), lambda i: (i, 0))
    tA = pl.BlockSpec((tm, AW), lambda i: (i, 0))
    rb = lambda c: pl.BlockSpec((tm, AW), lambda i: (i, c))
    return pl.pallas_call(
        body, name="bwd_gates", grid=(S // tm,),
        in_specs=[tD, full((D, D)), full((AW, D)), full((AW, D)), rb(0), rb(1), rb(2), rb(3), rb(4), rb(5),
                  tA, tA, tA, tA, full((1, AW))],
        out_specs=[tD, tD, pl.BlockSpec((tm, 3072), lambda i: (i, 0)), tA, tA, tA, full((1, AW))],
        out_shape=[_sds((S, D), CDT), _sds((S, D), CDT), _sds((S, REST_W), CDT), _sds((S, AW), CDT),
                   _sds((S, AW), F32), _sds((S, AW), CDT), _sds((1, AW), F32)],
        compiler_params=_cp(("arbitrary",), vmem=56),
    )(dmo, wout, wab, wpb, rest, rest, rest, rest, rest, rest, ag, pg, attn, mp, pool_scale)


def _pool_bwd(dmp, pt, pool_w, drest):
    S = dmp.shape[0]
    LS = S // R16
    PG = 128
    d3 = dmp.reshape(R16, LS, AW)
    dr3 = drest.reshape(R16, LS, REST_W)

    def body(d_ref, pt_ref, pw_ref, dr_in, du_ref, dpw_ref, g_ref, s_ref):
        del dr_in
        g = pl.program_id(0)
        win = jnp.left_shift(2, g)
        row = lax.broadcasted_iota(jnp.int32, (LS, PG), 0)
        pw = pw_ref[...].astype(CDT)
        dpw = jnp.zeros((PG, PG), F32)
        for r in range(R16):
            dm = d_ref[r]
            dpw = dpw + _dot(pt_ref[:, r * LS:(r + 1) * LS], dm, NN)
            dpooled = _dot(dm, pw, NT)
            g_ref[r] = dpooled
            s_ref[r] = dpooled / _pool_counts(r, LS, win)
        dpw_ref[...] = dpw
        for r in range(R16):
            acc = s_ref[r]
            for j in range(1, R16):
                coef = jnp.where(j < win, 1.0, 0.0).astype(F32)
                if r + j < R16:
                    src = s_ref[r + j]
                else:
                    src = jnp.where(row == LS - 1, 0.0, pltpu.roll(s_ref[r + j - R16], LS - 1, 0))
                acc = acc + coef * src
            du_ref[r] = (acc - g_ref[r]).astype(du_ref.dtype)

    du, dpw = pl.pallas_call(
        body, name="pool_bwd", grid=(4,),
        in_specs=[pl.BlockSpec((R16, LS, PG), lambda g: (0, 0, g)), pl.BlockSpec((PG, S), lambda g: (g, 0)),
                  pl.BlockSpec((None, PG, PG), lambda g: (g, 0, 0)), pl.BlockSpec(memory_space=pl.ANY)],
        out_specs=[pl.BlockSpec((R16, LS, PG), lambda g: (0, 0, 24 + g)),
                   pl.BlockSpec((None, PG, PG), lambda g: (g, 0, 0))],
        out_shape=[_sds((R16, LS, REST_W), CDT), _sds((4, PG, PG), F32)],
        scratch_shapes=[pltpu.VMEM((R16, LS, PG), F32), pltpu.VMEM((R16, LS, PG), F32)],
        input_output_aliases={3: 0},
        compiler_params=_cp(("parallel",)),
    )(d3, pt, pool_w, dr3)
    return du.reshape(S, REST_W), dpw


def _attn_bwd(qkv, dattn, Lb, dlb, biases, geom, dqkv_prev):
    S = qkv.shape[0]
    gi, T, R, has_prev, N = geom.gi, geom.T, geom.rows, geom.has_prev, geom.n_tiles
    H2 = T // 2
    nd_outer = len(geom.outer)
    qv = geom.view(qkv)
    cq, ck, cv = 3 * gi, 3 * gi + 1, 3 * gi + 2
    W3 = 3 * AW

    def heads(q_ref, kp_ref, kc_ref, vp_ref, vc_ref, do_ref, L_ref, dl_ref, ba_ref, bb_ref, pm, dsa_ref, dsb_ref, emit):
        q_lo, q_hi = _split_rows(_ld(q_ref), R)
        do_lo, do_hi = _split_rows(_ld(do_ref), R)
        L_lo, L_hi = _split_rows(_ld(L_ref), R)
        dl_lo, dl_hi = _split_rows(_ld(dl_ref), R)
        kc_lo, kc_hi = _split_rows(_ld(kc_ref), R)
        vc_lo, vc_hi = _split_rows(_ld(vc_ref), R)
        k_b = jnp.concatenate([kc_lo, kc_hi], axis=0)
        v_b = jnp.concatenate([vc_lo, vc_hi], axis=0)
        if has_prev:
            k_a = jnp.concatenate([_split_rows(_ld(kp_ref), R)[1], kc_lo], axis=0)
            v_a = jnp.concatenate([_split_rows(_ld(vp_ref), R)[1], vc_lo], axis=0)
        else:
            k_a, v_a = k_b, v_b
        low = lax.broadcasted_iota(jnp.int32, (1, 2 * HD), 1) < HD
        for j in range(NH // 2):
            sl = slice(2 * HD * j, 2 * HD * (j + 1))
            res = []
            for q, k, v, do, L, dl, b_ref, mask, ds_ref in (
                    (q_lo, k_a, v_a, do_lo, L_lo, dl_lo, ba_ref, pm, dsa_ref),
                    (q_hi, k_b, v_b, do_hi, L_hi, dl_hi, bb_ref, None, dsb_ref)):
                q2 = q[:, sl].astype(F32) * 0.125
                do2 = do[:, sl].astype(F32)
                k2, v2 = k[:, sl], v[:, sl]
                L2, dl2 = L[:, sl], dl[:, sl]
                qs = jnp.concatenate([jnp.where(low, q2, 0.0), jnp.where(low, 0.0, q2)], axis=0).astype(q.dtype)
                dom = jnp.concatenate([jnp.where(low, do2, 0.0), jnp.where(low, 0.0, do2)], axis=0).astype(q.dtype)
                Lc = jnp.concatenate([L2[:, 0:1], L2[:, HD:HD + 1]], axis=0)
                dlc = jnp.concatenate([dl2[:, 0:1], dl2[:, HD:HD + 1]], axis=0)
                s = _dot(qs, k2, NT) + b_ref[j]
                if mask is not None:
                    s = s + mask
                p = jnp.exp(s - Lc)
                ds = p * (_dot(dom, v2, NT) - dlc)
                ds_ref[j] += ds
                dsc = ds.astype(q.dtype)
                dq_st = _dot(dsc, k2, NN)
                res.append((jnp.where(low, dq_st[:H2], dq_st[H2:]) * 0.125, _dot(dsc, qs, TN),
                            _dot(p.astype(q.dtype), dom, TN)))
            (dq_lo, dk_a, dv_a), (dq_hi, dk_b, dv_b) = res
            dq = _merge_rows(dq_lo, dq_hi, R)
            if has_prev:
                emit(sl, dq, _merge_rows(dk_a[H2:] + dk_b[:H2], dk_b[H2:], R),
                     _merge_rows(dv_a[H2:] + dv_b[:H2], dv_b[H2:], R), dk_a[:H2], dv_a[:H2])
            else:
                emit(sl, dq, _merge_rows(dk_a[:H2] + dk_b[:H2], dk_a[H2:] + dk_b[H2:], R),
                     _merge_rows(dv_a[:H2] + dv_b[:H2], dv_a[H2:] + dv_b[H2:], R), None, None)

    def first_step():
        ok = pl.program_id(nd_outer) == 0 if has_prev else pl.program_id(0) == 0
        for a in range(nd_outer if has_prev else 0):
            ok = ok & (pl.program_id(a) == 0)
        return ok

    if has_prev:
        def body(q_ref, kp_ref, kc_ref, vp_ref, vc_ref, do_ref, L_ref, dl_ref, ba_ref, bb_ref, prev_in, out_ref,
                 dsa_ref, dsb_ref, hold, new, pp):
            del prev_in
            s_id = pl.program_id(nd_outer)

            @pl.when(first_step())
            def _():
                dsa_ref[...] = jnp.zeros_like(dsa_ref)
                dsb_ref[...] = jnp.zeros_like(dsb_ref)

            @pl.when(s_id < N)
            def _():
                def emit(sl, dq, dk, dv, dk_prev, dv_prev):
                    new[:, sl] = dq
                    new[:, slice(AW + sl.start, AW + sl.stop)] = dk
                    new[:, slice(2 * AW + sl.start, 2 * AW + sl.stop)] = dv
                    pp[:, sl] = dk_prev
                    pp[:, slice(AW + sl.start, AW + sl.stop)] = dv_prev

                heads(q_ref, kp_ref, kc_ref, vp_ref, vc_ref, do_ref, L_ref, dl_ref, ba_ref, bb_ref,
                      _prev_mask(s_id, H2, T), dsa_ref, dsb_ref, emit)

            @pl.when((s_id >= 1) & (s_id < N))
            def _():
                _st(out_ref, hold[:, 0:AW], slice(0, AW))
                late = _merge_rows(jnp.zeros((H2, 2 * AW), F32), pp[...], R)
                _st(out_ref, hold[:, AW:W3] + late, slice(AW, W3))

            @pl.when(s_id == N)
            def _():
                _st(out_ref, hold[...])

            @pl.when(s_id < N)
            def _():
                hold[...] = new[...]

        cur = lambda n: jnp.minimum(n, N - 1)
        prev = lambda n: jnp.clip(n - 1, 0, N - 1)
        delayed = lambda n: jnp.maximum(n - 1, 0)
        in_specs = [geom.spec(AW, cq, cur), geom.spec(AW, ck, prev), geom.spec(AW, ck, cur),
                    geom.spec(AW, cv, prev), geom.spec(AW, cv, cur),
                    geom.spec(AW, 0, cur), geom.spec(AW, 0, cur), geom.spec(AW, 0, cur)]
        args = [qv] * 5 + [geom.view(dattn), geom.view(Lb), geom.view(dlb)]
        out_spec = geom.spec(W3, gi, delayed)
        grid = geom.outer + (N + 1,)
        scratch = [pltpu.VMEM((T, W3), F32), pltpu.VMEM((T, W3), F32), pltpu.VMEM((H2, 2 * AW), F32)]
    else:
        def body(q_ref, kc_ref, vc_ref, do_ref, L_ref, dl_ref, ba_ref, bb_ref, prev_in, out_ref, dsa_ref, dsb_ref):
            del prev_in

            @pl.when(first_step())
            def _():
                dsa_ref[...] = jnp.zeros_like(dsa_ref)
                dsb_ref[...] = jnp.zeros_like(dsb_ref)

            def emit(sl, dq, dk, dv, dk_prev, dv_prev):
                _st(out_ref, dq, sl)
                _st(out_ref, dk, slice(AW + sl.start, AW + sl.stop))
                _st(out_ref, dv, slice(2 * AW + sl.start, 2 * AW + sl.stop))

            heads(q_ref, None, kc_ref, None, vc_ref, do_ref, L_ref, dl_ref, ba_ref, bb_ref, None, dsa_ref, dsb_ref,
                  emit)

        cur = lambda n: n
        in_specs = [geom.spec(AW, cq, cur), geom.spec(AW, ck, cur), geom.spec(AW, cv, cur),
                    geom.spec(AW, 0, cur), geom.spec(AW, 0, cur), geom.spec(AW, 0, cur)]
        args = [qv] * 3 + [geom.view(dattn), geom.view(Lb), geom.view(dlb)]
        out_spec = geom.spec(W3, gi, cur)
        grid = geom.outer
        scratch = []

    whole = lambda b: pl.BlockSpec(b.shape, lambda *ids: (0, 0, 0))
    biases = [b.reshape(NH // 2, 2 * H2, b.shape[2]) for b in biases]
    in_specs += [whole(b) for b in biases]
    in_specs.append(pl.BlockSpec(memory_space=pl.ANY))
    if dqkv_prev is None:
        last, aliases = biases[0], {}
    else:
        last, aliases = geom.view(dqkv_prev), {len(in_specs) - 1: 0}
    dq, dsa, dsb = pl.pallas_call(
        body, name=f"attn_bwd{gi}", grid=grid, in_specs=in_specs,
        out_specs=[out_spec] + [whole(b) for b in biases],
        out_shape=[_sds(geom.view_shape(QKV_W), CDT)] + [_sds(b.shape, F32) for b in biases],
        scratch_shapes=scratch,
        input_output_aliases=aliases,
        compiler_params=_cp(("arbitrary",) * len(grid), vmem=56),
    )(*args, *biases, last)
    return geom.unview(dq), [d.reshape(NH, H2, d.shape[2]) for d in (dsa, dsb)]


def _h_bwd(dh, dx2, x, norm_g, scale):
    S = x.shape[0]

    def body(dh_ref, dx2_ref, x_ref, g_ref, sc_ref, gx_ref, dsh_ref, dsc_ref, dng_ref, nat_ref):
        i = pl.program_id(0)

        @pl.when(i == 0)
        def _():
            dsh_ref[...] = jnp.zeros_like(dsh_ref)
            dsc_ref[...] = jnp.zeros_like(dsc_ref)
            dng_ref[...] = jnp.zeros_like(dng_ref)

        for r in range(R16):
            nat_ref[:, r, :] = dh_ref[r]
        dh = jnp.concatenate([nat_ref[l] for l in range(R16)], axis=0)
        g = g_ref[...]
        one_sc = 1.0 + sc_ref[...]
        xv = x_ref[...]
        r1 = lax.rsqrt(jnp.mean(xv * xv, axis=-1, keepdims=True) + EPS)
        xn = xv * r1
        dhx = dh * xn
        dxn = dh * (g * one_sc)
        dx1 = r1 * (dxn - xn * jnp.mean(dxn * xn, axis=-1, keepdims=True))
        gx_ref[...] = dx2_ref[...] + dx1
        dsh_ref[...] += jnp.sum(dh, axis=0, keepdims=True)
        dsc_ref[...] += jnp.sum(dhx * g, axis=0, keepdims=True)
        dng_ref[...] += jnp.sum(dhx * one_sc, axis=0, keepdims=True)

    nat = pl.BlockSpec((TOK, D), lambda i: (i, 0))
    vec = pl.BlockSpec((1, D), lambda i: (0, 0))
    return pl.pallas_call(
        body, name="h_bwd", grid=(S // TOK,),
        in_specs=[pl.BlockSpec((R16, R16, D), lambda i: (0, i, 0)), nat, nat, vec, vec],
        out_specs=[nat, vec, vec, vec],
        out_shape=[_sds((S, D), F32), _sds((1, D), F32), _sds((1, D), F32), _sds((1, D), F32)],
        scratch_shapes=[pltpu.VMEM((R16, R16, D), F32)],
        compiler_params=_cp(("arbitrary",)),
    )(dh, dx2, x, norm_g, scale)


def _local_step(x, target, shift, scale, gate, norm_g, w_in_full, pool_w, pool_scale, wab, wpb, wout, rel_bias,
                final_g):
    st = _local_pre(x, shift, scale, norm_g, rel_bias)
    st = _local_grads(st, target, gate, w_in_full, pool_w, pool_scale, wab, wpb, wout, final_g)
    d_rel_bias = _local_bias_grads(st)
    gx, dmod, dng = _local_input_grads(st, w_in_full, norm_g, scale)
    return dict(loss=st["loss"], grad_x=gx, dw_in=st["dw_in"], dw_ab=st["dw_ab"], dw_pb=st["dw_pb"],
                dw_out=st["dw_out"], d_norm_g=dng, d_pool_w=st["dpw"], d_pool_scale=st["dps"],
                d_rel_bias=d_rel_bias, d_final_g=st["gfg"], dmod=dmod)


def _local_pre(x, shift, scale, norm_g, rel_bias, dep=None):
    S = x.shape[0]
    LS = S // R16
    geoms = [_Geom(gi, LS) for gi in range(3)]
    idx_np = [[g.bucket_index(sub) for sub in range(2)] for g in geoms]
    buckets = [[tuple(int(b) for b in np.unique(ix) if b >= 0) for ix in pair] for pair in idx_np]
    idxs = [[jnp.asarray(ix) for ix in pair] for pair in idx_np]
    biases = [[_bias_table(idxs[gi][sub], buckets[gi][sub], rel_bias, gi, sub, dep) for sub in range(2)]
              for gi in range(3)]
    h = _norm_mod(x, norm_g, shift, scale, dep).reshape(S, D)
    return dict(geoms=geoms, idxs=idxs, buckets=buckets, biases=biases, x=x, h=h, ht=_tcast(h, "h_transpose"))


def _local_grads(st, target, gate, w_in_full, pool_w, pool_scale, wab, wpb, wout, final_g):
    geoms, idxs, biases, x, h, ht = (st[k] for k in ("geoms", "idxs", "biases", "x", "h", "ht"))
    S = h.shape[0]
    LS = S // R16
    qkv, rest = _proj(h, w_in_full)

    os_, ls_ = [], []
    for gi in range(3):
        o, l = _attn_fwd(qkv, biases[gi], geoms[gi])
        os_.append(o)
        ls_.append(l)
    attn, Lb, ag, agt = _attn_combine(os_, ls_, rest)
    pt, mp, pg, pgt = _pool_fwd(rest, pool_w, pool_scale)
    merged, mt = _branch_merge(ag, pg, wab, wpb, rest)
    dx2, dmo3, loss, gfg, dgate = _out_loss(merged.reshape(R16, LS, D), wout, x, target, gate, final_g)
    dmo = dmo3.reshape(S, D)

    dya, dyp, drest, dattn, dlb, dmp, dps = _bwd_gates(dmo, wout, wab, wpb, rest, ag, pg, attn, mp, pool_scale)
    dw_out = _mm_tn(mt, dmo, n_total=D, out_blk=lambda j: j, name="dw_out")
    dw_ab = _mm_tn(agt, dya, n_total=D, out_blk=lambda j: j, name="dw_attn_br")
    dw_pb = _mm_tn(pgt, dyp, n_total=D, out_blk=lambda j: j, name="dw_pool_br")
    drest, dpw = _pool_bwd(dmp, pt, pool_w, drest)

    dqkv = None
    dsums = []
    for gi in range(3):
        dqkv, dsum = _attn_bwd(qkv, dattn, Lb, dlb, biases[gi], geoms[gi], dqkv)
        dsums.append(dsum)

    dw_in = _mm_tn(ht, dqkv, n_total=8192, out_blk=lambda j: j, name="dw_in_qkv")
    dw_in = _mm_tn(ht, drest, n_total=8192, out_blk=_rest_blk, name="dw_in_rest", prev=dw_in)
    return dict(st, loss=loss[0, 0], dw_in=dw_in, dw_ab=dw_ab, dw_pb=dw_pb, dw_out=dw_out, dpw=dpw, dps=dps, gfg=gfg,
                dgate=dgate, dsums=dsums, dqkv=dqkv, drest=drest, dx2=dx2)


def _local_bias_grads(st, dep=None):
    drb = [_bias_grad(st["dsums"][gi][0], st["idxs"][gi][0], st["buckets"][gi][0], gi, 0, dep if gi == 0 else None) +
           _bias_grad(st["dsums"][gi][1], st["idxs"][gi][1], st["buckets"][gi][1], gi, 1) for gi in range(3)]
    return jnp.concatenate(drb, axis=1)


def _local_input_grads(st, w_in_full, norm_g, scale, dep=None):
    S = st["h"].shape[0]
    LS = S // R16
    dh = _dh(st["dqkv"], st["drest"], w_in_full, dep)
    gx, dsh, dsc, dng = _h_bwd(dh.reshape(R16, LS, D), st["dx2"], st["x"], norm_g, scale)
    return gx, jnp.concatenate([dsh, dsc, st["dgate"]], axis=1), dng


def _my_place():
    return lax.axis_index("x"), lax.axis_index("y"), lax.axis_index("c")


def _block_of(ref, axis, idx, width):
    if axis is None:
        return ref.at[idx]
    start = pl.multiple_of(idx * width, width)
    if axis == 0:
        return ref.at[pl.ds(start, width)]
    return ref.at[:, pl.ds(start, width)]


_SEM = pl.BlockSpec(memory_space=pltpu.SEMAPHORE)
_EFFECT = pltpu.SideEffectType.DATAFLOW_SIDE_EFFECTING


def _split_start(name, srcs, lands, ncopy, plan, dep=None):
    arrs = list(srcs) + list(lands)
    na, ns = len(arrs), len(srcs)
    dep_specs, dep_args = _after(dep)

    def body(*refs):
        send_sems, recv_sems, token = refs[na + len(dep_args)], refs[na + len(dep_args) + 1], refs[-1]
        for k, (s, d, dev) in enumerate(plan(refs[:ns], refs[ns:na])):
            pltpu.make_async_remote_copy(src_ref=s, dst_ref=d, send_sem=send_sems.at[k], recv_sem=recv_sems.at[k],
                                         device_id=dev, device_id_type=MESH).start()
        token[...] = jnp.zeros_like(token)

    outs = pl.pallas_call(
        body, name=name,
        out_shape=(pltpu.SemaphoreType.DMA((ncopy,)), pltpu.SemaphoreType.DMA((ncopy,)),
                   *[pltpu.HBM(a.shape, a.dtype) for a in arrs], _sds((8, 128), F32)),
        in_specs=[_HBM] * na + dep_specs,
        out_specs=(_SEM, _SEM, *[_HBM] * na, pl.BlockSpec(memory_space=pltpu.VMEM)),
        input_output_aliases={i: 2 + i for i in range(na)},
        compiler_params=pltpu.CompilerParams(has_side_effects=_EFFECT),
    )(*[pltpu.with_memory_space_constraint(a, pltpu.HBM) for a in arrs], *dep_args)
    return (outs[0], outs[1]), list(outs[2:2 + ns]), list(outs[2 + ns:2 + na]), outs[-1]


def _split_wait(name, sems, srcs, lands, plan, after):
    arrs = list(srcs) + list(lands)
    na, ns = len(arrs), len(srcs)
    after = list(after) if isinstance(after, (list, tuple)) else [after]

    def body(*refs):
        send_sems, recv_sems = refs[na], refs[na + 1]
        for k, (s, d, dev) in enumerate(plan(refs[:ns], refs[ns:na])):
            cp = pltpu.make_async_remote_copy(src_ref=s, dst_ref=d, send_sem=send_sems.at[k], recv_sem=recv_sems.at[k],
                                              device_id=dev, device_id_type=MESH)
            cp.wait_send()
            cp.wait_recv()

    outs = pl.pallas_call(
        body, name=name, out_shape=[pltpu.HBM(a.shape, a.dtype) for a in arrs],
        in_specs=[_HBM] * na + [_SEM, _SEM] + [pl.BlockSpec(memory_space=pl.ANY)] * len(after),
        out_specs=[_HBM] * na,
        input_output_aliases={i: i for i in range(na)},
        compiler_params=pltpu.CompilerParams(has_side_effects=_EFFECT),
    )(*arrs, sems[0], sems[1], *after)
    return list(outs[:ns]), list(outs[ns:])


def _gather_chips_plan(layout):
    def plan(src, land):
        x, y, c = _my_place()
        me = 4 * x + 2 * y + c
        return [(src[p], _block_of(land[p], layout[p][0], me, layout[p][1]), (cx, cy, c))
                for p in range(len(src)) for cx, cy in [(1 - x, y), (x, 1 - y)]]
    return plan


def _pair_plan(layout):
    def plan(src, land):
        x, y, c = _my_place()
        n = len(layout)
        return [(_block_of(src[p], layout[p][0], 2 * q + (1 - c), layout[p][1]), land[p].at[q], (x, y, 1 - c))
                for p in range(n) for q in range(4)] + [(src[n], land[n], (x, y, 1 - c))]
    return plan


def _chips_plan(src, land):
    x, y, c = _my_place()
    n = len(src) - 1
    chips = [(1 - x, y), (x, 1 - y), (1 - x, 1 - y)]
    return [(src[p].at[2 * cx + cy], land[p].at[2 * x + y], (cx, cy, c)) for p in range(n) for cx, cy in chips] + \
           [(src[n], land[n].at[2 * x + y], (cx, cy, c)) for cx, cy in chips]


def _pair_share(gathered):
    n = len(gathered)

    def body(*refs):
        land = refs[n:2 * n]
        send_sems, recv_sems, fwd_send, fwd_recv = refs[2 * n:]
        x, y, c = _my_place()
        k_dg = 3 - (2 * x + y)
        south = c == 0
        pass_on = 4 * jnp.where(south, 1 - x, x) + 2 * jnp.where(south, y, 1 - y) + c
        to = (jnp.where(south, x, 1 - x), jnp.where(south, 1 - y, y), c)

        def forward(p):
            return pltpu.make_async_remote_copy(
                src_ref=land[p].at[pass_on], dst_ref=land[p].at[pass_on], send_sem=fwd_send.at[p],
                recv_sem=fwd_recv.at[p], device_id=to, device_id_type=MESH)

        def to_sibling(p, k):
            blk = land[p].at[2 * k + c]
            return pltpu.make_async_remote_copy(
                src_ref=blk, dst_ref=blk, send_sem=send_sems.at[4 * p + k], recv_sem=recv_sems.at[4 * p + k],
                device_id=(x, y, 1 - c), device_id_type=MESH)

        for p in range(n):
            forward(p).start()
        for p in range(n):
            for k in range(4):
                @pl.when(k != k_dg)
                def _():
                    to_sibling(p, k).start()
        for p in range(n):
            dg = land[p].at[2 * k_dg + c]
            pltpu.make_async_remote_copy(src_ref=dg, dst_ref=dg, send_sem=fwd_send.at[p], recv_sem=fwd_recv.at[p],
                                         device_id=to, device_id_type=MESH).wait_recv()
            for k in range(4):
                @pl.when(k == k_dg)
                def _():
                    to_sibling(p, k).start()
        for p in range(n):
            forward(p).wait_send()
            for k in range(4):
                to_sibling(p, k).wait_send()
                blk = land[p].at[2 * k + 1 - c]
                pltpu.make_async_remote_copy(
                    src_ref=blk, dst_ref=blk, send_sem=send_sems.at[4 * p + k], recv_sem=recv_sems.at[4 * p + k],
                    device_id=(x, y, 1 - c), device_id_type=MESH).wait_recv()

    return pl.pallas_call(
        body, name="gather_pair_share", in_specs=[_HBM] * n, out_specs=[_HBM] * n,
        out_shape=[_sds(g.shape, g.dtype) for g in gathered],
        input_output_aliases={p: p for p in range(n)},
        scratch_shapes=[pltpu.SemaphoreType.DMA((4 * n,)), pltpu.SemaphoreType.DMA((4 * n,)),
                        pltpu.SemaphoreType.DMA((n,)), pltpu.SemaphoreType.DMA((n,))],
    )(*gathered)


def _place_slot(arrs, slot, n_slots, name):
    n = len(arrs)

    def body(s_ref, *refs):
        for p in range(n):
            refs[n + p][...] = refs[p][...]

    def spec(a):
        tail = a.shape[-2:]
        return pl.BlockSpec((None,) + tail, lambda i, s_ref: (s_ref[0], 0, 0))

    in_specs = [spec(a) if a.ndim == 3 else pl.BlockSpec(a.shape, lambda i, s_ref: (0, 0)) for a in arrs]
    return pl.pallas_call(
        body, name=name,
        grid_spec=pltpu.PrefetchScalarGridSpec(num_scalar_prefetch=1, grid=(1,), in_specs=in_specs,
                                               out_specs=[spec(a) for a in arrs]),
        out_shape=[_sds((n_slots,) + a.shape[-2:], a.dtype) for a in arrs],
        compiler_params=_cp(("arbitrary",)),
    )(slot, *arrs)


def _cols_from_blocks(w8, name):
    _, R, C = w8.shape

    def body(w_ref, o_ref):
        for e in range(8):
            o_ref[:, C * e:C * (e + 1)] = w_ref[e]

    return pl.pallas_call(body, name=name, out_shape=_sds((R, 8 * C), w8.dtype), compiler_params=_cp())(w8)


def _all_gather(srcs, out_shapes, layout, name):
    n = len(srcs)

    def body(*refs):
        src, out = refs[:n], refs[n:2 * n]
        send_sems, recv_sems, local_sems = refs[2 * n:]
        x, y, c = _my_place()
        me, sibling = (x, y, c), (x, y, 1 - c)
        chips = [(1 - x, y), (x, 1 - y), (1 - x, 1 - y)]

        def region(p, dev):
            px, py, pc = dev
            return _block_of(out[p], layout[p][0], 4 * px + 2 * py + pc, layout[p][1])

        def copy(p, k, block, to, src_ref=None):
            return pltpu.make_async_remote_copy(
                src_ref=region(p, block) if src_ref is None else src_ref, dst_ref=region(p, block),
                send_sem=send_sems.at[p * 7 + k], recv_sem=recv_sems.at[p * 7 + k],
                device_id=to, device_id_type=MESH)

        mine = [pltpu.make_async_copy(src[p], region(p, me), local_sems.at[p]) for p in range(n)]
        for cp in mine:
            cp.start()
        first = []
        for p in range(n):
            first.append(copy(p, 0, me, sibling, src[p]))
            first += [copy(p, 1 + j, me, (*chip, c), src[p]) for j, chip in enumerate(chips)]
        for cp in first:
            cp.start()
        passed = []
        for j, chip in enumerate(chips):
            for p in range(n):
                copy(p, 1 + j, (*chip, c), me).wait_recv()
                fwd = copy(p, 4 + j, (*chip, c), sibling)
                fwd.start()
                passed.append(fwd)
        for p in range(n):
            copy(p, 0, sibling, me).wait_recv()
            for j, chip in enumerate(chips):
                copy(p, 4 + j, (*chip, 1 - c), me).wait_recv()
        for cp in first + passed:
            cp.wait_send()
        for cp in mine:
            cp.wait()

    return pl.pallas_call(
        body, name=name, in_specs=[_HBM] * n, out_specs=[_HBM] * n,
        out_shape=[_sds(s, a.dtype) for s, a in zip(out_shapes, srcs)],
        scratch_shapes=[pltpu.SemaphoreType.DMA((7 * n,)), pltpu.SemaphoreType.DMA((7 * n,)),
                        pltpu.SemaphoreType.DMA((n,))],
    )(*srcs)


def _reduce_pair_exchange(dws, layout, smalls):
    n = len(dws)

    def body(*refs):
        src, sm = refs[:n], refs[n]
        out, sm_out = refs[n + 1:2 * n + 1], refs[2 * n + 1]
        send_sems, recv_sems = refs[2 * n + 2:]
        x, y, c = _my_place()
        sibling = (x, y, 1 - c)
        cps = []
        for p in range(n):
            for q in range(4):
                cps.append(pltpu.make_async_remote_copy(
                    src_ref=_block_of(src[p], layout[p][0], 2 * q + (1 - c), layout[p][1]), dst_ref=out[p].at[q],
                    send_sem=send_sems.at[4 * p + q], recv_sem=recv_sems.at[4 * p + q],
                    device_id=sibling, device_id_type=MESH))
        cps.append(pltpu.make_async_remote_copy(src_ref=sm, dst_ref=sm_out, send_sem=send_sems.at[4 * n],
                                                recv_sem=recv_sems.at[4 * n], device_id=sibling, device_id_type=MESH))
        for cp in cps:
            cp.start()
        for cp in cps:
            cp.wait()

    def shard_shape(a, lay):
        axis, w = lay
        return (4, w, a.shape[1]) if axis == 0 else (4, a.shape[0], w)

    return pl.pallas_call(
        body, name="reduce_pair_exchange", in_specs=[_HBM] * (n + 1), out_specs=[_HBM] * (n + 1),
        out_shape=[_sds(shard_shape(a, l), F32) for a, l in zip(dws, layout)] + [_sds(smalls.shape, F32)],
        scratch_shapes=[pltpu.SemaphoreType.DMA((4 * n + 1,)), pltpu.SemaphoreType.DMA((4 * n + 1,))],
    )(*dws, smalls)


def _pair_add(dw, recv, axis, width, cidx, name):
    if axis == 0:
        tr = min(width, 256)
        C = dw.shape[1]
        grid = (4, width // tr)
        nb = width // tr
        own = pl.BlockSpec((tr, C), lambda q, i, c_ref: ((2 * q + c_ref[0]) * nb + i, 0))
        oth = pl.BlockSpec((None, tr, C), lambda q, i, c_ref: (q, i, 0))
    else:
        R = dw.shape[0]
        tr = min(R, 256)
        grid = (4, R // tr)
        own = pl.BlockSpec((tr, width), lambda q, i, c_ref: (i, 2 * q + c_ref[0]))
        oth = pl.BlockSpec((None, tr, width), lambda q, i, c_ref: (q, i, 0))

    def body(c_ref, a_ref, b_ref, o_ref):
        o_ref[...] = (a_ref[...] + b_ref[...]).astype(o_ref.dtype)

    return pl.pallas_call(
        body, name=name,
        grid_spec=pltpu.PrefetchScalarGridSpec(num_scalar_prefetch=1, grid=grid, in_specs=[own, oth], out_specs=oth),
        out_shape=_sds(recv.shape, CDT), compiler_params=_cp(("parallel", "parallel")),
    )(cidx, dw, recv)


def _add2(a, b, name):
    def body(a_ref, b_ref, o_ref):
        o_ref[...] = a_ref[...] + b_ref[...]

    return pl.pallas_call(body, name=name, out_shape=_sds(a.shape, F32))(a, b)


def _reduce_chip_exchange(parts):
    n = len(parts)

    def body(*refs):
        src, out = refs[:n], refs[n:2 * n]
        send_sems, recv_sems, local_sems = refs[2 * n:]
        x, y, c = _my_place()
        my_slot = 2 * x + y
        chips = [(1 - x, y), (x, 1 - y), (1 - x, 1 - y)]
        cps = []
        for p in range(n):
            keep = pltpu.make_async_copy(src[p].at[my_slot], out[p].at[my_slot], local_sems.at[p])
            keep.start()
            cps.append(keep)
            for k, (cx, cy) in enumerate(chips):
                cps.append(pltpu.make_async_remote_copy(
                    src_ref=src[p].at[2 * cx + cy], dst_ref=out[p].at[my_slot],
                    send_sem=send_sems.at[3 * p + k], recv_sem=recv_sems.at[3 * p + k],
                    device_id=(cx, cy, c), device_id_type=MESH))
                cps[-1].start()
        for cp in cps:
            cp.wait()

    return pl.pallas_call(
        body, name="reduce_chip_exchange", in_specs=[_HBM] * n, out_specs=[_HBM] * n,
        out_shape=[_sds(a.shape, a.dtype) for a in parts],
        scratch_shapes=[pltpu.SemaphoreType.DMA((3 * n,)), pltpu.SemaphoreType.DMA((3 * n,)),
                        pltpu.SemaphoreType.DMA((n,))],
    )(*parts)


def _adaln_part(c_all, w_ada, b_part, dep=None):
    dep_specs, dep_args = _after(dep)
    vmem = pl.BlockSpec(memory_space=pltpu.VMEM)

    def body(c_ref, w_ref, b_ref, *rest):
        rest[-1][...] = _dot(c_ref[...].astype(CDT), w_ref[...].astype(CDT), NN) + b_ref[...]

    return pl.pallas_call(body, name="adaln_part", in_specs=[vmem, vmem, vmem] + dep_specs, out_specs=vmem,
                          out_shape=_sds((8, w_ada.shape[1]), F32),
                          compiler_params=_cp())(c_all, w_ada, b_part, *dep_args)


def _adam_math(w, g, m, v):
    m = ADAM_B1 * m + (1.0 - ADAM_B1) * g
    v = ADAM_B2 * v + (1.0 - ADAM_B2) * (g * g)
    m_hat = m / (1.0 - ADAM_B1 ** ADAM_STEP)
    v_hat = v / (1.0 - ADAM_B2 ** ADAM_STEP)
    delta = -ADAM_LR * (m_hat / (jnp.sqrt(v_hat) + ADAM_EPS) + ADAM_WD * w)
    return delta, m, v


def _adam(gparts, w, m, v, name):
    P, R, C = gparts.shape
    tr = R if R <= 256 else 256

    def body(g_ref, w_ref, m_ref, v_ref, go_ref, d_ref, mo_ref, vo_ref):
        g = g_ref[0].astype(F32)
        for i in range(1, P):
            g = g + g_ref[i].astype(F32)
        d, mn, vn = _adam_math(w_ref[...], g, m_ref[...], v_ref[...])
        go_ref[...] = g
        d_ref[...] = d
        mo_ref[...] = mn
        vo_ref[...] = vn

    t = pl.BlockSpec((tr, C), lambda i: (i, 0))
    return pl.pallas_call(
        body, name=name, grid=(R // tr,),
        in_specs=[pl.BlockSpec((P, tr, C), lambda i: (0, i, 0)), t, t, t], out_specs=[t, t, t, t],
        out_shape=[_sds((R, C), F32)] * 4, compiler_params=_cp(("parallel",)),
    )(gparts, w, m, v)


def _sum_parts(gparts, name):
    P, R, C = gparts.shape

    def body(g_ref, o_ref):
        g = g_ref[0]
        for i in range(1, P):
            g = g + g_ref[i]
        o_ref[...] = g

    return pl.pallas_call(body, name=name, out_shape=_sds((R, C), F32))(gparts)


def _adam_w_ada(c_t, dm, w, m, v):
    R, C = w.shape

    def body(c_ref, dm_ref, w_ref, m_ref, v_ref, go_ref, d_ref, mo_ref, vo_ref):
        g = c_ref[:, 0:1] * dm_ref[0:1, :]
        for b in range(1, 8):
            g = g + c_ref[:, b:b + 1] * dm_ref[b:b + 1, :]
        d, mn, vn = _adam_math(w_ref[...], g, m_ref[...], v_ref[...])
        go_ref[...] = g
        d_ref[...] = d
        mo_ref[...] = mn
        vo_ref[...] = vn

    return pl.pallas_call(body, name="adam_w_ada", out_shape=[_sds((R, C), F32)] * 4,
                          compiler_params=_cp())(c_t, dm, w, m, v)


_SM = dict(norm_g=(0, 8), pool_w=(8, 512), pool_scale=(520, 8), rel_bias=(528, 8), final_g=(536, 8),
           dmod=(544, 192), loss=(736, 8))
_SM_ROWS = 744


def _pack_rows(a, rows):
    flat = a.reshape(-1).astype(F32)
    return jnp.pad(flat, (0, rows * 128 - flat.shape[0])).reshape(rows, 128)


def kernel(x, c, norm_g, w_ada, b_ada, w_in, pool_w, pool_scale, w_attn_br, w_pool_br, w_out, rel_bias, final_g, loss_target, m_norm_g, m_w_ada, m_b_ada, m_w_in, m_pool_w, m_pool_scale, m_w_attn_br, m_w_pool_br, m_w_out, m_rel_bias, m_final_g, v_norm_g, v_w_ada, v_b_ada, v_w_in, v_pool_w, v_pool_scale, v_w_attn_br, v_w_pool_br, v_w_out, v_rel_bias, v_final_g):
    S = x.shape[1]
    px, py, pc = _my_place()
    me = 4 * px + 2 * py + pc
    cidx = jnp.reshape(pc, (1,)).astype(jnp.int32)

    layout = [(1, 1024), (1, 128), (1, 128), (0, 128)]
    final_g2 = final_g.reshape(1, D)
    me1 = jnp.reshape(me, (1,)).astype(jnp.int32)

    (c_rows,) = _all_gather([jnp.broadcast_to(c, (8, D))], [(64, D)], [(0, 8)], "gather_c")
    c_all = c_rows[::8]

    b_part = lax.dynamic_slice(b_ada, (0, me * 384), (1, 384))
    mod_part = _adaln_part(c_all, w_ada[0], b_part)
    (mod_all,) = _all_gather([mod_part], [(8, 8, 384)], [(None, 1)], "gather_mod")
    mod = lax.dynamic_index_in_dim(mod_all, me, axis=1, keepdims=False).reshape(1, 3 * D)
    shift, scale, gate = mod[:, :D], mod[:, D:2 * D], mod[:, 2 * D:]

    shards = [w_in[0].astype(CDT), w_attn_br[0].astype(CDT), w_pool_br[0].astype(CDT), w_out[0].astype(CDT)]
    gplan = _gather_chips_plan([(None, 1)] * 4)
    g_sems, shards, lands, tok = _split_start("gather_chips_start", shards,
                                              _place_slot(shards, me1, 8, "place_own_shards"), 8, gplan, dep=mod_all)
    st = _local_pre(x[0], shift, scale, norm_g, rel_bias, dep=tok)
    _, lands = _split_wait("gather_chips_wait", g_sems, shards, lands, gplan,
                           [st["ht"]] + [b for pair in st["biases"] for b in pair])
    w_in_full, wab8, wpb8, wout8 = _pair_share(lands)
    wab, wpb = _cols_from_blocks(wab8, "w_attn_br_cols"), _cols_from_blocks(wpb8, "w_pool_br_cols")
    wout = wout8.reshape(D, D)

    st = _local_grads(st, loss_target[0], gate, w_in_full, pool_w[0], pool_scale, wab, wpb, wout, final_g2)

    early = jnp.concatenate([_pack_rows(st["dpw"], 512), _pack_rows(st["dps"], 8), _pack_rows(st["gfg"], 8),
                             _pack_rows(st["loss"], 8)], axis=0)
    dws = [st["dw_in"], st["dw_ab"], st["dw_pb"], st["dw_out"]]
    pplan = _pair_plan(layout)
    pair_shapes = [(4, w, a.shape[1]) if ax == 0 else (4, a.shape[0], w) for a, (ax, w) in zip(dws, layout)]
    p_sems, srcs, recv, tok = _split_start("reduce_pair_start", dws + [early],
                                           [lax.empty(s, F32) for s in pair_shapes + [early.shape]], 17, pplan)
    d_rel_bias = _local_bias_grads(st, dep=tok)
    srcs, recv = _split_wait("reduce_pair_wait", p_sems, srcs, recv, pplan, d_rel_bias)
    parts = [_pair_add(dw, r, ax, w, cidx, f"pair_add{i}")
             for i, (dw, r, (ax, w)) in enumerate(zip(srcs[:4], recv[:4], layout))]
    parts.append(_add2(srcs[4], recv[4], "pair_add_smalls"))

    slot1 = jnp.reshape(2 * px + py, (1,)).astype(jnp.int32)
    c_sems, parts, lands, tok = _split_start("reduce_chips_start", parts,
                                             _place_slot(parts, slot1, 4, "place_own_partials"), 15, _chips_plan)
    grad_x, dmod, d_norm_g = _local_input_grads(st, w_in_full, norm_g, scale, dep=tok)

    late = jnp.concatenate([_pack_rows(dmod, 24), _pack_rows(d_norm_g, 8), _pack_rows(d_rel_bias, 8)], axis=0)
    (late_all,) = _all_gather([late], [(8, 40, 128)], [(None, 1)], "gather_late")

    def small_adam(gparts, w, m, v, nvalid, name):
        rows = gparts.shape[1]
        outs = _adam(gparts, _pack_rows(w, rows), _pack_rows(m, rows), _pack_rows(v, rows), f"adam_{name}")
        return [o.reshape(-1)[:nvalid].reshape(w.shape) for o in outs]

    o_norm_g = small_adam(late_all[:, 24:32], norm_g, m_norm_g, v_norm_g, D, "norm_g")
    o_rel_bias = small_adam(late_all[:, 32:40], rel_bias, m_rel_bias, v_rel_bias, 768, "rel_bias")
    dmod_all = late_all[:, 0:24].reshape(8, 3 * D)
    o_b_ada = [o.reshape(1, 3 * D) for o in _adam(dmod_all.reshape(8, 24, 128), b_ada.reshape(24, 128),
                                                   m_b_ada.reshape(24, 128), v_b_ada.reshape(24, 128), "adam_b_ada")]
    dm_mine = lax.dynamic_slice(dmod_all, (0, me * 384), (8, 384))
    o_w_ada = [o[None] for o in _adam_w_ada(c_all.T, dm_mine, w_ada[0], m_w_ada[0], v_w_ada[0])]

    _, (g_in, g_ab, g_pb, g_out, sm4) = _split_wait("reduce_chips_wait", c_sems, parts, lands, _chips_plan,
                                                    [o_w_ada[0], o_b_ada[0], o_norm_g[0], o_rel_bias[0]])
    o_w_in = [o[None] for o in _adam(g_in, w_in[0], m_w_in[0], v_w_in[0], "adam_w_in")]
    o_w_ab = [o[None] for o in _adam(g_ab, w_attn_br[0], m_w_attn_br[0], v_w_attn_br[0], "adam_w_attn_br")]
    o_w_pb = [o[None] for o in _adam(g_pb, w_pool_br[0], m_w_pool_br[0], v_w_pool_br[0], "adam_w_pool_br")]
    o_w_out = [o[None] for o in _adam(g_out, w_out[0], m_w_out[0], v_w_out[0], "adam_w_out")]
    o_pool_w = small_adam(sm4[:, 0:512], pool_w, m_pool_w, v_pool_w, 65536, "pool_w")
    o_pool_scale = small_adam(sm4[:, 512:520], pool_scale, m_pool_scale, v_pool_scale, AW, "pool_scale")
    o_final_g = small_adam(sm4[:, 520:528], final_g, m_final_g, v_final_g, D, "final_g")
    loss = _sum_parts(sm4[:, 528:536], "sum_loss")[0, 0]

    per_w = [o_norm_g, o_w_ada, o_b_ada, o_w_in, o_pool_w, o_pool_scale, o_w_ab, o_w_pb, o_w_out, o_rel_bias, o_final_g]
    outs = [loss, grad_x[None]]
    for k in range(4):
        outs += [o[k] for o in per_w]
    return tuple(outs)
```

```python
import functools
import math

import numpy as np
import jax
import jax.numpy as jnp
from jax import lax
from jax.experimental import pallas as pl
from jax.experimental.pallas import tpu as pltpu

F32 = jnp.float32
CDT = jnp.bfloat16
D = 1024
HD = 64
NH = 8
AW = 512
QKV_W = 4608
REST_W = 3584
R16 = 16
EPS = 1e-6
NEG = -1e30
N_BACK = 128
ADAM_LR, ADAM_B1, ADAM_B2, ADAM_EPS, ADAM_WD, ADAM_STEP = 0.001, 0.9, 0.999, 1e-08, 0.01, 10
MESH = pl.DeviceIdType.MESH
VMEM_MB = 1024 * 1024


def _cp(sem=None, vmem=48, **kw):
    if sem is not None:
        kw["dimension_semantics"] = sem
    return pltpu.CompilerParams(vmem_limit_bytes=vmem * VMEM_MB, **kw)


def _sds(shape, dtype):
    return jax.ShapeDtypeStruct(tuple(shape), dtype)


def _rest_blk(jj):
    return jnp.where(jj == 0, 9, jnp.where(jj == 6, 10, jj + 10))


def _sigmoid(z):
    return 0.5 * jnp.tanh(0.5 * z) + 0.5


def _dot(a, b, dims):
    return lax.dot_general(a, b, (dims, ((), ())), preferred_element_type=F32)


NN = ((1,), (0,))
NT = ((1,), (1,))
TN = ((0,), (0,))


def _ld(ref):
    if len(ref.shape) == 2:
        return ref[...]
    return jnp.concatenate([ref[a] for a in range(ref.shape[0])], axis=0)


def _st(ref, val, cols=None):
    val = val.astype(ref.dtype)
    if len(ref.shape) == 2:
        if cols is None:
            ref[...] = val
        else:
            ref[:, cols] = val
        return
    rows = ref.shape[1]
    for a in range(ref.shape[0]):
        if cols is None:
            ref[a] = val[a * rows:(a + 1) * rows]
        else:
            ref[a, :, cols] = val[a * rows:(a + 1) * rows]


REST_COLS = ((9, 10), (11, 16), (10, 11))
_HBM = pl.BlockSpec(memory_space=pltpu.HBM)


def _resident(shape):
    return pl.BlockSpec(shape, lambda *ids: (0,) * len(shape), pipeline_mode=pl.Buffered(1))


def _w512(w_ref, j):
    return w_ref[j // 2, :, 512 * (j % 2):512 * (j % 2 + 1)]


def _proj(h, w):
    S = h.shape[0]
    tm = 256

    def body(a_ref, w_ref, q_ref, r_ref):
        a = a_ref[...]
        for j in range(9):
            q_ref[:, 512 * j:512 * (j + 1)] = _dot(a, _w512(w_ref, j), NN).astype(q_ref.dtype)
        off = 0
        for lo, hi in REST_COLS:
            for j in range(lo, hi):
                r_ref[:, off:off + 512] = _dot(a, _w512(w_ref, j), NN)
                off += 512

    return pl.pallas_call(
        body, name="proj", grid=(S // tm,),
        in_specs=[pl.BlockSpec((tm, D), lambda i: (i, 0)), _resident((8, D, 1024))],
        out_specs=[pl.BlockSpec((tm, QKV_W), lambda i: (i, 0)), pl.BlockSpec((tm, REST_W), lambda i: (i, 0))],
        out_shape=[_sds((S, QKV_W), CDT), _sds((S, REST_W), F32)],
        compiler_params=_cp(("parallel",), vmem=56),
    )(h, w)


def _after(dep):
    return ([], []) if dep is None else ([pl.BlockSpec(memory_space=pl.ANY)], [dep])


def _dh(dqkv, drest, w, dep=None):
    S = dqkv.shape[0]
    tm = 256
    dep_specs, dep_args = _after(dep)

    def body(a_ref, b_ref, w_ref, *rest):
        o_ref = rest[-1]
        acc = _dot(a_ref[:, 0:512], _w512(w_ref, 0), NT)
        for j in range(1, 9):
            acc = acc + _dot(a_ref[:, 512 * j:512 * (j + 1)], _w512(w_ref, j), NT)
        off = 0
        for lo, hi in REST_COLS:
            for j in range(lo, hi):
                acc = acc + _dot(b_ref[:, off:off + 512], _w512(w_ref, j), NT)
                off += 512
        o_ref[...] = acc

    return pl.pallas_call(
        body, name="dh", grid=(S // tm,),
        in_specs=[pl.BlockSpec((tm, QKV_W), lambda i: (i, 0)), pl.BlockSpec((tm, REST_W), lambda i: (i, 0)),
                  _resident((8, D, 1024))] + dep_specs,
        out_specs=pl.BlockSpec((tm, D), lambda i: (i, 0)),
        out_shape=_sds((S, D), F32),
        compiler_params=_cp(("parallel",), vmem=56),
    )(dqkv, drest, w, *dep_args)


def _mm_tn(at, b, *, n_total, out_blk, name, prev=None, tn=512):
    R, S = at.shape
    nb = b.shape[1] // tn

    def body(*refs):
        refs[-1][...] = _dot(refs[0][...], refs[1][...], NN)

    in_specs = [_resident((R, S)), pl.BlockSpec((S, tn), lambda j: (0, j))]
    args = [at, b]
    aliases = {}
    if prev is not None:
        in_specs.append(pl.BlockSpec(memory_space=pl.ANY))
        args.append(prev)
        aliases = {2: 0}
    return pl.pallas_call(
        body, name=name, grid=(nb,), in_specs=in_specs,
        out_specs=pl.BlockSpec((R, tn), lambda j: (0, out_blk(j))),
        out_shape=_sds((R, n_total), F32),
        input_output_aliases=aliases,
        compiler_params=_cp(("parallel",)),
    )(*args)


def _tcast(a, name):
    M, C = a.shape
    tm = 512

    def body(a_ref, o_ref):
        o_ref[...] = a_ref[...].astype(F32).T.astype(o_ref.dtype)

    return pl.pallas_call(
        body, name=name, grid=(M // tm,),
        in_specs=[pl.BlockSpec((tm, C), lambda i: (i, 0))],
        out_specs=pl.BlockSpec((C, tm), lambda i: (0, i)),
        out_shape=_sds((C, M), CDT), compiler_params=_cp(("parallel",)),
    )(a)


TOK = R16 * R16


def _swap16():
    i = lax.broadcasted_iota(jnp.int32, (TOK, TOK), 0)
    j = lax.broadcasted_iota(jnp.int32, (TOK, TOK), 1)
    return jnp.where(j == jnp.bitwise_and(i, 15) * 16 + jnp.right_shift(i, 4), 1.0, 0.0).astype(CDT)


def _norm_mod(x, norm_g, shift, scale, dep=None):
    S = x.shape[0]
    dep_specs, dep_args = _after(dep)

    def body(x_ref, g_ref, sh_ref, sc_ref, *rest):
        xv = x_ref[...]
        xn = xv * lax.rsqrt(jnp.mean(xv * xv, axis=-1, keepdims=True) + EPS)
        h = (xn * (g_ref[...] * (1.0 + sc_ref[...])) + sh_ref[...]).astype(CDT)
        _st(rest[-1], _dot(_swap16(), h, NN))

    vec = pl.BlockSpec((1, D), lambda i: (0, 0))
    return pl.pallas_call(
        body, name="norm_mod", grid=(S // TOK,),
        in_specs=[pl.BlockSpec((TOK, D), lambda i: (i, 0)), vec, vec, vec] + dep_specs,
        out_specs=pl.BlockSpec((R16, R16, D), lambda i: (0, i, 0)),
        out_shape=_sds((R16, S // R16, D), CDT), compiler_params=_cp(("parallel",)),
    )(x, norm_g, shift, scale, *dep_args)


def _t5_bucket_np(n):
    nf = np.maximum(n, 1).astype(np.float32)
    large = 16 + (np.log(nf / np.float32(16)) / np.float32(math.log(2048 / 16)) * np.float32(16)).astype(np.int32)
    large = np.minimum(large, 31)
    return np.where(n < 16, n, large).astype(np.int32)


class _Geom:
    def __init__(self, gi, LS):
        self.gi, self.LS = gi, LS
        self.dil = (1, 4, 16)[gi]
        if gi == 0:
            self.lead, self.rows, self.n_tiles, self.has_prev = (R16,), 16, LS // 16, True
            a = np.arange(R16)[:, None]
            ll = np.arange(16)[None, :]
            self.pos = (16 * ll + a).reshape(-1)
        elif gi == 1:
            self.lead, self.rows, self.n_tiles, self.has_prev = (4,), 64, LS // 64, True
            a = np.arange(4)[:, None]
            ll = np.arange(64)[None, :]
            self.pos = (4 * ll + a).reshape(-1)
        else:
            self.lead, self.rows, self.n_tiles, self.has_prev = (), LS, 1, False
            self.pos = np.arange(LS)
        self.T = self.pos.shape[0]
        self.Tk = 2 * self.T if self.has_prev else self.T
        self.outer = {0: (), 1: (4,), 2: (R16,)}[gi]

    def view(self, arr):
        S, C = arr.shape
        if self.gi == 1:
            return arr.reshape(4, 4, self.LS, C)
        return arr.reshape(R16, self.LS, C)

    def view_shape(self, C):
        return (4, 4, self.LS, C) if self.gi == 1 else (R16, self.LS, C)

    def unview(self, arr):
        return arr.reshape(R16 * self.LS, arr.shape[-1])

    def spec(self, W, cblk, tile):
        if self.gi == 0:
            return pl.BlockSpec((R16, 16, W), lambda n: (0, tile(n), cblk))
        if self.gi == 1:
            return pl.BlockSpec((4, None, 64, W), lambda r, n: (0, r, tile(n), cblk))
        return pl.BlockSpec((None, self.LS, W), lambda r: (r, 0, cblk))

    def halves(self):
        R = self.rows
        idx = np.arange(self.T).reshape(-1, R)
        return idx[:, :R // 2].reshape(-1), idx[:, R // 2:].reshape(-1)

    def bucket_index(self, sub):
        lo, hi = self.halves()
        pos = self.pos
        if sub == 0:
            pq = pos[lo]
            pk = np.concatenate([pos[hi] - self.T, pos[lo]]) if self.has_prev else np.concatenate([pos[lo], pos[hi]])
        else:
            pq = pos[hi]
            pk = np.concatenate([pos[lo], pos[hi]])
        dist = pq[:, None] - pk[None, :]
        ok = (dist >= 0) & (dist <= N_BACK)
        bucket = _t5_bucket_np(np.clip(dist, 0, N_BACK) * self.dil)
        return np.where(ok, bucket, -1).astype(np.int32)


def _bias_table(idx, buckets, rel_bias, gi, sub, dep=None):
    T, Tk = idx.shape
    tr = 32
    dep_specs, dep_args = _after(dep)

    def body(idx_ref, rb_ref, *rest):
        h = pl.program_id(0)
        ix = idx_ref[...]
        acc = jnp.full(ix.shape, NEG, F32)
        for b in buckets:
            acc = jnp.where(ix == b, rb_ref[b, gi * NH + h], acc)
        rest[-1][...] = acc

    return pl.pallas_call(
        body, name=f"bias_table{gi}{'ab'[sub]}", grid=(NH, T // tr),
        in_specs=[pl.BlockSpec((tr, Tk), lambda h, i: (i, 0)), pl.BlockSpec(memory_space=pltpu.SMEM)] + dep_specs,
        out_specs=pl.BlockSpec((None, tr, Tk), lambda h, i: (h, i, 0)),
        out_shape=_sds((NH, T, Tk), F32), compiler_params=_cp(("parallel", "parallel")),
    )(idx, rel_bias, *dep_args)


def _bias_grad(dsum, idx, buckets, gi, sub, dep=None):
    _, T, Tk = dsum.shape
    dep_specs, dep_args = _after(dep)

    def body(ds_ref, idx_ref, *rest):
        o_ref, r_ref = rest[-2], rest[-1]
        r_ref[...] = jnp.zeros_like(r_ref)
        ix, ds = idx_ref[...], ds_ref[...]
        for b in buckets:
            r_ref[b:b + 1, :] = jnp.sum(jnp.where(ix == b, ds, 0.0), axis=0, keepdims=True)
        o_ref[...] = jnp.broadcast_to(jnp.sum(r_ref[...], axis=1, keepdims=True), (32, 128))

    out = pl.pallas_call(
        body, name=f"bias_grad{gi}{'ab'[sub]}", grid=(NH,),
        in_specs=[pl.BlockSpec((None, T, Tk), lambda h: (h, 0, 0)), pl.BlockSpec((T, Tk), lambda h: (0, 0))] + dep_specs,
        out_specs=pl.BlockSpec((None, 32, 128), lambda h: (h, 0, 0)),
        out_shape=_sds((NH, 32, 128), F32), scratch_shapes=[pltpu.VMEM((32, Tk), F32)],
        compiler_params=_cp(("parallel",)),
    )(dsum, idx, *dep_args)
    return out[:, :, 0].T


def _prev_mask(n, T, Tk):
    col = lax.broadcasted_iota(jnp.int32, (1, Tk), 1)
    return jnp.where((col < T) & (n == 0), NEG, 0.0).astype(F32)


def _split_rows(v, R):
    A, half = v.shape[0] // R, R // 2
    via_f32 = v.dtype != F32 and half % 16 != 0
    w = v.astype(F32) if via_f32 else v
    lo = jnp.concatenate([w[a * R:a * R + half] for a in range(A)], axis=0)
    hi = jnp.concatenate([w[a * R + half:(a + 1) * R] for a in range(A)], axis=0)
    return (lo.astype(v.dtype), hi.astype(v.dtype)) if via_f32 else (lo, hi)


def _merge_rows(lo, hi, R):
    half = R // 2
    A = lo.shape[0] // half
    return jnp.concatenate([x[a * half:(a + 1) * half] for a in range(A) for x in (lo, hi)], axis=0)


def _attn_fwd(qkv, biases, geom):
    S = qkv.shape[0]
    gi, T, R, has_prev = geom.gi, geom.T, geom.rows, geom.has_prev
    H2 = T // 2
    qv = geom.view(qkv)
    cq, ck, cv = 3 * gi, 3 * gi + 1, 3 * gi + 2

    def body(*refs):
        if has_prev:
            q_ref, kp_ref, kc_ref, vp_ref, vc_ref, ba_ref, bb_ref, o_ref, l_ref = refs
        else:
            q_ref, kc_ref, vc_ref, ba_ref, bb_ref, o_ref, l_ref = refs
        q_lo, q_hi = _split_rows(_ld(q_ref), R)
        kc_lo, kc_hi = _split_rows(_ld(kc_ref), R)
        vc_lo, vc_hi = _split_rows(_ld(vc_ref), R)
        k_b = jnp.concatenate([kc_lo, kc_hi], axis=0)
        v_b = jnp.concatenate([vc_lo, vc_hi], axis=0)
        if has_prev:
            n = pl.program_id(len(geom.outer))
            k_a = jnp.concatenate([_split_rows(_ld(kp_ref), R)[1], kc_lo], axis=0)
            v_a = jnp.concatenate([_split_rows(_ld(vp_ref), R)[1], vc_lo], axis=0)
            pm = _prev_mask(n, H2, T)
        else:
            k_a, v_a, pm = k_b, v_b, None
        low = lax.broadcasted_iota(jnp.int32, (1, 2 * HD), 1) < HD
        for j in range(NH // 2):
            cols = slice(2 * HD * j, 2 * HD * (j + 1))
            o_half, l_half = [], []
            for q, k, v, b_ref, mask in ((q_lo, k_a, v_a, ba_ref, pm), (q_hi, k_b, v_b, bb_ref, None)):
                q2 = q[:, cols].astype(F32) * 0.125
                k2, v2 = k[:, cols], v[:, cols]
                qs = jnp.concatenate([jnp.where(low, q2, 0.0), jnp.where(low, 0.0, q2)], axis=0).astype(q.dtype)
                s = _dot(qs, k2, NT) + b_ref[j]
                if mask is not None:
                    s = s + mask
                m = jnp.max(s, axis=1, keepdims=True)
                p = jnp.exp(s - m)
                l = jnp.sum(p, axis=1, keepdims=True)
                pv = _dot(p.astype(v.dtype), v2, NN) / l
                lse = m + jnp.log(l)
                o_half.append(jnp.where(low, pv[:H2], pv[H2:]))
                l_half.append(jnp.where(low, lse[:H2], lse[H2:]))
            _st(o_ref, _merge_rows(o_half[0], o_half[1], R), cols)
            _st(l_ref, _merge_rows(l_half[0], l_half[1], R), cols)

    cur = lambda n: n
    prev = lambda n: jnp.maximum(n - 1, 0)
    if has_prev:
        in_specs = [geom.spec(AW, cq, cur), geom.spec(AW, ck, prev), geom.spec(AW, ck, cur),
                    geom.spec(AW, cv, prev), geom.spec(AW, cv, cur)]
        args = [qv] * 5
    else:
        in_specs = [geom.spec(AW, cq, cur), geom.spec(AW, ck, cur), geom.spec(AW, cv, cur)]
        args = [qv] * 3
    nd = len(geom.outer) + (1 if has_prev else 0)
    biases = [b.reshape(NH // 2, 2 * H2, b.shape[2]) for b in biases]
    in_specs += [pl.BlockSpec(b.shape, lambda *ids: (0, 0, 0)) for b in biases]
    grid = geom.outer + ((geom.n_tiles,) if has_prev else ())
    oshape = geom.view_shape(AW)
    o, l = pl.pallas_call(
        body, name=f"attn_fwd{gi}", grid=grid, in_specs=in_specs,
        out_specs=[geom.spec(AW, 0, cur), geom.spec(AW, 0, cur)],
        out_shape=[_sds(oshape, F32), _sds(oshape, F32)],
        compiler_params=_cp(("arbitrary",) * nd),
    )(*args, *biases)
    return geom.unview(o), geom.unview(l)


def _attn_combine(os_, ls_, rest):
    S = rest.shape[0]
    tm = 512

    def body(o0, o1, o2, l0, l1, l2, z_ref, attn_ref, L_ref, ag_ref, agt_ref):
        la, lb, lc = l0[...], l1[...], l2[...]
        m = jnp.maximum(jnp.maximum(la, lb), lc)
        ea, eb, ec = jnp.exp(la - m), jnp.exp(lb - m), jnp.exp(lc - m)
        den = ea + eb + ec
        attn = (ea * o0[...] + eb * o1[...] + ec * o2[...]) / den
        attn_ref[...] = attn
        L_ref[...] = m + jnp.log(den)
        z = z_ref[...]
        ag = attn * (z * _sigmoid(z))
        ag_ref[...] = ag.astype(ag_ref.dtype)
        agt_ref[...] = ag.T.astype(agt_ref.dtype)

    t = pl.BlockSpec((tm, AW), lambda i: (i, 0))
    return pl.pallas_call(
        body, name="attn_combine", grid=(S // tm,),
        in_specs=[t] * 6 + [pl.BlockSpec((tm, AW), lambda i: (i, 0))],
        out_specs=[t, t, t, pl.BlockSpec((AW, tm), lambda i: (0, i))],
        out_shape=[_sds((S, AW), F32), _sds((S, AW), F32), _sds((S, AW), CDT), _sds((AW, S), CDT)],
        compiler_params=_cp(("parallel",)),
    )(*os_, *ls_, rest)


def _pool_counts(r, LS, win):
    l = lax.broadcasted_iota(jnp.int32, (LS, 1), 0)
    return jnp.minimum(16 * l + (r + 1), win).astype(F32)


def _pool_fwd(rest, pool_w, pool_scale):
    S = rest.shape[0]
    LS = S // R16
    r3 = rest.reshape(R16, LS, REST_W)
    PG = 128

    def body(u_ref, z_ref, pw_ref, ps_ref, pt_ref, mp_ref, pg_ref, pgt_ref, e_ref):
        g = pl.program_id(0)
        win = jnp.left_shift(2, g)
        row = lax.broadcasted_iota(jnp.int32, (LS, PG), 0)
        for r in range(R16):
            e_ref[R16 + r] = u_ref[r]
        for r in range(1, R16):
            e_ref[r] = jnp.where(row == 0, 0.0, pltpu.roll(u_ref[r], 1, 0))

        def double(s):
            for i in range(2 * R16 - 1, 2 * s - 1, -1):
                e_ref[i] = e_ref[i] + e_ref[i - s]

        double(1)
        for k in (1, 2, 3):
            pl.when(g >= k)(functools.partial(double, 2 ** k))
        pw = pw_ref[...].astype(CDT)
        for r in range(R16):
            u = u_ref[r]
            pooled = e_ref[R16 + r] / _pool_counts(r, LS, win) - u
            pc = pooled.astype(CDT)
            mp = _dot(pc, pw, NN)
            z = z_ref[r]
            pg = (mp * ps_ref[...]) * (z * _sigmoid(z))
            mp_ref[r] = mp
            pg_ref[r] = pg.astype(pg_ref.dtype)
            pt_ref[:, r * LS:(r + 1) * LS] = pooled.T.astype(pt_ref.dtype)
            pgt_ref[:, r * LS:(r + 1) * LS] = pg.T.astype(pgt_ref.dtype)

    col = lambda off: pl.BlockSpec((R16, LS, PG), lambda g: (0, 0, off + g))
    tsp = pl.BlockSpec((PG, S), lambda g: (g, 0))
    pt, mp, pg, pgt = pl.pallas_call(
        body, name="pool_fwd", grid=(4,),
        in_specs=[col(24), col(4), pl.BlockSpec((None, PG, PG), lambda g: (g, 0, 0)),
                  pl.BlockSpec((1, PG), lambda g: (0, g))],
        out_specs=[tsp, col(0), col(0), tsp],
        out_shape=[_sds((AW, S), CDT), _sds((R16, LS, AW), F32), _sds((R16, LS, AW), CDT), _sds((AW, S), CDT)],
        scratch_shapes=[pltpu.VMEM((2 * R16, LS, PG), F32)],
        compiler_params=_cp(("parallel",)),
    )(r3, r3, pool_w, pool_scale)
    return pt, mp.reshape(S, AW), pg.reshape(S, AW), pgt


def _branch_merge(ag, pg, wab, wpb, rest):
    S = ag.shape[0]
    tm, tn = 512, 512

    def body(ag_ref, pg_ref, wa_ref, wp_ref, ga_ref, gp_ref, m_ref, mt_ref):
        ya = _dot(ag_ref[...], wa_ref[...], NN)
        yp = _dot(pg_ref[...], wp_ref[...], NN)
        mg = _sigmoid(ga_ref[...]) * ya + _sigmoid(gp_ref[...]) * yp
        m_ref[...] = mg.astype(m_ref.dtype)
        mt_ref[...] = mg.T.astype(mt_ref.dtype)

    a = pl.BlockSpec((tm, AW), lambda i, j: (i, 0))
    w = pl.BlockSpec((AW, tn), lambda i, j: (0, j))
    o = pl.BlockSpec((tm, tn), lambda i, j: (i, j))
    return pl.pallas_call(
        body, name="branch_merge", grid=(S // tm, D // tn),
        in_specs=[a, a, w, w, pl.BlockSpec((tm, tn), lambda i, j: (i, 2 + j)),
                  pl.BlockSpec((tm, tn), lambda i, j: (i, 4 + j))],
        out_specs=[o, pl.BlockSpec((tn, tm), lambda i, j: (j, i))],
        out_shape=[_sds((S, D), CDT), _sds((D, S), CDT)],
        compiler_params=_cp(("parallel", "parallel")),
    )(ag, pg, wab, wpb, rest, rest)


def _out_loss(merged, wout, x, target, gate, final_g):
    S = x.shape[0]

    def body(m_ref, w_ref, x_ref, t_ref, gate_ref, fg_ref, dx2_ref, dmo_ref, loss_ref, gfg_ref, dgate_ref):
        i = pl.program_id(0)

        @pl.when(i == 0)
        def _():
            loss_ref[...] = jnp.zeros_like(loss_ref)
            gfg_ref[...] = jnp.zeros_like(gfg_ref)
            dgate_ref[...] = jnp.zeros_like(dgate_ref)

        swap = _swap16()
        mo = _dot(_dot(swap, _ld(m_ref), NN).astype(CDT), w_ref[...], NN)
        gate, fg = gate_ref[...], fg_ref[...]
        x2 = x_ref[...] + gate * mo
        r2 = lax.rsqrt(jnp.mean(x2 * x2, axis=-1, keepdims=True) + EPS)
        xn2 = x2 * r2
        diff = xn2 * fg - t_ref[...]
        loss = jnp.sum(jnp.sum(diff * diff, axis=-1, keepdims=True), axis=0, keepdims=True)
        dy = diff * (1.0 / D)
        dxn2 = dy * fg
        dx2 = r2 * (dxn2 - xn2 * jnp.mean(dxn2 * xn2, axis=-1, keepdims=True))
        dx2_ref[...] = dx2
        _st(dmo_ref, _dot(swap, (dx2 * gate).astype(CDT), NN))
        loss_ref[...] += jnp.broadcast_to(loss * (0.5 / D), loss_ref.shape)
        gfg_ref[...] += jnp.sum(dy * xn2, axis=0, keepdims=True)
        dgate_ref[...] += jnp.sum(dx2 * mo, axis=0, keepdims=True)

    slab = pl.BlockSpec((R16, R16, D), lambda i: (0, i, 0))
    nat = pl.BlockSpec((TOK, D), lambda i: (i, 0))
    vec = pl.BlockSpec((1, D), lambda i: (0, 0))
    return pl.pallas_call(
        body, name="out_loss", grid=(S // TOK,),
        in_specs=[slab, _resident((D, D)), nat, nat, vec, vec],
        out_specs=[nat, slab, pl.BlockSpec((1, 128), lambda i: (0, 0)), vec, vec],
        out_shape=[_sds((S, D), F32), _sds((R16, S // R16, D), CDT), _sds((1, 128), F32), _sds((1, D), F32),
                   _sds((1, D), F32)],
        compiler_params=_cp(("arbitrary",)),
    )(merged, wout, x, target, gate, final_g)


def _bwd_gates(dmo, wout, wab, wpb, rest, ag, pg, attn, mp, pool_scale):
    S = dmo.shape[0]
    tm = 256

    def body(dmo_ref, wo_ref, wa_ref, wp_ref, za_ref, zp_ref, ga0, ga1, gp0, gp1, ag_ref, pg_ref, at_ref, mp_ref,
             ps_ref, dya_ref, dyp_ref, dr_ref, dat_ref, dl_ref, dmp_ref, dps_ref):
        i = pl.program_id(0)

        @pl.when(i == 0)
        def _():
            dps_ref[...] = jnp.zeros_like(dps_ref)

        dm = _dot(dmo_ref[...], wo_ref[...], NT)
        sa = _sigmoid(jnp.concatenate([ga0[...], ga1[...]], axis=1))
        sp = _sigmoid(jnp.concatenate([gp0[...], gp1[...]], axis=1))
        dya = (dm * sa).astype(CDT)
        dyp = (dm * sp).astype(CDT)
        dya_ref[...] = dya
        dyp_ref[...] = dyp
        ya = _dot(ag_ref[...], wa_ref[...], NN)
        yp = _dot(pg_ref[...], wp_ref[...], NN)
        dr_ref[:, 1024:2048] = (dm * ya * sa * (1.0 - sa)).astype(dr_ref.dtype)
        dr_ref[:, 2048:3072] = (dm * yp * sp * (1.0 - sp)).astype(dr_ref.dtype)
        da = _dot(dya, wa_ref[...], NT)
        dp = _dot(dyp, wp_ref[...], NT)
        za = za_ref[...]
        sga = _sigmoid(za)
        attn = at_ref[...]
        dattn = da * (za * sga)
        dr_ref[:, 0:512] = (da * attn * (sga * (1.0 + za * (1.0 - sga)))).astype(dr_ref.dtype)
        dat_ref[...] = dattn.astype(dat_ref.dtype)
        prod = dattn * attn
        for h in range(NH):
            sl = slice(HD * h, HD * h + HD)
            dl_ref[:, sl] = jnp.broadcast_to(jnp.sum(prod[:, sl], axis=1, keepdims=True), (tm, HD))
        zp = zp_ref[...]
        sgp = _sigmoid(zp)
        mpre = mp_ref[...]
        ps = ps_ref[...]
        dmixed = dp * (zp * sgp)
        dr_ref[:, 512:1024] = (dp * (mpre * ps) * (sgp * (1.0 + zp * (1.0 - sgp)))).astype(dr_ref.dtype)
        dps_ref[...] += jnp.sum(dmixed * mpre, axis=0, keepdims=True)
        dmp_ref[...] = (dmixed * ps).astype(dmp_ref.dtype)

    full = lambda shape: pl.BlockSpec(shape, lambda i: (0, 0))
    tD = pl.BlockSpec((tm, D), lambda i: (i, 0))
    tA = pl.BlockSpec((tm, AW), lambda i: (i, 0))
    rb = lambda c: pl.BlockSpec((tm, AW), lambda i: (i, c))
    return pl.pallas_call(
        body, name="bwd_gates", grid=(S // tm,),
        in_specs=[tD, full((D, D)), full((AW, D)), full((AW, D)), rb(0), rb(1), rb(2), rb(3), rb(4), rb(5),
                  tA, tA, tA, tA, full((1, AW))],
        out_specs=[tD, tD, pl.BlockSpec((tm, 3072), lambda i: (i, 0)), tA, tA, tA, full((1, AW))],
        out_shape=[_sds((S, D), CDT), _sds((S, D), CDT), _sds((S, REST_W), CDT), _sds((S, AW), CDT),
                   _sds((S, AW), F32), _sds((S, AW), CDT), _sds((1, AW), F32)],
        compiler_params=_cp(("arbitrary",), vmem=56),
    )(dmo, wout, wab, wpb, rest, rest, rest, rest, rest, rest, ag, pg, attn, mp, pool_scale)


def _pool_bwd(dmp, pt, pool_w, drest):
    S = dmp.shape[0]
    LS = S // R16
    PG = 128
    d3 = dmp.reshape(R16, LS, AW)
    dr3 = drest.reshape(R16, LS, REST_W)

    def body(d_ref, pt_ref, pw_ref, dr_in, du_ref, dpw_ref, g_ref, s_ref):
        del dr_in
        g = pl.program_id(0)
        win = jnp.left_shift(2, g)
        row = lax.broadcasted_iota(jnp.int32, (LS, PG), 0)
        pw = pw_ref[...].astype(CDT)
        dpw = jnp.zeros((PG, PG), F32)
        for r in range(R16):
            dm = d_ref[r]
            dpw = dpw + _dot(pt_ref[:, r * LS:(r + 1) * LS], dm, NN)
            dpooled = _dot(dm, pw, NT)
            g_ref[r] = dpooled
            sc = dpooled / _pool_counts(r, LS, win)
            s_ref[r] = sc
            if r < R16 - 1:
                s_ref[R16 + r] = jnp.where(row == LS - 1, 0.0, pltpu.roll(sc, LS - 1, 0))
        dpw_ref[...] = dpw

        def double(s):
            for i in range(0, 2 * R16 - 2 * s):
                s_ref[i] = s_ref[i] + s_ref[i + s]

        double(1)
        for k in (1, 2, 3):
            pl.when(g >= k)(functools.partial(double, 2 ** k))
        for r in range(R16):
            du_ref[r] = (s_ref[r] - g_ref[r]).astype(du_ref.dtype)

    du, dpw = pl.pallas_call(
        body, name="pool_bwd", grid=(4,),
        in_specs=[pl.BlockSpec((R16, LS, PG), lambda g: (0, 0, g)), pl.BlockSpec((PG, S), lambda g: (g, 0)),
                  pl.BlockSpec((None, PG, PG), lambda g: (g, 0, 0)), pl.BlockSpec(memory_space=pl.ANY)],
        out_specs=[pl.BlockSpec((R16, LS, PG), lambda g: (0, 0, 24 + g)),
                   pl.BlockSpec((None, PG, PG), lambda g: (g, 0, 0))],
        out_shape=[_sds((R16, LS, REST_W), CDT), _sds((4, PG, PG), F32)],
        scratch_shapes=[pltpu.VMEM((R16, LS, PG), F32), pltpu.VMEM((2 * R16, LS, PG), F32)],
        input_output_aliases={3: 0},
        compiler_params=_cp(("parallel",)),
    )(d3, pt, pool_w, dr3)
    return du.reshape(S, REST_W), dpw


def _attn_bwd(qkv, dattn, Lb, dlb, biases, geom, dqkv_prev):
    S = qkv.shape[0]
    gi, T, R, has_prev, N = geom.gi, geom.T, geom.rows, geom.has_prev, geom.n_tiles
    H2 = T // 2
    nd_outer = len(geom.outer)
    qv = geom.view(qkv)
    cq, ck, cv = 3 * gi, 3 * gi + 1, 3 * gi + 2
    W3 = 3 * AW

    def heads(q_ref, kp_ref, kc_ref, vp_ref, vc_ref, do_ref, L_ref, dl_ref, ba_ref, bb_ref, pm, dsa_ref, dsb_ref, emit):
        q_lo, q_hi = _split_rows(_ld(q_ref), R)
        do_lo, do_hi = _split_rows(_ld(do_ref), R)
        L_lo, L_hi = _split_rows(_ld(L_ref), R)
        dl_lo, dl_hi = _split_rows(_ld(dl_ref), R)
        kc_lo, kc_hi = _split_rows(_ld(kc_ref), R)
        vc_lo, vc_hi = _split_rows(_ld(vc_ref), R)
        k_b = jnp.concatenate([kc_lo, kc_hi], axis=0)
        v_b = jnp.concatenate([vc_lo, vc_hi], axis=0)
        if has_prev:
            k_a = jnp.concatenate([_split_rows(_ld(kp_ref), R)[1], kc_lo], axis=0)
            v_a = jnp.concatenate([_split_rows(_ld(vp_ref), R)[1], vc_lo], axis=0)
        else:
            k_a, v_a = k_b, v_b
        low = lax.broadcasted_iota(jnp.int32, (1, 2 * HD), 1) < HD
        for j in range(NH // 2):
            sl = slice(2 * HD * j, 2 * HD * (j + 1))
            res = []
            for q, k, v, do, L, dl, b_ref, mask, ds_ref in (
                    (q_lo, k_a, v_a, do_lo, L_lo, dl_lo, ba_ref, pm, dsa_ref),
                    (q_hi, k_b, v_b, do_hi, L_hi, dl_hi, bb_ref, None, dsb_ref)):
                q2 = q[:, sl].astype(F32) * 0.125
                do2 = do[:, sl].astype(F32)
                k2, v2 = k[:, sl], v[:, sl]
                L2, dl2 = L[:, sl], dl[:, sl]
                qs = jnp.concatenate([jnp.where(low, q2, 0.0), jnp.where(low, 0.0, q2)], axis=0).astype(q.dtype)
                dom = jnp.concatenate([jnp.where(low, do2, 0.0), jnp.where(low, 0.0, do2)], axis=0).astype(q.dtype)
                Lc = jnp.concatenate([L2[:, 0:1], L2[:, HD:HD + 1]], axis=0)
                dlc = jnp.concatenate([dl2[:, 0:1], dl2[:, HD:HD + 1]], axis=0)
                s = _dot(qs, k2, NT) + b_ref[j]
                if mask is not None:
                    s = s + mask
                p = jnp.exp(s - Lc)
                ds = p * (_dot(dom, v2, NT) - dlc)
                ds_ref[j] += ds
                dsc = ds.astype(q.dtype)
                dq_st = _dot(dsc, k2, NN)
                res.append((jnp.where(low, dq_st[:H2], dq_st[H2:]) * 0.125, _dot(dsc, qs, TN),
                            _dot(p.astype(q.dtype), dom, TN)))
            (dq_lo, dk_a, dv_a), (dq_hi, dk_b, dv_b) = res
            dq = _merge_rows(dq_lo, dq_hi, R)
            if has_prev:
                emit(sl, dq, _merge_rows(dk_a[H2:] + dk_b[:H2], dk_b[H2:], R),
                     _merge_rows(dv_a[H2:] + dv_b[:H2], dv_b[H2:], R), dk_a[:H2], dv_a[:H2])
            else:
                emit(sl, dq, _merge_rows(dk_a[:H2] + dk_b[:H2], dk_a[H2:] + dk_b[H2:], R),
                     _merge_rows(dv_a[:H2] + dv_b[:H2], dv_a[H2:] + dv_b[H2:], R), None, None)

    def first_step():
        ok = pl.program_id(nd_outer) == 0 if has_prev else pl.program_id(0) == 0
        for a in range(nd_outer if has_prev else 0):
            ok = ok & (pl.program_id(a) == 0)
        return ok

    if has_prev:
        def body(q_ref, kp_ref, kc_ref, vp_ref, vc_ref, do_ref, L_ref, dl_ref, ba_ref, bb_ref, prev_in, out_ref,
                 dsa_ref, dsb_ref, hold, new, pp):
            del prev_in
            s_id = pl.program_id(nd_outer)

            @pl.when(first_step())
            def _():
                dsa_ref[...] = jnp.zeros_like(dsa_ref)
                dsb_ref[...] = jnp.zeros_like(dsb_ref)

            @pl.when(s_id < N)
            def _():
                def emit(sl, dq, dk, dv, dk_prev, dv_prev):
                    new[:, sl] = dq
                    new[:, slice(AW + sl.start, AW + sl.stop)] = dk
                    new[:, slice(2 * AW + sl.start, 2 * AW + sl.stop)] = dv
                    pp[:, sl] = dk_prev
                    pp[:, slice(AW + sl.start, AW + sl.stop)] = dv_prev

                heads(q_ref, kp_ref, kc_ref, vp_ref, vc_ref, do_ref, L_ref, dl_ref, ba_ref, bb_ref,
                      _prev_mask(s_id, H2, T), dsa_ref, dsb_ref, emit)

            @pl.when((s_id >= 1) & (s_id < N))
            def _():
                _st(out_ref, hold[:, 0:AW], slice(0, AW))
                late = _merge_rows(jnp.zeros((H2, 2 * AW), F32), pp[...], R)
                _st(out_ref, hold[:, AW:W3] + late, slice(AW, W3))

            @pl.when(s_id == N)
            def _():
                _st(out_ref, hold[...])

            @pl.when(s_id < N)
            def _():
                hold[...] = new[...]

        cur = lambda n: jnp.minimum(n, N - 1)
        prev = lambda n: jnp.clip(n - 1, 0, N - 1)
        delayed = lambda n: jnp.maximum(n - 1, 0)
        in_specs = [geom.spec(AW, cq, cur), geom.spec(AW, ck, prev), geom.spec(AW, ck, cur),
                    geom.spec(AW, cv, prev), geom.spec(AW, cv, cur),
                    geom.spec(AW, 0, cur), geom.spec(AW, 0, cur), geom.spec(AW, 0, cur)]
        args = [qv] * 5 + [geom.view(dattn), geom.view(Lb), geom.view(dlb)]
        out_spec = geom.spec(W3, gi, delayed)
        grid = geom.outer + (N + 1,)
        scratch = [pltpu.VMEM((T, W3), F32), pltpu.VMEM((T, W3), F32), pltpu.VMEM((H2, 2 * AW), F32)]
    else:
        def body(q_ref, kc_ref, vc_ref, do_ref, L_ref, dl_ref, ba_ref, bb_ref, prev_in, out_ref, dsa_ref, dsb_ref):
            del prev_in

            @pl.when(first_step())
            def _():
                dsa_ref[...] = jnp.zeros_like(dsa_ref)
                dsb_ref[...] = jnp.zeros_like(dsb_ref)

            def emit(sl, dq, dk, dv, dk_prev, dv_prev):
                _st(out_ref, dq, sl)
                _st(out_ref, dk, slice(AW + sl.start, AW + sl.stop))
                _st(out_ref, dv, slice(2 * AW + sl.start, 2 * AW + sl.stop))

            heads(q_ref, None, kc_ref, None, vc_ref, do_ref, L_ref, dl_ref, ba_ref, bb_ref, None, dsa_ref, dsb_ref,
                  emit)

        cur = lambda n: n
        in_specs = [geom.spec(AW, cq, cur), geom.spec(AW, ck, cur), geom.spec(AW, cv, cur),
                    geom.spec(AW, 0, cur), geom.spec(AW, 0, cur), geom.spec(AW, 0, cur)]
        args = [qv] * 3 + [geom.view(dattn), geom.view(Lb), geom.view(dlb)]
        out_spec = geom.spec(W3, gi, cur)
        grid = geom.outer
        scratch = []

    whole = lambda b: pl.BlockSpec(b.shape, lambda *ids: (0, 0, 0))
    biases = [b.reshape(NH // 2, 2 * H2, b.shape[2]) for b in biases]
    in_specs += [whole(b) for b in biases]
    in_specs.append(pl.BlockSpec(memory_space=pl.ANY))
    if dqkv_prev is None:
        last, aliases = biases[0], {}
    else:
        last, aliases = geom.view(dqkv_prev), {len(in_specs) - 1: 0}
    dq, dsa, dsb = pl.pallas_call(
        body, name=f"attn_bwd{gi}", grid=grid, in_specs=in_specs,
        out_specs=[out_spec] + [whole(b) for b in biases],
        out_shape=[_sds(geom.view_shape(QKV_W), CDT)] + [_sds(b.shape, F32) for b in biases],
        scratch_shapes=scratch,
        input_output_aliases=aliases,
        compiler_params=_cp(("arbitrary",) * len(grid), vmem=56),
    )(*args, *biases, last)
    return geom.unview(dq), [d.reshape(NH, H2, d.shape[2]) for d in (dsa, dsb)]


def _h_bwd(dh, dx2, x, norm_g, scale):
    S = x.shape[0]

    def body(dh_ref, dx2_ref, x_ref, g_ref, sc_ref, gx_ref, dsh_ref, dsc_ref, dng_ref, nat_ref):
        i = pl.program_id(0)

        @pl.when(i == 0)
        def _():
            dsh_ref[...] = jnp.zeros_like(dsh_ref)
            dsc_ref[...] = jnp.zeros_like(dsc_ref)
            dng_ref[...] = jnp.zeros_like(dng_ref)

        for r in range(R16):
            nat_ref[:, r, :] = dh_ref[r]
        dh = jnp.concatenate([nat_ref[l] for l in range(R16)], axis=0)
        g = g_ref[...]
        one_sc = 1.0 + sc_ref[...]
        xv = x_ref[...]
        r1 = lax.rsqrt(jnp.mean(xv * xv, axis=-1, keepdims=True) + EPS)
        xn = xv * r1
        dhx = dh * xn
        dxn = dh * (g * one_sc)
        dx1 = r1 * (dxn - xn * jnp.mean(dxn * xn, axis=-1, keepdims=True))
        gx_ref[...] = dx2_ref[...] + dx1
        dsh_ref[...] += jnp.sum(dh, axis=0, keepdims=True)
        dsc_ref[...] += jnp.sum(dhx * g, axis=0, keepdims=True)
        dng_ref[...] += jnp.sum(dhx * one_sc, axis=0, keepdims=True)

    nat = pl.BlockSpec((TOK, D), lambda i: (i, 0))
    vec = pl.BlockSpec((1, D), lambda i: (0, 0))
    return pl.pallas_call(
        body, name="h_bwd", grid=(S // TOK,),
        in_specs=[pl.BlockSpec((R16, R16, D), lambda i: (0, i, 0)), nat, nat, vec, vec],
        out_specs=[nat, vec, vec, vec],
        out_shape=[_sds((S, D), F32), _sds((1, D), F32), _sds((1, D), F32), _sds((1, D), F32)],
        scratch_shapes=[pltpu.VMEM((R16, R16, D), F32)],
        compiler_params=_cp(("arbitrary",)),
    )(dh, dx2, x, norm_g, scale)


def _local_step(x, target, shift, scale, gate, norm_g, w_in_full, pool_w, pool_scale, wab, wpb, wout, rel_bias,
                final_g):
    st = _local_pre(x, shift, scale, norm_g, rel_bias)
    st = _local_grads(st, target, gate, w_in_full, pool_w, pool_scale, wab, wpb, wout, final_g)
    d_rel_bias = _local_bias_grads(st)
    gx, dmod, dng = _local_input_grads(st, w_in_full, norm_g, scale)
    return dict(loss=st["loss"], grad_x=gx, dw_in=st["dw_in"], dw_ab=st["dw_ab"], dw_pb=st["dw_pb"],
                dw_out=st["dw_out"], d_norm_g=dng, d_pool_w=st["dpw"], d_pool_scale=st["dps"],
                d_rel_bias=d_rel_bias, d_final_g=st["gfg"], dmod=dmod)


def _local_pre(x, shift, scale, norm_g, rel_bias, dep=None):
    S = x.shape[0]
    LS = S // R16
    geoms = [_Geom(gi, LS) for gi in range(3)]
    idx_np = [[g.bucket_index(sub) for sub in range(2)] for g in geoms]
    buckets = [[tuple(int(b) for b in np.unique(ix) if b >= 0) for ix in pair] for pair in idx_np]
    idxs = [[jnp.asarray(ix) for ix in pair] for pair in idx_np]
    biases = [[_bias_table(idxs[gi][sub], buckets[gi][sub], rel_bias, gi, sub, dep) for sub in range(2)]
              for gi in range(3)]
    h = _norm_mod(x, norm_g, shift, scale, dep).reshape(S, D)
    return dict(geoms=geoms, idxs=idxs, buckets=buckets, biases=biases, x=x, h=h, ht=_tcast(h, "h_transpose"))


def _local_grads(st, target, gate, w_in_full, pool_w, pool_scale, wab, wpb, wout, final_g):
    geoms, idxs, biases, x, h, ht = (st[k] for k in ("geoms", "idxs", "biases", "x", "h", "ht"))
    S = h.shape[0]
    LS = S // R16
    qkv, rest = _proj(h, w_in_full)

    os_, ls_ = [], []
    for gi in range(3):
        o, l = _attn_fwd(qkv, biases[gi], geoms[gi])
        os_.append(o)
        ls_.append(l)
    attn, Lb, ag, agt = _attn_combine(os_, ls_, rest)
    pt, mp, pg, pgt = _pool_fwd(rest, pool_w, pool_scale)
    merged, mt = _branch_merge(ag, pg, wab, wpb, rest)
    dx2, dmo3, loss, gfg, dgate = _out_loss(merged.reshape(R16, LS, D), wout, x, target, gate, final_g)
    dmo = dmo3.reshape(S, D)

    dya, dyp, drest, dattn, dlb, dmp, dps = _bwd_gates(dmo, wout, wab, wpb, rest, ag, pg, attn, mp, pool_scale)
    dw_out = _mm_tn(mt, dmo, n_total=D, out_blk=lambda j: j, name="dw_out")
    dw_ab = _mm_tn(agt, dya, n_total=D, out_blk=lambda j: j, name="dw_attn_br")
    dw_pb = _mm_tn(pgt, dyp, n_total=D, out_blk=lambda j: j, name="dw_pool_br")
    drest, dpw = _pool_bwd(dmp, pt, pool_w, drest)

    dqkv = None
    dsums = []
    for gi in range(3):
        dqkv, dsum = _attn_bwd(qkv, dattn, Lb, dlb, biases[gi], geoms[gi], dqkv)
        dsums.append(dsum)

    dw_in = _mm_tn(ht, dqkv, n_total=8192, out_blk=lambda j: j, name="dw_in_qkv")
    dw_in = _mm_tn(ht, drest, n_total=8192, out_blk=_rest_blk, name="dw_in_rest", prev=dw_in)
    return dict(st, loss=loss[0, 0], dw_in=dw_in, dw_ab=dw_ab, dw_pb=dw_pb, dw_out=dw_out, dpw=dpw, dps=dps, gfg=gfg,
                dgate=dgate, dsums=dsums, dqkv=dqkv, drest=drest, dx2=dx2)


def _local_bias_grads(st, dep=None):
    drb = [_bias_grad(st["dsums"][gi][0], st["idxs"][gi][0], st["buckets"][gi][0], gi, 0, dep if gi == 0 else None) +
           _bias_grad(st["dsums"][gi][1], st["idxs"][gi][1], st["buckets"][gi][1], gi, 1) for gi in range(3)]
    return jnp.concatenate(drb, axis=1)


def _local_input_grads(st, w_in_full, norm_g, scale, dep=None):
    S = st["h"].shape[0]
    LS = S // R16
    dh = _dh(st["dqkv"], st["drest"], w_in_full, dep)
    gx, dsh, dsc, dng = _h_bwd(dh.reshape(R16, LS, D), st["dx2"], st["x"], norm_g, scale)
    return gx, jnp.concatenate([dsh, dsc, st["dgate"]], axis=1), dng


def _my_place():
    return lax.axis_index("x"), lax.axis_index("y"), lax.axis_index("c")


def _block_of(ref, axis, idx, width):
    if axis is None:
        return ref.at[idx]
    start = pl.multiple_of(idx * width, width)
    if axis == 0:
        return ref.at[pl.ds(start, width)]
    return ref.at[:, pl.ds(start, width)]


_SEM = pl.BlockSpec(memory_space=pltpu.SEMAPHORE)
_EFFECT = pltpu.SideEffectType.DATAFLOW_SIDE_EFFECTING


def _split_start(name, srcs, lands, ncopy, plan, dep=None):
    arrs = list(srcs) + list(lands)
    na, ns = len(arrs), len(srcs)
    dep_specs, dep_args = _after(dep)

    def body(*refs):
        send_sems, recv_sems, token = refs[na + len(dep_args)], refs[na + len(dep_args) + 1], refs[-1]
        for k, (s, d, dev) in enumerate(plan(refs[:ns], refs[ns:na])):
            pltpu.make_async_remote_copy(src_ref=s, dst_ref=d, send_sem=send_sems.at[k], recv_sem=recv_sems.at[k],
                                         device_id=dev, device_id_type=MESH).start()
        token[...] = jnp.zeros_like(token)

    outs = pl.pallas_call(
        body, name=name,
        out_shape=(pltpu.SemaphoreType.DMA((ncopy,)), pltpu.SemaphoreType.DMA((ncopy,)),
                   *[pltpu.HBM(a.shape, a.dtype) for a in arrs], _sds((8, 128), F32)),
        in_specs=[_HBM] * na + dep_specs,
        out_specs=(_SEM, _SEM, *[_HBM] * na, pl.BlockSpec(memory_space=pltpu.VMEM)),
        input_output_aliases={i: 2 + i for i in range(na)},
        compiler_params=pltpu.CompilerParams(has_side_effects=_EFFECT),
    )(*[pltpu.with_memory_space_constraint(a, pltpu.HBM) for a in arrs], *dep_args)
    return (outs[0], outs[1]), list(outs[2:2 + ns]), list(outs[2 + ns:2 + na]), outs[-1]


def _split_wait(name, sems, srcs, lands, plan, after):
    arrs = list(srcs) + list(lands)
    na, ns = len(arrs), len(srcs)
    after = list(after) if isinstance(after, (list, tuple)) else [after]

    def body(*refs):
        send_sems, recv_sems = refs[na], refs[na + 1]
        for k, (s, d, dev) in enumerate(plan(refs[:ns], refs[ns:na])):
            cp = pltpu.make_async_remote_copy(src_ref=s, dst_ref=d, send_sem=send_sems.at[k], recv_sem=recv_sems.at[k],
                                              device_id=dev, device_id_type=MESH)
            cp.wait_send()
            cp.wait_recv()

    outs = pl.pallas_call(
        body, name=name, out_shape=[pltpu.HBM(a.shape, a.dtype) for a in arrs],
        in_specs=[_HBM] * na + [_SEM, _SEM] + [pl.BlockSpec(memory_space=pl.ANY)] * len(after),
        out_specs=[_HBM] * na,
        input_output_aliases={i: i for i in range(na)},
        compiler_params=pltpu.CompilerParams(has_side_effects=_EFFECT),
    )(*arrs, sems[0], sems[1], *after)
    return list(outs[:ns]), list(outs[ns:])


def _gather_chips_plan(layout):
    def plan(src, land):
        x, y, c = _my_place()
        me = 4 * x + 2 * y + c
        return [(src[p], _block_of(land[p], layout[p][0], me, layout[p][1]), (cx, cy, c))
                for p in range(len(src)) for cx, cy in [(1 - x, y), (x, 1 - y)]]
    return plan


def _pair_plan(layout):
    def plan(src, land):
        x, y, c = _my_place()
        n = len(layout)
        return [(_block_of(src[p], layout[p][0], 2 * q + (1 - c), layout[p][1]), land[p].at[q], (x, y, 1 - c))
                for p in range(n) for q in range(4)] + [(src[p], land[p], (x, y, 1 - c)) for p in range(n, len(src))]
    return plan


def _chips_plan(n):
    def plan(src, land):
        x, y, c = _my_place()
        chips = [(1 - x, y), (x, 1 - y), (1 - x, 1 - y)]
        return [(src[p].at[2 * cx + cy], land[p].at[2 * x + y], (cx, cy, c)) for p in range(n) for cx, cy in chips] + \
               [(src[p], land[p].at[2 * x + y], (cx, cy, c)) for p in range(n, len(src)) for cx, cy in chips]
    return plan


def _pair_share(gathered):
    n = len(gathered)

    def body(*refs):
        land = refs[n:2 * n]
        send_sems, recv_sems, fwd_send, fwd_recv = refs[2 * n:]
        x, y, c = _my_place()
        k_dg = 3 - (2 * x + y)
        south = c == 0
        pass_on = 4 * jnp.where(south, 1 - x, x) + 2 * jnp.where(south, y, 1 - y) + c
        to = (jnp.where(south, x, 1 - x), jnp.where(south, 1 - y, y), c)

        def forward(p):
            return pltpu.make_async_remote_copy(
                src_ref=land[p].at[pass_on], dst_ref=land[p].at[pass_on], send_sem=fwd_send.at[p],
                recv_sem=fwd_recv.at[p], device_id=to, device_id_type=MESH)

        def to_sibling(p, k):
            blk = land[p].at[2 * k + c]
            return pltpu.make_async_remote_copy(
                src_ref=blk, dst_ref=blk, send_sem=send_sems.at[4 * p + k], recv_sem=recv_sems.at[4 * p + k],
                device_id=(x, y, 1 - c), device_id_type=MESH)

        for p in range(n):
            forward(p).start()
        for p in range(n):
            for k in range(4):
                @pl.when(k != k_dg)
                def _():
                    to_sibling(p, k).start()
        for p in range(n):
            dg = land[p].at[2 * k_dg + c]
            pltpu.make_async_remote_copy(src_ref=dg, dst_ref=dg, send_sem=fwd_send.at[p], recv_sem=fwd_recv.at[p],
                                         device_id=to, device_id_type=MESH).wait_recv()
            for k in range(4):
                @pl.when(k == k_dg)
                def _():
                    to_sibling(p, k).start()
        for p in range(n):
            forward(p).wait_send()
            for k in range(4):
                to_sibling(p, k).wait_send()
                blk = land[p].at[2 * k + 1 - c]
                pltpu.make_async_remote_copy(
                    src_ref=blk, dst_ref=blk, send_sem=send_sems.at[4 * p + k], recv_sem=recv_sems.at[4 * p + k],
                    device_id=(x, y, 1 - c), device_id_type=MESH).wait_recv()

    return pl.pallas_call(
        body, name="gather_pair_share", in_specs=[_HBM] * n, out_specs=[_HBM] * n,
        out_shape=[_sds(g.shape, g.dtype) for g in gathered],
        input_output_aliases={p: p for p in range(n)},
        scratch_shapes=[pltpu.SemaphoreType.DMA((4 * n,)), pltpu.SemaphoreType.DMA((4 * n,)),
                        pltpu.SemaphoreType.DMA((n,)), pltpu.SemaphoreType.DMA((n,))],
    )(*gathered)


def _place_slot(arrs, slot, n_slots, name):
    n = len(arrs)

    def body(s_ref, *refs):
        for p in range(n):
            refs[n + p][...] = refs[p][...]

    def spec(a):
        tail = a.shape[-2:]
        return pl.BlockSpec((None,) + tail, lambda i, s_ref: (s_ref[0], 0, 0))

    in_specs = [spec(a) if a.ndim == 3 else pl.BlockSpec(a.shape, lambda i, s_ref: (0, 0)) for a in arrs]
    return pl.pallas_call(
        body, name=name,
        grid_spec=pltpu.PrefetchScalarGridSpec(num_scalar_prefetch=1, grid=(1,), in_specs=in_specs,
                                               out_specs=[spec(a) for a in arrs]),
        out_shape=[_sds((n_slots,) + a.shape[-2:], a.dtype) for a in arrs],
        compiler_params=_cp(("arbitrary",)),
    )(slot, *arrs)


def _cols_from_blocks(w8, name):
    _, R, C = w8.shape

    def body(w_ref, o_ref):
        for e in range(8):
            o_ref[:, C * e:C * (e + 1)] = w_ref[e]

    return pl.pallas_call(body, name=name, out_shape=_sds((R, 8 * C), w8.dtype), compiler_params=_cp())(w8)


def _all_gather(srcs, out_shapes, layout, name):
    n = len(srcs)

    def body(*refs):
        src, out = refs[:n], refs[n:2 * n]
        send_sems, recv_sems, local_sems = refs[2 * n:]
        x, y, c = _my_place()
        me, sibling = (x, y, c), (x, y, 1 - c)
        chips = [(1 - x, y), (x, 1 - y), (1 - x, 1 - y)]

        def region(p, dev):
            px, py, pc = dev
            return _block_of(out[p], layout[p][0], 4 * px + 2 * py + pc, layout[p][1])

        def copy(p, k, block, to, src_ref=None):
            return pltpu.make_async_remote_copy(
                src_ref=region(p, block) if src_ref is None else src_ref, dst_ref=region(p, block),
                send_sem=send_sems.at[p * 7 + k], recv_sem=recv_sems.at[p * 7 + k],
                device_id=to, device_id_type=MESH)

        mine = [pltpu.make_async_copy(src[p], region(p, me), local_sems.at[p]) for p in range(n)]
        for cp in mine:
            cp.start()
        first = []
        for p in range(n):
            first.append(copy(p, 0, me, sibling, src[p]))
            first += [copy(p, 1 + j, me, (*chip, c), src[p]) for j, chip in enumerate(chips)]
        for cp in first:
            cp.start()
        passed = []
        for j, chip in enumerate(chips):
            for p in range(n):
                copy(p, 1 + j, (*chip, c), me).wait_recv()
                fwd = copy(p, 4 + j, (*chip, c), sibling)
                fwd.start()
                passed.append(fwd)
        for p in range(n):
            copy(p, 0, sibling, me).wait_recv()
            for j, chip in enumerate(chips):
                copy(p, 4 + j, (*chip, 1 - c), me).wait_recv()
        for cp in first + passed:
            cp.wait_send()
        for cp in mine:
            cp.wait()

    return pl.pallas_call(
        body, name=name, in_specs=[_HBM] * n, out_specs=[_HBM] * n,
        out_shape=[_sds(s, a.dtype) for s, a in zip(out_shapes, srcs)],
        scratch_shapes=[pltpu.SemaphoreType.DMA((7 * n,)), pltpu.SemaphoreType.DMA((7 * n,)),
                        pltpu.SemaphoreType.DMA((n,))],
    )(*srcs)


def _pair_add(dw, recv, axis, width, cidx, name):
    if axis == 0:
        tr = min(width, 256)
        C = dw.shape[1]
        grid = (4, width // tr)
        nb = width // tr
        own = pl.BlockSpec((tr, C), lambda q, i, c_ref: ((2 * q + c_ref[0]) * nb + i, 0))
        oth = pl.BlockSpec((None, tr, C), lambda q, i, c_ref: (q, i, 0))
    else:
        R = dw.shape[0]
        tr = min(R, 256)
        grid = (4, R // tr)
        own = pl.BlockSpec((tr, width), lambda q, i, c_ref: (i, 2 * q + c_ref[0]))
        oth = pl.BlockSpec((None, tr, width), lambda q, i, c_ref: (q, i, 0))

    def body(c_ref, a_ref, b_ref, o_ref):
        o_ref[...] = (a_ref[...] + b_ref[...]).astype(o_ref.dtype)

    return pl.pallas_call(
        body, name=name,
        grid_spec=pltpu.PrefetchScalarGridSpec(num_scalar_prefetch=1, grid=grid, in_specs=[own, oth], out_specs=oth),
        out_shape=_sds(recv.shape, CDT), compiler_params=_cp(("parallel", "parallel")),
    )(cidx, dw, recv)


def _add_pairs(xs, ys, name):
    n = len(xs)

    def body(*refs):
        for p in range(n):
            refs[2 * n + p][...] = refs[p][...] + refs[n + p][...]

    return pl.pallas_call(body, name=name, out_shape=[_sds(a.shape, F32) for a in xs])(*xs, *ys)


def _adaln_part(c_all, w_ada, b_part, dep=None):
    dep_specs, dep_args = _after(dep)
    vmem = pl.BlockSpec(memory_space=pltpu.VMEM)

    def body(c_ref, w_ref, b_ref, *rest):
        rest[-1][...] = _dot(c_ref[...].astype(CDT), w_ref[...].astype(CDT), NN) + b_ref[...]

    return pl.pallas_call(body, name="adaln_part", in_specs=[vmem, vmem, vmem] + dep_specs, out_specs=vmem,
                          out_shape=_sds((8, w_ada.shape[1]), F32),
                          compiler_params=_cp())(c_all, w_ada, b_part, *dep_args)


def _adam_math(w, g, m, v):
    m = ADAM_B1 * m + (1.0 - ADAM_B1) * g
    v = ADAM_B2 * v + (1.0 - ADAM_B2) * (g * g)
    m_hat = m / (1.0 - ADAM_B1 ** ADAM_STEP)
    v_hat = v / (1.0 - ADAM_B2 ** ADAM_STEP)
    delta = -ADAM_LR * (m_hat / (jnp.sqrt(v_hat) + ADAM_EPS) + ADAM_WD * w)
    return delta, m, v


def _adam(gparts, w, m, v, name):
    P, R, C = gparts.shape
    tr = R if R <= 256 else 256

    def body(g_ref, w_ref, m_ref, v_ref, go_ref, d_ref, mo_ref, vo_ref):
        g = g_ref[0].astype(F32)
        for i in range(1, P):
            g = g + g_ref[i].astype(F32)
        d, mn, vn = _adam_math(w_ref[...], g, m_ref[...], v_ref[...])
        go_ref[...] = g
        d_ref[...] = d
        mo_ref[...] = mn
        vo_ref[...] = vn

    t = pl.BlockSpec((tr, C), lambda i: (i, 0))
    return pl.pallas_call(
        body, name=name, grid=(R // tr,),
        in_specs=[pl.BlockSpec((P, tr, C), lambda i: (0, i, 0)), t, t, t], out_specs=[t, t, t, t],
        out_shape=[_sds((R, C), F32)] * 4, compiler_params=_cp(("parallel",)),
    )(gparts, w, m, v)


def _sum_parts(gparts, name):
    P, R, C = gparts.shape

    def body(g_ref, o_ref):
        g = g_ref[0]
        for i in range(1, P):
            g = g + g_ref[i]
        o_ref[...] = g

    return pl.pallas_call(body, name=name, out_shape=_sds((R, C), F32))(gparts)


def _adam_w_ada(c_t, dm, w, m, v):
    R, C = w.shape

    def body(c_ref, dm_ref, w_ref, m_ref, v_ref, go_ref, d_ref, mo_ref, vo_ref):
        g = c_ref[:, 0:1] * dm_ref[0:1, :]
        for b in range(1, 8):
            g = g + c_ref[:, b:b + 1] * dm_ref[b:b + 1, :]
        d, mn, vn = _adam_math(w_ref[...], g, m_ref[...], v_ref[...])
        go_ref[...] = g
        d_ref[...] = d
        mo_ref[...] = mn
        vo_ref[...] = vn

    return pl.pallas_call(body, name="adam_w_ada", out_shape=[_sds((R, C), F32)] * 4,
                          compiler_params=_cp())(c_t, dm, w, m, v)


def kernel(x, c, norm_g, w_ada, b_ada, w_in, pool_w, pool_scale, w_attn_br, w_pool_br, w_out, rel_bias, final_g, loss_target, m_norm_g, m_w_ada, m_b_ada, m_w_in, m_pool_w, m_pool_scale, m_w_attn_br, m_w_pool_br, m_w_out, m_rel_bias, m_final_g, v_norm_g, v_w_ada, v_b_ada, v_w_in, v_pool_w, v_pool_scale, v_w_attn_br, v_w_pool_br, v_w_out, v_rel_bias, v_final_g):
    S = x.shape[1]
    px, py, pc = _my_place()
    me = 4 * px + 2 * py + pc
    cidx = jnp.reshape(pc, (1,)).astype(jnp.int32)

    layout = [(1, 1024), (1, 128), (1, 128), (0, 128)]
    final_g2 = final_g.reshape(1, D)
    me1 = jnp.reshape(me, (1,)).astype(jnp.int32)

    (c_rows,) = _all_gather([jnp.broadcast_to(c, (8, D))], [(64, D)], [(0, 8)], "gather_c")
    c_all = c_rows[::8]

    b_part = lax.dynamic_slice(b_ada, (0, me * 384), (1, 384))
    mod_part = _adaln_part(c_all, w_ada[0], b_part)
    (mod_all,) = _all_gather([mod_part], [(8, 8, 384)], [(None, 1)], "gather_mod")
    mod = lax.dynamic_index_in_dim(mod_all, me, axis=1, keepdims=False).reshape(1, 3 * D)
    shift, scale, gate = mod[:, :D], mod[:, D:2 * D], mod[:, 2 * D:]

    shards = [w_in[0].astype(CDT), w_attn_br[0].astype(CDT), w_pool_br[0].astype(CDT), w_out[0].astype(CDT)]
    gplan = _gather_chips_plan([(None, 1)] * 4)
    g_sems, shards, lands, tok = _split_start("gather_chips_start", shards,
                                              _place_slot(shards, me1, 8, "place_own_shards"), 8, gplan, dep=mod_all)
    st = _local_pre(x[0], shift, scale, norm_g, rel_bias, dep=tok)
    _, lands = _split_wait("gather_chips_wait", g_sems, shards, lands, gplan,
                           [st["ht"]] + [b for pair in st["biases"] for b in pair])
    w_in_full, wab8, wpb8, wout8 = _pair_share(lands)
    wab, wpb = _cols_from_blocks(wab8, "w_attn_br_cols"), _cols_from_blocks(wpb8, "w_pool_br_cols")
    wout = wout8.reshape(D, D)

    st = _local_grads(st, loss_target[0], gate, w_in_full, pool_w[0], pool_scale, wab, wpb, wout, final_g2)

    early = [st["dpw"].reshape(AW, 128), st["dps"], st["gfg"], jnp.broadcast_to(st["loss"], (1, 128))]
    dws = [st["dw_in"], st["dw_ab"], st["dw_pb"], st["dw_out"]]
    pplan = _pair_plan(layout)
    pair_shapes = [(4, w, a.shape[1]) if ax == 0 else (4, a.shape[0], w) for a, (ax, w) in zip(dws, layout)]
    p_sems, srcs, recv, tok = _split_start(
        "reduce_pair_start", dws + early, [lax.empty(s, F32) for s in pair_shapes + [a.shape for a in early]], 20,
        pplan)
    d_rel_bias = _local_bias_grads(st, dep=tok)
    srcs, recv = _split_wait("reduce_pair_wait", p_sems, srcs, recv, pplan, d_rel_bias)
    parts = [_pair_add(dw, r, ax, w, cidx, f"pair_add{i}")
             for i, (dw, r, (ax, w)) in enumerate(zip(srcs[:4], recv[:4], layout))]
    parts += _add_pairs(srcs[4:], recv[4:], "pair_add_smalls")

    slot1 = jnp.reshape(2 * px + py, (1,)).astype(jnp.int32)
    cplan = _chips_plan(4)
    c_sems, parts, lands, tok = _split_start("reduce_chips_start", parts,
                                             _place_slot(parts, slot1, 4, "place_own_partials"), 24, cplan)
    grad_x, dmod, d_norm_g = _local_input_grads(st, w_in_full, norm_g, scale, dep=tok)

    dmod_all, ng_all, rb_all = _all_gather([dmod, d_norm_g, d_rel_bias], [(8, 1, 3 * D), (8, 1, D), (8, 32, 24)],
                                           [(None, 1)] * 3, "gather_late")
    o_norm_g = _adam(ng_all, norm_g, m_norm_g, v_norm_g, "adam_norm_g")
    o_rel_bias = _adam(rb_all, rel_bias, m_rel_bias, v_rel_bias, "adam_rel_bias")
    o_b_ada = _adam(dmod_all, b_ada, m_b_ada, v_b_ada, "adam_b_ada")
    dm_mine = lax.dynamic_slice(dmod_all.reshape(8, 3 * D), (0, me * 384), (8, 384))
    o_w_ada = [o[None] for o in _adam_w_ada(c_all.T, dm_mine, w_ada[0], m_w_ada[0], v_w_ada[0])]

    _, (g_in, g_ab, g_pb, g_out, g_pw, g_ps, g_fg, g_loss) = _split_wait(
        "reduce_chips_wait", c_sems, parts, lands, cplan, [o_w_ada[0], o_b_ada[0], o_norm_g[0], o_rel_bias[0]])
    o_w_in = [o[None] for o in _adam(g_in, w_in[0], m_w_in[0], v_w_in[0], "adam_w_in")]
    o_w_ab = [o[None] for o in _adam(g_ab, w_attn_br[0], m_w_attn_br[0], v_w_attn_br[0], "adam_w_attn_br")]
    o_w_pb = [o[None] for o in _adam(g_pb, w_pool_br[0], m_w_pool_br[0], v_w_pool_br[0], "adam_w_pool_br")]
    o_w_out = [o[None] for o in _adam(g_out, w_out[0], m_w_out[0], v_w_out[0], "adam_w_out")]
    o_pool_w = [o.reshape(pool_w.shape) for o in _adam(g_pw, pool_w.reshape(AW, 128), m_pool_w.reshape(AW, 128),
                                                       v_pool_w.reshape(AW, 128), "adam_pool_w")]
    o_pool_scale = _adam(g_ps, pool_scale, m_pool_scale, v_pool_scale, "adam_pool_scale")
    o_final_g = [o.reshape(D) for o in _adam(g_fg, final_g2, m_final_g.reshape(1, D), v_final_g.reshape(1, D),
                                             "adam_final_g")]
    loss = _sum_parts(g_loss, "sum_loss")[0, 0]

    per_w = [o_norm_g, o_w_ada, o_b_ada, o_w_in, o_pool_w, o_pool_scale, o_w_ab, o_w_pb, o_w_out, o_rel_bias, o_final_g]
    outs = [loss, grad_x[None]]
    for k in range(4):
        outs += [o[k] for o in per_w]
    return tuple(outs)
```

```python
import functools
import math

import numpy as np
import jax
import jax.numpy as jnp
from jax import lax
from jax.experimental import pallas as pl
from jax.experimental.pallas import tpu as pltpu

F32 = jnp.float32
CDT = jnp.bfloat16
D = 1024
HD = 64
NH = 8
AW = 512
QKV_W = 4608
REST_W = 3584
R16 = 16
EPS = 1e-6
NEG = -1e30
N_BACK = 128
ADAM_LR, ADAM_B1, ADAM_B2, ADAM_EPS, ADAM_WD, ADAM_STEP = 0.001, 0.9, 0.999, 1e-08, 0.01, 10
MESH = pl.DeviceIdType.MESH
VMEM_MB = 1024 * 1024


def _cp(sem=None, vmem=48, **kw):
    if sem is not None:
        kw["dimension_semantics"] = sem
    return pltpu.CompilerParams(vmem_limit_bytes=vmem * VMEM_MB, **kw)


def _sds(shape, dtype):
    return jax.ShapeDtypeStruct(tuple(shape), dtype)


def _rest_blk(jj):
    return jnp.where(jj == 0, 9, jnp.where(jj == 6, 10, jj + 10))


def _sigmoid(z):
    return 0.5 * jnp.tanh(0.5 * z) + 0.5


def _dot(a, b, dims):
    return lax.dot_general(a, b, (dims, ((), ())), preferred_element_type=F32)


NN = ((1,), (0,))
NT = ((1,), (1,))
TN = ((0,), (0,))


def _ld(ref):
    if len(ref.shape) == 2:
        return ref[...]
    return jnp.concatenate([ref[a] for a in range(ref.shape[0])], axis=0)


def _st(ref, val, cols=None):
    val = val.astype(ref.dtype)
    if len(ref.shape) == 2:
        if cols is None:
            ref[...] = val
        else:
            ref[:, cols] = val
        return
    rows = ref.shape[1]
    for a in range(ref.shape[0]):
        if cols is None:
            ref[a] = val[a * rows:(a + 1) * rows]
        else:
            ref[a, :, cols] = val[a * rows:(a + 1) * rows]


REST_COLS = ((9, 10), (11, 16), (10, 11))
_HBM = pl.BlockSpec(memory_space=pltpu.HBM)


def _resident(shape):
    return pl.BlockSpec(shape, lambda *ids: (0,) * len(shape), pipeline_mode=pl.Buffered(1))


def _w512(w_ref, j):
    return w_ref[j // 2, :, 512 * (j % 2):512 * (j % 2 + 1)]


def _proj(h, w):
    S = h.shape[0]
    tm = 256

    def body(a_ref, w_ref, q_ref, r_ref):
        a = a_ref[...]
        for j in range(9):
            q_ref[:, 512 * j:512 * (j + 1)] = _dot(a, _w512(w_ref, j), NN).astype(q_ref.dtype)
        off = 0
        for lo, hi in REST_COLS:
            for j in range(lo, hi):
                r_ref[:, off:off + 512] = _dot(a, _w512(w_ref, j), NN)
                off += 512

    return pl.pallas_call(
        body, name="proj", grid=(S // tm,),
        in_specs=[pl.BlockSpec((tm, D), lambda i: (i, 0)), _resident((8, D, 1024))],
        out_specs=[pl.BlockSpec((tm, QKV_W), lambda i: (i, 0)), pl.BlockSpec((tm, REST_W), lambda i: (i, 0))],
        out_shape=[_sds((S, QKV_W), CDT), _sds((S, REST_W), F32)],
        compiler_params=_cp(("parallel",), vmem=56),
    )(h, w)


def _after(dep):
    return ([], []) if dep is None else ([pl.BlockSpec(memory_space=pl.ANY)], [dep])


def _dh(dqkv, drest, w, dep=None):
    S = dqkv.shape[0]
    tm = 256
    dep_specs, dep_args = _after(dep)

    def body(a_ref, b_ref, w_ref, *rest):
        o_ref = rest[-1]
        acc = _dot(a_ref[:, 0:512], _w512(w_ref, 0), NT)
        for j in range(1, 9):
            acc = acc + _dot(a_ref[:, 512 * j:512 * (j + 1)], _w512(w_ref, j), NT)
        off = 0
        for lo, hi in REST_COLS:
            for j in range(lo, hi):
                acc = acc + _dot(b_ref[:, off:off + 512], _w512(w_ref, j), NT)
                off += 512
        o_ref[...] = acc

    return pl.pallas_call(
        body, name="dh", grid=(S // tm,),
        in_specs=[pl.BlockSpec((tm, QKV_W), lambda i: (i, 0)), pl.BlockSpec((tm, REST_W), lambda i: (i, 0)),
                  _resident((8, D, 1024))] + dep_specs,
        out_specs=pl.BlockSpec((tm, D), lambda i: (i, 0)),
        out_shape=_sds((S, D), F32),
        compiler_params=_cp(("parallel",), vmem=56),
    )(dqkv, drest, w, *dep_args)


def _mm_tn(at, b, *, n_total, out_blk, name, prev=None, tn=512):
    R, S = at.shape
    nb = b.shape[1] // tn

    def body(*refs):
        refs[-1][...] = _dot(refs[0][...], refs[1][...], NN)

    in_specs = [_resident((R, S)), pl.BlockSpec((S, tn), lambda j: (0, j))]
    args = [at, b]
    aliases = {}
    if prev is not None:
        in_specs.append(pl.BlockSpec(memory_space=pl.ANY))
        args.append(prev)
        aliases = {2: 0}
    return pl.pallas_call(
        body, name=name, grid=(nb,), in_specs=in_specs,
        out_specs=pl.BlockSpec((R, tn), lambda j: (0, out_blk(j))),
        out_shape=_sds((R, n_total), F32),
        input_output_aliases=aliases,
        compiler_params=_cp(("parallel",)),
    )(*args)


def _tcast(a, name):
    M, C = a.shape
    tm = 512

    def body(a_ref, o_ref):
        o_ref[...] = a_ref[...].astype(F32).T.astype(o_ref.dtype)

    return pl.pallas_call(
        body, name=name, grid=(M // tm,),
        in_specs=[pl.BlockSpec((tm, C), lambda i: (i, 0))],
        out_specs=pl.BlockSpec((C, tm), lambda i: (0, i)),
        out_shape=_sds((C, M), CDT), compiler_params=_cp(("parallel",)),
    )(a)


TOK = R16 * R16


def _swap16():
    i = lax.broadcasted_iota(jnp.int32, (TOK, TOK), 0)
    j = lax.broadcasted_iota(jnp.int32, (TOK, TOK), 1)
    return jnp.where(j == jnp.bitwise_and(i, 15) * 16 + jnp.right_shift(i, 4), 1.0, 0.0).astype(CDT)


def _norm_mod(x, norm_g, shift, scale, dep=None):
    S = x.shape[0]
    dep_specs, dep_args = _after(dep)

    def body(x_ref, g_ref, sh_ref, sc_ref, *rest):
        xv = x_ref[...]
        xn = xv * lax.rsqrt(jnp.mean(xv * xv, axis=-1, keepdims=True) + EPS)
        h = (xn * (g_ref[...] * (1.0 + sc_ref[...])) + sh_ref[...]).astype(CDT)
        _st(rest[-1], _dot(_swap16(), h, NN))

    vec = pl.BlockSpec((1, D), lambda i: (0, 0))
    return pl.pallas_call(
        body, name="norm_mod", grid=(S // TOK,),
        in_specs=[pl.BlockSpec((TOK, D), lambda i: (i, 0)), vec, vec, vec] + dep_specs,
        out_specs=pl.BlockSpec((R16, R16, D), lambda i: (0, i, 0)),
        out_shape=_sds((R16, S // R16, D), CDT), compiler_params=_cp(("parallel",)),
    )(x, norm_g, shift, scale, *dep_args)


def _t5_bucket_np(n):
    nf = np.maximum(n, 1).astype(np.float32)
    large = 16 + (np.log(nf / np.float32(16)) / np.float32(math.log(2048 / 16)) * np.float32(16)).astype(np.int32)
    large = np.minimum(large, 31)
    return np.where(n < 16, n, large).astype(np.int32)


class _Geom:
    def __init__(self, gi, LS):
        self.gi, self.LS = gi, LS
        self.dil = (1, 4, 16)[gi]
        if gi == 0:
            self.lead, self.rows, self.n_tiles, self.has_prev = (R16,), 16, LS // 16, True
            a = np.arange(R16)[:, None]
            ll = np.arange(16)[None, :]
            self.pos = (16 * ll + a).reshape(-1)
        elif gi == 1:
            self.lead, self.rows, self.n_tiles, self.has_prev = (4,), 64, LS // 64, True
            a = np.arange(4)[:, None]
            ll = np.arange(64)[None, :]
            self.pos = (4 * ll + a).reshape(-1)
        else:
            self.lead, self.rows, self.n_tiles, self.has_prev = (), LS, 1, False
            self.pos = np.arange(LS)
        self.T = self.pos.shape[0]
        self.Tk = 2 * self.T if self.has_prev else self.T
        self.outer = {0: (), 1: (4,), 2: (R16,)}[gi]

    def view(self, arr):
        S, C = arr.shape
        if self.gi == 1:
            return arr.reshape(4, 4, self.LS, C)
        return arr.reshape(R16, self.LS, C)

    def view_shape(self, C):
        return (4, 4, self.LS, C) if self.gi == 1 else (R16, self.LS, C)

    def unview(self, arr):
        return arr.reshape(R16 * self.LS, arr.shape[-1])

    def spec(self, W, cblk, tile):
        if self.gi == 0:
            return pl.BlockSpec((R16, 16, W), lambda n: (0, tile(n), cblk))
        if self.gi == 1:
            return pl.BlockSpec((4, None, 64, W), lambda r, n: (0, r, tile(n), cblk))
        return pl.BlockSpec((None, self.LS, W), lambda r: (r, 0, cblk))

    def halves(self):
        R = self.rows
        idx = np.arange(self.T).reshape(-1, R)
        return idx[:, :R // 2].reshape(-1), idx[:, R // 2:].reshape(-1)

    def bucket_index(self, sub):
        lo, hi = self.halves()
        pos = self.pos
        if sub == 0:
            pq = pos[lo]
            pk = np.concatenate([pos[hi] - self.T, pos[lo]]) if self.has_prev else np.concatenate([pos[lo], pos[hi]])
        else:
            pq = pos[hi]
            pk = np.concatenate([pos[lo], pos[hi]])
        dist = pq[:, None] - pk[None, :]
        ok = (dist >= 0) & (dist <= N_BACK)
        bucket = _t5_bucket_np(np.clip(dist, 0, N_BACK) * self.dil)
        return np.where(ok, bucket, -1).astype(np.int32)


def _bias_table(idx, buckets, rel_bias, gi, sub, dep=None):
    T, Tk = idx.shape
    tr = 32
    dep_specs, dep_args = _after(dep)

    def body(idx_ref, rb_ref, *rest):
        h = pl.program_id(0)
        ix = idx_ref[...]
        acc = jnp.full(ix.shape, NEG, F32)
        for b in buckets:
            acc = jnp.where(ix == b, rb_ref[b, gi * NH + h], acc)
        rest[-1][...] = acc

    return pl.pallas_call(
        body, name=f"bias_table{gi}{'ab'[sub]}", grid=(NH, T // tr),
        in_specs=[pl.BlockSpec((tr, Tk), lambda h, i: (i, 0)), pl.BlockSpec(memory_space=pltpu.SMEM)] + dep_specs,
        out_specs=pl.BlockSpec((None, tr, Tk), lambda h, i: (h, i, 0)),
        out_shape=_sds((NH, T, Tk), F32), compiler_params=_cp(("parallel", "parallel")),
    )(idx, rel_bias, *dep_args)


def _bias_grad(dsum, idx, buckets, gi, sub, dep=None):
    _, T, Tk = dsum.shape
    dep_specs, dep_args = _after(dep)

    def body(ds_ref, idx_ref, *rest):
        o_ref, r_ref = rest[-2], rest[-1]
        r_ref[...] = jnp.zeros_like(r_ref)
        ix, ds = idx_ref[...], ds_ref[...]
        for b in buckets:
            r_ref[b:b + 1, :] = jnp.sum(jnp.where(ix == b, ds, 0.0), axis=0, keepdims=True)
        o_ref[...] = jnp.broadcast_to(jnp.sum(r_ref[...], axis=1, keepdims=True), (32, 128))

    out = pl.pallas_call(
        body, name=f"bias_grad{gi}{'ab'[sub]}", grid=(NH,),
        in_specs=[pl.BlockSpec((None, T, Tk), lambda h: (h, 0, 0)), pl.BlockSpec((T, Tk), lambda h: (0, 0))] + dep_specs,
        out_specs=pl.BlockSpec((None, 32, 128), lambda h: (h, 0, 0)),
        out_shape=_sds((NH, 32, 128), F32), scratch_shapes=[pltpu.VMEM((32, Tk), F32)],
        compiler_params=_cp(("parallel",)),
    )(dsum, idx, *dep_args)
    return out[:, :, 0].T


def _prev_mask(n, T, Tk):
    col = lax.broadcasted_iota(jnp.int32, (1, Tk), 1)
    return jnp.where((col < T) & (n == 0), NEG, 0.0).astype(F32)


def _split_rows(v, R):
    A, half = v.shape[0] // R, R // 2
    via_f32 = v.dtype != F32 and half % 16 != 0
    w = v.astype(F32) if via_f32 else v
    lo = jnp.concatenate([w[a * R:a * R + half] for a in range(A)], axis=0)
    hi = jnp.concatenate([w[a * R + half:(a + 1) * R] for a in range(A)], axis=0)
    return (lo.astype(v.dtype), hi.astype(v.dtype)) if via_f32 else (lo, hi)


def _merge_rows(lo, hi, R):
    half = R // 2
    A = lo.shape[0] // half
    return jnp.concatenate([x[a * half:(a + 1) * half] for a in range(A) for x in (lo, hi)], axis=0)


def _attn_fwd(qkv, biases, geom):
    S = qkv.shape[0]
    gi, T, R, has_prev = geom.gi, geom.T, geom.rows, geom.has_prev
    H2 = T // 2
    qv = geom.view(qkv)
    cq, ck, cv = 3 * gi, 3 * gi + 1, 3 * gi + 2

    def body(*refs):
        if has_prev:
            q_ref, kp_ref, kc_ref, vp_ref, vc_ref, ba_ref, bb_ref, o_ref, l_ref = refs
        else:
            q_ref, kc_ref, vc_ref, ba_ref, bb_ref, o_ref, l_ref = refs
        q_lo, q_hi = _split_rows(_ld(q_ref), R)
        kc_lo, kc_hi = _split_rows(_ld(kc_ref), R)
        vc_lo, vc_hi = _split_rows(_ld(vc_ref), R)
        k_b = jnp.concatenate([kc_lo, kc_hi], axis=0)
        v_b = jnp.concatenate([vc_lo, vc_hi], axis=0)
        if has_prev:
            n = pl.program_id(len(geom.outer))
            k_a = jnp.concatenate([_split_rows(_ld(kp_ref), R)[1], kc_lo], axis=0)
            v_a = jnp.concatenate([_split_rows(_ld(vp_ref), R)[1], vc_lo], axis=0)
            pm = _prev_mask(n, H2, T)
        else:
            k_a, v_a, pm = k_b, v_b, None
        low = lax.broadcasted_iota(jnp.int32, (1, 2 * HD), 1) < HD
        for j in range(NH // 2):
            cols = slice(2 * HD * j, 2 * HD * (j + 1))
            o_half, l_half = [], []
            for q, k, v, b_ref, mask in ((q_lo, k_a, v_a, ba_ref, pm), (q_hi, k_b, v_b, bb_ref, None)):
                q2 = q[:, cols].astype(F32) * 0.125
                k2, v2 = k[:, cols], v[:, cols]
                qs = jnp.concatenate([jnp.where(low, q2, 0.0), jnp.where(low, 0.0, q2)], axis=0).astype(q.dtype)
                s = _dot(qs, k2, NT) + b_ref[j]
                if mask is not None:
                    s = s + mask
                m = jnp.max(s, axis=1, keepdims=True)
                p = jnp.exp(s - m)
                l = jnp.sum(p, axis=1, keepdims=True)
                pv = _dot(p.astype(v.dtype), v2, NN) / l
                lse = m + jnp.log(l)
                o_half.append(jnp.where(low, pv[:H2], pv[H2:]))
                l_half.append(jnp.where(low, lse[:H2], lse[H2:]))
            _st(o_ref, _merge_rows(o_half[0], o_half[1], R), cols)
            _st(l_ref, _merge_rows(l_half[0], l_half[1], R), cols)

    cur = lambda n: n
    prev = lambda n: jnp.maximum(n - 1, 0)
    if has_prev:
        in_specs = [geom.spec(AW, cq, cur), geom.spec(AW, ck, prev), geom.spec(AW, ck, cur),
                    geom.spec(AW, cv, prev), geom.spec(AW, cv, cur)]
        args = [qv] * 5
    else:
        in_specs = [geom.spec(AW, cq, cur), geom.spec(AW, ck, cur), geom.spec(AW, cv, cur)]
        args = [qv] * 3
    nd = len(geom.outer) + (1 if has_prev else 0)
    biases = [b.reshape(NH // 2, 2 * H2, b.shape[2]) for b in biases]
    in_specs += [pl.BlockSpec(b.shape, lambda *ids: (0, 0, 0)) for b in biases]
    grid = geom.outer + ((geom.n_tiles,) if has_prev else ())
    oshape = geom.view_shape(AW)
    o, l = pl.pallas_call(
        body, name=f"attn_fwd{gi}", grid=grid, in_specs=in_specs,
        out_specs=[geom.spec(AW, 0, cur), geom.spec(AW, 0, cur)],
        out_shape=[_sds(oshape, F32), _sds(oshape, F32)],
        compiler_params=_cp(("arbitrary",) * nd),
    )(*args, *biases)
    return geom.unview(o), geom.unview(l)


def _attn_combine(os_, ls_, rest):
    S = rest.shape[0]
    tm = 512

    def body(o0, o1, o2, l0, l1, l2, z_ref, attn_ref, L_ref, ag_ref, agt_ref):
        la, lb, lc = l0[...], l1[...], l2[...]
        m = jnp.maximum(jnp.maximum(la, lb), lc)
        ea, eb, ec = jnp.exp(la - m), jnp.exp(lb - m), jnp.exp(lc - m)
        den = ea + eb + ec
        attn = (ea * o0[...] + eb * o1[...] + ec * o2[...]) / den
        attn_ref[...] = attn
        L_ref[...] = m + jnp.log(den)
        z = z_ref[...]
        ag = attn * (z * _sigmoid(z))
        ag_ref[...] = ag.astype(ag_ref.dtype)
        agt_ref[...] = ag.T.astype(agt_ref.dtype)

    t = pl.BlockSpec((tm, AW), lambda i: (i, 0))
    return pl.pallas_call(
        body, name="attn_combine", grid=(S // tm,),
        in_specs=[t] * 6 + [pl.BlockSpec((tm, AW), lambda i: (i, 0))],
        out_specs=[t, t, t, pl.BlockSpec((AW, tm), lambda i: (0, i))],
        out_shape=[_sds((S, AW), F32), _sds((S, AW), F32), _sds((S, AW), CDT), _sds((AW, S), CDT)],
        compiler_params=_cp(("parallel",)),
    )(*os_, *ls_, rest)


def _pool_counts(r, LS, win):
    l = lax.broadcasted_iota(jnp.int32, (LS, 1), 0)
    return jnp.minimum(16 * l + (r + 1), win).astype(F32)


def _pool_fwd(rest, pool_w, pool_scale):
    S = rest.shape[0]
    LS = S // R16
    r3 = rest.reshape(R16, LS, REST_W)
    PG = 128

    def body(u_ref, z_ref, pw_ref, ps_ref, pt_ref, mp_ref, pg_ref, pgt_ref, e_ref):
        g = pl.program_id(0)
        win = jnp.left_shift(2, g)
        row = lax.broadcasted_iota(jnp.int32, (LS, PG), 0)
        for r in range(R16):
            e_ref[R16 + r] = u_ref[r]
        for r in range(1, R16):
            e_ref[r] = jnp.where(row == 0, 0.0, pltpu.roll(u_ref[r], 1, 0))

        def double(s):
            for i in range(2 * R16 - 1, 2 * s - 1, -1):
                e_ref[i] = e_ref[i] + e_ref[i - s]

        double(1)
        for k in (1, 2, 3):
            pl.when(g >= k)(functools.partial(double, 2 ** k))
        pw = pw_ref[...].astype(CDT)
        for r in range(R16):
            u = u_ref[r]
            pooled = e_ref[R16 + r] / _pool_counts(r, LS, win) - u
            pc = pooled.astype(CDT)
            mp = _dot(pc, pw, NN)
            z = z_ref[r]
            pg = (mp * ps_ref[...]) * (z * _sigmoid(z))
            mp_ref[r] = mp
            pg_ref[r] = pg.astype(pg_ref.dtype)
            pt_ref[:, r * LS:(r + 1) * LS] = pooled.T.astype(pt_ref.dtype)
            pgt_ref[:, r * LS:(r + 1) * LS] = pg.T.astype(pgt_ref.dtype)

    col = lambda off: pl.BlockSpec((R16, LS, PG), lambda g: (0, 0, off + g))
    tsp = pl.BlockSpec((PG, S), lambda g: (g, 0))
    pt, mp, pg, pgt = pl.pallas_call(
        body, name="pool_fwd", grid=(4,),
        in_specs=[col(24), col(4), pl.BlockSpec((None, PG, PG), lambda g: (g, 0, 0)),
                  pl.BlockSpec((1, PG), lambda g: (0, g))],
        out_specs=[tsp, col(0), col(0), tsp],
        out_shape=[_sds((AW, S), CDT), _sds((R16, LS, AW), F32), _sds((R16, LS, AW), CDT), _sds((AW, S), CDT)],
        scratch_shapes=[pltpu.VMEM((2 * R16, LS, PG), F32)],
        compiler_params=_cp(("parallel",)),
    )(r3, r3, pool_w, pool_scale)
    return pt, mp.reshape(S, AW), pg.reshape(S, AW), pgt


def _branch_merge(ag, pg, wab, wpb, rest):
    S = ag.shape[0]
    tm, tn = 512, 512

    def body(ag_ref, pg_ref, wa_ref, wp_ref, ga_ref, gp_ref, m_ref, mt_ref):
        ya = _dot(ag_ref[...], wa_ref[...], NN)
        yp = _dot(pg_ref[...], wp_ref[...], NN)
        mg = _sigmoid(ga_ref[...]) * ya + _sigmoid(gp_ref[...]) * yp
        m_ref[...] = mg.astype(m_ref.dtype)
        mt_ref[...] = mg.T.astype(mt_ref.dtype)

    a = pl.BlockSpec((tm, AW), lambda i, j: (i, 0))
    w = pl.BlockSpec((AW, tn), lambda i, j: (0, j))
    o = pl.BlockSpec((tm, tn), lambda i, j: (i, j))
    return pl.pallas_call(
        body, name="branch_merge", grid=(S // tm, D // tn),
        in_specs=[a, a, w, w, pl.BlockSpec((tm, tn), lambda i, j: (i, 2 + j)),
                  pl.BlockSpec((tm, tn), lambda i, j: (i, 4 + j))],
        out_specs=[o, pl.BlockSpec((tn, tm), lambda i, j: (j, i))],
        out_shape=[_sds((S, D), CDT), _sds((D, S), CDT)],
        compiler_params=_cp(("parallel", "parallel")),
    )(ag, pg, wab, wpb, rest, rest)


def _out_loss(merged, wout, x, target, gate, final_g):
    S = x.shape[0]

    def body(m_ref, w_ref, x_ref, t_ref, gate_ref, fg_ref, dx2_ref, dmo_ref, loss_ref, gfg_ref, dgate_ref):
        i = pl.program_id(0)

        @pl.when(i == 0)
        def _():
            loss_ref[...] = jnp.zeros_like(loss_ref)
            gfg_ref[...] = jnp.zeros_like(gfg_ref)
            dgate_ref[...] = jnp.zeros_like(dgate_ref)

        swap = _swap16()
        mo = _dot(_dot(swap, _ld(m_ref), NN).astype(CDT), w_ref[...], NN)
        gate, fg = gate_ref[...], fg_ref[...]
        x2 = x_ref[...] + gate * mo
        r2 = lax.rsqrt(jnp.mean(x2 * x2, axis=-1, keepdims=True) + EPS)
        xn2 = x2 * r2
        diff = xn2 * fg - t_ref[...]
        loss = jnp.sum(jnp.sum(diff * diff, axis=-1, keepdims=True), axis=0, keepdims=True)
        dy = diff * (1.0 / D)
        dxn2 = dy * fg
        dx2 = r2 * (dxn2 - xn2 * jnp.mean(dxn2 * xn2, axis=-1, keepdims=True))
        dx2_ref[...] = dx2
        _st(dmo_ref, _dot(swap, (dx2 * gate).astype(CDT), NN))
        loss_ref[...] += jnp.broadcast_to(loss * (0.5 / D), loss_ref.shape)
        gfg_ref[...] += jnp.sum(dy * xn2, axis=0, keepdims=True)
        dgate_ref[...] += jnp.sum(dx2 * mo, axis=0, keepdims=True)

    slab = pl.BlockSpec((R16, R16, D), lambda i: (0, i, 0))
    nat = pl.BlockSpec((TOK, D), lambda i: (i, 0))
    vec = pl.BlockSpec((1, D), lambda i: (0, 0))
    return pl.pallas_call(
        body, name="out_loss", grid=(S // TOK,),
        in_specs=[slab, _resident((D, D)), nat, nat, vec, vec],
        out_specs=[nat, slab, pl.BlockSpec((1, 128), lambda i: (0, 0)), vec, vec],
        out_shape=[_sds((S, D), F32), _sds((R16, S // R16, D), CDT), _sds((1, 128), F32), _sds((1, D), F32),
                   _sds((1, D), F32)],
        compiler_params=_cp(("arbitrary",)),
    )(merged, wout, x, target, gate, final_g)


def _bwd_gates(dmo, wout, wab, wpb, rest, ag, pg, attn, mp, pool_scale):
    S = dmo.shape[0]
    tm = 256

    def body(dmo_ref, wo_ref, wa_ref, wp_ref, za_ref, zp_ref, ga0, ga1, gp0, gp1, ag_ref, pg_ref, at_ref, mp_ref,
             ps_ref, dya_ref, dyp_ref, dr_ref, dat_ref, dl_ref, dmp_ref, dps_ref):
        i = pl.program_id(0)

        @pl.when(i == 0)
        def _():
            dps_ref[...] = jnp.zeros_like(dps_ref)

        dm = _dot(dmo_ref[...], wo_ref[...], NT)
        sa = _sigmoid(jnp.concatenate([ga0[...], ga1[...]], axis=1))
        sp = _sigmoid(jnp.concatenate([gp0[...], gp1[...]], axis=1))
        dya = (dm * sa).astype(CDT)
        dyp = (dm * sp).astype(CDT)
        dya_ref[...] = dya
        dyp_ref[...] = dyp
        ya = _dot(ag_ref[...], wa_ref[...], NN)
        yp = _dot(pg_ref[...], wp_ref[...], NN)
        dr_ref[:, 1024:2048] = (dm * ya * sa * (1.0 - sa)).astype(dr_ref.dtype)
        dr_ref[:, 2048:3072] = (dm * yp * sp * (1.0 - sp)).astype(dr_ref.dtype)
        da = _dot(dya, wa_ref[...], NT)
        dp = _dot(dyp, wp_ref[...], NT)
        za = za_ref[...]
        sga = _sigmoid(za)
        attn = at_ref[...]
        dattn = da * (za * sga)
        dr_ref[:, 0:512] = (da * attn * (sga * (1.0 + za * (1.0 - sga)))).astype(dr_ref.dtype)
        dat_ref[...] = dattn.astype(dat_ref.dtype)
        prod = dattn * attn
        for h in range(NH):
            sl = slice(HD * h, HD * h + HD)
            dl_ref[:, sl] = jnp.broadcast_to(jnp.sum(prod[:, sl], axis=1, keepdims=True), (tm, HD))
        zp = zp_ref[...]
        sgp = _sigmoid(zp)
        mpre = mp_ref[...]
        ps = ps_ref[...]
        dmixed = dp * (zp * sgp)
        dr_ref[:, 512:1024] = (dp * (mpre * ps) * (sgp * (1.0 + zp * (1.0 - sgp)))).astype(dr_ref.dtype)
        dps_ref[...] += jnp.sum(dmixed * mpre, axis=0, keepdims=True)
        dmp_ref[...] = (dmixed * ps).astype(dmp_ref.dtype)

    full = lambda shape: pl.BlockSpec(shape, lambda i: (0, 0))
    tD = pl.BlockSpec((tm, D), lambda i: (i, 0))
    tA = pl.BlockSpec((tm, AW), lambda i: (i, 0))
    rb = lambda c: pl.BlockSpec((tm, AW), lambda i: (i, c))
    return pl.pallas_call(
        body, name="bwd_gates", grid=(S // tm,),
        in_specs=[tD, full((D, D)), full((AW, D)), full((AW, D)), rb(0), rb(1), rb(2), rb(3), rb(4), rb(5),
                  tA, tA, tA, tA, full((1, AW))],
        out_specs=[tD, tD, pl.BlockSpec((tm, 3072), lambda i: (i, 0)), tA, tA, tA, full((1, AW))],
        out_shape=[_sds((S, D), CDT), _sds((S, D), CDT), _sds((S, REST_W), CDT), _sds((S, AW), CDT),
                   _sds((S, AW), F32), _sds((S, AW), CDT), _sds((1, AW), F32)],
        compiler_params=_cp(("arbitrary",), vmem=56),
    )(dmo, wout, wab, wpb, rest, rest, rest, rest, rest, rest, ag, pg, attn, mp, pool_scale)


def _pool_bwd(dmp, pt, pool_w, drest):
    S = dmp.shape[0]
    LS = S // R16
    PG = 128
    d3 = dmp.reshape(R16, LS, AW)
    dr3 = drest.reshape(R16, LS, REST_W)

    def body(d_ref, pt_ref, pw_ref, dr_in, du_ref, dpw_ref, g_ref, s_ref):
        del dr_in
        g = pl.program_id(0)
        win = jnp.left_shift(2, g)
        row = lax.broadcasted_iota(jnp.int32, (LS, PG), 0)
        pw = pw_ref[...].astype(CDT)
        dpw = jnp.zeros((PG, PG), F32)
        for r in range(R16):
            dm = d_ref[r]
            dpw = dpw + _dot(pt_ref[:, r * LS:(r + 1) * LS], dm, NN)
            dpooled = _dot(dm, pw, NT)
            g_ref[r] = dpooled
            sc = dpooled / _pool_counts(r, LS, win)
            s_ref[r] = sc
            if r < R16 - 1:
                s_ref[R16 + r] = jnp.where(row == LS - 1, 0.0, pltpu.roll(sc, LS - 1, 0))
        dpw_ref[...] = dpw

        def double(s):
            for i in range(0, 2 * R16 - 2 * s):
                s_ref[i] = s_ref[i] + s_ref[i + s]

        double(1)
        for k in (1, 2, 3):
            pl.when(g >= k)(functools.partial(double, 2 ** k))
        for r in range(R16):
            du_ref[r] = (s_ref[r] - g_ref[r]).astype(du_ref.dtype)

    du, dpw = pl.pallas_call(
        body, name="pool_bwd", grid=(4,),
        in_specs=[pl.BlockSpec((R16, LS, PG), lambda g: (0, 0, g)), pl.BlockSpec((PG, S), lambda g: (g, 0)),
                  pl.BlockSpec((None, PG, PG), lambda g: (g, 0, 0)), pl.BlockSpec(memory_space=pl.ANY)],
        out_specs=[pl.BlockSpec((R16, LS, PG), lambda g: (0, 0, 24 + g)),
                   pl.BlockSpec((None, PG, PG), lambda g: (g, 0, 0))],
        out_shape=[_sds((R16, LS, REST_W), CDT), _sds((4, PG, PG), F32)],
        scratch_shapes=[pltpu.VMEM((R16, LS, PG), F32), pltpu.VMEM((2 * R16, LS, PG), F32)],
        input_output_aliases={3: 0},
        compiler_params=_cp(("parallel",)),
    )(d3, pt, pool_w, dr3)
    return du.reshape(S, REST_W), dpw


def _attn_bwd(qkv, dattn, Lb, dlb, biases, geom, dqkv_prev):
    S = qkv.shape[0]
    gi, T, R, has_prev, N = geom.gi, geom.T, geom.rows, geom.has_prev, geom.n_tiles
    H2 = T // 2
    nd_outer = len(geom.outer)
    qv = geom.view(qkv)
    cq, ck, cv = 3 * gi, 3 * gi + 1, 3 * gi + 2
    W3 = 3 * AW

    def heads(q_ref, kp_ref, kc_ref, vp_ref, vc_ref, do_ref, L_ref, dl_ref, ba_ref, bb_ref, pm, dsa_ref, dsb_ref, emit):
        q_lo, q_hi = _split_rows(_ld(q_ref), R)
        do_lo, do_hi = _split_rows(_ld(do_ref), R)
        L_lo, L_hi = _split_rows(_ld(L_ref), R)
        dl_lo, dl_hi = _split_rows(_ld(dl_ref), R)
        kc_lo, kc_hi = _split_rows(_ld(kc_ref), R)
        vc_lo, vc_hi = _split_rows(_ld(vc_ref), R)
        k_b = jnp.concatenate([kc_lo, kc_hi], axis=0)
        v_b = jnp.concatenate([vc_lo, vc_hi], axis=0)
        if has_prev:
            k_a = jnp.concatenate([_split_rows(_ld(kp_ref), R)[1], kc_lo], axis=0)
            v_a = jnp.concatenate([_split_rows(_ld(vp_ref), R)[1], vc_lo], axis=0)
        else:
            k_a, v_a = k_b, v_b
        low = lax.broadcasted_iota(jnp.int32, (1, 2 * HD), 1) < HD
        for j in range(NH // 2):
            sl = slice(2 * HD * j, 2 * HD * (j + 1))
            res = []
            for q, k, v, do, L, dl, b_ref, mask, ds_ref in (
                    (q_lo, k_a, v_a, do_lo, L_lo, dl_lo, ba_ref, pm, dsa_ref),
                    (q_hi, k_b, v_b, do_hi, L_hi, dl_hi, bb_ref, None, dsb_ref)):
                q2 = q[:, sl].astype(F32) * 0.125
                do2 = do[:, sl].astype(F32)
                k2, v2 = k[:, sl], v[:, sl]
                L2, dl2 = L[:, sl], dl[:, sl]
                qs = jnp.concatenate([jnp.where(low, q2, 0.0), jnp.where(low, 0.0, q2)], axis=0).astype(q.dtype)
                dom = jnp.concatenate([jnp.where(low, do2, 0.0), jnp.where(low, 0.0, do2)], axis=0).astype(q.dtype)
                Lc = jnp.concatenate([L2[:, 0:1], L2[:, HD:HD + 1]], axis=0)
                dlc = jnp.concatenate([dl2[:, 0:1], dl2[:, HD:HD + 1]], axis=0)
                s = _dot(qs, k2, NT) + b_ref[j]
                if mask is not None:
                    s = s + mask
                p = jnp.exp(s - Lc)
                ds = p * (_dot(dom, v2, NT) - dlc)
                ds_ref[j] += ds
                dsc = ds.astype(q.dtype)
                dq_st = _dot(dsc, k2, NN)
                res.append((jnp.where(low, dq_st[:H2], dq_st[H2:]) * 0.125, _dot(dsc, qs, TN),
                            _dot(p.astype(q.dtype), dom, TN)))
            (dq_lo, dk_a, dv_a), (dq_hi, dk_b, dv_b) = res
            dq = _merge_rows(dq_lo, dq_hi, R)
            if has_prev:
                emit(sl, dq, _merge_rows(dk_a[H2:] + dk_b[:H2], dk_b[H2:], R),
                     _merge_rows(dv_a[H2:] + dv_b[:H2], dv_b[H2:], R), dk_a[:H2], dv_a[:H2])
            else:
                emit(sl, dq, _merge_rows(dk_a[:H2] + dk_b[:H2], dk_a[H2:] + dk_b[H2:], R),
                     _merge_rows(dv_a[:H2] + dv_b[:H2], dv_a[H2:] + dv_b[H2:], R), None, None)

    def first_step():
        ok = pl.program_id(nd_outer) == 0 if has_prev else pl.program_id(0) == 0
        for a in range(nd_outer if has_prev else 0):
            ok = ok & (pl.program_id(a) == 0)
        return ok

    if has_prev:
        def body(q_ref, kp_ref, kc_ref, vp_ref, vc_ref, do_ref, L_ref, dl_ref, ba_ref, bb_ref, prev_in, out_ref,
                 dsa_ref, dsb_ref, hold, new, pp):
            del prev_in
            s_id = pl.program_id(nd_outer)

            @pl.when(first_step())
            def _():
                dsa_ref[...] = jnp.zeros_like(dsa_ref)
                dsb_ref[...] = jnp.zeros_like(dsb_ref)

            @pl.when(s_id < N)
            def _():
                def emit(sl, dq, dk, dv, dk_prev, dv_prev):
                    new[:, sl] = dq
                    new[:, slice(AW + sl.start, AW + sl.stop)] = dk
                    new[:, slice(2 * AW + sl.start, 2 * AW + sl.stop)] = dv
                    pp[:, sl] = dk_prev
                    pp[:, slice(AW + sl.start, AW + sl.stop)] = dv_prev

                heads(q_ref, kp_ref, kc_ref, vp_ref, vc_ref, do_ref, L_ref, dl_ref, ba_ref, bb_ref,
                      _prev_mask(s_id, H2, T), dsa_ref, dsb_ref, emit)

            @pl.when((s_id >= 1) & (s_id < N))
            def _():
                _st(out_ref, hold[:, 0:AW], slice(0, AW))
                late = _merge_rows(jnp.zeros((H2, 2 * AW), F32), pp[...], R)
                _st(out_ref, hold[:, AW:W3] + late, slice(AW, W3))

            @pl.when(s_id == N)
            def _():
                _st(out_ref, hold[...])

            @pl.when(s_id < N)
            def _():
                hold[...] = new[...]

        cur = lambda n: jnp.minimum(n, N - 1)
        prev = lambda n: jnp.clip(n - 1, 0, N - 1)
        delayed = lambda n: jnp.maximum(n - 1, 0)
        in_specs = [geom.spec(AW, cq, cur), geom.spec(AW, ck, prev), geom.spec(AW, ck, cur),
                    geom.spec(AW, cv, prev), geom.spec(AW, cv, cur),
                    geom.spec(AW, 0, cur), geom.spec(AW, 0, cur), geom.spec(AW, 0, cur)]
        args = [qv] * 5 + [geom.view(dattn), geom.view(Lb), geom.view(dlb)]
        out_spec = geom.spec(W3, gi, delayed)
        grid = geom.outer + (N + 1,)
        scratch = [pltpu.VMEM((T, W3), F32), pltpu.VMEM((T, W3), F32), pltpu.VMEM((H2, 2 * AW), F32)]
    else:
        def body(q_ref, kc_ref, vc_ref, do_ref, L_ref, dl_ref, ba_ref, bb_ref, prev_in, out_ref, dsa_ref, dsb_ref):
            del prev_in

            @pl.when(first_step())
            def _():
                dsa_ref[...] = jnp.zeros_like(dsa_ref)
                dsb_ref[...] = jnp.zeros_like(dsb_ref)

            def emit(sl, dq, dk, dv, dk_prev, dv_prev):
                _st(out_ref, dq, sl)
                _st(out_ref, dk, slice(AW + sl.start, AW + sl.stop))
                _st(out_ref, dv, slice(2 * AW + sl.start, 2 * AW + sl.stop))

            heads(q_ref, None, kc_ref, None, vc_ref, do_ref, L_ref, dl_ref, ba_ref, bb_ref, None, dsa_ref, dsb_ref,
                  emit)

        cur = lambda n: n
        in_specs = [geom.spec(AW, cq, cur), geom.spec(AW, ck, cur), geom.spec(AW, cv, cur),
                    geom.spec(AW, 0, cur), geom.spec(AW, 0, cur), geom.spec(AW, 0, cur)]
        args = [qv] * 3 + [geom.view(dattn), geom.view(Lb), geom.view(dlb)]
        out_spec = geom.spec(W3, gi, cur)
        grid = geom.outer
        scratch = []

    whole = lambda b: pl.BlockSpec(b.shape, lambda *ids: (0, 0, 0))
    biases = [b.reshape(NH // 2, 2 * H2, b.shape[2]) for b in biases]
    in_specs += [whole(b) for b in biases]
    in_specs.append(pl.BlockSpec(memory_space=pl.ANY))
    if dqkv_prev is None:
        last, aliases = biases[0], {}
    else:
        last, aliases = geom.view(dqkv_prev), {len(in_specs) - 1: 0}
    dq, dsa, dsb = pl.pallas_call(
        body, name=f"attn_bwd{gi}", grid=grid, in_specs=in_specs,
        out_specs=[out_spec] + [whole(b) for b in biases],
        out_shape=[_sds(geom.view_shape(QKV_W), CDT)] + [_sds(b.shape, F32) for b in biases],
        scratch_shapes=scratch,
        input_output_aliases=aliases,
        compiler_params=_cp(("arbitrary",) * len(grid), vmem=56),
    )(*args, *biases, last)
    return geom.unview(dq), [d.reshape(NH, H2, d.shape[2]) for d in (dsa, dsb)]


def _h_bwd(dh, dx2, x, norm_g, scale):
    S = x.shape[0]

    def body(dh_ref, dx2_ref, x_ref, g_ref, sc_ref, gx_ref, dsh_ref, dsc_ref, dng_ref, nat_ref):
        i = pl.program_id(0)

        @pl.when(i == 0)
        def _():
            dsh_ref[...] = jnp.zeros_like(dsh_ref)
            dsc_ref[...] = jnp.zeros_like(dsc_ref)
            dng_ref[...] = jnp.zeros_like(dng_ref)

        for r in range(R16):
            nat_ref[:, r, :] = dh_ref[r]
        dh = jnp.concatenate([nat_ref[l] for l in range(R16)], axis=0)
        g = g_ref[...]
        one_sc = 1.0 + sc_ref[...]
        xv = x_ref[...]
        r1 = lax.rsqrt(jnp.mean(xv * xv, axis=-1, keepdims=True) + EPS)
        xn = xv * r1
        dhx = dh * xn
        dxn = dh * (g * one_sc)
        dx1 = r1 * (dxn - xn * jnp.mean(dxn * xn, axis=-1, keepdims=True))
        gx_ref[...] = dx2_ref[...] + dx1
        dsh_ref[...] += jnp.sum(dh, axis=0, keepdims=True)
        dsc_ref[...] += jnp.sum(dhx * g, axis=0, keepdims=True)
        dng_ref[...] += jnp.sum(dhx * one_sc, axis=0, keepdims=True)

    nat = pl.BlockSpec((TOK, D), lambda i: (i, 0))
    vec = pl.BlockSpec((1, D), lambda i: (0, 0))
    return pl.pallas_call(
        body, name="h_bwd", grid=(S // TOK,),
        in_specs=[pl.BlockSpec((R16, R16, D), lambda i: (0, i, 0)), nat, nat, vec, vec],
        out_specs=[nat, vec, vec, vec],
        out_shape=[_sds((S, D), F32), _sds((1, D), F32), _sds((1, D), F32), _sds((1, D), F32)],
        scratch_shapes=[pltpu.VMEM((R16, R16, D), F32)],
        compiler_params=_cp(("arbitrary",)),
    )(dh, dx2, x, norm_g, scale)


def _local_step(x, target, shift, scale, gate, norm_g, w_in_full, pool_w, pool_scale, wab, wpb, wout, rel_bias,
                final_g):
    st = _local_pre(x, shift, scale, norm_g, rel_bias)
    st = _local_grads(st, target, gate, w_in_full, pool_w, pool_scale, wab, wpb, wout, final_g)
    d_rel_bias = _local_bias_grads(st)
    gx, dmod, dng = _local_input_grads(st, w_in_full, norm_g, scale)
    return dict(loss=st["loss"], grad_x=gx, dw_in=st["dw_in"], dw_ab=st["dw_ab"], dw_pb=st["dw_pb"],
                dw_out=st["dw_out"], d_norm_g=dng, d_pool_w=st["dpw"], d_pool_scale=st["dps"],
                d_rel_bias=d_rel_bias, d_final_g=st["gfg"], dmod=dmod)


def _local_pre(x, shift, scale, norm_g, rel_bias, dep=None):
    S = x.shape[0]
    LS = S // R16
    geoms = [_Geom(gi, LS) for gi in range(3)]
    idx_np = [[g.bucket_index(sub) for sub in range(2)] for g in geoms]
    buckets = [[tuple(int(b) for b in np.unique(ix) if b >= 0) for ix in pair] for pair in idx_np]
    idxs = [[jnp.asarray(ix) for ix in pair] for pair in idx_np]
    biases = [[_bias_table(idxs[gi][sub], buckets[gi][sub], rel_bias, gi, sub, dep) for sub in range(2)]
              for gi in range(3)]
    h = _norm_mod(x, norm_g, shift, scale, dep).reshape(S, D)
    return dict(geoms=geoms, idxs=idxs, buckets=buckets, biases=biases, x=x, h=h, ht=_tcast(h, "h_transpose"))


def _local_grads(st, target, gate, w_in_full, pool_w, pool_scale, wab, wpb, wout, final_g):
    geoms, idxs, biases, x, h, ht = (st[k] for k in ("geoms", "idxs", "biases", "x", "h", "ht"))
    S = h.shape[0]
    LS = S // R16
    qkv, rest = _proj(h, w_in_full)

    os_, ls_ = [], []
    for gi in range(3):
        o, l = _attn_fwd(qkv, biases[gi], geoms[gi])
        os_.append(o)
        ls_.append(l)
    attn, Lb, ag, agt = _attn_combine(os_, ls_, rest)
    pt, mp, pg, pgt = _pool_fwd(rest, pool_w, pool_scale)
    merged, mt = _branch_merge(ag, pg, wab, wpb, rest)
    dx2, dmo3, loss, gfg, dgate = _out_loss(merged.reshape(R16, LS, D), wout, x, target, gate, final_g)
    dmo = dmo3.reshape(S, D)

    dya, dyp, drest, dattn, dlb, dmp, dps = _bwd_gates(dmo, wout, wab, wpb, rest, ag, pg, attn, mp, pool_scale)
    dw_out = _mm_tn(mt, dmo, n_total=D, out_blk=lambda j: j, name="dw_out")
    dw_ab = _mm_tn(agt, dya, n_total=D, out_blk=lambda j: j, name="dw_attn_br")
    dw_pb = _mm_tn(pgt, dyp, n_total=D, out_blk=lambda j: j, name="dw_pool_br")
    drest, dpw = _pool_bwd(dmp, pt, pool_w, drest)

    dqkv = None
    dsums = []
    for gi in range(3):
        dqkv, dsum = _attn_bwd(qkv, dattn, Lb, dlb, biases[gi], geoms[gi], dqkv)
        dsums.append(dsum)

    dw_in = _mm_tn(ht, dqkv, n_total=8192, out_blk=lambda j: j, name="dw_in_qkv")
    dw_in = _mm_tn(ht, drest, n_total=8192, out_blk=_rest_blk, name="dw_in_rest", prev=dw_in)
    return dict(st, loss=loss[0, 0], dw_in=dw_in, dw_ab=dw_ab, dw_pb=dw_pb, dw_out=dw_out, dpw=dpw, dps=dps, gfg=gfg,
                dgate=dgate, dsums=dsums, dqkv=dqkv, drest=drest, dx2=dx2)


def _local_bias_grads(st, dep=None):
    drb = [_bias_grad(st["dsums"][gi][0], st["idxs"][gi][0], st["buckets"][gi][0], gi, 0, dep if gi == 0 else None) +
           _bias_grad(st["dsums"][gi][1], st["idxs"][gi][1], st["buckets"][gi][1], gi, 1) for gi in range(3)]
    return jnp.concatenate(drb, axis=1)


def _local_input_grads(st, w_in_full, norm_g, scale, dep=None):
    S = st["h"].shape[0]
    LS = S // R16
    dh = _dh(st["dqkv"], st["drest"], w_in_full, dep)
    gx, dsh, dsc, dng = _h_bwd(dh.reshape(R16, LS, D), st["dx2"], st["x"], norm_g, scale)
    return gx, jnp.concatenate([dsh, dsc, st["dgate"]], axis=1), dng


def _my_place():
    return lax.axis_index("x"), lax.axis_index("y"), lax.axis_index("c")


def _block_of(ref, axis, idx, width):
    if axis is None:
        return ref.at[idx]
    start = pl.multiple_of(idx * width, width)
    if axis == 0:
        return ref.at[pl.ds(start, width)]
    return ref.at[:, pl.ds(start, width)]


_SEM = pl.BlockSpec(memory_space=pltpu.SEMAPHORE)
_EFFECT = pltpu.SideEffectType.DATAFLOW_SIDE_EFFECTING


def _split_start(name, srcs, lands, ncopy, plan, dep=None):
    arrs = list(srcs) + list(lands)
    na, ns = len(arrs), len(srcs)
    dep_specs, dep_args = _after(dep)

    def body(*refs):
        send_sems, recv_sems, token = refs[na + len(dep_args)], refs[na + len(dep_args) + 1], refs[-1]
        for k, (s, d, dev) in enumerate(plan(refs[:ns], refs[ns:na])):
            pltpu.make_async_remote_copy(src_ref=s, dst_ref=d, send_sem=send_sems.at[k], recv_sem=recv_sems.at[k],
                                         device_id=dev, device_id_type=MESH).start()
        token[...] = jnp.zeros_like(token)

    outs = pl.pallas_call(
        body, name=name,
        out_shape=(pltpu.SemaphoreType.DMA((ncopy,)), pltpu.SemaphoreType.DMA((ncopy,)),
                   *[pltpu.HBM(a.shape, a.dtype) for a in arrs], _sds((8, 128), F32)),
        in_specs=[_HBM] * na + dep_specs,
        out_specs=(_SEM, _SEM, *[_HBM] * na, pl.BlockSpec(memory_space=pltpu.VMEM)),
        input_output_aliases={i: 2 + i for i in range(na)},
        compiler_params=pltpu.CompilerParams(has_side_effects=_EFFECT),
    )(*[pltpu.with_memory_space_constraint(a, pltpu.HBM) for a in arrs], *dep_args)
    return (outs[0], outs[1]), list(outs[2:2 + ns]), list(outs[2 + ns:2 + na]), outs[-1]


def _split_wait(name, sems, srcs, lands, plan, after):
    arrs = list(srcs) + list(lands)
    na, ns = len(arrs), len(srcs)
    after = list(after) if isinstance(after, (list, tuple)) else [after]

    def body(*refs):
        send_sems, recv_sems = refs[na], refs[na + 1]
        for k, (s, d, dev) in enumerate(plan(refs[:ns], refs[ns:na])):
            cp = pltpu.make_async_remote_copy(src_ref=s, dst_ref=d, send_sem=send_sems.at[k], recv_sem=recv_sems.at[k],
                                              device_id=dev, device_id_type=MESH)
            cp.wait_send()
            cp.wait_recv()

    outs = pl.pallas_call(
        body, name=name, out_shape=[pltpu.HBM(a.shape, a.dtype) for a in arrs],
        in_specs=[_HBM] * na + [_SEM, _SEM] + [pl.BlockSpec(memory_space=pl.ANY)] * len(after),
        out_specs=[_HBM] * na,
        input_output_aliases={i: i for i in range(na)},
        compiler_params=pltpu.CompilerParams(has_side_effects=_EFFECT),
    )(*arrs, sems[0], sems[1], *after)
    return list(outs[:ns]), list(outs[ns:])


def _gather_chips_plan(layout):
    def plan(src, land):
        x, y, c = _my_place()
        me = 4 * x + 2 * y + c
        mine = [_block_of(land[p], layout[p][0], me, layout[p][1]) for p in range(len(land))]
        return [(mine[p], mine[p], (cx, cy, c)) for p in range(len(land)) for cx, cy in [(1 - x, y), (x, 1 - y)]]
    return plan


def _to_all_plan(src, land):
    x, y, c = _my_place()
    me = 4 * x + 2 * y + c
    flip = lambda v, f: 1 - v if f else v
    peers = [(flip(x, fx), flip(y, fy), flip(c, fc)) for fx in (0, 1) for fy in (0, 1) for fc in (0, 1)][1:]
    return [(land[p].at[me], land[p].at[me], peer) for p in range(len(land)) for peer in peers]


def _pair_plan(layout):
    def plan(src, land):
        x, y, c = _my_place()
        n = len(layout)
        return [(_block_of(src[p], layout[p][0], 2 * q + (1 - c), layout[p][1]), land[p].at[q], (x, y, 1 - c))
                for p in range(n) for q in range(4)] + [(src[p], land[p], (x, y, 1 - c)) for p in range(n, len(src))]
    return plan


def _chips_plan(n):
    def plan(src, land):
        x, y, c = _my_place()
        chips = [(1 - x, y), (x, 1 - y), (1 - x, 1 - y)]
        return [(src[p].at[2 * cx + cy], land[p].at[2 * x + y], (cx, cy, c)) for p in range(n) for cx, cy in chips] + \
               [(src[p], land[p].at[2 * x + y], (cx, cy, c)) for p in range(n, len(src)) for cx, cy in chips]
    return plan


def _pair_share(gathered):
    n = len(gathered)

    def body(*refs):
        land = refs[n:2 * n]
        send_sems, recv_sems, fwd_send, fwd_recv = refs[2 * n:]
        x, y, c = _my_place()
        k_dg = 3 - (2 * x + y)
        south = c == 0
        pass_on = 4 * jnp.where(south, 1 - x, x) + 2 * jnp.where(south, y, 1 - y) + c
        to = (jnp.where(south, x, 1 - x), jnp.where(south, 1 - y, y), c)

        def forward(p):
            return pltpu.make_async_remote_copy(
                src_ref=land[p].at[pass_on], dst_ref=land[p].at[pass_on], send_sem=fwd_send.at[p],
                recv_sem=fwd_recv.at[p], device_id=to, device_id_type=MESH)

        def to_sibling(p, k):
            blk = land[p].at[2 * k + c]
            return pltpu.make_async_remote_copy(
                src_ref=blk, dst_ref=blk, send_sem=send_sems.at[4 * p + k], recv_sem=recv_sems.at[4 * p + k],
                device_id=(x, y, 1 - c), device_id_type=MESH)

        for p in range(n):
            forward(p).start()
        for p in range(n):
            for k in range(4):
                @pl.when(k != k_dg)
                def _():
                    to_sibling(p, k).start()
        for p in range(n):
            dg = land[p].at[2 * k_dg + c]
            pltpu.make_async_remote_copy(src_ref=dg, dst_ref=dg, send_sem=fwd_send.at[p], recv_sem=fwd_recv.at[p],
                                         device_id=to, device_id_type=MESH).wait_recv()
            for k in range(4):
                @pl.when(k == k_dg)
                def _():
                    to_sibling(p, k).start()
        for p in range(n):
            forward(p).wait_send()
            for k in range(4):
                to_sibling(p, k).wait_send()
                blk = land[p].at[2 * k + 1 - c]
                pltpu.make_async_remote_copy(
                    src_ref=blk, dst_ref=blk, send_sem=send_sems.at[4 * p + k], recv_sem=recv_sems.at[4 * p + k],
                    device_id=(x, y, 1 - c), device_id_type=MESH).wait_recv()

    return pl.pallas_call(
        body, name="gather_pair_share", in_specs=[_HBM] * n, out_specs=[_HBM] * n,
        out_shape=[_sds(g.shape, g.dtype) for g in gathered],
        input_output_aliases={p: p for p in range(n)},
        scratch_shapes=[pltpu.SemaphoreType.DMA((4 * n,)), pltpu.SemaphoreType.DMA((4 * n,)),
                        pltpu.SemaphoreType.DMA((n,)), pltpu.SemaphoreType.DMA((n,))],
    )(*gathered)


def _place_slot(arrs, slot, n_slots, name, out_dtype=None):
    n = len(arrs)

    def body(s_ref, *refs):
        for p in range(n):
            refs[n + p][...] = refs[p][...].astype(refs[n + p].dtype)

    def spec(a):
        tail = a.shape[-2:]
        return pl.BlockSpec((None,) + tail, lambda i, s_ref: (s_ref[0], 0, 0))

    in_specs = [spec(a) if a.ndim == 3 else pl.BlockSpec(a.shape, lambda i, s_ref: (0, 0)) for a in arrs]
    return pl.pallas_call(
        body, name=name,
        grid_spec=pltpu.PrefetchScalarGridSpec(num_scalar_prefetch=1, grid=(1,), in_specs=in_specs,
                                               out_specs=[spec(a) for a in arrs]),
        out_shape=[_sds((n_slots,) + a.shape[-2:], out_dtype or a.dtype) for a in arrs],
        compiler_params=_cp(("arbitrary",)),
    )(slot, *arrs)


def _cols_from_blocks(w8, name):
    _, R, C = w8.shape

    def body(w_ref, o_ref):
        for e in range(8):
            o_ref[:, C * e:C * (e + 1)] = w_ref[e]

    return pl.pallas_call(body, name=name, out_shape=_sds((R, 8 * C), w8.dtype), compiler_params=_cp())(w8)


def _all_gather(srcs, out_shapes, layout, name):
    n = len(srcs)

    def body(*refs):
        src, out = refs[:n], refs[n:2 * n]
        send_sems, recv_sems, local_sems = refs[2 * n:]
        x, y, c = _my_place()
        me, sibling = (x, y, c), (x, y, 1 - c)
        chips = [(1 - x, y), (x, 1 - y), (1 - x, 1 - y)]

        def region(p, dev):
            px, py, pc = dev
            return _block_of(out[p], layout[p][0], 4 * px + 2 * py + pc, layout[p][1])

        def copy(p, k, block, to, src_ref=None):
            return pltpu.make_async_remote_copy(
                src_ref=region(p, block) if src_ref is None else src_ref, dst_ref=region(p, block),
                send_sem=send_sems.at[p * 7 + k], recv_sem=recv_sems.at[p * 7 + k],
                device_id=to, device_id_type=MESH)

        mine = [pltpu.make_async_copy(src[p], region(p, me), local_sems.at[p]) for p in range(n)]
        for cp in mine:
            cp.start()
        first = []
        for p in range(n):
            first.append(copy(p, 0, me, sibling, src[p]))
            first += [copy(p, 1 + j, me, (*chip, c), src[p]) for j, chip in enumerate(chips)]
        for cp in first:
            cp.start()
        passed = []
        for j, chip in enumerate(chips):
            for p in range(n):
                copy(p, 1 + j, (*chip, c), me).wait_recv()
                fwd = copy(p, 4 + j, (*chip, c), sibling)
                fwd.start()
                passed.append(fwd)
        for p in range(n):
            copy(p, 0, sibling, me).wait_recv()
            for j, chip in enumerate(chips):
                copy(p, 4 + j, (*chip, 1 - c), me).wait_recv()
        for cp in first + passed:
            cp.wait_send()
        for cp in mine:
            cp.wait()

    return pl.pallas_call(
        body, name=name, in_specs=[_HBM] * n, out_specs=[_HBM] * n,
        out_shape=[_sds(s, a.dtype) for s, a in zip(out_shapes, srcs)],
        scratch_shapes=[pltpu.SemaphoreType.DMA((7 * n,)), pltpu.SemaphoreType.DMA((7 * n,)),
                        pltpu.SemaphoreType.DMA((n,))],
    )(*srcs)


def _pair_add(dw, recv, axis, width, cidx, name):
    if axis == 0:
        tr = min(width, 256)
        C = dw.shape[1]
        grid = (4, width // tr)
        nb = width // tr
        own = pl.BlockSpec((tr, C), lambda q, i, c_ref: ((2 * q + c_ref[0]) * nb + i, 0))
        oth = pl.BlockSpec((None, tr, C), lambda q, i, c_ref: (q, i, 0))
    else:
        R = dw.shape[0]
        tr = min(R, 256)
        grid = (4, R // tr)
        own = pl.BlockSpec((tr, width), lambda q, i, c_ref: (i, 2 * q + c_ref[0]))
        oth = pl.BlockSpec((None, tr, width), lambda q, i, c_ref: (q, i, 0))

    def body(c_ref, a_ref, b_ref, o_ref):
        o_ref[...] = (a_ref[...] + b_ref[...]).astype(o_ref.dtype)

    return pl.pallas_call(
        body, name=name,
        grid_spec=pltpu.PrefetchScalarGridSpec(num_scalar_prefetch=1, grid=grid, in_specs=[own, oth], out_specs=oth),
        out_shape=_sds(recv.shape, CDT), compiler_params=_cp(("parallel", "parallel")),
    )(cidx, dw, recv)


def _add_pairs(xs, ys, name):
    n = len(xs)

    def body(*refs):
        for p in range(n):
            refs[2 * n + p][...] = refs[p][...] + refs[n + p][...]

    return pl.pallas_call(body, name=name, out_shape=[_sds(a.shape, F32) for a in xs])(*xs, *ys)


def _adaln_part(c_all, w_ada, b_part, dep=None):
    dep_specs, dep_args = _after(dep)
    vmem = pl.BlockSpec(memory_space=pltpu.VMEM)

    def body(c_ref, w_ref, b_ref, *rest):
        rest[-1][...] = _dot(c_ref[...].astype(CDT), w_ref[...].astype(CDT), NN) + b_ref[...]

    return pl.pallas_call(body, name="adaln_part", in_specs=[vmem, vmem, vmem] + dep_specs, out_specs=vmem,
                          out_shape=_sds((8, w_ada.shape[1]), F32),
                          compiler_params=_cp())(c_all, w_ada, b_part, *dep_args)


def _adam_math(w, g, m, v):
    m = ADAM_B1 * m + (1.0 - ADAM_B1) * g
    v = ADAM_B2 * v + (1.0 - ADAM_B2) * (g * g)
    m_hat = m / (1.0 - ADAM_B1 ** ADAM_STEP)
    v_hat = v / (1.0 - ADAM_B2 ** ADAM_STEP)
    delta = -ADAM_LR * (m_hat / (jnp.sqrt(v_hat) + ADAM_EPS) + ADAM_WD * w)
    return delta, m, v


def _adam(gparts, w, m, v, name):
    P, R, C = gparts.shape
    tr = R if R <= 256 else 256

    def body(g_ref, w_ref, m_ref, v_ref, go_ref, d_ref, mo_ref, vo_ref):
        g = g_ref[0].astype(F32)
        for i in range(1, P):
            g = g + g_ref[i].astype(F32)
        d, mn, vn = _adam_math(w_ref[...], g, m_ref[...], v_ref[...])
        go_ref[...] = g
        d_ref[...] = d
        mo_ref[...] = mn
        vo_ref[...] = vn

    t = pl.BlockSpec((tr, C), lambda i: (i, 0))
    return pl.pallas_call(
        body, name=name, grid=(R // tr,),
        in_specs=[pl.BlockSpec((P, tr, C), lambda i: (0, i, 0)), t, t, t], out_specs=[t, t, t, t],
        out_shape=[_sds((R, C), F32)] * 4, compiler_params=_cp(("parallel",)),
    )(gparts, w, m, v)


def _sum_parts(gparts, name):
    P, R, C = gparts.shape

    def body(g_ref, o_ref):
        g = g_ref[0]
        for i in range(1, P):
            g = g + g_ref[i]
        o_ref[...] = g

    return pl.pallas_call(body, name=name, out_shape=_sds((R, C), F32))(gparts)


def _adam_w_ada(c_t, dm, w, m, v):
    R, C = w.shape

    def body(c_ref, dm_ref, w_ref, m_ref, v_ref, go_ref, d_ref, mo_ref, vo_ref):
        g = c_ref[:, 0:1] * dm_ref[0:1, :]
        for b in range(1, 8):
            g = g + c_ref[:, b:b + 1] * dm_ref[b:b + 1, :]
        d, mn, vn = _adam_math(w_ref[...], g, m_ref[...], v_ref[...])
        go_ref[...] = g
        d_ref[...] = d
        mo_ref[...] = mn
        vo_ref[...] = vn

    return pl.pallas_call(body, name="adam_w_ada", out_shape=[_sds((R, C), F32)] * 4,
                          compiler_params=_cp())(c_t, dm, w, m, v)


def kernel(x, c, norm_g, w_ada, b_ada, w_in, pool_w, pool_scale, w_attn_br, w_pool_br, w_out, rel_bias, final_g, loss_target, m_norm_g, m_w_ada, m_b_ada, m_w_in, m_pool_w, m_pool_scale, m_w_attn_br, m_w_pool_br, m_w_out, m_rel_bias, m_final_g, v_norm_g, v_w_ada, v_b_ada, v_w_in, v_pool_w, v_pool_scale, v_w_attn_br, v_w_pool_br, v_w_out, v_rel_bias, v_final_g):
    S = x.shape[1]
    px, py, pc = _my_place()
    me = 4 * px + 2 * py + pc
    cidx = jnp.reshape(pc, (1,)).astype(jnp.int32)

    layout = [(1, 1024), (1, 128), (1, 128), (0, 128)]
    final_g2 = final_g.reshape(1, D)
    me1 = jnp.reshape(me, (1,)).astype(jnp.int32)

    (c_rows,) = _all_gather([jnp.broadcast_to(c, (8, D))], [(64, D)], [(0, 8)], "gather_c")
    c_all = c_rows[::8]

    b_part = lax.dynamic_slice(b_ada, (0, me * 384), (1, 384))
    mod_part = _adaln_part(c_all, w_ada[0], b_part)
    (mod_all,) = _all_gather([mod_part], [(8, 8, 384)], [(None, 1)], "gather_mod")
    mod = lax.dynamic_index_in_dim(mod_all, me, axis=1, keepdims=False).reshape(1, 3 * D)
    shift, scale, gate = mod[:, :D], mod[:, D:2 * D], mod[:, 2 * D:]

    gplan = _gather_chips_plan([(None, 1)] * 4)
    lands = _place_slot([w_in[0], w_attn_br[0], w_pool_br[0], w_out[0]], me1, 8, "place_own_shards", CDT)
    g_sems, _, lands, tok = _split_start("gather_chips_start", [], lands, 8, gplan, dep=mod_all)
    st = _local_pre(x[0], shift, scale, norm_g, rel_bias, dep=tok)
    _, lands = _split_wait("gather_chips_wait", g_sems, [], lands, gplan,
                           [st["ht"]] + [b for pair in st["biases"] for b in pair])
    w_in_full, wab8, wpb8, wout8 = _pair_share(lands)
    wab, wpb = _cols_from_blocks(wab8, "w_attn_br_cols"), _cols_from_blocks(wpb8, "w_pool_br_cols")
    wout = wout8.reshape(D, D)

    st = _local_grads(st, loss_target[0], gate, w_in_full, pool_w[0], pool_scale, wab, wpb, wout, final_g2)

    early = [st["dpw"].reshape(AW, 128), st["dps"], st["gfg"], jnp.broadcast_to(st["loss"], (1, 128))]
    dws = [st["dw_in"], st["dw_ab"], st["dw_pb"], st["dw_out"]]
    pplan = _pair_plan(layout)
    pair_shapes = [(4, w, a.shape[1]) if ax == 0 else (4, a.shape[0], w) for a, (ax, w) in zip(dws, layout)]
    p_sems, srcs, recv, tok = _split_start(
        "reduce_pair_start", dws + early, [lax.empty(s, F32) for s in pair_shapes + [a.shape for a in early]], 20,
        pplan)
    d_rel_bias = _local_bias_grads(st, dep=tok)
    srcs, recv = _split_wait("reduce_pair_wait", p_sems, srcs, recv, pplan, d_rel_bias)
    parts = [_pair_add(dw, r, ax, w, cidx, f"pair_add{i}")
             for i, (dw, r, (ax, w)) in enumerate(zip(srcs[:4], recv[:4], layout))]
    parts += _add_pairs(srcs[4:], recv[4:], "pair_add_smalls")

    slot1 = jnp.reshape(2 * px + py, (1,)).astype(jnp.int32)
    cplan = _chips_plan(4)
    c_sems, parts, lands, tok = _split_start("reduce_chips_start", parts,
                                             _place_slot(parts, slot1, 4, "place_own_partials"), 24, cplan)
    grad_x, dmod, d_norm_g = _local_input_grads(st, w_in_full, norm_g, scale, dep=tok)

    late = _place_slot([dmod, d_norm_g, d_rel_bias], me1, 8, "place_own_late")
    l_sems, _, late, tok = _split_start("gather_late_start", [], late, 21, _to_all_plan)

    _, (g_in, g_ab, g_pb, g_out, g_pw, g_ps, g_fg, g_loss) = _split_wait(
        "reduce_chips_wait", c_sems, parts, lands, cplan, tok)
    o_w_in = [o[None] for o in _adam(g_in, w_in[0], m_w_in[0], v_w_in[0], "adam_w_in")]
    o_w_ab = [o[None] for o in _adam(g_ab, w_attn_br[0], m_w_attn_br[0], v_w_attn_br[0], "adam_w_attn_br")]
    o_w_pb = [o[None] for o in _adam(g_pb, w_pool_br[0], m_w_pool_br[0], v_w_pool_br[0], "adam_w_pool_br")]
    o_w_out = [o[None] for o in _adam(g_out, w_out[0], m_w_out[0], v_w_out[0], "adam_w_out")]
    o_pool_w = [o.reshape(pool_w.shape) for o in _adam(g_pw, pool_w.reshape(AW, 128), m_pool_w.reshape(AW, 128),
                                                       v_pool_w.reshape(AW, 128), "adam_pool_w")]
    o_pool_scale = _adam(g_ps, pool_scale, m_pool_scale, v_pool_scale, "adam_pool_scale")
    o_final_g = [o.reshape(D) for o in _adam(g_fg, final_g2, m_final_g.reshape(1, D), v_final_g.reshape(1, D),
                                             "adam_final_g")]
    loss_row = _sum_parts(g_loss, "sum_loss")
    loss = loss_row[0, 0]

    _, (dmod_all, ng_all, rb_all) = _split_wait("gather_late_wait", l_sems, [], late, _to_all_plan,
                                                [o_w_in[0], o_w_out[0], o_pool_w[0], loss_row])
    o_norm_g = _adam(ng_all, norm_g, m_norm_g, v_norm_g, "adam_norm_g")
    o_rel_bias = _adam(rb_all, rel_bias, m_rel_bias, v_rel_bias, "adam_rel_bias")
    o_b_ada = _adam(dmod_all, b_ada, m_b_ada, v_b_ada, "adam_b_ada")
    dm_mine = lax.dynamic_slice(dmod_all.reshape(8, 3 * D), (0, me * 384), (8, 384))
    o_w_ada = [o[None] for o in _adam_w_ada(c_all.T, dm_mine, w_ada[0], m_w_ada[0], v_w_ada[0])]

    per_w = [o_norm_g, o_w_ada, o_b_ada, o_w_in, o_pool_w, o_pool_scale, o_w_ab, o_w_pb, o_w_out, o_rel_bias, o_final_g]
    outs = [loss, grad_x[None]]
    for k in range(4):
        outs += [o[k] for o in per_w]
    return tuple(outs)
```

```python
import functools
import math

import numpy as np
import jax
import jax.numpy as jnp
from jax import lax
from jax.experimental import pallas as pl
from jax.experimental.pallas import tpu as pltpu

F32 = jnp.float32
CDT = jnp.bfloat16
D = 1024
HD = 64
NH = 8
AW = 512
QKV_W = 4608
REST_W = 3584
R16 = 16
EPS = 1e-6
NEG = -1e30
N_BACK = 128
ADAM_LR, ADAM_B1, ADAM_B2, ADAM_EPS, ADAM_WD, ADAM_STEP = 0.001, 0.9, 0.999, 1e-08, 0.01, 10
MESH = pl.DeviceIdType.MESH
VMEM_MB = 1024 * 1024


def _cp(sem=None, vmem=48, **kw):
    if sem is not None:
        kw["dimension_semantics"] = sem
    return pltpu.CompilerParams(vmem_limit_bytes=vmem * VMEM_MB, **kw)


def _sds(shape, dtype):
    return jax.ShapeDtypeStruct(tuple(shape), dtype)


def _rest_blk(jj):
    return jnp.where(jj == 0, 9, jnp.where(jj == 6, 10, jj + 10))


def _sigmoid(z):
    return 0.5 * jnp.tanh(0.5 * z) + 0.5


def _dot(a, b, dims):
    return lax.dot_general(a, b, (dims, ((), ())), preferred_element_type=F32)


NN = ((1,), (0,))
NT = ((1,), (1,))
TN = ((0,), (0,))


def _ld(ref):
    if len(ref.shape) == 2:
        return ref[...]
    return jnp.concatenate([ref[a] for a in range(ref.shape[0])], axis=0)


def _st(ref, val, cols=None):
    val = val.astype(ref.dtype)
    if len(ref.shape) == 2:
        if cols is None:
            ref[...] = val
        else:
            ref[:, cols] = val
        return
    rows = ref.shape[1]
    for a in range(ref.shape[0]):
        if cols is None:
            ref[a] = val[a * rows:(a + 1) * rows]
        else:
            ref[a, :, cols] = val[a * rows:(a + 1) * rows]


REST_COLS = ((9, 10), (11, 16), (10, 11))
_HBM = pl.BlockSpec(memory_space=pltpu.HBM)


def _resident(shape):
    return pl.BlockSpec(shape, lambda *ids: (0,) * len(shape), pipeline_mode=pl.Buffered(1))


def _w512(w_ref, j):
    return w_ref[j // 2, :, 512 * (j % 2):512 * (j % 2 + 1)]


def _proj(h, w):
    S = h.shape[0]
    tm = 256

    def body(a_ref, w_ref, q_ref, r_ref):
        a = a_ref[...]
        for j in range(9):
            q_ref[:, 512 * j:512 * (j + 1)] = _dot(a, _w512(w_ref, j), NN).astype(q_ref.dtype)
        off = 0
        for lo, hi in REST_COLS:
            for j in range(lo, hi):
                r_ref[:, off:off + 512] = _dot(a, _w512(w_ref, j), NN)
                off += 512

    return pl.pallas_call(
        body, name="proj", grid=(S // tm,),
        in_specs=[pl.BlockSpec((tm, D), lambda i: (i, 0)), _resident((8, D, 1024))],
        out_specs=[pl.BlockSpec((tm, QKV_W), lambda i: (i, 0)), pl.BlockSpec((tm, REST_W), lambda i: (i, 0))],
        out_shape=[_sds((S, QKV_W), CDT), _sds((S, REST_W), F32)],
        compiler_params=_cp(("parallel",), vmem=56),
    )(h, w)


def _after(dep):
    return ([], []) if dep is None else ([pl.BlockSpec(memory_space=pl.ANY)], [dep])


def _dh(dqkv, drest, w, dep=None):
    S = dqkv.shape[0]
    tm = 256
    dep_specs, dep_args = _after(dep)

    def body(a_ref, b_ref, w_ref, *rest):
        o_ref = rest[-1]
        acc = _dot(a_ref[:, 0:512], _w512(w_ref, 0), NT)
        for j in range(1, 9):
            acc = acc + _dot(a_ref[:, 512 * j:512 * (j + 1)], _w512(w_ref, j), NT)
        off = 0
        for lo, hi in REST_COLS:
            for j in range(lo, hi):
                acc = acc + _dot(b_ref[:, off:off + 512], _w512(w_ref, j), NT)
                off += 512
        o_ref[...] = acc

    return pl.pallas_call(
        body, name="dh", grid=(S // tm,),
        in_specs=[pl.BlockSpec((tm, QKV_W), lambda i: (i, 0)), pl.BlockSpec((tm, REST_W), lambda i: (i, 0)),
                  _resident((8, D, 1024))] + dep_specs,
        out_specs=pl.BlockSpec((tm, D), lambda i: (i, 0)),
        out_shape=_sds((S, D), F32),
        compiler_params=_cp(("parallel",), vmem=56),
    )(dqkv, drest, w, *dep_args)


def _mm_tn(at, b, *, n_total, out_blk, name, prev=None, tn=512):
    R, S = at.shape
    nb = b.shape[1] // tn

    def body(*refs):
        refs[-1][...] = _dot(refs[0][...], refs[1][...], NN)

    in_specs = [_resident((R, S)), pl.BlockSpec((S, tn), lambda j: (0, j))]
    args = [at, b]
    aliases = {}
    if prev is not None:
        in_specs.append(pl.BlockSpec(memory_space=pl.ANY))
        args.append(prev)
        aliases = {2: 0}
    return pl.pallas_call(
        body, name=name, grid=(nb,), in_specs=in_specs,
        out_specs=pl.BlockSpec((R, tn), lambda j: (0, out_blk(j))),
        out_shape=_sds((R, n_total), F32),
        input_output_aliases=aliases,
        compiler_params=_cp(("parallel",)),
    )(*args)


def _tcast(a, name):
    M, C = a.shape
    tm = 512

    def body(a_ref, o_ref):
        o_ref[...] = a_ref[...].astype(F32).T.astype(o_ref.dtype)

    return pl.pallas_call(
        body, name=name, grid=(M // tm,),
        in_specs=[pl.BlockSpec((tm, C), lambda i: (i, 0))],
        out_specs=pl.BlockSpec((C, tm), lambda i: (0, i)),
        out_shape=_sds((C, M), CDT), compiler_params=_cp(("parallel",)),
    )(a)


TOK = R16 * R16


def _swap16():
    i = lax.broadcasted_iota(jnp.int32, (TOK, TOK), 0)
    j = lax.broadcasted_iota(jnp.int32, (TOK, TOK), 1)
    return jnp.where(j == jnp.bitwise_and(i, 15) * 16 + jnp.right_shift(i, 4), 1.0, 0.0).astype(CDT)


def _norm_mod(x, norm_g, shift, scale, dep=None):
    S = x.shape[0]
    dep_specs, dep_args = _after(dep)

    def body(x_ref, g_ref, sh_ref, sc_ref, *rest):
        xv = x_ref[...]
        xn = xv * lax.rsqrt(jnp.mean(xv * xv, axis=-1, keepdims=True) + EPS)
        h = (xn * (g_ref[...] * (1.0 + sc_ref[...])) + sh_ref[...]).astype(CDT)
        _st(rest[-1], _dot(_swap16(), h, NN))

    vec = pl.BlockSpec((1, D), lambda i: (0, 0))
    return pl.pallas_call(
        body, name="norm_mod", grid=(S // TOK,),
        in_specs=[pl.BlockSpec((TOK, D), lambda i: (i, 0)), vec, vec, vec] + dep_specs,
        out_specs=pl.BlockSpec((R16, R16, D), lambda i: (0, i, 0)),
        out_shape=_sds((R16, S // R16, D), CDT), compiler_params=_cp(("parallel",)),
    )(x, norm_g, shift, scale, *dep_args)


def _t5_bucket_np(n):
    nf = np.maximum(n, 1).astype(np.float32)
    large = 16 + (np.log(nf / np.float32(16)) / np.float32(math.log(2048 / 16)) * np.float32(16)).astype(np.int32)
    large = np.minimum(large, 31)
    return np.where(n < 16, n, large).astype(np.int32)


class _Geom:
    def __init__(self, gi, LS):
        self.gi, self.LS = gi, LS
        self.dil = (1, 4, 16)[gi]
        if gi == 0:
            self.lead, self.rows, self.n_tiles, self.has_prev = (R16,), 16, LS // 16, True
            a = np.arange(R16)[:, None]
            ll = np.arange(16)[None, :]
            self.pos = (16 * ll + a).reshape(-1)
        elif gi == 1:
            self.lead, self.rows, self.n_tiles, self.has_prev = (4,), 64, LS // 64, True
            a = np.arange(4)[:, None]
            ll = np.arange(64)[None, :]
            self.pos = (4 * ll + a).reshape(-1)
        else:
            self.lead, self.rows, self.n_tiles, self.has_prev = (), LS, 1, False
            self.pos = np.arange(LS)
        self.T = self.pos.shape[0]
        self.Tk = 2 * self.T if self.has_prev else self.T
        self.outer = {0: (), 1: (4,), 2: (R16,)}[gi]

    def view(self, arr):
        S, C = arr.shape
        if self.gi == 1:
            return arr.reshape(4, 4, self.LS, C)
        return arr.reshape(R16, self.LS, C)

    def view_shape(self, C):
        return (4, 4, self.LS, C) if self.gi == 1 else (R16, self.LS, C)

    def unview(self, arr):
        return arr.reshape(R16 * self.LS, arr.shape[-1])

    def spec(self, W, cblk, tile):
        if self.gi == 0:
            return pl.BlockSpec((R16, 16, W), lambda n: (0, tile(n), cblk))
        if self.gi == 1:
            return pl.BlockSpec((4, None, 64, W), lambda r, n: (0, r, tile(n), cblk))
        return pl.BlockSpec((None, self.LS, W), lambda r: (r, 0, cblk))

    def halves(self):
        R = self.rows
        idx = np.arange(self.T).reshape(-1, R)
        return idx[:, :R // 2].reshape(-1), idx[:, R // 2:].reshape(-1)

    def bucket_index(self, sub):
        lo, hi = self.halves()
        pos = self.pos
        if sub == 0:
            pq = pos[lo]
            pk = np.concatenate([pos[hi] - self.T, pos[lo]]) if self.has_prev else np.concatenate([pos[lo], pos[hi]])
        else:
            pq = pos[hi]
            pk = np.concatenate([pos[lo], pos[hi]])
        dist = pq[:, None] - pk[None, :]
        ok = (dist >= 0) & (dist <= N_BACK)
        bucket = _t5_bucket_np(np.clip(dist, 0, N_BACK) * self.dil)
        return np.where(ok, bucket, -1).astype(np.int32)


def _bias_table(idx, buckets, rel_bias, gi, sub, dep=None):
    T, Tk = idx.shape
    tr = 32
    dep_specs, dep_args = _after(dep)

    def body(idx_ref, rb_ref, *rest):
        h = pl.program_id(0)
        ix = idx_ref[...]
        acc = jnp.full(ix.shape, NEG, F32)
        for b in buckets:
            acc = jnp.where(ix == b, rb_ref[b, gi * NH + h], acc)
        rest[-1][...] = acc

    return pl.pallas_call(
        body, name=f"bias_table{gi}{'ab'[sub]}", grid=(NH, T // tr),
        in_specs=[pl.BlockSpec((tr, Tk), lambda h, i: (i, 0)), pl.BlockSpec(memory_space=pltpu.SMEM)] + dep_specs,
        out_specs=pl.BlockSpec((None, tr, Tk), lambda h, i: (h, i, 0)),
        out_shape=_sds((NH, T, Tk), F32), compiler_params=_cp(("parallel", "parallel")),
    )(idx, rel_bias, *dep_args)


def _bias_grad(dsum, idx, buckets, gi, sub, dep=None):
    _, T, Tk = dsum.shape
    dep_specs, dep_args = _after(dep)

    def body(ds_ref, idx_ref, *rest):
        o_ref, r_ref = rest[-2], rest[-1]
        r_ref[...] = jnp.zeros_like(r_ref)
        ix, ds = idx_ref[...], ds_ref[...]
        for b in buckets:
            r_ref[b:b + 1, :] = jnp.sum(jnp.where(ix == b, ds, 0.0), axis=0, keepdims=True)
        o_ref[...] = jnp.broadcast_to(jnp.sum(r_ref[...], axis=1, keepdims=True), (32, 128))

    out = pl.pallas_call(
        body, name=f"bias_grad{gi}{'ab'[sub]}", grid=(NH,),
        in_specs=[pl.BlockSpec((None, T, Tk), lambda h: (h, 0, 0)), pl.BlockSpec((T, Tk), lambda h: (0, 0))] + dep_specs,
        out_specs=pl.BlockSpec((None, 32, 128), lambda h: (h, 0, 0)),
        out_shape=_sds((NH, 32, 128), F32), scratch_shapes=[pltpu.VMEM((32, Tk), F32)],
        compiler_params=_cp(("parallel",)),
    )(dsum, idx, *dep_args)
    return out[:, :, 0].T


def _prev_mask(n, T, Tk):
    col = lax.broadcasted_iota(jnp.int32, (1, Tk), 1)
    return jnp.where((col < T) & (n == 0), NEG, 0.0).astype(F32)


def _split_rows(v, R):
    A, half = v.shape[0] // R, R // 2
    via_f32 = v.dtype != F32 and half % 16 != 0
    w = v.astype(F32) if via_f32 else v
    lo = jnp.concatenate([w[a * R:a * R + half] for a in range(A)], axis=0)
    hi = jnp.concatenate([w[a * R + half:(a + 1) * R] for a in range(A)], axis=0)
    return (lo.astype(v.dtype), hi.astype(v.dtype)) if via_f32 else (lo, hi)


def _merge_rows(lo, hi, R):
    half = R // 2
    A = lo.shape[0] // half
    return jnp.concatenate([x[a * half:(a + 1) * half] for a in range(A) for x in (lo, hi)], axis=0)


def _attn_fwd(qkv, biases, geom):
    S = qkv.shape[0]
    gi, T, R, has_prev = geom.gi, geom.T, geom.rows, geom.has_prev
    H2 = T // 2
    qv = geom.view(qkv)
    cq, ck, cv = 3 * gi, 3 * gi + 1, 3 * gi + 2

    def body(*refs):
        if has_prev:
            q_ref, kp_ref, kc_ref, vp_ref, vc_ref, ba_ref, bb_ref, o_ref, l_ref = refs
        else:
            q_ref, kc_ref, vc_ref, ba_ref, bb_ref, o_ref, l_ref = refs
        q_lo, q_hi = _split_rows(_ld(q_ref), R)
        kc_lo, kc_hi = _split_rows(_ld(kc_ref), R)
        vc_lo, vc_hi = _split_rows(_ld(vc_ref), R)
        k_b = jnp.concatenate([kc_lo, kc_hi], axis=0)
        v_b = jnp.concatenate([vc_lo, vc_hi], axis=0)
        if has_prev:
            n = pl.program_id(len(geom.outer))
            k_a = jnp.concatenate([_split_rows(_ld(kp_ref), R)[1], kc_lo], axis=0)
            v_a = jnp.concatenate([_split_rows(_ld(vp_ref), R)[1], vc_lo], axis=0)
            pm = _prev_mask(n, H2, T)
        else:
            k_a, v_a, pm = k_b, v_b, None
        low = lax.broadcasted_iota(jnp.int32, (1, 2 * HD), 1) < HD
        for j in range(NH // 2):
            cols = slice(2 * HD * j, 2 * HD * (j + 1))
            o_half, l_half = [], []
            for q, k, v, b_ref, mask in ((q_lo, k_a, v_a, ba_ref, pm), (q_hi, k_b, v_b, bb_ref, None)):
                q2 = q[:, cols].astype(F32) * 0.125
                k2, v2 = k[:, cols], v[:, cols]
                qs = jnp.concatenate([jnp.where(low, q2, 0.0), jnp.where(low, 0.0, q2)], axis=0).astype(q.dtype)
                s = _dot(qs, k2, NT) + b_ref[j]
                if mask is not None:
                    s = s + mask
                m = jnp.max(s, axis=1, keepdims=True)
                p = jnp.exp(s - m)
                l = jnp.sum(p, axis=1, keepdims=True)
                pv = _dot(p.astype(v.dtype), v2, NN) / l
                lse = m + jnp.log(l)
                o_half.append(jnp.where(low, pv[:H2], pv[H2:]))
                l_half.append(jnp.where(low, lse[:H2], lse[H2:]))
            _st(o_ref, _merge_rows(o_half[0], o_half[1], R), cols)
            _st(l_ref, _merge_rows(l_half[0], l_half[1], R), cols)

    cur = lambda n: n
    prev = lambda n: jnp.maximum(n - 1, 0)
    if has_prev:
        in_specs = [geom.spec(AW, cq, cur), geom.spec(AW, ck, prev), geom.spec(AW, ck, cur),
                    geom.spec(AW, cv, prev), geom.spec(AW, cv, cur)]
        args = [qv] * 5
    else:
        in_specs = [geom.spec(AW, cq, cur), geom.spec(AW, ck, cur), geom.spec(AW, cv, cur)]
        args = [qv] * 3
    nd = len(geom.outer) + (1 if has_prev else 0)
    biases = [b.reshape(NH // 2, 2 * H2, b.shape[2]) for b in biases]
    in_specs += [pl.BlockSpec(b.shape, lambda *ids: (0, 0, 0)) for b in biases]
    grid = geom.outer + ((geom.n_tiles,) if has_prev else ())
    oshape = geom.view_shape(AW)
    o, l = pl.pallas_call(
        body, name=f"attn_fwd{gi}", grid=grid, in_specs=in_specs,
        out_specs=[geom.spec(AW, 0, cur), geom.spec(AW, 0, cur)],
        out_shape=[_sds(oshape, F32), _sds(oshape, F32)],
        compiler_params=_cp(("arbitrary",) * nd),
    )(*args, *biases)
    return geom.unview(o), geom.unview(l)


def _attn_combine(os_, ls_, rest):
    S = rest.shape[0]
    tm = 512

    def body(o0, o1, o2, l0, l1, l2, z_ref, attn_ref, L_ref, ag_ref, agt_ref):
        la, lb, lc = l0[...], l1[...], l2[...]
        m = jnp.maximum(jnp.maximum(la, lb), lc)
        ea, eb, ec = jnp.exp(la - m), jnp.exp(lb - m), jnp.exp(lc - m)
        den = ea + eb + ec
        attn = (ea * o0[...] + eb * o1[...] + ec * o2[...]) / den
        attn_ref[...] = attn
        L_ref[...] = m + jnp.log(den)
        z = z_ref[...]
        ag = attn * (z * _sigmoid(z))
        ag_ref[...] = ag.astype(ag_ref.dtype)
        agt_ref[...] = ag.T.astype(agt_ref.dtype)

    t = pl.BlockSpec((tm, AW), lambda i: (i, 0))
    return pl.pallas_call(
        body, name="attn_combine", grid=(S // tm,),
        in_specs=[t] * 6 + [pl.BlockSpec((tm, AW), lambda i: (i, 0))],
        out_specs=[t, t, t, pl.BlockSpec((AW, tm), lambda i: (0, i))],
        out_shape=[_sds((S, AW), F32), _sds((S, AW), F32), _sds((S, AW), CDT), _sds((AW, S), CDT)],
        compiler_params=_cp(("parallel",)),
    )(*os_, *ls_, rest)


def _pool_counts(r, LS, win):
    l = lax.broadcasted_iota(jnp.int32, (LS, 1), 0)
    return jnp.minimum(16 * l + (r + 1), win).astype(F32)


def _pool_fwd(rest, pool_w, pool_scale):
    S = rest.shape[0]
    LS = S // R16
    r3 = rest.reshape(R16, LS, REST_W)
    PG = 128

    def body(u_ref, z_ref, pw_ref, ps_ref, pt_ref, mp_ref, pg_ref, pgt_ref, e_ref):
        g = pl.program_id(0)
        win = jnp.left_shift(2, g)
        row = lax.broadcasted_iota(jnp.int32, (LS, PG), 0)
        for r in range(R16):
            e_ref[R16 + r] = u_ref[r]
        for r in range(1, R16):
            e_ref[r] = jnp.where(row == 0, 0.0, pltpu.roll(u_ref[r], 1, 0))

        def double(s):
            for i in range(2 * R16 - 1, 2 * s - 1, -1):
                e_ref[i] = e_ref[i] + e_ref[i - s]

        double(1)
        for k in (1, 2, 3):
            pl.when(g >= k)(functools.partial(double, 2 ** k))
        pw = pw_ref[...].astype(CDT)
        for r in range(R16):
            u = u_ref[r]
            pooled = e_ref[R16 + r] / _pool_counts(r, LS, win) - u
            pc = pooled.astype(CDT)
            mp = _dot(pc, pw, NN)
            z = z_ref[r]
            pg = (mp * ps_ref[...]) * (z * _sigmoid(z))
            mp_ref[r] = mp
            pg_ref[r] = pg.astype(pg_ref.dtype)
            pt_ref[:, r * LS:(r + 1) * LS] = pooled.T.astype(pt_ref.dtype)
            pgt_ref[:, r * LS:(r + 1) * LS] = pg.T.astype(pgt_ref.dtype)

    col = lambda off: pl.BlockSpec((R16, LS, PG), lambda g: (0, 0, off + g))
    tsp = pl.BlockSpec((PG, S), lambda g: (g, 0))
    pt, mp, pg, pgt = pl.pallas_call(
        body, name="pool_fwd", grid=(4,),
        in_specs=[col(24), col(4), pl.BlockSpec((None, PG, PG), lambda g: (g, 0, 0)),
                  pl.BlockSpec((1, PG), lambda g: (0, g))],
        out_specs=[tsp, col(0), col(0), tsp],
        out_shape=[_sds((AW, S), CDT), _sds((R16, LS, AW), F32), _sds((R16, LS, AW), CDT), _sds((AW, S), CDT)],
        scratch_shapes=[pltpu.VMEM((2 * R16, LS, PG), F32)],
        compiler_params=_cp(("parallel",)),
    )(r3, r3, pool_w, pool_scale)
    return pt, mp.reshape(S, AW), pg.reshape(S, AW), pgt


def _branch_merge(ag, pg, wab, wpb, rest):
    S = ag.shape[0]
    tm, tn = 512, 512

    def body(ag_ref, pg_ref, wa_ref, wp_ref, ga_ref, gp_ref, m_ref, mt_ref):
        ya = _dot(ag_ref[...], wa_ref[...], NN)
        yp = _dot(pg_ref[...], wp_ref[...], NN)
        mg = _sigmoid(ga_ref[...]) * ya + _sigmoid(gp_ref[...]) * yp
        m_ref[...] = mg.astype(m_ref.dtype)
        mt_ref[...] = mg.T.astype(mt_ref.dtype)

    a = pl.BlockSpec((tm, AW), lambda i, j: (i, 0))
    w = pl.BlockSpec((AW, tn), lambda i, j: (0, j))
    o = pl.BlockSpec((tm, tn), lambda i, j: (i, j))
    return pl.pallas_call(
        body, name="branch_merge", grid=(S // tm, D // tn),
        in_specs=[a, a, w, w, pl.BlockSpec((tm, tn), lambda i, j: (i, 2 + j)),
                  pl.BlockSpec((tm, tn), lambda i, j: (i, 4 + j))],
        out_specs=[o, pl.BlockSpec((tn, tm), lambda i, j: (j, i))],
        out_shape=[_sds((S, D), CDT), _sds((D, S), CDT)],
        compiler_params=_cp(("parallel", "parallel")),
    )(ag, pg, wab, wpb, rest, rest)


def _out_loss(merged, wout, x, target, gate, final_g):
    S = x.shape[0]

    def body(m_ref, w_ref, x_ref, t_ref, gate_ref, fg_ref, dx2_ref, dmo_ref, loss_ref, gfg_ref, dgate_ref):
        i = pl.program_id(0)

        @pl.when(i == 0)
        def _():
            loss_ref[...] = jnp.zeros_like(loss_ref)
            gfg_ref[...] = jnp.zeros_like(gfg_ref)
            dgate_ref[...] = jnp.zeros_like(dgate_ref)

        swap = _swap16()
        mo = _dot(_dot(swap, _ld(m_ref), NN).astype(CDT), w_ref[...], NN)
        gate, fg = gate_ref[...], fg_ref[...]
        x2 = x_ref[...] + gate * mo
        r2 = lax.rsqrt(jnp.mean(x2 * x2, axis=-1, keepdims=True) + EPS)
        xn2 = x2 * r2
        diff = xn2 * fg - t_ref[...]
        loss = jnp.sum(jnp.sum(diff * diff, axis=-1, keepdims=True), axis=0, keepdims=True)
        dy = diff * (1.0 / D)
        dxn2 = dy * fg
        dx2 = r2 * (dxn2 - xn2 * jnp.mean(dxn2 * xn2, axis=-1, keepdims=True))
        dx2_ref[...] = dx2
        _st(dmo_ref, _dot(swap, (dx2 * gate).astype(CDT), NN))
        loss_ref[...] += jnp.broadcast_to(loss * (0.5 / D), loss_ref.shape)
        gfg_ref[...] += jnp.sum(dy * xn2, axis=0, keepdims=True)
        dgate_ref[...] += jnp.sum(dx2 * mo, axis=0, keepdims=True)

    slab = pl.BlockSpec((R16, R16, D), lambda i: (0, i, 0))
    nat = pl.BlockSpec((TOK, D), lambda i: (i, 0))
    vec = pl.BlockSpec((1, D), lambda i: (0, 0))
    return pl.pallas_call(
        body, name="out_loss", grid=(S // TOK,),
        in_specs=[slab, _resident((D, D)), nat, nat, vec, vec],
        out_specs=[nat, slab, pl.BlockSpec((1, 128), lambda i: (0, 0)), vec, vec],
        out_shape=[_sds((S, D), F32), _sds((R16, S // R16, D), CDT), _sds((1, 128), F32), _sds((1, D), F32),
                   _sds((1, D), F32)],
        compiler_params=_cp(("arbitrary",)),
    )(merged, wout, x, target, gate, final_g)


def _bwd_gates(dmo, wout, wab, wpb, rest, ag, pg, attn, mp, pool_scale):
    S = dmo.shape[0]
    tm = 256

    def body(dmo_ref, wo_ref, wa_ref, wp_ref, za_ref, zp_ref, ga0, ga1, gp0, gp1, ag_ref, pg_ref, at_ref, mp_ref,
             ps_ref, dya_ref, dyp_ref, dr_ref, dat_ref, dl_ref, dmp_ref, dps_ref):
        i = pl.program_id(0)

        @pl.when(i == 0)
        def _():
            dps_ref[...] = jnp.zeros_like(dps_ref)

        dm = _dot(dmo_ref[...], wo_ref[...], NT)
        sa = _sigmoid(jnp.concatenate([ga0[...], ga1[...]], axis=1))
        sp = _sigmoid(jnp.concatenate([gp0[...], gp1[...]], axis=1))
        dya = (dm * sa).astype(CDT)
        dyp = (dm * sp).astype(CDT)
        dya_ref[...] = dya
        dyp_ref[...] = dyp
        ya = _dot(ag_ref[...], wa_ref[...], NN)
        yp = _dot(pg_ref[...], wp_ref[...], NN)
        dr_ref[:, 1024:2048] = (dm * ya * sa * (1.0 - sa)).astype(dr_ref.dtype)
        dr_ref[:, 2048:3072] = (dm * yp * sp * (1.0 - sp)).astype(dr_ref.dtype)
        da = _dot(dya, wa_ref[...], NT)
        dp = _dot(dyp, wp_ref[...], NT)
        za = za_ref[...]
        sga = _sigmoid(za)
        attn = at_ref[...]
        dattn = da * (za * sga)
        dr_ref[:, 0:512] = (da * attn * (sga * (1.0 + za * (1.0 - sga)))).astype(dr_ref.dtype)
        dat_ref[...] = dattn.astype(dat_ref.dtype)
        prod = dattn * attn
        for h in range(NH):
            sl = slice(HD * h, HD * h + HD)
            dl_ref[:, sl] = jnp.broadcast_to(jnp.sum(prod[:, sl], axis=1, keepdims=True), (tm, HD))
        zp = zp_ref[...]
        sgp = _sigmoid(zp)
        mpre = mp_ref[...]
        ps = ps_ref[...]
        dmixed = dp * (zp * sgp)
        dr_ref[:, 512:1024] = (dp * (mpre * ps) * (sgp * (1.0 + zp * (1.0 - sgp)))).astype(dr_ref.dtype)
        dps_ref[...] += jnp.sum(dmixed * mpre, axis=0, keepdims=True)
        dmp_ref[...] = (dmixed * ps).astype(dmp_ref.dtype)

    full = lambda shape: pl.BlockSpec(shape, lambda i: (0, 0))
    tD = pl.BlockSpec((tm, D), lambda i: (i, 0))
    tA = pl.BlockSpec((tm, AW), lambda i: (i, 0))
    rb = lambda c: pl.BlockSpec((tm, AW), lambda i: (i, c))
    return pl.pallas_call(
        body, name="bwd_gates", grid=(S // tm,),
        in_specs=[tD, full((D, D)), full((AW, D)), full((AW, D)), rb(0), rb(1), rb(2), rb(3), rb(4), rb(5),
                  tA, tA, tA, tA, full((1, AW))],
        out_specs=[tD, tD, pl.BlockSpec((tm, 3072), lambda i: (i, 0)), tA, tA, tA, full((1, AW))],
        out_shape=[_sds((S, D), CDT), _sds((S, D), CDT), _sds((S, REST_W), CDT), _sds((S, AW), CDT),
                   _sds((S, AW), F32), _sds((S, AW), CDT), _sds((1, AW), F32)],
        compiler_params=_cp(("arbitrary",), vmem=56),
    )(dmo, wout, wab, wpb, rest, rest, rest, rest, rest, rest, ag, pg, attn, mp, pool_scale)


def _pool_bwd(dmp, pt, pool_w, drest):
    S = dmp.shape[0]
    LS = S // R16
    PG = 128
    d3 = dmp.reshape(R16, LS, AW)
    dr3 = drest.reshape(R16, LS, REST_W)

    def body(d_ref, pt_ref, pw_ref, dr_in, du_ref, dpw_ref, g_ref, s_ref):
        del dr_in
        g = pl.program_id(0)
        win = jnp.left_shift(2, g)
        row = lax.broadcasted_iota(jnp.int32, (LS, PG), 0)
        pw = pw_ref[...].astype(CDT)
        dpw = jnp.zeros((PG, PG), F32)
        for r in range(R16):
            dm = d_ref[r]
            dpw = dpw + _dot(pt_ref[:, r * LS:(r + 1) * LS], dm, NN)
            dpooled = _dot(dm, pw, NT)
            g_ref[r] = dpooled
            sc = dpooled / _pool_counts(r, LS, win)
            s_ref[r] = sc
            if r < R16 - 1:
                s_ref[R16 + r] = jnp.where(row == LS - 1, 0.0, pltpu.roll(sc, LS - 1, 0))
        dpw_ref[...] = dpw

        def double(s):
            for i in range(0, 2 * R16 - 2 * s):
                s_ref[i] = s_ref[i] + s_ref[i + s]

        double(1)
        for k in (1, 2, 3):
            pl.when(g >= k)(functools.partial(double, 2 ** k))
        for r in range(R16):
            du_ref[r] = (s_ref[r] - g_ref[r]).astype(du_ref.dtype)

    du, dpw = pl.pallas_call(
        body, name="pool_bwd", grid=(4,),
        in_specs=[pl.BlockSpec((R16, LS, PG), lambda g: (0, 0, g)), pl.BlockSpec((PG, S), lambda g: (g, 0)),
                  pl.BlockSpec((None, PG, PG), lambda g: (g, 0, 0)), pl.BlockSpec(memory_space=pl.ANY)],
        out_specs=[pl.BlockSpec((R16, LS, PG), lambda g: (0, 0, 24 + g)),
                   pl.BlockSpec((None, PG, PG), lambda g: (g, 0, 0))],
        out_shape=[_sds((R16, LS, REST_W), CDT), _sds((4, PG, PG), F32)],
        scratch_shapes=[pltpu.VMEM((R16, LS, PG), F32), pltpu.VMEM((2 * R16, LS, PG), F32)],
        input_output_aliases={3: 0},
        compiler_params=_cp(("parallel",)),
    )(d3, pt, pool_w, dr3)
    return du.reshape(S, REST_W), dpw


def _attn_bwd(qkv, dattn, Lb, dlb, biases, geom, dqkv_prev):
    S = qkv.shape[0]
    gi, T, R, has_prev, N = geom.gi, geom.T, geom.rows, geom.has_prev, geom.n_tiles
    H2 = T // 2
    nd_outer = len(geom.outer)
    qv = geom.view(qkv)
    cq, ck, cv = 3 * gi, 3 * gi + 1, 3 * gi + 2
    W3 = 3 * AW

    def heads(q_ref, kp_ref, kc_ref, vp_ref, vc_ref, do_ref, L_ref, dl_ref, ba_ref, bb_ref, pm, dsa_ref, dsb_ref, emit):
        q_lo, q_hi = _split_rows(_ld(q_ref), R)
        do_lo, do_hi = _split_rows(_ld(do_ref), R)
        L_lo, L_hi = _split_rows(_ld(L_ref), R)
        dl_lo, dl_hi = _split_rows(_ld(dl_ref), R)
        kc_lo, kc_hi = _split_rows(_ld(kc_ref), R)
        vc_lo, vc_hi = _split_rows(_ld(vc_ref), R)
        k_b = jnp.concatenate([kc_lo, kc_hi], axis=0)
        v_b = jnp.concatenate([vc_lo, vc_hi], axis=0)
        if has_prev:
            k_a = jnp.concatenate([_split_rows(_ld(kp_ref), R)[1], kc_lo], axis=0)
            v_a = jnp.concatenate([_split_rows(_ld(vp_ref), R)[1], vc_lo], axis=0)
        else:
            k_a, v_a = k_b, v_b
        low = lax.broadcasted_iota(jnp.int32, (1, 2 * HD), 1) < HD
        for j in range(NH // 2):
            sl = slice(2 * HD * j, 2 * HD * (j + 1))
            res = []
            for q, k, v, do, L, dl, b_ref, mask, ds_ref in (
                    (q_lo, k_a, v_a, do_lo, L_lo, dl_lo, ba_ref, pm, dsa_ref),
                    (q_hi, k_b, v_b, do_hi, L_hi, dl_hi, bb_ref, None, dsb_ref)):
                q2 = q[:, sl].astype(F32) * 0.125
                do2 = do[:, sl].astype(F32)
                k2, v2 = k[:, sl], v[:, sl]
                L2, dl2 = L[:, sl], dl[:, sl]
                qs = jnp.concatenate([jnp.where(low, q2, 0.0), jnp.where(low, 0.0, q2)], axis=0).astype(q.dtype)
                dom = jnp.concatenate([jnp.where(low, do2, 0.0), jnp.where(low, 0.0, do2)], axis=0).astype(q.dtype)
                Lc = jnp.concatenate([L2[:, 0:1], L2[:, HD:HD + 1]], axis=0)
                dlc = jnp.concatenate([dl2[:, 0:1], dl2[:, HD:HD + 1]], axis=0)
                s = _dot(qs, k2, NT) + b_ref[j]
                if mask is not None:
                    s = s + mask
                p = jnp.exp(s - Lc)
                ds = p * (_dot(dom, v2, NT) - dlc)
                ds_ref[j] += ds
                dsc = ds.astype(q.dtype)
                dq_st = _dot(dsc, k2, NN)
                res.append((jnp.where(low, dq_st[:H2], dq_st[H2:]) * 0.125, _dot(dsc, qs, TN),
                            _dot(p.astype(q.dtype), dom, TN)))
            (dq_lo, dk_a, dv_a), (dq_hi, dk_b, dv_b) = res
            dq = _merge_rows(dq_lo, dq_hi, R)
            if has_prev:
                emit(sl, dq, _merge_rows(dk_a[H2:] + dk_b[:H2], dk_b[H2:], R),
                     _merge_rows(dv_a[H2:] + dv_b[:H2], dv_b[H2:], R), dk_a[:H2], dv_a[:H2])
            else:
                emit(sl, dq, _merge_rows(dk_a[:H2] + dk_b[:H2], dk_a[H2:] + dk_b[H2:], R),
                     _merge_rows(dv_a[:H2] + dv_b[:H2], dv_a[H2:] + dv_b[H2:], R), None, None)

    def first_step():
        ok = pl.program_id(nd_outer) == 0 if has_prev else pl.program_id(0) == 0
        for a in range(nd_outer if has_prev else 0):
            ok = ok & (pl.program_id(a) == 0)
        return ok

    if has_prev:
        def body(q_ref, kp_ref, kc_ref, vp_ref, vc_ref, do_ref, L_ref, dl_ref, ba_ref, bb_ref, prev_in, out_ref,
                 dsa_ref, dsb_ref, hold, new, pp):
            del prev_in
            s_id = pl.program_id(nd_outer)

            @pl.when(first_step())
            def _():
                dsa_ref[...] = jnp.zeros_like(dsa_ref)
                dsb_ref[...] = jnp.zeros_like(dsb_ref)

            @pl.when(s_id < N)
            def _():
                def emit(sl, dq, dk, dv, dk_prev, dv_prev):
                    new[:, sl] = dq
                    new[:, slice(AW + sl.start, AW + sl.stop)] = dk
                    new[:, slice(2 * AW + sl.start, 2 * AW + sl.stop)] = dv
                    pp[:, sl] = dk_prev
                    pp[:, slice(AW + sl.start, AW + sl.stop)] = dv_prev

                heads(q_ref, kp_ref, kc_ref, vp_ref, vc_ref, do_ref, L_ref, dl_ref, ba_ref, bb_ref,
                      _prev_mask(s_id, H2, T), dsa_ref, dsb_ref, emit)

            @pl.when((s_id >= 1) & (s_id < N))
            def _():
                _st(out_ref, hold[:, 0:AW], slice(0, AW))
                late = _merge_rows(jnp.zeros((H2, 2 * AW), F32), pp[...], R)
                _st(out_ref, hold[:, AW:W3] + late, slice(AW, W3))

            @pl.when(s_id == N)
            def _():
                _st(out_ref, hold[...])

            @pl.when(s_id < N)
            def _():
                hold[...] = new[...]

        cur = lambda n: jnp.minimum(n, N - 1)
        prev = lambda n: jnp.clip(n - 1, 0, N - 1)
        delayed = lambda n: jnp.maximum(n - 1, 0)
        in_specs = [geom.spec(AW, cq, cur), geom.spec(AW, ck, prev), geom.spec(AW, ck, cur),
                    geom.spec(AW, cv, prev), geom.spec(AW, cv, cur),
                    geom.spec(AW, 0, cur), geom.spec(AW, 0, cur), geom.spec(AW, 0, cur)]
        args = [qv] * 5 + [geom.view(dattn), geom.view(Lb), geom.view(dlb)]
        out_spec = geom.spec(W3, gi, delayed)
        grid = geom.outer + (N + 1,)
        scratch = [pltpu.VMEM((T, W3), F32), pltpu.VMEM((T, W3), F32), pltpu.VMEM((H2, 2 * AW), F32)]
    else:
        def body(q_ref, kc_ref, vc_ref, do_ref, L_ref, dl_ref, ba_ref, bb_ref, prev_in, out_ref, dsa_ref, dsb_ref):
            del prev_in

            @pl.when(first_step())
            def _():
                dsa_ref[...] = jnp.zeros_like(dsa_ref)
                dsb_ref[...] = jnp.zeros_like(dsb_ref)

            def emit(sl, dq, dk, dv, dk_prev, dv_prev):
                _st(out_ref, dq, sl)
                _st(out_ref, dk, slice(AW + sl.start, AW + sl.stop))
                _st(out_ref, dv, slice(2 * AW + sl.start, 2 * AW + sl.stop))

            heads(q_ref, None, kc_ref, None, vc_ref, do_ref, L_ref, dl_ref, ba_ref, bb_ref, None, dsa_ref, dsb_ref,
                  emit)

        cur = lambda n: n
        in_specs = [geom.spec(AW, cq, cur), geom.spec(AW, ck, cur), geom.spec(AW, cv, cur),
                    geom.spec(AW, 0, cur), geom.spec(AW, 0, cur), geom.spec(AW, 0, cur)]
        args = [qv] * 3 + [geom.view(dattn), geom.view(Lb), geom.view(dlb)]
        out_spec = geom.spec(W3, gi, cur)
        grid = geom.outer
        scratch = []

    whole = lambda b: pl.BlockSpec(b.shape, lambda *ids: (0, 0, 0))
    biases = [b.reshape(NH // 2, 2 * H2, b.shape[2]) for b in biases]
    in_specs += [whole(b) for b in biases]
    in_specs.append(pl.BlockSpec(memory_space=pl.ANY))
    if dqkv_prev is None:
        last, aliases = biases[0], {}
    else:
        last, aliases = geom.view(dqkv_prev), {len(in_specs) - 1: 0}
    dq, dsa, dsb = pl.pallas_call(
        body, name=f"attn_bwd{gi}", grid=grid, in_specs=in_specs,
        out_specs=[out_spec] + [whole(b) for b in biases],
        out_shape=[_sds(geom.view_shape(QKV_W), CDT)] + [_sds(b.shape, F32) for b in biases],
        scratch_shapes=scratch,
        input_output_aliases=aliases,
        compiler_params=_cp(("arbitrary",) * len(grid), vmem=56),
    )(*args, *biases, last)
    return geom.unview(dq), [d.reshape(NH, H2, d.shape[2]) for d in (dsa, dsb)]


def _h_bwd(dh, dx2, x, norm_g, scale):
    S = x.shape[0]

    def body(dh_ref, dx2_ref, x_ref, g_ref, sc_ref, gx_ref, dsh_ref, dsc_ref, dng_ref, nat_ref):
        i = pl.program_id(0)

        @pl.when(i == 0)
        def _():
            dsh_ref[...] = jnp.zeros_like(dsh_ref)
            dsc_ref[...] = jnp.zeros_like(dsc_ref)
            dng_ref[...] = jnp.zeros_like(dng_ref)

        for r in range(R16):
            nat_ref[:, r, :] = dh_ref[r]
        dh = jnp.concatenate([nat_ref[l] for l in range(R16)], axis=0)
        g = g_ref[...]
        one_sc = 1.0 + sc_ref[...]
        xv = x_ref[...]
        r1 = lax.rsqrt(jnp.mean(xv * xv, axis=-1, keepdims=True) + EPS)
        xn = xv * r1
        dhx = dh * xn
        dxn = dh * (g * one_sc)
        dx1 = r1 * (dxn - xn * jnp.mean(dxn * xn, axis=-1, keepdims=True))
        gx_ref[...] = dx2_ref[...] + dx1
        dsh_ref[...] += jnp.sum(dh, axis=0, keepdims=True)
        dsc_ref[...] += jnp.sum(dhx * g, axis=0, keepdims=True)
        dng_ref[...] += jnp.sum(dhx * one_sc, axis=0, keepdims=True)

    nat = pl.BlockSpec((TOK, D), lambda i: (i, 0))
    vec = pl.BlockSpec((1, D), lambda i: (0, 0))
    return pl.pallas_call(
        body, name="h_bwd", grid=(S // TOK,),
        in_specs=[pl.BlockSpec((R16, R16, D), lambda i: (0, i, 0)), nat, nat, vec, vec],
        out_specs=[nat, vec, vec, vec],
        out_shape=[_sds((S, D), F32), _sds((1, D), F32), _sds((1, D), F32), _sds((1, D), F32)],
        scratch_shapes=[pltpu.VMEM((R16, R16, D), F32)],
        compiler_params=_cp(("arbitrary",)),
    )(dh, dx2, x, norm_g, scale)


def _local_step(x, target, shift, scale, gate, norm_g, w_in_full, pool_w, pool_scale, wab, wpb, wout, rel_bias,
                final_g):
    st = _local_pre(x, shift, scale, norm_g, rel_bias)
    st = _local_grads(st, target, gate, w_in_full, pool_w, pool_scale, wab, wpb, wout, final_g)
    d_rel_bias = _local_bias_grads(st)
    gx, dmod, dng = _local_input_grads(st, w_in_full, norm_g, scale)
    return dict(loss=st["loss"], grad_x=gx, dw_in=st["dw_in"], dw_ab=st["dw_ab"], dw_pb=st["dw_pb"],
                dw_out=st["dw_out"], d_norm_g=dng, d_pool_w=st["dpw"], d_pool_scale=st["dps"],
                d_rel_bias=d_rel_bias, d_final_g=st["gfg"], dmod=dmod)


def _local_pre(x, shift, scale, norm_g, rel_bias, dep=None):
    S = x.shape[0]
    LS = S // R16
    geoms = [_Geom(gi, LS) for gi in range(3)]
    idx_np = [[g.bucket_index(sub) for sub in range(2)] for g in geoms]
    buckets = [[tuple(int(b) for b in np.unique(ix) if b >= 0) for ix in pair] for pair in idx_np]
    idxs = [[jnp.asarray(ix) for ix in pair] for pair in idx_np]
    biases = [[_bias_table(idxs[gi][sub], buckets[gi][sub], rel_bias, gi, sub, dep) for sub in range(2)]
              for gi in range(3)]
    h = _norm_mod(x, norm_g, shift, scale, dep).reshape(S, D)
    return dict(geoms=geoms, idxs=idxs, buckets=buckets, biases=biases, x=x, h=h, ht=_tcast(h, "h_transpose"))


def _local_grads(st, target, gate, w_in_full, pool_w, pool_scale, wab, wpb, wout, final_g):
    geoms, idxs, biases, x, h, ht = (st[k] for k in ("geoms", "idxs", "biases", "x", "h", "ht"))
    S = h.shape[0]
    LS = S // R16
    qkv, rest = _proj(h, w_in_full)

    os_, ls_ = [], []
    for gi in range(3):
        o, l = _attn_fwd(qkv, biases[gi], geoms[gi])
        os_.append(o)
        ls_.append(l)
    attn, Lb, ag, agt = _attn_combine(os_, ls_, rest)
    pt, mp, pg, pgt = _pool_fwd(rest, pool_w, pool_scale)
    merged, mt = _branch_merge(ag, pg, wab, wpb, rest)
    dx2, dmo3, loss, gfg, dgate = _out_loss(merged.reshape(R16, LS, D), wout, x, target, gate, final_g)
    dmo = dmo3.reshape(S, D)

    dya, dyp, drest, dattn, dlb, dmp, dps = _bwd_gates(dmo, wout, wab, wpb, rest, ag, pg, attn, mp, pool_scale)
    dw_out = _mm_tn(mt, dmo, n_total=D, out_blk=lambda j: j, name="dw_out")
    dw_ab = _mm_tn(agt, dya, n_total=D, out_blk=lambda j: j, name="dw_attn_br")
    dw_pb = _mm_tn(pgt, dyp, n_total=D, out_blk=lambda j: j, name="dw_pool_br")
    drest, dpw = _pool_bwd(dmp, pt, pool_w, drest)

    dqkv = None
    dsums = []
    for gi in range(3):
        dqkv, dsum = _attn_bwd(qkv, dattn, Lb, dlb, biases[gi], geoms[gi], dqkv)
        dsums.append(dsum)

    dw_in = _mm_tn(ht, dqkv, n_total=8192, out_blk=lambda j: j, name="dw_in_qkv")
    dw_in = _mm_tn(ht, drest, n_total=8192, out_blk=_rest_blk, name="dw_in_rest", prev=dw_in)
    return dict(st, loss=loss[0, 0], dw_in=dw_in, dw_ab=dw_ab, dw_pb=dw_pb, dw_out=dw_out, dpw=dpw, dps=dps, gfg=gfg,
                dgate=dgate, dsums=dsums, dqkv=dqkv, drest=drest, dx2=dx2)


def _local_bias_grads(st, dep=None):
    drb = [_bias_grad(st["dsums"][gi][0], st["idxs"][gi][0], st["buckets"][gi][0], gi, 0, dep if gi == 0 else None) +
           _bias_grad(st["dsums"][gi][1], st["idxs"][gi][1], st["buckets"][gi][1], gi, 1) for gi in range(3)]
    return jnp.concatenate(drb, axis=1)


def _local_input_grads(st, w_in_full, norm_g, scale, dep=None):
    S = st["h"].shape[0]
    LS = S // R16
    dh = _dh(st["dqkv"], st["drest"], w_in_full, dep)
    gx, dsh, dsc, dng = _h_bwd(dh.reshape(R16, LS, D), st["dx2"], st["x"], norm_g, scale)
    return gx, jnp.concatenate([dsh, dsc, st["dgate"]], axis=1), dng


def _my_place():
    return lax.axis_index("x"), lax.axis_index("y"), lax.axis_index("c")


def _block_of(ref, axis, idx, width):
    if axis is None:
        return ref.at[idx]
    start = pl.multiple_of(idx * width, width)
    if axis == 0:
        return ref.at[pl.ds(start, width)]
    return ref.at[:, pl.ds(start, width)]


_SEM = pl.BlockSpec(memory_space=pltpu.SEMAPHORE)
_EFFECT = pltpu.SideEffectType.DATAFLOW_SIDE_EFFECTING


def _split_start(name, srcs, lands, ncopy, plan, dep=None):
    arrs = list(srcs) + list(lands)
    na, ns = len(arrs), len(srcs)
    dep_specs, dep_args = _after(dep)

    def body(*refs):
        send_sems, recv_sems, token = refs[na + len(dep_args)], refs[na + len(dep_args) + 1], refs[-1]
        for k, (s, d, dev) in enumerate(plan(refs[:ns], refs[ns:na])):
            pltpu.make_async_remote_copy(src_ref=s, dst_ref=d, send_sem=send_sems.at[k], recv_sem=recv_sems.at[k],
                                         device_id=dev, device_id_type=MESH).start()
        token[...] = jnp.zeros_like(token)

    outs = pl.pallas_call(
        body, name=name,
        out_shape=(pltpu.SemaphoreType.DMA((ncopy,)), pltpu.SemaphoreType.DMA((ncopy,)),
                   *[pltpu.HBM(a.shape, a.dtype) for a in arrs], _sds((8, 128), F32)),
        in_specs=[_HBM] * na + dep_specs,
        out_specs=(_SEM, _SEM, *[_HBM] * na, pl.BlockSpec(memory_space=pltpu.VMEM)),
        input_output_aliases={i: 2 + i for i in range(na)},
        compiler_params=pltpu.CompilerParams(has_side_effects=_EFFECT),
    )(*[pltpu.with_memory_space_constraint(a, pltpu.HBM) for a in arrs], *dep_args)
    return (outs[0], outs[1]), list(outs[2:2 + ns]), list(outs[2 + ns:2 + na]), outs[-1]


def _split_wait(name, sems, srcs, lands, plan, after):
    arrs = list(srcs) + list(lands)
    na, ns = len(arrs), len(srcs)
    after = list(after) if isinstance(after, (list, tuple)) else [after]

    def body(*refs):
        send_sems, recv_sems = refs[na], refs[na + 1]
        for k, (s, d, dev) in enumerate(plan(refs[:ns], refs[ns:na])):
            cp = pltpu.make_async_remote_copy(src_ref=s, dst_ref=d, send_sem=send_sems.at[k], recv_sem=recv_sems.at[k],
                                              device_id=dev, device_id_type=MESH)
            cp.wait_send()
            cp.wait_recv()

    outs = pl.pallas_call(
        body, name=name, out_shape=[pltpu.HBM(a.shape, a.dtype) for a in arrs],
        in_specs=[_HBM] * na + [_SEM, _SEM] + [pl.BlockSpec(memory_space=pl.ANY)] * len(after),
        out_specs=[_HBM] * na,
        input_output_aliases={i: i for i in range(na)},
        compiler_params=pltpu.CompilerParams(has_side_effects=_EFFECT),
    )(*arrs, sems[0], sems[1], *after)
    return list(outs[:ns]), list(outs[ns:])


def _gather_chips_plan(layout):
    def plan(src, land):
        x, y, c = _my_place()
        me = 4 * x + 2 * y + c
        mine = [_block_of(land[p], layout[p][0], me, layout[p][1]) for p in range(len(land))]
        return [(mine[p], mine[p], (cx, cy, c)) for p in range(len(land)) for cx, cy in [(1 - x, y), (x, 1 - y)]]
    return plan


def _pair_plan(layout):
    def plan(src, land):
        x, y, c = _my_place()
        n = len(layout)
        return [(_block_of(src[p], layout[p][0], 2 * q + (1 - c), layout[p][1]), land[p].at[q], (x, y, 1 - c))
                for p in range(n) for q in range(4)] + [(src[p], land[p], (x, y, 1 - c)) for p in range(n, len(src))]
    return plan


def _chips_plan(n):
    def plan(src, land):
        x, y, c = _my_place()
        chips = [(1 - x, y), (x, 1 - y), (1 - x, 1 - y)]
        return [(src[p].at[2 * cx + cy], land[p].at[2 * x + y], (cx, cy, c)) for p in range(n) for cx, cy in chips] + \
               [(src[p], land[p].at[2 * x + y], (cx, cy, c)) for p in range(n, len(src)) for cx, cy in chips]
    return plan


def _pair_share(gathered):
    n = len(gathered)

    def body(*refs):
        land = refs[n:2 * n]
        send_sems, recv_sems, fwd_send, fwd_recv = refs[2 * n:]
        x, y, c = _my_place()
        k_dg = 3 - (2 * x + y)
        south = c == 0
        pass_on = 4 * jnp.where(south, 1 - x, x) + 2 * jnp.where(south, y, 1 - y) + c
        to = (jnp.where(south, x, 1 - x), jnp.where(south, 1 - y, y), c)

        def forward(p):
            return pltpu.make_async_remote_copy(
                src_ref=land[p].at[pass_on], dst_ref=land[p].at[pass_on], send_sem=fwd_send.at[p],
                recv_sem=fwd_recv.at[p], device_id=to, device_id_type=MESH)

        def to_sibling(p, k):
            blk = land[p].at[2 * k + c]
            return pltpu.make_async_remote_copy(
                src_ref=blk, dst_ref=blk, send_sem=send_sems.at[4 * p + k], recv_sem=recv_sems.at[4 * p + k],
                device_id=(x, y, 1 - c), device_id_type=MESH)

        for p in range(n):
            forward(p).start()
        for p in range(n):
            for k in range(4):
                @pl.when(k != k_dg)
                def _():
                    to_sibling(p, k).start()
        for p in range(n):
            dg = land[p].at[2 * k_dg + c]
            pltpu.make_async_remote_copy(src_ref=dg, dst_ref=dg, send_sem=fwd_send.at[p], recv_sem=fwd_recv.at[p],
                                         device_id=to, device_id_type=MESH).wait_recv()
            for k in range(4):
                @pl.when(k == k_dg)
                def _():
                    to_sibling(p, k).start()
        for p in range(n):
            forward(p).wait_send()
            for k in range(4):
                to_sibling(p, k).wait_send()
                blk = land[p].at[2 * k + 1 - c]
                pltpu.make_async_remote_copy(
                    src_ref=blk, dst_ref=blk, send_sem=send_sems.at[4 * p + k], recv_sem=recv_sems.at[4 * p + k],
                    device_id=(x, y, 1 - c), device_id_type=MESH).wait_recv()

    return pl.pallas_call(
        body, name="gather_pair_share", in_specs=[_HBM] * n, out_specs=[_HBM] * n,
        out_shape=[_sds(g.shape, g.dtype) for g in gathered],
        input_output_aliases={p: p for p in range(n)},
        scratch_shapes=[pltpu.SemaphoreType.DMA((4 * n,)), pltpu.SemaphoreType.DMA((4 * n,)),
                        pltpu.SemaphoreType.DMA((n,)), pltpu.SemaphoreType.DMA((n,))],
    )(*gathered)


def _place_slot(arrs, slot, n_slots, name, out_dtype=None):
    n = len(arrs)

    def body(s_ref, *refs):
        for p in range(n):
            refs[n + p][...] = refs[p][...].astype(refs[n + p].dtype)

    def spec(a):
        tail = a.shape[-2:]
        return pl.BlockSpec((None,) + tail, lambda i, s_ref: (s_ref[0], 0, 0))

    in_specs = [spec(a) if a.ndim == 3 else pl.BlockSpec(a.shape, lambda i, s_ref: (0, 0)) for a in arrs]
    return pl.pallas_call(
        body, name=name,
        grid_spec=pltpu.PrefetchScalarGridSpec(num_scalar_prefetch=1, grid=(1,), in_specs=in_specs,
                                               out_specs=[spec(a) for a in arrs]),
        out_shape=[_sds((n_slots,) + a.shape[-2:], out_dtype or a.dtype) for a in arrs],
        compiler_params=_cp(("arbitrary",)),
    )(slot, *arrs)


def _cols_from_blocks(w8, name):
    _, R, C = w8.shape

    def body(w_ref, o_ref):
        for e in range(8):
            o_ref[:, C * e:C * (e + 1)] = w_ref[e]

    return pl.pallas_call(body, name=name, out_shape=_sds((R, 8 * C), w8.dtype), compiler_params=_cp())(w8)


def _all_gather_small(srcs, name):
    n = len(srcs)
    vmem = pl.BlockSpec(memory_space=pltpu.VMEM)

    def body(*refs):
        src, out = refs[:n], refs[n:2 * n]
        send_sems, recv_sems = refs[2 * n:]
        x, y, c = _my_place()
        me = 4 * x + 2 * y + c
        flip = lambda v, f: 1 - v if f else v
        peers = [(flip(x, fx), flip(y, fy), flip(c, fc)) for fx in (0, 1) for fy in (0, 1) for fc in (0, 1)][1:]
        cps = []
        for p in range(n):
            out[p][me] = src[p][...]
            for k, peer in enumerate(peers):
                cps.append(pltpu.make_async_remote_copy(
                    src_ref=src[p], dst_ref=out[p].at[me], send_sem=send_sems.at[7 * p + k],
                    recv_sem=recv_sems.at[7 * p + k], device_id=peer, device_id_type=MESH))
        for cp in cps:
            cp.start()
        for cp in cps:
            cp.wait()

    return pl.pallas_call(
        body, name=name, in_specs=[vmem] * n, out_specs=[vmem] * n,
        out_shape=[_sds((8,) + a.shape, a.dtype) for a in srcs],
        scratch_shapes=[pltpu.SemaphoreType.DMA((7 * n,)), pltpu.SemaphoreType.DMA((7 * n,))],
    )(*srcs)


def _all_gather(srcs, out_shapes, layout, name):
    n = len(srcs)

    def body(*refs):
        src, out = refs[:n], refs[n:2 * n]
        send_sems, recv_sems, local_sems = refs[2 * n:]
        x, y, c = _my_place()
        me, sibling = (x, y, c), (x, y, 1 - c)
        chips = [(1 - x, y), (x, 1 - y), (1 - x, 1 - y)]

        def region(p, dev):
            px, py, pc = dev
            return _block_of(out[p], layout[p][0], 4 * px + 2 * py + pc, layout[p][1])

        def copy(p, k, block, to, src_ref=None):
            return pltpu.make_async_remote_copy(
                src_ref=region(p, block) if src_ref is None else src_ref, dst_ref=region(p, block),
                send_sem=send_sems.at[p * 7 + k], recv_sem=recv_sems.at[p * 7 + k],
                device_id=to, device_id_type=MESH)

        mine = [pltpu.make_async_copy(src[p], region(p, me), local_sems.at[p]) for p in range(n)]
        for cp in mine:
            cp.start()
        first = []
        for p in range(n):
            first.append(copy(p, 0, me, sibling, src[p]))
            first += [copy(p, 1 + j, me, (*chip, c), src[p]) for j, chip in enumerate(chips)]
        for cp in first:
            cp.start()
        passed = []
        for j, chip in enumerate(chips):
            for p in range(n):
                copy(p, 1 + j, (*chip, c), me).wait_recv()
                fwd = copy(p, 4 + j, (*chip, c), sibling)
                fwd.start()
                passed.append(fwd)
        for p in range(n):
            copy(p, 0, sibling, me).wait_recv()
            for j, chip in enumerate(chips):
                copy(p, 4 + j, (*chip, 1 - c), me).wait_recv()
        for cp in first + passed:
            cp.wait_send()
        for cp in mine:
            cp.wait()

    return pl.pallas_call(
        body, name=name, in_specs=[_HBM] * n, out_specs=[_HBM] * n,
        out_shape=[_sds(s, a.dtype) for s, a in zip(out_shapes, srcs)],
        scratch_shapes=[pltpu.SemaphoreType.DMA((7 * n,)), pltpu.SemaphoreType.DMA((7 * n,)),
                        pltpu.SemaphoreType.DMA((n,))],
    )(*srcs)


def _pair_add(dw, recv, axis, width, cidx, name):
    if axis == 0:
        tr = min(width, 256)
        C = dw.shape[1]
        grid = (4, width // tr)
        nb = width // tr
        own = pl.BlockSpec((tr, C), lambda q, i, c_ref: ((2 * q + c_ref[0]) * nb + i, 0))
        oth = pl.BlockSpec((None, tr, C), lambda q, i, c_ref: (q, i, 0))
    else:
        R = dw.shape[0]
        tr = min(R, 256)
        grid = (4, R // tr)
        own = pl.BlockSpec((tr, width), lambda q, i, c_ref: (i, 2 * q + c_ref[0]))
        oth = pl.BlockSpec((None, tr, width), lambda q, i, c_ref: (q, i, 0))

    def body(c_ref, a_ref, b_ref, o_ref):
        o_ref[...] = (a_ref[...] + b_ref[...]).astype(o_ref.dtype)

    return pl.pallas_call(
        body, name=name,
        grid_spec=pltpu.PrefetchScalarGridSpec(num_scalar_prefetch=1, grid=grid, in_specs=[own, oth], out_specs=oth),
        out_shape=_sds(recv.shape, CDT), compiler_params=_cp(("parallel", "parallel")),
    )(cidx, dw, recv)


def _add_pairs(xs, ys, name):
    n = len(xs)

    def body(*refs):
        for p in range(n):
            refs[2 * n + p][...] = refs[p][...] + refs[n + p][...]

    return pl.pallas_call(body, name=name, out_shape=[_sds(a.shape, F32) for a in xs])(*xs, *ys)


def _adaln_part(c_all, w_ada, b_part, dep=None):
    dep_specs, dep_args = _after(dep)
    vmem = pl.BlockSpec(memory_space=pltpu.VMEM)

    def body(c_ref, w_ref, b_ref, *rest):
        rest[-1][...] = _dot(c_ref[...].astype(CDT), w_ref[...].astype(CDT), NN) + b_ref[...]

    return pl.pallas_call(body, name="adaln_part", in_specs=[vmem, vmem, vmem] + dep_specs, out_specs=vmem,
                          out_shape=_sds((8, w_ada.shape[1]), F32),
                          compiler_params=_cp())(c_all, w_ada, b_part, *dep_args)


def _adam_math(w, g, m, v):
    m = ADAM_B1 * m + (1.0 - ADAM_B1) * g
    v = ADAM_B2 * v + (1.0 - ADAM_B2) * (g * g)
    m_hat = m / (1.0 - ADAM_B1 ** ADAM_STEP)
    v_hat = v / (1.0 - ADAM_B2 ** ADAM_STEP)
    delta = -ADAM_LR * (m_hat / (jnp.sqrt(v_hat) + ADAM_EPS) + ADAM_WD * w)
    return delta, m, v


def _adam(gparts, w, m, v, name):
    P, R, C = gparts.shape
    tr = R if R <= 256 else 256

    def body(g_ref, w_ref, m_ref, v_ref, go_ref, d_ref, mo_ref, vo_ref):
        g = g_ref[0].astype(F32)
        for i in range(1, P):
            g = g + g_ref[i].astype(F32)
        d, mn, vn = _adam_math(w_ref[...], g, m_ref[...], v_ref[...])
        go_ref[...] = g
        d_ref[...] = d
        mo_ref[...] = mn
        vo_ref[...] = vn

    t = pl.BlockSpec((tr, C), lambda i: (i, 0))
    return pl.pallas_call(
        body, name=name, grid=(R // tr,),
        in_specs=[pl.BlockSpec((P, tr, C), lambda i: (0, i, 0)), t, t, t], out_specs=[t, t, t, t],
        out_shape=[_sds((R, C), F32)] * 4, compiler_params=_cp(("parallel",)),
    )(gparts, w, m, v)


def _sum_parts(gparts, name):
    P, R, C = gparts.shape

    def body(g_ref, o_ref):
        g = g_ref[0]
        for i in range(1, P):
            g = g + g_ref[i]
        o_ref[...] = g

    return pl.pallas_call(body, name=name, out_shape=_sds((R, C), F32))(gparts)


def _adam_w_ada(c_t, dm, w, m, v):
    R, C = w.shape

    def body(c_ref, dm_ref, w_ref, m_ref, v_ref, go_ref, d_ref, mo_ref, vo_ref):
        g = c_ref[:, 0:1] * dm_ref[0:1, :]
        for b in range(1, 8):
            g = g + c_ref[:, b:b + 1] * dm_ref[b:b + 1, :]
        d, mn, vn = _adam_math(w_ref[...], g, m_ref[...], v_ref[...])
        go_ref[...] = g
        d_ref[...] = d
        mo_ref[...] = mn
        vo_ref[...] = vn

    return pl.pallas_call(body, name="adam_w_ada", out_shape=[_sds((R, C), F32)] * 4,
                          compiler_params=_cp())(c_t, dm, w, m, v)


def kernel(x, c, norm_g, w_ada, b_ada, w_in, pool_w, pool_scale, w_attn_br, w_pool_br, w_out, rel_bias, final_g, loss_target, m_norm_g, m_w_ada, m_b_ada, m_w_in, m_pool_w, m_pool_scale, m_w_attn_br, m_w_pool_br, m_w_out, m_rel_bias, m_final_g, v_norm_g, v_w_ada, v_b_ada, v_w_in, v_pool_w, v_pool_scale, v_w_attn_br, v_w_pool_br, v_w_out, v_rel_bias, v_final_g):
    S = x.shape[1]
    px, py, pc = _my_place()
    me = 4 * px + 2 * py + pc
    cidx = jnp.reshape(pc, (1,)).astype(jnp.int32)

    layout = [(1, 1024), (1, 128), (1, 128), (0, 128)]
    final_g2 = final_g.reshape(1, D)
    me1 = jnp.reshape(me, (1,)).astype(jnp.int32)

    (c_all,) = _all_gather_small([c], "gather_c")
    c_all = c_all.reshape(8, D)

    b_part = lax.dynamic_slice(b_ada, (0, me * 384), (1, 384))
    mod_part = _adaln_part(c_all, w_ada[0], b_part)
    (mod_all,) = _all_gather_small([mod_part], "gather_mod")
    mod = lax.dynamic_index_in_dim(mod_all, me, axis=1, keepdims=False).reshape(1, 3 * D)
    shift, scale, gate = mod[:, :D], mod[:, D:2 * D], mod[:, 2 * D:]

    gplan = _gather_chips_plan([(None, 1)] * 4)
    lands = _place_slot([w_in[0], w_attn_br[0], w_pool_br[0], w_out[0]], me1, 8, "place_own_shards", CDT)
    g_sems, _, lands, tok = _split_start("gather_chips_start", [], lands, 8, gplan, dep=mod_all)
    st = _local_pre(x[0], shift, scale, norm_g, rel_bias, dep=tok)
    _, lands = _split_wait("gather_chips_wait", g_sems, [], lands, gplan,
                           [st["ht"]] + [b for pair in st["biases"] for b in pair])
    w_in_full, wab8, wpb8, wout8 = _pair_share(lands)
    wab, wpb = _cols_from_blocks(wab8, "w_attn_br_cols"), _cols_from_blocks(wpb8, "w_pool_br_cols")
    wout = wout8.reshape(D, D)

    st = _local_grads(st, loss_target[0], gate, w_in_full, pool_w[0], pool_scale, wab, wpb, wout, final_g2)

    early = [st["dpw"].reshape(AW, 128), st["dps"], st["gfg"], jnp.broadcast_to(st["loss"], (1, 128))]
    dws = [st["dw_in"], st["dw_ab"], st["dw_pb"], st["dw_out"]]
    pplan = _pair_plan(layout)
    pair_shapes = [(4, w, a.shape[1]) if ax == 0 else (4, a.shape[0], w) for a, (ax, w) in zip(dws, layout)]
    p_sems, srcs, recv, tok = _split_start(
        "reduce_pair_start", dws + early, [lax.empty(s, F32) for s in pair_shapes + [a.shape for a in early]], 20,
        pplan)
    d_rel_bias = _local_bias_grads(st, dep=tok)
    srcs, recv = _split_wait("reduce_pair_wait", p_sems, srcs, recv, pplan, d_rel_bias)
    parts = [_pair_add(dw, r, ax, w, cidx, f"pair_add{i}")
             for i, (dw, r, (ax, w)) in enumerate(zip(srcs[:4], recv[:4], layout))]
    parts += _add_pairs(srcs[4:], recv[4:], "pair_add_smalls")

    slot1 = jnp.reshape(2 * px + py, (1,)).astype(jnp.int32)
    cplan = _chips_plan(4)
    c_sems, parts, lands, tok = _split_start("reduce_chips_start", parts,
                                             _place_slot(parts, slot1, 4, "place_own_partials"), 24, cplan)
    grad_x, dmod, d_norm_g = _local_input_grads(st, w_in_full, norm_g, scale, dep=tok)

    dmod_all, ng_all, rb_all = _all_gather_small([dmod, d_norm_g, d_rel_bias], "gather_late")
    o_norm_g = _adam(ng_all, norm_g, m_norm_g, v_norm_g, "adam_norm_g")
    o_rel_bias = _adam(rb_all, rel_bias, m_rel_bias, v_rel_bias, "adam_rel_bias")
    o_b_ada = _adam(dmod_all, b_ada, m_b_ada, v_b_ada, "adam_b_ada")
    dm_mine = lax.dynamic_slice(dmod_all.reshape(8, 3 * D), (0, me * 384), (8, 384))
    o_w_ada = [o[None] for o in _adam_w_ada(c_all.T, dm_mine, w_ada[0], m_w_ada[0], v_w_ada[0])]

    _, (g_in, g_ab, g_pb, g_out, g_pw, g_ps, g_fg, g_loss) = _split_wait(
        "reduce_chips_wait", c_sems, parts, lands, cplan, [o_w_ada[0], o_b_ada[0], o_norm_g[0], o_rel_bias[0]])
    o_w_in = [o[None] for o in _adam(g_in, w_in[0], m_w_in[0], v_w_in[0], "adam_w_in")]
    o_w_ab = [o[None] for o in _adam(g_ab, w_attn_br[0], m_w_attn_br[0], v_w_attn_br[0], "adam_w_attn_br")]
    o_w_pb = [o[None] for o in _adam(g_pb, w_pool_br[0], m_w_pool_br[0], v_w_pool_br[0], "adam_w_pool_br")]
    o_w_out = [o[None] for o in _adam(g_out, w_out[0], m_w_out[0], v_w_out[0], "adam_w_out")]
    o_pool_w = [o.reshape(pool_w.shape) for o in _adam(g_pw, pool_w.reshape(AW, 128), m_pool_w.reshape(AW, 128),
                                                       v_pool_w.reshape(AW, 128), "adam_pool_w")]
    o_pool_scale = _adam(g_ps, pool_scale, m_pool_scale, v_pool_scale, "adam_pool_scale")
    o_final_g = [o.reshape(D) for o in _adam(g_fg, final_g2, m_final_g.reshape(1, D), v_final_g.reshape(1, D),
                                             "adam_final_g")]
    loss = _sum_parts(g_loss, "sum_loss")[0, 0]

    per_w = [o_norm_g, o_w_ada, o_b_ada, o_w_in, o_pool_w, o_pool_scale, o_w_ab, o_w_pb, o_w_out, o_rel_bias, o_final_g]
    outs = [loss, grad_x[None]]
    for k in range(4):
        outs += [o[k] for o in per_w]
    return tuple(outs)
```

```python
import functools
import math

import numpy as np
import jax
import jax.numpy as jnp
from jax import lax
from jax.experimental import pallas as pl
from jax.experimental.pallas import tpu as pltpu

F32 = jnp.float32
CDT = jnp.bfloat16
D = 1024
HD = 64
NH = 8
AW = 512
QKV_W = 4608
REST_W = 3584
R16 = 16
EPS = 1e-6
NEG = -1e30
N_BACK = 128
ADAM_LR, ADAM_B1, ADAM_B2, ADAM_EPS, ADAM_WD, ADAM_STEP = 0.001, 0.9, 0.999, 1e-08, 0.01, 10
MESH = pl.DeviceIdType.MESH
VMEM_MB = 1024 * 1024


def _cp(sem=None, vmem=48, **kw):
    if sem is not None:
        kw["dimension_semantics"] = sem
    return pltpu.CompilerParams(vmem_limit_bytes=vmem * VMEM_MB, **kw)


def _sds(shape, dtype):
    return jax.ShapeDtypeStruct(tuple(shape), dtype)


def _rest_blk(jj):
    return jnp.where(jj == 0, 9, jnp.where(jj == 6, 10, jj + 10))


def _sigmoid(z):
    return 0.5 * jnp.tanh(0.5 * z) + 0.5


def _dot(a, b, dims):
    return lax.dot_general(a, b, (dims, ((), ())), preferred_element_type=F32)


NN = ((1,), (0,))
NT = ((1,), (1,))
TN = ((0,), (0,))


def _ld(ref):
    if len(ref.shape) == 2:
        return ref[...]
    return jnp.concatenate([ref[a] for a in range(ref.shape[0])], axis=0)


def _st(ref, val, cols=None):
    val = val.astype(ref.dtype)
    if len(ref.shape) == 2:
        if cols is None:
            ref[...] = val
        else:
            ref[:, cols] = val
        return
    rows = ref.shape[1]
    for a in range(ref.shape[0]):
        if cols is None:
            ref[a] = val[a * rows:(a + 1) * rows]
        else:
            ref[a, :, cols] = val[a * rows:(a + 1) * rows]


REST_COLS = ((9, 10), (11, 16), (10, 11))
_HBM = pl.BlockSpec(memory_space=pltpu.HBM)


def _resident(shape):
    return pl.BlockSpec(shape, lambda *ids: (0,) * len(shape), pipeline_mode=pl.Buffered(1))


def _w512(w_ref, j):
    return w_ref[j // 2, :, 512 * (j % 2):512 * (j % 2 + 1)]


def _proj(h, w):
    S = h.shape[0]
    tm = 256

    def body(a_ref, w_ref, q_ref, r_ref):
        a = a_ref[...]
        for j in range(9):
            q_ref[:, 512 * j:512 * (j + 1)] = _dot(a, _w512(w_ref, j), NN).astype(q_ref.dtype)
        off = 0
        for lo, hi in REST_COLS:
            for j in range(lo, hi):
                r_ref[:, off:off + 512] = _dot(a, _w512(w_ref, j), NN)
                off += 512

    return pl.pallas_call(
        body, name="proj", grid=(S // tm,),
        in_specs=[pl.BlockSpec((tm, D), lambda i: (i, 0)), _resident((8, D, 1024))],
        out_specs=[pl.BlockSpec((tm, QKV_W), lambda i: (i, 0)), pl.BlockSpec((tm, REST_W), lambda i: (i, 0))],
        out_shape=[_sds((S, QKV_W), CDT), _sds((S, REST_W), F32)],
        compiler_params=_cp(("parallel",), vmem=56),
    )(h, w)


def _after(dep):
    return ([], []) if dep is None else ([pl.BlockSpec(memory_space=pl.ANY)], [dep])


def _dh(dqkv, drest, w, dep=None):
    S = dqkv.shape[0]
    tm = 256
    dep_specs, dep_args = _after(dep)

    def body(a_ref, b_ref, w_ref, *rest):
        o_ref = rest[-1]
        acc = _dot(a_ref[:, 0:512], _w512(w_ref, 0), NT)
        for j in range(1, 9):
            acc = acc + _dot(a_ref[:, 512 * j:512 * (j + 1)], _w512(w_ref, j), NT)
        off = 0
        for lo, hi in REST_COLS:
            for j in range(lo, hi):
                acc = acc + _dot(b_ref[:, off:off + 512], _w512(w_ref, j), NT)
                off += 512
        o_ref[...] = acc

    return pl.pallas_call(
        body, name="dh", grid=(S // tm,),
        in_specs=[pl.BlockSpec((tm, QKV_W), lambda i: (i, 0)), pl.BlockSpec((tm, REST_W), lambda i: (i, 0)),
                  _resident((8, D, 1024))] + dep_specs,
        out_specs=pl.BlockSpec((tm, D), lambda i: (i, 0)),
        out_shape=_sds((S, D), F32),
        compiler_params=_cp(("parallel",), vmem=56),
    )(dqkv, drest, w, *dep_args)


def _mm_tn(at, b, *, n_total, out_blk, name, prev=None, tn=512):
    R, S = at.shape
    nb = b.shape[1] // tn

    def body(*refs):
        refs[-1][...] = _dot(refs[0][...], refs[1][...], NN)

    in_specs = [_resident((R, S)), pl.BlockSpec((S, tn), lambda j: (0, j))]
    args = [at, b]
    aliases = {}
    if prev is not None:
        in_specs.append(pl.BlockSpec(memory_space=pl.ANY))
        args.append(prev)
        aliases = {2: 0}
    return pl.pallas_call(
        body, name=name, grid=(nb,), in_specs=in_specs,
        out_specs=pl.BlockSpec((R, tn), lambda j: (0, out_blk(j))),
        out_shape=_sds((R, n_total), F32),
        input_output_aliases=aliases,
        compiler_params=_cp(("parallel",)),
    )(*args)


def _tcast(a, name):
    M, C = a.shape
    tm = 512

    def body(a_ref, o_ref):
        o_ref[...] = a_ref[...].astype(F32).T.astype(o_ref.dtype)

    return pl.pallas_call(
        body, name=name, grid=(M // tm,),
        in_specs=[pl.BlockSpec((tm, C), lambda i: (i, 0))],
        out_specs=pl.BlockSpec((C, tm), lambda i: (0, i)),
        out_shape=_sds((C, M), CDT), compiler_params=_cp(("parallel",)),
    )(a)


TOK = R16 * R16


def _swap16():
    i = lax.broadcasted_iota(jnp.int32, (TOK, TOK), 0)
    j = lax.broadcasted_iota(jnp.int32, (TOK, TOK), 1)
    return jnp.where(j == jnp.bitwise_and(i, 15) * 16 + jnp.right_shift(i, 4), 1.0, 0.0).astype(CDT)


def _norm_mod(x, norm_g, shift, scale, dep=None):
    S = x.shape[0]
    dep_specs, dep_args = _after(dep)

    def body(x_ref, g_ref, sh_ref, sc_ref, *rest):
        xv = x_ref[...]
        xn = xv * lax.rsqrt(jnp.mean(xv * xv, axis=-1, keepdims=True) + EPS)
        h = (xn * (g_ref[...] * (1.0 + sc_ref[...])) + sh_ref[...]).astype(CDT)
        _st(rest[-1], _dot(_swap16(), h, NN))

    vec = pl.BlockSpec((1, D), lambda i: (0, 0))
    return pl.pallas_call(
        body, name="norm_mod", grid=(S // TOK,),
        in_specs=[pl.BlockSpec((TOK, D), lambda i: (i, 0)), vec, vec, vec] + dep_specs,
        out_specs=pl.BlockSpec((R16, R16, D), lambda i: (0, i, 0)),
        out_shape=_sds((R16, S // R16, D), CDT), compiler_params=_cp(("parallel",)),
    )(x, norm_g, shift, scale, *dep_args)


def _t5_bucket_np(n):
    nf = np.maximum(n, 1).astype(np.float32)
    large = 16 + (np.log(nf / np.float32(16)) / np.float32(math.log(2048 / 16)) * np.float32(16)).astype(np.int32)
    large = np.minimum(large, 31)
    return np.where(n < 16, n, large).astype(np.int32)


class _Geom:
    def __init__(self, gi, LS):
        self.gi, self.LS = gi, LS
        self.dil = (1, 4, 16)[gi]
        if gi == 0:
            self.lead, self.rows, self.n_tiles, self.has_prev = (R16,), 16, LS // 16, True
            a = np.arange(R16)[:, None]
            ll = np.arange(16)[None, :]
            self.pos = (16 * ll + a).reshape(-1)
        elif gi == 1:
            self.lead, self.rows, self.n_tiles, self.has_prev = (4,), 64, LS // 64, True
            a = np.arange(4)[:, None]
            ll = np.arange(64)[None, :]
            self.pos = (4 * ll + a).reshape(-1)
        else:
            self.lead, self.rows, self.n_tiles, self.has_prev = (), LS, 1, False
            self.pos = np.arange(LS)
        self.T = self.pos.shape[0]
        self.Tk = 2 * self.T if self.has_prev else self.T
        self.outer = {0: (), 1: (4,), 2: (R16,)}[gi]

    def view(self, arr):
        S, C = arr.shape
        if self.gi == 1:
            return arr.reshape(4, 4, self.LS, C)
        return arr.reshape(R16, self.LS, C)

    def view_shape(self, C):
        return (4, 4, self.LS, C) if self.gi == 1 else (R16, self.LS, C)

    def unview(self, arr):
        return arr.reshape(R16 * self.LS, arr.shape[-1])

    def spec(self, W, cblk, tile):
        if self.gi == 0:
            return pl.BlockSpec((R16, 16, W), lambda n: (0, tile(n), cblk))
        if self.gi == 1:
            return pl.BlockSpec((4, None, 64, W), lambda r, n: (0, r, tile(n), cblk))
        return pl.BlockSpec((None, self.LS, W), lambda r: (r, 0, cblk))

    def halves(self):
        R = self.rows
        idx = np.arange(self.T).reshape(-1, R)
        return idx[:, :R // 2].reshape(-1), idx[:, R // 2:].reshape(-1)

    def bucket_index(self, sub):
        lo, hi = self.halves()
        pos = self.pos
        if sub == 0:
            pq = pos[lo]
            pk = np.concatenate([pos[hi] - self.T, pos[lo]]) if self.has_prev else np.concatenate([pos[lo], pos[hi]])
        else:
            pq = pos[hi]
            pk = np.concatenate([pos[lo], pos[hi]])
        dist = pq[:, None] - pk[None, :]
        ok = (dist >= 0) & (dist <= N_BACK)
        bucket = _t5_bucket_np(np.clip(dist, 0, N_BACK) * self.dil)
        return np.where(ok, bucket, -1).astype(np.int32)


def _bias_table(idx, buckets, rel_bias, gi, sub, dep=None):
    T, Tk = idx.shape
    tr = 32
    dep_specs, dep_args = _after(dep)

    def body(idx_ref, rb_ref, *rest):
        h = pl.program_id(0)
        ix = idx_ref[...]
        acc = jnp.full(ix.shape, NEG, F32)
        for b in buckets:
            acc = jnp.where(ix == b, rb_ref[b, gi * NH + h], acc)
        rest[-1][...] = acc

    return pl.pallas_call(
        body, name=f"bias_table{gi}{'ab'[sub]}", grid=(NH, T // tr),
        in_specs=[pl.BlockSpec((tr, Tk), lambda h, i: (i, 0)), pl.BlockSpec(memory_space=pltpu.SMEM)] + dep_specs,
        out_specs=pl.BlockSpec((None, tr, Tk), lambda h, i: (h, i, 0)),
        out_shape=_sds((NH, T, Tk), F32), compiler_params=_cp(("parallel", "parallel")),
    )(idx, rel_bias, *dep_args)


def _bias_grad(dsum, idx, buckets, gi, sub, dep=None):
    _, T, Tk = dsum.shape
    dep_specs, dep_args = _after(dep)

    def body(ds_ref, idx_ref, *rest):
        o_ref, r_ref = rest[-2], rest[-1]
        r_ref[...] = jnp.zeros_like(r_ref)
        ix, ds = idx_ref[...], ds_ref[...]
        for b in buckets:
            r_ref[b:b + 1, :] = jnp.sum(jnp.where(ix == b, ds, 0.0), axis=0, keepdims=True)
        o_ref[...] = jnp.broadcast_to(jnp.sum(r_ref[...], axis=1, keepdims=True), (32, 128))

    out = pl.pallas_call(
        body, name=f"bias_grad{gi}{'ab'[sub]}", grid=(NH,),
        in_specs=[pl.BlockSpec((None, T, Tk), lambda h: (h, 0, 0)), pl.BlockSpec((T, Tk), lambda h: (0, 0))] + dep_specs,
        out_specs=pl.BlockSpec((None, 32, 128), lambda h: (h, 0, 0)),
        out_shape=_sds((NH, 32, 128), F32), scratch_shapes=[pltpu.VMEM((32, Tk), F32)],
        compiler_params=_cp(("parallel",)),
    )(dsum, idx, *dep_args)
    return out[:, :, 0].T


def _prev_mask(n, T, Tk):
    col = lax.broadcasted_iota(jnp.int32, (1, Tk), 1)
    return jnp.where((col < T) & (n == 0), NEG, 0.0).astype(F32)


def _split_rows(v, R):
    A, half = v.shape[0] // R, R // 2
    via_f32 = v.dtype != F32 and half % 16 != 0
    w = v.astype(F32) if via_f32 else v
    lo = jnp.concatenate([w[a * R:a * R + half] for a in range(A)], axis=0)
    hi = jnp.concatenate([w[a * R + half:(a + 1) * R] for a in range(A)], axis=0)
    return (lo.astype(v.dtype), hi.astype(v.dtype)) if via_f32 else (lo, hi)


def _merge_rows(lo, hi, R):
    half = R // 2
    A = lo.shape[0] // half
    return jnp.concatenate([x[a * half:(a + 1) * half] for a in range(A) for x in (lo, hi)], axis=0)


def _attn_fwd(qkv, biases, geom):
    S = qkv.shape[0]
    gi, T, R, has_prev = geom.gi, geom.T, geom.rows, geom.has_prev
    H2 = T // 2
    qv = geom.view(qkv)
    cq, ck, cv = 3 * gi, 3 * gi + 1, 3 * gi + 2

    def body(*refs):
        if has_prev:
            q_ref, kp_ref, kc_ref, vp_ref, vc_ref, ba_ref, bb_ref, o_ref, l_ref = refs
        else:
            q_ref, kc_ref, vc_ref, ba_ref, bb_ref, o_ref, l_ref = refs
        q_lo, q_hi = _split_rows(_ld(q_ref), R)
        kc_lo, kc_hi = _split_rows(_ld(kc_ref), R)
        vc_lo, vc_hi = _split_rows(_ld(vc_ref), R)
        k_b = jnp.concatenate([kc_lo, kc_hi], axis=0)
        v_b = jnp.concatenate([vc_lo, vc_hi], axis=0)
        if has_prev:
            n = pl.program_id(len(geom.outer))
            k_a = jnp.concatenate([_split_rows(_ld(kp_ref), R)[1], kc_lo], axis=0)
            v_a = jnp.concatenate([_split_rows(_ld(vp_ref), R)[1], vc_lo], axis=0)
            pm = _prev_mask(n, H2, T)
        else:
            k_a, v_a, pm = k_b, v_b, None
        low = lax.broadcasted_iota(jnp.int32, (1, 2 * HD), 1) < HD
        for j in range(NH // 2):
            cols = slice(2 * HD * j, 2 * HD * (j + 1))
            o_half, l_half = [], []
            for q, k, v, b_ref, mask in ((q_lo, k_a, v_a, ba_ref, pm), (q_hi, k_b, v_b, bb_ref, None)):
                q2 = q[:, cols].astype(F32) * 0.125
                k2, v2 = k[:, cols], v[:, cols]
                qs = jnp.concatenate([jnp.where(low, q2, 0.0), jnp.where(low, 0.0, q2)], axis=0).astype(q.dtype)
                s = _dot(qs, k2, NT) + b_ref[j]
                if mask is not None:
                    s = s + mask
                m = jnp.max(s, axis=1, keepdims=True)
                p = jnp.exp(s - m)
                l = jnp.sum(p, axis=1, keepdims=True)
                pv = _dot(p.astype(v.dtype), v2, NN) / l
                lse = m + jnp.log(l)
                o_half.append(jnp.where(low, pv[:H2], pv[H2:]))
                l_half.append(jnp.where(low, lse[:H2], lse[H2:]))
            _st(o_ref, _merge_rows(o_half[0], o_half[1], R), cols)
            _st(l_ref, _merge_rows(l_half[0], l_half[1], R), cols)

    cur = lambda n: n
    prev = lambda n: jnp.maximum(n - 1, 0)
    if has_prev:
        in_specs = [geom.spec(AW, cq, cur), geom.spec(AW, ck, prev), geom.spec(AW, ck, cur),
                    geom.spec(AW, cv, prev), geom.spec(AW, cv, cur)]
        args = [qv] * 5
    else:
        in_specs = [geom.spec(AW, cq, cur), geom.spec(AW, ck, cur), geom.spec(AW, cv, cur)]
        args = [qv] * 3
    nd = len(geom.outer) + (1 if has_prev else 0)
    biases = [b.reshape(NH // 2, 2 * H2, b.shape[2]) for b in biases]
    in_specs += [pl.BlockSpec(b.shape, lambda *ids: (0, 0, 0)) for b in biases]
    grid = geom.outer + ((geom.n_tiles,) if has_prev else ())
    oshape = geom.view_shape(AW)
    o, l = pl.pallas_call(
        body, name=f"attn_fwd{gi}", grid=grid, in_specs=in_specs,
        out_specs=[geom.spec(AW, 0, cur), geom.spec(AW, 0, cur)],
        out_shape=[_sds(oshape, F32), _sds(oshape, F32)],
        compiler_params=_cp(("arbitrary",) * nd),
    )(*args, *biases)
    return geom.unview(o), geom.unview(l)


def _attn_combine(os_, ls_, rest):
    S = rest.shape[0]
    tm = 512

    def body(o0, o1, o2, l0, l1, l2, z_ref, attn_ref, L_ref, ag_ref, agt_ref):
        la, lb, lc = l0[...], l1[...], l2[...]
        m = jnp.maximum(jnp.maximum(la, lb), lc)
        ea, eb, ec = jnp.exp(la - m), jnp.exp(lb - m), jnp.exp(lc - m)
        den = ea + eb + ec
        attn = (ea * o0[...] + eb * o1[...] + ec * o2[...]) / den
        attn_ref[...] = attn
        L_ref[...] = m + jnp.log(den)
        z = z_ref[...]
        ag = attn * (z * _sigmoid(z))
        ag_ref[...] = ag.astype(ag_ref.dtype)
        agt_ref[...] = ag.T.astype(agt_ref.dtype)

    t = pl.BlockSpec((tm, AW), lambda i: (i, 0))
    return pl.pallas_call(
        body, name="attn_combine", grid=(S // tm,),
        in_specs=[t] * 6 + [pl.BlockSpec((tm, AW), lambda i: (i, 0))],
        out_specs=[t, t, t, pl.BlockSpec((AW, tm), lambda i: (0, i))],
        out_shape=[_sds((S, AW), F32), _sds((S, AW), F32), _sds((S, AW), CDT), _sds((AW, S), CDT)],
        compiler_params=_cp(("parallel",)),
    )(*os_, *ls_, rest)


def _pool_counts(r, LS, win):
    l = lax.broadcasted_iota(jnp.int32, (LS, 1), 0)
    return jnp.minimum(16 * l + (r + 1), win).astype(F32)


def _pool_fwd(rest, pool_w, pool_scale):
    S = rest.shape[0]
    LS = S // R16
    r3 = rest.reshape(R16, LS, REST_W)
    PG = 128

    def body(u_ref, z_ref, pw_ref, ps_ref, pt_ref, mp_ref, pg_ref, pgt_ref, e_ref):
        g = pl.program_id(0)
        win = jnp.left_shift(2, g)
        row = lax.broadcasted_iota(jnp.int32, (LS, PG), 0)
        for r in range(R16):
            e_ref[R16 + r] = u_ref[r]
        for r in range(1, R16):
            e_ref[r] = jnp.where(row == 0, 0.0, pltpu.roll(u_ref[r], 1, 0))

        def double(s):
            for i in range(2 * R16 - 1, 2 * s - 1, -1):
                e_ref[i] = e_ref[i] + e_ref[i - s]

        double(1)
        for k in (1, 2, 3):
            pl.when(g >= k)(functools.partial(double, 2 ** k))
        pw = pw_ref[...].astype(CDT)
        for r in range(R16):
            u = u_ref[r]
            pooled = e_ref[R16 + r] / _pool_counts(r, LS, win) - u
            pc = pooled.astype(CDT)
            mp = _dot(pc, pw, NN)
            z = z_ref[r]
            pg = (mp * ps_ref[...]) * (z * _sigmoid(z))
            mp_ref[r] = mp
            pg_ref[r] = pg.astype(pg_ref.dtype)
            pt_ref[:, r * LS:(r + 1) * LS] = pooled.T.astype(pt_ref.dtype)
            pgt_ref[:, r * LS:(r + 1) * LS] = pg.T.astype(pgt_ref.dtype)

    col = lambda off: pl.BlockSpec((R16, LS, PG), lambda g: (0, 0, off + g))
    tsp = pl.BlockSpec((PG, S), lambda g: (g, 0))
    pt, mp, pg, pgt = pl.pallas_call(
        body, name="pool_fwd", grid=(4,),
        in_specs=[col(24), col(4), pl.BlockSpec((None, PG, PG), lambda g: (g, 0, 0)),
                  pl.BlockSpec((1, PG), lambda g: (0, g))],
        out_specs=[tsp, col(0), col(0), tsp],
        out_shape=[_sds((AW, S), CDT), _sds((R16, LS, AW), F32), _sds((R16, LS, AW), CDT), _sds((AW, S), CDT)],
        scratch_shapes=[pltpu.VMEM((2 * R16, LS, PG), F32)],
        compiler_params=_cp(("parallel",)),
    )(r3, r3, pool_w, pool_scale)
    return pt, mp.reshape(S, AW), pg.reshape(S, AW), pgt


def _branch_merge(ag, pg, wab, wpb, rest):
    S = ag.shape[0]
    tm, tn = 512, 512

    def body(ag_ref, pg_ref, wa_ref, wp_ref, ga_ref, gp_ref, m_ref, mt_ref):
        ya = _dot(ag_ref[...], wa_ref[...], NN)
        yp = _dot(pg_ref[...], wp_ref[...], NN)
        mg = _sigmoid(ga_ref[...]) * ya + _sigmoid(gp_ref[...]) * yp
        m_ref[...] = mg.astype(m_ref.dtype)
        mt_ref[...] = mg.T.astype(mt_ref.dtype)

    a = pl.BlockSpec((tm, AW), lambda i, j: (i, 0))
    w = pl.BlockSpec((AW, tn), lambda i, j: (0, j))
    o = pl.BlockSpec((tm, tn), lambda i, j: (i, j))
    return pl.pallas_call(
        body, name="branch_merge", grid=(S // tm, D // tn),
        in_specs=[a, a, w, w, pl.BlockSpec((tm, tn), lambda i, j: (i, 2 + j)),
                  pl.BlockSpec((tm, tn), lambda i, j: (i, 4 + j))],
        out_specs=[o, pl.BlockSpec((tn, tm), lambda i, j: (j, i))],
        out_shape=[_sds((S, D), CDT), _sds((D, S), CDT)],
        compiler_params=_cp(("parallel", "parallel")),
    )(ag, pg, wab, wpb, rest, rest)


def _out_loss(merged, wout, x, target, gate, final_g):
    S = x.shape[0]

    def body(m_ref, w_ref, x_ref, t_ref, gate_ref, fg_ref, dx2_ref, dmo_ref, loss_ref, gfg_ref, dgate_ref):
        i = pl.program_id(0)

        @pl.when(i == 0)
        def _():
            loss_ref[...] = jnp.zeros_like(loss_ref)
            gfg_ref[...] = jnp.zeros_like(gfg_ref)
            dgate_ref[...] = jnp.zeros_like(dgate_ref)

        swap = _swap16()
        mo = _dot(_dot(swap, _ld(m_ref), NN).astype(CDT), w_ref[...], NN)
        gate, fg = gate_ref[...], fg_ref[...]
        x2 = x_ref[...] + gate * mo
        r2 = lax.rsqrt(jnp.mean(x2 * x2, axis=-1, keepdims=True) + EPS)
        xn2 = x2 * r2
        diff = xn2 * fg - t_ref[...]
        loss = jnp.sum(jnp.sum(diff * diff, axis=-1, keepdims=True), axis=0, keepdims=True)
        dy = diff * (1.0 / D)
        dxn2 = dy * fg
        dx2 = r2 * (dxn2 - xn2 * jnp.mean(dxn2 * xn2, axis=-1, keepdims=True))
        dx2_ref[...] = dx2
        _st(dmo_ref, _dot(swap, (dx2 * gate).astype(CDT), NN))
        loss_ref[...] += jnp.broadcast_to(loss * (0.5 / D), loss_ref.shape)
        gfg_ref[...] += jnp.sum(dy * xn2, axis=0, keepdims=True)
        dgate_ref[...] += jnp.sum(dx2 * mo, axis=0, keepdims=True)

    slab = pl.BlockSpec((R16, R16, D), lambda i: (0, i, 0))
    nat = pl.BlockSpec((TOK, D), lambda i: (i, 0))
    vec = pl.BlockSpec((1, D), lambda i: (0, 0))
    return pl.pallas_call(
        body, name="out_loss", grid=(S // TOK,),
        in_specs=[slab, _resident((D, D)), nat, nat, vec, vec],
        out_specs=[nat, slab, pl.BlockSpec((1, 128), lambda i: (0, 0)), vec, vec],
        out_shape=[_sds((S, D), F32), _sds((R16, S // R16, D), CDT), _sds((1, 128), F32), _sds((1, D), F32),
                   _sds((1, D), F32)],
        compiler_params=_cp(("arbitrary",)),
    )(merged, wout, x, target, gate, final_g)


def _bwd_gates(dmo, wout, wab, wpb, rest, ag, pg, attn, mp, pool_scale):
    S = dmo.shape[0]
    tm = 256

    def body(dmo_ref, wo_ref, wa_ref, wp_ref, za_ref, zp_ref, ga0, ga1, gp0, gp1, ag_ref, pg_ref, at_ref, mp_ref,
             ps_ref, dya_ref, dyp_ref, dr_ref, dat_ref, dl_ref, dmp_ref, dps_ref):
        i = pl.program_id(0)

        @pl.when(i == 0)
        def _():
            dps_ref[...] = jnp.zeros_like(dps_ref)

        dm = _dot(dmo_ref[...], wo_ref[...], NT)
        sa = _sigmoid(jnp.concatenate([ga0[...], ga1[...]], axis=1))
        sp = _sigmoid(jnp.concatenate([gp0[...], gp1[...]], axis=1))
        dya = (dm * sa).astype(CDT)
        dyp = (dm * sp).astype(CDT)
        dya_ref[...] = dya
        dyp_ref[...] = dyp
        ya = _dot(ag_ref[...], wa_ref[...], NN)
        yp = _dot(pg_ref[...], wp_ref[...], NN)
        dr_ref[:, 1024:2048] = (dm * ya * sa * (1.0 - sa)).astype(dr_ref.dtype)
        dr_ref[:, 2048:3072] = (dm * yp * sp * (1.0 - sp)).astype(dr_ref.dtype)
        da = _dot(dya, wa_ref[...], NT)
        dp = _dot(dyp, wp_ref[...], NT)
        za = za_ref[...]
        sga = _sigmoid(za)
        attn = at_ref[...]
        dattn = da * (za * sga)
        dr_ref[:, 0:512] = (da * attn * (sga * (1.0 + za * (1.0 - sga)))).astype(dr_ref.dtype)
        dat_ref[...] = dattn.astype(dat_ref.dtype)
        prod = dattn * attn
        for h in range(NH):
            sl = slice(HD * h, HD * h + HD)
            dl_ref[:, sl] = jnp.broadcast_to(jnp.sum(prod[:, sl], axis=1, keepdims=True), (tm, HD))
        zp = zp_ref[...]
        sgp = _sigmoid(zp)
        mpre = mp_ref[...]
        ps = ps_ref[...]
        dmixed = dp * (zp * sgp)
        dr_ref[:, 512:1024] = (dp * (mpre * ps) * (sgp * (1.0 + zp * (1.0 - sgp)))).astype(dr_ref.dtype)
        dps_ref[...] += jnp.sum(dmixed * mpre, axis=0, keepdims=True)
        dmp_ref[...] = (dmixed * ps).astype(dmp_ref.dtype)

    full = lambda shape: pl.BlockSpec(shape, lambda i: (0, 0))
    tD = pl.BlockSpec((tm, D), lambda i: (i, 0))
    tA = pl.BlockSpec((tm, AW), lambda i: (i, 0))
    rb = lambda c: pl.BlockSpec((tm, AW), lambda i: (i, c))
    return pl.pallas_call(
        body, name="bwd_gates", grid=(S // tm,),
        in_specs=[tD, full((D, D)), full((AW, D)), full((AW, D)), rb(0), rb(1), rb(2), rb(3), rb(4), rb(5),
                  tA, tA, tA, tA, full((1, AW))],
        out_specs=[tD, tD, pl.BlockSpec((tm, 3072), lambda i: (i, 0)), tA, tA, tA, full((1, AW))],
        out_shape=[_sds((S, D), CDT), _sds((S, D), CDT), _sds((S, REST_W), CDT), _sds((S, AW), CDT),
                   _sds((S, AW), F32), _sds((S, AW), CDT), _sds((1, AW), F32)],
        compiler_params=_cp(("arbitrary",), vmem=56),
    )(dmo, wout, wab, wpb, rest, rest, rest, rest, rest, rest, ag, pg, attn, mp, pool_scale)


def _pool_bwd(dmp, pt, pool_w, drest):
    S = dmp.shape[0]
    LS = S // R16
    PG = 128
    d3 = dmp.reshape(R16, LS, AW)
    dr3 = drest.reshape(R16, LS, REST_W)

    def body(d_ref, pt_ref, pw_ref, dr_in, du_ref, dpw_ref, g_ref, s_ref):
        del dr_in
        g = pl.program_id(0)
        win = jnp.left_shift(2, g)
        row = lax.broadcasted_iota(jnp.int32, (LS, PG), 0)
        pw = pw_ref[...].astype(CDT)
        dpw = jnp.zeros((PG, PG), F32)
        for r in range(R16):
            dm = d_ref[r]
            dpw = dpw + _dot(pt_ref[:, r * LS:(r + 1) * LS], dm, NN)
            dpooled = _dot(dm, pw, NT)
            g_ref[r] = dpooled
            sc = dpooled / _pool_counts(r, LS, win)
            s_ref[r] = sc
            if r < R16 - 1:
                s_ref[R16 + r] = jnp.where(row == LS - 1, 0.0, pltpu.roll(sc, LS - 1, 0))
        dpw_ref[...] = dpw

        def double(s):
            for i in range(0, 2 * R16 - 2 * s):
                s_ref[i] = s_ref[i] + s_ref[i + s]

        double(1)
        for k in (1, 2, 3):
            pl.when(g >= k)(functools.partial(double, 2 ** k))
        for r in range(R16):
            du_ref[r] = (s_ref[r] - g_ref[r]).astype(du_ref.dtype)

    du, dpw = pl.pallas_call(
        body, name="pool_bwd", grid=(4,),
        in_specs=[pl.BlockSpec((R16, LS, PG), lambda g: (0, 0, g)), pl.BlockSpec((PG, S), lambda g: (g, 0)),
                  pl.BlockSpec((None, PG, PG), lambda g: (g, 0, 0)), pl.BlockSpec(memory_space=pl.ANY)],
        out_specs=[pl.BlockSpec((R16, LS, PG), lambda g: (0, 0, 24 + g)),
                   pl.BlockSpec((None, PG, PG), lambda g: (g, 0, 0))],
        out_shape=[_sds((R16, LS, REST_W), CDT), _sds((4, PG, PG), F32)],
        scratch_shapes=[pltpu.VMEM((R16, LS, PG), F32), pltpu.VMEM((2 * R16, LS, PG), F32)],
        input_output_aliases={3: 0},
        compiler_params=_cp(("parallel",)),
    )(d3, pt, pool_w, dr3)
    return du.reshape(S, REST_W), dpw


def _attn_bwd(qkv, dattn, Lb, dlb, biases, geom, dqkv_prev):
    S = qkv.shape[0]
    gi, T, R, has_prev, N = geom.gi, geom.T, geom.rows, geom.has_prev, geom.n_tiles
    H2 = T // 2
    nd_outer = len(geom.outer)
    qv = geom.view(qkv)
    cq, ck, cv = 3 * gi, 3 * gi + 1, 3 * gi + 2
    W3 = 3 * AW

    def heads(q_ref, kp_ref, kc_ref, vp_ref, vc_ref, do_ref, L_ref, dl_ref, ba_ref, bb_ref, pm, dsa_ref, dsb_ref, emit):
        q_lo, q_hi = _split_rows(_ld(q_ref), R)
        do_lo, do_hi = _split_rows(_ld(do_ref), R)
        L_lo, L_hi = _split_rows(_ld(L_ref), R)
        dl_lo, dl_hi = _split_rows(_ld(dl_ref), R)
        kc_lo, kc_hi = _split_rows(_ld(kc_ref), R)
        vc_lo, vc_hi = _split_rows(_ld(vc_ref), R)
        k_b = jnp.concatenate([kc_lo, kc_hi], axis=0)
        v_b = jnp.concatenate([vc_lo, vc_hi], axis=0)
        if has_prev:
            k_a = jnp.concatenate([_split_rows(_ld(kp_ref), R)[1], kc_lo], axis=0)
            v_a = jnp.concatenate([_split_rows(_ld(vp_ref), R)[1], vc_lo], axis=0)
        else:
            k_a, v_a = k_b, v_b
        low = lax.broadcasted_iota(jnp.int32, (1, 2 * HD), 1) < HD
        for j in range(NH // 2):
            sl = slice(2 * HD * j, 2 * HD * (j + 1))
            res = []
            for q, k, v, do, L, dl, b_ref, mask, ds_ref in (
                    (q_lo, k_a, v_a, do_lo, L_lo, dl_lo, ba_ref, pm, dsa_ref),
                    (q_hi, k_b, v_b, do_hi, L_hi, dl_hi, bb_ref, None, dsb_ref)):
                q2 = q[:, sl].astype(F32) * 0.125
                do2 = do[:, sl].astype(F32)
                k2, v2 = k[:, sl], v[:, sl]
                L2, dl2 = L[:, sl], dl[:, sl]
                qs = jnp.concatenate([jnp.where(low, q2, 0.0), jnp.where(low, 0.0, q2)], axis=0).astype(q.dtype)
                dom = jnp.concatenate([jnp.where(low, do2, 0.0), jnp.where(low, 0.0, do2)], axis=0).astype(q.dtype)
                Lc = jnp.concatenate([L2[:, 0:1], L2[:, HD:HD + 1]], axis=0)
                dlc = jnp.concatenate([dl2[:, 0:1], dl2[:, HD:HD + 1]], axis=0)
                s = _dot(qs, k2, NT) + b_ref[j]
                if mask is not None:
                    s = s + mask
                p = jnp.exp(s - Lc)
                ds = p * (_dot(dom, v2, NT) - dlc)
                ds_ref[j] += ds
                dsc = ds.astype(q.dtype)
                dq_st = _dot(dsc, k2, NN)
                res.append((jnp.where(low, dq_st[:H2], dq_st[H2:]) * 0.125, _dot(dsc, qs, TN),
                            _dot(p.astype(q.dtype), dom, TN)))
            (dq_lo, dk_a, dv_a), (dq_hi, dk_b, dv_b) = res
            dq = _merge_rows(dq_lo, dq_hi, R)
            if has_prev:
                emit(sl, dq, _merge_rows(dk_a[H2:] + dk_b[:H2], dk_b[H2:], R),
                     _merge_rows(dv_a[H2:] + dv_b[:H2], dv_b[H2:], R), dk_a[:H2], dv_a[:H2])
            else:
                emit(sl, dq, _merge_rows(dk_a[:H2] + dk_b[:H2], dk_a[H2:] + dk_b[H2:], R),
                     _merge_rows(dv_a[:H2] + dv_b[:H2], dv_a[H2:] + dv_b[H2:], R), None, None)

    def first_step():
        ok = pl.program_id(nd_outer) == 0 if has_prev else pl.program_id(0) == 0
        for a in range(nd_outer if has_prev else 0):
            ok = ok & (pl.program_id(a) == 0)
        return ok

    if has_prev:
        def body(q_ref, kp_ref, kc_ref, vp_ref, vc_ref, do_ref, L_ref, dl_ref, ba_ref, bb_ref, prev_in, out_ref,
                 dsa_ref, dsb_ref, hold, new, pp):
            del prev_in
            s_id = pl.program_id(nd_outer)

            @pl.when(first_step())
            def _():
                dsa_ref[...] = jnp.zeros_like(dsa_ref)
                dsb_ref[...] = jnp.zeros_like(dsb_ref)

            @pl.when(s_id < N)
            def _():
                def emit(sl, dq, dk, dv, dk_prev, dv_prev):
                    new[:, sl] = dq
                    new[:, slice(AW + sl.start, AW + sl.stop)] = dk
                    new[:, slice(2 * AW + sl.start, 2 * AW + sl.stop)] = dv
                    pp[:, sl] = dk_prev
                    pp[:, slice(AW + sl.start, AW + sl.stop)] = dv_prev

                heads(q_ref, kp_ref, kc_ref, vp_ref, vc_ref, do_ref, L_ref, dl_ref, ba_ref, bb_ref,
                      _prev_mask(s_id, H2, T), dsa_ref, dsb_ref, emit)

            @pl.when((s_id >= 1) & (s_id < N))
            def _():
                _st(out_ref, hold[:, 0:AW], slice(0, AW))
                late = _merge_rows(jnp.zeros((H2, 2 * AW), F32), pp[...], R)
                _st(out_ref, hold[:, AW:W3] + late, slice(AW, W3))

            @pl.when(s_id == N)
            def _():
                _st(out_ref, hold[...])

            @pl.when(s_id < N)
            def _():
                hold[...] = new[...]

        cur = lambda n: jnp.minimum(n, N - 1)
        prev = lambda n: jnp.clip(n - 1, 0, N - 1)
        delayed = lambda n: jnp.maximum(n - 1, 0)
        in_specs = [geom.spec(AW, cq, cur), geom.spec(AW, ck, prev), geom.spec(AW, ck, cur),
                    geom.spec(AW, cv, prev), geom.spec(AW, cv, cur),
                    geom.spec(AW, 0, cur), geom.spec(AW, 0, cur), geom.spec(AW, 0, cur)]
        args = [qv] * 5 + [geom.view(dattn), geom.view(Lb), geom.view(dlb)]
        out_spec = geom.spec(W3, gi, delayed)
        grid = geom.outer + (N + 1,)
        scratch = [pltpu.VMEM((T, W3), F32), pltpu.VMEM((T, W3), F32), pltpu.VMEM((H2, 2 * AW), F32)]
    else:
        def body(q_ref, kc_ref, vc_ref, do_ref, L_ref, dl_ref, ba_ref, bb_ref, prev_in, out_ref, dsa_ref, dsb_ref):
            del prev_in

            @pl.when(first_step())
            def _():
                dsa_ref[...] = jnp.zeros_like(dsa_ref)
                dsb_ref[...] = jnp.zeros_like(dsb_ref)

            def emit(sl, dq, dk, dv, dk_prev, dv_prev):
                _st(out_ref, dq, sl)
                _st(out_ref, dk, slice(AW + sl.start, AW + sl.stop))
                _st(out_ref, dv, slice(2 * AW + sl.start, 2 * AW + sl.stop))

            heads(q_ref, None, kc_ref, None, vc_ref, do_ref, L_ref, dl_ref, ba_ref, bb_ref, None, dsa_ref, dsb_ref,
                  emit)

        cur = lambda n: n
        in_specs = [geom.spec(AW, cq, cur), geom.spec(AW, ck, cur), geom.spec(AW, cv, cur),
                    geom.spec(AW, 0, cur), geom.spec(AW, 0, cur), geom.spec(AW, 0, cur)]
        args = [qv] * 3 + [geom.view(dattn), geom.view(Lb), geom.view(dlb)]
        out_spec = geom.spec(W3, gi, cur)
        grid = geom.outer
        scratch = []

    whole = lambda b: pl.BlockSpec(b.shape, lambda *ids: (0, 0, 0))
    biases = [b.reshape(NH // 2, 2 * H2, b.shape[2]) for b in biases]
    in_specs += [whole(b) for b in biases]
    in_specs.append(pl.BlockSpec(memory_space=pl.ANY))
    if dqkv_prev is None:
        last, aliases = biases[0], {}
    else:
        last, aliases = geom.view(dqkv_prev), {len(in_specs) - 1: 0}
    dq, dsa, dsb = pl.pallas_call(
        body, name=f"attn_bwd{gi}", grid=grid, in_specs=in_specs,
        out_specs=[out_spec] + [whole(b) for b in biases],
        out_shape=[_sds(geom.view_shape(QKV_W), CDT)] + [_sds(b.shape, F32) for b in biases],
        scratch_shapes=scratch,
        input_output_aliases=aliases,
        compiler_params=_cp(("arbitrary",) * len(grid), vmem=56),
    )(*args, *biases, last)
    return geom.unview(dq), [d.reshape(NH, H2, d.shape[2]) for d in (dsa, dsb)]


def _h_bwd(dh, dx2, x, norm_g, scale):
    S = x.shape[0]

    def body(dh_ref, dx2_ref, x_ref, g_ref, sc_ref, gx_ref, dsh_ref, dsc_ref, dng_ref, nat_ref):
        i = pl.program_id(0)

        @pl.when(i == 0)
        def _():
            dsh_ref[...] = jnp.zeros_like(dsh_ref)
            dsc_ref[...] = jnp.zeros_like(dsc_ref)
            dng_ref[...] = jnp.zeros_like(dng_ref)

        for r in range(R16):
            nat_ref[:, r, :] = dh_ref[r]
        dh = jnp.concatenate([nat_ref[l] for l in range(R16)], axis=0)
        g = g_ref[...]
        one_sc = 1.0 + sc_ref[...]
        xv = x_ref[...]
        r1 = lax.rsqrt(jnp.mean(xv * xv, axis=-1, keepdims=True) + EPS)
        xn = xv * r1
        dhx = dh * xn
        dxn = dh * (g * one_sc)
        dx1 = r1 * (dxn - xn * jnp.mean(dxn * xn, axis=-1, keepdims=True))
        gx_ref[...] = dx2_ref[...] + dx1
        dsh_ref[...] += jnp.sum(dh, axis=0, keepdims=True)
        dsc_ref[...] += jnp.sum(dhx * g, axis=0, keepdims=True)
        dng_ref[...] += jnp.sum(dhx * one_sc, axis=0, keepdims=True)

    nat = pl.BlockSpec((TOK, D), lambda i: (i, 0))
    vec = pl.BlockSpec((1, D), lambda i: (0, 0))
    return pl.pallas_call(
        body, name="h_bwd", grid=(S // TOK,),
        in_specs=[pl.BlockSpec((R16, R16, D), lambda i: (0, i, 0)), nat, nat, vec, vec],
        out_specs=[nat, vec, vec, vec],
        out_shape=[_sds((S, D), F32), _sds((1, D), F32), _sds((1, D), F32), _sds((1, D), F32)],
        scratch_shapes=[pltpu.VMEM((R16, R16, D), F32)],
        compiler_params=_cp(("arbitrary",)),
    )(dh, dx2, x, norm_g, scale)


def _local_step(x, target, shift, scale, gate, norm_g, w_in_full, pool_w, pool_scale, wab, wpb, wout, rel_bias,
                final_g):
    st = _local_pre(x, shift, scale, norm_g, rel_bias)
    st = _local_grads(st, target, gate, w_in_full, pool_w, pool_scale, wab, wpb, wout, final_g)
    d_rel_bias = _local_bias_grads(st)
    gx, dmod, dng = _local_input_grads(st, w_in_full, norm_g, scale)
    return dict(loss=st["loss"], grad_x=gx, dw_in=st["dw_in"], dw_ab=st["dw_ab"], dw_pb=st["dw_pb"],
                dw_out=st["dw_out"], d_norm_g=dng, d_pool_w=st["dpw"], d_pool_scale=st["dps"],
                d_rel_bias=d_rel_bias, d_final_g=st["gfg"], dmod=dmod)


def _local_pre(x, shift, scale, norm_g, rel_bias, dep=None):
    S = x.shape[0]
    LS = S // R16
    geoms = [_Geom(gi, LS) for gi in range(3)]
    idx_np = [[g.bucket_index(sub) for sub in range(2)] for g in geoms]
    buckets = [[tuple(int(b) for b in np.unique(ix) if b >= 0) for ix in pair] for pair in idx_np]
    idxs = [[jnp.asarray(ix) for ix in pair] for pair in idx_np]
    biases = [[_bias_table(idxs[gi][sub], buckets[gi][sub], rel_bias, gi, sub, dep) for sub in range(2)]
              for gi in range(3)]
    h = _norm_mod(x, norm_g, shift, scale, dep).reshape(S, D)
    return dict(geoms=geoms, idxs=idxs, buckets=buckets, biases=biases, x=x, h=h, ht=_tcast(h, "h_transpose"))


def _local_grads(st, target, gate, w_in_full, pool_w, pool_scale, wab, wpb, wout, final_g):
    geoms, idxs, biases, x, h, ht = (st[k] for k in ("geoms", "idxs", "biases", "x", "h", "ht"))
    S = h.shape[0]
    LS = S // R16
    qkv, rest = _proj(h, w_in_full)

    os_, ls_ = [], []
    for gi in range(3):
        o, l = _attn_fwd(qkv, biases[gi], geoms[gi])
        os_.append(o)
        ls_.append(l)
    attn, Lb, ag, agt = _attn_combine(os_, ls_, rest)
    pt, mp, pg, pgt = _pool_fwd(rest, pool_w, pool_scale)
    merged, mt = _branch_merge(ag, pg, wab, wpb, rest)
    dx2, dmo3, loss, gfg, dgate = _out_loss(merged.reshape(R16, LS, D), wout, x, target, gate, final_g)
    dmo = dmo3.reshape(S, D)

    dya, dyp, drest, dattn, dlb, dmp, dps = _bwd_gates(dmo, wout, wab, wpb, rest, ag, pg, attn, mp, pool_scale)
    dw_out = _mm_tn(mt, dmo, n_total=D, out_blk=lambda j: j, name="dw_out")
    dw_ab = _mm_tn(agt, dya, n_total=D, out_blk=lambda j: j, name="dw_attn_br")
    dw_pb = _mm_tn(pgt, dyp, n_total=D, out_blk=lambda j: j, name="dw_pool_br")
    drest, dpw = _pool_bwd(dmp, pt, pool_w, drest)

    dqkv = None
    dsums = []
    for gi in range(3):
        dqkv, dsum = _attn_bwd(qkv, dattn, Lb, dlb, biases[gi], geoms[gi], dqkv)
        dsums.append(dsum)

    dw_in = _mm_tn(ht, dqkv, n_total=8192, out_blk=lambda j: j, name="dw_in_qkv")
    dw_in = _mm_tn(ht, drest, n_total=8192, out_blk=_rest_blk, name="dw_in_rest", prev=dw_in)
    return dict(st, loss=loss[0, 0], dw_in=dw_in, dw_ab=dw_ab, dw_pb=dw_pb, dw_out=dw_out, dpw=dpw, dps=dps, gfg=gfg,
                dgate=dgate, dsums=dsums, dqkv=dqkv, drest=drest, dx2=dx2)


def _local_bias_grads(st, dep=None):
    drb = [_bias_grad(st["dsums"][gi][0], st["idxs"][gi][0], st["buckets"][gi][0], gi, 0, dep if gi == 0 else None) +
           _bias_grad(st["dsums"][gi][1], st["idxs"][gi][1], st["buckets"][gi][1], gi, 1) for gi in range(3)]
    return jnp.concatenate(drb, axis=1)


def _local_input_grads(st, w_in_full, norm_g, scale, dep=None):
    S = st["h"].shape[0]
    LS = S // R16
    dh = _dh(st["dqkv"], st["drest"], w_in_full, dep)
    gx, dsh, dsc, dng = _h_bwd(dh.reshape(R16, LS, D), st["dx2"], st["x"], norm_g, scale)
    return gx, jnp.concatenate([dsh, dsc, st["dgate"]], axis=1), dng


def _my_place():
    return lax.axis_index("x"), lax.axis_index("y"), lax.axis_index("c")


def _block_of(ref, axis, idx, width):
    if axis is None:
        return ref.at[idx]
    start = pl.multiple_of(idx * width, width)
    if axis == 0:
        return ref.at[pl.ds(start, width)]
    return ref.at[:, pl.ds(start, width)]


_SEM = pl.BlockSpec(memory_space=pltpu.SEMAPHORE)
_EFFECT = pltpu.SideEffectType.DATAFLOW_SIDE_EFFECTING


def _split_start(name, srcs, lands, ncopy, plan, dep=None):
    arrs = list(srcs) + list(lands)
    na, ns = len(arrs), len(srcs)
    dep_specs, dep_args = _after(dep)

    def body(*refs):
        send_sems, recv_sems, token = refs[na + len(dep_args)], refs[na + len(dep_args) + 1], refs[-1]
        for k, (s, d, dev) in enumerate(plan(refs[:ns], refs[ns:na])):
            pltpu.make_async_remote_copy(src_ref=s, dst_ref=d, send_sem=send_sems.at[k], recv_sem=recv_sems.at[k],
                                         device_id=dev, device_id_type=MESH).start()
        token[...] = jnp.zeros_like(token)

    outs = pl.pallas_call(
        body, name=name,
        out_shape=(pltpu.SemaphoreType.DMA((ncopy,)), pltpu.SemaphoreType.DMA((ncopy,)),
                   *[pltpu.HBM(a.shape, a.dtype) for a in arrs], _sds((8, 128), F32)),
        in_specs=[_HBM] * na + dep_specs,
        out_specs=(_SEM, _SEM, *[_HBM] * na, pl.BlockSpec(memory_space=pltpu.VMEM)),
        input_output_aliases={i: 2 + i for i in range(na)},
        compiler_params=pltpu.CompilerParams(has_side_effects=_EFFECT),
    )(*[pltpu.with_memory_space_constraint(a, pltpu.HBM) for a in arrs], *dep_args)
    return (outs[0], outs[1]), list(outs[2:2 + ns]), list(outs[2 + ns:2 + na]), outs[-1]


def _split_wait(name, sems, srcs, lands, plan, after):
    arrs = list(srcs) + list(lands)
    na, ns = len(arrs), len(srcs)
    after = list(after) if isinstance(after, (list, tuple)) else [after]

    def body(*refs):
        send_sems, recv_sems = refs[na], refs[na + 1]
        for k, (s, d, dev) in enumerate(plan(refs[:ns], refs[ns:na])):
            cp = pltpu.make_async_remote_copy(src_ref=s, dst_ref=d, send_sem=send_sems.at[k], recv_sem=recv_sems.at[k],
                                              device_id=dev, device_id_type=MESH)
            cp.wait_send()
            cp.wait_recv()

    outs = pl.pallas_call(
        body, name=name, out_shape=[pltpu.HBM(a.shape, a.dtype) for a in arrs],
        in_specs=[_HBM] * na + [_SEM, _SEM] + [pl.BlockSpec(memory_space=pl.ANY)] * len(after),
        out_specs=[_HBM] * na,
        input_output_aliases={i: i for i in range(na)},
        compiler_params=pltpu.CompilerParams(has_side_effects=_EFFECT),
    )(*arrs, sems[0], sems[1], *after)
    return list(outs[:ns]), list(outs[ns:])


def _gather_chips_plan(layout):
    def plan(src, land):
        x, y, c = _my_place()
        me = 4 * x + 2 * y + c
        mine = [_block_of(land[p], layout[p][0], me, layout[p][1]) for p in range(len(land))]
        return [(mine[p], mine[p], (cx, cy, c)) for p in range(len(land)) for cx, cy in [(1 - x, y), (x, 1 - y)]]
    return plan


def _pair_plan(layout):
    def plan(src, land):
        x, y, c = _my_place()
        n = len(layout)
        return [(_block_of(src[p], layout[p][0], 2 * q + (1 - c), layout[p][1]), land[p].at[q], (x, y, 1 - c))
                for p in range(n) for q in range(4)] + [(src[p], land[p], (x, y, 1 - c)) for p in range(n, len(src))]
    return plan


def _chips_plan(n):
    def plan(src, land):
        x, y, c = _my_place()
        chips = [(1 - x, y), (x, 1 - y), (1 - x, 1 - y)]
        return [(src[p].at[2 * cx + cy], land[p].at[2 * x + y], (cx, cy, c)) for p in range(n) for cx, cy in chips] + \
               [(src[p], land[p].at[2 * x + y], (cx, cy, c)) for p in range(n, len(src)) for cx, cy in chips]
    return plan


def _pair_share(gathered):
    n = len(gathered)

    def body(*refs):
        land = refs[n:2 * n]
        send_sems, recv_sems, fwd_send, fwd_recv = refs[2 * n:]
        x, y, c = _my_place()
        k_dg = 3 - (2 * x + y)
        south = c == 0
        pass_on = 4 * jnp.where(south, 1 - x, x) + 2 * jnp.where(south, y, 1 - y) + c
        to = (jnp.where(south, x, 1 - x), jnp.where(south, 1 - y, y), c)

        def forward(p):
            return pltpu.make_async_remote_copy(
                src_ref=land[p].at[pass_on], dst_ref=land[p].at[pass_on], send_sem=fwd_send.at[p],
                recv_sem=fwd_recv.at[p], device_id=to, device_id_type=MESH)

        def to_sibling(p, k):
            blk = land[p].at[2 * k + c]
            return pltpu.make_async_remote_copy(
                src_ref=blk, dst_ref=blk, send_sem=send_sems.at[4 * p + k], recv_sem=recv_sems.at[4 * p + k],
                device_id=(x, y, 1 - c), device_id_type=MESH)

        for p in range(n):
            forward(p).start()
        for p in range(n):
            for k in range(4):
                @pl.when(k != k_dg)
                def _():
                    to_sibling(p, k).start()
        for p in range(n):
            dg = land[p].at[2 * k_dg + c]
            pltpu.make_async_remote_copy(src_ref=dg, dst_ref=dg, send_sem=fwd_send.at[p], recv_sem=fwd_recv.at[p],
                                         device_id=to, device_id_type=MESH).wait_recv()
            for k in range(4):
                @pl.when(k == k_dg)
                def _():
                    to_sibling(p, k).start()
        for p in range(n):
            forward(p).wait_send()
            for k in range(4):
                to_sibling(p, k).wait_send()
                blk = land[p].at[2 * k + 1 - c]
                pltpu.make_async_remote_copy(
                    src_ref=blk, dst_ref=blk, send_sem=send_sems.at[4 * p + k], recv_sem=recv_sems.at[4 * p + k],
                    device_id=(x, y, 1 - c), device_id_type=MESH).wait_recv()

    return pl.pallas_call(
        body, name="gather_pair_share", in_specs=[_HBM] * n, out_specs=[_HBM] * n,
        out_shape=[_sds(g.shape, g.dtype) for g in gathered],
        input_output_aliases={p: p for p in range(n)},
        scratch_shapes=[pltpu.SemaphoreType.DMA((4 * n,)), pltpu.SemaphoreType.DMA((4 * n,)),
                        pltpu.SemaphoreType.DMA((n,)), pltpu.SemaphoreType.DMA((n,))],
    )(*gathered)


def _place_slot(arrs, slot, n_slots, name, out_dtype=None):
    n = len(arrs)

    def body(s_ref, *refs):
        for p in range(n):
            refs[n + p][...] = refs[p][...].astype(refs[n + p].dtype)

    def spec(a):
        tail = a.shape[-2:]
        return pl.BlockSpec((None,) + tail, lambda i, s_ref: (s_ref[0], 0, 0))

    in_specs = [spec(a) if a.ndim == 3 else pl.BlockSpec(a.shape, lambda i, s_ref: (0, 0)) for a in arrs]
    return pl.pallas_call(
        body, name=name,
        grid_spec=pltpu.PrefetchScalarGridSpec(num_scalar_prefetch=1, grid=(1,), in_specs=in_specs,
                                               out_specs=[spec(a) for a in arrs]),
        out_shape=[_sds((n_slots,) + a.shape[-2:], out_dtype or a.dtype) for a in arrs],
        compiler_params=_cp(("arbitrary",)),
    )(slot, *arrs)


def _cols_from_blocks(w8, name):
    _, R, C = w8.shape

    def body(w_ref, o_ref):
        for e in range(8):
            o_ref[:, C * e:C * (e + 1)] = w_ref[e]

    return pl.pallas_call(body, name=name, out_shape=_sds((R, 8 * C), w8.dtype), compiler_params=_cp())(w8)


def _all_gather_small(srcs, name):
    n = len(srcs)
    vmem = pl.BlockSpec(memory_space=pltpu.VMEM)

    def body(*refs):
        src, out = refs[:n], refs[n:2 * n]
        send_sems, recv_sems = refs[2 * n:]
        x, y, c = _my_place()
        me = 4 * x + 2 * y + c
        flip = lambda v, f: 1 - v if f else v
        peers = [(flip(x, fx), flip(y, fy), flip(c, fc)) for fx in (0, 1) for fy in (0, 1) for fc in (0, 1)][1:]
        cps = []
        for p in range(n):
            out[p][me] = src[p][...]
            for k, peer in enumerate(peers):
                cps.append(pltpu.make_async_remote_copy(
                    src_ref=src[p], dst_ref=out[p].at[me], send_sem=send_sems.at[7 * p + k],
                    recv_sem=recv_sems.at[7 * p + k], device_id=peer, device_id_type=MESH))
        for cp in cps:
            cp.start()
        for cp in cps:
            cp.wait()

    return pl.pallas_call(
        body, name=name, in_specs=[vmem] * n, out_specs=[vmem] * n,
        out_shape=[_sds((8,) + a.shape, a.dtype) for a in srcs],
        scratch_shapes=[pltpu.SemaphoreType.DMA((7 * n,)), pltpu.SemaphoreType.DMA((7 * n,))],
    )(*srcs)


def _adaln_exchange(c, w_ada, b_part):
    vmem = pl.BlockSpec(memory_space=pltpu.VMEM)

    def body(c_ref, w_ref, b_ref, call_ref, mod_ref, part_ref, send_sems, recv_sems):
        x, y, c = _my_place()
        me = 4 * x + 2 * y + c
        flip = lambda v, f: 1 - v if f else v
        peers = [(flip(x, fx), flip(y, fy), flip(c, fc)) for fx in (0, 1) for fy in (0, 1) for fc in (0, 1)][1:]
        call_ref[me] = c_ref[...]
        first = [pltpu.make_async_remote_copy(src_ref=c_ref, dst_ref=call_ref.at[me], send_sem=send_sems.at[k],
                                              recv_sem=recv_sems.at[k], device_id=peer, device_id_type=MESH)
                 for k, peer in enumerate(peers)]
        for cp in first:
            cp.start()
        for cp in first:
            cp.wait()
        c_all = jnp.concatenate([call_ref[j] for j in range(8)], axis=0)
        part_ref[...] = _dot(c_all.astype(CDT), w_ref[...].astype(CDT), NN) + b_ref[...]
        mod_ref[me] = part_ref[pl.ds(me, 1), :]
        second = [pltpu.make_async_remote_copy(
            src_ref=part_ref.at[pl.ds(4 * px + 2 * py + pc, 1)], dst_ref=mod_ref.at[me], send_sem=send_sems.at[7 + k],
            recv_sem=recv_sems.at[7 + k], device_id=(px, py, pc), device_id_type=MESH)
            for k, (px, py, pc) in enumerate(peers)]
        for cp in second:
            cp.start()
        for cp in second:
            cp.wait()

    return pl.pallas_call(
        body, name="adaln_exchange", in_specs=[vmem] * 3, out_specs=[vmem] * 2,
        out_shape=[_sds((8, 1, D), F32), _sds((8, 1, w_ada.shape[1]), F32)],
        scratch_shapes=[pltpu.VMEM((8, w_ada.shape[1]), F32), pltpu.SemaphoreType.DMA((14,)),
                        pltpu.SemaphoreType.DMA((14,))],
        compiler_params=_cp(),
    )(c, w_ada, b_part)


def _all_gather(srcs, out_shapes, layout, name):
    n = len(srcs)

    def body(*refs):
        src, out = refs[:n], refs[n:2 * n]
        send_sems, recv_sems, local_sems = refs[2 * n:]
        x, y, c = _my_place()
        me, sibling = (x, y, c), (x, y, 1 - c)
        chips = [(1 - x, y), (x, 1 - y), (1 - x, 1 - y)]

        def region(p, dev):
            px, py, pc = dev
            return _block_of(out[p], layout[p][0], 4 * px + 2 * py + pc, layout[p][1])

        def copy(p, k, block, to, src_ref=None):
            return pltpu.make_async_remote_copy(
                src_ref=region(p, block) if src_ref is None else src_ref, dst_ref=region(p, block),
                send_sem=send_sems.at[p * 7 + k], recv_sem=recv_sems.at[p * 7 + k],
                device_id=to, device_id_type=MESH)

        mine = [pltpu.make_async_copy(src[p], region(p, me), local_sems.at[p]) for p in range(n)]
        for cp in mine:
            cp.start()
        first = []
        for p in range(n):
            first.append(copy(p, 0, me, sibling, src[p]))
            first += [copy(p, 1 + j, me, (*chip, c), src[p]) for j, chip in enumerate(chips)]
        for cp in first:
            cp.start()
        passed = []
        for j, chip in enumerate(chips):
            for p in range(n):
                copy(p, 1 + j, (*chip, c), me).wait_recv()
                fwd = copy(p, 4 + j, (*chip, c), sibling)
                fwd.start()
                passed.append(fwd)
        for p in range(n):
            copy(p, 0, sibling, me).wait_recv()
            for j, chip in enumerate(chips):
                copy(p, 4 + j, (*chip, 1 - c), me).wait_recv()
        for cp in first + passed:
            cp.wait_send()
        for cp in mine:
            cp.wait()

    return pl.pallas_call(
        body, name=name, in_specs=[_HBM] * n, out_specs=[_HBM] * n,
        out_shape=[_sds(s, a.dtype) for s, a in zip(out_shapes, srcs)],
        scratch_shapes=[pltpu.SemaphoreType.DMA((7 * n,)), pltpu.SemaphoreType.DMA((7 * n,)),
                        pltpu.SemaphoreType.DMA((n,))],
    )(*srcs)


def _pair_add(dw, recv, axis, width, cidx, name):
    if axis == 0:
        tr = min(width, 256)
        C = dw.shape[1]
        grid = (4, width // tr)
        nb = width // tr
        own = pl.BlockSpec((tr, C), lambda q, i, c_ref: ((2 * q + c_ref[0]) * nb + i, 0))
        oth = pl.BlockSpec((None, tr, C), lambda q, i, c_ref: (q, i, 0))
    else:
        R = dw.shape[0]
        tr = min(R, 256)
        grid = (4, R // tr)
        own = pl.BlockSpec((tr, width), lambda q, i, c_ref: (i, 2 * q + c_ref[0]))
        oth = pl.BlockSpec((None, tr, width), lambda q, i, c_ref: (q, i, 0))

    def body(c_ref, a_ref, b_ref, o_ref):
        o_ref[...] = (a_ref[...] + b_ref[...]).astype(o_ref.dtype)

    return pl.pallas_call(
        body, name=name,
        grid_spec=pltpu.PrefetchScalarGridSpec(num_scalar_prefetch=1, grid=grid, in_specs=[own, oth], out_specs=oth),
        out_shape=_sds(recv.shape, CDT), compiler_params=_cp(("parallel", "parallel")),
    )(cidx, dw, recv)


def _add_pairs(xs, ys, name):
    n = len(xs)

    def body(*refs):
        for p in range(n):
            refs[2 * n + p][...] = refs[p][...] + refs[n + p][...]

    return pl.pallas_call(body, name=name, out_shape=[_sds(a.shape, F32) for a in xs])(*xs, *ys)


def _adaln_part(c_all, w_ada, b_part, dep=None):
    dep_specs, dep_args = _after(dep)
    vmem = pl.BlockSpec(memory_space=pltpu.VMEM)

    def body(c_ref, w_ref, b_ref, *rest):
        rest[-1][...] = _dot(c_ref[...].astype(CDT), w_ref[...].astype(CDT), NN) + b_ref[...]

    return pl.pallas_call(body, name="adaln_part", in_specs=[vmem, vmem, vmem] + dep_specs, out_specs=vmem,
                          out_shape=_sds((8, w_ada.shape[1]), F32),
                          compiler_params=_cp())(c_all, w_ada, b_part, *dep_args)


def _adam_math(w, g, m, v):
    m = ADAM_B1 * m + (1.0 - ADAM_B1) * g
    v = ADAM_B2 * v + (1.0 - ADAM_B2) * (g * g)
    m_hat = m / (1.0 - ADAM_B1 ** ADAM_STEP)
    v_hat = v / (1.0 - ADAM_B2 ** ADAM_STEP)
    delta = -ADAM_LR * (m_hat / (jnp.sqrt(v_hat) + ADAM_EPS) + ADAM_WD * w)
    return delta, m, v


def _adam(gparts, w, m, v, name):
    P, R, C = gparts.shape
    tr = R if R <= 256 else 256

    def body(g_ref, w_ref, m_ref, v_ref, go_ref, d_ref, mo_ref, vo_ref):
        g = g_ref[0].astype(F32)
        for i in range(1, P):
            g = g + g_ref[i].astype(F32)
        d, mn, vn = _adam_math(w_ref[...], g, m_ref[...], v_ref[...])
        go_ref[...] = g
        d_ref[...] = d
        mo_ref[...] = mn
        vo_ref[...] = vn

    t = pl.BlockSpec((tr, C), lambda i: (i, 0))
    return pl.pallas_call(
        body, name=name, grid=(R // tr,),
        in_specs=[pl.BlockSpec((P, tr, C), lambda i: (0, i, 0)), t, t, t], out_specs=[t, t, t, t],
        out_shape=[_sds((R, C), F32)] * 4, compiler_params=_cp(("parallel",)),
    )(gparts, w, m, v)


def _sum_parts(gparts, name):
    P, R, C = gparts.shape

    def body(g_ref, o_ref):
        g = g_ref[0]
        for i in range(1, P):
            g = g + g_ref[i]
        o_ref[...] = g

    return pl.pallas_call(body, name=name, out_shape=_sds((R, C), F32))(gparts)


def _adam_w_ada(c_t, dm, w, m, v):
    R, C = w.shape

    def body(c_ref, dm_ref, w_ref, m_ref, v_ref, go_ref, d_ref, mo_ref, vo_ref):
        g = c_ref[:, 0:1] * dm_ref[0:1, :]
        for b in range(1, 8):
            g = g + c_ref[:, b:b + 1] * dm_ref[b:b + 1, :]
        d, mn, vn = _adam_math(w_ref[...], g, m_ref[...], v_ref[...])
        go_ref[...] = g
        d_ref[...] = d
        mo_ref[...] = mn
        vo_ref[...] = vn

    return pl.pallas_call(body, name="adam_w_ada", out_shape=[_sds((R, C), F32)] * 4,
                          compiler_params=_cp())(c_t, dm, w, m, v)


def kernel(x, c, norm_g, w_ada, b_ada, w_in, pool_w, pool_scale, w_attn_br, w_pool_br, w_out, rel_bias, final_g, loss_target, m_norm_g, m_w_ada, m_b_ada, m_w_in, m_pool_w, m_pool_scale, m_w_attn_br, m_w_pool_br, m_w_out, m_rel_bias, m_final_g, v_norm_g, v_w_ada, v_b_ada, v_w_in, v_pool_w, v_pool_scale, v_w_attn_br, v_w_pool_br, v_w_out, v_rel_bias, v_final_g):
    S = x.shape[1]
    px, py, pc = _my_place()
    me = 4 * px + 2 * py + pc
    cidx = jnp.reshape(pc, (1,)).astype(jnp.int32)

    layout = [(1, 1024), (1, 128), (1, 128), (0, 128)]
    final_g2 = final_g.reshape(1, D)
    me1 = jnp.reshape(me, (1,)).astype(jnp.int32)

    b_part = lax.dynamic_slice(b_ada, (0, me * 384), (1, 384))
    c_all, mod_all = _adaln_exchange(c, w_ada[0], b_part)
    c_all = c_all.reshape(8, D)
    mod = mod_all.reshape(1, 3 * D)
    shift, scale, gate = mod[:, :D], mod[:, D:2 * D], mod[:, 2 * D:]

    gplan = _gather_chips_plan([(None, 1)] * 4)
    lands = _place_slot([w_in[0], w_attn_br[0], w_pool_br[0], w_out[0]], me1, 8, "place_own_shards", CDT)
    g_sems, _, lands, tok = _split_start("gather_chips_start", [], lands, 8, gplan, dep=mod_all)
    st = _local_pre(x[0], shift, scale, norm_g, rel_bias, dep=tok)
    _, lands = _split_wait("gather_chips_wait", g_sems, [], lands, gplan,
                           [st["ht"]] + [b for pair in st["biases"] for b in pair])
    w_in_full, wab8, wpb8, wout8 = _pair_share(lands)
    wab, wpb = _cols_from_blocks(wab8, "w_attn_br_cols"), _cols_from_blocks(wpb8, "w_pool_br_cols")
    wout = wout8.reshape(D, D)

    st = _local_grads(st, loss_target[0], gate, w_in_full, pool_w[0], pool_scale, wab, wpb, wout, final_g2)

    early = [st["dpw"].reshape(AW, 128), st["dps"], st["gfg"], jnp.broadcast_to(st["loss"], (1, 128))]
    dws = [st["dw_in"], st["dw_ab"], st["dw_pb"], st["dw_out"]]
    pplan = _pair_plan(layout)
    pair_shapes = [(4, w, a.shape[1]) if ax == 0 else (4, a.shape[0], w) for a, (ax, w) in zip(dws, layout)]
    p_sems, srcs, recv, tok = _split_start(
        "reduce_pair_start", dws + early, [lax.empty(s, F32) for s in pair_shapes + [a.shape for a in early]], 20,
        pplan)
    d_rel_bias = _local_bias_grads(st, dep=tok)
    srcs, recv = _split_wait("reduce_pair_wait", p_sems, srcs, recv, pplan, d_rel_bias)
    parts = [_pair_add(dw, r, ax, w, cidx, f"pair_add{i}")
             for i, (dw, r, (ax, w)) in enumerate(zip(srcs[:4], recv[:4], layout))]
    parts += _add_pairs(srcs[4:], recv[4:], "pair_add_smalls")

    slot1 = jnp.reshape(2 * px + py, (1,)).astype(jnp.int32)
    cplan = _chips_plan(4)
    c_sems, parts, lands, tok = _split_start("reduce_chips_start", parts,
                                             _place_slot(parts, slot1, 4, "place_own_partials"), 24, cplan)
    grad_x, dmod, d_norm_g = _local_input_grads(st, w_in_full, norm_g, scale, dep=tok)

    dmod_all, ng_all, rb_all = _all_gather_small([dmod, d_norm_g, d_rel_bias], "gather_late")
    o_norm_g = _adam(ng_all, norm_g, m_norm_g, v_norm_g, "adam_norm_g")
    o_rel_bias = _adam(rb_all, rel_bias, m_rel_bias, v_rel_bias, "adam_rel_bias")
    o_b_ada = _adam(dmod_all, b_ada, m_b_ada, v_b_ada, "adam_b_ada")
    dm_mine = lax.dynamic_slice(dmod_all.reshape(8, 3 * D), (0, me * 384), (8, 384))
    o_w_ada = [o[None] for o in _adam_w_ada(c_all.T, dm_mine, w_ada[0], m_w_ada[0], v_w_ada[0])]

    _, (g_in, g_ab, g_pb, g_out, g_pw, g_ps, g_fg, g_loss) = _split_wait(
        "reduce_chips_wait", c_sems, parts, lands, cplan, [o_w_ada[0], o_b_ada[0], o_norm_g[0], o_rel_bias[0]])
    o_w_in = [o[None] for o in _adam(g_in, w_in[0], m_w_in[0], v_w_in[0], "adam_w_in")]
    o_w_ab = [o[None] for o in _adam(g_ab, w_attn_br[0], m_w_attn_br[0], v_w_attn_br[0], "adam_w_attn_br")]
    o_w_pb = [o[None] for o in _adam(g_pb, w_pool_br[0], m_w_pool_br[0], v_w_pool_br[0], "adam_w_pool_br")]
    o_w_out = [o[None] for o in _adam(g_out, w_out[0], m_w_out[0], v_w_out[0], "adam_w_out")]
    o_pool_w = [o.reshape(pool_w.shape) for o in _adam(g_pw, pool_w.reshape(AW, 128), m_pool_w.reshape(AW, 128),
                                                       v_pool_w.reshape(AW, 128), "adam_pool_w")]
    o_pool_scale = _adam(g_ps, pool_scale, m_pool_scale, v_pool_scale, "adam_pool_scale")
    o_final_g = [o.reshape(D) for o in _adam(g_fg, final_g2, m_final_g.reshape(1, D), v_final_g.reshape(1, D),
                                             "adam_final_g")]
    loss = _sum_parts(g_loss, "sum_loss")[0, 0]

    per_w = [o_norm_g, o_w_ada, o_b_ada, o_w_in, o_pool_w, o_pool_scale, o_w_ab, o_w_pb, o_w_out, o_rel_bias, o_final_g]
    outs = [loss, grad_x[None]]
    for k in range(4):
        outs += [o[k] for o in per_w]
    return tuple(outs)
```

```python
import functools
import math

import numpy as np
import jax
import jax.numpy as jnp
from jax import lax
from jax.experimental import pallas as pl
from jax.experimental.pallas import tpu as pltpu

F32 = jnp.float32
CDT = jnp.bfloat16
D = 1024
HD = 64
NH = 8
AW = 512
QKV_W = 4608
REST_W = 3584
R16 = 16
EPS = 1e-6
NEG = -1e30
N_BACK = 128
ADAM_LR, ADAM_B1, ADAM_B2, ADAM_EPS, ADAM_WD, ADAM_STEP = 0.001, 0.9, 0.999, 1e-08, 0.01, 10
MESH = pl.DeviceIdType.MESH
VMEM_MB = 1024 * 1024


def _cp(sem=None, vmem=48, **kw):
    if sem is not None:
        kw["dimension_semantics"] = sem
    return pltpu.CompilerParams(vmem_limit_bytes=vmem * VMEM_MB, **kw)


def _sds(shape, dtype):
    return jax.ShapeDtypeStruct(tuple(shape), dtype)


def _rest_blk(jj):
    return jnp.where(jj == 0, 9, jnp.where(jj == 6, 10, jj + 10))


def _sigmoid(z):
    return 0.5 * jnp.tanh(0.5 * z) + 0.5


def _dot(a, b, dims):
    return lax.dot_general(a, b, (dims, ((), ())), preferred_element_type=F32)


NN = ((1,), (0,))
NT = ((1,), (1,))
TN = ((0,), (0,))


def _ld(ref):
    if len(ref.shape) == 2:
        return ref[...]
    return jnp.concatenate([ref[a] for a in range(ref.shape[0])], axis=0)


def _st(ref, val, cols=None):
    val = val.astype(ref.dtype)
    if len(ref.shape) == 2:
        if cols is None:
            ref[...] = val
        else:
            ref[:, cols] = val
        return
    rows = ref.shape[1]
    for a in range(ref.shape[0]):
        if cols is None:
            ref[a] = val[a * rows:(a + 1) * rows]
        else:
            ref[a, :, cols] = val[a * rows:(a + 1) * rows]


REST_COLS = ((9, 10), (11, 16), (10, 11))
_HBM = pl.BlockSpec(memory_space=pltpu.HBM)


def _resident(shape):
    return pl.BlockSpec(shape, lambda *ids: (0,) * len(shape), pipeline_mode=pl.Buffered(1))


def _w512(w_ref, j):
    return w_ref[j // 2, :, 512 * (j % 2):512 * (j % 2 + 1)]


class _WeightSlabs:
    @staticmethod
    def scratch(dtype):
        return [pltpu.VMEM((8, D, 1024), dtype), pltpu.SemaphoreType.DMA((8,))]

    def __init__(self, w_hbm, w_vmem, sems):
        self.w_hbm, self.w_vmem, self.sems = w_hbm, w_vmem, sems
        self.first = pl.program_id(0) == 0
        self.waited = set()

        @pl.when(self.first)
        def _():
            for e in range(8):
                self._copy(e).start()

    def _copy(self, e):
        return pltpu.make_async_copy(self.w_hbm.at[e], self.w_vmem.at[e], self.sems.at[e])

    def block(self, j):
        e = j // 2
        if e not in self.waited:
            self.waited.add(e)
            pl.when(self.first)(lambda: self._copy(e).wait())
        return _w512(self.w_vmem, j)


def _proj(h, w):
    S = h.shape[0]
    tm = 256

    def body(a_ref, w_hbm, q_ref, r_ref, w_vmem, sems):
        ws = _WeightSlabs(w_hbm, w_vmem, sems)
        a = a_ref[...]
        for j in range(9):
            q_ref[:, 512 * j:512 * (j + 1)] = _dot(a, ws.block(j), NN).astype(q_ref.dtype)
        off = 0
        for lo, hi in REST_COLS:
            for j in range(lo, hi):
                r_ref[:, off:off + 512] = _dot(a, ws.block(j), NN)
                off += 512

    return pl.pallas_call(
        body, name="proj", grid=(S // tm,),
        in_specs=[pl.BlockSpec((tm, D), lambda i: (i, 0)), pl.BlockSpec(memory_space=pl.ANY)],
        out_specs=[pl.BlockSpec((tm, QKV_W), lambda i: (i, 0)), pl.BlockSpec((tm, REST_W), lambda i: (i, 0))],
        out_shape=[_sds((S, QKV_W), CDT), _sds((S, REST_W), F32)],
        scratch_shapes=_WeightSlabs.scratch(w.dtype),
        compiler_params=_cp(("arbitrary",), vmem=56),
    )(h, w)


def _after(dep):
    return ([], []) if dep is None else ([pl.BlockSpec(memory_space=pl.ANY)], [dep])


def _dh(dqkv, drest, w, dep=None):
    S = dqkv.shape[0]
    tm = 256
    dep_specs, dep_args = _after(dep)

    def body(a_ref, b_ref, w_hbm, *rest):
        o_ref, w_vmem, sems = rest[-3], rest[-2], rest[-1]
        ws = _WeightSlabs(w_hbm, w_vmem, sems)
        acc = _dot(a_ref[:, 0:512], ws.block(0), NT)
        for j in range(1, 9):
            acc = acc + _dot(a_ref[:, 512 * j:512 * (j + 1)], ws.block(j), NT)
        off = 0
        for lo, hi in REST_COLS:
            for j in range(lo, hi):
                acc = acc + _dot(b_ref[:, off:off + 512], ws.block(j), NT)
                off += 512
        o_ref[...] = acc

    return pl.pallas_call(
        body, name="dh", grid=(S // tm,),
        in_specs=[pl.BlockSpec((tm, QKV_W), lambda i: (i, 0)), pl.BlockSpec((tm, REST_W), lambda i: (i, 0)),
                  pl.BlockSpec(memory_space=pl.ANY)] + dep_specs,
        out_specs=pl.BlockSpec((tm, D), lambda i: (i, 0)),
        out_shape=_sds((S, D), F32),
        scratch_shapes=_WeightSlabs.scratch(w.dtype),
        compiler_params=_cp(("arbitrary",), vmem=56),
    )(dqkv, drest, w, *dep_args)


def _mm_tn(at, b, *, n_total, out_blk, name, prev=None, tn=512):
    R, S = at.shape
    nb = b.shape[1] // tn

    def body(*refs):
        refs[-1][...] = _dot(refs[0][...], refs[1][...], NN)

    in_specs = [_resident((R, S)), pl.BlockSpec((S, tn), lambda j: (0, j))]
    args = [at, b]
    aliases = {}
    if prev is not None:
        in_specs.append(pl.BlockSpec(memory_space=pl.ANY))
        args.append(prev)
        aliases = {2: 0}
    return pl.pallas_call(
        body, name=name, grid=(nb,), in_specs=in_specs,
        out_specs=pl.BlockSpec((R, tn), lambda j: (0, out_blk(j))),
        out_shape=_sds((R, n_total), F32),
        input_output_aliases=aliases,
        compiler_params=_cp(("parallel",)),
    )(*args)


def _tcast(a, name):
    M, C = a.shape
    tm = 512

    def body(a_ref, o_ref):
        o_ref[...] = a_ref[...].astype(F32).T.astype(o_ref.dtype)

    return pl.pallas_call(
        body, name=name, grid=(M // tm,),
        in_specs=[pl.BlockSpec((tm, C), lambda i: (i, 0))],
        out_specs=pl.BlockSpec((C, tm), lambda i: (0, i)),
        out_shape=_sds((C, M), CDT), compiler_params=_cp(("parallel",)),
    )(a)


TOK = R16 * R16


def _swap16():
    i = lax.broadcasted_iota(jnp.int32, (TOK, TOK), 0)
    j = lax.broadcasted_iota(jnp.int32, (TOK, TOK), 1)
    return jnp.where(j == jnp.bitwise_and(i, 15) * 16 + jnp.right_shift(i, 4), 1.0, 0.0).astype(CDT)


def _norm_mod(x, norm_g, shift, scale, dep=None):
    S = x.shape[0]
    dep_specs, dep_args = _after(dep)

    def body(x_ref, g_ref, sh_ref, sc_ref, *rest):
        xv = x_ref[...]
        xn = xv * lax.rsqrt(jnp.mean(xv * xv, axis=-1, keepdims=True) + EPS)
        h = (xn * (g_ref[...] * (1.0 + sc_ref[...])) + sh_ref[...]).astype(CDT)
        _st(rest[-1], _dot(_swap16(), h, NN))

    vec = pl.BlockSpec((1, D), lambda i: (0, 0))
    return pl.pallas_call(
        body, name="norm_mod", grid=(S // TOK,),
        in_specs=[pl.BlockSpec((TOK, D), lambda i: (i, 0)), vec, vec, vec] + dep_specs,
        out_specs=pl.BlockSpec((R16, R16, D), lambda i: (0, i, 0)),
        out_shape=_sds((R16, S // R16, D), CDT), compiler_params=_cp(("parallel",)),
    )(x, norm_g, shift, scale, *dep_args)


def _t5_bucket_np(n):
    nf = np.maximum(n, 1).astype(np.float32)
    large = 16 + (np.log(nf / np.float32(16)) / np.float32(math.log(2048 / 16)) * np.float32(16)).astype(np.int32)
    large = np.minimum(large, 31)
    return np.where(n < 16, n, large).astype(np.int32)


class _Geom:
    def __init__(self, gi, LS):
        self.gi, self.LS = gi, LS
        self.dil = (1, 4, 16)[gi]
        if gi == 0:
            self.lead, self.rows, self.n_tiles, self.has_prev = (R16,), 16, LS // 16, True
            a = np.arange(R16)[:, None]
            ll = np.arange(16)[None, :]
            self.pos = (16 * ll + a).reshape(-1)
        elif gi == 1:
            self.lead, self.rows, self.n_tiles, self.has_prev = (4,), 64, LS // 64, True
            a = np.arange(4)[:, None]
            ll = np.arange(64)[None, :]
            self.pos = (4 * ll + a).reshape(-1)
        else:
            self.lead, self.rows, self.n_tiles, self.has_prev = (), LS, 1, False
            self.pos = np.arange(LS)
        self.T = self.pos.shape[0]
        self.Tk = 2 * self.T if self.has_prev else self.T
        self.outer = {0: (), 1: (4,), 2: (R16,)}[gi]

    def view(self, arr):
        S, C = arr.shape
        if self.gi == 1:
            return arr.reshape(4, 4, self.LS, C)
        return arr.reshape(R16, self.LS, C)

    def view_shape(self, C):
        return (4, 4, self.LS, C) if self.gi == 1 else (R16, self.LS, C)

    def unview(self, arr):
        return arr.reshape(R16 * self.LS, arr.shape[-1])

    def spec(self, W, cblk, tile):
        if self.gi == 0:
            return pl.BlockSpec((R16, 16, W), lambda n: (0, tile(n), cblk))
        if self.gi == 1:
            return pl.BlockSpec((4, None, 64, W), lambda r, n: (0, r, tile(n), cblk))
        return pl.BlockSpec((None, self.LS, W), lambda r: (r, 0, cblk))

    def halves(self):
        R = self.rows
        idx = np.arange(self.T).reshape(-1, R)
        return idx[:, :R // 2].reshape(-1), idx[:, R // 2:].reshape(-1)

    def bucket_index(self, sub):
        lo, hi = self.halves()
        pos = self.pos
        if sub == 0:
            pq = pos[lo]
            pk = np.concatenate([pos[hi] - self.T, pos[lo]]) if self.has_prev else np.concatenate([pos[lo], pos[hi]])
        else:
            pq = pos[hi]
            pk = np.concatenate([pos[lo], pos[hi]])
        dist = pq[:, None] - pk[None, :]
        ok = (dist >= 0) & (dist <= N_BACK)
        bucket = _t5_bucket_np(np.clip(dist, 0, N_BACK) * self.dil)
        return np.where(ok, bucket, -1).astype(np.int32)


def _bias_table(idx, buckets, rel_bias, gi, sub, dep=None):
    T, Tk = idx.shape
    tr = 32
    dep_specs, dep_args = _after(dep)

    def body(idx_ref, rb_ref, *rest):
        h = pl.program_id(0)
        ix = idx_ref[...]
        acc = jnp.full(ix.shape, NEG, F32)
        for b in buckets:
            acc = jnp.where(ix == b, rb_ref[b, gi * NH + h], acc)
        rest[-1][...] = acc

    return pl.pallas_call(
        body, name=f"bias_table{gi}{'ab'[sub]}", grid=(NH, T // tr),
        in_specs=[pl.BlockSpec((tr, Tk), lambda h, i: (i, 0)), pl.BlockSpec(memory_space=pltpu.SMEM)] + dep_specs,
        out_specs=pl.BlockSpec((None, tr, Tk), lambda h, i: (h, i, 0)),
        out_shape=_sds((NH, T, Tk), F32), compiler_params=_cp(("parallel", "parallel")),
    )(idx, rel_bias, *dep_args)


def _bias_grad(dsum, idx, buckets, gi, sub, dep=None):
    _, T, Tk = dsum.shape
    dep_specs, dep_args = _after(dep)

    def body(ds_ref, idx_ref, *rest):
        o_ref, r_ref = rest[-2], rest[-1]
        r_ref[...] = jnp.zeros_like(r_ref)
        ix, ds = idx_ref[...], ds_ref[...]
        for b in buckets:
            r_ref[b:b + 1, :] = jnp.sum(jnp.where(ix == b, ds, 0.0), axis=0, keepdims=True)
        o_ref[...] = jnp.broadcast_to(jnp.sum(r_ref[...], axis=1, keepdims=True), (32, 128))

    out = pl.pallas_call(
        body, name=f"bias_grad{gi}{'ab'[sub]}", grid=(NH,),
        in_specs=[pl.BlockSpec((None, T, Tk), lambda h: (h, 0, 0)), pl.BlockSpec((T, Tk), lambda h: (0, 0))] + dep_specs,
        out_specs=pl.BlockSpec((None, 32, 128), lambda h: (h, 0, 0)),
        out_shape=_sds((NH, 32, 128), F32), scratch_shapes=[pltpu.VMEM((32, Tk), F32)],
        compiler_params=_cp(("parallel",)),
    )(dsum, idx, *dep_args)
    return out[:, :, 0].T


def _prev_mask(n, T, Tk):
    col = lax.broadcasted_iota(jnp.int32, (1, Tk), 1)
    return jnp.where((col < T) & (n == 0), NEG, 0.0).astype(F32)


def _split_rows(v, R):
    A, half = v.shape[0] // R, R // 2
    via_f32 = v.dtype != F32 and half % 16 != 0
    w = v.astype(F32) if via_f32 else v
    lo = jnp.concatenate([w[a * R:a * R + half] for a in range(A)], axis=0)
    hi = jnp.concatenate([w[a * R + half:(a + 1) * R] for a in range(A)], axis=0)
    return (lo.astype(v.dtype), hi.astype(v.dtype)) if via_f32 else (lo, hi)


def _merge_rows(lo, hi, R):
    half = R // 2
    A = lo.shape[0] // half
    return jnp.concatenate([x[a * half:(a + 1) * half] for a in range(A) for x in (lo, hi)], axis=0)


def _attn_fwd(qkv, biases, geom):
    S = qkv.shape[0]
    gi, T, R, has_prev = geom.gi, geom.T, geom.rows, geom.has_prev
    H2 = T // 2
    qv = geom.view(qkv)
    cq, ck, cv = 3 * gi, 3 * gi + 1, 3 * gi + 2

    def body(*refs):
        if has_prev:
            q_ref, kp_ref, kc_ref, vp_ref, vc_ref, ba_ref, bb_ref, o_ref, l_ref = refs
        else:
            q_ref, kc_ref, vc_ref, ba_ref, bb_ref, o_ref, l_ref = refs
        q_lo, q_hi = _split_rows(_ld(q_ref), R)
        kc_lo, kc_hi = _split_rows(_ld(kc_ref), R)
        vc_lo, vc_hi = _split_rows(_ld(vc_ref), R)
        k_b = jnp.concatenate([kc_lo, kc_hi], axis=0)
        v_b = jnp.concatenate([vc_lo, vc_hi], axis=0)
        if has_prev:
            n = pl.program_id(len(geom.outer))
            k_a = jnp.concatenate([_split_rows(_ld(kp_ref), R)[1], kc_lo], axis=0)
            v_a = jnp.concatenate([_split_rows(_ld(vp_ref), R)[1], vc_lo], axis=0)
            pm = _prev_mask(n, H2, T)
        else:
            k_a, v_a, pm = k_b, v_b, None
        low = lax.broadcasted_iota(jnp.int32, (1, 2 * HD), 1) < HD
        for j in range(NH // 2):
            cols = slice(2 * HD * j, 2 * HD * (j + 1))
            o_half, l_half = [], []
            for q, k, v, b_ref, mask in ((q_lo, k_a, v_a, ba_ref, pm), (q_hi, k_b, v_b, bb_ref, None)):
                q2 = q[:, cols].astype(F32) * 0.125
                k2, v2 = k[:, cols], v[:, cols]
                qs = jnp.concatenate([jnp.where(low, q2, 0.0), jnp.where(low, 0.0, q2)], axis=0).astype(q.dtype)
                s = _dot(qs, k2, NT) + b_ref[j]
                if mask is not None:
                    s = s + mask
                m = jnp.max(s, axis=1, keepdims=True)
                p = jnp.exp(s - m)
                l = jnp.sum(p, axis=1, keepdims=True)
                pv = _dot(p.astype(v.dtype), v2, NN) / l
                lse = m + jnp.log(l)
                o_half.append(jnp.where(low, pv[:H2], pv[H2:]))
                l_half.append(jnp.where(low, lse[:H2], lse[H2:]))
            _st(o_ref, _merge_rows(o_half[0], o_half[1], R), cols)
            _st(l_ref, _merge_rows(l_half[0], l_half[1], R), cols)

    cur = lambda n: n
    prev = lambda n: jnp.maximum(n - 1, 0)
    if has_prev:
        in_specs = [geom.spec(AW, cq, cur), geom.spec(AW, ck, prev), geom.spec(AW, ck, cur),
                    geom.spec(AW, cv, prev), geom.spec(AW, cv, cur)]
        args = [qv] * 5
    else:
        in_specs = [geom.spec(AW, cq, cur), geom.spec(AW, ck, cur), geom.spec(AW, cv, cur)]
        args = [qv] * 3
    nd = len(geom.outer) + (1 if has_prev else 0)
    biases = [b.reshape(NH // 2, 2 * H2, b.shape[2]) for b in biases]
    in_specs += [pl.BlockSpec(b.shape, lambda *ids: (0, 0, 0)) for b in biases]
    grid = geom.outer + ((geom.n_tiles,) if has_prev else ())
    oshape = geom.view_shape(AW)
    o, l = pl.pallas_call(
        body, name=f"attn_fwd{gi}", grid=grid, in_specs=in_specs,
        out_specs=[geom.spec(AW, 0, cur), geom.spec(AW, 0, cur)],
        out_shape=[_sds(oshape, F32), _sds(oshape, F32)],
        compiler_params=_cp(("arbitrary",) * nd),
    )(*args, *biases)
    return geom.unview(o), geom.unview(l)


def _attn_combine(os_, ls_, rest):
    S = rest.shape[0]
    tm = 512

    def body(o0, o1, o2, l0, l1, l2, z_ref, attn_ref, L_ref, ag_ref, agt_ref):
        la, lb, lc = l0[...], l1[...], l2[...]
        m = jnp.maximum(jnp.maximum(la, lb), lc)
        ea, eb, ec = jnp.exp(la - m), jnp.exp(lb - m), jnp.exp(lc - m)
        den = ea + eb + ec
        attn = (ea * o0[...] + eb * o1[...] + ec * o2[...]) / den
        attn_ref[...] = attn
        L_ref[...] = m + jnp.log(den)
        z = z_ref[...]
        ag = attn * (z * _sigmoid(z))
        ag_ref[...] = ag.astype(ag_ref.dtype)
        agt_ref[...] = ag.T.astype(agt_ref.dtype)

    t = pl.BlockSpec((tm, AW), lambda i: (i, 0))
    return pl.pallas_call(
        body, name="attn_combine", grid=(S // tm,),
        in_specs=[t] * 6 + [pl.BlockSpec((tm, AW), lambda i: (i, 0))],
        out_specs=[t, t, t, pl.BlockSpec((AW, tm), lambda i: (0, i))],
        out_shape=[_sds((S, AW), F32), _sds((S, AW), F32), _sds((S, AW), CDT), _sds((AW, S), CDT)],
        compiler_params=_cp(("parallel",)),
    )(*os_, *ls_, rest)


def _pool_counts(r, LS, win):
    l = lax.broadcasted_iota(jnp.int32, (LS, 1), 0)
    return jnp.minimum(16 * l + (r + 1), win).astype(F32)


def _pool_fwd(rest, pool_w, pool_scale):
    S = rest.shape[0]
    LS = S // R16
    r3 = rest.reshape(R16, LS, REST_W)
    PG = 128

    def body(u_ref, z_ref, pw_ref, ps_ref, pt_ref, mp_ref, pg_ref, pgt_ref, e_ref):
        g = pl.program_id(0)
        win = jnp.left_shift(2, g)
        row = lax.broadcasted_iota(jnp.int32, (LS, PG), 0)
        for r in range(R16):
            e_ref[R16 + r] = u_ref[r]
        for r in range(1, R16):
            e_ref[r] = jnp.where(row == 0, 0.0, pltpu.roll(u_ref[r], 1, 0))

        def double(s):
            for i in range(2 * R16 - 1, 2 * s - 1, -1):
                e_ref[i] = e_ref[i] + e_ref[i - s]

        double(1)
        for k in (1, 2, 3):
            pl.when(g >= k)(functools.partial(double, 2 ** k))
        pw = pw_ref[...].astype(CDT)
        for r in range(R16):
            u = u_ref[r]
            pooled = e_ref[R16 + r] / _pool_counts(r, LS, win) - u
            pc = pooled.astype(CDT)
            mp = _dot(pc, pw, NN)
            z = z_ref[r]
            pg = (mp * ps_ref[...]) * (z * _sigmoid(z))
            mp_ref[r] = mp
            pg_ref[r] = pg.astype(pg_ref.dtype)
            pt_ref[:, r * LS:(r + 1) * LS] = pooled.T.astype(pt_ref.dtype)
            pgt_ref[:, r * LS:(r + 1) * LS] = pg.T.astype(pgt_ref.dtype)

    col = lambda off: pl.BlockSpec((R16, LS, PG), lambda g: (0, 0, off + g))
    tsp = pl.BlockSpec((PG, S), lambda g: (g, 0))
    pt, mp, pg, pgt = pl.pallas_call(
        body, name="pool_fwd", grid=(4,),
        in_specs=[col(24), col(4), pl.BlockSpec((None, PG, PG), lambda g: (g, 0, 0)),
                  pl.BlockSpec((1, PG), lambda g: (0, g))],
        out_specs=[tsp, col(0), col(0), tsp],
        out_shape=[_sds((AW, S), CDT), _sds((R16, LS, AW), F32), _sds((R16, LS, AW), CDT), _sds((AW, S), CDT)],
        scratch_shapes=[pltpu.VMEM((2 * R16, LS, PG), F32)],
        compiler_params=_cp(("parallel",)),
    )(r3, r3, pool_w, pool_scale)
    return pt, mp.reshape(S, AW), pg.reshape(S, AW), pgt


def _branch_merge(ag, pg, wab, wpb, rest):
    S = ag.shape[0]
    tm, tn = 512, 512

    def body(ag_ref, pg_ref, wa_ref, wp_ref, ga_ref, gp_ref, m_ref, mt_ref):
        ya = _dot(ag_ref[...], wa_ref[...], NN)
        yp = _dot(pg_ref[...], wp_ref[...], NN)
        mg = _sigmoid(ga_ref[...]) * ya + _sigmoid(gp_ref[...]) * yp
        m_ref[...] = mg.astype(m_ref.dtype)
        mt_ref[...] = mg.T.astype(mt_ref.dtype)

    a = pl.BlockSpec((tm, AW), lambda i, j: (i, 0))
    w = pl.BlockSpec((AW, tn), lambda i, j: (0, j))
    o = pl.BlockSpec((tm, tn), lambda i, j: (i, j))
    return pl.pallas_call(
        body, name="branch_merge", grid=(S // tm, D // tn),
        in_specs=[a, a, w, w, pl.BlockSpec((tm, tn), lambda i, j: (i, 2 + j)),
                  pl.BlockSpec((tm, tn), lambda i, j: (i, 4 + j))],
        out_specs=[o, pl.BlockSpec((tn, tm), lambda i, j: (j, i))],
        out_shape=[_sds((S, D), CDT), _sds((D, S), CDT)],
        compiler_params=_cp(("parallel", "parallel")),
    )(ag, pg, wab, wpb, rest, rest)


def _out_loss(merged, wout, x, target, gate, final_g):
    S = x.shape[0]

    def body(m_ref, w_ref, x_ref, t_ref, gate_ref, fg_ref, dx2_ref, dmo_ref, loss_ref, gfg_ref, dgate_ref):
        i = pl.program_id(0)

        @pl.when(i == 0)
        def _():
            loss_ref[...] = jnp.zeros_like(loss_ref)
            gfg_ref[...] = jnp.zeros_like(gfg_ref)
            dgate_ref[...] = jnp.zeros_like(dgate_ref)

        swap = _swap16()
        mo = _dot(_dot(swap, _ld(m_ref), NN).astype(CDT), w_ref[...], NN)
        gate, fg = gate_ref[...], fg_ref[...]
        x2 = x_ref[...] + gate * mo
        r2 = lax.rsqrt(jnp.mean(x2 * x2, axis=-1, keepdims=True) + EPS)
        xn2 = x2 * r2
        diff = xn2 * fg - t_ref[...]
        loss = jnp.sum(jnp.sum(diff * diff, axis=-1, keepdims=True), axis=0, keepdims=True)
        dy = diff * (1.0 / D)
        dxn2 = dy * fg
        dx2 = r2 * (dxn2 - xn2 * jnp.mean(dxn2 * xn2, axis=-1, keepdims=True))
        dx2_ref[...] = dx2
        _st(dmo_ref, _dot(swap, (dx2 * gate).astype(CDT), NN))
        loss_ref[...] += jnp.broadcast_to(loss * (0.5 / D), loss_ref.shape)
        gfg_ref[...] += jnp.sum(dy * xn2, axis=0, keepdims=True)
        dgate_ref[...] += jnp.sum(dx2 * mo, axis=0, keepdims=True)

    slab = pl.BlockSpec((R16, R16, D), lambda i: (0, i, 0))
    nat = pl.BlockSpec((TOK, D), lambda i: (i, 0))
    vec = pl.BlockSpec((1, D), lambda i: (0, 0))
    return pl.pallas_call(
        body, name="out_loss", grid=(S // TOK,),
        in_specs=[slab, _resident((D, D)), nat, nat, vec, vec],
        out_specs=[nat, slab, pl.BlockSpec((1, 128), lambda i: (0, 0)), vec, vec],
        out_shape=[_sds((S, D), F32), _sds((R16, S // R16, D), CDT), _sds((1, 128), F32), _sds((1, D), F32),
                   _sds((1, D), F32)],
        compiler_params=_cp(("arbitrary",)),
    )(merged, wout, x, target, gate, final_g)


def _bwd_gates(dmo, wout, wab, wpb, rest, ag, pg, attn, mp, pool_scale):
    S = dmo.shape[0]
    tm = 256

    def body(dmo_ref, wo_ref, wa_ref, wp_ref, za_ref, zp_ref, ga0, ga1, gp0, gp1, ag_ref, pg_ref, at_ref, mp_ref,
             ps_ref, dya_ref, dyp_ref, dr_ref, dat_ref, dl_ref, dmp_ref, dps_ref):
        i = pl.program_id(0)

        @pl.when(i == 0)
        def _():
            dps_ref[...] = jnp.zeros_like(dps_ref)

        dm = _dot(dmo_ref[...], wo_ref[...], NT)
        sa = _sigmoid(jnp.concatenate([ga0[...], ga1[...]], axis=1))
        sp = _sigmoid(jnp.concatenate([gp0[...], gp1[...]], axis=1))
        dya = (dm * sa).astype(CDT)
        dyp = (dm * sp).astype(CDT)
        dya_ref[...] = dya
        dyp_ref[...] = dyp
        ya = _dot(ag_ref[...], wa_ref[...], NN)
        yp = _dot(pg_ref[...], wp_ref[...], NN)
        dr_ref[:, 1024:2048] = (dm * ya * sa * (1.0 - sa)).astype(dr_ref.dtype)
        dr_ref[:, 2048:3072] = (dm * yp * sp * (1.0 - sp)).astype(dr_ref.dtype)
        da = _dot(dya, wa_ref[...], NT)
        dp = _dot(dyp, wp_ref[...], NT)
        za = za_ref[...]
        sga = _sigmoid(za)
        attn = at_ref[...]
        dattn = da * (za * sga)
        dr_ref[:, 0:512] = (da * attn * (sga * (1.0 + za * (1.0 - sga)))).astype(dr_ref.dtype)
        dat_ref[...] = dattn.astype(dat_ref.dtype)
        prod = dattn * attn
        for h in range(NH):
            sl = slice(HD * h, HD * h + HD)
            dl_ref[:, sl] = jnp.broadcast_to(jnp.sum(prod[:, sl], axis=1, keepdims=True), (tm, HD))
        zp = zp_ref[...]
        sgp = _sigmoid(zp)
        mpre = mp_ref[...]
        ps = ps_ref[...]
        dmixed = dp * (zp * sgp)
        dr_ref[:, 512:1024] = (dp * (mpre * ps) * (sgp * (1.0 + zp * (1.0 - sgp)))).astype(dr_ref.dtype)
        dps_ref[...] += jnp.sum(dmixed * mpre, axis=0, keepdims=True)
        dmp_ref[...] = (dmixed * ps).astype(dmp_ref.dtype)

    full = lambda shape: pl.BlockSpec(shape, lambda i: (0, 0))
    tD = pl.BlockSpec((tm, D), lambda i: (i, 0))
    tA = pl.BlockSpec((tm, AW), lambda i: (i, 0))
    rb = lambda c: pl.BlockSpec((tm, AW), lambda i: (i, c))
    return pl.pallas_call(
        body, name="bwd_gates", grid=(S // tm,),
        in_specs=[tD, full((D, D)), full((AW, D)), full((AW, D)), rb(0), rb(1), rb(2), rb(3), rb(4), rb(5),
                  tA, tA, tA, tA, full((1, AW))],
        out_specs=[tD, tD, pl.BlockSpec((tm, 3072), lambda i: (i, 0)), tA, tA, tA, full((1, AW))],
        out_shape=[_sds((S, D), CDT), _sds((S, D), CDT), _sds((S, REST_W), CDT), _sds((S, AW), CDT),
                   _sds((S, AW), F32), _sds((S, AW), CDT), _sds((1, AW), F32)],
        compiler_params=_cp(("arbitrary",), vmem=56),
    )(dmo, wout, wab, wpb, rest, rest, rest, rest, rest, rest, ag, pg, attn, mp, pool_scale)


def _pool_bwd(dmp, pt, pool_w, drest):
    S = dmp.shape[0]
    LS = S // R16
    PG = 128
    d3 = dmp.reshape(R16, LS, AW)
    dr3 = drest.reshape(R16, LS, REST_W)

    def body(d_ref, pt_ref, pw_ref, dr_in, du_ref, dpw_ref, g_ref, s_ref):
        del dr_in
        g = pl.program_id(0)
        win = jnp.left_shift(2, g)
        row = lax.broadcasted_iota(jnp.int32, (LS, PG), 0)
        pw = pw_ref[...].astype(CDT)
        dpw = jnp.zeros((PG, PG), F32)
        for r in range(R16):
            dm = d_ref[r]
            dpw = dpw + _dot(pt_ref[:, r * LS:(r + 1) * LS], dm, NN)
            dpooled = _dot(dm, pw, NT)
            g_ref[r] = dpooled
            sc = dpooled / _pool_counts(r, LS, win)
            s_ref[r] = sc
            if r < R16 - 1:
                s_ref[R16 + r] = jnp.where(row == LS - 1, 0.0, pltpu.roll(sc, LS - 1, 0))
        dpw_ref[...] = dpw

        def double(s):
            for i in range(0, 2 * R16 - 2 * s):
                s_ref[i] = s_ref[i] + s_ref[i + s]

        double(1)
        for k in (1, 2, 3):
            pl.when(g >= k)(functools.partial(double, 2 ** k))
        for r in range(R16):
            du_ref[r] = (s_ref[r] - g_ref[r]).astype(du_ref.dtype)

    du, dpw = pl.pallas_call(
        body, name="pool_bwd", grid=(4,),
        in_specs=[pl.BlockSpec((R16, LS, PG), lambda g: (0, 0, g)), pl.BlockSpec((PG, S), lambda g: (g, 0)),
                  pl.BlockSpec((None, PG, PG), lambda g: (g, 0, 0)), pl.BlockSpec(memory_space=pl.ANY)],
        out_specs=[pl.BlockSpec((R16, LS, PG), lambda g: (0, 0, 24 + g)),
                   pl.BlockSpec((None, PG, PG), lambda g: (g, 0, 0))],
        out_shape=[_sds((R16, LS, REST_W), CDT), _sds((4, PG, PG), F32)],
        scratch_shapes=[pltpu.VMEM((R16, LS, PG), F32), pltpu.VMEM((2 * R16, LS, PG), F32)],
        input_output_aliases={3: 0},
        compiler_params=_cp(("parallel",)),
    )(d3, pt, pool_w, dr3)
    return du.reshape(S, REST_W), dpw


def _attn_bwd(qkv, dattn, Lb, dlb, biases, geom, dqkv_prev):
    S = qkv.shape[0]
    gi, T, R, has_prev, N = geom.gi, geom.T, geom.rows, geom.has_prev, geom.n_tiles
    H2 = T // 2
    nd_outer = len(geom.outer)
    qv = geom.view(qkv)
    cq, ck, cv = 3 * gi, 3 * gi + 1, 3 * gi + 2
    W3 = 3 * AW

    def heads(q_ref, kp_ref, kc_ref, vp_ref, vc_ref, do_ref, L_ref, dl_ref, ba_ref, bb_ref, pm, dsa_ref, dsb_ref, emit):
        q_lo, q_hi = _split_rows(_ld(q_ref), R)
        do_lo, do_hi = _split_rows(_ld(do_ref), R)
        L_lo, L_hi = _split_rows(_ld(L_ref), R)
        dl_lo, dl_hi = _split_rows(_ld(dl_ref), R)
        kc_lo, kc_hi = _split_rows(_ld(kc_ref), R)
        vc_lo, vc_hi = _split_rows(_ld(vc_ref), R)
        k_b = jnp.concatenate([kc_lo, kc_hi], axis=0)
        v_b = jnp.concatenate([vc_lo, vc_hi], axis=0)
        if has_prev:
            k_a = jnp.concatenate([_split_rows(_ld(kp_ref), R)[1], kc_lo], axis=0)
            v_a = jnp.concatenate([_split_rows(_ld(vp_ref), R)[1], vc_lo], axis=0)
        else:
            k_a, v_a = k_b, v_b
        low = lax.broadcasted_iota(jnp.int32, (1, 2 * HD), 1) < HD
        for j in range(NH // 2):
            sl = slice(2 * HD * j, 2 * HD * (j + 1))
            res = []
            for q, k, v, do, L, dl, b_ref, mask, ds_ref in (
                    (q_lo, k_a, v_a, do_lo, L_lo, dl_lo, ba_ref, pm, dsa_ref),
                    (q_hi, k_b, v_b, do_hi, L_hi, dl_hi, bb_ref, None, dsb_ref)):
                q2 = q[:, sl].astype(F32) * 0.125
                do2 = do[:, sl].astype(F32)
                k2, v2 = k[:, sl], v[:, sl]
                L2, dl2 = L[:, sl], dl[:, sl]
                qs = jnp.concatenate([jnp.where(low, q2, 0.0), jnp.where(low, 0.0, q2)], axis=0).astype(q.dtype)
                dom = jnp.concatenate([jnp.where(low, do2, 0.0), jnp.where(low, 0.0, do2)], axis=0).astype(q.dtype)
                Lc = jnp.concatenate([L2[:, 0:1], L2[:, HD:HD + 1]], axis=0)
                dlc = jnp.concatenate([dl2[:, 0:1], dl2[:, HD:HD + 1]], axis=0)
                s = _dot(qs, k2, NT) + b_ref[j]
                if mask is not None:
                    s = s + mask
                p = jnp.exp(s - Lc)
                ds = p * (_dot(dom, v2, NT) - dlc)
                ds_ref[j] += ds
                dsc = ds.astype(q.dtype)
                dq_st = _dot(dsc, k2, NN)
                res.append((jnp.where(low, dq_st[:H2], dq_st[H2:]) * 0.125, _dot(dsc, qs, TN),
                            _dot(p.astype(q.dtype), dom, TN)))
            (dq_lo, dk_a, dv_a), (dq_hi, dk_b, dv_b) = res
            dq = _merge_rows(dq_lo, dq_hi, R)
            if has_prev:
                emit(sl, dq, _merge_rows(dk_a[H2:] + dk_b[:H2], dk_b[H2:], R),
                     _merge_rows(dv_a[H2:] + dv_b[:H2], dv_b[H2:], R), dk_a[:H2], dv_a[:H2])
            else:
                emit(sl, dq, _merge_rows(dk_a[:H2] + dk_b[:H2], dk_a[H2:] + dk_b[H2:], R),
                     _merge_rows(dv_a[:H2] + dv_b[:H2], dv_a[H2:] + dv_b[H2:], R), None, None)

    def first_step():
        ok = pl.program_id(nd_outer) == 0 if has_prev else pl.program_id(0) == 0
        for a in range(nd_outer if has_prev else 0):
            ok = ok & (pl.program_id(a) == 0)
        return ok

    if has_prev:
        def body(q_ref, kp_ref, kc_ref, vp_ref, vc_ref, do_ref, L_ref, dl_ref, ba_ref, bb_ref, prev_in, out_ref,
                 dsa_ref, dsb_ref, hold, new, pp):
            del prev_in
            s_id = pl.program_id(nd_outer)

            @pl.when(first_step())
            def _():
                dsa_ref[...] = jnp.zeros_like(dsa_ref)
                dsb_ref[...] = jnp.zeros_like(dsb_ref)

            @pl.when(s_id < N)
            def _():
                def emit(sl, dq, dk, dv, dk_prev, dv_prev):
                    new[:, sl] = dq
                    new[:, slice(AW + sl.start, AW + sl.stop)] = dk
                    new[:, slice(2 * AW + sl.start, 2 * AW + sl.stop)] = dv
                    pp[:, sl] = dk_prev
                    pp[:, slice(AW + sl.start, AW + sl.stop)] = dv_prev

                heads(q_ref, kp_ref, kc_ref, vp_ref, vc_ref, do_ref, L_ref, dl_ref, ba_ref, bb_ref,
                      _prev_mask(s_id, H2, T), dsa_ref, dsb_ref, emit)

            @pl.when((s_id >= 1) & (s_id < N))
            def _():
                _st(out_ref, hold[:, 0:AW], slice(0, AW))
                late = _merge_rows(jnp.zeros((H2, 2 * AW), F32), pp[...], R)
                _st(out_ref, hold[:, AW:W3] + late, slice(AW, W3))

            @pl.when(s_id == N)
            def _():
                _st(out_ref, hold[...])

            @pl.when(s_id < N)
            def _():
                hold[...] = new[...]

        cur = lambda n: jnp.minimum(n, N - 1)
        prev = lambda n: jnp.clip(n - 1, 0, N - 1)
        delayed = lambda n: jnp.maximum(n - 1, 0)
        in_specs = [geom.spec(AW, cq, cur), geom.spec(AW, ck, prev), geom.spec(AW, ck, cur),
                    geom.spec(AW, cv, prev), geom.spec(AW, cv, cur),
                    geom.spec(AW, 0, cur), geom.spec(AW, 0, cur), geom.spec(AW, 0, cur)]
        args = [qv] * 5 + [geom.view(dattn), geom.view(Lb), geom.view(dlb)]
        out_spec = geom.spec(W3, gi, delayed)
        grid = geom.outer + (N + 1,)
        scratch = [pltpu.VMEM((T, W3), F32), pltpu.VMEM((T, W3), F32), pltpu.VMEM((H2, 2 * AW), F32)]
    else:
        def body(q_ref, kc_ref, vc_ref, do_ref, L_ref, dl_ref, ba_ref, bb_ref, prev_in, out_ref, dsa_ref, dsb_ref):
            del prev_in

            @pl.when(first_step())
            def _():
                dsa_ref[...] = jnp.zeros_like(dsa_ref)
                dsb_ref[...] = jnp.zeros_like(dsb_ref)

            def emit(sl, dq, dk, dv, dk_prev, dv_prev):
                _st(out_ref, dq, sl)
                _st(out_ref, dk, slice(AW + sl.start, AW + sl.stop))
                _st(out_ref, dv, slice(2 * AW + sl.start, 2 * AW + sl.stop))

            heads(q_ref, None, kc_ref, None, vc_ref, do_ref, L_ref, dl_ref, ba_ref, bb_ref, None, dsa_ref, dsb_ref,
                  emit)

        cur = lambda n: n
        in_specs = [geom.spec(AW, cq, cur), geom.spec(AW, ck, cur), geom.spec(AW, cv, cur),
                    geom.spec(AW, 0, cur), geom.spec(AW, 0, cur), geom.spec(AW, 0, cur)]
        args = [qv] * 3 + [geom.view(dattn), geom.view(Lb), geom.view(dlb)]
        out_spec = geom.spec(W3, gi, cur)
        grid = geom.outer
        scratch = []

    whole = lambda b: pl.BlockSpec(b.shape, lambda *ids: (0, 0, 0))
    biases = [b.reshape(NH // 2, 2 * H2, b.shape[2]) for b in biases]
    in_specs += [whole(b) for b in biases]
    in_specs.append(pl.BlockSpec(memory_space=pl.ANY))
    if dqkv_prev is None:
        last, aliases = biases[0], {}
    else:
        last, aliases = geom.view(dqkv_prev), {len(in_specs) - 1: 0}
    dq, dsa, dsb = pl.pallas_call(
        body, name=f"attn_bwd{gi}", grid=grid, in_specs=in_specs,
        out_specs=[out_spec] + [whole(b) for b in biases],
        out_shape=[_sds(geom.view_shape(QKV_W), CDT)] + [_sds(b.shape, F32) for b in biases],
        scratch_shapes=scratch,
        input_output_aliases=aliases,
        compiler_params=_cp(("arbitrary",) * len(grid), vmem=56),
    )(*args, *biases, last)
    return geom.unview(dq), [d.reshape(NH, H2, d.shape[2]) for d in (dsa, dsb)]


def _h_bwd(dh, dx2, x, norm_g, scale):
    S = x.shape[0]

    def body(dh_ref, dx2_ref, x_ref, g_ref, sc_ref, gx_ref, dsh_ref, dsc_ref, dng_ref, nat_ref):
        i = pl.program_id(0)

        @pl.when(i == 0)
        def _():
            dsh_ref[...] = jnp.zeros_like(dsh_ref)
            dsc_ref[...] = jnp.zeros_like(dsc_ref)
            dng_ref[...] = jnp.zeros_like(dng_ref)

        for r in range(R16):
            nat_ref[:, r, :] = dh_ref[r]
        dh = jnp.concatenate([nat_ref[l] for l in range(R16)], axis=0)
        g = g_ref[...]
        one_sc = 1.0 + sc_ref[...]
        xv = x_ref[...]
        r1 = lax.rsqrt(jnp.mean(xv * xv, axis=-1, keepdims=True) + EPS)
        xn = xv * r1
        dhx = dh * xn
        dxn = dh * (g * one_sc)
        dx1 = r1 * (dxn - xn * jnp.mean(dxn * xn, axis=-1, keepdims=True))
        gx_ref[...] = dx2_ref[...] + dx1
        dsh_ref[...] += jnp.sum(dh, axis=0, keepdims=True)
        dsc_ref[...] += jnp.sum(dhx * g, axis=0, keepdims=True)
        dng_ref[...] += jnp.sum(dhx * one_sc, axis=0, keepdims=True)

    nat = pl.BlockSpec((TOK, D), lambda i: (i, 0))
    vec = pl.BlockSpec((1, D), lambda i: (0, 0))
    return pl.pallas_call(
        body, name="h_bwd", grid=(S // TOK,),
        in_specs=[pl.BlockSpec((R16, R16, D), lambda i: (0, i, 0)), nat, nat, vec, vec],
        out_specs=[nat, vec, vec, vec],
        out_shape=[_sds((S, D), F32), _sds((1, D), F32), _sds((1, D), F32), _sds((1, D), F32)],
        scratch_shapes=[pltpu.VMEM((R16, R16, D), F32)],
        compiler_params=_cp(("arbitrary",)),
    )(dh, dx2, x, norm_g, scale)


def _local_step(x, target, shift, scale, gate, norm_g, w_in_full, pool_w, pool_scale, wab, wpb, wout, rel_bias,
                final_g):
    st = _local_pre(x, shift, scale, norm_g, rel_bias)
    st = _local_grads(st, target, gate, w_in_full, pool_w, pool_scale, wab, wpb, wout, final_g)
    d_rel_bias = _local_bias_grads(st)
    gx, dmod, dng = _local_input_grads(st, w_in_full, norm_g, scale)
    return dict(loss=st["loss"], grad_x=gx, dw_in=st["dw_in"], dw_ab=st["dw_ab"], dw_pb=st["dw_pb"],
                dw_out=st["dw_out"], d_norm_g=dng, d_pool_w=st["dpw"], d_pool_scale=st["dps"],
                d_rel_bias=d_rel_bias, d_final_g=st["gfg"], dmod=dmod)


def _local_pre(x, shift, scale, norm_g, rel_bias, dep=None):
    S = x.shape[0]
    LS = S // R16
    geoms = [_Geom(gi, LS) for gi in range(3)]
    idx_np = [[g.bucket_index(sub) for sub in range(2)] for g in geoms]
    buckets = [[tuple(int(b) for b in np.unique(ix) if b >= 0) for ix in pair] for pair in idx_np]
    idxs = [[jnp.asarray(ix) for ix in pair] for pair in idx_np]
    biases = [[_bias_table(idxs[gi][sub], buckets[gi][sub], rel_bias, gi, sub, dep) for sub in range(2)]
              for gi in range(3)]
    h = _norm_mod(x, norm_g, shift, scale, dep).reshape(S, D)
    return dict(geoms=geoms, idxs=idxs, buckets=buckets, biases=biases, x=x, h=h, ht=_tcast(h, "h_transpose"))


def _local_grads(st, target, gate, w_in_full, pool_w, pool_scale, wab, wpb, wout, final_g):
    geoms, idxs, biases, x, h, ht = (st[k] for k in ("geoms", "idxs", "biases", "x", "h", "ht"))
    S = h.shape[0]
    LS = S // R16
    qkv, rest = _proj(h, w_in_full)

    os_, ls_ = [], []
    for gi in range(3):
        o, l = _attn_fwd(qkv, biases[gi], geoms[gi])
        os_.append(o)
        ls_.append(l)
    attn, Lb, ag, agt = _attn_combine(os_, ls_, rest)
    pt, mp, pg, pgt = _pool_fwd(rest, pool_w, pool_scale)
    merged, mt = _branch_merge(ag, pg, wab, wpb, rest)
    dx2, dmo3, loss, gfg, dgate = _out_loss(merged.reshape(R16, LS, D), wout, x, target, gate, final_g)
    dmo = dmo3.reshape(S, D)

    dya, dyp, drest, dattn, dlb, dmp, dps = _bwd_gates(dmo, wout, wab, wpb, rest, ag, pg, attn, mp, pool_scale)
    dw_out = _mm_tn(mt, dmo, n_total=D, out_blk=lambda j: j, name="dw_out")
    dw_ab = _mm_tn(agt, dya, n_total=D, out_blk=lambda j: j, name="dw_attn_br")
    dw_pb = _mm_tn(pgt, dyp, n_total=D, out_blk=lambda j: j, name="dw_pool_br")
    drest, dpw = _pool_bwd(dmp, pt, pool_w, drest)

    dqkv = None
    dsums = []
    for gi in range(3):
        dqkv, dsum = _attn_bwd(qkv, dattn, Lb, dlb, biases[gi], geoms[gi], dqkv)
        dsums.append(dsum)

    dw_in = _mm_tn(ht, dqkv, n_total=8192, out_blk=lambda j: j, name="dw_in_qkv")
    dw_in = _mm_tn(ht, drest, n_total=8192, out_blk=_rest_blk, name="dw_in_rest", prev=dw_in)
    return dict(st, loss=loss[0, 0], dw_in=dw_in, dw_ab=dw_ab, dw_pb=dw_pb, dw_out=dw_out, dpw=dpw, dps=dps, gfg=gfg,
                dgate=dgate, dsums=dsums, dqkv=dqkv, drest=drest, dx2=dx2)


def _local_bias_grads(st, dep=None):
    drb = [_bias_grad(st["dsums"][gi][0], st["idxs"][gi][0], st["buckets"][gi][0], gi, 0, dep if gi == 0 else None) +
           _bias_grad(st["dsums"][gi][1], st["idxs"][gi][1], st["buckets"][gi][1], gi, 1) for gi in range(3)]
    return jnp.concatenate(drb, axis=1)


def _local_input_grads(st, w_in_full, norm_g, scale, dep=None):
    S = st["h"].shape[0]
    LS = S // R16
    dh = _dh(st["dqkv"], st["drest"], w_in_full, dep)
    gx, dsh, dsc, dng = _h_bwd(dh.reshape(R16, LS, D), st["dx2"], st["x"], norm_g, scale)
    return gx, jnp.concatenate([dsh, dsc, st["dgate"]], axis=1), dng


def _my_place():
    return lax.axis_index("x"), lax.axis_index("y"), lax.axis_index("c")


def _block_of(ref, axis, idx, width):
    if axis is None:
        return ref.at[idx]
    start = pl.multiple_of(idx * width, width)
    if axis == 0:
        return ref.at[pl.ds(start, width)]
    return ref.at[:, pl.ds(start, width)]


_SEM = pl.BlockSpec(memory_space=pltpu.SEMAPHORE)
_EFFECT = pltpu.SideEffectType.DATAFLOW_SIDE_EFFECTING


def _split_start(name, srcs, lands, ncopy, plan, dep=None):
    arrs = list(srcs) + list(lands)
    na, ns = len(arrs), len(srcs)
    dep_specs, dep_args = _after(dep)

    def body(*refs):
        send_sems, recv_sems, token = refs[na + len(dep_args)], refs[na + len(dep_args) + 1], refs[-1]
        for k, (s, d, dev) in enumerate(plan(refs[:ns], refs[ns:na])):
            pltpu.make_async_remote_copy(src_ref=s, dst_ref=d, send_sem=send_sems.at[k], recv_sem=recv_sems.at[k],
                                         device_id=dev, device_id_type=MESH).start()
        token[...] = jnp.zeros_like(token)

    outs = pl.pallas_call(
        body, name=name,
        out_shape=(pltpu.SemaphoreType.DMA((ncopy,)), pltpu.SemaphoreType.DMA((ncopy,)),
                   *[pltpu.HBM(a.shape, a.dtype) for a in arrs], _sds((8, 128), F32)),
        in_specs=[_HBM] * na + dep_specs,
        out_specs=(_SEM, _SEM, *[_HBM] * na, pl.BlockSpec(memory_space=pltpu.VMEM)),
        input_output_aliases={i: 2 + i for i in range(na)},
        compiler_params=pltpu.CompilerParams(has_side_effects=_EFFECT),
    )(*[pltpu.with_memory_space_constraint(a, pltpu.HBM) for a in arrs], *dep_args)
    return (outs[0], outs[1]), list(outs[2:2 + ns]), list(outs[2 + ns:2 + na]), outs[-1]


def _split_wait(name, sems, srcs, lands, plan, after):
    arrs = list(srcs) + list(lands)
    na, ns = len(arrs), len(srcs)
    after = list(after) if isinstance(after, (list, tuple)) else [after]

    def body(*refs):
        send_sems, recv_sems = refs[na], refs[na + 1]
        for k, (s, d, dev) in enumerate(plan(refs[:ns], refs[ns:na])):
            cp = pltpu.make_async_remote_copy(src_ref=s, dst_ref=d, send_sem=send_sems.at[k], recv_sem=recv_sems.at[k],
                                              device_id=dev, device_id_type=MESH)
            cp.wait_send()
            cp.wait_recv()

    outs = pl.pallas_call(
        body, name=name, out_shape=[pltpu.HBM(a.shape, a.dtype) for a in arrs],
        in_specs=[_HBM] * na + [_SEM, _SEM] + [pl.BlockSpec(memory_space=pl.ANY)] * len(after),
        out_specs=[_HBM] * na,
        input_output_aliases={i: i for i in range(na)},
        compiler_params=pltpu.CompilerParams(has_side_effects=_EFFECT),
    )(*arrs, sems[0], sems[1], *after)
    return list(outs[:ns]), list(outs[ns:])


def _gather_chips_plan(layout):
    def plan(src, land):
        x, y, c = _my_place()
        me = 4 * x + 2 * y + c
        mine = [_block_of(land[p], layout[p][0], me, layout[p][1]) for p in range(len(land))]
        return [(mine[p], mine[p], (cx, cy, c)) for p in range(len(land)) for cx, cy in [(1 - x, y), (x, 1 - y)]]
    return plan


def _pair_plan(layout):
    def plan(src, land):
        x, y, c = _my_place()
        n = len(layout)
        return [(_block_of(src[p], layout[p][0], 2 * q + (1 - c), layout[p][1]), land[p].at[q], (x, y, 1 - c))
                for p in range(n) for q in range(4)] + [(src[p], land[p], (x, y, 1 - c)) for p in range(n, len(src))]
    return plan


def _chips_plan(n):
    def plan(src, land):
        x, y, c = _my_place()
        chips = [(1 - x, y), (x, 1 - y), (1 - x, 1 - y)]
        return [(src[p].at[2 * cx + cy], land[p].at[2 * x + y], (cx, cy, c)) for p in range(n) for cx, cy in chips] + \
               [(src[p], land[p].at[2 * x + y], (cx, cy, c)) for p in range(n, len(src)) for cx, cy in chips]
    return plan


def _pair_share(gathered):
    n = len(gathered)

    def body(*refs):
        land = refs[n:2 * n]
        send_sems, recv_sems, fwd_send, fwd_recv = refs[2 * n:]
        x, y, c = _my_place()
        k_dg = 3 - (2 * x + y)
        south = c == 0
        pass_on = 4 * jnp.where(south, 1 - x, x) + 2 * jnp.where(south, y, 1 - y) + c
        to = (jnp.where(south, x, 1 - x), jnp.where(south, 1 - y, y), c)

        def forward(p):
            return pltpu.make_async_remote_copy(
                src_ref=land[p].at[pass_on], dst_ref=land[p].at[pass_on], send_sem=fwd_send.at[p],
                recv_sem=fwd_recv.at[p], device_id=to, device_id_type=MESH)

        def to_sibling(p, k):
            blk = land[p].at[2 * k + c]
            return pltpu.make_async_remote_copy(
                src_ref=blk, dst_ref=blk, send_sem=send_sems.at[4 * p + k], recv_sem=recv_sems.at[4 * p + k],
                device_id=(x, y, 1 - c), device_id_type=MESH)

        for p in range(n):
            forward(p).start()
        for p in range(n):
            for k in range(4):
                @pl.when(k != k_dg)
                def _():
                    to_sibling(p, k).start()
        for p in range(n):
            dg = land[p].at[2 * k_dg + c]
            pltpu.make_async_remote_copy(src_ref=dg, dst_ref=dg, send_sem=fwd_send.at[p], recv_sem=fwd_recv.at[p],
                                         device_id=to, device_id_type=MESH).wait_recv()
            for k in range(4):
                @pl.when(k == k_dg)
                def _():
                    to_sibling(p, k).start()
        for p in range(n):
            forward(p).wait_send()
            for k in range(4):
                to_sibling(p, k).wait_send()
                blk = land[p].at[2 * k + 1 - c]
                pltpu.make_async_remote_copy(
                    src_ref=blk, dst_ref=blk, send_sem=send_sems.at[4 * p + k], recv_sem=recv_sems.at[4 * p + k],
                    device_id=(x, y, 1 - c), device_id_type=MESH).wait_recv()

    return pl.pallas_call(
        body, name="gather_pair_share", in_specs=[_HBM] * n, out_specs=[_HBM] * n,
        out_shape=[_sds(g.shape, g.dtype) for g in gathered],
        input_output_aliases={p: p for p in range(n)},
        scratch_shapes=[pltpu.SemaphoreType.DMA((4 * n,)), pltpu.SemaphoreType.DMA((4 * n,)),
                        pltpu.SemaphoreType.DMA((n,)), pltpu.SemaphoreType.DMA((n,))],
    )(*gathered)


def _place_slot(arrs, slot, n_slots, name, out_dtype=None):
    n = len(arrs)

    def body(s_ref, *refs):
        for p in range(n):
            refs[n + p][...] = refs[p][...].astype(refs[n + p].dtype)

    def spec(a):
        tail = a.shape[-2:]
        return pl.BlockSpec((None,) + tail, lambda i, s_ref: (s_ref[0], 0, 0))

    in_specs = [spec(a) if a.ndim == 3 else pl.BlockSpec(a.shape, lambda i, s_ref: (0, 0)) for a in arrs]
    return pl.pallas_call(
        body, name=name,
        grid_spec=pltpu.PrefetchScalarGridSpec(num_scalar_prefetch=1, grid=(1,), in_specs=in_specs,
                                               out_specs=[spec(a) for a in arrs]),
        out_shape=[_sds((n_slots,) + a.shape[-2:], out_dtype or a.dtype) for a in arrs],
        compiler_params=_cp(("arbitrary",)),
    )(slot, *arrs)


def _cols_from_blocks(w8, name):
    _, R, C = w8.shape

    def body(w_ref, o_ref):
        for e in range(8):
            o_ref[:, C * e:C * (e + 1)] = w_ref[e]

    return pl.pallas_call(body, name=name, out_shape=_sds((R, 8 * C), w8.dtype), compiler_params=_cp())(w8)


def _all_gather_small(srcs, name):
    n = len(srcs)
    vmem = pl.BlockSpec(memory_space=pltpu.VMEM)

    def body(*refs):
        src, out = refs[:n], refs[n:2 * n]
        send_sems, recv_sems = refs[2 * n:]
        x, y, c = _my_place()
        me = 4 * x + 2 * y + c
        flip = lambda v, f: 1 - v if f else v
        peers = [(flip(x, fx), flip(y, fy), flip(c, fc)) for fx in (0, 1) for fy in (0, 1) for fc in (0, 1)][1:]
        cps = []
        for p in range(n):
            out[p][me] = src[p][...]
            for k, peer in enumerate(peers):
                cps.append(pltpu.make_async_remote_copy(
                    src_ref=src[p], dst_ref=out[p].at[me], send_sem=send_sems.at[7 * p + k],
                    recv_sem=recv_sems.at[7 * p + k], device_id=peer, device_id_type=MESH))
        for cp in cps:
            cp.start()
        for cp in cps:
            cp.wait()

    return pl.pallas_call(
        body, name=name, in_specs=[vmem] * n, out_specs=[vmem] * n,
        out_shape=[_sds((8,) + a.shape, a.dtype) for a in srcs],
        scratch_shapes=[pltpu.SemaphoreType.DMA((7 * n,)), pltpu.SemaphoreType.DMA((7 * n,))],
    )(*srcs)


def _adaln_exchange(c, w_ada, b_part):
    vmem = pl.BlockSpec(memory_space=pltpu.VMEM)

    def body(c_ref, w_ref, b_ref, call_ref, mod_ref, part_ref, send_sems, recv_sems):
        x, y, c = _my_place()
        me = 4 * x + 2 * y + c
        flip = lambda v, f: 1 - v if f else v
        peers = [(flip(x, fx), flip(y, fy), flip(c, fc)) for fx in (0, 1) for fy in (0, 1) for fc in (0, 1)][1:]
        call_ref[me] = c_ref[...]
        first = [pltpu.make_async_remote_copy(src_ref=c_ref, dst_ref=call_ref.at[me], send_sem=send_sems.at[k],
                                              recv_sem=recv_sems.at[k], device_id=peer, device_id_type=MESH)
                 for k, peer in enumerate(peers)]
        for cp in first:
            cp.start()
        for cp in first:
            cp.wait()
        c_all = jnp.concatenate([call_ref[j] for j in range(8)], axis=0)
        part_ref[...] = _dot(c_all.astype(CDT), w_ref[...].astype(CDT), NN) + b_ref[...]
        mod_ref[me] = part_ref[pl.ds(me, 1), :]
        second = [pltpu.make_async_remote_copy(
            src_ref=part_ref.at[pl.ds(4 * px + 2 * py + pc, 1)], dst_ref=mod_ref.at[me], send_sem=send_sems.at[7 + k],
            recv_sem=recv_sems.at[7 + k], device_id=(px, py, pc), device_id_type=MESH)
            for k, (px, py, pc) in enumerate(peers)]
        for cp in second:
            cp.start()
        for cp in second:
            cp.wait()

    return pl.pallas_call(
        body, name="adaln_exchange", in_specs=[vmem] * 3, out_specs=[vmem] * 2,
        out_shape=[_sds((8, 1, D), F32), _sds((8, 1, w_ada.shape[1]), F32)],
        scratch_shapes=[pltpu.VMEM((8, w_ada.shape[1]), F32), pltpu.SemaphoreType.DMA((14,)),
                        pltpu.SemaphoreType.DMA((14,))],
        compiler_params=_cp(),
    )(c, w_ada, b_part)


def _pair_add(dw, recv, axis, width, cidx, name):
    if axis == 0:
        tr = min(width, 256)
        C = dw.shape[1]
        grid = (4, width // tr)
        nb = width // tr
        own = pl.BlockSpec((tr, C), lambda q, i, c_ref: ((2 * q + c_ref[0]) * nb + i, 0))
        oth = pl.BlockSpec((None, tr, C), lambda q, i, c_ref: (q, i, 0))
    else:
        R = dw.shape[0]
        tr = min(R, 256)
        grid = (4, R // tr)
        own = pl.BlockSpec((tr, width), lambda q, i, c_ref: (i, 2 * q + c_ref[0]))
        oth = pl.BlockSpec((None, tr, width), lambda q, i, c_ref: (q, i, 0))

    def body(c_ref, a_ref, b_ref, o_ref):
        o_ref[...] = (a_ref[...] + b_ref[...]).astype(o_ref.dtype)

    return pl.pallas_call(
        body, name=name,
        grid_spec=pltpu.PrefetchScalarGridSpec(num_scalar_prefetch=1, grid=grid, in_specs=[own, oth], out_specs=oth),
        out_shape=_sds(recv.shape, CDT), compiler_params=_cp(("parallel", "parallel")),
    )(cidx, dw, recv)


def _add_pairs(xs, ys, name):
    n = len(xs)

    def body(*refs):
        for p in range(n):
            refs[2 * n + p][...] = refs[p][...] + refs[n + p][...]

    return pl.pallas_call(body, name=name, out_shape=[_sds(a.shape, F32) for a in xs])(*xs, *ys)


def _adam_math(w, g, m, v):
    m = ADAM_B1 * m + (1.0 - ADAM_B1) * g
    v = ADAM_B2 * v + (1.0 - ADAM_B2) * (g * g)
    m_hat = m / (1.0 - ADAM_B1 ** ADAM_STEP)
    v_hat = v / (1.0 - ADAM_B2 ** ADAM_STEP)
    delta = -ADAM_LR * (m_hat / (jnp.sqrt(v_hat) + ADAM_EPS) + ADAM_WD * w)
    return delta, m, v


def _adam(gparts, w, m, v, name):
    P, R, C = gparts.shape
    tr = R if R <= 256 else 256

    def body(g_ref, w_ref, m_ref, v_ref, go_ref, d_ref, mo_ref, vo_ref):
        g = g_ref[0].astype(F32)
        for i in range(1, P):
            g = g + g_ref[i].astype(F32)
        d, mn, vn = _adam_math(w_ref[...], g, m_ref[...], v_ref[...])
        go_ref[...] = g
        d_ref[...] = d
        mo_ref[...] = mn
        vo_ref[...] = vn

    t = pl.BlockSpec((tr, C), lambda i: (i, 0))
    return pl.pallas_call(
        body, name=name, grid=(R // tr,),
        in_specs=[pl.BlockSpec((P, tr, C), lambda i: (0, i, 0)), t, t, t], out_specs=[t, t, t, t],
        out_shape=[_sds((R, C), F32)] * 4, compiler_params=_cp(("parallel",)),
    )(gparts, w, m, v)


def _sum_parts(gparts, name):
    P, R, C = gparts.shape

    def body(g_ref, o_ref):
        g = g_ref[0]
        for i in range(1, P):
            g = g + g_ref[i]
        o_ref[...] = g

    return pl.pallas_call(body, name=name, out_shape=_sds((R, C), F32))(gparts)


def _adam_w_ada(c_t, dm, w, m, v):
    R, C = w.shape

    def body(c_ref, dm_ref, w_ref, m_ref, v_ref, go_ref, d_ref, mo_ref, vo_ref):
        g = c_ref[:, 0:1] * dm_ref[0:1, :]
        for b in range(1, 8):
            g = g + c_ref[:, b:b + 1] * dm_ref[b:b + 1, :]
        d, mn, vn = _adam_math(w_ref[...], g, m_ref[...], v_ref[...])
        go_ref[...] = g
        d_ref[...] = d
        mo_ref[...] = mn
        vo_ref[...] = vn

    return pl.pallas_call(body, name="adam_w_ada", out_shape=[_sds((R, C), F32)] * 4,
                          compiler_params=_cp())(c_t, dm, w, m, v)


def kernel(x, c, norm_g, w_ada, b_ada, w_in, pool_w, pool_scale, w_attn_br, w_pool_br, w_out, rel_bias, final_g, loss_target, m_norm_g, m_w_ada, m_b_ada, m_w_in, m_pool_w, m_pool_scale, m_w_attn_br, m_w_pool_br, m_w_out, m_rel_bias, m_final_g, v_norm_g, v_w_ada, v_b_ada, v_w_in, v_pool_w, v_pool_scale, v_w_attn_br, v_w_pool_br, v_w_out, v_rel_bias, v_final_g):
    S = x.shape[1]
    px, py, pc = _my_place()
    me = 4 * px + 2 * py + pc
    cidx = jnp.reshape(pc, (1,)).astype(jnp.int32)

    layout = [(1, 1024), (1, 128), (1, 128), (0, 128)]
    final_g2 = final_g.reshape(1, D)
    me1 = jnp.reshape(me, (1,)).astype(jnp.int32)

    b_part = lax.dynamic_slice(b_ada, (0, me * 384), (1, 384))
    c_all, mod_all = _adaln_exchange(c, w_ada[0], b_part)
    c_all = c_all.reshape(8, D)
    mod = mod_all.reshape(1, 3 * D)
    shift, scale, gate = mod[:, :D], mod[:, D:2 * D], mod[:, 2 * D:]

    gplan = _gather_chips_plan([(None, 1)] * 4)
    lands = _place_slot([w_in[0], w_attn_br[0], w_pool_br[0], w_out[0]], me1, 8, "place_own_shards", CDT)
    g_sems, _, lands, tok = _split_start("gather_chips_start", [], lands, 8, gplan, dep=mod_all)
    st = _local_pre(x[0], shift, scale, norm_g, rel_bias, dep=tok)
    _, lands = _split_wait("gather_chips_wait", g_sems, [], lands, gplan,
                           [st["ht"]] + [b for pair in st["biases"] for b in pair])
    w_in_full, wab8, wpb8, wout8 = _pair_share(lands)
    wab, wpb = _cols_from_blocks(wab8, "w_attn_br_cols"), _cols_from_blocks(wpb8, "w_pool_br_cols")
    wout = wout8.reshape(D, D)

    st = _local_grads(st, loss_target[0], gate, w_in_full, pool_w[0], pool_scale, wab, wpb, wout, final_g2)

    early = [st["dpw"].reshape(AW, 128), st["dps"], st["gfg"], jnp.broadcast_to(st["loss"], (1, 128))]
    dws = [st["dw_in"], st["dw_ab"], st["dw_pb"], st["dw_out"]]
    pplan = _pair_plan(layout)
    pair_shapes = [(4, w, a.shape[1]) if ax == 0 else (4, a.shape[0], w) for a, (ax, w) in zip(dws, layout)]
    p_sems, srcs, recv, tok = _split_start(
        "reduce_pair_start", dws + early, [lax.empty(s, F32) for s in pair_shapes + [a.shape for a in early]], 20,
        pplan)
    d_rel_bias = _local_bias_grads(st, dep=tok)
    srcs, recv = _split_wait("reduce_pair_wait", p_sems, srcs, recv, pplan, d_rel_bias)
    parts = [_pair_add(dw, r, ax, w, cidx, f"pair_add{i}")
             for i, (dw, r, (ax, w)) in enumerate(zip(srcs[:4], recv[:4], layout))]
    parts += _add_pairs(srcs[4:], recv[4:], "pair_add_smalls")

    slot1 = jnp.reshape(2 * px + py, (1,)).astype(jnp.int32)
    cplan = _chips_plan(4)
    c_sems, parts, lands, tok = _split_start("reduce_chips_start", parts,
                                             _place_slot(parts, slot1, 4, "place_own_partials"), 24, cplan)
    grad_x, dmod, d_norm_g = _local_input_grads(st, w_in_full, norm_g, scale, dep=tok)

    dmod_all, ng_all, rb_all = _all_gather_small([dmod, d_norm_g, d_rel_bias], "gather_late")
    o_norm_g = _adam(ng_all, norm_g, m_norm_g, v_norm_g, "adam_norm_g")
    o_rel_bias = _adam(rb_all, rel_bias, m_rel_bias, v_rel_bias, "adam_rel_bias")
    o_b_ada = _adam(dmod_all, b_ada, m_b_ada, v_b_ada, "adam_b_ada")
    dm_mine = lax.dynamic_slice(dmod_all.reshape(8, 3 * D), (0, me * 384), (8, 384))
    o_w_ada = [o[None] for o in _adam_w_ada(c_all.T, dm_mine, w_ada[0], m_w_ada[0], v_w_ada[0])]

    _, (g_in, g_ab, g_pb, g_out, g_pw, g_ps, g_fg, g_loss) = _split_wait(
        "reduce_chips_wait", c_sems, parts, lands, cplan, [o_w_ada[0], o_b_ada[0], o_norm_g[0], o_rel_bias[0]])
    o_w_in = [o[None] for o in _adam(g_in, w_in[0], m_w_in[0], v_w_in[0], "adam_w_in")]
    o_w_ab = [o[None] for o in _adam(g_ab, w_attn_br[0], m_w_attn_br[0], v_w_attn_br[0], "adam_w_attn_br")]
    o_w_pb = [o[None] for o in _adam(g_pb, w_pool_br[0], m_w_pool_br[0], v_w_pool_br[0], "adam_w_pool_br")]
    o_w_out = [o[None] for o in _adam(g_out, w_out[0], m_w_out[0], v_w_out[0], "adam_w_out")]
    o_pool_w = [o.reshape(pool_w.shape) for o in _adam(g_pw, pool_w.reshape(AW, 128), m_pool_w.reshape(AW, 128),
                                                       v_pool_w.reshape(AW, 128), "adam_pool_w")]
    o_pool_scale = _adam(g_ps, pool_scale, m_pool_scale, v_pool_scale, "adam_pool_scale")
    o_final_g = [o.reshape(D) for o in _adam(g_fg, final_g2, m_final_g.reshape(1, D), v_final_g.reshape(1, D),
                                             "adam_final_g")]
    loss = _sum_parts(g_loss, "sum_loss")[0, 0]

    per_w = [o_norm_g, o_w_ada, o_b_ada, o_w_in, o_pool_w, o_pool_scale, o_w_ab, o_w_pb, o_w_out, o_rel_bias, o_final_g]
    outs = [loss, grad_x[None]]
    for k in range(4):
        outs += [o[k] for o in per_w]
    return tuple(outs)
```

```python
import functools
import math

import numpy as np
import jax
import jax.numpy as jnp
from jax import lax
from jax.experimental import pallas as pl
from jax.experimental.pallas import tpu as pltpu

F32 = jnp.float32
CDT = jnp.bfloat16
D = 1024
HD = 64
NH = 8
AW = 512
QKV_W = 4608
REST_W = 3584
R16 = 16
EPS = 1e-6
NEG = -1e30
N_BACK = 128
ADAM_LR, ADAM_B1, ADAM_B2, ADAM_EPS, ADAM_WD, ADAM_STEP = 0.001, 0.9, 0.999, 1e-08, 0.01, 10
MESH = pl.DeviceIdType.MESH
VMEM_MB = 1024 * 1024


def _cp(sem=None, vmem=48, **kw):
    if sem is not None:
        kw["dimension_semantics"] = sem
    return pltpu.CompilerParams(vmem_limit_bytes=vmem * VMEM_MB, **kw)


def _sds(shape, dtype):
    return jax.ShapeDtypeStruct(tuple(shape), dtype)


def _rest_blk(jj):
    return jnp.where(jj == 0, 9, jnp.where(jj == 6, 10, jj + 10))


def _sigmoid(z):
    return 0.5 * jnp.tanh(0.5 * z) + 0.5


def _dot(a, b, dims):
    return lax.dot_general(a, b, (dims, ((), ())), preferred_element_type=F32)


NN = ((1,), (0,))
NT = ((1,), (1,))
TN = ((0,), (0,))


def _ld(ref):
    if len(ref.shape) == 2:
        return ref[...]
    return jnp.concatenate([ref[a] for a in range(ref.shape[0])], axis=0)


def _st(ref, val, cols=None):
    val = val.astype(ref.dtype)
    if len(ref.shape) == 2:
        if cols is None:
            ref[...] = val
        else:
            ref[:, cols] = val
        return
    rows = ref.shape[1]
    for a in range(ref.shape[0]):
        if cols is None:
            ref[a] = val[a * rows:(a + 1) * rows]
        else:
            ref[a, :, cols] = val[a * rows:(a + 1) * rows]


REST_COLS = ((9, 10), (11, 16), (10, 11))
_HBM = pl.BlockSpec(memory_space=pltpu.HBM)


def _resident(shape):
    return pl.BlockSpec(shape, lambda *ids: (0,) * len(shape), pipeline_mode=pl.Buffered(1))


def _w512(w_ref, j):
    return w_ref[j // 2, :, 512 * (j % 2):512 * (j % 2 + 1)]


def _proj(h, w):
    S = h.shape[0]
    tm = 512

    def body(a_ref, w_ref, q_ref, r_ref):
        a = a_ref[...]
        for j in range(9):
            q_ref[:, 512 * j:512 * (j + 1)] = _dot(a, _w512(w_ref, j), NN).astype(q_ref.dtype)
        off = 0
        for lo, hi in REST_COLS:
            for j in range(lo, hi):
                r_ref[:, off:off + 512] = _dot(a, _w512(w_ref, j), NN)
                off += 512

    return pl.pallas_call(
        body, name="proj", grid=(S // tm,),
        in_specs=[pl.BlockSpec((tm, D), lambda i: (i, 0)), _resident((8, D, 1024))],
        out_specs=[pl.BlockSpec((tm, QKV_W), lambda i: (i, 0)), pl.BlockSpec((tm, REST_W), lambda i: (i, 0))],
        out_shape=[_sds((S, QKV_W), CDT), _sds((S, REST_W), F32)],
        compiler_params=_cp(("parallel",), vmem=56),
    )(h, w)


def _after(dep):
    return ([], []) if dep is None else ([pl.BlockSpec(memory_space=pl.ANY)], [dep])


def _dh(dqkv, drest, w, dep=None):
    S = dqkv.shape[0]
    tm = 512
    dep_specs, dep_args = _after(dep)

    def body(a_ref, b_ref, w_ref, *rest):
        o_ref = rest[-1]
        acc = _dot(a_ref[:, 0:512], _w512(w_ref, 0), NT)
        for j in range(1, 9):
            acc = acc + _dot(a_ref[:, 512 * j:512 * (j + 1)], _w512(w_ref, j), NT)
        off = 0
        for lo, hi in REST_COLS:
            for j in range(lo, hi):
                acc = acc + _dot(b_ref[:, off:off + 512], _w512(w_ref, j), NT)
                off += 512
        o_ref[...] = acc

    return pl.pallas_call(
        body, name="dh", grid=(S // tm,),
        in_specs=[pl.BlockSpec((tm, QKV_W), lambda i: (i, 0)), pl.BlockSpec((tm, REST_W), lambda i: (i, 0)),
                  _resident((8, D, 1024))] + dep_specs,
        out_specs=pl.BlockSpec((tm, D), lambda i: (i, 0)),
        out_shape=_sds((S, D), F32),
        compiler_params=_cp(("parallel",), vmem=56),
    )(dqkv, drest, w, *dep_args)


def _mm_tn(at, b, *, n_total, out_blk, name, prev=None, tn=512):
    R, S = at.shape
    nb = b.shape[1] // tn

    def body(*refs):
        refs[-1][...] = _dot(refs[0][...], refs[1][...], NN)

    in_specs = [_resident((R, S)), pl.BlockSpec((S, tn), lambda j: (0, j))]
    args = [at, b]
    aliases = {}
    if prev is not None:
        in_specs.append(pl.BlockSpec(memory_space=pl.ANY))
        args.append(prev)
        aliases = {2: 0}
    return pl.pallas_call(
        body, name=name, grid=(nb,), in_specs=in_specs,
        out_specs=pl.BlockSpec((R, tn), lambda j: (0, out_blk(j))),
        out_shape=_sds((R, n_total), F32),
        input_output_aliases=aliases,
        compiler_params=_cp(("parallel",)),
    )(*args)


def _tcast(a, name):
    M, C = a.shape
    tm = 512

    def body(a_ref, o_ref):
        o_ref[...] = a_ref[...].astype(F32).T.astype(o_ref.dtype)

    return pl.pallas_call(
        body, name=name, grid=(M // tm,),
        in_specs=[pl.BlockSpec((tm, C), lambda i: (i, 0))],
        out_specs=pl.BlockSpec((C, tm), lambda i: (0, i)),
        out_shape=_sds((C, M), CDT), compiler_params=_cp(("parallel",)),
    )(a)


TOK = R16 * R16


def _swap16():
    i = lax.broadcasted_iota(jnp.int32, (TOK, TOK), 0)
    j = lax.broadcasted_iota(jnp.int32, (TOK, TOK), 1)
    return jnp.where(j == jnp.bitwise_and(i, 15) * 16 + jnp.right_shift(i, 4), 1.0, 0.0).astype(CDT)


def _norm_mod(x, norm_g, shift, scale, dep=None):
    S = x.shape[0]
    dep_specs, dep_args = _after(dep)

    def body(x_ref, g_ref, sh_ref, sc_ref, *rest):
        xv = x_ref[...]
        xn = xv * lax.rsqrt(jnp.mean(xv * xv, axis=-1, keepdims=True) + EPS)
        h = (xn * (g_ref[...] * (1.0 + sc_ref[...])) + sh_ref[...]).astype(CDT)
        _st(rest[-1], _dot(_swap16(), h, NN))

    vec = pl.BlockSpec((1, D), lambda i: (0, 0))
    return pl.pallas_call(
        body, name="norm_mod", grid=(S // TOK,),
        in_specs=[pl.BlockSpec((TOK, D), lambda i: (i, 0)), vec, vec, vec] + dep_specs,
        out_specs=pl.BlockSpec((R16, R16, D), lambda i: (0, i, 0)),
        out_shape=_sds((R16, S // R16, D), CDT), compiler_params=_cp(("parallel",)),
    )(x, norm_g, shift, scale, *dep_args)


def _t5_bucket_np(n):
    nf = np.maximum(n, 1).astype(np.float32)
    large = 16 + (np.log(nf / np.float32(16)) / np.float32(math.log(2048 / 16)) * np.float32(16)).astype(np.int32)
    large = np.minimum(large, 31)
    return np.where(n < 16, n, large).astype(np.int32)


class _Geom:
    def __init__(self, gi, LS):
        self.gi, self.LS = gi, LS
        self.dil = (1, 4, 16)[gi]
        if gi == 0:
            self.lead, self.rows, self.n_tiles, self.has_prev = (R16,), 16, LS // 16, True
            a = np.arange(R16)[:, None]
            ll = np.arange(16)[None, :]
            self.pos = (16 * ll + a).reshape(-1)
        elif gi == 1:
            self.lead, self.rows, self.n_tiles, self.has_prev = (4,), 64, LS // 64, True
            a = np.arange(4)[:, None]
            ll = np.arange(64)[None, :]
            self.pos = (4 * ll + a).reshape(-1)
        else:
            self.lead, self.rows, self.n_tiles, self.has_prev = (), LS, 1, False
            self.pos = np.arange(LS)
        self.T = self.pos.shape[0]
        self.Tk = 2 * self.T if self.has_prev else self.T
        self.outer = {0: (), 1: (4,), 2: (R16,)}[gi]

    def view(self, arr):
        S, C = arr.shape
        if self.gi == 1:
            return arr.reshape(4, 4, self.LS, C)
        return arr.reshape(R16, self.LS, C)

    def view_shape(self, C):
        return (4, 4, self.LS, C) if self.gi == 1 else (R16, self.LS, C)

    def unview(self, arr):
        return arr.reshape(R16 * self.LS, arr.shape[-1])

    def spec(self, W, cblk, tile):
        if self.gi == 0:
            return pl.BlockSpec((R16, 16, W), lambda n: (0, tile(n), cblk))
        if self.gi == 1:
            return pl.BlockSpec((4, None, 64, W), lambda r, n: (0, r, tile(n), cblk))
        return pl.BlockSpec((None, self.LS, W), lambda r: (r, 0, cblk))

    def halves(self):
        R = self.rows
        idx = np.arange(self.T).reshape(-1, R)
        return idx[:, :R // 2].reshape(-1), idx[:, R // 2:].reshape(-1)

    def bucket_index(self, sub):
        lo, hi = self.halves()
        pos = self.pos
        if sub == 0:
            pq = pos[lo]
            pk = np.concatenate([pos[hi] - self.T, pos[lo]]) if self.has_prev else np.concatenate([pos[lo], pos[hi]])
        else:
            pq = pos[hi]
            pk = np.concatenate([pos[lo], pos[hi]])
        dist = pq[:, None] - pk[None, :]
        ok = (dist >= 0) & (dist <= N_BACK)
        bucket = _t5_bucket_np(np.clip(dist, 0, N_BACK) * self.dil)
        return np.where(ok, bucket, -1).astype(np.int32)


def _bias_table(idx, buckets, rel_bias, gi, sub, dep=None):
    T, Tk = idx.shape
    tr = 32
    dep_specs, dep_args = _after(dep)

    def body(idx_ref, rb_ref, *rest):
        h = pl.program_id(0)
        ix = idx_ref[...]
        acc = jnp.full(ix.shape, NEG, F32)
        for b in buckets:
            acc = jnp.where(ix == b, rb_ref[b, gi * NH + h], acc)
        rest[-1][...] = acc

    return pl.pallas_call(
        body, name=f"bias_table{gi}{'ab'[sub]}", grid=(NH, T // tr),
        in_specs=[pl.BlockSpec((tr, Tk), lambda h, i: (i, 0)), pl.BlockSpec(memory_space=pltpu.SMEM)] + dep_specs,
        out_specs=pl.BlockSpec((None, tr, Tk), lambda h, i: (h, i, 0)),
        out_shape=_sds((NH, T, Tk), F32), compiler_params=_cp(("parallel", "parallel")),
    )(idx, rel_bias, *dep_args)


def _bias_grad(dsum, idx, buckets, gi, sub, dep=None):
    _, T, Tk = dsum.shape
    dep_specs, dep_args = _after(dep)

    def body(ds_ref, idx_ref, *rest):
        o_ref, r_ref = rest[-2], rest[-1]
        r_ref[...] = jnp.zeros_like(r_ref)
        ix, ds = idx_ref[...], ds_ref[...]
        for b in buckets:
            r_ref[b:b + 1, :] = jnp.sum(jnp.where(ix == b, ds, 0.0), axis=0, keepdims=True)
        o_ref[...] = jnp.broadcast_to(jnp.sum(r_ref[...], axis=1, keepdims=True), (32, 128))

    out = pl.pallas_call(
        body, name=f"bias_grad{gi}{'ab'[sub]}", grid=(NH,),
        in_specs=[pl.BlockSpec((None, T, Tk), lambda h: (h, 0, 0)), pl.BlockSpec((T, Tk), lambda h: (0, 0))] + dep_specs,
        out_specs=pl.BlockSpec((None, 32, 128), lambda h: (h, 0, 0)),
        out_shape=_sds((NH, 32, 128), F32), scratch_shapes=[pltpu.VMEM((32, Tk), F32)],
        compiler_params=_cp(("parallel",)),
    )(dsum, idx, *dep_args)
    return out[:, :, 0].T


def _prev_mask(n, T, Tk):
    col = lax.broadcasted_iota(jnp.int32, (1, Tk), 1)
    return jnp.where((col < T) & (n == 0), NEG, 0.0).astype(F32)


def _split_rows(v, R):
    A, half = v.shape[0] // R, R // 2
    via_f32 = v.dtype != F32 and half % 16 != 0
    w = v.astype(F32) if via_f32 else v
    lo = jnp.concatenate([w[a * R:a * R + half] for a in range(A)], axis=0)
    hi = jnp.concatenate([w[a * R + half:(a + 1) * R] for a in range(A)], axis=0)
    return (lo.astype(v.dtype), hi.astype(v.dtype)) if via_f32 else (lo, hi)


def _merge_rows(lo, hi, R):
    half = R // 2
    A = lo.shape[0] // half
    return jnp.concatenate([x[a * half:(a + 1) * half] for a in range(A) for x in (lo, hi)], axis=0)


def _attn_fwd(qkv, biases, geom):
    S = qkv.shape[0]
    gi, T, R, has_prev = geom.gi, geom.T, geom.rows, geom.has_prev
    H2 = T // 2
    qv = geom.view(qkv)
    cq, ck, cv = 3 * gi, 3 * gi + 1, 3 * gi + 2

    def body(*refs):
        if has_prev:
            q_ref, kp_ref, kc_ref, vp_ref, vc_ref, ba_ref, bb_ref, o_ref, l_ref = refs
        else:
            q_ref, kc_ref, vc_ref, ba_ref, bb_ref, o_ref, l_ref = refs
        q_lo, q_hi = _split_rows(_ld(q_ref), R)
        kc_lo, kc_hi = _split_rows(_ld(kc_ref), R)
        vc_lo, vc_hi = _split_rows(_ld(vc_ref), R)
        k_b = jnp.concatenate([kc_lo, kc_hi], axis=0)
        v_b = jnp.concatenate([vc_lo, vc_hi], axis=0)
        if has_prev:
            n = pl.program_id(len(geom.outer))
            k_a = jnp.concatenate([_split_rows(_ld(kp_ref), R)[1], kc_lo], axis=0)
            v_a = jnp.concatenate([_split_rows(_ld(vp_ref), R)[1], vc_lo], axis=0)
            pm = _prev_mask(n, H2, T)
        else:
            k_a, v_a, pm = k_b, v_b, None
        low = lax.broadcasted_iota(jnp.int32, (1, 2 * HD), 1) < HD
        for j in range(NH // 2):
            cols = slice(2 * HD * j, 2 * HD * (j + 1))
            o_half, l_half = [], []
            for q, k, v, b_ref, mask in ((q_lo, k_a, v_a, ba_ref, pm), (q_hi, k_b, v_b, bb_ref, None)):
                q2 = q[:, cols].astype(F32) * 0.125
                k2, v2 = k[:, cols], v[:, cols]
                qs = jnp.concatenate([jnp.where(low, q2, 0.0), jnp.where(low, 0.0, q2)], axis=0).astype(q.dtype)
                s = _dot(qs, k2, NT) + b_ref[j]
                if mask is not None:
                    s = s + mask
                m = jnp.max(s, axis=1, keepdims=True)
                p = jnp.exp(s - m)
                l = jnp.sum(p, axis=1, keepdims=True)
                pv = _dot(p.astype(v.dtype), v2, NN) / l
                lse = m + jnp.log(l)
                o_half.append(jnp.where(low, pv[:H2], pv[H2:]))
                l_half.append(jnp.where(low, lse[:H2], lse[H2:]))
            _st(o_ref, _merge_rows(o_half[0], o_half[1], R), cols)
            _st(l_ref, _merge_rows(l_half[0], l_half[1], R), cols)

    cur = lambda n: n
    prev = lambda n: jnp.maximum(n - 1, 0)
    if has_prev:
        in_specs = [geom.spec(AW, cq, cur), geom.spec(AW, ck, prev), geom.spec(AW, ck, cur),
                    geom.spec(AW, cv, prev), geom.spec(AW, cv, cur)]
        args = [qv] * 5
    else:
        in_specs = [geom.spec(AW, cq, cur), geom.spec(AW, ck, cur), geom.spec(AW, cv, cur)]
        args = [qv] * 3
    nd = len(geom.outer) + (1 if has_prev else 0)
    biases = [b.reshape(NH // 2, 2 * H2, b.shape[2]) for b in biases]
    in_specs += [pl.BlockSpec(b.shape, lambda *ids: (0, 0, 0)) for b in biases]
    grid = geom.outer + ((geom.n_tiles,) if has_prev else ())
    oshape = geom.view_shape(AW)
    o, l = pl.pallas_call(
        body, name=f"attn_fwd{gi}", grid=grid, in_specs=in_specs,
        out_specs=[geom.spec(AW, 0, cur), geom.spec(AW, 0, cur)],
        out_shape=[_sds(oshape, F32), _sds(oshape, F32)],
        compiler_params=_cp(("arbitrary",) * nd),
    )(*args, *biases)
    return geom.unview(o), geom.unview(l)


def _attn_combine(os_, ls_, rest):
    S = rest.shape[0]
    tm = 512

    def body(o0, o1, o2, l0, l1, l2, z_ref, attn_ref, L_ref, ag_ref, agt_ref):
        la, lb, lc = l0[...], l1[...], l2[...]
        m = jnp.maximum(jnp.maximum(la, lb), lc)
        ea, eb, ec = jnp.exp(la - m), jnp.exp(lb - m), jnp.exp(lc - m)
        den = ea + eb + ec
        attn = (ea * o0[...] + eb * o1[...] + ec * o2[...]) / den
        attn_ref[...] = attn
        L_ref[...] = m + jnp.log(den)
        z = z_ref[...]
        ag = attn * (z * _sigmoid(z))
        ag_ref[...] = ag.astype(ag_ref.dtype)
        agt_ref[...] = ag.T.astype(agt_ref.dtype)

    t = pl.BlockSpec((tm, AW), lambda i: (i, 0))
    return pl.pallas_call(
        body, name="attn_combine", grid=(S // tm,),
        in_specs=[t] * 6 + [pl.BlockSpec((tm, AW), lambda i: (i, 0))],
        out_specs=[t, t, t, pl.BlockSpec((AW, tm), lambda i: (0, i))],
        out_shape=[_sds((S, AW), F32), _sds((S, AW), F32), _sds((S, AW), CDT), _sds((AW, S), CDT)],
        compiler_params=_cp(("parallel",)),
    )(*os_, *ls_, rest)


def _pool_counts(r, LS, win):
    l = lax.broadcasted_iota(jnp.int32, (LS, 1), 0)
    return jnp.minimum(16 * l + (r + 1), win).astype(F32)


def _pool_fwd(rest, pool_w, pool_scale):
    S = rest.shape[0]
    LS = S // R16
    r3 = rest.reshape(R16, LS, REST_W)
    PG = 128

    def body(u_ref, z_ref, pw_ref, ps_ref, pt_ref, mp_ref, pg_ref, pgt_ref, e_ref):
        g = pl.program_id(0)
        win = jnp.left_shift(2, g)
        row = lax.broadcasted_iota(jnp.int32, (LS, PG), 0)
        for r in range(R16):
            e_ref[R16 + r] = u_ref[r]
        for r in range(1, R16):
            e_ref[r] = jnp.where(row == 0, 0.0, pltpu.roll(u_ref[r], 1, 0))

        def double(s):
            for i in range(2 * R16 - 1, 2 * s - 1, -1):
                e_ref[i] = e_ref[i] + e_ref[i - s]

        double(1)
        for k in (1, 2, 3):
            pl.when(g >= k)(functools.partial(double, 2 ** k))
        pw = pw_ref[...].astype(CDT)
        for r in range(R16):
            u = u_ref[r]
            pooled = e_ref[R16 + r] / _pool_counts(r, LS, win) - u
            pc = pooled.astype(CDT)
            mp = _dot(pc, pw, NN)
            z = z_ref[r]
            pg = (mp * ps_ref[...]) * (z * _sigmoid(z))
            mp_ref[r] = mp
            pg_ref[r] = pg.astype(pg_ref.dtype)
            pt_ref[:, r * LS:(r + 1) * LS] = pooled.T.astype(pt_ref.dtype)
            pgt_ref[:, r * LS:(r + 1) * LS] = pg.T.astype(pgt_ref.dtype)

    col = lambda off: pl.BlockSpec((R16, LS, PG), lambda g: (0, 0, off + g))
    tsp = pl.BlockSpec((PG, S), lambda g: (g, 0))
    pt, mp, pg, pgt = pl.pallas_call(
        body, name="pool_fwd", grid=(4,),
        in_specs=[col(24), col(4), pl.BlockSpec((None, PG, PG), lambda g: (g, 0, 0)),
                  pl.BlockSpec((1, PG), lambda g: (0, g))],
        out_specs=[tsp, col(0), col(0), tsp],
        out_shape=[_sds((AW, S), CDT), _sds((R16, LS, AW), F32), _sds((R16, LS, AW), CDT), _sds((AW, S), CDT)],
        scratch_shapes=[pltpu.VMEM((2 * R16, LS, PG), F32)],
        compiler_params=_cp(("parallel",)),
    )(r3, r3, pool_w, pool_scale)
    return pt, mp.reshape(S, AW), pg.reshape(S, AW), pgt


def _branch_merge(ag, pg, wab, wpb, rest):
    S = ag.shape[0]
    tm, tn = 512, 512

    def body(ag_ref, pg_ref, wa_ref, wp_ref, ga_ref, gp_ref, m_ref, mt_ref):
        ya = _dot(ag_ref[...], wa_ref[...], NN)
        yp = _dot(pg_ref[...], wp_ref[...], NN)
        mg = _sigmoid(ga_ref[...]) * ya + _sigmoid(gp_ref[...]) * yp
        m_ref[...] = mg.astype(m_ref.dtype)
        mt_ref[...] = mg.T.astype(mt_ref.dtype)

    a = pl.BlockSpec((tm, AW), lambda i, j: (i, 0))
    w = pl.BlockSpec((AW, tn), lambda i, j: (0, j))
    o = pl.BlockSpec((tm, tn), lambda i, j: (i, j))
    return pl.pallas_call(
        body, name="branch_merge", grid=(S // tm, D // tn),
        in_specs=[a, a, w, w, pl.BlockSpec((tm, tn), lambda i, j: (i, 2 + j)),
                  pl.BlockSpec((tm, tn), lambda i, j: (i, 4 + j))],
        out_specs=[o, pl.BlockSpec((tn, tm), lambda i, j: (j, i))],
        out_shape=[_sds((S, D), CDT), _sds((D, S), CDT)],
        compiler_params=_cp(("parallel", "parallel")),
    )(ag, pg, wab, wpb, rest, rest)


def _out_loss(merged, wout, x, target, gate, final_g):
    S = x.shape[0]

    def body(m_ref, w_ref, x_ref, t_ref, gate_ref, fg_ref, dx2_ref, dmo_ref, loss_ref, gfg_ref, dgate_ref):
        i = pl.program_id(0)

        @pl.when(i == 0)
        def _():
            loss_ref[...] = jnp.zeros_like(loss_ref)
            gfg_ref[...] = jnp.zeros_like(gfg_ref)
            dgate_ref[...] = jnp.zeros_like(dgate_ref)

        swap = _swap16()
        mo = _dot(_dot(swap, _ld(m_ref), NN).astype(CDT), w_ref[...], NN)
        gate, fg = gate_ref[...], fg_ref[...]
        x2 = x_ref[...] + gate * mo
        r2 = lax.rsqrt(jnp.mean(x2 * x2, axis=-1, keepdims=True) + EPS)
        xn2 = x2 * r2
        diff = xn2 * fg - t_ref[...]
        loss = jnp.sum(jnp.sum(diff * diff, axis=-1, keepdims=True), axis=0, keepdims=True)
        dy = diff * (1.0 / D)
        dxn2 = dy * fg
        dx2 = r2 * (dxn2 - xn2 * jnp.mean(dxn2 * xn2, axis=-1, keepdims=True))
        dx2_ref[...] = dx2
        _st(dmo_ref, _dot(swap, (dx2 * gate).astype(CDT), NN))
        loss_ref[...] += jnp.broadcast_to(loss * (0.5 / D), loss_ref.shape)
        gfg_ref[...] += jnp.sum(dy * xn2, axis=0, keepdims=True)
        dgate_ref[...] += jnp.sum(dx2 * mo, axis=0, keepdims=True)

    slab = pl.BlockSpec((R16, R16, D), lambda i: (0, i, 0))
    nat = pl.BlockSpec((TOK, D), lambda i: (i, 0))
    vec = pl.BlockSpec((1, D), lambda i: (0, 0))
    return pl.pallas_call(
        body, name="out_loss", grid=(S // TOK,),
        in_specs=[slab, _resident((D, D)), nat, nat, vec, vec],
        out_specs=[nat, slab, pl.BlockSpec((1, 128), lambda i: (0, 0)), vec, vec],
        out_shape=[_sds((S, D), F32), _sds((R16, S // R16, D), CDT), _sds((1, 128), F32), _sds((1, D), F32),
                   _sds((1, D), F32)],
        compiler_params=_cp(("arbitrary",)),
    )(merged, wout, x, target, gate, final_g)


def _bwd_gates(dmo, wout, wab, wpb, rest, ag, pg, attn, mp, pool_scale):
    S = dmo.shape[0]
    tm = 256

    def body(dmo_ref, wo_ref, wa_ref, wp_ref, za_ref, zp_ref, ga0, ga1, gp0, gp1, ag_ref, pg_ref, at_ref, mp_ref,
             ps_ref, dya_ref, dyp_ref, dr_ref, dat_ref, dl_ref, dmp_ref, dps_ref):
        i = pl.program_id(0)

        @pl.when(i == 0)
        def _():
            dps_ref[...] = jnp.zeros_like(dps_ref)

        dm = _dot(dmo_ref[...], wo_ref[...], NT)
        sa = _sigmoid(jnp.concatenate([ga0[...], ga1[...]], axis=1))
        sp = _sigmoid(jnp.concatenate([gp0[...], gp1[...]], axis=1))
        dya = (dm * sa).astype(CDT)
        dyp = (dm * sp).astype(CDT)
        dya_ref[...] = dya
        dyp_ref[...] = dyp
        ya = _dot(ag_ref[...], wa_ref[...], NN)
        yp = _dot(pg_ref[...], wp_ref[...], NN)
        dr_ref[:, 1024:2048] = (dm * ya * sa * (1.0 - sa)).astype(dr_ref.dtype)
        dr_ref[:, 2048:3072] = (dm * yp * sp * (1.0 - sp)).astype(dr_ref.dtype)
        da = _dot(dya, wa_ref[...], NT)
        dp = _dot(dyp, wp_ref[...], NT)
        za = za_ref[...]
        sga = _sigmoid(za)
        attn = at_ref[...]
        dattn = da * (za * sga)
        dr_ref[:, 0:512] = (da * attn * (sga * (1.0 + za * (1.0 - sga)))).astype(dr_ref.dtype)
        dat_ref[...] = dattn.astype(dat_ref.dtype)
        prod = dattn * attn
        for h in range(NH):
            sl = slice(HD * h, HD * h + HD)
            dl_ref[:, sl] = jnp.broadcast_to(jnp.sum(prod[:, sl], axis=1, keepdims=True), (tm, HD))
        zp = zp_ref[...]
        sgp = _sigmoid(zp)
        mpre = mp_ref[...]
        ps = ps_ref[...]
        dmixed = dp * (zp * sgp)
        dr_ref[:, 512:1024] = (dp * (mpre * ps) * (sgp * (1.0 + zp * (1.0 - sgp)))).astype(dr_ref.dtype)
        dps_ref[...] += jnp.sum(dmixed * mpre, axis=0, keepdims=True)
        dmp_ref[...] = (dmixed * ps).astype(dmp_ref.dtype)

    full = lambda shape: pl.BlockSpec(shape, lambda i: (0, 0))
    tD = pl.BlockSpec((tm, D), lambda i: (i, 0))
    tA = pl.BlockSpec((tm, AW), lambda i: (i, 0))
    rb = lambda c: pl.BlockSpec((tm, AW), lambda i: (i, c))
    return pl.pallas_call(
        body, name="bwd_gates", grid=(S // tm,),
        in_specs=[tD, full((D, D)), full((AW, D)), full((AW, D)), rb(0), rb(1), rb(2), rb(3), rb(4), rb(5),
                  tA, tA, tA, tA, full((1, AW))],
        out_specs=[tD, tD, pl.BlockSpec((tm, 3072), lambda i: (i, 0)), tA, tA, tA, full((1, AW))],
        out_shape=[_sds((S, D), CDT), _sds((S, D), CDT), _sds((S, REST_W), CDT), _sds((S, AW), CDT),
                   _sds((S, AW), F32), _sds((S, AW), CDT), _sds((1, AW), F32)],
        compiler_params=_cp(("arbitrary",), vmem=56),
    )(dmo, wout, wab, wpb, rest, rest, rest, rest, rest, rest, ag, pg, attn, mp, pool_scale)


def _pool_bwd(dmp, pt, pool_w, drest):
    S = dmp.shape[0]
    LS = S // R16
    PG = 128
    d3 = dmp.reshape(R16, LS, AW)
    dr3 = drest.reshape(R16, LS, REST_W)

    def body(d_ref, pt_ref, pw_ref, dr_in, du_ref, dpw_ref, g_ref, s_ref):
        del dr_in
        g = pl.program_id(0)
        win = jnp.left_shift(2, g)
        row = lax.broadcasted_iota(jnp.int32, (LS, PG), 0)
        pw = pw_ref[...].astype(CDT)
        dpw = jnp.zeros((PG, PG), F32)
        for r in range(R16):
            dm = d_ref[r]
            dpw = dpw + _dot(pt_ref[:, r * LS:(r + 1) * LS], dm, NN)
            dpooled = _dot(dm, pw, NT)
            g_ref[r] = dpooled
            sc = dpooled / _pool_counts(r, LS, win)
            s_ref[r] = sc
            if r < R16 - 1:
                s_ref[R16 + r] = jnp.where(row == LS - 1, 0.0, pltpu.roll(sc, LS - 1, 0))
        dpw_ref[...] = dpw

        def double(s):
            for i in range(0, 2 * R16 - 2 * s):
                s_ref[i] = s_ref[i] + s_ref[i + s]

        double(1)
        for k in (1, 2, 3):
            pl.when(g >= k)(functools.partial(double, 2 ** k))
        for r in range(R16):
            du_ref[r] = (s_ref[r] - g_ref[r]).astype(du_ref.dtype)

    du, dpw = pl.pallas_call(
        body, name="pool_bwd", grid=(4,),
        in_specs=[pl.BlockSpec((R16, LS, PG), lambda g: (0, 0, g)), pl.BlockSpec((PG, S), lambda g: (g, 0)),
                  pl.BlockSpec((None, PG, PG), lambda g: (g, 0, 0)), pl.BlockSpec(memory_space=pl.ANY)],
        out_specs=[pl.BlockSpec((R16, LS, PG), lambda g: (0, 0, 24 + g)),
                   pl.BlockSpec((None, PG, PG), lambda g: (g, 0, 0))],
        out_shape=[_sds((R16, LS, REST_W), CDT), _sds((4, PG, PG), F32)],
        scratch_shapes=[pltpu.VMEM((R16, LS, PG), F32), pltpu.VMEM((2 * R16, LS, PG), F32)],
        input_output_aliases={3: 0},
        compiler_params=_cp(("parallel",)),
    )(d3, pt, pool_w, dr3)
    return du.reshape(S, REST_W), dpw


def _attn_bwd(qkv, dattn, Lb, dlb, biases, geom, dqkv_prev):
    S = qkv.shape[0]
    gi, T, R, has_prev, N = geom.gi, geom.T, geom.rows, geom.has_prev, geom.n_tiles
    H2 = T // 2
    nd_outer = len(geom.outer)
    qv = geom.view(qkv)
    cq, ck, cv = 3 * gi, 3 * gi + 1, 3 * gi + 2
    W3 = 3 * AW

    def heads(q_ref, kp_ref, kc_ref, vp_ref, vc_ref, do_ref, L_ref, dl_ref, ba_ref, bb_ref, pm, dsa_ref, dsb_ref, emit):
        q_lo, q_hi = _split_rows(_ld(q_ref), R)
        do_lo, do_hi = _split_rows(_ld(do_ref), R)
        L_lo, L_hi = _split_rows(_ld(L_ref), R)
        dl_lo, dl_hi = _split_rows(_ld(dl_ref), R)
        kc_lo, kc_hi = _split_rows(_ld(kc_ref), R)
        vc_lo, vc_hi = _split_rows(_ld(vc_ref), R)
        k_b = jnp.concatenate([kc_lo, kc_hi], axis=0)
        v_b = jnp.concatenate([vc_lo, vc_hi], axis=0)
        if has_prev:
            k_a = jnp.concatenate([_split_rows(_ld(kp_ref), R)[1], kc_lo], axis=0)
            v_a = jnp.concatenate([_split_rows(_ld(vp_ref), R)[1], vc_lo], axis=0)
        else:
            k_a, v_a = k_b, v_b
        low = lax.broadcasted_iota(jnp.int32, (1, 2 * HD), 1) < HD
        for j in range(NH // 2):
            sl = slice(2 * HD * j, 2 * HD * (j + 1))
            res = []
            for q, k, v, do, L, dl, b_ref, mask, ds_ref in (
                    (q_lo, k_a, v_a, do_lo, L_lo, dl_lo, ba_ref, pm, dsa_ref),
                    (q_hi, k_b, v_b, do_hi, L_hi, dl_hi, bb_ref, None, dsb_ref)):
                q2 = q[:, sl].astype(F32) * 0.125
                do2 = do[:, sl].astype(F32)
                k2, v2 = k[:, sl], v[:, sl]
                L2, dl2 = L[:, sl], dl[:, sl]
                qs = jnp.concatenate([jnp.where(low, q2, 0.0), jnp.where(low, 0.0, q2)], axis=0).astype(q.dtype)
                dom = jnp.concatenate([jnp.where(low, do2, 0.0), jnp.where(low, 0.0, do2)], axis=0).astype(q.dtype)
                Lc = jnp.concatenate([L2[:, 0:1], L2[:, HD:HD + 1]], axis=0)
                dlc = jnp.concatenate([dl2[:, 0:1], dl2[:, HD:HD + 1]], axis=0)
                s = _dot(qs, k2, NT) + b_ref[j]
                if mask is not None:
                    s = s + mask
                p = jnp.exp(s - Lc)
                ds = p * (_dot(dom, v2, NT) - dlc)
                ds_ref[j] += ds
                dsc = ds.astype(q.dtype)
                dq_st = _dot(dsc, k2, NN)
                res.append((jnp.where(low, dq_st[:H2], dq_st[H2:]) * 0.125, _dot(dsc, qs, TN),
                            _dot(p.astype(q.dtype), dom, TN)))
            (dq_lo, dk_a, dv_a), (dq_hi, dk_b, dv_b) = res
            dq = _merge_rows(dq_lo, dq_hi, R)
            if has_prev:
                emit(sl, dq, _merge_rows(dk_a[H2:] + dk_b[:H2], dk_b[H2:], R),
                     _merge_rows(dv_a[H2:] + dv_b[:H2], dv_b[H2:], R), dk_a[:H2], dv_a[:H2])
            else:
                emit(sl, dq, _merge_rows(dk_a[:H2] + dk_b[:H2], dk_a[H2:] + dk_b[H2:], R),
                     _merge_rows(dv_a[:H2] + dv_b[:H2], dv_a[H2:] + dv_b[H2:], R), None, None)

    def first_step():
        ok = pl.program_id(nd_outer) == 0 if has_prev else pl.program_id(0) == 0
        for a in range(nd_outer if has_prev else 0):
            ok = ok & (pl.program_id(a) == 0)
        return ok

    if has_prev:
        def body(q_ref, kp_ref, kc_ref, vp_ref, vc_ref, do_ref, L_ref, dl_ref, ba_ref, bb_ref, prev_in, out_ref,
                 dsa_ref, dsb_ref, hold, new, pp):
            del prev_in
            s_id = pl.program_id(nd_outer)

            @pl.when(first_step())
            def _():
                dsa_ref[...] = jnp.zeros_like(dsa_ref)
                dsb_ref[...] = jnp.zeros_like(dsb_ref)

            @pl.when(s_id < N)
            def _():
                def emit(sl, dq, dk, dv, dk_prev, dv_prev):
                    new[:, sl] = dq
                    new[:, slice(AW + sl.start, AW + sl.stop)] = dk
                    new[:, slice(2 * AW + sl.start, 2 * AW + sl.stop)] = dv
                    pp[:, sl] = dk_prev
                    pp[:, slice(AW + sl.start, AW + sl.stop)] = dv_prev

                heads(q_ref, kp_ref, kc_ref, vp_ref, vc_ref, do_ref, L_ref, dl_ref, ba_ref, bb_ref,
                      _prev_mask(s_id, H2, T), dsa_ref, dsb_ref, emit)

            @pl.when((s_id >= 1) & (s_id < N))
            def _():
                _st(out_ref, hold[:, 0:AW], slice(0, AW))
                late = _merge_rows(jnp.zeros((H2, 2 * AW), F32), pp[...], R)
                _st(out_ref, hold[:, AW:W3] + late, slice(AW, W3))

            @pl.when(s_id == N)
            def _():
                _st(out_ref, hold[...])

            @pl.when(s_id < N)
            def _():
                hold[...] = new[...]

        cur = lambda n: jnp.minimum(n, N - 1)
        prev = lambda n: jnp.clip(n - 1, 0, N - 1)
        delayed = lambda n: jnp.maximum(n - 1, 0)
        in_specs = [geom.spec(AW, cq, cur), geom.spec(AW, ck, prev), geom.spec(AW, ck, cur),
                    geom.spec(AW, cv, prev), geom.spec(AW, cv, cur),
                    geom.spec(AW, 0, cur), geom.spec(AW, 0, cur), geom.spec(AW, 0, cur)]
        args = [qv] * 5 + [geom.view(dattn), geom.view(Lb), geom.view(dlb)]
        out_spec = geom.spec(W3, gi, delayed)
        grid = geom.outer + (N + 1,)
        scratch = [pltpu.VMEM((T, W3), F32), pltpu.VMEM((T, W3), F32), pltpu.VMEM((H2, 2 * AW), F32)]
    else:
        def body(q_ref, kc_ref, vc_ref, do_ref, L_ref, dl_ref, ba_ref, bb_ref, prev_in, out_ref, dsa_ref, dsb_ref):
            del prev_in

            @pl.when(first_step())
            def _():
                dsa_ref[...] = jnp.zeros_like(dsa_ref)
                dsb_ref[...] = jnp.zeros_like(dsb_ref)

            def emit(sl, dq, dk, dv, dk_prev, dv_prev):
                _st(out_ref, dq, sl)
                _st(out_ref, dk, slice(AW + sl.start, AW + sl.stop))
                _st(out_ref, dv, slice(2 * AW + sl.start, 2 * AW + sl.stop))

            heads(q_ref, None, kc_ref, None, vc_ref, do_ref, L_ref, dl_ref, ba_ref, bb_ref, None, dsa_ref, dsb_ref,
                  emit)

        cur = lambda n: n
        in_specs = [geom.spec(AW, cq, cur), geom.spec(AW, ck, cur), geom.spec(AW, cv, cur),
                    geom.spec(AW, 0, cur), geom.spec(AW, 0, cur), geom.spec(AW, 0, cur)]
        args = [qv] * 3 + [geom.view(dattn), geom.view(Lb), geom.view(dlb)]
        out_spec = geom.spec(W3, gi, cur)
        grid = geom.outer
        scratch = []

    whole = lambda b: pl.BlockSpec(b.shape, lambda *ids: (0, 0, 0))
    biases = [b.reshape(NH // 2, 2 * H2, b.shape[2]) for b in biases]
    in_specs += [whole(b) for b in biases]
    in_specs.append(pl.BlockSpec(memory_space=pl.ANY))
    if dqkv_prev is None:
        last, aliases = biases[0], {}
    else:
        last, aliases = geom.view(dqkv_prev), {len(in_specs) - 1: 0}
    dq, dsa, dsb = pl.pallas_call(
        body, name=f"attn_bwd{gi}", grid=grid, in_specs=in_specs,
        out_specs=[out_spec] + [whole(b) for b in biases],
        out_shape=[_sds(geom.view_shape(QKV_W), CDT)] + [_sds(b.shape, F32) for b in biases],
        scratch_shapes=scratch,
        input_output_aliases=aliases,
        compiler_params=_cp(("arbitrary",) * len(grid), vmem=56),
    )(*args, *biases, last)
    return geom.unview(dq), [d.reshape(NH, H2, d.shape[2]) for d in (dsa, dsb)]


def _h_bwd(dh, dx2, x, norm_g, scale):
    S = x.shape[0]

    def body(dh_ref, dx2_ref, x_ref, g_ref, sc_ref, gx_ref, dsh_ref, dsc_ref, dng_ref, nat_ref):
        i = pl.program_id(0)

        @pl.when(i == 0)
        def _():
            dsh_ref[...] = jnp.zeros_like(dsh_ref)
            dsc_ref[...] = jnp.zeros_like(dsc_ref)
            dng_ref[...] = jnp.zeros_like(dng_ref)

        for r in range(R16):
            nat_ref[:, r, :] = dh_ref[r]
        dh = jnp.concatenate([nat_ref[l] for l in range(R16)], axis=0)
        g = g_ref[...]
        one_sc = 1.0 + sc_ref[...]
        xv = x_ref[...]
        r1 = lax.rsqrt(jnp.mean(xv * xv, axis=-1, keepdims=True) + EPS)
        xn = xv * r1
        dhx = dh * xn
        dxn = dh * (g * one_sc)
        dx1 = r1 * (dxn - xn * jnp.mean(dxn * xn, axis=-1, keepdims=True))
        gx_ref[...] = dx2_ref[...] + dx1
        dsh_ref[...] += jnp.sum(dh, axis=0, keepdims=True)
        dsc_ref[...] += jnp.sum(dhx * g, axis=0, keepdims=True)
        dng_ref[...] += jnp.sum(dhx * one_sc, axis=0, keepdims=True)

    nat = pl.BlockSpec((TOK, D), lambda i: (i, 0))
    vec = pl.BlockSpec((1, D), lambda i: (0, 0))
    return pl.pallas_call(
        body, name="h_bwd", grid=(S // TOK,),
        in_specs=[pl.BlockSpec((R16, R16, D), lambda i: (0, i, 0)), nat, nat, vec, vec],
        out_specs=[nat, vec, vec, vec],
        out_shape=[_sds((S, D), F32), _sds((1, D), F32), _sds((1, D), F32), _sds((1, D), F32)],
        scratch_shapes=[pltpu.VMEM((R16, R16, D), F32)],
        compiler_params=_cp(("arbitrary",)),
    )(dh, dx2, x, norm_g, scale)


def _local_step(x, target, shift, scale, gate, norm_g, w_in_full, pool_w, pool_scale, wab, wpb, wout, rel_bias,
                final_g):
    st = _local_pre(x, shift, scale, norm_g, rel_bias)
    st = _local_grads(st, target, gate, w_in_full, pool_w, pool_scale, wab, wpb, wout, final_g)
    d_rel_bias = _local_bias_grads(st)
    gx, dmod, dng = _local_input_grads(st, w_in_full, norm_g, scale)
    return dict(loss=st["loss"], grad_x=gx, dw_in=st["dw_in"], dw_ab=st["dw_ab"], dw_pb=st["dw_pb"],
                dw_out=st["dw_out"], d_norm_g=dng, d_pool_w=st["dpw"], d_pool_scale=st["dps"],
                d_rel_bias=d_rel_bias, d_final_g=st["gfg"], dmod=dmod)


def _local_pre(x, shift, scale, norm_g, rel_bias, dep=None):
    S = x.shape[0]
    LS = S // R16
    geoms = [_Geom(gi, LS) for gi in range(3)]
    idx_np = [[g.bucket_index(sub) for sub in range(2)] for g in geoms]
    buckets = [[tuple(int(b) for b in np.unique(ix) if b >= 0) for ix in pair] for pair in idx_np]
    idxs = [[jnp.asarray(ix) for ix in pair] for pair in idx_np]
    biases = [[_bias_table(idxs[gi][sub], buckets[gi][sub], rel_bias, gi, sub, dep) for sub in range(2)]
              for gi in range(3)]
    h = _norm_mod(x, norm_g, shift, scale, dep).reshape(S, D)
    return dict(geoms=geoms, idxs=idxs, buckets=buckets, biases=biases, x=x, h=h, ht=_tcast(h, "h_transpose"))


def _local_grads(st, target, gate, w_in_full, pool_w, pool_scale, wab, wpb, wout, final_g):
    geoms, idxs, biases, x, h, ht = (st[k] for k in ("geoms", "idxs", "biases", "x", "h", "ht"))
    S = h.shape[0]
    LS = S // R16
    qkv, rest = _proj(h, w_in_full)

    os_, ls_ = [], []
    for gi in range(3):
        o, l = _attn_fwd(qkv, biases[gi], geoms[gi])
        os_.append(o)
        ls_.append(l)
    attn, Lb, ag, agt = _attn_combine(os_, ls_, rest)
    pt, mp, pg, pgt = _pool_fwd(rest, pool_w, pool_scale)
    merged, mt = _branch_merge(ag, pg, wab, wpb, rest)
    dx2, dmo3, loss, gfg, dgate = _out_loss(merged.reshape(R16, LS, D), wout, x, target, gate, final_g)
    dmo = dmo3.reshape(S, D)

    dya, dyp, drest, dattn, dlb, dmp, dps = _bwd_gates(dmo, wout, wab, wpb, rest, ag, pg, attn, mp, pool_scale)
    dw_out = _mm_tn(mt, dmo, n_total=D, out_blk=lambda j: j, name="dw_out")
    dw_ab = _mm_tn(agt, dya, n_total=D, out_blk=lambda j: j, name="dw_attn_br")
    dw_pb = _mm_tn(pgt, dyp, n_total=D, out_blk=lambda j: j, name="dw_pool_br")
    drest, dpw = _pool_bwd(dmp, pt, pool_w, drest)

    dqkv = None
    dsums = []
    for gi in range(3):
        dqkv, dsum = _attn_bwd(qkv, dattn, Lb, dlb, biases[gi], geoms[gi], dqkv)
        dsums.append(dsum)

    dw_in = _mm_tn(ht, dqkv, n_total=8192, out_blk=lambda j: j, name="dw_in_qkv")
    dw_in = _mm_tn(ht, drest, n_total=8192, out_blk=_rest_blk, name="dw_in_rest", prev=dw_in)
    return dict(st, loss=loss[0, 0], dw_in=dw_in, dw_ab=dw_ab, dw_pb=dw_pb, dw_out=dw_out, dpw=dpw, dps=dps, gfg=gfg,
                dgate=dgate, dsums=dsums, dqkv=dqkv, drest=drest, dx2=dx2)


def _local_bias_grads(st, dep=None):
    drb = [_bias_grad(st["dsums"][gi][0], st["idxs"][gi][0], st["buckets"][gi][0], gi, 0, dep if gi == 0 else None) +
           _bias_grad(st["dsums"][gi][1], st["idxs"][gi][1], st["buckets"][gi][1], gi, 1) for gi in range(3)]
    return jnp.concatenate(drb, axis=1)


def _local_input_grads(st, w_in_full, norm_g, scale, dep=None):
    S = st["h"].shape[0]
    LS = S // R16
    dh = _dh(st["dqkv"], st["drest"], w_in_full, dep)
    gx, dsh, dsc, dng = _h_bwd(dh.reshape(R16, LS, D), st["dx2"], st["x"], norm_g, scale)
    return gx, jnp.concatenate([dsh, dsc, st["dgate"]], axis=1), dng


def _my_place():
    return lax.axis_index("x"), lax.axis_index("y"), lax.axis_index("c")


def _block_of(ref, axis, idx, width):
    if axis is None:
        return ref.at[idx]
    start = pl.multiple_of(idx * width, width)
    if axis == 0:
        return ref.at[pl.ds(start, width)]
    return ref.at[:, pl.ds(start, width)]


_SEM = pl.BlockSpec(memory_space=pltpu.SEMAPHORE)
_EFFECT = pltpu.SideEffectType.DATAFLOW_SIDE_EFFECTING


def _split_start(name, srcs, lands, ncopy, plan, dep=None):
    arrs = list(srcs) + list(lands)
    na, ns = len(arrs), len(srcs)
    dep_specs, dep_args = _after(dep)

    def body(*refs):
        send_sems, recv_sems, token = refs[na + len(dep_args)], refs[na + len(dep_args) + 1], refs[-1]
        for k, (s, d, dev) in enumerate(plan(refs[:ns], refs[ns:na])):
            pltpu.make_async_remote_copy(src_ref=s, dst_ref=d, send_sem=send_sems.at[k], recv_sem=recv_sems.at[k],
                                         device_id=dev, device_id_type=MESH).start()
        token[...] = jnp.zeros_like(token)

    outs = pl.pallas_call(
        body, name=name,
        out_shape=(pltpu.SemaphoreType.DMA((ncopy,)), pltpu.SemaphoreType.DMA((ncopy,)),
                   *[pltpu.HBM(a.shape, a.dtype) for a in arrs], _sds((8, 128), F32)),
        in_specs=[_HBM] * na + dep_specs,
        out_specs=(_SEM, _SEM, *[_HBM] * na, pl.BlockSpec(memory_space=pltpu.VMEM)),
        input_output_aliases={i: 2 + i for i in range(na)},
        compiler_params=pltpu.CompilerParams(has_side_effects=_EFFECT),
    )(*[pltpu.with_memory_space_constraint(a, pltpu.HBM) for a in arrs], *dep_args)
    return (outs[0], outs[1]), list(outs[2:2 + ns]), list(outs[2 + ns:2 + na]), outs[-1]


def _split_wait(name, sems, srcs, lands, plan, after):
    arrs = list(srcs) + list(lands)
    na, ns = len(arrs), len(srcs)
    after = list(after) if isinstance(after, (list, tuple)) else [after]

    def body(*refs):
        send_sems, recv_sems = refs[na], refs[na + 1]
        for k, (s, d, dev) in enumerate(plan(refs[:ns], refs[ns:na])):
            cp = pltpu.make_async_remote_copy(src_ref=s, dst_ref=d, send_sem=send_sems.at[k], recv_sem=recv_sems.at[k],
                                              device_id=dev, device_id_type=MESH)
            cp.wait_send()
            cp.wait_recv()

    outs = pl.pallas_call(
        body, name=name, out_shape=[pltpu.HBM(a.shape, a.dtype) for a in arrs],
        in_specs=[_HBM] * na + [_SEM, _SEM] + [pl.BlockSpec(memory_space=pl.ANY)] * len(after),
        out_specs=[_HBM] * na,
        input_output_aliases={i: i for i in range(na)},
        compiler_params=pltpu.CompilerParams(has_side_effects=_EFFECT),
    )(*arrs, sems[0], sems[1], *after)
    return list(outs[:ns]), list(outs[ns:])


def _gather_chips_plan(layout):
    def plan(src, land):
        x, y, c = _my_place()
        me = 4 * x + 2 * y + c
        mine = [_block_of(land[p], layout[p][0], me, layout[p][1]) for p in range(len(land))]
        return [(mine[p], mine[p], (cx, cy, c)) for p in range(len(land)) for cx, cy in [(1 - x, y), (x, 1 - y)]]
    return plan


def _pair_plan(layout):
    def plan(src, land):
        x, y, c = _my_place()
        n = len(layout)
        return [(_block_of(src[p], layout[p][0], 2 * q + (1 - c), layout[p][1]), land[p].at[q], (x, y, 1 - c))
                for p in range(n) for q in range(4)] + [(src[p], land[p], (x, y, 1 - c)) for p in range(n, len(src))]
    return plan


def _chips_plan(n):
    def plan(src, land):
        x, y, c = _my_place()
        chips = [(1 - x, y), (x, 1 - y), (1 - x, 1 - y)]
        return [(src[p].at[2 * cx + cy], land[p].at[2 * x + y], (cx, cy, c)) for p in range(n) for cx, cy in chips] + \
               [(src[p], land[p].at[2 * x + y], (cx, cy, c)) for p in range(n, len(src)) for cx, cy in chips]
    return plan


def _pair_share(gathered):
    n = len(gathered)

    def body(*refs):
        land = refs[n:2 * n]
        send_sems, recv_sems, fwd_send, fwd_recv = refs[2 * n:]
        x, y, c = _my_place()
        k_dg = 3 - (2 * x + y)
        south = c == 0
        pass_on = 4 * jnp.where(south, 1 - x, x) + 2 * jnp.where(south, y, 1 - y) + c
        to = (jnp.where(south, x, 1 - x), jnp.where(south, 1 - y, y), c)

        def forward(p):
            return pltpu.make_async_remote_copy(
                src_ref=land[p].at[pass_on], dst_ref=land[p].at[pass_on], send_sem=fwd_send.at[p],
                recv_sem=fwd_recv.at[p], device_id=to, device_id_type=MESH)

        def to_sibling(p, k):
            blk = land[p].at[2 * k + c]
            return pltpu.make_async_remote_copy(
                src_ref=blk, dst_ref=blk, send_sem=send_sems.at[4 * p + k], recv_sem=recv_sems.at[4 * p + k],
                device_id=(x, y, 1 - c), device_id_type=MESH)

        for p in range(n):
            forward(p).start()
        for p in range(n):
            for k in range(4):
                @pl.when(k != k_dg)
                def _():
                    to_sibling(p, k).start()
        for p in range(n):
            dg = land[p].at[2 * k_dg + c]
            pltpu.make_async_remote_copy(src_ref=dg, dst_ref=dg, send_sem=fwd_send.at[p], recv_sem=fwd_recv.at[p],
                                         device_id=to, device_id_type=MESH).wait_recv()
            for k in range(4):
                @pl.when(k == k_dg)
                def _():
                    to_sibling(p, k).start()
        for p in range(n):
            forward(p).wait_send()
            for k in range(4):
                to_sibling(p, k).wait_send()
                blk = land[p].at[2 * k + 1 - c]
                pltpu.make_async_remote_copy(
                    src_ref=blk, dst_ref=blk, send_sem=send_sems.at[4 * p + k], recv_sem=recv_sems.at[4 * p + k],
                    device_id=(x, y, 1 - c), device_id_type=MESH).wait_recv()

    return pl.pallas_call(
        body, name="gather_pair_share", in_specs=[_HBM] * n, out_specs=[_HBM] * n,
        out_shape=[_sds(g.shape, g.dtype) for g in gathered],
        input_output_aliases={p: p for p in range(n)},
        scratch_shapes=[pltpu.SemaphoreType.DMA((4 * n,)), pltpu.SemaphoreType.DMA((4 * n,)),
                        pltpu.SemaphoreType.DMA((n,)), pltpu.SemaphoreType.DMA((n,))],
    )(*gathered)


def _place_slot(arrs, slot, n_slots, name, out_dtype=None):
    n = len(arrs)

    def body(s_ref, *refs):
        for p in range(n):
            refs[n + p][...] = refs[p][...].astype(refs[n + p].dtype)

    def spec(a):
        tail = a.shape[-2:]
        return pl.BlockSpec((None,) + tail, lambda i, s_ref: (s_ref[0], 0, 0))

    in_specs = [spec(a) if a.ndim == 3 else pl.BlockSpec(a.shape, lambda i, s_ref: (0, 0)) for a in arrs]
    return pl.pallas_call(
        body, name=name,
        grid_spec=pltpu.PrefetchScalarGridSpec(num_scalar_prefetch=1, grid=(1,), in_specs=in_specs,
                                               out_specs=[spec(a) for a in arrs]),
        out_shape=[_sds((n_slots,) + a.shape[-2:], out_dtype or a.dtype) for a in arrs],
        compiler_params=_cp(("arbitrary",)),
    )(slot, *arrs)


def _cols_from_blocks(w8, name):
    _, R, C = w8.shape

    def body(w_ref, o_ref):
        for e in range(8):
            o_ref[:, C * e:C * (e + 1)] = w_ref[e]

    return pl.pallas_call(body, name=name, out_shape=_sds((R, 8 * C), w8.dtype), compiler_params=_cp())(w8)


def _all_gather_small(srcs, name):
    n = len(srcs)
    vmem = pl.BlockSpec(memory_space=pltpu.VMEM)

    def body(*refs):
        src, out = refs[:n], refs[n:2 * n]
        send_sems, recv_sems = refs[2 * n:]
        x, y, c = _my_place()
        me = 4 * x + 2 * y + c
        flip = lambda v, f: 1 - v if f else v
        peers = [(flip(x, fx), flip(y, fy), flip(c, fc)) for fx in (0, 1) for fy in (0, 1) for fc in (0, 1)][1:]
        cps = []
        for p in range(n):
            out[p][me] = src[p][...]
            for k, peer in enumerate(peers):
                cps.append(pltpu.make_async_remote_copy(
                    src_ref=src[p], dst_ref=out[p].at[me], send_sem=send_sems.at[7 * p + k],
                    recv_sem=recv_sems.at[7 * p + k], device_id=peer, device_id_type=MESH))
        for cp in cps:
            cp.start()
        for cp in cps:
            cp.wait()

    return pl.pallas_call(
        body, name=name, in_specs=[vmem] * n, out_specs=[vmem] * n,
        out_shape=[_sds((8,) + a.shape, a.dtype) for a in srcs],
        scratch_shapes=[pltpu.SemaphoreType.DMA((7 * n,)), pltpu.SemaphoreType.DMA((7 * n,))],
    )(*srcs)


def _adaln_exchange(c, w_ada, b_part):
    vmem = pl.BlockSpec(memory_space=pltpu.VMEM)

    def body(c_ref, w_ref, b_ref, call_ref, mod_ref, part_ref, send_sems, recv_sems):
        x, y, c = _my_place()
        me = 4 * x + 2 * y + c
        flip = lambda v, f: 1 - v if f else v
        peers = [(flip(x, fx), flip(y, fy), flip(c, fc)) for fx in (0, 1) for fy in (0, 1) for fc in (0, 1)][1:]
        call_ref[me] = c_ref[...]
        first = [pltpu.make_async_remote_copy(src_ref=c_ref, dst_ref=call_ref.at[me], send_sem=send_sems.at[k],
                                              recv_sem=recv_sems.at[k], device_id=peer, device_id_type=MESH)
                 for k, peer in enumerate(peers)]
        for cp in first:
            cp.start()
        for cp in first:
            cp.wait()
        c_all = jnp.concatenate([call_ref[j] for j in range(8)], axis=0)
        part_ref[...] = _dot(c_all.astype(CDT), w_ref[...].astype(CDT), NN) + b_ref[...]
        mod_ref[me] = part_ref[pl.ds(me, 1), :]
        second = [pltpu.make_async_remote_copy(
            src_ref=part_ref.at[pl.ds(4 * px + 2 * py + pc, 1)], dst_ref=mod_ref.at[me], send_sem=send_sems.at[7 + k],
            recv_sem=recv_sems.at[7 + k], device_id=(px, py, pc), device_id_type=MESH)
            for k, (px, py, pc) in enumerate(peers)]
        for cp in second:
            cp.start()
        for cp in second:
            cp.wait()

    return pl.pallas_call(
        body, name="adaln_exchange", in_specs=[vmem] * 3, out_specs=[vmem] * 2,
        out_shape=[_sds((8, 1, D), F32), _sds((8, 1, w_ada.shape[1]), F32)],
        scratch_shapes=[pltpu.VMEM((8, w_ada.shape[1]), F32), pltpu.SemaphoreType.DMA((14,)),
                        pltpu.SemaphoreType.DMA((14,))],
        compiler_params=_cp(),
    )(c, w_ada, b_part)


def _pair_add(dw, recv, axis, width, cidx, name):
    if axis == 0:
        tr = min(width, 256)
        C = dw.shape[1]
        grid = (4, width // tr)
        nb = width // tr
        own = pl.BlockSpec((tr, C), lambda q, i, c_ref: ((2 * q + c_ref[0]) * nb + i, 0))
        oth = pl.BlockSpec((None, tr, C), lambda q, i, c_ref: (q, i, 0))
    else:
        R = dw.shape[0]
        tr = min(R, 256)
        grid = (4, R // tr)
        own = pl.BlockSpec((tr, width), lambda q, i, c_ref: (i, 2 * q + c_ref[0]))
        oth = pl.BlockSpec((None, tr, width), lambda q, i, c_ref: (q, i, 0))

    def body(c_ref, a_ref, b_ref, o_ref):
        o_ref[...] = (a_ref[...] + b_ref[...]).astype(o_ref.dtype)

    return pl.pallas_call(
        body, name=name,
        grid_spec=pltpu.PrefetchScalarGridSpec(num_scalar_prefetch=1, grid=grid, in_specs=[own, oth], out_specs=oth),
        out_shape=_sds(recv.shape, CDT), compiler_params=_cp(("parallel", "parallel")),
    )(cidx, dw, recv)


def _add_pairs(xs, ys, name):
    n = len(xs)

    def body(*refs):
        for p in range(n):
            refs[2 * n + p][...] = refs[p][...] + refs[n + p][...]

    return pl.pallas_call(body, name=name, out_shape=[_sds(a.shape, F32) for a in xs])(*xs, *ys)


def _adam_math(w, g, m, v):
    m = ADAM_B1 * m + (1.0 - ADAM_B1) * g
    v = ADAM_B2 * v + (1.0 - ADAM_B2) * (g * g)
    m_hat = m / (1.0 - ADAM_B1 ** ADAM_STEP)
    v_hat = v / (1.0 - ADAM_B2 ** ADAM_STEP)
    delta = -ADAM_LR * (m_hat / (jnp.sqrt(v_hat) + ADAM_EPS) + ADAM_WD * w)
    return delta, m, v


def _adam(gparts, w, m, v, name):
    P, R, C = gparts.shape
    tr = R if R <= 256 else 256

    def body(g_ref, w_ref, m_ref, v_ref, go_ref, d_ref, mo_ref, vo_ref):
        g = g_ref[0].astype(F32)
        for i in range(1, P):
            g = g + g_ref[i].astype(F32)
        d, mn, vn = _adam_math(w_ref[...], g, m_ref[...], v_ref[...])
        go_ref[...] = g
        d_ref[...] = d
        mo_ref[...] = mn
        vo_ref[...] = vn

    t = pl.BlockSpec((tr, C), lambda i: (i, 0))
    return pl.pallas_call(
        body, name=name, grid=(R // tr,),
        in_specs=[pl.BlockSpec((P, tr, C), lambda i: (0, i, 0)), t, t, t], out_specs=[t, t, t, t],
        out_shape=[_sds((R, C), F32)] * 4, compiler_params=_cp(("parallel",)),
    )(gparts, w, m, v)


def _sum_parts(gparts, name):
    P, R, C = gparts.shape

    def body(g_ref, o_ref):
        g = g_ref[0]
        for i in range(1, P):
            g = g + g_ref[i]
        o_ref[...] = g

    return pl.pallas_call(body, name=name, out_shape=_sds((R, C), F32))(gparts)


def _adam_w_ada(c_t, dm, w, m, v):
    R, C = w.shape

    def body(c_ref, dm_ref, w_ref, m_ref, v_ref, go_ref, d_ref, mo_ref, vo_ref):
        g = c_ref[:, 0:1] * dm_ref[0:1, :]
        for b in range(1, 8):
            g = g + c_ref[:, b:b + 1] * dm_ref[b:b + 1, :]
        d, mn, vn = _adam_math(w_ref[...], g, m_ref[...], v_ref[...])
        go_ref[...] = g
        d_ref[...] = d
        mo_ref[...] = mn
        vo_ref[...] = vn

    return pl.pallas_call(body, name="adam_w_ada", out_shape=[_sds((R, C), F32)] * 4,
                          compiler_params=_cp())(c_t, dm, w, m, v)


def kernel(x, c, norm_g, w_ada, b_ada, w_in, pool_w, pool_scale, w_attn_br, w_pool_br, w_out, rel_bias, final_g, loss_target, m_norm_g, m_w_ada, m_b_ada, m_w_in, m_pool_w, m_pool_scale, m_w_attn_br, m_w_pool_br, m_w_out, m_rel_bias, m_final_g, v_norm_g, v_w_ada, v_b_ada, v_w_in, v_pool_w, v_pool_scale, v_w_attn_br, v_w_pool_br, v_w_out, v_rel_bias, v_final_g):
    S = x.shape[1]
    px, py, pc = _my_place()
    me = 4 * px + 2 * py + pc
    cidx = jnp.reshape(pc, (1,)).astype(jnp.int32)

    layout = [(1, 1024), (1, 128), (1, 128), (0, 128)]
    final_g2 = final_g.reshape(1, D)
    me1 = jnp.reshape(me, (1,)).astype(jnp.int32)

    b_part = lax.dynamic_slice(b_ada, (0, me * 384), (1, 384))
    c_all, mod_all = _adaln_exchange(c, w_ada[0], b_part)
    c_all = c_all.reshape(8, D)
    mod = mod_all.reshape(1, 3 * D)
    shift, scale, gate = mod[:, :D], mod[:, D:2 * D], mod[:, 2 * D:]

    gplan = _gather_chips_plan([(None, 1)] * 4)
    lands = _place_slot([w_in[0], w_attn_br[0], w_pool_br[0], w_out[0]], me1, 8, "place_own_shards", CDT)
    g_sems, _, lands, tok = _split_start("gather_chips_start", [], lands, 8, gplan, dep=mod_all)
    st = _local_pre(x[0], shift, scale, norm_g, rel_bias, dep=tok)
    _, lands = _split_wait("gather_chips_wait", g_sems, [], lands, gplan,
                           [st["ht"]] + [b for pair in st["biases"] for b in pair])
    w_in_full, wab8, wpb8, wout8 = _pair_share(lands)
    wab, wpb = _cols_from_blocks(wab8, "w_attn_br_cols"), _cols_from_blocks(wpb8, "w_pool_br_cols")
    wout = wout8.reshape(D, D)

    st = _local_grads(st, loss_target[0], gate, w_in_full, pool_w[0], pool_scale, wab, wpb, wout, final_g2)

    early = [st["dpw"].reshape(AW, 128), st["dps"], st["gfg"], jnp.broadcast_to(st["loss"], (1, 128))]
    dws = [st["dw_in"], st["dw_ab"], st["dw_pb"], st["dw_out"]]
    pplan = _pair_plan(layout)
    pair_shapes = [(4, w, a.shape[1]) if ax == 0 else (4, a.shape[0], w) for a, (ax, w) in zip(dws, layout)]
    p_sems, srcs, recv, tok = _split_start(
        "reduce_pair_start", dws + early, [lax.empty(s, F32) for s in pair_shapes + [a.shape for a in early]], 20,
        pplan)
    d_rel_bias = _local_bias_grads(st, dep=tok)
    srcs, recv = _split_wait("reduce_pair_wait", p_sems, srcs, recv, pplan, d_rel_bias)
    parts = [_pair_add(dw, r, ax, w, cidx, f"pair_add{i}")
             for i, (dw, r, (ax, w)) in enumerate(zip(srcs[:4], recv[:4], layout))]
    parts += _add_pairs(srcs[4:], recv[4:], "pair_add_smalls")

    slot1 = jnp.reshape(2 * px + py, (1,)).astype(jnp.int32)
    cplan = _chips_plan(4)
    c_sems, parts, lands, tok = _split_start("reduce_chips_start", parts,
                                             _place_slot(parts, slot1, 4, "place_own_partials"), 24, cplan)
    grad_x, dmod, d_norm_g = _local_input_grads(st, w_in_full, norm_g, scale, dep=tok)

    dmod_all, ng_all, rb_all = _all_gather_small([dmod, d_norm_g, d_rel_bias], "gather_late")
    o_norm_g = _adam(ng_all, norm_g, m_norm_g, v_norm_g, "adam_norm_g")
    o_rel_bias = _adam(rb_all, rel_bias, m_rel_bias, v_rel_bias, "adam_rel_bias")
    o_b_ada = _adam(dmod_all, b_ada, m_b_ada, v_b_ada, "adam_b_ada")
    dm_mine = lax.dynamic_slice(dmod_all.reshape(8, 3 * D), (0, me * 384), (8, 384))
    o_w_ada = [o[None] for o in _adam_w_ada(c_all.T, dm_mine, w_ada[0], m_w_ada[0], v_w_ada[0])]

    _, (g_in, g_ab, g_pb, g_out, g_pw, g_ps, g_fg, g_loss) = _split_wait(
        "reduce_chips_wait", c_sems, parts, lands, cplan, [o_w_ada[0], o_b_ada[0], o_norm_g[0], o_rel_bias[0]])
    o_w_in = [o[None] for o in _adam(g_in, w_in[0], m_w_in[0], v_w_in[0], "adam_w_in")]
    o_w_ab = [o[None] for o in _adam(g_ab, w_attn_br[0], m_w_attn_br[0], v_w_attn_br[0], "adam_w_attn_br")]
    o_w_pb = [o[None] for o in _adam(g_pb, w_pool_br[0], m_w_pool_br[0], v_w_pool_br[0], "adam_w_pool_br")]
    o_w_out = [o[None] for o in _adam(g_out, w_out[0], m_w_out[0], v_w_out[0], "adam_w_out")]
    o_pool_w = [o.reshape(pool_w.shape) for o in _adam(g_pw, pool_w.reshape(AW, 128), m_pool_w.reshape(AW, 128),
                                                       v_pool_w.reshape(AW, 128), "adam_pool_w")]
    o_pool_scale = _adam(g_ps, pool_scale, m_pool_scale, v_pool_scale, "adam_pool_scale")
    o_final_g = [o.reshape(D) for o in _adam(g_fg, final_g2, m_final_g.reshape(1, D), v_final_g.reshape(1, D),
                                             "adam_final_g")]
    loss = _sum_parts(g_loss, "sum_loss")[0, 0]

    per_w = [o_norm_g, o_w_ada, o_b_ada, o_w_in, o_pool_w, o_pool_scale, o_w_ab, o_w_pb, o_w_out, o_rel_bias, o_final_g]
    outs = [loss, grad_x[None]]
    for k in range(4):
        outs += [o[k] for o in per_w]
    return tuple(outs)
```

```python
import functools
import math

import numpy as np
import jax
import jax.numpy as jnp
from jax import lax
from jax.experimental import pallas as pl
from jax.experimental.pallas import tpu as pltpu

F32 = jnp.float32
CDT = jnp.bfloat16
D = 1024
HD = 64
NH = 8
AW = 512
QKV_W = 4608
REST_W = 3584
R16 = 16
EPS = 1e-6
NEG = -1e30
N_BACK = 128
ADAM_LR, ADAM_B1, ADAM_B2, ADAM_EPS, ADAM_WD, ADAM_STEP = 0.001, 0.9, 0.999, 1e-08, 0.01, 10
MESH = pl.DeviceIdType.MESH
VMEM_MB = 1024 * 1024


def _cp(sem=None, vmem=48, **kw):
    if sem is not None:
        kw["dimension_semantics"] = sem
    return pltpu.CompilerParams(vmem_limit_bytes=vmem * VMEM_MB, **kw)


def _sds(shape, dtype):
    return jax.ShapeDtypeStruct(tuple(shape), dtype)


def _rest_blk(jj):
    return jnp.where(jj == 0, 9, jnp.where(jj == 6, 10, jj + 10))


def _sigmoid(z):
    return 0.5 * jnp.tanh(0.5 * z) + 0.5


def _dot(a, b, dims):
    return lax.dot_general(a, b, (dims, ((), ())), preferred_element_type=F32)


NN = ((1,), (0,))
NT = ((1,), (1,))
TN = ((0,), (0,))


def _ld(ref):
    if len(ref.shape) == 2:
        return ref[...]
    return jnp.concatenate([ref[a] for a in range(ref.shape[0])], axis=0)


def _st(ref, val, cols=None):
    val = val.astype(ref.dtype)
    if len(ref.shape) == 2:
        if cols is None:
            ref[...] = val
        else:
            ref[:, cols] = val
        return
    rows = ref.shape[1]
    for a in range(ref.shape[0]):
        if cols is None:
            ref[a] = val[a * rows:(a + 1) * rows]
        else:
            ref[a, :, cols] = val[a * rows:(a + 1) * rows]


REST_COLS = ((9, 10), (11, 16), (10, 11))
_HBM = pl.BlockSpec(memory_space=pltpu.HBM)


def _resident(shape):
    return pl.BlockSpec(shape, lambda *ids: (0,) * len(shape), pipeline_mode=pl.Buffered(1))


def _w512(w_ref, j):
    return w_ref[j // 2, :, 512 * (j % 2):512 * (j % 2 + 1)]


def _proj_own(h, w_own, dep=None):
    S = h.shape[0]
    tm = 512
    dep_specs, dep_args = _after(dep)

    def body(a_ref, w_ref, *rest):
        rest[-1][...] = _dot(a_ref[...], w_ref[...].astype(CDT), NN)

    return pl.pallas_call(
        body, name="proj_own", grid=(S // tm,),
        in_specs=[pl.BlockSpec((tm, D), lambda i: (i, 0)), pl.BlockSpec((D, 1024), lambda i: (0, 0))] + dep_specs,
        out_specs=pl.BlockSpec((tm, 1024), lambda i: (i, 0)),
        out_shape=_sds((S, 1024), F32), compiler_params=_cp(("parallel",)),
    )(h, w_own, *dep_args)


def _proj(h, w, own, me):
    S = h.shape[0]
    tm = 512

    def body(me_ref, a_ref, w_ref, own_ref, q_ref, r_ref):
        a = a_ref[...]

        def block(j, store):
            mine = me_ref[0] == j // 2
            pl.when(mine)(lambda: store(own_ref[:, 512 * (j % 2):512 * (j % 2 + 1)]))
            pl.when(jnp.logical_not(mine))(lambda: store(_dot(a, _w512(w_ref, j), NN)))

        def to_qkv(j):
            def store(val):
                q_ref[:, 512 * j:512 * (j + 1)] = val.astype(q_ref.dtype)
            return store

        def to_rest(off):
            def store(val):
                r_ref[:, off:off + 512] = val
            return store

        for j in range(9):
            block(j, to_qkv(j))
        off = 0
        for lo, hi in REST_COLS:
            for j in range(lo, hi):
                block(j, to_rest(off))
                off += 512

    return pl.pallas_call(
        body, name="proj",
        grid_spec=pltpu.PrefetchScalarGridSpec(
            num_scalar_prefetch=1, grid=(S // tm,),
            in_specs=[pl.BlockSpec((tm, D), lambda i, m: (i, 0)),
                      pl.BlockSpec((8, D, 1024), lambda i, m: (0, 0, 0), pipeline_mode=pl.Buffered(1)),
                      pl.BlockSpec((tm, 1024), lambda i, m: (i, 0))],
            out_specs=[pl.BlockSpec((tm, QKV_W), lambda i, m: (i, 0)), pl.BlockSpec((tm, REST_W), lambda i, m: (i, 0))]),
        out_shape=[_sds((S, QKV_W), CDT), _sds((S, REST_W), F32)],
        compiler_params=_cp(("parallel",), vmem=56),
    )(me, h, w, own)


def _after(dep):
    return ([], []) if dep is None else ([pl.BlockSpec(memory_space=pl.ANY)], [dep])


def _dh(dqkv, drest, w, dep=None):
    S = dqkv.shape[0]
    tm = 512
    dep_specs, dep_args = _after(dep)

    def body(a_ref, b_ref, w_ref, *rest):
        o_ref = rest[-1]
        acc = _dot(a_ref[:, 0:512], _w512(w_ref, 0), NT)
        for j in range(1, 9):
            acc = acc + _dot(a_ref[:, 512 * j:512 * (j + 1)], _w512(w_ref, j), NT)
        off = 0
        for lo, hi in REST_COLS:
            for j in range(lo, hi):
                acc = acc + _dot(b_ref[:, off:off + 512], _w512(w_ref, j), NT)
                off += 512
        o_ref[...] = acc

    return pl.pallas_call(
        body, name="dh", grid=(S // tm,),
        in_specs=[pl.BlockSpec((tm, QKV_W), lambda i: (i, 0)), pl.BlockSpec((tm, REST_W), lambda i: (i, 0)),
                  _resident((8, D, 1024))] + dep_specs,
        out_specs=pl.BlockSpec((tm, D), lambda i: (i, 0)),
        out_shape=_sds((S, D), F32),
        compiler_params=_cp(("parallel",), vmem=56),
    )(dqkv, drest, w, *dep_args)


def _mm_tn(at, b, *, n_total, out_blk, name, prev=None, tn=512):
    R, S = at.shape
    nb = b.shape[1] // tn

    def body(*refs):
        refs[-1][...] = _dot(refs[0][...], refs[1][...], NN)

    in_specs = [_resident((R, S)), pl.BlockSpec((S, tn), lambda j: (0, j))]
    args = [at, b]
    aliases = {}
    if prev is not None:
        in_specs.append(pl.BlockSpec(memory_space=pl.ANY))
        args.append(prev)
        aliases = {2: 0}
    return pl.pallas_call(
        body, name=name, grid=(nb,), in_specs=in_specs,
        out_specs=pl.BlockSpec((R, tn), lambda j: (0, out_blk(j))),
        out_shape=_sds((R, n_total), F32),
        input_output_aliases=aliases,
        compiler_params=_cp(("parallel",)),
    )(*args)


def _tcast(a, name):
    M, C = a.shape
    tm = 512

    def body(a_ref, o_ref):
        o_ref[...] = a_ref[...].astype(F32).T.astype(o_ref.dtype)

    return pl.pallas_call(
        body, name=name, grid=(M // tm,),
        in_specs=[pl.BlockSpec((tm, C), lambda i: (i, 0))],
        out_specs=pl.BlockSpec((C, tm), lambda i: (0, i)),
        out_shape=_sds((C, M), CDT), compiler_params=_cp(("parallel",)),
    )(a)


TOK = R16 * R16


def _swap16():
    i = lax.broadcasted_iota(jnp.int32, (TOK, TOK), 0)
    j = lax.broadcasted_iota(jnp.int32, (TOK, TOK), 1)
    return jnp.where(j == jnp.bitwise_and(i, 15) * 16 + jnp.right_shift(i, 4), 1.0, 0.0).astype(CDT)


def _norm_mod(x, norm_g, shift, scale, dep=None):
    S = x.shape[0]
    dep_specs, dep_args = _after(dep)

    def body(x_ref, g_ref, sh_ref, sc_ref, *rest):
        xv = x_ref[...]
        xn = xv * lax.rsqrt(jnp.mean(xv * xv, axis=-1, keepdims=True) + EPS)
        h = (xn * (g_ref[...] * (1.0 + sc_ref[...])) + sh_ref[...]).astype(CDT)
        _st(rest[-1], _dot(_swap16(), h, NN))

    vec = pl.BlockSpec((1, D), lambda i: (0, 0))
    return pl.pallas_call(
        body, name="norm_mod", grid=(S // TOK,),
        in_specs=[pl.BlockSpec((TOK, D), lambda i: (i, 0)), vec, vec, vec] + dep_specs,
        out_specs=pl.BlockSpec((R16, R16, D), lambda i: (0, i, 0)),
        out_shape=_sds((R16, S // R16, D), CDT), compiler_params=_cp(("parallel",)),
    )(x, norm_g, shift, scale, *dep_args)


def _t5_bucket_np(n):
    nf = np.maximum(n, 1).astype(np.float32)
    large = 16 + (np.log(nf / np.float32(16)) / np.float32(math.log(2048 / 16)) * np.float32(16)).astype(np.int32)
    large = np.minimum(large, 31)
    return np.where(n < 16, n, large).astype(np.int32)


class _Geom:
    def __init__(self, gi, LS):
        self.gi, self.LS = gi, LS
        self.dil = (1, 4, 16)[gi]
        if gi == 0:
            self.lead, self.rows, self.n_tiles, self.has_prev = (R16,), 16, LS // 16, True
            a = np.arange(R16)[:, None]
            ll = np.arange(16)[None, :]
            self.pos = (16 * ll + a).reshape(-1)
        elif gi == 1:
            self.lead, self.rows, self.n_tiles, self.has_prev = (4,), 64, LS // 64, True
            a = np.arange(4)[:, None]
            ll = np.arange(64)[None, :]
            self.pos = (4 * ll + a).reshape(-1)
        else:
            self.lead, self.rows, self.n_tiles, self.has_prev = (), LS, 1, False
            self.pos = np.arange(LS)
        self.T = self.pos.shape[0]
        self.Tk = 2 * self.T if self.has_prev else self.T
        self.outer = {0: (), 1: (4,), 2: (R16,)}[gi]

    def view(self, arr):
        S, C = arr.shape
        if self.gi == 1:
            return arr.reshape(4, 4, self.LS, C)
        return arr.reshape(R16, self.LS, C)

    def view_shape(self, C):
        return (4, 4, self.LS, C) if self.gi == 1 else (R16, self.LS, C)

    def unview(self, arr):
        return arr.reshape(R16 * self.LS, arr.shape[-1])

    def spec(self, W, cblk, tile):
        if self.gi == 0:
            return pl.BlockSpec((R16, 16, W), lambda n: (0, tile(n), cblk))
        if self.gi == 1:
            return pl.BlockSpec((4, None, 64, W), lambda r, n: (0, r, tile(n), cblk))
        return pl.BlockSpec((None, self.LS, W), lambda r: (r, 0, cblk))

    def halves(self):
        R = self.rows
        idx = np.arange(self.T).reshape(-1, R)
        return idx[:, :R // 2].reshape(-1), idx[:, R // 2:].reshape(-1)

    def bucket_index(self, sub):
        lo, hi = self.halves()
        pos = self.pos
        if sub == 0:
            pq = pos[lo]
            pk = np.concatenate([pos[hi] - self.T, pos[lo]]) if self.has_prev else np.concatenate([pos[lo], pos[hi]])
        else:
            pq = pos[hi]
            pk = np.concatenate([pos[lo], pos[hi]])
        dist = pq[:, None] - pk[None, :]
        ok = (dist >= 0) & (dist <= N_BACK)
        bucket = _t5_bucket_np(np.clip(dist, 0, N_BACK) * self.dil)
        return np.where(ok, bucket, -1).astype(np.int32)


def _bias_table(idx, buckets, rel_bias, gi, sub, dep=None):
    T, Tk = idx.shape
    tr = 32
    dep_specs, dep_args = _after(dep)

    def body(idx_ref, rb_ref, *rest):
        h = pl.program_id(0)
        ix = idx_ref[...]
        acc = jnp.full(ix.shape, NEG, F32)
        for b in buckets:
            acc = jnp.where(ix == b, rb_ref[b, gi * NH + h], acc)
        rest[-1][...] = acc

    return pl.pallas_call(
        body, name=f"bias_table{gi}{'ab'[sub]}", grid=(NH, T // tr),
        in_specs=[pl.BlockSpec((tr, Tk), lambda h, i: (i, 0)), pl.BlockSpec(memory_space=pltpu.SMEM)] + dep_specs,
        out_specs=pl.BlockSpec((None, tr, Tk), lambda h, i: (h, i, 0)),
        out_shape=_sds((NH, T, Tk), F32), compiler_params=_cp(("parallel", "parallel")),
    )(idx, rel_bias, *dep_args)


def _bias_grad(dsum, idx, buckets, gi, sub, dep=None):
    _, T, Tk = dsum.shape
    dep_specs, dep_args = _after(dep)

    def body(ds_ref, idx_ref, *rest):
        o_ref, r_ref = rest[-2], rest[-1]
        r_ref[...] = jnp.zeros_like(r_ref)
        ix, ds = idx_ref[...], ds_ref[...]
        for b in buckets:
            r_ref[b:b + 1, :] = jnp.sum(jnp.where(ix == b, ds, 0.0), axis=0, keepdims=True)
        o_ref[...] = jnp.broadcast_to(jnp.sum(r_ref[...], axis=1, keepdims=True), (32, 128))

    out = pl.pallas_call(
        body, name=f"bias_grad{gi}{'ab'[sub]}", grid=(NH,),
        in_specs=[pl.BlockSpec((None, T, Tk), lambda h: (h, 0, 0)), pl.BlockSpec((T, Tk), lambda h: (0, 0))] + dep_specs,
        out_specs=pl.BlockSpec((None, 32, 128), lambda h: (h, 0, 0)),
        out_shape=_sds((NH, 32, 128), F32), scratch_shapes=[pltpu.VMEM((32, Tk), F32)],
        compiler_params=_cp(("parallel",)),
    )(dsum, idx, *dep_args)
    return out[:, :, 0].T


def _prev_mask(n, T, Tk):
    col = lax.broadcasted_iota(jnp.int32, (1, Tk), 1)
    return jnp.where((col < T) & (n == 0), NEG, 0.0).astype(F32)


def _split_rows(v, R):
    A, half = v.shape[0] // R, R // 2
    via_f32 = v.dtype != F32 and half % 16 != 0
    w = v.astype(F32) if via_f32 else v
    lo = jnp.concatenate([w[a * R:a * R + half] for a in range(A)], axis=0)
    hi = jnp.concatenate([w[a * R + half:(a + 1) * R] for a in range(A)], axis=0)
    return (lo.astype(v.dtype), hi.astype(v.dtype)) if via_f32 else (lo, hi)


def _merge_rows(lo, hi, R):
    half = R // 2
    A = lo.shape[0] // half
    return jnp.concatenate([x[a * half:(a + 1) * half] for a in range(A) for x in (lo, hi)], axis=0)


def _attn_fwd(qkv, biases, geom):
    S = qkv.shape[0]
    gi, T, R, has_prev = geom.gi, geom.T, geom.rows, geom.has_prev
    H2 = T // 2
    qv = geom.view(qkv)
    cq, ck, cv = 3 * gi, 3 * gi + 1, 3 * gi + 2

    def body(*refs):
        if has_prev:
            q_ref, kp_ref, kc_ref, vp_ref, vc_ref, ba_ref, bb_ref, o_ref, l_ref = refs
        else:
            q_ref, kc_ref, vc_ref, ba_ref, bb_ref, o_ref, l_ref = refs
        q_lo, q_hi = _split_rows(_ld(q_ref), R)
        kc_lo, kc_hi = _split_rows(_ld(kc_ref), R)
        vc_lo, vc_hi = _split_rows(_ld(vc_ref), R)
        k_b = jnp.concatenate([kc_lo, kc_hi], axis=0)
        v_b = jnp.concatenate([vc_lo, vc_hi], axis=0)
        if has_prev:
            n = pl.program_id(len(geom.outer))
            k_a = jnp.concatenate([_split_rows(_ld(kp_ref), R)[1], kc_lo], axis=0)
            v_a = jnp.concatenate([_split_rows(_ld(vp_ref), R)[1], vc_lo], axis=0)
            pm = _prev_mask(n, H2, T)
        else:
            k_a, v_a, pm = k_b, v_b, None
        low = lax.broadcasted_iota(jnp.int32, (1, 2 * HD), 1) < HD
        for j in range(NH // 2):
            cols = slice(2 * HD * j, 2 * HD * (j + 1))
            o_half, l_half = [], []
            for q, k, v, b_ref, mask in ((q_lo, k_a, v_a, ba_ref, pm), (q_hi, k_b, v_b, bb_ref, None)):
                q2 = q[:, cols].astype(F32) * 0.125
                k2, v2 = k[:, cols], v[:, cols]
                qs = jnp.concatenate([jnp.where(low, q2, 0.0), jnp.where(low, 0.0, q2)], axis=0).astype(q.dtype)
                s = _dot(qs, k2, NT) + b_ref[j]
                if mask is not None:
                    s = s + mask
                m = jnp.max(s, axis=1, keepdims=True)
                p = jnp.exp(s - m)
                l = jnp.sum(p, axis=1, keepdims=True)
                pv = _dot(p.astype(v.dtype), v2, NN) / l
                lse = m + jnp.log(l)
                o_half.append(jnp.where(low, pv[:H2], pv[H2:]))
                l_half.append(jnp.where(low, lse[:H2], lse[H2:]))
            _st(o_ref, _merge_rows(o_half[0], o_half[1], R), cols)
            _st(l_ref, _merge_rows(l_half[0], l_half[1], R), cols)

    cur = lambda n: n
    prev = lambda n: jnp.maximum(n - 1, 0)
    if has_prev:
        in_specs = [geom.spec(AW, cq, cur), geom.spec(AW, ck, prev), geom.spec(AW, ck, cur),
                    geom.spec(AW, cv, prev), geom.spec(AW, cv, cur)]
        args = [qv] * 5
    else:
        in_specs = [geom.spec(AW, cq, cur), geom.spec(AW, ck, cur), geom.spec(AW, cv, cur)]
        args = [qv] * 3
    nd = len(geom.outer) + (1 if has_prev else 0)
    biases = [b.reshape(NH // 2, 2 * H2, b.shape[2]) for b in biases]
    in_specs += [pl.BlockSpec(b.shape, lambda *ids: (0, 0, 0)) for b in biases]
    grid = geom.outer + ((geom.n_tiles,) if has_prev else ())
    oshape = geom.view_shape(AW)
    o, l = pl.pallas_call(
        body, name=f"attn_fwd{gi}", grid=grid, in_specs=in_specs,
        out_specs=[geom.spec(AW, 0, cur), geom.spec(AW, 0, cur)],
        out_shape=[_sds(oshape, F32), _sds(oshape, F32)],
        compiler_params=_cp(("arbitrary",) * nd),
    )(*args, *biases)
    return geom.unview(o), geom.unview(l)


def _attn_combine(os_, ls_, rest):
    S = rest.shape[0]
    tm = 512

    def body(o0, o1, o2, l0, l1, l2, z_ref, attn_ref, L_ref, ag_ref, agt_ref):
        la, lb, lc = l0[...], l1[...], l2[...]
        m = jnp.maximum(jnp.maximum(la, lb), lc)
        ea, eb, ec = jnp.exp(la - m), jnp.exp(lb - m), jnp.exp(lc - m)
        den = ea + eb + ec
        attn = (ea * o0[...] + eb * o1[...] + ec * o2[...]) / den
        attn_ref[...] = attn
        L_ref[...] = m + jnp.log(den)
        z = z_ref[...]
        ag = attn * (z * _sigmoid(z))
        ag_ref[...] = ag.astype(ag_ref.dtype)
        agt_ref[...] = ag.T.astype(agt_ref.dtype)

    t = pl.BlockSpec((tm, AW), lambda i: (i, 0))
    return pl.pallas_call(
        body, name="attn_combine", grid=(S // tm,),
        in_specs=[t] * 6 + [pl.BlockSpec((tm, AW), lambda i: (i, 0))],
        out_specs=[t, t, t, pl.BlockSpec((AW, tm), lambda i: (0, i))],
        out_shape=[_sds((S, AW), F32), _sds((S, AW), F32), _sds((S, AW), CDT), _sds((AW, S), CDT)],
        compiler_params=_cp(("parallel",)),
    )(*os_, *ls_, rest)


def _pool_counts(r, LS, win):
    l = lax.broadcasted_iota(jnp.int32, (LS, 1), 0)
    return jnp.minimum(16 * l + (r + 1), win).astype(F32)


def _pool_fwd(rest, pool_w, pool_scale):
    S = rest.shape[0]
    LS = S // R16
    r3 = rest.reshape(R16, LS, REST_W)
    PG = 128

    def body(u_ref, z_ref, pw_ref, ps_ref, pt_ref, mp_ref, pg_ref, pgt_ref, e_ref):
        g = pl.program_id(0)
        win = jnp.left_shift(2, g)
        row = lax.broadcasted_iota(jnp.int32, (LS, PG), 0)
        for r in range(R16):
            e_ref[R16 + r] = u_ref[r]
        for r in range(1, R16):
            e_ref[r] = jnp.where(row == 0, 0.0, pltpu.roll(u_ref[r], 1, 0))

        def double(s):
            for i in range(2 * R16 - 1, 2 * s - 1, -1):
                e_ref[i] = e_ref[i] + e_ref[i - s]

        double(1)
        for k in (1, 2, 3):
            pl.when(g >= k)(functools.partial(double, 2 ** k))
        pw = pw_ref[...].astype(CDT)
        for r in range(R16):
            u = u_ref[r]
            pooled = e_ref[R16 + r] / _pool_counts(r, LS, win) - u
            pc = pooled.astype(CDT)
            mp = _dot(pc, pw, NN)
            z = z_ref[r]
            pg = (mp * ps_ref[...]) * (z * _sigmoid(z))
            mp_ref[r] = mp
            pg_ref[r] = pg.astype(pg_ref.dtype)
            pt_ref[:, r * LS:(r + 1) * LS] = pooled.T.astype(pt_ref.dtype)
            pgt_ref[:, r * LS:(r + 1) * LS] = pg.T.astype(pgt_ref.dtype)

    col = lambda off: pl.BlockSpec((R16, LS, PG), lambda g: (0, 0, off + g))
    tsp = pl.BlockSpec((PG, S), lambda g: (g, 0))
    pt, mp, pg, pgt = pl.pallas_call(
        body, name="pool_fwd", grid=(4,),
        in_specs=[col(24), col(4), pl.BlockSpec((None, PG, PG), lambda g: (g, 0, 0)),
                  pl.BlockSpec((1, PG), lambda g: (0, g))],
        out_specs=[tsp, col(0), col(0), tsp],
        out_shape=[_sds((AW, S), CDT), _sds((R16, LS, AW), F32), _sds((R16, LS, AW), CDT), _sds((AW, S), CDT)],
        scratch_shapes=[pltpu.VMEM((2 * R16, LS, PG), F32)],
        compiler_params=_cp(("parallel",)),
    )(r3, r3, pool_w, pool_scale)
    return pt, mp.reshape(S, AW), pg.reshape(S, AW), pgt


def _branch_merge(ag, pg, wab, wpb, rest):
    S = ag.shape[0]
    tm, tn = 512, 512

    def body(ag_ref, pg_ref, wa_ref, wp_ref, ga_ref, gp_ref, m_ref, mt_ref):
        ya = _dot(ag_ref[...], wa_ref[...], NN)
        yp = _dot(pg_ref[...], wp_ref[...], NN)
        mg = _sigmoid(ga_ref[...]) * ya + _sigmoid(gp_ref[...]) * yp
        m_ref[...] = mg.astype(m_ref.dtype)
        mt_ref[...] = mg.T.astype(mt_ref.dtype)

    a = pl.BlockSpec((tm, AW), lambda i, j: (i, 0))
    w = pl.BlockSpec((AW, tn), lambda i, j: (0, j))
    o = pl.BlockSpec((tm, tn), lambda i, j: (i, j))
    return pl.pallas_call(
        body, name="branch_merge", grid=(S // tm, D // tn),
        in_specs=[a, a, w, w, pl.BlockSpec((tm, tn), lambda i, j: (i, 2 + j)),
                  pl.BlockSpec((tm, tn), lambda i, j: (i, 4 + j))],
        out_specs=[o, pl.BlockSpec((tn, tm), lambda i, j: (j, i))],
        out_shape=[_sds((S, D), CDT), _sds((D, S), CDT)],
        compiler_params=_cp(("parallel", "parallel")),
    )(ag, pg, wab, wpb, rest, rest)


def _out_loss(merged, wout, x, target, gate, final_g):
    S = x.shape[0]

    def body(m_ref, w_ref, x_ref, t_ref, gate_ref, fg_ref, dx2_ref, dmo_ref, loss_ref, gfg_ref, dgate_ref):
        i = pl.program_id(0)

        @pl.when(i == 0)
        def _():
            loss_ref[...] = jnp.zeros_like(loss_ref)
            gfg_ref[...] = jnp.zeros_like(gfg_ref)
            dgate_ref[...] = jnp.zeros_like(dgate_ref)

        swap = _swap16()
        mo = _dot(_dot(swap, _ld(m_ref), NN).astype(CDT), w_ref[...], NN)
        gate, fg = gate_ref[...], fg_ref[...]
        x2 = x_ref[...] + gate * mo
        r2 = lax.rsqrt(jnp.mean(x2 * x2, axis=-1, keepdims=True) + EPS)
        xn2 = x2 * r2
        diff = xn2 * fg - t_ref[...]
        loss = jnp.sum(jnp.sum(diff * diff, axis=-1, keepdims=True), axis=0, keepdims=True)
        dy = diff * (1.0 / D)
        dxn2 = dy * fg
        dx2 = r2 * (dxn2 - xn2 * jnp.mean(dxn2 * xn2, axis=-1, keepdims=True))
        dx2_ref[...] = dx2
        _st(dmo_ref, _dot(swap, (dx2 * gate).astype(CDT), NN))
        loss_ref[...] += jnp.broadcast_to(loss * (0.5 / D), loss_ref.shape)
        gfg_ref[...] += jnp.sum(dy * xn2, axis=0, keepdims=True)
        dgate_ref[...] += jnp.sum(dx2 * mo, axis=0, keepdims=True)

    slab = pl.BlockSpec((R16, R16, D), lambda i: (0, i, 0))
    nat = pl.BlockSpec((TOK, D), lambda i: (i, 0))
    vec = pl.BlockSpec((1, D), lambda i: (0, 0))
    return pl.pallas_call(
        body, name="out_loss", grid=(S // TOK,),
        in_specs=[slab, _resident((D, D)), nat, nat, vec, vec],
        out_specs=[nat, slab, pl.BlockSpec((1, 128), lambda i: (0, 0)), vec, vec],
        out_shape=[_sds((S, D), F32), _sds((R16, S // R16, D), CDT), _sds((1, 128), F32), _sds((1, D), F32),
                   _sds((1, D), F32)],
        compiler_params=_cp(("arbitrary",)),
    )(merged, wout, x, target, gate, final_g)


def _bwd_gates(dmo, wout, wab, wpb, rest, ag, pg, attn, mp, pool_scale):
    S = dmo.shape[0]
    tm = 256

    def body(dmo_ref, wo_ref, wa_ref, wp_ref, za_ref, zp_ref, ga0, ga1, gp0, gp1, ag_ref, pg_ref, at_ref, mp_ref,
             ps_ref, dya_ref, dyp_ref, dr_ref, dat_ref, dl_ref, dmp_ref, dps_ref):
        i = pl.program_id(0)

        @pl.when(i == 0)
        def _():
            dps_ref[...] = jnp.zeros_like(dps_ref)

        dm = _dot(dmo_ref[...], wo_ref[...], NT)
        sa = _sigmoid(jnp.concatenate([ga0[...], ga1[...]], axis=1))
        sp = _sigmoid(jnp.concatenate([gp0[...], gp1[...]], axis=1))
        dya = (dm * sa).astype(CDT)
        dyp = (dm * sp).astype(CDT)
        dya_ref[...] = dya
        dyp_ref[...] = dyp
        ya = _dot(ag_ref[...], wa_ref[...], NN)
        yp = _dot(pg_ref[...], wp_ref[...], NN)
        dr_ref[:, 1024:2048] = (dm * ya * sa * (1.0 - sa)).astype(dr_ref.dtype)
        dr_ref[:, 2048:3072] = (dm * yp * sp * (1.0 - sp)).astype(dr_ref.dtype)
        da = _dot(dya, wa_ref[...], NT)
        dp = _dot(dyp, wp_ref[...], NT)
        za = za_ref[...]
        sga = _sigmoid(za)
        attn = at_ref[...]
        dattn = da * (za * sga)
        dr_ref[:, 0:512] = (da * attn * (sga * (1.0 + za * (1.0 - sga)))).astype(dr_ref.dtype)
        dat_ref[...] = dattn.astype(dat_ref.dtype)
        prod = dattn * attn
        for h in range(NH):
            sl = slice(HD * h, HD * h + HD)
            dl_ref[:, sl] = jnp.broadcast_to(jnp.sum(prod[:, sl], axis=1, keepdims=True), (tm, HD))
        zp = zp_ref[...]
        sgp = _sigmoid(zp)
        mpre = mp_ref[...]
        ps = ps_ref[...]
        dmixed = dp * (zp * sgp)
        dr_ref[:, 512:1024] = (dp * (mpre * ps) * (sgp * (1.0 + zp * (1.0 - sgp)))).astype(dr_ref.dtype)
        dps_ref[...] += jnp.sum(dmixed * mpre, axis=0, keepdims=True)
        dmp_ref[...] = (dmixed * ps).astype(dmp_ref.dtype)

    full = lambda shape: pl.BlockSpec(shape, lambda i: (0, 0))
    tD = pl.BlockSpec((tm, D), lambda i: (i, 0))
    tA = pl.BlockSpec((tm, AW), lambda i: (i, 0))
    rb = lambda c: pl.BlockSpec((tm, AW), lambda i: (i, c))
    return pl.pallas_call(
        body, name="bwd_gates", grid=(S // tm,),
        in_specs=[tD, full((D, D)), full((AW, D)), full((AW, D)), rb(0), rb(1), rb(2), rb(3), rb(4), rb(5),
                  tA, tA, tA, tA, full((1, AW))],
        out_specs=[tD, tD, pl.BlockSpec((tm, 3072), lambda i: (i, 0)), tA, tA, tA, full((1, AW))],
        out_shape=[_sds((S, D), CDT), _sds((S, D), CDT), _sds((S, REST_W), CDT), _sds((S, AW), CDT),
                   _sds((S, AW), F32), _sds((S, AW), CDT), _sds((1, AW), F32)],
        compiler_params=_cp(("arbitrary",), vmem=56),
    )(dmo, wout, wab, wpb, rest, rest, rest, rest, rest, rest, ag, pg, attn, mp, pool_scale)


def _pool_bwd(dmp, pt, pool_w, drest):
    S = dmp.shape[0]
    LS = S // R16
    PG = 128
    d3 = dmp.reshape(R16, LS, AW)
    dr3 = drest.reshape(R16, LS, REST_W)

    def body(d_ref, pt_ref, pw_ref, dr_in, du_ref, dpw_ref, g_ref, s_ref):
        del dr_in
        g = pl.program_id(0)
        win = jnp.left_shift(2, g)
        row = lax.broadcasted_iota(jnp.int32, (LS, PG), 0)
        pw = pw_ref[...].astype(CDT)
        dpw = jnp.zeros((PG, PG), F32)
        for r in range(R16):
            dm = d_ref[r]
            dpw = dpw + _dot(pt_ref[:, r * LS:(r + 1) * LS], dm, NN)
            dpooled = _dot(dm, pw, NT)
            g_ref[r] = dpooled
            sc = dpooled / _pool_counts(r, LS, win)
            s_ref[r] = sc
            if r < R16 - 1:
                s_ref[R16 + r] = jnp.where(row == LS - 1, 0.0, pltpu.roll(sc, LS - 1, 0))
        dpw_ref[...] = dpw

        def double(s):
            for i in range(0, 2 * R16 - 2 * s):
                s_ref[i] = s_ref[i] + s_ref[i + s]

        double(1)
        for k in (1, 2, 3):
            pl.when(g >= k)(functools.partial(double, 2 ** k))
        for r in range(R16):
            du_ref[r] = (s_ref[r] - g_ref[r]).astype(du_ref.dtype)

    du, dpw = pl.pallas_call(
        body, name="pool_bwd", grid=(4,),
        in_specs=[pl.BlockSpec((R16, LS, PG), lambda g: (0, 0, g)), pl.BlockSpec((PG, S), lambda g: (g, 0)),
                  pl.BlockSpec((None, PG, PG), lambda g: (g, 0, 0)), pl.BlockSpec(memory_space=pl.ANY)],
        out_specs=[pl.BlockSpec((R16, LS, PG), lambda g: (0, 0, 24 + g)),
                   pl.BlockSpec((None, PG, PG), lambda g: (g, 0, 0))],
        out_shape=[_sds((R16, LS, REST_W), CDT), _sds((4, PG, PG), F32)],
        scratch_shapes=[pltpu.VMEM((R16, LS, PG), F32), pltpu.VMEM((2 * R16, LS, PG), F32)],
        input_output_aliases={3: 0},
        compiler_params=_cp(("parallel",)),
    )(d3, pt, pool_w, dr3)
    return du.reshape(S, REST_W), dpw


def _attn_bwd(qkv, dattn, Lb, dlb, biases, geom, dqkv_prev):
    S = qkv.shape[0]
    gi, T, R, has_prev, N = geom.gi, geom.T, geom.rows, geom.has_prev, geom.n_tiles
    H2 = T // 2
    nd_outer = len(geom.outer)
    qv = geom.view(qkv)
    cq, ck, cv = 3 * gi, 3 * gi + 1, 3 * gi + 2
    W3 = 3 * AW

    def heads(q_ref, kp_ref, kc_ref, vp_ref, vc_ref, do_ref, L_ref, dl_ref, ba_ref, bb_ref, pm, dsa_ref, dsb_ref, emit):
        q_lo, q_hi = _split_rows(_ld(q_ref), R)
        do_lo, do_hi = _split_rows(_ld(do_ref), R)
        L_lo, L_hi = _split_rows(_ld(L_ref), R)
        dl_lo, dl_hi = _split_rows(_ld(dl_ref), R)
        kc_lo, kc_hi = _split_rows(_ld(kc_ref), R)
        vc_lo, vc_hi = _split_rows(_ld(vc_ref), R)
        k_b = jnp.concatenate([kc_lo, kc_hi], axis=0)
        v_b = jnp.concatenate([vc_lo, vc_hi], axis=0)
        if has_prev:
            k_a = jnp.concatenate([_split_rows(_ld(kp_ref), R)[1], kc_lo], axis=0)
            v_a = jnp.concatenate([_split_rows(_ld(vp_ref), R)[1], vc_lo], axis=0)
        else:
            k_a, v_a = k_b, v_b
        low = lax.broadcasted_iota(jnp.int32, (1, 2 * HD), 1) < HD
        for j in range(NH // 2):
            sl = slice(2 * HD * j, 2 * HD * (j + 1))
            res = []
            for q, k, v, do, L, dl, b_ref, mask, ds_ref in (
                    (q_lo, k_a, v_a, do_lo, L_lo, dl_lo, ba_ref, pm, dsa_ref),
                    (q_hi, k_b, v_b, do_hi, L_hi, dl_hi, bb_ref, None, dsb_ref)):
                q2 = q[:, sl].astype(F32) * 0.125
                do2 = do[:, sl].astype(F32)
                k2, v2 = k[:, sl], v[:, sl]
                L2, dl2 = L[:, sl], dl[:, sl]
                qs = jnp.concatenate([jnp.where(low, q2, 0.0), jnp.where(low, 0.0, q2)], axis=0).astype(q.dtype)
                dom = jnp.concatenate([jnp.where(low, do2, 0.0), jnp.where(low, 0.0, do2)], axis=0).astype(q.dtype)
                Lc = jnp.concatenate([L2[:, 0:1], L2[:, HD:HD + 1]], axis=0)
                dlc = jnp.concatenate([dl2[:, 0:1], dl2[:, HD:HD + 1]], axis=0)
                s = _dot(qs, k2, NT) + b_ref[j]
                if mask is not None:
                    s = s + mask
                p = jnp.exp(s - Lc)
                ds = p * (_dot(dom, v2, NT) - dlc)
                ds_ref[j] += ds
                dsc = ds.astype(q.dtype)
                dq_st = _dot(dsc, k2, NN)
                res.append((jnp.where(low, dq_st[:H2], dq_st[H2:]) * 0.125, _dot(dsc, qs, TN),
                            _dot(p.astype(q.dtype), dom, TN)))
            (dq_lo, dk_a, dv_a), (dq_hi, dk_b, dv_b) = res
            dq = _merge_rows(dq_lo, dq_hi, R)
            if has_prev:
                emit(sl, dq, _merge_rows(dk_a[H2:] + dk_b[:H2], dk_b[H2:], R),
                     _merge_rows(dv_a[H2:] + dv_b[:H2], dv_b[H2:], R), dk_a[:H2], dv_a[:H2])
            else:
                emit(sl, dq, _merge_rows(dk_a[:H2] + dk_b[:H2], dk_a[H2:] + dk_b[H2:], R),
                     _merge_rows(dv_a[:H2] + dv_b[:H2], dv_a[H2:] + dv_b[H2:], R), None, None)

    def first_step():
        ok = pl.program_id(nd_outer) == 0 if has_prev else pl.program_id(0) == 0
        for a in range(nd_outer if has_prev else 0):
            ok = ok & (pl.program_id(a) == 0)
        return ok

    if has_prev:
        def body(q_ref, kp_ref, kc_ref, vp_ref, vc_ref, do_ref, L_ref, dl_ref, ba_ref, bb_ref, prev_in, out_ref,
                 dsa_ref, dsb_ref, hold, new, pp):
            del prev_in
            s_id = pl.program_id(nd_outer)

            @pl.when(first_step())
            def _():
                dsa_ref[...] = jnp.zeros_like(dsa_ref)
                dsb_ref[...] = jnp.zeros_like(dsb_ref)

            @pl.when(s_id < N)
            def _():
                def emit(sl, dq, dk, dv, dk_prev, dv_prev):
                    new[:, sl] = dq
                    new[:, slice(AW + sl.start, AW + sl.stop)] = dk
                    new[:, slice(2 * AW + sl.start, 2 * AW + sl.stop)] = dv
                    pp[:, sl] = dk_prev
                    pp[:, slice(AW + sl.start, AW + sl.stop)] = dv_prev

                heads(q_ref, kp_ref, kc_ref, vp_ref, vc_ref, do_ref, L_ref, dl_ref, ba_ref, bb_ref,
                      _prev_mask(s_id, H2, T), dsa_ref, dsb_ref, emit)

            @pl.when((s_id >= 1) & (s_id < N))
            def _():
                _st(out_ref, hold[:, 0:AW], slice(0, AW))
                late = _merge_rows(jnp.zeros((H2, 2 * AW), F32), pp[...], R)
                _st(out_ref, hold[:, AW:W3] + late, slice(AW, W3))

            @pl.when(s_id == N)
            def _():
                _st(out_ref, hold[...])

            @pl.when(s_id < N)
            def _():
                hold[...] = new[...]

        cur = lambda n: jnp.minimum(n, N - 1)
        prev = lambda n: jnp.clip(n - 1, 0, N - 1)
        delayed = lambda n: jnp.maximum(n - 1, 0)
        in_specs = [geom.spec(AW, cq, cur), geom.spec(AW, ck, prev), geom.spec(AW, ck, cur),
                    geom.spec(AW, cv, prev), geom.spec(AW, cv, cur),
                    geom.spec(AW, 0, cur), geom.spec(AW, 0, cur), geom.spec(AW, 0, cur)]
        args = [qv] * 5 + [geom.view(dattn), geom.view(Lb), geom.view(dlb)]
        out_spec = geom.spec(W3, gi, delayed)
        grid = geom.outer + (N + 1,)
        scratch = [pltpu.VMEM((T, W3), F32), pltpu.VMEM((T, W3), F32), pltpu.VMEM((H2, 2 * AW), F32)]
    else:
        def body(q_ref, kc_ref, vc_ref, do_ref, L_ref, dl_ref, ba_ref, bb_ref, prev_in, out_ref, dsa_ref, dsb_ref):
            del prev_in

            @pl.when(first_step())
            def _():
                dsa_ref[...] = jnp.zeros_like(dsa_ref)
                dsb_ref[...] = jnp.zeros_like(dsb_ref)

            def emit(sl, dq, dk, dv, dk_prev, dv_prev):
                _st(out_ref, dq, sl)
                _st(out_ref, dk, slice(AW + sl.start, AW + sl.stop))
                _st(out_ref, dv, slice(2 * AW + sl.start, 2 * AW + sl.stop))

            heads(q_ref, None, kc_ref, None, vc_ref, do_ref, L_ref, dl_ref, ba_ref, bb_ref, None, dsa_ref, dsb_ref,
                  emit)

        cur = lambda n: n
        in_specs = [geom.spec(AW, cq, cur), geom.spec(AW, ck, cur), geom.spec(AW, cv, cur),
                    geom.spec(AW, 0, cur), geom.spec(AW, 0, cur), geom.spec(AW, 0, cur)]
        args = [qv] * 3 + [geom.view(dattn), geom.view(Lb), geom.view(dlb)]
        out_spec = geom.spec(W3, gi, cur)
        grid = geom.outer
        scratch = []

    whole = lambda b: pl.BlockSpec(b.shape, lambda *ids: (0, 0, 0))
    biases = [b.reshape(NH // 2, 2 * H2, b.shape[2]) for b in biases]
    in_specs += [whole(b) for b in biases]
    in_specs.append(pl.BlockSpec(memory_space=pl.ANY))
    if dqkv_prev is None:
        last, aliases = biases[0], {}
    else:
        last, aliases = geom.view(dqkv_prev), {len(in_specs) - 1: 0}
    dq, dsa, dsb = pl.pallas_call(
        body, name=f"attn_bwd{gi}", grid=grid, in_specs=in_specs,
        out_specs=[out_spec] + [whole(b) for b in biases],
        out_shape=[_sds(geom.view_shape(QKV_W), CDT)] + [_sds(b.shape, F32) for b in biases],
        scratch_shapes=scratch,
        input_output_aliases=aliases,
        compiler_params=_cp(("arbitrary",) * len(grid), vmem=56),
    )(*args, *biases, last)
    return geom.unview(dq), [d.reshape(NH, H2, d.shape[2]) for d in (dsa, dsb)]


def _h_bwd(dh, dx2, x, norm_g, scale):
    S = x.shape[0]

    def body(dh_ref, dx2_ref, x_ref, g_ref, sc_ref, gx_ref, dsh_ref, dsc_ref, dng_ref, nat_ref):
        i = pl.program_id(0)

        @pl.when(i == 0)
        def _():
            dsh_ref[...] = jnp.zeros_like(dsh_ref)
            dsc_ref[...] = jnp.zeros_like(dsc_ref)
            dng_ref[...] = jnp.zeros_like(dng_ref)

        for r in range(R16):
            nat_ref[:, r, :] = dh_ref[r]
        dh = jnp.concatenate([nat_ref[l] for l in range(R16)], axis=0)
        g = g_ref[...]
        one_sc = 1.0 + sc_ref[...]
        xv = x_ref[...]
        r1 = lax.rsqrt(jnp.mean(xv * xv, axis=-1, keepdims=True) + EPS)
        xn = xv * r1
        dhx = dh * xn
        dxn = dh * (g * one_sc)
        dx1 = r1 * (dxn - xn * jnp.mean(dxn * xn, axis=-1, keepdims=True))
        gx_ref[...] = dx2_ref[...] + dx1
        dsh_ref[...] += jnp.sum(dh, axis=0, keepdims=True)
        dsc_ref[...] += jnp.sum(dhx * g, axis=0, keepdims=True)
        dng_ref[...] += jnp.sum(dhx * one_sc, axis=0, keepdims=True)

    nat = pl.BlockSpec((TOK, D), lambda i: (i, 0))
    vec = pl.BlockSpec((1, D), lambda i: (0, 0))
    return pl.pallas_call(
        body, name="h_bwd", grid=(S // TOK,),
        in_specs=[pl.BlockSpec((R16, R16, D), lambda i: (0, i, 0)), nat, nat, vec, vec],
        out_specs=[nat, vec, vec, vec],
        out_shape=[_sds((S, D), F32), _sds((1, D), F32), _sds((1, D), F32), _sds((1, D), F32)],
        scratch_shapes=[pltpu.VMEM((R16, R16, D), F32)],
        compiler_params=_cp(("arbitrary",)),
    )(dh, dx2, x, norm_g, scale)


def _local_step(x, target, shift, scale, gate, norm_g, w_in_full, pool_w, pool_scale, wab, wpb, wout, rel_bias,
                final_g):
    st = _local_pre(x, shift, scale, norm_g, rel_bias, w_in_full[0])
    st = _local_grads(st, target, gate, w_in_full, pool_w, pool_scale, wab, wpb, wout, final_g,
                      jnp.zeros((1,), jnp.int32))
    d_rel_bias = _local_bias_grads(st)
    gx, dmod, dng = _local_input_grads(st, w_in_full, norm_g, scale)
    return dict(loss=st["loss"], grad_x=gx, dw_in=st["dw_in"], dw_ab=st["dw_ab"], dw_pb=st["dw_pb"],
                dw_out=st["dw_out"], d_norm_g=dng, d_pool_w=st["dpw"], d_pool_scale=st["dps"],
                d_rel_bias=d_rel_bias, d_final_g=st["gfg"], dmod=dmod)


def _local_pre(x, shift, scale, norm_g, rel_bias, w_own, dep=None):
    S = x.shape[0]
    LS = S // R16
    geoms = [_Geom(gi, LS) for gi in range(3)]
    idx_np = [[g.bucket_index(sub) for sub in range(2)] for g in geoms]
    buckets = [[tuple(int(b) for b in np.unique(ix) if b >= 0) for ix in pair] for pair in idx_np]
    idxs = [[jnp.asarray(ix) for ix in pair] for pair in idx_np]
    biases = [[_bias_table(idxs[gi][sub], buckets[gi][sub], rel_bias, gi, sub, dep) for sub in range(2)]
              for gi in range(3)]
    h = _norm_mod(x, norm_g, shift, scale, dep).reshape(S, D)
    return dict(geoms=geoms, idxs=idxs, buckets=buckets, biases=biases, x=x, h=h, ht=_tcast(h, "h_transpose"),
                own=_proj_own(h, w_own))


def _local_grads(st, target, gate, w_in_full, pool_w, pool_scale, wab, wpb, wout, final_g, me):
    geoms, idxs, biases, x, h, ht = (st[k] for k in ("geoms", "idxs", "biases", "x", "h", "ht"))
    S = h.shape[0]
    LS = S // R16
    qkv, rest = _proj(h, w_in_full, st["own"], me)

    os_, ls_ = [], []
    for gi in range(3):
        o, l = _attn_fwd(qkv, biases[gi], geoms[gi])
        os_.append(o)
        ls_.append(l)
    attn, Lb, ag, agt = _attn_combine(os_, ls_, rest)
    pt, mp, pg, pgt = _pool_fwd(rest, pool_w, pool_scale)
    merged, mt = _branch_merge(ag, pg, wab, wpb, rest)
    dx2, dmo3, loss, gfg, dgate = _out_loss(merged.reshape(R16, LS, D), wout, x, target, gate, final_g)
    dmo = dmo3.reshape(S, D)

    dya, dyp, drest, dattn, dlb, dmp, dps = _bwd_gates(dmo, wout, wab, wpb, rest, ag, pg, attn, mp, pool_scale)
    dw_out = _mm_tn(mt, dmo, n_total=D, out_blk=lambda j: j, name="dw_out")
    dw_ab = _mm_tn(agt, dya, n_total=D, out_blk=lambda j: j, name="dw_attn_br")
    dw_pb = _mm_tn(pgt, dyp, n_total=D, out_blk=lambda j: j, name="dw_pool_br")
    drest, dpw = _pool_bwd(dmp, pt, pool_w, drest)

    dqkv = None
    dsums = []
    for gi in range(3):
        dqkv, dsum = _attn_bwd(qkv, dattn, Lb, dlb, biases[gi], geoms[gi], dqkv)
        dsums.append(dsum)

    dw_in = _mm_tn(ht, dqkv, n_total=8192, out_blk=lambda j: j, name="dw_in_qkv")
    dw_in = _mm_tn(ht, drest, n_total=8192, out_blk=_rest_blk, name="dw_in_rest", prev=dw_in)
    return dict(st, loss=loss[0, 0], dw_in=dw_in, dw_ab=dw_ab, dw_pb=dw_pb, dw_out=dw_out, dpw=dpw, dps=dps, gfg=gfg,
                dgate=dgate, dsums=dsums, dqkv=dqkv, drest=drest, dx2=dx2)


def _local_bias_grads(st, dep=None):
    drb = [_bias_grad(st["dsums"][gi][0], st["idxs"][gi][0], st["buckets"][gi][0], gi, 0, dep if gi == 0 else None) +
           _bias_grad(st["dsums"][gi][1], st["idxs"][gi][1], st["buckets"][gi][1], gi, 1) for gi in range(3)]
    return jnp.concatenate(drb, axis=1)


def _local_input_grads(st, w_in_full, norm_g, scale, dep=None):
    S = st["h"].shape[0]
    LS = S // R16
    dh = _dh(st["dqkv"], st["drest"], w_in_full, dep)
    gx, dsh, dsc, dng = _h_bwd(dh.reshape(R16, LS, D), st["dx2"], st["x"], norm_g, scale)
    return gx, jnp.concatenate([dsh, dsc, st["dgate"]], axis=1), dng


def _my_place():
    return lax.axis_index("x"), lax.axis_index("y"), lax.axis_index("c")


def _block_of(ref, axis, idx, width):
    if axis is None:
        return ref.at[idx]
    start = pl.multiple_of(idx * width, width)
    if axis == 0:
        return ref.at[pl.ds(start, width)]
    return ref.at[:, pl.ds(start, width)]


_SEM = pl.BlockSpec(memory_space=pltpu.SEMAPHORE)
_EFFECT = pltpu.SideEffectType.DATAFLOW_SIDE_EFFECTING


def _split_start(name, srcs, lands, ncopy, plan, dep=None):
    arrs = list(srcs) + list(lands)
    na, ns = len(arrs), len(srcs)
    dep_specs, dep_args = _after(dep)

    def body(*refs):
        send_sems, recv_sems, token = refs[na + len(dep_args)], refs[na + len(dep_args) + 1], refs[-1]
        for k, (s, d, dev) in enumerate(plan(refs[:ns], refs[ns:na])):
            pltpu.make_async_remote_copy(src_ref=s, dst_ref=d, send_sem=send_sems.at[k], recv_sem=recv_sems.at[k],
                                         device_id=dev, device_id_type=MESH).start()
        token[...] = jnp.zeros_like(token)

    outs = pl.pallas_call(
        body, name=name,
        out_shape=(pltpu.SemaphoreType.DMA((ncopy,)), pltpu.SemaphoreType.DMA((ncopy,)),
                   *[pltpu.HBM(a.shape, a.dtype) for a in arrs], _sds((8, 128), F32)),
        in_specs=[_HBM] * na + dep_specs,
        out_specs=(_SEM, _SEM, *[_HBM] * na, pl.BlockSpec(memory_space=pltpu.VMEM)),
        input_output_aliases={i: 2 + i for i in range(na)},
        compiler_params=pltpu.CompilerParams(has_side_effects=_EFFECT),
    )(*[pltpu.with_memory_space_constraint(a, pltpu.HBM) for a in arrs], *dep_args)
    return (outs[0], outs[1]), list(outs[2:2 + ns]), list(outs[2 + ns:2 + na]), outs[-1]


def _split_wait(name, sems, srcs, lands, plan, after):
    arrs = list(srcs) + list(lands)
    na, ns = len(arrs), len(srcs)
    after = list(after) if isinstance(after, (list, tuple)) else [after]

    def body(*refs):
        send_sems, recv_sems = refs[na], refs[na + 1]
        for k, (s, d, dev) in enumerate(plan(refs[:ns], refs[ns:na])):
            cp = pltpu.make_async_remote_copy(src_ref=s, dst_ref=d, send_sem=send_sems.at[k], recv_sem=recv_sems.at[k],
                                              device_id=dev, device_id_type=MESH)
            cp.wait_send()
            cp.wait_recv()

    outs = pl.pallas_call(
        body, name=name, out_shape=[pltpu.HBM(a.shape, a.dtype) for a in arrs],
        in_specs=[_HBM] * na + [_SEM, _SEM] + [pl.BlockSpec(memory_space=pl.ANY)] * len(after),
        out_specs=[_HBM] * na,
        input_output_aliases={i: i for i in range(na)},
        compiler_params=pltpu.CompilerParams(has_side_effects=_EFFECT),
    )(*arrs, sems[0], sems[1], *after)
    return list(outs[:ns]), list(outs[ns:])


def _gather_chips_plan(layout):
    def plan(src, land):
        x, y, c = _my_place()
        me = 4 * x + 2 * y + c
        mine = [_block_of(land[p], layout[p][0], me, layout[p][1]) for p in range(len(land))]
        return [(mine[p], mine[p], (cx, cy, c)) for p in range(len(land)) for cx, cy in [(1 - x, y), (x, 1 - y)]]
    return plan


def _pair_plan(layout):
    def plan(src, land):
        x, y, c = _my_place()
        n = len(layout)
        return [(_block_of(src[p], layout[p][0], 2 * q + (1 - c), layout[p][1]), land[p].at[q], (x, y, 1 - c))
                for p in range(n) for q in range(4)] + [(src[p], land[p], (x, y, 1 - c)) for p in range(n, len(src))]
    return plan


def _chips_plan(n):
    def plan(src, land):
        x, y, c = _my_place()
        chips = [(1 - x, y), (x, 1 - y), (1 - x, 1 - y)]
        return [(src[p].at[2 * cx + cy], land[p].at[2 * x + y], (cx, cy, c)) for p in range(n) for cx, cy in chips] + \
               [(src[p], land[p].at[2 * x + y], (cx, cy, c)) for p in range(n, len(src)) for cx, cy in chips]
    return plan


def _pair_share(gathered):
    n = len(gathered)

    def body(*refs):
        land = refs[n:2 * n]
        send_sems, recv_sems, fwd_send, fwd_recv = refs[2 * n:]
        x, y, c = _my_place()
        k_dg = 3 - (2 * x + y)
        south = c == 0
        pass_on = 4 * jnp.where(south, 1 - x, x) + 2 * jnp.where(south, y, 1 - y) + c
        to = (jnp.where(south, x, 1 - x), jnp.where(south, 1 - y, y), c)

        def forward(p):
            return pltpu.make_async_remote_copy(
                src_ref=land[p].at[pass_on], dst_ref=land[p].at[pass_on], send_sem=fwd_send.at[p],
                recv_sem=fwd_recv.at[p], device_id=to, device_id_type=MESH)

        def to_sibling(p, k):
            blk = land[p].at[2 * k + c]
            return pltpu.make_async_remote_copy(
                src_ref=blk, dst_ref=blk, send_sem=send_sems.at[4 * p + k], recv_sem=recv_sems.at[4 * p + k],
                device_id=(x, y, 1 - c), device_id_type=MESH)

        for p in range(n):
            forward(p).start()
        for p in range(n):
            for k in range(4):
                @pl.when(k != k_dg)
                def _():
                    to_sibling(p, k).start()
        for p in range(n):
            dg = land[p].at[2 * k_dg + c]
            pltpu.make_async_remote_copy(src_ref=dg, dst_ref=dg, send_sem=fwd_send.at[p], recv_sem=fwd_recv.at[p],
                                         device_id=to, device_id_type=MESH).wait_recv()
            for k in range(4):
                @pl.when(k == k_dg)
                def _():
                    to_sibling(p, k).start()
        for p in range(n):
            forward(p).wait_send()
            for k in range(4):
                to_sibling(p, k).wait_send()
                blk = land[p].at[2 * k + 1 - c]
                pltpu.make_async_remote_copy(
                    src_ref=blk, dst_ref=blk, send_sem=send_sems.at[4 * p + k], recv_sem=recv_sems.at[4 * p + k],
                    device_id=(x, y, 1 - c), device_id_type=MESH).wait_recv()

    return pl.pallas_call(
        body, name="gather_pair_share", in_specs=[_HBM] * n, out_specs=[_HBM] * n,
        out_shape=[_sds(g.shape, g.dtype) for g in gathered],
        input_output_aliases={p: p for p in range(n)},
        scratch_shapes=[pltpu.SemaphoreType.DMA((4 * n,)), pltpu.SemaphoreType.DMA((4 * n,)),
                        pltpu.SemaphoreType.DMA((n,)), pltpu.SemaphoreType.DMA((n,))],
    )(*gathered)


def _place_slot(arrs, slot, n_slots, name, out_dtype=None):
    n = len(arrs)

    def body(s_ref, *refs):
        for p in range(n):
            refs[n + p][...] = refs[p][...].astype(refs[n + p].dtype)

    def spec(a):
        tail = a.shape[-2:]
        return pl.BlockSpec((None,) + tail, lambda i, s_ref: (s_ref[0], 0, 0))

    in_specs = [spec(a) if a.ndim == 3 else pl.BlockSpec(a.shape, lambda i, s_ref: (0, 0)) for a in arrs]
    return pl.pallas_call(
        body, name=name,
        grid_spec=pltpu.PrefetchScalarGridSpec(num_scalar_prefetch=1, grid=(1,), in_specs=in_specs,
                                               out_specs=[spec(a) for a in arrs]),
        out_shape=[_sds((n_slots,) + a.shape[-2:], out_dtype or a.dtype) for a in arrs],
        compiler_params=_cp(("arbitrary",)),
    )(slot, *arrs)


def _cols_from_blocks(w8, name):
    _, R, C = w8.shape

    def body(w_ref, o_ref):
        for e in range(8):
            o_ref[:, C * e:C * (e + 1)] = w_ref[e]

    return pl.pallas_call(body, name=name, out_shape=_sds((R, 8 * C), w8.dtype), compiler_params=_cp())(w8)


def _all_gather_small(srcs, name):
    n = len(srcs)
    vmem = pl.BlockSpec(memory_space=pltpu.VMEM)

    def body(*refs):
        src, out = refs[:n], refs[n:2 * n]
        send_sems, recv_sems = refs[2 * n:]
        x, y, c = _my_place()
        me = 4 * x + 2 * y + c
        flip = lambda v, f: 1 - v if f else v
        peers = [(flip(x, fx), flip(y, fy), flip(c, fc)) for fx in (0, 1) for fy in (0, 1) for fc in (0, 1)][1:]
        cps = []
        for p in range(n):
            out[p][me] = src[p][...]
            for k, peer in enumerate(peers):
                cps.append(pltpu.make_async_remote_copy(
                    src_ref=src[p], dst_ref=out[p].at[me], send_sem=send_sems.at[7 * p + k],
                    recv_sem=recv_sems.at[7 * p + k], device_id=peer, device_id_type=MESH))
        for cp in cps:
            cp.start()
        for cp in cps:
            cp.wait()

    return pl.pallas_call(
        body, name=name, in_specs=[vmem] * n, out_specs=[vmem] * n,
        out_shape=[_sds((8,) + a.shape, a.dtype) for a in srcs],
        scratch_shapes=[pltpu.SemaphoreType.DMA((7 * n,)), pltpu.SemaphoreType.DMA((7 * n,))],
    )(*srcs)


def _adaln_exchange(c, w_ada, b_part):
    vmem = pl.BlockSpec(memory_space=pltpu.VMEM)

    def body(c_ref, w_ref, b_ref, call_ref, mod_ref, part_ref, send_sems, recv_sems):
        x, y, c = _my_place()
        me = 4 * x + 2 * y + c
        flip = lambda v, f: 1 - v if f else v
        peers = [(flip(x, fx), flip(y, fy), flip(c, fc)) for fx in (0, 1) for fy in (0, 1) for fc in (0, 1)][1:]
        call_ref[me] = c_ref[...]
        first = [pltpu.make_async_remote_copy(src_ref=c_ref, dst_ref=call_ref.at[me], send_sem=send_sems.at[k],
                                              recv_sem=recv_sems.at[k], device_id=peer, device_id_type=MESH)
                 for k, peer in enumerate(peers)]
        for cp in first:
            cp.start()
        for cp in first:
            cp.wait()
        c_all = jnp.concatenate([call_ref[j] for j in range(8)], axis=0)
        part_ref[...] = _dot(c_all.astype(CDT), w_ref[...].astype(CDT), NN) + b_ref[...]
        mod_ref[me] = part_ref[pl.ds(me, 1), :]
        second = [pltpu.make_async_remote_copy(
            src_ref=part_ref.at[pl.ds(4 * px + 2 * py + pc, 1)], dst_ref=mod_ref.at[me], send_sem=send_sems.at[7 + k],
            recv_sem=recv_sems.at[7 + k], device_id=(px, py, pc), device_id_type=MESH)
            for k, (px, py, pc) in enumerate(peers)]
        for cp in second:
            cp.start()
        for cp in second:
            cp.wait()

    return pl.pallas_call(
        body, name="adaln_exchange", in_specs=[vmem] * 3, out_specs=[vmem] * 2,
        out_shape=[_sds((8, 1, D), F32), _sds((8, 1, w_ada.shape[1]), F32)],
        scratch_shapes=[pltpu.VMEM((8, w_ada.shape[1]), F32), pltpu.SemaphoreType.DMA((14,)),
                        pltpu.SemaphoreType.DMA((14,))],
        compiler_params=_cp(),
    )(c, w_ada, b_part)


def _pair_add(dw, recv, axis, width, cidx, name):
    if axis == 0:
        tr = min(width, 256)
        C = dw.shape[1]
        grid = (4, width // tr)
        nb = width // tr
        own = pl.BlockSpec((tr, C), lambda q, i, c_ref: ((2 * q + c_ref[0]) * nb + i, 0))
        oth = pl.BlockSpec((None, tr, C), lambda q, i, c_ref: (q, i, 0))
    else:
        R = dw.shape[0]
        tr = min(R, 256)
        grid = (4, R // tr)
        own = pl.BlockSpec((tr, width), lambda q, i, c_ref: (i, 2 * q + c_ref[0]))
        oth = pl.BlockSpec((None, tr, width), lambda q, i, c_ref: (q, i, 0))

    def body(c_ref, a_ref, b_ref, o_ref):
        o_ref[...] = (a_ref[...] + b_ref[...]).astype(o_ref.dtype)

    return pl.pallas_call(
        body, name=name,
        grid_spec=pltpu.PrefetchScalarGridSpec(num_scalar_prefetch=1, grid=grid, in_specs=[own, oth], out_specs=oth),
        out_shape=_sds(recv.shape, CDT), compiler_params=_cp(("parallel", "parallel")),
    )(cidx, dw, recv)


def _add_pairs(xs, ys, name):
    n = len(xs)

    def body(*refs):
        for p in range(n):
            refs[2 * n + p][...] = refs[p][...] + refs[n + p][...]

    return pl.pallas_call(body, name=name, out_shape=[_sds(a.shape, F32) for a in xs])(*xs, *ys)


def _adam_math(w, g, m, v):
    m = ADAM_B1 * m + (1.0 - ADAM_B1) * g
    v = ADAM_B2 * v + (1.0 - ADAM_B2) * (g * g)
    m_hat = m / (1.0 - ADAM_B1 ** ADAM_STEP)
    v_hat = v / (1.0 - ADAM_B2 ** ADAM_STEP)
    delta = -ADAM_LR * (m_hat / (jnp.sqrt(v_hat) + ADAM_EPS) + ADAM_WD * w)
    return delta, m, v


def _adam(gparts, w, m, v, name):
    P, R, C = gparts.shape
    tr = R if R <= 256 else 256

    def body(g_ref, w_ref, m_ref, v_ref, go_ref, d_ref, mo_ref, vo_ref):
        g = g_ref[0].astype(F32)
        for i in range(1, P):
            g = g + g_ref[i].astype(F32)
        d, mn, vn = _adam_math(w_ref[...], g, m_ref[...], v_ref[...])
        go_ref[...] = g
        d_ref[...] = d
        mo_ref[...] = mn
        vo_ref[...] = vn

    t = pl.BlockSpec((tr, C), lambda i: (i, 0))
    return pl.pallas_call(
        body, name=name, grid=(R // tr,),
        in_specs=[pl.BlockSpec((P, tr, C), lambda i: (0, i, 0)), t, t, t], out_specs=[t, t, t, t],
        out_shape=[_sds((R, C), F32)] * 4, compiler_params=_cp(("parallel",)),
    )(gparts, w, m, v)


def _sum_parts(gparts, name):
    P, R, C = gparts.shape

    def body(g_ref, o_ref):
        g = g_ref[0]
        for i in range(1, P):
            g = g + g_ref[i]
        o_ref[...] = g

    return pl.pallas_call(body, name=name, out_shape=_sds((R, C), F32))(gparts)


def _adam_w_ada(c_t, dm, w, m, v):
    R, C = w.shape

    def body(c_ref, dm_ref, w_ref, m_ref, v_ref, go_ref, d_ref, mo_ref, vo_ref):
        g = c_ref[:, 0:1] * dm_ref[0:1, :]
        for b in range(1, 8):
            g = g + c_ref[:, b:b + 1] * dm_ref[b:b + 1, :]
        d, mn, vn = _adam_math(w_ref[...], g, m_ref[...], v_ref[...])
        go_ref[...] = g
        d_ref[...] = d
        mo_ref[...] = mn
        vo_ref[...] = vn

    return pl.pallas_call(body, name="adam_w_ada", out_shape=[_sds((R, C), F32)] * 4,
                          compiler_params=_cp())(c_t, dm, w, m, v)


def kernel(x, c, norm_g, w_ada, b_ada, w_in, pool_w, pool_scale, w_attn_br, w_pool_br, w_out, rel_bias, final_g, loss_target, m_norm_g, m_w_ada, m_b_ada, m_w_in, m_pool_w, m_pool_scale, m_w_attn_br, m_w_pool_br, m_w_out, m_rel_bias, m_final_g, v_norm_g, v_w_ada, v_b_ada, v_w_in, v_pool_w, v_pool_scale, v_w_attn_br, v_w_pool_br, v_w_out, v_rel_bias, v_final_g):
    S = x.shape[1]
    px, py, pc = _my_place()
    me = 4 * px + 2 * py + pc
    cidx = jnp.reshape(pc, (1,)).astype(jnp.int32)

    layout = [(1, 1024), (1, 128), (1, 128), (0, 128)]
    final_g2 = final_g.reshape(1, D)
    me1 = jnp.reshape(me, (1,)).astype(jnp.int32)

    b_part = lax.dynamic_slice(b_ada, (0, me * 384), (1, 384))
    c_all, mod_all = _adaln_exchange(c, w_ada[0], b_part)
    c_all = c_all.reshape(8, D)
    mod = mod_all.reshape(1, 3 * D)
    shift, scale, gate = mod[:, :D], mod[:, D:2 * D], mod[:, 2 * D:]

    gplan = _gather_chips_plan([(None, 1)] * 4)
    lands = _place_slot([w_in[0], w_attn_br[0], w_pool_br[0], w_out[0]], me1, 8, "place_own_shards", CDT)
    g_sems, _, lands, tok = _split_start("gather_chips_start", [], lands, 8, gplan, dep=mod_all)
    st = _local_pre(x[0], shift, scale, norm_g, rel_bias, w_in[0], dep=tok)
    _, lands = _split_wait("gather_chips_wait", g_sems, [], lands, gplan,
                           [st["ht"], st["own"]] + [b for pair in st["biases"] for b in pair])
    w_in_full, wab8, wpb8, wout8 = _pair_share(lands)
    wab, wpb = _cols_from_blocks(wab8, "w_attn_br_cols"), _cols_from_blocks(wpb8, "w_pool_br_cols")
    wout = wout8.reshape(D, D)

    st = _local_grads(st, loss_target[0], gate, w_in_full, pool_w[0], pool_scale, wab, wpb, wout, final_g2, me1)

    early = [st["dpw"].reshape(AW, 128), st["dps"], st["gfg"], jnp.broadcast_to(st["loss"], (1, 128))]
    dws = [st["dw_in"], st["dw_ab"], st["dw_pb"], st["dw_out"]]
    pplan = _pair_plan(layout)
    pair_shapes = [(4, w, a.shape[1]) if ax == 0 else (4, a.shape[0], w) for a, (ax, w) in zip(dws, layout)]
    p_sems, srcs, recv, tok = _split_start(
        "reduce_pair_start", dws + early, [lax.empty(s, F32) for s in pair_shapes + [a.shape for a in early]], 20,
        pplan)
    d_rel_bias = _local_bias_grads(st, dep=tok)
    srcs, recv = _split_wait("reduce_pair_wait", p_sems, srcs, recv, pplan, d_rel_bias)
    parts = [_pair_add(dw, r, ax, w, cidx, f"pair_add{i}")
             for i, (dw, r, (ax, w)) in enumerate(zip(srcs[:4], recv[:4], layout))]
    parts += _add_pairs(srcs[4:], recv[4:], "pair_add_smalls")

    slot1 = jnp.reshape(2 * px + py, (1,)).astype(jnp.int32)
    cplan = _chips_plan(4)
    c_sems, parts, lands, tok = _split_start("reduce_chips_start", parts,
                                             _place_slot(parts, slot1, 4, "place_own_partials"), 24, cplan)
    grad_x, dmod, d_norm_g = _local_input_grads(st, w_in_full, norm_g, scale, dep=tok)

    dmod_all, ng_all, rb_all = _all_gather_small([dmod, d_norm_g, d_rel_bias], "gather_late")
    o_norm_g = _adam(ng_all, norm_g, m_norm_g, v_norm_g, "adam_norm_g")
    o_rel_bias = _adam(rb_all, rel_bias, m_rel_bias, v_rel_bias, "adam_rel_bias")
    o_b_ada = _adam(dmod_all, b_ada, m_b_ada, v_b_ada, "adam_b_ada")
    dm_mine = lax.dynamic_slice(dmod_all.reshape(8, 3 * D), (0, me * 384), (8, 384))
    o_w_ada = [o[None] for o in _adam_w_ada(c_all.T, dm_mine, w_ada[0], m_w_ada[0], v_w_ada[0])]

    _, (g_in, g_ab, g_pb, g_out, g_pw, g_ps, g_fg, g_loss) = _split_wait(
        "reduce_chips_wait", c_sems, parts, lands, cplan, [o_w_ada[0], o_b_ada[0], o_norm_g[0], o_rel_bias[0]])
    o_w_in = [o[None] for o in _adam(g_in, w_in[0], m_w_in[0], v_w_in[0], "adam_w_in")]
    o_w_ab = [o[None] for o in _adam(g_ab, w_attn_br[0], m_w_attn_br[0], v_w_attn_br[0], "adam_w_attn_br")]
    o_w_pb = [o[None] for o in _adam(g_pb, w_pool_br[0], m_w_pool_br[0], v_w_pool_br[0], "adam_w_pool_br")]
    o_w_out = [o[None] for o in _adam(g_out, w_out[0], m_w_out[0], v_w_out[0], "adam_w_out")]
    o_pool_w = [o.reshape(pool_w.shape) for o in _adam(g_pw, pool_w.reshape(AW, 128), m_pool_w.reshape(AW, 128),
                                                       v_pool_w.reshape(AW, 128), "adam_pool_w")]
    o_pool_scale = _adam(g_ps, pool_scale, m_pool_scale, v_pool_scale, "adam_pool_scale")
    o_final_g = [o.reshape(D) for o in _adam(g_fg, final_g2, m_final_g.reshape(1, D), v_final_g.reshape(1, D),
                                             "adam_final_g")]
    loss = _sum_parts(g_loss, "sum_loss")[0, 0]

    per_w = [o_norm_g, o_w_ada, o_b_ada, o_w_in, o_pool_w, o_pool_scale, o_w_ab, o_w_pb, o_w_out, o_rel_bias, o_final_g]
    outs = [loss, grad_x[None]]
    for k in range(4):
        outs += [o[k] for o in per_w]
    return tuple(outs)
```

```python
import functools
import math

import numpy as np
import jax
import jax.numpy as jnp
from jax import lax
from jax.experimental import pallas as pl
from jax.experimental.pallas import tpu as pltpu

F32 = jnp.float32
CDT = jnp.bfloat16
D = 1024
HD = 64
NH = 8
AW = 512
QKV_W = 4608
REST_W = 3584
R16 = 16
EPS = 1e-6
NEG = -1e30
N_BACK = 128
ADAM_LR, ADAM_B1, ADAM_B2, ADAM_EPS, ADAM_WD, ADAM_STEP = 0.001, 0.9, 0.999, 1e-08, 0.01, 10
MESH = pl.DeviceIdType.MESH
VMEM_MB = 1024 * 1024


def _cp(sem=None, vmem=48, **kw):
    if sem is not None:
        kw["dimension_semantics"] = sem
    return pltpu.CompilerParams(vmem_limit_bytes=vmem * VMEM_MB, **kw)


def _sds(shape, dtype):
    return jax.ShapeDtypeStruct(tuple(shape), dtype)


def _rest_blk(jj):
    return jnp.where(jj == 0, 9, jnp.where(jj == 6, 10, jj + 10))


def _sigmoid(z):
    return 0.5 * jnp.tanh(0.5 * z) + 0.5


def _dot(a, b, dims):
    return lax.dot_general(a, b, (dims, ((), ())), preferred_element_type=F32)


NN = ((1,), (0,))
NT = ((1,), (1,))
TN = ((0,), (0,))


def _ld(ref):
    if len(ref.shape) == 2:
        return ref[...]
    return jnp.concatenate([ref[a] for a in range(ref.shape[0])], axis=0)


def _st(ref, val, cols=None):
    val = val.astype(ref.dtype)
    if len(ref.shape) == 2:
        if cols is None:
            ref[...] = val
        else:
            ref[:, cols] = val
        return
    rows = ref.shape[1]
    for a in range(ref.shape[0]):
        if cols is None:
            ref[a] = val[a * rows:(a + 1) * rows]
        else:
            ref[a, :, cols] = val[a * rows:(a + 1) * rows]


REST_COLS = ((9, 10), (11, 16), (10, 11))
_HBM = pl.BlockSpec(memory_space=pltpu.HBM)


def _resident(shape):
    return pl.BlockSpec(shape, lambda *ids: (0,) * len(shape), pipeline_mode=pl.Buffered(1))


def _w512(w_ref, j):
    return w_ref[j // 2, :, 512 * (j % 2):512 * (j % 2 + 1)]


def _proj(h, w):
    S = h.shape[0]
    tm = 512

    def body(a_ref, w_ref, q_ref, r_ref):
        a = a_ref[...]
        for j in range(9):
            q_ref[:, 512 * j:512 * (j + 1)] = _dot(a, _w512(w_ref, j), NN).astype(q_ref.dtype)
        off = 0
        for lo, hi in REST_COLS:
            for j in range(lo, hi):
                r_ref[:, off:off + 512] = _dot(a, _w512(w_ref, j), NN)
                off += 512

    return pl.pallas_call(
        body, name="proj", grid=(S // tm,),
        in_specs=[pl.BlockSpec((tm, D), lambda i: (i, 0)), _resident((8, D, 1024))],
        out_specs=[pl.BlockSpec((tm, QKV_W), lambda i: (i, 0)), pl.BlockSpec((tm, REST_W), lambda i: (i, 0))],
        out_shape=[_sds((S, QKV_W), CDT), _sds((S, REST_W), F32)],
        compiler_params=_cp(("parallel",), vmem=56),
    )(h, w)


def _after(dep):
    return ([], []) if dep is None else ([pl.BlockSpec(memory_space=pl.ANY)], [dep])


def _dh(dqkv, drest, w, dep=None):
    S = dqkv.shape[0]
    tm = 512
    dep_specs, dep_args = _after(dep)

    def body(a_ref, b_ref, w_ref, *rest):
        o_ref = rest[-1]
        acc = _dot(a_ref[:, 0:512], _w512(w_ref, 0), NT)
        for j in range(1, 9):
            acc = acc + _dot(a_ref[:, 512 * j:512 * (j + 1)], _w512(w_ref, j), NT)
        off = 0
        for lo, hi in REST_COLS:
            for j in range(lo, hi):
                acc = acc + _dot(b_ref[:, off:off + 512], _w512(w_ref, j), NT)
                off += 512
        o_ref[...] = acc

    return pl.pallas_call(
        body, name="dh", grid=(S // tm,),
        in_specs=[pl.BlockSpec((tm, QKV_W), lambda i: (i, 0)), pl.BlockSpec((tm, REST_W), lambda i: (i, 0)),
                  _resident((8, D, 1024))] + dep_specs,
        out_specs=pl.BlockSpec((tm, D), lambda i: (i, 0)),
        out_shape=_sds((S, D), F32),
        compiler_params=_cp(("parallel",), vmem=56),
    )(dqkv, drest, w, *dep_args)


def _mm_tn(at, b, *, n_total, out_blk, name, prev=None, tn=512):
    R, S = at.shape
    nb = b.shape[1] // tn

    def body(*refs):
        refs[-1][...] = _dot(refs[0][...], refs[1][...], NN)

    in_specs = [_resident((R, S)), pl.BlockSpec((S, tn), lambda j: (0, j))]
    args = [at, b]
    aliases = {}
    if prev is not None:
        in_specs.append(pl.BlockSpec(memory_space=pl.ANY))
        args.append(prev)
        aliases = {2: 0}
    return pl.pallas_call(
        body, name=name, grid=(nb,), in_specs=in_specs,
        out_specs=pl.BlockSpec((R, tn), lambda j: (0, out_blk(j))),
        out_shape=_sds((R, n_total), F32),
        input_output_aliases=aliases,
        compiler_params=_cp(("parallel",)),
    )(*args)


def _tcast(a, name):
    M, C = a.shape
    tm = 512

    def body(a_ref, o_ref):
        o_ref[...] = a_ref[...].astype(F32).T.astype(o_ref.dtype)

    return pl.pallas_call(
        body, name=name, grid=(M // tm,),
        in_specs=[pl.BlockSpec((tm, C), lambda i: (i, 0))],
        out_specs=pl.BlockSpec((C, tm), lambda i: (0, i)),
        out_shape=_sds((C, M), CDT), compiler_params=_cp(("parallel",)),
    )(a)


TOK = R16 * R16


def _swap16():
    i = lax.broadcasted_iota(jnp.int32, (TOK, TOK), 0)
    j = lax.broadcasted_iota(jnp.int32, (TOK, TOK), 1)
    return jnp.where(j == jnp.bitwise_and(i, 15) * 16 + jnp.right_shift(i, 4), 1.0, 0.0).astype(CDT)


def _norm_mod(x, norm_g, shift, scale, dep=None):
    S = x.shape[0]
    dep_specs, dep_args = _after(dep)

    def body(x_ref, g_ref, sh_ref, sc_ref, *rest):
        xv = x_ref[...]
        xn = xv * lax.rsqrt(jnp.mean(xv * xv, axis=-1, keepdims=True) + EPS)
        h = (xn * (g_ref[...] * (1.0 + sc_ref[...])) + sh_ref[...]).astype(CDT)
        _st(rest[-1], _dot(_swap16(), h, NN))

    vec = pl.BlockSpec((1, D), lambda i: (0, 0))
    return pl.pallas_call(
        body, name="norm_mod", grid=(S // TOK,),
        in_specs=[pl.BlockSpec((TOK, D), lambda i: (i, 0)), vec, vec, vec] + dep_specs,
        out_specs=pl.BlockSpec((R16, R16, D), lambda i: (0, i, 0)),
        out_shape=_sds((R16, S // R16, D), CDT), compiler_params=_cp(("parallel",)),
    )(x, norm_g, shift, scale, *dep_args)


def _t5_bucket_np(n):
    nf = np.maximum(n, 1).astype(np.float32)
    large = 16 + (np.log(nf / np.float32(16)) / np.float32(math.log(2048 / 16)) * np.float32(16)).astype(np.int32)
    large = np.minimum(large, 31)
    return np.where(n < 16, n, large).astype(np.int32)


class _Geom:
    def __init__(self, gi, LS):
        self.gi, self.LS = gi, LS
        self.dil = (1, 4, 16)[gi]
        if gi == 0:
            self.lead, self.rows, self.n_tiles, self.has_prev = (R16,), 16, LS // 16, True
            a = np.arange(R16)[:, None]
            ll = np.arange(16)[None, :]
            self.pos = (16 * ll + a).reshape(-1)
        elif gi == 1:
            self.lead, self.rows, self.n_tiles, self.has_prev = (4,), 64, LS // 64, True
            a = np.arange(4)[:, None]
            ll = np.arange(64)[None, :]
            self.pos = (4 * ll + a).reshape(-1)
        else:
            self.lead, self.rows, self.n_tiles, self.has_prev = (), LS, 1, False
            self.pos = np.arange(LS)
        self.T = self.pos.shape[0]
        self.Tk = 2 * self.T if self.has_prev else self.T
        self.outer = {0: (), 1: (4,), 2: (R16,)}[gi]

    def view(self, arr):
        S, C = arr.shape
        if self.gi == 1:
            return arr.reshape(4, 4, self.LS, C)
        return arr.reshape(R16, self.LS, C)

    def view_shape(self, C):
        return (4, 4, self.LS, C) if self.gi == 1 else (R16, self.LS, C)

    def unview(self, arr):
        return arr.reshape(R16 * self.LS, arr.shape[-1])

    def spec(self, W, cblk, tile):
        if self.gi == 0:
            return pl.BlockSpec((R16, 16, W), lambda n: (0, tile(n), cblk))
        if self.gi == 1:
            return pl.BlockSpec((4, None, 64, W), lambda r, n: (0, r, tile(n), cblk))
        return pl.BlockSpec((None, self.LS, W), lambda r: (r, 0, cblk))

    def halves(self):
        R = self.rows
        idx = np.arange(self.T).reshape(-1, R)
        return idx[:, :R // 2].reshape(-1), idx[:, R // 2:].reshape(-1)

    def bucket_index(self, sub):
        lo, hi = self.halves()
        pos = self.pos
        if sub == 0:
            pq = pos[lo]
            pk = np.concatenate([pos[hi] - self.T, pos[lo]]) if self.has_prev else np.concatenate([pos[lo], pos[hi]])
        else:
            pq = pos[hi]
            pk = np.concatenate([pos[lo], pos[hi]])
        dist = pq[:, None] - pk[None, :]
        ok = (dist >= 0) & (dist <= N_BACK)
        bucket = _t5_bucket_np(np.clip(dist, 0, N_BACK) * self.dil)
        return np.where(ok, bucket, -1).astype(np.int32)


def _bias_table(idx, buckets, rel_bias, gi, sub, dep=None):
    T, Tk = idx.shape
    tr = 32
    dep_specs, dep_args = _after(dep)

    def body(idx_ref, rb_ref, *rest):
        h = pl.program_id(0)
        ix = idx_ref[...]
        acc = jnp.full(ix.shape, NEG, F32)
        for b in buckets:
            acc = jnp.where(ix == b, rb_ref[b, gi * NH + h], acc)
        rest[-1][...] = acc

    return pl.pallas_call(
        body, name=f"bias_table{gi}{'ab'[sub]}", grid=(NH, T // tr),
        in_specs=[pl.BlockSpec((tr, Tk), lambda h, i: (i, 0)), pl.BlockSpec(memory_space=pltpu.SMEM)] + dep_specs,
        out_specs=pl.BlockSpec((None, tr, Tk), lambda h, i: (h, i, 0)),
        out_shape=_sds((NH, T, Tk), F32), compiler_params=_cp(("parallel", "parallel")),
    )(idx, rel_bias, *dep_args)


def _bias_grad(dsum, idx, buckets, gi, sub, dep=None):
    _, T, Tk = dsum.shape
    dep_specs, dep_args = _after(dep)

    def body(ds_ref, idx_ref, *rest):
        o_ref, r_ref = rest[-2], rest[-1]
        r_ref[...] = jnp.zeros_like(r_ref)
        ix, ds = idx_ref[...], ds_ref[...]
        for b in buckets:
            r_ref[b:b + 1, :] = jnp.sum(jnp.where(ix == b, ds, 0.0), axis=0, keepdims=True)
        o_ref[...] = jnp.broadcast_to(jnp.sum(r_ref[...], axis=1, keepdims=True), (32, 128))

    out = pl.pallas_call(
        body, name=f"bias_grad{gi}{'ab'[sub]}", grid=(NH,),
        in_specs=[pl.BlockSpec((None, T, Tk), lambda h: (h, 0, 0)), pl.BlockSpec((T, Tk), lambda h: (0, 0))] + dep_specs,
        out_specs=pl.BlockSpec((None, 32, 128), lambda h: (h, 0, 0)),
        out_shape=_sds((NH, 32, 128), F32), scratch_shapes=[pltpu.VMEM((32, Tk), F32)],
        compiler_params=_cp(("parallel",)),
    )(dsum, idx, *dep_args)
    return out[:, :, 0].T


def _prev_mask(n, T, Tk):
    col = lax.broadcasted_iota(jnp.int32, (1, Tk), 1)
    return jnp.where((col < T) & (n == 0), NEG, 0.0).astype(F32)


def _split_rows(v, R):
    A, half = v.shape[0] // R, R // 2
    via_f32 = v.dtype != F32 and half % 16 != 0
    w = v.astype(F32) if via_f32 else v
    lo = jnp.concatenate([w[a * R:a * R + half] for a in range(A)], axis=0)
    hi = jnp.concatenate([w[a * R + half:(a + 1) * R] for a in range(A)], axis=0)
    return (lo.astype(v.dtype), hi.astype(v.dtype)) if via_f32 else (lo, hi)


def _merge_rows(lo, hi, R):
    half = R // 2
    A = lo.shape[0] // half
    return jnp.concatenate([x[a * half:(a + 1) * half] for a in range(A) for x in (lo, hi)], axis=0)


def _attn_fwd(qkv, biases, geom):
    S = qkv.shape[0]
    gi, T, R, has_prev = geom.gi, geom.T, geom.rows, geom.has_prev
    H2 = T // 2
    qv = geom.view(qkv)
    cq, ck, cv = 3 * gi, 3 * gi + 1, 3 * gi + 2

    def body(*refs):
        if has_prev:
            q_ref, kp_ref, kc_ref, vp_ref, vc_ref, ba_ref, bb_ref, o_ref, l_ref = refs
        else:
            q_ref, kc_ref, vc_ref, ba_ref, bb_ref, o_ref, l_ref = refs
        q_lo, q_hi = _split_rows(_ld(q_ref), R)
        kc_lo, kc_hi = _split_rows(_ld(kc_ref), R)
        vc_lo, vc_hi = _split_rows(_ld(vc_ref), R)
        k_b = jnp.concatenate([kc_lo, kc_hi], axis=0)
        v_b = jnp.concatenate([vc_lo, vc_hi], axis=0)
        if has_prev:
            n = pl.program_id(len(geom.outer))
            k_a = jnp.concatenate([_split_rows(_ld(kp_ref), R)[1], kc_lo], axis=0)
            v_a = jnp.concatenate([_split_rows(_ld(vp_ref), R)[1], vc_lo], axis=0)
            pm = _prev_mask(n, H2, T)
        else:
            k_a, v_a, pm = k_b, v_b, None
        low = lax.broadcasted_iota(jnp.int32, (1, 2 * HD), 1) < HD
        for j in range(NH // 2):
            cols = slice(2 * HD * j, 2 * HD * (j + 1))
            o_half, l_half = [], []
            for q, k, v, b_ref, mask in ((q_lo, k_a, v_a, ba_ref, pm), (q_hi, k_b, v_b, bb_ref, None)):
                q2 = q[:, cols].astype(F32) * 0.125
                k2, v2 = k[:, cols], v[:, cols]
                qs = jnp.concatenate([jnp.where(low, q2, 0.0), jnp.where(low, 0.0, q2)], axis=0).astype(q.dtype)
                s = _dot(qs, k2, NT) + b_ref[j]
                if mask is not None:
                    s = s + mask
                m = jnp.max(s, axis=1, keepdims=True)
                p = jnp.exp(s - m)
                l = jnp.sum(p, axis=1, keepdims=True)
                pv = _dot(p.astype(v.dtype), v2, NN) / l
                lse = m + jnp.log(l)
                o_half.append(jnp.where(low, pv[:H2], pv[H2:]))
                l_half.append(jnp.where(low, lse[:H2], lse[H2:]))
            _st(o_ref, _merge_rows(o_half[0], o_half[1], R), cols)
            _st(l_ref, _merge_rows(l_half[0], l_half[1], R), cols)

    cur = lambda n: n
    prev = lambda n: jnp.maximum(n - 1, 0)
    if has_prev:
        in_specs = [geom.spec(AW, cq, cur), geom.spec(AW, ck, prev), geom.spec(AW, ck, cur),
                    geom.spec(AW, cv, prev), geom.spec(AW, cv, cur)]
        args = [qv] * 5
    else:
        in_specs = [geom.spec(AW, cq, cur), geom.spec(AW, ck, cur), geom.spec(AW, cv, cur)]
        args = [qv] * 3
    nd = len(geom.outer) + (1 if has_prev else 0)
    biases = [b.reshape(NH // 2, 2 * H2, b.shape[2]) for b in biases]
    in_specs += [pl.BlockSpec(b.shape, lambda *ids: (0, 0, 0)) for b in biases]
    grid = geom.outer + ((geom.n_tiles,) if has_prev else ())
    oshape = geom.view_shape(AW)
    o, l = pl.pallas_call(
        body, name=f"attn_fwd{gi}", grid=grid, in_specs=in_specs,
        out_specs=[geom.spec(AW, 0, cur), geom.spec(AW, 0, cur)],
        out_shape=[_sds(oshape, F32), _sds(oshape, F32)],
        compiler_params=_cp(("arbitrary",) * nd),
    )(*args, *biases)
    return geom.unview(o), geom.unview(l)


def _attn_combine(os_, ls_, rest):
    S = rest.shape[0]
    tm = 512

    def body(o0, o1, o2, l0, l1, l2, z_ref, attn_ref, L_ref, ag_ref, agt_ref):
        la, lb, lc = l0[...], l1[...], l2[...]
        m = jnp.maximum(jnp.maximum(la, lb), lc)
        ea, eb, ec = jnp.exp(la - m), jnp.exp(lb - m), jnp.exp(lc - m)
        den = ea + eb + ec
        attn = (ea * o0[...] + eb * o1[...] + ec * o2[...]) / den
        attn_ref[...] = attn
        L_ref[...] = m + jnp.log(den)
        z = z_ref[...]
        ag = attn * (z * _sigmoid(z))
        ag_ref[...] = ag.astype(ag_ref.dtype)
        agt_ref[...] = ag.T.astype(agt_ref.dtype)

    t = pl.BlockSpec((tm, AW), lambda i: (i, 0))
    return pl.pallas_call(
        body, name="attn_combine", grid=(S // tm,),
        in_specs=[t] * 6 + [pl.BlockSpec((tm, AW), lambda i: (i, 0))],
        out_specs=[t, t, t, pl.BlockSpec((AW, tm), lambda i: (0, i))],
        out_shape=[_sds((S, AW), F32), _sds((S, AW), F32), _sds((S, AW), CDT), _sds((AW, S), CDT)],
        compiler_params=_cp(("parallel",)),
    )(*os_, *ls_, rest)


def _pool_counts(r, LS, win):
    l = lax.broadcasted_iota(jnp.int32, (LS, 1), 0)
    return jnp.minimum(16 * l + (r + 1), win).astype(F32)


def _pool_fwd(rest, pool_w, pool_scale):
    S = rest.shape[0]
    LS = S // R16
    r3 = rest.reshape(R16, LS, REST_W)
    PG = 128

    def body(u_ref, z_ref, pw_ref, ps_ref, pt_ref, mp_ref, pg_ref, pgt_ref, e_ref):
        g = pl.program_id(0)
        win = jnp.left_shift(2, g)
        row = lax.broadcasted_iota(jnp.int32, (LS, PG), 0)
        for r in range(R16):
            e_ref[R16 + r] = u_ref[r]
        for r in range(1, R16):
            e_ref[r] = jnp.where(row == 0, 0.0, pltpu.roll(u_ref[r], 1, 0))

        def double(s):
            for i in range(2 * R16 - 1, 2 * s - 1, -1):
                e_ref[i] = e_ref[i] + e_ref[i - s]

        double(1)
        for k in (1, 2, 3):
            pl.when(g >= k)(functools.partial(double, 2 ** k))
        pw = pw_ref[...].astype(CDT)
        for r in range(R16):
            u = u_ref[r]
            pooled = e_ref[R16 + r] / _pool_counts(r, LS, win) - u
            pc = pooled.astype(CDT)
            mp = _dot(pc, pw, NN)
            z = z_ref[r]
            pg = (mp * ps_ref[...]) * (z * _sigmoid(z))
            mp_ref[r] = mp
            pg_ref[r] = pg.astype(pg_ref.dtype)
            pt_ref[:, r * LS:(r + 1) * LS] = pooled.T.astype(pt_ref.dtype)
            pgt_ref[:, r * LS:(r + 1) * LS] = pg.T.astype(pgt_ref.dtype)

    col = lambda off: pl.BlockSpec((R16, LS, PG), lambda g: (0, 0, off + g))
    tsp = pl.BlockSpec((PG, S), lambda g: (g, 0))
    pt, mp, pg, pgt = pl.pallas_call(
        body, name="pool_fwd", grid=(4,),
        in_specs=[col(24), col(4), pl.BlockSpec((None, PG, PG), lambda g: (g, 0, 0)),
                  pl.BlockSpec((1, PG), lambda g: (0, g))],
        out_specs=[tsp, col(0), col(0), tsp],
        out_shape=[_sds((AW, S), CDT), _sds((R16, LS, AW), F32), _sds((R16, LS, AW), CDT), _sds((AW, S), CDT)],
        scratch_shapes=[pltpu.VMEM((2 * R16, LS, PG), F32)],
        compiler_params=_cp(("parallel",)),
    )(r3, r3, pool_w, pool_scale)
    return pt, mp.reshape(S, AW), pg.reshape(S, AW), pgt


def _branch_merge(ag, pg, wab, wpb, rest):
    S = ag.shape[0]
    tm, tn = 512, 512

    def body(ag_ref, pg_ref, wa_ref, wp_ref, ga_ref, gp_ref, m_ref, mt_ref):
        ya = _dot(ag_ref[...], wa_ref[...], NN)
        yp = _dot(pg_ref[...], wp_ref[...], NN)
        mg = _sigmoid(ga_ref[...]) * ya + _sigmoid(gp_ref[...]) * yp
        m_ref[...] = mg.astype(m_ref.dtype)
        mt_ref[...] = mg.T.astype(mt_ref.dtype)

    a = pl.BlockSpec((tm, AW), lambda i, j: (i, 0))
    w = pl.BlockSpec((AW, tn), lambda i, j: (0, j))
    o = pl.BlockSpec((tm, tn), lambda i, j: (i, j))
    return pl.pallas_call(
        body, name="branch_merge", grid=(S // tm, D // tn),
        in_specs=[a, a, w, w, pl.BlockSpec((tm, tn), lambda i, j: (i, 2 + j)),
                  pl.BlockSpec((tm, tn), lambda i, j: (i, 4 + j))],
        out_specs=[o, pl.BlockSpec((tn, tm), lambda i, j: (j, i))],
        out_shape=[_sds((S, D), CDT), _sds((D, S), CDT)],
        compiler_params=_cp(("parallel", "parallel")),
    )(ag, pg, wab, wpb, rest, rest)


def _out_loss(merged, wout, x, target, gate, final_g):
    S = x.shape[0]

    def body(m_ref, w_ref, x_ref, t_ref, gate_ref, fg_ref, dx2_ref, dmo_ref, loss_ref, gfg_ref, dgate_ref):
        i = pl.program_id(0)

        @pl.when(i == 0)
        def _():
            loss_ref[...] = jnp.zeros_like(loss_ref)
            gfg_ref[...] = jnp.zeros_like(gfg_ref)
            dgate_ref[...] = jnp.zeros_like(dgate_ref)

        swap = _swap16()
        mo = _dot(_dot(swap, _ld(m_ref), NN).astype(CDT), w_ref[...], NN)
        gate, fg = gate_ref[...], fg_ref[...]
        x2 = x_ref[...] + gate * mo
        r2 = lax.rsqrt(jnp.mean(x2 * x2, axis=-1, keepdims=True) + EPS)
        xn2 = x2 * r2
        diff = xn2 * fg - t_ref[...]
        loss = jnp.sum(jnp.sum(diff * diff, axis=-1, keepdims=True), axis=0, keepdims=True)
        dy = diff * (1.0 / D)
        dxn2 = dy * fg
        dx2 = r2 * (dxn2 - xn2 * jnp.mean(dxn2 * xn2, axis=-1, keepdims=True))
        dx2_ref[...] = dx2
        _st(dmo_ref, _dot(swap, (dx2 * gate).astype(CDT), NN))
        loss_ref[...] += jnp.broadcast_to(loss * (0.5 / D), loss_ref.shape)
        gfg_ref[...] += jnp.sum(dy * xn2, axis=0, keepdims=True)
        dgate_ref[...] += jnp.sum(dx2 * mo, axis=0, keepdims=True)

    slab = pl.BlockSpec((R16, R16, D), lambda i: (0, i, 0))
    nat = pl.BlockSpec((TOK, D), lambda i: (i, 0))
    vec = pl.BlockSpec((1, D), lambda i: (0, 0))
    return pl.pallas_call(
        body, name="out_loss", grid=(S // TOK,),
        in_specs=[slab, _resident((D, D)), nat, nat, vec, vec],
        out_specs=[nat, slab, pl.BlockSpec((1, 128), lambda i: (0, 0)), vec, vec],
        out_shape=[_sds((S, D), F32), _sds((R16, S // R16, D), CDT), _sds((1, 128), F32), _sds((1, D), F32),
                   _sds((1, D), F32)],
        compiler_params=_cp(("arbitrary",)),
    )(merged, wout, x, target, gate, final_g)


def _bwd_gates(dmo, wout, wab, wpb, rest, ag, pg, attn, mp, pool_scale):
    S = dmo.shape[0]
    tm = 256

    def body(dmo_ref, wo_ref, wa_ref, wp_ref, za_ref, zp_ref, ga0, ga1, gp0, gp1, ag_ref, pg_ref, at_ref, mp_ref,
             ps_ref, dya_ref, dyp_ref, dr_ref, dat_ref, dl_ref, dmp_ref, dps_ref):
        i = pl.program_id(0)

        @pl.when(i == 0)
        def _():
            dps_ref[...] = jnp.zeros_like(dps_ref)

        dm = _dot(dmo_ref[...], wo_ref[...], NT)
        sa = _sigmoid(jnp.concatenate([ga0[...], ga1[...]], axis=1))
        sp = _sigmoid(jnp.concatenate([gp0[...], gp1[...]], axis=1))
        dya = (dm * sa).astype(CDT)
        dyp = (dm * sp).astype(CDT)
        dya_ref[...] = dya
        dyp_ref[...] = dyp
        ya = _dot(ag_ref[...], wa_ref[...], NN)
        yp = _dot(pg_ref[...], wp_ref[...], NN)
        dr_ref[:, 1024:2048] = (dm * ya * sa * (1.0 - sa)).astype(dr_ref.dtype)
        dr_ref[:, 2048:3072] = (dm * yp * sp * (1.0 - sp)).astype(dr_ref.dtype)
        da = _dot(dya, wa_ref[...], NT)
        dp = _dot(dyp, wp_ref[...], NT)
        za = za_ref[...]
        sga = _sigmoid(za)
        attn = at_ref[...]
        dattn = da * (za * sga)
        dr_ref[:, 0:512] = (da * attn * (sga * (1.0 + za * (1.0 - sga)))).astype(dr_ref.dtype)
        dat_ref[...] = dattn.astype(dat_ref.dtype)
        prod = dattn * attn
        low = lax.broadcasted_iota(jnp.int32, (1, 2 * HD), 1) < HD
        for j in range(NH // 2):
            x2 = prod[:, 2 * HD * j:2 * HD * (j + 1)]
            s0 = jnp.sum(jnp.where(low, x2, 0.0), axis=1, keepdims=True)
            s1 = jnp.sum(jnp.where(low, 0.0, x2), axis=1, keepdims=True)
            dl_ref[:, 2 * HD * j:2 * HD * (j + 1)] = jnp.where(low, s0, s1)
        zp = zp_ref[...]
        sgp = _sigmoid(zp)
        mpre = mp_ref[...]
        ps = ps_ref[...]
        dmixed = dp * (zp * sgp)
        dr_ref[:, 512:1024] = (dp * (mpre * ps) * (sgp * (1.0 + zp * (1.0 - sgp)))).astype(dr_ref.dtype)
        dps_ref[...] += jnp.sum(dmixed * mpre, axis=0, keepdims=True)
        dmp_ref[...] = (dmixed * ps).astype(dmp_ref.dtype)

    full = lambda shape: pl.BlockSpec(shape, lambda i: (0, 0))
    tD = pl.BlockSpec((tm, D), lambda i: (i, 0))
    tA = pl.BlockSpec((tm, AW), lambda i: (i, 0))
    rb = lambda c: pl.BlockSpec((tm, AW), lambda i: (i, c))
    return pl.pallas_call(
        body, name="bwd_gates", grid=(S // tm,),
        in_specs=[tD, full((D, D)), full((AW, D)), full((AW, D)), rb(0), rb(1), rb(2), rb(3), rb(4), rb(5),
                  tA, tA, tA, tA, full((1, AW))],
        out_specs=[tD, tD, pl.BlockSpec((tm, 3072), lambda i: (i, 0)), tA, tA, tA, full((1, AW))],
        out_shape=[_sds((S, D), CDT), _sds((S, D), CDT), _sds((S, REST_W), CDT), _sds((S, AW), CDT),
                   _sds((S, AW), F32), _sds((S, AW), CDT), _sds((1, AW), F32)],
        compiler_params=_cp(("arbitrary",), vmem=56),
    )(dmo, wout, wab, wpb, rest, rest, rest, rest, rest, rest, ag, pg, attn, mp, pool_scale)


def _pool_bwd(dmp, pt, pool_w, drest):
    S = dmp.shape[0]
    LS = S // R16
    PG = 128
    d3 = dmp.reshape(R16, LS, AW)
    dr3 = drest.reshape(R16, LS, REST_W)

    def body(d_ref, pt_ref, pw_ref, dr_in, du_ref, dpw_ref, g_ref, s_ref):
        del dr_in
        g = pl.program_id(0)
        win = jnp.left_shift(2, g)
        row = lax.broadcasted_iota(jnp.int32, (LS, PG), 0)
        pw = pw_ref[...].astype(CDT)
        dpw = jnp.zeros((PG, PG), F32)
        for r in range(R16):
            dm = d_ref[r]
            dpw = dpw + _dot(pt_ref[:, r * LS:(r + 1) * LS], dm, NN)
            dpooled = _dot(dm, pw, NT)
            g_ref[r] = dpooled
            sc = dpooled / _pool_counts(r, LS, win)
            s_ref[r] = sc
            if r < R16 - 1:
                s_ref[R16 + r] = jnp.where(row == LS - 1, 0.0, pltpu.roll(sc, LS - 1, 0))
        dpw_ref[...] = dpw

        def double(s):
            for i in range(0, 2 * R16 - 2 * s):
                s_ref[i] = s_ref[i] + s_ref[i + s]

        double(1)
        for k in (1, 2, 3):
            pl.when(g >= k)(functools.partial(double, 2 ** k))
        for r in range(R16):
            du_ref[r] = (s_ref[r] - g_ref[r]).astype(du_ref.dtype)

    du, dpw = pl.pallas_call(
        body, name="pool_bwd", grid=(4,),
        in_specs=[pl.BlockSpec((R16, LS, PG), lambda g: (0, 0, g)), pl.BlockSpec((PG, S), lambda g: (g, 0)),
                  pl.BlockSpec((None, PG, PG), lambda g: (g, 0, 0)), pl.BlockSpec(memory_space=pl.ANY)],
        out_specs=[pl.BlockSpec((R16, LS, PG), lambda g: (0, 0, 24 + g)),
                   pl.BlockSpec((None, PG, PG), lambda g: (g, 0, 0))],
        out_shape=[_sds((R16, LS, REST_W), CDT), _sds((4, PG, PG), F32)],
        scratch_shapes=[pltpu.VMEM((R16, LS, PG), F32), pltpu.VMEM((2 * R16, LS, PG), F32)],
        input_output_aliases={3: 0},
        compiler_params=_cp(("parallel",)),
    )(d3, pt, pool_w, dr3)
    return du.reshape(S, REST_W), dpw


def _attn_bwd(qkv, dattn, Lb, dlb, biases, geom, dqkv_prev):
    S = qkv.shape[0]
    gi, T, R, has_prev, N = geom.gi, geom.T, geom.rows, geom.has_prev, geom.n_tiles
    H2 = T // 2
    nd_outer = len(geom.outer)
    qv = geom.view(qkv)
    cq, ck, cv = 3 * gi, 3 * gi + 1, 3 * gi + 2
    W3 = 3 * AW

    def heads(q_ref, kp_ref, kc_ref, vp_ref, vc_ref, do_ref, L_ref, dl_ref, ba_ref, bb_ref, pm, dsa_ref, dsb_ref, emit):
        q_lo, q_hi = _split_rows(_ld(q_ref), R)
        do_lo, do_hi = _split_rows(_ld(do_ref), R)
        L_lo, L_hi = _split_rows(_ld(L_ref), R)
        dl_lo, dl_hi = _split_rows(_ld(dl_ref), R)
        kc_lo, kc_hi = _split_rows(_ld(kc_ref), R)
        vc_lo, vc_hi = _split_rows(_ld(vc_ref), R)
        k_b = jnp.concatenate([kc_lo, kc_hi], axis=0)
        v_b = jnp.concatenate([vc_lo, vc_hi], axis=0)
        if has_prev:
            k_a = jnp.concatenate([_split_rows(_ld(kp_ref), R)[1], kc_lo], axis=0)
            v_a = jnp.concatenate([_split_rows(_ld(vp_ref), R)[1], vc_lo], axis=0)
        else:
            k_a, v_a = k_b, v_b
        low = lax.broadcasted_iota(jnp.int32, (1, 2 * HD), 1) < HD
        for j in range(NH // 2):
            sl = slice(2 * HD * j, 2 * HD * (j + 1))
            res = []
            for q, k, v, do, L, dl, b_ref, mask, ds_ref in (
                    (q_lo, k_a, v_a, do_lo, L_lo, dl_lo, ba_ref, pm, dsa_ref),
                    (q_hi, k_b, v_b, do_hi, L_hi, dl_hi, bb_ref, None, dsb_ref)):
                q2 = q[:, sl].astype(F32) * 0.125
                do2 = do[:, sl].astype(F32)
                k2, v2 = k[:, sl], v[:, sl]
                L2, dl2 = L[:, sl], dl[:, sl]
                qs = jnp.concatenate([jnp.where(low, q2, 0.0), jnp.where(low, 0.0, q2)], axis=0).astype(q.dtype)
                dom = jnp.concatenate([jnp.where(low, do2, 0.0), jnp.where(low, 0.0, do2)], axis=0).astype(q.dtype)
                Lc = jnp.concatenate([L2[:, 0:1], L2[:, HD:HD + 1]], axis=0)
                dlc = jnp.concatenate([dl2[:, 0:1], dl2[:, HD:HD + 1]], axis=0)
                s = _dot(qs, k2, NT) + b_ref[j]
                if mask is not None:
                    s = s + mask
                p = jnp.exp(s - Lc)
                ds = p * (_dot(dom, v2, NT) - dlc)
                ds_ref[j] += ds
                dsc = ds.astype(q.dtype)
                dq_st = _dot(dsc, k2, NN)
                res.append((jnp.where(low, dq_st[:H2], dq_st[H2:]) * 0.125, _dot(dsc, qs, TN),
                            _dot(p.astype(q.dtype), dom, TN)))
            (dq_lo, dk_a, dv_a), (dq_hi, dk_b, dv_b) = res
            dq = _merge_rows(dq_lo, dq_hi, R)
            if has_prev:
                emit(sl, dq, _merge_rows(dk_a[H2:] + dk_b[:H2], dk_b[H2:], R),
                     _merge_rows(dv_a[H2:] + dv_b[:H2], dv_b[H2:], R), dk_a[:H2], dv_a[:H2])
            else:
                emit(sl, dq, _merge_rows(dk_a[:H2] + dk_b[:H2], dk_a[H2:] + dk_b[H2:], R),
                     _merge_rows(dv_a[:H2] + dv_b[:H2], dv_a[H2:] + dv_b[H2:], R), None, None)

    def first_step():
        ok = pl.program_id(nd_outer) == 0 if has_prev else pl.program_id(0) == 0
        for a in range(nd_outer if has_prev else 0):
            ok = ok & (pl.program_id(a) == 0)
        return ok

    if has_prev:
        def body(q_ref, kp_ref, kc_ref, vp_ref, vc_ref, do_ref, L_ref, dl_ref, ba_ref, bb_ref, prev_in, out_ref,
                 dsa_ref, dsb_ref, hold, new, pp):
            del prev_in
            s_id = pl.program_id(nd_outer)

            @pl.when(first_step())
            def _():
                dsa_ref[...] = jnp.zeros_like(dsa_ref)
                dsb_ref[...] = jnp.zeros_like(dsb_ref)

            @pl.when(s_id < N)
            def _():
                def emit(sl, dq, dk, dv, dk_prev, dv_prev):
                    new[:, sl] = dq
                    new[:, slice(AW + sl.start, AW + sl.stop)] = dk
                    new[:, slice(2 * AW + sl.start, 2 * AW + sl.stop)] = dv
                    pp[:, sl] = dk_prev
                    pp[:, slice(AW + sl.start, AW + sl.stop)] = dv_prev

                heads(q_ref, kp_ref, kc_ref, vp_ref, vc_ref, do_ref, L_ref, dl_ref, ba_ref, bb_ref,
                      _prev_mask(s_id, H2, T), dsa_ref, dsb_ref, emit)

            @pl.when((s_id >= 1) & (s_id < N))
            def _():
                _st(out_ref, hold[:, 0:AW], slice(0, AW))
                late = _merge_rows(jnp.zeros((H2, 2 * AW), F32), pp[...], R)
                _st(out_ref, hold[:, AW:W3] + late, slice(AW, W3))

            @pl.when(s_id == N)
            def _():
                _st(out_ref, hold[...])

            @pl.when(s_id < N)
            def _():
                hold[...] = new[...]

        cur = lambda n: jnp.minimum(n, N - 1)
        prev = lambda n: jnp.clip(n - 1, 0, N - 1)
        delayed = lambda n: jnp.maximum(n - 1, 0)
        in_specs = [geom.spec(AW, cq, cur), geom.spec(AW, ck, prev), geom.spec(AW, ck, cur),
                    geom.spec(AW, cv, prev), geom.spec(AW, cv, cur),
                    geom.spec(AW, 0, cur), geom.spec(AW, 0, cur), geom.spec(AW, 0, cur)]
        args = [qv] * 5 + [geom.view(dattn), geom.view(Lb), geom.view(dlb)]
        out_spec = geom.spec(W3, gi, delayed)
        grid = geom.outer + (N + 1,)
        scratch = [pltpu.VMEM((T, W3), F32), pltpu.VMEM((T, W3), F32), pltpu.VMEM((H2, 2 * AW), F32)]
    else:
        def body(q_ref, kc_ref, vc_ref, do_ref, L_ref, dl_ref, ba_ref, bb_ref, prev_in, out_ref, dsa_ref, dsb_ref):
            del prev_in

            @pl.when(first_step())
            def _():
                dsa_ref[...] = jnp.zeros_like(dsa_ref)
                dsb_ref[...] = jnp.zeros_like(dsb_ref)

            def emit(sl, dq, dk, dv, dk_prev, dv_prev):
                _st(out_ref, dq, sl)
                _st(out_ref, dk, slice(AW + sl.start, AW + sl.stop))
                _st(out_ref, dv, slice(2 * AW + sl.start, 2 * AW + sl.stop))

            heads(q_ref, None, kc_ref, None, vc_ref, do_ref, L_ref, dl_ref, ba_ref, bb_ref, None, dsa_ref, dsb_ref,
                  emit)

        cur = lambda n: n
        in_specs = [geom.spec(AW, cq, cur), geom.spec(AW, ck, cur), geom.spec(AW, cv, cur),
                    geom.spec(AW, 0, cur), geom.spec(AW, 0, cur), geom.spec(AW, 0, cur)]
        args = [qv] * 3 + [geom.view(dattn), geom.view(Lb), geom.view(dlb)]
        out_spec = geom.spec(W3, gi, cur)
        grid = geom.outer
        scratch = []

    whole = lambda b: pl.BlockSpec(b.shape, lambda *ids: (0, 0, 0))
    biases = [b.reshape(NH // 2, 2 * H2, b.shape[2]) for b in biases]
    in_specs += [whole(b) for b in biases]
    in_specs.append(pl.BlockSpec(memory_space=pl.ANY))
    if dqkv_prev is None:
        last, aliases = biases[0], {}
    else:
        last, aliases = geom.view(dqkv_prev), {len(in_specs) - 1: 0}
    dq, dsa, dsb = pl.pallas_call(
        body, name=f"attn_bwd{gi}", grid=grid, in_specs=in_specs,
        out_specs=[out_spec] + [whole(b) for b in biases],
        out_shape=[_sds(geom.view_shape(QKV_W), CDT)] + [_sds(b.shape, F32) for b in biases],
        scratch_shapes=scratch,
        input_output_aliases=aliases,
        compiler_params=_cp(("arbitrary",) * len(grid), vmem=56),
    )(*args, *biases, last)
    return geom.unview(dq), [d.reshape(NH, H2, d.shape[2]) for d in (dsa, dsb)]


def _h_bwd(dh, dx2, x, norm_g, scale):
    S = x.shape[0]

    def body(dh_ref, dx2_ref, x_ref, g_ref, sc_ref, gx_ref, dsh_ref, dsc_ref, dng_ref, nat_ref):
        i = pl.program_id(0)

        @pl.when(i == 0)
        def _():
            dsh_ref[...] = jnp.zeros_like(dsh_ref)
            dsc_ref[...] = jnp.zeros_like(dsc_ref)
            dng_ref[...] = jnp.zeros_like(dng_ref)

        for r in range(R16):
            nat_ref[:, r, :] = dh_ref[r]
        dh = jnp.concatenate([nat_ref[l] for l in range(R16)], axis=0)
        g = g_ref[...]
        one_sc = 1.0 + sc_ref[...]
        xv = x_ref[...]
        r1 = lax.rsqrt(jnp.mean(xv * xv, axis=-1, keepdims=True) + EPS)
        xn = xv * r1
        dhx = dh * xn
        dxn = dh * (g * one_sc)
        dx1 = r1 * (dxn - xn * jnp.mean(dxn * xn, axis=-1, keepdims=True))
        gx_ref[...] = dx2_ref[...] + dx1
        dsh_ref[...] += jnp.sum(dh, axis=0, keepdims=True)
        dsc_ref[...] += jnp.sum(dhx * g, axis=0, keepdims=True)
        dng_ref[...] += jnp.sum(dhx * one_sc, axis=0, keepdims=True)

    nat = pl.BlockSpec((TOK, D), lambda i: (i, 0))
    vec = pl.BlockSpec((1, D), lambda i: (0, 0))
    return pl.pallas_call(
        body, name="h_bwd", grid=(S // TOK,),
        in_specs=[pl.BlockSpec((R16, R16, D), lambda i: (0, i, 0)), nat, nat, vec, vec],
        out_specs=[nat, vec, vec, vec],
        out_shape=[_sds((S, D), F32), _sds((1, D), F32), _sds((1, D), F32), _sds((1, D), F32)],
        scratch_shapes=[pltpu.VMEM((R16, R16, D), F32)],
        compiler_params=_cp(("arbitrary",)),
    )(dh, dx2, x, norm_g, scale)


def _local_step(x, target, shift, scale, gate, norm_g, w_in_full, pool_w, pool_scale, wab, wpb, wout, rel_bias,
                final_g):
    st = _local_pre(x, shift, scale, norm_g, rel_bias)
    st = _local_grads(st, target, gate, w_in_full, pool_w, pool_scale, wab, wpb, wout, final_g)
    d_rel_bias = _local_bias_grads(st)
    gx, dmod, dng = _local_input_grads(st, w_in_full, norm_g, scale)
    return dict(loss=st["loss"], grad_x=gx, dw_in=st["dw_in"], dw_ab=st["dw_ab"], dw_pb=st["dw_pb"],
                dw_out=st["dw_out"], d_norm_g=dng, d_pool_w=st["dpw"], d_pool_scale=st["dps"],
                d_rel_bias=d_rel_bias, d_final_g=st["gfg"], dmod=dmod)


def _local_pre(x, shift, scale, norm_g, rel_bias, dep=None):
    S = x.shape[0]
    LS = S // R16
    geoms = [_Geom(gi, LS) for gi in range(3)]
    idx_np = [[g.bucket_index(sub) for sub in range(2)] for g in geoms]
    buckets = [[tuple(int(b) for b in np.unique(ix) if b >= 0) for ix in pair] for pair in idx_np]
    idxs = [[jnp.asarray(ix) for ix in pair] for pair in idx_np]
    biases = [[_bias_table(idxs[gi][sub], buckets[gi][sub], rel_bias, gi, sub, dep) for sub in range(2)]
              for gi in range(3)]
    h = _norm_mod(x, norm_g, shift, scale, dep).reshape(S, D)
    return dict(geoms=geoms, idxs=idxs, buckets=buckets, biases=biases, x=x, h=h, ht=_tcast(h, "h_transpose"))


def _local_grads(st, target, gate, w_in_full, pool_w, pool_scale, wab, wpb, wout, final_g):
    geoms, idxs, biases, x, h, ht = (st[k] for k in ("geoms", "idxs", "biases", "x", "h", "ht"))
    S = h.shape[0]
    LS = S // R16
    qkv, rest = _proj(h, w_in_full)

    os_, ls_ = [], []
    for gi in range(3):
        o, l = _attn_fwd(qkv, biases[gi], geoms[gi])
        os_.append(o)
        ls_.append(l)
    attn, Lb, ag, agt = _attn_combine(os_, ls_, rest)
    pt, mp, pg, pgt = _pool_fwd(rest, pool_w, pool_scale)
    merged, mt = _branch_merge(ag, pg, wab, wpb, rest)
    dx2, dmo3, loss, gfg, dgate = _out_loss(merged.reshape(R16, LS, D), wout, x, target, gate, final_g)
    dmo = dmo3.reshape(S, D)

    dya, dyp, drest, dattn, dlb, dmp, dps = _bwd_gates(dmo, wout, wab, wpb, rest, ag, pg, attn, mp, pool_scale)
    dw_out = _mm_tn(mt, dmo, n_total=D, out_blk=lambda j: j, name="dw_out")
    dw_ab = _mm_tn(agt, dya, n_total=D, out_blk=lambda j: j, name="dw_attn_br")
    dw_pb = _mm_tn(pgt, dyp, n_total=D, out_blk=lambda j: j, name="dw_pool_br")
    drest, dpw = _pool_bwd(dmp, pt, pool_w, drest)

    dqkv = None
    dsums = []
    for gi in range(3):
        dqkv, dsum = _attn_bwd(qkv, dattn, Lb, dlb, biases[gi], geoms[gi], dqkv)
        dsums.append(dsum)

    dw_in = _mm_tn(ht, dqkv, n_total=8192, out_blk=lambda j: j, name="dw_in_qkv")
    dw_in = _mm_tn(ht, drest, n_total=8192, out_blk=_rest_blk, name="dw_in_rest", prev=dw_in)
    return dict(st, loss=loss[0, 0], dw_in=dw_in, dw_ab=dw_ab, dw_pb=dw_pb, dw_out=dw_out, dpw=dpw, dps=dps, gfg=gfg,
                dgate=dgate, dsums=dsums, dqkv=dqkv, drest=drest, dx2=dx2)


def _local_bias_grads(st, dep=None):
    drb = [_bias_grad(st["dsums"][gi][0], st["idxs"][gi][0], st["buckets"][gi][0], gi, 0, dep if gi == 0 else None) +
           _bias_grad(st["dsums"][gi][1], st["idxs"][gi][1], st["buckets"][gi][1], gi, 1) for gi in range(3)]
    return jnp.concatenate(drb, axis=1)


def _local_input_grads(st, w_in_full, norm_g, scale, dep=None):
    S = st["h"].shape[0]
    LS = S // R16
    dh = _dh(st["dqkv"], st["drest"], w_in_full, dep)
    gx, dsh, dsc, dng = _h_bwd(dh.reshape(R16, LS, D), st["dx2"], st["x"], norm_g, scale)
    return gx, jnp.concatenate([dsh, dsc, st["dgate"]], axis=1), dng


def _my_place():
    return lax.axis_index("x"), lax.axis_index("y"), lax.axis_index("c")


def _block_of(ref, axis, idx, width):
    if axis is None:
        return ref.at[idx]
    start = pl.multiple_of(idx * width, width)
    if axis == 0:
        return ref.at[pl.ds(start, width)]
    return ref.at[:, pl.ds(start, width)]


_SEM = pl.BlockSpec(memory_space=pltpu.SEMAPHORE)
_EFFECT = pltpu.SideEffectType.DATAFLOW_SIDE_EFFECTING


def _split_start(name, srcs, lands, ncopy, plan, dep=None):
    arrs = list(srcs) + list(lands)
    na, ns = len(arrs), len(srcs)
    dep_specs, dep_args = _after(dep)

    def body(*refs):
        send_sems, recv_sems, token = refs[na + len(dep_args)], refs[na + len(dep_args) + 1], refs[-1]
        for k, (s, d, dev) in enumerate(plan(refs[:ns], refs[ns:na])):
            pltpu.make_async_remote_copy(src_ref=s, dst_ref=d, send_sem=send_sems.at[k], recv_sem=recv_sems.at[k],
                                         device_id=dev, device_id_type=MESH).start()
        token[...] = jnp.zeros_like(token)

    outs = pl.pallas_call(
        body, name=name,
        out_shape=(pltpu.SemaphoreType.DMA((ncopy,)), pltpu.SemaphoreType.DMA((ncopy,)),
                   *[pltpu.HBM(a.shape, a.dtype) for a in arrs], _sds((8, 128), F32)),
        in_specs=[_HBM] * na + dep_specs,
        out_specs=(_SEM, _SEM, *[_HBM] * na, pl.BlockSpec(memory_space=pltpu.VMEM)),
        input_output_aliases={i: 2 + i for i in range(na)},
        compiler_params=pltpu.CompilerParams(has_side_effects=_EFFECT),
    )(*[pltpu.with_memory_space_constraint(a, pltpu.HBM) for a in arrs], *dep_args)
    return (outs[0], outs[1]), list(outs[2:2 + ns]), list(outs[2 + ns:2 + na]), outs[-1]


def _split_wait(name, sems, srcs, lands, plan, after):
    arrs = list(srcs) + list(lands)
    na, ns = len(arrs), len(srcs)
    after = list(after) if isinstance(after, (list, tuple)) else [after]

    def body(*refs):
        send_sems, recv_sems = refs[na], refs[na + 1]
        for k, (s, d, dev) in enumerate(plan(refs[:ns], refs[ns:na])):
            cp = pltpu.make_async_remote_copy(src_ref=s, dst_ref=d, send_sem=send_sems.at[k], recv_sem=recv_sems.at[k],
                                              device_id=dev, device_id_type=MESH)
            cp.wait_send()
            cp.wait_recv()

    outs = pl.pallas_call(
        body, name=name, out_shape=[pltpu.HBM(a.shape, a.dtype) for a in arrs],
        in_specs=[_HBM] * na + [_SEM, _SEM] + [pl.BlockSpec(memory_space=pl.ANY)] * len(after),
        out_specs=[_HBM] * na,
        input_output_aliases={i: i for i in range(na)},
        compiler_params=pltpu.CompilerParams(has_side_effects=_EFFECT),
    )(*arrs, sems[0], sems[1], *after)
    return list(outs[:ns]), list(outs[ns:])


def _gather_chips_plan(layout):
    def plan(src, land):
        x, y, c = _my_place()
        me = 4 * x + 2 * y + c
        mine = [_block_of(land[p], layout[p][0], me, layout[p][1]) for p in range(len(land))]
        return [(mine[p], mine[p], (cx, cy, c)) for p in range(len(land)) for cx, cy in [(1 - x, y), (x, 1 - y)]]
    return plan


def _pair_plan(layout):
    def plan(src, land):
        x, y, c = _my_place()
        n = len(layout)
        return [(_block_of(src[p], layout[p][0], 2 * q + (1 - c), layout[p][1]), land[p].at[q], (x, y, 1 - c))
                for p in range(n) for q in range(4)] + [(src[p], land[p], (x, y, 1 - c)) for p in range(n, len(src))]
    return plan


def _chips_plan(n):
    def plan(src, land):
        x, y, c = _my_place()
        chips = [(1 - x, y), (x, 1 - y), (1 - x, 1 - y)]
        return [(src[p].at[2 * cx + cy], land[p].at[2 * x + y], (cx, cy, c)) for p in range(n) for cx, cy in chips] + \
               [(src[p], land[p].at[2 * x + y], (cx, cy, c)) for p in range(n, len(src)) for cx, cy in chips]
    return plan


def _pair_share(gathered):
    n = len(gathered)

    def body(*refs):
        land = refs[n:2 * n]
        send_sems, recv_sems, fwd_send, fwd_recv = refs[2 * n:]
        x, y, c = _my_place()
        k_dg = 3 - (2 * x + y)
        south = c == 0
        pass_on = 4 * jnp.where(south, 1 - x, x) + 2 * jnp.where(south, y, 1 - y) + c
        to = (jnp.where(south, x, 1 - x), jnp.where(south, 1 - y, y), c)

        def forward(p):
            return pltpu.make_async_remote_copy(
                src_ref=land[p].at[pass_on], dst_ref=land[p].at[pass_on], send_sem=fwd_send.at[p],
                recv_sem=fwd_recv.at[p], device_id=to, device_id_type=MESH)

        def to_sibling(p, k):
            blk = land[p].at[2 * k + c]
            return pltpu.make_async_remote_copy(
                src_ref=blk, dst_ref=blk, send_sem=send_sems.at[4 * p + k], recv_sem=recv_sems.at[4 * p + k],
                device_id=(x, y, 1 - c), device_id_type=MESH)

        for p in range(n):
            forward(p).start()
        for p in range(n):
            for k in range(4):
                @pl.when(k != k_dg)
                def _():
                    to_sibling(p, k).start()
        for p in range(n):
            dg = land[p].at[2 * k_dg + c]
            pltpu.make_async_remote_copy(src_ref=dg, dst_ref=dg, send_sem=fwd_send.at[p], recv_sem=fwd_recv.at[p],
                                         device_id=to, device_id_type=MESH).wait_recv()
            for k in range(4):
                @pl.when(k == k_dg)
                def _():
                    to_sibling(p, k).start()
        for p in range(n):
            forward(p).wait_send()
            for k in range(4):
                to_sibling(p, k).wait_send()
                blk = land[p].at[2 * k + 1 - c]
                pltpu.make_async_remote_copy(
                    src_ref=blk, dst_ref=blk, send_sem=send_sems.at[4 * p + k], recv_sem=recv_sems.at[4 * p + k],
                    device_id=(x, y, 1 - c), device_id_type=MESH).wait_recv()

    return pl.pallas_call(
        body, name="gather_pair_share", in_specs=[_HBM] * n, out_specs=[_HBM] * n,
        out_shape=[_sds(g.shape, g.dtype) for g in gathered],
        input_output_aliases={p: p for p in range(n)},
        scratch_shapes=[pltpu.SemaphoreType.DMA((4 * n,)), pltpu.SemaphoreType.DMA((4 * n,)),
                        pltpu.SemaphoreType.DMA((n,)), pltpu.SemaphoreType.DMA((n,))],
    )(*gathered)


def _place_slot(arrs, slot, n_slots, name, out_dtype=None):
    n = len(arrs)

    def body(s_ref, *refs):
        for p in range(n):
            refs[n + p][...] = refs[p][...].astype(refs[n + p].dtype)

    def spec(a):
        tail = a.shape[-2:]
        return pl.BlockSpec((None,) + tail, lambda i, s_ref: (s_ref[0], 0, 0))

    in_specs = [spec(a) if a.ndim == 3 else pl.BlockSpec(a.shape, lambda i, s_ref: (0, 0)) for a in arrs]
    return pl.pallas_call(
        body, name=name,
        grid_spec=pltpu.PrefetchScalarGridSpec(num_scalar_prefetch=1, grid=(1,), in_specs=in_specs,
                                               out_specs=[spec(a) for a in arrs]),
        out_shape=[_sds((n_slots,) + a.shape[-2:], out_dtype or a.dtype) for a in arrs],
        compiler_params=_cp(("arbitrary",)),
    )(slot, *arrs)


def _cols_from_blocks(w8, name):
    _, R, C = w8.shape

    def body(w_ref, o_ref):
        for e in range(8):
            o_ref[:, C * e:C * (e + 1)] = w_ref[e]

    return pl.pallas_call(body, name=name, out_shape=_sds((R, 8 * C), w8.dtype), compiler_params=_cp())(w8)


def _all_gather_small(srcs, name):
    n = len(srcs)
    vmem = pl.BlockSpec(memory_space=pltpu.VMEM)

    def body(*refs):
        src, out = refs[:n], refs[n:2 * n]
        send_sems, recv_sems = refs[2 * n:]
        x, y, c = _my_place()
        me = 4 * x + 2 * y + c
        flip = lambda v, f: 1 - v if f else v
        peers = [(flip(x, fx), flip(y, fy), flip(c, fc)) for fx in (0, 1) for fy in (0, 1) for fc in (0, 1)][1:]
        cps = []
        for p in range(n):
            out[p][me] = src[p][...]
            for k, peer in enumerate(peers):
                cps.append(pltpu.make_async_remote_copy(
                    src_ref=src[p], dst_ref=out[p].at[me], send_sem=send_sems.at[7 * p + k],
                    recv_sem=recv_sems.at[7 * p + k], device_id=peer, device_id_type=MESH))
        for cp in cps:
            cp.start()
        for cp in cps:
            cp.wait()

    return pl.pallas_call(
        body, name=name, in_specs=[vmem] * n, out_specs=[vmem] * n,
        out_shape=[_sds((8,) + a.shape, a.dtype) for a in srcs],
        scratch_shapes=[pltpu.SemaphoreType.DMA((7 * n,)), pltpu.SemaphoreType.DMA((7 * n,))],
    )(*srcs)


def _adaln_exchange(c, w_ada, b_part):
    vmem = pl.BlockSpec(memory_space=pltpu.VMEM)

    def body(c_ref, w_ref, b_ref, call_ref, mod_ref, part_ref, send_sems, recv_sems):
        x, y, c = _my_place()
        me = 4 * x + 2 * y + c
        flip = lambda v, f: 1 - v if f else v
        peers = [(flip(x, fx), flip(y, fy), flip(c, fc)) for fx in (0, 1) for fy in (0, 1) for fc in (0, 1)][1:]
        call_ref[me] = c_ref[...]
        first = [pltpu.make_async_remote_copy(src_ref=c_ref, dst_ref=call_ref.at[me], send_sem=send_sems.at[k],
                                              recv_sem=recv_sems.at[k], device_id=peer, device_id_type=MESH)
                 for k, peer in enumerate(peers)]
        for cp in first:
            cp.start()
        for cp in first:
            cp.wait()
        c_all = jnp.concatenate([call_ref[j] for j in range(8)], axis=0)
        part_ref[...] = _dot(c_all.astype(CDT), w_ref[...].astype(CDT), NN) + b_ref[...]
        mod_ref[me] = part_ref[pl.ds(me, 1), :]
        second = [pltpu.make_async_remote_copy(
            src_ref=part_ref.at[pl.ds(4 * px + 2 * py + pc, 1)], dst_ref=mod_ref.at[me], send_sem=send_sems.at[7 + k],
            recv_sem=recv_sems.at[7 + k], device_id=(px, py, pc), device_id_type=MESH)
            for k, (px, py, pc) in enumerate(peers)]
        for cp in second:
            cp.start()
        for cp in second:
            cp.wait()

    return pl.pallas_call(
        body, name="adaln_exchange", in_specs=[vmem] * 3, out_specs=[vmem] * 2,
        out_shape=[_sds((8, 1, D), F32), _sds((8, 1, w_ada.shape[1]), F32)],
        scratch_shapes=[pltpu.VMEM((8, w_ada.shape[1]), F32), pltpu.SemaphoreType.DMA((14,)),
                        pltpu.SemaphoreType.DMA((14,))],
        compiler_params=_cp(),
    )(c, w_ada, b_part)


def _pair_add(dw, recv, axis, width, cidx, name):
    if axis == 0:
        tr = min(width, 256)
        C = dw.shape[1]
        grid = (4, width // tr)
        nb = width // tr
        own = pl.BlockSpec((tr, C), lambda q, i, c_ref: ((2 * q + c_ref[0]) * nb + i, 0))
        oth = pl.BlockSpec((None, tr, C), lambda q, i, c_ref: (q, i, 0))
    else:
        R = dw.shape[0]
        tr = min(R, 256)
        grid = (4, R // tr)
        own = pl.BlockSpec((tr, width), lambda q, i, c_ref: (i, 2 * q + c_ref[0]))
        oth = pl.BlockSpec((None, tr, width), lambda q, i, c_ref: (q, i, 0))

    def body(c_ref, a_ref, b_ref, o_ref):
        o_ref[...] = (a_ref[...] + b_ref[...]).astype(o_ref.dtype)

    return pl.pallas_call(
        body, name=name,
        grid_spec=pltpu.PrefetchScalarGridSpec(num_scalar_prefetch=1, grid=grid, in_specs=[own, oth], out_specs=oth),
        out_shape=_sds(recv.shape, CDT), compiler_params=_cp(("parallel", "parallel")),
    )(cidx, dw, recv)


def _add_pairs(xs, ys, name):
    n = len(xs)

    def body(*refs):
        for p in range(n):
            refs[2 * n + p][...] = refs[p][...] + refs[n + p][...]

    return pl.pallas_call(body, name=name, out_shape=[_sds(a.shape, F32) for a in xs])(*xs, *ys)


def _adam_math(w, g, m, v):
    m = ADAM_B1 * m + (1.0 - ADAM_B1) * g
    v = ADAM_B2 * v + (1.0 - ADAM_B2) * (g * g)
    m_hat = m / (1.0 - ADAM_B1 ** ADAM_STEP)
    v_hat = v / (1.0 - ADAM_B2 ** ADAM_STEP)
    delta = -ADAM_LR * (m_hat / (jnp.sqrt(v_hat) + ADAM_EPS) + ADAM_WD * w)
    return delta, m, v


def _adam(gparts, w, m, v, name):
    P, R, C = gparts.shape
    tr = R if R <= 256 else 256

    def body(g_ref, w_ref, m_ref, v_ref, go_ref, d_ref, mo_ref, vo_ref):
        g = g_ref[0].astype(F32)
        for i in range(1, P):
            g = g + g_ref[i].astype(F32)
        d, mn, vn = _adam_math(w_ref[...], g, m_ref[...], v_ref[...])
        go_ref[...] = g
        d_ref[...] = d
        mo_ref[...] = mn
        vo_ref[...] = vn

    t = pl.BlockSpec((tr, C), lambda i: (i, 0))
    return pl.pallas_call(
        body, name=name, grid=(R // tr,),
        in_specs=[pl.BlockSpec((P, tr, C), lambda i: (0, i, 0)), t, t, t], out_specs=[t, t, t, t],
        out_shape=[_sds((R, C), F32)] * 4, compiler_params=_cp(("parallel",)),
    )(gparts, w, m, v)


def _sum_parts(gparts, name):
    P, R, C = gparts.shape

    def body(g_ref, o_ref):
        g = g_ref[0]
        for i in range(1, P):
            g = g + g_ref[i]
        o_ref[...] = g

    return pl.pallas_call(body, name=name, out_shape=_sds((R, C), F32))(gparts)


def _adam_w_ada(c_t, dm, w, m, v):
    R, C = w.shape

    def body(c_ref, dm_ref, w_ref, m_ref, v_ref, go_ref, d_ref, mo_ref, vo_ref):
        g = c_ref[:, 0:1] * dm_ref[0:1, :]
        for b in range(1, 8):
            g = g + c_ref[:, b:b + 1] * dm_ref[b:b + 1, :]
        d, mn, vn = _adam_math(w_ref[...], g, m_ref[...], v_ref[...])
        go_ref[...] = g
        d_ref[...] = d
        mo_ref[...] = mn
        vo_ref[...] = vn

    return pl.pallas_call(body, name="adam_w_ada", out_shape=[_sds((R, C), F32)] * 4,
                          compiler_params=_cp())(c_t, dm, w, m, v)


def kernel(x, c, norm_g, w_ada, b_ada, w_in, pool_w, pool_scale, w_attn_br, w_pool_br, w_out, rel_bias, final_g, loss_target, m_norm_g, m_w_ada, m_b_ada, m_w_in, m_pool_w, m_pool_scale, m_w_attn_br, m_w_pool_br, m_w_out, m_rel_bias, m_final_g, v_norm_g, v_w_ada, v_b_ada, v_w_in, v_pool_w, v_pool_scale, v_w_attn_br, v_w_pool_br, v_w_out, v_rel_bias, v_final_g):
    S = x.shape[1]
    px, py, pc = _my_place()
    me = 4 * px + 2 * py + pc
    cidx = jnp.reshape(pc, (1,)).astype(jnp.int32)

    layout = [(1, 1024), (1, 128), (1, 128), (0, 128)]
    final_g2 = final_g.reshape(1, D)
    me1 = jnp.reshape(me, (1,)).astype(jnp.int32)

    b_part = lax.dynamic_slice(b_ada, (0, me * 384), (1, 384))
    c_all, mod_all = _adaln_exchange(c, w_ada[0], b_part)
    c_all = c_all.reshape(8, D)
    mod = mod_all.reshape(1, 3 * D)
    shift, scale, gate = mod[:, :D], mod[:, D:2 * D], mod[:, 2 * D:]

    gplan = _gather_chips_plan([(None, 1)] * 4)
    lands = _place_slot([w_in[0], w_attn_br[0], w_pool_br[0], w_out[0]], me1, 8, "place_own_shards", CDT)
    g_sems, _, lands, tok = _split_start("gather_chips_start", [], lands, 8, gplan, dep=mod_all)
    st = _local_pre(x[0], shift, scale, norm_g, rel_bias, dep=tok)
    _, lands = _split_wait("gather_chips_wait", g_sems, [], lands, gplan,
                           [st["ht"]] + [b for pair in st["biases"] for b in pair])
    w_in_full, wab8, wpb8, wout8 = _pair_share(lands)
    wab, wpb = _cols_from_blocks(wab8, "w_attn_br_cols"), _cols_from_blocks(wpb8, "w_pool_br_cols")
    wout = wout8.reshape(D, D)

    st = _local_grads(st, loss_target[0], gate, w_in_full, pool_w[0], pool_scale, wab, wpb, wout, final_g2)

    early = [st["dpw"].reshape(AW, 128), st["dps"], st["gfg"], jnp.broadcast_to(st["loss"], (1, 128))]
    dws = [st["dw_in"], st["dw_ab"], st["dw_pb"], st["dw_out"]]
    pplan = _pair_plan(layout)
    pair_shapes = [(4, w, a.shape[1]) if ax == 0 else (4, a.shape[0], w) for a, (ax, w) in zip(dws, layout)]
    p_sems, srcs, recv, tok = _split_start(
        "reduce_pair_start", dws + early, [lax.empty(s, F32) for s in pair_shapes + [a.shape for a in early]], 20,
        pplan)
    d_rel_bias = _local_bias_grads(st, dep=tok)
    srcs, recv = _split_wait("reduce_pair_wait", p_sems, srcs, recv, pplan, d_rel_bias)
    parts = [_pair_add(dw, r, ax, w, cidx, f"pair_add{i}")
             for i, (dw, r, (ax, w)) in enumerate(zip(srcs[:4], recv[:4], layout))]
    parts += _add_pairs(srcs[4:], recv[4:], "pair_add_smalls")

    slot1 = jnp.reshape(2 * px + py, (1,)).astype(jnp.int32)
    cplan = _chips_plan(4)
    c_sems, parts, lands, tok = _split_start("reduce_chips_start", parts,
                                             _place_slot(parts, slot1, 4, "place_own_partials"), 24, cplan)
    grad_x, dmod, d_norm_g = _local_input_grads(st, w_in_full, norm_g, scale, dep=tok)

    dmod_all, ng_all, rb_all = _all_gather_small([dmod, d_norm_g, d_rel_bias], "gather_late")
    o_norm_g = _adam(ng_all, norm_g, m_norm_g, v_norm_g, "adam_norm_g")
    o_rel_bias = _adam(rb_all, rel_bias, m_rel_bias, v_rel_bias, "adam_rel_bias")
    o_b_ada = _adam(dmod_all, b_ada, m_b_ada, v_b_ada, "adam_b_ada")
    dm_mine = lax.dynamic_slice(dmod_all.reshape(8, 3 * D), (0, me * 384), (8, 384))
    o_w_ada = [o[None] for o in _adam_w_ada(c_all.T, dm_mine, w_ada[0], m_w_ada[0], v_w_ada[0])]

    _, (g_in, g_ab, g_pb, g_out, g_pw, g_ps, g_fg, g_loss) = _split_wait(
        "reduce_chips_wait", c_sems, parts, lands, cplan, [o_w_ada[0], o_b_ada[0], o_norm_g[0], o_rel_bias[0]])
    o_w_in = [o[None] for o in _adam(g_in, w_in[0], m_w_in[0], v_w_in[0], "adam_w_in")]
    o_w_ab = [o[None] for o in _adam(g_ab, w_attn_br[0], m_w_attn_br[0], v_w_attn_br[0], "adam_w_attn_br")]
    o_w_pb = [o[None] for o in _adam(g_pb, w_pool_br[0], m_w_pool_br[0], v_w_pool_br[0], "adam_w_pool_br")]
    o_w_out = [o[None] for o in _adam(g_out, w_out[0], m_w_out[0], v_w_out[0], "adam_w_out")]
    o_pool_w = [o.reshape(pool_w.shape) for o in _adam(g_pw, pool_w.reshape(AW, 128), m_pool_w.reshape(AW, 128),
                                                       v_pool_w.reshape(AW, 128), "adam_pool_w")]
    o_pool_scale = _adam(g_ps, pool_scale, m_pool_scale, v_pool_scale, "adam_pool_scale")
    o_final_g = [o.reshape(D) for o in _adam(g_fg, final_g2, m_final_g.reshape(1, D), v_final_g.reshape(1, D),
                                             "adam_final_g")]
    loss = _sum_parts(g_loss, "sum_loss")[0, 0]

    per_w = [o_norm_g, o_w_ada, o_b_ada, o_w_in, o_pool_w, o_pool_scale, o_w_ab, o_w_pb, o_w_out, o_rel_bias, o_final_g]
    outs = [loss, grad_x[None]]
    for k in range(4):
        outs += [o[k] for o in per_w]
    return tuple(outs)
```

```python
import functools
import math

import numpy as np
import jax
import jax.numpy as jnp
from jax import lax
from jax.experimental import pallas as pl
from jax.experimental.pallas import tpu as pltpu

F32 = jnp.float32
CDT = jnp.bfloat16
D = 1024
HD = 64
NH = 8
AW = 512
QKV_W = 4608
REST_W = 3584
R16 = 16
EPS = 1e-6
NEG = -1e30
N_BACK = 128
ADAM_LR, ADAM_B1, ADAM_B2, ADAM_EPS, ADAM_WD, ADAM_STEP = 0.001, 0.9, 0.999, 1e-08, 0.01, 10
MESH = pl.DeviceIdType.MESH
VMEM_MB = 1024 * 1024


def _cp(sem=None, vmem=48, **kw):
    if sem is not None:
        kw["dimension_semantics"] = sem
    return pltpu.CompilerParams(vmem_limit_bytes=vmem * VMEM_MB, **kw)


def _sds(shape, dtype):
    return jax.ShapeDtypeStruct(tuple(shape), dtype)


def _rest_blk(jj):
    return jnp.where(jj == 0, 9, jnp.where(jj == 6, 10, jj + 10))


def _sigmoid(z):
    return 0.5 * jnp.tanh(0.5 * z) + 0.5


def _dot(a, b, dims):
    return lax.dot_general(a, b, (dims, ((), ())), preferred_element_type=F32)


NN = ((1,), (0,))
NT = ((1,), (1,))
TN = ((0,), (0,))


def _ld(ref):
    if len(ref.shape) == 2:
        return ref[...]
    return jnp.concatenate([ref[a] for a in range(ref.shape[0])], axis=0)


def _st(ref, val, cols=None):
    val = val.astype(ref.dtype)
    if len(ref.shape) == 2:
        if cols is None:
            ref[...] = val
        else:
            ref[:, cols] = val
        return
    rows = ref.shape[1]
    for a in range(ref.shape[0]):
        if cols is None:
            ref[a] = val[a * rows:(a + 1) * rows]
        else:
            ref[a, :, cols] = val[a * rows:(a + 1) * rows]


REST_COLS = ((9, 10), (11, 16), (10, 11))
_HBM = pl.BlockSpec(memory_space=pltpu.HBM)


def _resident(shape):
    return pl.BlockSpec(shape, lambda *ids: (0,) * len(shape), pipeline_mode=pl.Buffered(1))


def _w512(w_ref, j):
    return w_ref[j // 2, :, 512 * (j % 2):512 * (j % 2 + 1)]


def _proj(h, w):
    S = h.shape[0]
    tm = 512

    def body(a_ref, w_ref, q_ref, r_ref):
        a = a_ref[...]
        for j in range(9):
            q_ref[:, 512 * j:512 * (j + 1)] = _dot(a, _w512(w_ref, j), NN).astype(q_ref.dtype)
        off = 0
        for lo, hi in REST_COLS:
            for j in range(lo, hi):
                r_ref[:, off:off + 512] = _dot(a, _w512(w_ref, j), NN)
                off += 512

    return pl.pallas_call(
        body, name="proj", grid=(S // tm,),
        in_specs=[pl.BlockSpec((tm, D), lambda i: (i, 0)), _resident((8, D, 1024))],
        out_specs=[pl.BlockSpec((tm, QKV_W), lambda i: (i, 0)), pl.BlockSpec((tm, REST_W), lambda i: (i, 0))],
        out_shape=[_sds((S, QKV_W), CDT), _sds((S, REST_W), F32)],
        compiler_params=_cp(("parallel",), vmem=56),
    )(h, w)


def _after(dep):
    return ([], []) if dep is None else ([pl.BlockSpec(memory_space=pl.ANY)], [dep])


def _dh(dqkv, drest, w, dep=None):
    S = dqkv.shape[0]
    tm = 512
    dep_specs, dep_args = _after(dep)

    def body(a_ref, b_ref, w_ref, *rest):
        o_ref = rest[-1]
        acc = _dot(a_ref[:, 0:512], _w512(w_ref, 0), NT)
        for j in range(1, 9):
            acc = acc + _dot(a_ref[:, 512 * j:512 * (j + 1)], _w512(w_ref, j), NT)
        off = 0
        for lo, hi in REST_COLS:
            for j in range(lo, hi):
                acc = acc + _dot(b_ref[:, off:off + 512], _w512(w_ref, j), NT)
                off += 512
        o_ref[...] = acc

    return pl.pallas_call(
        body, name="dh", grid=(S // tm,),
        in_specs=[pl.BlockSpec((tm, QKV_W), lambda i: (i, 0)), pl.BlockSpec((tm, REST_W), lambda i: (i, 0)),
                  _resident((8, D, 1024))] + dep_specs,
        out_specs=pl.BlockSpec((tm, D), lambda i: (i, 0)),
        out_shape=_sds((S, D), F32),
        compiler_params=_cp(("parallel",), vmem=56),
    )(dqkv, drest, w, *dep_args)


def _mm_tn(at, b, *, n_total, out_blk, name, prev=None, tn=512):
    R, S = at.shape
    nb = b.shape[1] // tn

    def body(*refs):
        refs[-1][...] = _dot(refs[0][...], refs[1][...], NN)

    in_specs = [_resident((R, S)), pl.BlockSpec((S, tn), lambda j: (0, j))]
    args = [at, b]
    aliases = {}
    if prev is not None:
        in_specs.append(pl.BlockSpec(memory_space=pl.ANY))
        args.append(prev)
        aliases = {2: 0}
    return pl.pallas_call(
        body, name=name, grid=(nb,), in_specs=in_specs,
        out_specs=pl.BlockSpec((R, tn), lambda j: (0, out_blk(j))),
        out_shape=_sds((R, n_total), F32),
        input_output_aliases=aliases,
        compiler_params=_cp(("parallel",)),
    )(*args)


def _tcast(a, name):
    M, C = a.shape
    tm = 512

    def body(a_ref, o_ref):
        o_ref[...] = a_ref[...].astype(F32).T.astype(o_ref.dtype)

    return pl.pallas_call(
        body, name=name, grid=(M // tm,),
        in_specs=[pl.BlockSpec((tm, C), lambda i: (i, 0))],
        out_specs=pl.BlockSpec((C, tm), lambda i: (0, i)),
        out_shape=_sds((C, M), CDT), compiler_params=_cp(("parallel",)),
    )(a)


TOK = R16 * R16


def _swap16():
    i = lax.broadcasted_iota(jnp.int32, (TOK, TOK), 0)
    j = lax.broadcasted_iota(jnp.int32, (TOK, TOK), 1)
    return jnp.where(j == jnp.bitwise_and(i, 15) * 16 + jnp.right_shift(i, 4), 1.0, 0.0).astype(CDT)


def _norm_mod(x, norm_g, shift, scale, dep=None):
    S = x.shape[0]
    dep_specs, dep_args = _after(dep)

    def body(x_ref, g_ref, sh_ref, sc_ref, *rest):
        xv = x_ref[...]
        xn = xv * lax.rsqrt(jnp.mean(xv * xv, axis=-1, keepdims=True) + EPS)
        h = (xn * (g_ref[...] * (1.0 + sc_ref[...])) + sh_ref[...]).astype(CDT)
        _st(rest[-1], _dot(_swap16(), h, NN))

    vec = pl.BlockSpec((1, D), lambda i: (0, 0))
    return pl.pallas_call(
        body, name="norm_mod", grid=(S // TOK,),
        in_specs=[pl.BlockSpec((TOK, D), lambda i: (i, 0)), vec, vec, vec] + dep_specs,
        out_specs=pl.BlockSpec((R16, R16, D), lambda i: (0, i, 0)),
        out_shape=_sds((R16, S // R16, D), CDT), compiler_params=_cp(("parallel",)),
    )(x, norm_g, shift, scale, *dep_args)


def _t5_bucket_np(n):
    nf = np.maximum(n, 1).astype(np.float32)
    large = 16 + (np.log(nf / np.float32(16)) / np.float32(math.log(2048 / 16)) * np.float32(16)).astype(np.int32)
    large = np.minimum(large, 31)
    return np.where(n < 16, n, large).astype(np.int32)


class _Geom:
    def __init__(self, gi, LS):
        self.gi, self.LS = gi, LS
        self.dil = (1, 4, 16)[gi]
        if gi == 0:
            self.lead, self.rows, self.n_tiles, self.has_prev = (R16,), 16, LS // 16, True
            a = np.arange(R16)[:, None]
            ll = np.arange(16)[None, :]
            self.pos = (16 * ll + a).reshape(-1)
        elif gi == 1:
            self.lead, self.rows, self.n_tiles, self.has_prev = (4,), 64, LS // 64, True
            a = np.arange(4)[:, None]
            ll = np.arange(64)[None, :]
            self.pos = (4 * ll + a).reshape(-1)
        else:
            self.lead, self.rows, self.n_tiles, self.has_prev = (), LS, 1, False
            self.pos = np.arange(LS)
        self.T = self.pos.shape[0]
        self.Tk = 2 * self.T if self.has_prev else self.T
        self.outer = {0: (), 1: (4,), 2: (R16,)}[gi]

    def view(self, arr):
        S, C = arr.shape
        if self.gi == 1:
            return arr.reshape(4, 4, self.LS, C)
        return arr.reshape(R16, self.LS, C)

    def view_shape(self, C):
        return (4, 4, self.LS, C) if self.gi == 1 else (R16, self.LS, C)

    def unview(self, arr):
        return arr.reshape(R16 * self.LS, arr.shape[-1])

    def spec(self, W, cblk, tile):
        if self.gi == 0:
            return pl.BlockSpec((R16, 16, W), lambda n: (0, tile(n), cblk))
        if self.gi == 1:
            return pl.BlockSpec((4, None, 64, W), lambda r, n: (0, r, tile(n), cblk))
        return pl.BlockSpec((None, self.LS, W), lambda r: (r, 0, cblk))

    def halves(self):
        R = self.rows
        idx = np.arange(self.T).reshape(-1, R)
        return idx[:, :R // 2].reshape(-1), idx[:, R // 2:].reshape(-1)

    def bucket_index(self, sub):
        lo, hi = self.halves()
        pos = self.pos
        if sub == 0:
            pq = pos[lo]
            pk = np.concatenate([pos[hi] - self.T, pos[lo]]) if self.has_prev else np.concatenate([pos[lo], pos[hi]])
        else:
            pq = pos[hi]
            pk = np.concatenate([pos[lo], pos[hi]])
        dist = pq[:, None] - pk[None, :]
        ok = (dist >= 0) & (dist <= N_BACK)
        bucket = _t5_bucket_np(np.clip(dist, 0, N_BACK) * self.dil)
        return np.where(ok, bucket, -1).astype(np.int32)


def _bias_table(idx, buckets, rel_bias, gi, sub, dep=None):
    T, Tk = idx.shape
    tr = 32
    dep_specs, dep_args = _after(dep)

    def body(idx_ref, rb_ref, *rest):
        h = pl.program_id(0)
        ix = idx_ref[...]
        acc = jnp.full(ix.shape, NEG, F32)
        for b in buckets:
            acc = jnp.where(ix == b, rb_ref[b, gi * NH + h], acc)
        rest[-1][...] = acc

    return pl.pallas_call(
        body, name=f"bias_table{gi}{'ab'[sub]}", grid=(NH, T // tr),
        in_specs=[pl.BlockSpec((tr, Tk), lambda h, i: (i, 0)), pl.BlockSpec(memory_space=pltpu.SMEM)] + dep_specs,
        out_specs=pl.BlockSpec((None, tr, Tk), lambda h, i: (h, i, 0)),
        out_shape=_sds((NH, T, Tk), F32), compiler_params=_cp(("parallel", "parallel")),
    )(idx, rel_bias, *dep_args)


def _bias_grad(dsum, idx, buckets, gi, sub, dep=None):
    _, T, Tk = dsum.shape
    dep_specs, dep_args = _after(dep)

    def body(ds_ref, idx_ref, *rest):
        o_ref, r_ref = rest[-2], rest[-1]
        r_ref[...] = jnp.zeros_like(r_ref)
        ix, ds = idx_ref[...], ds_ref[...]
        for b in buckets:
            r_ref[b:b + 1, :] = jnp.sum(jnp.where(ix == b, ds, 0.0), axis=0, keepdims=True)
        o_ref[...] = jnp.broadcast_to(jnp.sum(r_ref[...], axis=1, keepdims=True), (32, 128))

    out = pl.pallas_call(
        body, name=f"bias_grad{gi}{'ab'[sub]}", grid=(NH,),
        in_specs=[pl.BlockSpec((None, T, Tk), lambda h: (h, 0, 0)), pl.BlockSpec((T, Tk), lambda h: (0, 0))] + dep_specs,
        out_specs=pl.BlockSpec((None, 32, 128), lambda h: (h, 0, 0)),
        out_shape=_sds((NH, 32, 128), F32), scratch_shapes=[pltpu.VMEM((32, Tk), F32)],
        compiler_params=_cp(("parallel",)),
    )(dsum, idx, *dep_args)
    return out[:, :, 0].T


def _prev_mask(n, T, Tk):
    col = lax.broadcasted_iota(jnp.int32, (1, Tk), 1)
    return jnp.where((col < T) & (n == 0), NEG, 0.0).astype(F32)


def _split_rows(v, R):
    A, half = v.shape[0] // R, R // 2
    via_f32 = v.dtype != F32 and half % 16 != 0
    w = v.astype(F32) if via_f32 else v
    lo = jnp.concatenate([w[a * R:a * R + half] for a in range(A)], axis=0)
    hi = jnp.concatenate([w[a * R + half:(a + 1) * R] for a in range(A)], axis=0)
    return (lo.astype(v.dtype), hi.astype(v.dtype)) if via_f32 else (lo, hi)


def _merge_rows(lo, hi, R):
    half = R // 2
    A = lo.shape[0] // half
    return jnp.concatenate([x[a * half:(a + 1) * half] for a in range(A) for x in (lo, hi)], axis=0)


def _attn_fwd(qkv, biases, geom):
    S = qkv.shape[0]
    gi, T, R, has_prev = geom.gi, geom.T, geom.rows, geom.has_prev
    H2 = T // 2
    qv = geom.view(qkv)
    cq, ck, cv = 3 * gi, 3 * gi + 1, 3 * gi + 2

    def body(*refs):
        if has_prev:
            q_ref, kp_ref, kc_ref, vp_ref, vc_ref, ba_ref, bb_ref, o_ref, l_ref = refs
        else:
            q_ref, kc_ref, vc_ref, ba_ref, bb_ref, o_ref, l_ref = refs
        q_lo, q_hi = _split_rows(_ld(q_ref), R)
        kc_lo, kc_hi = _split_rows(_ld(kc_ref), R)
        vc_lo, vc_hi = _split_rows(_ld(vc_ref), R)
        k_b = jnp.concatenate([kc_lo, kc_hi], axis=0)
        v_b = jnp.concatenate([vc_lo, vc_hi], axis=0)
        if has_prev:
            n = pl.program_id(len(geom.outer))
            k_a = jnp.concatenate([_split_rows(_ld(kp_ref), R)[1], kc_lo], axis=0)
            v_a = jnp.concatenate([_split_rows(_ld(vp_ref), R)[1], vc_lo], axis=0)
            pm = _prev_mask(n, H2, T)
        else:
            k_a, v_a, pm = k_b, v_b, None
        low = lax.broadcasted_iota(jnp.int32, (1, 2 * HD), 1) < HD
        for j in range(NH // 2):
            cols = slice(2 * HD * j, 2 * HD * (j + 1))
            o_half, l_half = [], []
            for q, k, v, b_ref, mask in ((q_lo, k_a, v_a, ba_ref, pm), (q_hi, k_b, v_b, bb_ref, None)):
                q2 = q[:, cols].astype(F32) * 0.125
                k2, v2 = k[:, cols], v[:, cols]
                qs = jnp.concatenate([jnp.where(low, q2, 0.0), jnp.where(low, 0.0, q2)], axis=0).astype(q.dtype)
                s = _dot(qs, k2, NT) + b_ref[j]
                if mask is not None:
                    s = s + mask
                m = jnp.max(s, axis=1, keepdims=True)
                p = jnp.exp(s - m)
                l = jnp.sum(p, axis=1, keepdims=True)
                pv = _dot(p.astype(v.dtype), v2, NN) / l
                lse = m + jnp.log(l)
                o_half.append(jnp.where(low, pv[:H2], pv[H2:]))
                l_half.append(jnp.where(low, lse[:H2], lse[H2:]))
            _st(o_ref, _merge_rows(o_half[0], o_half[1], R), cols)
            _st(l_ref, _merge_rows(l_half[0], l_half[1], R), cols)

    cur = lambda n: n
    prev = lambda n: jnp.maximum(n - 1, 0)
    if has_prev:
        in_specs = [geom.spec(AW, cq, cur), geom.spec(AW, ck, prev), geom.spec(AW, ck, cur),
                    geom.spec(AW, cv, prev), geom.spec(AW, cv, cur)]
        args = [qv] * 5
    else:
        in_specs = [geom.spec(AW, cq, cur), geom.spec(AW, ck, cur), geom.spec(AW, cv, cur)]
        args = [qv] * 3
    nd = len(geom.outer) + (1 if has_prev else 0)
    biases = [b.reshape(NH // 2, 2 * H2, b.shape[2]) for b in biases]
    in_specs += [pl.BlockSpec(b.shape, lambda *ids: (0, 0, 0)) for b in biases]
    grid = geom.outer + ((geom.n_tiles,) if has_prev else ())
    oshape = geom.view_shape(AW)
    o, l = pl.pallas_call(
        body, name=f"attn_fwd{gi}", grid=grid, in_specs=in_specs,
        out_specs=[geom.spec(AW, 0, cur), geom.spec(AW, 0, cur)],
        out_shape=[_sds(oshape, F32), _sds(oshape, F32)],
        compiler_params=_cp(("arbitrary",) * nd),
    )(*args, *biases)
    return geom.unview(o), geom.unview(l)


def _attn_combine(os_, ls_, rest):
    S = rest.shape[0]
    tm = 512

    def body(o0, o1, o2, l0, l1, l2, z_ref, attn_ref, L_ref, ag_ref, agt_ref):
        la, lb, lc = l0[...], l1[...], l2[...]
        m = jnp.maximum(jnp.maximum(la, lb), lc)
        ea, eb, ec = jnp.exp(la - m), jnp.exp(lb - m), jnp.exp(lc - m)
        den = ea + eb + ec
        attn = (ea * o0[...] + eb * o1[...] + ec * o2[...]) / den
        attn_ref[...] = attn
        L_ref[...] = m + jnp.log(den)
        z = z_ref[...]
        ag = attn * (z * _sigmoid(z))
        ag_ref[...] = ag.astype(ag_ref.dtype)
        agt_ref[...] = ag.T.astype(agt_ref.dtype)

    t = pl.BlockSpec((tm, AW), lambda i: (i, 0))
    return pl.pallas_call(
        body, name="attn_combine", grid=(S // tm,),
        in_specs=[t] * 6 + [pl.BlockSpec((tm, AW), lambda i: (i, 0))],
        out_specs=[t, t, t, pl.BlockSpec((AW, tm), lambda i: (0, i))],
        out_shape=[_sds((S, AW), F32), _sds((S, AW), F32), _sds((S, AW), CDT), _sds((AW, S), CDT)],
        compiler_params=_cp(("parallel",)),
    )(*os_, *ls_, rest)


def _pool_counts(r, LS, win):
    l = lax.broadcasted_iota(jnp.int32, (LS, 1), 0)
    return jnp.minimum(16 * l + (r + 1), win).astype(F32)


def _pool_fwd(rest, pool_w, pool_scale):
    S = rest.shape[0]
    LS = S // R16
    r3 = rest.reshape(R16, LS, REST_W)
    PG = 128

    def body(u_ref, z_ref, pw_ref, ps_ref, pt_ref, mp_ref, pg_ref, pgt_ref, e_ref):
        g = pl.program_id(0)
        win = jnp.left_shift(2, g)
        row = lax.broadcasted_iota(jnp.int32, (LS, PG), 0)
        for r in range(R16):
            e_ref[R16 + r] = u_ref[r]
        for r in range(1, R16):
            e_ref[r] = jnp.where(row == 0, 0.0, pltpu.roll(u_ref[r], 1, 0))

        def double(s):
            for i in range(2 * R16 - 1, 2 * s - 1, -1):
                e_ref[i] = e_ref[i] + e_ref[i - s]

        double(1)
        for k in (1, 2, 3):
            pl.when(g >= k)(functools.partial(double, 2 ** k))
        pw = pw_ref[...].astype(CDT)
        for r in range(R16):
            u = u_ref[r]
            pooled = e_ref[R16 + r] / _pool_counts(r, LS, win) - u
            pc = pooled.astype(CDT)
            mp = _dot(pc, pw, NN)
            z = z_ref[r]
            pg = (mp * ps_ref[...]) * (z * _sigmoid(z))
            mp_ref[r] = mp
            pg_ref[r] = pg.astype(pg_ref.dtype)
            pt_ref[:, r * LS:(r + 1) * LS] = pooled.T.astype(pt_ref.dtype)
            pgt_ref[:, r * LS:(r + 1) * LS] = pg.T.astype(pgt_ref.dtype)

    col = lambda off: pl.BlockSpec((R16, LS, PG), lambda g: (0, 0, off + g))
    tsp = pl.BlockSpec((PG, S), lambda g: (g, 0))
    pt, mp, pg, pgt = pl.pallas_call(
        body, name="pool_fwd", grid=(4,),
        in_specs=[col(24), col(4), pl.BlockSpec((None, PG, PG), lambda g: (g, 0, 0)),
                  pl.BlockSpec((1, PG), lambda g: (0, g))],
        out_specs=[tsp, col(0), col(0), tsp],
        out_shape=[_sds((AW, S), CDT), _sds((R16, LS, AW), F32), _sds((R16, LS, AW), CDT), _sds((AW, S), CDT)],
        scratch_shapes=[pltpu.VMEM((2 * R16, LS, PG), F32)],
        compiler_params=_cp(("parallel",)),
    )(r3, r3, pool_w, pool_scale)
    return pt, mp.reshape(S, AW), pg.reshape(S, AW), pgt


def _branch_merge(ag, pg, wab, wpb, rest):
    S = ag.shape[0]
    tm, tn = 512, 512

    def body(ag_ref, pg_ref, wa_ref, wp_ref, ga_ref, gp_ref, m_ref, mt_ref):
        ya = _dot(ag_ref[...], wa_ref[...], NN)
        yp = _dot(pg_ref[...], wp_ref[...], NN)
        mg = _sigmoid(ga_ref[...]) * ya + _sigmoid(gp_ref[...]) * yp
        m_ref[...] = mg.astype(m_ref.dtype)
        mt_ref[...] = mg.T.astype(mt_ref.dtype)

    a = pl.BlockSpec((tm, AW), lambda i, j: (i, 0))
    w = pl.BlockSpec((AW, tn), lambda i, j: (0, j))
    o = pl.BlockSpec((tm, tn), lambda i, j: (i, j))
    return pl.pallas_call(
        body, name="branch_merge", grid=(S // tm, D // tn),
        in_specs=[a, a, w, w, pl.BlockSpec((tm, tn), lambda i, j: (i, 2 + j)),
                  pl.BlockSpec((tm, tn), lambda i, j: (i, 4 + j))],
        out_specs=[o, pl.BlockSpec((tn, tm), lambda i, j: (j, i))],
        out_shape=[_sds((S, D), CDT), _sds((D, S), CDT)],
        compiler_params=_cp(("parallel", "parallel")),
    )(ag, pg, wab, wpb, rest, rest)


def _out_loss(merged, wout, x, target, gate, final_g):
    S = x.shape[0]

    def body(m_ref, w_ref, x_ref, t_ref, gate_ref, fg_ref, dx2_ref, dmo_ref, loss_ref, gfg_ref, dgate_ref):
        i = pl.program_id(0)

        @pl.when(i == 0)
        def _():
            loss_ref[...] = jnp.zeros_like(loss_ref)
            gfg_ref[...] = jnp.zeros_like(gfg_ref)
            dgate_ref[...] = jnp.zeros_like(dgate_ref)

        swap = _swap16()
        mo = _dot(_dot(swap, _ld(m_ref), NN).astype(CDT), w_ref[...], NN)
        gate, fg = gate_ref[...], fg_ref[...]
        x2 = x_ref[...] + gate * mo
        r2 = lax.rsqrt(jnp.mean(x2 * x2, axis=-1, keepdims=True) + EPS)
        xn2 = x2 * r2
        diff = xn2 * fg - t_ref[...]
        loss = jnp.sum(jnp.sum(diff * diff, axis=-1, keepdims=True), axis=0, keepdims=True)
        dy = diff * (1.0 / D)
        dxn2 = dy * fg
        dx2 = r2 * (dxn2 - xn2 * jnp.mean(dxn2 * xn2, axis=-1, keepdims=True))
        dx2_ref[...] = dx2
        _st(dmo_ref, _dot(swap, (dx2 * gate).astype(CDT), NN))
        loss_ref[...] += jnp.broadcast_to(loss * (0.5 / D), loss_ref.shape)
        gfg_ref[...] += jnp.sum(dy * xn2, axis=0, keepdims=True)
        dgate_ref[...] += jnp.sum(dx2 * mo, axis=0, keepdims=True)

    slab = pl.BlockSpec((R16, R16, D), lambda i: (0, i, 0))
    nat = pl.BlockSpec((TOK, D), lambda i: (i, 0))
    vec = pl.BlockSpec((1, D), lambda i: (0, 0))
    return pl.pallas_call(
        body, name="out_loss", grid=(S // TOK,),
        in_specs=[slab, _resident((D, D)), nat, nat, vec, vec],
        out_specs=[nat, slab, pl.BlockSpec((1, 128), lambda i: (0, 0)), vec, vec],
        out_shape=[_sds((S, D), F32), _sds((R16, S // R16, D), CDT), _sds((1, 128), F32), _sds((1, D), F32),
                   _sds((1, D), F32)],
        compiler_params=_cp(("arbitrary",)),
    )(merged, wout, x, target, gate, final_g)


def _bwd_gates(dmo, wout, wab, wpb, rest, ag, pg, attn, mp, pool_scale):
    S = dmo.shape[0]
    tm = 256

    def body(dmo_ref, wo_ref, wa_ref, wp_ref, za_ref, zp_ref, ga0, ga1, gp0, gp1, ag_ref, pg_ref, at_ref, mp_ref,
             ps_ref, dya_ref, dyp_ref, dr_ref, dat_ref, dl_ref, dmp_ref, dps_ref):
        i = pl.program_id(0)

        @pl.when(i == 0)
        def _():
            dps_ref[...] = jnp.zeros_like(dps_ref)

        dm = _dot(dmo_ref[...], wo_ref[...], NT)
        sa = _sigmoid(jnp.concatenate([ga0[...], ga1[...]], axis=1))
        sp = _sigmoid(jnp.concatenate([gp0[...], gp1[...]], axis=1))
        dya = (dm * sa).astype(CDT)
        dyp = (dm * sp).astype(CDT)
        dya_ref[...] = dya
        dyp_ref[...] = dyp
        ya = _dot(ag_ref[...], wa_ref[...], NN)
        yp = _dot(pg_ref[...], wp_ref[...], NN)
        dr_ref[:, 1024:2048] = (dm * ya * sa * (1.0 - sa)).astype(dr_ref.dtype)
        dr_ref[:, 2048:3072] = (dm * yp * sp * (1.0 - sp)).astype(dr_ref.dtype)
        da = _dot(dya, wa_ref[...], NT)
        dp = _dot(dyp, wp_ref[...], NT)
        za = za_ref[...]
        sga = _sigmoid(za)
        attn = at_ref[...]
        dattn = da * (za * sga)
        dr_ref[:, 0:512] = (da * attn * (sga * (1.0 + za * (1.0 - sga)))).astype(dr_ref.dtype)
        dat_ref[...] = dattn.astype(dat_ref.dtype)
        prod = dattn * attn
        low = lax.broadcasted_iota(jnp.int32, (1, 2 * HD), 1) < HD
        for j in range(NH // 2):
            x2 = prod[:, 2 * HD * j:2 * HD * (j + 1)]
            s0 = jnp.sum(jnp.where(low, x2, 0.0), axis=1, keepdims=True)
            s1 = jnp.sum(jnp.where(low, 0.0, x2), axis=1, keepdims=True)
            dl_ref[:, 2 * HD * j:2 * HD * (j + 1)] = jnp.where(low, s0, s1)
        zp = zp_ref[...]
        sgp = _sigmoid(zp)
        mpre = mp_ref[...]
        ps = ps_ref[...]
        dmixed = dp * (zp * sgp)
        dr_ref[:, 512:1024] = (dp * (mpre * ps) * (sgp * (1.0 + zp * (1.0 - sgp)))).astype(dr_ref.dtype)
        dps_ref[...] += jnp.sum(dmixed * mpre, axis=0, keepdims=True)
        dmp_ref[...] = (dmixed * ps).astype(dmp_ref.dtype)

    full = lambda shape: pl.BlockSpec(shape, lambda i: (0, 0))
    tD = pl.BlockSpec((tm, D), lambda i: (i, 0))
    tA = pl.BlockSpec((tm, AW), lambda i: (i, 0))
    rb = lambda c: pl.BlockSpec((tm, AW), lambda i: (i, c))
    return pl.pallas_call(
        body, name="bwd_gates", grid=(S // tm,),
        in_specs=[tD, full((D, D)), full((AW, D)), full((AW, D)), rb(0), rb(1), rb(2), rb(3), rb(4), rb(5),
                  tA, tA, tA, tA, full((1, AW))],
        out_specs=[tD, tD, pl.BlockSpec((tm, 3072), lambda i: (i, 0)), tA, tA, tA, full((1, AW))],
        out_shape=[_sds((S, D), CDT), _sds((S, D), CDT), _sds((S, REST_W), CDT), _sds((S, AW), CDT),
                   _sds((S, AW), F32), _sds((S, AW), CDT), _sds((1, AW), F32)],
        compiler_params=_cp(("arbitrary",), vmem=56),
    )(dmo, wout, wab, wpb, rest, rest, rest, rest, rest, rest, ag, pg, attn, mp, pool_scale)


def _pool_bwd(dmp, pt, pool_w, drest):
    S = dmp.shape[0]
    LS = S // R16
    PG = 128
    d3 = dmp.reshape(R16, LS, AW)
    dr3 = drest.reshape(R16, LS, REST_W)

    def body(d_ref, pt_ref, pw_ref, dr_in, du_ref, dpw_ref, g_ref, s_ref):
        del dr_in
        g = pl.program_id(0)
        win = jnp.left_shift(2, g)
        row = lax.broadcasted_iota(jnp.int32, (LS, PG), 0)
        pw = pw_ref[...].astype(CDT)
        dpw = jnp.zeros((PG, PG), F32)
        for r in range(R16):
            dm = d_ref[r]
            dpw = dpw + _dot(pt_ref[:, r * LS:(r + 1) * LS], dm, NN)
            dpooled = _dot(dm, pw, NT)
            g_ref[r] = dpooled
            sc = dpooled / _pool_counts(r, LS, win)
            s_ref[r] = sc
            if r < R16 - 1:
                s_ref[R16 + r] = jnp.where(row == LS - 1, 0.0, pltpu.roll(sc, LS - 1, 0))
        dpw_ref[...] = dpw

        def double(s):
            for i in range(0, 2 * R16 - 2 * s):
                s_ref[i] = s_ref[i] + s_ref[i + s]

        double(1)
        for k in (1, 2, 3):
            pl.when(g >= k)(functools.partial(double, 2 ** k))
        for r in range(R16):
            du_ref[r] = (s_ref[r] - g_ref[r]).astype(du_ref.dtype)

    du, dpw = pl.pallas_call(
        body, name="pool_bwd", grid=(4,),
        in_specs=[pl.BlockSpec((R16, LS, PG), lambda g: (0, 0, g)), pl.BlockSpec((PG, S), lambda g: (g, 0)),
                  pl.BlockSpec((None, PG, PG), lambda g: (g, 0, 0)), pl.BlockSpec(memory_space=pl.ANY)],
        out_specs=[pl.BlockSpec((R16, LS, PG), lambda g: (0, 0, 24 + g)),
                   pl.BlockSpec((None, PG, PG), lambda g: (g, 0, 0))],
        out_shape=[_sds((R16, LS, REST_W), CDT), _sds((4, PG, PG), F32)],
        scratch_shapes=[pltpu.VMEM((R16, LS, PG), F32), pltpu.VMEM((2 * R16, LS, PG), F32)],
        input_output_aliases={3: 0},
        compiler_params=_cp(("parallel",)),
    )(d3, pt, pool_w, dr3)
    return du.reshape(S, REST_W), dpw


def _attn_bwd(qkv, dattn, Lb, dlb, biases, geom, dqkv_prev):
    S = qkv.shape[0]
    gi, T, R, has_prev, N = geom.gi, geom.T, geom.rows, geom.has_prev, geom.n_tiles
    H2 = T // 2
    nd_outer = len(geom.outer)
    qv = geom.view(qkv)
    cq, ck, cv = 3 * gi, 3 * gi + 1, 3 * gi + 2
    W3 = 3 * AW

    def heads(q_ref, kp_ref, kc_ref, vp_ref, vc_ref, do_ref, L_ref, dl_ref, ba_ref, bb_ref, pm, dsa_ref, dsb_ref, emit):
        q_lo, q_hi = _split_rows(_ld(q_ref), R)
        do_lo, do_hi = _split_rows(_ld(do_ref), R)
        L_lo, L_hi = _split_rows(_ld(L_ref), R)
        dl_lo, dl_hi = _split_rows(_ld(dl_ref), R)
        kc_lo, kc_hi = _split_rows(_ld(kc_ref), R)
        vc_lo, vc_hi = _split_rows(_ld(vc_ref), R)
        k_b = jnp.concatenate([kc_lo, kc_hi], axis=0)
        v_b = jnp.concatenate([vc_lo, vc_hi], axis=0)
        if has_prev:
            k_a = jnp.concatenate([_split_rows(_ld(kp_ref), R)[1], kc_lo], axis=0)
            v_a = jnp.concatenate([_split_rows(_ld(vp_ref), R)[1], vc_lo], axis=0)
        else:
            k_a, v_a = k_b, v_b
        low = lax.broadcasted_iota(jnp.int32, (1, 2 * HD), 1) < HD
        for j in range(NH // 2):
            sl = slice(2 * HD * j, 2 * HD * (j + 1))
            res = []
            for q, k, v, do, L, dl, b_ref, mask, ds_ref in (
                    (q_lo, k_a, v_a, do_lo, L_lo, dl_lo, ba_ref, pm, dsa_ref),
                    (q_hi, k_b, v_b, do_hi, L_hi, dl_hi, bb_ref, None, dsb_ref)):
                q2 = q[:, sl].astype(F32) * 0.125
                do2 = do[:, sl].astype(F32)
                k2, v2 = k[:, sl], v[:, sl]
                L2, dl2 = L[:, sl], dl[:, sl]
                qs = jnp.concatenate([jnp.where(low, q2, 0.0), jnp.where(low, 0.0, q2)], axis=0).astype(q.dtype)
                dom = jnp.concatenate([jnp.where(low, do2, 0.0), jnp.where(low, 0.0, do2)], axis=0).astype(q.dtype)
                Lc = jnp.concatenate([L2[:, 0:1], L2[:, HD:HD + 1]], axis=0)
                dlc = jnp.concatenate([dl2[:, 0:1], dl2[:, HD:HD + 1]], axis=0)
                s = _dot(qs, k2, NT) + b_ref[j]
                if mask is not None:
                    s = s + mask
                p = jnp.exp(s - Lc)
                ds = p * (_dot(dom, v2, NT) - dlc)
                ds_ref[j] += ds
                dsc = ds.astype(q.dtype)
                dq_st = _dot(dsc, k2, NN)
                res.append((jnp.where(low, dq_st[:H2], dq_st[H2:]) * 0.125, _dot(dsc, qs, TN),
                            _dot(p.astype(q.dtype), dom, TN)))
            (dq_lo, dk_a, dv_a), (dq_hi, dk_b, dv_b) = res
            dq = _merge_rows(dq_lo, dq_hi, R)
            if has_prev:
                emit(sl, dq, _merge_rows(dk_a[H2:] + dk_b[:H2], dk_b[H2:], R),
                     _merge_rows(dv_a[H2:] + dv_b[:H2], dv_b[H2:], R), dk_a[:H2], dv_a[:H2])
            else:
                emit(sl, dq, _merge_rows(dk_a[:H2] + dk_b[:H2], dk_a[H2:] + dk_b[H2:], R),
                     _merge_rows(dv_a[:H2] + dv_b[:H2], dv_a[H2:] + dv_b[H2:], R), None, None)

    def first_step():
        ok = pl.program_id(nd_outer) == 0 if has_prev else pl.program_id(0) == 0
        for a in range(nd_outer if has_prev else 0):
            ok = ok & (pl.program_id(a) == 0)
        return ok

    if has_prev:
        def body(q_ref, kp_ref, kc_ref, vp_ref, vc_ref, do_ref, L_ref, dl_ref, ba_ref, bb_ref, prev_in, out_ref,
                 dsa_ref, dsb_ref, hold, new, pp):
            del prev_in
            s_id = pl.program_id(nd_outer)

            @pl.when(first_step())
            def _():
                dsa_ref[...] = jnp.zeros_like(dsa_ref)
                dsb_ref[...] = jnp.zeros_like(dsb_ref)

            @pl.when(s_id < N)
            def _():
                def emit(sl, dq, dk, dv, dk_prev, dv_prev):
                    new[:, sl] = dq
                    new[:, slice(AW + sl.start, AW + sl.stop)] = dk
                    new[:, slice(2 * AW + sl.start, 2 * AW + sl.stop)] = dv
                    pp[:, sl] = dk_prev
                    pp[:, slice(AW + sl.start, AW + sl.stop)] = dv_prev

                heads(q_ref, kp_ref, kc_ref, vp_ref, vc_ref, do_ref, L_ref, dl_ref, ba_ref, bb_ref,
                      _prev_mask(s_id, H2, T), dsa_ref, dsb_ref, emit)

            @pl.when((s_id >= 1) & (s_id < N))
            def _():
                _st(out_ref, hold[:, 0:AW], slice(0, AW))
                late = _merge_rows(jnp.zeros((H2, 2 * AW), F32), pp[...], R)
                _st(out_ref, hold[:, AW:W3] + late, slice(AW, W3))

            @pl.when(s_id == N)
            def _():
                _st(out_ref, hold[...])

            @pl.when(s_id < N)
            def _():
                hold[...] = new[...]

        cur = lambda n: jnp.minimum(n, N - 1)
        prev = lambda n: jnp.clip(n - 1, 0, N - 1)
        delayed = lambda n: jnp.maximum(n - 1, 0)
        in_specs = [geom.spec(AW, cq, cur), geom.spec(AW, ck, prev), geom.spec(AW, ck, cur),
                    geom.spec(AW, cv, prev), geom.spec(AW, cv, cur),
                    geom.spec(AW, 0, cur), geom.spec(AW, 0, cur), geom.spec(AW, 0, cur)]
        args = [qv] * 5 + [geom.view(dattn), geom.view(Lb), geom.view(dlb)]
        out_spec = geom.spec(W3, gi, delayed)
        grid = geom.outer + (N + 1,)
        scratch = [pltpu.VMEM((T, W3), F32), pltpu.VMEM((T, W3), F32), pltpu.VMEM((H2, 2 * AW), F32)]
    else:
        def body(q_ref, kc_ref, vc_ref, do_ref, L_ref, dl_ref, ba_ref, bb_ref, prev_in, out_ref, dsa_ref, dsb_ref):
            del prev_in

            @pl.when(first_step())
            def _():
                dsa_ref[...] = jnp.zeros_like(dsa_ref)
                dsb_ref[...] = jnp.zeros_like(dsb_ref)

            def emit(sl, dq, dk, dv, dk_prev, dv_prev):
                _st(out_ref, dq, sl)
                _st(out_ref, dk, slice(AW + sl.start, AW + sl.stop))
                _st(out_ref, dv, slice(2 * AW + sl.start, 2 * AW + sl.stop))

            heads(q_ref, None, kc_ref, None, vc_ref, do_ref, L_ref, dl_ref, ba_ref, bb_ref, None, dsa_ref, dsb_ref,
                  emit)

        cur = lambda n: n
        in_specs = [geom.spec(AW, cq, cur), geom.spec(AW, ck, cur), geom.spec(AW, cv, cur),
                    geom.spec(AW, 0, cur), geom.spec(AW, 0, cur), geom.spec(AW, 0, cur)]
        args = [qv] * 3 + [geom.view(dattn), geom.view(Lb), geom.view(dlb)]
        out_spec = geom.spec(W3, gi, cur)
        grid = geom.outer
        scratch = []

    whole = lambda b: pl.BlockSpec(b.shape, lambda *ids: (0, 0, 0))
    biases = [b.reshape(NH // 2, 2 * H2, b.shape[2]) for b in biases]
    in_specs += [whole(b) for b in biases]
    in_specs.append(pl.BlockSpec(memory_space=pl.ANY))
    if dqkv_prev is None:
        last, aliases = biases[0], {}
    else:
        last, aliases = geom.view(dqkv_prev), {len(in_specs) - 1: 0}
    dq, dsa, dsb = pl.pallas_call(
        body, name=f"attn_bwd{gi}", grid=grid, in_specs=in_specs,
        out_specs=[out_spec] + [whole(b) for b in biases],
        out_shape=[_sds(geom.view_shape(QKV_W), CDT)] + [_sds(b.shape, F32) for b in biases],
        scratch_shapes=scratch,
        input_output_aliases=aliases,
        compiler_params=_cp(("arbitrary",) * len(grid), vmem=56),
    )(*args, *biases, last)
    return geom.unview(dq), [d.reshape(NH, H2, d.shape[2]) for d in (dsa, dsb)]


def _h_bwd(dh, dx2, x, norm_g, scale):
    S = x.shape[0]

    def body(dh_ref, dx2_ref, x_ref, g_ref, sc_ref, gx_ref, dsh_ref, dsc_ref, dng_ref, nat_ref):
        i = pl.program_id(0)

        @pl.when(i == 0)
        def _():
            dsh_ref[...] = jnp.zeros_like(dsh_ref)
            dsc_ref[...] = jnp.zeros_like(dsc_ref)
            dng_ref[...] = jnp.zeros_like(dng_ref)

        for r in range(R16):
            nat_ref[:, r, :] = dh_ref[r]
        dh = jnp.concatenate([nat_ref[l] for l in range(R16)], axis=0)
        g = g_ref[...]
        one_sc = 1.0 + sc_ref[...]
        xv = x_ref[...]
        r1 = lax.rsqrt(jnp.mean(xv * xv, axis=-1, keepdims=True) + EPS)
        xn = xv * r1
        dhx = dh * xn
        dxn = dh * (g * one_sc)
        dx1 = r1 * (dxn - xn * jnp.mean(dxn * xn, axis=-1, keepdims=True))
        gx_ref[...] = dx2_ref[...] + dx1
        dsh_ref[...] += jnp.sum(dh, axis=0, keepdims=True)
        dsc_ref[...] += jnp.sum(dhx * g, axis=0, keepdims=True)
        dng_ref[...] += jnp.sum(dhx * one_sc, axis=0, keepdims=True)

    nat = pl.BlockSpec((TOK, D), lambda i: (i, 0))
    vec = pl.BlockSpec((1, D), lambda i: (0, 0))
    return pl.pallas_call(
        body, name="h_bwd", grid=(S // TOK,),
        in_specs=[pl.BlockSpec((R16, R16, D), lambda i: (0, i, 0)), nat, nat, vec, vec],
        out_specs=[nat, vec, vec, vec],
        out_shape=[_sds((S, D), F32), _sds((1, D), F32), _sds((1, D), F32), _sds((1, D), F32)],
        scratch_shapes=[pltpu.VMEM((R16, R16, D), F32)],
        compiler_params=_cp(("arbitrary",)),
    )(dh, dx2, x, norm_g, scale)


def _local_step(x, target, shift, scale, gate, norm_g, w_in_full, pool_w, pool_scale, wab, wpb, wout, rel_bias,
                final_g):
    st = _local_pre(x, shift, scale, norm_g, rel_bias)
    st = _local_grads(st, target, gate, w_in_full, pool_w, pool_scale, wab, wpb, wout, final_g)
    d_rel_bias = _local_bias_grads(st)
    gx, dmod, dng = _local_input_grads(st, w_in_full, norm_g, scale)
    return dict(loss=st["loss"], grad_x=gx, dw_in=st["dw_in"], dw_ab=st["dw_ab"], dw_pb=st["dw_pb"],
                dw_out=st["dw_out"], d_norm_g=dng, d_pool_w=st["dpw"], d_pool_scale=st["dps"],
                d_rel_bias=d_rel_bias, d_final_g=st["gfg"], dmod=dmod)


def _local_pre(x, shift, scale, norm_g, rel_bias, dep=None):
    S = x.shape[0]
    LS = S // R16
    geoms = [_Geom(gi, LS) for gi in range(3)]
    idx_np = [[g.bucket_index(sub) for sub in range(2)] for g in geoms]
    buckets = [[tuple(int(b) for b in np.unique(ix) if b >= 0) for ix in pair] for pair in idx_np]
    idxs = [[jnp.asarray(ix) for ix in pair] for pair in idx_np]
    biases = [[_bias_table(idxs[gi][sub], buckets[gi][sub], rel_bias, gi, sub, dep) for sub in range(2)]
              for gi in range(3)]
    h = _norm_mod(x, norm_g, shift, scale, dep).reshape(S, D)
    return dict(geoms=geoms, idxs=idxs, buckets=buckets, biases=biases, x=x, h=h, ht=_tcast(h, "h_transpose"))


def _local_grads(st, target, gate, w_in_full, pool_w, pool_scale, wab, wpb, wout, final_g):
    geoms, idxs, biases, x, h, ht = (st[k] for k in ("geoms", "idxs", "biases", "x", "h", "ht"))
    S = h.shape[0]
    LS = S // R16
    qkv, rest = _proj(h, w_in_full)

    os_, ls_ = [], []
    for gi in range(3):
        o, l = _attn_fwd(qkv, biases[gi], geoms[gi])
        os_.append(o)
        ls_.append(l)
    attn, Lb, ag, agt = _attn_combine(os_, ls_, rest)
    pt, mp, pg, pgt = _pool_fwd(rest, pool_w, pool_scale)
    merged, mt = _branch_merge(ag, pg, wab, wpb, rest)
    dx2, dmo3, loss, gfg, dgate = _out_loss(merged.reshape(R16, LS, D), wout, x, target, gate, final_g)
    dmo = dmo3.reshape(S, D)

    dya, dyp, drest, dattn, dlb, dmp, dps = _bwd_gates(dmo, wout, wab, wpb, rest, ag, pg, attn, mp, pool_scale)
    dw_out = _mm_tn(mt, dmo, n_total=D, out_blk=lambda j: j, name="dw_out")
    dw_ab = _mm_tn(agt, dya, n_total=D, out_blk=lambda j: j, name="dw_attn_br")
    dw_pb = _mm_tn(pgt, dyp, n_total=D, out_blk=lambda j: j, name="dw_pool_br")
    drest, dpw = _pool_bwd(dmp, pt, pool_w, drest)

    dqkv = None
    dsums = []
    for gi in range(3):
        dqkv, dsum = _attn_bwd(qkv, dattn, Lb, dlb, biases[gi], geoms[gi], dqkv)
        dsums.append(dsum)

    dw_in = _mm_tn(ht, dqkv, n_total=8192, out_blk=lambda j: j, name="dw_in_qkv")
    dw_in = _mm_tn(ht, drest, n_total=8192, out_blk=_rest_blk, name="dw_in_rest", prev=dw_in)
    return dict(st, loss=loss[0, 0], dw_in=dw_in, dw_ab=dw_ab, dw_pb=dw_pb, dw_out=dw_out, dpw=dpw, dps=dps, gfg=gfg,
                dgate=dgate, dsums=dsums, dqkv=dqkv, drest=drest, dx2=dx2)


def _local_bias_grads(st, dep=None):
    drb = [_bias_grad(st["dsums"][gi][0], st["idxs"][gi][0], st["buckets"][gi][0], gi, 0, dep if gi == 0 else None) +
           _bias_grad(st["dsums"][gi][1], st["idxs"][gi][1], st["buckets"][gi][1], gi, 1) for gi in range(3)]
    return jnp.concatenate(drb, axis=1)


def _dh_h_bwd(dqkv, drest, w, dx2, x, norm_g, scale, dep=None):
    S = x.shape[0]
    dep_specs, dep_args = _after(dep)

    def body(a_ref, b_ref, w_ref, dx2_ref, x_ref, g_ref, sc_ref, *rest):
        gx_ref, dsh_ref, dsc_ref, dng_ref, nat_ref = rest[-5:]
        i = pl.program_id(0)

        @pl.when(i == 0)
        def _():
            dsh_ref[...] = jnp.zeros_like(dsh_ref)
            dsc_ref[...] = jnp.zeros_like(dsc_ref)
            dng_ref[...] = jnp.zeros_like(dng_ref)

        a, b = _ld(a_ref), _ld(b_ref)
        acc = _dot(a[:, 0:512], _w512(w_ref, 0), NT)
        for j in range(1, 9):
            acc = acc + _dot(a[:, 512 * j:512 * (j + 1)], _w512(w_ref, j), NT)
        off = 0
        for lo, hi in REST_COLS:
            for j in range(lo, hi):
                acc = acc + _dot(b[:, off:off + 512], _w512(w_ref, j), NT)
                off += 512
        for r in range(R16):
            nat_ref[:, r, :] = acc[R16 * r:R16 * (r + 1)]
        dh = jnp.concatenate([nat_ref[l] for l in range(R16)], axis=0)
        g = g_ref[...]
        one_sc = 1.0 + sc_ref[...]
        xv = x_ref[...]
        r1 = lax.rsqrt(jnp.mean(xv * xv, axis=-1, keepdims=True) + EPS)
        xn = xv * r1
        dhx = dh * xn
        dxn = dh * (g * one_sc)
        dx1 = r1 * (dxn - xn * jnp.mean(dxn * xn, axis=-1, keepdims=True))
        gx_ref[...] = dx2_ref[...] + dx1
        dsh_ref[...] += jnp.sum(dh, axis=0, keepdims=True)
        dsc_ref[...] += jnp.sum(dhx * g, axis=0, keepdims=True)
        dng_ref[...] += jnp.sum(dhx * one_sc, axis=0, keepdims=True)

    nat = pl.BlockSpec((TOK, D), lambda i: (i, 0))
    vec = pl.BlockSpec((1, D), lambda i: (0, 0))
    slab = lambda W: pl.BlockSpec((R16, R16, W), lambda i: (0, i, 0))
    return pl.pallas_call(
        body, name="dh_h_bwd", grid=(S // TOK,),
        in_specs=[slab(QKV_W), slab(REST_W), _resident((8, D, 1024)), nat, nat, vec, vec] + dep_specs,
        out_specs=[nat, vec, vec, vec],
        out_shape=[_sds((S, D), F32), _sds((1, D), F32), _sds((1, D), F32), _sds((1, D), F32)],
        scratch_shapes=[pltpu.VMEM((R16, R16, D), F32)],
        compiler_params=_cp(("arbitrary",), vmem=56),
    )(dqkv, drest, w, dx2, x, norm_g, scale, *dep_args)


def _local_input_grads(st, w_in_full, norm_g, scale, dep=None):
    S = st["h"].shape[0]
    LS = S // R16
    gx, dsh, dsc, dng = _dh_h_bwd(st["dqkv"].reshape(R16, LS, QKV_W), st["drest"].reshape(R16, LS, REST_W),
                                  w_in_full, st["dx2"], st["x"], norm_g, scale, dep)
    return gx, jnp.concatenate([dsh, dsc, st["dgate"]], axis=1), dng


def _my_place():
    return lax.axis_index("x"), lax.axis_index("y"), lax.axis_index("c")


def _block_of(ref, axis, idx, width):
    if axis is None:
        return ref.at[idx]
    start = pl.multiple_of(idx * width, width)
    if axis == 0:
        return ref.at[pl.ds(start, width)]
    return ref.at[:, pl.ds(start, width)]


_SEM = pl.BlockSpec(memory_space=pltpu.SEMAPHORE)
_EFFECT = pltpu.SideEffectType.DATAFLOW_SIDE_EFFECTING


def _split_start(name, srcs, lands, ncopy, plan, dep=None):
    arrs = list(srcs) + list(lands)
    na, ns = len(arrs), len(srcs)
    dep_specs, dep_args = _after(dep)

    def body(*refs):
        send_sems, recv_sems, token = refs[na + len(dep_args)], refs[na + len(dep_args) + 1], refs[-1]
        for k, (s, d, dev) in enumerate(plan(refs[:ns], refs[ns:na])):
            pltpu.make_async_remote_copy(src_ref=s, dst_ref=d, send_sem=send_sems.at[k], recv_sem=recv_sems.at[k],
                                         device_id=dev, device_id_type=MESH).start()
        token[...] = jnp.zeros_like(token)

    outs = pl.pallas_call(
        body, name=name,
        out_shape=(pltpu.SemaphoreType.DMA((ncopy,)), pltpu.SemaphoreType.DMA((ncopy,)),
                   *[pltpu.HBM(a.shape, a.dtype) for a in arrs], _sds((8, 128), F32)),
        in_specs=[_HBM] * na + dep_specs,
        out_specs=(_SEM, _SEM, *[_HBM] * na, pl.BlockSpec(memory_space=pltpu.VMEM)),
        input_output_aliases={i: 2 + i for i in range(na)},
        compiler_params=pltpu.CompilerParams(has_side_effects=_EFFECT),
    )(*[pltpu.with_memory_space_constraint(a, pltpu.HBM) for a in arrs], *dep_args)
    return (outs[0], outs[1]), list(outs[2:2 + ns]), list(outs[2 + ns:2 + na]), outs[-1]


def _split_wait(name, sems, srcs, lands, plan, after):
    arrs = list(srcs) + list(lands)
    na, ns = len(arrs), len(srcs)
    after = list(after) if isinstance(after, (list, tuple)) else [after]

    def body(*refs):
        send_sems, recv_sems = refs[na], refs[na + 1]
        for k, (s, d, dev) in enumerate(plan(refs[:ns], refs[ns:na])):
            cp = pltpu.make_async_remote_copy(src_ref=s, dst_ref=d, send_sem=send_sems.at[k], recv_sem=recv_sems.at[k],
                                              device_id=dev, device_id_type=MESH)
            cp.wait_send()
            cp.wait_recv()

    outs = pl.pallas_call(
        body, name=name, out_shape=[pltpu.HBM(a.shape, a.dtype) for a in arrs],
        in_specs=[_HBM] * na + [_SEM, _SEM] + [pl.BlockSpec(memory_space=pl.ANY)] * len(after),
        out_specs=[_HBM] * na,
        input_output_aliases={i: i for i in range(na)},
        compiler_params=pltpu.CompilerParams(has_side_effects=_EFFECT),
    )(*arrs, sems[0], sems[1], *after)
    return list(outs[:ns]), list(outs[ns:])


def _gather_chips_plan(layout):
    def plan(src, land):
        x, y, c = _my_place()
        me = 4 * x + 2 * y + c
        mine = [_block_of(land[p], layout[p][0], me, layout[p][1]) for p in range(len(land))]
        return [(mine[p], mine[p], (cx, cy, c)) for p in range(len(land)) for cx, cy in [(1 - x, y), (x, 1 - y)]]
    return plan


def _pair_plan(layout):
    def plan(src, land):
        x, y, c = _my_place()
        n = len(layout)
        return [(_block_of(src[p], layout[p][0], 2 * q + (1 - c), layout[p][1]), land[p].at[q], (x, y, 1 - c))
                for p in range(n) for q in range(4)] + [(src[p], land[p], (x, y, 1 - c)) for p in range(n, len(src))]
    return plan


def _chips_plan(n):
    def plan(src, land):
        x, y, c = _my_place()
        chips = [(1 - x, y), (x, 1 - y), (1 - x, 1 - y)]
        return [(src[p].at[2 * cx + cy], land[p].at[2 * x + y], (cx, cy, c)) for p in range(n) for cx, cy in chips] + \
               [(src[p], land[p].at[2 * x + y], (cx, cy, c)) for p in range(n, len(src)) for cx, cy in chips]
    return plan


def _pair_share(gathered):
    n = len(gathered)

    def body(*refs):
        land = refs[n:2 * n]
        send_sems, recv_sems, fwd_send, fwd_recv = refs[2 * n:]
        x, y, c = _my_place()
        k_dg = 3 - (2 * x + y)
        south = c == 0
        pass_on = 4 * jnp.where(south, 1 - x, x) + 2 * jnp.where(south, y, 1 - y) + c
        to = (jnp.where(south, x, 1 - x), jnp.where(south, 1 - y, y), c)

        def forward(p):
            return pltpu.make_async_remote_copy(
                src_ref=land[p].at[pass_on], dst_ref=land[p].at[pass_on], send_sem=fwd_send.at[p],
                recv_sem=fwd_recv.at[p], device_id=to, device_id_type=MESH)

        def to_sibling(p, k):
            blk = land[p].at[2 * k + c]
            return pltpu.make_async_remote_copy(
                src_ref=blk, dst_ref=blk, send_sem=send_sems.at[4 * p + k], recv_sem=recv_sems.at[4 * p + k],
                device_id=(x, y, 1 - c), device_id_type=MESH)

        for p in range(n):
            forward(p).start()
        for p in range(n):
            for k in range(4):
                @pl.when(k != k_dg)
                def _():
                    to_sibling(p, k).start()
        for p in range(n):
            dg = land[p].at[2 * k_dg + c]
            pltpu.make_async_remote_copy(src_ref=dg, dst_ref=dg, send_sem=fwd_send.at[p], recv_sem=fwd_recv.at[p],
                                         device_id=to, device_id_type=MESH).wait_recv()
            for k in range(4):
                @pl.when(k == k_dg)
                def _():
                    to_sibling(p, k).start()
        for p in range(n):
            forward(p).wait_send()
            for k in range(4):
                to_sibling(p, k).wait_send()
                blk = land[p].at[2 * k + 1 - c]
                pltpu.make_async_remote_copy(
                    src_ref=blk, dst_ref=blk, send_sem=send_sems.at[4 * p + k], recv_sem=recv_sems.at[4 * p + k],
                    device_id=(x, y, 1 - c), device_id_type=MESH).wait_recv()

    return pl.pallas_call(
        body, name="gather_pair_share", in_specs=[_HBM] * n, out_specs=[_HBM] * n,
        out_shape=[_sds(g.shape, g.dtype) for g in gathered],
        input_output_aliases={p: p for p in range(n)},
        scratch_shapes=[pltpu.SemaphoreType.DMA((4 * n,)), pltpu.SemaphoreType.DMA((4 * n,)),
                        pltpu.SemaphoreType.DMA((n,)), pltpu.SemaphoreType.DMA((n,))],
    )(*gathered)


def _place_slot(arrs, slot, n_slots, name, out_dtype=None):
    n = len(arrs)

    def body(s_ref, *refs):
        for p in range(n):
            refs[n + p][...] = refs[p][...].astype(refs[n + p].dtype)

    def spec(a):
        tail = a.shape[-2:]
        return pl.BlockSpec((None,) + tail, lambda i, s_ref: (s_ref[0], 0, 0))

    in_specs = [spec(a) if a.ndim == 3 else pl.BlockSpec(a.shape, lambda i, s_ref: (0, 0)) for a in arrs]
    return pl.pallas_call(
        body, name=name,
        grid_spec=pltpu.PrefetchScalarGridSpec(num_scalar_prefetch=1, grid=(1,), in_specs=in_specs,
                                               out_specs=[spec(a) for a in arrs]),
        out_shape=[_sds((n_slots,) + a.shape[-2:], out_dtype or a.dtype) for a in arrs],
        compiler_params=_cp(("arbitrary",)),
    )(slot, *arrs)


def _cols_from_blocks(w8, name):
    _, R, C = w8.shape

    def body(w_ref, o_ref):
        for e in range(8):
            o_ref[:, C * e:C * (e + 1)] = w_ref[e]

    return pl.pallas_call(body, name=name, out_shape=_sds((R, 8 * C), w8.dtype), compiler_params=_cp())(w8)


def _all_gather_small(srcs, name):
    n = len(srcs)
    vmem = pl.BlockSpec(memory_space=pltpu.VMEM)

    def body(*refs):
        src, out = refs[:n], refs[n:2 * n]
        send_sems, recv_sems = refs[2 * n:]
        x, y, c = _my_place()
        me = 4 * x + 2 * y + c
        flip = lambda v, f: 1 - v if f else v
        peers = [(flip(x, fx), flip(y, fy), flip(c, fc)) for fx in (0, 1) for fy in (0, 1) for fc in (0, 1)][1:]
        cps = []
        for p in range(n):
            out[p][me] = src[p][...]
            for k, peer in enumerate(peers):
                cps.append(pltpu.make_async_remote_copy(
                    src_ref=src[p], dst_ref=out[p].at[me], send_sem=send_sems.at[7 * p + k],
                    recv_sem=recv_sems.at[7 * p + k], device_id=peer, device_id_type=MESH))
        for cp in cps:
            cp.start()
        for cp in cps:
            cp.wait()

    return pl.pallas_call(
        body, name=name, in_specs=[vmem] * n, out_specs=[vmem] * n,
        out_shape=[_sds((8,) + a.shape, a.dtype) for a in srcs],
        scratch_shapes=[pltpu.SemaphoreType.DMA((7 * n,)), pltpu.SemaphoreType.DMA((7 * n,))],
    )(*srcs)


def _adaln_exchange(c, w_ada, b_part):
    vmem = pl.BlockSpec(memory_space=pltpu.VMEM)

    def body(c_ref, w_ref, b_ref, call_ref, mod_ref, part_ref, send_sems, recv_sems):
        x, y, c = _my_place()
        me = 4 * x + 2 * y + c
        flip = lambda v, f: 1 - v if f else v
        peers = [(flip(x, fx), flip(y, fy), flip(c, fc)) for fx in (0, 1) for fy in (0, 1) for fc in (0, 1)][1:]
        call_ref[me] = c_ref[...]
        first = [pltpu.make_async_remote_copy(src_ref=c_ref, dst_ref=call_ref.at[me], send_sem=send_sems.at[k],
                                              recv_sem=recv_sems.at[k], device_id=peer, device_id_type=MESH)
                 for k, peer in enumerate(peers)]
        for cp in first:
            cp.start()
        for cp in first:
            cp.wait()
        c_all = jnp.concatenate([call_ref[j] for j in range(8)], axis=0)
        part_ref[...] = _dot(c_all.astype(CDT), w_ref[...].astype(CDT), NN) + b_ref[...]
        mod_ref[me] = part_ref[pl.ds(me, 1), :]
        second = [pltpu.make_async_remote_copy(
            src_ref=part_ref.at[pl.ds(4 * px + 2 * py + pc, 1)], dst_ref=mod_ref.at[me], send_sem=send_sems.at[7 + k],
            recv_sem=recv_sems.at[7 + k], device_id=(px, py, pc), device_id_type=MESH)
            for k, (px, py, pc) in enumerate(peers)]
        for cp in second:
            cp.start()
        for cp in second:
            cp.wait()

    return pl.pallas_call(
        body, name="adaln_exchange", in_specs=[vmem] * 3, out_specs=[vmem] * 2,
        out_shape=[_sds((8, 1, D), F32), _sds((8, 1, w_ada.shape[1]), F32)],
        scratch_shapes=[pltpu.VMEM((8, w_ada.shape[1]), F32), pltpu.SemaphoreType.DMA((14,)),
                        pltpu.SemaphoreType.DMA((14,))],
        compiler_params=_cp(),
    )(c, w_ada, b_part)


def _pair_add(dw, recv, axis, width, cidx, name):
    if axis == 0:
        tr = min(width, 256)
        C = dw.shape[1]
        grid = (4, width // tr)
        nb = width // tr
        own = pl.BlockSpec((tr, C), lambda q, i, c_ref: ((2 * q + c_ref[0]) * nb + i, 0))
        oth = pl.BlockSpec((None, tr, C), lambda q, i, c_ref: (q, i, 0))
    else:
        R = dw.shape[0]
        tr = min(R, 256)
        grid = (4, R // tr)
        own = pl.BlockSpec((tr, width), lambda q, i, c_ref: (i, 2 * q + c_ref[0]))
        oth = pl.BlockSpec((None, tr, width), lambda q, i, c_ref: (q, i, 0))

    def body(c_ref, a_ref, b_ref, o_ref):
        o_ref[...] = (a_ref[...] + b_ref[...]).astype(o_ref.dtype)

    return pl.pallas_call(
        body, name=name,
        grid_spec=pltpu.PrefetchScalarGridSpec(num_scalar_prefetch=1, grid=grid, in_specs=[own, oth], out_specs=oth),
        out_shape=_sds(recv.shape, CDT), compiler_params=_cp(("parallel", "parallel")),
    )(cidx, dw, recv)


def _add_pairs(xs, ys, name):
    n = len(xs)

    def body(*refs):
        for p in range(n):
            refs[2 * n + p][...] = refs[p][...] + refs[n + p][...]

    return pl.pallas_call(body, name=name, out_shape=[_sds(a.shape, F32) for a in xs])(*xs, *ys)


def _adam_math(w, g, m, v):
    m = ADAM_B1 * m + (1.0 - ADAM_B1) * g
    v = ADAM_B2 * v + (1.0 - ADAM_B2) * (g * g)
    m_hat = m / (1.0 - ADAM_B1 ** ADAM_STEP)
    v_hat = v / (1.0 - ADAM_B2 ** ADAM_STEP)
    delta = -ADAM_LR * (m_hat / (jnp.sqrt(v_hat) + ADAM_EPS) + ADAM_WD * w)
    return delta, m, v


def _adam(gparts, w, m, v, name):
    P, R, C = gparts.shape
    tr = R if R <= 256 else 256

    def body(g_ref, w_ref, m_ref, v_ref, go_ref, d_ref, mo_ref, vo_ref):
        g = g_ref[0].astype(F32)
        for i in range(1, P):
            g = g + g_ref[i].astype(F32)
        d, mn, vn = _adam_math(w_ref[...], g, m_ref[...], v_ref[...])
        go_ref[...] = g
        d_ref[...] = d
        mo_ref[...] = mn
        vo_ref[...] = vn

    t = pl.BlockSpec((tr, C), lambda i: (i, 0))
    return pl.pallas_call(
        body, name=name, grid=(R // tr,),
        in_specs=[pl.BlockSpec((P, tr, C), lambda i: (0, i, 0)), t, t, t], out_specs=[t, t, t, t],
        out_shape=[_sds((R, C), F32)] * 4, compiler_params=_cp(("parallel",)),
    )(gparts, w, m, v)


def _sum_parts(gparts, name):
    P, R, C = gparts.shape

    def body(g_ref, o_ref):
        g = g_ref[0]
        for i in range(1, P):
            g = g + g_ref[i]
        o_ref[...] = g

    return pl.pallas_call(body, name=name, out_shape=_sds((R, C), F32))(gparts)


def _adam_w_ada(c_t, dm, w, m, v):
    R, C = w.shape

    def body(c_ref, dm_ref, w_ref, m_ref, v_ref, go_ref, d_ref, mo_ref, vo_ref):
        g = c_ref[:, 0:1] * dm_ref[0:1, :]
        for b in range(1, 8):
            g = g + c_ref[:, b:b + 1] * dm_ref[b:b + 1, :]
        d, mn, vn = _adam_math(w_ref[...], g, m_ref[...], v_ref[...])
        go_ref[...] = g
        d_ref[...] = d
        mo_ref[...] = mn
        vo_ref[...] = vn

    return pl.pallas_call(body, name="adam_w_ada", out_shape=[_sds((R, C), F32)] * 4,
                          compiler_params=_cp())(c_t, dm, w, m, v)


def kernel(x, c, norm_g, w_ada, b_ada, w_in, pool_w, pool_scale, w_attn_br, w_pool_br, w_out, rel_bias, final_g, loss_target, m_norm_g, m_w_ada, m_b_ada, m_w_in, m_pool_w, m_pool_scale, m_w_attn_br, m_w_pool_br, m_w_out, m_rel_bias, m_final_g, v_norm_g, v_w_ada, v_b_ada, v_w_in, v_pool_w, v_pool_scale, v_w_attn_br, v_w_pool_br, v_w_out, v_rel_bias, v_final_g):
    S = x.shape[1]
    px, py, pc = _my_place()
    me = 4 * px + 2 * py + pc
    cidx = jnp.reshape(pc, (1,)).astype(jnp.int32)

    layout = [(1, 1024), (1, 128), (1, 128), (0, 128)]
    final_g2 = final_g.reshape(1, D)
    me1 = jnp.reshape(me, (1,)).astype(jnp.int32)

    b_part = lax.dynamic_slice(b_ada, (0, me * 384), (1, 384))
    c_all, mod_all = _adaln_exchange(c, w_ada[0], b_part)
    c_all = c_all.reshape(8, D)
    mod = mod_all.reshape(1, 3 * D)
    shift, scale, gate = mod[:, :D], mod[:, D:2 * D], mod[:, 2 * D:]

    gplan = _gather_chips_plan([(None, 1)] * 4)
    lands = _place_slot([w_in[0], w_attn_br[0], w_pool_br[0], w_out[0]], me1, 8, "place_own_shards", CDT)
    g_sems, _, lands, tok = _split_start("gather_chips_start", [], lands, 8, gplan, dep=mod_all)
    st = _local_pre(x[0], shift, scale, norm_g, rel_bias, dep=tok)
    _, lands = _split_wait("gather_chips_wait", g_sems, [], lands, gplan,
                           [st["ht"]] + [b for pair in st["biases"] for b in pair])
    w_in_full, wab8, wpb8, wout8 = _pair_share(lands)
    wab, wpb = _cols_from_blocks(wab8, "w_attn_br_cols"), _cols_from_blocks(wpb8, "w_pool_br_cols")
    wout = wout8.reshape(D, D)

    st = _local_grads(st, loss_target[0], gate, w_in_full, pool_w[0], pool_scale, wab, wpb, wout, final_g2)

    early = [st["dpw"].reshape(AW, 128), st["dps"], st["gfg"], jnp.broadcast_to(st["loss"], (1, 128))]
    dws = [st["dw_in"], st["dw_ab"], st["dw_pb"], st["dw_out"]]
    pplan = _pair_plan(layout)
    pair_shapes = [(4, w, a.shape[1]) if ax == 0 else (4, a.shape[0], w) for a, (ax, w) in zip(dws, layout)]
    p_sems, srcs, recv, tok = _split_start(
        "reduce_pair_start", dws + early, [lax.empty(s, F32) for s in pair_shapes + [a.shape for a in early]], 20,
        pplan)
    d_rel_bias = _local_bias_grads(st, dep=tok)
    srcs, recv = _split_wait("reduce_pair_wait", p_sems, srcs, recv, pplan, d_rel_bias)
    parts = [_pair_add(dw, r, ax, w, cidx, f"pair_add{i}")
             for i, (dw, r, (ax, w)) in enumerate(zip(srcs[:4], recv[:4], layout))]
    parts += _add_pairs(srcs[4:], recv[4:], "pair_add_smalls")

    slot1 = jnp.reshape(2 * px + py, (1,)).astype(jnp.int32)
    cplan = _chips_plan(4)
    c_sems, parts, lands, tok = _split_start("reduce_chips_start", parts,
                                             _place_slot(parts, slot1, 4, "place_own_partials"), 24, cplan)
    grad_x, dmod, d_norm_g = _local_input_grads(st, w_in_full, norm_g, scale, dep=tok)

    dmod_all, ng_all, rb_all = _all_gather_small([dmod, d_norm_g, d_rel_bias], "gather_late")
    o_norm_g = _adam(ng_all, norm_g, m_norm_g, v_norm_g, "adam_norm_g")
    o_rel_bias = _adam(rb_all, rel_bias, m_rel_bias, v_rel_bias, "adam_rel_bias")
    o_b_ada = _adam(dmod_all, b_ada, m_b_ada, v_b_ada, "adam_b_ada")
    dm_mine = lax.dynamic_slice(dmod_all.reshape(8, 3 * D), (0, me * 384), (8, 384))
    o_w_ada = [o[None] for o in _adam_w_ada(c_all.T, dm_mine, w_ada[0], m_w_ada[0], v_w_ada[0])]

    _, (g_in, g_ab, g_pb, g_out, g_pw, g_ps, g_fg, g_loss) = _split_wait(
        "reduce_chips_wait", c_sems, parts, lands, cplan, [o_w_ada[0], o_b_ada[0], o_norm_g[0], o_rel_bias[0]])
    o_w_in = [o[None] for o in _adam(g_in, w_in[0], m_w_in[0], v_w_in[0], "adam_w_in")]
    o_w_ab = [o[None] for o in _adam(g_ab, w_attn_br[0], m_w_attn_br[0], v_w_attn_br[0], "adam_w_attn_br")]
    o_w_pb = [o[None] for o in _adam(g_pb, w_pool_br[0], m_w_pool_br[0], v_w_pool_br[0], "adam_w_pool_br")]
    o_w_out = [o[None] for o in _adam(g_out, w_out[0], m_w_out[0], v_w_out[0], "adam_w_out")]
    o_pool_w = [o.reshape(pool_w.shape) for o in _adam(g_pw, pool_w.reshape(AW, 128), m_pool_w.reshape(AW, 128),
                                                       v_pool_w.reshape(AW, 128), "adam_pool_w")]
    o_pool_scale = _adam(g_ps, pool_scale, m_pool_scale, v_pool_scale, "adam_pool_scale")
    o_final_g = [o.reshape(D) for o in _adam(g_fg, final_g2, m_final_g.reshape(1, D), v_final_g.reshape(1, D),
                                             "adam_final_g")]
    loss = _sum_parts(g_loss, "sum_loss")[0, 0]

    per_w = [o_norm_g, o_w_ada, o_b_ada, o_w_in, o_pool_w, o_pool_scale, o_w_ab, o_w_pb, o_w_out, o_rel_bias, o_final_g]
    outs = [loss, grad_x[None]]
    for k in range(4):
        outs += [o[k] for o in per_w]
    return tuple(outs)
```

```python
import functools
import math

import numpy as np
import jax
import jax.numpy as jnp
from jax import lax
from jax.experimental import pallas as pl
from jax.experimental.pallas import tpu as pltpu

F32 = jnp.float32
CDT = jnp.bfloat16
D = 1024
HD = 64
NH = 8
AW = 512
QKV_W = 4608
REST_W = 3584
R16 = 16
EPS = 1e-6
NEG = -1e30
N_BACK = 128
ADAM_LR, ADAM_B1, ADAM_B2, ADAM_EPS, ADAM_WD, ADAM_STEP = 0.001, 0.9, 0.999, 1e-08, 0.01, 10
MESH = pl.DeviceIdType.MESH
VMEM_MB = 1024 * 1024


def _cp(sem=None, vmem=48, **kw):
    if sem is not None:
        kw["dimension_semantics"] = sem
    return pltpu.CompilerParams(vmem_limit_bytes=vmem * VMEM_MB, **kw)


def _sds(shape, dtype):
    return jax.ShapeDtypeStruct(tuple(shape), dtype)


def _rest_blk(jj):
    return jnp.where(jj == 0, 9, jnp.where(jj == 6, 10, jj + 10))


def _sigmoid(z):
    return 0.5 * jnp.tanh(0.5 * z) + 0.5


def _dot(a, b, dims):
    return lax.dot_general(a, b, (dims, ((), ())), preferred_element_type=F32)


NN = ((1,), (0,))
NT = ((1,), (1,))
TN = ((0,), (0,))


def _ld(ref):
    if len(ref.shape) == 2:
        return ref[...]
    return jnp.concatenate([ref[a] for a in range(ref.shape[0])], axis=0)


def _st(ref, val, cols=None):
    val = val.astype(ref.dtype)
    if len(ref.shape) == 2:
        if cols is None:
            ref[...] = val
        else:
            ref[:, cols] = val
        return
    rows = ref.shape[1]
    for a in range(ref.shape[0]):
        if cols is None:
            ref[a] = val[a * rows:(a + 1) * rows]
        else:
            ref[a, :, cols] = val[a * rows:(a + 1) * rows]


REST_COLS = ((9, 10), (11, 16), (10, 11))
_HBM = pl.BlockSpec(memory_space=pltpu.HBM)


def _resident(shape):
    return pl.BlockSpec(shape, lambda *ids: (0,) * len(shape), pipeline_mode=pl.Buffered(1))


def _w512(w_ref, j):
    return w_ref[j // 2, :, 512 * (j % 2):512 * (j % 2 + 1)]


def _proj(h, w):
    S = h.shape[0]
    tm = 512

    def body(a_ref, w_ref, q_ref, r_ref):
        a = a_ref[...]
        for j in range(9):
            q_ref[:, 512 * j:512 * (j + 1)] = _dot(a, _w512(w_ref, j), NN).astype(q_ref.dtype)
        off = 0
        for lo, hi in REST_COLS:
            for j in range(lo, hi):
                r_ref[:, off:off + 512] = _dot(a, _w512(w_ref, j), NN)
                off += 512

    return pl.pallas_call(
        body, name="proj", grid=(S // tm,),
        in_specs=[pl.BlockSpec((tm, D), lambda i: (i, 0)), _resident((8, D, 1024))],
        out_specs=[pl.BlockSpec((tm, QKV_W), lambda i: (i, 0)), pl.BlockSpec((tm, REST_W), lambda i: (i, 0))],
        out_shape=[_sds((S, QKV_W), CDT), _sds((S, REST_W), F32)],
        compiler_params=_cp(("parallel",), vmem=56),
    )(h, w)


def _after(dep):
    return ([], []) if dep is None else ([pl.BlockSpec(memory_space=pl.ANY)], [dep])


def _dh(dqkv, drest, w, dep=None):
    S = dqkv.shape[0]
    tm = 512
    dep_specs, dep_args = _after(dep)

    def body(a_ref, b_ref, w_ref, *rest):
        o_ref = rest[-1]
        acc = _dot(a_ref[:, 0:512], _w512(w_ref, 0), NT)
        for j in range(1, 9):
            acc = acc + _dot(a_ref[:, 512 * j:512 * (j + 1)], _w512(w_ref, j), NT)
        off = 0
        for lo, hi in REST_COLS:
            for j in range(lo, hi):
                acc = acc + _dot(b_ref[:, off:off + 512], _w512(w_ref, j), NT)
                off += 512
        o_ref[...] = acc

    return pl.pallas_call(
        body, name="dh", grid=(S // tm,),
        in_specs=[pl.BlockSpec((tm, QKV_W), lambda i: (i, 0)), pl.BlockSpec((tm, REST_W), lambda i: (i, 0)),
                  _resident((8, D, 1024))] + dep_specs,
        out_specs=pl.BlockSpec((tm, D), lambda i: (i, 0)),
        out_shape=_sds((S, D), F32),
        compiler_params=_cp(("parallel",), vmem=56),
    )(dqkv, drest, w, *dep_args)


def _mm_tn(at, b, *, n_total, out_blk, name, prev=None, tn=512, with_low=False):
    R, S = at.shape
    nb = b.shape[1] // tn
    n_out = 2 if with_low else 1

    def body(*refs):
        res = _dot(refs[0][...], refs[1][...], NN)
        refs[-n_out][...] = res
        if with_low:
            refs[-1][...] = res.astype(CDT)

    in_specs = [_resident((R, S)), pl.BlockSpec((S, tn), lambda j: (0, j))]
    args = [at, b]
    aliases = {}
    if prev is not None:
        prev = list(prev) if with_low else [prev]
        in_specs += [pl.BlockSpec(memory_space=pl.ANY)] * n_out
        args += prev
        aliases = {2 + k: k for k in range(n_out)}
    out_spec = pl.BlockSpec((R, tn), lambda j: (0, out_blk(j)))
    outs = pl.pallas_call(
        body, name=name, grid=(nb,), in_specs=in_specs,
        out_specs=[out_spec] * n_out,
        out_shape=[_sds((R, n_total), F32), _sds((R, n_total), CDT)][:n_out],
        input_output_aliases=aliases,
        compiler_params=_cp(("parallel",)),
    )(*args)
    return tuple(outs) if with_low else outs[0]


def _tcast(a, name):
    M, C = a.shape
    tm = 512

    def body(a_ref, o_ref):
        o_ref[...] = a_ref[...].astype(F32).T.astype(o_ref.dtype)

    return pl.pallas_call(
        body, name=name, grid=(M // tm,),
        in_specs=[pl.BlockSpec((tm, C), lambda i: (i, 0))],
        out_specs=pl.BlockSpec((C, tm), lambda i: (0, i)),
        out_shape=_sds((C, M), CDT), compiler_params=_cp(("parallel",)),
    )(a)


TOK = R16 * R16


def _swap16():
    i = lax.broadcasted_iota(jnp.int32, (TOK, TOK), 0)
    j = lax.broadcasted_iota(jnp.int32, (TOK, TOK), 1)
    return jnp.where(j == jnp.bitwise_and(i, 15) * 16 + jnp.right_shift(i, 4), 1.0, 0.0).astype(CDT)


def _norm_mod(x, norm_g, shift, scale, dep=None):
    S = x.shape[0]
    dep_specs, dep_args = _after(dep)

    def body(x_ref, g_ref, sh_ref, sc_ref, *rest):
        xv = x_ref[...]
        xn = xv * lax.rsqrt(jnp.mean(xv * xv, axis=-1, keepdims=True) + EPS)
        h = (xn * (g_ref[...] * (1.0 + sc_ref[...])) + sh_ref[...]).astype(CDT)
        _st(rest[-1], _dot(_swap16(), h, NN))

    vec = pl.BlockSpec((1, D), lambda i: (0, 0))
    return pl.pallas_call(
        body, name="norm_mod", grid=(S // TOK,),
        in_specs=[pl.BlockSpec((TOK, D), lambda i: (i, 0)), vec, vec, vec] + dep_specs,
        out_specs=pl.BlockSpec((R16, R16, D), lambda i: (0, i, 0)),
        out_shape=_sds((R16, S // R16, D), CDT), compiler_params=_cp(("parallel",)),
    )(x, norm_g, shift, scale, *dep_args)


def _t5_bucket_np(n):
    nf = np.maximum(n, 1).astype(np.float32)
    large = 16 + (np.log(nf / np.float32(16)) / np.float32(math.log(2048 / 16)) * np.float32(16)).astype(np.int32)
    large = np.minimum(large, 31)
    return np.where(n < 16, n, large).astype(np.int32)


class _Geom:
    def __init__(self, gi, LS):
        self.gi, self.LS = gi, LS
        self.dil = (1, 4, 16)[gi]
        if gi == 0:
            self.lead, self.rows, self.n_tiles, self.has_prev = (R16,), 16, LS // 16, True
            a = np.arange(R16)[:, None]
            ll = np.arange(16)[None, :]
            self.pos = (16 * ll + a).reshape(-1)
        elif gi == 1:
            self.lead, self.rows, self.n_tiles, self.has_prev = (4,), 64, LS // 64, True
            a = np.arange(4)[:, None]
            ll = np.arange(64)[None, :]
            self.pos = (4 * ll + a).reshape(-1)
        else:
            self.lead, self.rows, self.n_tiles, self.has_prev = (), LS, 1, False
            self.pos = np.arange(LS)
        self.T = self.pos.shape[0]
        self.Tk = 2 * self.T if self.has_prev else self.T
        self.outer = {0: (), 1: (4,), 2: (R16,)}[gi]

    def view(self, arr):
        S, C = arr.shape
        if self.gi == 1:
            return arr.reshape(4, 4, self.LS, C)
        return arr.reshape(R16, self.LS, C)

    def view_shape(self, C):
        return (4, 4, self.LS, C) if self.gi == 1 else (R16, self.LS, C)

    def unview(self, arr):
        return arr.reshape(R16 * self.LS, arr.shape[-1])

    def spec(self, W, cblk, tile):
        if self.gi == 0:
            return pl.BlockSpec((R16, 16, W), lambda n: (0, tile(n), cblk))
        if self.gi == 1:
            return pl.BlockSpec((4, None, 64, W), lambda r, n: (0, r, tile(n), cblk))
        return pl.BlockSpec((None, self.LS, W), lambda r: (r, 0, cblk))

    def halves(self):
        R = self.rows
        idx = np.arange(self.T).reshape(-1, R)
        return idx[:, :R // 2].reshape(-1), idx[:, R // 2:].reshape(-1)

    def bucket_index(self, sub):
        lo, hi = self.halves()
        pos = self.pos
        if sub == 0:
            pq = pos[lo]
            pk = np.concatenate([pos[hi] - self.T, pos[lo]]) if self.has_prev else np.concatenate([pos[lo], pos[hi]])
        else:
            pq = pos[hi]
            pk = np.concatenate([pos[lo], pos[hi]])
        dist = pq[:, None] - pk[None, :]
        ok = (dist >= 0) & (dist <= N_BACK)
        bucket = _t5_bucket_np(np.clip(dist, 0, N_BACK) * self.dil)
        return np.where(ok, bucket, -1).astype(np.int32)


def _bias_table(idx, buckets, rel_bias, gi, sub, dep=None):
    T, Tk = idx.shape
    tr = 32
    dep_specs, dep_args = _after(dep)

    def body(idx_ref, rb_ref, *rest):
        h = pl.program_id(0)
        ix = idx_ref[...]
        acc = jnp.full(ix.shape, NEG, F32)
        for b in buckets:
            acc = jnp.where(ix == b, rb_ref[b, gi * NH + h], acc)
        rest[-1][...] = acc

    return pl.pallas_call(
        body, name=f"bias_table{gi}{'ab'[sub]}", grid=(NH, T // tr),
        in_specs=[pl.BlockSpec((tr, Tk), lambda h, i: (i, 0)), pl.BlockSpec(memory_space=pltpu.SMEM)] + dep_specs,
        out_specs=pl.BlockSpec((None, tr, Tk), lambda h, i: (h, i, 0)),
        out_shape=_sds((NH, T, Tk), F32), compiler_params=_cp(("parallel", "parallel")),
    )(idx, rel_bias, *dep_args)


def _bias_grad(dsum, idx, buckets, gi, sub, dep=None):
    _, T, Tk = dsum.shape
    dep_specs, dep_args = _after(dep)

    def body(ds_ref, idx_ref, *rest):
        o_ref, r_ref = rest[-2], rest[-1]
        r_ref[...] = jnp.zeros_like(r_ref)
        ix, ds = idx_ref[...], ds_ref[...]
        for b in buckets:
            r_ref[b:b + 1, :] = jnp.sum(jnp.where(ix == b, ds, 0.0), axis=0, keepdims=True)
        o_ref[...] = jnp.broadcast_to(jnp.sum(r_ref[...], axis=1, keepdims=True), (32, 128))

    out = pl.pallas_call(
        body, name=f"bias_grad{gi}{'ab'[sub]}", grid=(NH,),
        in_specs=[pl.BlockSpec((None, T, Tk), lambda h: (h, 0, 0)), pl.BlockSpec((T, Tk), lambda h: (0, 0))] + dep_specs,
        out_specs=pl.BlockSpec((None, 32, 128), lambda h: (h, 0, 0)),
        out_shape=_sds((NH, 32, 128), F32), scratch_shapes=[pltpu.VMEM((32, Tk), F32)],
        compiler_params=_cp(("parallel",)),
    )(dsum, idx, *dep_args)
    return out[:, :, 0].T


def _prev_mask(n, T, Tk):
    col = lax.broadcasted_iota(jnp.int32, (1, Tk), 1)
    return jnp.where((col < T) & (n == 0), NEG, 0.0).astype(F32)


def _split_rows(v, R):
    A, half = v.shape[0] // R, R // 2
    via_f32 = v.dtype != F32 and half % 16 != 0
    w = v.astype(F32) if via_f32 else v
    lo = jnp.concatenate([w[a * R:a * R + half] for a in range(A)], axis=0)
    hi = jnp.concatenate([w[a * R + half:(a + 1) * R] for a in range(A)], axis=0)
    return (lo.astype(v.dtype), hi.astype(v.dtype)) if via_f32 else (lo, hi)


def _merge_rows(lo, hi, R):
    half = R // 2
    A = lo.shape[0] // half
    return jnp.concatenate([x[a * half:(a + 1) * half] for a in range(A) for x in (lo, hi)], axis=0)


def _attn_fwd(qkv, biases, geom):
    S = qkv.shape[0]
    gi, T, R, has_prev = geom.gi, geom.T, geom.rows, geom.has_prev
    H2 = T // 2
    qv = geom.view(qkv)
    cq, ck, cv = 3 * gi, 3 * gi + 1, 3 * gi + 2

    def body(*refs):
        if has_prev:
            q_ref, kp_ref, kc_ref, vp_ref, vc_ref, ba_ref, bb_ref, o_ref, l_ref = refs
        else:
            q_ref, kc_ref, vc_ref, ba_ref, bb_ref, o_ref, l_ref = refs
        q_lo, q_hi = _split_rows(_ld(q_ref), R)
        kc_lo, kc_hi = _split_rows(_ld(kc_ref), R)
        vc_lo, vc_hi = _split_rows(_ld(vc_ref), R)
        k_b = jnp.concatenate([kc_lo, kc_hi], axis=0)
        v_b = jnp.concatenate([vc_lo, vc_hi], axis=0)
        if has_prev:
            n = pl.program_id(len(geom.outer))
            k_a = jnp.concatenate([_split_rows(_ld(kp_ref), R)[1], kc_lo], axis=0)
            v_a = jnp.concatenate([_split_rows(_ld(vp_ref), R)[1], vc_lo], axis=0)
            pm = _prev_mask(n, H2, T)
        else:
            k_a, v_a, pm = k_b, v_b, None
        low = lax.broadcasted_iota(jnp.int32, (1, 2 * HD), 1) < HD
        for j in range(NH // 2):
            cols = slice(2 * HD * j, 2 * HD * (j + 1))
            o_half, l_half = [], []
            for q, k, v, b_ref, mask in ((q_lo, k_a, v_a, ba_ref, pm), (q_hi, k_b, v_b, bb_ref, None)):
                q2 = q[:, cols].astype(F32) * 0.125
                k2, v2 = k[:, cols], v[:, cols]
                qs = jnp.concatenate([jnp.where(low, q2, 0.0), jnp.where(low, 0.0, q2)], axis=0).astype(q.dtype)
                s = _dot(qs, k2, NT) + b_ref[j]
                if mask is not None:
                    s = s + mask
                m = jnp.max(s, axis=1, keepdims=True)
                p = jnp.exp(s - m)
                l = jnp.sum(p, axis=1, keepdims=True)
                pv = _dot(p.astype(v.dtype), v2, NN) / l
                lse = m + jnp.log(l)
                o_half.append(jnp.where(low, pv[:H2], pv[H2:]))
                l_half.append(jnp.where(low, lse[:H2], lse[H2:]))
            _st(o_ref, _merge_rows(o_half[0], o_half[1], R), cols)
            _st(l_ref, _merge_rows(l_half[0], l_half[1], R), cols)

    cur = lambda n: n
    prev = lambda n: jnp.maximum(n - 1, 0)
    if has_prev:
        in_specs = [geom.spec(AW, cq, cur), geom.spec(AW, ck, prev), geom.spec(AW, ck, cur),
                    geom.spec(AW, cv, prev), geom.spec(AW, cv, cur)]
        args = [qv] * 5
    else:
        in_specs = [geom.spec(AW, cq, cur), geom.spec(AW, ck, cur), geom.spec(AW, cv, cur)]
        args = [qv] * 3
    nd = len(geom.outer) + (1 if has_prev else 0)
    biases = [b.reshape(NH // 2, 2 * H2, b.shape[2]) for b in biases]
    in_specs += [pl.BlockSpec(b.shape, lambda *ids: (0, 0, 0)) for b in biases]
    grid = geom.outer + ((geom.n_tiles,) if has_prev else ())
    oshape = geom.view_shape(AW)
    o, l = pl.pallas_call(
        body, name=f"attn_fwd{gi}", grid=grid, in_specs=in_specs,
        out_specs=[geom.spec(AW, 0, cur), geom.spec(AW, 0, cur)],
        out_shape=[_sds(oshape, F32), _sds(oshape, F32)],
        compiler_params=_cp(("arbitrary",) * nd),
    )(*args, *biases)
    return geom.unview(o), geom.unview(l)


def _attn_combine(os_, ls_, rest):
    S = rest.shape[0]
    tm = 512

    def body(o0, o1, o2, l0, l1, l2, z_ref, attn_ref, L_ref, ag_ref, agt_ref):
        la, lb, lc = l0[...], l1[...], l2[...]
        m = jnp.maximum(jnp.maximum(la, lb), lc)
        ea, eb, ec = jnp.exp(la - m), jnp.exp(lb - m), jnp.exp(lc - m)
        den = ea + eb + ec
        attn = (ea * o0[...] + eb * o1[...] + ec * o2[...]) / den
        attn_ref[...] = attn
        L_ref[...] = m + jnp.log(den)
        z = z_ref[...]
        ag = attn * (z * _sigmoid(z))
        ag_ref[...] = ag.astype(ag_ref.dtype)
        agt_ref[...] = ag.T.astype(agt_ref.dtype)

    t = pl.BlockSpec((tm, AW), lambda i: (i, 0))
    return pl.pallas_call(
        body, name="attn_combine", grid=(S // tm,),
        in_specs=[t] * 6 + [pl.BlockSpec((tm, AW), lambda i: (i, 0))],
        out_specs=[t, t, t, pl.BlockSpec((AW, tm), lambda i: (0, i))],
        out_shape=[_sds((S, AW), F32), _sds((S, AW), F32), _sds((S, AW), CDT), _sds((AW, S), CDT)],
        compiler_params=_cp(("parallel",)),
    )(*os_, *ls_, rest)


def _pool_counts(r, LS, win):
    l = lax.broadcasted_iota(jnp.int32, (LS, 1), 0)
    return jnp.minimum(16 * l + (r + 1), win).astype(F32)


def _pool_fwd(rest, pool_w, pool_scale):
    S = rest.shape[0]
    LS = S // R16
    r3 = rest.reshape(R16, LS, REST_W)
    PG = 128

    def body(u_ref, z_ref, pw_ref, ps_ref, pt_ref, mp_ref, pg_ref, pgt_ref, e_ref):
        g = pl.program_id(0)
        win = jnp.left_shift(2, g)
        row = lax.broadcasted_iota(jnp.int32, (LS, PG), 0)
        for r in range(R16):
            e_ref[R16 + r] = u_ref[r]
        for r in range(1, R16):
            e_ref[r] = jnp.where(row == 0, 0.0, pltpu.roll(u_ref[r], 1, 0))

        def double(s):
            for i in range(2 * R16 - 1, 2 * s - 1, -1):
                e_ref[i] = e_ref[i] + e_ref[i - s]

        double(1)
        for k in (1, 2, 3):
            pl.when(g >= k)(functools.partial(double, 2 ** k))
        pw = pw_ref[...].astype(CDT)
        for r in range(R16):
            u = u_ref[r]
            pooled = e_ref[R16 + r] / _pool_counts(r, LS, win) - u
            pc = pooled.astype(CDT)
            mp = _dot(pc, pw, NN)
            z = z_ref[r]
            pg = (mp * ps_ref[...]) * (z * _sigmoid(z))
            mp_ref[r] = mp
            pg_ref[r] = pg.astype(pg_ref.dtype)
            pt_ref[:, r * LS:(r + 1) * LS] = pooled.T.astype(pt_ref.dtype)
            pgt_ref[:, r * LS:(r + 1) * LS] = pg.T.astype(pgt_ref.dtype)

    col = lambda off: pl.BlockSpec((R16, LS, PG), lambda g: (0, 0, off + g))
    tsp = pl.BlockSpec((PG, S), lambda g: (g, 0))
    pt, mp, pg, pgt = pl.pallas_call(
        body, name="pool_fwd", grid=(4,),
        in_specs=[col(24), col(4), pl.BlockSpec((None, PG, PG), lambda g: (g, 0, 0)),
                  pl.BlockSpec((1, PG), lambda g: (0, g))],
        out_specs=[tsp, col(0), col(0), tsp],
        out_shape=[_sds((AW, S), CDT), _sds((R16, LS, AW), F32), _sds((R16, LS, AW), CDT), _sds((AW, S), CDT)],
        scratch_shapes=[pltpu.VMEM((2 * R16, LS, PG), F32)],
        compiler_params=_cp(("parallel",)),
    )(r3, r3, pool_w, pool_scale)
    return pt, mp.reshape(S, AW), pg.reshape(S, AW), pgt


def _branch_merge(ag, pg, wab, wpb, rest):
    S = ag.shape[0]
    tm, tn = 512, 512

    def body(ag_ref, pg_ref, wa_ref, wp_ref, ga_ref, gp_ref, m_ref, mt_ref):
        ya = _dot(ag_ref[...], wa_ref[...], NN)
        yp = _dot(pg_ref[...], wp_ref[...], NN)
        mg = _sigmoid(ga_ref[...]) * ya + _sigmoid(gp_ref[...]) * yp
        m_ref[...] = mg.astype(m_ref.dtype)
        mt_ref[...] = mg.T.astype(mt_ref.dtype)

    a = pl.BlockSpec((tm, AW), lambda i, j: (i, 0))
    w = pl.BlockSpec((AW, tn), lambda i, j: (0, j))
    o = pl.BlockSpec((tm, tn), lambda i, j: (i, j))
    return pl.pallas_call(
        body, name="branch_merge", grid=(S // tm, D // tn),
        in_specs=[a, a, w, w, pl.BlockSpec((tm, tn), lambda i, j: (i, 2 + j)),
                  pl.BlockSpec((tm, tn), lambda i, j: (i, 4 + j))],
        out_specs=[o, pl.BlockSpec((tn, tm), lambda i, j: (j, i))],
        out_shape=[_sds((S, D), CDT), _sds((D, S), CDT)],
        compiler_params=_cp(("parallel", "parallel")),
    )(ag, pg, wab, wpb, rest, rest)


def _out_loss(merged, wout, x, target, gate, final_g):
    S = x.shape[0]

    def body(m_ref, w_ref, x_ref, t_ref, gate_ref, fg_ref, dx2_ref, dmo_ref, loss_ref, gfg_ref, dgate_ref):
        i = pl.program_id(0)

        @pl.when(i == 0)
        def _():
            loss_ref[...] = jnp.zeros_like(loss_ref)
            gfg_ref[...] = jnp.zeros_like(gfg_ref)
            dgate_ref[...] = jnp.zeros_like(dgate_ref)

        swap = _swap16()
        mo = _dot(_dot(swap, _ld(m_ref), NN).astype(CDT), w_ref[...], NN)
        gate, fg = gate_ref[...], fg_ref[...]
        x2 = x_ref[...] + gate * mo
        r2 = lax.rsqrt(jnp.mean(x2 * x2, axis=-1, keepdims=True) + EPS)
        xn2 = x2 * r2
        diff = xn2 * fg - t_ref[...]
        loss = jnp.sum(jnp.sum(diff * diff, axis=-1, keepdims=True), axis=0, keepdims=True)
        dy = diff * (1.0 / D)
        dxn2 = dy * fg
        dx2 = r2 * (dxn2 - xn2 * jnp.mean(dxn2 * xn2, axis=-1, keepdims=True))
        dx2_ref[...] = dx2
        _st(dmo_ref, _dot(swap, (dx2 * gate).astype(CDT), NN))
        loss_ref[...] += jnp.broadcast_to(loss * (0.5 / D), loss_ref.shape)
        gfg_ref[...] += jnp.sum(dy * xn2, axis=0, keepdims=True)
        dgate_ref[...] += jnp.sum(dx2 * mo, axis=0, keepdims=True)

    slab = pl.BlockSpec((R16, R16, D), lambda i: (0, i, 0))
    nat = pl.BlockSpec((TOK, D), lambda i: (i, 0))
    vec = pl.BlockSpec((1, D), lambda i: (0, 0))
    return pl.pallas_call(
        body, name="out_loss", grid=(S // TOK,),
        in_specs=[slab, _resident((D, D)), nat, nat, vec, vec],
        out_specs=[nat, slab, pl.BlockSpec((1, 128), lambda i: (0, 0)), vec, vec],
        out_shape=[_sds((S, D), F32), _sds((R16, S // R16, D), CDT), _sds((1, 128), F32), _sds((1, D), F32),
                   _sds((1, D), F32)],
        compiler_params=_cp(("arbitrary",)),
    )(merged, wout, x, target, gate, final_g)


def _bwd_gates(dmo, wout, wab, wpb, rest, ag, pg, attn, mp, pool_scale):
    S = dmo.shape[0]
    tm = 256

    def body(dmo_ref, wo_ref, wa_ref, wp_ref, za_ref, zp_ref, ga0, ga1, gp0, gp1, ag_ref, pg_ref, at_ref, mp_ref,
             ps_ref, dya_ref, dyp_ref, dr_ref, dat_ref, dl_ref, dmp_ref, dps_ref):
        i = pl.program_id(0)

        @pl.when(i == 0)
        def _():
            dps_ref[...] = jnp.zeros_like(dps_ref)

        dm = _dot(dmo_ref[...], wo_ref[...], NT)
        sa = _sigmoid(jnp.concatenate([ga0[...], ga1[...]], axis=1))
        sp = _sigmoid(jnp.concatenate([gp0[...], gp1[...]], axis=1))
        dya = (dm * sa).astype(CDT)
        dyp = (dm * sp).astype(CDT)
        dya_ref[...] = dya
        dyp_ref[...] = dyp
        ya = _dot(ag_ref[...], wa_ref[...], NN)
        yp = _dot(pg_ref[...], wp_ref[...], NN)
        dr_ref[:, 1024:2048] = (dm * ya * sa * (1.0 - sa)).astype(dr_ref.dtype)
        dr_ref[:, 2048:3072] = (dm * yp * sp * (1.0 - sp)).astype(dr_ref.dtype)
        da = _dot(dya, wa_ref[...], NT)
        dp = _dot(dyp, wp_ref[...], NT)
        za = za_ref[...]
        sga = _sigmoid(za)
        attn = at_ref[...]
        dattn = da * (za * sga)
        dr_ref[:, 0:512] = (da * attn * (sga * (1.0 + za * (1.0 - sga)))).astype(dr_ref.dtype)
        dat_ref[...] = dattn.astype(dat_ref.dtype)
        prod = dattn * attn
        low = lax.broadcasted_iota(jnp.int32, (1, 2 * HD), 1) < HD
        for j in range(NH // 2):
            x2 = prod[:, 2 * HD * j:2 * HD * (j + 1)]
            s0 = jnp.sum(jnp.where(low, x2, 0.0), axis=1, keepdims=True)
            s1 = jnp.sum(jnp.where(low, 0.0, x2), axis=1, keepdims=True)
            dl_ref[:, 2 * HD * j:2 * HD * (j + 1)] = jnp.where(low, s0, s1)
        zp = zp_ref[...]
        sgp = _sigmoid(zp)
        mpre = mp_ref[...]
        ps = ps_ref[...]
        dmixed = dp * (zp * sgp)
        dr_ref[:, 512:1024] = (dp * (mpre * ps) * (sgp * (1.0 + zp * (1.0 - sgp)))).astype(dr_ref.dtype)
        dps_ref[...] += jnp.sum(dmixed * mpre, axis=0, keepdims=True)
        dmp_ref[...] = (dmixed * ps).astype(dmp_ref.dtype)

    full = lambda shape: pl.BlockSpec(shape, lambda i: (0, 0))
    tD = pl.BlockSpec((tm, D), lambda i: (i, 0))
    tA = pl.BlockSpec((tm, AW), lambda i: (i, 0))
    rb = lambda c: pl.BlockSpec((tm, AW), lambda i: (i, c))
    return pl.pallas_call(
        body, name="bwd_gates", grid=(S // tm,),
        in_specs=[tD, full((D, D)), full((AW, D)), full((AW, D)), rb(0), rb(1), rb(2), rb(3), rb(4), rb(5),
                  tA, tA, tA, tA, full((1, AW))],
        out_specs=[tD, tD, pl.BlockSpec((tm, 3072), lambda i: (i, 0)), tA, tA, tA, full((1, AW))],
        out_shape=[_sds((S, D), CDT), _sds((S, D), CDT), _sds((S, REST_W), CDT), _sds((S, AW), CDT),
                   _sds((S, AW), F32), _sds((S, AW), CDT), _sds((1, AW), F32)],
        compiler_params=_cp(("arbitrary",), vmem=56),
    )(dmo, wout, wab, wpb, rest, rest, rest, rest, rest, rest, ag, pg, attn, mp, pool_scale)


def _pool_bwd(dmp, pt, pool_w, drest):
    S = dmp.shape[0]
    LS = S // R16
    PG = 128
    d3 = dmp.reshape(R16, LS, AW)
    dr3 = drest.reshape(R16, LS, REST_W)

    def body(d_ref, pt_ref, pw_ref, dr_in, du_ref, dpw_ref, g_ref, s_ref):
        del dr_in
        g = pl.program_id(0)
        win = jnp.left_shift(2, g)
        row = lax.broadcasted_iota(jnp.int32, (LS, PG), 0)
        pw = pw_ref[...].astype(CDT)
        dpw = jnp.zeros((PG, PG), F32)
        for r in range(R16):
            dm = d_ref[r]
            dpw = dpw + _dot(pt_ref[:, r * LS:(r + 1) * LS], dm, NN)
            dpooled = _dot(dm, pw, NT)
            g_ref[r] = dpooled
            sc = dpooled / _pool_counts(r, LS, win)
            s_ref[r] = sc
            if r < R16 - 1:
                s_ref[R16 + r] = jnp.where(row == LS - 1, 0.0, pltpu.roll(sc, LS - 1, 0))
        dpw_ref[...] = dpw

        def double(s):
            for i in range(0, 2 * R16 - 2 * s):
                s_ref[i] = s_ref[i] + s_ref[i + s]

        double(1)
        for k in (1, 2, 3):
            pl.when(g >= k)(functools.partial(double, 2 ** k))
        for r in range(R16):
            du_ref[r] = (s_ref[r] - g_ref[r]).astype(du_ref.dtype)

    du, dpw = pl.pallas_call(
        body, name="pool_bwd", grid=(4,),
        in_specs=[pl.BlockSpec((R16, LS, PG), lambda g: (0, 0, g)), pl.BlockSpec((PG, S), lambda g: (g, 0)),
                  pl.BlockSpec((None, PG, PG), lambda g: (g, 0, 0)), pl.BlockSpec(memory_space=pl.ANY)],
        out_specs=[pl.BlockSpec((R16, LS, PG), lambda g: (0, 0, 24 + g)),
                   pl.BlockSpec((None, PG, PG), lambda g: (g, 0, 0))],
        out_shape=[_sds((R16, LS, REST_W), CDT), _sds((4, PG, PG), F32)],
        scratch_shapes=[pltpu.VMEM((R16, LS, PG), F32), pltpu.VMEM((2 * R16, LS, PG), F32)],
        input_output_aliases={3: 0},
        compiler_params=_cp(("parallel",)),
    )(d3, pt, pool_w, dr3)
    return du.reshape(S, REST_W), dpw


def _attn_bwd(qkv, dattn, Lb, dlb, biases, geom, dqkv_prev):
    S = qkv.shape[0]
    gi, T, R, has_prev, N = geom.gi, geom.T, geom.rows, geom.has_prev, geom.n_tiles
    H2 = T // 2
    nd_outer = len(geom.outer)
    qv = geom.view(qkv)
    cq, ck, cv = 3 * gi, 3 * gi + 1, 3 * gi + 2
    W3 = 3 * AW

    def heads(q_ref, kp_ref, kc_ref, vp_ref, vc_ref, do_ref, L_ref, dl_ref, ba_ref, bb_ref, pm, dsa_ref, dsb_ref, emit):
        q_lo, q_hi = _split_rows(_ld(q_ref), R)
        do_lo, do_hi = _split_rows(_ld(do_ref), R)
        L_lo, L_hi = _split_rows(_ld(L_ref), R)
        dl_lo, dl_hi = _split_rows(_ld(dl_ref), R)
        kc_lo, kc_hi = _split_rows(_ld(kc_ref), R)
        vc_lo, vc_hi = _split_rows(_ld(vc_ref), R)
        k_b = jnp.concatenate([kc_lo, kc_hi], axis=0)
        v_b = jnp.concatenate([vc_lo, vc_hi], axis=0)
        if has_prev:
            k_a = jnp.concatenate([_split_rows(_ld(kp_ref), R)[1], kc_lo], axis=0)
            v_a = jnp.concatenate([_split_rows(_ld(vp_ref), R)[1], vc_lo], axis=0)
        else:
            k_a, v_a = k_b, v_b
        low = lax.broadcasted_iota(jnp.int32, (1, 2 * HD), 1) < HD
        for j in range(NH // 2):
            sl = slice(2 * HD * j, 2 * HD * (j + 1))
            res = []
            for q, k, v, do, L, dl, b_ref, mask, ds_ref in (
                    (q_lo, k_a, v_a, do_lo, L_lo, dl_lo, ba_ref, pm, dsa_ref),
                    (q_hi, k_b, v_b, do_hi, L_hi, dl_hi, bb_ref, None, dsb_ref)):
                q2 = q[:, sl].astype(F32) * 0.125
                do2 = do[:, sl].astype(F32)
                k2, v2 = k[:, sl], v[:, sl]
                L2, dl2 = L[:, sl], dl[:, sl]
                qs = jnp.concatenate([jnp.where(low, q2, 0.0), jnp.where(low, 0.0, q2)], axis=0).astype(q.dtype)
                dom = jnp.concatenate([jnp.where(low, do2, 0.0), jnp.where(low, 0.0, do2)], axis=0).astype(q.dtype)
                Lc = jnp.concatenate([L2[:, 0:1], L2[:, HD:HD + 1]], axis=0)
                dlc = jnp.concatenate([dl2[:, 0:1], dl2[:, HD:HD + 1]], axis=0)
                s = _dot(qs, k2, NT) + b_ref[j]
                if mask is not None:
                    s = s + mask
                p = jnp.exp(s - Lc)
                ds = p * (_dot(dom, v2, NT) - dlc)
                ds_ref[j] += ds
                dsc = ds.astype(q.dtype)
                dq_st = _dot(dsc, k2, NN)
                res.append((jnp.where(low, dq_st[:H2], dq_st[H2:]) * 0.125, _dot(dsc, qs, TN),
                            _dot(p.astype(q.dtype), dom, TN)))
            (dq_lo, dk_a, dv_a), (dq_hi, dk_b, dv_b) = res
            dq = _merge_rows(dq_lo, dq_hi, R)
            if has_prev:
                emit(sl, dq, _merge_rows(dk_a[H2:] + dk_b[:H2], dk_b[H2:], R),
                     _merge_rows(dv_a[H2:] + dv_b[:H2], dv_b[H2:], R), dk_a[:H2], dv_a[:H2])
            else:
                emit(sl, dq, _merge_rows(dk_a[:H2] + dk_b[:H2], dk_a[H2:] + dk_b[H2:], R),
                     _merge_rows(dv_a[:H2] + dv_b[:H2], dv_a[H2:] + dv_b[H2:], R), None, None)

    def first_step():
        ok = pl.program_id(nd_outer) == 0 if has_prev else pl.program_id(0) == 0
        for a in range(nd_outer if has_prev else 0):
            ok = ok & (pl.program_id(a) == 0)
        return ok

    if has_prev:
        def body(q_ref, kp_ref, kc_ref, vp_ref, vc_ref, do_ref, L_ref, dl_ref, ba_ref, bb_ref, prev_in, out_ref,
                 dsa_ref, dsb_ref, hold, new, pp):
            del prev_in
            s_id = pl.program_id(nd_outer)

            @pl.when(first_step())
            def _():
                dsa_ref[...] = jnp.zeros_like(dsa_ref)
                dsb_ref[...] = jnp.zeros_like(dsb_ref)

            @pl.when(s_id < N)
            def _():
                def emit(sl, dq, dk, dv, dk_prev, dv_prev):
                    new[:, sl] = dq
                    new[:, slice(AW + sl.start, AW + sl.stop)] = dk
                    new[:, slice(2 * AW + sl.start, 2 * AW + sl.stop)] = dv
                    pp[:, sl] = dk_prev
                    pp[:, slice(AW + sl.start, AW + sl.stop)] = dv_prev

                heads(q_ref, kp_ref, kc_ref, vp_ref, vc_ref, do_ref, L_ref, dl_ref, ba_ref, bb_ref,
                      _prev_mask(s_id, H2, T), dsa_ref, dsb_ref, emit)

            @pl.when((s_id >= 1) & (s_id < N))
            def _():
                _st(out_ref, hold[:, 0:AW], slice(0, AW))
                late = _merge_rows(jnp.zeros((H2, 2 * AW), F32), pp[...], R)
                _st(out_ref, hold[:, AW:W3] + late, slice(AW, W3))

            @pl.when(s_id == N)
            def _():
                _st(out_ref, hold[...])

            @pl.when(s_id < N)
            def _():
                hold[...] = new[...]

        cur = lambda n: jnp.minimum(n, N - 1)
        prev = lambda n: jnp.clip(n - 1, 0, N - 1)
        delayed = lambda n: jnp.maximum(n - 1, 0)
        in_specs = [geom.spec(AW, cq, cur), geom.spec(AW, ck, prev), geom.spec(AW, ck, cur),
                    geom.spec(AW, cv, prev), geom.spec(AW, cv, cur),
                    geom.spec(AW, 0, cur), geom.spec(AW, 0, cur), geom.spec(AW, 0, cur)]
        args = [qv] * 5 + [geom.view(dattn), geom.view(Lb), geom.view(dlb)]
        out_spec = geom.spec(W3, gi, delayed)
        grid = geom.outer + (N + 1,)
        scratch = [pltpu.VMEM((T, W3), F32), pltpu.VMEM((T, W3), F32), pltpu.VMEM((H2, 2 * AW), F32)]
    else:
        def body(q_ref, kc_ref, vc_ref, do_ref, L_ref, dl_ref, ba_ref, bb_ref, prev_in, out_ref, dsa_ref, dsb_ref):
            del prev_in

            @pl.when(first_step())
            def _():
                dsa_ref[...] = jnp.zeros_like(dsa_ref)
                dsb_ref[...] = jnp.zeros_like(dsb_ref)

            def emit(sl, dq, dk, dv, dk_prev, dv_prev):
                _st(out_ref, dq, sl)
                _st(out_ref, dk, slice(AW + sl.start, AW + sl.stop))
                _st(out_ref, dv, slice(2 * AW + sl.start, 2 * AW + sl.stop))

            heads(q_ref, None, kc_ref, None, vc_ref, do_ref, L_ref, dl_ref, ba_ref, bb_ref, None, dsa_ref, dsb_ref,
                  emit)

        cur = lambda n: n
        in_specs = [geom.spec(AW, cq, cur), geom.spec(AW, ck, cur), geom.spec(AW, cv, cur),
                    geom.spec(AW, 0, cur), geom.spec(AW, 0, cur), geom.spec(AW, 0, cur)]
        args = [qv] * 3 + [geom.view(dattn), geom.view(Lb), geom.view(dlb)]
        out_spec = geom.spec(W3, gi, cur)
        grid = geom.outer
        scratch = []

    whole = lambda b: pl.BlockSpec(b.shape, lambda *ids: (0, 0, 0))
    biases = [b.reshape(NH // 2, 2 * H2, b.shape[2]) for b in biases]
    in_specs += [whole(b) for b in biases]
    in_specs.append(pl.BlockSpec(memory_space=pl.ANY))
    if dqkv_prev is None:
        last, aliases = biases[0], {}
    else:
        last, aliases = geom.view(dqkv_prev), {len(in_specs) - 1: 0}
    dq, dsa, dsb = pl.pallas_call(
        body, name=f"attn_bwd{gi}", grid=grid, in_specs=in_specs,
        out_specs=[out_spec] + [whole(b) for b in biases],
        out_shape=[_sds(geom.view_shape(QKV_W), CDT)] + [_sds(b.shape, F32) for b in biases],
        scratch_shapes=scratch,
        input_output_aliases=aliases,
        compiler_params=_cp(("arbitrary",) * len(grid), vmem=56),
    )(*args, *biases, last)
    return geom.unview(dq), [d.reshape(NH, H2, d.shape[2]) for d in (dsa, dsb)]


def _h_bwd(dh, dx2, x, norm_g, scale):
    S = x.shape[0]

    def body(dh_ref, dx2_ref, x_ref, g_ref, sc_ref, gx_ref, dsh_ref, dsc_ref, dng_ref, nat_ref):
        i = pl.program_id(0)

        @pl.when(i == 0)
        def _():
            dsh_ref[...] = jnp.zeros_like(dsh_ref)
            dsc_ref[...] = jnp.zeros_like(dsc_ref)
            dng_ref[...] = jnp.zeros_like(dng_ref)

        for r in range(R16):
            nat_ref[:, r, :] = dh_ref[r]
        dh = jnp.concatenate([nat_ref[l] for l in range(R16)], axis=0)
        g = g_ref[...]
        one_sc = 1.0 + sc_ref[...]
        xv = x_ref[...]
        r1 = lax.rsqrt(jnp.mean(xv * xv, axis=-1, keepdims=True) + EPS)
        xn = xv * r1
        dhx = dh * xn
        dxn = dh * (g * one_sc)
        dx1 = r1 * (dxn - xn * jnp.mean(dxn * xn, axis=-1, keepdims=True))
        gx_ref[...] = dx2_ref[...] + dx1
        dsh_ref[...] += jnp.sum(dh, axis=0, keepdims=True)
        dsc_ref[...] += jnp.sum(dhx * g, axis=0, keepdims=True)
        dng_ref[...] += jnp.sum(dhx * one_sc, axis=0, keepdims=True)

    nat = pl.BlockSpec((TOK, D), lambda i: (i, 0))
    vec = pl.BlockSpec((1, D), lambda i: (0, 0))
    return pl.pallas_call(
        body, name="h_bwd", grid=(S // TOK,),
        in_specs=[pl.BlockSpec((R16, R16, D), lambda i: (0, i, 0)), nat, nat, vec, vec],
        out_specs=[nat, vec, vec, vec],
        out_shape=[_sds((S, D), F32), _sds((1, D), F32), _sds((1, D), F32), _sds((1, D), F32)],
        scratch_shapes=[pltpu.VMEM((R16, R16, D), F32)],
        compiler_params=_cp(("arbitrary",)),
    )(dh, dx2, x, norm_g, scale)


def _local_step(x, target, shift, scale, gate, norm_g, w_in_full, pool_w, pool_scale, wab, wpb, wout, rel_bias,
                final_g):
    st = _local_pre(x, shift, scale, norm_g, rel_bias)
    st = _local_grads(st, target, gate, w_in_full, pool_w, pool_scale, wab, wpb, wout, final_g)
    d_rel_bias = _local_bias_grads(st)
    gx, dmod, dng = _local_input_grads(st, w_in_full, norm_g, scale)
    return dict(loss=st["loss"], grad_x=gx, dw_in=st["dw_in"], dw_ab=st["dw_ab"], dw_pb=st["dw_pb"],
                dw_out=st["dw_out"], d_norm_g=dng, d_pool_w=st["dpw"], d_pool_scale=st["dps"],
                d_rel_bias=d_rel_bias, d_final_g=st["gfg"], dmod=dmod)


def _local_pre(x, shift, scale, norm_g, rel_bias, dep=None):
    S = x.shape[0]
    LS = S // R16
    geoms = [_Geom(gi, LS) for gi in range(3)]
    idx_np = [[g.bucket_index(sub) for sub in range(2)] for g in geoms]
    buckets = [[tuple(int(b) for b in np.unique(ix) if b >= 0) for ix in pair] for pair in idx_np]
    idxs = [[jnp.asarray(ix) for ix in pair] for pair in idx_np]
    biases = [[_bias_table(idxs[gi][sub], buckets[gi][sub], rel_bias, gi, sub, dep) for sub in range(2)]
              for gi in range(3)]
    h = _norm_mod(x, norm_g, shift, scale, dep).reshape(S, D)
    return dict(geoms=geoms, idxs=idxs, buckets=buckets, biases=biases, x=x, h=h, ht=_tcast(h, "h_transpose"))


def _local_grads(st, target, gate, w_in_full, pool_w, pool_scale, wab, wpb, wout, final_g):
    geoms, idxs, biases, x, h, ht = (st[k] for k in ("geoms", "idxs", "biases", "x", "h", "ht"))
    S = h.shape[0]
    LS = S // R16
    qkv, rest = _proj(h, w_in_full)

    os_, ls_ = [], []
    for gi in range(3):
        o, l = _attn_fwd(qkv, biases[gi], geoms[gi])
        os_.append(o)
        ls_.append(l)
    attn, Lb, ag, agt = _attn_combine(os_, ls_, rest)
    pt, mp, pg, pgt = _pool_fwd(rest, pool_w, pool_scale)
    merged, mt = _branch_merge(ag, pg, wab, wpb, rest)
    dx2, dmo3, loss, gfg, dgate = _out_loss(merged.reshape(R16, LS, D), wout, x, target, gate, final_g)
    dmo = dmo3.reshape(S, D)

    dya, dyp, drest, dattn, dlb, dmp, dps = _bwd_gates(dmo, wout, wab, wpb, rest, ag, pg, attn, mp, pool_scale)
    dw_out = _mm_tn(mt, dmo, n_total=D, out_blk=lambda j: j, name="dw_out")
    dw_ab = _mm_tn(agt, dya, n_total=D, out_blk=lambda j: j, name="dw_attn_br")
    dw_pb = _mm_tn(pgt, dyp, n_total=D, out_blk=lambda j: j, name="dw_pool_br")
    drest, dpw = _pool_bwd(dmp, pt, pool_w, drest)

    dqkv = None
    dsums = []
    for gi in range(3):
        dqkv, dsum = _attn_bwd(qkv, dattn, Lb, dlb, biases[gi], geoms[gi], dqkv)
        dsums.append(dsum)

    dw_in = _mm_tn(ht, dqkv, n_total=8192, out_blk=lambda j: j, name="dw_in_qkv", with_low=True)
    dw_in, dw_in_low = _mm_tn(ht, drest, n_total=8192, out_blk=_rest_blk, name="dw_in_rest", prev=dw_in, with_low=True)
    return dict(st, loss=loss[0, 0], dw_in=dw_in, dw_in_low=dw_in_low, dw_ab=dw_ab, dw_pb=dw_pb, dw_out=dw_out, dpw=dpw, dps=dps, gfg=gfg,
                dgate=dgate, dsums=dsums, dqkv=dqkv, drest=drest, dx2=dx2)


def _local_bias_grads(st, dep=None):
    drb = [_bias_grad(st["dsums"][gi][0], st["idxs"][gi][0], st["buckets"][gi][0], gi, 0, dep if gi == 0 else None) +
           _bias_grad(st["dsums"][gi][1], st["idxs"][gi][1], st["buckets"][gi][1], gi, 1) for gi in range(3)]
    return jnp.concatenate(drb, axis=1)


def _dh_h_bwd(dqkv, drest, w, dx2, x, norm_g, scale, dep=None):
    S = x.shape[0]
    dep_specs, dep_args = _after(dep)

    def body(a_ref, b_ref, w_ref, dx2_ref, x_ref, g_ref, sc_ref, *rest):
        gx_ref, dsh_ref, dsc_ref, dng_ref, nat_ref = rest[-5:]
        i = pl.program_id(0)

        @pl.when(i == 0)
        def _():
            dsh_ref[...] = jnp.zeros_like(dsh_ref)
            dsc_ref[...] = jnp.zeros_like(dsc_ref)
            dng_ref[...] = jnp.zeros_like(dng_ref)

        a, b = _ld(a_ref), _ld(b_ref)
        acc = _dot(a[:, 0:512], _w512(w_ref, 0), NT)
        for j in range(1, 9):
            acc = acc + _dot(a[:, 512 * j:512 * (j + 1)], _w512(w_ref, j), NT)
        off = 0
        for lo, hi in REST_COLS:
            for j in range(lo, hi):
                acc = acc + _dot(b[:, off:off + 512], _w512(w_ref, j), NT)
                off += 512
        for r in range(R16):
            nat_ref[:, r, :] = acc[R16 * r:R16 * (r + 1)]
        dh = jnp.concatenate([nat_ref[l] for l in range(R16)], axis=0)
        g = g_ref[...]
        one_sc = 1.0 + sc_ref[...]
        xv = x_ref[...]
        r1 = lax.rsqrt(jnp.mean(xv * xv, axis=-1, keepdims=True) + EPS)
        xn = xv * r1
        dhx = dh * xn
        dxn = dh * (g * one_sc)
        dx1 = r1 * (dxn - xn * jnp.mean(dxn * xn, axis=-1, keepdims=True))
        gx_ref[...] = dx2_ref[...] + dx1
        dsh_ref[...] += jnp.sum(dh, axis=0, keepdims=True)
        dsc_ref[...] += jnp.sum(dhx * g, axis=0, keepdims=True)
        dng_ref[...] += jnp.sum(dhx * one_sc, axis=0, keepdims=True)

    nat = pl.BlockSpec((TOK, D), lambda i: (i, 0))
    vec = pl.BlockSpec((1, D), lambda i: (0, 0))
    slab = lambda W: pl.BlockSpec((R16, R16, W), lambda i: (0, i, 0))
    return pl.pallas_call(
        body, name="dh_h_bwd", grid=(S // TOK,),
        in_specs=[slab(QKV_W), slab(REST_W), _resident((8, D, 1024)), nat, nat, vec, vec] + dep_specs,
        out_specs=[nat, vec, vec, vec],
        out_shape=[_sds((S, D), F32), _sds((1, D), F32), _sds((1, D), F32), _sds((1, D), F32)],
        scratch_shapes=[pltpu.VMEM((R16, R16, D), F32)],
        compiler_params=_cp(("arbitrary",), vmem=56),
    )(dqkv, drest, w, dx2, x, norm_g, scale, *dep_args)


def _local_input_grads(st, w_in_full, norm_g, scale, dep=None):
    S = st["h"].shape[0]
    LS = S // R16
    gx, dsh, dsc, dng = _dh_h_bwd(st["dqkv"].reshape(R16, LS, QKV_W), st["drest"].reshape(R16, LS, REST_W),
                                  w_in_full, st["dx2"], st["x"], norm_g, scale, dep)
    return gx, jnp.concatenate([dsh, dsc, st["dgate"]], axis=1), dng


def _my_place():
    return lax.axis_index("x"), lax.axis_index("y"), lax.axis_index("c")


def _block_of(ref, axis, idx, width):
    if axis is None:
        return ref.at[idx]
    start = pl.multiple_of(idx * width, width)
    if axis == 0:
        return ref.at[pl.ds(start, width)]
    return ref.at[:, pl.ds(start, width)]


_SEM = pl.BlockSpec(memory_space=pltpu.SEMAPHORE)
_EFFECT = pltpu.SideEffectType.DATAFLOW_SIDE_EFFECTING


def _split_start(name, srcs, lands, ncopy, plan, dep=None):
    arrs = list(srcs) + list(lands)
    na, ns = len(arrs), len(srcs)
    dep_specs, dep_args = _after(dep)

    def body(*refs):
        send_sems, recv_sems, token = refs[na + len(dep_args)], refs[na + len(dep_args) + 1], refs[-1]
        for k, (s, d, dev) in enumerate(plan(refs[:ns], refs[ns:na])):
            pltpu.make_async_remote_copy(src_ref=s, dst_ref=d, send_sem=send_sems.at[k], recv_sem=recv_sems.at[k],
                                         device_id=dev, device_id_type=MESH).start()
        token[...] = jnp.zeros_like(token)

    outs = pl.pallas_call(
        body, name=name,
        out_shape=(pltpu.SemaphoreType.DMA((ncopy,)), pltpu.SemaphoreType.DMA((ncopy,)),
                   *[pltpu.HBM(a.shape, a.dtype) for a in arrs], _sds((8, 128), F32)),
        in_specs=[_HBM] * na + dep_specs,
        out_specs=(_SEM, _SEM, *[_HBM] * na, pl.BlockSpec(memory_space=pltpu.VMEM)),
        input_output_aliases={i: 2 + i for i in range(na)},
        compiler_params=pltpu.CompilerParams(has_side_effects=_EFFECT),
    )(*[pltpu.with_memory_space_constraint(a, pltpu.HBM) for a in arrs], *dep_args)
    return (outs[0], outs[1]), list(outs[2:2 + ns]), list(outs[2 + ns:2 + na]), outs[-1]


def _split_wait(name, sems, srcs, lands, plan, after):
    arrs = list(srcs) + list(lands)
    na, ns = len(arrs), len(srcs)
    after = list(after) if isinstance(after, (list, tuple)) else [after]

    def body(*refs):
        send_sems, recv_sems = refs[na], refs[na + 1]
        for k, (s, d, dev) in enumerate(plan(refs[:ns], refs[ns:na])):
            cp = pltpu.make_async_remote_copy(src_ref=s, dst_ref=d, send_sem=send_sems.at[k], recv_sem=recv_sems.at[k],
                                              device_id=dev, device_id_type=MESH)
            cp.wait_send()
            cp.wait_recv()

    outs = pl.pallas_call(
        body, name=name, out_shape=[pltpu.HBM(a.shape, a.dtype) for a in arrs],
        in_specs=[_HBM] * na + [_SEM, _SEM] + [pl.BlockSpec(memory_space=pl.ANY)] * len(after),
        out_specs=[_HBM] * na,
        input_output_aliases={i: i for i in range(na)},
        compiler_params=pltpu.CompilerParams(has_side_effects=_EFFECT),
    )(*arrs, sems[0], sems[1], *after)
    return list(outs[:ns]), list(outs[ns:])


def _gather_chips_plan(layout):
    def plan(src, land):
        x, y, c = _my_place()
        me = 4 * x + 2 * y + c
        mine = [_block_of(land[p], layout[p][0], me, layout[p][1]) for p in range(len(land))]
        return [(mine[p], mine[p], (cx, cy, c)) for p in range(len(land)) for cx, cy in [(1 - x, y), (x, 1 - y)]]
    return plan


def _pair_plan(layout):
    def plan(src, land):
        x, y, c = _my_place()
        n = len(layout)
        return [(_block_of(src[p], layout[p][0], 2 * q + (1 - c), layout[p][1]), land[p].at[q], (x, y, 1 - c))
                for p in range(n) for q in range(4)] + [(src[p], land[p], (x, y, 1 - c)) for p in range(n, len(src))]
    return plan


def _chips_plan(n):
    def plan(src, land):
        x, y, c = _my_place()
        chips = [(1 - x, y), (x, 1 - y), (1 - x, 1 - y)]
        return [(src[p].at[2 * cx + cy], land[p].at[2 * x + y], (cx, cy, c)) for p in range(n) for cx, cy in chips] + \
               [(src[p], land[p].at[2 * x + y], (cx, cy, c)) for p in range(n, len(src)) for cx, cy in chips]
    return plan


def _pair_share(gathered):
    n = len(gathered)

    def body(*refs):
        land = refs[n:2 * n]
        send_sems, recv_sems, fwd_send, fwd_recv = refs[2 * n:]
        x, y, c = _my_place()
        k_dg = 3 - (2 * x + y)
        south = c == 0
        pass_on = 4 * jnp.where(south, 1 - x, x) + 2 * jnp.where(south, y, 1 - y) + c
        to = (jnp.where(south, x, 1 - x), jnp.where(south, 1 - y, y), c)

        def forward(p):
            return pltpu.make_async_remote_copy(
                src_ref=land[p].at[pass_on], dst_ref=land[p].at[pass_on], send_sem=fwd_send.at[p],
                recv_sem=fwd_recv.at[p], device_id=to, device_id_type=MESH)

        def to_sibling(p, k):
            blk = land[p].at[2 * k + c]
            return pltpu.make_async_remote_copy(
                src_ref=blk, dst_ref=blk, send_sem=send_sems.at[4 * p + k], recv_sem=recv_sems.at[4 * p + k],
                device_id=(x, y, 1 - c), device_id_type=MESH)

        for p in range(n):
            forward(p).start()
        for p in range(n):
            for k in range(4):
                @pl.when(k != k_dg)
                def _():
                    to_sibling(p, k).start()
        for p in range(n):
            dg = land[p].at[2 * k_dg + c]
            pltpu.make_async_remote_copy(src_ref=dg, dst_ref=dg, send_sem=fwd_send.at[p], recv_sem=fwd_recv.at[p],
                                         device_id=to, device_id_type=MESH).wait_recv()
            for k in range(4):
                @pl.when(k == k_dg)
                def _():
                    to_sibling(p, k).start()
        for p in range(n):
            forward(p).wait_send()
            for k in range(4):
                to_sibling(p, k).wait_send()
                blk = land[p].at[2 * k + 1 - c]
                pltpu.make_async_remote_copy(
                    src_ref=blk, dst_ref=blk, send_sem=send_sems.at[4 * p + k], recv_sem=recv_sems.at[4 * p + k],
                    device_id=(x, y, 1 - c), device_id_type=MESH).wait_recv()

    return pl.pallas_call(
        body, name="gather_pair_share", in_specs=[_HBM] * n, out_specs=[_HBM] * n,
        out_shape=[_sds(g.shape, g.dtype) for g in gathered],
        input_output_aliases={p: p for p in range(n)},
        scratch_shapes=[pltpu.SemaphoreType.DMA((4 * n,)), pltpu.SemaphoreType.DMA((4 * n,)),
                        pltpu.SemaphoreType.DMA((n,)), pltpu.SemaphoreType.DMA((n,))],
    )(*gathered)


def _place_slot(arrs, slot, n_slots, name, out_dtype=None):
    n = len(arrs)

    def body(s_ref, *refs):
        for p in range(n):
            refs[n + p][...] = refs[p][...].astype(refs[n + p].dtype)

    def spec(a):
        tail = a.shape[-2:]
        return pl.BlockSpec((None,) + tail, lambda i, s_ref: (s_ref[0], 0, 0))

    in_specs = [spec(a) if a.ndim == 3 else pl.BlockSpec(a.shape, lambda i, s_ref: (0, 0)) for a in arrs]
    return pl.pallas_call(
        body, name=name,
        grid_spec=pltpu.PrefetchScalarGridSpec(num_scalar_prefetch=1, grid=(1,), in_specs=in_specs,
                                               out_specs=[spec(a) for a in arrs]),
        out_shape=[_sds((n_slots,) + a.shape[-2:], out_dtype or a.dtype) for a in arrs],
        compiler_params=_cp(("arbitrary",)),
    )(slot, *arrs)


def _cols_from_blocks(w8, name):
    _, R, C = w8.shape

    def body(w_ref, o_ref):
        for e in range(8):
            o_ref[:, C * e:C * (e + 1)] = w_ref[e]

    return pl.pallas_call(body, name=name, out_shape=_sds((R, 8 * C), w8.dtype), compiler_params=_cp())(w8)


def _all_gather_small(srcs, name):
    n = len(srcs)
    vmem = pl.BlockSpec(memory_space=pltpu.VMEM)

    def body(*refs):
        src, out = refs[:n], refs[n:2 * n]
        send_sems, recv_sems = refs[2 * n:]
        x, y, c = _my_place()
        me = 4 * x + 2 * y + c
        flip = lambda v, f: 1 - v if f else v
        peers = [(flip(x, fx), flip(y, fy), flip(c, fc)) for fx in (0, 1) for fy in (0, 1) for fc in (0, 1)][1:]
        cps = []
        for p in range(n):
            out[p][me] = src[p][...]
            for k, peer in enumerate(peers):
                cps.append(pltpu.make_async_remote_copy(
                    src_ref=src[p], dst_ref=out[p].at[me], send_sem=send_sems.at[7 * p + k],
                    recv_sem=recv_sems.at[7 * p + k], device_id=peer, device_id_type=MESH))
        for cp in cps:
            cp.start()
        for cp in cps:
            cp.wait()

    return pl.pallas_call(
        body, name=name, in_specs=[vmem] * n, out_specs=[vmem] * n,
        out_shape=[_sds((8,) + a.shape, a.dtype) for a in srcs],
        scratch_shapes=[pltpu.SemaphoreType.DMA((7 * n,)), pltpu.SemaphoreType.DMA((7 * n,))],
    )(*srcs)


def _adaln_exchange(c, w_ada, b_part):
    vmem = pl.BlockSpec(memory_space=pltpu.VMEM)

    def body(c_ref, w_ref, b_ref, call_ref, mod_ref, part_ref, send_sems, recv_sems):
        x, y, c = _my_place()
        me = 4 * x + 2 * y + c
        flip = lambda v, f: 1 - v if f else v
        peers = [(flip(x, fx), flip(y, fy), flip(c, fc)) for fx in (0, 1) for fy in (0, 1) for fc in (0, 1)][1:]
        call_ref[me] = c_ref[...]
        first = [pltpu.make_async_remote_copy(src_ref=c_ref, dst_ref=call_ref.at[me], send_sem=send_sems.at[k],
                                              recv_sem=recv_sems.at[k], device_id=peer, device_id_type=MESH)
                 for k, peer in enumerate(peers)]
        for cp in first:
            cp.start()
        for cp in first:
            cp.wait()
        c_all = jnp.concatenate([call_ref[j] for j in range(8)], axis=0)
        part_ref[...] = _dot(c_all.astype(CDT), w_ref[...].astype(CDT), NN) + b_ref[...]
        mod_ref[me] = part_ref[pl.ds(me, 1), :]
        second = [pltpu.make_async_remote_copy(
            src_ref=part_ref.at[pl.ds(4 * px + 2 * py + pc, 1)], dst_ref=mod_ref.at[me], send_sem=send_sems.at[7 + k],
            recv_sem=recv_sems.at[7 + k], device_id=(px, py, pc), device_id_type=MESH)
            for k, (px, py, pc) in enumerate(peers)]
        for cp in second:
            cp.start()
        for cp in second:
            cp.wait()

    return pl.pallas_call(
        body, name="adaln_exchange", in_specs=[vmem] * 3, out_specs=[vmem] * 2,
        out_shape=[_sds((8, 1, D), F32), _sds((8, 1, w_ada.shape[1]), F32)],
        scratch_shapes=[pltpu.VMEM((8, w_ada.shape[1]), F32), pltpu.SemaphoreType.DMA((14,)),
                        pltpu.SemaphoreType.DMA((14,))],
        compiler_params=_cp(),
    )(c, w_ada, b_part)


def _pair_add(dw, recv, axis, width, cidx, name):
    if axis == 0:
        tr = min(width, 256)
        C = dw.shape[1]
        grid = (4, width // tr)
        nb = width // tr
        own = pl.BlockSpec((tr, C), lambda q, i, c_ref: ((2 * q + c_ref[0]) * nb + i, 0))
        oth = pl.BlockSpec((None, tr, C), lambda q, i, c_ref: (q, i, 0))
    else:
        R = dw.shape[0]
        tr = min(R, 256)
        grid = (4, R // tr)
        own = pl.BlockSpec((tr, width), lambda q, i, c_ref: (i, 2 * q + c_ref[0]))
        oth = pl.BlockSpec((None, tr, width), lambda q, i, c_ref: (q, i, 0))

    def body(c_ref, a_ref, b_ref, o_ref):
        o_ref[...] = (a_ref[...] + b_ref[...].astype(F32)).astype(o_ref.dtype)

    return pl.pallas_call(
        body, name=name,
        grid_spec=pltpu.PrefetchScalarGridSpec(num_scalar_prefetch=1, grid=grid, in_specs=[own, oth], out_specs=oth),
        out_shape=_sds(recv.shape, CDT), compiler_params=_cp(("parallel", "parallel")),
    )(cidx, dw, recv)


def _add_pairs(xs, ys, name):
    n = len(xs)

    def body(*refs):
        for p in range(n):
            refs[2 * n + p][...] = refs[p][...] + refs[n + p][...]

    return pl.pallas_call(body, name=name, out_shape=[_sds(a.shape, F32) for a in xs])(*xs, *ys)


def _adam_math(w, g, m, v):
    m = ADAM_B1 * m + (1.0 - ADAM_B1) * g
    v = ADAM_B2 * v + (1.0 - ADAM_B2) * (g * g)
    m_hat = m / (1.0 - ADAM_B1 ** ADAM_STEP)
    v_hat = v / (1.0 - ADAM_B2 ** ADAM_STEP)
    delta = -ADAM_LR * (m_hat / (jnp.sqrt(v_hat) + ADAM_EPS) + ADAM_WD * w)
    return delta, m, v


def _adam(gparts, w, m, v, name):
    P, R, C = gparts.shape
    tr = R if R <= 256 else 256

    def body(g_ref, w_ref, m_ref, v_ref, go_ref, d_ref, mo_ref, vo_ref):
        g = g_ref[0].astype(F32)
        for i in range(1, P):
            g = g + g_ref[i].astype(F32)
        d, mn, vn = _adam_math(w_ref[...], g, m_ref[...], v_ref[...])
        go_ref[...] = g
        d_ref[...] = d
        mo_ref[...] = mn
        vo_ref[...] = vn

    t = pl.BlockSpec((tr, C), lambda i: (i, 0))
    return pl.pallas_call(
        body, name=name, grid=(R // tr,),
        in_specs=[pl.BlockSpec((P, tr, C), lambda i: (0, i, 0)), t, t, t], out_specs=[t, t, t, t],
        out_shape=[_sds((R, C), F32)] * 4, compiler_params=_cp(("parallel",)),
    )(gparts, w, m, v)


def _sum_parts(gparts, name):
    P, R, C = gparts.shape

    def body(g_ref, o_ref):
        g = g_ref[0]
        for i in range(1, P):
            g = g + g_ref[i]
        o_ref[...] = g

    return pl.pallas_call(body, name=name, out_shape=_sds((R, C), F32))(gparts)


def _adam_w_ada(c_t, dm, w, m, v):
    R, C = w.shape

    def body(c_ref, dm_ref, w_ref, m_ref, v_ref, go_ref, d_ref, mo_ref, vo_ref):
        g = c_ref[:, 0:1] * dm_ref[0:1, :]
        for b in range(1, 8):
            g = g + c_ref[:, b:b + 1] * dm_ref[b:b + 1, :]
        d, mn, vn = _adam_math(w_ref[...], g, m_ref[...], v_ref[...])
        go_ref[...] = g
        d_ref[...] = d
        mo_ref[...] = mn
        vo_ref[...] = vn

    return pl.pallas_call(body, name="adam_w_ada", out_shape=[_sds((R, C), F32)] * 4,
                          compiler_params=_cp())(c_t, dm, w, m, v)


def kernel(x, c, norm_g, w_ada, b_ada, w_in, pool_w, pool_scale, w_attn_br, w_pool_br, w_out, rel_bias, final_g, loss_target, m_norm_g, m_w_ada, m_b_ada, m_w_in, m_pool_w, m_pool_scale, m_w_attn_br, m_w_pool_br, m_w_out, m_rel_bias, m_final_g, v_norm_g, v_w_ada, v_b_ada, v_w_in, v_pool_w, v_pool_scale, v_w_attn_br, v_w_pool_br, v_w_out, v_rel_bias, v_final_g):
    S = x.shape[1]
    px, py, pc = _my_place()
    me = 4 * px + 2 * py + pc
    cidx = jnp.reshape(pc, (1,)).astype(jnp.int32)

    layout = [(1, 1024), (1, 128), (1, 128), (0, 128)]
    final_g2 = final_g.reshape(1, D)
    me1 = jnp.reshape(me, (1,)).astype(jnp.int32)

    b_part = lax.dynamic_slice(b_ada, (0, me * 384), (1, 384))
    c_all, mod_all = _adaln_exchange(c, w_ada[0], b_part)
    c_all = c_all.reshape(8, D)
    mod = mod_all.reshape(1, 3 * D)
    shift, scale, gate = mod[:, :D], mod[:, D:2 * D], mod[:, 2 * D:]

    gplan = _gather_chips_plan([(None, 1)] * 4)
    lands = _place_slot([w_in[0], w_attn_br[0], w_pool_br[0], w_out[0]], me1, 8, "place_own_shards", CDT)
    g_sems, _, lands, tok = _split_start("gather_chips_start", [], lands, 8, gplan, dep=mod_all)
    st = _local_pre(x[0], shift, scale, norm_g, rel_bias, dep=tok)
    _, lands = _split_wait("gather_chips_wait", g_sems, [], lands, gplan,
                           [st["ht"]] + [b for pair in st["biases"] for b in pair])
    w_in_full, wab8, wpb8, wout8 = _pair_share(lands)
    wab, wpb = _cols_from_blocks(wab8, "w_attn_br_cols"), _cols_from_blocks(wpb8, "w_pool_br_cols")
    wout = wout8.reshape(D, D)

    st = _local_grads(st, loss_target[0], gate, w_in_full, pool_w[0], pool_scale, wab, wpb, wout, final_g2)

    early = [st["dpw"].reshape(AW, 128), st["dps"], st["gfg"], jnp.broadcast_to(st["loss"], (1, 128))]
    dws = [st["dw_in"], st["dw_ab"], st["dw_pb"], st["dw_out"]]
    sent = [st["dw_in_low"]] + dws[1:] + early
    pplan = _pair_plan(layout)
    pair_shapes = [(4, w, a.shape[1]) if ax == 0 else (4, a.shape[0], w) for a, (ax, w) in zip(dws, layout)]
    p_sems, sent, recv, tok = _split_start(
        "reduce_pair_start", sent,
        [lax.empty(s, a.dtype) for s, a in zip(pair_shapes + [a.shape for a in early], sent)], 20, pplan)
    d_rel_bias = _local_bias_grads(st, dep=tok)
    sent, recv = _split_wait("reduce_pair_wait", p_sems, sent, recv, pplan, d_rel_bias)
    parts = [_pair_add(dw, r, ax, w, cidx, f"pair_add{i}")
             for i, (dw, r, (ax, w)) in enumerate(zip([dws[0]] + sent[1:4], recv[:4], layout))]
    parts += _add_pairs(sent[4:], recv[4:], "pair_add_smalls")

    slot1 = jnp.reshape(2 * px + py, (1,)).astype(jnp.int32)
    cplan = _chips_plan(4)
    c_sems, parts, lands, tok = _split_start("reduce_chips_start", parts,
                                             _place_slot(parts, slot1, 4, "place_own_partials"), 24, cplan)
    grad_x, dmod, d_norm_g = _local_input_grads(st, w_in_full, norm_g, scale, dep=tok)

    dmod_all, ng_all, rb_all = _all_gather_small([dmod, d_norm_g, d_rel_bias], "gather_late")
    o_norm_g = _adam(ng_all, norm_g, m_norm_g, v_norm_g, "adam_norm_g")
    o_rel_bias = _adam(rb_all, rel_bias, m_rel_bias, v_rel_bias, "adam_rel_bias")
    o_b_ada = _adam(dmod_all, b_ada, m_b_ada, v_b_ada, "adam_b_ada")
    dm_mine = lax.dynamic_slice(dmod_all.reshape(8, 3 * D), (0, me * 384), (8, 384))
    o_w_ada = [o[None] for o in _adam_w_ada(c_all.T, dm_mine, w_ada[0], m_w_ada[0], v_w_ada[0])]

    _, (g_in, g_ab, g_pb, g_out, g_pw, g_ps, g_fg, g_loss) = _split_wait(
        "reduce_chips_wait", c_sems, parts, lands, cplan, [o_w_ada[0], o_b_ada[0], o_norm_g[0], o_rel_bias[0]])
    o_w_in = [o[None] for o in _adam(g_in, w_in[0], m_w_in[0], v_w_in[0], "adam_w_in")]
    o_w_ab = [o[None] for o in _adam(g_ab, w_attn_br[0], m_w_attn_br[0], v_w_attn_br[0], "adam_w_attn_br")]
    o_w_pb = [o[None] for o in _adam(g_pb, w_pool_br[0], m_w_pool_br[0], v_w_pool_br[0], "adam_w_pool_br")]
    o_w_out = [o[None] for o in _adam(g_out, w_out[0], m_w_out[0], v_w_out[0], "adam_w_out")]
    o_pool_w = [o.reshape(pool_w.shape) for o in _adam(g_pw, pool_w.reshape(AW, 128), m_pool_w.reshape(AW, 128),
                                                       v_pool_w.reshape(AW, 128), "adam_pool_w")]
    o_pool_scale = _adam(g_ps, pool_scale, m_pool_scale, v_pool_scale, "adam_pool_scale")
    o_final_g = [o.reshape(D) for o in _adam(g_fg, final_g2, m_final_g.reshape(1, D), v_final_g.reshape(1, D),
                                             "adam_final_g")]
    loss = _sum_parts(g_loss, "sum_loss")[0, 0]

    per_w = [o_norm_g, o_w_ada, o_b_ada, o_w_in, o_pool_w, o_pool_scale, o_w_ab, o_w_pb, o_w_out, o_rel_bias, o_final_g]
    outs = [loss, grad_x[None]]
    for k in range(4):
        outs += [o[k] for o in per_w]
    return tuple(outs)
```

```python
import functools
import math

import numpy as np
import jax
import jax.numpy as jnp
from jax import lax
from jax.experimental import pallas as pl
from jax.experimental.pallas import tpu as pltpu

F32 = jnp.float32
CDT = jnp.bfloat16
D = 1024
HD = 64
NH = 8
AW = 512
QKV_W = 4608
REST_W = 3584
R16 = 16
EPS = 1e-6
NEG = -1e30
N_BACK = 128
ADAM_LR, ADAM_B1, ADAM_B2, ADAM_EPS, ADAM_WD, ADAM_STEP = 0.001, 0.9, 0.999, 1e-08, 0.01, 10
MESH = pl.DeviceIdType.MESH
VMEM_MB = 1024 * 1024


def _cp(sem=None, vmem=48, **kw):
    if sem is not None:
        kw["dimension_semantics"] = sem
    return pltpu.CompilerParams(vmem_limit_bytes=vmem * VMEM_MB, **kw)


def _sds(shape, dtype):
    return jax.ShapeDtypeStruct(tuple(shape), dtype)


def _rest_blk(jj):
    return jnp.where(jj == 0, 9, jnp.where(jj == 6, 10, jj + 10))


def _sigmoid(z):
    return 0.5 * jnp.tanh(0.5 * z) + 0.5


def _dot(a, b, dims):
    return lax.dot_general(a, b, (dims, ((), ())), preferred_element_type=F32)


NN = ((1,), (0,))
NT = ((1,), (1,))
TN = ((0,), (0,))


def _ld(ref):
    if len(ref.shape) == 2:
        return ref[...]
    return jnp.concatenate([ref[a] for a in range(ref.shape[0])], axis=0)


def _st(ref, val, cols=None):
    val = val.astype(ref.dtype)
    if len(ref.shape) == 2:
        if cols is None:
            ref[...] = val
        else:
            ref[:, cols] = val
        return
    rows = ref.shape[1]
    for a in range(ref.shape[0]):
        if cols is None:
            ref[a] = val[a * rows:(a + 1) * rows]
        else:
            ref[a, :, cols] = val[a * rows:(a + 1) * rows]


REST_COLS = ((9, 10), (11, 16), (10, 11))
_HBM = pl.BlockSpec(memory_space=pltpu.HBM)


def _resident(shape):
    return pl.BlockSpec(shape, lambda *ids: (0,) * len(shape), pipeline_mode=pl.Buffered(1))


def _w512(w_ref, j):
    return w_ref[j // 2, :, 512 * (j % 2):512 * (j % 2 + 1)]


def _proj(h, w):
    S = h.shape[0]
    tm = 512

    def body(a_ref, w_ref, q_ref, r_ref):
        a = a_ref[...]
        for j in range(9):
            q_ref[:, 512 * j:512 * (j + 1)] = _dot(a, _w512(w_ref, j), NN).astype(q_ref.dtype)
        off = 0
        for lo, hi in REST_COLS:
            for j in range(lo, hi):
                r_ref[:, off:off + 512] = _dot(a, _w512(w_ref, j), NN)
                off += 512

    return pl.pallas_call(
        body, name="proj", grid=(S // tm,),
        in_specs=[pl.BlockSpec((tm, D), lambda i: (i, 0)), _resident((8, D, 1024))],
        out_specs=[pl.BlockSpec((tm, QKV_W), lambda i: (i, 0)), pl.BlockSpec((tm, REST_W), lambda i: (i, 0))],
        out_shape=[_sds((S, QKV_W), CDT), _sds((S, REST_W), F32)],
        compiler_params=_cp(("parallel",), vmem=56),
    )(h, w)


def _after(dep):
    return ([], []) if dep is None else ([pl.BlockSpec(memory_space=pl.ANY)], [dep])


def _dh(dqkv, drest, w, dep=None):
    S = dqkv.shape[0]
    tm = 512
    dep_specs, dep_args = _after(dep)

    def body(a_ref, b_ref, w_ref, *rest):
        o_ref = rest[-1]
        acc = _dot(a_ref[:, 0:512], _w512(w_ref, 0), NT)
        for j in range(1, 9):
            acc = acc + _dot(a_ref[:, 512 * j:512 * (j + 1)], _w512(w_ref, j), NT)
        off = 0
        for lo, hi in REST_COLS:
            for j in range(lo, hi):
                acc = acc + _dot(b_ref[:, off:off + 512], _w512(w_ref, j), NT)
                off += 512
        o_ref[...] = acc

    return pl.pallas_call(
        body, name="dh", grid=(S // tm,),
        in_specs=[pl.BlockSpec((tm, QKV_W), lambda i: (i, 0)), pl.BlockSpec((tm, REST_W), lambda i: (i, 0)),
                  _resident((8, D, 1024))] + dep_specs,
        out_specs=pl.BlockSpec((tm, D), lambda i: (i, 0)),
        out_shape=_sds((S, D), F32),
        compiler_params=_cp(("parallel",), vmem=56),
    )(dqkv, drest, w, *dep_args)


def _mm_tn(at, b, *, n_total, out_blk, name, prev=None, tn=512, with_low=False):
    R, S = at.shape
    nb = b.shape[1] // tn
    n_out = 2 if with_low else 1

    def body(*refs):
        res = _dot(refs[0][...], refs[1][...], NN)
        refs[-n_out][...] = res
        if with_low:
            refs[-1][...] = res.astype(CDT)

    in_specs = [_resident((R, S)), pl.BlockSpec((S, tn), lambda j: (0, j))]
    args = [at, b]
    aliases = {}
    if prev is not None:
        prev = list(prev) if with_low else [prev]
        in_specs += [pl.BlockSpec(memory_space=pl.ANY)] * n_out
        args += prev
        aliases = {2 + k: k for k in range(n_out)}
    out_spec = pl.BlockSpec((R, tn), lambda j: (0, out_blk(j)))
    outs = pl.pallas_call(
        body, name=name, grid=(nb,), in_specs=in_specs,
        out_specs=[out_spec] * n_out,
        out_shape=[_sds((R, n_total), F32), _sds((R, n_total), CDT)][:n_out],
        input_output_aliases=aliases,
        compiler_params=_cp(("parallel",)),
    )(*args)
    return tuple(outs) if with_low else outs[0]


def _tcast(a, name):
    M, C = a.shape
    tm = 512

    def body(a_ref, o_ref):
        o_ref[...] = a_ref[...].astype(F32).T.astype(o_ref.dtype)

    return pl.pallas_call(
        body, name=name, grid=(M // tm,),
        in_specs=[pl.BlockSpec((tm, C), lambda i: (i, 0))],
        out_specs=pl.BlockSpec((C, tm), lambda i: (0, i)),
        out_shape=_sds((C, M), CDT), compiler_params=_cp(("parallel",)),
    )(a)


TOK = R16 * R16


def _swap16():
    i = lax.broadcasted_iota(jnp.int32, (TOK, TOK), 0)
    j = lax.broadcasted_iota(jnp.int32, (TOK, TOK), 1)
    return jnp.where(j == jnp.bitwise_and(i, 15) * 16 + jnp.right_shift(i, 4), 1.0, 0.0).astype(CDT)


def _norm_mod(x, norm_g, shift, scale, dep=None):
    S = x.shape[0]
    dep_specs, dep_args = _after(dep)

    def body(x_ref, g_ref, sh_ref, sc_ref, *rest):
        xv = x_ref[...]
        xn = xv * lax.rsqrt(jnp.mean(xv * xv, axis=-1, keepdims=True) + EPS)
        h = (xn * (g_ref[...] * (1.0 + sc_ref[...])) + sh_ref[...]).astype(CDT)
        _st(rest[-1], _dot(_swap16(), h, NN))

    vec = pl.BlockSpec((1, D), lambda i: (0, 0))
    return pl.pallas_call(
        body, name="norm_mod", grid=(S // TOK,),
        in_specs=[pl.BlockSpec((TOK, D), lambda i: (i, 0)), vec, vec, vec] + dep_specs,
        out_specs=pl.BlockSpec((R16, R16, D), lambda i: (0, i, 0)),
        out_shape=_sds((R16, S // R16, D), CDT), compiler_params=_cp(("parallel",)),
    )(x, norm_g, shift, scale, *dep_args)


def _t5_bucket_np(n):
    nf = np.maximum(n, 1).astype(np.float32)
    large = 16 + (np.log(nf / np.float32(16)) / np.float32(math.log(2048 / 16)) * np.float32(16)).astype(np.int32)
    large = np.minimum(large, 31)
    return np.where(n < 16, n, large).astype(np.int32)


class _Geom:
    def __init__(self, gi, LS):
        self.gi, self.LS = gi, LS
        self.dil = (1, 4, 16)[gi]
        if gi == 0:
            self.lead, self.rows, self.n_tiles, self.has_prev = (R16,), 16, LS // 16, True
            a = np.arange(R16)[:, None]
            ll = np.arange(16)[None, :]
            self.pos = (16 * ll + a).reshape(-1)
        elif gi == 1:
            self.lead, self.rows, self.n_tiles, self.has_prev = (4,), 64, LS // 64, True
            a = np.arange(4)[:, None]
            ll = np.arange(64)[None, :]
            self.pos = (4 * ll + a).reshape(-1)
        else:
            self.lead, self.rows, self.n_tiles, self.has_prev = (), LS, 1, False
            self.pos = np.arange(LS)
        self.T = self.pos.shape[0]
        self.Tk = 2 * self.T if self.has_prev else self.T
        self.outer = {0: (), 1: (4,), 2: (R16,)}[gi]

    def view(self, arr):
        S, C = arr.shape
        if self.gi == 1:
            return arr.reshape(4, 4, self.LS, C)
        return arr.reshape(R16, self.LS, C)

    def view_shape(self, C):
        return (4, 4, self.LS, C) if self.gi == 1 else (R16, self.LS, C)

    def unview(self, arr):
        return arr.reshape(R16 * self.LS, arr.shape[-1])

    def spec(self, W, cblk, tile):
        if self.gi == 0:
            return pl.BlockSpec((R16, 16, W), lambda n: (0, tile(n), cblk))
        if self.gi == 1:
            return pl.BlockSpec((4, None, 64, W), lambda r, n: (0, r, tile(n), cblk))
        return pl.BlockSpec((None, self.LS, W), lambda r: (r, 0, cblk))

    def halves(self):
        R = self.rows
        idx = np.arange(self.T).reshape(-1, R)
        return idx[:, :R // 2].reshape(-1), idx[:, R // 2:].reshape(-1)

    def bucket_index(self, sub):
        lo, hi = self.halves()
        pos = self.pos
        if sub == 0:
            pq = pos[lo]
            pk = np.concatenate([pos[hi] - self.T, pos[lo]]) if self.has_prev else np.concatenate([pos[lo], pos[hi]])
        else:
            pq = pos[hi]
            pk = np.concatenate([pos[lo], pos[hi]])
        dist = pq[:, None] - pk[None, :]
        ok = (dist >= 0) & (dist <= N_BACK)
        bucket = _t5_bucket_np(np.clip(dist, 0, N_BACK) * self.dil)
        return np.where(ok, bucket, -1).astype(np.int32)


def _bias_table(idx, buckets, rel_bias, gi, sub, dep=None):
    T, Tk = idx.shape
    tr = 32
    dep_specs, dep_args = _after(dep)

    def body(idx_ref, rb_ref, *rest):
        h = pl.program_id(0)
        ix = idx_ref[...]
        acc = jnp.full(ix.shape, NEG, F32)
        for b in buckets:
            acc = jnp.where(ix == b, rb_ref[b, gi * NH + h], acc)
        rest[-1][...] = acc

    return pl.pallas_call(
        body, name=f"bias_table{gi}{'ab'[sub]}", grid=(NH, T // tr),
        in_specs=[pl.BlockSpec((tr, Tk), lambda h, i: (i, 0)), pl.BlockSpec(memory_space=pltpu.SMEM)] + dep_specs,
        out_specs=pl.BlockSpec((None, tr, Tk), lambda h, i: (h, i, 0)),
        out_shape=_sds((NH, T, Tk), F32), compiler_params=_cp(("parallel", "parallel")),
    )(idx, rel_bias, *dep_args)


def _bias_grad(dsum, idx, buckets, gi, sub, dep=None):
    _, T, Tk = dsum.shape
    dep_specs, dep_args = _after(dep)

    def body(ds_ref, idx_ref, *rest):
        o_ref, r_ref = rest[-2], rest[-1]
        r_ref[...] = jnp.zeros_like(r_ref)
        ix, ds = idx_ref[...], ds_ref[...]
        for b in buckets:
            r_ref[b:b + 1, :] = jnp.sum(jnp.where(ix == b, ds, 0.0), axis=0, keepdims=True)
        o_ref[...] = jnp.broadcast_to(jnp.sum(r_ref[...], axis=1, keepdims=True), (32, 128))

    out = pl.pallas_call(
        body, name=f"bias_grad{gi}{'ab'[sub]}", grid=(NH,),
        in_specs=[pl.BlockSpec((None, T, Tk), lambda h: (h, 0, 0)), pl.BlockSpec((T, Tk), lambda h: (0, 0))] + dep_specs,
        out_specs=pl.BlockSpec((None, 32, 128), lambda h: (h, 0, 0)),
        out_shape=_sds((NH, 32, 128), F32), scratch_shapes=[pltpu.VMEM((32, Tk), F32)],
        compiler_params=_cp(("parallel",)),
    )(dsum, idx, *dep_args)
    return out[:, :, 0].T


def _prev_mask(n, T, Tk):
    col = lax.broadcasted_iota(jnp.int32, (1, Tk), 1)
    return jnp.where((col < T) & (n == 0), NEG, 0.0).astype(F32)


def _split_rows(v, R):
    A, half = v.shape[0] // R, R // 2
    via_f32 = v.dtype != F32 and half % 16 != 0
    w = v.astype(F32) if via_f32 else v
    lo = jnp.concatenate([w[a * R:a * R + half] for a in range(A)], axis=0)
    hi = jnp.concatenate([w[a * R + half:(a + 1) * R] for a in range(A)], axis=0)
    return (lo.astype(v.dtype), hi.astype(v.dtype)) if via_f32 else (lo, hi)


def _merge_rows(lo, hi, R):
    half = R // 2
    A = lo.shape[0] // half
    return jnp.concatenate([x[a * half:(a + 1) * half] for a in range(A) for x in (lo, hi)], axis=0)


def _attn_fwd(qkv, biases, geom):
    S = qkv.shape[0]
    gi, T, R, has_prev = geom.gi, geom.T, geom.rows, geom.has_prev
    H2 = T // 2
    qv = geom.view(qkv)
    cq, ck, cv = 3 * gi, 3 * gi + 1, 3 * gi + 2

    def body(*refs):
        if has_prev:
            q_ref, kp_ref, kc_ref, vp_ref, vc_ref, ba_ref, bb_ref, o_ref, l_ref = refs
        else:
            q_ref, kc_ref, vc_ref, ba_ref, bb_ref, o_ref, l_ref = refs
        q_lo, q_hi = _split_rows(_ld(q_ref), R)
        kc_lo, kc_hi = _split_rows(_ld(kc_ref), R)
        vc_lo, vc_hi = _split_rows(_ld(vc_ref), R)
        k_b = jnp.concatenate([kc_lo, kc_hi], axis=0)
        v_b = jnp.concatenate([vc_lo, vc_hi], axis=0)
        if has_prev:
            n = pl.program_id(len(geom.outer))
            k_a = jnp.concatenate([_split_rows(_ld(kp_ref), R)[1], kc_lo], axis=0)
            v_a = jnp.concatenate([_split_rows(_ld(vp_ref), R)[1], vc_lo], axis=0)
            pm = _prev_mask(n, H2, T)
        else:
            k_a, v_a, pm = k_b, v_b, None
        low = lax.broadcasted_iota(jnp.int32, (1, 2 * HD), 1) < HD
        for j in range(NH // 2):
            cols = slice(2 * HD * j, 2 * HD * (j + 1))
            o_half, l_half = [], []
            for q, k, v, b_ref, mask in ((q_lo, k_a, v_a, ba_ref, pm), (q_hi, k_b, v_b, bb_ref, None)):
                q2 = q[:, cols].astype(F32) * 0.125
                k2, v2 = k[:, cols], v[:, cols]
                qs = jnp.concatenate([jnp.where(low, q2, 0.0), jnp.where(low, 0.0, q2)], axis=0).astype(q.dtype)
                s = _dot(qs, k2, NT) + b_ref[j]
                if mask is not None:
                    s = s + mask
                m = jnp.max(s, axis=1, keepdims=True)
                p = jnp.exp(s - m)
                l = jnp.sum(p, axis=1, keepdims=True)
                pv = _dot(p.astype(v.dtype), v2, NN) / l
                lse = m + jnp.log(l)
                o_half.append(jnp.where(low, pv[:H2], pv[H2:]))
                l_half.append(jnp.where(low, lse[:H2], lse[H2:]))
            _st(o_ref, _merge_rows(o_half[0], o_half[1], R), cols)
            _st(l_ref, _merge_rows(l_half[0], l_half[1], R), cols)

    cur = lambda n: n
    prev = lambda n: jnp.maximum(n - 1, 0)
    if has_prev:
        in_specs = [geom.spec(AW, cq, cur), geom.spec(AW, ck, prev), geom.spec(AW, ck, cur),
                    geom.spec(AW, cv, prev), geom.spec(AW, cv, cur)]
        args = [qv] * 5
    else:
        in_specs = [geom.spec(AW, cq, cur), geom.spec(AW, ck, cur), geom.spec(AW, cv, cur)]
        args = [qv] * 3
    nd = len(geom.outer) + (1 if has_prev else 0)
    biases = [b.reshape(NH // 2, 2 * H2, b.shape[2]) for b in biases]
    in_specs += [pl.BlockSpec(b.shape, lambda *ids: (0, 0, 0)) for b in biases]
    grid = geom.outer + ((geom.n_tiles,) if has_prev else ())
    oshape = geom.view_shape(AW)
    o, l = pl.pallas_call(
        body, name=f"attn_fwd{gi}", grid=grid, in_specs=in_specs,
        out_specs=[geom.spec(AW, 0, cur), geom.spec(AW, 0, cur)],
        out_shape=[_sds(oshape, F32), _sds(oshape, F32)],
        compiler_params=_cp(("arbitrary",) * nd),
    )(*args, *biases)
    return geom.unview(o), geom.unview(l)


def _attn_combine(os_, ls_, rest):
    S = rest.shape[0]
    tm = 512

    def body(o0, o1, o2, l0, l1, l2, z_ref, attn_ref, L_ref, ag_ref, agt_ref):
        la, lb, lc = l0[...], l1[...], l2[...]
        m = jnp.maximum(jnp.maximum(la, lb), lc)
        ea, eb, ec = jnp.exp(la - m), jnp.exp(lb - m), jnp.exp(lc - m)
        den = ea + eb + ec
        attn = (ea * o0[...] + eb * o1[...] + ec * o2[...]) / den
        attn_ref[...] = attn
        L_ref[...] = m + jnp.log(den)
        z = z_ref[...]
        ag = attn * (z * _sigmoid(z))
        ag_ref[...] = ag.astype(ag_ref.dtype)
        agt_ref[...] = ag.T.astype(agt_ref.dtype)

    t = pl.BlockSpec((tm, AW), lambda i: (i, 0))
    return pl.pallas_call(
        body, name="attn_combine", grid=(S // tm,),
        in_specs=[t] * 6 + [pl.BlockSpec((tm, AW), lambda i: (i, 0))],
        out_specs=[t, t, t, pl.BlockSpec((AW, tm), lambda i: (0, i))],
        out_shape=[_sds((S, AW), F32), _sds((S, AW), F32), _sds((S, AW), CDT), _sds((AW, S), CDT)],
        compiler_params=_cp(("parallel",)),
    )(*os_, *ls_, rest)


def _pool_counts(r, LS, win):
    l = lax.broadcasted_iota(jnp.int32, (LS, 1), 0)
    return jnp.minimum(16 * l + (r + 1), win).astype(F32)


def _pool_fwd(rest, pool_w, pool_scale):
    S = rest.shape[0]
    LS = S // R16
    r3 = rest.reshape(R16, LS, REST_W)
    PG = 128

    def body(u_ref, z_ref, pw_ref, ps_ref, pt_ref, mp_ref, pg_ref, pgt_ref, e_ref):
        g = pl.program_id(0)
        win = jnp.left_shift(2, g)
        row = lax.broadcasted_iota(jnp.int32, (LS, PG), 0)
        for r in range(R16):
            e_ref[R16 + r] = u_ref[r]
        for r in range(1, R16):
            e_ref[r] = jnp.where(row == 0, 0.0, pltpu.roll(u_ref[r], 1, 0))

        def double(s):
            for i in range(2 * R16 - 1, 2 * s - 1, -1):
                e_ref[i] = e_ref[i] + e_ref[i - s]

        double(1)
        for k in (1, 2, 3):
            pl.when(g >= k)(functools.partial(double, 2 ** k))
        pw = pw_ref[...].astype(CDT)
        for r in range(R16):
            u = u_ref[r]
            pooled = e_ref[R16 + r] / _pool_counts(r, LS, win) - u
            pc = pooled.astype(CDT)
            mp = _dot(pc, pw, NN)
            z = z_ref[r]
            pg = (mp * ps_ref[...]) * (z * _sigmoid(z))
            mp_ref[r] = mp
            pg_ref[r] = pg.astype(pg_ref.dtype)
            pt_ref[:, r * LS:(r + 1) * LS] = pooled.T.astype(pt_ref.dtype)
            pgt_ref[:, r * LS:(r + 1) * LS] = pg.T.astype(pgt_ref.dtype)

    col = lambda off: pl.BlockSpec((R16, LS, PG), lambda g: (0, 0, off + g))
    tsp = pl.BlockSpec((PG, S), lambda g: (g, 0))
    pt, mp, pg, pgt = pl.pallas_call(
        body, name="pool_fwd", grid=(4,),
        in_specs=[col(24), col(4), pl.BlockSpec((None, PG, PG), lambda g: (g, 0, 0)),
                  pl.BlockSpec((1, PG), lambda g: (0, g))],
        out_specs=[tsp, col(0), col(0), tsp],
        out_shape=[_sds((AW, S), CDT), _sds((R16, LS, AW), F32), _sds((R16, LS, AW), CDT), _sds((AW, S), CDT)],
        scratch_shapes=[pltpu.VMEM((2 * R16, LS, PG), F32)],
        compiler_params=_cp(("parallel",)),
    )(r3, r3, pool_w, pool_scale)
    return pt, mp.reshape(S, AW), pg.reshape(S, AW), pgt


def _branch_merge(ag, pg, wab, wpb, rest):
    S = ag.shape[0]
    tm, tn = 512, 512

    def body(ag_ref, pg_ref, wa_ref, wp_ref, ga_ref, gp_ref, m_ref, mt_ref):
        ya = _dot(ag_ref[...], wa_ref[...], NN)
        yp = _dot(pg_ref[...], wp_ref[...], NN)
        mg = _sigmoid(ga_ref[...]) * ya + _sigmoid(gp_ref[...]) * yp
        m_ref[...] = mg.astype(m_ref.dtype)
        mt_ref[...] = mg.T.astype(mt_ref.dtype)

    a = pl.BlockSpec((tm, AW), lambda i, j: (i, 0))
    w = pl.BlockSpec((AW, tn), lambda i, j: (0, j))
    o = pl.BlockSpec((tm, tn), lambda i, j: (i, j))
    return pl.pallas_call(
        body, name="branch_merge", grid=(S // tm, D // tn),
        in_specs=[a, a, w, w, pl.BlockSpec((tm, tn), lambda i, j: (i, 2 + j)),
                  pl.BlockSpec((tm, tn), lambda i, j: (i, 4 + j))],
        out_specs=[o, pl.BlockSpec((tn, tm), lambda i, j: (j, i))],
        out_shape=[_sds((S, D), CDT), _sds((D, S), CDT)],
        compiler_params=_cp(("parallel", "parallel")),
    )(ag, pg, wab, wpb, rest, rest)


def _out_loss(merged, wout, x, target, gate, final_g):
    S = x.shape[0]

    def body(m_ref, w_ref, x_ref, t_ref, gate_ref, fg_ref, dx2_ref, dmo_ref, loss_ref, gfg_ref, dgate_ref):
        i = pl.program_id(0)

        @pl.when(i == 0)
        def _():
            loss_ref[...] = jnp.zeros_like(loss_ref)
            gfg_ref[...] = jnp.zeros_like(gfg_ref)
            dgate_ref[...] = jnp.zeros_like(dgate_ref)

        swap = _swap16()
        mo = _dot(_dot(swap, _ld(m_ref), NN).astype(CDT), w_ref[...], NN)
        gate, fg = gate_ref[...], fg_ref[...]
        x2 = x_ref[...] + gate * mo
        r2 = lax.rsqrt(jnp.mean(x2 * x2, axis=-1, keepdims=True) + EPS)
        xn2 = x2 * r2
        diff = xn2 * fg - t_ref[...]
        loss = jnp.sum(jnp.sum(diff * diff, axis=-1, keepdims=True), axis=0, keepdims=True)
        dy = diff * (1.0 / D)
        dxn2 = dy * fg
        dx2 = r2 * (dxn2 - xn2 * jnp.mean(dxn2 * xn2, axis=-1, keepdims=True))
        dx2_ref[...] = dx2
        _st(dmo_ref, _dot(swap, (dx2 * gate).astype(CDT), NN))
        loss_ref[...] += jnp.broadcast_to(loss * (0.5 / D), loss_ref.shape)
        gfg_ref[...] += jnp.sum(dy * xn2, axis=0, keepdims=True)
        dgate_ref[...] += jnp.sum(dx2 * mo, axis=0, keepdims=True)

    slab = pl.BlockSpec((R16, R16, D), lambda i: (0, i, 0))
    nat = pl.BlockSpec((TOK, D), lambda i: (i, 0))
    vec = pl.BlockSpec((1, D), lambda i: (0, 0))
    return pl.pallas_call(
        body, name="out_loss", grid=(S // TOK,),
        in_specs=[slab, _resident((D, D)), nat, nat, vec, vec],
        out_specs=[nat, slab, pl.BlockSpec((1, 128), lambda i: (0, 0)), vec, vec],
        out_shape=[_sds((S, D), F32), _sds((R16, S // R16, D), CDT), _sds((1, 128), F32), _sds((1, D), F32),
                   _sds((1, D), F32)],
        compiler_params=_cp(("arbitrary",)),
    )(merged, wout, x, target, gate, final_g)


def _bwd_gates(dmo, wout, wab, wpb, rest, ag, pg, attn, mp, pool_scale):
    S = dmo.shape[0]
    tm = 256

    def body(dmo_ref, wo_ref, wa_ref, wp_ref, za_ref, zp_ref, ga0, ga1, gp0, gp1, ag_ref, pg_ref, at_ref, mp_ref,
             ps_ref, dya_ref, dyp_ref, dr_ref, dat_ref, dl_ref, dmp_ref, dps_ref):
        i = pl.program_id(0)

        @pl.when(i == 0)
        def _():
            dps_ref[...] = jnp.zeros_like(dps_ref)

        dm = _dot(dmo_ref[...], wo_ref[...], NT)
        sa = _sigmoid(jnp.concatenate([ga0[...], ga1[...]], axis=1))
        sp = _sigmoid(jnp.concatenate([gp0[...], gp1[...]], axis=1))
        dya = (dm * sa).astype(CDT)
        dyp = (dm * sp).astype(CDT)
        dya_ref[...] = dya
        dyp_ref[...] = dyp
        ya = _dot(ag_ref[...], wa_ref[...], NN)
        yp = _dot(pg_ref[...], wp_ref[...], NN)
        dr_ref[:, 1024:2048] = (dm * ya * sa * (1.0 - sa)).astype(dr_ref.dtype)
        dr_ref[:, 2048:3072] = (dm * yp * sp * (1.0 - sp)).astype(dr_ref.dtype)
        da = _dot(dya, wa_ref[...], NT)
        dp = _dot(dyp, wp_ref[...], NT)
        za = za_ref[...]
        sga = _sigmoid(za)
        attn = at_ref[...]
        dattn = da * (za * sga)
        dr_ref[:, 0:512] = (da * attn * (sga * (1.0 + za * (1.0 - sga)))).astype(dr_ref.dtype)
        dat_ref[...] = dattn.astype(dat_ref.dtype)
        prod = dattn * attn
        low = lax.broadcasted_iota(jnp.int32, (1, 2 * HD), 1) < HD
        for j in range(NH // 2):
            x2 = prod[:, 2 * HD * j:2 * HD * (j + 1)]
            s0 = jnp.sum(jnp.where(low, x2, 0.0), axis=1, keepdims=True)
            s1 = jnp.sum(jnp.where(low, 0.0, x2), axis=1, keepdims=True)
            dl_ref[:, 2 * HD * j:2 * HD * (j + 1)] = jnp.where(low, s0, s1)
        zp = zp_ref[...]
        sgp = _sigmoid(zp)
        mpre = mp_ref[...]
        ps = ps_ref[...]
        dmixed = dp * (zp * sgp)
        dr_ref[:, 512:1024] = (dp * (mpre * ps) * (sgp * (1.0 + zp * (1.0 - sgp)))).astype(dr_ref.dtype)
        dps_ref[...] += jnp.sum(dmixed * mpre, axis=0, keepdims=True)
        dmp_ref[...] = (dmixed * ps).astype(dmp_ref.dtype)

    full = lambda shape: pl.BlockSpec(shape, lambda i: (0, 0))
    tD = pl.BlockSpec((tm, D), lambda i: (i, 0))
    tA = pl.BlockSpec((tm, AW), lambda i: (i, 0))
    rb = lambda c: pl.BlockSpec((tm, AW), lambda i: (i, c))
    return pl.pallas_call(
        body, name="bwd_gates", grid=(S // tm,),
        in_specs=[tD, full((D, D)), full((AW, D)), full((AW, D)), rb(0), rb(1), rb(2), rb(3), rb(4), rb(5),
                  tA, tA, tA, tA, full((1, AW))],
        out_specs=[tD, tD, pl.BlockSpec((tm, 3072), lambda i: (i, 0)), tA, tA, tA, full((1, AW))],
        out_shape=[_sds((S, D), CDT), _sds((S, D), CDT), _sds((S, REST_W), CDT), _sds((S, AW), CDT),
                   _sds((S, AW), F32), _sds((S, AW), CDT), _sds((1, AW), F32)],
        compiler_params=_cp(("arbitrary",), vmem=56),
    )(dmo, wout, wab, wpb, rest, rest, rest, rest, rest, rest, ag, pg, attn, mp, pool_scale)


def _pool_bwd(dmp, pt, pool_w, drest):
    S = dmp.shape[0]
    LS = S // R16
    PG = 128
    d3 = dmp.reshape(R16, LS, AW)
    dr3 = drest.reshape(R16, LS, REST_W)

    def body(d_ref, pt_ref, pw_ref, dr_in, du_ref, dpw_ref, g_ref, s_ref):
        del dr_in
        g = pl.program_id(0)
        win = jnp.left_shift(2, g)
        row = lax.broadcasted_iota(jnp.int32, (LS, PG), 0)
        pw = pw_ref[...].astype(CDT)
        dpw = jnp.zeros((PG, PG), F32)
        for r in range(R16):
            dm = d_ref[r]
            dpw = dpw + _dot(pt_ref[:, r * LS:(r + 1) * LS], dm, NN)
            dpooled = _dot(dm, pw, NT)
            g_ref[r] = dpooled
            sc = dpooled / _pool_counts(r, LS, win)
            s_ref[r] = sc
            if r < R16 - 1:
                s_ref[R16 + r] = jnp.where(row == LS - 1, 0.0, pltpu.roll(sc, LS - 1, 0))
        dpw_ref[...] = dpw

        def double(s):
            for i in range(0, 2 * R16 - 2 * s):
                s_ref[i] = s_ref[i] + s_ref[i + s]

        double(1)
        for k in (1, 2, 3):
            pl.when(g >= k)(functools.partial(double, 2 ** k))
        for r in range(R16):
            du_ref[r] = (s_ref[r] - g_ref[r]).astype(du_ref.dtype)

    du, dpw = pl.pallas_call(
        body, name="pool_bwd", grid=(4,),
        in_specs=[pl.BlockSpec((R16, LS, PG), lambda g: (0, 0, g)), pl.BlockSpec((PG, S), lambda g: (g, 0)),
                  pl.BlockSpec((None, PG, PG), lambda g: (g, 0, 0)), pl.BlockSpec(memory_space=pl.ANY)],
        out_specs=[pl.BlockSpec((R16, LS, PG), lambda g: (0, 0, 24 + g)),
                   pl.BlockSpec((None, PG, PG), lambda g: (g, 0, 0))],
        out_shape=[_sds((R16, LS, REST_W), CDT), _sds((4, PG, PG), F32)],
        scratch_shapes=[pltpu.VMEM((R16, LS, PG), F32), pltpu.VMEM((2 * R16, LS, PG), F32)],
        input_output_aliases={3: 0},
        compiler_params=_cp(("parallel",)),
    )(d3, pt, pool_w, dr3)
    return du.reshape(S, REST_W), dpw


def _attn_bwd(qkv, dattn, Lb, dlb, biases, geom, dqkv_prev):
    S = qkv.shape[0]
    gi, T, R, has_prev, N = geom.gi, geom.T, geom.rows, geom.has_prev, geom.n_tiles
    H2 = T // 2
    nd_outer = len(geom.outer)
    qv = geom.view(qkv)
    cq, ck, cv = 3 * gi, 3 * gi + 1, 3 * gi + 2
    W3 = 3 * AW

    def heads(q_ref, kp_ref, kc_ref, vp_ref, vc_ref, do_ref, L_ref, dl_ref, ba_ref, bb_ref, pm, dsa_ref, dsb_ref, emit):
        q_lo, q_hi = _split_rows(_ld(q_ref), R)
        do_lo, do_hi = _split_rows(_ld(do_ref), R)
        L_lo, L_hi = _split_rows(_ld(L_ref), R)
        dl_lo, dl_hi = _split_rows(_ld(dl_ref), R)
        kc_lo, kc_hi = _split_rows(_ld(kc_ref), R)
        vc_lo, vc_hi = _split_rows(_ld(vc_ref), R)
        k_b = jnp.concatenate([kc_lo, kc_hi], axis=0)
        v_b = jnp.concatenate([vc_lo, vc_hi], axis=0)
        if has_prev:
            k_a = jnp.concatenate([_split_rows(_ld(kp_ref), R)[1], kc_lo], axis=0)
            v_a = jnp.concatenate([_split_rows(_ld(vp_ref), R)[1], vc_lo], axis=0)
        else:
            k_a, v_a = k_b, v_b
        low = lax.broadcasted_iota(jnp.int32, (1, 2 * HD), 1) < HD
        for j in range(NH // 2):
            sl = slice(2 * HD * j, 2 * HD * (j + 1))
            res = []
            for q, k, v, do, L, dl, b_ref, mask, ds_ref in (
                    (q_lo, k_a, v_a, do_lo, L_lo, dl_lo, ba_ref, pm, dsa_ref),
                    (q_hi, k_b, v_b, do_hi, L_hi, dl_hi, bb_ref, None, dsb_ref)):
                q2 = q[:, sl].astype(F32) * 0.125
                do2 = do[:, sl].astype(F32)
                k2, v2 = k[:, sl], v[:, sl]
                L2, dl2 = L[:, sl], dl[:, sl]
                qs = jnp.concatenate([jnp.where(low, q2, 0.0), jnp.where(low, 0.0, q2)], axis=0).astype(q.dtype)
                dom = jnp.concatenate([jnp.where(low, do2, 0.0), jnp.where(low, 0.0, do2)], axis=0).astype(q.dtype)
                Lc = jnp.concatenate([L2[:, 0:1], L2[:, HD:HD + 1]], axis=0)
                dlc = jnp.concatenate([dl2[:, 0:1], dl2[:, HD:HD + 1]], axis=0)
                s = _dot(qs, k2, NT) + b_ref[j]
                if mask is not None:
                    s = s + mask
                p = jnp.exp(s - Lc)
                ds = p * (_dot(dom, v2, NT) - dlc)
                ds_ref[j] += ds
                dsc = ds.astype(q.dtype)
                dq_st = _dot(dsc, k2, NN)
                res.append((jnp.where(low, dq_st[:H2], dq_st[H2:]) * 0.125, _dot(dsc, qs, TN),
                            _dot(p.astype(q.dtype), dom, TN)))
            (dq_lo, dk_a, dv_a), (dq_hi, dk_b, dv_b) = res
            dq = _merge_rows(dq_lo, dq_hi, R)
            if has_prev:
                emit(sl, dq, _merge_rows(dk_a[H2:] + dk_b[:H2], dk_b[H2:], R),
                     _merge_rows(dv_a[H2:] + dv_b[:H2], dv_b[H2:], R), dk_a[:H2], dv_a[:H2])
            else:
                emit(sl, dq, _merge_rows(dk_a[:H2] + dk_b[:H2], dk_a[H2:] + dk_b[H2:], R),
                     _merge_rows(dv_a[:H2] + dv_b[:H2], dv_a[H2:] + dv_b[H2:], R), None, None)

    def first_step():
        ok = pl.program_id(nd_outer) == 0 if has_prev else pl.program_id(0) == 0
        for a in range(nd_outer if has_prev else 0):
            ok = ok & (pl.program_id(a) == 0)
        return ok

    if has_prev:
        def body(q_ref, kp_ref, kc_ref, vp_ref, vc_ref, do_ref, L_ref, dl_ref, ba_ref, bb_ref, prev_in, out_ref,
                 dsa_ref, dsb_ref, hold, new, pp):
            del prev_in
            s_id = pl.program_id(nd_outer)

            @pl.when(first_step())
            def _():
                dsa_ref[...] = jnp.zeros_like(dsa_ref)
                dsb_ref[...] = jnp.zeros_like(dsb_ref)

            @pl.when(s_id < N)
            def _():
                def emit(sl, dq, dk, dv, dk_prev, dv_prev):
                    new[:, sl] = dq
                    new[:, slice(AW + sl.start, AW + sl.stop)] = dk
                    new[:, slice(2 * AW + sl.start, 2 * AW + sl.stop)] = dv
                    pp[:, sl] = dk_prev
                    pp[:, slice(AW + sl.start, AW + sl.stop)] = dv_prev

                heads(q_ref, kp_ref, kc_ref, vp_ref, vc_ref, do_ref, L_ref, dl_ref, ba_ref, bb_ref,
                      _prev_mask(s_id, H2, T), dsa_ref, dsb_ref, emit)

            @pl.when((s_id >= 1) & (s_id < N))
            def _():
                _st(out_ref, hold[:, 0:AW], slice(0, AW))
                late = _merge_rows(jnp.zeros((H2, 2 * AW), F32), pp[...], R)
                _st(out_ref, hold[:, AW:W3] + late, slice(AW, W3))

            @pl.when(s_id == N)
            def _():
                _st(out_ref, hold[...])

            @pl.when(s_id < N)
            def _():
                hold[...] = new[...]

        cur = lambda n: jnp.minimum(n, N - 1)
        prev = lambda n: jnp.clip(n - 1, 0, N - 1)
        delayed = lambda n: jnp.maximum(n - 1, 0)
        in_specs = [geom.spec(AW, cq, cur), geom.spec(AW, ck, prev), geom.spec(AW, ck, cur),
                    geom.spec(AW, cv, prev), geom.spec(AW, cv, cur),
                    geom.spec(AW, 0, cur), geom.spec(AW, 0, cur), geom.spec(AW, 0, cur)]
        args = [qv] * 5 + [geom.view(dattn), geom.view(Lb), geom.view(dlb)]
        out_spec = geom.spec(W3, gi, delayed)
        grid = geom.outer + (N + 1,)
        scratch = [pltpu.VMEM((T, W3), F32), pltpu.VMEM((T, W3), F32), pltpu.VMEM((H2, 2 * AW), F32)]
    else:
        def body(q_ref, kc_ref, vc_ref, do_ref, L_ref, dl_ref, ba_ref, bb_ref, prev_in, out_ref, dsa_ref, dsb_ref):
            del prev_in

            @pl.when(first_step())
            def _():
                dsa_ref[...] = jnp.zeros_like(dsa_ref)
                dsb_ref[...] = jnp.zeros_like(dsb_ref)

            def emit(sl, dq, dk, dv, dk_prev, dv_prev):
                _st(out_ref, dq, sl)
                _st(out_ref, dk, slice(AW + sl.start, AW + sl.stop))
                _st(out_ref, dv, slice(2 * AW + sl.start, 2 * AW + sl.stop))

            heads(q_ref, None, kc_ref, None, vc_ref, do_ref, L_ref, dl_ref, ba_ref, bb_ref, None, dsa_ref, dsb_ref,
                  emit)

        cur = lambda n: n
        in_specs = [geom.spec(AW, cq, cur), geom.spec(AW, ck, cur), geom.spec(AW, cv, cur),
                    geom.spec(AW, 0, cur), geom.spec(AW, 0, cur), geom.spec(AW, 0, cur)]
        args = [qv] * 3 + [geom.view(dattn), geom.view(Lb), geom.view(dlb)]
        out_spec = geom.spec(W3, gi, cur)
        grid = geom.outer
        scratch = []

    whole = lambda b: pl.BlockSpec(b.shape, lambda *ids: (0, 0, 0))
    biases = [b.reshape(NH // 2, 2 * H2, b.shape[2]) for b in biases]
    in_specs += [whole(b) for b in biases]
    in_specs.append(pl.BlockSpec(memory_space=pl.ANY))
    if dqkv_prev is None:
        last, aliases = biases[0], {}
    else:
        last, aliases = geom.view(dqkv_prev), {len(in_specs) - 1: 0}
    dq, dsa, dsb = pl.pallas_call(
        body, name=f"attn_bwd{gi}", grid=grid, in_specs=in_specs,
        out_specs=[out_spec] + [whole(b) for b in biases],
        out_shape=[_sds(geom.view_shape(QKV_W), CDT)] + [_sds(b.shape, F32) for b in biases],
        scratch_shapes=scratch,
        input_output_aliases=aliases,
        compiler_params=_cp(("arbitrary",) * len(grid), vmem=56),
    )(*args, *biases, last)
    return geom.unview(dq), [d.reshape(NH, H2, d.shape[2]) for d in (dsa, dsb)]


def _h_bwd(dh, dx2, x, norm_g, scale):
    S = x.shape[0]

    def body(dh_ref, dx2_ref, x_ref, g_ref, sc_ref, gx_ref, dsh_ref, dsc_ref, dng_ref, nat_ref):
        i = pl.program_id(0)

        @pl.when(i == 0)
        def _():
            dsh_ref[...] = jnp.zeros_like(dsh_ref)
            dsc_ref[...] = jnp.zeros_like(dsc_ref)
            dng_ref[...] = jnp.zeros_like(dng_ref)

        for r in range(R16):
            nat_ref[:, r, :] = dh_ref[r]
        dh = jnp.concatenate([nat_ref[l] for l in range(R16)], axis=0)
        g = g_ref[...]
        one_sc = 1.0 + sc_ref[...]
        xv = x_ref[...]
        r1 = lax.rsqrt(jnp.mean(xv * xv, axis=-1, keepdims=True) + EPS)
        xn = xv * r1
        dhx = dh * xn
        dxn = dh * (g * one_sc)
        dx1 = r1 * (dxn - xn * jnp.mean(dxn * xn, axis=-1, keepdims=True))
        gx_ref[...] = dx2_ref[...] + dx1
        dsh_ref[...] += jnp.sum(dh, axis=0, keepdims=True)
        dsc_ref[...] += jnp.sum(dhx * g, axis=0, keepdims=True)
        dng_ref[...] += jnp.sum(dhx * one_sc, axis=0, keepdims=True)

    nat = pl.BlockSpec((TOK, D), lambda i: (i, 0))
    vec = pl.BlockSpec((1, D), lambda i: (0, 0))
    return pl.pallas_call(
        body, name="h_bwd", grid=(S // TOK,),
        in_specs=[pl.BlockSpec((R16, R16, D), lambda i: (0, i, 0)), nat, nat, vec, vec],
        out_specs=[nat, vec, vec, vec],
        out_shape=[_sds((S, D), F32), _sds((1, D), F32), _sds((1, D), F32), _sds((1, D), F32)],
        scratch_shapes=[pltpu.VMEM((R16, R16, D), F32)],
        compiler_params=_cp(("arbitrary",)),
    )(dh, dx2, x, norm_g, scale)


def _local_step(x, target, shift, scale, gate, norm_g, w_in_full, pool_w, pool_scale, wab, wpb, wout, rel_bias,
                final_g):
    st = _local_pre(x, shift, scale, norm_g, rel_bias)
    st = _local_grads(st, target, gate, w_in_full, pool_w, pool_scale, wab, wpb, wout, final_g)
    d_rel_bias = _local_bias_grads(st, 0) + _local_bias_grads(st, 1)
    gx, dmod, dng = _local_input_grads(st, w_in_full, norm_g, scale)
    return dict(loss=st["loss"], grad_x=gx, dw_in=st["dw_in"], dw_ab=st["dw_ab"], dw_pb=st["dw_pb"],
                dw_out=st["dw_out"], d_norm_g=dng, d_pool_w=st["dpw"], d_pool_scale=st["dps"],
                d_rel_bias=d_rel_bias, d_final_g=st["gfg"], dmod=dmod)


def _local_pre(x, shift, scale, norm_g, rel_bias, dep=None):
    S = x.shape[0]
    LS = S // R16
    geoms = [_Geom(gi, LS) for gi in range(3)]
    idx_np = [[g.bucket_index(sub) for sub in range(2)] for g in geoms]
    buckets = [[tuple(int(b) for b in np.unique(ix) if b >= 0) for ix in pair] for pair in idx_np]
    idxs = [[jnp.asarray(ix) for ix in pair] for pair in idx_np]
    biases = [[_bias_table(idxs[gi][sub], buckets[gi][sub], rel_bias, gi, sub, dep) for sub in range(2)]
              for gi in range(3)]
    h = _norm_mod(x, norm_g, shift, scale, dep).reshape(S, D)
    return dict(geoms=geoms, idxs=idxs, buckets=buckets, biases=biases, x=x, h=h, ht=_tcast(h, "h_transpose"))


def _local_grads(st, target, gate, w_in_full, pool_w, pool_scale, wab, wpb, wout, final_g):
    geoms, idxs, biases, x, h, ht = (st[k] for k in ("geoms", "idxs", "biases", "x", "h", "ht"))
    S = h.shape[0]
    LS = S // R16
    qkv, rest = _proj(h, w_in_full)

    os_, ls_ = [], []
    for gi in range(3):
        o, l = _attn_fwd(qkv, biases[gi], geoms[gi])
        os_.append(o)
        ls_.append(l)
    attn, Lb, ag, agt = _attn_combine(os_, ls_, rest)
    pt, mp, pg, pgt = _pool_fwd(rest, pool_w, pool_scale)
    merged, mt = _branch_merge(ag, pg, wab, wpb, rest)
    dx2, dmo3, loss, gfg, dgate = _out_loss(merged.reshape(R16, LS, D), wout, x, target, gate, final_g)
    dmo = dmo3.reshape(S, D)

    dya, dyp, drest, dattn, dlb, dmp, dps = _bwd_gates(dmo, wout, wab, wpb, rest, ag, pg, attn, mp, pool_scale)
    dw_out = _mm_tn(mt, dmo, n_total=D, out_blk=lambda j: j, name="dw_out")
    dw_ab = _mm_tn(agt, dya, n_total=D, out_blk=lambda j: j, name="dw_attn_br")
    dw_pb = _mm_tn(pgt, dyp, n_total=D, out_blk=lambda j: j, name="dw_pool_br")
    drest, dpw = _pool_bwd(dmp, pt, pool_w, drest)

    dqkv = None
    dsums = []
    for gi in range(3):
        dqkv, dsum = _attn_bwd(qkv, dattn, Lb, dlb, biases[gi], geoms[gi], dqkv)
        dsums.append(dsum)

    dw_in = _mm_tn(ht, dqkv, n_total=8192, out_blk=lambda j: j, name="dw_in_qkv", with_low=True)
    dw_in, dw_in_low = _mm_tn(ht, drest, n_total=8192, out_blk=_rest_blk, name="dw_in_rest", prev=dw_in, with_low=True)
    return dict(st, loss=loss[0, 0], dw_in=dw_in, dw_in_low=dw_in_low, dw_ab=dw_ab, dw_pb=dw_pb, dw_out=dw_out, dpw=dpw, dps=dps, gfg=gfg,
                dgate=dgate, dsums=dsums, dqkv=dqkv, drest=drest, dx2=dx2)


def _local_bias_grads(st, sub, dep=None):
    return jnp.concatenate([_bias_grad(st["dsums"][gi][sub], st["idxs"][gi][sub], st["buckets"][gi][sub], gi, sub, dep)
                            for gi in range(3)], axis=1)


def _dh_h_bwd(dqkv, drest, w, dx2, x, norm_g, scale, dep=None):
    S = x.shape[0]
    dep_specs, dep_args = _after(dep)

    def body(a_ref, b_ref, w_ref, dx2_ref, x_ref, g_ref, sc_ref, *rest):
        gx_ref, dsh_ref, dsc_ref, dng_ref, nat_ref = rest[-5:]
        i = pl.program_id(0)

        @pl.when(i == 0)
        def _():
            dsh_ref[...] = jnp.zeros_like(dsh_ref)
            dsc_ref[...] = jnp.zeros_like(dsc_ref)
            dng_ref[...] = jnp.zeros_like(dng_ref)

        a, b = _ld(a_ref), _ld(b_ref)
        acc = _dot(a[:, 0:512], _w512(w_ref, 0), NT)
        for j in range(1, 9):
            acc = acc + _dot(a[:, 512 * j:512 * (j + 1)], _w512(w_ref, j), NT)
        off = 0
        for lo, hi in REST_COLS:
            for j in range(lo, hi):
                acc = acc + _dot(b[:, off:off + 512], _w512(w_ref, j), NT)
                off += 512
        for r in range(R16):
            nat_ref[:, r, :] = acc[R16 * r:R16 * (r + 1)]
        dh = jnp.concatenate([nat_ref[l] for l in range(R16)], axis=0)
        g = g_ref[...]
        one_sc = 1.0 + sc_ref[...]
        xv = x_ref[...]
        r1 = lax.rsqrt(jnp.mean(xv * xv, axis=-1, keepdims=True) + EPS)
        xn = xv * r1
        dhx = dh * xn
        dxn = dh * (g * one_sc)
        dx1 = r1 * (dxn - xn * jnp.mean(dxn * xn, axis=-1, keepdims=True))
        gx_ref[...] = dx2_ref[...] + dx1
        dsh_ref[...] += jnp.sum(dh, axis=0, keepdims=True)
        dsc_ref[...] += jnp.sum(dhx * g, axis=0, keepdims=True)
        dng_ref[...] += jnp.sum(dhx * one_sc, axis=0, keepdims=True)

    nat = pl.BlockSpec((TOK, D), lambda i: (i, 0))
    vec = pl.BlockSpec((1, D), lambda i: (0, 0))
    slab = lambda W: pl.BlockSpec((R16, R16, W), lambda i: (0, i, 0))
    return pl.pallas_call(
        body, name="dh_h_bwd", grid=(S // TOK,),
        in_specs=[slab(QKV_W), slab(REST_W), _resident((8, D, 1024)), nat, nat, vec, vec] + dep_specs,
        out_specs=[nat, vec, vec, vec],
        out_shape=[_sds((S, D), F32), _sds((1, D), F32), _sds((1, D), F32), _sds((1, D), F32)],
        scratch_shapes=[pltpu.VMEM((R16, R16, D), F32)],
        compiler_params=_cp(("arbitrary",), vmem=56),
    )(dqkv, drest, w, dx2, x, norm_g, scale, *dep_args)


def _local_input_grads(st, w_in_full, norm_g, scale, dep=None):
    S = st["h"].shape[0]
    LS = S // R16
    gx, dsh, dsc, dng = _dh_h_bwd(st["dqkv"].reshape(R16, LS, QKV_W), st["drest"].reshape(R16, LS, REST_W),
                                  w_in_full, st["dx2"], st["x"], norm_g, scale, dep)
    return gx, jnp.concatenate([dsh, dsc, st["dgate"]], axis=1), dng


def _my_place():
    return lax.axis_index("x"), lax.axis_index("y"), lax.axis_index("c")


def _block_of(ref, axis, idx, width):
    if axis is None:
        return ref.at[idx]
    start = pl.multiple_of(idx * width, width)
    if axis == 0:
        return ref.at[pl.ds(start, width)]
    return ref.at[:, pl.ds(start, width)]


_SEM = pl.BlockSpec(memory_space=pltpu.SEMAPHORE)
_EFFECT = pltpu.SideEffectType.DATAFLOW_SIDE_EFFECTING


def _split_start(name, srcs, lands, ncopy, plan, dep=None):
    arrs = list(srcs) + list(lands)
    na, ns = len(arrs), len(srcs)
    dep_specs, dep_args = _after(dep)

    def body(*refs):
        send_sems, recv_sems, token = refs[na + len(dep_args)], refs[na + len(dep_args) + 1], refs[-1]
        for k, (s, d, dev) in enumerate(plan(refs[:ns], refs[ns:na])):
            pltpu.make_async_remote_copy(src_ref=s, dst_ref=d, send_sem=send_sems.at[k], recv_sem=recv_sems.at[k],
                                         device_id=dev, device_id_type=MESH).start()
        token[...] = jnp.zeros_like(token)

    outs = pl.pallas_call(
        body, name=name,
        out_shape=(pltpu.SemaphoreType.DMA((ncopy,)), pltpu.SemaphoreType.DMA((ncopy,)),
                   *[pltpu.HBM(a.shape, a.dtype) for a in arrs], _sds((8, 128), F32)),
        in_specs=[_HBM] * na + dep_specs,
        out_specs=(_SEM, _SEM, *[_HBM] * na, pl.BlockSpec(memory_space=pltpu.VMEM)),
        input_output_aliases={i: 2 + i for i in range(na)},
        compiler_params=pltpu.CompilerParams(has_side_effects=_EFFECT),
    )(*[pltpu.with_memory_space_constraint(a, pltpu.HBM) for a in arrs], *dep_args)
    return (outs[0], outs[1]), list(outs[2:2 + ns]), list(outs[2 + ns:2 + na]), outs[-1]


def _split_wait(name, sems, srcs, lands, plan, after):
    arrs = list(srcs) + list(lands)
    na, ns = len(arrs), len(srcs)
    after = list(after) if isinstance(after, (list, tuple)) else [after]

    def body(*refs):
        send_sems, recv_sems = refs[na], refs[na + 1]
        for k, (s, d, dev) in enumerate(plan(refs[:ns], refs[ns:na])):
            cp = pltpu.make_async_remote_copy(src_ref=s, dst_ref=d, send_sem=send_sems.at[k], recv_sem=recv_sems.at[k],
                                              device_id=dev, device_id_type=MESH)
            cp.wait_send()
            cp.wait_recv()

    outs = pl.pallas_call(
        body, name=name, out_shape=[pltpu.HBM(a.shape, a.dtype) for a in arrs],
        in_specs=[_HBM] * na + [_SEM, _SEM] + [pl.BlockSpec(memory_space=pl.ANY)] * len(after),
        out_specs=[_HBM] * na,
        input_output_aliases={i: i for i in range(na)},
        compiler_params=pltpu.CompilerParams(has_side_effects=_EFFECT),
    )(*arrs, sems[0], sems[1], *after)
    return list(outs[:ns]), list(outs[ns:])


def _gather_chips_plan(layout):
    def plan(src, land):
        x, y, c = _my_place()
        me = 4 * x + 2 * y + c
        mine = [_block_of(land[p], layout[p][0], me, layout[p][1]) for p in range(len(land))]
        return [(mine[p], mine[p], (cx, cy, c)) for p in range(len(land)) for cx, cy in [(1 - x, y), (x, 1 - y)]]
    return plan


def _pair_plan(layout):
    def plan(src, land):
        x, y, c = _my_place()
        n = len(layout)
        return [(_block_of(src[p], layout[p][0], 2 * q + (1 - c), layout[p][1]), land[p].at[q], (x, y, 1 - c))
                for p in range(n) for q in range(4)] + [(src[p], land[p], (x, y, 1 - c)) for p in range(n, len(src))]
    return plan


def _chips_plan(n):
    def plan(src, land):
        x, y, c = _my_place()
        chips = [(1 - x, y), (x, 1 - y), (1 - x, 1 - y)]
        return [(src[p].at[2 * cx + cy], land[p].at[2 * x + y], (cx, cy, c)) for p in range(n) for cx, cy in chips] + \
               [(src[p], land[p].at[2 * x + y], (cx, cy, c)) for p in range(n, len(src)) for cx, cy in chips]
    return plan


def _pair_share(gathered):
    n = len(gathered)

    def body(*refs):
        land = refs[n:2 * n]
        send_sems, recv_sems, fwd_send, fwd_recv = refs[2 * n:]
        x, y, c = _my_place()
        k_dg = 3 - (2 * x + y)
        south = c == 0
        pass_on = 4 * jnp.where(south, 1 - x, x) + 2 * jnp.where(south, y, 1 - y) + c
        to = (jnp.where(south, x, 1 - x), jnp.where(south, 1 - y, y), c)

        def forward(p):
            return pltpu.make_async_remote_copy(
                src_ref=land[p].at[pass_on], dst_ref=land[p].at[pass_on], send_sem=fwd_send.at[p],
                recv_sem=fwd_recv.at[p], device_id=to, device_id_type=MESH)

        def to_sibling(p, k):
            blk = land[p].at[2 * k + c]
            return pltpu.make_async_remote_copy(
                src_ref=blk, dst_ref=blk, send_sem=send_sems.at[4 * p + k], recv_sem=recv_sems.at[4 * p + k],
                device_id=(x, y, 1 - c), device_id_type=MESH)

        for p in range(n):
            forward(p).start()
        for p in range(n):
            for k in range(4):
                @pl.when(k != k_dg)
                def _():
                    to_sibling(p, k).start()
        for p in range(n):
            dg = land[p].at[2 * k_dg + c]
            pltpu.make_async_remote_copy(src_ref=dg, dst_ref=dg, send_sem=fwd_send.at[p], recv_sem=fwd_recv.at[p],
                                         device_id=to, device_id_type=MESH).wait_recv()
            for k in range(4):
                @pl.when(k == k_dg)
                def _():
                    to_sibling(p, k).start()
        for p in range(n):
            forward(p).wait_send()
            for k in range(4):
                to_sibling(p, k).wait_send()
                blk = land[p].at[2 * k + 1 - c]
                pltpu.make_async_remote_copy(
                    src_ref=blk, dst_ref=blk, send_sem=send_sems.at[4 * p + k], recv_sem=recv_sems.at[4 * p + k],
                    device_id=(x, y, 1 - c), device_id_type=MESH).wait_recv()

    return pl.pallas_call(
        body, name="gather_pair_share", in_specs=[_HBM] * n, out_specs=[_HBM] * n,
        out_shape=[_sds(g.shape, g.dtype) for g in gathered],
        input_output_aliases={p: p for p in range(n)},
        scratch_shapes=[pltpu.SemaphoreType.DMA((4 * n,)), pltpu.SemaphoreType.DMA((4 * n,)),
                        pltpu.SemaphoreType.DMA((n,)), pltpu.SemaphoreType.DMA((n,))],
    )(*gathered)


def _place_slot(arrs, slot, n_slots, name, out_dtype=None):
    n = len(arrs)

    def body(s_ref, *refs):
        for p in range(n):
            refs[n + p][...] = refs[p][...].astype(refs[n + p].dtype)

    def spec(a):
        tail = a.shape[-2:]
        return pl.BlockSpec((None,) + tail, lambda i, s_ref: (s_ref[0], 0, 0))

    in_specs = [spec(a) if a.ndim == 3 else pl.BlockSpec(a.shape, lambda i, s_ref: (0, 0)) for a in arrs]
    return pl.pallas_call(
        body, name=name,
        grid_spec=pltpu.PrefetchScalarGridSpec(num_scalar_prefetch=1, grid=(1,), in_specs=in_specs,
                                               out_specs=[spec(a) for a in arrs]),
        out_shape=[_sds((n_slots,) + a.shape[-2:], out_dtype or a.dtype) for a in arrs],
        compiler_params=_cp(("arbitrary",)),
    )(slot, *arrs)


def _cols_from_blocks(w8, name):
    _, R, C = w8.shape

    def body(w_ref, o_ref):
        for e in range(8):
            o_ref[:, C * e:C * (e + 1)] = w_ref[e]

    return pl.pallas_call(body, name=name, out_shape=_sds((R, 8 * C), w8.dtype), compiler_params=_cp())(w8)


def _all_gather_small(srcs, name):
    n = len(srcs)
    vmem = pl.BlockSpec(memory_space=pltpu.VMEM)

    def body(*refs):
        src, out = refs[:n], refs[n:2 * n]
        send_sems, recv_sems = refs[2 * n:]
        x, y, c = _my_place()
        me = 4 * x + 2 * y + c
        flip = lambda v, f: 1 - v if f else v
        peers = [(flip(x, fx), flip(y, fy), flip(c, fc)) for fx in (0, 1) for fy in (0, 1) for fc in (0, 1)][1:]
        cps = []
        for p in range(n):
            out[p][me] = src[p][...]
            for k, peer in enumerate(peers):
                cps.append(pltpu.make_async_remote_copy(
                    src_ref=src[p], dst_ref=out[p].at[me], send_sem=send_sems.at[7 * p + k],
                    recv_sem=recv_sems.at[7 * p + k], device_id=peer, device_id_type=MESH))
        for cp in cps:
            cp.start()
        for cp in cps:
            cp.wait()

    return pl.pallas_call(
        body, name=name, in_specs=[vmem] * n, out_specs=[vmem] * n,
        out_shape=[_sds((8,) + a.shape, a.dtype) for a in srcs],
        scratch_shapes=[pltpu.SemaphoreType.DMA((7 * n,)), pltpu.SemaphoreType.DMA((7 * n,))],
    )(*srcs)


def _adaln_exchange(c, w_ada, b_part):
    vmem = pl.BlockSpec(memory_space=pltpu.VMEM)

    def body(c_ref, w_ref, b_ref, call_ref, mod_ref, part_ref, send_sems, recv_sems):
        x, y, c = _my_place()
        me = 4 * x + 2 * y + c
        flip = lambda v, f: 1 - v if f else v
        peers = [(flip(x, fx), flip(y, fy), flip(c, fc)) for fx in (0, 1) for fy in (0, 1) for fc in (0, 1)][1:]
        call_ref[me] = c_ref[...]
        first = [pltpu.make_async_remote_copy(src_ref=c_ref, dst_ref=call_ref.at[me], send_sem=send_sems.at[k],
                                              recv_sem=recv_sems.at[k], device_id=peer, device_id_type=MESH)
                 for k, peer in enumerate(peers)]
        for cp in first:
            cp.start()
        for cp in first:
            cp.wait()
        c_all = jnp.concatenate([call_ref[j] for j in range(8)], axis=0)
        part_ref[...] = _dot(c_all.astype(CDT), w_ref[...].astype(CDT), NN) + b_ref[...]
        mod_ref[me] = part_ref[pl.ds(me, 1), :]
        second = [pltpu.make_async_remote_copy(
            src_ref=part_ref.at[pl.ds(4 * px + 2 * py + pc, 1)], dst_ref=mod_ref.at[me], send_sem=send_sems.at[7 + k],
            recv_sem=recv_sems.at[7 + k], device_id=(px, py, pc), device_id_type=MESH)
            for k, (px, py, pc) in enumerate(peers)]
        for cp in second:
            cp.start()
        for cp in second:
            cp.wait()

    return pl.pallas_call(
        body, name="adaln_exchange", in_specs=[vmem] * 3, out_specs=[vmem] * 2,
        out_shape=[_sds((8, 1, D), F32), _sds((8, 1, w_ada.shape[1]), F32)],
        scratch_shapes=[pltpu.VMEM((8, w_ada.shape[1]), F32), pltpu.SemaphoreType.DMA((14,)),
                        pltpu.SemaphoreType.DMA((14,))],
        compiler_params=_cp(),
    )(c, w_ada, b_part)


def _pair_add(dw, recv, axis, width, cidx, name):
    if axis == 0:
        tr = min(width, 256)
        C = dw.shape[1]
        grid = (4, width // tr)
        nb = width // tr
        own = pl.BlockSpec((tr, C), lambda q, i, c_ref: ((2 * q + c_ref[0]) * nb + i, 0))
        oth = pl.BlockSpec((None, tr, C), lambda q, i, c_ref: (q, i, 0))
    else:
        R = dw.shape[0]
        tr = min(R, 256)
        grid = (4, R // tr)
        own = pl.BlockSpec((tr, width), lambda q, i, c_ref: (i, 2 * q + c_ref[0]))
        oth = pl.BlockSpec((None, tr, width), lambda q, i, c_ref: (q, i, 0))

    def body(c_ref, a_ref, b_ref, o_ref):
        o_ref[...] = (a_ref[...] + b_ref[...].astype(F32)).astype(o_ref.dtype)

    return pl.pallas_call(
        body, name=name,
        grid_spec=pltpu.PrefetchScalarGridSpec(num_scalar_prefetch=1, grid=grid, in_specs=[own, oth], out_specs=oth),
        out_shape=_sds(recv.shape, CDT), compiler_params=_cp(("parallel", "parallel")),
    )(cidx, dw, recv)


def _add_pairs(xs, ys, name):
    n = len(xs)

    def body(*refs):
        for p in range(n):
            refs[2 * n + p][...] = refs[p][...] + refs[n + p][...]

    return pl.pallas_call(body, name=name, out_shape=[_sds(a.shape, F32) for a in xs])(*xs, *ys)


def _adam_math(w, g, m, v):
    m = ADAM_B1 * m + (1.0 - ADAM_B1) * g
    v = ADAM_B2 * v + (1.0 - ADAM_B2) * (g * g)
    m_hat = m / (1.0 - ADAM_B1 ** ADAM_STEP)
    v_hat = v / (1.0 - ADAM_B2 ** ADAM_STEP)
    delta = -ADAM_LR * (m_hat / (jnp.sqrt(v_hat) + ADAM_EPS) + ADAM_WD * w)
    return delta, m, v


def _adam(gparts, w, m, v, name):
    P, R, C = gparts.shape
    tr = R if R <= 256 else 256

    def body(g_ref, w_ref, m_ref, v_ref, go_ref, d_ref, mo_ref, vo_ref):
        g = g_ref[0].astype(F32)
        for i in range(1, P):
            g = g + g_ref[i].astype(F32)
        d, mn, vn = _adam_math(w_ref[...], g, m_ref[...], v_ref[...])
        go_ref[...] = g
        d_ref[...] = d
        mo_ref[...] = mn
        vo_ref[...] = vn

    t = pl.BlockSpec((tr, C), lambda i: (i, 0))
    return pl.pallas_call(
        body, name=name, grid=(R // tr,),
        in_specs=[pl.BlockSpec((P, tr, C), lambda i: (0, i, 0)), t, t, t], out_specs=[t, t, t, t],
        out_shape=[_sds((R, C), F32)] * 4, compiler_params=_cp(("parallel",)),
    )(gparts, w, m, v)


def _sum_parts(gparts, name):
    P, R, C = gparts.shape

    def body(g_ref, o_ref):
        g = g_ref[0]
        for i in range(1, P):
            g = g + g_ref[i]
        o_ref[...] = g

    return pl.pallas_call(body, name=name, out_shape=_sds((R, C), F32))(gparts)


def _adam_w_ada(c_t, dm, w, m, v):
    R, C = w.shape

    def body(c_ref, dm_ref, w_ref, m_ref, v_ref, go_ref, d_ref, mo_ref, vo_ref):
        g = c_ref[:, 0:1] * dm_ref[0:1, :]
        for b in range(1, 8):
            g = g + c_ref[:, b:b + 1] * dm_ref[b:b + 1, :]
        d, mn, vn = _adam_math(w_ref[...], g, m_ref[...], v_ref[...])
        go_ref[...] = g
        d_ref[...] = d
        mo_ref[...] = mn
        vo_ref[...] = vn

    return pl.pallas_call(body, name="adam_w_ada", out_shape=[_sds((R, C), F32)] * 4,
                          compiler_params=_cp())(c_t, dm, w, m, v)


def kernel(x, c, norm_g, w_ada, b_ada, w_in, pool_w, pool_scale, w_attn_br, w_pool_br, w_out, rel_bias, final_g, loss_target, m_norm_g, m_w_ada, m_b_ada, m_w_in, m_pool_w, m_pool_scale, m_w_attn_br, m_w_pool_br, m_w_out, m_rel_bias, m_final_g, v_norm_g, v_w_ada, v_b_ada, v_w_in, v_pool_w, v_pool_scale, v_w_attn_br, v_w_pool_br, v_w_out, v_rel_bias, v_final_g):
    S = x.shape[1]
    px, py, pc = _my_place()
    me = 4 * px + 2 * py + pc
    cidx = jnp.reshape(pc, (1,)).astype(jnp.int32)

    layout = [(1, 1024), (1, 128), (1, 128), (0, 128)]
    final_g2 = final_g.reshape(1, D)
    me1 = jnp.reshape(me, (1,)).astype(jnp.int32)

    b_part = lax.dynamic_slice(b_ada, (0, me * 384), (1, 384))
    c_all, mod_all = _adaln_exchange(c, w_ada[0], b_part)
    c_all = c_all.reshape(8, D)
    mod = mod_all.reshape(1, 3 * D)
    shift, scale, gate = mod[:, :D], mod[:, D:2 * D], mod[:, 2 * D:]

    gplan = _gather_chips_plan([(None, 1)] * 4)
    lands = _place_slot([w_in[0], w_attn_br[0], w_pool_br[0], w_out[0]], me1, 8, "place_own_shards", CDT)
    g_sems, _, lands, tok = _split_start("gather_chips_start", [], lands, 8, gplan, dep=mod_all)
    st = _local_pre(x[0], shift, scale, norm_g, rel_bias, dep=tok)
    _, lands = _split_wait("gather_chips_wait", g_sems, [], lands, gplan,
                           [st["ht"]] + [b for pair in st["biases"] for b in pair])
    w_in_full, wab8, wpb8, wout8 = _pair_share(lands)
    wab, wpb = _cols_from_blocks(wab8, "w_attn_br_cols"), _cols_from_blocks(wpb8, "w_pool_br_cols")
    wout = wout8.reshape(D, D)

    st = _local_grads(st, loss_target[0], gate, w_in_full, pool_w[0], pool_scale, wab, wpb, wout, final_g2)

    early = [st["dpw"].reshape(AW, 128), st["dps"], st["gfg"], jnp.broadcast_to(st["loss"], (1, 128))]
    dws = [st["dw_in"], st["dw_ab"], st["dw_pb"], st["dw_out"]]
    sent = [st["dw_in_low"]] + dws[1:] + early
    pplan = _pair_plan(layout)
    pair_shapes = [(4, w, a.shape[1]) if ax == 0 else (4, a.shape[0], w) for a, (ax, w) in zip(dws, layout)]
    p_sems, sent, recv, tok = _split_start(
        "reduce_pair_start", sent,
        [lax.empty(s, a.dtype) for s, a in zip(pair_shapes + [a.shape for a in early], sent)], 20, pplan)
    d_rel_bias = _local_bias_grads(st, 0, dep=tok)
    sent, recv = _split_wait("reduce_pair_wait", p_sems, sent, recv, pplan, d_rel_bias)
    parts = [_pair_add(dw, r, ax, w, cidx, f"pair_add{i}")
             for i, (dw, r, (ax, w)) in enumerate(zip([dws[0]] + sent[1:4], recv[:4], layout))]
    parts += _add_pairs(sent[4:], recv[4:], "pair_add_smalls")

    slot1 = jnp.reshape(2 * px + py, (1,)).astype(jnp.int32)
    cplan = _chips_plan(4)
    c_sems, parts, lands, tok = _split_start("reduce_chips_start", parts,
                                             _place_slot(parts, slot1, 4, "place_own_partials"), 24, cplan)
    d_rel_bias = d_rel_bias + _local_bias_grads(st, 1, dep=tok)
    grad_x, dmod, d_norm_g = _local_input_grads(st, w_in_full, norm_g, scale, dep=tok)

    dmod_all, ng_all, rb_all = _all_gather_small([dmod, d_norm_g, d_rel_bias], "gather_late")
    o_norm_g = _adam(ng_all, norm_g, m_norm_g, v_norm_g, "adam_norm_g")
    o_rel_bias = _adam(rb_all, rel_bias, m_rel_bias, v_rel_bias, "adam_rel_bias")
    o_b_ada = _adam(dmod_all, b_ada, m_b_ada, v_b_ada, "adam_b_ada")
    dm_mine = lax.dynamic_slice(dmod_all.reshape(8, 3 * D), (0, me * 384), (8, 384))
    o_w_ada = [o[None] for o in _adam_w_ada(c_all.T, dm_mine, w_ada[0], m_w_ada[0], v_w_ada[0])]

    _, (g_in, g_ab, g_pb, g_out, g_pw, g_ps, g_fg, g_loss) = _split_wait(
        "reduce_chips_wait", c_sems, parts, lands, cplan, [o_w_ada[0], o_b_ada[0], o_norm_g[0], o_rel_bias[0]])
    o_w_in = [o[None] for o in _adam(g_in, w_in[0], m_w_in[0], v_w_in[0], "adam_w_in")]
    o_w_ab = [o[None] for o in _adam(g_ab, w_attn_br[0], m_w_attn_br[0], v_w_attn_br[0], "adam_w_attn_br")]
    o_w_pb = [o[None] for o in _adam(g_pb, w_pool_br[0], m_w_pool_br[0], v_w_pool_br[0], "adam_w_pool_br")]
    o_w_out = [o[None] for o in _adam(g_out, w_out[0], m_w_out[0], v_w_out[0], "adam_w_out")]
    o_pool_w = [o.reshape(pool_w.shape) for o in _adam(g_pw, pool_w.reshape(AW, 128), m_pool_w.reshape(AW, 128),
                                                       v_pool_w.reshape(AW, 128), "adam_pool_w")]
    o_pool_scale = _adam(g_ps, pool_scale, m_pool_scale, v_pool_scale, "adam_pool_scale")
    o_final_g = [o.reshape(D) for o in _adam(g_fg, final_g2, m_final_g.reshape(1, D), v_final_g.reshape(1, D),
                                             "adam_final_g")]
    loss = _sum_parts(g_loss, "sum_loss")[0, 0]

    per_w = [o_norm_g, o_w_ada, o_b_ada, o_w_in, o_pool_w, o_pool_scale, o_w_ab, o_w_pb, o_w_out, o_rel_bias, o_final_g]
    outs = [loss, grad_x[None]]
    for k in range(4):
        outs += [o[k] for o in per_w]
    return tuple(outs)
```

```python
import functools
import math

import numpy as np
import jax
import jax.numpy as jnp
from jax import lax
from jax.experimental import pallas as pl
from jax.experimental.pallas import tpu as pltpu

F32 = jnp.float32
CDT = jnp.bfloat16
D = 1024
HD = 64
NH = 8
AW = 512
QKV_W = 4608
REST_W = 3584
R16 = 16
EPS = 1e-6
NEG = -1e30
N_BACK = 128
ADAM_LR, ADAM_B1, ADAM_B2, ADAM_EPS, ADAM_WD, ADAM_STEP = 0.001, 0.9, 0.999, 1e-08, 0.01, 10
MESH = pl.DeviceIdType.MESH
VMEM_MB = 1024 * 1024


def _cp(sem=None, vmem=48, **kw):
    if sem is not None:
        kw["dimension_semantics"] = sem
    return pltpu.CompilerParams(vmem_limit_bytes=vmem * VMEM_MB, **kw)


def _sds(shape, dtype):
    return jax.ShapeDtypeStruct(tuple(shape), dtype)


def _rest_blk(jj):
    return jnp.where(jj == 0, 9, jnp.where(jj == 6, 10, jj + 10))


def _sigmoid(z):
    return 0.5 * jnp.tanh(0.5 * z) + 0.5


def _dot(a, b, dims):
    return lax.dot_general(a, b, (dims, ((), ())), preferred_element_type=F32)


NN = ((1,), (0,))
NT = ((1,), (1,))
TN = ((0,), (0,))


def _ld(ref):
    if len(ref.shape) == 2:
        return ref[...]
    return jnp.concatenate([ref[a] for a in range(ref.shape[0])], axis=0)


def _st(ref, val, cols=None):
    val = val.astype(ref.dtype)
    if len(ref.shape) == 2:
        if cols is None:
            ref[...] = val
        else:
            ref[:, cols] = val
        return
    rows = ref.shape[1]
    for a in range(ref.shape[0]):
        if cols is None:
            ref[a] = val[a * rows:(a + 1) * rows]
        else:
            ref[a, :, cols] = val[a * rows:(a + 1) * rows]


REST_COLS = ((9, 10), (11, 16), (10, 11))
_HBM = pl.BlockSpec(memory_space=pltpu.HBM)


def _resident(shape):
    return pl.BlockSpec(shape, lambda *ids: (0,) * len(shape), pipeline_mode=pl.Buffered(1))


def _w512(w_ref, j):
    return w_ref[j // 2, :, 512 * (j % 2):512 * (j % 2 + 1)]


def _proj(h, w):
    S = h.shape[0]
    tm = 512

    def body(a_ref, w_ref, q_ref, r_ref):
        a = a_ref[...]
        for j in range(9):
            q_ref[:, 512 * j:512 * (j + 1)] = _dot(a, _w512(w_ref, j), NN).astype(q_ref.dtype)
        off = 0
        for lo, hi in REST_COLS:
            for j in range(lo, hi):
                r_ref[:, off:off + 512] = _dot(a, _w512(w_ref, j), NN)
                off += 512

    return pl.pallas_call(
        body, name="proj", grid=(S // tm,),
        in_specs=[pl.BlockSpec((tm, D), lambda i: (i, 0)), _resident((8, D, 1024))],
        out_specs=[pl.BlockSpec((tm, QKV_W), lambda i: (i, 0)), pl.BlockSpec((tm, REST_W), lambda i: (i, 0))],
        out_shape=[_sds((S, QKV_W), CDT), _sds((S, REST_W), F32)],
        compiler_params=_cp(("parallel",), vmem=56),
    )(h, w)


def _after(dep):
    return ([], []) if dep is None else ([pl.BlockSpec(memory_space=pl.ANY)], [dep])


def _dh(dqkv, drest, w, dep=None):
    S = dqkv.shape[0]
    tm = 512
    dep_specs, dep_args = _after(dep)

    def body(a_ref, b_ref, w_ref, *rest):
        o_ref = rest[-1]
        acc = _dot(a_ref[:, 0:512], _w512(w_ref, 0), NT)
        for j in range(1, 9):
            acc = acc + _dot(a_ref[:, 512 * j:512 * (j + 1)], _w512(w_ref, j), NT)
        off = 0
        for lo, hi in REST_COLS:
            for j in range(lo, hi):
                acc = acc + _dot(b_ref[:, off:off + 512], _w512(w_ref, j), NT)
                off += 512
        o_ref[...] = acc

    return pl.pallas_call(
        body, name="dh", grid=(S // tm,),
        in_specs=[pl.BlockSpec((tm, QKV_W), lambda i: (i, 0)), pl.BlockSpec((tm, REST_W), lambda i: (i, 0)),
                  _resident((8, D, 1024))] + dep_specs,
        out_specs=pl.BlockSpec((tm, D), lambda i: (i, 0)),
        out_shape=_sds((S, D), F32),
        compiler_params=_cp(("parallel",), vmem=56),
    )(dqkv, drest, w, *dep_args)


def _mm_tn(at, b, *, n_total, out_blk, name, prev=None, tn=512, with_low=False):
    R, S = at.shape
    nb = b.shape[1] // tn
    n_out = 2 if with_low else 1

    def body(*refs):
        res = _dot(refs[0][...], refs[1][...], NN)
        refs[-n_out][...] = res
        if with_low:
            refs[-1][...] = res.astype(CDT)

    in_specs = [_resident((R, S)), pl.BlockSpec((S, tn), lambda j: (0, j))]
    args = [at, b]
    aliases = {}
    if prev is not None:
        prev = list(prev) if with_low else [prev]
        in_specs += [pl.BlockSpec(memory_space=pl.ANY)] * n_out
        args += prev
        aliases = {2 + k: k for k in range(n_out)}
    out_spec = pl.BlockSpec((R, tn), lambda j: (0, out_blk(j)))
    outs = pl.pallas_call(
        body, name=name, grid=(nb,), in_specs=in_specs,
        out_specs=[out_spec] * n_out,
        out_shape=[_sds((R, n_total), F32), _sds((R, n_total), CDT)][:n_out],
        input_output_aliases=aliases,
        compiler_params=_cp(("parallel",)),
    )(*args)
    return tuple(outs) if with_low else outs[0]


def _tcast(a, name):
    M, C = a.shape
    tm = 512

    def body(a_ref, o_ref):
        o_ref[...] = a_ref[...].astype(F32).T.astype(o_ref.dtype)

    return pl.pallas_call(
        body, name=name, grid=(M // tm,),
        in_specs=[pl.BlockSpec((tm, C), lambda i: (i, 0))],
        out_specs=pl.BlockSpec((C, tm), lambda i: (0, i)),
        out_shape=_sds((C, M), CDT), compiler_params=_cp(("parallel",)),
    )(a)


TOK = R16 * R16


def _swap16():
    i = lax.broadcasted_iota(jnp.int32, (TOK, TOK), 0)
    j = lax.broadcasted_iota(jnp.int32, (TOK, TOK), 1)
    return jnp.where(j == jnp.bitwise_and(i, 15) * 16 + jnp.right_shift(i, 4), 1.0, 0.0).astype(CDT)


def _norm_mod(x, norm_g, shift, scale, dep=None):
    S = x.shape[0]
    dep_specs, dep_args = _after(dep)

    def body(x_ref, g_ref, sh_ref, sc_ref, *rest):
        xv = x_ref[...]
        xn = xv * lax.rsqrt(jnp.mean(xv * xv, axis=-1, keepdims=True) + EPS)
        h = (xn * (g_ref[...] * (1.0 + sc_ref[...])) + sh_ref[...]).astype(CDT)
        _st(rest[-1], _dot(_swap16(), h, NN))

    vec = pl.BlockSpec((1, D), lambda i: (0, 0))
    return pl.pallas_call(
        body, name="norm_mod", grid=(S // TOK,),
        in_specs=[pl.BlockSpec((TOK, D), lambda i: (i, 0)), vec, vec, vec] + dep_specs,
        out_specs=pl.BlockSpec((R16, R16, D), lambda i: (0, i, 0)),
        out_shape=_sds((R16, S // R16, D), CDT), compiler_params=_cp(("parallel",)),
    )(x, norm_g, shift, scale, *dep_args)


def _t5_bucket_np(n):
    nf = np.maximum(n, 1).astype(np.float32)
    large = 16 + (np.log(nf / np.float32(16)) / np.float32(math.log(2048 / 16)) * np.float32(16)).astype(np.int32)
    large = np.minimum(large, 31)
    return np.where(n < 16, n, large).astype(np.int32)


class _Geom:
    def __init__(self, gi, LS):
        self.gi, self.LS = gi, LS
        self.dil = (1, 4, 16)[gi]
        if gi == 0:
            self.lead, self.rows, self.n_tiles, self.has_prev = (R16,), 16, LS // 16, True
            a = np.arange(R16)[:, None]
            ll = np.arange(16)[None, :]
            self.pos = (16 * ll + a).reshape(-1)
        elif gi == 1:
            self.lead, self.rows, self.n_tiles, self.has_prev = (4,), 64, LS // 64, True
            a = np.arange(4)[:, None]
            ll = np.arange(64)[None, :]
            self.pos = (4 * ll + a).reshape(-1)
        else:
            self.lead, self.rows, self.n_tiles, self.has_prev = (), LS, 1, False
            self.pos = np.arange(LS)
        self.T = self.pos.shape[0]
        self.Tk = 2 * self.T if self.has_prev else self.T
        self.outer = {0: (), 1: (4,), 2: (R16,)}[gi]

    def view(self, arr):
        S, C = arr.shape
        if self.gi == 1:
            return arr.reshape(4, 4, self.LS, C)
        return arr.reshape(R16, self.LS, C)

    def view_shape(self, C):
        return (4, 4, self.LS, C) if self.gi == 1 else (R16, self.LS, C)

    def unview(self, arr):
        return arr.reshape(R16 * self.LS, arr.shape[-1])

    def spec(self, W, cblk, tile):
        if self.gi == 0:
            return pl.BlockSpec((R16, 16, W), lambda n: (0, tile(n), cblk))
        if self.gi == 1:
            return pl.BlockSpec((4, None, 64, W), lambda r, n: (0, r, tile(n), cblk))
        return pl.BlockSpec((None, self.LS, W), lambda r: (r, 0, cblk))

    def halves(self):
        R = self.rows
        idx = np.arange(self.T).reshape(-1, R)
        return idx[:, :R // 2].reshape(-1), idx[:, R // 2:].reshape(-1)

    def bucket_index(self, sub):
        lo, hi = self.halves()
        pos = self.pos
        if sub == 0:
            pq = pos[lo]
            pk = np.concatenate([pos[hi] - self.T, pos[lo]]) if self.has_prev else np.concatenate([pos[lo], pos[hi]])
        else:
            pq = pos[hi]
            pk = np.concatenate([pos[lo], pos[hi]])
        dist = pq[:, None] - pk[None, :]
        ok = (dist >= 0) & (dist <= N_BACK)
        bucket = _t5_bucket_np(np.clip(dist, 0, N_BACK) * self.dil)
        return np.where(ok, bucket, -1).astype(np.int32)


def _bias_table(idx, buckets, rel_bias, gi, sub, dep=None):
    T, Tk = idx.shape
    tr = 32
    dep_specs, dep_args = _after(dep)

    def body(idx_ref, rb_ref, *rest):
        h = pl.program_id(0)
        ix = idx_ref[...]
        acc = jnp.full(ix.shape, NEG, F32)
        for b in buckets:
            acc = jnp.where(ix == b, rb_ref[b, gi * NH + h], acc)
        rest[-1][...] = acc

    return pl.pallas_call(
        body, name=f"bias_table{gi}{'ab'[sub]}", grid=(NH, T // tr),
        in_specs=[pl.BlockSpec((tr, Tk), lambda h, i: (i, 0)), pl.BlockSpec(memory_space=pltpu.SMEM)] + dep_specs,
        out_specs=pl.BlockSpec((None, tr, Tk), lambda h, i: (h, i, 0)),
        out_shape=_sds((NH, T, Tk), F32), compiler_params=_cp(("parallel", "parallel")),
    )(idx, rel_bias, *dep_args)


def _bias_grad(dsum, idx, buckets, gi, sub, dep=None):
    _, T, Tk = dsum.shape
    dep_specs, dep_args = _after(dep)

    def body(ds_ref, idx_ref, *rest):
        o_ref, r_ref = rest[-2], rest[-1]
        r_ref[...] = jnp.zeros_like(r_ref)
        ix, ds = idx_ref[...], ds_ref[...]
        for b in buckets:
            r_ref[b:b + 1, :] = jnp.sum(jnp.where(ix == b, ds, 0.0), axis=0, keepdims=True)
        o_ref[...] = jnp.broadcast_to(jnp.sum(r_ref[...], axis=1, keepdims=True), (32, 128))

    out = pl.pallas_call(
        body, name=f"bias_grad{gi}{'ab'[sub]}", grid=(NH,),
        in_specs=[pl.BlockSpec((None, T, Tk), lambda h: (h, 0, 0)), pl.BlockSpec((T, Tk), lambda h: (0, 0))] + dep_specs,
        out_specs=pl.BlockSpec((None, 32, 128), lambda h: (h, 0, 0)),
        out_shape=_sds((NH, 32, 128), F32), scratch_shapes=[pltpu.VMEM((32, Tk), F32)],
        compiler_params=_cp(("parallel",)),
    )(dsum, idx, *dep_args)
    return out[:, :, 0].T


def _prev_mask(n, T, Tk):
    col = lax.broadcasted_iota(jnp.int32, (1, Tk), 1)
    return jnp.where((col < T) & (n == 0), NEG, 0.0).astype(F32)


def _split_rows(v, R):
    A, half = v.shape[0] // R, R // 2
    via_f32 = v.dtype != F32 and half % 16 != 0
    w = v.astype(F32) if via_f32 else v
    lo = jnp.concatenate([w[a * R:a * R + half] for a in range(A)], axis=0)
    hi = jnp.concatenate([w[a * R + half:(a + 1) * R] for a in range(A)], axis=0)
    return (lo.astype(v.dtype), hi.astype(v.dtype)) if via_f32 else (lo, hi)


def _merge_rows(lo, hi, R):
    half = R // 2
    A = lo.shape[0] // half
    return jnp.concatenate([x[a * half:(a + 1) * half] for a in range(A) for x in (lo, hi)], axis=0)


def _attn_fwd(qkv, biases, geom):
    S = qkv.shape[0]
    gi, T, R, has_prev = geom.gi, geom.T, geom.rows, geom.has_prev
    H2 = T // 2
    qv = geom.view(qkv)
    cq, ck, cv = 3 * gi, 3 * gi + 1, 3 * gi + 2

    def body(*refs):
        if has_prev:
            q_ref, kp_ref, kc_ref, vp_ref, vc_ref, ba_ref, bb_ref, o_ref, l_ref = refs
        else:
            q_ref, kc_ref, vc_ref, ba_ref, bb_ref, o_ref, l_ref = refs
        q_lo, q_hi = _split_rows(_ld(q_ref), R)
        kc_lo, kc_hi = _split_rows(_ld(kc_ref), R)
        vc_lo, vc_hi = _split_rows(_ld(vc_ref), R)
        k_b = jnp.concatenate([kc_lo, kc_hi], axis=0)
        v_b = jnp.concatenate([vc_lo, vc_hi], axis=0)
        if has_prev:
            n = pl.program_id(len(geom.outer))
            k_a = jnp.concatenate([_split_rows(_ld(kp_ref), R)[1], kc_lo], axis=0)
            v_a = jnp.concatenate([_split_rows(_ld(vp_ref), R)[1], vc_lo], axis=0)
            pm = _prev_mask(n, H2, T)
        else:
            k_a, v_a, pm = k_b, v_b, None
        low = lax.broadcasted_iota(jnp.int32, (1, 2 * HD), 1) < HD
        for j in range(NH // 2):
            cols = slice(2 * HD * j, 2 * HD * (j + 1))
            o_half, l_half = [], []
            for q, k, v, b_ref, mask in ((q_lo, k_a, v_a, ba_ref, pm), (q_hi, k_b, v_b, bb_ref, None)):
                q2 = q[:, cols].astype(F32) * 0.125
                k2, v2 = k[:, cols], v[:, cols]
                qs = jnp.concatenate([jnp.where(low, q2, 0.0), jnp.where(low, 0.0, q2)], axis=0).astype(q.dtype)
                s = _dot(qs, k2, NT) + b_ref[j]
                if mask is not None:
                    s = s + mask
                m = jnp.max(s, axis=1, keepdims=True)
                p = jnp.exp(s - m)
                l = jnp.sum(p, axis=1, keepdims=True)
                pv = _dot(p.astype(v.dtype), v2, NN) / l
                lse = m + jnp.log(l)
                o_half.append(jnp.where(low, pv[:H2], pv[H2:]))
                l_half.append(jnp.where(low, lse[:H2], lse[H2:]))
            _st(o_ref, _merge_rows(o_half[0], o_half[1], R), cols)
            _st(l_ref, _merge_rows(l_half[0], l_half[1], R), cols)

    cur = lambda n: n
    prev = lambda n: jnp.maximum(n - 1, 0)
    if has_prev:
        in_specs = [geom.spec(AW, cq, cur), geom.spec(AW, ck, prev), geom.spec(AW, ck, cur),
                    geom.spec(AW, cv, prev), geom.spec(AW, cv, cur)]
        args = [qv] * 5
    else:
        in_specs = [geom.spec(AW, cq, cur), geom.spec(AW, ck, cur), geom.spec(AW, cv, cur)]
        args = [qv] * 3
    nd = len(geom.outer) + (1 if has_prev else 0)
    biases = [b.reshape(NH // 2, 2 * H2, b.shape[2]) for b in biases]
    in_specs += [pl.BlockSpec(b.shape, lambda *ids: (0, 0, 0)) for b in biases]
    grid = geom.outer + ((geom.n_tiles,) if has_prev else ())
    oshape = geom.view_shape(AW)
    o, l = pl.pallas_call(
        body, name=f"attn_fwd{gi}", grid=grid, in_specs=in_specs,
        out_specs=[geom.spec(AW, 0, cur), geom.spec(AW, 0, cur)],
        out_shape=[_sds(oshape, F32), _sds(oshape, F32)],
        compiler_params=_cp(("arbitrary",) * nd),
    )(*args, *biases)
    return geom.unview(o), geom.unview(l)


def _attn_combine(os_, ls_, rest):
    S = rest.shape[0]
    tm = 512

    def body(o0, o1, o2, l0, l1, l2, z_ref, attn_ref, L_ref, ag_ref, agt_ref):
        la, lb, lc = l0[...], l1[...], l2[...]
        m = jnp.maximum(jnp.maximum(la, lb), lc)
        ea, eb, ec = jnp.exp(la - m), jnp.exp(lb - m), jnp.exp(lc - m)
        den = ea + eb + ec
        attn = (ea * o0[...] + eb * o1[...] + ec * o2[...]) / den
        attn_ref[...] = attn
        L_ref[...] = m + jnp.log(den)
        z = z_ref[...]
        ag = attn * (z * _sigmoid(z))
        ag_ref[...] = ag.astype(ag_ref.dtype)
        agt_ref[...] = ag.T.astype(agt_ref.dtype)

    t = pl.BlockSpec((tm, AW), lambda i: (i, 0))
    return pl.pallas_call(
        body, name="attn_combine", grid=(S // tm,),
        in_specs=[t] * 6 + [pl.BlockSpec((tm, AW), lambda i: (i, 0))],
        out_specs=[t, t, t, pl.BlockSpec((AW, tm), lambda i: (0, i))],
        out_shape=[_sds((S, AW), F32), _sds((S, AW), F32), _sds((S, AW), CDT), _sds((AW, S), CDT)],
        compiler_params=_cp(("parallel",)),
    )(*os_, *ls_, rest)


def _pool_counts(r, LS, win):
    l = lax.broadcasted_iota(jnp.int32, (LS, 1), 0)
    return jnp.minimum(16 * l + (r + 1), win).astype(F32)


def _pool_fwd(rest, pool_w, pool_scale):
    S = rest.shape[0]
    LS = S // R16
    r3 = rest.reshape(R16, LS, REST_W)
    PG = 128

    def body(u_ref, z_ref, pw_ref, ps_ref, pt_ref, mp_ref, pg_ref, pgt_ref, e_ref):
        g = pl.program_id(0)
        win = jnp.left_shift(2, g)
        row = lax.broadcasted_iota(jnp.int32, (LS, PG), 0)
        for r in range(R16):
            e_ref[R16 + r] = u_ref[r]
        for r in range(1, R16):
            e_ref[r] = jnp.where(row == 0, 0.0, pltpu.roll(u_ref[r], 1, 0))

        def double(s):
            for i in range(2 * R16 - 1, 2 * s - 1, -1):
                e_ref[i] = e_ref[i] + e_ref[i - s]

        double(1)
        for k in (1, 2, 3):
            pl.when(g >= k)(functools.partial(double, 2 ** k))
        pw = pw_ref[...].astype(CDT)
        for r in range(R16):
            u = u_ref[r]
            pooled = e_ref[R16 + r] / _pool_counts(r, LS, win) - u
            pc = pooled.astype(CDT)
            mp = _dot(pc, pw, NN)
            z = z_ref[r]
            pg = (mp * ps_ref[...]) * (z * _sigmoid(z))
            mp_ref[r] = mp
            pg_ref[r] = pg.astype(pg_ref.dtype)
            pt_ref[:, r * LS:(r + 1) * LS] = pooled.T.astype(pt_ref.dtype)
            pgt_ref[:, r * LS:(r + 1) * LS] = pg.T.astype(pgt_ref.dtype)

    col = lambda off: pl.BlockSpec((R16, LS, PG), lambda g: (0, 0, off + g))
    tsp = pl.BlockSpec((PG, S), lambda g: (g, 0))
    pt, mp, pg, pgt = pl.pallas_call(
        body, name="pool_fwd", grid=(4,),
        in_specs=[col(24), col(4), pl.BlockSpec((None, PG, PG), lambda g: (g, 0, 0)),
                  pl.BlockSpec((1, PG), lambda g: (0, g))],
        out_specs=[tsp, col(0), col(0), tsp],
        out_shape=[_sds((AW, S), CDT), _sds((R16, LS, AW), F32), _sds((R16, LS, AW), CDT), _sds((AW, S), CDT)],
        scratch_shapes=[pltpu.VMEM((2 * R16, LS, PG), F32)],
        compiler_params=_cp(("parallel",)),
    )(r3, r3, pool_w, pool_scale)
    return pt, mp.reshape(S, AW), pg.reshape(S, AW), pgt


def _branch_merge(ag, pg, wab, wpb, rest):
    S = ag.shape[0]
    tm, tn = 512, 512

    def body(ag_ref, pg_ref, wa_ref, wp_ref, ga_ref, gp_ref, m_ref, mt_ref):
        ya = _dot(ag_ref[...], wa_ref[...], NN)
        yp = _dot(pg_ref[...], wp_ref[...], NN)
        mg = _sigmoid(ga_ref[...]) * ya + _sigmoid(gp_ref[...]) * yp
        m_ref[...] = mg.astype(m_ref.dtype)
        mt_ref[...] = mg.T.astype(mt_ref.dtype)

    a = pl.BlockSpec((tm, AW), lambda i, j: (i, 0))
    w = pl.BlockSpec((AW, tn), lambda i, j: (0, j))
    o = pl.BlockSpec((tm, tn), lambda i, j: (i, j))
    return pl.pallas_call(
        body, name="branch_merge", grid=(S // tm, D // tn),
        in_specs=[a, a, w, w, pl.BlockSpec((tm, tn), lambda i, j: (i, 2 + j)),
                  pl.BlockSpec((tm, tn), lambda i, j: (i, 4 + j))],
        out_specs=[o, pl.BlockSpec((tn, tm), lambda i, j: (j, i))],
        out_shape=[_sds((S, D), CDT), _sds((D, S), CDT)],
        compiler_params=_cp(("parallel", "parallel")),
    )(ag, pg, wab, wpb, rest, rest)


def _out_loss(merged, wout, x, target, gate, final_g):
    S = x.shape[0]

    def body(m_ref, w_ref, x_ref, t_ref, gate_ref, fg_ref, dx2_ref, dmo_ref, loss_ref, gfg_ref, dgate_ref):
        i = pl.program_id(0)

        @pl.when(i == 0)
        def _():
            loss_ref[...] = jnp.zeros_like(loss_ref)
            gfg_ref[...] = jnp.zeros_like(gfg_ref)
            dgate_ref[...] = jnp.zeros_like(dgate_ref)

        swap = _swap16()
        mo = _dot(_dot(swap, _ld(m_ref), NN).astype(CDT), w_ref[...], NN)
        gate, fg = gate_ref[...], fg_ref[...]
        x2 = x_ref[...] + gate * mo
        r2 = lax.rsqrt(jnp.mean(x2 * x2, axis=-1, keepdims=True) + EPS)
        xn2 = x2 * r2
        diff = xn2 * fg - t_ref[...]
        loss = jnp.sum(jnp.sum(diff * diff, axis=-1, keepdims=True), axis=0, keepdims=True)
        dy = diff * (1.0 / D)
        dxn2 = dy * fg
        dx2 = r2 * (dxn2 - xn2 * jnp.mean(dxn2 * xn2, axis=-1, keepdims=True))
        dx2_ref[...] = dx2
        _st(dmo_ref, _dot(swap, (dx2 * gate).astype(CDT), NN))
        loss_ref[...] += jnp.broadcast_to(loss * (0.5 / D), loss_ref.shape)
        gfg_ref[...] += jnp.sum(dy * xn2, axis=0, keepdims=True)
        dgate_ref[...] += jnp.sum(dx2 * mo, axis=0, keepdims=True)

    slab = pl.BlockSpec((R16, R16, D), lambda i: (0, i, 0))
    nat = pl.BlockSpec((TOK, D), lambda i: (i, 0))
    vec = pl.BlockSpec((1, D), lambda i: (0, 0))
    return pl.pallas_call(
        body, name="out_loss", grid=(S // TOK,),
        in_specs=[slab, _resident((D, D)), nat, nat, vec, vec],
        out_specs=[nat, slab, pl.BlockSpec((1, 128), lambda i: (0, 0)), vec, vec],
        out_shape=[_sds((S, D), F32), _sds((R16, S // R16, D), CDT), _sds((1, 128), F32), _sds((1, D), F32),
                   _sds((1, D), F32)],
        compiler_params=_cp(("arbitrary",)),
    )(merged, wout, x, target, gate, final_g)


def _bwd_gates(dmo, wout, wab, wpb, rest, ag, pg, attn, mp, pool_scale):
    S = dmo.shape[0]
    tm = 256

    def body(dmo_ref, wo_ref, wa_ref, wp_ref, za_ref, zp_ref, ga0, ga1, gp0, gp1, ag_ref, pg_ref, at_ref, mp_ref,
             ps_ref, dya_ref, dyp_ref, dr_ref, dat_ref, dl_ref, dmp_ref, dps_ref):
        i = pl.program_id(0)

        @pl.when(i == 0)
        def _():
            dps_ref[...] = jnp.zeros_like(dps_ref)

        dm = _dot(dmo_ref[...], wo_ref[...], NT)
        sa = _sigmoid(jnp.concatenate([ga0[...], ga1[...]], axis=1))
        sp = _sigmoid(jnp.concatenate([gp0[...], gp1[...]], axis=1))
        dya = (dm * sa).astype(CDT)
        dyp = (dm * sp).astype(CDT)
        dya_ref[...] = dya
        dyp_ref[...] = dyp
        ya = _dot(ag_ref[...], wa_ref[...], NN)
        yp = _dot(pg_ref[...], wp_ref[...], NN)
        dr_ref[:, 1024:2048] = (dm * ya * sa * (1.0 - sa)).astype(dr_ref.dtype)
        dr_ref[:, 2048:3072] = (dm * yp * sp * (1.0 - sp)).astype(dr_ref.dtype)
        da = _dot(dya, wa_ref[...], NT)
        dp = _dot(dyp, wp_ref[...], NT)
        za = za_ref[...]
        sga = _sigmoid(za)
        attn = at_ref[...]
        dattn = da * (za * sga)
        dr_ref[:, 0:512] = (da * attn * (sga * (1.0 + za * (1.0 - sga)))).astype(dr_ref.dtype)
        dat_ref[...] = dattn.astype(dat_ref.dtype)
        prod = dattn * attn
        low = lax.broadcasted_iota(jnp.int32, (1, 2 * HD), 1) < HD
        for j in range(NH // 2):
            x2 = prod[:, 2 * HD * j:2 * HD * (j + 1)]
            s0 = jnp.sum(jnp.where(low, x2, 0.0), axis=1, keepdims=True)
            s1 = jnp.sum(jnp.where(low, 0.0, x2), axis=1, keepdims=True)
            dl_ref[:, 2 * HD * j:2 * HD * (j + 1)] = jnp.where(low, s0, s1)
        zp = zp_ref[...]
        sgp = _sigmoid(zp)
        mpre = mp_ref[...]
        ps = ps_ref[...]
        dmixed = dp * (zp * sgp)
        dr_ref[:, 512:1024] = (dp * (mpre * ps) * (sgp * (1.0 + zp * (1.0 - sgp)))).astype(dr_ref.dtype)
        dps_ref[...] += jnp.sum(dmixed * mpre, axis=0, keepdims=True)
        dmp_ref[...] = (dmixed * ps).astype(dmp_ref.dtype)

    full = lambda shape: pl.BlockSpec(shape, lambda i: (0, 0))
    tD = pl.BlockSpec((tm, D), lambda i: (i, 0))
    tA = pl.BlockSpec((tm, AW), lambda i: (i, 0))
    rb = lambda c: pl.BlockSpec((tm, AW), lambda i: (i, c))
    return pl.pallas_call(
        body, name="bwd_gates", grid=(S // tm,),
        in_specs=[tD, full((D, D)), full((AW, D)), full((AW, D)), rb(0), rb(1), rb(2), rb(3), rb(4), rb(5),
                  tA, tA, tA, tA, full((1, AW))],
        out_specs=[tD, tD, pl.BlockSpec((tm, 3072), lambda i: (i, 0)), tA, tA, tA, full((1, AW))],
        out_shape=[_sds((S, D), CDT), _sds((S, D), CDT), _sds((S, REST_W), CDT), _sds((S, AW), CDT),
                   _sds((S, AW), F32), _sds((S, AW), CDT), _sds((1, AW), F32)],
        compiler_params=_cp(("arbitrary",), vmem=56),
    )(dmo, wout, wab, wpb, rest, rest, rest, rest, rest, rest, ag, pg, attn, mp, pool_scale)


def _pool_bwd(dmp, pt, pool_w, drest):
    S = dmp.shape[0]
    LS = S // R16
    PG = 128
    d3 = dmp.reshape(R16, LS, AW)
    dr3 = drest.reshape(R16, LS, REST_W)

    def body(d_ref, pt_ref, pw_ref, dr_in, du_ref, dpw_ref, g_ref, s_ref):
        del dr_in
        g = pl.program_id(0)
        win = jnp.left_shift(2, g)
        row = lax.broadcasted_iota(jnp.int32, (LS, PG), 0)
        pw = pw_ref[...].astype(CDT)
        dpw = jnp.zeros((PG, PG), F32)
        for r in range(R16):
            dm = d_ref[r]
            dpw = dpw + _dot(pt_ref[:, r * LS:(r + 1) * LS], dm, NN)
            dpooled = _dot(dm, pw, NT)
            g_ref[r] = dpooled
            sc = dpooled / _pool_counts(r, LS, win)
            s_ref[r] = sc
            if r < R16 - 1:
                s_ref[R16 + r] = jnp.where(row == LS - 1, 0.0, pltpu.roll(sc, LS - 1, 0))
        dpw_ref[...] = dpw

        def double(s):
            for i in range(0, 2 * R16 - 2 * s):
                s_ref[i] = s_ref[i] + s_ref[i + s]

        double(1)
        for k in (1, 2, 3):
            pl.when(g >= k)(functools.partial(double, 2 ** k))
        for r in range(R16):
            du_ref[r] = (s_ref[r] - g_ref[r]).astype(du_ref.dtype)

    du, dpw = pl.pallas_call(
        body, name="pool_bwd", grid=(4,),
        in_specs=[pl.BlockSpec((R16, LS, PG), lambda g: (0, 0, g)), pl.BlockSpec((PG, S), lambda g: (g, 0)),
                  pl.BlockSpec((None, PG, PG), lambda g: (g, 0, 0)), pl.BlockSpec(memory_space=pl.ANY)],
        out_specs=[pl.BlockSpec((R16, LS, PG), lambda g: (0, 0, 24 + g)),
                   pl.BlockSpec((None, PG, PG), lambda g: (g, 0, 0))],
        out_shape=[_sds((R16, LS, REST_W), CDT), _sds((4, PG, PG), F32)],
        scratch_shapes=[pltpu.VMEM((R16, LS, PG), F32), pltpu.VMEM((2 * R16, LS, PG), F32)],
        input_output_aliases={3: 0},
        compiler_params=_cp(("parallel",)),
    )(d3, pt, pool_w, dr3)
    return du.reshape(S, REST_W), dpw


def _attn_bwd(qkv, dattn, Lb, dlb, biases, geom, dqkv_prev):
    S = qkv.shape[0]
    gi, T, R, has_prev, N = geom.gi, geom.T, geom.rows, geom.has_prev, geom.n_tiles
    H2 = T // 2
    nd_outer = len(geom.outer)
    qv = geom.view(qkv)
    cq, ck, cv = 3 * gi, 3 * gi + 1, 3 * gi + 2
    W3 = 3 * AW

    def heads(q_ref, kp_ref, kc_ref, vp_ref, vc_ref, do_ref, L_ref, dl_ref, ba_ref, bb_ref, pm, dsa_ref, dsb_ref, emit):
        q_lo, q_hi = _split_rows(_ld(q_ref), R)
        do_lo, do_hi = _split_rows(_ld(do_ref), R)
        L_lo, L_hi = _split_rows(_ld(L_ref), R)
        dl_lo, dl_hi = _split_rows(_ld(dl_ref), R)
        kc_lo, kc_hi = _split_rows(_ld(kc_ref), R)
        vc_lo, vc_hi = _split_rows(_ld(vc_ref), R)
        k_b = jnp.concatenate([kc_lo, kc_hi], axis=0)
        v_b = jnp.concatenate([vc_lo, vc_hi], axis=0)
        if has_prev:
            k_a = jnp.concatenate([_split_rows(_ld(kp_ref), R)[1], kc_lo], axis=0)
            v_a = jnp.concatenate([_split_rows(_ld(vp_ref), R)[1], vc_lo], axis=0)
        else:
            k_a, v_a = k_b, v_b
        low = lax.broadcasted_iota(jnp.int32, (1, 2 * HD), 1) < HD
        for j in range(NH // 2):
            sl = slice(2 * HD * j, 2 * HD * (j + 1))
            res = []
            for q, k, v, do, L, dl, b_ref, mask, ds_ref in (
                    (q_lo, k_a, v_a, do_lo, L_lo, dl_lo, ba_ref, pm, dsa_ref),
                    (q_hi, k_b, v_b, do_hi, L_hi, dl_hi, bb_ref, None, dsb_ref)):
                q2 = q[:, sl].astype(F32) * 0.125
                do2 = do[:, sl].astype(F32)
                k2, v2 = k[:, sl], v[:, sl]
                L2, dl2 = L[:, sl], dl[:, sl]
                qs = jnp.concatenate([jnp.where(low, q2, 0.0), jnp.where(low, 0.0, q2)], axis=0).astype(q.dtype)
                dom = jnp.concatenate([jnp.where(low, do2, 0.0), jnp.where(low, 0.0, do2)], axis=0).astype(q.dtype)
                Lc = jnp.concatenate([L2[:, 0:1], L2[:, HD:HD + 1]], axis=0)
                dlc = jnp.concatenate([dl2[:, 0:1], dl2[:, HD:HD + 1]], axis=0)
                s = _dot(qs, k2, NT) + b_ref[j]
                if mask is not None:
                    s = s + mask
                p = jnp.exp(s - Lc)
                ds = p * (_dot(dom, v2, NT) - dlc)
                ds_ref[j] += ds
                dsc = ds.astype(q.dtype)
                dq_st = _dot(dsc, k2, NN)
                res.append((jnp.where(low, dq_st[:H2], dq_st[H2:]) * 0.125, _dot(dsc, qs, TN),
                            _dot(p.astype(q.dtype), dom, TN)))
            (dq_lo, dk_a, dv_a), (dq_hi, dk_b, dv_b) = res
            dq = _merge_rows(dq_lo, dq_hi, R)
            if has_prev:
                emit(sl, dq, _merge_rows(dk_a[H2:] + dk_b[:H2], dk_b[H2:], R),
                     _merge_rows(dv_a[H2:] + dv_b[:H2], dv_b[H2:], R), dk_a[:H2], dv_a[:H2])
            else:
                emit(sl, dq, _merge_rows(dk_a[:H2] + dk_b[:H2], dk_a[H2:] + dk_b[H2:], R),
                     _merge_rows(dv_a[:H2] + dv_b[:H2], dv_a[H2:] + dv_b[H2:], R), None, None)

    def first_step():
        ok = pl.program_id(nd_outer) == 0 if has_prev else pl.program_id(0) == 0
        for a in range(nd_outer if has_prev else 0):
            ok = ok & (pl.program_id(a) == 0)
        return ok

    if has_prev:
        def body(q_ref, kp_ref, kc_ref, vp_ref, vc_ref, do_ref, L_ref, dl_ref, ba_ref, bb_ref, prev_in, out_ref,
                 dsa_ref, dsb_ref, hold, new, pp):
            del prev_in
            s_id = pl.program_id(nd_outer)

            @pl.when(first_step())
            def _():
                dsa_ref[...] = jnp.zeros_like(dsa_ref)
                dsb_ref[...] = jnp.zeros_like(dsb_ref)

            @pl.when(s_id < N)
            def _():
                def emit(sl, dq, dk, dv, dk_prev, dv_prev):
                    new[:, sl] = dq
                    new[:, slice(AW + sl.start, AW + sl.stop)] = dk
                    new[:, slice(2 * AW + sl.start, 2 * AW + sl.stop)] = dv
                    pp[:, sl] = dk_prev
                    pp[:, slice(AW + sl.start, AW + sl.stop)] = dv_prev

                heads(q_ref, kp_ref, kc_ref, vp_ref, vc_ref, do_ref, L_ref, dl_ref, ba_ref, bb_ref,
                      _prev_mask(s_id, H2, T), dsa_ref, dsb_ref, emit)

            @pl.when((s_id >= 1) & (s_id < N))
            def _():
                _st(out_ref, hold[:, 0:AW], slice(0, AW))
                late = _merge_rows(jnp.zeros((H2, 2 * AW), F32), pp[...], R)
                _st(out_ref, hold[:, AW:W3] + late, slice(AW, W3))

            @pl.when(s_id == N)
            def _():
                _st(out_ref, hold[...])

            @pl.when(s_id < N)
            def _():
                hold[...] = new[...]

        cur = lambda n: jnp.minimum(n, N - 1)
        prev = lambda n: jnp.clip(n - 1, 0, N - 1)
        delayed = lambda n: jnp.maximum(n - 1, 0)
        in_specs = [geom.spec(AW, cq, cur), geom.spec(AW, ck, prev), geom.spec(AW, ck, cur),
                    geom.spec(AW, cv, prev), geom.spec(AW, cv, cur),
                    geom.spec(AW, 0, cur), geom.spec(AW, 0, cur), geom.spec(AW, 0, cur)]
        args = [qv] * 5 + [geom.view(dattn), geom.view(Lb), geom.view(dlb)]
        out_spec = geom.spec(W3, gi, delayed)
        grid = geom.outer + (N + 1,)
        scratch = [pltpu.VMEM((T, W3), F32), pltpu.VMEM((T, W3), F32), pltpu.VMEM((H2, 2 * AW), F32)]
    else:
        def body(q_ref, kc_ref, vc_ref, do_ref, L_ref, dl_ref, ba_ref, bb_ref, prev_in, out_ref, dsa_ref, dsb_ref):
            del prev_in

            @pl.when(first_step())
            def _():
                dsa_ref[...] = jnp.zeros_like(dsa_ref)
                dsb_ref[...] = jnp.zeros_like(dsb_ref)

            def emit(sl, dq, dk, dv, dk_prev, dv_prev):
                _st(out_ref, dq, sl)
                _st(out_ref, dk, slice(AW + sl.start, AW + sl.stop))
                _st(out_ref, dv, slice(2 * AW + sl.start, 2 * AW + sl.stop))

            heads(q_ref, None, kc_ref, None, vc_ref, do_ref, L_ref, dl_ref, ba_ref, bb_ref, None, dsa_ref, dsb_ref,
                  emit)

        cur = lambda n: n
        in_specs = [geom.spec(AW, cq, cur), geom.spec(AW, ck, cur), geom.spec(AW, cv, cur),
                    geom.spec(AW, 0, cur), geom.spec(AW, 0, cur), geom.spec(AW, 0, cur)]
        args = [qv] * 3 + [geom.view(dattn), geom.view(Lb), geom.view(dlb)]
        out_spec = geom.spec(W3, gi, cur)
        grid = geom.outer
        scratch = []

    whole = lambda b: pl.BlockSpec(b.shape, lambda *ids: (0, 0, 0))
    biases = [b.reshape(NH // 2, 2 * H2, b.shape[2]) for b in biases]
    in_specs += [whole(b) for b in biases]
    in_specs.append(pl.BlockSpec(memory_space=pl.ANY))
    if dqkv_prev is None:
        last, aliases = biases[0], {}
    else:
        last, aliases = geom.view(dqkv_prev), {len(in_specs) - 1: 0}
    dq, dsa, dsb = pl.pallas_call(
        body, name=f"attn_bwd{gi}", grid=grid, in_specs=in_specs,
        out_specs=[out_spec] + [whole(b) for b in biases],
        out_shape=[_sds(geom.view_shape(QKV_W), CDT)] + [_sds(b.shape, F32) for b in biases],
        scratch_shapes=scratch,
        input_output_aliases=aliases,
        compiler_params=_cp(("arbitrary",) * len(grid), vmem=56),
    )(*args, *biases, last)
    return geom.unview(dq), [d.reshape(NH, H2, d.shape[2]) for d in (dsa, dsb)]


def _h_bwd(dh, dx2, x, norm_g, scale):
    S = x.shape[0]

    def body(dh_ref, dx2_ref, x_ref, g_ref, sc_ref, gx_ref, dsh_ref, dsc_ref, dng_ref, nat_ref):
        i = pl.program_id(0)

        @pl.when(i == 0)
        def _():
            dsh_ref[...] = jnp.zeros_like(dsh_ref)
            dsc_ref[...] = jnp.zeros_like(dsc_ref)
            dng_ref[...] = jnp.zeros_like(dng_ref)

        for r in range(R16):
            nat_ref[:, r, :] = dh_ref[r]
        dh = jnp.concatenate([nat_ref[l] for l in range(R16)], axis=0)
        g = g_ref[...]
        one_sc = 1.0 + sc_ref[...]
        xv = x_ref[...]
        r1 = lax.rsqrt(jnp.mean(xv * xv, axis=-1, keepdims=True) + EPS)
        xn = xv * r1
        dhx = dh * xn
        dxn = dh * (g * one_sc)
        dx1 = r1 * (dxn - xn * jnp.mean(dxn * xn, axis=-1, keepdims=True))
        gx_ref[...] = dx2_ref[...] + dx1
        dsh_ref[...] += jnp.sum(dh, axis=0, keepdims=True)
        dsc_ref[...] += jnp.sum(dhx * g, axis=0, keepdims=True)
        dng_ref[...] += jnp.sum(dhx * one_sc, axis=0, keepdims=True)

    nat = pl.BlockSpec((TOK, D), lambda i: (i, 0))
    vec = pl.BlockSpec((1, D), lambda i: (0, 0))
    return pl.pallas_call(
        body, name="h_bwd", grid=(S // TOK,),
        in_specs=[pl.BlockSpec((R16, R16, D), lambda i: (0, i, 0)), nat, nat, vec, vec],
        out_specs=[nat, vec, vec, vec],
        out_shape=[_sds((S, D), F32), _sds((1, D), F32), _sds((1, D), F32), _sds((1, D), F32)],
        scratch_shapes=[pltpu.VMEM((R16, R16, D), F32)],
        compiler_params=_cp(("arbitrary",)),
    )(dh, dx2, x, norm_g, scale)


def _local_step(x, target, shift, scale, gate, norm_g, w_in_full, pool_w, pool_scale, wab, wpb, wout, rel_bias,
                final_g):
    st = _local_pre(x, shift, scale, norm_g, rel_bias)
    st = _local_grads(st, target, gate, w_in_full, pool_w, pool_scale, wab, wpb, wout, final_g)
    d_rel_bias = _local_bias_grads(st, _BIAS_GRADS_EARLY + _BIAS_GRADS_LATE)
    gx, dmod, dng = _local_input_grads(st, w_in_full, norm_g, scale)
    return dict(loss=st["loss"], grad_x=gx, dw_in=st["dw_in"], dw_ab=st["dw_ab"], dw_pb=st["dw_pb"],
                dw_out=st["dw_out"], d_norm_g=dng, d_pool_w=st["dpw"], d_pool_scale=st["dps"],
                d_rel_bias=d_rel_bias, d_final_g=st["gfg"], dmod=dmod)


def _local_pre(x, shift, scale, norm_g, rel_bias, dep=None):
    S = x.shape[0]
    LS = S // R16
    geoms = [_Geom(gi, LS) for gi in range(3)]
    idx_np = [[g.bucket_index(sub) for sub in range(2)] for g in geoms]
    buckets = [[tuple(int(b) for b in np.unique(ix) if b >= 0) for ix in pair] for pair in idx_np]
    idxs = [[jnp.asarray(ix) for ix in pair] for pair in idx_np]
    biases = [[_bias_table(idxs[gi][sub], buckets[gi][sub], rel_bias, gi, sub, dep) for sub in range(2)]
              for gi in range(3)]
    h = _norm_mod(x, norm_g, shift, scale, dep).reshape(S, D)
    return dict(geoms=geoms, idxs=idxs, buckets=buckets, biases=biases, x=x, h=h, ht=_tcast(h, "h_transpose"))


def _local_grads(st, target, gate, w_in_full, pool_w, pool_scale, wab, wpb, wout, final_g):
    geoms, idxs, biases, x, h, ht = (st[k] for k in ("geoms", "idxs", "biases", "x", "h", "ht"))
    S = h.shape[0]
    LS = S // R16
    qkv, rest = _proj(h, w_in_full)

    os_, ls_ = [], []
    for gi in range(3):
        o, l = _attn_fwd(qkv, biases[gi], geoms[gi])
        os_.append(o)
        ls_.append(l)
    attn, Lb, ag, agt = _attn_combine(os_, ls_, rest)
    pt, mp, pg, pgt = _pool_fwd(rest, pool_w, pool_scale)
    merged, mt = _branch_merge(ag, pg, wab, wpb, rest)
    dx2, dmo3, loss, gfg, dgate = _out_loss(merged.reshape(R16, LS, D), wout, x, target, gate, final_g)
    dmo = dmo3.reshape(S, D)

    dya, dyp, drest, dattn, dlb, dmp, dps = _bwd_gates(dmo, wout, wab, wpb, rest, ag, pg, attn, mp, pool_scale)
    dw_out = _mm_tn(mt, dmo, n_total=D, out_blk=lambda j: j, name="dw_out")
    dw_ab = _mm_tn(agt, dya, n_total=D, out_blk=lambda j: j, name="dw_attn_br")
    dw_pb = _mm_tn(pgt, dyp, n_total=D, out_blk=lambda j: j, name="dw_pool_br")
    drest, dpw = _pool_bwd(dmp, pt, pool_w, drest)

    dqkv = None
    dsums = []
    for gi in range(3):
        dqkv, dsum = _attn_bwd(qkv, dattn, Lb, dlb, biases[gi], geoms[gi], dqkv)
        dsums.append(dsum)

    dw_in = _mm_tn(ht, dqkv, n_total=8192, out_blk=lambda j: j, name="dw_in_qkv", with_low=True)
    dw_in, dw_in_low = _mm_tn(ht, drest, n_total=8192, out_blk=_rest_blk, name="dw_in_rest", prev=dw_in, with_low=True)
    return dict(st, loss=loss[0, 0], dw_in=dw_in, dw_in_low=dw_in_low, dw_ab=dw_ab, dw_pb=dw_pb, dw_out=dw_out, dpw=dpw, dps=dps, gfg=gfg,
                dgate=dgate, dsums=dsums, dqkv=dqkv, drest=drest, dx2=dx2)


def _local_bias_grads(st, which, dep=None):
    cols = []
    for gi in range(3):
        got = [_bias_grad(st["dsums"][gi][sub], st["idxs"][gi][sub], st["buckets"][gi][sub], gi, sub, dep)
               for sub in range(2) if (gi, sub) in which]
        cols.append(sum(got) if got else jnp.zeros((32, NH), F32))
    return jnp.concatenate(cols, axis=1)


_BIAS_GRADS_EARLY = ((0, 0), (0, 1), (1, 0), (2, 0))
_BIAS_GRADS_LATE = ((1, 1), (2, 1))


def _dh_h_bwd(dqkv, drest, w, dx2, x, norm_g, scale, dep=None):
    S = x.shape[0]
    dep_specs, dep_args = _after(dep)

    def body(a_ref, b_ref, w_ref, dx2_ref, x_ref, g_ref, sc_ref, *rest):
        gx_ref, dsh_ref, dsc_ref, dng_ref, nat_ref = rest[-5:]
        i = pl.program_id(0)

        @pl.when(i == 0)
        def _():
            dsh_ref[...] = jnp.zeros_like(dsh_ref)
            dsc_ref[...] = jnp.zeros_like(dsc_ref)
            dng_ref[...] = jnp.zeros_like(dng_ref)

        a, b = _ld(a_ref), _ld(b_ref)
        acc = _dot(a[:, 0:512], _w512(w_ref, 0), NT)
        for j in range(1, 9):
            acc = acc + _dot(a[:, 512 * j:512 * (j + 1)], _w512(w_ref, j), NT)
        off = 0
        for lo, hi in REST_COLS:
            for j in range(lo, hi):
                acc = acc + _dot(b[:, off:off + 512], _w512(w_ref, j), NT)
                off += 512
        for r in range(R16):
            nat_ref[:, r, :] = acc[R16 * r:R16 * (r + 1)]
        dh = jnp.concatenate([nat_ref[l] for l in range(R16)], axis=0)
        g = g_ref[...]
        one_sc = 1.0 + sc_ref[...]
        xv = x_ref[...]
        r1 = lax.rsqrt(jnp.mean(xv * xv, axis=-1, keepdims=True) + EPS)
        xn = xv * r1
        dhx = dh * xn
        dxn = dh * (g * one_sc)
        dx1 = r1 * (dxn - xn * jnp.mean(dxn * xn, axis=-1, keepdims=True))
        gx_ref[...] = dx2_ref[...] + dx1
        dsh_ref[...] += jnp.sum(dh, axis=0, keepdims=True)
        dsc_ref[...] += jnp.sum(dhx * g, axis=0, keepdims=True)
        dng_ref[...] += jnp.sum(dhx * one_sc, axis=0, keepdims=True)

    nat = pl.BlockSpec((TOK, D), lambda i: (i, 0))
    vec = pl.BlockSpec((1, D), lambda i: (0, 0))
    slab = lambda W: pl.BlockSpec((R16, R16, W), lambda i: (0, i, 0))
    return pl.pallas_call(
        body, name="dh_h_bwd", grid=(S // TOK,),
        in_specs=[slab(QKV_W), slab(REST_W), _resident((8, D, 1024)), nat, nat, vec, vec] + dep_specs,
        out_specs=[nat, vec, vec, vec],
        out_shape=[_sds((S, D), F32), _sds((1, D), F32), _sds((1, D), F32), _sds((1, D), F32)],
        scratch_shapes=[pltpu.VMEM((R16, R16, D), F32)],
        compiler_params=_cp(("arbitrary",), vmem=56),
    )(dqkv, drest, w, dx2, x, norm_g, scale, *dep_args)


def _local_input_grads(st, w_in_full, norm_g, scale, dep=None):
    S = st["h"].shape[0]
    LS = S // R16
    gx, dsh, dsc, dng = _dh_h_bwd(st["dqkv"].reshape(R16, LS, QKV_W), st["drest"].reshape(R16, LS, REST_W),
                                  w_in_full, st["dx2"], st["x"], norm_g, scale, dep)
    return gx, jnp.concatenate([dsh, dsc, st["dgate"]], axis=1), dng


def _my_place():
    return lax.axis_index("x"), lax.axis_index("y"), lax.axis_index("c")


def _block_of(ref, axis, idx, width):
    if axis is None:
        return ref.at[idx]
    start = pl.multiple_of(idx * width, width)
    if axis == 0:
        return ref.at[pl.ds(start, width)]
    return ref.at[:, pl.ds(start, width)]


_SEM = pl.BlockSpec(memory_space=pltpu.SEMAPHORE)
_EFFECT = pltpu.SideEffectType.DATAFLOW_SIDE_EFFECTING


def _split_start(name, srcs, lands, ncopy, plan, dep=None):
    arrs = list(srcs) + list(lands)
    na, ns = len(arrs), len(srcs)
    dep_specs, dep_args = _after(dep)

    def body(*refs):
        send_sems, recv_sems, token = refs[na + len(dep_args)], refs[na + len(dep_args) + 1], refs[-1]
        for k, (s, d, dev) in enumerate(plan(refs[:ns], refs[ns:na])):
            pltpu.make_async_remote_copy(src_ref=s, dst_ref=d, send_sem=send_sems.at[k], recv_sem=recv_sems.at[k],
                                         device_id=dev, device_id_type=MESH).start()
        token[...] = jnp.zeros_like(token)

    outs = pl.pallas_call(
        body, name=name,
        out_shape=(pltpu.SemaphoreType.DMA((ncopy,)), pltpu.SemaphoreType.DMA((ncopy,)),
                   *[pltpu.HBM(a.shape, a.dtype) for a in arrs], _sds((8, 128), F32)),
        in_specs=[_HBM] * na + dep_specs,
        out_specs=(_SEM, _SEM, *[_HBM] * na, pl.BlockSpec(memory_space=pltpu.VMEM)),
        input_output_aliases={i: 2 + i for i in range(na)},
        compiler_params=pltpu.CompilerParams(has_side_effects=_EFFECT),
    )(*[pltpu.with_memory_space_constraint(a, pltpu.HBM) for a in arrs], *dep_args)
    return (outs[0], outs[1]), list(outs[2:2 + ns]), list(outs[2 + ns:2 + na]), outs[-1]


def _split_wait(name, sems, srcs, lands, plan, after):
    arrs = list(srcs) + list(lands)
    na, ns = len(arrs), len(srcs)
    after = list(after) if isinstance(after, (list, tuple)) else [after]

    def body(*refs):
        send_sems, recv_sems = refs[na], refs[na + 1]
        for k, (s, d, dev) in enumerate(plan(refs[:ns], refs[ns:na])):
            cp = pltpu.make_async_remote_copy(src_ref=s, dst_ref=d, send_sem=send_sems.at[k], recv_sem=recv_sems.at[k],
                                              device_id=dev, device_id_type=MESH)
            cp.wait_send()
            cp.wait_recv()

    outs = pl.pallas_call(
        body, name=name, out_shape=[pltpu.HBM(a.shape, a.dtype) for a in arrs],
        in_specs=[_HBM] * na + [_SEM, _SEM] + [pl.BlockSpec(memory_space=pl.ANY)] * len(after),
        out_specs=[_HBM] * na,
        input_output_aliases={i: i for i in range(na)},
        compiler_params=pltpu.CompilerParams(has_side_effects=_EFFECT),
    )(*arrs, sems[0], sems[1], *after)
    return list(outs[:ns]), list(outs[ns:])


def _gather_chips_plan(layout):
    def plan(src, land):
        x, y, c = _my_place()
        me = 4 * x + 2 * y + c
        mine = [_block_of(land[p], layout[p][0], me, layout[p][1]) for p in range(len(land))]
        return [(mine[p], mine[p], (cx, cy, c)) for p in range(len(land)) for cx, cy in [(1 - x, y), (x, 1 - y)]]
    return plan


def _pair_plan(layout):
    def plan(src, land):
        x, y, c = _my_place()
        n = len(layout)
        return [(_block_of(src[p], layout[p][0], 2 * q + (1 - c), layout[p][1]), land[p].at[q], (x, y, 1 - c))
                for p in range(n) for q in range(4)] + [(src[p], land[p], (x, y, 1 - c)) for p in range(n, len(src))]
    return plan


def _chips_plan(n):
    def plan(src, land):
        x, y, c = _my_place()
        chips = [(1 - x, y), (x, 1 - y), (1 - x, 1 - y)]
        return [(src[p].at[2 * cx + cy], land[p].at[2 * x + y], (cx, cy, c)) for p in range(n) for cx, cy in chips] + \
               [(src[p], land[p].at[2 * x + y], (cx, cy, c)) for p in range(n, len(src)) for cx, cy in chips]
    return plan


def _pair_share(gathered):
    n = len(gathered)

    def body(*refs):
        land = refs[n:2 * n]
        send_sems, recv_sems, fwd_send, fwd_recv = refs[2 * n:]
        x, y, c = _my_place()
        k_dg = 3 - (2 * x + y)
        south = c == 0
        pass_on = 4 * jnp.where(south, 1 - x, x) + 2 * jnp.where(south, y, 1 - y) + c
        to = (jnp.where(south, x, 1 - x), jnp.where(south, 1 - y, y), c)

        def forward(p):
            return pltpu.make_async_remote_copy(
                src_ref=land[p].at[pass_on], dst_ref=land[p].at[pass_on], send_sem=fwd_send.at[p],
                recv_sem=fwd_recv.at[p], device_id=to, device_id_type=MESH)

        def to_sibling(p, k):
            blk = land[p].at[2 * k + c]
            return pltpu.make_async_remote_copy(
                src_ref=blk, dst_ref=blk, send_sem=send_sems.at[4 * p + k], recv_sem=recv_sems.at[4 * p + k],
                device_id=(x, y, 1 - c), device_id_type=MESH)

        for p in range(n):
            forward(p).start()
        for p in range(n):
            for k in range(4):
                @pl.when(k != k_dg)
                def _():
                    to_sibling(p, k).start()
        for p in range(n):
            dg = land[p].at[2 * k_dg + c]
            pltpu.make_async_remote_copy(src_ref=dg, dst_ref=dg, send_sem=fwd_send.at[p], recv_sem=fwd_recv.at[p],
                                         device_id=to, device_id_type=MESH).wait_recv()
            for k in range(4):
                @pl.when(k == k_dg)
                def _():
                    to_sibling(p, k).start()
        for p in range(n):
            forward(p).wait_send()
            for k in range(4):
                to_sibling(p, k).wait_send()
                blk = land[p].at[2 * k + 1 - c]
                pltpu.make_async_remote_copy(
                    src_ref=blk, dst_ref=blk, send_sem=send_sems.at[4 * p + k], recv_sem=recv_sems.at[4 * p + k],
                    device_id=(x, y, 1 - c), device_id_type=MESH).wait_recv()

    return pl.pallas_call(
        body, name="gather_pair_share", in_specs=[_HBM] * n, out_specs=[_HBM] * n,
        out_shape=[_sds(g.shape, g.dtype) for g in gathered],
        input_output_aliases={p: p for p in range(n)},
        scratch_shapes=[pltpu.SemaphoreType.DMA((4 * n,)), pltpu.SemaphoreType.DMA((4 * n,)),
                        pltpu.SemaphoreType.DMA((n,)), pltpu.SemaphoreType.DMA((n,))],
    )(*gathered)


def _place_slot(arrs, slot, n_slots, name, out_dtype=None):
    n = len(arrs)

    def body(s_ref, *refs):
        for p in range(n):
            refs[n + p][...] = refs[p][...].astype(refs[n + p].dtype)

    def spec(a):
        tail = a.shape[-2:]
        return pl.BlockSpec((None,) + tail, lambda i, s_ref: (s_ref[0], 0, 0))

    in_specs = [spec(a) if a.ndim == 3 else pl.BlockSpec(a.shape, lambda i, s_ref: (0, 0)) for a in arrs]
    return pl.pallas_call(
        body, name=name,
        grid_spec=pltpu.PrefetchScalarGridSpec(num_scalar_prefetch=1, grid=(1,), in_specs=in_specs,
                                               out_specs=[spec(a) for a in arrs]),
        out_shape=[_sds((n_slots,) + a.shape[-2:], out_dtype or a.dtype) for a in arrs],
        compiler_params=_cp(("arbitrary",)),
    )(slot, *arrs)


def _cols_from_blocks(w8, name):
    _, R, C = w8.shape

    def body(w_ref, o_ref):
        for e in range(8):
            o_ref[:, C * e:C * (e + 1)] = w_ref[e]

    return pl.pallas_call(body, name=name, out_shape=_sds((R, 8 * C), w8.dtype), compiler_params=_cp())(w8)


def _all_gather_small(srcs, name):
    n = len(srcs)
    vmem = pl.BlockSpec(memory_space=pltpu.VMEM)

    def body(*refs):
        src, out = refs[:n], refs[n:2 * n]
        send_sems, recv_sems = refs[2 * n:]
        x, y, c = _my_place()
        me = 4 * x + 2 * y + c
        flip = lambda v, f: 1 - v if f else v
        peers = [(flip(x, fx), flip(y, fy), flip(c, fc)) for fx in (0, 1) for fy in (0, 1) for fc in (0, 1)][1:]
        cps = []
        for p in range(n):
            out[p][me] = src[p][...]
            for k, peer in enumerate(peers):
                cps.append(pltpu.make_async_remote_copy(
                    src_ref=src[p], dst_ref=out[p].at[me], send_sem=send_sems.at[7 * p + k],
                    recv_sem=recv_sems.at[7 * p + k], device_id=peer, device_id_type=MESH))
        for cp in cps:
            cp.start()
        for cp in cps:
            cp.wait()

    return pl.pallas_call(
        body, name=name, in_specs=[vmem] * n, out_specs=[vmem] * n,
        out_shape=[_sds((8,) + a.shape, a.dtype) for a in srcs],
        scratch_shapes=[pltpu.SemaphoreType.DMA((7 * n,)), pltpu.SemaphoreType.DMA((7 * n,))],
    )(*srcs)


def _adaln_exchange(c, w_ada, b_part):
    vmem = pl.BlockSpec(memory_space=pltpu.VMEM)

    def body(c_ref, w_ref, b_ref, call_ref, mod_ref, part_ref, send_sems, recv_sems):
        x, y, c = _my_place()
        me = 4 * x + 2 * y + c
        flip = lambda v, f: 1 - v if f else v
        peers = [(flip(x, fx), flip(y, fy), flip(c, fc)) for fx in (0, 1) for fy in (0, 1) for fc in (0, 1)][1:]
        call_ref[me] = c_ref[...]
        first = [pltpu.make_async_remote_copy(src_ref=c_ref, dst_ref=call_ref.at[me], send_sem=send_sems.at[k],
                                              recv_sem=recv_sems.at[k], device_id=peer, device_id_type=MESH)
                 for k, peer in enumerate(peers)]
        for cp in first:
            cp.start()
        for cp in first:
            cp.wait()
        c_all = jnp.concatenate([call_ref[j] for j in range(8)], axis=0)
        part_ref[...] = _dot(c_all.astype(CDT), w_ref[...].astype(CDT), NN) + b_ref[...]
        mod_ref[me] = part_ref[pl.ds(me, 1), :]
        second = [pltpu.make_async_remote_copy(
            src_ref=part_ref.at[pl.ds(4 * px + 2 * py + pc, 1)], dst_ref=mod_ref.at[me], send_sem=send_sems.at[7 + k],
            recv_sem=recv_sems.at[7 + k], device_id=(px, py, pc), device_id_type=MESH)
            for k, (px, py, pc) in enumerate(peers)]
        for cp in second:
            cp.start()
        for cp in second:
            cp.wait()

    return pl.pallas_call(
        body, name="adaln_exchange", in_specs=[vmem] * 3, out_specs=[vmem] * 2,
        out_shape=[_sds((8, 1, D), F32), _sds((8, 1, w_ada.shape[1]), F32)],
        scratch_shapes=[pltpu.VMEM((8, w_ada.shape[1]), F32), pltpu.SemaphoreType.DMA((14,)),
                        pltpu.SemaphoreType.DMA((14,))],
        compiler_params=_cp(),
    )(c, w_ada, b_part)


def _pair_add(dw, recv, axis, width, cidx, name):
    if axis == 0:
        tr = min(width, 256)
        C = dw.shape[1]
        grid = (4, width // tr)
        nb = width // tr
        own = pl.BlockSpec((tr, C), lambda q, i, c_ref: ((2 * q + c_ref[0]) * nb + i, 0))
        oth = pl.BlockSpec((None, tr, C), lambda q, i, c_ref: (q, i, 0))
    else:
        R = dw.shape[0]
        tr = min(R, 1024)
        grid = (4, R // tr)
        own = pl.BlockSpec((tr, width), lambda q, i, c_ref: (i, 2 * q + c_ref[0]))
        oth = pl.BlockSpec((None, tr, width), lambda q, i, c_ref: (q, i, 0))

    def body(c_ref, a_ref, b_ref, o_ref):
        o_ref[...] = (a_ref[...] + b_ref[...].astype(F32)).astype(o_ref.dtype)

    return pl.pallas_call(
        body, name=name,
        grid_spec=pltpu.PrefetchScalarGridSpec(num_scalar_prefetch=1, grid=grid, in_specs=[own, oth], out_specs=oth),
        out_shape=_sds(recv.shape, CDT), compiler_params=_cp(("parallel", "parallel")),
    )(cidx, dw, recv)


def _add_pairs(xs, ys, name):
    n = len(xs)

    def body(*refs):
        for p in range(n):
            refs[2 * n + p][...] = refs[p][...] + refs[n + p][...]

    return pl.pallas_call(body, name=name, out_shape=[_sds(a.shape, F32) for a in xs])(*xs, *ys)


def _adam_math(w, g, m, v):
    m = ADAM_B1 * m + (1.0 - ADAM_B1) * g
    v = ADAM_B2 * v + (1.0 - ADAM_B2) * (g * g)
    m_hat = m / (1.0 - ADAM_B1 ** ADAM_STEP)
    v_hat = v / (1.0 - ADAM_B2 ** ADAM_STEP)
    delta = -ADAM_LR * (m_hat / (jnp.sqrt(v_hat) + ADAM_EPS) + ADAM_WD * w)
    return delta, m, v


def _adam(gparts, w, m, v, name):
    P, R, C = gparts.shape
    tr = R if R <= 256 else 256

    def body(g_ref, w_ref, m_ref, v_ref, go_ref, d_ref, mo_ref, vo_ref):
        g = g_ref[0].astype(F32)
        for i in range(1, P):
            g = g + g_ref[i].astype(F32)
        d, mn, vn = _adam_math(w_ref[...], g, m_ref[...], v_ref[...])
        go_ref[...] = g
        d_ref[...] = d
        mo_ref[...] = mn
        vo_ref[...] = vn

    t = pl.BlockSpec((tr, C), lambda i: (i, 0))
    return pl.pallas_call(
        body, name=name, grid=(R // tr,),
        in_specs=[pl.BlockSpec((P, tr, C), lambda i: (0, i, 0)), t, t, t], out_specs=[t, t, t, t],
        out_shape=[_sds((R, C), F32)] * 4, compiler_params=_cp(("parallel",)),
    )(gparts, w, m, v)


def _sum_parts(gparts, name):
    P, R, C = gparts.shape

    def body(g_ref, o_ref):
        g = g_ref[0]
        for i in range(1, P):
            g = g + g_ref[i]
        o_ref[...] = g

    return pl.pallas_call(body, name=name, out_shape=_sds((R, C), F32))(gparts)


def _adam_w_ada(c_t, dm, w, m, v):
    R, C = w.shape

    def body(c_ref, dm_ref, w_ref, m_ref, v_ref, go_ref, d_ref, mo_ref, vo_ref):
        g = c_ref[:, 0:1] * dm_ref[0:1, :]
        for b in range(1, 8):
            g = g + c_ref[:, b:b + 1] * dm_ref[b:b + 1, :]
        d, mn, vn = _adam_math(w_ref[...], g, m_ref[...], v_ref[...])
        go_ref[...] = g
        d_ref[...] = d
        mo_ref[...] = mn
        vo_ref[...] = vn

    return pl.pallas_call(body, name="adam_w_ada", out_shape=[_sds((R, C), F32)] * 4,
                          compiler_params=_cp())(c_t, dm, w, m, v)


def kernel(x, c, norm_g, w_ada, b_ada, w_in, pool_w, pool_scale, w_attn_br, w_pool_br, w_out, rel_bias, final_g, loss_target, m_norm_g, m_w_ada, m_b_ada, m_w_in, m_pool_w, m_pool_scale, m_w_attn_br, m_w_pool_br, m_w_out, m_rel_bias, m_final_g, v_norm_g, v_w_ada, v_b_ada, v_w_in, v_pool_w, v_pool_scale, v_w_attn_br, v_w_pool_br, v_w_out, v_rel_bias, v_final_g):
    S = x.shape[1]
    px, py, pc = _my_place()
    me = 4 * px + 2 * py + pc
    cidx = jnp.reshape(pc, (1,)).astype(jnp.int32)

    layout = [(1, 1024), (1, 128), (1, 128), (0, 128)]
    final_g2 = final_g.reshape(1, D)
    me1 = jnp.reshape(me, (1,)).astype(jnp.int32)

    b_part = lax.dynamic_slice(b_ada, (0, me * 384), (1, 384))
    c_all, mod_all = _adaln_exchange(c, w_ada[0], b_part)
    c_all = c_all.reshape(8, D)
    mod = mod_all.reshape(1, 3 * D)
    shift, scale, gate = mod[:, :D], mod[:, D:2 * D], mod[:, 2 * D:]

    gplan = _gather_chips_plan([(None, 1)] * 4)
    lands = _place_slot([w_in[0], w_attn_br[0], w_pool_br[0], w_out[0]], me1, 8, "place_own_shards", CDT)
    g_sems, _, lands, tok = _split_start("gather_chips_start", [], lands, 8, gplan, dep=mod_all)
    st = _local_pre(x[0], shift, scale, norm_g, rel_bias, dep=tok)
    _, lands = _split_wait("gather_chips_wait", g_sems, [], lands, gplan,
                           [st["ht"]] + [b for pair in st["biases"] for b in pair])
    w_in_full, wab8, wpb8, wout8 = _pair_share(lands)
    wab, wpb = _cols_from_blocks(wab8, "w_attn_br_cols"), _cols_from_blocks(wpb8, "w_pool_br_cols")
    wout = wout8.reshape(D, D)

    st = _local_grads(st, loss_target[0], gate, w_in_full, pool_w[0], pool_scale, wab, wpb, wout, final_g2)

    early = [st["dpw"].reshape(AW, 128), st["dps"], st["gfg"], jnp.broadcast_to(st["loss"], (1, 128))]
    dws = [st["dw_in"], st["dw_ab"], st["dw_pb"], st["dw_out"]]
    sent = [st["dw_in_low"]] + dws[1:] + early
    pplan = _pair_plan(layout)
    pair_shapes = [(4, w, a.shape[1]) if ax == 0 else (4, a.shape[0], w) for a, (ax, w) in zip(dws, layout)]
    p_sems, sent, recv, tok = _split_start(
        "reduce_pair_start", sent,
        [lax.empty(s, a.dtype) for s, a in zip(pair_shapes + [a.shape for a in early], sent)], 20, pplan)
    d_rel_bias = _local_bias_grads(st, _BIAS_GRADS_EARLY, dep=tok)
    sent, recv = _split_wait("reduce_pair_wait", p_sems, sent, recv, pplan, d_rel_bias)
    parts = [_pair_add(dw, r, ax, w, cidx, f"pair_add{i}")
             for i, (dw, r, (ax, w)) in enumerate(zip([dws[0]] + sent[1:4], recv[:4], layout))]
    parts += _add_pairs(sent[4:], recv[4:], "pair_add_smalls")

    slot1 = jnp.reshape(2 * px + py, (1,)).astype(jnp.int32)
    cplan = _chips_plan(4)
    c_sems, parts, lands, tok = _split_start("reduce_chips_start", parts,
                                             _place_slot(parts, slot1, 4, "place_own_partials"), 24, cplan)
    d_rel_bias = d_rel_bias + _local_bias_grads(st, _BIAS_GRADS_LATE, dep=tok)
    grad_x, dmod, d_norm_g = _local_input_grads(st, w_in_full, norm_g, scale, dep=tok)

    dmod_all, ng_all, rb_all = _all_gather_small([dmod, d_norm_g, d_rel_bias], "gather_late")
    o_norm_g = _adam(ng_all, norm_g, m_norm_g, v_norm_g, "adam_norm_g")
    o_rel_bias = _adam(rb_all, rel_bias, m_rel_bias, v_rel_bias, "adam_rel_bias")
    o_b_ada = _adam(dmod_all, b_ada, m_b_ada, v_b_ada, "adam_b_ada")
    dm_mine = lax.dynamic_slice(dmod_all.reshape(8, 3 * D), (0, me * 384), (8, 384))
    o_w_ada = [o[None] for o in _adam_w_ada(c_all.T, dm_mine, w_ada[0], m_w_ada[0], v_w_ada[0])]

    _, (g_in, g_ab, g_pb, g_out, g_pw, g_ps, g_fg, g_loss) = _split_wait(
        "reduce_chips_wait", c_sems, parts, lands, cplan, [o_w_ada[0], o_b_ada[0], o_norm_g[0], o_rel_bias[0]])
    o_w_in = [o[None] for o in _adam(g_in, w_in[0], m_w_in[0], v_w_in[0], "adam_w_in")]
    o_w_ab = [o[None] for o in _adam(g_ab, w_attn_br[0], m_w_attn_br[0], v_w_attn_br[0], "adam_w_attn_br")]
    o_w_pb = [o[None] for o in _adam(g_pb, w_pool_br[0], m_w_pool_br[0], v_w_pool_br[0], "adam_w_pool_br")]
    o_w_out = [o[None] for o in _adam(g_out, w_out[0], m_w_out[0], v_w_out[0], "adam_w_out")]
    o_pool_w = [o.reshape(pool_w.shape) for o in _adam(g_pw, pool_w.reshape(AW, 128), m_pool_w.reshape(AW, 128),
                                                       v_pool_w.reshape(AW, 128), "adam_pool_w")]
    o_pool_scale = _adam(g_ps, pool_scale, m_pool_scale, v_pool_scale, "adam_pool_scale")
    o_final_g = [o.reshape(D) for o in _adam(g_fg, final_g2, m_final_g.reshape(1, D), v_final_g.reshape(1, D),
                                             "adam_final_g")]
    loss = _sum_parts(g_loss, "sum_loss")[0, 0]

    per_w = [o_norm_g, o_w_ada, o_b_ada, o_w_in, o_pool_w, o_pool_scale, o_w_ab, o_w_pb, o_w_out, o_rel_bias, o_final_g]
    outs = [loss, grad_x[None]]
    for k in range(4):
        outs += [o[k] for o in per_w]
    return tuple(outs)
```

```python
import functools
import math

import numpy as np
import jax
import jax.numpy as jnp
from jax import lax
from jax.experimental import pallas as pl
from jax.experimental.pallas import tpu as pltpu

F32 = jnp.float32
CDT = jnp.bfloat16
D = 1024
HD = 64
NH = 8
AW = 512
QKV_W = 4608
REST_W = 3584
R16 = 16
EPS = 1e-6
NEG = -1e30
N_BACK = 128
ADAM_LR, ADAM_B1, ADAM_B2, ADAM_EPS, ADAM_WD, ADAM_STEP = 0.001, 0.9, 0.999, 1e-08, 0.01, 10
MESH = pl.DeviceIdType.MESH
VMEM_MB = 1024 * 1024


def _cp(sem=None, vmem=48, **kw):
    if sem is not None:
        kw["dimension_semantics"] = sem
    return pltpu.CompilerParams(vmem_limit_bytes=vmem * VMEM_MB, **kw)


def _sds(shape, dtype):
    return jax.ShapeDtypeStruct(tuple(shape), dtype)


def _rest_blk(jj):
    return jnp.where(jj == 0, 9, jnp.where(jj == 6, 10, jj + 10))


def _sigmoid(z):
    return 0.5 * jnp.tanh(0.5 * z) + 0.5


def _dot(a, b, dims):
    return lax.dot_general(a, b, (dims, ((), ())), preferred_element_type=F32)


NN = ((1,), (0,))
NT = ((1,), (1,))
TN = ((0,), (0,))


def _ld(ref):
    if len(ref.shape) == 2:
        return ref[...]
    return jnp.concatenate([ref[a] for a in range(ref.shape[0])], axis=0)


def _st(ref, val, cols=None):
    val = val.astype(ref.dtype)
    if len(ref.shape) == 2:
        if cols is None:
            ref[...] = val
        else:
            ref[:, cols] = val
        return
    rows = ref.shape[1]
    for a in range(ref.shape[0]):
        if cols is None:
            ref[a] = val[a * rows:(a + 1) * rows]
        else:
            ref[a, :, cols] = val[a * rows:(a + 1) * rows]


REST_COLS = ((9, 10), (11, 16), (10, 11))
_HBM = pl.BlockSpec(memory_space=pltpu.HBM)


def _resident(shape):
    return pl.BlockSpec(shape, lambda *ids: (0,) * len(shape), pipeline_mode=pl.Buffered(1))


def _w512(w_ref, j):
    return w_ref[j // 2, :, 512 * (j % 2):512 * (j % 2 + 1)]


def _proj(h, w):
    S = h.shape[0]
    tm = 512

    def body(a_ref, w_ref, q_ref, r_ref):
        a = a_ref[...]
        for j in range(9):
            q_ref[:, 512 * j:512 * (j + 1)] = _dot(a, _w512(w_ref, j), NN).astype(q_ref.dtype)
        off = 0
        for lo, hi in REST_COLS:
            for j in range(lo, hi):
                r_ref[:, off:off + 512] = _dot(a, _w512(w_ref, j), NN)
                off += 512

    return pl.pallas_call(
        body, name="proj", grid=(S // tm,),
        in_specs=[pl.BlockSpec((tm, D), lambda i: (i, 0)), _resident((8, D, 1024))],
        out_specs=[pl.BlockSpec((tm, QKV_W), lambda i: (i, 0)), pl.BlockSpec((tm, REST_W), lambda i: (i, 0))],
        out_shape=[_sds((S, QKV_W), CDT), _sds((S, REST_W), F32)],
        compiler_params=_cp(("parallel",), vmem=56),
    )(h, w)


def _after(dep):
    return ([], []) if dep is None else ([pl.BlockSpec(memory_space=pl.ANY)], [dep])


def _dh(dqkv, drest, w, dep=None):
    S = dqkv.shape[0]
    tm = 512
    dep_specs, dep_args = _after(dep)

    def body(a_ref, b_ref, w_ref, *rest):
        o_ref = rest[-1]
        acc = _dot(a_ref[:, 0:512], _w512(w_ref, 0), NT)
        for j in range(1, 9):
            acc = acc + _dot(a_ref[:, 512 * j:512 * (j + 1)], _w512(w_ref, j), NT)
        off = 0
        for lo, hi in REST_COLS:
            for j in range(lo, hi):
                acc = acc + _dot(b_ref[:, off:off + 512], _w512(w_ref, j), NT)
                off += 512
        o_ref[...] = acc

    return pl.pallas_call(
        body, name="dh", grid=(S // tm,),
        in_specs=[pl.BlockSpec((tm, QKV_W), lambda i: (i, 0)), pl.BlockSpec((tm, REST_W), lambda i: (i, 0)),
                  _resident((8, D, 1024))] + dep_specs,
        out_specs=pl.BlockSpec((tm, D), lambda i: (i, 0)),
        out_shape=_sds((S, D), F32),
        compiler_params=_cp(("parallel",), vmem=56),
    )(dqkv, drest, w, *dep_args)


def _mm_tn(at, b, *, n_total, out_blk, name, prev=None, tn=512, with_low=False):
    R, S = at.shape
    nb = b.shape[1] // tn
    n_out = 2 if with_low else 1

    def body(*refs):
        res = _dot(refs[0][...], refs[1][...], NN)
        refs[-n_out][...] = res
        if with_low:
            refs[-1][...] = res.astype(CDT)

    in_specs = [_resident((R, S)), pl.BlockSpec((S, tn), lambda j: (0, j))]
    args = [at, b]
    aliases = {}
    if prev is not None:
        prev = list(prev) if with_low else [prev]
        in_specs += [pl.BlockSpec(memory_space=pl.ANY)] * n_out
        args += prev
        aliases = {2 + k: k for k in range(n_out)}
    out_spec = pl.BlockSpec((R, tn), lambda j: (0, out_blk(j)))
    outs = pl.pallas_call(
        body, name=name, grid=(nb,), in_specs=in_specs,
        out_specs=[out_spec] * n_out,
        out_shape=[_sds((R, n_total), F32), _sds((R, n_total), CDT)][:n_out],
        input_output_aliases=aliases,
        compiler_params=_cp(("parallel",)),
    )(*args)
    return tuple(outs) if with_low else outs[0]


def _tcast(a, name):
    M, C = a.shape
    tm = 512

    def body(a_ref, o_ref):
        o_ref[...] = a_ref[...].astype(F32).T.astype(o_ref.dtype)

    return pl.pallas_call(
        body, name=name, grid=(M // tm,),
        in_specs=[pl.BlockSpec((tm, C), lambda i: (i, 0))],
        out_specs=pl.BlockSpec((C, tm), lambda i: (0, i)),
        out_shape=_sds((C, M), CDT), compiler_params=_cp(("parallel",)),
    )(a)


TOK = R16 * R16


def _swap16():
    i = lax.broadcasted_iota(jnp.int32, (TOK, TOK), 0)
    j = lax.broadcasted_iota(jnp.int32, (TOK, TOK), 1)
    return jnp.where(j == jnp.bitwise_and(i, 15) * 16 + jnp.right_shift(i, 4), 1.0, 0.0).astype(CDT)


def _norm_mod(x, norm_g, shift, scale, dep=None):
    S = x.shape[0]
    dep_specs, dep_args = _after(dep)

    def body(x_ref, g_ref, sh_ref, sc_ref, *rest):
        xv = x_ref[...]
        xn = xv * lax.rsqrt(jnp.mean(xv * xv, axis=-1, keepdims=True) + EPS)
        h = (xn * (g_ref[...] * (1.0 + sc_ref[...])) + sh_ref[...]).astype(CDT)
        _st(rest[-1], _dot(_swap16(), h, NN))

    vec = pl.BlockSpec((1, D), lambda i: (0, 0))
    return pl.pallas_call(
        body, name="norm_mod", grid=(S // TOK,),
        in_specs=[pl.BlockSpec((TOK, D), lambda i: (i, 0)), vec, vec, vec] + dep_specs,
        out_specs=pl.BlockSpec((R16, R16, D), lambda i: (0, i, 0)),
        out_shape=_sds((R16, S // R16, D), CDT), compiler_params=_cp(("parallel",)),
    )(x, norm_g, shift, scale, *dep_args)


def _t5_bucket_np(n):
    nf = np.maximum(n, 1).astype(np.float32)
    large = 16 + (np.log(nf / np.float32(16)) / np.float32(math.log(2048 / 16)) * np.float32(16)).astype(np.int32)
    large = np.minimum(large, 31)
    return np.where(n < 16, n, large).astype(np.int32)


class _Geom:
    def __init__(self, gi, LS):
        self.gi, self.LS = gi, LS
        self.dil = (1, 4, 16)[gi]
        if gi == 0:
            self.lead, self.rows, self.n_tiles, self.has_prev = (R16,), 16, LS // 16, True
            a = np.arange(R16)[:, None]
            ll = np.arange(16)[None, :]
            self.pos = (16 * ll + a).reshape(-1)
        elif gi == 1:
            self.lead, self.rows, self.n_tiles, self.has_prev = (4,), 64, LS // 64, True
            a = np.arange(4)[:, None]
            ll = np.arange(64)[None, :]
            self.pos = (4 * ll + a).reshape(-1)
        else:
            self.lead, self.rows, self.n_tiles, self.has_prev = (), LS, 1, False
            self.pos = np.arange(LS)
        self.T = self.pos.shape[0]
        self.Tk = 2 * self.T if self.has_prev else self.T
        self.outer = {0: (), 1: (4,), 2: (R16,)}[gi]

    def view(self, arr):
        S, C = arr.shape
        if self.gi == 1:
            return arr.reshape(4, 4, self.LS, C)
        return arr.reshape(R16, self.LS, C)

    def view_shape(self, C):
        return (4, 4, self.LS, C) if self.gi == 1 else (R16, self.LS, C)

    def unview(self, arr):
        return arr.reshape(R16 * self.LS, arr.shape[-1])

    def spec(self, W, cblk, tile):
        if self.gi == 0:
            return pl.BlockSpec((R16, 16, W), lambda n: (0, tile(n), cblk))
        if self.gi == 1:
            return pl.BlockSpec((4, None, 64, W), lambda r, n: (0, r, tile(n), cblk))
        return pl.BlockSpec((None, self.LS, W), lambda r: (r, 0, cblk))

    def halves(self):
        R = self.rows
        idx = np.arange(self.T).reshape(-1, R)
        return idx[:, :R // 2].reshape(-1), idx[:, R // 2:].reshape(-1)

    def bucket_index(self, sub):
        lo, hi = self.halves()
        pos = self.pos
        if sub == 0:
            pq = pos[lo]
            pk = np.concatenate([pos[hi] - self.T, pos[lo]]) if self.has_prev else np.concatenate([pos[lo], pos[hi]])
        else:
            pq = pos[hi]
            pk = np.concatenate([pos[lo], pos[hi]])
        dist = pq[:, None] - pk[None, :]
        ok = (dist >= 0) & (dist <= N_BACK)
        bucket = _t5_bucket_np(np.clip(dist, 0, N_BACK) * self.dil)
        return np.where(ok, bucket, -1).astype(np.int32)


def _bias_table(idx, buckets, rel_bias, gi, sub, dep=None):
    T, Tk = idx.shape
    tr = 32
    dep_specs, dep_args = _after(dep)

    def body(idx_ref, rb_ref, *rest):
        h = pl.program_id(0)
        ix = idx_ref[...]
        acc = jnp.full(ix.shape, NEG, F32)
        for b in buckets:
            acc = jnp.where(ix == b, rb_ref[b, gi * NH + h], acc)
        rest[-1][...] = acc

    return pl.pallas_call(
        body, name=f"bias_table{gi}{'ab'[sub]}", grid=(NH, T // tr),
        in_specs=[pl.BlockSpec((tr, Tk), lambda h, i: (i, 0)), pl.BlockSpec(memory_space=pltpu.SMEM)] + dep_specs,
        out_specs=pl.BlockSpec((None, tr, Tk), lambda h, i: (h, i, 0)),
        out_shape=_sds((NH, T, Tk), F32), compiler_params=_cp(("parallel", "parallel")),
    )(idx, rel_bias, *dep_args)


def _bias_grad(dsum, idx, buckets, gi, sub, dep=None):
    _, T, Tk = dsum.shape
    dep_specs, dep_args = _after(dep)

    def body(ds_ref, idx_ref, *rest):
        o_ref, r_ref = rest[-2], rest[-1]
        r_ref[...] = jnp.zeros_like(r_ref)
        ix, ds = idx_ref[...], ds_ref[...]
        for b in buckets:
            r_ref[b:b + 1, :] = jnp.sum(jnp.where(ix == b, ds, 0.0), axis=0, keepdims=True)
        o_ref[...] = jnp.broadcast_to(jnp.sum(r_ref[...], axis=1, keepdims=True), (32, 128))

    out = pl.pallas_call(
        body, name=f"bias_grad{gi}{'ab'[sub]}", grid=(NH,),
        in_specs=[pl.BlockSpec((None, T, Tk), lambda h: (h, 0, 0)), pl.BlockSpec((T, Tk), lambda h: (0, 0))] + dep_specs,
        out_specs=pl.BlockSpec((None, 32, 128), lambda h: (h, 0, 0)),
        out_shape=_sds((NH, 32, 128), F32), scratch_shapes=[pltpu.VMEM((32, Tk), F32)],
        compiler_params=_cp(("parallel",)),
    )(dsum, idx, *dep_args)
    return out[:, :, 0].T


def _prev_mask(n, T, Tk):
    col = lax.broadcasted_iota(jnp.int32, (1, Tk), 1)
    return jnp.where((col < T) & (n == 0), NEG, 0.0).astype(F32)


def _split_rows(v, R):
    A, half = v.shape[0] // R, R // 2
    via_f32 = v.dtype != F32 and half % 16 != 0
    w = v.astype(F32) if via_f32 else v
    lo = jnp.concatenate([w[a * R:a * R + half] for a in range(A)], axis=0)
    hi = jnp.concatenate([w[a * R + half:(a + 1) * R] for a in range(A)], axis=0)
    return (lo.astype(v.dtype), hi.astype(v.dtype)) if via_f32 else (lo, hi)


def _merge_rows(lo, hi, R):
    half = R // 2
    A = lo.shape[0] // half
    return jnp.concatenate([x[a * half:(a + 1) * half] for a in range(A) for x in (lo, hi)], axis=0)


def _attn_fwd(qkv, biases, geom, others=None):
    S = qkv.shape[0]
    gi, T, R, has_prev = geom.gi, geom.T, geom.rows, geom.has_prev
    H2 = T // 2
    qv = geom.view(qkv)
    cq, ck, cv = 3 * gi, 3 * gi + 1, 3 * gi + 2
    assert others is None or not has_prev

    def body(*refs):
        if has_prev:
            q_ref, kp_ref, kc_ref, vp_ref, vc_ref, ba_ref, bb_ref, o_ref, l_ref = refs
        elif others is not None:
            (q_ref, kc_ref, vc_ref, ba_ref, bb_ref, o0_ref, o1_ref, l0_ref, l1_ref, z_ref,
             attn_ref, L_ref, ag_ref, agt_ref) = refs
        else:
            q_ref, kc_ref, vc_ref, ba_ref, bb_ref, o_ref, l_ref = refs
        q_lo, q_hi = _split_rows(_ld(q_ref), R)
        kc_lo, kc_hi = _split_rows(_ld(kc_ref), R)
        vc_lo, vc_hi = _split_rows(_ld(vc_ref), R)
        k_b = jnp.concatenate([kc_lo, kc_hi], axis=0)
        v_b = jnp.concatenate([vc_lo, vc_hi], axis=0)
        if has_prev:
            n = pl.program_id(len(geom.outer))
            k_a = jnp.concatenate([_split_rows(_ld(kp_ref), R)[1], kc_lo], axis=0)
            v_a = jnp.concatenate([_split_rows(_ld(vp_ref), R)[1], vc_lo], axis=0)
            pm = _prev_mask(n, H2, T)
        else:
            k_a, v_a, pm = k_b, v_b, None
        low = lax.broadcasted_iota(jnp.int32, (1, 2 * HD), 1) < HD
        for j in range(NH // 2):
            cols = slice(2 * HD * j, 2 * HD * (j + 1))
            o_half, l_half = [], []
            for q, k, v, b_ref, mask in ((q_lo, k_a, v_a, ba_ref, pm), (q_hi, k_b, v_b, bb_ref, None)):
                q2 = q[:, cols].astype(F32) * 0.125
                k2, v2 = k[:, cols], v[:, cols]
                qs = jnp.concatenate([jnp.where(low, q2, 0.0), jnp.where(low, 0.0, q2)], axis=0).astype(q.dtype)
                s = _dot(qs, k2, NT) + b_ref[j]
                if mask is not None:
                    s = s + mask
                m = jnp.max(s, axis=1, keepdims=True)
                p = jnp.exp(s - m)
                l = jnp.sum(p, axis=1, keepdims=True)
                pv = _dot(p.astype(v.dtype), v2, NN) / l
                lse = m + jnp.log(l)
                o_half.append(jnp.where(low, pv[:H2], pv[H2:]))
                l_half.append(jnp.where(low, lse[:H2], lse[H2:]))
            o2, lc = _merge_rows(o_half[0], o_half[1], R), _merge_rows(l_half[0], l_half[1], R)
            if others is None:
                _st(o_ref, o2, cols)
                _st(l_ref, lc, cols)
                continue
            la, lb = l0_ref[:, cols], l1_ref[:, cols]
            m = jnp.maximum(jnp.maximum(la, lb), lc)
            ea, eb, ec = jnp.exp(la - m), jnp.exp(lb - m), jnp.exp(lc - m)
            den = ea + eb + ec
            attn = (ea * o0_ref[:, cols] + eb * o1_ref[:, cols] + ec * o2) / den
            attn_ref[:, cols] = attn
            L_ref[:, cols] = m + jnp.log(den)
            z = z_ref[:, cols]
            ag = attn * (z * _sigmoid(z))
            ag_ref[:, cols] = ag.astype(ag_ref.dtype)
            agt_ref[cols, :] = ag.T.astype(agt_ref.dtype)

    cur = lambda n: n
    prev = lambda n: jnp.maximum(n - 1, 0)
    if has_prev:
        in_specs = [geom.spec(AW, cq, cur), geom.spec(AW, ck, prev), geom.spec(AW, ck, cur),
                    geom.spec(AW, cv, prev), geom.spec(AW, cv, cur)]
        args = [qv] * 5
    else:
        in_specs = [geom.spec(AW, cq, cur), geom.spec(AW, ck, cur), geom.spec(AW, cv, cur)]
        args = [qv] * 3
    nd = len(geom.outer) + (1 if has_prev else 0)
    biases = [b.reshape(NH // 2, 2 * H2, b.shape[2]) for b in biases]
    in_specs += [pl.BlockSpec(b.shape, lambda *ids: (0, 0, 0)) for b in biases]
    grid = geom.outer + ((geom.n_tiles,) if has_prev else ())
    oshape = geom.view_shape(AW)
    tile = geom.spec(AW, 0, cur)
    if others is None:
        o, l = pl.pallas_call(
            body, name=f"attn_fwd{gi}", grid=grid, in_specs=in_specs, out_specs=[tile, tile],
            out_shape=[_sds(oshape, F32), _sds(oshape, F32)],
            compiler_params=_cp(("arbitrary",) * nd),
        )(*args, *biases)
        return geom.unview(o), geom.unview(l)
    os_, ls_, rest = others
    attn, Lb, ag, agt = pl.pallas_call(
        body, name=f"attn_fwd{gi}_merge", grid=grid, in_specs=in_specs + [tile] * 5,
        out_specs=[tile, tile, tile, pl.BlockSpec((AW, T), lambda r: (0, r))],
        out_shape=[_sds(oshape, F32), _sds(oshape, F32), _sds(oshape, CDT), _sds((AW, S), CDT)],
        compiler_params=_cp(("arbitrary",) * nd),
    )(*args, *biases, *[geom.view(a) for a in (*os_, *ls_, rest)])
    return geom.unview(attn), geom.unview(Lb), geom.unview(ag), agt


def _pool_counts(r, LS, win):
    l = lax.broadcasted_iota(jnp.int32, (LS, 1), 0)
    return jnp.minimum(16 * l + (r + 1), win).astype(F32)


def _pool_fwd(rest, pool_w, pool_scale):
    S = rest.shape[0]
    LS = S // R16
    r3 = rest.reshape(R16, LS, REST_W)
    PG = 128

    def body(u_ref, z_ref, pw_ref, ps_ref, pt_ref, mp_ref, pg_ref, pgt_ref, e_ref):
        g = pl.program_id(0)
        win = jnp.left_shift(2, g)
        row = lax.broadcasted_iota(jnp.int32, (LS, PG), 0)
        for r in range(R16):
            e_ref[R16 + r] = u_ref[r]
        for r in range(1, R16):
            e_ref[r] = jnp.where(row == 0, 0.0, pltpu.roll(u_ref[r], 1, 0))

        def double(s):
            for i in range(2 * R16 - 1, 2 * s - 1, -1):
                e_ref[i] = e_ref[i] + e_ref[i - s]

        double(1)
        for k in (1, 2, 3):
            pl.when(g >= k)(functools.partial(double, 2 ** k))
        pw = pw_ref[...].astype(CDT)
        for r in range(R16):
            u = u_ref[r]
            pooled = e_ref[R16 + r] / _pool_counts(r, LS, win) - u
            pc = pooled.astype(CDT)
            mp = _dot(pc, pw, NN)
            z = z_ref[r]
            pg = (mp * ps_ref[...]) * (z * _sigmoid(z))
            mp_ref[r] = mp
            pg_ref[r] = pg.astype(pg_ref.dtype)
            pt_ref[:, r * LS:(r + 1) * LS] = pooled.T.astype(pt_ref.dtype)
            pgt_ref[:, r * LS:(r + 1) * LS] = pg.T.astype(pgt_ref.dtype)

    col = lambda off: pl.BlockSpec((R16, LS, PG), lambda g: (0, 0, off + g))
    tsp = pl.BlockSpec((PG, S), lambda g: (g, 0))
    pt, mp, pg, pgt = pl.pallas_call(
        body, name="pool_fwd", grid=(4,),
        in_specs=[col(24), col(4), pl.BlockSpec((None, PG, PG), lambda g: (g, 0, 0)),
                  pl.BlockSpec((1, PG), lambda g: (0, g))],
        out_specs=[tsp, col(0), col(0), tsp],
        out_shape=[_sds((AW, S), CDT), _sds((R16, LS, AW), F32), _sds((R16, LS, AW), CDT), _sds((AW, S), CDT)],
        scratch_shapes=[pltpu.VMEM((2 * R16, LS, PG), F32)],
        compiler_params=_cp(("parallel",)),
    )(r3, r3, pool_w, pool_scale)
    return pt, mp.reshape(S, AW), pg.reshape(S, AW), pgt


def _branch_merge(ag, pg, wab, wpb, rest):
    S = ag.shape[0]
    tm, tn = 512, 512

    def body(ag_ref, pg_ref, wa_ref, wp_ref, ga_ref, gp_ref, m_ref, mt_ref):
        ya = _dot(ag_ref[...], wa_ref[...], NN)
        yp = _dot(pg_ref[...], wp_ref[...], NN)
        mg = _sigmoid(ga_ref[...]) * ya + _sigmoid(gp_ref[...]) * yp
        m_ref[...] = mg.astype(m_ref.dtype)
        mt_ref[...] = mg.T.astype(mt_ref.dtype)

    a = pl.BlockSpec((tm, AW), lambda i, j: (i, 0))
    w = pl.BlockSpec((AW, tn), lambda i, j: (0, j))
    o = pl.BlockSpec((tm, tn), lambda i, j: (i, j))
    return pl.pallas_call(
        body, name="branch_merge", grid=(S // tm, D // tn),
        in_specs=[a, a, w, w, pl.BlockSpec((tm, tn), lambda i, j: (i, 2 + j)),
                  pl.BlockSpec((tm, tn), lambda i, j: (i, 4 + j))],
        out_specs=[o, pl.BlockSpec((tn, tm), lambda i, j: (j, i))],
        out_shape=[_sds((S, D), CDT), _sds((D, S), CDT)],
        compiler_params=_cp(("parallel", "parallel")),
    )(ag, pg, wab, wpb, rest, rest)


def _out_loss(merged, wout, x, target, gate, final_g):
    S = x.shape[0]

    def body(m_ref, w_ref, x_ref, t_ref, gate_ref, fg_ref, dx2_ref, dmo_ref, loss_ref, gfg_ref, dgate_ref):
        i = pl.program_id(0)

        @pl.when(i == 0)
        def _():
            loss_ref[...] = jnp.zeros_like(loss_ref)
            gfg_ref[...] = jnp.zeros_like(gfg_ref)
            dgate_ref[...] = jnp.zeros_like(dgate_ref)

        swap = _swap16()
        mo = _dot(_dot(swap, _ld(m_ref), NN).astype(CDT), w_ref[...], NN)
        gate, fg = gate_ref[...], fg_ref[...]
        x2 = x_ref[...] + gate * mo
        r2 = lax.rsqrt(jnp.mean(x2 * x2, axis=-1, keepdims=True) + EPS)
        xn2 = x2 * r2
        diff = xn2 * fg - t_ref[...]
        loss = jnp.sum(jnp.sum(diff * diff, axis=-1, keepdims=True), axis=0, keepdims=True)
        dy = diff * (1.0 / D)
        dxn2 = dy * fg
        dx2 = r2 * (dxn2 - xn2 * jnp.mean(dxn2 * xn2, axis=-1, keepdims=True))
        dx2_ref[...] = dx2
        _st(dmo_ref, _dot(swap, (dx2 * gate).astype(CDT), NN))
        loss_ref[...] += jnp.broadcast_to(loss * (0.5 / D), loss_ref.shape)
        gfg_ref[...] += jnp.sum(dy * xn2, axis=0, keepdims=True)
        dgate_ref[...] += jnp.sum(dx2 * mo, axis=0, keepdims=True)

    slab = pl.BlockSpec((R16, R16, D), lambda i: (0, i, 0))
    nat = pl.BlockSpec((TOK, D), lambda i: (i, 0))
    vec = pl.BlockSpec((1, D), lambda i: (0, 0))
    return pl.pallas_call(
        body, name="out_loss", grid=(S // TOK,),
        in_specs=[slab, _resident((D, D)), nat, nat, vec, vec],
        out_specs=[nat, slab, pl.BlockSpec((1, 128), lambda i: (0, 0)), vec, vec],
        out_shape=[_sds((S, D), F32), _sds((R16, S // R16, D), CDT), _sds((1, 128), F32), _sds((1, D), F32),
                   _sds((1, D), F32)],
        compiler_params=_cp(("arbitrary",)),
    )(merged, wout, x, target, gate, final_g)


def _bwd_gates(dmo, wout, wab, wpb, rest, ag, pg, attn, mp, pool_scale):
    S = dmo.shape[0]
    tm = 256

    def body(dmo_ref, wo_ref, wa_ref, wp_ref, za_ref, zp_ref, ga0, ga1, gp0, gp1, ag_ref, pg_ref, at_ref, mp_ref,
             ps_ref, dya_ref, dyp_ref, dr_ref, dat_ref, dl_ref, dmp_ref, dps_ref):
        i = pl.program_id(0)

        @pl.when(i == 0)
        def _():
            dps_ref[...] = jnp.zeros_like(dps_ref)

        dm = _dot(dmo_ref[...], wo_ref[...], NT)
        sa = _sigmoid(jnp.concatenate([ga0[...], ga1[...]], axis=1))
        sp = _sigmoid(jnp.concatenate([gp0[...], gp1[...]], axis=1))
        dya = (dm * sa).astype(CDT)
        dyp = (dm * sp).astype(CDT)
        dya_ref[...] = dya
        dyp_ref[...] = dyp
        ya = _dot(ag_ref[...], wa_ref[...], NN)
        yp = _dot(pg_ref[...], wp_ref[...], NN)
        dr_ref[:, 1024:2048] = (dm * ya * sa * (1.0 - sa)).astype(dr_ref.dtype)
        dr_ref[:, 2048:3072] = (dm * yp * sp * (1.0 - sp)).astype(dr_ref.dtype)
        da = _dot(dya, wa_ref[...], NT)
        dp = _dot(dyp, wp_ref[...], NT)
        za = za_ref[...]
        sga = _sigmoid(za)
        attn = at_ref[...]
        dattn = da * (za * sga)
        dr_ref[:, 0:512] = (da * attn * (sga * (1.0 + za * (1.0 - sga)))).astype(dr_ref.dtype)
        dat_ref[...] = dattn.astype(dat_ref.dtype)
        prod = dattn * attn
        low = lax.broadcasted_iota(jnp.int32, (1, 2 * HD), 1) < HD
        for j in range(NH // 2):
            x2 = prod[:, 2 * HD * j:2 * HD * (j + 1)]
            s0 = jnp.sum(jnp.where(low, x2, 0.0), axis=1, keepdims=True)
            s1 = jnp.sum(jnp.where(low, 0.0, x2), axis=1, keepdims=True)
            dl_ref[:, 2 * HD * j:2 * HD * (j + 1)] = jnp.where(low, s0, s1)
        zp = zp_ref[...]
        sgp = _sigmoid(zp)
        mpre = mp_ref[...]
        ps = ps_ref[...]
        dmixed = dp * (zp * sgp)
        dr_ref[:, 512:1024] = (dp * (mpre * ps) * (sgp * (1.0 + zp * (1.0 - sgp)))).astype(dr_ref.dtype)
        dps_ref[...] += jnp.sum(dmixed * mpre, axis=0, keepdims=True)
        dmp_ref[...] = (dmixed * ps).astype(dmp_ref.dtype)

    full = lambda shape: pl.BlockSpec(shape, lambda i: (0, 0))
    tD = pl.BlockSpec((tm, D), lambda i: (i, 0))
    tA = pl.BlockSpec((tm, AW), lambda i: (i, 0))
    rb = lambda c: pl.BlockSpec((tm, AW), lambda i: (i, c))
    return pl.pallas_call(
        body, name="bwd_gates", grid=(S // tm,),
        in_specs=[tD, full((D, D)), full((AW, D)), full((AW, D)), rb(0), rb(1), rb(2), rb(3), rb(4), rb(5),
                  tA, tA, tA, tA, full((1, AW))],
        out_specs=[tD, tD, pl.BlockSpec((tm, 3072), lambda i: (i, 0)), tA, tA, tA, full((1, AW))],
        out_shape=[_sds((S, D), CDT), _sds((S, D), CDT), _sds((S, REST_W), CDT), _sds((S, AW), CDT),
                   _sds((S, AW), F32), _sds((S, AW), CDT), _sds((1, AW), F32)],
        compiler_params=_cp(("arbitrary",), vmem=56),
    )(dmo, wout, wab, wpb, rest, rest, rest, rest, rest, rest, ag, pg, attn, mp, pool_scale)


def _pool_bwd(dmp, pt, pool_w, drest):
    S = dmp.shape[0]
    LS = S // R16
    PG = 128
    d3 = dmp.reshape(R16, LS, AW)
    dr3 = drest.reshape(R16, LS, REST_W)

    def body(d_ref, pt_ref, pw_ref, dr_in, du_ref, dpw_ref, g_ref, s_ref):
        del dr_in
        g = pl.program_id(0)
        win = jnp.left_shift(2, g)
        row = lax.broadcasted_iota(jnp.int32, (LS, PG), 0)
        pw = pw_ref[...].astype(CDT)
        dpw = jnp.zeros((PG, PG), F32)
        for r in range(R16):
            dm = d_ref[r]
            dpw = dpw + _dot(pt_ref[:, r * LS:(r + 1) * LS], dm, NN)
            dpooled = _dot(dm, pw, NT)
            g_ref[r] = dpooled
            sc = dpooled / _pool_counts(r, LS, win)
            s_ref[r] = sc
            if r < R16 - 1:
                s_ref[R16 + r] = jnp.where(row == LS - 1, 0.0, pltpu.roll(sc, LS - 1, 0))
        dpw_ref[...] = dpw

        def double(s):
            for i in range(0, 2 * R16 - 2 * s):
                s_ref[i] = s_ref[i] + s_ref[i + s]

        double(1)
        for k in (1, 2, 3):
            pl.when(g >= k)(functools.partial(double, 2 ** k))
        for r in range(R16):
            du_ref[r] = (s_ref[r] - g_ref[r]).astype(du_ref.dtype)

    du, dpw = pl.pallas_call(
        body, name="pool_bwd", grid=(4,),
        in_specs=[pl.BlockSpec((R16, LS, PG), lambda g: (0, 0, g)), pl.BlockSpec((PG, S), lambda g: (g, 0)),
                  pl.BlockSpec((None, PG, PG), lambda g: (g, 0, 0)), pl.BlockSpec(memory_space=pl.ANY)],
        out_specs=[pl.BlockSpec((R16, LS, PG), lambda g: (0, 0, 24 + g)),
                   pl.BlockSpec((None, PG, PG), lambda g: (g, 0, 0))],
        out_shape=[_sds((R16, LS, REST_W), CDT), _sds((4, PG, PG), F32)],
        scratch_shapes=[pltpu.VMEM((R16, LS, PG), F32), pltpu.VMEM((2 * R16, LS, PG), F32)],
        input_output_aliases={3: 0},
        compiler_params=_cp(("parallel",)),
    )(d3, pt, pool_w, dr3)
    return du.reshape(S, REST_W), dpw


def _attn_bwd(qkv, dattn, Lb, dlb, biases, geom, dqkv_prev):
    S = qkv.shape[0]
    gi, T, R, has_prev, N = geom.gi, geom.T, geom.rows, geom.has_prev, geom.n_tiles
    H2 = T // 2
    nd_outer = len(geom.outer)
    qv = geom.view(qkv)
    cq, ck, cv = 3 * gi, 3 * gi + 1, 3 * gi + 2
    W3 = 3 * AW

    def heads(q_ref, kp_ref, kc_ref, vp_ref, vc_ref, do_ref, L_ref, dl_ref, ba_ref, bb_ref, pm, dsa_ref, dsb_ref, emit):
        q_lo, q_hi = _split_rows(_ld(q_ref), R)
        do_lo, do_hi = _split_rows(_ld(do_ref), R)
        L_lo, L_hi = _split_rows(_ld(L_ref), R)
        dl_lo, dl_hi = _split_rows(_ld(dl_ref), R)
        kc_lo, kc_hi = _split_rows(_ld(kc_ref), R)
        vc_lo, vc_hi = _split_rows(_ld(vc_ref), R)
        k_b = jnp.concatenate([kc_lo, kc_hi], axis=0)
        v_b = jnp.concatenate([vc_lo, vc_hi], axis=0)
        if has_prev:
            k_a = jnp.concatenate([_split_rows(_ld(kp_ref), R)[1], kc_lo], axis=0)
            v_a = jnp.concatenate([_split_rows(_ld(vp_ref), R)[1], vc_lo], axis=0)
        else:
            k_a, v_a = k_b, v_b
        low = lax.broadcasted_iota(jnp.int32, (1, 2 * HD), 1) < HD
        for j in range(NH // 2):
            sl = slice(2 * HD * j, 2 * HD * (j + 1))
            res = []
            for q, k, v, do, L, dl, b_ref, mask, ds_ref in (
                    (q_lo, k_a, v_a, do_lo, L_lo, dl_lo, ba_ref, pm, dsa_ref),
                    (q_hi, k_b, v_b, do_hi, L_hi, dl_hi, bb_ref, None, dsb_ref)):
                q2 = q[:, sl].astype(F32) * 0.125
                do2 = do[:, sl].astype(F32)
                k2, v2 = k[:, sl], v[:, sl]
                L2, dl2 = L[:, sl], dl[:, sl]
                qs = jnp.concatenate([jnp.where(low, q2, 0.0), jnp.where(low, 0.0, q2)], axis=0).astype(q.dtype)
                dom = jnp.concatenate([jnp.where(low, do2, 0.0), jnp.where(low, 0.0, do2)], axis=0).astype(q.dtype)
                Lc = jnp.concatenate([L2[:, 0:1], L2[:, HD:HD + 1]], axis=0)
                dlc = jnp.concatenate([dl2[:, 0:1], dl2[:, HD:HD + 1]], axis=0)
                s = _dot(qs, k2, NT) + b_ref[j]
                if mask is not None:
                    s = s + mask
                p = jnp.exp(s - Lc)
                ds = p * (_dot(dom, v2, NT) - dlc)
                ds_ref[j] += ds
                dsc = ds.astype(q.dtype)
                dq_st = _dot(dsc, k2, NN)
                res.append((jnp.where(low, dq_st[:H2], dq_st[H2:]) * 0.125, _dot(dsc, qs, TN),
                            _dot(p.astype(q.dtype), dom, TN)))
            (dq_lo, dk_a, dv_a), (dq_hi, dk_b, dv_b) = res
            dq = _merge_rows(dq_lo, dq_hi, R)
            if has_prev:
                emit(sl, dq, _merge_rows(dk_a[H2:] + dk_b[:H2], dk_b[H2:], R),
                     _merge_rows(dv_a[H2:] + dv_b[:H2], dv_b[H2:], R), dk_a[:H2], dv_a[:H2])
            else:
                emit(sl, dq, _merge_rows(dk_a[:H2] + dk_b[:H2], dk_a[H2:] + dk_b[H2:], R),
                     _merge_rows(dv_a[:H2] + dv_b[:H2], dv_a[H2:] + dv_b[H2:], R), None, None)

    def first_step():
        ok = pl.program_id(nd_outer) == 0 if has_prev else pl.program_id(0) == 0
        for a in range(nd_outer if has_prev else 0):
            ok = ok & (pl.program_id(a) == 0)
        return ok

    if has_prev:
        def body(q_ref, kp_ref, kc_ref, vp_ref, vc_ref, do_ref, L_ref, dl_ref, ba_ref, bb_ref, prev_in, out_ref,
                 dsa_ref, dsb_ref, hold, new, pp):
            del prev_in
            s_id = pl.program_id(nd_outer)

            @pl.when(first_step())
            def _():
                dsa_ref[...] = jnp.zeros_like(dsa_ref)
                dsb_ref[...] = jnp.zeros_like(dsb_ref)

            @pl.when(s_id < N)
            def _():
                def emit(sl, dq, dk, dv, dk_prev, dv_prev):
                    new[:, sl] = dq
                    new[:, slice(AW + sl.start, AW + sl.stop)] = dk
                    new[:, slice(2 * AW + sl.start, 2 * AW + sl.stop)] = dv
                    pp[:, sl] = dk_prev
                    pp[:, slice(AW + sl.start, AW + sl.stop)] = dv_prev

                heads(q_ref, kp_ref, kc_ref, vp_ref, vc_ref, do_ref, L_ref, dl_ref, ba_ref, bb_ref,
                      _prev_mask(s_id, H2, T), dsa_ref, dsb_ref, emit)

            @pl.when((s_id >= 1) & (s_id < N))
            def _():
                _st(out_ref, hold[:, 0:AW], slice(0, AW))
                late = _merge_rows(jnp.zeros((H2, 2 * AW), F32), pp[...], R)
                _st(out_ref, hold[:, AW:W3] + late, slice(AW, W3))

            @pl.when(s_id == N)
            def _():
                _st(out_ref, hold[...])

            @pl.when(s_id < N)
            def _():
                hold[...] = new[...]

        cur = lambda n: jnp.minimum(n, N - 1)
        prev = lambda n: jnp.clip(n - 1, 0, N - 1)
        delayed = lambda n: jnp.maximum(n - 1, 0)
        in_specs = [geom.spec(AW, cq, cur), geom.spec(AW, ck, prev), geom.spec(AW, ck, cur),
                    geom.spec(AW, cv, prev), geom.spec(AW, cv, cur),
                    geom.spec(AW, 0, cur), geom.spec(AW, 0, cur), geom.spec(AW, 0, cur)]
        args = [qv] * 5 + [geom.view(dattn), geom.view(Lb), geom.view(dlb)]
        out_spec = geom.spec(W3, gi, delayed)
        grid = geom.outer + (N + 1,)
        scratch = [pltpu.VMEM((T, W3), F32), pltpu.VMEM((T, W3), F32), pltpu.VMEM((H2, 2 * AW), F32)]
    else:
        def body(q_ref, kc_ref, vc_ref, do_ref, L_ref, dl_ref, ba_ref, bb_ref, prev_in, out_ref, dsa_ref, dsb_ref):
            del prev_in

            @pl.when(first_step())
            def _():
                dsa_ref[...] = jnp.zeros_like(dsa_ref)
                dsb_ref[...] = jnp.zeros_like(dsb_ref)

            def emit(sl, dq, dk, dv, dk_prev, dv_prev):
                _st(out_ref, dq, sl)
                _st(out_ref, dk, slice(AW + sl.start, AW + sl.stop))
                _st(out_ref, dv, slice(2 * AW + sl.start, 2 * AW + sl.stop))

            heads(q_ref, None, kc_ref, None, vc_ref, do_ref, L_ref, dl_ref, ba_ref, bb_ref, None, dsa_ref, dsb_ref,
                  emit)

        cur = lambda n: n
        in_specs = [geom.spec(AW, cq, cur), geom.spec(AW, ck, cur), geom.spec(AW, cv, cur),
                    geom.spec(AW, 0, cur), geom.spec(AW, 0, cur), geom.spec(AW, 0, cur)]
        args = [qv] * 3 + [geom.view(dattn), geom.view(Lb), geom.view(dlb)]
        out_spec = geom.spec(W3, gi, cur)
        grid = geom.outer
        scratch = []

    whole = lambda b: pl.BlockSpec(b.shape, lambda *ids: (0, 0, 0))
    biases = [b.reshape(NH // 2, 2 * H2, b.shape[2]) for b in biases]
    in_specs += [whole(b) for b in biases]
    in_specs.append(pl.BlockSpec(memory_space=pl.ANY))
    if dqkv_prev is None:
        last, aliases = biases[0], {}
    else:
        last, aliases = geom.view(dqkv_prev), {len(in_specs) - 1: 0}
    dq, dsa, dsb = pl.pallas_call(
        body, name=f"attn_bwd{gi}", grid=grid, in_specs=in_specs,
        out_specs=[out_spec] + [whole(b) for b in biases],
        out_shape=[_sds(geom.view_shape(QKV_W), CDT)] + [_sds(b.shape, F32) for b in biases],
        scratch_shapes=scratch,
        input_output_aliases=aliases,
        compiler_params=_cp(("arbitrary",) * len(grid), vmem=56),
    )(*args, *biases, last)
    return geom.unview(dq), [d.reshape(NH, H2, d.shape[2]) for d in (dsa, dsb)]


def _h_bwd(dh, dx2, x, norm_g, scale):
    S = x.shape[0]

    def body(dh_ref, dx2_ref, x_ref, g_ref, sc_ref, gx_ref, dsh_ref, dsc_ref, dng_ref, nat_ref):
        i = pl.program_id(0)

        @pl.when(i == 0)
        def _():
            dsh_ref[...] = jnp.zeros_like(dsh_ref)
            dsc_ref[...] = jnp.zeros_like(dsc_ref)
            dng_ref[...] = jnp.zeros_like(dng_ref)

        for r in range(R16):
            nat_ref[:, r, :] = dh_ref[r]
        dh = jnp.concatenate([nat_ref[l] for l in range(R16)], axis=0)
        g = g_ref[...]
        one_sc = 1.0 + sc_ref[...]
        xv = x_ref[...]
        r1 = lax.rsqrt(jnp.mean(xv * xv, axis=-1, keepdims=True) + EPS)
        xn = xv * r1
        dhx = dh * xn
        dxn = dh * (g * one_sc)
        dx1 = r1 * (dxn - xn * jnp.mean(dxn * xn, axis=-1, keepdims=True))
        gx_ref[...] = dx2_ref[...] + dx1
        dsh_ref[...] += jnp.sum(dh, axis=0, keepdims=True)
        dsc_ref[...] += jnp.sum(dhx * g, axis=0, keepdims=True)
        dng_ref[...] += jnp.sum(dhx * one_sc, axis=0, keepdims=True)

    nat = pl.BlockSpec((TOK, D), lambda i: (i, 0))
    vec = pl.BlockSpec((1, D), lambda i: (0, 0))
    return pl.pallas_call(
        body, name="h_bwd", grid=(S // TOK,),
        in_specs=[pl.BlockSpec((R16, R16, D), lambda i: (0, i, 0)), nat, nat, vec, vec],
        out_specs=[nat, vec, vec, vec],
        out_shape=[_sds((S, D), F32), _sds((1, D), F32), _sds((1, D), F32), _sds((1, D), F32)],
        scratch_shapes=[pltpu.VMEM((R16, R16, D), F32)],
        compiler_params=_cp(("arbitrary",)),
    )(dh, dx2, x, norm_g, scale)


def _local_step(x, target, shift, scale, gate, norm_g, w_in_full, pool_w, pool_scale, wab, wpb, wout, rel_bias,
                final_g):
    st = _local_pre(x, shift, scale, norm_g, rel_bias)
    st = _local_grads(st, target, gate, w_in_full, pool_w, pool_scale, wab, wpb, wout, final_g)
    d_rel_bias = _local_bias_grads(st, _BIAS_GRADS_EARLY + _BIAS_GRADS_LATE)
    gx, dmod, dng = _local_input_grads(st, w_in_full, norm_g, scale)
    return dict(loss=st["loss"], grad_x=gx, dw_in=st["dw_in"], dw_ab=st["dw_ab"], dw_pb=st["dw_pb"],
                dw_out=st["dw_out"], d_norm_g=dng, d_pool_w=st["dpw"], d_pool_scale=st["dps"],
                d_rel_bias=d_rel_bias, d_final_g=st["gfg"], dmod=dmod)


def _local_pre(x, shift, scale, norm_g, rel_bias, dep=None):
    S = x.shape[0]
    LS = S // R16
    geoms = [_Geom(gi, LS) for gi in range(3)]
    idx_np = [[g.bucket_index(sub) for sub in range(2)] for g in geoms]
    buckets = [[tuple(int(b) for b in np.unique(ix) if b >= 0) for ix in pair] for pair in idx_np]
    idxs = [[jnp.asarray(ix) for ix in pair] for pair in idx_np]
    biases = [[_bias_table(idxs[gi][sub], buckets[gi][sub], rel_bias, gi, sub, dep) for sub in range(2)]
              for gi in range(3)]
    h = _norm_mod(x, norm_g, shift, scale, dep).reshape(S, D)
    return dict(geoms=geoms, idxs=idxs, buckets=buckets, biases=biases, x=x, h=h, ht=_tcast(h, "h_transpose"))


def _local_grads(st, target, gate, w_in_full, pool_w, pool_scale, wab, wpb, wout, final_g):
    geoms, idxs, biases, x, h, ht = (st[k] for k in ("geoms", "idxs", "biases", "x", "h", "ht"))
    S = h.shape[0]
    LS = S // R16
    qkv, rest = _proj(h, w_in_full)

    os_, ls_ = [], []
    for gi in range(2):
        o, l = _attn_fwd(qkv, biases[gi], geoms[gi])
        os_.append(o)
        ls_.append(l)
    attn, Lb, ag, agt = _attn_fwd(qkv, biases[2], geoms[2], others=(os_, ls_, rest))
    pt, mp, pg, pgt = _pool_fwd(rest, pool_w, pool_scale)
    merged, mt = _branch_merge(ag, pg, wab, wpb, rest)
    dx2, dmo3, loss, gfg, dgate = _out_loss(merged.reshape(R16, LS, D), wout, x, target, gate, final_g)
    dmo = dmo3.reshape(S, D)

    dya, dyp, drest, dattn, dlb, dmp, dps = _bwd_gates(dmo, wout, wab, wpb, rest, ag, pg, attn, mp, pool_scale)
    dw_out = _mm_tn(mt, dmo, n_total=D, out_blk=lambda j: j, name="dw_out")
    dw_ab = _mm_tn(agt, dya, n_total=D, out_blk=lambda j: j, name="dw_attn_br")
    dw_pb = _mm_tn(pgt, dyp, n_total=D, out_blk=lambda j: j, name="dw_pool_br")
    drest, dpw = _pool_bwd(dmp, pt, pool_w, drest)

    dqkv = None
    dsums = []
    for gi in range(3):
        dqkv, dsum = _attn_bwd(qkv, dattn, Lb, dlb, biases[gi], geoms[gi], dqkv)
        dsums.append(dsum)

    dw_in = _mm_tn(ht, dqkv, n_total=8192, out_blk=lambda j: j, name="dw_in_qkv", with_low=True)
    dw_in, dw_in_low = _mm_tn(ht, drest, n_total=8192, out_blk=_rest_blk, name="dw_in_rest", prev=dw_in, with_low=True)
    return dict(st, loss=loss[0, 0], dw_in=dw_in, dw_in_low=dw_in_low, dw_ab=dw_ab, dw_pb=dw_pb, dw_out=dw_out, dpw=dpw, dps=dps, gfg=gfg,
                dgate=dgate, dsums=dsums, dqkv=dqkv, drest=drest, dx2=dx2)


def _local_bias_grads(st, which, dep=None):
    cols = []
    for gi in range(3):
        got = [_bias_grad(st["dsums"][gi][sub], st["idxs"][gi][sub], st["buckets"][gi][sub], gi, sub, dep)
               for sub in range(2) if (gi, sub) in which]
        cols.append(sum(got) if got else jnp.zeros((32, NH), F32))
    return jnp.concatenate(cols, axis=1)


_BIAS_GRADS_EARLY = ((0, 0), (0, 1), (1, 0), (2, 0))
_BIAS_GRADS_LATE = ((1, 1), (2, 1))


def _dh_h_bwd(dqkv, drest, w, dx2, x, norm_g, scale, dep=None):
    S = x.shape[0]
    dep_specs, dep_args = _after(dep)

    def body(a_ref, b_ref, w_ref, dx2_ref, x_ref, g_ref, sc_ref, *rest):
        gx_ref, dsh_ref, dsc_ref, dng_ref, nat_ref = rest[-5:]
        i = pl.program_id(0)

        @pl.when(i == 0)
        def _():
            dsh_ref[...] = jnp.zeros_like(dsh_ref)
            dsc_ref[...] = jnp.zeros_like(dsc_ref)
            dng_ref[...] = jnp.zeros_like(dng_ref)

        a, b = _ld(a_ref), _ld(b_ref)
        acc = _dot(a[:, 0:512], _w512(w_ref, 0), NT)
        for j in range(1, 9):
            acc = acc + _dot(a[:, 512 * j:512 * (j + 1)], _w512(w_ref, j), NT)
        off = 0
        for lo, hi in REST_COLS:
            for j in range(lo, hi):
                acc = acc + _dot(b[:, off:off + 512], _w512(w_ref, j), NT)
                off += 512
        for r in range(R16):
            nat_ref[:, r, :] = acc[R16 * r:R16 * (r + 1)]
        dh = jnp.concatenate([nat_ref[l] for l in range(R16)], axis=0)
        g = g_ref[...]
        one_sc = 1.0 + sc_ref[...]
        xv = x_ref[...]
        r1 = lax.rsqrt(jnp.mean(xv * xv, axis=-1, keepdims=True) + EPS)
        xn = xv * r1
        dhx = dh * xn
        dxn = dh * (g * one_sc)
        dx1 = r1 * (dxn - xn * jnp.mean(dxn * xn, axis=-1, keepdims=True))
        gx_ref[...] = dx2_ref[...] + dx1
        dsh_ref[...] += jnp.sum(dh, axis=0, keepdims=True)
        dsc_ref[...] += jnp.sum(dhx * g, axis=0, keepdims=True)
        dng_ref[...] += jnp.sum(dhx * one_sc, axis=0, keepdims=True)

    nat = pl.BlockSpec((TOK, D), lambda i: (i, 0))
    vec = pl.BlockSpec((1, D), lambda i: (0, 0))
    slab = lambda W: pl.BlockSpec((R16, R16, W), lambda i: (0, i, 0))
    return pl.pallas_call(
        body, name="dh_h_bwd", grid=(S // TOK,),
        in_specs=[slab(QKV_W), slab(REST_W), _resident((8, D, 1024)), nat, nat, vec, vec] + dep_specs,
        out_specs=[nat, vec, vec, vec],
        out_shape=[_sds((S, D), F32), _sds((1, D), F32), _sds((1, D), F32), _sds((1, D), F32)],
        scratch_shapes=[pltpu.VMEM((R16, R16, D), F32)],
        compiler_params=_cp(("arbitrary",), vmem=56),
    )(dqkv, drest, w, dx2, x, norm_g, scale, *dep_args)


def _local_input_grads(st, w_in_full, norm_g, scale, dep=None):
    S = st["h"].shape[0]
    LS = S // R16
    gx, dsh, dsc, dng = _dh_h_bwd(st["dqkv"].reshape(R16, LS, QKV_W), st["drest"].reshape(R16, LS, REST_W),
                                  w_in_full, st["dx2"], st["x"], norm_g, scale, dep)
    return gx, jnp.concatenate([dsh, dsc, st["dgate"]], axis=1), dng


def _my_place():
    return lax.axis_index("x"), lax.axis_index("y"), lax.axis_index("c")


def _block_of(ref, axis, idx, width):
    if axis is None:
        return ref.at[idx]
    start = pl.multiple_of(idx * width, width)
    if axis == 0:
        return ref.at[pl.ds(start, width)]
    return ref.at[:, pl.ds(start, width)]


_SEM = pl.BlockSpec(memory_space=pltpu.SEMAPHORE)
_EFFECT = pltpu.SideEffectType.DATAFLOW_SIDE_EFFECTING


def _split_start(name, srcs, lands, ncopy, plan, dep=None):
    arrs = list(srcs) + list(lands)
    na, ns = len(arrs), len(srcs)
    dep_specs, dep_args = _after(dep)

    def body(*refs):
        send_sems, recv_sems, token = refs[na + len(dep_args)], refs[na + len(dep_args) + 1], refs[-1]
        for k, (s, d, dev) in enumerate(plan(refs[:ns], refs[ns:na])):
            pltpu.make_async_remote_copy(src_ref=s, dst_ref=d, send_sem=send_sems.at[k], recv_sem=recv_sems.at[k],
                                         device_id=dev, device_id_type=MESH).start()
        token[...] = jnp.zeros_like(token)

    outs = pl.pallas_call(
        body, name=name,
        out_shape=(pltpu.SemaphoreType.DMA((ncopy,)), pltpu.SemaphoreType.DMA((ncopy,)),
                   *[pltpu.HBM(a.shape, a.dtype) for a in arrs], _sds((8, 128), F32)),
        in_specs=[_HBM] * na + dep_specs,
        out_specs=(_SEM, _SEM, *[_HBM] * na, pl.BlockSpec(memory_space=pltpu.VMEM)),
        input_output_aliases={i: 2 + i for i in range(na)},
        compiler_params=pltpu.CompilerParams(has_side_effects=_EFFECT),
    )(*[pltpu.with_memory_space_constraint(a, pltpu.HBM) for a in arrs], *dep_args)
    return (outs[0], outs[1]), list(outs[2:2 + ns]), list(outs[2 + ns:2 + na]), outs[-1]


def _split_wait(name, sems, srcs, lands, plan, after):
    arrs = list(srcs) + list(lands)
    na, ns = len(arrs), len(srcs)
    after = list(after) if isinstance(after, (list, tuple)) else [after]

    def body(*refs):
        send_sems, recv_sems = refs[na], refs[na + 1]
        for k, (s, d, dev) in enumerate(plan(refs[:ns], refs[ns:na])):
            cp = pltpu.make_async_remote_copy(src_ref=s, dst_ref=d, send_sem=send_sems.at[k], recv_sem=recv_sems.at[k],
                                              device_id=dev, device_id_type=MESH)
            cp.wait_send()
            cp.wait_recv()

    outs = pl.pallas_call(
        body, name=name, out_shape=[pltpu.HBM(a.shape, a.dtype) for a in arrs],
        in_specs=[_HBM] * na + [_SEM, _SEM] + [pl.BlockSpec(memory_space=pl.ANY)] * len(after),
        out_specs=[_HBM] * na,
        input_output_aliases={i: i for i in range(na)},
        compiler_params=pltpu.CompilerParams(has_side_effects=_EFFECT),
    )(*arrs, sems[0], sems[1], *after)
    return list(outs[:ns]), list(outs[ns:])


def _gather_chips_plan(layout):
    def plan(src, land):
        x, y, c = _my_place()
        me = 4 * x + 2 * y + c
        mine = [_block_of(land[p], layout[p][0], me, layout[p][1]) for p in range(len(land))]
        return [(mine[p], mine[p], (cx, cy, c)) for p in range(len(land)) for cx, cy in [(1 - x, y), (x, 1 - y)]]
    return plan


def _pair_plan(layout):
    def plan(src, land):
        x, y, c = _my_place()
        n = len(layout)
        return [(_block_of(src[p], layout[p][0], 2 * q + (1 - c), layout[p][1]), land[p].at[q], (x, y, 1 - c))
                for p in range(n) for q in range(4)] + [(src[p], land[p], (x, y, 1 - c)) for p in range(n, len(src))]
    return plan


def _chips_plan(n):
    def plan(src, land):
        x, y, c = _my_place()
        chips = [(1 - x, y), (x, 1 - y), (1 - x, 1 - y)]
        return [(src[p].at[2 * cx + cy], land[p].at[2 * x + y], (cx, cy, c)) for p in range(n) for cx, cy in chips] + \
               [(src[p], land[p].at[2 * x + y], (cx, cy, c)) for p in range(n, len(src)) for cx, cy in chips]
    return plan


def _pair_share(gathered):
    n = len(gathered)

    def body(*refs):
        land = refs[n:2 * n]
        send_sems, recv_sems, fwd_send, fwd_recv = refs[2 * n:]
        x, y, c = _my_place()
        k_dg = 3 - (2 * x + y)
        south = c == 0
        pass_on = 4 * jnp.where(south, 1 - x, x) + 2 * jnp.where(south, y, 1 - y) + c
        to = (jnp.where(south, x, 1 - x), jnp.where(south, 1 - y, y), c)

        def forward(p):
            return pltpu.make_async_remote_copy(
                src_ref=land[p].at[pass_on], dst_ref=land[p].at[pass_on], send_sem=fwd_send.at[p],
                recv_sem=fwd_recv.at[p], device_id=to, device_id_type=MESH)

        def to_sibling(p, k):
            blk = land[p].at[2 * k + c]
            return pltpu.make_async_remote_copy(
                src_ref=blk, dst_ref=blk, send_sem=send_sems.at[4 * p + k], recv_sem=recv_sems.at[4 * p + k],
                device_id=(x, y, 1 - c), device_id_type=MESH)

        for p in range(n):
            forward(p).start()
        for p in range(n):
            for k in range(4):
                @pl.when(k != k_dg)
                def _():
                    to_sibling(p, k).start()
        for p in range(n):
            dg = land[p].at[2 * k_dg + c]
            pltpu.make_async_remote_copy(src_ref=dg, dst_ref=dg, send_sem=fwd_send.at[p], recv_sem=fwd_recv.at[p],
                                         device_id=to, device_id_type=MESH).wait_recv()
            for k in range(4):
                @pl.when(k == k_dg)
                def _():
                    to_sibling(p, k).start()
        for p in range(n):
            forward(p).wait_send()
            for k in range(4):
                to_sibling(p, k).wait_send()
                blk = land[p].at[2 * k + 1 - c]
                pltpu.make_async_remote_copy(
                    src_ref=blk, dst_ref=blk, send_sem=send_sems.at[4 * p + k], recv_sem=recv_sems.at[4 * p + k],
                    device_id=(x, y, 1 - c), device_id_type=MESH).wait_recv()

    return pl.pallas_call(
        body, name="gather_pair_share", in_specs=[_HBM] * n, out_specs=[_HBM] * n,
        out_shape=[_sds(g.shape, g.dtype) for g in gathered],
        input_output_aliases={p: p for p in range(n)},
        scratch_shapes=[pltpu.SemaphoreType.DMA((4 * n,)), pltpu.SemaphoreType.DMA((4 * n,)),
                        pltpu.SemaphoreType.DMA((n,)), pltpu.SemaphoreType.DMA((n,))],
    )(*gathered)


def _place_slot(arrs, slot, n_slots, name, out_dtype=None):
    n = len(arrs)

    def body(s_ref, *refs):
        for p in range(n):
            refs[n + p][...] = refs[p][...].astype(refs[n + p].dtype)

    def spec(a):
        tail = a.shape[-2:]
        return pl.BlockSpec((None,) + tail, lambda i, s_ref: (s_ref[0], 0, 0))

    in_specs = [spec(a) if a.ndim == 3 else pl.BlockSpec(a.shape, lambda i, s_ref: (0, 0)) for a in arrs]
    return pl.pallas_call(
        body, name=name,
        grid_spec=pltpu.PrefetchScalarGridSpec(num_scalar_prefetch=1, grid=(1,), in_specs=in_specs,
                                               out_specs=[spec(a) for a in arrs]),
        out_shape=[_sds((n_slots,) + a.shape[-2:], out_dtype or a.dtype) for a in arrs],
        compiler_params=_cp(("arbitrary",)),
    )(slot, *arrs)


def _cols_from_blocks(w8, name):
    _, R, C = w8.shape

    def body(w_ref, o_ref):
        for e in range(8):
            o_ref[:, C * e:C * (e + 1)] = w_ref[e]

    return pl.pallas_call(body, name=name, out_shape=_sds((R, 8 * C), w8.dtype), compiler_params=_cp())(w8)


def _all_gather_small(srcs, name):
    n = len(srcs)
    vmem = pl.BlockSpec(memory_space=pltpu.VMEM)

    def body(*refs):
        src, out = refs[:n], refs[n:2 * n]
        send_sems, recv_sems = refs[2 * n:]
        x, y, c = _my_place()
        me = 4 * x + 2 * y + c
        flip = lambda v, f: 1 - v if f else v
        peers = [(flip(x, fx), flip(y, fy), flip(c, fc)) for fx in (0, 1) for fy in (0, 1) for fc in (0, 1)][1:]
        cps = []
        for p in range(n):
            out[p][me] = src[p][...]
            for k, peer in enumerate(peers):
                cps.append(pltpu.make_async_remote_copy(
                    src_ref=src[p], dst_ref=out[p].at[me], send_sem=send_sems.at[7 * p + k],
                    recv_sem=recv_sems.at[7 * p + k], device_id=peer, device_id_type=MESH))
        for cp in cps:
            cp.start()
        for cp in cps:
            cp.wait()

    return pl.pallas_call(
        body, name=name, in_specs=[vmem] * n, out_specs=[vmem] * n,
        out_shape=[_sds((8,) + a.shape, a.dtype) for a in srcs],
        scratch_shapes=[pltpu.SemaphoreType.DMA((7 * n,)), pltpu.SemaphoreType.DMA((7 * n,))],
    )(*srcs)


def _adaln_exchange(c, w_ada, b_part):
    vmem = pl.BlockSpec(memory_space=pltpu.VMEM)

    def body(c_ref, w_ref, b_ref, call_ref, mod_ref, part_ref, send_sems, recv_sems):
        x, y, c = _my_place()
        me = 4 * x + 2 * y + c
        flip = lambda v, f: 1 - v if f else v
        peers = [(flip(x, fx), flip(y, fy), flip(c, fc)) for fx in (0, 1) for fy in (0, 1) for fc in (0, 1)][1:]
        call_ref[me] = c_ref[...]
        first = [pltpu.make_async_remote_copy(src_ref=c_ref, dst_ref=call_ref.at[me], send_sem=send_sems.at[k],
                                              recv_sem=recv_sems.at[k], device_id=peer, device_id_type=MESH)
                 for k, peer in enumerate(peers)]
        for cp in first:
            cp.start()
        for cp in first:
            cp.wait()
        c_all = jnp.concatenate([call_ref[j] for j in range(8)], axis=0)
        part_ref[...] = _dot(c_all.astype(CDT), w_ref[...].astype(CDT), NN) + b_ref[...]
        mod_ref[me] = part_ref[pl.ds(me, 1), :]
        second = [pltpu.make_async_remote_copy(
            src_ref=part_ref.at[pl.ds(4 * px + 2 * py + pc, 1)], dst_ref=mod_ref.at[me], send_sem=send_sems.at[7 + k],
            recv_sem=recv_sems.at[7 + k], device_id=(px, py, pc), device_id_type=MESH)
            for k, (px, py, pc) in enumerate(peers)]
        for cp in second:
            cp.start()
        for cp in second:
            cp.wait()

    return pl.pallas_call(
        body, name="adaln_exchange", in_specs=[vmem] * 3, out_specs=[vmem] * 2,
        out_shape=[_sds((8, 1, D), F32), _sds((8, 1, w_ada.shape[1]), F32)],
        scratch_shapes=[pltpu.VMEM((8, w_ada.shape[1]), F32), pltpu.SemaphoreType.DMA((14,)),
                        pltpu.SemaphoreType.DMA((14,))],
        compiler_params=_cp(),
    )(c, w_ada, b_part)


def _pair_add(dw, recv, axis, width, cidx, name):
    if axis == 0:
        tr = min(width, 256)
        C = dw.shape[1]
        grid = (4, width // tr)
        nb = width // tr
        own = pl.BlockSpec((tr, C), lambda q, i, c_ref: ((2 * q + c_ref[0]) * nb + i, 0))
        oth = pl.BlockSpec((None, tr, C), lambda q, i, c_ref: (q, i, 0))
    else:
        R = dw.shape[0]
        tr = min(R, 1024)
        grid = (4, R // tr)
        own = pl.BlockSpec((tr, width), lambda q, i, c_ref: (i, 2 * q + c_ref[0]))
        oth = pl.BlockSpec((None, tr, width), lambda q, i, c_ref: (q, i, 0))

    def body(c_ref, a_ref, b_ref, o_ref):
        o_ref[...] = (a_ref[...] + b_ref[...].astype(F32)).astype(o_ref.dtype)

    return pl.pallas_call(
        body, name=name,
        grid_spec=pltpu.PrefetchScalarGridSpec(num_scalar_prefetch=1, grid=grid, in_specs=[own, oth], out_specs=oth),
        out_shape=_sds(recv.shape, CDT), compiler_params=_cp(("parallel", "parallel")),
    )(cidx, dw, recv)


def _add_pairs(xs, ys, name):
    n = len(xs)

    def body(*refs):
        for p in range(n):
            refs[2 * n + p][...] = refs[p][...] + refs[n + p][...]

    return pl.pallas_call(body, name=name, out_shape=[_sds(a.shape, F32) for a in xs])(*xs, *ys)


def _adam_math(w, g, m, v):
    m = ADAM_B1 * m + (1.0 - ADAM_B1) * g
    v = ADAM_B2 * v + (1.0 - ADAM_B2) * (g * g)
    m_hat = m / (1.0 - ADAM_B1 ** ADAM_STEP)
    v_hat = v / (1.0 - ADAM_B2 ** ADAM_STEP)
    delta = -ADAM_LR * (m_hat / (jnp.sqrt(v_hat) + ADAM_EPS) + ADAM_WD * w)
    return delta, m, v


def _adam(gparts, w, m, v, name):
    P, R, C = gparts.shape
    tr = R if R <= 256 else 256

    def body(g_ref, w_ref, m_ref, v_ref, go_ref, d_ref, mo_ref, vo_ref):
        g = g_ref[0].astype(F32)
        for i in range(1, P):
            g = g + g_ref[i].astype(F32)
        d, mn, vn = _adam_math(w_ref[...], g, m_ref[...], v_ref[...])
        go_ref[...] = g
        d_ref[...] = d
        mo_ref[...] = mn
        vo_ref[...] = vn

    t = pl.BlockSpec((tr, C), lambda i: (i, 0))
    return pl.pallas_call(
        body, name=name, grid=(R // tr,),
        in_specs=[pl.BlockSpec((P, tr, C), lambda i: (0, i, 0)), t, t, t], out_specs=[t, t, t, t],
        out_shape=[_sds((R, C), F32)] * 4, compiler_params=_cp(("parallel",)),
    )(gparts, w, m, v)


def _sum_parts(gparts, name):
    P, R, C = gparts.shape

    def body(g_ref, o_ref):
        g = g_ref[0]
        for i in range(1, P):
            g = g + g_ref[i]
        o_ref[...] = g

    return pl.pallas_call(body, name=name, out_shape=_sds((R, C), F32))(gparts)


def _adam_w_ada(c_t, dm, w, m, v):
    R, C = w.shape

    def body(c_ref, dm_ref, w_ref, m_ref, v_ref, go_ref, d_ref, mo_ref, vo_ref):
        g = c_ref[:, 0:1] * dm_ref[0:1, :]
        for b in range(1, 8):
            g = g + c_ref[:, b:b + 1] * dm_ref[b:b + 1, :]
        d, mn, vn = _adam_math(w_ref[...], g, m_ref[...], v_ref[...])
        go_ref[...] = g
        d_ref[...] = d
        mo_ref[...] = mn
        vo_ref[...] = vn

    return pl.pallas_call(body, name="adam_w_ada", out_shape=[_sds((R, C), F32)] * 4,
                          compiler_params=_cp())(c_t, dm, w, m, v)


def kernel(x, c, norm_g, w_ada, b_ada, w_in, pool_w, pool_scale, w_attn_br, w_pool_br, w_out, rel_bias, final_g, loss_target, m_norm_g, m_w_ada, m_b_ada, m_w_in, m_pool_w, m_pool_scale, m_w_attn_br, m_w_pool_br, m_w_out, m_rel_bias, m_final_g, v_norm_g, v_w_ada, v_b_ada, v_w_in, v_pool_w, v_pool_scale, v_w_attn_br, v_w_pool_br, v_w_out, v_rel_bias, v_final_g):
    S = x.shape[1]
    px, py, pc = _my_place()
    me = 4 * px + 2 * py + pc
    cidx = jnp.reshape(pc, (1,)).astype(jnp.int32)

    layout = [(1, 1024), (1, 128), (1, 128), (0, 128)]
    final_g2 = final_g.reshape(1, D)
    me1 = jnp.reshape(me, (1,)).astype(jnp.int32)

    b_part = lax.dynamic_slice(b_ada, (0, me * 384), (1, 384))
    c_all, mod_all = _adaln_exchange(c, w_ada[0], b_part)
    c_all = c_all.reshape(8, D)
    mod = mod_all.reshape(1, 3 * D)
    shift, scale, gate = mod[:, :D], mod[:, D:2 * D], mod[:, 2 * D:]

    gplan = _gather_chips_plan([(None, 1)] * 4)
    lands = _place_slot([w_in[0], w_attn_br[0], w_pool_br[0], w_out[0]], me1, 8, "place_own_shards", CDT)
    g_sems, _, lands, tok = _split_start("gather_chips_start", [], lands, 8, gplan, dep=mod_all)
    st = _local_pre(x[0], shift, scale, norm_g, rel_bias, dep=tok)
    _, lands = _split_wait("gather_chips_wait", g_sems, [], lands, gplan,
                           [st["ht"]] + [b for pair in st["biases"] for b in pair])
    w_in_full, wab8, wpb8, wout8 = _pair_share(lands)
    wab, wpb = _cols_from_blocks(wab8, "w_attn_br_cols"), _cols_from_blocks(wpb8, "w_pool_br_cols")
    wout = wout8.reshape(D, D)

    st = _local_grads(st, loss_target[0], gate, w_in_full, pool_w[0], pool_scale, wab, wpb, wout, final_g2)

    early = [st["dpw"].reshape(AW, 128), st["dps"], st["gfg"], jnp.broadcast_to(st["loss"], (1, 128))]
    dws = [st["dw_in"], st["dw_ab"], st["dw_pb"], st["dw_out"]]
    sent = [st["dw_in_low"]] + dws[1:] + early
    pplan = _pair_plan(layout)
    pair_shapes = [(4, w, a.shape[1]) if ax == 0 else (4, a.shape[0], w) for a, (ax, w) in zip(dws, layout)]
    p_sems, sent, recv, tok = _split_start(
        "reduce_pair_start", sent,
        [lax.empty(s, a.dtype) for s, a in zip(pair_shapes + [a.shape for a in early], sent)], 20, pplan)
    d_rel_bias = _local_bias_grads(st, _BIAS_GRADS_EARLY, dep=tok)
    sent, recv = _split_wait("reduce_pair_wait", p_sems, sent, recv, pplan, d_rel_bias)
    parts = [_pair_add(dw, r, ax, w, cidx, f"pair_add{i}")
             for i, (dw, r, (ax, w)) in enumerate(zip([dws[0]] + sent[1:4], recv[:4], layout))]
    parts += _add_pairs(sent[4:], recv[4:], "pair_add_smalls")

    slot1 = jnp.reshape(2 * px + py, (1,)).astype(jnp.int32)
    cplan = _chips_plan(4)
    c_sems, parts, lands, tok = _split_start("reduce_chips_start", parts,
                                             _place_slot(parts, slot1, 4, "place_own_partials"), 24, cplan)
    d_rel_bias = d_rel_bias + _local_bias_grads(st, _BIAS_GRADS_LATE, dep=tok)
    grad_x, dmod, d_norm_g = _local_input_grads(st, w_in_full, norm_g, scale, dep=tok)

    dmod_all, ng_all, rb_all = _all_gather_small([dmod, d_norm_g, d_rel_bias], "gather_late")
    o_norm_g = _adam(ng_all, norm_g, m_norm_g, v_norm_g, "adam_norm_g")
    o_rel_bias = _adam(rb_all, rel_bias, m_rel_bias, v_rel_bias, "adam_rel_bias")
    o_b_ada = _adam(dmod_all, b_ada, m_b_ada, v_b_ada, "adam_b_ada")
    dm_mine = lax.dynamic_slice(dmod_all.reshape(8, 3 * D), (0, me * 384), (8, 384))
    o_w_ada = [o[None] for o in _adam_w_ada(c_all.T, dm_mine, w_ada[0], m_w_ada[0], v_w_ada[0])]

    _, (g_in, g_ab, g_pb, g_out, g_pw, g_ps, g_fg, g_loss) = _split_wait(
        "reduce_chips_wait", c_sems, parts, lands, cplan, [o_w_ada[0], o_b_ada[0], o_norm_g[0], o_rel_bias[0]])
    o_w_in = [o[None] for o in _adam(g_in, w_in[0], m_w_in[0], v_w_in[0], "adam_w_in")]
    o_w_ab = [o[None] for o in _adam(g_ab, w_attn_br[0], m_w_attn_br[0], v_w_attn_br[0], "adam_w_attn_br")]
    o_w_pb = [o[None] for o in _adam(g_pb, w_pool_br[0], m_w_pool_br[0], v_w_pool_br[0], "adam_w_pool_br")]
    o_w_out = [o[None] for o in _adam(g_out, w_out[0], m_w_out[0], v_w_out[0], "adam_w_out")]
    o_pool_w = [o.reshape(pool_w.shape) for o in _adam(g_pw, pool_w.reshape(AW, 128), m_pool_w.reshape(AW, 128),
                                                       v_pool_w.reshape(AW, 128), "adam_pool_w")]
    o_pool_scale = _adam(g_ps, pool_scale, m_pool_scale, v_pool_scale, "adam_pool_scale")
    o_final_g = [o.reshape(D) for o in _adam(g_fg, final_g2, m_final_g.reshape(1, D), v_final_g.reshape(1, D),
                                             "adam_final_g")]
    loss = _sum_parts(g_loss, "sum_loss")[0, 0]

    per_w = [o_norm_g, o_w_ada, o_b_ada, o_w_in, o_pool_w, o_pool_scale, o_w_ab, o_w_pb, o_w_out, o_rel_bias, o_final_g]
    outs = [loss, grad_x[None]]
    for k in range(4):
        outs += [o[k] for o in per_w]
    return tuple(outs)
```

```python
import functools
import math

import numpy as np
import jax
import jax.numpy as jnp
from jax import lax
from jax.experimental import pallas as pl
from jax.experimental.pallas import tpu as pltpu

F32 = jnp.float32
CDT = jnp.bfloat16
D = 1024
HD = 64
NH = 8
AW = 512
QKV_W = 4608
REST_W = 3584
R16 = 16
EPS = 1e-6
NEG = -1e30
N_BACK = 128
ADAM_LR, ADAM_B1, ADAM_B2, ADAM_EPS, ADAM_WD, ADAM_STEP = 0.001, 0.9, 0.999, 1e-08, 0.01, 10
MESH = pl.DeviceIdType.MESH
VMEM_MB = 1024 * 1024


def _cp(sem=None, vmem=48, **kw):
    if sem is not None:
        kw["dimension_semantics"] = sem
    return pltpu.CompilerParams(vmem_limit_bytes=vmem * VMEM_MB, **kw)


def _sds(shape, dtype):
    return jax.ShapeDtypeStruct(tuple(shape), dtype)


def _rest_blk(jj):
    return jnp.where(jj == 0, 9, jnp.where(jj == 6, 10, jj + 10))


def _sigmoid(z):
    return 0.5 * jnp.tanh(0.5 * z) + 0.5


def _dot(a, b, dims):
    return lax.dot_general(a, b, (dims, ((), ())), preferred_element_type=F32)


NN = ((1,), (0,))
NT = ((1,), (1,))
TN = ((0,), (0,))


def _ld(ref):
    if len(ref.shape) == 2:
        return ref[...]
    return jnp.concatenate([ref[a] for a in range(ref.shape[0])], axis=0)


def _st(ref, val, cols=None):
    val = val.astype(ref.dtype)
    if len(ref.shape) == 2:
        if cols is None:
            ref[...] = val
        else:
            ref[:, cols] = val
        return
    rows = ref.shape[1]
    for a in range(ref.shape[0]):
        if cols is None:
            ref[a] = val[a * rows:(a + 1) * rows]
        else:
            ref[a, :, cols] = val[a * rows:(a + 1) * rows]


REST_COLS = ((9, 10), (11, 16), (10, 11))
_HBM = pl.BlockSpec(memory_space=pltpu.HBM)


def _resident(shape):
    return pl.BlockSpec(shape, lambda *ids: (0,) * len(shape), pipeline_mode=pl.Buffered(1))


def _w512(w_ref, j):
    return w_ref[j // 2, :, 512 * (j % 2):512 * (j % 2 + 1)]


def _proj(h, w):
    S = h.shape[0]
    tm = 512

    def body(a_ref, w_ref, q_ref, r_ref):
        a = a_ref[...]
        for j in range(9):
            q_ref[:, 512 * j:512 * (j + 1)] = _dot(a, _w512(w_ref, j), NN).astype(q_ref.dtype)
        off = 0
        for lo, hi in REST_COLS:
            for j in range(lo, hi):
                r_ref[:, off:off + 512] = _dot(a, _w512(w_ref, j), NN)
                off += 512

    return pl.pallas_call(
        body, name="proj", grid=(S // tm,),
        in_specs=[pl.BlockSpec((tm, D), lambda i: (i, 0)), _resident((8, D, 1024))],
        out_specs=[pl.BlockSpec((tm, QKV_W), lambda i: (i, 0)), pl.BlockSpec((tm, REST_W), lambda i: (i, 0))],
        out_shape=[_sds((S, QKV_W), CDT), _sds((S, REST_W), F32)],
        compiler_params=_cp(("parallel",), vmem=56),
    )(h, w)


def _after(dep):
    return ([], []) if dep is None else ([pl.BlockSpec(memory_space=pl.ANY)], [dep])


def _dh(dqkv, drest, w, dep=None):
    S = dqkv.shape[0]
    tm = 512
    dep_specs, dep_args = _after(dep)

    def body(a_ref, b_ref, w_ref, *rest):
        o_ref = rest[-1]
        acc = _dot(a_ref[:, 0:512], _w512(w_ref, 0), NT)
        for j in range(1, 9):
            acc = acc + _dot(a_ref[:, 512 * j:512 * (j + 1)], _w512(w_ref, j), NT)
        off = 0
        for lo, hi in REST_COLS:
            for j in range(lo, hi):
                acc = acc + _dot(b_ref[:, off:off + 512], _w512(w_ref, j), NT)
                off += 512
        o_ref[...] = acc

    return pl.pallas_call(
        body, name="dh", grid=(S // tm,),
        in_specs=[pl.BlockSpec((tm, QKV_W), lambda i: (i, 0)), pl.BlockSpec((tm, REST_W), lambda i: (i, 0)),
                  _resident((8, D, 1024))] + dep_specs,
        out_specs=pl.BlockSpec((tm, D), lambda i: (i, 0)),
        out_shape=_sds((S, D), F32),
        compiler_params=_cp(("parallel",), vmem=56),
    )(dqkv, drest, w, *dep_args)


def _mm_tn(at, b, *, n_total, out_blk, name, prev=None, tn=512, with_low=False):
    R, S = at.shape
    nb = b.shape[1] // tn
    n_out = 2 if with_low else 1

    def body(*refs):
        res = _dot(refs[0][...], refs[1][...], NN)
        refs[-n_out][...] = res
        if with_low:
            refs[-1][...] = res.astype(CDT)

    in_specs = [_resident((R, S)), pl.BlockSpec((S, tn), lambda j: (0, j))]
    args = [at, b]
    aliases = {}
    if prev is not None:
        prev = list(prev) if with_low else [prev]
        in_specs += [pl.BlockSpec(memory_space=pl.ANY)] * n_out
        args += prev
        aliases = {2 + k: k for k in range(n_out)}
    out_spec = pl.BlockSpec((R, tn), lambda j: (0, out_blk(j)))
    outs = pl.pallas_call(
        body, name=name, grid=(nb,), in_specs=in_specs,
        out_specs=[out_spec] * n_out,
        out_shape=[_sds((R, n_total), F32), _sds((R, n_total), CDT)][:n_out],
        input_output_aliases=aliases,
        compiler_params=_cp(("parallel",)),
    )(*args)
    return tuple(outs) if with_low else outs[0]


def _tcast(a, name):
    M, C = a.shape
    tm = 512

    def body(a_ref, o_ref):
        o_ref[...] = a_ref[...].astype(F32).T.astype(o_ref.dtype)

    return pl.pallas_call(
        body, name=name, grid=(M // tm,),
        in_specs=[pl.BlockSpec((tm, C), lambda i: (i, 0))],
        out_specs=pl.BlockSpec((C, tm), lambda i: (0, i)),
        out_shape=_sds((C, M), CDT), compiler_params=_cp(("parallel",)),
    )(a)


TOK = R16 * R16


def _swap16():
    i = lax.broadcasted_iota(jnp.int32, (TOK, TOK), 0)
    j = lax.broadcasted_iota(jnp.int32, (TOK, TOK), 1)
    return jnp.where(j == jnp.bitwise_and(i, 15) * 16 + jnp.right_shift(i, 4), 1.0, 0.0).astype(CDT)


def _norm_mod(x, norm_g, shift, scale, dep=None):
    S = x.shape[0]
    dep_specs, dep_args = _after(dep)

    def body(x_ref, g_ref, sh_ref, sc_ref, *rest):
        xv = x_ref[...]
        xn = xv * lax.rsqrt(jnp.mean(xv * xv, axis=-1, keepdims=True) + EPS)
        h = (xn * (g_ref[...] * (1.0 + sc_ref[...])) + sh_ref[...]).astype(CDT)
        _st(rest[-1], _dot(_swap16(), h, NN))

    vec = pl.BlockSpec((1, D), lambda i: (0, 0))
    return pl.pallas_call(
        body, name="norm_mod", grid=(S // TOK,),
        in_specs=[pl.BlockSpec((TOK, D), lambda i: (i, 0)), vec, vec, vec] + dep_specs,
        out_specs=pl.BlockSpec((R16, R16, D), lambda i: (0, i, 0)),
        out_shape=_sds((R16, S // R16, D), CDT), compiler_params=_cp(("parallel",)),
    )(x, norm_g, shift, scale, *dep_args)


def _t5_bucket_np(n):
    nf = np.maximum(n, 1).astype(np.float32)
    large = 16 + (np.log(nf / np.float32(16)) / np.float32(math.log(2048 / 16)) * np.float32(16)).astype(np.int32)
    large = np.minimum(large, 31)
    return np.where(n < 16, n, large).astype(np.int32)


class _Geom:
    def __init__(self, gi, LS):
        self.gi, self.LS = gi, LS
        self.dil = (1, 4, 16)[gi]
        if gi == 0:
            self.lead, self.rows, self.n_tiles, self.has_prev = (R16,), 16, LS // 16, True
            a = np.arange(R16)[:, None]
            ll = np.arange(16)[None, :]
            self.pos = (16 * ll + a).reshape(-1)
        elif gi == 1:
            self.lead, self.rows, self.n_tiles, self.has_prev = (4,), 64, LS // 64, True
            a = np.arange(4)[:, None]
            ll = np.arange(64)[None, :]
            self.pos = (4 * ll + a).reshape(-1)
        else:
            self.lead, self.rows, self.n_tiles, self.has_prev = (), LS, 1, False
            self.pos = np.arange(LS)
        self.T = self.pos.shape[0]
        self.Tk = 2 * self.T if self.has_prev else self.T
        self.outer = {0: (), 1: (4,), 2: (R16,)}[gi]

    def view(self, arr):
        S, C = arr.shape
        if self.gi == 1:
            return arr.reshape(4, 4, self.LS, C)
        return arr.reshape(R16, self.LS, C)

    def view_shape(self, C):
        return (4, 4, self.LS, C) if self.gi == 1 else (R16, self.LS, C)

    def unview(self, arr):
        return arr.reshape(R16 * self.LS, arr.shape[-1])

    def spec(self, W, cblk, tile):
        if self.gi == 0:
            return pl.BlockSpec((R16, 16, W), lambda n: (0, tile(n), cblk))
        if self.gi == 1:
            return pl.BlockSpec((4, None, 64, W), lambda r, n: (0, r, tile(n), cblk))
        return pl.BlockSpec((None, self.LS, W), lambda r: (r, 0, cblk))

    def halves(self):
        R = self.rows
        idx = np.arange(self.T).reshape(-1, R)
        return idx[:, :R // 2].reshape(-1), idx[:, R // 2:].reshape(-1)

    def bucket_index(self, sub):
        lo, hi = self.halves()
        pos = self.pos
        if sub == 0:
            pq = pos[lo]
            pk = np.concatenate([pos[hi] - self.T, pos[lo]]) if self.has_prev else np.concatenate([pos[lo], pos[hi]])
        else:
            pq = pos[hi]
            pk = np.concatenate([pos[lo], pos[hi]])
        dist = pq[:, None] - pk[None, :]
        ok = (dist >= 0) & (dist <= N_BACK)
        bucket = _t5_bucket_np(np.clip(dist, 0, N_BACK) * self.dil)
        return np.where(ok, bucket, -1).astype(np.int32)


def _bias_table(idx, buckets, rel_bias, gi, sub, dep=None):
    T, Tk = idx.shape
    tr = 32
    dep_specs, dep_args = _after(dep)

    def body(idx_ref, rb_ref, *rest):
        h = pl.program_id(0)
        ix = idx_ref[...]
        acc = jnp.full(ix.shape, NEG, F32)
        for b in buckets:
            acc = jnp.where(ix == b, rb_ref[b, gi * NH + h], acc)
        rest[-1][...] = acc

    return pl.pallas_call(
        body, name=f"bias_table{gi}{'ab'[sub]}", grid=(NH, T // tr),
        in_specs=[pl.BlockSpec((tr, Tk), lambda h, i: (i, 0)), pl.BlockSpec(memory_space=pltpu.SMEM)] + dep_specs,
        out_specs=pl.BlockSpec((None, tr, Tk), lambda h, i: (h, i, 0)),
        out_shape=_sds((NH, T, Tk), F32), compiler_params=_cp(("parallel", "parallel")),
    )(idx, rel_bias, *dep_args)


def _bias_grad(dsum, idx, buckets, gi, sub, dep=None):
    _, T, Tk = dsum.shape
    dep_specs, dep_args = _after(dep)

    def body(ds_ref, idx_ref, *rest):
        o_ref, r_ref = rest[-2], rest[-1]
        r_ref[...] = jnp.zeros_like(r_ref)
        ix, ds = idx_ref[...], ds_ref[...]
        for b in buckets:
            r_ref[b:b + 1, :] = jnp.sum(jnp.where(ix == b, ds, 0.0), axis=0, keepdims=True)
        o_ref[...] = jnp.broadcast_to(jnp.sum(r_ref[...], axis=1, keepdims=True), (32, 128))

    out = pl.pallas_call(
        body, name=f"bias_grad{gi}{'ab'[sub]}", grid=(NH,),
        in_specs=[pl.BlockSpec((None, T, Tk), lambda h: (h, 0, 0)), pl.BlockSpec((T, Tk), lambda h: (0, 0))] + dep_specs,
        out_specs=pl.BlockSpec((None, 32, 128), lambda h: (h, 0, 0)),
        out_shape=_sds((NH, 32, 128), F32), scratch_shapes=[pltpu.VMEM((32, Tk), F32)],
        compiler_params=_cp(("parallel",)),
    )(dsum, idx, *dep_args)
    return out[:, :, 0].T


def _prev_mask(n, T, Tk):
    col = lax.broadcasted_iota(jnp.int32, (1, Tk), 1)
    return jnp.where((col < T) & (n == 0), NEG, 0.0).astype(F32)


def _split_rows(v, R):
    A, half = v.shape[0] // R, R // 2
    via_f32 = v.dtype != F32 and half % 16 != 0
    w = v.astype(F32) if via_f32 else v
    lo = jnp.concatenate([w[a * R:a * R + half] for a in range(A)], axis=0)
    hi = jnp.concatenate([w[a * R + half:(a + 1) * R] for a in range(A)], axis=0)
    return (lo.astype(v.dtype), hi.astype(v.dtype)) if via_f32 else (lo, hi)


def _merge_rows(lo, hi, R):
    half = R // 2
    A = lo.shape[0] // half
    return jnp.concatenate([x[a * half:(a + 1) * half] for a in range(A) for x in (lo, hi)], axis=0)


def _attn_fwd(qkv, biases, geom, others=None):
    S = qkv.shape[0]
    gi, T, R, has_prev = geom.gi, geom.T, geom.rows, geom.has_prev
    H2 = T // 2
    qv = geom.view(qkv)
    cq, ck, cv = 3 * gi, 3 * gi + 1, 3 * gi + 2
    assert others is None or not has_prev

    def body(*refs):
        if has_prev:
            q_ref, kp_ref, kc_ref, vp_ref, vc_ref, ba_ref, bb_ref, o_ref, l_ref = refs
        elif others is not None:
            (q_ref, kc_ref, vc_ref, ba_ref, bb_ref, o0_ref, o1_ref, l0_ref, l1_ref, z_ref,
             attn_ref, L_ref, ag_ref, agt_ref) = refs
        else:
            q_ref, kc_ref, vc_ref, ba_ref, bb_ref, o_ref, l_ref = refs
        q_lo, q_hi = _split_rows(_ld(q_ref), R)
        kc_lo, kc_hi = _split_rows(_ld(kc_ref), R)
        vc_lo, vc_hi = _split_rows(_ld(vc_ref), R)
        k_b = jnp.concatenate([kc_lo, kc_hi], axis=0)
        v_b = jnp.concatenate([vc_lo, vc_hi], axis=0)
        if has_prev:
            n = pl.program_id(len(geom.outer))
            k_a = jnp.concatenate([_split_rows(_ld(kp_ref), R)[1], kc_lo], axis=0)
            v_a = jnp.concatenate([_split_rows(_ld(vp_ref), R)[1], vc_lo], axis=0)
            pm = _prev_mask(n, H2, T)
        else:
            k_a, v_a, pm = k_b, v_b, None
        low = lax.broadcasted_iota(jnp.int32, (1, 2 * HD), 1) < HD
        for j in range(NH // 2):
            cols = slice(2 * HD * j, 2 * HD * (j + 1))
            o_half, l_half = [], []
            for q, k, v, b_ref, mask in ((q_lo, k_a, v_a, ba_ref, pm), (q_hi, k_b, v_b, bb_ref, None)):
                q2 = q[:, cols].astype(F32) * 0.125
                k2, v2 = k[:, cols], v[:, cols]
                qs = jnp.concatenate([jnp.where(low, q2, 0.0), jnp.where(low, 0.0, q2)], axis=0).astype(q.dtype)
                s = _dot(qs, k2, NT) + b_ref[j]
                if mask is not None:
                    s = s + mask
                m = jnp.max(s, axis=1, keepdims=True)
                p = jnp.exp(s - m)
                l = jnp.sum(p, axis=1, keepdims=True)
                pv = _dot(p.astype(v.dtype), v2, NN) / l
                lse = m + jnp.log(l)
                o_half.append(jnp.where(low, pv[:H2], pv[H2:]))
                l_half.append(jnp.where(low, lse[:H2], lse[H2:]))
            o2, lc = _merge_rows(o_half[0], o_half[1], R), _merge_rows(l_half[0], l_half[1], R)
            if others is None:
                _st(o_ref, o2, cols)
                _st(l_ref, lc, cols)
                continue
            la, lb = l0_ref[:, cols], l1_ref[:, cols]
            m = jnp.maximum(jnp.maximum(la, lb), lc)
            ea, eb, ec = jnp.exp(la - m), jnp.exp(lb - m), jnp.exp(lc - m)
            den = ea + eb + ec
            attn = (ea * o0_ref[:, cols] + eb * o1_ref[:, cols] + ec * o2) / den
            attn_ref[:, cols] = attn
            L_ref[:, cols] = m + jnp.log(den)
            z = z_ref[:, cols]
            ag = attn * (z * _sigmoid(z))
            ag_ref[:, cols] = ag.astype(ag_ref.dtype)
            agt_ref[cols, :] = ag.T.astype(agt_ref.dtype)

    cur = lambda n: n
    prev = lambda n: jnp.maximum(n - 1, 0)
    if has_prev:
        in_specs = [geom.spec(AW, cq, cur), geom.spec(AW, ck, prev), geom.spec(AW, ck, cur),
                    geom.spec(AW, cv, prev), geom.spec(AW, cv, cur)]
        args = [qv] * 5
    else:
        in_specs = [geom.spec(AW, cq, cur), geom.spec(AW, ck, cur), geom.spec(AW, cv, cur)]
        args = [qv] * 3
    nd = len(geom.outer) + (1 if has_prev else 0)
    biases = [b.reshape(NH // 2, 2 * H2, b.shape[2]) for b in biases]
    in_specs += [pl.BlockSpec(b.shape, lambda *ids: (0, 0, 0)) for b in biases]
    grid = geom.outer + ((geom.n_tiles,) if has_prev else ())
    oshape = geom.view_shape(AW)
    tile = geom.spec(AW, 0, cur)
    if others is None:
        o, l = pl.pallas_call(
            body, name=f"attn_fwd{gi}", grid=grid, in_specs=in_specs, out_specs=[tile, tile],
            out_shape=[_sds(oshape, F32), _sds(oshape, F32)],
            compiler_params=_cp(("arbitrary",) * nd),
        )(*args, *biases)
        return geom.unview(o), geom.unview(l)
    os_, ls_, rest = others
    attn, Lb, ag, agt = pl.pallas_call(
        body, name=f"attn_fwd{gi}_merge", grid=grid, in_specs=in_specs + [tile] * 5,
        out_specs=[tile, tile, tile, pl.BlockSpec((AW, T), lambda r: (0, r))],
        out_shape=[_sds(oshape, F32), _sds(oshape, F32), _sds(oshape, CDT), _sds((AW, S), CDT)],
        compiler_params=_cp(("arbitrary",) * nd),
    )(*args, *biases, *[geom.view(a) for a in (*os_, *ls_, rest)])
    return geom.unview(attn), geom.unview(Lb), geom.unview(ag), agt


def _pool_counts(r, LS, win):
    l = lax.broadcasted_iota(jnp.int32, (LS, 1), 0)
    return jnp.minimum(16 * l + (r + 1), win).astype(F32)


def _pool_fwd(rest, pool_w, pool_scale):
    S = rest.shape[0]
    LS = S // R16
    r3 = rest.reshape(R16, LS, REST_W)
    PG = 128

    def body(u_ref, z_ref, pw_ref, ps_ref, pt_ref, mp_ref, pg_ref, pgt_ref, e_ref):
        g = pl.program_id(0)
        win = jnp.left_shift(2, g)
        row = lax.broadcasted_iota(jnp.int32, (LS, PG), 0)
        for r in range(R16):
            e_ref[R16 + r] = u_ref[r]
        for r in range(1, R16):
            e_ref[r] = jnp.where(row == 0, 0.0, pltpu.roll(u_ref[r], 1, 0))

        def double(s):
            for i in range(2 * R16 - 1, 2 * s - 1, -1):
                e_ref[i] = e_ref[i] + e_ref[i - s]

        double(1)
        for k in (1, 2, 3):
            pl.when(g >= k)(functools.partial(double, 2 ** k))
        pw = pw_ref[...].astype(CDT)
        for r in range(R16):
            u = u_ref[r]
            pooled = e_ref[R16 + r] / _pool_counts(r, LS, win) - u
            pc = pooled.astype(CDT)
            mp = _dot(pc, pw, NN)
            z = z_ref[r]
            pg = (mp * ps_ref[...]) * (z * _sigmoid(z))
            mp_ref[r] = mp
            pg_ref[r] = pg.astype(pg_ref.dtype)
            pt_ref[:, r * LS:(r + 1) * LS] = pooled.T.astype(pt_ref.dtype)
            pgt_ref[:, r * LS:(r + 1) * LS] = pg.T.astype(pgt_ref.dtype)

    col = lambda off: pl.BlockSpec((R16, LS, PG), lambda g: (0, 0, off + g))
    tsp = pl.BlockSpec((PG, S), lambda g: (g, 0))
    pt, mp, pg, pgt = pl.pallas_call(
        body, name="pool_fwd", grid=(4,),
        in_specs=[col(24), col(4), pl.BlockSpec((None, PG, PG), lambda g: (g, 0, 0)),
                  pl.BlockSpec((1, PG), lambda g: (0, g))],
        out_specs=[tsp, col(0), col(0), tsp],
        out_shape=[_sds((AW, S), CDT), _sds((R16, LS, AW), F32), _sds((R16, LS, AW), CDT), _sds((AW, S), CDT)],
        scratch_shapes=[pltpu.VMEM((2 * R16, LS, PG), F32)],
        compiler_params=_cp(("parallel",)),
    )(r3, r3, pool_w, pool_scale)
    return pt, mp.reshape(S, AW), pg.reshape(S, AW), pgt


def _branch_merge(ag, pg, wab, wpb, rest):
    S = ag.shape[0]
    tm, tn = 512, 512

    def body(ag_ref, pg_ref, wa_ref, wp_ref, ga_ref, gp_ref, m_ref, mt_ref):
        ya = _dot(ag_ref[...], wa_ref[...], NN)
        yp = _dot(pg_ref[...], wp_ref[...], NN)
        mg = _sigmoid(ga_ref[...]) * ya + _sigmoid(gp_ref[...]) * yp
        m_ref[...] = mg.astype(m_ref.dtype)
        mt_ref[...] = mg.T.astype(mt_ref.dtype)

    a = pl.BlockSpec((tm, AW), lambda i, j: (i, 0))
    w = pl.BlockSpec((AW, tn), lambda i, j: (0, j))
    o = pl.BlockSpec((tm, tn), lambda i, j: (i, j))
    return pl.pallas_call(
        body, name="branch_merge", grid=(S // tm, D // tn),
        in_specs=[a, a, w, w, pl.BlockSpec((tm, tn), lambda i, j: (i, 2 + j)),
                  pl.BlockSpec((tm, tn), lambda i, j: (i, 4 + j))],
        out_specs=[o, pl.BlockSpec((tn, tm), lambda i, j: (j, i))],
        out_shape=[_sds((S, D), CDT), _sds((D, S), CDT)],
        compiler_params=_cp(("parallel", "parallel")),
    )(ag, pg, wab, wpb, rest, rest)


def _merge_out_loss(ag, pg, rest, wab, wpb, wout, x, target, gate, final_g):
    S = x.shape[0]
    LS = S // R16

    def body(ag_ref, pg_ref, ga_ref, gp_ref, wa_ref, wp_ref, w_ref, x_ref, t_ref, gate_ref, fg_ref,
             dx2_ref, dmo_ref, loss_ref, gfg_ref, dgate_ref, dw_ref):
        i = pl.program_id(0)

        @pl.when(i == 0)
        def _():
            loss_ref[...] = jnp.zeros_like(loss_ref)
            gfg_ref[...] = jnp.zeros_like(gfg_ref)
            dgate_ref[...] = jnp.zeros_like(dgate_ref)
            dw_ref[...] = jnp.zeros_like(dw_ref)

        swap = _swap16()
        ya = _dot(_ld(ag_ref), wa_ref[...], NN)
        yp = _dot(_ld(pg_ref), wp_ref[...], NN)
        mg = (_sigmoid(_ld(ga_ref)) * ya + _sigmoid(_ld(gp_ref)) * yp).astype(CDT)
        m_nat = _dot(swap, mg, NN)
        mo = _dot(m_nat.astype(CDT), w_ref[...], NN)
        gate, fg = gate_ref[...], fg_ref[...]
        x2 = x_ref[...] + gate * mo
        r2 = lax.rsqrt(jnp.mean(x2 * x2, axis=-1, keepdims=True) + EPS)
        xn2 = x2 * r2
        diff = xn2 * fg - t_ref[...]
        loss = jnp.sum(jnp.sum(diff * diff, axis=-1, keepdims=True), axis=0, keepdims=True)
        dy = diff * (1.0 / D)
        dxn2 = dy * fg
        dx2 = r2 * (dxn2 - xn2 * jnp.mean(dxn2 * xn2, axis=-1, keepdims=True))
        dx2_ref[...] = dx2
        dmo = (dx2 * gate).astype(CDT)
        _st(dmo_ref, _dot(swap, dmo, NN))
        loss_ref[...] += jnp.broadcast_to(loss * (0.5 / D), loss_ref.shape)
        gfg_ref[...] += jnp.sum(dy * xn2, axis=0, keepdims=True)
        dgate_ref[...] += jnp.sum(dx2 * mo, axis=0, keepdims=True)
        dw_ref[...] += _dot(m_nat.T.astype(CDT), dmo, NN)

    def slab(w, cblk):
        return pl.BlockSpec((R16, R16, w), lambda i: (0, i, cblk))

    nat = pl.BlockSpec((TOK, D), lambda i: (i, 0))
    vec = pl.BlockSpec((1, D), lambda i: (0, 0))
    r3 = rest.reshape(R16, LS, REST_W)
    return pl.pallas_call(
        body, name="merge_out_loss", grid=(S // TOK,),
        in_specs=[slab(AW, 0), slab(AW, 0), slab(D, 1), slab(D, 2), _resident((AW, D)), _resident((AW, D)),
                  _resident((D, D)), nat, nat, vec, vec],
        out_specs=[nat, slab(D, 0), pl.BlockSpec((1, 128), lambda i: (0, 0)), vec, vec,
                   pl.BlockSpec((D, D), lambda i: (0, 0))],
        out_shape=[_sds((S, D), F32), _sds((R16, LS, D), CDT), _sds((1, 128), F32), _sds((1, D), F32),
                   _sds((1, D), F32), _sds((D, D), F32)],
        compiler_params=_cp(("arbitrary",)),
    )(ag.reshape(R16, LS, AW), pg.reshape(R16, LS, AW), r3, r3, wab, wpb, wout, x, target, gate, final_g)


def _out_loss(merged, wout, x, target, gate, final_g):
    S = x.shape[0]

    def body(m_ref, w_ref, x_ref, t_ref, gate_ref, fg_ref, dx2_ref, dmo_ref, loss_ref, gfg_ref, dgate_ref):
        i = pl.program_id(0)

        @pl.when(i == 0)
        def _():
            loss_ref[...] = jnp.zeros_like(loss_ref)
            gfg_ref[...] = jnp.zeros_like(gfg_ref)
            dgate_ref[...] = jnp.zeros_like(dgate_ref)

        swap = _swap16()
        mo = _dot(_dot(swap, _ld(m_ref), NN).astype(CDT), w_ref[...], NN)
        gate, fg = gate_ref[...], fg_ref[...]
        x2 = x_ref[...] + gate * mo
        r2 = lax.rsqrt(jnp.mean(x2 * x2, axis=-1, keepdims=True) + EPS)
        xn2 = x2 * r2
        diff = xn2 * fg - t_ref[...]
        loss = jnp.sum(jnp.sum(diff * diff, axis=-1, keepdims=True), axis=0, keepdims=True)
        dy = diff * (1.0 / D)
        dxn2 = dy * fg
        dx2 = r2 * (dxn2 - xn2 * jnp.mean(dxn2 * xn2, axis=-1, keepdims=True))
        dx2_ref[...] = dx2
        _st(dmo_ref, _dot(swap, (dx2 * gate).astype(CDT), NN))
        loss_ref[...] += jnp.broadcast_to(loss * (0.5 / D), loss_ref.shape)
        gfg_ref[...] += jnp.sum(dy * xn2, axis=0, keepdims=True)
        dgate_ref[...] += jnp.sum(dx2 * mo, axis=0, keepdims=True)

    slab = pl.BlockSpec((R16, R16, D), lambda i: (0, i, 0))
    nat = pl.BlockSpec((TOK, D), lambda i: (i, 0))
    vec = pl.BlockSpec((1, D), lambda i: (0, 0))
    return pl.pallas_call(
        body, name="out_loss", grid=(S // TOK,),
        in_specs=[slab, _resident((D, D)), nat, nat, vec, vec],
        out_specs=[nat, slab, pl.BlockSpec((1, 128), lambda i: (0, 0)), vec, vec],
        out_shape=[_sds((S, D), F32), _sds((R16, S // R16, D), CDT), _sds((1, 128), F32), _sds((1, D), F32),
                   _sds((1, D), F32)],
        compiler_params=_cp(("arbitrary",)),
    )(merged, wout, x, target, gate, final_g)


def _bwd_gates(dmo, wout, wab, wpb, rest, ag, pg, attn, mp, pool_scale):
    S = dmo.shape[0]
    tm = 256

    def body(dmo_ref, wo_ref, wa_ref, wp_ref, za_ref, zp_ref, ga0, ga1, gp0, gp1, ag_ref, pg_ref, at_ref, mp_ref,
             ps_ref, dya_ref, dyp_ref, dr_ref, dat_ref, dl_ref, dmp_ref, dps_ref):
        i = pl.program_id(0)

        @pl.when(i == 0)
        def _():
            dps_ref[...] = jnp.zeros_like(dps_ref)

        dm = _dot(dmo_ref[...], wo_ref[...], NT)
        sa = _sigmoid(jnp.concatenate([ga0[...], ga1[...]], axis=1))
        sp = _sigmoid(jnp.concatenate([gp0[...], gp1[...]], axis=1))
        dya = (dm * sa).astype(CDT)
        dyp = (dm * sp).astype(CDT)
        dya_ref[...] = dya
        dyp_ref[...] = dyp
        ya = _dot(ag_ref[...], wa_ref[...], NN)
        yp = _dot(pg_ref[...], wp_ref[...], NN)
        dr_ref[:, 1024:2048] = (dm * ya * sa * (1.0 - sa)).astype(dr_ref.dtype)
        dr_ref[:, 2048:3072] = (dm * yp * sp * (1.0 - sp)).astype(dr_ref.dtype)
        da = _dot(dya, wa_ref[...], NT)
        dp = _dot(dyp, wp_ref[...], NT)
        za = za_ref[...]
        sga = _sigmoid(za)
        attn = at_ref[...]
        dattn = da * (za * sga)
        dr_ref[:, 0:512] = (da * attn * (sga * (1.0 + za * (1.0 - sga)))).astype(dr_ref.dtype)
        dat_ref[...] = dattn.astype(dat_ref.dtype)
        prod = dattn * attn
        low = lax.broadcasted_iota(jnp.int32, (1, 2 * HD), 1) < HD
        for j in range(NH // 2):
            x2 = prod[:, 2 * HD * j:2 * HD * (j + 1)]
            s0 = jnp.sum(jnp.where(low, x2, 0.0), axis=1, keepdims=True)
            s1 = jnp.sum(jnp.where(low, 0.0, x2), axis=1, keepdims=True)
            dl_ref[:, 2 * HD * j:2 * HD * (j + 1)] = jnp.where(low, s0, s1)
        zp = zp_ref[...]
        sgp = _sigmoid(zp)
        mpre = mp_ref[...]
        ps = ps_ref[...]
        dmixed = dp * (zp * sgp)
        dr_ref[:, 512:1024] = (dp * (mpre * ps) * (sgp * (1.0 + zp * (1.0 - sgp)))).astype(dr_ref.dtype)
        dps_ref[...] += jnp.sum(dmixed * mpre, axis=0, keepdims=True)
        dmp_ref[...] = (dmixed * ps).astype(dmp_ref.dtype)

    full = lambda shape: pl.BlockSpec(shape, lambda i: (0, 0))
    tD = pl.BlockSpec((tm, D), lambda i: (i, 0))
    tA = pl.BlockSpec((tm, AW), lambda i: (i, 0))
    rb = lambda c: pl.BlockSpec((tm, AW), lambda i: (i, c))
    return pl.pallas_call(
        body, name="bwd_gates", grid=(S // tm,),
        in_specs=[tD, full((D, D)), full((AW, D)), full((AW, D)), rb(0), rb(1), rb(2), rb(3), rb(4), rb(5),
                  tA, tA, tA, tA, full((1, AW))],
        out_specs=[tD, tD, pl.BlockSpec((tm, 3072), lambda i: (i, 0)), tA, tA, tA, full((1, AW))],
        out_shape=[_sds((S, D), CDT), _sds((S, D), CDT), _sds((S, REST_W), CDT), _sds((S, AW), CDT),
                   _sds((S, AW), F32), _sds((S, AW), CDT), _sds((1, AW), F32)],
        compiler_params=_cp(("arbitrary",), vmem=56),
    )(dmo, wout, wab, wpb, rest, rest, rest, rest, rest, rest, ag, pg, attn, mp, pool_scale)


def _pool_bwd(dmp, pt, pool_w, drest):
    S = dmp.shape[0]
    LS = S // R16
    PG = 128
    d3 = dmp.reshape(R16, LS, AW)
    dr3 = drest.reshape(R16, LS, REST_W)

    def body(d_ref, pt_ref, pw_ref, dr_in, du_ref, dpw_ref, g_ref, s_ref):
        del dr_in
        g = pl.program_id(0)
        win = jnp.left_shift(2, g)
        row = lax.broadcasted_iota(jnp.int32, (LS, PG), 0)
        pw = pw_ref[...].astype(CDT)
        dpw = jnp.zeros((PG, PG), F32)
        for r in range(R16):
            dm = d_ref[r]
            dpw = dpw + _dot(pt_ref[:, r * LS:(r + 1) * LS], dm, NN)
            dpooled = _dot(dm, pw, NT)
            g_ref[r] = dpooled
            sc = dpooled / _pool_counts(r, LS, win)
            s_ref[r] = sc
            if r < R16 - 1:
                s_ref[R16 + r] = jnp.where(row == LS - 1, 0.0, pltpu.roll(sc, LS - 1, 0))
        dpw_ref[...] = dpw

        def double(s):
            for i in range(0, 2 * R16 - 2 * s):
                s_ref[i] = s_ref[i] + s_ref[i + s]

        double(1)
        for k in (1, 2, 3):
            pl.when(g >= k)(functools.partial(double, 2 ** k))
        for r in range(R16):
            du_ref[r] = (s_ref[r] - g_ref[r]).astype(du_ref.dtype)

    du, dpw = pl.pallas_call(
        body, name="pool_bwd", grid=(4,),
        in_specs=[pl.BlockSpec((R16, LS, PG), lambda g: (0, 0, g)), pl.BlockSpec((PG, S), lambda g: (g, 0)),
                  pl.BlockSpec((None, PG, PG), lambda g: (g, 0, 0)), pl.BlockSpec(memory_space=pl.ANY)],
        out_specs=[pl.BlockSpec((R16, LS, PG), lambda g: (0, 0, 24 + g)),
                   pl.BlockSpec((None, PG, PG), lambda g: (g, 0, 0))],
        out_shape=[_sds((R16, LS, REST_W), CDT), _sds((4, PG, PG), F32)],
        scratch_shapes=[pltpu.VMEM((R16, LS, PG), F32), pltpu.VMEM((2 * R16, LS, PG), F32)],
        input_output_aliases={3: 0},
        compiler_params=_cp(("parallel",)),
    )(d3, pt, pool_w, dr3)
    return du.reshape(S, REST_W), dpw


def _attn_bwd(qkv, dattn, Lb, dlb, biases, geom, dqkv_prev):
    S = qkv.shape[0]
    gi, T, R, has_prev, N = geom.gi, geom.T, geom.rows, geom.has_prev, geom.n_tiles
    H2 = T // 2
    nd_outer = len(geom.outer)
    qv = geom.view(qkv)
    cq, ck, cv = 3 * gi, 3 * gi + 1, 3 * gi + 2
    W3 = 3 * AW

    def heads(q_ref, kp_ref, kc_ref, vp_ref, vc_ref, do_ref, L_ref, dl_ref, ba_ref, bb_ref, pm, dsa_ref, dsb_ref, emit):
        q_lo, q_hi = _split_rows(_ld(q_ref), R)
        do_lo, do_hi = _split_rows(_ld(do_ref), R)
        L_lo, L_hi = _split_rows(_ld(L_ref), R)
        dl_lo, dl_hi = _split_rows(_ld(dl_ref), R)
        kc_lo, kc_hi = _split_rows(_ld(kc_ref), R)
        vc_lo, vc_hi = _split_rows(_ld(vc_ref), R)
        k_b = jnp.concatenate([kc_lo, kc_hi], axis=0)
        v_b = jnp.concatenate([vc_lo, vc_hi], axis=0)
        if has_prev:
            k_a = jnp.concatenate([_split_rows(_ld(kp_ref), R)[1], kc_lo], axis=0)
            v_a = jnp.concatenate([_split_rows(_ld(vp_ref), R)[1], vc_lo], axis=0)
        else:
            k_a, v_a = k_b, v_b
        low = lax.broadcasted_iota(jnp.int32, (1, 2 * HD), 1) < HD
        for j in range(NH // 2):
            sl = slice(2 * HD * j, 2 * HD * (j + 1))
            res = []
            for q, k, v, do, L, dl, b_ref, mask, ds_ref in (
                    (q_lo, k_a, v_a, do_lo, L_lo, dl_lo, ba_ref, pm, dsa_ref),
                    (q_hi, k_b, v_b, do_hi, L_hi, dl_hi, bb_ref, None, dsb_ref)):
                q2 = q[:, sl].astype(F32) * 0.125
                do2 = do[:, sl].astype(F32)
                k2, v2 = k[:, sl], v[:, sl]
                L2, dl2 = L[:, sl], dl[:, sl]
                qs = jnp.concatenate([jnp.where(low, q2, 0.0), jnp.where(low, 0.0, q2)], axis=0).astype(q.dtype)
                dom = jnp.concatenate([jnp.where(low, do2, 0.0), jnp.where(low, 0.0, do2)], axis=0).astype(q.dtype)
                Lc = jnp.concatenate([L2[:, 0:1], L2[:, HD:HD + 1]], axis=0)
                dlc = jnp.concatenate([dl2[:, 0:1], dl2[:, HD:HD + 1]], axis=0)
                s = _dot(qs, k2, NT) + b_ref[j]
                if mask is not None:
                    s = s + mask
                p = jnp.exp(s - Lc)
                ds = p * (_dot(dom, v2, NT) - dlc)
                ds_ref[j] += ds
                dsc = ds.astype(q.dtype)
                dq_st = _dot(dsc, k2, NN)
                res.append((jnp.where(low, dq_st[:H2], dq_st[H2:]) * 0.125, _dot(dsc, qs, TN),
                            _dot(p.astype(q.dtype), dom, TN)))
            (dq_lo, dk_a, dv_a), (dq_hi, dk_b, dv_b) = res
            dq = _merge_rows(dq_lo, dq_hi, R)
            if has_prev:
                emit(sl, dq, _merge_rows(dk_a[H2:] + dk_b[:H2], dk_b[H2:], R),
                     _merge_rows(dv_a[H2:] + dv_b[:H2], dv_b[H2:], R), dk_a[:H2], dv_a[:H2])
            else:
                emit(sl, dq, _merge_rows(dk_a[:H2] + dk_b[:H2], dk_a[H2:] + dk_b[H2:], R),
                     _merge_rows(dv_a[:H2] + dv_b[:H2], dv_a[H2:] + dv_b[H2:], R), None, None)

    def first_step():
        ok = pl.program_id(nd_outer) == 0 if has_prev else pl.program_id(0) == 0
        for a in range(nd_outer if has_prev else 0):
            ok = ok & (pl.program_id(a) == 0)
        return ok

    if has_prev:
        def body(q_ref, kp_ref, kc_ref, vp_ref, vc_ref, do_ref, L_ref, dl_ref, ba_ref, bb_ref, prev_in, out_ref,
                 dsa_ref, dsb_ref, hold, new, pp):
            del prev_in
            s_id = pl.program_id(nd_outer)

            @pl.when(first_step())
            def _():
                dsa_ref[...] = jnp.zeros_like(dsa_ref)
                dsb_ref[...] = jnp.zeros_like(dsb_ref)

            @pl.when(s_id < N)
            def _():
                def emit(sl, dq, dk, dv, dk_prev, dv_prev):
                    new[:, sl] = dq
                    new[:, slice(AW + sl.start, AW + sl.stop)] = dk
                    new[:, slice(2 * AW + sl.start, 2 * AW + sl.stop)] = dv
                    pp[:, sl] = dk_prev
                    pp[:, slice(AW + sl.start, AW + sl.stop)] = dv_prev

                heads(q_ref, kp_ref, kc_ref, vp_ref, vc_ref, do_ref, L_ref, dl_ref, ba_ref, bb_ref,
                      _prev_mask(s_id, H2, T), dsa_ref, dsb_ref, emit)

            @pl.when((s_id >= 1) & (s_id < N))
            def _():
                _st(out_ref, hold[:, 0:AW], slice(0, AW))
                late = _merge_rows(jnp.zeros((H2, 2 * AW), F32), pp[...], R)
                _st(out_ref, hold[:, AW:W3] + late, slice(AW, W3))

            @pl.when(s_id == N)
            def _():
                _st(out_ref, hold[...])

            @pl.when(s_id < N)
            def _():
                hold[...] = new[...]

        cur = lambda n: jnp.minimum(n, N - 1)
        prev = lambda n: jnp.clip(n - 1, 0, N - 1)
        delayed = lambda n: jnp.maximum(n - 1, 0)
        in_specs = [geom.spec(AW, cq, cur), geom.spec(AW, ck, prev), geom.spec(AW, ck, cur),
                    geom.spec(AW, cv, prev), geom.spec(AW, cv, cur),
                    geom.spec(AW, 0, cur), geom.spec(AW, 0, cur), geom.spec(AW, 0, cur)]
        args = [qv] * 5 + [geom.view(dattn), geom.view(Lb), geom.view(dlb)]
        out_spec = geom.spec(W3, gi, delayed)
        grid = geom.outer + (N + 1,)
        scratch = [pltpu.VMEM((T, W3), F32), pltpu.VMEM((T, W3), F32), pltpu.VMEM((H2, 2 * AW), F32)]
    else:
        def body(q_ref, kc_ref, vc_ref, do_ref, L_ref, dl_ref, ba_ref, bb_ref, prev_in, out_ref, dsa_ref, dsb_ref):
            del prev_in

            @pl.when(first_step())
            def _():
                dsa_ref[...] = jnp.zeros_like(dsa_ref)
                dsb_ref[...] = jnp.zeros_like(dsb_ref)

            def emit(sl, dq, dk, dv, dk_prev, dv_prev):
                _st(out_ref, dq, sl)
                _st(out_ref, dk, slice(AW + sl.start, AW + sl.stop))
                _st(out_ref, dv, slice(2 * AW + sl.start, 2 * AW + sl.stop))

            heads(q_ref, None, kc_ref, None, vc_ref, do_ref, L_ref, dl_ref, ba_ref, bb_ref, None, dsa_ref, dsb_ref,
                  emit)

        cur = lambda n: n
        in_specs = [geom.spec(AW, cq, cur), geom.spec(AW, ck, cur), geom.spec(AW, cv, cur),
                    geom.spec(AW, 0, cur), geom.spec(AW, 0, cur), geom.spec(AW, 0, cur)]
        args = [qv] * 3 + [geom.view(dattn), geom.view(Lb), geom.view(dlb)]
        out_spec = geom.spec(W3, gi, cur)
        grid = geom.outer
        scratch = []

    whole = lambda b: pl.BlockSpec(b.shape, lambda *ids: (0, 0, 0))
    biases = [b.reshape(NH // 2, 2 * H2, b.shape[2]) for b in biases]
    in_specs += [whole(b) for b in biases]
    in_specs.append(pl.BlockSpec(memory_space=pl.ANY))
    if dqkv_prev is None:
        last, aliases = biases[0], {}
    else:
        last, aliases = geom.view(dqkv_prev), {len(in_specs) - 1: 0}
    dq, dsa, dsb = pl.pallas_call(
        body, name=f"attn_bwd{gi}", grid=grid, in_specs=in_specs,
        out_specs=[out_spec] + [whole(b) for b in biases],
        out_shape=[_sds(geom.view_shape(QKV_W), CDT)] + [_sds(b.shape, F32) for b in biases],
        scratch_shapes=scratch,
        input_output_aliases=aliases,
        compiler_params=_cp(("arbitrary",) * len(grid), vmem=56),
    )(*args, *biases, last)
    return geom.unview(dq), [d.reshape(NH, H2, d.shape[2]) for d in (dsa, dsb)]


def _h_bwd(dh, dx2, x, norm_g, scale):
    S = x.shape[0]

    def body(dh_ref, dx2_ref, x_ref, g_ref, sc_ref, gx_ref, dsh_ref, dsc_ref, dng_ref, nat_ref):
        i = pl.program_id(0)

        @pl.when(i == 0)
        def _():
            dsh_ref[...] = jnp.zeros_like(dsh_ref)
            dsc_ref[...] = jnp.zeros_like(dsc_ref)
            dng_ref[...] = jnp.zeros_like(dng_ref)

        for r in range(R16):
            nat_ref[:, r, :] = dh_ref[r]
        dh = jnp.concatenate([nat_ref[l] for l in range(R16)], axis=0)
        g = g_ref[...]
        one_sc = 1.0 + sc_ref[...]
        xv = x_ref[...]
        r1 = lax.rsqrt(jnp.mean(xv * xv, axis=-1, keepdims=True) + EPS)
        xn = xv * r1
        dhx = dh * xn
        dxn = dh * (g * one_sc)
        dx1 = r1 * (dxn - xn * jnp.mean(dxn * xn, axis=-1, keepdims=True))
        gx_ref[...] = dx2_ref[...] + dx1
        dsh_ref[...] += jnp.sum(dh, axis=0, keepdims=True)
        dsc_ref[...] += jnp.sum(dhx * g, axis=0, keepdims=True)
        dng_ref[...] += jnp.sum(dhx * one_sc, axis=0, keepdims=True)

    nat = pl.BlockSpec((TOK, D), lambda i: (i, 0))
    vec = pl.BlockSpec((1, D), lambda i: (0, 0))
    return pl.pallas_call(
        body, name="h_bwd", grid=(S // TOK,),
        in_specs=[pl.BlockSpec((R16, R16, D), lambda i: (0, i, 0)), nat, nat, vec, vec],
        out_specs=[nat, vec, vec, vec],
        out_shape=[_sds((S, D), F32), _sds((1, D), F32), _sds((1, D), F32), _sds((1, D), F32)],
        scratch_shapes=[pltpu.VMEM((R16, R16, D), F32)],
        compiler_params=_cp(("arbitrary",)),
    )(dh, dx2, x, norm_g, scale)


def _local_step(x, target, shift, scale, gate, norm_g, w_in_full, pool_w, pool_scale, wab, wpb, wout, rel_bias,
                final_g):
    st = _local_pre(x, shift, scale, norm_g, rel_bias)
    st = _local_grads(st, target, gate, w_in_full, pool_w, pool_scale, wab, wpb, wout, final_g)
    d_rel_bias = _local_bias_grads(st, _BIAS_GRADS_EARLY + _BIAS_GRADS_LATE)
    gx, dmod, dng = _local_input_grads(st, w_in_full, norm_g, scale)
    return dict(loss=st["loss"], grad_x=gx, dw_in=st["dw_in"], dw_ab=st["dw_ab"], dw_pb=st["dw_pb"],
                dw_out=st["dw_out"], d_norm_g=dng, d_pool_w=st["dpw"], d_pool_scale=st["dps"],
                d_rel_bias=d_rel_bias, d_final_g=st["gfg"], dmod=dmod)


def _local_pre(x, shift, scale, norm_g, rel_bias, dep=None):
    S = x.shape[0]
    LS = S // R16
    geoms = [_Geom(gi, LS) for gi in range(3)]
    idx_np = [[g.bucket_index(sub) for sub in range(2)] for g in geoms]
    buckets = [[tuple(int(b) for b in np.unique(ix) if b >= 0) for ix in pair] for pair in idx_np]
    idxs = [[jnp.asarray(ix) for ix in pair] for pair in idx_np]
    biases = [[_bias_table(idxs[gi][sub], buckets[gi][sub], rel_bias, gi, sub, dep) for sub in range(2)]
              for gi in range(3)]
    h = _norm_mod(x, norm_g, shift, scale, dep).reshape(S, D)
    return dict(geoms=geoms, idxs=idxs, buckets=buckets, biases=biases, x=x, h=h, ht=_tcast(h, "h_transpose"))


def _local_grads(st, target, gate, w_in_full, pool_w, pool_scale, wab, wpb, wout, final_g):
    geoms, idxs, biases, x, h, ht = (st[k] for k in ("geoms", "idxs", "biases", "x", "h", "ht"))
    S = h.shape[0]
    LS = S // R16
    qkv, rest = _proj(h, w_in_full)

    os_, ls_ = [], []
    for gi in range(2):
        o, l = _attn_fwd(qkv, biases[gi], geoms[gi])
        os_.append(o)
        ls_.append(l)
    attn, Lb, ag, agt = _attn_fwd(qkv, biases[2], geoms[2], others=(os_, ls_, rest))
    pt, mp, pg, pgt = _pool_fwd(rest, pool_w, pool_scale)
    dx2, dmo3, loss, gfg, dgate, dw_out = _merge_out_loss(ag, pg, rest, wab, wpb, wout, x, target, gate, final_g)
    dmo = dmo3.reshape(S, D)

    dya, dyp, drest, dattn, dlb, dmp, dps = _bwd_gates(dmo, wout, wab, wpb, rest, ag, pg, attn, mp, pool_scale)
    dw_ab = _mm_tn(agt, dya, n_total=D, out_blk=lambda j: j, name="dw_attn_br")
    dw_pb = _mm_tn(pgt, dyp, n_total=D, out_blk=lambda j: j, name="dw_pool_br")
    drest, dpw = _pool_bwd(dmp, pt, pool_w, drest)

    dqkv = None
    dsums = []
    for gi in range(3):
        dqkv, dsum = _attn_bwd(qkv, dattn, Lb, dlb, biases[gi], geoms[gi], dqkv)
        dsums.append(dsum)

    dw_in = _mm_tn(ht, dqkv, n_total=8192, out_blk=lambda j: j, name="dw_in_qkv", with_low=True)
    dw_in, dw_in_low = _mm_tn(ht, drest, n_total=8192, out_blk=_rest_blk, name="dw_in_rest", prev=dw_in, with_low=True)
    return dict(st, loss=loss[0, 0], dw_in=dw_in, dw_in_low=dw_in_low, dw_ab=dw_ab, dw_pb=dw_pb, dw_out=dw_out, dpw=dpw, dps=dps, gfg=gfg,
                dgate=dgate, dsums=dsums, dqkv=dqkv, drest=drest, dx2=dx2)


def _local_bias_grads(st, which, dep=None):
    cols = []
    for gi in range(3):
        got = [_bias_grad(st["dsums"][gi][sub], st["idxs"][gi][sub], st["buckets"][gi][sub], gi, sub, dep)
               for sub in range(2) if (gi, sub) in which]
        cols.append(sum(got) if got else jnp.zeros((32, NH), F32))
    return jnp.concatenate(cols, axis=1)


_BIAS_GRADS_EARLY = ((0, 0), (0, 1), (1, 0), (2, 0))
_BIAS_GRADS_LATE = ((1, 1), (2, 1))


def _dh_h_bwd(dqkv, drest, w, dx2, x, norm_g, scale, dep=None):
    S = x.shape[0]
    dep_specs, dep_args = _after(dep)

    def body(a_ref, b_ref, w_ref, dx2_ref, x_ref, g_ref, sc_ref, *rest):
        gx_ref, dsh_ref, dsc_ref, dng_ref, nat_ref = rest[-5:]
        i = pl.program_id(0)

        @pl.when(i == 0)
        def _():
            dsh_ref[...] = jnp.zeros_like(dsh_ref)
            dsc_ref[...] = jnp.zeros_like(dsc_ref)
            dng_ref[...] = jnp.zeros_like(dng_ref)

        a, b = _ld(a_ref), _ld(b_ref)
        acc = _dot(a[:, 0:512], _w512(w_ref, 0), NT)
        for j in range(1, 9):
            acc = acc + _dot(a[:, 512 * j:512 * (j + 1)], _w512(w_ref, j), NT)
        off = 0
        for lo, hi in REST_COLS:
            for j in range(lo, hi):
                acc = acc + _dot(b[:, off:off + 512], _w512(w_ref, j), NT)
                off += 512
        for r in range(R16):
            nat_ref[:, r, :] = acc[R16 * r:R16 * (r + 1)]
        dh = jnp.concatenate([nat_ref[l] for l in range(R16)], axis=0)
        g = g_ref[...]
        one_sc = 1.0 + sc_ref[...]
        xv = x_ref[...]
        r1 = lax.rsqrt(jnp.mean(xv * xv, axis=-1, keepdims=True) + EPS)
        xn = xv * r1
        dhx = dh * xn
        dxn = dh * (g * one_sc)
        dx1 = r1 * (dxn - xn * jnp.mean(dxn * xn, axis=-1, keepdims=True))
        gx_ref[...] = dx2_ref[...] + dx1
        dsh_ref[...] += jnp.sum(dh, axis=0, keepdims=True)
        dsc_ref[...] += jnp.sum(dhx * g, axis=0, keepdims=True)
        dng_ref[...] += jnp.sum(dhx * one_sc, axis=0, keepdims=True)

    nat = pl.BlockSpec((TOK, D), lambda i: (i, 0))
    vec = pl.BlockSpec((1, D), lambda i: (0, 0))
    slab = lambda W: pl.BlockSpec((R16, R16, W), lambda i: (0, i, 0))
    return pl.pallas_call(
        body, name="dh_h_bwd", grid=(S // TOK,),
        in_specs=[slab(QKV_W), slab(REST_W), _resident((8, D, 1024)), nat, nat, vec, vec] + dep_specs,
        out_specs=[nat, vec, vec, vec],
        out_shape=[_sds((S, D), F32), _sds((1, D), F32), _sds((1, D), F32), _sds((1, D), F32)],
        scratch_shapes=[pltpu.VMEM((R16, R16, D), F32)],
        compiler_params=_cp(("arbitrary",), vmem=56),
    )(dqkv, drest, w, dx2, x, norm_g, scale, *dep_args)


def _local_input_grads(st, w_in_full, norm_g, scale, dep=None):
    S = st["h"].shape[0]
    LS = S // R16
    gx, dsh, dsc, dng = _dh_h_bwd(st["dqkv"].reshape(R16, LS, QKV_W), st["drest"].reshape(R16, LS, REST_W),
                                  w_in_full, st["dx2"], st["x"], norm_g, scale, dep)
    return gx, jnp.concatenate([dsh, dsc, st["dgate"]], axis=1), dng


def _my_place():
    return lax.axis_index("x"), lax.axis_index("y"), lax.axis_index("c")


def _block_of(ref, axis, idx, width):
    if axis is None:
        return ref.at[idx]
    start = pl.multiple_of(idx * width, width)
    if axis == 0:
        return ref.at[pl.ds(start, width)]
    return ref.at[:, pl.ds(start, width)]


_SEM = pl.BlockSpec(memory_space=pltpu.SEMAPHORE)
_EFFECT = pltpu.SideEffectType.DATAFLOW_SIDE_EFFECTING


def _split_start(name, srcs, lands, ncopy, plan, dep=None):
    arrs = list(srcs) + list(lands)
    na, ns = len(arrs), len(srcs)
    dep_specs, dep_args = _after(dep)

    def body(*refs):
        send_sems, recv_sems, token = refs[na + len(dep_args)], refs[na + len(dep_args) + 1], refs[-1]
        for k, (s, d, dev) in enumerate(plan(refs[:ns], refs[ns:na])):
            pltpu.make_async_remote_copy(src_ref=s, dst_ref=d, send_sem=send_sems.at[k], recv_sem=recv_sems.at[k],
                                         device_id=dev, device_id_type=MESH).start()
        token[...] = jnp.zeros_like(token)

    outs = pl.pallas_call(
        body, name=name,
        out_shape=(pltpu.SemaphoreType.DMA((ncopy,)), pltpu.SemaphoreType.DMA((ncopy,)),
                   *[pltpu.HBM(a.shape, a.dtype) for a in arrs], _sds((8, 128), F32)),
        in_specs=[_HBM] * na + dep_specs,
        out_specs=(_SEM, _SEM, *[_HBM] * na, pl.BlockSpec(memory_space=pltpu.VMEM)),
        input_output_aliases={i: 2 + i for i in range(na)},
        compiler_params=pltpu.CompilerParams(has_side_effects=_EFFECT),
    )(*[pltpu.with_memory_space_constraint(a, pltpu.HBM) for a in arrs], *dep_args)
    return (outs[0], outs[1]), list(outs[2:2 + ns]), list(outs[2 + ns:2 + na]), outs[-1]


def _split_wait(name, sems, srcs, lands, plan, after):
    arrs = list(srcs) + list(lands)
    na, ns = len(arrs), len(srcs)
    after = list(after) if isinstance(after, (list, tuple)) else [after]

    def body(*refs):
        send_sems, recv_sems = refs[na], refs[na + 1]
        for k, (s, d, dev) in enumerate(plan(refs[:ns], refs[ns:na])):
            cp = pltpu.make_async_remote_copy(src_ref=s, dst_ref=d, send_sem=send_sems.at[k], recv_sem=recv_sems.at[k],
                                              device_id=dev, device_id_type=MESH)
            cp.wait_send()
            cp.wait_recv()

    outs = pl.pallas_call(
        body, name=name, out_shape=[pltpu.HBM(a.shape, a.dtype) for a in arrs],
        in_specs=[_HBM] * na + [_SEM, _SEM] + [pl.BlockSpec(memory_space=pl.ANY)] * len(after),
        out_specs=[_HBM] * na,
        input_output_aliases={i: i for i in range(na)},
        compiler_params=pltpu.CompilerParams(has_side_effects=_EFFECT),
    )(*arrs, sems[0], sems[1], *after)
    return list(outs[:ns]), list(outs[ns:])


def _gather_chips_plan(layout):
    def plan(src, land):
        x, y, c = _my_place()
        me = 4 * x + 2 * y + c
        mine = [_block_of(land[p], layout[p][0], me, layout[p][1]) for p in range(len(land))]
        return [(mine[p], mine[p], (cx, cy, c)) for p in range(len(land)) for cx, cy in [(1 - x, y), (x, 1 - y)]]
    return plan


def _pair_plan(layout):
    def plan(src, land):
        x, y, c = _my_place()
        n = len(layout)
        return [(_block_of(src[p], layout[p][0], 2 * q + (1 - c), layout[p][1]), land[p].at[q], (x, y, 1 - c))
                for p in range(n) for q in range(4)] + [(src[p], land[p], (x, y, 1 - c)) for p in range(n, len(src))]
    return plan


def _chips_plan(n):
    def plan(src, land):
        x, y, c = _my_place()
        chips = [(1 - x, y), (x, 1 - y), (1 - x, 1 - y)]
        return [(src[p].at[2 * cx + cy], land[p].at[2 * x + y], (cx, cy, c)) for p in range(n) for cx, cy in chips] + \
               [(src[p], land[p].at[2 * x + y], (cx, cy, c)) for p in range(n, len(src)) for cx, cy in chips]
    return plan


def _pair_share(gathered):
    n = len(gathered)

    def body(*refs):
        land = refs[n:2 * n]
        send_sems, recv_sems, fwd_send, fwd_recv = refs[2 * n:]
        x, y, c = _my_place()
        k_dg = 3 - (2 * x + y)
        south = c == 0
        pass_on = 4 * jnp.where(south, 1 - x, x) + 2 * jnp.where(south, y, 1 - y) + c
        to = (jnp.where(south, x, 1 - x), jnp.where(south, 1 - y, y), c)

        def forward(p):
            return pltpu.make_async_remote_copy(
                src_ref=land[p].at[pass_on], dst_ref=land[p].at[pass_on], send_sem=fwd_send.at[p],
                recv_sem=fwd_recv.at[p], device_id=to, device_id_type=MESH)

        def to_sibling(p, k):
            blk = land[p].at[2 * k + c]
            return pltpu.make_async_remote_copy(
                src_ref=blk, dst_ref=blk, send_sem=send_sems.at[4 * p + k], recv_sem=recv_sems.at[4 * p + k],
                device_id=(x, y, 1 - c), device_id_type=MESH)

        for p in range(n):
            forward(p).start()
        for p in range(n):
            for k in range(4):
                @pl.when(k != k_dg)
                def _():
                    to_sibling(p, k).start()
        for p in range(n):
            dg = land[p].at[2 * k_dg + c]
            pltpu.make_async_remote_copy(src_ref=dg, dst_ref=dg, send_sem=fwd_send.at[p], recv_sem=fwd_recv.at[p],
                                         device_id=to, device_id_type=MESH).wait_recv()
            for k in range(4):
                @pl.when(k == k_dg)
                def _():
                    to_sibling(p, k).start()
        for p in range(n):
            forward(p).wait_send()
            for k in range(4):
                to_sibling(p, k).wait_send()
                blk = land[p].at[2 * k + 1 - c]
                pltpu.make_async_remote_copy(
                    src_ref=blk, dst_ref=blk, send_sem=send_sems.at[4 * p + k], recv_sem=recv_sems.at[4 * p + k],
                    device_id=(x, y, 1 - c), device_id_type=MESH).wait_recv()

    return pl.pallas_call(
        body, name="gather_pair_share", in_specs=[_HBM] * n, out_specs=[_HBM] * n,
        out_shape=[_sds(g.shape, g.dtype) for g in gathered],
        input_output_aliases={p: p for p in range(n)},
        scratch_shapes=[pltpu.SemaphoreType.DMA((4 * n,)), pltpu.SemaphoreType.DMA((4 * n,)),
                        pltpu.SemaphoreType.DMA((n,)), pltpu.SemaphoreType.DMA((n,))],
    )(*gathered)


def _place_slot(arrs, slot, n_slots, name, out_dtype=None):
    n = len(arrs)

    def body(s_ref, *refs):
        for p in range(n):
            refs[n + p][...] = refs[p][...].astype(refs[n + p].dtype)

    def spec(a):
        tail = a.shape[-2:]
        return pl.BlockSpec((None,) + tail, lambda i, s_ref: (s_ref[0], 0, 0))

    in_specs = [spec(a) if a.ndim == 3 else pl.BlockSpec(a.shape, lambda i, s_ref: (0, 0)) for a in arrs]
    return pl.pallas_call(
        body, name=name,
        grid_spec=pltpu.PrefetchScalarGridSpec(num_scalar_prefetch=1, grid=(1,), in_specs=in_specs,
                                               out_specs=[spec(a) for a in arrs]),
        out_shape=[_sds((n_slots,) + a.shape[-2:], out_dtype or a.dtype) for a in arrs],
        compiler_params=_cp(("arbitrary",)),
    )(slot, *arrs)


def _cols_from_blocks(w8, name):
    _, R, C = w8.shape

    def body(w_ref, o_ref):
        for e in range(8):
            o_ref[:, C * e:C * (e + 1)] = w_ref[e]

    return pl.pallas_call(body, name=name, out_shape=_sds((R, 8 * C), w8.dtype), compiler_params=_cp())(w8)


def _all_gather_small(srcs, name):
    n = len(srcs)
    vmem = pl.BlockSpec(memory_space=pltpu.VMEM)

    def body(*refs):
        src, out = refs[:n], refs[n:2 * n]
        send_sems, recv_sems = refs[2 * n:]
        x, y, c = _my_place()
        me = 4 * x + 2 * y + c
        flip = lambda v, f: 1 - v if f else v
        peers = [(flip(x, fx), flip(y, fy), flip(c, fc)) for fx in (0, 1) for fy in (0, 1) for fc in (0, 1)][1:]
        cps = []
        for p in range(n):
            out[p][me] = src[p][...]
            for k, peer in enumerate(peers):
                cps.append(pltpu.make_async_remote_copy(
                    src_ref=src[p], dst_ref=out[p].at[me], send_sem=send_sems.at[7 * p + k],
                    recv_sem=recv_sems.at[7 * p + k], device_id=peer, device_id_type=MESH))
        for cp in cps:
            cp.start()
        for cp in cps:
            cp.wait()

    return pl.pallas_call(
        body, name=name, in_specs=[vmem] * n, out_specs=[vmem] * n,
        out_shape=[_sds((8,) + a.shape, a.dtype) for a in srcs],
        scratch_shapes=[pltpu.SemaphoreType.DMA((7 * n,)), pltpu.SemaphoreType.DMA((7 * n,))],
    )(*srcs)


def _adaln_exchange(c, w_ada, b_part):
    vmem = pl.BlockSpec(memory_space=pltpu.VMEM)

    def body(c_ref, w_ref, b_ref, call_ref, mod_ref, part_ref, send_sems, recv_sems):
        x, y, c = _my_place()
        me = 4 * x + 2 * y + c
        flip = lambda v, f: 1 - v if f else v
        peers = [(flip(x, fx), flip(y, fy), flip(c, fc)) for fx in (0, 1) for fy in (0, 1) for fc in (0, 1)][1:]
        call_ref[me] = c_ref[...]
        first = [pltpu.make_async_remote_copy(src_ref=c_ref, dst_ref=call_ref.at[me], send_sem=send_sems.at[k],
                                              recv_sem=recv_sems.at[k], device_id=peer, device_id_type=MESH)
                 for k, peer in enumerate(peers)]
        for cp in first:
            cp.start()
        for cp in first:
            cp.wait()
        c_all = jnp.concatenate([call_ref[j] for j in range(8)], axis=0)
        part_ref[...] = _dot(c_all.astype(CDT), w_ref[...].astype(CDT), NN) + b_ref[...]
        mod_ref[me] = part_ref[pl.ds(me, 1), :]
        second = [pltpu.make_async_remote_copy(
            src_ref=part_ref.at[pl.ds(4 * px + 2 * py + pc, 1)], dst_ref=mod_ref.at[me], send_sem=send_sems.at[7 + k],
            recv_sem=recv_sems.at[7 + k], device_id=(px, py, pc), device_id_type=MESH)
            for k, (px, py, pc) in enumerate(peers)]
        for cp in second:
            cp.start()
        for cp in second:
            cp.wait()

    return pl.pallas_call(
        body, name="adaln_exchange", in_specs=[vmem] * 3, out_specs=[vmem] * 2,
        out_shape=[_sds((8, 1, D), F32), _sds((8, 1, w_ada.shape[1]), F32)],
        scratch_shapes=[pltpu.VMEM((8, w_ada.shape[1]), F32), pltpu.SemaphoreType.DMA((14,)),
                        pltpu.SemaphoreType.DMA((14,))],
        compiler_params=_cp(),
    )(c, w_ada, b_part)


def _pair_add(dw, recv, axis, width, cidx, name):
    if axis == 0:
        tr = min(width, 256)
        C = dw.shape[1]
        grid = (4, width // tr)
        nb = width // tr
        own = pl.BlockSpec((tr, C), lambda q, i, c_ref: ((2 * q + c_ref[0]) * nb + i, 0))
        oth = pl.BlockSpec((None, tr, C), lambda q, i, c_ref: (q, i, 0))
    else:
        R = dw.shape[0]
        tr = min(R, 1024)
        grid = (4, R // tr)
        own = pl.BlockSpec((tr, width), lambda q, i, c_ref: (i, 2 * q + c_ref[0]))
        oth = pl.BlockSpec((None, tr, width), lambda q, i, c_ref: (q, i, 0))

    def body(c_ref, a_ref, b_ref, o_ref):
        o_ref[...] = (a_ref[...] + b_ref[...].astype(F32)).astype(o_ref.dtype)

    return pl.pallas_call(
        body, name=name,
        grid_spec=pltpu.PrefetchScalarGridSpec(num_scalar_prefetch=1, grid=grid, in_specs=[own, oth], out_specs=oth),
        out_shape=_sds(recv.shape, CDT), compiler_params=_cp(("parallel", "parallel")),
    )(cidx, dw, recv)


def _add_pairs(xs, ys, name):
    n = len(xs)

    def body(*refs):
        for p in range(n):
            refs[2 * n + p][...] = refs[p][...] + refs[n + p][...]

    return pl.pallas_call(body, name=name, out_shape=[_sds(a.shape, F32) for a in xs])(*xs, *ys)


def _adam_math(w, g, m, v):
    m = ADAM_B1 * m + (1.0 - ADAM_B1) * g
    v = ADAM_B2 * v + (1.0 - ADAM_B2) * (g * g)
    m_hat = m / (1.0 - ADAM_B1 ** ADAM_STEP)
    v_hat = v / (1.0 - ADAM_B2 ** ADAM_STEP)
    delta = -ADAM_LR * (m_hat / (jnp.sqrt(v_hat) + ADAM_EPS) + ADAM_WD * w)
    return delta, m, v


def _adam(gparts, w, m, v, name):
    P, R, C = gparts.shape
    tr = R if R <= 256 else 256

    def body(g_ref, w_ref, m_ref, v_ref, go_ref, d_ref, mo_ref, vo_ref):
        g = g_ref[0].astype(F32)
        for i in range(1, P):
            g = g + g_ref[i].astype(F32)
        d, mn, vn = _adam_math(w_ref[...], g, m_ref[...], v_ref[...])
        go_ref[...] = g
        d_ref[...] = d
        mo_ref[...] = mn
        vo_ref[...] = vn

    t = pl.BlockSpec((tr, C), lambda i: (i, 0))
    return pl.pallas_call(
        body, name=name, grid=(R // tr,),
        in_specs=[pl.BlockSpec((P, tr, C), lambda i: (0, i, 0)), t, t, t], out_specs=[t, t, t, t],
        out_shape=[_sds((R, C), F32)] * 4, compiler_params=_cp(("parallel",)),
    )(gparts, w, m, v)


def _sum_parts(gparts, name):
    P, R, C = gparts.shape

    def body(g_ref, o_ref):
        g = g_ref[0]
        for i in range(1, P):
            g = g + g_ref[i]
        o_ref[...] = g

    return pl.pallas_call(body, name=name, out_shape=_sds((R, C), F32))(gparts)


def _adam_w_ada(c_t, dm, w, m, v):
    R, C = w.shape

    def body(c_ref, dm_ref, w_ref, m_ref, v_ref, go_ref, d_ref, mo_ref, vo_ref):
        g = c_ref[:, 0:1] * dm_ref[0:1, :]
        for b in range(1, 8):
            g = g + c_ref[:, b:b + 1] * dm_ref[b:b + 1, :]
        d, mn, vn = _adam_math(w_ref[...], g, m_ref[...], v_ref[...])
        go_ref[...] = g
        d_ref[...] = d
        mo_ref[...] = mn
        vo_ref[...] = vn

    return pl.pallas_call(body, name="adam_w_ada", out_shape=[_sds((R, C), F32)] * 4,
                          compiler_params=_cp())(c_t, dm, w, m, v)


def kernel(x, c, norm_g, w_ada, b_ada, w_in, pool_w, pool_scale, w_attn_br, w_pool_br, w_out, rel_bias, final_g, loss_target, m_norm_g, m_w_ada, m_b_ada, m_w_in, m_pool_w, m_pool_scale, m_w_attn_br, m_w_pool_br, m_w_out, m_rel_bias, m_final_g, v_norm_g, v_w_ada, v_b_ada, v_w_in, v_pool_w, v_pool_scale, v_w_attn_br, v_w_pool_br, v_w_out, v_rel_bias, v_final_g):
    S = x.shape[1]
    px, py, pc = _my_place()
    me = 4 * px + 2 * py + pc
    cidx = jnp.reshape(pc, (1,)).astype(jnp.int32)

    layout = [(1, 1024), (1, 128), (1, 128), (0, 128)]
    final_g2 = final_g.reshape(1, D)
    me1 = jnp.reshape(me, (1,)).astype(jnp.int32)

    b_part = lax.dynamic_slice(b_ada, (0, me * 384), (1, 384))
    c_all, mod_all = _adaln_exchange(c, w_ada[0], b_part)
    c_all = c_all.reshape(8, D)
    mod = mod_all.reshape(1, 3 * D)
    shift, scale, gate = mod[:, :D], mod[:, D:2 * D], mod[:, 2 * D:]

    gplan = _gather_chips_plan([(None, 1)] * 4)
    lands = _place_slot([w_in[0], w_attn_br[0], w_pool_br[0], w_out[0]], me1, 8, "place_own_shards", CDT)
    g_sems, _, lands, tok = _split_start("gather_chips_start", [], lands, 8, gplan, dep=mod_all)
    st = _local_pre(x[0], shift, scale, norm_g, rel_bias, dep=tok)
    _, lands = _split_wait("gather_chips_wait", g_sems, [], lands, gplan,
                           [st["ht"]] + [b for pair in st["biases"] for b in pair])
    w_in_full, wab8, wpb8, wout8 = _pair_share(lands)
    wab, wpb = _cols_from_blocks(wab8, "w_attn_br_cols"), _cols_from_blocks(wpb8, "w_pool_br_cols")
    wout = wout8.reshape(D, D)

    st = _local_grads(st, loss_target[0], gate, w_in_full, pool_w[0], pool_scale, wab, wpb, wout, final_g2)

    early = [st["dpw"].reshape(AW, 128), st["dps"], st["gfg"], jnp.broadcast_to(st["loss"], (1, 128))]
    dws = [st["dw_in"], st["dw_ab"], st["dw_pb"], st["dw_out"]]
    sent = [st["dw_in_low"]] + dws[1:] + early
    pplan = _pair_plan(layout)
    pair_shapes = [(4, w, a.shape[1]) if ax == 0 else (4, a.shape[0], w) for a, (ax, w) in zip(dws, layout)]
    p_sems, sent, recv, tok = _split_start(
        "reduce_pair_start", sent,
        [lax.empty(s, a.dtype) for s, a in zip(pair_shapes + [a.shape for a in early], sent)], 20, pplan)
    d_rel_bias = _local_bias_grads(st, _BIAS_GRADS_EARLY, dep=tok)
    sent, recv = _split_wait("reduce_pair_wait", p_sems, sent, recv, pplan, d_rel_bias)
    parts = [_pair_add(dw, r, ax, w, cidx, f"pair_add{i}")
             for i, (dw, r, (ax, w)) in enumerate(zip([dws[0]] + sent[1:4], recv[:4], layout))]
    parts += _add_pairs(sent[4:], recv[4:], "pair_add_smalls")

    slot1 = jnp.reshape(2 * px + py, (1,)).astype(jnp.int32)
    cplan = _chips_plan(4)
    c_sems, parts, lands, tok = _split_start("reduce_chips_start", parts,
                                             _place_slot(parts, slot1, 4, "place_own_partials"), 24, cplan)
    d_rel_bias = d_rel_bias + _local_bias_grads(st, _BIAS_GRADS_LATE, dep=tok)
    grad_x, dmod, d_norm_g = _local_input_grads(st, w_in_full, norm_g, scale, dep=tok)

    dmod_all, ng_all, rb_all = _all_gather_small([dmod, d_norm_g, d_rel_bias], "gather_late")
    o_norm_g = _adam(ng_all, norm_g, m_norm_g, v_norm_g, "adam_norm_g")
    o_rel_bias = _adam(rb_all, rel_bias, m_rel_bias, v_rel_bias, "adam_rel_bias")
    o_b_ada = _adam(dmod_all, b_ada, m_b_ada, v_b_ada, "adam_b_ada")
    dm_mine = lax.dynamic_slice(dmod_all.reshape(8, 3 * D), (0, me * 384), (8, 384))
    o_w_ada = [o[None] for o in _adam_w_ada(c_all.T, dm_mine, w_ada[0], m_w_ada[0], v_w_ada[0])]

    _, (g_in, g_ab, g_pb, g_out, g_pw, g_ps, g_fg, g_loss) = _split_wait(
        "reduce_chips_wait", c_sems, parts, lands, cplan, [o_w_ada[0], o_b_ada[0], o_norm_g[0], o_rel_bias[0]])
    o_w_in = [o[None] for o in _adam(g_in, w_in[0], m_w_in[0], v_w_in[0], "adam_w_in")]
    o_w_ab = [o[None] for o in _adam(g_ab, w_attn_br[0], m_w_attn_br[0], v_w_attn_br[0], "adam_w_attn_br")]
    o_w_pb = [o[None] for o in _adam(g_pb, w_pool_br[0], m_w_pool_br[0], v_w_pool_br[0], "adam_w_pool_br")]
    o_w_out = [o[None] for o in _adam(g_out, w_out[0], m_w_out[0], v_w_out[0], "adam_w_out")]
    o_pool_w = [o.reshape(pool_w.shape) for o in _adam(g_pw, pool_w.reshape(AW, 128), m_pool_w.reshape(AW, 128),
                                                       v_pool_w.reshape(AW, 128), "adam_pool_w")]
    o_pool_scale = _adam(g_ps, pool_scale, m_pool_scale, v_pool_scale, "adam_pool_scale")
    o_final_g = [o.reshape(D) for o in _adam(g_fg, final_g2, m_final_g.reshape(1, D), v_final_g.reshape(1, D),
                                             "adam_final_g")]
    loss = _sum_parts(g_loss, "sum_loss")[0, 0]

    per_w = [o_norm_g, o_w_ada, o_b_ada, o_w_in, o_pool_w, o_pool_scale, o_w_ab, o_w_pb, o_w_out, o_rel_bias, o_final_g]
    outs = [loss, grad_x[None]]
    for k in range(4):
        outs += [o[k] for o in per_w]
    return tuple(outs)
```

```python
import functools
import math

import numpy as np
import jax
import jax.numpy as jnp
from jax import lax
from jax.experimental import pallas as pl
from jax.experimental.pallas import tpu as pltpu

F32 = jnp.float32
CDT = jnp.bfloat16
D = 1024
HD = 64
NH = 8
AW = 512
QKV_W = 4608
REST_W = 3584
R16 = 16
EPS = 1e-6
NEG = -1e30
N_BACK = 128
ADAM_LR, ADAM_B1, ADAM_B2, ADAM_EPS, ADAM_WD, ADAM_STEP = 0.001, 0.9, 0.999, 1e-08, 0.01, 10
MESH = pl.DeviceIdType.MESH
VMEM_MB = 1024 * 1024


def _cp(sem=None, vmem=48, **kw):
    if sem is not None:
        kw["dimension_semantics"] = sem
    return pltpu.CompilerParams(vmem_limit_bytes=vmem * VMEM_MB, **kw)


def _sds(shape, dtype):
    return jax.ShapeDtypeStruct(tuple(shape), dtype)


def _rest_blk(jj):
    return jnp.where(jj == 0, 9, jnp.where(jj == 6, 10, jj + 10))


def _sigmoid(z):
    return 0.5 * jnp.tanh(0.5 * z) + 0.5


def _dot(a, b, dims):
    return lax.dot_general(a, b, (dims, ((), ())), preferred_element_type=F32)


NN = ((1,), (0,))
NT = ((1,), (1,))
TN = ((0,), (0,))


def _ld(ref):
    if len(ref.shape) == 2:
        return ref[...]
    return jnp.concatenate([ref[a] for a in range(ref.shape[0])], axis=0)


def _st(ref, val, cols=None):
    val = val.astype(ref.dtype)
    if len(ref.shape) == 2:
        if cols is None:
            ref[...] = val
        else:
            ref[:, cols] = val
        return
    rows = ref.shape[1]
    for a in range(ref.shape[0]):
        if cols is None:
            ref[a] = val[a * rows:(a + 1) * rows]
        else:
            ref[a, :, cols] = val[a * rows:(a + 1) * rows]


REST_COLS = ((9, 10), (11, 16), (10, 11))
_HBM = pl.BlockSpec(memory_space=pltpu.HBM)


def _resident(shape):
    return pl.BlockSpec(shape, lambda *ids: (0,) * len(shape), pipeline_mode=pl.Buffered(1))


def _w512(w_ref, j):
    return w_ref[j // 2, :, 512 * (j % 2):512 * (j % 2 + 1)]


def _proj(h, w):
    S = h.shape[0]
    tm = 512

    def body(a_ref, w_ref, q_ref, r_ref):
        a = a_ref[...]
        for j in range(9):
            q_ref[:, 512 * j:512 * (j + 1)] = _dot(a, _w512(w_ref, j), NN).astype(q_ref.dtype)
        off = 0
        for lo, hi in REST_COLS:
            for j in range(lo, hi):
                r_ref[:, off:off + 512] = _dot(a, _w512(w_ref, j), NN)
                off += 512

    return pl.pallas_call(
        body, name="proj", grid=(S // tm,),
        in_specs=[pl.BlockSpec((tm, D), lambda i: (i, 0)), _resident((8, D, 1024))],
        out_specs=[pl.BlockSpec((tm, QKV_W), lambda i: (i, 0)), pl.BlockSpec((tm, REST_W), lambda i: (i, 0))],
        out_shape=[_sds((S, QKV_W), CDT), _sds((S, REST_W), F32)],
        compiler_params=_cp(("parallel",), vmem=56),
    )(h, w)


def _after(dep):
    return ([], []) if dep is None else ([pl.BlockSpec(memory_space=pl.ANY)], [dep])


def _dh(dqkv, drest, w, dep=None):
    S = dqkv.shape[0]
    tm = 512
    dep_specs, dep_args = _after(dep)

    def body(a_ref, b_ref, w_ref, *rest):
        o_ref = rest[-1]
        acc = _dot(a_ref[:, 0:512], _w512(w_ref, 0), NT)
        for j in range(1, 9):
            acc = acc + _dot(a_ref[:, 512 * j:512 * (j + 1)], _w512(w_ref, j), NT)
        off = 0
        for lo, hi in REST_COLS:
            for j in range(lo, hi):
                acc = acc + _dot(b_ref[:, off:off + 512], _w512(w_ref, j), NT)
                off += 512
        o_ref[...] = acc

    return pl.pallas_call(
        body, name="dh", grid=(S // tm,),
        in_specs=[pl.BlockSpec((tm, QKV_W), lambda i: (i, 0)), pl.BlockSpec((tm, REST_W), lambda i: (i, 0)),
                  _resident((8, D, 1024))] + dep_specs,
        out_specs=pl.BlockSpec((tm, D), lambda i: (i, 0)),
        out_shape=_sds((S, D), F32),
        compiler_params=_cp(("parallel",), vmem=56),
    )(dqkv, drest, w, *dep_args)


def _mm_tn(at, b, *, n_total, out_blk, name, prev=None, tn=512, with_low=False):
    R, S = at.shape
    nb = b.shape[1] // tn
    n_out = 2 if with_low else 1

    def body(*refs):
        res = _dot(refs[0][...], refs[1][...], NN)
        refs[-n_out][...] = res
        if with_low:
            refs[-1][...] = res.astype(CDT)

    in_specs = [_resident((R, S)), pl.BlockSpec((S, tn), lambda j: (0, j))]
    args = [at, b]
    aliases = {}
    if prev is not None:
        prev = list(prev) if with_low else [prev]
        in_specs += [pl.BlockSpec(memory_space=pl.ANY)] * n_out
        args += prev
        aliases = {2 + k: k for k in range(n_out)}
    out_spec = pl.BlockSpec((R, tn), lambda j: (0, out_blk(j)))
    outs = pl.pallas_call(
        body, name=name, grid=(nb,), in_specs=in_specs,
        out_specs=[out_spec] * n_out,
        out_shape=[_sds((R, n_total), F32), _sds((R, n_total), CDT)][:n_out],
        input_output_aliases=aliases,
        compiler_params=_cp(("parallel",)),
    )(*args)
    return tuple(outs) if with_low else outs[0]


def _tcast(a, name):
    M, C = a.shape
    tm = 512

    def body(a_ref, o_ref):
        o_ref[...] = a_ref[...].astype(F32).T.astype(o_ref.dtype)

    return pl.pallas_call(
        body, name=name, grid=(M // tm,),
        in_specs=[pl.BlockSpec((tm, C), lambda i: (i, 0))],
        out_specs=pl.BlockSpec((C, tm), lambda i: (0, i)),
        out_shape=_sds((C, M), CDT), compiler_params=_cp(("parallel",)),
    )(a)


TOK = R16 * R16


def _swap16():
    i = lax.broadcasted_iota(jnp.int32, (TOK, TOK), 0)
    j = lax.broadcasted_iota(jnp.int32, (TOK, TOK), 1)
    return jnp.where(j == jnp.bitwise_and(i, 15) * 16 + jnp.right_shift(i, 4), 1.0, 0.0).astype(CDT)


def _norm_mod(x, norm_g, shift, scale, dep=None):
    S = x.shape[0]
    dep_specs, dep_args = _after(dep)

    def body(x_ref, g_ref, sh_ref, sc_ref, *rest):
        xv = x_ref[...]
        xn = xv * lax.rsqrt(jnp.mean(xv * xv, axis=-1, keepdims=True) + EPS)
        h = (xn * (g_ref[...] * (1.0 + sc_ref[...])) + sh_ref[...]).astype(CDT)
        _st(rest[-1], _dot(_swap16(), h, NN))

    vec = pl.BlockSpec((1, D), lambda i: (0, 0))
    return pl.pallas_call(
        body, name="norm_mod", grid=(S // TOK,),
        in_specs=[pl.BlockSpec((TOK, D), lambda i: (i, 0)), vec, vec, vec] + dep_specs,
        out_specs=pl.BlockSpec((R16, R16, D), lambda i: (0, i, 0)),
        out_shape=_sds((R16, S // R16, D), CDT), compiler_params=_cp(("parallel",)),
    )(x, norm_g, shift, scale, *dep_args)


def _t5_bucket_np(n):
    nf = np.maximum(n, 1).astype(np.float32)
    large = 16 + (np.log(nf / np.float32(16)) / np.float32(math.log(2048 / 16)) * np.float32(16)).astype(np.int32)
    large = np.minimum(large, 31)
    return np.where(n < 16, n, large).astype(np.int32)


class _Geom:
    def __init__(self, gi, LS):
        self.gi, self.LS = gi, LS
        self.dil = (1, 4, 16)[gi]
        if gi == 0:
            self.lead, self.rows, self.n_tiles, self.has_prev = (R16,), 16, LS // 16, True
            a = np.arange(R16)[:, None]
            ll = np.arange(16)[None, :]
            self.pos = (16 * ll + a).reshape(-1)
        elif gi == 1:
            self.lead, self.rows, self.n_tiles, self.has_prev = (4,), 64, LS // 64, True
            a = np.arange(4)[:, None]
            ll = np.arange(64)[None, :]
            self.pos = (4 * ll + a).reshape(-1)
        else:
            self.lead, self.rows, self.n_tiles, self.has_prev = (), LS, 1, False
            self.pos = np.arange(LS)
        self.T = self.pos.shape[0]
        self.Tk = 2 * self.T if self.has_prev else self.T
        self.outer = {0: (), 1: (4,), 2: (R16,)}[gi]

    def view(self, arr):
        S, C = arr.shape
        if self.gi == 1:
            return arr.reshape(4, 4, self.LS, C)
        return arr.reshape(R16, self.LS, C)

    def view_shape(self, C):
        return (4, 4, self.LS, C) if self.gi == 1 else (R16, self.LS, C)

    def unview(self, arr):
        return arr.reshape(R16 * self.LS, arr.shape[-1])

    def spec(self, W, cblk, tile):
        if self.gi == 0:
            return pl.BlockSpec((R16, 16, W), lambda n: (0, tile(n), cblk))
        if self.gi == 1:
            return pl.BlockSpec((4, None, 64, W), lambda r, n: (0, r, tile(n), cblk))
        return pl.BlockSpec((None, self.LS, W), lambda r: (r, 0, cblk))

    def halves(self):
        R = self.rows
        idx = np.arange(self.T).reshape(-1, R)
        return idx[:, :R // 2].reshape(-1), idx[:, R // 2:].reshape(-1)

    def bucket_index(self, sub):
        lo, hi = self.halves()
        pos = self.pos
        if sub == 0:
            pq = pos[lo]
            pk = np.concatenate([pos[hi] - self.T, pos[lo]]) if self.has_prev else np.concatenate([pos[lo], pos[hi]])
        else:
            pq = pos[hi]
            pk = np.concatenate([pos[lo], pos[hi]])
        dist = pq[:, None] - pk[None, :]
        ok = (dist >= 0) & (dist <= N_BACK)
        bucket = _t5_bucket_np(np.clip(dist, 0, N_BACK) * self.dil)
        return np.where(ok, bucket, -1).astype(np.int32)


def _bias_table(idx, buckets, rel_bias, gi, sub, dep=None):
    T, Tk = idx.shape
    tr = 32
    dep_specs, dep_args = _after(dep)

    def body(idx_ref, rb_ref, *rest):
        h = pl.program_id(0)
        ix = idx_ref[...]
        acc = jnp.full(ix.shape, NEG, F32)
        for b in buckets:
            acc = jnp.where(ix == b, rb_ref[b, gi * NH + h], acc)
        rest[-1][...] = acc

    return pl.pallas_call(
        body, name=f"bias_table{gi}{'ab'[sub]}", grid=(NH, T // tr),
        in_specs=[pl.BlockSpec((tr, Tk), lambda h, i: (i, 0)), pl.BlockSpec(memory_space=pltpu.SMEM)] + dep_specs,
        out_specs=pl.BlockSpec((None, tr, Tk), lambda h, i: (h, i, 0)),
        out_shape=_sds((NH, T, Tk), F32), compiler_params=_cp(("parallel", "parallel")),
    )(idx, rel_bias, *dep_args)


def _bias_grad(dsum, idx, buckets, gi, sub, dep=None):
    _, T, Tk = dsum.shape
    dep_specs, dep_args = _after(dep)

    def body(ds_ref, idx_ref, *rest):
        o_ref, r_ref = rest[-2], rest[-1]
        r_ref[...] = jnp.zeros_like(r_ref)
        ix, ds = idx_ref[...], ds_ref[...]
        for b in buckets:
            r_ref[b:b + 1, :] = jnp.sum(jnp.where(ix == b, ds, 0.0), axis=0, keepdims=True)
        o_ref[...] = jnp.broadcast_to(jnp.sum(r_ref[...], axis=1, keepdims=True), (32, 128))

    out = pl.pallas_call(
        body, name=f"bias_grad{gi}{'ab'[sub]}", grid=(NH,),
        in_specs=[pl.BlockSpec((None, T, Tk), lambda h: (h, 0, 0)), pl.BlockSpec((T, Tk), lambda h: (0, 0))] + dep_specs,
        out_specs=pl.BlockSpec((None, 32, 128), lambda h: (h, 0, 0)),
        out_shape=_sds((NH, 32, 128), F32), scratch_shapes=[pltpu.VMEM((32, Tk), F32)],
        compiler_params=_cp(("parallel",)),
    )(dsum, idx, *dep_args)
    return out[:, :, 0].T


def _prev_mask(n, T, Tk):
    col = lax.broadcasted_iota(jnp.int32, (1, Tk), 1)
    return jnp.where((col < T) & (n == 0), NEG, 0.0).astype(F32)


def _split_rows(v, R):
    A, half = v.shape[0] // R, R // 2
    via_f32 = v.dtype != F32 and half % 16 != 0
    w = v.astype(F32) if via_f32 else v
    lo = jnp.concatenate([w[a * R:a * R + half] for a in range(A)], axis=0)
    hi = jnp.concatenate([w[a * R + half:(a + 1) * R] for a in range(A)], axis=0)
    return (lo.astype(v.dtype), hi.astype(v.dtype)) if via_f32 else (lo, hi)


def _merge_rows(lo, hi, R):
    half = R // 2
    A = lo.shape[0] // half
    return jnp.concatenate([x[a * half:(a + 1) * half] for a in range(A) for x in (lo, hi)], axis=0)


def _attn_fwd(qkv, biases, geom, others=None):
    S = qkv.shape[0]
    gi, T, R, has_prev = geom.gi, geom.T, geom.rows, geom.has_prev
    H2 = T // 2
    qv = geom.view(qkv)
    cq, ck, cv = 3 * gi, 3 * gi + 1, 3 * gi + 2
    assert others is None or not has_prev

    def body(*refs):
        if has_prev:
            q_ref, kp_ref, kc_ref, vp_ref, vc_ref, ba_ref, bb_ref, o_ref, l_ref = refs
        elif others is not None:
            (q_ref, kc_ref, vc_ref, ba_ref, bb_ref, o0_ref, o1_ref, l0_ref, l1_ref, z_ref,
             attn_ref, L_ref, ag_ref) = refs
        else:
            q_ref, kc_ref, vc_ref, ba_ref, bb_ref, o_ref, l_ref = refs
        q_lo, q_hi = _split_rows(_ld(q_ref), R)
        kc_lo, kc_hi = _split_rows(_ld(kc_ref), R)
        vc_lo, vc_hi = _split_rows(_ld(vc_ref), R)
        k_b = jnp.concatenate([kc_lo, kc_hi], axis=0)
        v_b = jnp.concatenate([vc_lo, vc_hi], axis=0)
        if has_prev:
            n = pl.program_id(len(geom.outer))
            k_a = jnp.concatenate([_split_rows(_ld(kp_ref), R)[1], kc_lo], axis=0)
            v_a = jnp.concatenate([_split_rows(_ld(vp_ref), R)[1], vc_lo], axis=0)
            pm = _prev_mask(n, H2, T)
        else:
            k_a, v_a, pm = k_b, v_b, None
        low = lax.broadcasted_iota(jnp.int32, (1, 2 * HD), 1) < HD
        for j in range(NH // 2):
            cols = slice(2 * HD * j, 2 * HD * (j + 1))
            o_half, l_half = [], []
            for q, k, v, b_ref, mask in ((q_lo, k_a, v_a, ba_ref, pm), (q_hi, k_b, v_b, bb_ref, None)):
                q2 = q[:, cols].astype(F32) * 0.125
                k2, v2 = k[:, cols], v[:, cols]
                qs = jnp.concatenate([jnp.where(low, q2, 0.0), jnp.where(low, 0.0, q2)], axis=0).astype(q.dtype)
                s = _dot(qs, k2, NT) + b_ref[j]
                if mask is not None:
                    s = s + mask
                m = jnp.max(s, axis=1, keepdims=True)
                p = jnp.exp(s - m)
                l = jnp.sum(p, axis=1, keepdims=True)
                pv = _dot(p.astype(v.dtype), v2, NN) / l
                lse = m + jnp.log(l)
                o_half.append(jnp.where(low, pv[:H2], pv[H2:]))
                l_half.append(jnp.where(low, lse[:H2], lse[H2:]))
            o2, lc = _merge_rows(o_half[0], o_half[1], R), _merge_rows(l_half[0], l_half[1], R)
            if others is None:
                _st(o_ref, o2, cols)
                _st(l_ref, lc, cols)
                continue
            la, lb = l0_ref[:, cols], l1_ref[:, cols]
            m = jnp.maximum(jnp.maximum(la, lb), lc)
            ea, eb, ec = jnp.exp(la - m), jnp.exp(lb - m), jnp.exp(lc - m)
            den = ea + eb + ec
            attn = (ea * o0_ref[:, cols] + eb * o1_ref[:, cols] + ec * o2) / den
            attn_ref[:, cols] = attn
            L_ref[:, cols] = m + jnp.log(den)
            z = z_ref[:, cols]
            ag = attn * (z * _sigmoid(z))
            ag_ref[:, cols] = ag.astype(ag_ref.dtype)

    cur = lambda n: n
    prev = lambda n: jnp.maximum(n - 1, 0)
    if has_prev:
        in_specs = [geom.spec(AW, cq, cur), geom.spec(AW, ck, prev), geom.spec(AW, ck, cur),
                    geom.spec(AW, cv, prev), geom.spec(AW, cv, cur)]
        args = [qv] * 5
    else:
        in_specs = [geom.spec(AW, cq, cur), geom.spec(AW, ck, cur), geom.spec(AW, cv, cur)]
        args = [qv] * 3
    nd = len(geom.outer) + (1 if has_prev else 0)
    biases = [b.reshape(NH // 2, 2 * H2, b.shape[2]) for b in biases]
    in_specs += [pl.BlockSpec(b.shape, lambda *ids: (0, 0, 0)) for b in biases]
    grid = geom.outer + ((geom.n_tiles,) if has_prev else ())
    oshape = geom.view_shape(AW)
    tile = geom.spec(AW, 0, cur)
    if others is None:
        o, l = pl.pallas_call(
            body, name=f"attn_fwd{gi}", grid=grid, in_specs=in_specs, out_specs=[tile, tile],
            out_shape=[_sds(oshape, F32), _sds(oshape, F32)],
            compiler_params=_cp(("arbitrary",) * nd),
        )(*args, *biases)
        return geom.unview(o), geom.unview(l)
    os_, ls_, rest = others
    attn, Lb, ag = pl.pallas_call(
        body, name=f"attn_fwd{gi}_merge", grid=grid, in_specs=in_specs + [tile] * 5,
        out_specs=[tile, tile, tile],
        out_shape=[_sds(oshape, F32), _sds(oshape, F32), _sds(oshape, CDT)],
        compiler_params=_cp(("arbitrary",) * nd),
    )(*args, *biases, *[geom.view(a) for a in (*os_, *ls_, rest)])
    return geom.unview(attn), geom.unview(Lb), geom.unview(ag)


def _pool_counts(r, LS, win):
    l = lax.broadcasted_iota(jnp.int32, (LS, 1), 0)
    return jnp.minimum(16 * l + (r + 1), win).astype(F32)


def _pool_fwd(rest, pool_w, pool_scale):
    S = rest.shape[0]
    LS = S // R16
    r3 = rest.reshape(R16, LS, REST_W)
    PG = 128

    def body(u_ref, z_ref, pw_ref, ps_ref, pt_ref, mp_ref, pg_ref, e_ref):
        g = pl.program_id(0)
        win = jnp.left_shift(2, g)
        row = lax.broadcasted_iota(jnp.int32, (LS, PG), 0)
        for r in range(R16):
            e_ref[R16 + r] = u_ref[r]
        for r in range(1, R16):
            e_ref[r] = jnp.where(row == 0, 0.0, pltpu.roll(u_ref[r], 1, 0))

        def double(s):
            for i in range(2 * R16 - 1, 2 * s - 1, -1):
                e_ref[i] = e_ref[i] + e_ref[i - s]

        double(1)
        for k in (1, 2, 3):
            pl.when(g >= k)(functools.partial(double, 2 ** k))
        pw = pw_ref[...].astype(CDT)
        for r in range(R16):
            u = u_ref[r]
            pooled = e_ref[R16 + r] / _pool_counts(r, LS, win) - u
            pc = pooled.astype(CDT)
            mp = _dot(pc, pw, NN)
            z = z_ref[r]
            pg = (mp * ps_ref[...]) * (z * _sigmoid(z))
            mp_ref[r] = mp
            pg_ref[r] = pg.astype(pg_ref.dtype)
            pt_ref[:, r * LS:(r + 1) * LS] = pooled.T.astype(pt_ref.dtype)

    col = lambda off: pl.BlockSpec((R16, LS, PG), lambda g: (0, 0, off + g))
    tsp = pl.BlockSpec((PG, S), lambda g: (g, 0))
    pt, mp, pg = pl.pallas_call(
        body, name="pool_fwd", grid=(4,),
        in_specs=[col(24), col(4), pl.BlockSpec((None, PG, PG), lambda g: (g, 0, 0)),
                  pl.BlockSpec((1, PG), lambda g: (0, g))],
        out_specs=[tsp, col(0), col(0)],
        out_shape=[_sds((AW, S), CDT), _sds((R16, LS, AW), F32), _sds((R16, LS, AW), CDT)],
        scratch_shapes=[pltpu.VMEM((2 * R16, LS, PG), F32)],
        compiler_params=_cp(("parallel",)),
    )(r3, r3, pool_w, pool_scale)
    return pt, mp.reshape(S, AW), pg.reshape(S, AW)


def _branch_merge(ag, pg, wab, wpb, rest):
    S = ag.shape[0]
    tm, tn = 512, 512

    def body(ag_ref, pg_ref, wa_ref, wp_ref, ga_ref, gp_ref, m_ref, mt_ref):
        ya = _dot(ag_ref[...], wa_ref[...], NN)
        yp = _dot(pg_ref[...], wp_ref[...], NN)
        mg = _sigmoid(ga_ref[...]) * ya + _sigmoid(gp_ref[...]) * yp
        m_ref[...] = mg.astype(m_ref.dtype)
        mt_ref[...] = mg.T.astype(mt_ref.dtype)

    a = pl.BlockSpec((tm, AW), lambda i, j: (i, 0))
    w = pl.BlockSpec((AW, tn), lambda i, j: (0, j))
    o = pl.BlockSpec((tm, tn), lambda i, j: (i, j))
    return pl.pallas_call(
        body, name="branch_merge", grid=(S // tm, D // tn),
        in_specs=[a, a, w, w, pl.BlockSpec((tm, tn), lambda i, j: (i, 2 + j)),
                  pl.BlockSpec((tm, tn), lambda i, j: (i, 4 + j))],
        out_specs=[o, pl.BlockSpec((tn, tm), lambda i, j: (j, i))],
        out_shape=[_sds((S, D), CDT), _sds((D, S), CDT)],
        compiler_params=_cp(("parallel", "parallel")),
    )(ag, pg, wab, wpb, rest, rest)


def _merge_out_loss(ag, pg, rest, wab, wpb, wout, x, target, gate, final_g):
    S = x.shape[0]
    LS = S // R16

    def body(ag_ref, pg_ref, ga_ref, gp_ref, wa_ref, wp_ref, w_ref, x_ref, t_ref, gate_ref, fg_ref,
             dx2_ref, dmo_ref, loss_ref, gfg_ref, dgate_ref, dw_ref):
        i = pl.program_id(0)

        @pl.when(i == 0)
        def _():
            loss_ref[...] = jnp.zeros_like(loss_ref)
            gfg_ref[...] = jnp.zeros_like(gfg_ref)
            dgate_ref[...] = jnp.zeros_like(dgate_ref)
            dw_ref[...] = jnp.zeros_like(dw_ref)

        swap = _swap16()
        ya = _dot(_ld(ag_ref), wa_ref[...], NN)
        yp = _dot(_ld(pg_ref), wp_ref[...], NN)
        mg = (_sigmoid(_ld(ga_ref)) * ya + _sigmoid(_ld(gp_ref)) * yp).astype(CDT)
        m_nat = _dot(swap, mg, NN)
        mo = _dot(m_nat.astype(CDT), w_ref[...], NN)
        gate, fg = gate_ref[...], fg_ref[...]
        x2 = x_ref[...] + gate * mo
        r2 = lax.rsqrt(jnp.mean(x2 * x2, axis=-1, keepdims=True) + EPS)
        xn2 = x2 * r2
        diff = xn2 * fg - t_ref[...]
        loss = jnp.sum(jnp.sum(diff * diff, axis=-1, keepdims=True), axis=0, keepdims=True)
        dy = diff * (1.0 / D)
        dxn2 = dy * fg
        dx2 = r2 * (dxn2 - xn2 * jnp.mean(dxn2 * xn2, axis=-1, keepdims=True))
        dx2_ref[...] = dx2
        dmo = (dx2 * gate).astype(CDT)
        _st(dmo_ref, _dot(swap, dmo, NN))
        loss_ref[...] += jnp.broadcast_to(loss * (0.5 / D), loss_ref.shape)
        gfg_ref[...] += jnp.sum(dy * xn2, axis=0, keepdims=True)
        dgate_ref[...] += jnp.sum(dx2 * mo, axis=0, keepdims=True)
        dw_ref[...] += _dot(m_nat.T.astype(CDT), dmo, NN)

    def slab(w, cblk):
        return pl.BlockSpec((R16, R16, w), lambda i: (0, i, cblk))

    nat = pl.BlockSpec((TOK, D), lambda i: (i, 0))
    vec = pl.BlockSpec((1, D), lambda i: (0, 0))
    r3 = rest.reshape(R16, LS, REST_W)
    return pl.pallas_call(
        body, name="merge_out_loss", grid=(S // TOK,),
        in_specs=[slab(AW, 0), slab(AW, 0), slab(D, 1), slab(D, 2), _resident((AW, D)), _resident((AW, D)),
                  _resident((D, D)), nat, nat, vec, vec],
        out_specs=[nat, slab(D, 0), pl.BlockSpec((1, 128), lambda i: (0, 0)), vec, vec,
                   pl.BlockSpec((D, D), lambda i: (0, 0))],
        out_shape=[_sds((S, D), F32), _sds((R16, LS, D), CDT), _sds((1, 128), F32), _sds((1, D), F32),
                   _sds((1, D), F32), _sds((D, D), F32)],
        compiler_params=_cp(("arbitrary",)),
    )(ag.reshape(R16, LS, AW), pg.reshape(R16, LS, AW), r3, r3, wab, wpb, wout, x, target, gate, final_g)


def _out_loss(merged, wout, x, target, gate, final_g):
    S = x.shape[0]

    def body(m_ref, w_ref, x_ref, t_ref, gate_ref, fg_ref, dx2_ref, dmo_ref, loss_ref, gfg_ref, dgate_ref):
        i = pl.program_id(0)

        @pl.when(i == 0)
        def _():
            loss_ref[...] = jnp.zeros_like(loss_ref)
            gfg_ref[...] = jnp.zeros_like(gfg_ref)
            dgate_ref[...] = jnp.zeros_like(dgate_ref)

        swap = _swap16()
        mo = _dot(_dot(swap, _ld(m_ref), NN).astype(CDT), w_ref[...], NN)
        gate, fg = gate_ref[...], fg_ref[...]
        x2 = x_ref[...] + gate * mo
        r2 = lax.rsqrt(jnp.mean(x2 * x2, axis=-1, keepdims=True) + EPS)
        xn2 = x2 * r2
        diff = xn2 * fg - t_ref[...]
        loss = jnp.sum(jnp.sum(diff * diff, axis=-1, keepdims=True), axis=0, keepdims=True)
        dy = diff * (1.0 / D)
        dxn2 = dy * fg
        dx2 = r2 * (dxn2 - xn2 * jnp.mean(dxn2 * xn2, axis=-1, keepdims=True))
        dx2_ref[...] = dx2
        _st(dmo_ref, _dot(swap, (dx2 * gate).astype(CDT), NN))
        loss_ref[...] += jnp.broadcast_to(loss * (0.5 / D), loss_ref.shape)
        gfg_ref[...] += jnp.sum(dy * xn2, axis=0, keepdims=True)
        dgate_ref[...] += jnp.sum(dx2 * mo, axis=0, keepdims=True)

    slab = pl.BlockSpec((R16, R16, D), lambda i: (0, i, 0))
    nat = pl.BlockSpec((TOK, D), lambda i: (i, 0))
    vec = pl.BlockSpec((1, D), lambda i: (0, 0))
    return pl.pallas_call(
        body, name="out_loss", grid=(S // TOK,),
        in_specs=[slab, _resident((D, D)), nat, nat, vec, vec],
        out_specs=[nat, slab, pl.BlockSpec((1, 128), lambda i: (0, 0)), vec, vec],
        out_shape=[_sds((S, D), F32), _sds((R16, S // R16, D), CDT), _sds((1, 128), F32), _sds((1, D), F32),
                   _sds((1, D), F32)],
        compiler_params=_cp(("arbitrary",)),
    )(merged, wout, x, target, gate, final_g)


def _bwd_gates(dmo, wout, wab, wpb, rest, ag, pg, attn, mp, pool_scale):
    S = dmo.shape[0]
    tm = 256

    def body(dmo_ref, wo_ref, wa_ref, wp_ref, za_ref, zp_ref, ga0, ga1, gp0, gp1, ag_ref, pg_ref, at_ref, mp_ref,
             ps_ref, dwa_ref, dwp_ref, dr_ref, dat_ref, dl_ref, dmp_ref, dps_ref):
        i = pl.program_id(0)

        @pl.when(i == 0)
        def _():
            dps_ref[...] = jnp.zeros_like(dps_ref)
            dwa_ref[...] = jnp.zeros_like(dwa_ref)
            dwp_ref[...] = jnp.zeros_like(dwp_ref)

        dm = _dot(dmo_ref[...], wo_ref[...], NT)
        sa = _sigmoid(jnp.concatenate([ga0[...], ga1[...]], axis=1))
        sp = _sigmoid(jnp.concatenate([gp0[...], gp1[...]], axis=1))
        dya = (dm * sa).astype(CDT)
        dyp = (dm * sp).astype(CDT)
        ag, pg = ag_ref[...], pg_ref[...]
        dwa_ref[...] += _dot(ag.astype(F32).T.astype(CDT), dya, NN)
        dwp_ref[...] += _dot(pg.astype(F32).T.astype(CDT), dyp, NN)
        ya = _dot(ag, wa_ref[...], NN)
        yp = _dot(pg, wp_ref[...], NN)
        dr_ref[:, 1024:2048] = (dm * ya * sa * (1.0 - sa)).astype(dr_ref.dtype)
        dr_ref[:, 2048:3072] = (dm * yp * sp * (1.0 - sp)).astype(dr_ref.dtype)
        da = _dot(dya, wa_ref[...], NT)
        dp = _dot(dyp, wp_ref[...], NT)
        za = za_ref[...]
        sga = _sigmoid(za)
        attn = at_ref[...]
        dattn = da * (za * sga)
        dr_ref[:, 0:512] = (da * attn * (sga * (1.0 + za * (1.0 - sga)))).astype(dr_ref.dtype)
        dat_ref[...] = dattn.astype(dat_ref.dtype)
        prod = dattn * attn
        low = lax.broadcasted_iota(jnp.int32, (1, 2 * HD), 1) < HD
        for j in range(NH // 2):
            x2 = prod[:, 2 * HD * j:2 * HD * (j + 1)]
            s0 = jnp.sum(jnp.where(low, x2, 0.0), axis=1, keepdims=True)
            s1 = jnp.sum(jnp.where(low, 0.0, x2), axis=1, keepdims=True)
            dl_ref[:, 2 * HD * j:2 * HD * (j + 1)] = jnp.where(low, s0, s1)
        zp = zp_ref[...]
        sgp = _sigmoid(zp)
        mpre = mp_ref[...]
        ps = ps_ref[...]
        dmixed = dp * (zp * sgp)
        dr_ref[:, 512:1024] = (dp * (mpre * ps) * (sgp * (1.0 + zp * (1.0 - sgp)))).astype(dr_ref.dtype)
        dps_ref[...] += jnp.sum(dmixed * mpre, axis=0, keepdims=True)
        dmp_ref[...] = (dmixed * ps).astype(dmp_ref.dtype)

    full = lambda shape: pl.BlockSpec(shape, lambda i: (0, 0))
    tD = pl.BlockSpec((tm, D), lambda i: (i, 0))
    tA = pl.BlockSpec((tm, AW), lambda i: (i, 0))
    rb = lambda c: pl.BlockSpec((tm, AW), lambda i: (i, c))
    return pl.pallas_call(
        body, name="bwd_gates", grid=(S // tm,),
        in_specs=[tD, full((D, D)), full((AW, D)), full((AW, D)), rb(0), rb(1), rb(2), rb(3), rb(4), rb(5),
                  tA, tA, tA, tA, full((1, AW))],
        out_specs=[full((AW, D)), full((AW, D)), pl.BlockSpec((tm, 3072), lambda i: (i, 0)), tA, tA, tA, full((1, AW))],
        out_shape=[_sds((AW, D), F32), _sds((AW, D), F32), _sds((S, REST_W), CDT), _sds((S, AW), CDT),
                   _sds((S, AW), F32), _sds((S, AW), CDT), _sds((1, AW), F32)],
        compiler_params=_cp(("arbitrary",), vmem=56),
    )(dmo, wout, wab, wpb, rest, rest, rest, rest, rest, rest, ag, pg, attn, mp, pool_scale)


def _pool_bwd(dmp, pt, pool_w, drest):
    S = dmp.shape[0]
    LS = S // R16
    PG = 128
    d3 = dmp.reshape(R16, LS, AW)
    dr3 = drest.reshape(R16, LS, REST_W)

    def body(d_ref, pt_ref, pw_ref, dr_in, du_ref, dpw_ref, g_ref, s_ref):
        del dr_in
        g = pl.program_id(0)
        win = jnp.left_shift(2, g)
        row = lax.broadcasted_iota(jnp.int32, (LS, PG), 0)
        pw = pw_ref[...].astype(CDT)
        dpw = jnp.zeros((PG, PG), F32)
        for r in range(R16):
            dm = d_ref[r]
            dpw = dpw + _dot(pt_ref[:, r * LS:(r + 1) * LS], dm, NN)
            dpooled = _dot(dm, pw, NT)
            g_ref[r] = dpooled
            sc = dpooled / _pool_counts(r, LS, win)
            s_ref[r] = sc
            if r < R16 - 1:
                s_ref[R16 + r] = jnp.where(row == LS - 1, 0.0, pltpu.roll(sc, LS - 1, 0))
        dpw_ref[...] = dpw

        def double(s):
            for i in range(0, 2 * R16 - 2 * s):
                s_ref[i] = s_ref[i] + s_ref[i + s]

        double(1)
        for k in (1, 2, 3):
            pl.when(g >= k)(functools.partial(double, 2 ** k))
        for r in range(R16):
            du_ref[r] = (s_ref[r] - g_ref[r]).astype(du_ref.dtype)

    du, dpw = pl.pallas_call(
        body, name="pool_bwd", grid=(4,),
        in_specs=[pl.BlockSpec((R16, LS, PG), lambda g: (0, 0, g)), pl.BlockSpec((PG, S), lambda g: (g, 0)),
                  pl.BlockSpec((None, PG, PG), lambda g: (g, 0, 0)), pl.BlockSpec(memory_space=pl.ANY)],
        out_specs=[pl.BlockSpec((R16, LS, PG), lambda g: (0, 0, 24 + g)),
                   pl.BlockSpec((None, PG, PG), lambda g: (g, 0, 0))],
        out_shape=[_sds((R16, LS, REST_W), CDT), _sds((4, PG, PG), F32)],
        scratch_shapes=[pltpu.VMEM((R16, LS, PG), F32), pltpu.VMEM((2 * R16, LS, PG), F32)],
        input_output_aliases={3: 0},
        compiler_params=_cp(("parallel",)),
    )(d3, pt, pool_w, dr3)
    return du.reshape(S, REST_W), dpw


def _attn_bwd(qkv, dattn, Lb, dlb, biases, geom, dqkv_prev):
    S = qkv.shape[0]
    gi, T, R, has_prev, N = geom.gi, geom.T, geom.rows, geom.has_prev, geom.n_tiles
    H2 = T // 2
    nd_outer = len(geom.outer)
    qv = geom.view(qkv)
    cq, ck, cv = 3 * gi, 3 * gi + 1, 3 * gi + 2
    W3 = 3 * AW

    def heads(q_ref, kp_ref, kc_ref, vp_ref, vc_ref, do_ref, L_ref, dl_ref, ba_ref, bb_ref, pm, dsa_ref, dsb_ref, emit):
        q_lo, q_hi = _split_rows(_ld(q_ref), R)
        do_lo, do_hi = _split_rows(_ld(do_ref), R)
        L_lo, L_hi = _split_rows(_ld(L_ref), R)
        dl_lo, dl_hi = _split_rows(_ld(dl_ref), R)
        kc_lo, kc_hi = _split_rows(_ld(kc_ref), R)
        vc_lo, vc_hi = _split_rows(_ld(vc_ref), R)
        k_b = jnp.concatenate([kc_lo, kc_hi], axis=0)
        v_b = jnp.concatenate([vc_lo, vc_hi], axis=0)
        if has_prev:
            k_a = jnp.concatenate([_split_rows(_ld(kp_ref), R)[1], kc_lo], axis=0)
            v_a = jnp.concatenate([_split_rows(_ld(vp_ref), R)[1], vc_lo], axis=0)
        else:
            k_a, v_a = k_b, v_b
        low = lax.broadcasted_iota(jnp.int32, (1, 2 * HD), 1) < HD
        for j in range(NH // 2):
            sl = slice(2 * HD * j, 2 * HD * (j + 1))
            res = []
            for q, k, v, do, L, dl, b_ref, mask, ds_ref in (
                    (q_lo, k_a, v_a, do_lo, L_lo, dl_lo, ba_ref, pm, dsa_ref),
                    (q_hi, k_b, v_b, do_hi, L_hi, dl_hi, bb_ref, None, dsb_ref)):
                q2 = q[:, sl].astype(F32) * 0.125
                do2 = do[:, sl].astype(F32)
                k2, v2 = k[:, sl], v[:, sl]
                L2, dl2 = L[:, sl], dl[:, sl]
                qs = jnp.concatenate([jnp.where(low, q2, 0.0), jnp.where(low, 0.0, q2)], axis=0).astype(q.dtype)
                dom = jnp.concatenate([jnp.where(low, do2, 0.0), jnp.where(low, 0.0, do2)], axis=0).astype(q.dtype)
                Lc = jnp.concatenate([L2[:, 0:1], L2[:, HD:HD + 1]], axis=0)
                dlc = jnp.concatenate([dl2[:, 0:1], dl2[:, HD:HD + 1]], axis=0)
                s = _dot(qs, k2, NT) + b_ref[j]
                if mask is not None:
                    s = s + mask
                p = jnp.exp(s - Lc)
                ds = p * (_dot(dom, v2, NT) - dlc)
                ds_ref[j] += ds
                dsc = ds.astype(q.dtype)
                dq_st = _dot(dsc, k2, NN)
                res.append((jnp.where(low, dq_st[:H2], dq_st[H2:]) * 0.125, _dot(dsc, qs, TN),
                            _dot(p.astype(q.dtype), dom, TN)))
            (dq_lo, dk_a, dv_a), (dq_hi, dk_b, dv_b) = res
            dq = _merge_rows(dq_lo, dq_hi, R)
            if has_prev:
                emit(sl, dq, _merge_rows(dk_a[H2:] + dk_b[:H2], dk_b[H2:], R),
                     _merge_rows(dv_a[H2:] + dv_b[:H2], dv_b[H2:], R), dk_a[:H2], dv_a[:H2])
            else:
                emit(sl, dq, _merge_rows(dk_a[:H2] + dk_b[:H2], dk_a[H2:] + dk_b[H2:], R),
                     _merge_rows(dv_a[:H2] + dv_b[:H2], dv_a[H2:] + dv_b[H2:], R), None, None)

    def first_step():
        ok = pl.program_id(nd_outer) == 0 if has_prev else pl.program_id(0) == 0
        for a in range(nd_outer if has_prev else 0):
            ok = ok & (pl.program_id(a) == 0)
        return ok

    if has_prev:
        def body(q_ref, kp_ref, kc_ref, vp_ref, vc_ref, do_ref, L_ref, dl_ref, ba_ref, bb_ref, prev_in, out_ref,
                 dsa_ref, dsb_ref, hold, new, pp):
            del prev_in
            s_id = pl.program_id(nd_outer)

            @pl.when(first_step())
            def _():
                dsa_ref[...] = jnp.zeros_like(dsa_ref)
                dsb_ref[...] = jnp.zeros_like(dsb_ref)

            @pl.when(s_id < N)
            def _():
                def emit(sl, dq, dk, dv, dk_prev, dv_prev):
                    new[:, sl] = dq
                    new[:, slice(AW + sl.start, AW + sl.stop)] = dk
                    new[:, slice(2 * AW + sl.start, 2 * AW + sl.stop)] = dv
                    pp[:, sl] = dk_prev
                    pp[:, slice(AW + sl.start, AW + sl.stop)] = dv_prev

                heads(q_ref, kp_ref, kc_ref, vp_ref, vc_ref, do_ref, L_ref, dl_ref, ba_ref, bb_ref,
                      _prev_mask(s_id, H2, T), dsa_ref, dsb_ref, emit)

            @pl.when((s_id >= 1) & (s_id < N))
            def _():
                _st(out_ref, hold[:, 0:AW], slice(0, AW))
                late = _merge_rows(jnp.zeros((H2, 2 * AW), F32), pp[...], R)
                _st(out_ref, hold[:, AW:W3] + late, slice(AW, W3))

            @pl.when(s_id == N)
            def _():
                _st(out_ref, hold[...])

            @pl.when(s_id < N)
            def _():
                hold[...] = new[...]

        cur = lambda n: jnp.minimum(n, N - 1)
        prev = lambda n: jnp.clip(n - 1, 0, N - 1)
        delayed = lambda n: jnp.maximum(n - 1, 0)
        in_specs = [geom.spec(AW, cq, cur), geom.spec(AW, ck, prev), geom.spec(AW, ck, cur),
                    geom.spec(AW, cv, prev), geom.spec(AW, cv, cur),
                    geom.spec(AW, 0, cur), geom.spec(AW, 0, cur), geom.spec(AW, 0, cur)]
        args = [qv] * 5 + [geom.view(dattn), geom.view(Lb), geom.view(dlb)]
        out_spec = geom.spec(W3, gi, delayed)
        grid = geom.outer + (N + 1,)
        scratch = [pltpu.VMEM((T, W3), F32), pltpu.VMEM((T, W3), F32), pltpu.VMEM((H2, 2 * AW), F32)]
    else:
        def body(q_ref, kc_ref, vc_ref, do_ref, L_ref, dl_ref, ba_ref, bb_ref, prev_in, out_ref, dsa_ref, dsb_ref):
            del prev_in

            @pl.when(first_step())
            def _():
                dsa_ref[...] = jnp.zeros_like(dsa_ref)
                dsb_ref[...] = jnp.zeros_like(dsb_ref)

            def emit(sl, dq, dk, dv, dk_prev, dv_prev):
                _st(out_ref, dq, sl)
                _st(out_ref, dk, slice(AW + sl.start, AW + sl.stop))
                _st(out_ref, dv, slice(2 * AW + sl.start, 2 * AW + sl.stop))

            heads(q_ref, None, kc_ref, None, vc_ref, do_ref, L_ref, dl_ref, ba_ref, bb_ref, None, dsa_ref, dsb_ref,
                  emit)

        cur = lambda n: n
        in_specs = [geom.spec(AW, cq, cur), geom.spec(AW, ck, cur), geom.spec(AW, cv, cur),
                    geom.spec(AW, 0, cur), geom.spec(AW, 0, cur), geom.spec(AW, 0, cur)]
        args = [qv] * 3 + [geom.view(dattn), geom.view(Lb), geom.view(dlb)]
        out_spec = geom.spec(W3, gi, cur)
        grid = geom.outer
        scratch = []

    whole = lambda b: pl.BlockSpec(b.shape, lambda *ids: (0, 0, 0))
    biases = [b.reshape(NH // 2, 2 * H2, b.shape[2]) for b in biases]
    in_specs += [whole(b) for b in biases]
    in_specs.append(pl.BlockSpec(memory_space=pl.ANY))
    if dqkv_prev is None:
        last, aliases = biases[0], {}
    else:
        last, aliases = geom.view(dqkv_prev), {len(in_specs) - 1: 0}
    dq, dsa, dsb = pl.pallas_call(
        body, name=f"attn_bwd{gi}", grid=grid, in_specs=in_specs,
        out_specs=[out_spec] + [whole(b) for b in biases],
        out_shape=[_sds(geom.view_shape(QKV_W), CDT)] + [_sds(b.shape, F32) for b in biases],
        scratch_shapes=scratch,
        input_output_aliases=aliases,
        compiler_params=_cp(("arbitrary",) * len(grid), vmem=56),
    )(*args, *biases, last)
    return geom.unview(dq), [d.reshape(NH, H2, d.shape[2]) for d in (dsa, dsb)]


def _h_bwd(dh, dx2, x, norm_g, scale):
    S = x.shape[0]

    def body(dh_ref, dx2_ref, x_ref, g_ref, sc_ref, gx_ref, dsh_ref, dsc_ref, dng_ref, nat_ref):
        i = pl.program_id(0)

        @pl.when(i == 0)
        def _():
            dsh_ref[...] = jnp.zeros_like(dsh_ref)
            dsc_ref[...] = jnp.zeros_like(dsc_ref)
            dng_ref[...] = jnp.zeros_like(dng_ref)

        for r in range(R16):
            nat_ref[:, r, :] = dh_ref[r]
        dh = jnp.concatenate([nat_ref[l] for l in range(R16)], axis=0)
        g = g_ref[...]
        one_sc = 1.0 + sc_ref[...]
        xv = x_ref[...]
        r1 = lax.rsqrt(jnp.mean(xv * xv, axis=-1, keepdims=True) + EPS)
        xn = xv * r1
        dhx = dh * xn
        dxn = dh * (g * one_sc)
        dx1 = r1 * (dxn - xn * jnp.mean(dxn * xn, axis=-1, keepdims=True))
        gx_ref[...] = dx2_ref[...] + dx1
        dsh_ref[...] += jnp.sum(dh, axis=0, keepdims=True)
        dsc_ref[...] += jnp.sum(dhx * g, axis=0, keepdims=True)
        dng_ref[...] += jnp.sum(dhx * one_sc, axis=0, keepdims=True)

    nat = pl.BlockSpec((TOK, D), lambda i: (i, 0))
    vec = pl.BlockSpec((1, D), lambda i: (0, 0))
    return pl.pallas_call(
        body, name="h_bwd", grid=(S // TOK,),
        in_specs=[pl.BlockSpec((R16, R16, D), lambda i: (0, i, 0)), nat, nat, vec, vec],
        out_specs=[nat, vec, vec, vec],
        out_shape=[_sds((S, D), F32), _sds((1, D), F32), _sds((1, D), F32), _sds((1, D), F32)],
        scratch_shapes=[pltpu.VMEM((R16, R16, D), F32)],
        compiler_params=_cp(("arbitrary",)),
    )(dh, dx2, x, norm_g, scale)


def _local_step(x, target, shift, scale, gate, norm_g, w_in_full, pool_w, pool_scale, wab, wpb, wout, rel_bias,
                final_g):
    st = _local_pre(x, shift, scale, norm_g, rel_bias)
    st = _local_grads(st, target, gate, w_in_full, pool_w, pool_scale, wab, wpb, wout, final_g)
    d_rel_bias = _local_bias_grads(st, _BIAS_GRADS_EARLY + _BIAS_GRADS_LATE)
    gx, dmod, dng = _local_input_grads(st, w_in_full, norm_g, scale)
    return dict(loss=st["loss"], grad_x=gx, dw_in=st["dw_in"], dw_ab=st["dw_ab"], dw_pb=st["dw_pb"],
                dw_out=st["dw_out"], d_norm_g=dng, d_pool_w=st["dpw"], d_pool_scale=st["dps"],
                d_rel_bias=d_rel_bias, d_final_g=st["gfg"], dmod=dmod)


def _local_pre(x, shift, scale, norm_g, rel_bias, dep=None):
    S = x.shape[0]
    LS = S // R16
    geoms = [_Geom(gi, LS) for gi in range(3)]
    idx_np = [[g.bucket_index(sub) for sub in range(2)] for g in geoms]
    buckets = [[tuple(int(b) for b in np.unique(ix) if b >= 0) for ix in pair] for pair in idx_np]
    idxs = [[jnp.asarray(ix) for ix in pair] for pair in idx_np]
    biases = [[_bias_table(idxs[gi][sub], buckets[gi][sub], rel_bias, gi, sub, dep) for sub in range(2)]
              for gi in range(3)]
    h = _norm_mod(x, norm_g, shift, scale, dep).reshape(S, D)
    return dict(geoms=geoms, idxs=idxs, buckets=buckets, biases=biases, x=x, h=h, ht=_tcast(h, "h_transpose"))


def _local_grads(st, target, gate, w_in_full, pool_w, pool_scale, wab, wpb, wout, final_g):
    geoms, idxs, biases, x, h, ht = (st[k] for k in ("geoms", "idxs", "biases", "x", "h", "ht"))
    S = h.shape[0]
    LS = S // R16
    qkv, rest = _proj(h, w_in_full)

    os_, ls_ = [], []
    for gi in range(2):
        o, l = _attn_fwd(qkv, biases[gi], geoms[gi])
        os_.append(o)
        ls_.append(l)
    attn, Lb, ag = _attn_fwd(qkv, biases[2], geoms[2], others=(os_, ls_, rest))
    pt, mp, pg = _pool_fwd(rest, pool_w, pool_scale)
    dx2, dmo3, loss, gfg, dgate, dw_out = _merge_out_loss(ag, pg, rest, wab, wpb, wout, x, target, gate, final_g)
    dmo = dmo3.reshape(S, D)

    dw_ab, dw_pb, drest, dattn, dlb, dmp, dps = _bwd_gates(dmo, wout, wab, wpb, rest, ag, pg, attn, mp, pool_scale)
    drest, dpw = _pool_bwd(dmp, pt, pool_w, drest)

    dqkv = None
    dsums = []
    for gi in range(3):
        dqkv, dsum = _attn_bwd(qkv, dattn, Lb, dlb, biases[gi], geoms[gi], dqkv)
        dsums.append(dsum)

    dw_in = _mm_tn(ht, dqkv, n_total=8192, out_blk=lambda j: j, name="dw_in_qkv", with_low=True)
    dw_in, dw_in_low = _mm_tn(ht, drest, n_total=8192, out_blk=_rest_blk, name="dw_in_rest", prev=dw_in, with_low=True)
    return dict(st, loss=loss[0, 0], dw_in=dw_in, dw_in_low=dw_in_low, dw_ab=dw_ab, dw_pb=dw_pb, dw_out=dw_out, dpw=dpw, dps=dps, gfg=gfg,
                dgate=dgate, dsums=dsums, dqkv=dqkv, drest=drest, dx2=dx2)


def _local_bias_grads(st, which, dep=None):
    cols = []
    for gi in range(3):
        got = [_bias_grad(st["dsums"][gi][sub], st["idxs"][gi][sub], st["buckets"][gi][sub], gi, sub, dep)
               for sub in range(2) if (gi, sub) in which]
        cols.append(sum(got) if got else jnp.zeros((32, NH), F32))
    return jnp.concatenate(cols, axis=1)


_BIAS_GRADS_EARLY = ((0, 0), (0, 1), (1, 0), (2, 0))
_BIAS_GRADS_LATE = ((1, 1), (2, 1))


def _dh_h_bwd(dqkv, drest, w, dx2, x, norm_g, scale, dep=None):
    S = x.shape[0]
    dep_specs, dep_args = _after(dep)

    def body(a_ref, b_ref, w_ref, dx2_ref, x_ref, g_ref, sc_ref, *rest):
        gx_ref, dsh_ref, dsc_ref, dng_ref, nat_ref = rest[-5:]
        i = pl.program_id(0)

        @pl.when(i == 0)
        def _():
            dsh_ref[...] = jnp.zeros_like(dsh_ref)
            dsc_ref[...] = jnp.zeros_like(dsc_ref)
            dng_ref[...] = jnp.zeros_like(dng_ref)

        a, b = _ld(a_ref), _ld(b_ref)
        acc = _dot(a[:, 0:512], _w512(w_ref, 0), NT)
        for j in range(1, 9):
            acc = acc + _dot(a[:, 512 * j:512 * (j + 1)], _w512(w_ref, j), NT)
        off = 0
        for lo, hi in REST_COLS:
            for j in range(lo, hi):
                acc = acc + _dot(b[:, off:off + 512], _w512(w_ref, j), NT)
                off += 512
        for r in range(R16):
            nat_ref[:, r, :] = acc[R16 * r:R16 * (r + 1)]
        dh = jnp.concatenate([nat_ref[l] for l in range(R16)], axis=0)
        g = g_ref[...]
        one_sc = 1.0 + sc_ref[...]
        xv = x_ref[...]
        r1 = lax.rsqrt(jnp.mean(xv * xv, axis=-1, keepdims=True) + EPS)
        xn = xv * r1
        dhx = dh * xn
        dxn = dh * (g * one_sc)
        dx1 = r1 * (dxn - xn * jnp.mean(dxn * xn, axis=-1, keepdims=True))
        gx_ref[...] = dx2_ref[...] + dx1
        dsh_ref[...] += jnp.sum(dh, axis=0, keepdims=True)
        dsc_ref[...] += jnp.sum(dhx * g, axis=0, keepdims=True)
        dng_ref[...] += jnp.sum(dhx * one_sc, axis=0, keepdims=True)

    nat = pl.BlockSpec((TOK, D), lambda i: (i, 0))
    vec = pl.BlockSpec((1, D), lambda i: (0, 0))
    slab = lambda W: pl.BlockSpec((R16, R16, W), lambda i: (0, i, 0))
    return pl.pallas_call(
        body, name="dh_h_bwd", grid=(S // TOK,),
        in_specs=[slab(QKV_W), slab(REST_W), _resident((8, D, 1024)), nat, nat, vec, vec] + dep_specs,
        out_specs=[nat, vec, vec, vec],
        out_shape=[_sds((S, D), F32), _sds((1, D), F32), _sds((1, D), F32), _sds((1, D), F32)],
        scratch_shapes=[pltpu.VMEM((R16, R16, D), F32)],
        compiler_params=_cp(("arbitrary",), vmem=56),
    )(dqkv, drest, w, dx2, x, norm_g, scale, *dep_args)


def _local_input_grads(st, w_in_full, norm_g, scale, dep=None):
    S = st["h"].shape[0]
    LS = S // R16
    gx, dsh, dsc, dng = _dh_h_bwd(st["dqkv"].reshape(R16, LS, QKV_W), st["drest"].reshape(R16, LS, REST_W),
                                  w_in_full, st["dx2"], st["x"], norm_g, scale, dep)
    return gx, jnp.concatenate([dsh, dsc, st["dgate"]], axis=1), dng


def _my_place():
    return lax.axis_index("x"), lax.axis_index("y"), lax.axis_index("c")


def _block_of(ref, axis, idx, width):
    if axis is None:
        return ref.at[idx]
    start = pl.multiple_of(idx * width, width)
    if axis == 0:
        return ref.at[pl.ds(start, width)]
    return ref.at[:, pl.ds(start, width)]


_SEM = pl.BlockSpec(memory_space=pltpu.SEMAPHORE)
_EFFECT = pltpu.SideEffectType.DATAFLOW_SIDE_EFFECTING


def _split_start(name, srcs, lands, ncopy, plan, dep=None):
    arrs = list(srcs) + list(lands)
    na, ns = len(arrs), len(srcs)
    dep_specs, dep_args = _after(dep)

    def body(*refs):
        send_sems, recv_sems, token = refs[na + len(dep_args)], refs[na + len(dep_args) + 1], refs[-1]
        for k, (s, d, dev) in enumerate(plan(refs[:ns], refs[ns:na])):
            pltpu.make_async_remote_copy(src_ref=s, dst_ref=d, send_sem=send_sems.at[k], recv_sem=recv_sems.at[k],
                                         device_id=dev, device_id_type=MESH).start()
        token[...] = jnp.zeros_like(token)

    outs = pl.pallas_call(
        body, name=name,
        out_shape=(pltpu.SemaphoreType.DMA((ncopy,)), pltpu.SemaphoreType.DMA((ncopy,)),
                   *[pltpu.HBM(a.shape, a.dtype) for a in arrs], _sds((8, 128), F32)),
        in_specs=[_HBM] * na + dep_specs,
        out_specs=(_SEM, _SEM, *[_HBM] * na, pl.BlockSpec(memory_space=pltpu.VMEM)),
        input_output_aliases={i: 2 + i for i in range(na)},
        compiler_params=pltpu.CompilerParams(has_side_effects=_EFFECT),
    )(*[pltpu.with_memory_space_constraint(a, pltpu.HBM) for a in arrs], *dep_args)
    return (outs[0], outs[1]), list(outs[2:2 + ns]), list(outs[2 + ns:2 + na]), outs[-1]


def _split_wait(name, sems, srcs, lands, plan, after):
    arrs = list(srcs) + list(lands)
    na, ns = len(arrs), len(srcs)
    after = list(after) if isinstance(after, (list, tuple)) else [after]

    def body(*refs):
        send_sems, recv_sems = refs[na], refs[na + 1]
        for k, (s, d, dev) in enumerate(plan(refs[:ns], refs[ns:na])):
            cp = pltpu.make_async_remote_copy(src_ref=s, dst_ref=d, send_sem=send_sems.at[k], recv_sem=recv_sems.at[k],
                                              device_id=dev, device_id_type=MESH)
            cp.wait_send()
            cp.wait_recv()

    outs = pl.pallas_call(
        body, name=name, out_shape=[pltpu.HBM(a.shape, a.dtype) for a in arrs],
        in_specs=[_HBM] * na + [_SEM, _SEM] + [pl.BlockSpec(memory_space=pl.ANY)] * len(after),
        out_specs=[_HBM] * na,
        input_output_aliases={i: i for i in range(na)},
        compiler_params=pltpu.CompilerParams(has_side_effects=_EFFECT),
    )(*arrs, sems[0], sems[1], *after)
    return list(outs[:ns]), list(outs[ns:])


def _gather_chips_plan(layout):
    def plan(src, land):
        x, y, c = _my_place()
        me = 4 * x + 2 * y + c
        mine = [_block_of(land[p], layout[p][0], me, layout[p][1]) for p in range(len(land))]
        return [(mine[p], mine[p], (cx, cy, c)) for p in range(len(land)) for cx, cy in [(1 - x, y), (x, 1 - y)]]
    return plan


def _pair_plan(layout):
    def plan(src, land):
        x, y, c = _my_place()
        n = len(layout)
        return [(_block_of(src[p], layout[p][0], 2 * q + (1 - c), layout[p][1]), land[p].at[q], (x, y, 1 - c))
                for p in range(n) for q in range(4)] + [(src[p], land[p], (x, y, 1 - c)) for p in range(n, len(src))]
    return plan


def _chips_plan(n):
    def plan(src, land):
        x, y, c = _my_place()
        chips = [(1 - x, y), (x, 1 - y), (1 - x, 1 - y)]
        return [(src[p].at[2 * cx + cy], land[p].at[2 * x + y], (cx, cy, c)) for p in range(n) for cx, cy in chips] + \
               [(src[p], land[p].at[2 * x + y], (cx, cy, c)) for p in range(n, len(src)) for cx, cy in chips]
    return plan


def _pair_share(gathered):
    n = len(gathered)

    def body(*refs):
        land = refs[n:2 * n]
        send_sems, recv_sems, fwd_send, fwd_recv = refs[2 * n:]
        x, y, c = _my_place()
        k_dg = 3 - (2 * x + y)
        south = c == 0
        pass_on = 4 * jnp.where(south, 1 - x, x) + 2 * jnp.where(south, y, 1 - y) + c
        to = (jnp.where(south, x, 1 - x), jnp.where(south, 1 - y, y), c)

        def forward(p):
            return pltpu.make_async_remote_copy(
                src_ref=land[p].at[pass_on], dst_ref=land[p].at[pass_on], send_sem=fwd_send.at[p],
                recv_sem=fwd_recv.at[p], device_id=to, device_id_type=MESH)

        def to_sibling(p, k):
            blk = land[p].at[2 * k + c]
            return pltpu.make_async_remote_copy(
                src_ref=blk, dst_ref=blk, send_sem=send_sems.at[4 * p + k], recv_sem=recv_sems.at[4 * p + k],
                device_id=(x, y, 1 - c), device_id_type=MESH)

        for p in range(n):
            forward(p).start()
        for p in range(n):
            for k in range(4):
                @pl.when(k != k_dg)
                def _():
                    to_sibling(p, k).start()
        for p in range(n):
            dg = land[p].at[2 * k_dg + c]
            pltpu.make_async_remote_copy(src_ref=dg, dst_ref=dg, send_sem=fwd_send.at[p], recv_sem=fwd_recv.at[p],
                                         device_id=to, device_id_type=MESH).wait_recv()
            for k in range(4):
                @pl.when(k == k_dg)
                def _():
                    to_sibling(p, k).start()
        for p in range(n):
            forward(p).wait_send()
            for k in range(4):
                to_sibling(p, k).wait_send()
                blk = land[p].at[2 * k + 1 - c]
                pltpu.make_async_remote_copy(
                    src_ref=blk, dst_ref=blk, send_sem=send_sems.at[4 * p + k], recv_sem=recv_sems.at[4 * p + k],
                    device_id=(x, y, 1 - c), device_id_type=MESH).wait_recv()

    return pl.pallas_call(
        body, name="gather_pair_share", in_specs=[_HBM] * n, out_specs=[_HBM] * n,
        out_shape=[_sds(g.shape, g.dtype) for g in gathered],
        input_output_aliases={p: p for p in range(n)},
        scratch_shapes=[pltpu.SemaphoreType.DMA((4 * n,)), pltpu.SemaphoreType.DMA((4 * n,)),
                        pltpu.SemaphoreType.DMA((n,)), pltpu.SemaphoreType.DMA((n,))],
    )(*gathered)


def _place_slot(arrs, slot, n_slots, name, out_dtype=None):
    n = len(arrs)

    def body(s_ref, *refs):
        for p in range(n):
            refs[n + p][...] = refs[p][...].astype(refs[n + p].dtype)

    def spec(a):
        tail = a.shape[-2:]
        return pl.BlockSpec((None,) + tail, lambda i, s_ref: (s_ref[0], 0, 0))

    in_specs = [spec(a) if a.ndim == 3 else pl.BlockSpec(a.shape, lambda i, s_ref: (0, 0)) for a in arrs]
    return pl.pallas_call(
        body, name=name,
        grid_spec=pltpu.PrefetchScalarGridSpec(num_scalar_prefetch=1, grid=(1,), in_specs=in_specs,
                                               out_specs=[spec(a) for a in arrs]),
        out_shape=[_sds((n_slots,) + a.shape[-2:], out_dtype or a.dtype) for a in arrs],
        compiler_params=_cp(("arbitrary",)),
    )(slot, *arrs)


def _cols_from_blocks(w8, name):
    _, R, C = w8.shape

    def body(w_ref, o_ref):
        for e in range(8):
            o_ref[:, C * e:C * (e + 1)] = w_ref[e]

    return pl.pallas_call(body, name=name, out_shape=_sds((R, 8 * C), w8.dtype), compiler_params=_cp())(w8)


def _all_gather_small(srcs, name):
    n = len(srcs)
    vmem = pl.BlockSpec(memory_space=pltpu.VMEM)

    def body(*refs):
        src, out = refs[:n], refs[n:2 * n]
        send_sems, recv_sems = refs[2 * n:]
        x, y, c = _my_place()
        me = 4 * x + 2 * y + c
        flip = lambda v, f: 1 - v if f else v
        peers = [(flip(x, fx), flip(y, fy), flip(c, fc)) for fx in (0, 1) for fy in (0, 1) for fc in (0, 1)][1:]
        cps = []
        for p in range(n):
            out[p][me] = src[p][...]
            for k, peer in enumerate(peers):
                cps.append(pltpu.make_async_remote_copy(
                    src_ref=src[p], dst_ref=out[p].at[me], send_sem=send_sems.at[7 * p + k],
                    recv_sem=recv_sems.at[7 * p + k], device_id=peer, device_id_type=MESH))
        for cp in cps:
            cp.start()
        for cp in cps:
            cp.wait()

    return pl.pallas_call(
        body, name=name, in_specs=[vmem] * n, out_specs=[vmem] * n,
        out_shape=[_sds((8,) + a.shape, a.dtype) for a in srcs],
        scratch_shapes=[pltpu.SemaphoreType.DMA((7 * n,)), pltpu.SemaphoreType.DMA((7 * n,))],
    )(*srcs)


def _adaln_exchange(c, w_ada, b_part):
    vmem = pl.BlockSpec(memory_space=pltpu.VMEM)

    def body(c_ref, w_ref, b_ref, call_ref, mod_ref, part_ref, send_sems, recv_sems):
        x, y, c = _my_place()
        me = 4 * x + 2 * y + c
        flip = lambda v, f: 1 - v if f else v
        peers = [(flip(x, fx), flip(y, fy), flip(c, fc)) for fx in (0, 1) for fy in (0, 1) for fc in (0, 1)][1:]
        call_ref[me] = c_ref[...]
        first = [pltpu.make_async_remote_copy(src_ref=c_ref, dst_ref=call_ref.at[me], send_sem=send_sems.at[k],
                                              recv_sem=recv_sems.at[k], device_id=peer, device_id_type=MESH)
                 for k, peer in enumerate(peers)]
        for cp in first:
            cp.start()
        for cp in first:
            cp.wait()
        c_all = jnp.concatenate([call_ref[j] for j in range(8)], axis=0)
        part_ref[...] = _dot(c_all.astype(CDT), w_ref[...].astype(CDT), NN) + b_ref[...]
        mod_ref[me] = part_ref[pl.ds(me, 1), :]
        second = [pltpu.make_async_remote_copy(
            src_ref=part_ref.at[pl.ds(4 * px + 2 * py + pc, 1)], dst_ref=mod_ref.at[me], send_sem=send_sems.at[7 + k],
            recv_sem=recv_sems.at[7 + k], device_id=(px, py, pc), device_id_type=MESH)
            for k, (px, py, pc) in enumerate(peers)]
        for cp in second:
            cp.start()
        for cp in second:
            cp.wait()

    return pl.pallas_call(
        body, name="adaln_exchange", in_specs=[vmem] * 3, out_specs=[vmem] * 2,
        out_shape=[_sds((8, 1, D), F32), _sds((8, 1, w_ada.shape[1]), F32)],
        scratch_shapes=[pltpu.VMEM((8, w_ada.shape[1]), F32), pltpu.SemaphoreType.DMA((14,)),
                        pltpu.SemaphoreType.DMA((14,))],
        compiler_params=_cp(),
    )(c, w_ada, b_part)


def _pair_add(dw, recv, axis, width, cidx, name):
    if axis == 0:
        tr = min(width, 256)
        C = dw.shape[1]
        grid = (4, width // tr)
        nb = width // tr
        own = pl.BlockSpec((tr, C), lambda q, i, c_ref: ((2 * q + c_ref[0]) * nb + i, 0))
        oth = pl.BlockSpec((None, tr, C), lambda q, i, c_ref: (q, i, 0))
    else:
        R = dw.shape[0]
        tr = min(R, 1024)
        grid = (4, R // tr)
        own = pl.BlockSpec((tr, width), lambda q, i, c_ref: (i, 2 * q + c_ref[0]))
        oth = pl.BlockSpec((None, tr, width), lambda q, i, c_ref: (q, i, 0))

    def body(c_ref, a_ref, b_ref, o_ref):
        o_ref[...] = (a_ref[...] + b_ref[...].astype(F32)).astype(o_ref.dtype)

    return pl.pallas_call(
        body, name=name,
        grid_spec=pltpu.PrefetchScalarGridSpec(num_scalar_prefetch=1, grid=grid, in_specs=[own, oth], out_specs=oth),
        out_shape=_sds(recv.shape, CDT), compiler_params=_cp(("parallel", "parallel")),
    )(cidx, dw, recv)


def _add_pairs(xs, ys, name):
    n = len(xs)

    def body(*refs):
        for p in range(n):
            refs[2 * n + p][...] = refs[p][...] + refs[n + p][...]

    return pl.pallas_call(body, name=name, out_shape=[_sds(a.shape, F32) for a in xs])(*xs, *ys)


def _adam_math(w, g, m, v):
    m = ADAM_B1 * m + (1.0 - ADAM_B1) * g
    v = ADAM_B2 * v + (1.0 - ADAM_B2) * (g * g)
    m_hat = m / (1.0 - ADAM_B1 ** ADAM_STEP)
    v_hat = v / (1.0 - ADAM_B2 ** ADAM_STEP)
    delta = -ADAM_LR * (m_hat / (jnp.sqrt(v_hat) + ADAM_EPS) + ADAM_WD * w)
    return delta, m, v


def _adam(gparts, w, m, v, name):
    P, R, C = gparts.shape
    tr = R if R <= 256 else 256

    def body(g_ref, w_ref, m_ref, v_ref, go_ref, d_ref, mo_ref, vo_ref):
        g = g_ref[0].astype(F32)
        for i in range(1, P):
            g = g + g_ref[i].astype(F32)
        d, mn, vn = _adam_math(w_ref[...], g, m_ref[...], v_ref[...])
        go_ref[...] = g
        d_ref[...] = d
        mo_ref[...] = mn
        vo_ref[...] = vn

    t = pl.BlockSpec((tr, C), lambda i: (i, 0))
    return pl.pallas_call(
        body, name=name, grid=(R // tr,),
        in_specs=[pl.BlockSpec((P, tr, C), lambda i: (0, i, 0)), t, t, t], out_specs=[t, t, t, t],
        out_shape=[_sds((R, C), F32)] * 4, compiler_params=_cp(("parallel",)),
    )(gparts, w, m, v)


def _sum_parts(gparts, name):
    P, R, C = gparts.shape

    def body(g_ref, o_ref):
        g = g_ref[0]
        for i in range(1, P):
            g = g + g_ref[i]
        o_ref[...] = g

    return pl.pallas_call(body, name=name, out_shape=_sds((R, C), F32))(gparts)


def _adam_w_ada(c_t, dm, w, m, v):
    R, C = w.shape

    def body(c_ref, dm_ref, w_ref, m_ref, v_ref, go_ref, d_ref, mo_ref, vo_ref):
        g = c_ref[:, 0:1] * dm_ref[0:1, :]
        for b in range(1, 8):
            g = g + c_ref[:, b:b + 1] * dm_ref[b:b + 1, :]
        d, mn, vn = _adam_math(w_ref[...], g, m_ref[...], v_ref[...])
        go_ref[...] = g
        d_ref[...] = d
        mo_ref[...] = mn
        vo_ref[...] = vn

    return pl.pallas_call(body, name="adam_w_ada", out_shape=[_sds((R, C), F32)] * 4,
                          compiler_params=_cp())(c_t, dm, w, m, v)


def kernel(x, c, norm_g, w_ada, b_ada, w_in, pool_w, pool_scale, w_attn_br, w_pool_br, w_out, rel_bias, final_g, loss_target, m_norm_g, m_w_ada, m_b_ada, m_w_in, m_pool_w, m_pool_scale, m_w_attn_br, m_w_pool_br, m_w_out, m_rel_bias, m_final_g, v_norm_g, v_w_ada, v_b_ada, v_w_in, v_pool_w, v_pool_scale, v_w_attn_br, v_w_pool_br, v_w_out, v_rel_bias, v_final_g):
    S = x.shape[1]
    px, py, pc = _my_place()
    me = 4 * px + 2 * py + pc
    cidx = jnp.reshape(pc, (1,)).astype(jnp.int32)

    layout = [(1, 1024), (1, 128), (1, 128), (0, 128)]
    final_g2 = final_g.reshape(1, D)
    me1 = jnp.reshape(me, (1,)).astype(jnp.int32)

    b_part = lax.dynamic_slice(b_ada, (0, me * 384), (1, 384))
    c_all, mod_all = _adaln_exchange(c, w_ada[0], b_part)
    c_all = c_all.reshape(8, D)
    mod = mod_all.reshape(1, 3 * D)
    shift, scale, gate = mod[:, :D], mod[:, D:2 * D], mod[:, 2 * D:]

    gplan = _gather_chips_plan([(None, 1)] * 4)
    lands = _place_slot([w_in[0], w_attn_br[0], w_pool_br[0], w_out[0]], me1, 8, "place_own_shards", CDT)
    g_sems, _, lands, tok = _split_start("gather_chips_start", [], lands, 8, gplan, dep=mod_all)
    st = _local_pre(x[0], shift, scale, norm_g, rel_bias, dep=tok)
    _, lands = _split_wait("gather_chips_wait", g_sems, [], lands, gplan,
                           [st["ht"]] + [b for pair in st["biases"] for b in pair])
    w_in_full, wab8, wpb8, wout8 = _pair_share(lands)
    wab, wpb = _cols_from_blocks(wab8, "w_attn_br_cols"), _cols_from_blocks(wpb8, "w_pool_br_cols")
    wout = wout8.reshape(D, D)

    st = _local_grads(st, loss_target[0], gate, w_in_full, pool_w[0], pool_scale, wab, wpb, wout, final_g2)

    early = [st["dpw"].reshape(AW, 128), st["dps"], st["gfg"], jnp.broadcast_to(st["loss"], (1, 128))]
    dws = [st["dw_in"], st["dw_ab"], st["dw_pb"], st["dw_out"]]
    sent = [st["dw_in_low"]] + dws[1:] + early
    pplan = _pair_plan(layout)
    pair_shapes = [(4, w, a.shape[1]) if ax == 0 else (4, a.shape[0], w) for a, (ax, w) in zip(dws, layout)]
    p_sems, sent, recv, tok = _split_start(
        "reduce_pair_start", sent,
        [lax.empty(s, a.dtype) for s, a in zip(pair_shapes + [a.shape for a in early], sent)], 20, pplan)
    d_rel_bias = _local_bias_grads(st, _BIAS_GRADS_EARLY, dep=tok)
    sent, recv = _split_wait("reduce_pair_wait", p_sems, sent, recv, pplan, d_rel_bias)
    parts = [_pair_add(dw, r, ax, w, cidx, f"pair_add{i}")
             for i, (dw, r, (ax, w)) in enumerate(zip([dws[0]] + sent[1:4], recv[:4], layout))]
    parts += _add_pairs(sent[4:], recv[4:], "pair_add_smalls")

    slot1 = jnp.reshape(2 * px + py, (1,)).astype(jnp.int32)
    cplan = _chips_plan(4)
    c_sems, parts, lands, tok = _split_start("reduce_chips_start", parts,
                                             _place_slot(parts, slot1, 4, "place_own_partials"), 24, cplan)
    d_rel_bias = d_rel_bias + _local_bias_grads(st, _BIAS_GRADS_LATE, dep=tok)
    grad_x, dmod, d_norm_g = _local_input_grads(st, w_in_full, norm_g, scale, dep=tok)

    dmod_all, ng_all, rb_all = _all_gather_small([dmod, d_norm_g, d_rel_bias], "gather_late")
    o_norm_g = _adam(ng_all, norm_g, m_norm_g, v_norm_g, "adam_norm_g")
    o_rel_bias = _adam(rb_all, rel_bias, m_rel_bias, v_rel_bias, "adam_rel_bias")
    o_b_ada = _adam(dmod_all, b_ada, m_b_ada, v_b_ada, "adam_b_ada")
    dm_mine = lax.dynamic_slice(dmod_all.reshape(8, 3 * D), (0, me * 384), (8, 384))
    o_w_ada = [o[None] for o in _adam_w_ada(c_all.T, dm_mine, w_ada[0], m_w_ada[0], v_w_ada[0])]

    _, (g_in, g_ab, g_pb, g_out, g_pw, g_ps, g_fg, g_loss) = _split_wait(
        "reduce_chips_wait", c_sems, parts, lands, cplan, [o_w_ada[0], o_b_ada[0], o_norm_g[0], o_rel_bias[0]])
    o_w_in = [o[None] for o in _adam(g_in, w_in[0], m_w_in[0], v_w_in[0], "adam_w_in")]
    o_w_ab = [o[None] for o in _adam(g_ab, w_attn_br[0], m_w_attn_br[0], v_w_attn_br[0], "adam_w_attn_br")]
    o_w_pb = [o[None] for o in _adam(g_pb, w_pool_br[0], m_w_pool_br[0], v_w_pool_br[0], "adam_w_pool_br")]
    o_w_out = [o[None] for o in _adam(g_out, w_out[0], m_w_out[0], v_w_out[0], "adam_w_out")]
    o_pool_w = [o.reshape(pool_w.shape) for o in _adam(g_pw, pool_w.reshape(AW, 128), m_pool_w.reshape(AW, 128),
                                                       v_pool_w.reshape(AW, 128), "adam_pool_w")]
    o_pool_scale = _adam(g_ps, pool_scale, m_pool_scale, v_pool_scale, "adam_pool_scale")
    o_final_g = [o.reshape(D) for o in _adam(g_fg, final_g2, m_final_g.reshape(1, D), v_final_g.reshape(1, D),
                                             "adam_final_g")]
    loss = _sum_parts(g_loss, "sum_loss")[0, 0]

    per_w = [o_norm_g, o_w_ada, o_b_ada, o_w_in, o_pool_w, o_pool_scale, o_w_ab, o_w_pb, o_w_out, o_rel_bias, o_final_g]
    outs = [loss, grad_x[None]]
    for k in range(4):
        outs += [o[k] for o in per_w]
    return tuple(outs)
```

```python
import functools
import math

import numpy as np
import jax
import jax.numpy as jnp
from jax import lax
from jax.experimental import pallas as pl
from jax.experimental.pallas import tpu as pltpu

F32 = jnp.float32
CDT = jnp.bfloat16
D = 1024
HD = 64
NH = 8
AW = 512
QKV_W = 4608
REST_W = 3584
R16 = 16
EPS = 1e-6
NEG = -1e30
N_BACK = 128
ADAM_LR, ADAM_B1, ADAM_B2, ADAM_EPS, ADAM_WD, ADAM_STEP = 0.001, 0.9, 0.999, 1e-08, 0.01, 10
MESH = pl.DeviceIdType.MESH
VMEM_MB = 1024 * 1024


def _cp(sem=None, vmem=48, **kw):
    if sem is not None:
        kw["dimension_semantics"] = sem
    return pltpu.CompilerParams(vmem_limit_bytes=vmem * VMEM_MB, **kw)


def _sds(shape, dtype):
    return jax.ShapeDtypeStruct(tuple(shape), dtype)


def _rest_blk(jj):
    return jnp.where(jj == 0, 9, jnp.where(jj == 6, 10, jj + 10))


def _sigmoid(z):
    return 0.5 * jnp.tanh(0.5 * z) + 0.5


def _dot(a, b, dims):
    return lax.dot_general(a, b, (dims, ((), ())), preferred_element_type=F32)


NN = ((1,), (0,))
NT = ((1,), (1,))
TN = ((0,), (0,))


def _ld(ref):
    if len(ref.shape) == 2:
        return ref[...]
    return jnp.concatenate([ref[a] for a in range(ref.shape[0])], axis=0)


def _st(ref, val, cols=None):
    val = val.astype(ref.dtype)
    if len(ref.shape) == 2:
        if cols is None:
            ref[...] = val
        else:
            ref[:, cols] = val
        return
    rows = ref.shape[1]
    for a in range(ref.shape[0]):
        if cols is None:
            ref[a] = val[a * rows:(a + 1) * rows]
        else:
            ref[a, :, cols] = val[a * rows:(a + 1) * rows]


REST_COLS = ((9, 10), (11, 16), (10, 11))
_HBM = pl.BlockSpec(memory_space=pltpu.HBM)


def _resident(shape):
    return pl.BlockSpec(shape, lambda *ids: (0,) * len(shape), pipeline_mode=pl.Buffered(1))


def _w512(w_ref, j):
    return w_ref[j // 2, :, 512 * (j % 2):512 * (j % 2 + 1)]


def _proj(h, w):
    S = h.shape[0]
    tm = 512

    def body(a_ref, w_ref, q_ref, r_ref):
        a = a_ref[...]
        for j in range(9):
            q_ref[:, 512 * j:512 * (j + 1)] = _dot(a, _w512(w_ref, j), NN).astype(q_ref.dtype)
        off = 0
        for lo, hi in REST_COLS:
            for j in range(lo, hi):
                r_ref[:, off:off + 512] = _dot(a, _w512(w_ref, j), NN)
                off += 512

    return pl.pallas_call(
        body, name="proj", grid=(S // tm,),
        in_specs=[pl.BlockSpec((tm, D), lambda i: (i, 0)), _resident((8, D, 1024))],
        out_specs=[pl.BlockSpec((tm, QKV_W), lambda i: (i, 0)), pl.BlockSpec((tm, REST_W), lambda i: (i, 0))],
        out_shape=[_sds((S, QKV_W), CDT), _sds((S, REST_W), F32)],
        compiler_params=_cp(("parallel",), vmem=56),
    )(h, w)


def _after(dep):
    return ([], []) if dep is None else ([pl.BlockSpec(memory_space=pl.ANY)], [dep])


def _dh(dqkv, drest, w, dep=None):
    S = dqkv.shape[0]
    tm = 512
    dep_specs, dep_args = _after(dep)

    def body(a_ref, b_ref, w_ref, *rest):
        o_ref = rest[-1]
        acc = _dot(a_ref[:, 0:512], _w512(w_ref, 0), NT)
        for j in range(1, 9):
            acc = acc + _dot(a_ref[:, 512 * j:512 * (j + 1)], _w512(w_ref, j), NT)
        off = 0
        for lo, hi in REST_COLS:
            for j in range(lo, hi):
                acc = acc + _dot(b_ref[:, off:off + 512], _w512(w_ref, j), NT)
                off += 512
        o_ref[...] = acc

    return pl.pallas_call(
        body, name="dh", grid=(S // tm,),
        in_specs=[pl.BlockSpec((tm, QKV_W), lambda i: (i, 0)), pl.BlockSpec((tm, REST_W), lambda i: (i, 0)),
                  _resident((8, D, 1024))] + dep_specs,
        out_specs=pl.BlockSpec((tm, D), lambda i: (i, 0)),
        out_shape=_sds((S, D), F32),
        compiler_params=_cp(("parallel",), vmem=56),
    )(dqkv, drest, w, *dep_args)


def _mm_tn(at, b, *, n_total, out_blk, name, prev=None, tn=512, with_low=False):
    R, S = at.shape
    nb = b.shape[1] // tn
    n_out = 2 if with_low else 1

    def body(*refs):
        res = _dot(refs[0][...], refs[1][...], NN)
        refs[-n_out][...] = res
        if with_low:
            refs[-1][...] = res.astype(CDT)

    in_specs = [_resident((R, S)), pl.BlockSpec((S, tn), lambda j: (0, j))]
    args = [at, b]
    aliases = {}
    if prev is not None:
        prev = list(prev) if with_low else [prev]
        in_specs += [pl.BlockSpec(memory_space=pl.ANY)] * n_out
        args += prev
        aliases = {2 + k: k for k in range(n_out)}
    out_spec = pl.BlockSpec((R, tn), lambda j: (0, out_blk(j)))
    outs = pl.pallas_call(
        body, name=name, grid=(nb,), in_specs=in_specs,
        out_specs=[out_spec] * n_out,
        out_shape=[_sds((R, n_total), F32), _sds((R, n_total), CDT)][:n_out],
        input_output_aliases=aliases,
        compiler_params=_cp(("parallel",)),
    )(*args)
    return tuple(outs) if with_low else outs[0]


def _tcast(a, name):
    M, C = a.shape
    tm = 512

    def body(a_ref, o_ref):
        o_ref[...] = a_ref[...].astype(F32).T.astype(o_ref.dtype)

    return pl.pallas_call(
        body, name=name, grid=(M // tm,),
        in_specs=[pl.BlockSpec((tm, C), lambda i: (i, 0))],
        out_specs=pl.BlockSpec((C, tm), lambda i: (0, i)),
        out_shape=_sds((C, M), CDT), compiler_params=_cp(("parallel",)),
    )(a)


TOK = R16 * R16


def _swap16():
    i = lax.broadcasted_iota(jnp.int32, (TOK, TOK), 0)
    j = lax.broadcasted_iota(jnp.int32, (TOK, TOK), 1)
    return jnp.where(j == jnp.bitwise_and(i, 15) * 16 + jnp.right_shift(i, 4), 1.0, 0.0).astype(CDT)


def _norm_mod(x, norm_g, shift, scale, dep=None):
    S = x.shape[0]
    dep_specs, dep_args = _after(dep)

    def body(x_ref, g_ref, sh_ref, sc_ref, *rest):
        xv = x_ref[...]
        xn = xv * lax.rsqrt(jnp.mean(xv * xv, axis=-1, keepdims=True) + EPS)
        h = (xn * (g_ref[...] * (1.0 + sc_ref[...])) + sh_ref[...]).astype(CDT)
        _st(rest[-1], _dot(_swap16(), h, NN))

    vec = pl.BlockSpec((1, D), lambda i: (0, 0))
    return pl.pallas_call(
        body, name="norm_mod", grid=(S // TOK,),
        in_specs=[pl.BlockSpec((TOK, D), lambda i: (i, 0)), vec, vec, vec] + dep_specs,
        out_specs=pl.BlockSpec((R16, R16, D), lambda i: (0, i, 0)),
        out_shape=_sds((R16, S // R16, D), CDT), compiler_params=_cp(("parallel",)),
    )(x, norm_g, shift, scale, *dep_args)


def _t5_bucket_np(n):
    nf = np.maximum(n, 1).astype(np.float32)
    large = 16 + (np.log(nf / np.float32(16)) / np.float32(math.log(2048 / 16)) * np.float32(16)).astype(np.int32)
    large = np.minimum(large, 31)
    return np.where(n < 16, n, large).astype(np.int32)


class _Geom:
    def __init__(self, gi, LS):
        self.gi, self.LS = gi, LS
        self.dil = (1, 4, 16)[gi]
        if gi == 0:
            self.lead, self.rows, self.n_tiles, self.has_prev = (R16,), 16, LS // 16, True
            a = np.arange(R16)[:, None]
            ll = np.arange(16)[None, :]
            self.pos = (16 * ll + a).reshape(-1)
        elif gi == 1:
            self.lead, self.rows, self.n_tiles, self.has_prev = (4,), 64, LS // 64, True
            a = np.arange(4)[:, None]
            ll = np.arange(64)[None, :]
            self.pos = (4 * ll + a).reshape(-1)
        else:
            self.lead, self.rows, self.n_tiles, self.has_prev = (), LS, 1, False
            self.pos = np.arange(LS)
        self.T = self.pos.shape[0]
        self.Tk = 2 * self.T if self.has_prev else self.T
        self.outer = {0: (), 1: (4,), 2: (R16,)}[gi]

    def view(self, arr):
        S, C = arr.shape
        if self.gi == 1:
            return arr.reshape(4, 4, self.LS, C)
        return arr.reshape(R16, self.LS, C)

    def view_shape(self, C):
        return (4, 4, self.LS, C) if self.gi == 1 else (R16, self.LS, C)

    def unview(self, arr):
        return arr.reshape(R16 * self.LS, arr.shape[-1])

    def spec(self, W, cblk, tile):
        if self.gi == 0:
            return pl.BlockSpec((R16, 16, W), lambda n: (0, tile(n), cblk))
        if self.gi == 1:
            return pl.BlockSpec((4, None, 64, W), lambda r, n: (0, r, tile(n), cblk))
        return pl.BlockSpec((None, self.LS, W), lambda r: (r, 0, cblk))

    def halves(self):
        R = self.rows
        idx = np.arange(self.T).reshape(-1, R)
        return idx[:, :R // 2].reshape(-1), idx[:, R // 2:].reshape(-1)

    def bucket_index(self, sub):
        lo, hi = self.halves()
        pos = self.pos
        if sub == 0:
            pq = pos[lo]
            pk = np.concatenate([pos[hi] - self.T, pos[lo]]) if self.has_prev else np.concatenate([pos[lo], pos[hi]])
        else:
            pq = pos[hi]
            pk = np.concatenate([pos[lo], pos[hi]])
        dist = pq[:, None] - pk[None, :]
        ok = (dist >= 0) & (dist <= N_BACK)
        bucket = _t5_bucket_np(np.clip(dist, 0, N_BACK) * self.dil)
        return np.where(ok, bucket, -1).astype(np.int32)


def _bias_table(idx, buckets, rel_bias, gi, sub, dep=None):
    T, Tk = idx.shape
    tr = 32
    dep_specs, dep_args = _after(dep)

    def body(idx_ref, rb_ref, *rest):
        h = pl.program_id(0)
        ix = idx_ref[...]
        acc = jnp.full(ix.shape, NEG, F32)
        for b in buckets:
            acc = jnp.where(ix == b, rb_ref[b, gi * NH + h], acc)
        rest[-1][...] = acc

    return pl.pallas_call(
        body, name=f"bias_table{gi}{'ab'[sub]}", grid=(NH, T // tr),
        in_specs=[pl.BlockSpec((tr, Tk), lambda h, i: (i, 0)), pl.BlockSpec(memory_space=pltpu.SMEM)] + dep_specs,
        out_specs=pl.BlockSpec((None, tr, Tk), lambda h, i: (h, i, 0)),
        out_shape=_sds((NH, T, Tk), F32), compiler_params=_cp(("parallel", "parallel")),
    )(idx, rel_bias, *dep_args)


def _bias_grad(dsum, idx, buckets, gi, sub, dep=None):
    _, T, Tk = dsum.shape
    dep_specs, dep_args = _after(dep)

    def body(ds_ref, idx_ref, *rest):
        o_ref, r_ref = rest[-2], rest[-1]
        r_ref[...] = jnp.zeros_like(r_ref)
        ix, ds = idx_ref[...], ds_ref[...]
        for b in buckets:
            r_ref[b:b + 1, :] = jnp.sum(jnp.where(ix == b, ds, 0.0), axis=0, keepdims=True)
        o_ref[...] = jnp.broadcast_to(jnp.sum(r_ref[...], axis=1, keepdims=True), (32, 128))

    out = pl.pallas_call(
        body, name=f"bias_grad{gi}{'ab'[sub]}", grid=(NH,),
        in_specs=[pl.BlockSpec((None, T, Tk), lambda h: (h, 0, 0)), pl.BlockSpec((T, Tk), lambda h: (0, 0))] + dep_specs,
        out_specs=pl.BlockSpec((None, 32, 128), lambda h: (h, 0, 0)),
        out_shape=_sds((NH, 32, 128), F32), scratch_shapes=[pltpu.VMEM((32, Tk), F32)],
        compiler_params=_cp(("parallel",)),
    )(dsum, idx, *dep_args)
    return out[:, :, 0].T


def _prev_mask(n, T, Tk):
    col = lax.broadcasted_iota(jnp.int32, (1, Tk), 1)
    return jnp.where((col < T) & (n == 0), NEG, 0.0).astype(F32)


def _split_rows(v, R):
    A, half = v.shape[0] // R, R // 2
    via_f32 = v.dtype != F32 and half % 16 != 0
    w = v.astype(F32) if via_f32 else v
    lo = jnp.concatenate([w[a * R:a * R + half] for a in range(A)], axis=0)
    hi = jnp.concatenate([w[a * R + half:(a + 1) * R] for a in range(A)], axis=0)
    return (lo.astype(v.dtype), hi.astype(v.dtype)) if via_f32 else (lo, hi)


def _merge_rows(lo, hi, R):
    half = R // 2
    A = lo.shape[0] // half
    return jnp.concatenate([x[a * half:(a + 1) * half] for a in range(A) for x in (lo, hi)], axis=0)


def _attn_fwd(qkv, biases, geom, others=None):
    S = qkv.shape[0]
    gi, T, R, has_prev = geom.gi, geom.T, geom.rows, geom.has_prev
    H2 = T // 2
    qv = geom.view(qkv)
    cq, ck, cv = 3 * gi, 3 * gi + 1, 3 * gi + 2
    assert others is None or not has_prev

    def body(*refs):
        if has_prev:
            q_ref, kp_ref, kc_ref, vp_ref, vc_ref, ba_ref, bb_ref, o_ref, l_ref = refs
        elif others is not None:
            (q_ref, kc_ref, vc_ref, ba_ref, bb_ref, o0_ref, o1_ref, l0_ref, l1_ref, z_ref,
             attn_ref, L_ref, ag_ref) = refs
        else:
            q_ref, kc_ref, vc_ref, ba_ref, bb_ref, o_ref, l_ref = refs
        q_lo, q_hi = _split_rows(_ld(q_ref), R)
        kc_lo, kc_hi = _split_rows(_ld(kc_ref), R)
        vc_lo, vc_hi = _split_rows(_ld(vc_ref), R)
        k_b = jnp.concatenate([kc_lo, kc_hi], axis=0)
        v_b = jnp.concatenate([vc_lo, vc_hi], axis=0)
        if has_prev:
            n = pl.program_id(len(geom.outer))
            k_a = jnp.concatenate([_split_rows(_ld(kp_ref), R)[1], kc_lo], axis=0)
            v_a = jnp.concatenate([_split_rows(_ld(vp_ref), R)[1], vc_lo], axis=0)
            pm = _prev_mask(n, H2, T)
        else:
            k_a, v_a, pm = k_b, v_b, None
        low = lax.broadcasted_iota(jnp.int32, (1, 2 * HD), 1) < HD
        for j in range(NH // 2):
            cols = slice(2 * HD * j, 2 * HD * (j + 1))
            o_half, l_half = [], []
            for q, k, v, b_ref, mask in ((q_lo, k_a, v_a, ba_ref, pm), (q_hi, k_b, v_b, bb_ref, None)):
                q2 = q[:, cols].astype(F32) * 0.125
                k2, v2 = k[:, cols], v[:, cols]
                qs = jnp.concatenate([jnp.where(low, q2, 0.0), jnp.where(low, 0.0, q2)], axis=0).astype(q.dtype)
                s = _dot(qs, k2, NT) + b_ref[j]
                if mask is not None:
                    s = s + mask
                m = jnp.max(s, axis=1, keepdims=True)
                p = jnp.exp(s - m)
                l = jnp.sum(p, axis=1, keepdims=True)
                pv = _dot(p.astype(v.dtype), v2, NN) / l
                lse = m + jnp.log(l)
                o_half.append(jnp.where(low, pv[:H2], pv[H2:]))
                l_half.append(jnp.where(low, lse[:H2], lse[H2:]))
            o2, lc = _merge_rows(o_half[0], o_half[1], R), _merge_rows(l_half[0], l_half[1], R)
            if others is None:
                _st(o_ref, o2, cols)
                _st(l_ref, lc, cols)
                continue
            la, lb = l0_ref[:, cols], l1_ref[:, cols]
            m = jnp.maximum(jnp.maximum(la, lb), lc)
            ea, eb, ec = jnp.exp(la - m), jnp.exp(lb - m), jnp.exp(lc - m)
            den = ea + eb + ec
            attn = (ea * o0_ref[:, cols] + eb * o1_ref[:, cols] + ec * o2) / den
            attn_ref[:, cols] = attn
            L_ref[:, cols] = m + jnp.log(den)
            z = z_ref[:, cols]
            ag = attn * (z * _sigmoid(z))
            ag_ref[:, cols] = ag.astype(ag_ref.dtype)

    cur = lambda n: n
    prev = lambda n: jnp.maximum(n - 1, 0)
    if has_prev:
        in_specs = [geom.spec(AW, cq, cur), geom.spec(AW, ck, prev), geom.spec(AW, ck, cur),
                    geom.spec(AW, cv, prev), geom.spec(AW, cv, cur)]
        args = [qv] * 5
    else:
        in_specs = [geom.spec(AW, cq, cur), geom.spec(AW, ck, cur), geom.spec(AW, cv, cur)]
        args = [qv] * 3
    nd = len(geom.outer) + (1 if has_prev else 0)
    biases = [b.reshape(NH // 2, 2 * H2, b.shape[2]) for b in biases]
    in_specs += [pl.BlockSpec(b.shape, lambda *ids: (0, 0, 0)) for b in biases]
    grid = geom.outer + ((geom.n_tiles,) if has_prev else ())
    oshape = geom.view_shape(AW)
    tile = geom.spec(AW, 0, cur)
    if others is None:
        o, l = pl.pallas_call(
            body, name=f"attn_fwd{gi}", grid=grid, in_specs=in_specs, out_specs=[tile, tile],
            out_shape=[_sds(oshape, F32), _sds(oshape, F32)],
            compiler_params=_cp(("arbitrary",) * nd),
        )(*args, *biases)
        return geom.unview(o), geom.unview(l)
    os_, ls_, rest = others
    attn, Lb, ag = pl.pallas_call(
        body, name=f"attn_fwd{gi}_merge", grid=grid, in_specs=in_specs + [tile] * 5,
        out_specs=[tile, tile, tile],
        out_shape=[_sds(oshape, F32), _sds(oshape, F32), _sds(oshape, CDT)],
        compiler_params=_cp(("arbitrary",) * nd),
    )(*args, *biases, *[geom.view(a) for a in (*os_, *ls_, rest)])
    return geom.unview(attn), geom.unview(Lb), geom.unview(ag)


def _pool_counts(r, LS, win):
    l = lax.broadcasted_iota(jnp.int32, (LS, 1), 0)
    return jnp.minimum(16 * l + (r + 1), win).astype(F32)


def _pool_fwd(rest, pool_w, pool_scale):
    S = rest.shape[0]
    LS = S // R16
    r3 = rest.reshape(R16, LS, REST_W)
    PG = 128

    def body(u_ref, z_ref, pw_ref, ps_ref, pt_ref, mp_ref, pg_ref, e_ref):
        g = pl.program_id(0)
        win = jnp.left_shift(2, g)
        row = lax.broadcasted_iota(jnp.int32, (LS, PG), 0)
        for r in range(R16):
            e_ref[R16 + r] = u_ref[r]
        for r in range(1, R16):
            e_ref[r] = jnp.where(row == 0, 0.0, pltpu.roll(u_ref[r], 1, 0))

        def double(s):
            for i in range(2 * R16 - 1, 2 * s - 1, -1):
                e_ref[i] = e_ref[i] + e_ref[i - s]

        double(1)
        for k in (1, 2, 3):
            pl.when(g >= k)(functools.partial(double, 2 ** k))
        pw = pw_ref[...].astype(CDT)
        for r in range(R16):
            u = u_ref[r]
            pooled = e_ref[R16 + r] / _pool_counts(r, LS, win) - u
            pc = pooled.astype(CDT)
            mp = _dot(pc, pw, NN)
            z = z_ref[r]
            pg = (mp * ps_ref[...]) * (z * _sigmoid(z))
            mp_ref[r] = mp
            pg_ref[r] = pg.astype(pg_ref.dtype)
            pt_ref[:, r * LS:(r + 1) * LS] = pooled.T.astype(pt_ref.dtype)

    col = lambda off: pl.BlockSpec((R16, LS, PG), lambda g: (0, 0, off + g))
    tsp = pl.BlockSpec((PG, S), lambda g: (g, 0))
    pt, mp, pg = pl.pallas_call(
        body, name="pool_fwd", grid=(4,),
        in_specs=[col(24), col(4), pl.BlockSpec((None, PG, PG), lambda g: (g, 0, 0)),
                  pl.BlockSpec((1, PG), lambda g: (0, g))],
        out_specs=[tsp, col(0), col(0)],
        out_shape=[_sds((AW, S), CDT), _sds((R16, LS, AW), F32), _sds((R16, LS, AW), CDT)],
        scratch_shapes=[pltpu.VMEM((2 * R16, LS, PG), F32)],
        compiler_params=_cp(("parallel",)),
    )(r3, r3, pool_w, pool_scale)
    return pt, mp.reshape(S, AW), pg.reshape(S, AW)


def _branch_merge(ag, pg, wab, wpb, rest):
    S = ag.shape[0]
    tm, tn = 512, 512

    def body(ag_ref, pg_ref, wa_ref, wp_ref, ga_ref, gp_ref, m_ref, mt_ref):
        ya = _dot(ag_ref[...], wa_ref[...], NN)
        yp = _dot(pg_ref[...], wp_ref[...], NN)
        mg = _sigmoid(ga_ref[...]) * ya + _sigmoid(gp_ref[...]) * yp
        m_ref[...] = mg.astype(m_ref.dtype)
        mt_ref[...] = mg.T.astype(mt_ref.dtype)

    a = pl.BlockSpec((tm, AW), lambda i, j: (i, 0))
    w = pl.BlockSpec((AW, tn), lambda i, j: (0, j))
    o = pl.BlockSpec((tm, tn), lambda i, j: (i, j))
    return pl.pallas_call(
        body, name="branch_merge", grid=(S // tm, D // tn),
        in_specs=[a, a, w, w, pl.BlockSpec((tm, tn), lambda i, j: (i, 2 + j)),
                  pl.BlockSpec((tm, tn), lambda i, j: (i, 4 + j))],
        out_specs=[o, pl.BlockSpec((tn, tm), lambda i, j: (j, i))],
        out_shape=[_sds((S, D), CDT), _sds((D, S), CDT)],
        compiler_params=_cp(("parallel", "parallel")),
    )(ag, pg, wab, wpb, rest, rest)


def _merge_out_loss(ag, pg, rest, wab, wpb, wout, x, target, gate, final_g):
    S = x.shape[0]
    LS = S // R16

    def body(ag_ref, pg_ref, ga_ref, gp_ref, wa_ref, wp_ref, w_ref, x_ref, t_ref, gate_ref, fg_ref,
             dx2_ref, dmo_ref, loss_ref, gfg_ref, dgate_ref, dw_ref):
        i = pl.program_id(0)

        @pl.when(i == 0)
        def _():
            loss_ref[...] = jnp.zeros_like(loss_ref)
            gfg_ref[...] = jnp.zeros_like(gfg_ref)
            dgate_ref[...] = jnp.zeros_like(dgate_ref)
            dw_ref[...] = jnp.zeros_like(dw_ref)

        swap = _swap16()
        ya = _dot(_ld(ag_ref), wa_ref[...], NN)
        yp = _dot(_ld(pg_ref), wp_ref[...], NN)
        mg = (_sigmoid(_ld(ga_ref)) * ya + _sigmoid(_ld(gp_ref)) * yp).astype(CDT)
        m_nat = _dot(swap, mg, NN)
        mo = _dot(m_nat.astype(CDT), w_ref[...], NN)
        gate, fg = gate_ref[...], fg_ref[...]
        x2 = x_ref[...] + gate * mo
        r2 = lax.rsqrt(jnp.mean(x2 * x2, axis=-1, keepdims=True) + EPS)
        xn2 = x2 * r2
        diff = xn2 * fg - t_ref[...]
        loss = jnp.sum(jnp.sum(diff * diff, axis=-1, keepdims=True), axis=0, keepdims=True)
        dy = diff * (1.0 / D)
        dxn2 = dy * fg
        dx2 = r2 * (dxn2 - xn2 * jnp.mean(dxn2 * xn2, axis=-1, keepdims=True))
        dx2_ref[...] = dx2
        dmo = (dx2 * gate).astype(CDT)
        _st(dmo_ref, _dot(swap, dmo, NN))
        loss_ref[...] += jnp.broadcast_to(loss * (0.5 / D), loss_ref.shape)
        gfg_ref[...] += jnp.sum(dy * xn2, axis=0, keepdims=True)
        dgate_ref[...] += jnp.sum(dx2 * mo, axis=0, keepdims=True)
        dw_ref[...] += _dot(m_nat.T.astype(CDT), dmo, NN)

    def slab(w, cblk):
        return pl.BlockSpec((R16, R16, w), lambda i: (0, i, cblk))

    nat = pl.BlockSpec((TOK, D), lambda i: (i, 0))
    vec = pl.BlockSpec((1, D), lambda i: (0, 0))
    r3 = rest.reshape(R16, LS, REST_W)
    return pl.pallas_call(
        body, name="merge_out_loss", grid=(S // TOK,),
        in_specs=[slab(AW, 0), slab(AW, 0), slab(D, 1), slab(D, 2), _resident((AW, D)), _resident((AW, D)),
                  _resident((D, D)), nat, nat, vec, vec],
        out_specs=[nat, slab(D, 0), pl.BlockSpec((1, 128), lambda i: (0, 0)), vec, vec,
                   pl.BlockSpec((D, D), lambda i: (0, 0))],
        out_shape=[_sds((S, D), F32), _sds((R16, LS, D), CDT), _sds((1, 128), F32), _sds((1, D), F32),
                   _sds((1, D), F32), _sds((D, D), F32)],
        compiler_params=_cp(("arbitrary",)),
    )(ag.reshape(R16, LS, AW), pg.reshape(R16, LS, AW), r3, r3, wab, wpb, wout, x, target, gate, final_g)


def _out_loss(merged, wout, x, target, gate, final_g):
    S = x.shape[0]

    def body(m_ref, w_ref, x_ref, t_ref, gate_ref, fg_ref, dx2_ref, dmo_ref, loss_ref, gfg_ref, dgate_ref):
        i = pl.program_id(0)

        @pl.when(i == 0)
        def _():
            loss_ref[...] = jnp.zeros_like(loss_ref)
            gfg_ref[...] = jnp.zeros_like(gfg_ref)
            dgate_ref[...] = jnp.zeros_like(dgate_ref)

        swap = _swap16()
        mo = _dot(_dot(swap, _ld(m_ref), NN).astype(CDT), w_ref[...], NN)
        gate, fg = gate_ref[...], fg_ref[...]
        x2 = x_ref[...] + gate * mo
        r2 = lax.rsqrt(jnp.mean(x2 * x2, axis=-1, keepdims=True) + EPS)
        xn2 = x2 * r2
        diff = xn2 * fg - t_ref[...]
        loss = jnp.sum(jnp.sum(diff * diff, axis=-1, keepdims=True), axis=0, keepdims=True)
        dy = diff * (1.0 / D)
        dxn2 = dy * fg
        dx2 = r2 * (dxn2 - xn2 * jnp.mean(dxn2 * xn2, axis=-1, keepdims=True))
        dx2_ref[...] = dx2
        _st(dmo_ref, _dot(swap, (dx2 * gate).astype(CDT), NN))
        loss_ref[...] += jnp.broadcast_to(loss * (0.5 / D), loss_ref.shape)
        gfg_ref[...] += jnp.sum(dy * xn2, axis=0, keepdims=True)
        dgate_ref[...] += jnp.sum(dx2 * mo, axis=0, keepdims=True)

    slab = pl.BlockSpec((R16, R16, D), lambda i: (0, i, 0))
    nat = pl.BlockSpec((TOK, D), lambda i: (i, 0))
    vec = pl.BlockSpec((1, D), lambda i: (0, 0))
    return pl.pallas_call(
        body, name="out_loss", grid=(S // TOK,),
        in_specs=[slab, _resident((D, D)), nat, nat, vec, vec],
        out_specs=[nat, slab, pl.BlockSpec((1, 128), lambda i: (0, 0)), vec, vec],
        out_shape=[_sds((S, D), F32), _sds((R16, S // R16, D), CDT), _sds((1, 128), F32), _sds((1, D), F32),
                   _sds((1, D), F32)],
        compiler_params=_cp(("arbitrary",)),
    )(merged, wout, x, target, gate, final_g)


def _bwd_gates(dmo, wout, wab, wpb, rest, ag, pg, attn, mp, pool_scale):
    S = dmo.shape[0]
    tm = 256

    def body(dmo_ref, wo_ref, wa_ref, wp_ref, za_ref, zp_ref, ga0, ga1, gp0, gp1, ag_ref, pg_ref, at_ref, mp_ref,
             ps_ref, dwa_ref, dwp_ref, dr_ref, dat_ref, dl_ref, dmp_ref, dps_ref):
        i = pl.program_id(0)

        @pl.when(i == 0)
        def _():
            dps_ref[...] = jnp.zeros_like(dps_ref)
            dwa_ref[...] = jnp.zeros_like(dwa_ref)
            dwp_ref[...] = jnp.zeros_like(dwp_ref)

        dm = _dot(dmo_ref[...], wo_ref[...], NT)
        sa = _sigmoid(jnp.concatenate([ga0[...], ga1[...]], axis=1))
        sp = _sigmoid(jnp.concatenate([gp0[...], gp1[...]], axis=1))
        dya = (dm * sa).astype(CDT)
        dyp = (dm * sp).astype(CDT)
        ag, pg = ag_ref[...], pg_ref[...]
        dwa_ref[...] += _dot(ag.astype(F32).T.astype(CDT), dya, NN)
        dwp_ref[...] += _dot(pg.astype(F32).T.astype(CDT), dyp, NN)
        ya = _dot(ag, wa_ref[...], NN)
        yp = _dot(pg, wp_ref[...], NN)
        dr_ref[:, 1024:2048] = (dm * ya * sa * (1.0 - sa)).astype(dr_ref.dtype)
        dr_ref[:, 2048:3072] = (dm * yp * sp * (1.0 - sp)).astype(dr_ref.dtype)
        da = _dot(dya, wa_ref[...], NT)
        dp = _dot(dyp, wp_ref[...], NT)
        za = za_ref[...]
        sga = _sigmoid(za)
        attn = at_ref[...]
        dattn = da * (za * sga)
        dr_ref[:, 0:512] = (da * attn * (sga * (1.0 + za * (1.0 - sga)))).astype(dr_ref.dtype)
        dat_ref[...] = dattn.astype(dat_ref.dtype)
        prod = dattn * attn
        low = lax.broadcasted_iota(jnp.int32, (1, 2 * HD), 1) < HD
        for j in range(NH // 2):
            x2 = prod[:, 2 * HD * j:2 * HD * (j + 1)]
            s0 = jnp.sum(jnp.where(low, x2, 0.0), axis=1, keepdims=True)
            s1 = jnp.sum(jnp.where(low, 0.0, x2), axis=1, keepdims=True)
            dl_ref[:, 2 * HD * j:2 * HD * (j + 1)] = jnp.where(low, s0, s1)
        zp = zp_ref[...]
        sgp = _sigmoid(zp)
        mpre = mp_ref[...]
        ps = ps_ref[...]
        dmixed = dp * (zp * sgp)
        dr_ref[:, 512:1024] = (dp * (mpre * ps) * (sgp * (1.0 + zp * (1.0 - sgp)))).astype(dr_ref.dtype)
        dps_ref[...] += jnp.sum(dmixed * mpre, axis=0, keepdims=True)
        dmp_ref[...] = (dmixed * ps).astype(dmp_ref.dtype)

    full = lambda shape: pl.BlockSpec(shape, lambda i: (0, 0))
    tD = pl.BlockSpec((tm, D), lambda i: (i, 0))
    tA = pl.BlockSpec((tm, AW), lambda i: (i, 0))
    rb = lambda c: pl.BlockSpec((tm, AW), lambda i: (i, c))
    return pl.pallas_call(
        body, name="bwd_gates", grid=(S // tm,),
        in_specs=[tD, full((D, D)), full((AW, D)), full((AW, D)), rb(0), rb(1), rb(2), rb(3), rb(4), rb(5),
                  tA, tA, tA, tA, full((1, AW))],
        out_specs=[full((AW, D)), full((AW, D)), pl.BlockSpec((tm, 3072), lambda i: (i, 0)), tA, tA, tA, full((1, AW))],
        out_shape=[_sds((AW, D), F32), _sds((AW, D), F32), _sds((S, REST_W), CDT), _sds((S, AW), CDT),
                   _sds((S, AW), F32), _sds((S, AW), CDT), _sds((1, AW), F32)],
        compiler_params=_cp(("arbitrary",), vmem=56),
    )(dmo, wout, wab, wpb, rest, rest, rest, rest, rest, rest, ag, pg, attn, mp, pool_scale)


def _pool_bwd(dmp, pt, pool_w, drest):
    S = dmp.shape[0]
    LS = S // R16
    PG = 128
    d3 = dmp.reshape(R16, LS, AW)
    dr3 = drest.reshape(R16, LS, REST_W)

    def body(d_ref, pt_ref, pw_ref, dr_in, du_ref, dpw_ref, g_ref, s_ref):
        del dr_in
        g = pl.program_id(0)
        win = jnp.left_shift(2, g)
        row = lax.broadcasted_iota(jnp.int32, (LS, PG), 0)
        pw = pw_ref[...].astype(CDT)
        dpw = jnp.zeros((PG, PG), F32)
        for r in range(R16):
            dm = d_ref[r]
            dpw = dpw + _dot(pt_ref[:, r * LS:(r + 1) * LS], dm, NN)
            dpooled = _dot(dm, pw, NT)
            g_ref[r] = dpooled
            sc = dpooled / _pool_counts(r, LS, win)
            s_ref[r] = sc
            if r < R16 - 1:
                s_ref[R16 + r] = jnp.where(row == LS - 1, 0.0, pltpu.roll(sc, LS - 1, 0))
        dpw_ref[...] = dpw

        def double(s):
            for i in range(0, 2 * R16 - 2 * s):
                s_ref[i] = s_ref[i] + s_ref[i + s]

        double(1)
        for k in (1, 2, 3):
            pl.when(g >= k)(functools.partial(double, 2 ** k))
        for r in range(R16):
            du_ref[r] = (s_ref[r] - g_ref[r]).astype(du_ref.dtype)

    du, dpw = pl.pallas_call(
        body, name="pool_bwd", grid=(4,),
        in_specs=[pl.BlockSpec((R16, LS, PG), lambda g: (0, 0, g)), pl.BlockSpec((PG, S), lambda g: (g, 0)),
                  pl.BlockSpec((None, PG, PG), lambda g: (g, 0, 0)), pl.BlockSpec(memory_space=pl.ANY)],
        out_specs=[pl.BlockSpec((R16, LS, PG), lambda g: (0, 0, 24 + g)),
                   pl.BlockSpec((None, PG, PG), lambda g: (g, 0, 0))],
        out_shape=[_sds((R16, LS, REST_W), CDT), _sds((4, PG, PG), F32)],
        scratch_shapes=[pltpu.VMEM((R16, LS, PG), F32), pltpu.VMEM((2 * R16, LS, PG), F32)],
        input_output_aliases={3: 0},
        compiler_params=_cp(("parallel",)),
    )(d3, pt, pool_w, dr3)
    return du.reshape(S, REST_W), dpw


def _attn_bwd(qkv, dattn, Lb, dlb, biases, geom, dqkv_prev):
    S = qkv.shape[0]
    gi, T, R, has_prev, N = geom.gi, geom.T, geom.rows, geom.has_prev, geom.n_tiles
    H2 = T // 2
    nd_outer = len(geom.outer)
    qv = geom.view(qkv)
    cq, ck, cv = 3 * gi, 3 * gi + 1, 3 * gi + 2
    W3 = 3 * AW

    def heads(q_ref, kp_ref, kc_ref, vp_ref, vc_ref, do_ref, L_ref, dl_ref, ba_ref, bb_ref, pm, dsa_ref, dsb_ref, emit):
        q_lo, q_hi = _split_rows(_ld(q_ref), R)
        do_lo, do_hi = _split_rows(_ld(do_ref), R)
        L_lo, L_hi = _split_rows(_ld(L_ref), R)
        dl_lo, dl_hi = _split_rows(_ld(dl_ref), R)
        kc_lo, kc_hi = _split_rows(_ld(kc_ref), R)
        vc_lo, vc_hi = _split_rows(_ld(vc_ref), R)
        k_b = jnp.concatenate([kc_lo, kc_hi], axis=0)
        v_b = jnp.concatenate([vc_lo, vc_hi], axis=0)
        if has_prev:
            k_a = jnp.concatenate([_split_rows(_ld(kp_ref), R)[1], kc_lo], axis=0)
            v_a = jnp.concatenate([_split_rows(_ld(vp_ref), R)[1], vc_lo], axis=0)
        else:
            k_a, v_a = k_b, v_b
        low = lax.broadcasted_iota(jnp.int32, (1, 2 * HD), 1) < HD
        for j in range(NH // 2):
            sl = slice(2 * HD * j, 2 * HD * (j + 1))
            res = []
            for q, k, v, do, L, dl, b_ref, mask, ds_ref in (
                    (q_lo, k_a, v_a, do_lo, L_lo, dl_lo, ba_ref, pm, dsa_ref),
                    (q_hi, k_b, v_b, do_hi, L_hi, dl_hi, bb_ref, None, dsb_ref)):
                q2 = q[:, sl].astype(F32) * 0.125
                do2 = do[:, sl].astype(F32)
                k2, v2 = k[:, sl], v[:, sl]
                L2, dl2 = L[:, sl], dl[:, sl]
                qs = jnp.concatenate([jnp.where(low, q2, 0.0), jnp.where(low, 0.0, q2)], axis=0).astype(q.dtype)
                dom = jnp.concatenate([jnp.where(low, do2, 0.0), jnp.where(low, 0.0, do2)], axis=0).astype(q.dtype)
                Lc = jnp.concatenate([L2[:, 0:1], L2[:, HD:HD + 1]], axis=0)
                dlc = jnp.concatenate([dl2[:, 0:1], dl2[:, HD:HD + 1]], axis=0)
                s = _dot(qs, k2, NT) + b_ref[j]
                if mask is not None:
                    s = s + mask
                p = jnp.exp(s - Lc)
                ds = p * (_dot(dom, v2, NT) - dlc)
                ds_ref[j] += ds
                dsc = ds.astype(q.dtype)
                dq_st = _dot(dsc, k2, NN)
                res.append((jnp.where(low, dq_st[:H2], dq_st[H2:]) * 0.125, _dot(dsc, qs, TN),
                            _dot(p.astype(q.dtype), dom, TN)))
            (dq_lo, dk_a, dv_a), (dq_hi, dk_b, dv_b) = res
            dq = _merge_rows(dq_lo, dq_hi, R)
            if has_prev:
                emit(sl, dq, _merge_rows(dk_a[H2:] + dk_b[:H2], dk_b[H2:], R),
                     _merge_rows(dv_a[H2:] + dv_b[:H2], dv_b[H2:], R), dk_a[:H2], dv_a[:H2])
            else:
                emit(sl, dq, _merge_rows(dk_a[:H2] + dk_b[:H2], dk_a[H2:] + dk_b[H2:], R),
                     _merge_rows(dv_a[:H2] + dv_b[:H2], dv_a[H2:] + dv_b[H2:], R), None, None)

    def first_step():
        ok = pl.program_id(nd_outer) == 0 if has_prev else pl.program_id(0) == 0
        for a in range(nd_outer if has_prev else 0):
            ok = ok & (pl.program_id(a) == 0)
        return ok

    if has_prev:
        def body(q_ref, kp_ref, kc_ref, vp_ref, vc_ref, do_ref, L_ref, dl_ref, ba_ref, bb_ref, prev_in, out_ref,
                 dsa_ref, dsb_ref, hold, new, pp):
            del prev_in
            s_id = pl.program_id(nd_outer)

            @pl.when(first_step())
            def _():
                dsa_ref[...] = jnp.zeros_like(dsa_ref)
                dsb_ref[...] = jnp.zeros_like(dsb_ref)

            @pl.when(s_id < N)
            def _():
                def emit(sl, dq, dk, dv, dk_prev, dv_prev):
                    new[:, sl] = dq
                    new[:, slice(AW + sl.start, AW + sl.stop)] = dk
                    new[:, slice(2 * AW + sl.start, 2 * AW + sl.stop)] = dv
                    pp[:, sl] = dk_prev
                    pp[:, slice(AW + sl.start, AW + sl.stop)] = dv_prev

                heads(q_ref, kp_ref, kc_ref, vp_ref, vc_ref, do_ref, L_ref, dl_ref, ba_ref, bb_ref,
                      _prev_mask(s_id, H2, T), dsa_ref, dsb_ref, emit)

            @pl.when((s_id >= 1) & (s_id < N))
            def _():
                _st(out_ref, hold[:, 0:AW], slice(0, AW))
                late = _merge_rows(jnp.zeros((H2, 2 * AW), F32), pp[...], R)
                _st(out_ref, hold[:, AW:W3] + late, slice(AW, W3))

            @pl.when(s_id == N)
            def _():
                _st(out_ref, hold[...])

            @pl.when(s_id < N)
            def _():
                hold[...] = new[...]

        cur = lambda n: jnp.minimum(n, N - 1)
        prev = lambda n: jnp.clip(n - 1, 0, N - 1)
        delayed = lambda n: jnp.maximum(n - 1, 0)
        in_specs = [geom.spec(AW, cq, cur), geom.spec(AW, ck, prev), geom.spec(AW, ck, cur),
                    geom.spec(AW, cv, prev), geom.spec(AW, cv, cur),
                    geom.spec(AW, 0, cur), geom.spec(AW, 0, cur), geom.spec(AW, 0, cur)]
        args = [qv] * 5 + [geom.view(dattn), geom.view(Lb), geom.view(dlb)]
        out_spec = geom.spec(W3, gi, delayed)
        grid = geom.outer + (N + 1,)
        scratch = [pltpu.VMEM((T, W3), F32), pltpu.VMEM((T, W3), F32), pltpu.VMEM((H2, 2 * AW), F32)]
    else:
        def body(q_ref, kc_ref, vc_ref, do_ref, L_ref, dl_ref, ba_ref, bb_ref, prev_in, out_ref, dsa_ref, dsb_ref):
            del prev_in

            @pl.when(first_step())
            def _():
                dsa_ref[...] = jnp.zeros_like(dsa_ref)
                dsb_ref[...] = jnp.zeros_like(dsb_ref)

            def emit(sl, dq, dk, dv, dk_prev, dv_prev):
                _st(out_ref, dq, sl)
                _st(out_ref, dk, slice(AW + sl.start, AW + sl.stop))
                _st(out_ref, dv, slice(2 * AW + sl.start, 2 * AW + sl.stop))

            heads(q_ref, None, kc_ref, None, vc_ref, do_ref, L_ref, dl_ref, ba_ref, bb_ref, None, dsa_ref, dsb_ref,
                  emit)

        cur = lambda n: n
        in_specs = [geom.spec(AW, cq, cur), geom.spec(AW, ck, cur), geom.spec(AW, cv, cur),
                    geom.spec(AW, 0, cur), geom.spec(AW, 0, cur), geom.spec(AW, 0, cur)]
        args = [qv] * 3 + [geom.view(dattn), geom.view(Lb), geom.view(dlb)]
        out_spec = geom.spec(W3, gi, cur)
        grid = geom.outer
        scratch = []

    whole = lambda b: pl.BlockSpec(b.shape, lambda *ids: (0, 0, 0))
    biases = [b.reshape(NH // 2, 2 * H2, b.shape[2]) for b in biases]
    in_specs += [whole(b) for b in biases]
    in_specs.append(pl.BlockSpec(memory_space=pl.ANY))
    if dqkv_prev is None:
        last, aliases = biases[0], {}
    else:
        last, aliases = geom.view(dqkv_prev), {len(in_specs) - 1: 0}
    dq, dsa, dsb = pl.pallas_call(
        body, name=f"attn_bwd{gi}", grid=grid, in_specs=in_specs,
        out_specs=[out_spec] + [whole(b) for b in biases],
        out_shape=[_sds(geom.view_shape(QKV_W), CDT)] + [_sds(b.shape, F32) for b in biases],
        scratch_shapes=scratch,
        input_output_aliases=aliases,
        compiler_params=_cp(("arbitrary",) * len(grid), vmem=56),
    )(*args, *biases, last)
    return geom.unview(dq), [d.reshape(NH, H2, d.shape[2]) for d in (dsa, dsb)]


def _h_bwd(dh, dx2, x, norm_g, scale):
    S = x.shape[0]

    def body(dh_ref, dx2_ref, x_ref, g_ref, sc_ref, gx_ref, dsh_ref, dsc_ref, dng_ref, nat_ref):
        i = pl.program_id(0)

        @pl.when(i == 0)
        def _():
            dsh_ref[...] = jnp.zeros_like(dsh_ref)
            dsc_ref[...] = jnp.zeros_like(dsc_ref)
            dng_ref[...] = jnp.zeros_like(dng_ref)

        for r in range(R16):
            nat_ref[:, r, :] = dh_ref[r]
        dh = jnp.concatenate([nat_ref[l] for l in range(R16)], axis=0)
        g = g_ref[...]
        one_sc = 1.0 + sc_ref[...]
        xv = x_ref[...]
        r1 = lax.rsqrt(jnp.mean(xv * xv, axis=-1, keepdims=True) + EPS)
        xn = xv * r1
        dhx = dh * xn
        dxn = dh * (g * one_sc)
        dx1 = r1 * (dxn - xn * jnp.mean(dxn * xn, axis=-1, keepdims=True))
        gx_ref[...] = dx2_ref[...] + dx1
        dsh_ref[...] += jnp.sum(dh, axis=0, keepdims=True)
        dsc_ref[...] += jnp.sum(dhx * g, axis=0, keepdims=True)
        dng_ref[...] += jnp.sum(dhx * one_sc, axis=0, keepdims=True)

    nat = pl.BlockSpec((TOK, D), lambda i: (i, 0))
    vec = pl.BlockSpec((1, D), lambda i: (0, 0))
    return pl.pallas_call(
        body, name="h_bwd", grid=(S // TOK,),
        in_specs=[pl.BlockSpec((R16, R16, D), lambda i: (0, i, 0)), nat, nat, vec, vec],
        out_specs=[nat, vec, vec, vec],
        out_shape=[_sds((S, D), F32), _sds((1, D), F32), _sds((1, D), F32), _sds((1, D), F32)],
        scratch_shapes=[pltpu.VMEM((R16, R16, D), F32)],
        compiler_params=_cp(("arbitrary",)),
    )(dh, dx2, x, norm_g, scale)


def _local_step(x, target, shift, scale, gate, norm_g, w_in_full, pool_w, pool_scale, wab, wpb, wout, rel_bias,
                final_g):
    st = _local_pre(x, shift, scale, norm_g, rel_bias)
    st = _local_grads(st, target, gate, w_in_full, pool_w, pool_scale, lambda after: (wab, wpb, wout), final_g)
    d_rel_bias = _local_bias_grads(st, _BIAS_GRADS_EARLY + _BIAS_GRADS_LATE)
    gx, dmod, dng = _local_input_grads(st, w_in_full, norm_g, scale)
    return dict(loss=st["loss"], grad_x=gx, dw_in=st["dw_in"], dw_ab=st["dw_ab"], dw_pb=st["dw_pb"],
                dw_out=st["dw_out"], d_norm_g=dng, d_pool_w=st["dpw"], d_pool_scale=st["dps"],
                d_rel_bias=d_rel_bias, d_final_g=st["gfg"], dmod=dmod)


def _local_pre(x, shift, scale, norm_g, rel_bias, dep=None):
    S = x.shape[0]
    LS = S // R16
    geoms = [_Geom(gi, LS) for gi in range(3)]
    idx_np = [[g.bucket_index(sub) for sub in range(2)] for g in geoms]
    buckets = [[tuple(int(b) for b in np.unique(ix) if b >= 0) for ix in pair] for pair in idx_np]
    idxs = [[jnp.asarray(ix) for ix in pair] for pair in idx_np]
    biases = [[_bias_table(idxs[gi][sub], buckets[gi][sub], rel_bias, gi, sub, dep) for sub in range(2)]
              for gi in range(3)]
    h = _norm_mod(x, norm_g, shift, scale, dep).reshape(S, D)
    return dict(geoms=geoms, idxs=idxs, buckets=buckets, biases=biases, x=x, h=h, ht=_tcast(h, "h_transpose"))


def _local_grads(st, target, gate, w_in_full, pool_w, pool_scale, branch_weights, final_g):
    geoms, idxs, biases, x, h, ht = (st[k] for k in ("geoms", "idxs", "biases", "x", "h", "ht"))
    S = h.shape[0]
    LS = S // R16
    qkv, rest = _proj(h, w_in_full)

    os_, ls_ = [], []
    for gi in range(2):
        o, l = _attn_fwd(qkv, biases[gi], geoms[gi])
        os_.append(o)
        ls_.append(l)
    attn, Lb, ag = _attn_fwd(qkv, biases[2], geoms[2], others=(os_, ls_, rest))
    pt, mp, pg = _pool_fwd(rest, pool_w, pool_scale)
    wab, wpb, wout = branch_weights([ag, pg])
    dx2, dmo3, loss, gfg, dgate, dw_out = _merge_out_loss(ag, pg, rest, wab, wpb, wout, x, target, gate, final_g)
    dmo = dmo3.reshape(S, D)

    dw_ab, dw_pb, drest, dattn, dlb, dmp, dps = _bwd_gates(dmo, wout, wab, wpb, rest, ag, pg, attn, mp, pool_scale)
    drest, dpw = _pool_bwd(dmp, pt, pool_w, drest)

    dqkv = None
    dsums = []
    for gi in range(3):
        dqkv, dsum = _attn_bwd(qkv, dattn, Lb, dlb, biases[gi], geoms[gi], dqkv)
        dsums.append(dsum)

    dw_in = _mm_tn(ht, dqkv, n_total=8192, out_blk=lambda j: j, name="dw_in_qkv", with_low=True)
    dw_in, dw_in_low = _mm_tn(ht, drest, n_total=8192, out_blk=_rest_blk, name="dw_in_rest", prev=dw_in, with_low=True)
    return dict(st, loss=loss[0, 0], dw_in=dw_in, dw_in_low=dw_in_low, dw_ab=dw_ab, dw_pb=dw_pb, dw_out=dw_out, dpw=dpw, dps=dps, gfg=gfg,
                dgate=dgate, dsums=dsums, dqkv=dqkv, drest=drest, dx2=dx2)


def _local_bias_grads(st, which, dep=None):
    cols = []
    for gi in range(3):
        got = [_bias_grad(st["dsums"][gi][sub], st["idxs"][gi][sub], st["buckets"][gi][sub], gi, sub, dep)
               for sub in range(2) if (gi, sub) in which]
        cols.append(sum(got) if got else jnp.zeros((32, NH), F32))
    return jnp.concatenate(cols, axis=1)


_BIAS_GRADS_EARLY = ((0, 0), (0, 1), (1, 0), (2, 0))
_BIAS_GRADS_LATE = ((1, 1), (2, 1))


def _dh_h_bwd(dqkv, drest, w, dx2, x, norm_g, scale, dep=None):
    S = x.shape[0]
    dep_specs, dep_args = _after(dep)

    def body(a_ref, b_ref, w_ref, dx2_ref, x_ref, g_ref, sc_ref, *rest):
        gx_ref, dsh_ref, dsc_ref, dng_ref, nat_ref = rest[-5:]
        i = pl.program_id(0)

        @pl.when(i == 0)
        def _():
            dsh_ref[...] = jnp.zeros_like(dsh_ref)
            dsc_ref[...] = jnp.zeros_like(dsc_ref)
            dng_ref[...] = jnp.zeros_like(dng_ref)

        a, b = _ld(a_ref), _ld(b_ref)
        acc = _dot(a[:, 0:512], _w512(w_ref, 0), NT)
        for j in range(1, 9):
            acc = acc + _dot(a[:, 512 * j:512 * (j + 1)], _w512(w_ref, j), NT)
        off = 0
        for lo, hi in REST_COLS:
            for j in range(lo, hi):
                acc = acc + _dot(b[:, off:off + 512], _w512(w_ref, j), NT)
                off += 512
        for r in range(R16):
            nat_ref[:, r, :] = acc[R16 * r:R16 * (r + 1)]
        dh = jnp.concatenate([nat_ref[l] for l in range(R16)], axis=0)
        g = g_ref[...]
        one_sc = 1.0 + sc_ref[...]
        xv = x_ref[...]
        r1 = lax.rsqrt(jnp.mean(xv * xv, axis=-1, keepdims=True) + EPS)
        xn = xv * r1
        dhx = dh * xn
        dxn = dh * (g * one_sc)
        dx1 = r1 * (dxn - xn * jnp.mean(dxn * xn, axis=-1, keepdims=True))
        gx_ref[...] = dx2_ref[...] + dx1
        dsh_ref[...] += jnp.sum(dh, axis=0, keepdims=True)
        dsc_ref[...] += jnp.sum(dhx * g, axis=0, keepdims=True)
        dng_ref[...] += jnp.sum(dhx * one_sc, axis=0, keepdims=True)

    nat = pl.BlockSpec((TOK, D), lambda i: (i, 0))
    vec = pl.BlockSpec((1, D), lambda i: (0, 0))
    slab = lambda W: pl.BlockSpec((R16, R16, W), lambda i: (0, i, 0))
    return pl.pallas_call(
        body, name="dh_h_bwd", grid=(S // TOK,),
        in_specs=[slab(QKV_W), slab(REST_W), _resident((8, D, 1024)), nat, nat, vec, vec] + dep_specs,
        out_specs=[nat, vec, vec, vec],
        out_shape=[_sds((S, D), F32), _sds((1, D), F32), _sds((1, D), F32), _sds((1, D), F32)],
        scratch_shapes=[pltpu.VMEM((R16, R16, D), F32)],
        compiler_params=_cp(("arbitrary",), vmem=56),
    )(dqkv, drest, w, dx2, x, norm_g, scale, *dep_args)


def _local_input_grads(st, w_in_full, norm_g, scale, dep=None):
    S = st["h"].shape[0]
    LS = S // R16
    gx, dsh, dsc, dng = _dh_h_bwd(st["dqkv"].reshape(R16, LS, QKV_W), st["drest"].reshape(R16, LS, REST_W),
                                  w_in_full, st["dx2"], st["x"], norm_g, scale, dep)
    return gx, jnp.concatenate([dsh, dsc, st["dgate"]], axis=1), dng


def _my_place():
    return lax.axis_index("x"), lax.axis_index("y"), lax.axis_index("c")


def _block_of(ref, axis, idx, width):
    if axis is None:
        return ref.at[idx]
    start = pl.multiple_of(idx * width, width)
    if axis == 0:
        return ref.at[pl.ds(start, width)]
    return ref.at[:, pl.ds(start, width)]


_SEM = pl.BlockSpec(memory_space=pltpu.SEMAPHORE)
_EFFECT = pltpu.SideEffectType.DATAFLOW_SIDE_EFFECTING


def _split_start(name, srcs, lands, ncopy, plan, dep=None):
    arrs = list(srcs) + list(lands)
    na, ns = len(arrs), len(srcs)
    dep_specs, dep_args = _after(dep)

    def body(*refs):
        send_sems, recv_sems, token = refs[na + len(dep_args)], refs[na + len(dep_args) + 1], refs[-1]
        for k, (s, d, dev) in enumerate(plan(refs[:ns], refs[ns:na])):
            pltpu.make_async_remote_copy(src_ref=s, dst_ref=d, send_sem=send_sems.at[k], recv_sem=recv_sems.at[k],
                                         device_id=dev, device_id_type=MESH).start()
        token[...] = jnp.zeros_like(token)

    outs = pl.pallas_call(
        body, name=name,
        out_shape=(pltpu.SemaphoreType.DMA((ncopy,)), pltpu.SemaphoreType.DMA((ncopy,)),
                   *[pltpu.HBM(a.shape, a.dtype) for a in arrs], _sds((8, 128), F32)),
        in_specs=[_HBM] * na + dep_specs,
        out_specs=(_SEM, _SEM, *[_HBM] * na, pl.BlockSpec(memory_space=pltpu.VMEM)),
        input_output_aliases={i: 2 + i for i in range(na)},
        compiler_params=pltpu.CompilerParams(has_side_effects=_EFFECT),
    )(*[pltpu.with_memory_space_constraint(a, pltpu.HBM) for a in arrs], *dep_args)
    return (outs[0], outs[1]), list(outs[2:2 + ns]), list(outs[2 + ns:2 + na]), outs[-1]


def _split_wait(name, sems, srcs, lands, plan, after):
    arrs = list(srcs) + list(lands)
    na, ns = len(arrs), len(srcs)
    after = list(after) if isinstance(after, (list, tuple)) else [after]

    def body(*refs):
        send_sems, recv_sems = refs[na], refs[na + 1]
        for k, (s, d, dev) in enumerate(plan(refs[:ns], refs[ns:na])):
            cp = pltpu.make_async_remote_copy(src_ref=s, dst_ref=d, send_sem=send_sems.at[k], recv_sem=recv_sems.at[k],
                                              device_id=dev, device_id_type=MESH)
            cp.wait_send()
            cp.wait_recv()

    outs = pl.pallas_call(
        body, name=name, out_shape=[pltpu.HBM(a.shape, a.dtype) for a in arrs],
        in_specs=[_HBM] * na + [_SEM, _SEM] + [pl.BlockSpec(memory_space=pl.ANY)] * len(after),
        out_specs=[_HBM] * na,
        input_output_aliases={i: i for i in range(na)},
        compiler_params=pltpu.CompilerParams(has_side_effects=_EFFECT),
    )(*arrs, sems[0], sems[1], *after)
    return list(outs[:ns]), list(outs[ns:])


def _gather_chips_plan(layout):
    def plan(src, land):
        x, y, c = _my_place()
        me = 4 * x + 2 * y + c
        mine = [_block_of(land[p], layout[p][0], me, layout[p][1]) for p in range(len(land))]
        return [(mine[p], mine[p], (cx, cy, c)) for p in range(len(land)) for cx, cy in [(1 - x, y), (x, 1 - y)]]
    return plan


def _to_all_plan(src, land):
    x, y, c = _my_place()
    me = 4 * x + 2 * y + c
    flip = lambda v, f: 1 - v if f else v
    peers = [(flip(x, fx), flip(y, fy), flip(c, fc)) for fx in (0, 1) for fy in (0, 1) for fc in (0, 1)][1:]
    return [(land[p].at[me], land[p].at[me], peer) for p in range(len(land)) for peer in peers]


def _pair_plan(layout):
    def plan(src, land):
        x, y, c = _my_place()
        n = len(layout)
        return [(_block_of(src[p], layout[p][0], 2 * q + (1 - c), layout[p][1]), land[p].at[q], (x, y, 1 - c))
                for p in range(n) for q in range(4)] + [(src[p], land[p], (x, y, 1 - c)) for p in range(n, len(src))]
    return plan


def _chips_plan(n):
    def plan(src, land):
        x, y, c = _my_place()
        chips = [(1 - x, y), (x, 1 - y), (1 - x, 1 - y)]
        return [(src[p].at[2 * cx + cy], land[p].at[2 * x + y], (cx, cy, c)) for p in range(n) for cx, cy in chips] + \
               [(src[p], land[p].at[2 * x + y], (cx, cy, c)) for p in range(n, len(src)) for cx, cy in chips]
    return plan


def _pair_share(gathered):
    n = len(gathered)

    def body(*refs):
        land = refs[n:2 * n]
        send_sems, recv_sems, fwd_send, fwd_recv = refs[2 * n:]
        x, y, c = _my_place()
        k_dg = 3 - (2 * x + y)
        south = c == 0
        pass_on = 4 * jnp.where(south, 1 - x, x) + 2 * jnp.where(south, y, 1 - y) + c
        to = (jnp.where(south, x, 1 - x), jnp.where(south, 1 - y, y), c)

        def forward(p):
            return pltpu.make_async_remote_copy(
                src_ref=land[p].at[pass_on], dst_ref=land[p].at[pass_on], send_sem=fwd_send.at[p],
                recv_sem=fwd_recv.at[p], device_id=to, device_id_type=MESH)

        def to_sibling(p, k):
            blk = land[p].at[2 * k + c]
            return pltpu.make_async_remote_copy(
                src_ref=blk, dst_ref=blk, send_sem=send_sems.at[4 * p + k], recv_sem=recv_sems.at[4 * p + k],
                device_id=(x, y, 1 - c), device_id_type=MESH)

        for p in range(n):
            forward(p).start()
        for p in range(n):
            for k in range(4):
                @pl.when(k != k_dg)
                def _():
                    to_sibling(p, k).start()
        for p in range(n):
            dg = land[p].at[2 * k_dg + c]
            pltpu.make_async_remote_copy(src_ref=dg, dst_ref=dg, send_sem=fwd_send.at[p], recv_sem=fwd_recv.at[p],
                                         device_id=to, device_id_type=MESH).wait_recv()
            for k in range(4):
                @pl.when(k == k_dg)
                def _():
                    to_sibling(p, k).start()
        for p in range(n):
            forward(p).wait_send()
            for k in range(4):
                to_sibling(p, k).wait_send()
                blk = land[p].at[2 * k + 1 - c]
                pltpu.make_async_remote_copy(
                    src_ref=blk, dst_ref=blk, send_sem=send_sems.at[4 * p + k], recv_sem=recv_sems.at[4 * p + k],
                    device_id=(x, y, 1 - c), device_id_type=MESH).wait_recv()

    return pl.pallas_call(
        body, name="gather_pair_share", in_specs=[_HBM] * n, out_specs=[_HBM] * n,
        out_shape=[_sds(g.shape, g.dtype) for g in gathered],
        input_output_aliases={p: p for p in range(n)},
        scratch_shapes=[pltpu.SemaphoreType.DMA((4 * n,)), pltpu.SemaphoreType.DMA((4 * n,)),
                        pltpu.SemaphoreType.DMA((n,)), pltpu.SemaphoreType.DMA((n,))],
    )(*gathered)


def _place_slot(arrs, slot, n_slots, name, out_dtype=None):
    n = len(arrs)

    def body(s_ref, *refs):
        for p in range(n):
            refs[n + p][...] = refs[p][...].astype(refs[n + p].dtype)

    def spec(a):
        tail = a.shape[-2:]
        return pl.BlockSpec((None,) + tail, lambda i, s_ref: (s_ref[0], 0, 0))

    in_specs = [spec(a) if a.ndim == 3 else pl.BlockSpec(a.shape, lambda i, s_ref: (0, 0)) for a in arrs]
    return pl.pallas_call(
        body, name=name,
        grid_spec=pltpu.PrefetchScalarGridSpec(num_scalar_prefetch=1, grid=(1,), in_specs=in_specs,
                                               out_specs=[spec(a) for a in arrs]),
        out_shape=[_sds((n_slots,) + a.shape[-2:], out_dtype or a.dtype) for a in arrs],
        compiler_params=_cp(("arbitrary",)),
    )(slot, *arrs)


def _cols_from_blocks(w8, name):
    _, R, C = w8.shape

    def body(w_ref, o_ref):
        for e in range(8):
            o_ref[:, C * e:C * (e + 1)] = w_ref[e]

    return pl.pallas_call(body, name=name, out_shape=_sds((R, 8 * C), w8.dtype), compiler_params=_cp())(w8)


def _all_gather_small(srcs, name):
    n = len(srcs)
    vmem = pl.BlockSpec(memory_space=pltpu.VMEM)

    def body(*refs):
        src, out = refs[:n], refs[n:2 * n]
        send_sems, recv_sems = refs[2 * n:]
        x, y, c = _my_place()
        me = 4 * x + 2 * y + c
        flip = lambda v, f: 1 - v if f else v
        peers = [(flip(x, fx), flip(y, fy), flip(c, fc)) for fx in (0, 1) for fy in (0, 1) for fc in (0, 1)][1:]
        cps = []
        for p in range(n):
            out[p][me] = src[p][...]
            for k, peer in enumerate(peers):
                cps.append(pltpu.make_async_remote_copy(
                    src_ref=src[p], dst_ref=out[p].at[me], send_sem=send_sems.at[7 * p + k],
                    recv_sem=recv_sems.at[7 * p + k], device_id=peer, device_id_type=MESH))
        for cp in cps:
            cp.start()
        for cp in cps:
            cp.wait()

    return pl.pallas_call(
        body, name=name, in_specs=[vmem] * n, out_specs=[vmem] * n,
        out_shape=[_sds((8,) + a.shape, a.dtype) for a in srcs],
        scratch_shapes=[pltpu.SemaphoreType.DMA((7 * n,)), pltpu.SemaphoreType.DMA((7 * n,))],
    )(*srcs)


def _adaln_exchange(c, w_ada, b_part):
    vmem = pl.BlockSpec(memory_space=pltpu.VMEM)

    def body(c_ref, w_ref, b_ref, call_ref, mod_ref, part_ref, send_sems, recv_sems):
        x, y, c = _my_place()
        me = 4 * x + 2 * y + c
        flip = lambda v, f: 1 - v if f else v
        peers = [(flip(x, fx), flip(y, fy), flip(c, fc)) for fx in (0, 1) for fy in (0, 1) for fc in (0, 1)][1:]
        call_ref[me] = c_ref[...]
        first = [pltpu.make_async_remote_copy(src_ref=c_ref, dst_ref=call_ref.at[me], send_sem=send_sems.at[k],
                                              recv_sem=recv_sems.at[k], device_id=peer, device_id_type=MESH)
                 for k, peer in enumerate(peers)]
        for cp in first:
            cp.start()
        for cp in first:
            cp.wait()
        c_all = jnp.concatenate([call_ref[j] for j in range(8)], axis=0)
        part_ref[...] = _dot(c_all.astype(CDT), w_ref[...].astype(CDT), NN) + b_ref[...]
        mod_ref[me] = part_ref[pl.ds(me, 1), :]
        second = [pltpu.make_async_remote_copy(
            src_ref=part_ref.at[pl.ds(4 * px + 2 * py + pc, 1)], dst_ref=mod_ref.at[me], send_sem=send_sems.at[7 + k],
            recv_sem=recv_sems.at[7 + k], device_id=(px, py, pc), device_id_type=MESH)
            for k, (px, py, pc) in enumerate(peers)]
        for cp in second:
            cp.start()
        for cp in second:
            cp.wait()

    return pl.pallas_call(
        body, name="adaln_exchange", in_specs=[vmem] * 3, out_specs=[vmem] * 2,
        out_shape=[_sds((8, 1, D), F32), _sds((8, 1, w_ada.shape[1]), F32)],
        scratch_shapes=[pltpu.VMEM((8, w_ada.shape[1]), F32), pltpu.SemaphoreType.DMA((14,)),
                        pltpu.SemaphoreType.DMA((14,))],
        compiler_params=_cp(),
    )(c, w_ada, b_part)


def _pair_add(dw, recv, axis, width, cidx, name):
    if axis == 0:
        tr = min(width, 256)
        C = dw.shape[1]
        grid = (4, width // tr)
        nb = width // tr
        own = pl.BlockSpec((tr, C), lambda q, i, c_ref: ((2 * q + c_ref[0]) * nb + i, 0))
        oth = pl.BlockSpec((None, tr, C), lambda q, i, c_ref: (q, i, 0))
    else:
        R = dw.shape[0]
        tr = min(R, 1024)
        grid = (4, R // tr)
        own = pl.BlockSpec((tr, width), lambda q, i, c_ref: (i, 2 * q + c_ref[0]))
        oth = pl.BlockSpec((None, tr, width), lambda q, i, c_ref: (q, i, 0))

    def body(c_ref, a_ref, b_ref, o_ref):
        o_ref[...] = (a_ref[...] + b_ref[...].astype(F32)).astype(o_ref.dtype)

    return pl.pallas_call(
        body, name=name,
        grid_spec=pltpu.PrefetchScalarGridSpec(num_scalar_prefetch=1, grid=grid, in_specs=[own, oth], out_specs=oth),
        out_shape=_sds(recv.shape, CDT), compiler_params=_cp(("parallel", "parallel")),
    )(cidx, dw, recv)


def _add_pairs(xs, ys, name):
    n = len(xs)

    def body(*refs):
        for p in range(n):
            refs[2 * n + p][...] = refs[p][...] + refs[n + p][...]

    return pl.pallas_call(body, name=name, out_shape=[_sds(a.shape, F32) for a in xs])(*xs, *ys)


def _adam_math(w, g, m, v):
    m = ADAM_B1 * m + (1.0 - ADAM_B1) * g
    v = ADAM_B2 * v + (1.0 - ADAM_B2) * (g * g)
    m_hat = m / (1.0 - ADAM_B1 ** ADAM_STEP)
    v_hat = v / (1.0 - ADAM_B2 ** ADAM_STEP)
    delta = -ADAM_LR * (m_hat / (jnp.sqrt(v_hat) + ADAM_EPS) + ADAM_WD * w)
    return delta, m, v


def _adam(gparts, w, m, v, name):
    P, R, C = gparts.shape
    tr = R if R <= 256 else 256

    def body(g_ref, w_ref, m_ref, v_ref, go_ref, d_ref, mo_ref, vo_ref):
        g = g_ref[0].astype(F32)
        for i in range(1, P):
            g = g + g_ref[i].astype(F32)
        d, mn, vn = _adam_math(w_ref[...], g, m_ref[...], v_ref[...])
        go_ref[...] = g
        d_ref[...] = d
        mo_ref[...] = mn
        vo_ref[...] = vn

    t = pl.BlockSpec((tr, C), lambda i: (i, 0))
    return pl.pallas_call(
        body, name=name, grid=(R // tr,),
        in_specs=[pl.BlockSpec((P, tr, C), lambda i: (0, i, 0)), t, t, t], out_specs=[t, t, t, t],
        out_shape=[_sds((R, C), F32)] * 4, compiler_params=_cp(("parallel",)),
    )(gparts, w, m, v)


def _sum_parts(gparts, name):
    P, R, C = gparts.shape

    def body(g_ref, o_ref):
        g = g_ref[0]
        for i in range(1, P):
            g = g + g_ref[i]
        o_ref[...] = g

    return pl.pallas_call(body, name=name, out_shape=_sds((R, C), F32))(gparts)


def _adam_w_ada(c_t, dm, w, m, v):
    R, C = w.shape

    def body(c_ref, dm_ref, w_ref, m_ref, v_ref, go_ref, d_ref, mo_ref, vo_ref):
        g = c_ref[:, 0:1] * dm_ref[0:1, :]
        for b in range(1, 8):
            g = g + c_ref[:, b:b + 1] * dm_ref[b:b + 1, :]
        d, mn, vn = _adam_math(w_ref[...], g, m_ref[...], v_ref[...])
        go_ref[...] = g
        d_ref[...] = d
        mo_ref[...] = mn
        vo_ref[...] = vn

    return pl.pallas_call(body, name="adam_w_ada", out_shape=[_sds((R, C), F32)] * 4,
                          compiler_params=_cp())(c_t, dm, w, m, v)


def kernel(x, c, norm_g, w_ada, b_ada, w_in, pool_w, pool_scale, w_attn_br, w_pool_br, w_out, rel_bias, final_g, loss_target, m_norm_g, m_w_ada, m_b_ada, m_w_in, m_pool_w, m_pool_scale, m_w_attn_br, m_w_pool_br, m_w_out, m_rel_bias, m_final_g, v_norm_g, v_w_ada, v_b_ada, v_w_in, v_pool_w, v_pool_scale, v_w_attn_br, v_w_pool_br, v_w_out, v_rel_bias, v_final_g):
    S = x.shape[1]
    px, py, pc = _my_place()
    me = 4 * px + 2 * py + pc
    cidx = jnp.reshape(pc, (1,)).astype(jnp.int32)

    layout = [(1, 1024), (1, 128), (1, 128), (0, 128)]
    final_g2 = final_g.reshape(1, D)
    me1 = jnp.reshape(me, (1,)).astype(jnp.int32)

    b_part = lax.dynamic_slice(b_ada, (0, me * 384), (1, 384))
    c_all, mod_all = _adaln_exchange(c, w_ada[0], b_part)
    c_all = c_all.reshape(8, D)
    mod = mod_all.reshape(1, 3 * D)
    shift, scale, gate = mod[:, :D], mod[:, D:2 * D], mod[:, 2 * D:]

    gplan = _gather_chips_plan([(None, 1)])
    lands = _place_slot([w_in[0], w_attn_br[0], w_pool_br[0], w_out[0]], me1, 8, "place_own_shards", CDT)
    g_sems, _, w_land, tok = _split_start("gather_chips_start", [], lands[:1], 2, gplan, dep=mod_all)
    st = _local_pre(x[0], shift, scale, norm_g, rel_bias, dep=tok)
    _, w_land = _split_wait("gather_chips_wait", g_sems, [], w_land, gplan,
                            [st["ht"]] + [b for pair in st["biases"] for b in pair])
    (w_in_full,) = _pair_share(w_land)

    b_sems, _, b_land, _ = _split_start("gather_branch_start", [], lands[1:], 21, _to_all_plan, dep=w_in_full)

    def branch_weights(after):
        _, (wab8, wpb8, wout8) = _split_wait("gather_branch_wait", b_sems, [], b_land, _to_all_plan, after)
        return (_cols_from_blocks(wab8, "w_attn_br_cols"), _cols_from_blocks(wpb8, "w_pool_br_cols"),
                wout8.reshape(D, D))

    st = _local_grads(st, loss_target[0], gate, w_in_full, pool_w[0], pool_scale, branch_weights, final_g2)

    early = [st["dpw"].reshape(AW, 128), st["dps"], st["gfg"], jnp.broadcast_to(st["loss"], (1, 128))]
    dws = [st["dw_in"], st["dw_ab"], st["dw_pb"], st["dw_out"]]
    sent = [st["dw_in_low"]] + dws[1:] + early
    pplan = _pair_plan(layout)
    pair_shapes = [(4, w, a.shape[1]) if ax == 0 else (4, a.shape[0], w) for a, (ax, w) in zip(dws, layout)]
    p_sems, sent, recv, tok = _split_start(
        "reduce_pair_start", sent,
        [lax.empty(s, a.dtype) for s, a in zip(pair_shapes + [a.shape for a in early], sent)], 20, pplan)
    d_rel_bias = _local_bias_grads(st, _BIAS_GRADS_EARLY, dep=tok)
    sent, recv = _split_wait("reduce_pair_wait", p_sems, sent, recv, pplan, d_rel_bias)
    parts = [_pair_add(dw, r, ax, w, cidx, f"pair_add{i}")
             for i, (dw, r, (ax, w)) in enumerate(zip([dws[0]] + sent[1:4], recv[:4], layout))]
    parts += _add_pairs(sent[4:], recv[4:], "pair_add_smalls")

    slot1 = jnp.reshape(2 * px + py, (1,)).astype(jnp.int32)
    cplan = _chips_plan(4)
    c_sems, parts, lands, tok = _split_start("reduce_chips_start", parts,
                                             _place_slot(parts, slot1, 4, "place_own_partials"), 24, cplan)
    d_rel_bias = d_rel_bias + _local_bias_grads(st, _BIAS_GRADS_LATE, dep=tok)
    grad_x, dmod, d_norm_g = _local_input_grads(st, w_in_full, norm_g, scale, dep=tok)

    dmod_all, ng_all, rb_all = _all_gather_small([dmod, d_norm_g, d_rel_bias], "gather_late")
    o_norm_g = _adam(ng_all, norm_g, m_norm_g, v_norm_g, "adam_norm_g")
    o_rel_bias = _adam(rb_all, rel_bias, m_rel_bias, v_rel_bias, "adam_rel_bias")
    o_b_ada = _adam(dmod_all, b_ada, m_b_ada, v_b_ada, "adam_b_ada")
    dm_mine = lax.dynamic_slice(dmod_all.reshape(8, 3 * D), (0, me * 384), (8, 384))
    o_w_ada = [o[None] for o in _adam_w_ada(c_all.T, dm_mine, w_ada[0], m_w_ada[0], v_w_ada[0])]

    _, (g_in, g_ab, g_pb, g_out, g_pw, g_ps, g_fg, g_loss) = _split_wait(
        "reduce_chips_wait", c_sems, parts, lands, cplan, [o_w_ada[0], o_b_ada[0], o_norm_g[0], o_rel_bias[0]])
    o_w_in = [o[None] for o in _adam(g_in, w_in[0], m_w_in[0], v_w_in[0], "adam_w_in")]
    o_w_ab = [o[None] for o in _adam(g_ab, w_attn_br[0], m_w_attn_br[0], v_w_attn_br[0], "adam_w_attn_br")]
    o_w_pb = [o[None] for o in _adam(g_pb, w_pool_br[0], m_w_pool_br[0], v_w_pool_br[0], "adam_w_pool_br")]
    o_w_out = [o[None] for o in _adam(g_out, w_out[0], m_w_out[0], v_w_out[0], "adam_w_out")]
    o_pool_w = [o.reshape(pool_w.shape) for o in _adam(g_pw, pool_w.reshape(AW, 128), m_pool_w.reshape(AW, 128),
                                                       v_pool_w.reshape(AW, 128), "adam_pool_w")]
    o_pool_scale = _adam(g_ps, pool_scale, m_pool_scale, v_pool_scale, "adam_pool_scale")
    o_final_g = [o.reshape(D) for o in _adam(g_fg, final_g2, m_final_g.reshape(1, D), v_final_g.reshape(1, D),
                                             "adam_final_g")]
    loss = _sum_parts(g_loss, "sum_loss")[0, 0]

    per_w = [o_norm_g, o_w_ada, o_b_ada, o_w_in, o_pool_w, o_pool_scale, o_w_ab, o_w_pb, o_w_out, o_rel_bias, o_final_g]
    outs = [loss, grad_x[None]]
    for k in range(4):
        outs += [o[k] for o in per_w]
    return tuple(outs)
```

```python
import functools
import math

import numpy as np
import jax
import jax.numpy as jnp
from jax import lax
from jax.experimental import pallas as pl
from jax.experimental.pallas import tpu as pltpu

F32 = jnp.float32
CDT = jnp.bfloat16
D = 1024
HD = 64
NH = 8
AW = 512
QKV_W = 4608
REST_W = 3584
R16 = 16
EPS = 1e-6
NEG = -1e30
N_BACK = 128
ADAM_LR, ADAM_B1, ADAM_B2, ADAM_EPS, ADAM_WD, ADAM_STEP = 0.001, 0.9, 0.999, 1e-08, 0.01, 10
MESH = pl.DeviceIdType.MESH
VMEM_MB = 1024 * 1024


def _cp(sem=None, vmem=48, **kw):
    if sem is not None:
        kw["dimension_semantics"] = sem
    return pltpu.CompilerParams(vmem_limit_bytes=vmem * VMEM_MB, **kw)


def _sds(shape, dtype):
    return jax.ShapeDtypeStruct(tuple(shape), dtype)


def _rest_blk(jj):
    return jnp.where(jj == 0, 9, jnp.where(jj == 6, 10, jj + 10))


def _sigmoid(z):
    return 0.5 * jnp.tanh(0.5 * z) + 0.5


def _dot(a, b, dims):
    return lax.dot_general(a, b, (dims, ((), ())), preferred_element_type=F32)


NN = ((1,), (0,))
NT = ((1,), (1,))
TN = ((0,), (0,))


def _ld(ref):
    if len(ref.shape) == 2:
        return ref[...]
    return jnp.concatenate([ref[a] for a in range(ref.shape[0])], axis=0)


def _st(ref, val, cols=None):
    val = val.astype(ref.dtype)
    if len(ref.shape) == 2:
        if cols is None:
            ref[...] = val
        else:
            ref[:, cols] = val
        return
    rows = ref.shape[1]
    for a in range(ref.shape[0]):
        if cols is None:
            ref[a] = val[a * rows:(a + 1) * rows]
        else:
            ref[a, :, cols] = val[a * rows:(a + 1) * rows]


REST_COLS = ((9, 10), (11, 16), (10, 11))
_HBM = pl.BlockSpec(memory_space=pltpu.HBM)


def _resident(shape):
    return pl.BlockSpec(shape, lambda *ids: (0,) * len(shape), pipeline_mode=pl.Buffered(1))


def _w512(w_ref, j):
    return w_ref[j // 2, :, 512 * (j % 2):512 * (j % 2 + 1)]


def _proj(h, w, dep=None):
    S = h.shape[0]
    tm = 512
    dep_specs, dep_args = _after(dep)

    def body(a_ref, w_ref, *rest):
        q_ref, r_ref = rest[-2:]
        a = a_ref[...]
        for j in range(9):
            q_ref[:, 512 * j:512 * (j + 1)] = _dot(a, _w512(w_ref, j), NN).astype(q_ref.dtype)
        off = 0
        for lo, hi in REST_COLS:
            for j in range(lo, hi):
                r_ref[:, off:off + 512] = _dot(a, _w512(w_ref, j), NN)
                off += 512

    return pl.pallas_call(
        body, name="proj", grid=(S // tm,),
        in_specs=[pl.BlockSpec((tm, D), lambda i: (i, 0)), _resident((8, D, 1024))] + dep_specs,
        out_specs=[pl.BlockSpec((tm, QKV_W), lambda i: (i, 0)), pl.BlockSpec((tm, REST_W), lambda i: (i, 0))],
        out_shape=[_sds((S, QKV_W), CDT), _sds((S, REST_W), F32)],
        compiler_params=_cp(("parallel",), vmem=56),
    )(h, w, *dep_args)


def _after(dep):
    return ([], []) if dep is None else ([pl.BlockSpec(memory_space=pl.ANY)], [dep])


def _dh(dqkv, drest, w, dep=None):
    S = dqkv.shape[0]
    tm = 512
    dep_specs, dep_args = _after(dep)

    def body(a_ref, b_ref, w_ref, *rest):
        o_ref = rest[-1]
        acc = _dot(a_ref[:, 0:512], _w512(w_ref, 0), NT)
        for j in range(1, 9):
            acc = acc + _dot(a_ref[:, 512 * j:512 * (j + 1)], _w512(w_ref, j), NT)
        off = 0
        for lo, hi in REST_COLS:
            for j in range(lo, hi):
                acc = acc + _dot(b_ref[:, off:off + 512], _w512(w_ref, j), NT)
                off += 512
        o_ref[...] = acc

    return pl.pallas_call(
        body, name="dh", grid=(S // tm,),
        in_specs=[pl.BlockSpec((tm, QKV_W), lambda i: (i, 0)), pl.BlockSpec((tm, REST_W), lambda i: (i, 0)),
                  _resident((8, D, 1024))] + dep_specs,
        out_specs=pl.BlockSpec((tm, D), lambda i: (i, 0)),
        out_shape=_sds((S, D), F32),
        compiler_params=_cp(("parallel",), vmem=56),
    )(dqkv, drest, w, *dep_args)


def _mm_tn(at, b, *, n_total, out_blk, name, prev=None, tn=512, with_low=False):
    R, S = at.shape
    nb = b.shape[1] // tn
    n_out = 2 if with_low else 1

    def body(*refs):
        res = _dot(refs[0][...], refs[1][...], NN)
        refs[-n_out][...] = res
        if with_low:
            refs[-1][...] = res.astype(CDT)

    in_specs = [_resident((R, S)), pl.BlockSpec((S, tn), lambda j: (0, j))]
    args = [at, b]
    aliases = {}
    if prev is not None:
        prev = list(prev) if with_low else [prev]
        in_specs += [pl.BlockSpec(memory_space=pl.ANY)] * n_out
        args += prev
        aliases = {2 + k: k for k in range(n_out)}
    out_spec = pl.BlockSpec((R, tn), lambda j: (0, out_blk(j)))
    outs = pl.pallas_call(
        body, name=name, grid=(nb,), in_specs=in_specs,
        out_specs=[out_spec] * n_out,
        out_shape=[_sds((R, n_total), F32), _sds((R, n_total), CDT)][:n_out],
        input_output_aliases=aliases,
        compiler_params=_cp(("parallel",)),
    )(*args)
    return tuple(outs) if with_low else outs[0]


def _tcast(a, name):
    M, C = a.shape
    tm = 512

    def body(a_ref, o_ref):
        o_ref[...] = a_ref[...].astype(F32).T.astype(o_ref.dtype)

    return pl.pallas_call(
        body, name=name, grid=(M // tm,),
        in_specs=[pl.BlockSpec((tm, C), lambda i: (i, 0))],
        out_specs=pl.BlockSpec((C, tm), lambda i: (0, i)),
        out_shape=_sds((C, M), CDT), compiler_params=_cp(("parallel",)),
    )(a)


TOK = R16 * R16


def _swap16():
    i = lax.broadcasted_iota(jnp.int32, (TOK, TOK), 0)
    j = lax.broadcasted_iota(jnp.int32, (TOK, TOK), 1)
    return jnp.where(j == jnp.bitwise_and(i, 15) * 16 + jnp.right_shift(i, 4), 1.0, 0.0).astype(CDT)


def _norm_mod(x, norm_g, shift, scale, dep=None):
    S = x.shape[0]
    dep_specs, dep_args = _after(dep)

    def body(x_ref, g_ref, sh_ref, sc_ref, *rest):
        xv = x_ref[...]
        xn = xv * lax.rsqrt(jnp.mean(xv * xv, axis=-1, keepdims=True) + EPS)
        h = (xn * (g_ref[...] * (1.0 + sc_ref[...])) + sh_ref[...]).astype(CDT)
        _st(rest[-1], _dot(_swap16(), h, NN))

    vec = pl.BlockSpec((1, D), lambda i: (0, 0))
    return pl.pallas_call(
        body, name="norm_mod", grid=(S // TOK,),
        in_specs=[pl.BlockSpec((TOK, D), lambda i: (i, 0)), vec, vec, vec] + dep_specs,
        out_specs=pl.BlockSpec((R16, R16, D), lambda i: (0, i, 0)),
        out_shape=_sds((R16, S // R16, D), CDT), compiler_params=_cp(("parallel",)),
    )(x, norm_g, shift, scale, *dep_args)


def _t5_bucket_np(n):
    nf = np.maximum(n, 1).astype(np.float32)
    large = 16 + (np.log(nf / np.float32(16)) / np.float32(math.log(2048 / 16)) * np.float32(16)).astype(np.int32)
    large = np.minimum(large, 31)
    return np.where(n < 16, n, large).astype(np.int32)


class _Geom:
    def __init__(self, gi, LS):
        self.gi, self.LS = gi, LS
        self.dil = (1, 4, 16)[gi]
        if gi == 0:
            self.lead, self.rows, self.n_tiles, self.has_prev = (R16,), 16, LS // 16, True
            a = np.arange(R16)[:, None]
            ll = np.arange(16)[None, :]
            self.pos = (16 * ll + a).reshape(-1)
        elif gi == 1:
            self.lead, self.rows, self.n_tiles, self.has_prev = (4,), 64, LS // 64, True
            a = np.arange(4)[:, None]
            ll = np.arange(64)[None, :]
            self.pos = (4 * ll + a).reshape(-1)
        else:
            self.lead, self.rows, self.n_tiles, self.has_prev = (), LS, 1, False
            self.pos = np.arange(LS)
        self.T = self.pos.shape[0]
        self.Tk = 2 * self.T if self.has_prev else self.T
        self.outer = {0: (), 1: (4,), 2: (R16,)}[gi]

    def view(self, arr):
        S, C = arr.shape
        if self.gi == 1:
            return arr.reshape(4, 4, self.LS, C)
        return arr.reshape(R16, self.LS, C)

    def view_shape(self, C):
        return (4, 4, self.LS, C) if self.gi == 1 else (R16, self.LS, C)

    def unview(self, arr):
        return arr.reshape(R16 * self.LS, arr.shape[-1])

    def spec(self, W, cblk, tile):
        if self.gi == 0:
            return pl.BlockSpec((R16, 16, W), lambda n: (0, tile(n), cblk))
        if self.gi == 1:
            return pl.BlockSpec((4, None, 64, W), lambda r, n: (0, r, tile(n), cblk))
        return pl.BlockSpec((None, self.LS, W), lambda r: (r, 0, cblk))

    def halves(self):
        R = self.rows
        idx = np.arange(self.T).reshape(-1, R)
        return idx[:, :R // 2].reshape(-1), idx[:, R // 2:].reshape(-1)

    def bucket_index(self, sub):
        lo, hi = self.halves()
        pos = self.pos
        if sub == 0:
            pq = pos[lo]
            pk = np.concatenate([pos[hi] - self.T, pos[lo]]) if self.has_prev else np.concatenate([pos[lo], pos[hi]])
        else:
            pq = pos[hi]
            pk = np.concatenate([pos[lo], pos[hi]])
        dist = pq[:, None] - pk[None, :]
        ok = (dist >= 0) & (dist <= N_BACK)
        bucket = _t5_bucket_np(np.clip(dist, 0, N_BACK) * self.dil)
        return np.where(ok, bucket, -1).astype(np.int32)


def _bias_table(idx, buckets, rel_bias, gi, sub, dep=None):
    T, Tk = idx.shape
    tr = 32
    dep_specs, dep_args = _after(dep)

    def body(idx_ref, rb_ref, *rest):
        h = pl.program_id(0)
        ix = idx_ref[...]
        acc = jnp.full(ix.shape, NEG, F32)
        for b in buckets:
            acc = jnp.where(ix == b, rb_ref[b, gi * NH + h], acc)
        rest[-1][...] = acc

    return pl.pallas_call(
        body, name=f"bias_table{gi}{'ab'[sub]}", grid=(NH, T // tr),
        in_specs=[pl.BlockSpec((tr, Tk), lambda h, i: (i, 0)), pl.BlockSpec(memory_space=pltpu.SMEM)] + dep_specs,
        out_specs=pl.BlockSpec((None, tr, Tk), lambda h, i: (h, i, 0)),
        out_shape=_sds((NH, T, Tk), F32), compiler_params=_cp(("parallel", "parallel")),
    )(idx, rel_bias, *dep_args)


def _bias_grad(dsum, idx, buckets, gi, sub, dep=None):
    _, T, Tk = dsum.shape
    dep_specs, dep_args = _after(dep)

    def body(ds_ref, idx_ref, *rest):
        o_ref, r_ref = rest[-2], rest[-1]
        r_ref[...] = jnp.zeros_like(r_ref)
        ix, ds = idx_ref[...], ds_ref[...]
        for b in buckets:
            r_ref[b:b + 1, :] = jnp.sum(jnp.where(ix == b, ds, 0.0), axis=0, keepdims=True)
        o_ref[...] = jnp.broadcast_to(jnp.sum(r_ref[...], axis=1, keepdims=True), (32, 128))

    out = pl.pallas_call(
        body, name=f"bias_grad{gi}{'ab'[sub]}", grid=(NH,),
        in_specs=[pl.BlockSpec((None, T, Tk), lambda h: (h, 0, 0)), pl.BlockSpec((T, Tk), lambda h: (0, 0))] + dep_specs,
        out_specs=pl.BlockSpec((None, 32, 128), lambda h: (h, 0, 0)),
        out_shape=_sds((NH, 32, 128), F32), scratch_shapes=[pltpu.VMEM((32, Tk), F32)],
        compiler_params=_cp(("parallel",)),
    )(dsum, idx, *dep_args)
    return out[:, :, 0].T


def _prev_mask(n, T, Tk):
    col = lax.broadcasted_iota(jnp.int32, (1, Tk), 1)
    return jnp.where((col < T) & (n == 0), NEG, 0.0).astype(F32)


def _split_rows(v, R):
    A, half = v.shape[0] // R, R // 2
    via_f32 = v.dtype != F32 and half % 16 != 0
    w = v.astype(F32) if via_f32 else v
    lo = jnp.concatenate([w[a * R:a * R + half] for a in range(A)], axis=0)
    hi = jnp.concatenate([w[a * R + half:(a + 1) * R] for a in range(A)], axis=0)
    return (lo.astype(v.dtype), hi.astype(v.dtype)) if via_f32 else (lo, hi)


def _merge_rows(lo, hi, R):
    half = R // 2
    A = lo.shape[0] // half
    return jnp.concatenate([x[a * half:(a + 1) * half] for a in range(A) for x in (lo, hi)], axis=0)


def _attn_fwd(qkv, biases, geom, others=None):
    S = qkv.shape[0]
    gi, T, R, has_prev = geom.gi, geom.T, geom.rows, geom.has_prev
    H2 = T // 2
    qv = geom.view(qkv)
    cq, ck, cv = 3 * gi, 3 * gi + 1, 3 * gi + 2
    assert others is None or not has_prev

    def body(*refs):
        if has_prev:
            q_ref, kp_ref, kc_ref, vp_ref, vc_ref, ba_ref, bb_ref, o_ref, l_ref = refs
        elif others is not None:
            (q_ref, kc_ref, vc_ref, ba_ref, bb_ref, o0_ref, o1_ref, l0_ref, l1_ref, z_ref,
             attn_ref, L_ref, ag_ref) = refs
        else:
            q_ref, kc_ref, vc_ref, ba_ref, bb_ref, o_ref, l_ref = refs
        q_lo, q_hi = _split_rows(_ld(q_ref), R)
        kc_lo, kc_hi = _split_rows(_ld(kc_ref), R)
        vc_lo, vc_hi = _split_rows(_ld(vc_ref), R)
        k_b = jnp.concatenate([kc_lo, kc_hi], axis=0)
        v_b = jnp.concatenate([vc_lo, vc_hi], axis=0)
        if has_prev:
            n = pl.program_id(len(geom.outer))
            k_a = jnp.concatenate([_split_rows(_ld(kp_ref), R)[1], kc_lo], axis=0)
            v_a = jnp.concatenate([_split_rows(_ld(vp_ref), R)[1], vc_lo], axis=0)
            pm = _prev_mask(n, H2, T)
        else:
            k_a, v_a, pm = k_b, v_b, None
        low = lax.broadcasted_iota(jnp.int32, (1, 2 * HD), 1) < HD
        for j in range(NH // 2):
            cols = slice(2 * HD * j, 2 * HD * (j + 1))
            o_half, l_half = [], []
            for q, k, v, b_ref, mask in ((q_lo, k_a, v_a, ba_ref, pm), (q_hi, k_b, v_b, bb_ref, None)):
                q2 = q[:, cols].astype(F32) * 0.125
                k2, v2 = k[:, cols], v[:, cols]
                qs = jnp.concatenate([jnp.where(low, q2, 0.0), jnp.where(low, 0.0, q2)], axis=0).astype(q.dtype)
                s = _dot(qs, k2, NT) + b_ref[j]
                if mask is not None:
                    s = s + mask
                m = jnp.max(s, axis=1, keepdims=True)
                p = jnp.exp(s - m)
                l = jnp.sum(p, axis=1, keepdims=True)
                pv = _dot(p.astype(v.dtype), v2, NN) / l
                lse = m + jnp.log(l)
                o_half.append(jnp.where(low, pv[:H2], pv[H2:]))
                l_half.append(jnp.where(low, lse[:H2], lse[H2:]))
            o2, lc = _merge_rows(o_half[0], o_half[1], R), _merge_rows(l_half[0], l_half[1], R)
            if others is None:
                _st(o_ref, o2, cols)
                _st(l_ref, lc, cols)
                continue
            la, lb = l0_ref[:, cols], l1_ref[:, cols]
            m = jnp.maximum(jnp.maximum(la, lb), lc)
            ea, eb, ec = jnp.exp(la - m), jnp.exp(lb - m), jnp.exp(lc - m)
            den = ea + eb + ec
            attn = (ea * o0_ref[:, cols] + eb * o1_ref[:, cols] + ec * o2) / den
            attn_ref[:, cols] = attn
            L_ref[:, cols] = m + jnp.log(den)
            z = z_ref[:, cols]
            ag = attn * (z * _sigmoid(z))
            ag_ref[:, cols] = ag.astype(ag_ref.dtype)

    cur = lambda n: n
    prev = lambda n: jnp.maximum(n - 1, 0)
    if has_prev:
        in_specs = [geom.spec(AW, cq, cur), geom.spec(AW, ck, prev), geom.spec(AW, ck, cur),
                    geom.spec(AW, cv, prev), geom.spec(AW, cv, cur)]
        args = [qv] * 5
    else:
        in_specs = [geom.spec(AW, cq, cur), geom.spec(AW, ck, cur), geom.spec(AW, cv, cur)]
        args = [qv] * 3
    nd = len(geom.outer) + (1 if has_prev else 0)
    biases = [b.reshape(NH // 2, 2 * H2, b.shape[2]) for b in biases]
    in_specs += [pl.BlockSpec(b.shape, lambda *ids: (0, 0, 0)) for b in biases]
    grid = geom.outer + ((geom.n_tiles,) if has_prev else ())
    oshape = geom.view_shape(AW)
    tile = geom.spec(AW, 0, cur)
    if others is None:
        o, l = pl.pallas_call(
            body, name=f"attn_fwd{gi}", grid=grid, in_specs=in_specs, out_specs=[tile, tile],
            out_shape=[_sds(oshape, F32), _sds(oshape, F32)],
            compiler_params=_cp(("arbitrary",) * nd),
        )(*args, *biases)
        return geom.unview(o), geom.unview(l)
    os_, ls_, rest = others
    attn, Lb, ag = pl.pallas_call(
        body, name=f"attn_fwd{gi}_merge", grid=grid, in_specs=in_specs + [tile] * 5,
        out_specs=[tile, tile, tile],
        out_shape=[_sds(oshape, F32), _sds(oshape, F32), _sds(oshape, CDT)],
        compiler_params=_cp(("arbitrary",) * nd),
    )(*args, *biases, *[geom.view(a) for a in (*os_, *ls_, rest)])
    return geom.unview(attn), geom.unview(Lb), geom.unview(ag)


def _pool_counts(r, LS, win):
    l = lax.broadcasted_iota(jnp.int32, (LS, 1), 0)
    return jnp.minimum(16 * l + (r + 1), win).astype(F32)


def _pool_fwd(rest, pool_w, pool_scale):
    S = rest.shape[0]
    LS = S // R16
    r3 = rest.reshape(R16, LS, REST_W)
    PG = 128

    def body(u_ref, z_ref, pw_ref, ps_ref, pt_ref, mp_ref, pg_ref, e_ref):
        g = pl.program_id(0)
        win = jnp.left_shift(2, g)
        row = lax.broadcasted_iota(jnp.int32, (LS, PG), 0)
        for r in range(R16):
            e_ref[R16 + r] = u_ref[r]
        for r in range(1, R16):
            e_ref[r] = jnp.where(row == 0, 0.0, pltpu.roll(u_ref[r], 1, 0))

        def double(s):
            for i in range(2 * R16 - 1, 2 * s - 1, -1):
                e_ref[i] = e_ref[i] + e_ref[i - s]

        double(1)
        for k in (1, 2, 3):
            pl.when(g >= k)(functools.partial(double, 2 ** k))
        pw = pw_ref[...].astype(CDT)
        for r in range(R16):
            u = u_ref[r]
            pooled = e_ref[R16 + r] / _pool_counts(r, LS, win) - u
            pc = pooled.astype(CDT)
            mp = _dot(pc, pw, NN)
            z = z_ref[r]
            pg = (mp * ps_ref[...]) * (z * _sigmoid(z))
            mp_ref[r] = mp
            pg_ref[r] = pg.astype(pg_ref.dtype)
            pt_ref[:, r * LS:(r + 1) * LS] = pooled.T.astype(pt_ref.dtype)

    col = lambda off: pl.BlockSpec((R16, LS, PG), lambda g: (0, 0, off + g))
    tsp = pl.BlockSpec((PG, S), lambda g: (g, 0))
    pt, mp, pg = pl.pallas_call(
        body, name="pool_fwd", grid=(4,),
        in_specs=[col(24), col(4), pl.BlockSpec((None, PG, PG), lambda g: (g, 0, 0)),
                  pl.BlockSpec((1, PG), lambda g: (0, g))],
        out_specs=[tsp, col(0), col(0)],
        out_shape=[_sds((AW, S), CDT), _sds((R16, LS, AW), F32), _sds((R16, LS, AW), CDT)],
        scratch_shapes=[pltpu.VMEM((2 * R16, LS, PG), F32)],
        compiler_params=_cp(("parallel",)),
    )(r3, r3, pool_w, pool_scale)
    return pt, mp.reshape(S, AW), pg.reshape(S, AW)


def _branch_merge(ag, pg, wab, wpb, rest):
    S = ag.shape[0]
    tm, tn = 512, 512

    def body(ag_ref, pg_ref, wa_ref, wp_ref, ga_ref, gp_ref, m_ref, mt_ref):
        ya = _dot(ag_ref[...], wa_ref[...], NN)
        yp = _dot(pg_ref[...], wp_ref[...], NN)
        mg = _sigmoid(ga_ref[...]) * ya + _sigmoid(gp_ref[...]) * yp
        m_ref[...] = mg.astype(m_ref.dtype)
        mt_ref[...] = mg.T.astype(mt_ref.dtype)

    a = pl.BlockSpec((tm, AW), lambda i, j: (i, 0))
    w = pl.BlockSpec((AW, tn), lambda i, j: (0, j))
    o = pl.BlockSpec((tm, tn), lambda i, j: (i, j))
    return pl.pallas_call(
        body, name="branch_merge", grid=(S // tm, D // tn),
        in_specs=[a, a, w, w, pl.BlockSpec((tm, tn), lambda i, j: (i, 2 + j)),
                  pl.BlockSpec((tm, tn), lambda i, j: (i, 4 + j))],
        out_specs=[o, pl.BlockSpec((tn, tm), lambda i, j: (j, i))],
        out_shape=[_sds((S, D), CDT), _sds((D, S), CDT)],
        compiler_params=_cp(("parallel", "parallel")),
    )(ag, pg, wab, wpb, rest, rest)


def _merge_out_loss(ag, pg, rest, wab, wpb, wout, x, target, gate, final_g):
    S = x.shape[0]
    LS = S // R16

    def body(ag_ref, pg_ref, ga_ref, gp_ref, wa_ref, wp_ref, w_ref, x_ref, t_ref, gate_ref, fg_ref,
             dx2_ref, dmo_ref, loss_ref, gfg_ref, dgate_ref, dw_ref):
        i = pl.program_id(0)

        @pl.when(i == 0)
        def _():
            loss_ref[...] = jnp.zeros_like(loss_ref)
            gfg_ref[...] = jnp.zeros_like(gfg_ref)
            dgate_ref[...] = jnp.zeros_like(dgate_ref)
            dw_ref[...] = jnp.zeros_like(dw_ref)

        swap = _swap16()
        ya = _dot(_ld(ag_ref), wa_ref[...], NN)
        yp = _dot(_ld(pg_ref), wp_ref[...], NN)
        mg = (_sigmoid(_ld(ga_ref)) * ya + _sigmoid(_ld(gp_ref)) * yp).astype(CDT)
        m_nat = _dot(swap, mg, NN)
        mo = _dot(m_nat.astype(CDT), w_ref[...], NN)
        gate, fg = gate_ref[...], fg_ref[...]
        x2 = x_ref[...] + gate * mo
        r2 = lax.rsqrt(jnp.mean(x2 * x2, axis=-1, keepdims=True) + EPS)
        xn2 = x2 * r2
        diff = xn2 * fg - t_ref[...]
        loss = jnp.sum(jnp.sum(diff * diff, axis=-1, keepdims=True), axis=0, keepdims=True)
        dy = diff * (1.0 / D)
        dxn2 = dy * fg
        dx2 = r2 * (dxn2 - xn2 * jnp.mean(dxn2 * xn2, axis=-1, keepdims=True))
        dx2_ref[...] = dx2
        dmo = (dx2 * gate).astype(CDT)
        _st(dmo_ref, _dot(swap, dmo, NN))
        loss_ref[...] += jnp.broadcast_to(loss * (0.5 / D), loss_ref.shape)
        gfg_ref[...] += jnp.sum(dy * xn2, axis=0, keepdims=True)
        dgate_ref[...] += jnp.sum(dx2 * mo, axis=0, keepdims=True)
        dw_ref[...] += _dot(m_nat.T.astype(CDT), dmo, NN)

    def slab(w, cblk):
        return pl.BlockSpec((R16, R16, w), lambda i: (0, i, cblk))

    nat = pl.BlockSpec((TOK, D), lambda i: (i, 0))
    vec = pl.BlockSpec((1, D), lambda i: (0, 0))
    r3 = rest.reshape(R16, LS, REST_W)
    return pl.pallas_call(
        body, name="merge_out_loss", grid=(S // TOK,),
        in_specs=[slab(AW, 0), slab(AW, 0), slab(D, 1), slab(D, 2), _resident((AW, D)), _resident((AW, D)),
                  _resident((D, D)), nat, nat, vec, vec],
        out_specs=[nat, slab(D, 0), pl.BlockSpec((1, 128), lambda i: (0, 0)), vec, vec,
                   pl.BlockSpec((D, D), lambda i: (0, 0))],
        out_shape=[_sds((S, D), F32), _sds((R16, LS, D), CDT), _sds((1, 128), F32), _sds((1, D), F32),
                   _sds((1, D), F32), _sds((D, D), F32)],
        compiler_params=_cp(("arbitrary",)),
    )(ag.reshape(R16, LS, AW), pg.reshape(R16, LS, AW), r3, r3, wab, wpb, wout, x, target, gate, final_g)


def _out_loss(merged, wout, x, target, gate, final_g):
    S = x.shape[0]

    def body(m_ref, w_ref, x_ref, t_ref, gate_ref, fg_ref, dx2_ref, dmo_ref, loss_ref, gfg_ref, dgate_ref):
        i = pl.program_id(0)

        @pl.when(i == 0)
        def _():
            loss_ref[...] = jnp.zeros_like(loss_ref)
            gfg_ref[...] = jnp.zeros_like(gfg_ref)
            dgate_ref[...] = jnp.zeros_like(dgate_ref)

        swap = _swap16()
        mo = _dot(_dot(swap, _ld(m_ref), NN).astype(CDT), w_ref[...], NN)
        gate, fg = gate_ref[...], fg_ref[...]
        x2 = x_ref[...] + gate * mo
        r2 = lax.rsqrt(jnp.mean(x2 * x2, axis=-1, keepdims=True) + EPS)
        xn2 = x2 * r2
        diff = xn2 * fg - t_ref[...]
        loss = jnp.sum(jnp.sum(diff * diff, axis=-1, keepdims=True), axis=0, keepdims=True)
        dy = diff * (1.0 / D)
        dxn2 = dy * fg
        dx2 = r2 * (dxn2 - xn2 * jnp.mean(dxn2 * xn2, axis=-1, keepdims=True))
        dx2_ref[...] = dx2
        _st(dmo_ref, _dot(swap, (dx2 * gate).astype(CDT), NN))
        loss_ref[...] += jnp.broadcast_to(loss * (0.5 / D), loss_ref.shape)
        gfg_ref[...] += jnp.sum(dy * xn2, axis=0, keepdims=True)
        dgate_ref[...] += jnp.sum(dx2 * mo, axis=0, keepdims=True)

    slab = pl.BlockSpec((R16, R16, D), lambda i: (0, i, 0))
    nat = pl.BlockSpec((TOK, D), lambda i: (i, 0))
    vec = pl.BlockSpec((1, D), lambda i: (0, 0))
    return pl.pallas_call(
        body, name="out_loss", grid=(S // TOK,),
        in_specs=[slab, _resident((D, D)), nat, nat, vec, vec],
        out_specs=[nat, slab, pl.BlockSpec((1, 128), lambda i: (0, 0)), vec, vec],
        out_shape=[_sds((S, D), F32), _sds((R16, S // R16, D), CDT), _sds((1, 128), F32), _sds((1, D), F32),
                   _sds((1, D), F32)],
        compiler_params=_cp(("arbitrary",)),
    )(merged, wout, x, target, gate, final_g)


def _bwd_gates(dmo, wout, wab, wpb, rest, ag, pg, attn, mp, pool_scale):
    S = dmo.shape[0]
    tm = 256

    def body(dmo_ref, wo_ref, wa_ref, wp_ref, za_ref, zp_ref, ga0, ga1, gp0, gp1, ag_ref, pg_ref, at_ref, mp_ref,
             ps_ref, dwa_ref, dwp_ref, dr_ref, dat_ref, dl_ref, dmp_ref, dps_ref):
        i = pl.program_id(0)

        @pl.when(i == 0)
        def _():
            dps_ref[...] = jnp.zeros_like(dps_ref)
            dwa_ref[...] = jnp.zeros_like(dwa_ref)
            dwp_ref[...] = jnp.zeros_like(dwp_ref)

        dm = _dot(dmo_ref[...], wo_ref[...], NT)
        sa = _sigmoid(jnp.concatenate([ga0[...], ga1[...]], axis=1))
        sp = _sigmoid(jnp.concatenate([gp0[...], gp1[...]], axis=1))
        dya = (dm * sa).astype(CDT)
        dyp = (dm * sp).astype(CDT)
        ag, pg = ag_ref[...], pg_ref[...]
        dwa_ref[...] += _dot(ag.astype(F32).T.astype(CDT), dya, NN)
        dwp_ref[...] += _dot(pg.astype(F32).T.astype(CDT), dyp, NN)
        ya = _dot(ag, wa_ref[...], NN)
        yp = _dot(pg, wp_ref[...], NN)
        dr_ref[:, 1024:2048] = (dm * ya * sa * (1.0 - sa)).astype(dr_ref.dtype)
        dr_ref[:, 2048:3072] = (dm * yp * sp * (1.0 - sp)).astype(dr_ref.dtype)
        da = _dot(dya, wa_ref[...], NT)
        dp = _dot(dyp, wp_ref[...], NT)
        za = za_ref[...]
        sga = _sigmoid(za)
        attn = at_ref[...]
        dattn = da * (za * sga)
        dr_ref[:, 0:512] = (da * attn * (sga * (1.0 + za * (1.0 - sga)))).astype(dr_ref.dtype)
        dat_ref[...] = dattn.astype(dat_ref.dtype)
        prod = dattn * attn
        low = lax.broadcasted_iota(jnp.int32, (1, 2 * HD), 1) < HD
        for j in range(NH // 2):
            x2 = prod[:, 2 * HD * j:2 * HD * (j + 1)]
            s0 = jnp.sum(jnp.where(low, x2, 0.0), axis=1, keepdims=True)
            s1 = jnp.sum(jnp.where(low, 0.0, x2), axis=1, keepdims=True)
            dl_ref[:, 2 * HD * j:2 * HD * (j + 1)] = jnp.where(low, s0, s1)
        zp = zp_ref[...]
        sgp = _sigmoid(zp)
        mpre = mp_ref[...]
        ps = ps_ref[...]
        dmixed = dp * (zp * sgp)
        dr_ref[:, 512:1024] = (dp * (mpre * ps) * (sgp * (1.0 + zp * (1.0 - sgp)))).astype(dr_ref.dtype)
        dps_ref[...] += jnp.sum(dmixed * mpre, axis=0, keepdims=True)
        dmp_ref[...] = (dmixed * ps).astype(dmp_ref.dtype)

    full = lambda shape: pl.BlockSpec(shape, lambda i: (0, 0))
    tD = pl.BlockSpec((tm, D), lambda i: (i, 0))
    tA = pl.BlockSpec((tm, AW), lambda i: (i, 0))
    rb = lambda c: pl.BlockSpec((tm, AW), lambda i: (i, c))
    return pl.pallas_call(
        body, name="bwd_gates", grid=(S // tm,),
        in_specs=[tD, full((D, D)), full((AW, D)), full((AW, D)), rb(0), rb(1), rb(2), rb(3), rb(4), rb(5),
                  tA, tA, tA, tA, full((1, AW))],
        out_specs=[full((AW, D)), full((AW, D)), pl.BlockSpec((tm, 3072), lambda i: (i, 0)), tA, tA, tA, full((1, AW))],
        out_shape=[_sds((AW, D), F32), _sds((AW, D), F32), _sds((S, REST_W), CDT), _sds((S, AW), CDT),
                   _sds((S, AW), F32), _sds((S, AW), CDT), _sds((1, AW), F32)],
        compiler_params=_cp(("arbitrary",), vmem=56),
    )(dmo, wout, wab, wpb, rest, rest, rest, rest, rest, rest, ag, pg, attn, mp, pool_scale)


def _pool_bwd(dmp, pt, pool_w, drest):
    S = dmp.shape[0]
    LS = S // R16
    PG = 128
    d3 = dmp.reshape(R16, LS, AW)
    dr3 = drest.reshape(R16, LS, REST_W)

    def body(d_ref, pt_ref, pw_ref, dr_in, du_ref, dpw_ref, g_ref, s_ref):
        del dr_in
        g = pl.program_id(0)
        win = jnp.left_shift(2, g)
        row = lax.broadcasted_iota(jnp.int32, (LS, PG), 0)
        pw = pw_ref[...].astype(CDT)
        dpw = jnp.zeros((PG, PG), F32)
        for r in range(R16):
            dm = d_ref[r]
            dpw = dpw + _dot(pt_ref[:, r * LS:(r + 1) * LS], dm, NN)
            dpooled = _dot(dm, pw, NT)
            g_ref[r] = dpooled
            sc = dpooled / _pool_counts(r, LS, win)
            s_ref[r] = sc
            if r < R16 - 1:
                s_ref[R16 + r] = jnp.where(row == LS - 1, 0.0, pltpu.roll(sc, LS - 1, 0))
        dpw_ref[...] = dpw

        def double(s):
            for i in range(0, 2 * R16 - 2 * s):
                s_ref[i] = s_ref[i] + s_ref[i + s]

        double(1)
        for k in (1, 2, 3):
            pl.when(g >= k)(functools.partial(double, 2 ** k))
        for r in range(R16):
            du_ref[r] = (s_ref[r] - g_ref[r]).astype(du_ref.dtype)

    du, dpw = pl.pallas_call(
        body, name="pool_bwd", grid=(4,),
        in_specs=[pl.BlockSpec((R16, LS, PG), lambda g: (0, 0, g)), pl.BlockSpec((PG, S), lambda g: (g, 0)),
                  pl.BlockSpec((None, PG, PG), lambda g: (g, 0, 0)), pl.BlockSpec(memory_space=pl.ANY)],
        out_specs=[pl.BlockSpec((R16, LS, PG), lambda g: (0, 0, 24 + g)),
                   pl.BlockSpec((None, PG, PG), lambda g: (g, 0, 0))],
        out_shape=[_sds((R16, LS, REST_W), CDT), _sds((4, PG, PG), F32)],
        scratch_shapes=[pltpu.VMEM((R16, LS, PG), F32), pltpu.VMEM((2 * R16, LS, PG), F32)],
        input_output_aliases={3: 0},
        compiler_params=_cp(("parallel",)),
    )(d3, pt, pool_w, dr3)
    return du.reshape(S, REST_W), dpw


def _attn_bwd(qkv, dattn, Lb, dlb, biases, geom, dqkv_prev):
    S = qkv.shape[0]
    gi, T, R, has_prev, N = geom.gi, geom.T, geom.rows, geom.has_prev, geom.n_tiles
    H2 = T // 2
    nd_outer = len(geom.outer)
    qv = geom.view(qkv)
    cq, ck, cv = 3 * gi, 3 * gi + 1, 3 * gi + 2
    W3 = 3 * AW

    def heads(q_ref, kp_ref, kc_ref, vp_ref, vc_ref, do_ref, L_ref, dl_ref, ba_ref, bb_ref, pm, dsa_ref, dsb_ref, emit):
        q_lo, q_hi = _split_rows(_ld(q_ref), R)
        do_lo, do_hi = _split_rows(_ld(do_ref), R)
        L_lo, L_hi = _split_rows(_ld(L_ref), R)
        dl_lo, dl_hi = _split_rows(_ld(dl_ref), R)
        kc_lo, kc_hi = _split_rows(_ld(kc_ref), R)
        vc_lo, vc_hi = _split_rows(_ld(vc_ref), R)
        k_b = jnp.concatenate([kc_lo, kc_hi], axis=0)
        v_b = jnp.concatenate([vc_lo, vc_hi], axis=0)
        if has_prev:
            k_a = jnp.concatenate([_split_rows(_ld(kp_ref), R)[1], kc_lo], axis=0)
            v_a = jnp.concatenate([_split_rows(_ld(vp_ref), R)[1], vc_lo], axis=0)
        else:
            k_a, v_a = k_b, v_b
        low = lax.broadcasted_iota(jnp.int32, (1, 2 * HD), 1) < HD
        for j in range(NH // 2):
            sl = slice(2 * HD * j, 2 * HD * (j + 1))
            res = []
            for q, k, v, do, L, dl, b_ref, mask, ds_ref in (
                    (q_lo, k_a, v_a, do_lo, L_lo, dl_lo, ba_ref, pm, dsa_ref),
                    (q_hi, k_b, v_b, do_hi, L_hi, dl_hi, bb_ref, None, dsb_ref)):
                q2 = q[:, sl].astype(F32) * 0.125
                do2 = do[:, sl].astype(F32)
                k2, v2 = k[:, sl], v[:, sl]
                L2, dl2 = L[:, sl], dl[:, sl]
                qs = jnp.concatenate([jnp.where(low, q2, 0.0), jnp.where(low, 0.0, q2)], axis=0).astype(q.dtype)
                dom = jnp.concatenate([jnp.where(low, do2, 0.0), jnp.where(low, 0.0, do2)], axis=0).astype(q.dtype)
                Lc = jnp.concatenate([L2[:, 0:1], L2[:, HD:HD + 1]], axis=0)
                dlc = jnp.concatenate([dl2[:, 0:1], dl2[:, HD:HD + 1]], axis=0)
                s = _dot(qs, k2, NT) + b_ref[j]
                if mask is not None:
                    s = s + mask
                p = jnp.exp(s - Lc)
                ds = p * (_dot(dom, v2, NT) - dlc)
                ds_ref[j] += ds
                dsc = ds.astype(q.dtype)
                dq_st = _dot(dsc, k2, NN)
                res.append((jnp.where(low, dq_st[:H2], dq_st[H2:]) * 0.125, _dot(dsc, qs, TN),
                            _dot(p.astype(q.dtype), dom, TN)))
            (dq_lo, dk_a, dv_a), (dq_hi, dk_b, dv_b) = res
            dq = _merge_rows(dq_lo, dq_hi, R)
            if has_prev:
                emit(sl, dq, _merge_rows(dk_a[H2:] + dk_b[:H2], dk_b[H2:], R),
                     _merge_rows(dv_a[H2:] + dv_b[:H2], dv_b[H2:], R), dk_a[:H2], dv_a[:H2])
            else:
                emit(sl, dq, _merge_rows(dk_a[:H2] + dk_b[:H2], dk_a[H2:] + dk_b[H2:], R),
                     _merge_rows(dv_a[:H2] + dv_b[:H2], dv_a[H2:] + dv_b[H2:], R), None, None)

    def first_step():
        ok = pl.program_id(nd_outer) == 0 if has_prev else pl.program_id(0) == 0
        for a in range(nd_outer if has_prev else 0):
            ok = ok & (pl.program_id(a) == 0)
        return ok

    if has_prev:
        def body(q_ref, kp_ref, kc_ref, vp_ref, vc_ref, do_ref, L_ref, dl_ref, ba_ref, bb_ref, prev_in, out_ref,
                 dsa_ref, dsb_ref, hold, new, pp):
            del prev_in
            s_id = pl.program_id(nd_outer)

            @pl.when(first_step())
            def _():
                dsa_ref[...] = jnp.zeros_like(dsa_ref)
                dsb_ref[...] = jnp.zeros_like(dsb_ref)

            @pl.when(s_id < N)
            def _():
                def emit(sl, dq, dk, dv, dk_prev, dv_prev):
                    new[:, sl] = dq
                    new[:, slice(AW + sl.start, AW + sl.stop)] = dk
                    new[:, slice(2 * AW + sl.start, 2 * AW + sl.stop)] = dv
                    pp[:, sl] = dk_prev
                    pp[:, slice(AW + sl.start, AW + sl.stop)] = dv_prev

                heads(q_ref, kp_ref, kc_ref, vp_ref, vc_ref, do_ref, L_ref, dl_ref, ba_ref, bb_ref,
                      _prev_mask(s_id, H2, T), dsa_ref, dsb_ref, emit)

            @pl.when((s_id >= 1) & (s_id < N))
            def _():
                _st(out_ref, hold[:, 0:AW], slice(0, AW))
                late = _merge_rows(jnp.zeros((H2, 2 * AW), F32), pp[...], R)
                _st(out_ref, hold[:, AW:W3] + late, slice(AW, W3))

            @pl.when(s_id == N)
            def _():
                _st(out_ref, hold[...])

            @pl.when(s_id < N)
            def _():
                hold[...] = new[...]

        cur = lambda n: jnp.minimum(n, N - 1)
        prev = lambda n: jnp.clip(n - 1, 0, N - 1)
        delayed = lambda n: jnp.maximum(n - 1, 0)
        in_specs = [geom.spec(AW, cq, cur), geom.spec(AW, ck, prev), geom.spec(AW, ck, cur),
                    geom.spec(AW, cv, prev), geom.spec(AW, cv, cur),
                    geom.spec(AW, 0, cur), geom.spec(AW, 0, cur), geom.spec(AW, 0, cur)]
        args = [qv] * 5 + [geom.view(dattn), geom.view(Lb), geom.view(dlb)]
        out_spec = geom.spec(W3, gi, delayed)
        grid = geom.outer + (N + 1,)
        scratch = [pltpu.VMEM((T, W3), F32), pltpu.VMEM((T, W3), F32), pltpu.VMEM((H2, 2 * AW), F32)]
    else:
        def body(q_ref, kc_ref, vc_ref, do_ref, L_ref, dl_ref, ba_ref, bb_ref, prev_in, out_ref, dsa_ref, dsb_ref):
            del prev_in

            @pl.when(first_step())
            def _():
                dsa_ref[...] = jnp.zeros_like(dsa_ref)
                dsb_ref[...] = jnp.zeros_like(dsb_ref)

            def emit(sl, dq, dk, dv, dk_prev, dv_prev):
                _st(out_ref, dq, sl)
                _st(out_ref, dk, slice(AW + sl.start, AW + sl.stop))
                _st(out_ref, dv, slice(2 * AW + sl.start, 2 * AW + sl.stop))

            heads(q_ref, None, kc_ref, None, vc_ref, do_ref, L_ref, dl_ref, ba_ref, bb_ref, None, dsa_ref, dsb_ref,
                  emit)

        cur = lambda n: n
        in_specs = [geom.spec(AW, cq, cur), geom.spec(AW, ck, cur), geom.spec(AW, cv, cur),
                    geom.spec(AW, 0, cur), geom.spec(AW, 0, cur), geom.spec(AW, 0, cur)]
        args = [qv] * 3 + [geom.view(dattn), geom.view(Lb), geom.view(dlb)]
        out_spec = geom.spec(W3, gi, cur)
        grid = geom.outer
        scratch = []

    whole = lambda b: pl.BlockSpec(b.shape, lambda *ids: (0, 0, 0))
    biases = [b.reshape(NH // 2, 2 * H2, b.shape[2]) for b in biases]
    in_specs += [whole(b) for b in biases]
    in_specs.append(pl.BlockSpec(memory_space=pl.ANY))
    if dqkv_prev is None:
        last, aliases = biases[0], {}
    else:
        last, aliases = geom.view(dqkv_prev), {len(in_specs) - 1: 0}
    dq, dsa, dsb = pl.pallas_call(
        body, name=f"attn_bwd{gi}", grid=grid, in_specs=in_specs,
        out_specs=[out_spec] + [whole(b) for b in biases],
        out_shape=[_sds(geom.view_shape(QKV_W), CDT)] + [_sds(b.shape, F32) for b in biases],
        scratch_shapes=scratch,
        input_output_aliases=aliases,
        compiler_params=_cp(("arbitrary",) * len(grid), vmem=56),
    )(*args, *biases, last)
    return geom.unview(dq), [d.reshape(NH, H2, d.shape[2]) for d in (dsa, dsb)]


def _h_bwd(dh, dx2, x, norm_g, scale):
    S = x.shape[0]

    def body(dh_ref, dx2_ref, x_ref, g_ref, sc_ref, gx_ref, dsh_ref, dsc_ref, dng_ref, nat_ref):
        i = pl.program_id(0)

        @pl.when(i == 0)
        def _():
            dsh_ref[...] = jnp.zeros_like(dsh_ref)
            dsc_ref[...] = jnp.zeros_like(dsc_ref)
            dng_ref[...] = jnp.zeros_like(dng_ref)

        for r in range(R16):
            nat_ref[:, r, :] = dh_ref[r]
        dh = jnp.concatenate([nat_ref[l] for l in range(R16)], axis=0)
        g = g_ref[...]
        one_sc = 1.0 + sc_ref[...]
        xv = x_ref[...]
        r1 = lax.rsqrt(jnp.mean(xv * xv, axis=-1, keepdims=True) + EPS)
        xn = xv * r1
        dhx = dh * xn
        dxn = dh * (g * one_sc)
        dx1 = r1 * (dxn - xn * jnp.mean(dxn * xn, axis=-1, keepdims=True))
        gx_ref[...] = dx2_ref[...] + dx1
        dsh_ref[...] += jnp.sum(dh, axis=0, keepdims=True)
        dsc_ref[...] += jnp.sum(dhx * g, axis=0, keepdims=True)
        dng_ref[...] += jnp.sum(dhx * one_sc, axis=0, keepdims=True)

    nat = pl.BlockSpec((TOK, D), lambda i: (i, 0))
    vec = pl.BlockSpec((1, D), lambda i: (0, 0))
    return pl.pallas_call(
        body, name="h_bwd", grid=(S // TOK,),
        in_specs=[pl.BlockSpec((R16, R16, D), lambda i: (0, i, 0)), nat, nat, vec, vec],
        out_specs=[nat, vec, vec, vec],
        out_shape=[_sds((S, D), F32), _sds((1, D), F32), _sds((1, D), F32), _sds((1, D), F32)],
        scratch_shapes=[pltpu.VMEM((R16, R16, D), F32)],
        compiler_params=_cp(("arbitrary",)),
    )(dh, dx2, x, norm_g, scale)


def _local_step(x, target, shift, scale, gate, norm_g, w_in_full, pool_w, pool_scale, wab, wpb, wout, rel_bias,
                final_g):
    st = _local_pre(x, shift, scale, norm_g, rel_bias)
    st = _local_grads(st, target, gate, w_in_full, pool_w, pool_scale, lambda after: (wab, wpb, wout), final_g)
    d_rel_bias = _local_bias_grads(st, _BIAS_GRADS_EARLY + _BIAS_GRADS_LATE)
    gx, dmod, dng = _local_input_grads(st, w_in_full, norm_g, scale)
    return dict(loss=st["loss"], grad_x=gx, dw_in=st["dw_in"], dw_ab=st["dw_ab"], dw_pb=st["dw_pb"],
                dw_out=st["dw_out"], d_norm_g=dng, d_pool_w=st["dpw"], d_pool_scale=st["dps"],
                d_rel_bias=d_rel_bias, d_final_g=st["gfg"], dmod=dmod)


def _local_pre(x, shift, scale, norm_g, rel_bias, dep=None):
    S = x.shape[0]
    LS = S // R16
    geoms = [_Geom(gi, LS) for gi in range(3)]
    idx_np = [[g.bucket_index(sub) for sub in range(2)] for g in geoms]
    buckets = [[tuple(int(b) for b in np.unique(ix) if b >= 0) for ix in pair] for pair in idx_np]
    idxs = [[jnp.asarray(ix) for ix in pair] for pair in idx_np]
    biases = [[_bias_table(idxs[gi][sub], buckets[gi][sub], rel_bias, gi, sub, dep) for sub in range(2)]
              for gi in range(3)]
    h = _norm_mod(x, norm_g, shift, scale, dep).reshape(S, D)
    return dict(geoms=geoms, idxs=idxs, buckets=buckets, biases=biases, x=x, h=h, ht=_tcast(h, "h_transpose"))


def _local_grads(st, target, gate, w_in_full, pool_w, pool_scale, branch_weights, final_g):
    geoms, idxs, biases, x, h, ht = (st[k] for k in ("geoms", "idxs", "biases", "x", "h", "ht"))
    S = h.shape[0]
    LS = S // R16
    qkv, rest = _proj(h, w_in_full, st.get("proj_after"))

    os_, ls_ = [], []
    for gi in range(2):
        o, l = _attn_fwd(qkv, biases[gi], geoms[gi])
        os_.append(o)
        ls_.append(l)
    attn, Lb, ag = _attn_fwd(qkv, biases[2], geoms[2], others=(os_, ls_, rest))
    pt, mp, pg = _pool_fwd(rest, pool_w, pool_scale)
    wab, wpb, wout = branch_weights([ag, pg])
    dx2, dmo3, loss, gfg, dgate, dw_out = _merge_out_loss(ag, pg, rest, wab, wpb, wout, x, target, gate, final_g)
    dmo = dmo3.reshape(S, D)

    dw_ab, dw_pb, drest, dattn, dlb, dmp, dps = _bwd_gates(dmo, wout, wab, wpb, rest, ag, pg, attn, mp, pool_scale)
    drest, dpw = _pool_bwd(dmp, pt, pool_w, drest)

    dqkv = None
    dsums = []
    for gi in range(3):
        dqkv, dsum = _attn_bwd(qkv, dattn, Lb, dlb, biases[gi], geoms[gi], dqkv)
        dsums.append(dsum)

    dw_in = _mm_tn(ht, dqkv, n_total=8192, out_blk=lambda j: j, name="dw_in_qkv", with_low=True)
    dw_in, dw_in_low = _mm_tn(ht, drest, n_total=8192, out_blk=_rest_blk, name="dw_in_rest", prev=dw_in, with_low=True)
    return dict(st, loss=loss[0, 0], dw_in=dw_in, dw_in_low=dw_in_low, dw_ab=dw_ab, dw_pb=dw_pb, dw_out=dw_out, dpw=dpw, dps=dps, gfg=gfg,
                dgate=dgate, dsums=dsums, dqkv=dqkv, drest=drest, dx2=dx2)


def _local_bias_grads(st, which, dep=None):
    cols = []
    for gi in range(3):
        got = [_bias_grad(st["dsums"][gi][sub], st["idxs"][gi][sub], st["buckets"][gi][sub], gi, sub, dep)
               for sub in range(2) if (gi, sub) in which]
        cols.append(sum(got) if got else jnp.zeros((32, NH), F32))
    return jnp.concatenate(cols, axis=1)


_BIAS_GRADS_EARLY = ((0, 0), (0, 1), (1, 0), (2, 0))
_BIAS_GRADS_LATE = ((1, 1), (2, 1))


def _dh_h_bwd(dqkv, drest, w, dx2, x, norm_g, scale, dep=None):
    S = x.shape[0]
    dep_specs, dep_args = _after(dep)

    def body(a_ref, b_ref, w_ref, dx2_ref, x_ref, g_ref, sc_ref, *rest):
        gx_ref, dsh_ref, dsc_ref, dng_ref, nat_ref = rest[-5:]
        i = pl.program_id(0)

        @pl.when(i == 0)
        def _():
            dsh_ref[...] = jnp.zeros_like(dsh_ref)
            dsc_ref[...] = jnp.zeros_like(dsc_ref)
            dng_ref[...] = jnp.zeros_like(dng_ref)

        a, b = _ld(a_ref), _ld(b_ref)
        acc = _dot(a[:, 0:512], _w512(w_ref, 0), NT)
        for j in range(1, 9):
            acc = acc + _dot(a[:, 512 * j:512 * (j + 1)], _w512(w_ref, j), NT)
        off = 0
        for lo, hi in REST_COLS:
            for j in range(lo, hi):
                acc = acc + _dot(b[:, off:off + 512], _w512(w_ref, j), NT)
                off += 512
        for r in range(R16):
            nat_ref[:, r, :] = acc[R16 * r:R16 * (r + 1)]
        dh = jnp.concatenate([nat_ref[l] for l in range(R16)], axis=0)
        g = g_ref[...]
        one_sc = 1.0 + sc_ref[...]
        xv = x_ref[...]
        r1 = lax.rsqrt(jnp.mean(xv * xv, axis=-1, keepdims=True) + EPS)
        xn = xv * r1
        dhx = dh * xn
        dxn = dh * (g * one_sc)
        dx1 = r1 * (dxn - xn * jnp.mean(dxn * xn, axis=-1, keepdims=True))
        gx_ref[...] = dx2_ref[...] + dx1
        dsh_ref[...] += jnp.sum(dh, axis=0, keepdims=True)
        dsc_ref[...] += jnp.sum(dhx * g, axis=0, keepdims=True)
        dng_ref[...] += jnp.sum(dhx * one_sc, axis=0, keepdims=True)

    nat = pl.BlockSpec((TOK, D), lambda i: (i, 0))
    vec = pl.BlockSpec((1, D), lambda i: (0, 0))
    slab = lambda W: pl.BlockSpec((R16, R16, W), lambda i: (0, i, 0))
    return pl.pallas_call(
        body, name="dh_h_bwd", grid=(S // TOK,),
        in_specs=[slab(QKV_W), slab(REST_W), _resident((8, D, 1024)), nat, nat, vec, vec] + dep_specs,
        out_specs=[nat, vec, vec, vec],
        out_shape=[_sds((S, D), F32), _sds((1, D), F32), _sds((1, D), F32), _sds((1, D), F32)],
        scratch_shapes=[pltpu.VMEM((R16, R16, D), F32)],
        compiler_params=_cp(("arbitrary",), vmem=56),
    )(dqkv, drest, w, dx2, x, norm_g, scale, *dep_args)


def _local_input_grads(st, w_in_full, norm_g, scale, dep=None):
    S = st["h"].shape[0]
    LS = S // R16
    gx, dsh, dsc, dng = _dh_h_bwd(st["dqkv"].reshape(R16, LS, QKV_W), st["drest"].reshape(R16, LS, REST_W),
                                  w_in_full, st["dx2"], st["x"], norm_g, scale, dep)
    return gx, jnp.concatenate([dsh, dsc, st["dgate"]], axis=1), dng


def _my_place():
    return lax.axis_index("x"), lax.axis_index("y"), lax.axis_index("c")


def _block_of(ref, axis, idx, width):
    if axis is None:
        return ref.at[idx]
    start = pl.multiple_of(idx * width, width)
    if axis == 0:
        return ref.at[pl.ds(start, width)]
    return ref.at[:, pl.ds(start, width)]


_SEM = pl.BlockSpec(memory_space=pltpu.SEMAPHORE)
_EFFECT = pltpu.SideEffectType.DATAFLOW_SIDE_EFFECTING


def _split_start(name, srcs, lands, ncopy, plan, dep=None):
    arrs = list(srcs) + list(lands)
    na, ns = len(arrs), len(srcs)
    dep_specs, dep_args = _after(dep)

    def body(*refs):
        send_sems, recv_sems, token = refs[na + len(dep_args)], refs[na + len(dep_args) + 1], refs[-1]
        for k, (s, d, dev) in enumerate(plan(refs[:ns], refs[ns:na])):
            pltpu.make_async_remote_copy(src_ref=s, dst_ref=d, send_sem=send_sems.at[k], recv_sem=recv_sems.at[k],
                                         device_id=dev, device_id_type=MESH).start()
        token[...] = jnp.zeros_like(token)

    outs = pl.pallas_call(
        body, name=name,
        out_shape=(pltpu.SemaphoreType.DMA((ncopy,)), pltpu.SemaphoreType.DMA((ncopy,)),
                   *[pltpu.HBM(a.shape, a.dtype) for a in arrs], _sds((8, 128), F32)),
        in_specs=[_HBM] * na + dep_specs,
        out_specs=(_SEM, _SEM, *[_HBM] * na, pl.BlockSpec(memory_space=pltpu.VMEM)),
        input_output_aliases={i: 2 + i for i in range(na)},
        compiler_params=pltpu.CompilerParams(has_side_effects=_EFFECT),
    )(*[pltpu.with_memory_space_constraint(a, pltpu.HBM) for a in arrs], *dep_args)
    return (outs[0], outs[1]), list(outs[2:2 + ns]), list(outs[2 + ns:2 + na]), outs[-1]


def _split_wait(name, sems, srcs, lands, plan, after):
    arrs = list(srcs) + list(lands)
    na, ns = len(arrs), len(srcs)
    after = list(after) if isinstance(after, (list, tuple)) else [after]

    def body(*refs):
        send_sems, recv_sems = refs[na], refs[na + 1]
        for k, (s, d, dev) in enumerate(plan(refs[:ns], refs[ns:na])):
            cp = pltpu.make_async_remote_copy(src_ref=s, dst_ref=d, send_sem=send_sems.at[k], recv_sem=recv_sems.at[k],
                                              device_id=dev, device_id_type=MESH)
            cp.wait_send()
            cp.wait_recv()

    outs = pl.pallas_call(
        body, name=name, out_shape=[pltpu.HBM(a.shape, a.dtype) for a in arrs],
        in_specs=[_HBM] * na + [_SEM, _SEM] + [pl.BlockSpec(memory_space=pl.ANY)] * len(after),
        out_specs=[_HBM] * na,
        input_output_aliases={i: i for i in range(na)},
        compiler_params=pltpu.CompilerParams(has_side_effects=_EFFECT),
    )(*arrs, sems[0], sems[1], *after)
    return list(outs[:ns]), list(outs[ns:])


def _gather_chips_plan(layout):
    def plan(src, land):
        x, y, c = _my_place()
        me = 4 * x + 2 * y + c
        mine = [_block_of(land[p], layout[p][0], me, layout[p][1]) for p in range(len(land))]
        return [(mine[p], mine[p], (cx, cy, c)) for p in range(len(land)) for cx, cy in [(1 - x, y), (x, 1 - y)]]
    return plan


def _to_all_plan(src, land):
    x, y, c = _my_place()
    me = 4 * x + 2 * y + c
    flip = lambda v, f: 1 - v if f else v
    peers = [(flip(x, fx), flip(y, fy), flip(c, fc)) for fx in (0, 1) for fy in (0, 1) for fc in (0, 1)][1:]
    return [(land[p].at[me], land[p].at[me], peer) for p in range(len(land)) for peer in peers]


def _pair_plan(layout):
    def plan(src, land):
        x, y, c = _my_place()
        n = len(layout)
        return [(_block_of(src[p], layout[p][0], 2 * q + (1 - c), layout[p][1]), land[p].at[q], (x, y, 1 - c))
                for p in range(n) for q in range(4)] + [(src[p], land[p], (x, y, 1 - c)) for p in range(n, len(src))]
    return plan


def _chips_plan(n):
    def plan(src, land):
        x, y, c = _my_place()
        chips = [(1 - x, y), (x, 1 - y), (1 - x, 1 - y)]
        return [(src[p].at[2 * cx + cy], land[p].at[2 * x + y], (cx, cy, c)) for p in range(n) for cx, cy in chips] + \
               [(src[p], land[p].at[2 * x + y], (cx, cy, c)) for p in range(n, len(src)) for cx, cy in chips]
    return plan


def _pair_share(gathered):
    n = len(gathered)

    def body(*refs):
        land = refs[n:2 * n]
        send_sems, recv_sems, fwd_send, fwd_recv = refs[2 * n:]
        x, y, c = _my_place()
        k_dg = 3 - (2 * x + y)
        south = c == 0
        pass_on = 4 * jnp.where(south, 1 - x, x) + 2 * jnp.where(south, y, 1 - y) + c
        to = (jnp.where(south, x, 1 - x), jnp.where(south, 1 - y, y), c)

        def forward(p):
            return pltpu.make_async_remote_copy(
                src_ref=land[p].at[pass_on], dst_ref=land[p].at[pass_on], send_sem=fwd_send.at[p],
                recv_sem=fwd_recv.at[p], device_id=to, device_id_type=MESH)

        def to_sibling(p, k):
            blk = land[p].at[2 * k + c]
            return pltpu.make_async_remote_copy(
                src_ref=blk, dst_ref=blk, send_sem=send_sems.at[4 * p + k], recv_sem=recv_sems.at[4 * p + k],
                device_id=(x, y, 1 - c), device_id_type=MESH)

        for p in range(n):
            forward(p).start()
        for p in range(n):
            for k in range(4):
                @pl.when(k != k_dg)
                def _():
                    to_sibling(p, k).start()
        for p in range(n):
            dg = land[p].at[2 * k_dg + c]
            pltpu.make_async_remote_copy(src_ref=dg, dst_ref=dg, send_sem=fwd_send.at[p], recv_sem=fwd_recv.at[p],
                                         device_id=to, device_id_type=MESH).wait_recv()
            for k in range(4):
                @pl.when(k == k_dg)
                def _():
                    to_sibling(p, k).start()
        for p in range(n):
            forward(p).wait_send()
            for k in range(4):
                to_sibling(p, k).wait_send()
                blk = land[p].at[2 * k + 1 - c]
                pltpu.make_async_remote_copy(
                    src_ref=blk, dst_ref=blk, send_sem=send_sems.at[4 * p + k], recv_sem=recv_sems.at[4 * p + k],
                    device_id=(x, y, 1 - c), device_id_type=MESH).wait_recv()

    return pl.pallas_call(
        body, name="gather_pair_share", in_specs=[_HBM] * n, out_specs=[_HBM] * n,
        out_shape=[_sds(g.shape, g.dtype) for g in gathered],
        input_output_aliases={p: p for p in range(n)},
        scratch_shapes=[pltpu.SemaphoreType.DMA((4 * n,)), pltpu.SemaphoreType.DMA((4 * n,)),
                        pltpu.SemaphoreType.DMA((n,)), pltpu.SemaphoreType.DMA((n,))],
    )(*gathered)


def _place_slot(arrs, slot, n_slots, name, out_dtype=None):
    n = len(arrs)

    def body(s_ref, *refs):
        for p in range(n):
            refs[n + p][...] = refs[p][...].astype(refs[n + p].dtype)

    def spec(a):
        tail = a.shape[-2:]
        return pl.BlockSpec((None,) + tail, lambda i, s_ref: (s_ref[0], 0, 0))

    in_specs = [spec(a) if a.ndim == 3 else pl.BlockSpec(a.shape, lambda i, s_ref: (0, 0)) for a in arrs]
    return pl.pallas_call(
        body, name=name,
        grid_spec=pltpu.PrefetchScalarGridSpec(num_scalar_prefetch=1, grid=(1,), in_specs=in_specs,
                                               out_specs=[spec(a) for a in arrs]),
        out_shape=[_sds((n_slots,) + a.shape[-2:], out_dtype or a.dtype) for a in arrs],
        compiler_params=_cp(("arbitrary",)),
    )(slot, *arrs)


def _cols_from_blocks(w8, name):
    _, R, C = w8.shape

    def body(w_ref, o_ref):
        for e in range(8):
            o_ref[:, C * e:C * (e + 1)] = w_ref[e]

    return pl.pallas_call(body, name=name, out_shape=_sds((R, 8 * C), w8.dtype), compiler_params=_cp())(w8)


def _all_gather_small(srcs, name):
    n = len(srcs)
    vmem = pl.BlockSpec(memory_space=pltpu.VMEM)

    def body(*refs):
        src, out = refs[:n], refs[n:2 * n]
        send_sems, recv_sems = refs[2 * n:]
        x, y, c = _my_place()
        me = 4 * x + 2 * y + c
        flip = lambda v, f: 1 - v if f else v
        peers = [(flip(x, fx), flip(y, fy), flip(c, fc)) for fx in (0, 1) for fy in (0, 1) for fc in (0, 1)][1:]
        cps = []
        for p in range(n):
            out[p][me] = src[p][...]
            for k, peer in enumerate(peers):
                cps.append(pltpu.make_async_remote_copy(
                    src_ref=src[p], dst_ref=out[p].at[me], send_sem=send_sems.at[7 * p + k],
                    recv_sem=recv_sems.at[7 * p + k], device_id=peer, device_id_type=MESH))
        for cp in cps:
            cp.start()
        for cp in cps:
            cp.wait()

    return pl.pallas_call(
        body, name=name, in_specs=[vmem] * n, out_specs=[vmem] * n,
        out_shape=[_sds((8,) + a.shape, a.dtype) for a in srcs],
        scratch_shapes=[pltpu.SemaphoreType.DMA((7 * n,)), pltpu.SemaphoreType.DMA((7 * n,))],
    )(*srcs)


def _adaln_exchange(c, w_ada, b_part):
    vmem = pl.BlockSpec(memory_space=pltpu.VMEM)

    def body(c_ref, w_ref, b_ref, call_ref, mod_ref, part_ref, send_sems, recv_sems):
        x, y, c = _my_place()
        me = 4 * x + 2 * y + c
        flip = lambda v, f: 1 - v if f else v
        peers = [(flip(x, fx), flip(y, fy), flip(c, fc)) for fx in (0, 1) for fy in (0, 1) for fc in (0, 1)][1:]
        call_ref[me] = c_ref[...]
        first = [pltpu.make_async_remote_copy(src_ref=c_ref, dst_ref=call_ref.at[me], send_sem=send_sems.at[k],
                                              recv_sem=recv_sems.at[k], device_id=peer, device_id_type=MESH)
                 for k, peer in enumerate(peers)]
        for cp in first:
            cp.start()
        for cp in first:
            cp.wait()
        c_all = jnp.concatenate([call_ref[j] for j in range(8)], axis=0)
        part_ref[...] = _dot(c_all.astype(CDT), w_ref[...].astype(CDT), NN) + b_ref[...]
        mod_ref[me] = part_ref[pl.ds(me, 1), :]
        second = [pltpu.make_async_remote_copy(
            src_ref=part_ref.at[pl.ds(4 * px + 2 * py + pc, 1)], dst_ref=mod_ref.at[me], send_sem=send_sems.at[7 + k],
            recv_sem=recv_sems.at[7 + k], device_id=(px, py, pc), device_id_type=MESH)
            for k, (px, py, pc) in enumerate(peers)]
        for cp in second:
            cp.start()
        for cp in second:
            cp.wait()

    return pl.pallas_call(
        body, name="adaln_exchange", in_specs=[vmem] * 3, out_specs=[vmem] * 2,
        out_shape=[_sds((8, 1, D), F32), _sds((8, 1, w_ada.shape[1]), F32)],
        scratch_shapes=[pltpu.VMEM((8, w_ada.shape[1]), F32), pltpu.SemaphoreType.DMA((14,)),
                        pltpu.SemaphoreType.DMA((14,))],
        compiler_params=_cp(),
    )(c, w_ada, b_part)


def _pair_add(dw, recv, axis, width, cidx, name):
    if axis == 0:
        tr = min(width, 256)
        C = dw.shape[1]
        grid = (4, width // tr)
        nb = width // tr
        own = pl.BlockSpec((tr, C), lambda q, i, c_ref: ((2 * q + c_ref[0]) * nb + i, 0))
        oth = pl.BlockSpec((None, tr, C), lambda q, i, c_ref: (q, i, 0))
    else:
        R = dw.shape[0]
        tr = min(R, 1024)
        grid = (4, R // tr)
        own = pl.BlockSpec((tr, width), lambda q, i, c_ref: (i, 2 * q + c_ref[0]))
        oth = pl.BlockSpec((None, tr, width), lambda q, i, c_ref: (q, i, 0))

    def body(c_ref, a_ref, b_ref, o_ref):
        o_ref[...] = (a_ref[...] + b_ref[...].astype(F32)).astype(o_ref.dtype)

    return pl.pallas_call(
        body, name=name,
        grid_spec=pltpu.PrefetchScalarGridSpec(num_scalar_prefetch=1, grid=grid, in_specs=[own, oth], out_specs=oth),
        out_shape=_sds(recv.shape, CDT), compiler_params=_cp(("parallel", "parallel")),
    )(cidx, dw, recv)


def _add_pairs(xs, ys, name):
    n = len(xs)

    def body(*refs):
        for p in range(n):
            refs[2 * n + p][...] = refs[p][...] + refs[n + p][...]

    return pl.pallas_call(body, name=name, out_shape=[_sds(a.shape, F32) for a in xs])(*xs, *ys)


def _adam_math(w, g, m, v):
    m = ADAM_B1 * m + (1.0 - ADAM_B1) * g
    v = ADAM_B2 * v + (1.0 - ADAM_B2) * (g * g)
    m_hat = m / (1.0 - ADAM_B1 ** ADAM_STEP)
    v_hat = v / (1.0 - ADAM_B2 ** ADAM_STEP)
    delta = -ADAM_LR * (m_hat / (jnp.sqrt(v_hat) + ADAM_EPS) + ADAM_WD * w)
    return delta, m, v


def _adam(gparts, w, m, v, name):
    P, R, C = gparts.shape
    tr = R if R <= 256 else 256

    def body(g_ref, w_ref, m_ref, v_ref, go_ref, d_ref, mo_ref, vo_ref):
        g = g_ref[0].astype(F32)
        for i in range(1, P):
            g = g + g_ref[i].astype(F32)
        d, mn, vn = _adam_math(w_ref[...], g, m_ref[...], v_ref[...])
        go_ref[...] = g
        d_ref[...] = d
        mo_ref[...] = mn
        vo_ref[...] = vn

    t = pl.BlockSpec((tr, C), lambda i: (i, 0))
    return pl.pallas_call(
        body, name=name, grid=(R // tr,),
        in_specs=[pl.BlockSpec((P, tr, C), lambda i: (0, i, 0)), t, t, t], out_specs=[t, t, t, t],
        out_shape=[_sds((R, C), F32)] * 4, compiler_params=_cp(("parallel",)),
    )(gparts, w, m, v)


def _sum_parts(gparts, name):
    P, R, C = gparts.shape

    def body(g_ref, o_ref):
        g = g_ref[0]
        for i in range(1, P):
            g = g + g_ref[i]
        o_ref[...] = g

    return pl.pallas_call(body, name=name, out_shape=_sds((R, C), F32))(gparts)


def _adam_w_ada(c_t, dm, w, m, v):
    R, C = w.shape

    def body(c_ref, dm_ref, w_ref, m_ref, v_ref, go_ref, d_ref, mo_ref, vo_ref):
        g = c_ref[:, 0:1] * dm_ref[0:1, :]
        for b in range(1, 8):
            g = g + c_ref[:, b:b + 1] * dm_ref[b:b + 1, :]
        d, mn, vn = _adam_math(w_ref[...], g, m_ref[...], v_ref[...])
        go_ref[...] = g
        d_ref[...] = d
        mo_ref[...] = mn
        vo_ref[...] = vn

    return pl.pallas_call(body, name="adam_w_ada", out_shape=[_sds((R, C), F32)] * 4,
                          compiler_params=_cp())(c_t, dm, w, m, v)


def kernel(x, c, norm_g, w_ada, b_ada, w_in, pool_w, pool_scale, w_attn_br, w_pool_br, w_out, rel_bias, final_g, loss_target, m_norm_g, m_w_ada, m_b_ada, m_w_in, m_pool_w, m_pool_scale, m_w_attn_br, m_w_pool_br, m_w_out, m_rel_bias, m_final_g, v_norm_g, v_w_ada, v_b_ada, v_w_in, v_pool_w, v_pool_scale, v_w_attn_br, v_w_pool_br, v_w_out, v_rel_bias, v_final_g):
    S = x.shape[1]
    px, py, pc = _my_place()
    me = 4 * px + 2 * py + pc
    cidx = jnp.reshape(pc, (1,)).astype(jnp.int32)

    layout = [(1, 1024), (1, 128), (1, 128), (0, 128)]
    final_g2 = final_g.reshape(1, D)
    me1 = jnp.reshape(me, (1,)).astype(jnp.int32)

    b_part = lax.dynamic_slice(b_ada, (0, me * 384), (1, 384))
    c_all, mod_all = _adaln_exchange(c, w_ada[0], b_part)
    c_all = c_all.reshape(8, D)
    mod = mod_all.reshape(1, 3 * D)
    shift, scale, gate = mod[:, :D], mod[:, D:2 * D], mod[:, 2 * D:]

    gplan = _gather_chips_plan([(None, 1)])
    lands = _place_slot([w_in[0], w_attn_br[0], w_pool_br[0], w_out[0]], me1, 8, "place_own_shards", CDT)
    g_sems, _, w_land, tok = _split_start("gather_chips_start", [], lands[:1], 2, gplan, dep=mod_all)
    st = _local_pre(x[0], shift, scale, norm_g, rel_bias, dep=tok)
    _, w_land = _split_wait("gather_chips_wait", g_sems, [], w_land, gplan,
                            [st["ht"]] + [b for pair in st["biases"] for b in pair])
    (w_in_full,) = _pair_share(w_land)

    b_sems, _, b_land, tok = _split_start("gather_branch_start", [], lands[1:], 21, _to_all_plan, dep=w_in_full)
    st["proj_after"] = tok

    def branch_weights(after):
        _, (wab8, wpb8, wout8) = _split_wait("gather_branch_wait", b_sems, [], b_land, _to_all_plan, after)
        return (_cols_from_blocks(wab8, "w_attn_br_cols"), _cols_from_blocks(wpb8, "w_pool_br_cols"),
                wout8.reshape(D, D))

    st = _local_grads(st, loss_target[0], gate, w_in_full, pool_w[0], pool_scale, branch_weights, final_g2)

    early = [st["dpw"].reshape(AW, 128), st["dps"], st["gfg"], jnp.broadcast_to(st["loss"], (1, 128))]
    dws = [st["dw_in"], st["dw_ab"], st["dw_pb"], st["dw_out"]]
    sent = [st["dw_in_low"]] + dws[1:] + early
    pplan = _pair_plan(layout)
    pair_shapes = [(4, w, a.shape[1]) if ax == 0 else (4, a.shape[0], w) for a, (ax, w) in zip(dws, layout)]
    p_sems, sent, recv, tok = _split_start(
        "reduce_pair_start", sent,
        [lax.empty(s, a.dtype) for s, a in zip(pair_shapes + [a.shape for a in early], sent)], 20, pplan)
    d_rel_bias = _local_bias_grads(st, _BIAS_GRADS_EARLY, dep=tok)
    sent, recv = _split_wait("reduce_pair_wait", p_sems, sent, recv, pplan, d_rel_bias)
    parts = [_pair_add(dw, r, ax, w, cidx, f"pair_add{i}")
             for i, (dw, r, (ax, w)) in enumerate(zip([dws[0]] + sent[1:4], recv[:4], layout))]
    parts += _add_pairs(sent[4:], recv[4:], "pair_add_smalls")

    slot1 = jnp.reshape(2 * px + py, (1,)).astype(jnp.int32)
    cplan = _chips_plan(4)
    c_sems, parts, lands, tok = _split_start("reduce_chips_start", parts,
                                             _place_slot(parts, slot1, 4, "place_own_partials"), 24, cplan)
    d_rel_bias = d_rel_bias + _local_bias_grads(st, _BIAS_GRADS_LATE, dep=tok)
    grad_x, dmod, d_norm_g = _local_input_grads(st, w_in_full, norm_g, scale, dep=tok)

    dmod_all, ng_all, rb_all = _all_gather_small([dmod, d_norm_g, d_rel_bias], "gather_late")
    o_norm_g = _adam(ng_all, norm_g, m_norm_g, v_norm_g, "adam_norm_g")
    o_rel_bias = _adam(rb_all, rel_bias, m_rel_bias, v_rel_bias, "adam_rel_bias")
    o_b_ada = _adam(dmod_all, b_ada, m_b_ada, v_b_ada, "adam_b_ada")
    dm_mine = lax.dynamic_slice(dmod_all.reshape(8, 3 * D), (0, me * 384), (8, 384))
    o_w_ada = [o[None] for o in _adam_w_ada(c_all.T, dm_mine, w_ada[0], m_w_ada[0], v_w_ada[0])]

    _, (g_in, g_ab, g_pb, g_out, g_pw, g_ps, g_fg, g_loss) = _split_wait(
        "reduce_chips_wait", c_sems, parts, lands, cplan, [o_w_ada[0], o_b_ada[0], o_norm_g[0], o_rel_bias[0]])
    o_w_in = [o[None] for o in _adam(g_in, w_in[0], m_w_in[0], v_w_in[0], "adam_w_in")]
    o_w_ab = [o[None] for o in _adam(g_ab, w_attn_br[0], m_w_attn_br[0], v_w_attn_br[0], "adam_w_attn_br")]
    o_w_pb = [o[None] for o in _adam(g_pb, w_pool_br[0], m_w_pool_br[0], v_w_pool_br[0], "adam_w_pool_br")]
    o_w_out = [o[None] for o in _adam(g_out, w_out[0], m_w_out[0], v_w_out[0], "adam_w_out")]
    o_pool_w = [o.reshape(pool_w.shape) for o in _adam(g_pw, pool_w.reshape(AW, 128), m_pool_w.reshape(AW, 128),
                                                       v_pool_w.reshape(AW, 128), "adam_pool_w")]
    o_pool_scale = _adam(g_ps, pool_scale, m_pool_scale, v_pool_scale, "adam_pool_scale")
    o_final_g = [o.reshape(D) for o in _adam(g_fg, final_g2, m_final_g.reshape(1, D), v_final_g.reshape(1, D),
                                             "adam_final_g")]
    loss = _sum_parts(g_loss, "sum_loss")[0, 0]

    per_w = [o_norm_g, o_w_ada, o_b_ada, o_w_in, o_pool_w, o_pool_scale, o_w_ab, o_w_pb, o_w_out, o_rel_bias, o_final_g]
    outs = [loss, grad_x[None]]
    for k in range(4):
        outs += [o[k] for o in per_w]
    return tuple(outs)
```

```python
import functools
import math

import numpy as np
import jax
import jax.numpy as jnp
from jax import lax
from jax.experimental import pallas as pl
from jax.experimental.pallas import tpu as pltpu

F32 = jnp.float32
CDT = jnp.bfloat16
D = 1024
HD = 64
NH = 8
AW = 512
QKV_W = 4608
REST_W = 3584
R16 = 16
EPS = 1e-6
NEG = -1e30
N_BACK = 128
ADAM_LR, ADAM_B1, ADAM_B2, ADAM_EPS, ADAM_WD, ADAM_STEP = 0.001, 0.9, 0.999, 1e-08, 0.01, 10
MESH = pl.DeviceIdType.MESH
VMEM_MB = 1024 * 1024


def _cp(sem=None, vmem=48, **kw):
    if sem is not None:
        kw["dimension_semantics"] = sem
    return pltpu.CompilerParams(vmem_limit_bytes=vmem * VMEM_MB, **kw)


def _sds(shape, dtype):
    return jax.ShapeDtypeStruct(tuple(shape), dtype)


def _in_hbm(shape, dtype):
    return pltpu.HBM(tuple(shape), dtype)


def _pin(a):
    return pltpu.with_memory_space_constraint(a, pltpu.HBM)


def _rest_blk(jj):
    return jnp.where(jj == 0, 9, jnp.where(jj == 6, 10, jj + 10))


def _sigmoid(z):
    return 0.5 * jnp.tanh(0.5 * z) + 0.5


def _dot(a, b, dims):
    return lax.dot_general(a, b, (dims, ((), ())), preferred_element_type=F32)


NN = ((1,), (0,))
NT = ((1,), (1,))
TN = ((0,), (0,))


def _ld(ref):
    if len(ref.shape) == 2:
        return ref[...]
    return jnp.concatenate([ref[a] for a in range(ref.shape[0])], axis=0)


def _st(ref, val, cols=None):
    val = val.astype(ref.dtype)
    if len(ref.shape) == 2:
        if cols is None:
            ref[...] = val
        else:
            ref[:, cols] = val
        return
    rows = ref.shape[1]
    for a in range(ref.shape[0]):
        if cols is None:
            ref[a] = val[a * rows:(a + 1) * rows]
        else:
            ref[a, :, cols] = val[a * rows:(a + 1) * rows]


REST_COLS = ((9, 10), (11, 16), (10, 11))
_HBM = pl.BlockSpec(memory_space=pltpu.HBM)


def _resident(shape):
    return pl.BlockSpec(shape, lambda *ids: (0,) * len(shape), pipeline_mode=pl.Buffered(1))


def _w512(w_ref, j):
    return w_ref[j // 2, :, 512 * (j % 2):512 * (j % 2 + 1)]


def _proj(h, w):
    S = h.shape[0]
    tm = 512

    def body(a_ref, w_ref, q_ref, r_ref):
        a = a_ref[...]
        for j in range(9):
            q_ref[:, 512 * j:512 * (j + 1)] = _dot(a, _w512(w_ref, j), NN).astype(q_ref.dtype)
        off = 0
        for lo, hi in REST_COLS:
            for j in range(lo, hi):
                r_ref[:, off:off + 512] = _dot(a, _w512(w_ref, j), NN)
                off += 512

    return pl.pallas_call(
        body, name="proj", grid=(S // tm,),
        in_specs=[pl.BlockSpec((tm, D), lambda i: (i, 0)), _resident((8, D, 1024))],
        out_specs=[pl.BlockSpec((tm, QKV_W), lambda i: (i, 0)), pl.BlockSpec((tm, REST_W), lambda i: (i, 0))],
        out_shape=[_sds((S, QKV_W), CDT), _sds((S, REST_W), F32)],
        compiler_params=_cp(("parallel",), vmem=56),
    )(h, w)


def _after(dep):
    return ([], []) if dep is None else ([pl.BlockSpec(memory_space=pl.ANY)], [dep])


def _dh(dqkv, drest, w, dep=None):
    S = dqkv.shape[0]
    tm = 512
    dep_specs, dep_args = _after(dep)

    def body(a_ref, b_ref, w_ref, *rest):
        o_ref = rest[-1]
        acc = _dot(a_ref[:, 0:512], _w512(w_ref, 0), NT)
        for j in range(1, 9):
            acc = acc + _dot(a_ref[:, 512 * j:512 * (j + 1)], _w512(w_ref, j), NT)
        off = 0
        for lo, hi in REST_COLS:
            for j in range(lo, hi):
                acc = acc + _dot(b_ref[:, off:off + 512], _w512(w_ref, j), NT)
                off += 512
        o_ref[...] = acc

    return pl.pallas_call(
        body, name="dh", grid=(S // tm,),
        in_specs=[pl.BlockSpec((tm, QKV_W), lambda i: (i, 0)), pl.BlockSpec((tm, REST_W), lambda i: (i, 0)),
                  _resident((8, D, 1024))] + dep_specs,
        out_specs=pl.BlockSpec((tm, D), lambda i: (i, 0)),
        out_shape=_sds((S, D), F32),
        compiler_params=_cp(("parallel",), vmem=56),
    )(dqkv, drest, w, *dep_args)


def _mm_tn(at, b, *, n_total, out_blk, name, prev=None, tn=512, with_low=False):
    R, S = at.shape
    nb = b.shape[1] // tn
    n_out = 2 if with_low else 1

    def body(*refs):
        res = _dot(refs[0][...], refs[1][...], NN)
        refs[-n_out][...] = res
        if with_low:
            refs[-1][...] = res.astype(CDT)

    in_specs = [_resident((R, S)), pl.BlockSpec((S, tn), lambda j: (0, j))]
    args = [at, b]
    aliases = {}
    if prev is not None:
        prev = list(prev) if with_low else [prev]
        in_specs += [pl.BlockSpec(memory_space=pl.ANY)] * n_out
        args += prev
        aliases = {2 + k: k for k in range(n_out)}
    out_spec = pl.BlockSpec((R, tn), lambda j: (0, out_blk(j)))
    outs = pl.pallas_call(
        body, name=name, grid=(nb,), in_specs=in_specs,
        out_specs=[out_spec] * n_out,
        out_shape=[_sds((R, n_total), F32), _sds((R, n_total), CDT)][:n_out],
        input_output_aliases=aliases,
        compiler_params=_cp(("parallel",)),
    )(*args)
    return tuple(outs) if with_low else outs[0]


def _tcast(a, name):
    M, C = a.shape
    tm = 512

    def body(a_ref, o_ref):
        o_ref[...] = a_ref[...].astype(F32).T.astype(o_ref.dtype)

    return pl.pallas_call(
        body, name=name, grid=(M // tm,),
        in_specs=[pl.BlockSpec((tm, C), lambda i: (i, 0))],
        out_specs=pl.BlockSpec((C, tm), lambda i: (0, i)),
        out_shape=_sds((C, M), CDT), compiler_params=_cp(("parallel",)),
    )(a)


TOK = R16 * R16


def _swap16():
    i = lax.broadcasted_iota(jnp.int32, (TOK, TOK), 0)
    j = lax.broadcasted_iota(jnp.int32, (TOK, TOK), 1)
    return jnp.where(j == jnp.bitwise_and(i, 15) * 16 + jnp.right_shift(i, 4), 1.0, 0.0).astype(CDT)


def _norm_mod(x, norm_g, shift, scale, dep=None):
    S = x.shape[0]
    dep_specs, dep_args = _after(dep)

    def body(x_ref, g_ref, sh_ref, sc_ref, *rest):
        xv = x_ref[...]
        xn = xv * lax.rsqrt(jnp.mean(xv * xv, axis=-1, keepdims=True) + EPS)
        h = (xn * (g_ref[...] * (1.0 + sc_ref[...])) + sh_ref[...]).astype(CDT)
        _st(rest[-1], _dot(_swap16(), h, NN))

    vec = pl.BlockSpec((1, D), lambda i: (0, 0))
    return pl.pallas_call(
        body, name="norm_mod", grid=(S // TOK,),
        in_specs=[pl.BlockSpec((TOK, D), lambda i: (i, 0)), vec, vec, vec] + dep_specs,
        out_specs=pl.BlockSpec((R16, R16, D), lambda i: (0, i, 0)),
        out_shape=_sds((R16, S // R16, D), CDT), compiler_params=_cp(("parallel",)),
    )(x, norm_g, shift, scale, *dep_args)


def _t5_bucket_np(n):
    nf = np.maximum(n, 1).astype(np.float32)
    large = 16 + (np.log(nf / np.float32(16)) / np.float32(math.log(2048 / 16)) * np.float32(16)).astype(np.int32)
    large = np.minimum(large, 31)
    return np.where(n < 16, n, large).astype(np.int32)


class _Geom:
    def __init__(self, gi, LS):
        self.gi, self.LS = gi, LS
        self.dil = (1, 4, 16)[gi]
        if gi == 0:
            self.lead, self.rows, self.n_tiles, self.has_prev = (R16,), 16, LS // 16, True
            a = np.arange(R16)[:, None]
            ll = np.arange(16)[None, :]
            self.pos = (16 * ll + a).reshape(-1)
        elif gi == 1:
            self.lead, self.rows, self.n_tiles, self.has_prev = (4,), 64, LS // 64, True
            a = np.arange(4)[:, None]
            ll = np.arange(64)[None, :]
            self.pos = (4 * ll + a).reshape(-1)
        else:
            self.lead, self.rows, self.n_tiles, self.has_prev = (), LS, 1, False
            self.pos = np.arange(LS)
        self.T = self.pos.shape[0]
        self.Tk = 2 * self.T if self.has_prev else self.T
        self.outer = {0: (), 1: (4,), 2: (R16,)}[gi]

    def view(self, arr):
        S, C = arr.shape
        if self.gi == 1:
            return arr.reshape(4, 4, self.LS, C)
        return arr.reshape(R16, self.LS, C)

    def view_shape(self, C):
        return (4, 4, self.LS, C) if self.gi == 1 else (R16, self.LS, C)

    def unview(self, arr):
        return arr.reshape(R16 * self.LS, arr.shape[-1])

    def spec(self, W, cblk, tile):
        if self.gi == 0:
            return pl.BlockSpec((R16, 16, W), lambda n: (0, tile(n), cblk))
        if self.gi == 1:
            return pl.BlockSpec((4, None, 64, W), lambda r, n: (0, r, tile(n), cblk))
        return pl.BlockSpec((None, self.LS, W), lambda r: (r, 0, cblk))

    def halves(self):
        R = self.rows
        idx = np.arange(self.T).reshape(-1, R)
        return idx[:, :R // 2].reshape(-1), idx[:, R // 2:].reshape(-1)

    def bucket_index(self, sub):
        lo, hi = self.halves()
        pos = self.pos
        if sub == 0:
            pq = pos[lo]
            pk = np.concatenate([pos[hi] - self.T, pos[lo]]) if self.has_prev else np.concatenate([pos[lo], pos[hi]])
        else:
            pq = pos[hi]
            pk = np.concatenate([pos[lo], pos[hi]])
        dist = pq[:, None] - pk[None, :]
        ok = (dist >= 0) & (dist <= N_BACK)
        bucket = _t5_bucket_np(np.clip(dist, 0, N_BACK) * self.dil)
        return np.where(ok, bucket, -1).astype(np.int32)


def _bias_table(idx, buckets, rel_bias, gi, sub, dep=None):
    T, Tk = idx.shape
    tr = 32
    dep_specs, dep_args = _after(dep)

    def body(idx_ref, rb_ref, *rest):
        h = pl.program_id(0)
        ix = idx_ref[...]
        acc = jnp.full(ix.shape, NEG, F32)
        for b in buckets:
            acc = jnp.where(ix == b, rb_ref[b, gi * NH + h], acc)
        rest[-1][...] = acc

    return pl.pallas_call(
        body, name=f"bias_table{gi}{'ab'[sub]}", grid=(NH, T // tr),
        in_specs=[pl.BlockSpec((tr, Tk), lambda h, i: (i, 0)), pl.BlockSpec(memory_space=pltpu.SMEM)] + dep_specs,
        out_specs=pl.BlockSpec((None, tr, Tk), lambda h, i: (h, i, 0)),
        out_shape=_sds((NH, T, Tk), F32), compiler_params=_cp(("parallel", "parallel")),
    )(idx, rel_bias, *dep_args)


def _bias_grad(dsum, idx, buckets, gi, sub, dep=None):
    _, T, Tk = dsum.shape
    dep_specs, dep_args = _after(dep)

    def body(ds_ref, idx_ref, *rest):
        o_ref, r_ref = rest[-2], rest[-1]
        r_ref[...] = jnp.zeros_like(r_ref)
        ix, ds = idx_ref[...], ds_ref[...]
        for b in buckets:
            r_ref[b:b + 1, :] = jnp.sum(jnp.where(ix == b, ds, 0.0), axis=0, keepdims=True)
        o_ref[...] = jnp.broadcast_to(jnp.sum(r_ref[...], axis=1, keepdims=True), (32, 128))

    out = pl.pallas_call(
        body, name=f"bias_grad{gi}{'ab'[sub]}", grid=(NH,),
        in_specs=[pl.BlockSpec((None, T, Tk), lambda h: (h, 0, 0)), pl.BlockSpec((T, Tk), lambda h: (0, 0))] + dep_specs,
        out_specs=pl.BlockSpec((None, 32, 128), lambda h: (h, 0, 0)),
        out_shape=_sds((NH, 32, 128), F32), scratch_shapes=[pltpu.VMEM((32, Tk), F32)],
        compiler_params=_cp(("parallel",)),
    )(dsum, idx, *dep_args)
    return out[:, :, 0].T


def _prev_mask(n, T, Tk):
    col = lax.broadcasted_iota(jnp.int32, (1, Tk), 1)
    return jnp.where((col < T) & (n == 0), NEG, 0.0).astype(F32)


def _split_rows(v, R):
    A, half = v.shape[0] // R, R // 2
    via_f32 = v.dtype != F32 and half % 16 != 0
    w = v.astype(F32) if via_f32 else v
    lo = jnp.concatenate([w[a * R:a * R + half] for a in range(A)], axis=0)
    hi = jnp.concatenate([w[a * R + half:(a + 1) * R] for a in range(A)], axis=0)
    return (lo.astype(v.dtype), hi.astype(v.dtype)) if via_f32 else (lo, hi)


def _merge_rows(lo, hi, R):
    half = R // 2
    A = lo.shape[0] // half
    return jnp.concatenate([x[a * half:(a + 1) * half] for a in range(A) for x in (lo, hi)], axis=0)


def _attn_fwd(qkv, biases, geom, others=None):
    S = qkv.shape[0]
    gi, T, R, has_prev = geom.gi, geom.T, geom.rows, geom.has_prev
    H2 = T // 2
    qv = geom.view(qkv)
    cq, ck, cv = 3 * gi, 3 * gi + 1, 3 * gi + 2
    assert others is None or not has_prev

    def body(*refs):
        if has_prev:
            q_ref, kp_ref, kc_ref, vp_ref, vc_ref, ba_ref, bb_ref, o_ref, l_ref = refs
        elif others is not None:
            (q_ref, kc_ref, vc_ref, ba_ref, bb_ref, o0_ref, o1_ref, l0_ref, l1_ref, z_ref,
             attn_ref, L_ref, ag_ref) = refs
        else:
            q_ref, kc_ref, vc_ref, ba_ref, bb_ref, o_ref, l_ref = refs
        q_lo, q_hi = _split_rows(_ld(q_ref), R)
        kc_lo, kc_hi = _split_rows(_ld(kc_ref), R)
        vc_lo, vc_hi = _split_rows(_ld(vc_ref), R)
        k_b = jnp.concatenate([kc_lo, kc_hi], axis=0)
        v_b = jnp.concatenate([vc_lo, vc_hi], axis=0)
        if has_prev:
            n = pl.program_id(len(geom.outer))
            k_a = jnp.concatenate([_split_rows(_ld(kp_ref), R)[1], kc_lo], axis=0)
            v_a = jnp.concatenate([_split_rows(_ld(vp_ref), R)[1], vc_lo], axis=0)
            pm = _prev_mask(n, H2, T)
        else:
            k_a, v_a, pm = k_b, v_b, None
        low = lax.broadcasted_iota(jnp.int32, (1, 2 * HD), 1) < HD
        for j in range(NH // 2):
            cols = slice(2 * HD * j, 2 * HD * (j + 1))
            o_half, l_half = [], []
            for q, k, v, b_ref, mask in ((q_lo, k_a, v_a, ba_ref, pm), (q_hi, k_b, v_b, bb_ref, None)):
                q2 = q[:, cols].astype(F32) * 0.125
                k2, v2 = k[:, cols], v[:, cols]
                qs = jnp.concatenate([jnp.where(low, q2, 0.0), jnp.where(low, 0.0, q2)], axis=0).astype(q.dtype)
                s = _dot(qs, k2, NT) + b_ref[j]
                if mask is not None:
                    s = s + mask
                m = jnp.max(s, axis=1, keepdims=True)
                p = jnp.exp(s - m)
                l = jnp.sum(p, axis=1, keepdims=True)
                pv = _dot(p.astype(v.dtype), v2, NN) / l
                lse = m + jnp.log(l)
                o_half.append(jnp.where(low, pv[:H2], pv[H2:]))
                l_half.append(jnp.where(low, lse[:H2], lse[H2:]))
            o2, lc = _merge_rows(o_half[0], o_half[1], R), _merge_rows(l_half[0], l_half[1], R)
            if others is None:
                _st(o_ref, o2, cols)
                _st(l_ref, lc, cols)
                continue
            la, lb = l0_ref[:, cols], l1_ref[:, cols]
            m = jnp.maximum(jnp.maximum(la, lb), lc)
            ea, eb, ec = jnp.exp(la - m), jnp.exp(lb - m), jnp.exp(lc - m)
            den = ea + eb + ec
            attn = (ea * o0_ref[:, cols] + eb * o1_ref[:, cols] + ec * o2) / den
            attn_ref[:, cols] = attn
            L_ref[:, cols] = m + jnp.log(den)
            z = z_ref[:, cols]
            ag = attn * (z * _sigmoid(z))
            ag_ref[:, cols] = ag.astype(ag_ref.dtype)

    cur = lambda n: n
    prev = lambda n: jnp.maximum(n - 1, 0)
    if has_prev:
        in_specs = [geom.spec(AW, cq, cur), geom.spec(AW, ck, prev), geom.spec(AW, ck, cur),
                    geom.spec(AW, cv, prev), geom.spec(AW, cv, cur)]
        args = [qv] * 5
    else:
        in_specs = [geom.spec(AW, cq, cur), geom.spec(AW, ck, cur), geom.spec(AW, cv, cur)]
        args = [qv] * 3
    nd = len(geom.outer) + (1 if has_prev else 0)
    biases = [b.reshape(NH // 2, 2 * H2, b.shape[2]) for b in biases]
    in_specs += [pl.BlockSpec(b.shape, lambda *ids: (0, 0, 0)) for b in biases]
    grid = geom.outer + ((geom.n_tiles,) if has_prev else ())
    oshape = geom.view_shape(AW)
    tile = geom.spec(AW, 0, cur)
    if others is None:
        o, l = pl.pallas_call(
            body, name=f"attn_fwd{gi}", grid=grid, in_specs=in_specs, out_specs=[tile, tile],
            out_shape=[_sds(oshape, F32), _sds(oshape, F32)],
            compiler_params=_cp(("arbitrary",) * nd),
        )(*args, *biases)
        return geom.unview(o), geom.unview(l)
    os_, ls_, rest = others
    attn, Lb, ag = pl.pallas_call(
        body, name=f"attn_fwd{gi}_merge", grid=grid, in_specs=in_specs + [tile] * 5,
        out_specs=[tile, tile, tile],
        out_shape=[_sds(oshape, F32), _sds(oshape, F32), _sds(oshape, CDT)],
        compiler_params=_cp(("arbitrary",) * nd),
    )(*args, *biases, *[geom.view(a) for a in (*os_, *ls_, rest)])
    return geom.unview(attn), geom.unview(Lb), geom.unview(ag)


def _pool_counts(r, LS, win):
    l = lax.broadcasted_iota(jnp.int32, (LS, 1), 0)
    return jnp.minimum(16 * l + (r + 1), win).astype(F32)


def _pool_fwd(rest, pool_w, pool_scale):
    S = rest.shape[0]
    LS = S // R16
    r3 = rest.reshape(R16, LS, REST_W)
    PG = 128

    def body(u_ref, z_ref, pw_ref, ps_ref, pt_ref, mp_ref, pg_ref, e_ref):
        g = pl.program_id(0)
        win = jnp.left_shift(2, g)
        row = lax.broadcasted_iota(jnp.int32, (LS, PG), 0)
        for r in range(R16):
            e_ref[R16 + r] = u_ref[r]
        for r in range(1, R16):
            e_ref[r] = jnp.where(row == 0, 0.0, pltpu.roll(u_ref[r], 1, 0))

        def double(s):
            for i in range(2 * R16 - 1, 2 * s - 1, -1):
                e_ref[i] = e_ref[i] + e_ref[i - s]

        double(1)
        for k in (1, 2, 3):
            pl.when(g >= k)(functools.partial(double, 2 ** k))
        pw = pw_ref[...].astype(CDT)
        for r in range(R16):
            u = u_ref[r]
            pooled = e_ref[R16 + r] / _pool_counts(r, LS, win) - u
            pc = pooled.astype(CDT)
            mp = _dot(pc, pw, NN)
            z = z_ref[r]
            pg = (mp * ps_ref[...]) * (z * _sigmoid(z))
            mp_ref[r] = mp
            pg_ref[r] = pg.astype(pg_ref.dtype)
            pt_ref[:, r * LS:(r + 1) * LS] = pooled.T.astype(pt_ref.dtype)

    col = lambda off: pl.BlockSpec((R16, LS, PG), lambda g: (0, 0, off + g))
    tsp = pl.BlockSpec((PG, S), lambda g: (g, 0))
    pt, mp, pg = pl.pallas_call(
        body, name="pool_fwd", grid=(4,),
        in_specs=[col(24), col(4), pl.BlockSpec((None, PG, PG), lambda g: (g, 0, 0)),
                  pl.BlockSpec((1, PG), lambda g: (0, g))],
        out_specs=[tsp, col(0), col(0)],
        out_shape=[_sds((AW, S), CDT), _sds((R16, LS, AW), F32), _sds((R16, LS, AW), CDT)],
        scratch_shapes=[pltpu.VMEM((2 * R16, LS, PG), F32)],
        compiler_params=_cp(("parallel",)),
    )(r3, r3, pool_w, pool_scale)
    return pt, mp.reshape(S, AW), pg.reshape(S, AW)


def _branch_merge(ag, pg, wab, wpb, rest):
    S = ag.shape[0]
    tm, tn = 512, 512

    def body(ag_ref, pg_ref, wa_ref, wp_ref, ga_ref, gp_ref, m_ref, mt_ref):
        ya = _dot(ag_ref[...], wa_ref[...], NN)
        yp = _dot(pg_ref[...], wp_ref[...], NN)
        mg = _sigmoid(ga_ref[...]) * ya + _sigmoid(gp_ref[...]) * yp
        m_ref[...] = mg.astype(m_ref.dtype)
        mt_ref[...] = mg.T.astype(mt_ref.dtype)

    a = pl.BlockSpec((tm, AW), lambda i, j: (i, 0))
    w = pl.BlockSpec((AW, tn), lambda i, j: (0, j))
    o = pl.BlockSpec((tm, tn), lambda i, j: (i, j))
    return pl.pallas_call(
        body, name="branch_merge", grid=(S // tm, D // tn),
        in_specs=[a, a, w, w, pl.BlockSpec((tm, tn), lambda i, j: (i, 2 + j)),
                  pl.BlockSpec((tm, tn), lambda i, j: (i, 4 + j))],
        out_specs=[o, pl.BlockSpec((tn, tm), lambda i, j: (j, i))],
        out_shape=[_sds((S, D), CDT), _sds((D, S), CDT)],
        compiler_params=_cp(("parallel", "parallel")),
    )(ag, pg, wab, wpb, rest, rest)


def _merge_out_loss(ag, pg, rest, wab, wpb, wout, x, target, gate, final_g):
    S = x.shape[0]
    LS = S // R16

    def body(ag_ref, pg_ref, ga_ref, gp_ref, wa_ref, wp_ref, w_ref, x_ref, t_ref, gate_ref, fg_ref,
             dx2_ref, dmo_ref, loss_ref, gfg_ref, dgate_ref, dw_ref):
        i = pl.program_id(0)

        @pl.when(i == 0)
        def _():
            loss_ref[...] = jnp.zeros_like(loss_ref)
            gfg_ref[...] = jnp.zeros_like(gfg_ref)
            dgate_ref[...] = jnp.zeros_like(dgate_ref)
            dw_ref[...] = jnp.zeros_like(dw_ref)

        swap = _swap16()
        ya = _dot(_ld(ag_ref), wa_ref[...], NN)
        yp = _dot(_ld(pg_ref), wp_ref[...], NN)
        mg = (_sigmoid(_ld(ga_ref)) * ya + _sigmoid(_ld(gp_ref)) * yp).astype(CDT)
        m_nat = _dot(swap, mg, NN)
        mo = _dot(m_nat.astype(CDT), w_ref[...], NN)
        gate, fg = gate_ref[...], fg_ref[...]
        x2 = x_ref[...] + gate * mo
        r2 = lax.rsqrt(jnp.mean(x2 * x2, axis=-1, keepdims=True) + EPS)
        xn2 = x2 * r2
        diff = xn2 * fg - t_ref[...]
        loss = jnp.sum(jnp.sum(diff * diff, axis=-1, keepdims=True), axis=0, keepdims=True)
        dy = diff * (1.0 / D)
        dxn2 = dy * fg
        dx2 = r2 * (dxn2 - xn2 * jnp.mean(dxn2 * xn2, axis=-1, keepdims=True))
        dx2_ref[...] = dx2
        dmo = (dx2 * gate).astype(CDT)
        _st(dmo_ref, _dot(swap, dmo, NN))
        loss_ref[...] += jnp.broadcast_to(loss * (0.5 / D), loss_ref.shape)
        gfg_ref[...] += jnp.sum(dy * xn2, axis=0, keepdims=True)
        dgate_ref[...] += jnp.sum(dx2 * mo, axis=0, keepdims=True)
        dw_ref[...] += _dot(m_nat.T.astype(CDT), dmo, NN)

    def slab(w, cblk):
        return pl.BlockSpec((R16, R16, w), lambda i: (0, i, cblk))

    nat = pl.BlockSpec((TOK, D), lambda i: (i, 0))
    vec = pl.BlockSpec((1, D), lambda i: (0, 0))
    r3 = rest.reshape(R16, LS, REST_W)
    return pl.pallas_call(
        body, name="merge_out_loss", grid=(S // TOK,),
        in_specs=[slab(AW, 0), slab(AW, 0), slab(D, 1), slab(D, 2), _resident((AW, D)), _resident((AW, D)),
                  _resident((D, D)), nat, nat, vec, vec],
        out_specs=[nat, slab(D, 0), pl.BlockSpec((1, 128), lambda i: (0, 0)), vec, vec,
                   pl.BlockSpec((D, D), lambda i: (0, 0))],
        out_shape=[_sds((S, D), F32), _sds((R16, LS, D), CDT), _sds((1, 128), F32), _sds((1, D), F32),
                   _sds((1, D), F32), _sds((D, D), F32)],
        compiler_params=_cp(("arbitrary",)),
    )(ag.reshape(R16, LS, AW), pg.reshape(R16, LS, AW), r3, r3, wab, wpb, wout, x, target, gate, final_g)


def _out_loss(merged, wout, x, target, gate, final_g):
    S = x.shape[0]

    def body(m_ref, w_ref, x_ref, t_ref, gate_ref, fg_ref, dx2_ref, dmo_ref, loss_ref, gfg_ref, dgate_ref):
        i = pl.program_id(0)

        @pl.when(i == 0)
        def _():
            loss_ref[...] = jnp.zeros_like(loss_ref)
            gfg_ref[...] = jnp.zeros_like(gfg_ref)
            dgate_ref[...] = jnp.zeros_like(dgate_ref)

        swap = _swap16()
        mo = _dot(_dot(swap, _ld(m_ref), NN).astype(CDT), w_ref[...], NN)
        gate, fg = gate_ref[...], fg_ref[...]
        x2 = x_ref[...] + gate * mo
        r2 = lax.rsqrt(jnp.mean(x2 * x2, axis=-1, keepdims=True) + EPS)
        xn2 = x2 * r2
        diff = xn2 * fg - t_ref[...]
        loss = jnp.sum(jnp.sum(diff * diff, axis=-1, keepdims=True), axis=0, keepdims=True)
        dy = diff * (1.0 / D)
        dxn2 = dy * fg
        dx2 = r2 * (dxn2 - xn2 * jnp.mean(dxn2 * xn2, axis=-1, keepdims=True))
        dx2_ref[...] = dx2
        _st(dmo_ref, _dot(swap, (dx2 * gate).astype(CDT), NN))
        loss_ref[...] += jnp.broadcast_to(loss * (0.5 / D), loss_ref.shape)
        gfg_ref[...] += jnp.sum(dy * xn2, axis=0, keepdims=True)
        dgate_ref[...] += jnp.sum(dx2 * mo, axis=0, keepdims=True)

    slab = pl.BlockSpec((R16, R16, D), lambda i: (0, i, 0))
    nat = pl.BlockSpec((TOK, D), lambda i: (i, 0))
    vec = pl.BlockSpec((1, D), lambda i: (0, 0))
    return pl.pallas_call(
        body, name="out_loss", grid=(S // TOK,),
        in_specs=[slab, _resident((D, D)), nat, nat, vec, vec],
        out_specs=[nat, slab, pl.BlockSpec((1, 128), lambda i: (0, 0)), vec, vec],
        out_shape=[_sds((S, D), F32), _sds((R16, S // R16, D), CDT), _sds((1, 128), F32), _sds((1, D), F32),
                   _sds((1, D), F32)],
        compiler_params=_cp(("arbitrary",)),
    )(merged, wout, x, target, gate, final_g)


def _bwd_gates(dmo, wout, wab, wpb, rest, ag, pg, attn, mp, pool_scale):
    S = dmo.shape[0]
    tm = 256

    def body(dmo_ref, wo_ref, wa_ref, wp_ref, za_ref, zp_ref, ga0, ga1, gp0, gp1, ag_ref, pg_ref, at_ref, mp_ref,
             ps_ref, dwa_ref, dwp_ref, dr_ref, dat_ref, dl_ref, dmp_ref, dps_ref):
        i = pl.program_id(0)

        @pl.when(i == 0)
        def _():
            dps_ref[...] = jnp.zeros_like(dps_ref)
            dwa_ref[...] = jnp.zeros_like(dwa_ref)
            dwp_ref[...] = jnp.zeros_like(dwp_ref)

        dm = _dot(dmo_ref[...], wo_ref[...], NT)
        sa = _sigmoid(jnp.concatenate([ga0[...], ga1[...]], axis=1))
        sp = _sigmoid(jnp.concatenate([gp0[...], gp1[...]], axis=1))
        dya = (dm * sa).astype(CDT)
        dyp = (dm * sp).astype(CDT)
        ag, pg = ag_ref[...], pg_ref[...]
        dwa_ref[...] += _dot(ag.astype(F32).T.astype(CDT), dya, NN)
        dwp_ref[...] += _dot(pg.astype(F32).T.astype(CDT), dyp, NN)
        ya = _dot(ag, wa_ref[...], NN)
        yp = _dot(pg, wp_ref[...], NN)
        dr_ref[:, 1024:2048] = (dm * ya * sa * (1.0 - sa)).astype(dr_ref.dtype)
        dr_ref[:, 2048:3072] = (dm * yp * sp * (1.0 - sp)).astype(dr_ref.dtype)
        da = _dot(dya, wa_ref[...], NT)
        dp = _dot(dyp, wp_ref[...], NT)
        za = za_ref[...]
        sga = _sigmoid(za)
        attn = at_ref[...]
        dattn = da * (za * sga)
        dr_ref[:, 0:512] = (da * attn * (sga * (1.0 + za * (1.0 - sga)))).astype(dr_ref.dtype)
        dat_ref[...] = dattn.astype(dat_ref.dtype)
        prod = dattn * attn
        low = lax.broadcasted_iota(jnp.int32, (1, 2 * HD), 1) < HD
        for j in range(NH // 2):
            x2 = prod[:, 2 * HD * j:2 * HD * (j + 1)]
            s0 = jnp.sum(jnp.where(low, x2, 0.0), axis=1, keepdims=True)
            s1 = jnp.sum(jnp.where(low, 0.0, x2), axis=1, keepdims=True)
            dl_ref[:, 2 * HD * j:2 * HD * (j + 1)] = jnp.where(low, s0, s1)
        zp = zp_ref[...]
        sgp = _sigmoid(zp)
        mpre = mp_ref[...]
        ps = ps_ref[...]
        dmixed = dp * (zp * sgp)
        dr_ref[:, 512:1024] = (dp * (mpre * ps) * (sgp * (1.0 + zp * (1.0 - sgp)))).astype(dr_ref.dtype)
        dps_ref[...] += jnp.sum(dmixed * mpre, axis=0, keepdims=True)
        dmp_ref[...] = (dmixed * ps).astype(dmp_ref.dtype)

    full = lambda shape: pl.BlockSpec(shape, lambda i: (0, 0))
    tD = pl.BlockSpec((tm, D), lambda i: (i, 0))
    tA = pl.BlockSpec((tm, AW), lambda i: (i, 0))
    rb = lambda c: pl.BlockSpec((tm, AW), lambda i: (i, c))
    return pl.pallas_call(
        body, name="bwd_gates", grid=(S // tm,),
        in_specs=[tD, full((D, D)), full((AW, D)), full((AW, D)), rb(0), rb(1), rb(2), rb(3), rb(4), rb(5),
                  tA, tA, tA, tA, full((1, AW))],
        out_specs=[full((AW, D)), full((AW, D)), pl.BlockSpec((tm, 3072), lambda i: (i, 0)), tA, tA, tA, full((1, AW))],
        out_shape=[_sds((AW, D), F32), _sds((AW, D), F32), _sds((S, REST_W), CDT), _sds((S, AW), CDT),
                   _sds((S, AW), F32), _sds((S, AW), CDT), _sds((1, AW), F32)],
        compiler_params=_cp(("arbitrary",), vmem=56),
    )(dmo, wout, wab, wpb, rest, rest, rest, rest, rest, rest, ag, pg, attn, mp, pool_scale)


def _pool_bwd(dmp, pt, pool_w, drest):
    S = dmp.shape[0]
    LS = S // R16
    PG = 128
    d3 = dmp.reshape(R16, LS, AW)
    dr3 = drest.reshape(R16, LS, REST_W)

    def body(d_ref, pt_ref, pw_ref, dr_in, du_ref, dpw_ref, g_ref, s_ref):
        del dr_in
        g = pl.program_id(0)
        win = jnp.left_shift(2, g)
        row = lax.broadcasted_iota(jnp.int32, (LS, PG), 0)
        pw = pw_ref[...].astype(CDT)
        dpw = jnp.zeros((PG, PG), F32)
        for r in range(R16):
            dm = d_ref[r]
            dpw = dpw + _dot(pt_ref[:, r * LS:(r + 1) * LS], dm, NN)
            dpooled = _dot(dm, pw, NT)
            g_ref[r] = dpooled
            sc = dpooled / _pool_counts(r, LS, win)
            s_ref[r] = sc
            if r < R16 - 1:
                s_ref[R16 + r] = jnp.where(row == LS - 1, 0.0, pltpu.roll(sc, LS - 1, 0))
        dpw_ref[...] = dpw

        def double(s):
            for i in range(0, 2 * R16 - 2 * s):
                s_ref[i] = s_ref[i] + s_ref[i + s]

        double(1)
        for k in (1, 2, 3):
            pl.when(g >= k)(functools.partial(double, 2 ** k))
        for r in range(R16):
            du_ref[r] = (s_ref[r] - g_ref[r]).astype(du_ref.dtype)

    du, dpw = pl.pallas_call(
        body, name="pool_bwd", grid=(4,),
        in_specs=[pl.BlockSpec((R16, LS, PG), lambda g: (0, 0, g)), pl.BlockSpec((PG, S), lambda g: (g, 0)),
                  pl.BlockSpec((None, PG, PG), lambda g: (g, 0, 0)), pl.BlockSpec(memory_space=pl.ANY)],
        out_specs=[pl.BlockSpec((R16, LS, PG), lambda g: (0, 0, 24 + g)),
                   pl.BlockSpec((None, PG, PG), lambda g: (g, 0, 0))],
        out_shape=[_sds((R16, LS, REST_W), CDT), _sds((4, PG, PG), F32)],
        scratch_shapes=[pltpu.VMEM((R16, LS, PG), F32), pltpu.VMEM((2 * R16, LS, PG), F32)],
        input_output_aliases={3: 0},
        compiler_params=_cp(("parallel",)),
    )(d3, pt, pool_w, dr3)
    return du.reshape(S, REST_W), dpw


def _attn_bwd(qkv, dattn, Lb, dlb, biases, geom, dqkv_prev):
    S = qkv.shape[0]
    gi, T, R, has_prev, N = geom.gi, geom.T, geom.rows, geom.has_prev, geom.n_tiles
    H2 = T // 2
    nd_outer = len(geom.outer)
    qv = geom.view(qkv)
    cq, ck, cv = 3 * gi, 3 * gi + 1, 3 * gi + 2
    W3 = 3 * AW

    def heads(q_ref, kp_ref, kc_ref, vp_ref, vc_ref, do_ref, L_ref, dl_ref, ba_ref, bb_ref, pm, dsa_ref, dsb_ref, emit):
        q_lo, q_hi = _split_rows(_ld(q_ref), R)
        do_lo, do_hi = _split_rows(_ld(do_ref), R)
        L_lo, L_hi = _split_rows(_ld(L_ref), R)
        dl_lo, dl_hi = _split_rows(_ld(dl_ref), R)
        kc_lo, kc_hi = _split_rows(_ld(kc_ref), R)
        vc_lo, vc_hi = _split_rows(_ld(vc_ref), R)
        k_b = jnp.concatenate([kc_lo, kc_hi], axis=0)
        v_b = jnp.concatenate([vc_lo, vc_hi], axis=0)
        if has_prev:
            k_a = jnp.concatenate([_split_rows(_ld(kp_ref), R)[1], kc_lo], axis=0)
            v_a = jnp.concatenate([_split_rows(_ld(vp_ref), R)[1], vc_lo], axis=0)
        else:
            k_a, v_a = k_b, v_b
        low = lax.broadcasted_iota(jnp.int32, (1, 2 * HD), 1) < HD
        for j in range(NH // 2):
            sl = slice(2 * HD * j, 2 * HD * (j + 1))
            res = []
            for q, k, v, do, L, dl, b_ref, mask, ds_ref in (
                    (q_lo, k_a, v_a, do_lo, L_lo, dl_lo, ba_ref, pm, dsa_ref),
                    (q_hi, k_b, v_b, do_hi, L_hi, dl_hi, bb_ref, None, dsb_ref)):
                q2 = q[:, sl].astype(F32) * 0.125
                do2 = do[:, sl].astype(F32)
                k2, v2 = k[:, sl], v[:, sl]
                L2, dl2 = L[:, sl], dl[:, sl]
                qs = jnp.concatenate([jnp.where(low, q2, 0.0), jnp.where(low, 0.0, q2)], axis=0).astype(q.dtype)
                dom = jnp.concatenate([jnp.where(low, do2, 0.0), jnp.where(low, 0.0, do2)], axis=0).astype(q.dtype)
                Lc = jnp.concatenate([L2[:, 0:1], L2[:, HD:HD + 1]], axis=0)
                dlc = jnp.concatenate([dl2[:, 0:1], dl2[:, HD:HD + 1]], axis=0)
                s = _dot(qs, k2, NT) + b_ref[j]
                if mask is not None:
                    s = s + mask
                p = jnp.exp(s - Lc)
                ds = p * (_dot(dom, v2, NT) - dlc)
                ds_ref[j] += ds
                dsc = ds.astype(q.dtype)
                dq_st = _dot(dsc, k2, NN)
                res.append((jnp.where(low, dq_st[:H2], dq_st[H2:]) * 0.125, _dot(dsc, qs, TN),
                            _dot(p.astype(q.dtype), dom, TN)))
            (dq_lo, dk_a, dv_a), (dq_hi, dk_b, dv_b) = res
            dq = _merge_rows(dq_lo, dq_hi, R)
            if has_prev:
                emit(sl, dq, _merge_rows(dk_a[H2:] + dk_b[:H2], dk_b[H2:], R),
                     _merge_rows(dv_a[H2:] + dv_b[:H2], dv_b[H2:], R), dk_a[:H2], dv_a[:H2])
            else:
                emit(sl, dq, _merge_rows(dk_a[:H2] + dk_b[:H2], dk_a[H2:] + dk_b[H2:], R),
                     _merge_rows(dv_a[:H2] + dv_b[:H2], dv_a[H2:] + dv_b[H2:], R), None, None)

    def first_step():
        ok = pl.program_id(nd_outer) == 0 if has_prev else pl.program_id(0) == 0
        for a in range(nd_outer if has_prev else 0):
            ok = ok & (pl.program_id(a) == 0)
        return ok

    if has_prev:
        def body(q_ref, kp_ref, kc_ref, vp_ref, vc_ref, do_ref, L_ref, dl_ref, ba_ref, bb_ref, prev_in, out_ref,
                 dsa_ref, dsb_ref, hold, new, pp):
            del prev_in
            s_id = pl.program_id(nd_outer)

            @pl.when(first_step())
            def _():
                dsa_ref[...] = jnp.zeros_like(dsa_ref)
                dsb_ref[...] = jnp.zeros_like(dsb_ref)

            @pl.when(s_id < N)
            def _():
                def emit(sl, dq, dk, dv, dk_prev, dv_prev):
                    new[:, sl] = dq
                    new[:, slice(AW + sl.start, AW + sl.stop)] = dk
                    new[:, slice(2 * AW + sl.start, 2 * AW + sl.stop)] = dv
                    pp[:, sl] = dk_prev
                    pp[:, slice(AW + sl.start, AW + sl.stop)] = dv_prev

                heads(q_ref, kp_ref, kc_ref, vp_ref, vc_ref, do_ref, L_ref, dl_ref, ba_ref, bb_ref,
                      _prev_mask(s_id, H2, T), dsa_ref, dsb_ref, emit)

            @pl.when((s_id >= 1) & (s_id < N))
            def _():
                _st(out_ref, hold[:, 0:AW], slice(0, AW))
                late = _merge_rows(jnp.zeros((H2, 2 * AW), F32), pp[...], R)
                _st(out_ref, hold[:, AW:W3] + late, slice(AW, W3))

            @pl.when(s_id == N)
            def _():
                _st(out_ref, hold[...])

            @pl.when(s_id < N)
            def _():
                hold[...] = new[...]

        cur = lambda n: jnp.minimum(n, N - 1)
        prev = lambda n: jnp.clip(n - 1, 0, N - 1)
        delayed = lambda n: jnp.maximum(n - 1, 0)
        in_specs = [geom.spec(AW, cq, cur), geom.spec(AW, ck, prev), geom.spec(AW, ck, cur),
                    geom.spec(AW, cv, prev), geom.spec(AW, cv, cur),
                    geom.spec(AW, 0, cur), geom.spec(AW, 0, cur), geom.spec(AW, 0, cur)]
        args = [qv] * 5 + [geom.view(dattn), geom.view(Lb), geom.view(dlb)]
        out_spec = geom.spec(W3, gi, delayed)
        grid = geom.outer + (N + 1,)
        scratch = [pltpu.VMEM((T, W3), F32), pltpu.VMEM((T, W3), F32), pltpu.VMEM((H2, 2 * AW), F32)]
    else:
        def body(q_ref, kc_ref, vc_ref, do_ref, L_ref, dl_ref, ba_ref, bb_ref, prev_in, out_ref, dsa_ref, dsb_ref):
            del prev_in

            @pl.when(first_step())
            def _():
                dsa_ref[...] = jnp.zeros_like(dsa_ref)
                dsb_ref[...] = jnp.zeros_like(dsb_ref)

            def emit(sl, dq, dk, dv, dk_prev, dv_prev):
                _st(out_ref, dq, sl)
                _st(out_ref, dk, slice(AW + sl.start, AW + sl.stop))
                _st(out_ref, dv, slice(2 * AW + sl.start, 2 * AW + sl.stop))

            heads(q_ref, None, kc_ref, None, vc_ref, do_ref, L_ref, dl_ref, ba_ref, bb_ref, None, dsa_ref, dsb_ref,
                  emit)

        cur = lambda n: n
        in_specs = [geom.spec(AW, cq, cur), geom.spec(AW, ck, cur), geom.spec(AW, cv, cur),
                    geom.spec(AW, 0, cur), geom.spec(AW, 0, cur), geom.spec(AW, 0, cur)]
        args = [qv] * 3 + [geom.view(dattn), geom.view(Lb), geom.view(dlb)]
        out_spec = geom.spec(W3, gi, cur)
        grid = geom.outer
        scratch = []

    whole = lambda b: pl.BlockSpec(b.shape, lambda *ids: (0, 0, 0))
    biases = [b.reshape(NH // 2, 2 * H2, b.shape[2]) for b in biases]
    in_specs += [whole(b) for b in biases]
    in_specs.append(pl.BlockSpec(memory_space=pl.ANY))
    if dqkv_prev is None:
        last, aliases = biases[0], {}
    else:
        last, aliases = geom.view(dqkv_prev), {len(in_specs) - 1: 0}
    dq, dsa, dsb = pl.pallas_call(
        body, name=f"attn_bwd{gi}", grid=grid, in_specs=in_specs,
        out_specs=[out_spec] + [whole(b) for b in biases],
        out_shape=[_sds(geom.view_shape(QKV_W), CDT)] + [_sds(b.shape, F32) for b in biases],
        scratch_shapes=scratch,
        input_output_aliases=aliases,
        compiler_params=_cp(("arbitrary",) * len(grid), vmem=56),
    )(*args, *biases, last)
    return geom.unview(dq), [d.reshape(NH, H2, d.shape[2]) for d in (dsa, dsb)]


def _h_bwd(dh, dx2, x, norm_g, scale):
    S = x.shape[0]

    def body(dh_ref, dx2_ref, x_ref, g_ref, sc_ref, gx_ref, dsh_ref, dsc_ref, dng_ref, nat_ref):
        i = pl.program_id(0)

        @pl.when(i == 0)
        def _():
            dsh_ref[...] = jnp.zeros_like(dsh_ref)
            dsc_ref[...] = jnp.zeros_like(dsc_ref)
            dng_ref[...] = jnp.zeros_like(dng_ref)

        for r in range(R16):
            nat_ref[:, r, :] = dh_ref[r]
        dh = jnp.concatenate([nat_ref[l] for l in range(R16)], axis=0)
        g = g_ref[...]
        one_sc = 1.0 + sc_ref[...]
        xv = x_ref[...]
        r1 = lax.rsqrt(jnp.mean(xv * xv, axis=-1, keepdims=True) + EPS)
        xn = xv * r1
        dhx = dh * xn
        dxn = dh * (g * one_sc)
        dx1 = r1 * (dxn - xn * jnp.mean(dxn * xn, axis=-1, keepdims=True))
        gx_ref[...] = dx2_ref[...] + dx1
        dsh_ref[...] += jnp.sum(dh, axis=0, keepdims=True)
        dsc_ref[...] += jnp.sum(dhx * g, axis=0, keepdims=True)
        dng_ref[...] += jnp.sum(dhx * one_sc, axis=0, keepdims=True)

    nat = pl.BlockSpec((TOK, D), lambda i: (i, 0))
    vec = pl.BlockSpec((1, D), lambda i: (0, 0))
    return pl.pallas_call(
        body, name="h_bwd", grid=(S // TOK,),
        in_specs=[pl.BlockSpec((R16, R16, D), lambda i: (0, i, 0)), nat, nat, vec, vec],
        out_specs=[nat, vec, vec, vec],
        out_shape=[_sds((S, D), F32), _sds((1, D), F32), _sds((1, D), F32), _sds((1, D), F32)],
        scratch_shapes=[pltpu.VMEM((R16, R16, D), F32)],
        compiler_params=_cp(("arbitrary",)),
    )(dh, dx2, x, norm_g, scale)


def _local_step(x, target, shift, scale, gate, norm_g, w_in_full, pool_w, pool_scale, wab, wpb, wout, rel_bias,
                final_g):
    st = _local_pre(x, shift, scale, norm_g, rel_bias)
    st = _local_grads(st, target, gate, w_in_full, pool_w, pool_scale, wab, wpb, wout, final_g)
    d_rel_bias = _local_bias_grads(st, _BIAS_GRADS_EARLY + _BIAS_GRADS_LATE)
    gx, dmod, dng = _local_input_grads(st, w_in_full, norm_g, scale)
    return dict(loss=st["loss"], grad_x=gx, dw_in=st["dw_in"], dw_ab=st["dw_ab"], dw_pb=st["dw_pb"],
                dw_out=st["dw_out"], d_norm_g=dng, d_pool_w=st["dpw"], d_pool_scale=st["dps"],
                d_rel_bias=d_rel_bias, d_final_g=st["gfg"], dmod=dmod)


def _local_pre(x, shift, scale, norm_g, rel_bias, dep=None):
    S = x.shape[0]
    LS = S // R16
    geoms = [_Geom(gi, LS) for gi in range(3)]
    idx_np = [[g.bucket_index(sub) for sub in range(2)] for g in geoms]
    buckets = [[tuple(int(b) for b in np.unique(ix) if b >= 0) for ix in pair] for pair in idx_np]
    idxs = [[jnp.asarray(ix) for ix in pair] for pair in idx_np]
    biases = [[_bias_table(idxs[gi][sub], buckets[gi][sub], rel_bias, gi, sub, dep) for sub in range(2)]
              for gi in range(3)]
    h = _norm_mod(x, norm_g, shift, scale, dep).reshape(S, D)
    return dict(geoms=geoms, idxs=idxs, buckets=buckets, biases=biases, x=x, h=h, ht=_tcast(h, "h_transpose"))


def _local_grads(st, target, gate, w_in_full, pool_w, pool_scale, wab, wpb, wout, final_g):
    geoms, idxs, biases, x, h, ht = (st[k] for k in ("geoms", "idxs", "biases", "x", "h", "ht"))
    S = h.shape[0]
    LS = S // R16
    qkv, rest = _proj(h, w_in_full)

    os_, ls_ = [], []
    for gi in range(2):
        o, l = _attn_fwd(qkv, biases[gi], geoms[gi])
        os_.append(o)
        ls_.append(l)
    attn, Lb, ag = _attn_fwd(qkv, biases[2], geoms[2], others=(os_, ls_, rest))
    pt, mp, pg = _pool_fwd(rest, pool_w, pool_scale)
    dx2, dmo3, loss, gfg, dgate, dw_out = _merge_out_loss(ag, pg, rest, wab, wpb, wout, x, target, gate, final_g)
    dmo = dmo3.reshape(S, D)

    dw_ab, dw_pb, drest, dattn, dlb, dmp, dps = _bwd_gates(dmo, wout, wab, wpb, rest, ag, pg, attn, mp, pool_scale)
    drest, dpw = _pool_bwd(dmp, pt, pool_w, drest)

    dqkv = None
    dsums = []
    for gi in range(3):
        dqkv, dsum = _attn_bwd(qkv, dattn, Lb, dlb, biases[gi], geoms[gi], dqkv)
        dsums.append(dsum)

    dw_in = _mm_tn(ht, dqkv, n_total=8192, out_blk=lambda j: j, name="dw_in_qkv", with_low=True)
    dw_in, dw_in_low = _mm_tn(ht, drest, n_total=8192, out_blk=_rest_blk, name="dw_in_rest", prev=dw_in, with_low=True)
    return dict(st, loss=loss[0, 0], dw_in=dw_in, dw_in_low=dw_in_low, dw_ab=dw_ab, dw_pb=dw_pb, dw_out=dw_out, dpw=dpw, dps=dps, gfg=gfg,
                dgate=dgate, dsums=dsums, dqkv=dqkv, drest=drest, dx2=dx2)


def _local_bias_grads(st, which, dep=None):
    cols = []
    for gi in range(3):
        got = [_bias_grad(st["dsums"][gi][sub], st["idxs"][gi][sub], st["buckets"][gi][sub], gi, sub, dep)
               for sub in range(2) if (gi, sub) in which]
        cols.append(sum(got) if got else jnp.zeros((32, NH), F32))
    return jnp.concatenate(cols, axis=1)


_BIAS_GRADS_EARLY = ((0, 0), (0, 1), (1, 0), (2, 0))
_BIAS_GRADS_LATE = ((1, 1), (2, 1))


def _dh_h_bwd(dqkv, drest, w, dx2, x, norm_g, scale, dep=None):
    S = x.shape[0]
    dep_specs, dep_args = _after(dep)

    def body(a_ref, b_ref, w_ref, dx2_ref, x_ref, g_ref, sc_ref, *rest):
        gx_ref, dsh_ref, dsc_ref, dng_ref, nat_ref = rest[-5:]
        i = pl.program_id(0)

        @pl.when(i == 0)
        def _():
            dsh_ref[...] = jnp.zeros_like(dsh_ref)
            dsc_ref[...] = jnp.zeros_like(dsc_ref)
            dng_ref[...] = jnp.zeros_like(dng_ref)

        a, b = _ld(a_ref), _ld(b_ref)
        acc = _dot(a[:, 0:512], _w512(w_ref, 0), NT)
        for j in range(1, 9):
            acc = acc + _dot(a[:, 512 * j:512 * (j + 1)], _w512(w_ref, j), NT)
        off = 0
        for lo, hi in REST_COLS:
            for j in range(lo, hi):
                acc = acc + _dot(b[:, off:off + 512], _w512(w_ref, j), NT)
                off += 512
        for r in range(R16):
            nat_ref[:, r, :] = acc[R16 * r:R16 * (r + 1)]
        dh = jnp.concatenate([nat_ref[l] for l in range(R16)], axis=0)
        g = g_ref[...]
        one_sc = 1.0 + sc_ref[...]
        xv = x_ref[...]
        r1 = lax.rsqrt(jnp.mean(xv * xv, axis=-1, keepdims=True) + EPS)
        xn = xv * r1
        dhx = dh * xn
        dxn = dh * (g * one_sc)
        dx1 = r1 * (dxn - xn * jnp.mean(dxn * xn, axis=-1, keepdims=True))
        gx_ref[...] = dx2_ref[...] + dx1
        dsh_ref[...] += jnp.sum(dh, axis=0, keepdims=True)
        dsc_ref[...] += jnp.sum(dhx * g, axis=0, keepdims=True)
        dng_ref[...] += jnp.sum(dhx * one_sc, axis=0, keepdims=True)

    nat = pl.BlockSpec((TOK, D), lambda i: (i, 0))
    vec = pl.BlockSpec((1, D), lambda i: (0, 0))
    slab = lambda W: pl.BlockSpec((R16, R16, W), lambda i: (0, i, 0))
    return pl.pallas_call(
        body, name="dh_h_bwd", grid=(S // TOK,),
        in_specs=[slab(QKV_W), slab(REST_W), _resident((8, D, 1024)), nat, nat, vec, vec] + dep_specs,
        out_specs=[nat, vec, vec, vec],
        out_shape=[_sds((S, D), F32), _sds((1, D), F32), _sds((1, D), F32), _sds((1, D), F32)],
        scratch_shapes=[pltpu.VMEM((R16, R16, D), F32)],
        compiler_params=_cp(("arbitrary",), vmem=56),
    )(dqkv, drest, w, dx2, x, norm_g, scale, *dep_args)


def _local_input_grads(st, w_in_full, norm_g, scale, dep=None):
    S = st["h"].shape[0]
    LS = S // R16
    gx, dsh, dsc, dng = _dh_h_bwd(st["dqkv"].reshape(R16, LS, QKV_W), st["drest"].reshape(R16, LS, REST_W),
                                  w_in_full, st["dx2"], st["x"], norm_g, scale, dep)
    return gx, jnp.concatenate([dsh, dsc, st["dgate"]], axis=1), dng


def _my_place():
    return lax.axis_index("x"), lax.axis_index("y"), lax.axis_index("c")


def _block_of(ref, axis, idx, width):
    if axis is None:
        return ref.at[idx]
    start = pl.multiple_of(idx * width, width)
    if axis == 0:
        return ref.at[pl.ds(start, width)]
    return ref.at[:, pl.ds(start, width)]


_SEM = pl.BlockSpec(memory_space=pltpu.SEMAPHORE)
_EFFECT = pltpu.SideEffectType.DATAFLOW_SIDE_EFFECTING


def _split_start(name, srcs, lands, ncopy, plan, dep=None):
    arrs = list(srcs) + list(lands)
    na, ns = len(arrs), len(srcs)
    dep_specs, dep_args = _after(dep)

    def body(*refs):
        send_sems, recv_sems, token = refs[na + len(dep_args)], refs[na + len(dep_args) + 1], refs[-1]
        for k, (s, d, dev) in enumerate(plan(refs[:ns], refs[ns:na])):
            pltpu.make_async_remote_copy(src_ref=s, dst_ref=d, send_sem=send_sems.at[k], recv_sem=recv_sems.at[k],
                                         device_id=dev, device_id_type=MESH).start()
        token[...] = jnp.zeros_like(token)

    outs = pl.pallas_call(
        body, name=name,
        out_shape=(pltpu.SemaphoreType.DMA((ncopy,)), pltpu.SemaphoreType.DMA((ncopy,)),
                   *[pltpu.HBM(a.shape, a.dtype) for a in arrs], _sds((8, 128), F32)),
        in_specs=[_HBM] * na + dep_specs,
        out_specs=(_SEM, _SEM, *[_HBM] * na, pl.BlockSpec(memory_space=pltpu.VMEM)),
        input_output_aliases={i: 2 + i for i in range(na)},
        compiler_params=pltpu.CompilerParams(has_side_effects=_EFFECT),
    )(*[pltpu.with_memory_space_constraint(a, pltpu.HBM) for a in arrs], *dep_args)
    return (outs[0], outs[1]), list(outs[2:2 + ns]), list(outs[2 + ns:2 + na]), outs[-1]


def _split_wait(name, sems, srcs, lands, plan, after):
    arrs = list(srcs) + list(lands)
    na, ns = len(arrs), len(srcs)
    after = list(after) if isinstance(after, (list, tuple)) else [after]

    def body(*refs):
        send_sems, recv_sems = refs[na], refs[na + 1]
        for k, (s, d, dev) in enumerate(plan(refs[:ns], refs[ns:na])):
            cp = pltpu.make_async_remote_copy(src_ref=s, dst_ref=d, send_sem=send_sems.at[k], recv_sem=recv_sems.at[k],
                                              device_id=dev, device_id_type=MESH)
            cp.wait_send()
            cp.wait_recv()

    outs = pl.pallas_call(
        body, name=name, out_shape=[pltpu.HBM(a.shape, a.dtype) for a in arrs],
        in_specs=[_HBM] * na + [_SEM, _SEM] + [pl.BlockSpec(memory_space=pl.ANY)] * len(after),
        out_specs=[_HBM] * na,
        input_output_aliases={i: i for i in range(na)},
        compiler_params=pltpu.CompilerParams(has_side_effects=_EFFECT),
    )(*arrs, sems[0], sems[1], *after)
    return list(outs[:ns]), list(outs[ns:])


def _gather_chips_plan(layout):
    def plan(src, land):
        x, y, c = _my_place()
        me = 4 * x + 2 * y + c
        mine = [_block_of(land[p], layout[p][0], me, layout[p][1]) for p in range(len(land))]
        return [(mine[p], mine[p], (cx, cy, c)) for p in range(len(land)) for cx, cy in [(1 - x, y), (x, 1 - y)]]
    return plan


def _pair_plan(layout):
    def plan(src, land):
        x, y, c = _my_place()
        n = len(layout)
        return [(_block_of(src[p], layout[p][0], 2 * q + (1 - c), layout[p][1]), land[p].at[q], (x, y, 1 - c))
                for p in range(n) for q in range(4)] + [(src[p], land[p], (x, y, 1 - c)) for p in range(n, len(src))]
    return plan


def _chips_plan(n):
    def plan(src, land):
        x, y, c = _my_place()
        chips = [(1 - x, y), (x, 1 - y), (1 - x, 1 - y)]
        return [(src[p].at[2 * cx + cy], land[p].at[2 * x + y], (cx, cy, c)) for p in range(n) for cx, cy in chips] + \
               [(src[p], land[p].at[2 * x + y], (cx, cy, c)) for p in range(n, len(src)) for cx, cy in chips]
    return plan


def _pair_share(gathered):
    n = len(gathered)

    def body(*refs):
        land = refs[n:2 * n]
        send_sems, recv_sems, fwd_send, fwd_recv = refs[2 * n:]
        x, y, c = _my_place()
        k_dg = 3 - (2 * x + y)
        south = c == 0
        pass_on = 4 * jnp.where(south, 1 - x, x) + 2 * jnp.where(south, y, 1 - y) + c
        to = (jnp.where(south, x, 1 - x), jnp.where(south, 1 - y, y), c)

        def forward(p):
            return pltpu.make_async_remote_copy(
                src_ref=land[p].at[pass_on], dst_ref=land[p].at[pass_on], send_sem=fwd_send.at[p],
                recv_sem=fwd_recv.at[p], device_id=to, device_id_type=MESH)

        def to_sibling(p, k):
            blk = land[p].at[2 * k + c]
            return pltpu.make_async_remote_copy(
                src_ref=blk, dst_ref=blk, send_sem=send_sems.at[4 * p + k], recv_sem=recv_sems.at[4 * p + k],
                device_id=(x, y, 1 - c), device_id_type=MESH)

        for p in range(n):
            forward(p).start()
        for p in range(n):
            for k in range(4):
                @pl.when(k != k_dg)
                def _():
                    to_sibling(p, k).start()
        for p in range(n):
            dg = land[p].at[2 * k_dg + c]
            pltpu.make_async_remote_copy(src_ref=dg, dst_ref=dg, send_sem=fwd_send.at[p], recv_sem=fwd_recv.at[p],
                                         device_id=to, device_id_type=MESH).wait_recv()
            for k in range(4):
                @pl.when(k == k_dg)
                def _():
                    to_sibling(p, k).start()
        for p in range(n):
            forward(p).wait_send()
            for k in range(4):
                to_sibling(p, k).wait_send()
                blk = land[p].at[2 * k + 1 - c]
                pltpu.make_async_remote_copy(
                    src_ref=blk, dst_ref=blk, send_sem=send_sems.at[4 * p + k], recv_sem=recv_sems.at[4 * p + k],
                    device_id=(x, y, 1 - c), device_id_type=MESH).wait_recv()

    return pl.pallas_call(
        body, name="gather_pair_share", in_specs=[_HBM] * n, out_specs=[_HBM] * n,
        out_shape=[_sds(g.shape, g.dtype) for g in gathered],
        input_output_aliases={p: p for p in range(n)},
        scratch_shapes=[pltpu.SemaphoreType.DMA((4 * n,)), pltpu.SemaphoreType.DMA((4 * n,)),
                        pltpu.SemaphoreType.DMA((n,)), pltpu.SemaphoreType.DMA((n,))],
    )(*gathered)


def _place_slot(arrs, slot, n_slots, name, out_dtype=None):
    n = len(arrs)

    def body(s_ref, *refs):
        for p in range(n):
            refs[n + p][...] = refs[p][...].astype(refs[n + p].dtype)

    def spec(a):
        tail = a.shape[-2:]
        return pl.BlockSpec((None,) + tail, lambda i, s_ref: (s_ref[0], 0, 0))

    in_specs = [spec(a) if a.ndim == 3 else pl.BlockSpec(a.shape, lambda i, s_ref: (0, 0)) for a in arrs]
    return pl.pallas_call(
        body, name=name,
        grid_spec=pltpu.PrefetchScalarGridSpec(num_scalar_prefetch=1, grid=(1,), in_specs=in_specs,
                                               out_specs=[spec(a) for a in arrs]),
        out_shape=[_in_hbm((n_slots,) + a.shape[-2:], out_dtype or a.dtype) for a in arrs],
        compiler_params=_cp(("arbitrary",)),
    )(slot, *[_pin(a) for a in arrs])


def _cols_from_blocks(w8, name):
    _, R, C = w8.shape

    def body(w_ref, o_ref):
        for e in range(8):
            o_ref[:, C * e:C * (e + 1)] = w_ref[e]

    return pl.pallas_call(body, name=name, out_shape=_sds((R, 8 * C), w8.dtype), compiler_params=_cp())(w8)


def _all_gather_small(srcs, name):
    n = len(srcs)
    vmem = pl.BlockSpec(memory_space=pltpu.VMEM)

    def body(*refs):
        src, out = refs[:n], refs[n:2 * n]
        send_sems, recv_sems = refs[2 * n:]
        x, y, c = _my_place()
        me = 4 * x + 2 * y + c
        flip = lambda v, f: 1 - v if f else v
        peers = [(flip(x, fx), flip(y, fy), flip(c, fc)) for fx in (0, 1) for fy in (0, 1) for fc in (0, 1)][1:]
        cps = []
        for p in range(n):
            out[p][me] = src[p][...]
            for k, peer in enumerate(peers):
                cps.append(pltpu.make_async_remote_copy(
                    src_ref=src[p], dst_ref=out[p].at[me], send_sem=send_sems.at[7 * p + k],
                    recv_sem=recv_sems.at[7 * p + k], device_id=peer, device_id_type=MESH))
        for cp in cps:
            cp.start()
        for cp in cps:
            cp.wait()

    return pl.pallas_call(
        body, name=name, in_specs=[vmem] * n, out_specs=[vmem] * n,
        out_shape=[_sds((8,) + a.shape, a.dtype) for a in srcs],
        scratch_shapes=[pltpu.SemaphoreType.DMA((7 * n,)), pltpu.SemaphoreType.DMA((7 * n,))],
    )(*srcs)


def _adaln_exchange(c, w_ada, b_part):
    vmem = pl.BlockSpec(memory_space=pltpu.VMEM)

    def body(c_ref, w_ref, b_ref, call_ref, mod_ref, part_ref, send_sems, recv_sems):
        x, y, c = _my_place()
        me = 4 * x + 2 * y + c
        flip = lambda v, f: 1 - v if f else v
        peers = [(flip(x, fx), flip(y, fy), flip(c, fc)) for fx in (0, 1) for fy in (0, 1) for fc in (0, 1)][1:]
        call_ref[me] = c_ref[...]
        first = [pltpu.make_async_remote_copy(src_ref=c_ref, dst_ref=call_ref.at[me], send_sem=send_sems.at[k],
                                              recv_sem=recv_sems.at[k], device_id=peer, device_id_type=MESH)
                 for k, peer in enumerate(peers)]
        for cp in first:
            cp.start()
        for cp in first:
            cp.wait()
        c_all = jnp.concatenate([call_ref[j] for j in range(8)], axis=0)
        part_ref[...] = _dot(c_all.astype(CDT), w_ref[...].astype(CDT), NN) + b_ref[...]
        mod_ref[me] = part_ref[pl.ds(me, 1), :]
        second = [pltpu.make_async_remote_copy(
            src_ref=part_ref.at[pl.ds(4 * px + 2 * py + pc, 1)], dst_ref=mod_ref.at[me], send_sem=send_sems.at[7 + k],
            recv_sem=recv_sems.at[7 + k], device_id=(px, py, pc), device_id_type=MESH)
            for k, (px, py, pc) in enumerate(peers)]
        for cp in second:
            cp.start()
        for cp in second:
            cp.wait()

    return pl.pallas_call(
        body, name="adaln_exchange", in_specs=[vmem] * 3, out_specs=[vmem] * 2,
        out_shape=[_sds((8, 1, D), F32), _sds((8, 1, w_ada.shape[1]), F32)],
        scratch_shapes=[pltpu.VMEM((8, w_ada.shape[1]), F32), pltpu.SemaphoreType.DMA((14,)),
                        pltpu.SemaphoreType.DMA((14,))],
        compiler_params=_cp(),
    )(c, w_ada, b_part)


def _pair_add(dw, recv, axis, width, cidx, name):
    if axis == 0:
        tr = min(width, 256)
        C = dw.shape[1]
        grid = (4, width // tr)
        nb = width // tr
        own = pl.BlockSpec((tr, C), lambda q, i, c_ref: ((2 * q + c_ref[0]) * nb + i, 0))
        oth = pl.BlockSpec((None, tr, C), lambda q, i, c_ref: (q, i, 0))
    else:
        R = dw.shape[0]
        tr = min(R, 1024)
        grid = (4, R // tr)
        own = pl.BlockSpec((tr, width), lambda q, i, c_ref: (i, 2 * q + c_ref[0]))
        oth = pl.BlockSpec((None, tr, width), lambda q, i, c_ref: (q, i, 0))

    def body(c_ref, a_ref, b_ref, o_ref):
        o_ref[...] = (a_ref[...] + b_ref[...].astype(F32)).astype(o_ref.dtype)

    return pl.pallas_call(
        body, name=name,
        grid_spec=pltpu.PrefetchScalarGridSpec(num_scalar_prefetch=1, grid=grid, in_specs=[own, oth], out_specs=oth),
        out_shape=_in_hbm(recv.shape, CDT), compiler_params=_cp(("parallel", "parallel")),
    )(cidx, _pin(dw), _pin(recv))


def _add_pairs(xs, ys, name):
    n = len(xs)

    def body(*refs):
        for p in range(n):
            refs[2 * n + p][...] = refs[p][...] + refs[n + p][...]

    return pl.pallas_call(body, name=name, out_shape=[_sds(a.shape, F32) for a in xs])(*xs, *ys)


def _adam_math(w, g, m, v):
    m = ADAM_B1 * m + (1.0 - ADAM_B1) * g
    v = ADAM_B2 * v + (1.0 - ADAM_B2) * (g * g)
    m_hat = m / (1.0 - ADAM_B1 ** ADAM_STEP)
    v_hat = v / (1.0 - ADAM_B2 ** ADAM_STEP)
    delta = -ADAM_LR * (m_hat / (jnp.sqrt(v_hat) + ADAM_EPS) + ADAM_WD * w)
    return delta, m, v


def _adam(gparts, w, m, v, name):
    P, R, C = gparts.shape
    tr = R if R <= 256 else 256

    def body(g_ref, w_ref, m_ref, v_ref, go_ref, d_ref, mo_ref, vo_ref):
        g = g_ref[0].astype(F32)
        for i in range(1, P):
            g = g + g_ref[i].astype(F32)
        d, mn, vn = _adam_math(w_ref[...], g, m_ref[...], v_ref[...])
        go_ref[...] = g
        d_ref[...] = d
        mo_ref[...] = mn
        vo_ref[...] = vn

    t = pl.BlockSpec((tr, C), lambda i: (i, 0))
    return pl.pallas_call(
        body, name=name, grid=(R // tr,),
        in_specs=[pl.BlockSpec((P, tr, C), lambda i: (0, i, 0)), t, t, t], out_specs=[t, t, t, t],
        out_shape=[_in_hbm((R, C), F32)] * 4, compiler_params=_cp(("parallel",)),
    )(*[_pin(a) for a in (gparts, w, m, v)])


def _sum_parts(gparts, name):
    P, R, C = gparts.shape

    def body(g_ref, o_ref):
        g = g_ref[0]
        for i in range(1, P):
            g = g + g_ref[i]
        o_ref[...] = g

    return pl.pallas_call(body, name=name, out_shape=_sds((R, C), F32))(gparts)


def _adam_w_ada(c_t, dm, w, m, v):
    R, C = w.shape

    def body(c_ref, dm_ref, w_ref, m_ref, v_ref, go_ref, d_ref, mo_ref, vo_ref):
        g = c_ref[:, 0:1] * dm_ref[0:1, :]
        for b in range(1, 8):
            g = g + c_ref[:, b:b + 1] * dm_ref[b:b + 1, :]
        d, mn, vn = _adam_math(w_ref[...], g, m_ref[...], v_ref[...])
        go_ref[...] = g
        d_ref[...] = d
        mo_ref[...] = mn
        vo_ref[...] = vn

    return pl.pallas_call(body, name="adam_w_ada", out_shape=[_sds((R, C), F32)] * 4,
                          compiler_params=_cp())(c_t, dm, w, m, v)


def kernel(x, c, norm_g, w_ada, b_ada, w_in, pool_w, pool_scale, w_attn_br, w_pool_br, w_out, rel_bias, final_g, loss_target, m_norm_g, m_w_ada, m_b_ada, m_w_in, m_pool_w, m_pool_scale, m_w_attn_br, m_w_pool_br, m_w_out, m_rel_bias, m_final_g, v_norm_g, v_w_ada, v_b_ada, v_w_in, v_pool_w, v_pool_scale, v_w_attn_br, v_w_pool_br, v_w_out, v_rel_bias, v_final_g):
    S = x.shape[1]
    px, py, pc = _my_place()
    me = 4 * px + 2 * py + pc
    cidx = jnp.reshape(pc, (1,)).astype(jnp.int32)

    layout = [(1, 1024), (1, 128), (1, 128), (0, 128)]
    final_g2 = final_g.reshape(1, D)
    me1 = jnp.reshape(me, (1,)).astype(jnp.int32)

    b_part = lax.dynamic_slice(b_ada, (0, me * 384), (1, 384))
    c_all, mod_all = _adaln_exchange(c, w_ada[0], b_part)
    c_all = c_all.reshape(8, D)
    mod = mod_all.reshape(1, 3 * D)
    shift, scale, gate = mod[:, :D], mod[:, D:2 * D], mod[:, 2 * D:]

    gplan = _gather_chips_plan([(None, 1)] * 4)
    lands = _place_slot([w_in[0], w_attn_br[0], w_pool_br[0], w_out[0]], me1, 8, "place_own_shards", CDT)
    g_sems, _, lands, tok = _split_start("gather_chips_start", [], lands, 8, gplan, dep=mod_all)
    st = _local_pre(x[0], shift, scale, norm_g, rel_bias, dep=tok)
    _, lands = _split_wait("gather_chips_wait", g_sems, [], lands, gplan,
                           [st["ht"]] + [b for pair in st["biases"] for b in pair])
    w_in_full, wab8, wpb8, wout8 = _pair_share(lands)
    wab, wpb = _cols_from_blocks(wab8, "w_attn_br_cols"), _cols_from_blocks(wpb8, "w_pool_br_cols")
    wout = wout8.reshape(D, D)

    st = _local_grads(st, loss_target[0], gate, w_in_full, pool_w[0], pool_scale, wab, wpb, wout, final_g2)

    early = [st["dpw"].reshape(AW, 128), st["dps"], st["gfg"], jnp.broadcast_to(st["loss"], (1, 128))]
    dws = [st["dw_in"], st["dw_ab"], st["dw_pb"], st["dw_out"]]
    sent = [st["dw_in_low"]] + dws[1:] + early
    pplan = _pair_plan(layout)
    pair_shapes = [(4, w, a.shape[1]) if ax == 0 else (4, a.shape[0], w) for a, (ax, w) in zip(dws, layout)]
    p_sems, sent, recv, tok = _split_start(
        "reduce_pair_start", sent,
        [lax.empty(s, a.dtype) for s, a in zip(pair_shapes + [a.shape for a in early], sent)], 20, pplan)
    d_rel_bias = _local_bias_grads(st, _BIAS_GRADS_EARLY, dep=tok)
    sent, recv = _split_wait("reduce_pair_wait", p_sems, sent, recv, pplan, d_rel_bias)
    parts = [_pair_add(dw, r, ax, w, cidx, f"pair_add{i}")
             for i, (dw, r, (ax, w)) in enumerate(zip([dws[0]] + sent[1:4], recv[:4], layout))]
    parts += _add_pairs(sent[4:], recv[4:], "pair_add_smalls")

    slot1 = jnp.reshape(2 * px + py, (1,)).astype(jnp.int32)
    cplan = _chips_plan(4)
    c_sems, parts, lands, tok = _split_start("reduce_chips_start", parts,
                                             _place_slot(parts, slot1, 4, "place_own_partials"), 24, cplan)
    d_rel_bias = d_rel_bias + _local_bias_grads(st, _BIAS_GRADS_LATE, dep=tok)
    grad_x, dmod, d_norm_g = _local_input_grads(st, w_in_full, norm_g, scale, dep=tok)

    dmod_all, ng_all, rb_all = _all_gather_small([dmod, d_norm_g, d_rel_bias], "gather_late")
    o_norm_g = _adam(ng_all, norm_g, m_norm_g, v_norm_g, "adam_norm_g")
    o_rel_bias = _adam(rb_all, rel_bias, m_rel_bias, v_rel_bias, "adam_rel_bias")
    o_b_ada = _adam(dmod_all, b_ada, m_b_ada, v_b_ada, "adam_b_ada")
    dm_mine = lax.dynamic_slice(dmod_all.reshape(8, 3 * D), (0, me * 384), (8, 384))
    o_w_ada = [o[None] for o in _adam_w_ada(c_all.T, dm_mine, w_ada[0], m_w_ada[0], v_w_ada[0])]

    _, (g_in, g_ab, g_pb, g_out, g_pw, g_ps, g_fg, g_loss) = _split_wait(
        "reduce_chips_wait", c_sems, parts, lands, cplan, [o_w_ada[0], o_b_ada[0], o_norm_g[0], o_rel_bias[0]])
    o_w_in = [o[None] for o in _adam(g_in, w_in[0], m_w_in[0], v_w_in[0], "adam_w_in")]
    o_w_ab = [o[None] for o in _adam(g_ab, w_attn_br[0], m_w_attn_br[0], v_w_attn_br[0], "adam_w_attn_br")]
    o_w_pb = [o[None] for o in _adam(g_pb, w_pool_br[0], m_w_pool_br[0], v_w_pool_br[0], "adam_w_pool_br")]
    o_w_out = [o[None] for o in _adam(g_out, w_out[0], m_w_out[0], v_w_out[0], "adam_w_out")]
    o_pool_w = [o.reshape(pool_w.shape) for o in _adam(g_pw, pool_w.reshape(AW, 128), m_pool_w.reshape(AW, 128),
                                                       v_pool_w.reshape(AW, 128), "adam_pool_w")]
    o_pool_scale = _adam(g_ps, pool_scale, m_pool_scale, v_pool_scale, "adam_pool_scale")
    o_final_g = [o.reshape(D) for o in _adam(g_fg, final_g2, m_final_g.reshape(1, D), v_final_g.reshape(1, D),
                                             "adam_final_g")]
    loss = _sum_parts(g_loss, "sum_loss")[0, 0]

    per_w = [o_norm_g, o_w_ada, o_b_ada, o_w_in, o_pool_w, o_pool_scale, o_w_ab, o_w_pb, o_w_out, o_rel_bias, o_final_g]
    outs = [loss, grad_x[None]]
    for k in range(4):
        outs += [o[k] for o in per_w]
    return tuple(outs)
```

```python
import functools
import math

import numpy as np
import jax
import jax.numpy as jnp
from jax import lax
from jax.experimental import pallas as pl
from jax.experimental.pallas import tpu as pltpu

F32 = jnp.float32
CDT = jnp.bfloat16
D = 1024
HD = 64
NH = 8
AW = 512
QKV_W = 4608
REST_W = 3584
R16 = 16
EPS = 1e-6
NEG = -1e30
N_BACK = 128
ADAM_LR, ADAM_B1, ADAM_B2, ADAM_EPS, ADAM_WD, ADAM_STEP = 0.001, 0.9, 0.999, 1e-08, 0.01, 10
MESH = pl.DeviceIdType.MESH
VMEM_MB = 1024 * 1024


def _cp(sem=None, vmem=48, **kw):
    if sem is not None:
        kw["dimension_semantics"] = sem
    return pltpu.CompilerParams(vmem_limit_bytes=vmem * VMEM_MB, **kw)


def _sds(shape, dtype):
    return jax.ShapeDtypeStruct(tuple(shape), dtype)


def _in_hbm(shape, dtype):
    return pltpu.HBM(tuple(shape), dtype)


def _pin(a):
    return pltpu.with_memory_space_constraint(a, pltpu.HBM)


def _is_big(a):
    return a.size * a.dtype.itemsize >= 4 * VMEM_MB


def _rest_blk(jj):
    return jnp.where(jj == 0, 9, jnp.where(jj == 6, 10, jj + 10))


def _sigmoid(z):
    return 0.5 * jnp.tanh(0.5 * z) + 0.5


def _dot(a, b, dims):
    return lax.dot_general(a, b, (dims, ((), ())), preferred_element_type=F32)


NN = ((1,), (0,))
NT = ((1,), (1,))
TN = ((0,), (0,))


def _ld(ref):
    if len(ref.shape) == 2:
        return ref[...]
    return jnp.concatenate([ref[a] for a in range(ref.shape[0])], axis=0)


def _st(ref, val, cols=None):
    val = val.astype(ref.dtype)
    if len(ref.shape) == 2:
        if cols is None:
            ref[...] = val
        else:
            ref[:, cols] = val
        return
    rows = ref.shape[1]
    for a in range(ref.shape[0]):
        if cols is None:
            ref[a] = val[a * rows:(a + 1) * rows]
        else:
            ref[a, :, cols] = val[a * rows:(a + 1) * rows]


REST_COLS = ((9, 10), (11, 16), (10, 11))
_HBM = pl.BlockSpec(memory_space=pltpu.HBM)


def _resident(shape):
    return pl.BlockSpec(shape, lambda *ids: (0,) * len(shape), pipeline_mode=pl.Buffered(1))


def _w512(w_ref, j):
    return w_ref[j // 2, :, 512 * (j % 2):512 * (j % 2 + 1)]


def _proj(h, w):
    S = h.shape[0]
    tm = 512

    def body(a_ref, w_ref, q_ref, r_ref):
        a = a_ref[...]
        for j in range(9):
            q_ref[:, 512 * j:512 * (j + 1)] = _dot(a, _w512(w_ref, j), NN).astype(q_ref.dtype)
        off = 0
        for lo, hi in REST_COLS:
            for j in range(lo, hi):
                r_ref[:, off:off + 512] = _dot(a, _w512(w_ref, j), NN)
                off += 512

    return pl.pallas_call(
        body, name="proj", grid=(S // tm,),
        in_specs=[pl.BlockSpec((tm, D), lambda i: (i, 0)), _resident((8, D, 1024))],
        out_specs=[pl.BlockSpec((tm, QKV_W), lambda i: (i, 0)), pl.BlockSpec((tm, REST_W), lambda i: (i, 0))],
        out_shape=[_sds((S, QKV_W), CDT), _sds((S, REST_W), F32)],
        compiler_params=_cp(("parallel",), vmem=56),
    )(h, w)


def _after(dep):
    return ([], []) if dep is None else ([pl.BlockSpec(memory_space=pl.ANY)], [dep])


def _dh(dqkv, drest, w, dep=None):
    S = dqkv.shape[0]
    tm = 512
    dep_specs, dep_args = _after(dep)

    def body(a_ref, b_ref, w_ref, *rest):
        o_ref = rest[-1]
        acc = _dot(a_ref[:, 0:512], _w512(w_ref, 0), NT)
        for j in range(1, 9):
            acc = acc + _dot(a_ref[:, 512 * j:512 * (j + 1)], _w512(w_ref, j), NT)
        off = 0
        for lo, hi in REST_COLS:
            for j in range(lo, hi):
                acc = acc + _dot(b_ref[:, off:off + 512], _w512(w_ref, j), NT)
                off += 512
        o_ref[...] = acc

    return pl.pallas_call(
        body, name="dh", grid=(S // tm,),
        in_specs=[pl.BlockSpec((tm, QKV_W), lambda i: (i, 0)), pl.BlockSpec((tm, REST_W), lambda i: (i, 0)),
                  _resident((8, D, 1024))] + dep_specs,
        out_specs=pl.BlockSpec((tm, D), lambda i: (i, 0)),
        out_shape=_sds((S, D), F32),
        compiler_params=_cp(("parallel",), vmem=56),
    )(dqkv, drest, w, *dep_args)


def _mm_tn(at, b, *, n_total, out_blk, name, prev=None, tn=512, with_low=False):
    R, S = at.shape
    nb = b.shape[1] // tn
    n_out = 2 if with_low else 1

    def body(*refs):
        res = _dot(refs[0][...], refs[1][...], NN)
        refs[-n_out][...] = res
        if with_low:
            refs[-1][...] = res.astype(CDT)

    in_specs = [_resident((R, S)), pl.BlockSpec((S, tn), lambda j: (0, j))]
    args = [at, b]
    aliases = {}
    if prev is not None:
        prev = list(prev) if with_low else [prev]
        in_specs += [pl.BlockSpec(memory_space=pl.ANY)] * n_out
        args += prev
        aliases = {2 + k: k for k in range(n_out)}
    out_spec = pl.BlockSpec((R, tn), lambda j: (0, out_blk(j)))
    outs = pl.pallas_call(
        body, name=name, grid=(nb,), in_specs=in_specs,
        out_specs=[out_spec] * n_out,
        out_shape=[_sds((R, n_total), F32), _sds((R, n_total), CDT)][:n_out],
        input_output_aliases=aliases,
        compiler_params=_cp(("parallel",)),
    )(*args)
    return tuple(outs) if with_low else outs[0]


def _tcast(a, name):
    M, C = a.shape
    tm = 512

    def body(a_ref, o_ref):
        o_ref[...] = a_ref[...].astype(F32).T.astype(o_ref.dtype)

    return pl.pallas_call(
        body, name=name, grid=(M // tm,),
        in_specs=[pl.BlockSpec((tm, C), lambda i: (i, 0))],
        out_specs=pl.BlockSpec((C, tm), lambda i: (0, i)),
        out_shape=_sds((C, M), CDT), compiler_params=_cp(("parallel",)),
    )(a)


TOK = R16 * R16


def _swap16():
    i = lax.broadcasted_iota(jnp.int32, (TOK, TOK), 0)
    j = lax.broadcasted_iota(jnp.int32, (TOK, TOK), 1)
    return jnp.where(j == jnp.bitwise_and(i, 15) * 16 + jnp.right_shift(i, 4), 1.0, 0.0).astype(CDT)


def _norm_mod(x, norm_g, shift, scale, dep=None):
    S = x.shape[0]
    dep_specs, dep_args = _after(dep)

    def body(x_ref, g_ref, sh_ref, sc_ref, *rest):
        xv = x_ref[...]
        xn = xv * lax.rsqrt(jnp.mean(xv * xv, axis=-1, keepdims=True) + EPS)
        h = (xn * (g_ref[...] * (1.0 + sc_ref[...])) + sh_ref[...]).astype(CDT)
        _st(rest[-1], _dot(_swap16(), h, NN))

    vec = pl.BlockSpec((1, D), lambda i: (0, 0))
    return pl.pallas_call(
        body, name="norm_mod", grid=(S // TOK,),
        in_specs=[pl.BlockSpec((TOK, D), lambda i: (i, 0)), vec, vec, vec] + dep_specs,
        out_specs=pl.BlockSpec((R16, R16, D), lambda i: (0, i, 0)),
        out_shape=_sds((R16, S // R16, D), CDT), compiler_params=_cp(("parallel",)),
    )(x, norm_g, shift, scale, *dep_args)


def _t5_bucket_np(n):
    nf = np.maximum(n, 1).astype(np.float32)
    large = 16 + (np.log(nf / np.float32(16)) / np.float32(math.log(2048 / 16)) * np.float32(16)).astype(np.int32)
    large = np.minimum(large, 31)
    return np.where(n < 16, n, large).astype(np.int32)


class _Geom:
    def __init__(self, gi, LS):
        self.gi, self.LS = gi, LS
        self.dil = (1, 4, 16)[gi]
        if gi == 0:
            self.lead, self.rows, self.n_tiles, self.has_prev = (R16,), 16, LS // 16, True
            a = np.arange(R16)[:, None]
            ll = np.arange(16)[None, :]
            self.pos = (16 * ll + a).reshape(-1)
        elif gi == 1:
            self.lead, self.rows, self.n_tiles, self.has_prev = (4,), 64, LS // 64, True
            a = np.arange(4)[:, None]
            ll = np.arange(64)[None, :]
            self.pos = (4 * ll + a).reshape(-1)
        else:
            self.lead, self.rows, self.n_tiles, self.has_prev = (), LS, 1, False
            self.pos = np.arange(LS)
        self.T = self.pos.shape[0]
        self.Tk = 2 * self.T if self.has_prev else self.T
        self.outer = {0: (), 1: (4,), 2: (R16,)}[gi]

    def view(self, arr):
        S, C = arr.shape
        if self.gi == 1:
            return arr.reshape(4, 4, self.LS, C)
        return arr.reshape(R16, self.LS, C)

    def view_shape(self, C):
        return (4, 4, self.LS, C) if self.gi == 1 else (R16, self.LS, C)

    def unview(self, arr):
        return arr.reshape(R16 * self.LS, arr.shape[-1])

    def spec(self, W, cblk, tile):
        if self.gi == 0:
            return pl.BlockSpec((R16, 16, W), lambda n: (0, tile(n), cblk))
        if self.gi == 1:
            return pl.BlockSpec((4, None, 64, W), lambda r, n: (0, r, tile(n), cblk))
        return pl.BlockSpec((None, self.LS, W), lambda r: (r, 0, cblk))

    def halves(self):
        R = self.rows
        idx = np.arange(self.T).reshape(-1, R)
        return idx[:, :R // 2].reshape(-1), idx[:, R // 2:].reshape(-1)

    def bucket_index(self, sub):
        lo, hi = self.halves()
        pos = self.pos
        if sub == 0:
            pq = pos[lo]
            pk = np.concatenate([pos[hi] - self.T, pos[lo]]) if self.has_prev else np.concatenate([pos[lo], pos[hi]])
        else:
            pq = pos[hi]
            pk = np.concatenate([pos[lo], pos[hi]])
        dist = pq[:, None] - pk[None, :]
        ok = (dist >= 0) & (dist <= N_BACK)
        bucket = _t5_bucket_np(np.clip(dist, 0, N_BACK) * self.dil)
        return np.where(ok, bucket, -1).astype(np.int32)


def _bias_table(idx, buckets, rel_bias, gi, sub, dep=None):
    T, Tk = idx.shape
    tr = 32
    dep_specs, dep_args = _after(dep)

    def body(idx_ref, rb_ref, *rest):
        h = pl.program_id(0)
        ix = idx_ref[...]
        acc = jnp.full(ix.shape, NEG, F32)
        for b in buckets:
            acc = jnp.where(ix == b, rb_ref[b, gi * NH + h], acc)
        rest[-1][...] = acc

    return pl.pallas_call(
        body, name=f"bias_table{gi}{'ab'[sub]}", grid=(NH, T // tr),
        in_specs=[pl.BlockSpec((tr, Tk), lambda h, i: (i, 0)), pl.BlockSpec(memory_space=pltpu.SMEM)] + dep_specs,
        out_specs=pl.BlockSpec((None, tr, Tk), lambda h, i: (h, i, 0)),
        out_shape=_sds((NH, T, Tk), F32), compiler_params=_cp(("parallel", "parallel")),
    )(idx, rel_bias, *dep_args)


def _bias_grad(dsum, idx, buckets, gi, sub, dep=None):
    _, T, Tk = dsum.shape
    dep_specs, dep_args = _after(dep)

    def body(ds_ref, idx_ref, *rest):
        o_ref, r_ref = rest[-2], rest[-1]
        r_ref[...] = jnp.zeros_like(r_ref)
        ix, ds = idx_ref[...], ds_ref[...]
        for b in buckets:
            r_ref[b:b + 1, :] = jnp.sum(jnp.where(ix == b, ds, 0.0), axis=0, keepdims=True)
        o_ref[...] = jnp.broadcast_to(jnp.sum(r_ref[...], axis=1, keepdims=True), (32, 128))

    out = pl.pallas_call(
        body, name=f"bias_grad{gi}{'ab'[sub]}", grid=(NH,),
        in_specs=[pl.BlockSpec((None, T, Tk), lambda h: (h, 0, 0)), pl.BlockSpec((T, Tk), lambda h: (0, 0))] + dep_specs,
        out_specs=pl.BlockSpec((None, 32, 128), lambda h: (h, 0, 0)),
        out_shape=_sds((NH, 32, 128), F32), scratch_shapes=[pltpu.VMEM((32, Tk), F32)],
        compiler_params=_cp(("parallel",)),
    )(dsum, idx, *dep_args)
    return out[:, :, 0]


def _prev_mask(n, T, Tk):
    col = lax.broadcasted_iota(jnp.int32, (1, Tk), 1)
    return jnp.where((col < T) & (n == 0), NEG, 0.0).astype(F32)


def _split_rows(v, R):
    A, half = v.shape[0] // R, R // 2
    via_f32 = v.dtype != F32 and half % 16 != 0
    w = v.astype(F32) if via_f32 else v
    lo = jnp.concatenate([w[a * R:a * R + half] for a in range(A)], axis=0)
    hi = jnp.concatenate([w[a * R + half:(a + 1) * R] for a in range(A)], axis=0)
    return (lo.astype(v.dtype), hi.astype(v.dtype)) if via_f32 else (lo, hi)


def _merge_rows(lo, hi, R):
    half = R // 2
    A = lo.shape[0] // half
    return jnp.concatenate([x[a * half:(a + 1) * half] for a in range(A) for x in (lo, hi)], axis=0)


def _attn_fwd(qkv, biases, geom, others=None):
    S = qkv.shape[0]
    gi, T, R, has_prev = geom.gi, geom.T, geom.rows, geom.has_prev
    H2 = T // 2
    qv = geom.view(qkv)
    cq, ck, cv = 3 * gi, 3 * gi + 1, 3 * gi + 2
    assert others is None or not has_prev

    def body(*refs):
        if has_prev:
            q_ref, kp_ref, kc_ref, vp_ref, vc_ref, ba_ref, bb_ref, o_ref, l_ref = refs
        elif others is not None:
            (q_ref, kc_ref, vc_ref, ba_ref, bb_ref, o0_ref, o1_ref, l0_ref, l1_ref, z_ref,
             attn_ref, L_ref, ag_ref) = refs
        else:
            q_ref, kc_ref, vc_ref, ba_ref, bb_ref, o_ref, l_ref = refs
        q_lo, q_hi = _split_rows(_ld(q_ref), R)
        kc_lo, kc_hi = _split_rows(_ld(kc_ref), R)
        vc_lo, vc_hi = _split_rows(_ld(vc_ref), R)
        k_b = jnp.concatenate([kc_lo, kc_hi], axis=0)
        v_b = jnp.concatenate([vc_lo, vc_hi], axis=0)
        if has_prev:
            n = pl.program_id(len(geom.outer))
            k_a = jnp.concatenate([_split_rows(_ld(kp_ref), R)[1], kc_lo], axis=0)
            v_a = jnp.concatenate([_split_rows(_ld(vp_ref), R)[1], vc_lo], axis=0)
            pm = _prev_mask(n, H2, T)
        else:
            k_a, v_a, pm = k_b, v_b, None
        low = lax.broadcasted_iota(jnp.int32, (1, 2 * HD), 1) < HD
        for j in range(NH // 2):
            cols = slice(2 * HD * j, 2 * HD * (j + 1))
            o_half, l_half = [], []
            for q, k, v, b_ref, mask in ((q_lo, k_a, v_a, ba_ref, pm), (q_hi, k_b, v_b, bb_ref, None)):
                q2 = q[:, cols].astype(F32) * 0.125
                k2, v2 = k[:, cols], v[:, cols]
                qs = jnp.concatenate([jnp.where(low, q2, 0.0), jnp.where(low, 0.0, q2)], axis=0).astype(q.dtype)
                s = _dot(qs, k2, NT) + b_ref[j]
                if mask is not None:
                    s = s + mask
                m = jnp.max(s, axis=1, keepdims=True)
                p = jnp.exp(s - m)
                l = jnp.sum(p, axis=1, keepdims=True)
                pv = _dot(p.astype(v.dtype), v2, NN) / l
                lse = m + jnp.log(l)
                o_half.append(jnp.where(low, pv[:H2], pv[H2:]))
                l_half.append(jnp.where(low, lse[:H2], lse[H2:]))
            o2, lc = _merge_rows(o_half[0], o_half[1], R), _merge_rows(l_half[0], l_half[1], R)
            if others is None:
                _st(o_ref, o2, cols)
                _st(l_ref, lc, cols)
                continue
            la, lb = l0_ref[:, cols], l1_ref[:, cols]
            m = jnp.maximum(jnp.maximum(la, lb), lc)
            ea, eb, ec = jnp.exp(la - m), jnp.exp(lb - m), jnp.exp(lc - m)
            den = ea + eb + ec
            attn = (ea * o0_ref[:, cols] + eb * o1_ref[:, cols] + ec * o2) / den
            attn_ref[:, cols] = attn
            L_ref[:, cols] = m + jnp.log(den)
            z = z_ref[:, cols]
            ag = attn * (z * _sigmoid(z))
            ag_ref[:, cols] = ag.astype(ag_ref.dtype)

    cur = lambda n: n
    prev = lambda n: jnp.maximum(n - 1, 0)
    if has_prev:
        in_specs = [geom.spec(AW, cq, cur), geom.spec(AW, ck, prev), geom.spec(AW, ck, cur),
                    geom.spec(AW, cv, prev), geom.spec(AW, cv, cur)]
        args = [qv] * 5
    else:
        in_specs = [geom.spec(AW, cq, cur), geom.spec(AW, ck, cur), geom.spec(AW, cv, cur)]
        args = [qv] * 3
    nd = len(geom.outer) + (1 if has_prev else 0)
    biases = [b.reshape(NH // 2, 2 * H2, b.shape[2]) for b in biases]
    in_specs += [pl.BlockSpec(b.shape, lambda *ids: (0, 0, 0)) for b in biases]
    grid = geom.outer + ((geom.n_tiles,) if has_prev else ())
    oshape = geom.view_shape(AW)
    tile = geom.spec(AW, 0, cur)
    if others is None:
        o, l = pl.pallas_call(
            body, name=f"attn_fwd{gi}", grid=grid, in_specs=in_specs, out_specs=[tile, tile],
            out_shape=[_sds(oshape, F32), _sds(oshape, F32)],
            compiler_params=_cp(("arbitrary",) * nd),
        )(*args, *biases)
        return geom.unview(o), geom.unview(l)
    os_, ls_, rest = others
    attn, Lb, ag = pl.pallas_call(
        body, name=f"attn_fwd{gi}_merge", grid=grid, in_specs=in_specs + [tile] * 5,
        out_specs=[tile, tile, tile],
        out_shape=[_sds(oshape, F32), _sds(oshape, F32), _sds(oshape, CDT)],
        compiler_params=_cp(("arbitrary",) * nd),
    )(*args, *biases, *[geom.view(a) for a in (*os_, *ls_, rest)])
    return geom.unview(attn), geom.unview(Lb), geom.unview(ag)


def _pool_counts(r, LS, win):
    l = lax.broadcasted_iota(jnp.int32, (LS, 1), 0)
    return jnp.minimum(16 * l + (r + 1), win).astype(F32)


def _pool_fwd(rest, pool_w, pool_scale):
    S = rest.shape[0]
    LS = S // R16
    r3 = rest.reshape(R16, LS, REST_W)
    PG = 128

    def body(u_ref, z_ref, pw_ref, ps_ref, pt_ref, mp_ref, pg_ref, e_ref):
        g = pl.program_id(0)
        win = jnp.left_shift(2, g)
        row = lax.broadcasted_iota(jnp.int32, (LS, PG), 0)
        for r in range(R16):
            e_ref[R16 + r] = u_ref[r]
        for r in range(1, R16):
            e_ref[r] = jnp.where(row == 0, 0.0, pltpu.roll(u_ref[r], 1, 0))

        def double(s):
            for i in range(2 * R16 - 1, 2 * s - 1, -1):
                e_ref[i] = e_ref[i] + e_ref[i - s]

        double(1)
        for k in (1, 2, 3):
            pl.when(g >= k)(functools.partial(double, 2 ** k))
        pw = pw_ref[...].astype(CDT)
        for r in range(R16):
            u = u_ref[r]
            pooled = e_ref[R16 + r] / _pool_counts(r, LS, win) - u
            pc = pooled.astype(CDT)
            mp = _dot(pc, pw, NN)
            z = z_ref[r]
            pg = (mp * ps_ref[...]) * (z * _sigmoid(z))
            mp_ref[r] = mp
            pg_ref[r] = pg.astype(pg_ref.dtype)
            pt_ref[:, r * LS:(r + 1) * LS] = pooled.T.astype(pt_ref.dtype)

    col = lambda off: pl.BlockSpec((R16, LS, PG), lambda g: (0, 0, off + g))
    tsp = pl.BlockSpec((PG, S), lambda g: (g, 0))
    pt, mp, pg = pl.pallas_call(
        body, name="pool_fwd", grid=(4,),
        in_specs=[col(24), col(4), pl.BlockSpec((None, PG, PG), lambda g: (g, 0, 0)),
                  pl.BlockSpec((1, PG), lambda g: (0, g))],
        out_specs=[tsp, col(0), col(0)],
        out_shape=[_sds((AW, S), CDT), _sds((R16, LS, AW), F32), _sds((R16, LS, AW), CDT)],
        scratch_shapes=[pltpu.VMEM((2 * R16, LS, PG), F32)],
        compiler_params=_cp(("parallel",)),
    )(r3, r3, pool_w, pool_scale)
    return pt, mp.reshape(S, AW), pg.reshape(S, AW)


def _branch_merge(ag, pg, wab, wpb, rest):
    S = ag.shape[0]
    tm, tn = 512, 512

    def body(ag_ref, pg_ref, wa_ref, wp_ref, ga_ref, gp_ref, m_ref, mt_ref):
        ya = _dot(ag_ref[...], wa_ref[...], NN)
        yp = _dot(pg_ref[...], wp_ref[...], NN)
        mg = _sigmoid(ga_ref[...]) * ya + _sigmoid(gp_ref[...]) * yp
        m_ref[...] = mg.astype(m_ref.dtype)
        mt_ref[...] = mg.T.astype(mt_ref.dtype)

    a = pl.BlockSpec((tm, AW), lambda i, j: (i, 0))
    w = pl.BlockSpec((AW, tn), lambda i, j: (0, j))
    o = pl.BlockSpec((tm, tn), lambda i, j: (i, j))
    return pl.pallas_call(
        body, name="branch_merge", grid=(S // tm, D // tn),
        in_specs=[a, a, w, w, pl.BlockSpec((tm, tn), lambda i, j: (i, 2 + j)),
                  pl.BlockSpec((tm, tn), lambda i, j: (i, 4 + j))],
        out_specs=[o, pl.BlockSpec((tn, tm), lambda i, j: (j, i))],
        out_shape=[_sds((S, D), CDT), _sds((D, S), CDT)],
        compiler_params=_cp(("parallel", "parallel")),
    )(ag, pg, wab, wpb, rest, rest)


def _merge_out_loss(ag, pg, rest, wab, wpb, wout, x, target, gate, final_g):
    S = x.shape[0]
    LS = S // R16

    def body(ag_ref, pg_ref, ga_ref, gp_ref, wa_ref, wp_ref, w_ref, x_ref, t_ref, gate_ref, fg_ref,
             dx2_ref, dmo_ref, loss_ref, gfg_ref, dgate_ref, dw_ref):
        i = pl.program_id(0)

        @pl.when(i == 0)
        def _():
            loss_ref[...] = jnp.zeros_like(loss_ref)
            gfg_ref[...] = jnp.zeros_like(gfg_ref)
            dgate_ref[...] = jnp.zeros_like(dgate_ref)
            dw_ref[...] = jnp.zeros_like(dw_ref)

        swap = _swap16()
        ya = _dot(_ld(ag_ref), wa_ref[...], NN)
        yp = _dot(_ld(pg_ref), wp_ref[...], NN)
        mg = (_sigmoid(_ld(ga_ref)) * ya + _sigmoid(_ld(gp_ref)) * yp).astype(CDT)
        m_nat = _dot(swap, mg, NN)
        mo = _dot(m_nat.astype(CDT), w_ref[...], NN)
        gate, fg = gate_ref[...], fg_ref[...]
        x2 = x_ref[...] + gate * mo
        r2 = lax.rsqrt(jnp.mean(x2 * x2, axis=-1, keepdims=True) + EPS)
        xn2 = x2 * r2
        diff = xn2 * fg - t_ref[...]
        loss = jnp.sum(jnp.sum(diff * diff, axis=-1, keepdims=True), axis=0, keepdims=True)
        dy = diff * (1.0 / D)
        dxn2 = dy * fg
        dx2 = r2 * (dxn2 - xn2 * jnp.mean(dxn2 * xn2, axis=-1, keepdims=True))
        dx2_ref[...] = dx2
        dmo = (dx2 * gate).astype(CDT)
        _st(dmo_ref, _dot(swap, dmo, NN))
        loss_ref[...] += jnp.broadcast_to(loss * (0.5 / D), loss_ref.shape)
        gfg_ref[...] += jnp.sum(dy * xn2, axis=0, keepdims=True)
        dgate_ref[...] += jnp.sum(dx2 * mo, axis=0, keepdims=True)
        dw_ref[...] += _dot(m_nat.T.astype(CDT), dmo, NN)

    def slab(w, cblk):
        return pl.BlockSpec((R16, R16, w), lambda i: (0, i, cblk))

    nat = pl.BlockSpec((TOK, D), lambda i: (i, 0))
    vec = pl.BlockSpec((1, D), lambda i: (0, 0))
    r3 = rest.reshape(R16, LS, REST_W)
    return pl.pallas_call(
        body, name="merge_out_loss", grid=(S // TOK,),
        in_specs=[slab(AW, 0), slab(AW, 0), slab(D, 1), slab(D, 2), _resident((AW, D)), _resident((AW, D)),
                  _resident((D, D)), nat, nat, vec, vec],
        out_specs=[nat, slab(D, 0), pl.BlockSpec((1, 128), lambda i: (0, 0)), vec, vec,
                   pl.BlockSpec((D, D), lambda i: (0, 0))],
        out_shape=[_sds((S, D), F32), _sds((R16, LS, D), CDT), _sds((1, 128), F32), _sds((1, D), F32),
                   _sds((1, D), F32), _sds((D, D), F32)],
        compiler_params=_cp(("arbitrary",)),
    )(ag.reshape(R16, LS, AW), pg.reshape(R16, LS, AW), r3, r3, wab, wpb, wout, x, target, gate, final_g)


def _out_loss(merged, wout, x, target, gate, final_g):
    S = x.shape[0]

    def body(m_ref, w_ref, x_ref, t_ref, gate_ref, fg_ref, dx2_ref, dmo_ref, loss_ref, gfg_ref, dgate_ref):
        i = pl.program_id(0)

        @pl.when(i == 0)
        def _():
            loss_ref[...] = jnp.zeros_like(loss_ref)
            gfg_ref[...] = jnp.zeros_like(gfg_ref)
            dgate_ref[...] = jnp.zeros_like(dgate_ref)

        swap = _swap16()
        mo = _dot(_dot(swap, _ld(m_ref), NN).astype(CDT), w_ref[...], NN)
        gate, fg = gate_ref[...], fg_ref[...]
        x2 = x_ref[...] + gate * mo
        r2 = lax.rsqrt(jnp.mean(x2 * x2, axis=-1, keepdims=True) + EPS)
        xn2 = x2 * r2
        diff = xn2 * fg - t_ref[...]
        loss = jnp.sum(jnp.sum(diff * diff, axis=-1, keepdims=True), axis=0, keepdims=True)
        dy = diff * (1.0 / D)
        dxn2 = dy * fg
        dx2 = r2 * (dxn2 - xn2 * jnp.mean(dxn2 * xn2, axis=-1, keepdims=True))
        dx2_ref[...] = dx2
        _st(dmo_ref, _dot(swap, (dx2 * gate).astype(CDT), NN))
        loss_ref[...] += jnp.broadcast_to(loss * (0.5 / D), loss_ref.shape)
        gfg_ref[...] += jnp.sum(dy * xn2, axis=0, keepdims=True)
        dgate_ref[...] += jnp.sum(dx2 * mo, axis=0, keepdims=True)

    slab = pl.BlockSpec((R16, R16, D), lambda i: (0, i, 0))
    nat = pl.BlockSpec((TOK, D), lambda i: (i, 0))
    vec = pl.BlockSpec((1, D), lambda i: (0, 0))
    return pl.pallas_call(
        body, name="out_loss", grid=(S // TOK,),
        in_specs=[slab, _resident((D, D)), nat, nat, vec, vec],
        out_specs=[nat, slab, pl.BlockSpec((1, 128), lambda i: (0, 0)), vec, vec],
        out_shape=[_sds((S, D), F32), _sds((R16, S // R16, D), CDT), _sds((1, 128), F32), _sds((1, D), F32),
                   _sds((1, D), F32)],
        compiler_params=_cp(("arbitrary",)),
    )(merged, wout, x, target, gate, final_g)


def _bwd_gates(dmo, wout, wab, wpb, rest, ag, pg, attn, mp, pool_scale):
    S = dmo.shape[0]
    tm = 256

    def body(dmo_ref, wo_ref, wa_ref, wp_ref, za_ref, zp_ref, ga0, ga1, gp0, gp1, ag_ref, pg_ref, at_ref, mp_ref,
             ps_ref, dwa_ref, dwp_ref, dr_ref, dat_ref, dl_ref, dmp_ref, dps_ref):
        i = pl.program_id(0)

        @pl.when(i == 0)
        def _():
            dps_ref[...] = jnp.zeros_like(dps_ref)
            dwa_ref[...] = jnp.zeros_like(dwa_ref)
            dwp_ref[...] = jnp.zeros_like(dwp_ref)

        dm = _dot(dmo_ref[...], wo_ref[...], NT)
        sa = _sigmoid(jnp.concatenate([ga0[...], ga1[...]], axis=1))
        sp = _sigmoid(jnp.concatenate([gp0[...], gp1[...]], axis=1))
        dya = (dm * sa).astype(CDT)
        dyp = (dm * sp).astype(CDT)
        ag, pg = ag_ref[...], pg_ref[...]
        dwa_ref[...] += _dot(ag.astype(F32).T.astype(CDT), dya, NN)
        dwp_ref[...] += _dot(pg.astype(F32).T.astype(CDT), dyp, NN)
        ya = _dot(ag, wa_ref[...], NN)
        yp = _dot(pg, wp_ref[...], NN)
        dr_ref[:, 1024:2048] = (dm * ya * sa * (1.0 - sa)).astype(dr_ref.dtype)
        dr_ref[:, 2048:3072] = (dm * yp * sp * (1.0 - sp)).astype(dr_ref.dtype)
        da = _dot(dya, wa_ref[...], NT)
        dp = _dot(dyp, wp_ref[...], NT)
        za = za_ref[...]
        sga = _sigmoid(za)
        attn = at_ref[...]
        dattn = da * (za * sga)
        dr_ref[:, 0:512] = (da * attn * (sga * (1.0 + za * (1.0 - sga)))).astype(dr_ref.dtype)
        dat_ref[...] = dattn.astype(dat_ref.dtype)
        prod = dattn * attn
        low = lax.broadcasted_iota(jnp.int32, (1, 2 * HD), 1) < HD
        for j in range(NH // 2):
            x2 = prod[:, 2 * HD * j:2 * HD * (j + 1)]
            s0 = jnp.sum(jnp.where(low, x2, 0.0), axis=1, keepdims=True)
            s1 = jnp.sum(jnp.where(low, 0.0, x2), axis=1, keepdims=True)
            dl_ref[:, 2 * HD * j:2 * HD * (j + 1)] = jnp.where(low, s0, s1)
        zp = zp_ref[...]
        sgp = _sigmoid(zp)
        mpre = mp_ref[...]
        ps = ps_ref[...]
        dmixed = dp * (zp * sgp)
        dr_ref[:, 512:1024] = (dp * (mpre * ps) * (sgp * (1.0 + zp * (1.0 - sgp)))).astype(dr_ref.dtype)
        dps_ref[...] += jnp.sum(dmixed * mpre, axis=0, keepdims=True)
        dmp_ref[...] = (dmixed * ps).astype(dmp_ref.dtype)

    full = lambda shape: pl.BlockSpec(shape, lambda i: (0, 0))
    tD = pl.BlockSpec((tm, D), lambda i: (i, 0))
    tA = pl.BlockSpec((tm, AW), lambda i: (i, 0))
    rb = lambda c: pl.BlockSpec((tm, AW), lambda i: (i, c))
    return pl.pallas_call(
        body, name="bwd_gates", grid=(S // tm,),
        in_specs=[tD, full((D, D)), full((AW, D)), full((AW, D)), rb(0), rb(1), rb(2), rb(3), rb(4), rb(5),
                  tA, tA, tA, tA, full((1, AW))],
        out_specs=[full((AW, D)), full((AW, D)), pl.BlockSpec((tm, 3072), lambda i: (i, 0)), tA, tA, tA, full((1, AW))],
        out_shape=[_sds((AW, D), F32), _sds((AW, D), F32), _sds((S, REST_W), CDT), _sds((S, AW), CDT),
                   _sds((S, AW), F32), _sds((S, AW), CDT), _sds((1, AW), F32)],
        compiler_params=_cp(("arbitrary",), vmem=56),
    )(dmo, wout, wab, wpb, rest, rest, rest, rest, rest, rest, ag, pg, attn, mp, pool_scale)


def _pool_bwd(dmp, pt, pool_w, drest):
    S = dmp.shape[0]
    LS = S // R16
    PG = 128
    d3 = dmp.reshape(R16, LS, AW)
    dr3 = drest.reshape(R16, LS, REST_W)

    def body(d_ref, pt_ref, pw_ref, dr_in, du_ref, dpw_ref, g_ref, s_ref):
        del dr_in
        g = pl.program_id(0)
        win = jnp.left_shift(2, g)
        row = lax.broadcasted_iota(jnp.int32, (LS, PG), 0)
        pw = pw_ref[...].astype(CDT)
        dpw = jnp.zeros((PG, PG), F32)
        for r in range(R16):
            dm = d_ref[r]
            dpw = dpw + _dot(pt_ref[:, r * LS:(r + 1) * LS], dm, NN)
            dpooled = _dot(dm, pw, NT)
            g_ref[r] = dpooled
            sc = dpooled / _pool_counts(r, LS, win)
            s_ref[r] = sc
            if r < R16 - 1:
                s_ref[R16 + r] = jnp.where(row == LS - 1, 0.0, pltpu.roll(sc, LS - 1, 0))
        dpw_ref[...] = dpw

        def double(s):
            for i in range(0, 2 * R16 - 2 * s):
                s_ref[i] = s_ref[i] + s_ref[i + s]

        double(1)
        for k in (1, 2, 3):
            pl.when(g >= k)(functools.partial(double, 2 ** k))
        for r in range(R16):
            du_ref[r] = (s_ref[r] - g_ref[r]).astype(du_ref.dtype)

    du, dpw = pl.pallas_call(
        body, name="pool_bwd", grid=(4,),
        in_specs=[pl.BlockSpec((R16, LS, PG), lambda g: (0, 0, g)), pl.BlockSpec((PG, S), lambda g: (g, 0)),
                  pl.BlockSpec((None, PG, PG), lambda g: (g, 0, 0)), pl.BlockSpec(memory_space=pl.ANY)],
        out_specs=[pl.BlockSpec((R16, LS, PG), lambda g: (0, 0, 24 + g)),
                   pl.BlockSpec((None, PG, PG), lambda g: (g, 0, 0))],
        out_shape=[_sds((R16, LS, REST_W), CDT), _sds((4, PG, PG), F32)],
        scratch_shapes=[pltpu.VMEM((R16, LS, PG), F32), pltpu.VMEM((2 * R16, LS, PG), F32)],
        input_output_aliases={3: 0},
        compiler_params=_cp(("parallel",)),
    )(d3, pt, pool_w, dr3)
    return du.reshape(S, REST_W), dpw


def _attn_bwd(qkv, dattn, Lb, dlb, biases, geom, dqkv_prev):
    S = qkv.shape[0]
    gi, T, R, has_prev, N = geom.gi, geom.T, geom.rows, geom.has_prev, geom.n_tiles
    H2 = T // 2
    nd_outer = len(geom.outer)
    qv = geom.view(qkv)
    cq, ck, cv = 3 * gi, 3 * gi + 1, 3 * gi + 2
    W3 = 3 * AW

    def heads(q_ref, kp_ref, kc_ref, vp_ref, vc_ref, do_ref, L_ref, dl_ref, ba_ref, bb_ref, pm, dsa_ref, dsb_ref, emit):
        q_lo, q_hi = _split_rows(_ld(q_ref), R)
        do_lo, do_hi = _split_rows(_ld(do_ref), R)
        L_lo, L_hi = _split_rows(_ld(L_ref), R)
        dl_lo, dl_hi = _split_rows(_ld(dl_ref), R)
        kc_lo, kc_hi = _split_rows(_ld(kc_ref), R)
        vc_lo, vc_hi = _split_rows(_ld(vc_ref), R)
        k_b = jnp.concatenate([kc_lo, kc_hi], axis=0)
        v_b = jnp.concatenate([vc_lo, vc_hi], axis=0)
        if has_prev:
            k_a = jnp.concatenate([_split_rows(_ld(kp_ref), R)[1], kc_lo], axis=0)
            v_a = jnp.concatenate([_split_rows(_ld(vp_ref), R)[1], vc_lo], axis=0)
        else:
            k_a, v_a = k_b, v_b
        low = lax.broadcasted_iota(jnp.int32, (1, 2 * HD), 1) < HD
        for j in range(NH // 2):
            sl = slice(2 * HD * j, 2 * HD * (j + 1))
            res = []
            for q, k, v, do, L, dl, b_ref, mask, ds_ref in (
                    (q_lo, k_a, v_a, do_lo, L_lo, dl_lo, ba_ref, pm, dsa_ref),
                    (q_hi, k_b, v_b, do_hi, L_hi, dl_hi, bb_ref, None, dsb_ref)):
                q2 = q[:, sl].astype(F32) * 0.125
                do2 = do[:, sl].astype(F32)
                k2, v2 = k[:, sl], v[:, sl]
                L2, dl2 = L[:, sl], dl[:, sl]
                qs = jnp.concatenate([jnp.where(low, q2, 0.0), jnp.where(low, 0.0, q2)], axis=0).astype(q.dtype)
                dom = jnp.concatenate([jnp.where(low, do2, 0.0), jnp.where(low, 0.0, do2)], axis=0).astype(q.dtype)
                Lc = jnp.concatenate([L2[:, 0:1], L2[:, HD:HD + 1]], axis=0)
                dlc = jnp.concatenate([dl2[:, 0:1], dl2[:, HD:HD + 1]], axis=0)
                s = _dot(qs, k2, NT) + b_ref[j]
                if mask is not None:
                    s = s + mask
                p = jnp.exp(s - Lc)
                ds = p * (_dot(dom, v2, NT) - dlc)
                ds_ref[j] += ds
                dsc = ds.astype(q.dtype)
                dq_st = _dot(dsc, k2, NN)
                res.append((jnp.where(low, dq_st[:H2], dq_st[H2:]) * 0.125, _dot(dsc, qs, TN),
                            _dot(p.astype(q.dtype), dom, TN)))
            (dq_lo, dk_a, dv_a), (dq_hi, dk_b, dv_b) = res
            dq = _merge_rows(dq_lo, dq_hi, R)
            if has_prev:
                emit(sl, dq, _merge_rows(dk_a[H2:] + dk_b[:H2], dk_b[H2:], R),
                     _merge_rows(dv_a[H2:] + dv_b[:H2], dv_b[H2:], R), dk_a[:H2], dv_a[:H2])
            else:
                emit(sl, dq, _merge_rows(dk_a[:H2] + dk_b[:H2], dk_a[H2:] + dk_b[H2:], R),
                     _merge_rows(dv_a[:H2] + dv_b[:H2], dv_a[H2:] + dv_b[H2:], R), None, None)

    def first_step():
        ok = pl.program_id(nd_outer) == 0 if has_prev else pl.program_id(0) == 0
        for a in range(nd_outer if has_prev else 0):
            ok = ok & (pl.program_id(a) == 0)
        return ok

    if has_prev:
        def body(q_ref, kp_ref, kc_ref, vp_ref, vc_ref, do_ref, L_ref, dl_ref, ba_ref, bb_ref, prev_in, out_ref,
                 dsa_ref, dsb_ref, hold, new, pp):
            del prev_in
            s_id = pl.program_id(nd_outer)

            @pl.when(first_step())
            def _():
                dsa_ref[...] = jnp.zeros_like(dsa_ref)
                dsb_ref[...] = jnp.zeros_like(dsb_ref)

            @pl.when(s_id < N)
            def _():
                def emit(sl, dq, dk, dv, dk_prev, dv_prev):
                    new[:, sl] = dq
                    new[:, slice(AW + sl.start, AW + sl.stop)] = dk
                    new[:, slice(2 * AW + sl.start, 2 * AW + sl.stop)] = dv
                    pp[:, sl] = dk_prev
                    pp[:, slice(AW + sl.start, AW + sl.stop)] = dv_prev

                heads(q_ref, kp_ref, kc_ref, vp_ref, vc_ref, do_ref, L_ref, dl_ref, ba_ref, bb_ref,
                      _prev_mask(s_id, H2, T), dsa_ref, dsb_ref, emit)

            @pl.when((s_id >= 1) & (s_id < N))
            def _():
                _st(out_ref, hold[:, 0:AW], slice(0, AW))
                late = _merge_rows(jnp.zeros((H2, 2 * AW), F32), pp[...], R)
                _st(out_ref, hold[:, AW:W3] + late, slice(AW, W3))

            @pl.when(s_id == N)
            def _():
                _st(out_ref, hold[...])

            @pl.when(s_id < N)
            def _():
                hold[...] = new[...]

        cur = lambda n: jnp.minimum(n, N - 1)
        prev = lambda n: jnp.clip(n - 1, 0, N - 1)
        delayed = lambda n: jnp.maximum(n - 1, 0)
        in_specs = [geom.spec(AW, cq, cur), geom.spec(AW, ck, prev), geom.spec(AW, ck, cur),
                    geom.spec(AW, cv, prev), geom.spec(AW, cv, cur),
                    geom.spec(AW, 0, cur), geom.spec(AW, 0, cur), geom.spec(AW, 0, cur)]
        args = [qv] * 5 + [geom.view(dattn), geom.view(Lb), geom.view(dlb)]
        out_spec = geom.spec(W3, gi, delayed)
        grid = geom.outer + (N + 1,)
        scratch = [pltpu.VMEM((T, W3), F32), pltpu.VMEM((T, W3), F32), pltpu.VMEM((H2, 2 * AW), F32)]
    else:
        def body(q_ref, kc_ref, vc_ref, do_ref, L_ref, dl_ref, ba_ref, bb_ref, prev_in, out_ref, dsa_ref, dsb_ref):
            del prev_in

            @pl.when(first_step())
            def _():
                dsa_ref[...] = jnp.zeros_like(dsa_ref)
                dsb_ref[...] = jnp.zeros_like(dsb_ref)

            def emit(sl, dq, dk, dv, dk_prev, dv_prev):
                _st(out_ref, dq, sl)
                _st(out_ref, dk, slice(AW + sl.start, AW + sl.stop))
                _st(out_ref, dv, slice(2 * AW + sl.start, 2 * AW + sl.stop))

            heads(q_ref, None, kc_ref, None, vc_ref, do_ref, L_ref, dl_ref, ba_ref, bb_ref, None, dsa_ref, dsb_ref,
                  emit)

        cur = lambda n: n
        in_specs = [geom.spec(AW, cq, cur), geom.spec(AW, ck, cur), geom.spec(AW, cv, cur),
                    geom.spec(AW, 0, cur), geom.spec(AW, 0, cur), geom.spec(AW, 0, cur)]
        args = [qv] * 3 + [geom.view(dattn), geom.view(Lb), geom.view(dlb)]
        out_spec = geom.spec(W3, gi, cur)
        grid = geom.outer
        scratch = []

    whole = lambda b: pl.BlockSpec(b.shape, lambda *ids: (0, 0, 0))
    biases = [b.reshape(NH // 2, 2 * H2, b.shape[2]) for b in biases]
    in_specs += [whole(b) for b in biases]
    in_specs.append(pl.BlockSpec(memory_space=pl.ANY))
    if dqkv_prev is None:
        last, aliases = biases[0], {}
    else:
        last, aliases = geom.view(dqkv_prev), {len(in_specs) - 1: 0}
    dq, dsa, dsb = pl.pallas_call(
        body, name=f"attn_bwd{gi}", grid=grid, in_specs=in_specs,
        out_specs=[out_spec] + [whole(b) for b in biases],
        out_shape=[_sds(geom.view_shape(QKV_W), CDT)] + [_sds(b.shape, F32) for b in biases],
        scratch_shapes=scratch,
        input_output_aliases=aliases,
        compiler_params=_cp(("arbitrary",) * len(grid), vmem=56),
    )(*args, *biases, last)
    return geom.unview(dq), [d.reshape(NH, H2, d.shape[2]) for d in (dsa, dsb)]


def _h_bwd(dh, dx2, x, norm_g, scale):
    S = x.shape[0]

    def body(dh_ref, dx2_ref, x_ref, g_ref, sc_ref, gx_ref, dsh_ref, dsc_ref, dng_ref, nat_ref):
        i = pl.program_id(0)

        @pl.when(i == 0)
        def _():
            dsh_ref[...] = jnp.zeros_like(dsh_ref)
            dsc_ref[...] = jnp.zeros_like(dsc_ref)
            dng_ref[...] = jnp.zeros_like(dng_ref)

        for r in range(R16):
            nat_ref[:, r, :] = dh_ref[r]
        dh = jnp.concatenate([nat_ref[l] for l in range(R16)], axis=0)
        g = g_ref[...]
        one_sc = 1.0 + sc_ref[...]
        xv = x_ref[...]
        r1 = lax.rsqrt(jnp.mean(xv * xv, axis=-1, keepdims=True) + EPS)
        xn = xv * r1
        dhx = dh * xn
        dxn = dh * (g * one_sc)
        dx1 = r1 * (dxn - xn * jnp.mean(dxn * xn, axis=-1, keepdims=True))
        gx_ref[...] = dx2_ref[...] + dx1
        dsh_ref[...] += jnp.sum(dh, axis=0, keepdims=True)
        dsc_ref[...] += jnp.sum(dhx * g, axis=0, keepdims=True)
        dng_ref[...] += jnp.sum(dhx * one_sc, axis=0, keepdims=True)

    nat = pl.BlockSpec((TOK, D), lambda i: (i, 0))
    vec = pl.BlockSpec((1, D), lambda i: (0, 0))
    return pl.pallas_call(
        body, name="h_bwd", grid=(S // TOK,),
        in_specs=[pl.BlockSpec((R16, R16, D), lambda i: (0, i, 0)), nat, nat, vec, vec],
        out_specs=[nat, vec, vec, vec],
        out_shape=[_sds((S, D), F32), _sds((1, D), F32), _sds((1, D), F32), _sds((1, D), F32)],
        scratch_shapes=[pltpu.VMEM((R16, R16, D), F32)],
        compiler_params=_cp(("arbitrary",)),
    )(dh, dx2, x, norm_g, scale)


def _local_step(x, target, shift, scale, gate, norm_g, w_in_full, pool_w, pool_scale, wab, wpb, wout, rel_bias,
                final_g):
    st = _local_pre(x, shift, scale, norm_g, rel_bias)
    st = _local_grads(st, target, gate, w_in_full, pool_w, pool_scale, wab, wpb, wout, final_g)
    d_rel_bias = _local_bias_grads(st, _BIAS_GRADS_EARLY + _BIAS_GRADS_LATE)
    gx, dmod, dng = _local_input_grads(st, w_in_full, norm_g, scale)
    return dict(loss=st["loss"], grad_x=gx, dw_in=st["dw_in"], dw_ab=st["dw_ab"], dw_pb=st["dw_pb"],
                dw_out=st["dw_out"], d_norm_g=dng, d_pool_w=st["dpw"], d_pool_scale=st["dps"],
                d_rel_bias=d_rel_bias.T, d_final_g=st["gfg"], dmod=dmod)


def _local_pre(x, shift, scale, norm_g, rel_bias, dep=None):
    S = x.shape[0]
    LS = S // R16
    geoms = [_Geom(gi, LS) for gi in range(3)]
    idx_np = [[g.bucket_index(sub) for sub in range(2)] for g in geoms]
    buckets = [[tuple(int(b) for b in np.unique(ix) if b >= 0) for ix in pair] for pair in idx_np]
    idxs = [[jnp.asarray(ix) for ix in pair] for pair in idx_np]
    biases = [[_bias_table(idxs[gi][sub], buckets[gi][sub], rel_bias, gi, sub, dep) for sub in range(2)]
              for gi in range(3)]
    h = _norm_mod(x, norm_g, shift, scale, dep).reshape(S, D)
    return dict(geoms=geoms, idxs=idxs, buckets=buckets, biases=biases, x=x, h=h, ht=_tcast(h, "h_transpose"))


def _local_grads(st, target, gate, w_in_full, pool_w, pool_scale, wab, wpb, wout, final_g):
    geoms, idxs, biases, x, h, ht = (st[k] for k in ("geoms", "idxs", "biases", "x", "h", "ht"))
    S = h.shape[0]
    LS = S // R16
    qkv, rest = _proj(h, w_in_full)

    os_, ls_ = [], []
    for gi in range(2):
        o, l = _attn_fwd(qkv, biases[gi], geoms[gi])
        os_.append(o)
        ls_.append(l)
    attn, Lb, ag = _attn_fwd(qkv, biases[2], geoms[2], others=(os_, ls_, rest))
    pt, mp, pg = _pool_fwd(rest, pool_w, pool_scale)
    dx2, dmo3, loss, gfg, dgate, dw_out = _merge_out_loss(ag, pg, rest, wab, wpb, wout, x, target, gate, final_g)
    dmo = dmo3.reshape(S, D)

    dw_ab, dw_pb, drest, dattn, dlb, dmp, dps = _bwd_gates(dmo, wout, wab, wpb, rest, ag, pg, attn, mp, pool_scale)
    drest, dpw = _pool_bwd(dmp, pt, pool_w, drest)

    dqkv = None
    dsums = []
    for gi in range(3):
        dqkv, dsum = _attn_bwd(qkv, dattn, Lb, dlb, biases[gi], geoms[gi], dqkv)
        dsums.append(dsum)

    dw_in = _mm_tn(ht, dqkv, n_total=8192, out_blk=lambda j: j, name="dw_in_qkv", with_low=True)
    dw_in, dw_in_low = _mm_tn(ht, drest, n_total=8192, out_blk=_rest_blk, name="dw_in_rest", prev=dw_in, with_low=True)
    return dict(st, loss=loss[0, 0], dw_in=dw_in, dw_in_low=dw_in_low, dw_ab=dw_ab, dw_pb=dw_pb, dw_out=dw_out, dpw=dpw, dps=dps, gfg=gfg,
                dgate=dgate, dsums=dsums, dqkv=dqkv, drest=drest, dx2=dx2)


def _local_bias_grads(st, which, dep=None):
    rows = []
    for gi in range(3):
        got = [_bias_grad(st["dsums"][gi][sub], st["idxs"][gi][sub], st["buckets"][gi][sub], gi, sub, dep)
               for sub in range(2) if (gi, sub) in which]
        rows.append(sum(got) if got else jnp.zeros((NH, 32), F32))
    return jnp.concatenate(rows, axis=0)


_BIAS_GRADS_EARLY = ((0, 0), (0, 1), (1, 0), (2, 0))
_BIAS_GRADS_LATE = ((1, 1), (2, 1))


def _dh_h_bwd(dqkv, drest, w, dx2, x, norm_g, scale, dep=None):
    S = x.shape[0]
    dep_specs, dep_args = _after(dep)

    def body(a_ref, b_ref, w_ref, dx2_ref, x_ref, g_ref, sc_ref, *rest):
        gx_ref, dsh_ref, dsc_ref, dng_ref, nat_ref = rest[-5:]
        i = pl.program_id(0)

        @pl.when(i == 0)
        def _():
            dsh_ref[...] = jnp.zeros_like(dsh_ref)
            dsc_ref[...] = jnp.zeros_like(dsc_ref)
            dng_ref[...] = jnp.zeros_like(dng_ref)

        a, b = _ld(a_ref), _ld(b_ref)
        acc = _dot(a[:, 0:512], _w512(w_ref, 0), NT)
        for j in range(1, 9):
            acc = acc + _dot(a[:, 512 * j:512 * (j + 1)], _w512(w_ref, j), NT)
        off = 0
        for lo, hi in REST_COLS:
            for j in range(lo, hi):
                acc = acc + _dot(b[:, off:off + 512], _w512(w_ref, j), NT)
                off += 512
        for r in range(R16):
            nat_ref[:, r, :] = acc[R16 * r:R16 * (r + 1)]
        dh = jnp.concatenate([nat_ref[l] for l in range(R16)], axis=0)
        g = g_ref[...]
        one_sc = 1.0 + sc_ref[...]
        xv = x_ref[...]
        r1 = lax.rsqrt(jnp.mean(xv * xv, axis=-1, keepdims=True) + EPS)
        xn = xv * r1
        dhx = dh * xn
        dxn = dh * (g * one_sc)
        dx1 = r1 * (dxn - xn * jnp.mean(dxn * xn, axis=-1, keepdims=True))
        gx_ref[...] = dx2_ref[...] + dx1
        dsh_ref[...] += jnp.sum(dh, axis=0, keepdims=True)
        dsc_ref[...] += jnp.sum(dhx * g, axis=0, keepdims=True)
        dng_ref[...] += jnp.sum(dhx * one_sc, axis=0, keepdims=True)

    nat = pl.BlockSpec((TOK, D), lambda i: (i, 0))
    vec = pl.BlockSpec((1, D), lambda i: (0, 0))
    slab = lambda W: pl.BlockSpec((R16, R16, W), lambda i: (0, i, 0))
    return pl.pallas_call(
        body, name="dh_h_bwd", grid=(S // TOK,),
        in_specs=[slab(QKV_W), slab(REST_W), _resident((8, D, 1024)), nat, nat, vec, vec] + dep_specs,
        out_specs=[nat, vec, vec, vec],
        out_shape=[_sds((S, D), F32), _sds((1, D), F32), _sds((1, D), F32), _sds((1, D), F32)],
        scratch_shapes=[pltpu.VMEM((R16, R16, D), F32)],
        compiler_params=_cp(("arbitrary",), vmem=56),
    )(dqkv, drest, w, dx2, x, norm_g, scale, *dep_args)


def _local_input_grads(st, w_in_full, norm_g, scale, dep=None):
    S = st["h"].shape[0]
    LS = S // R16
    gx, dsh, dsc, dng = _dh_h_bwd(st["dqkv"].reshape(R16, LS, QKV_W), st["drest"].reshape(R16, LS, REST_W),
                                  w_in_full, st["dx2"], st["x"], norm_g, scale, dep)
    return gx, jnp.concatenate([dsh, dsc, st["dgate"]], axis=1), dng


def _my_place():
    return lax.axis_index("x"), lax.axis_index("y"), lax.axis_index("c")


def _block_of(ref, axis, idx, width):
    if axis is None:
        return ref.at[idx]
    start = pl.multiple_of(idx * width, width)
    if axis == 0:
        return ref.at[pl.ds(start, width)]
    return ref.at[:, pl.ds(start, width)]


_SEM = pl.BlockSpec(memory_space=pltpu.SEMAPHORE)
_EFFECT = pltpu.SideEffectType.DATAFLOW_SIDE_EFFECTING


def _split_start(name, srcs, lands, ncopy, plan, dep=None):
    arrs = list(srcs) + list(lands)
    na, ns = len(arrs), len(srcs)
    dep_specs, dep_args = _after(dep)

    def body(*refs):
        send_sems, recv_sems, token = refs[na + len(dep_args)], refs[na + len(dep_args) + 1], refs[-1]
        for k, (s, d, dev) in enumerate(plan(refs[:ns], refs[ns:na])):
            pltpu.make_async_remote_copy(src_ref=s, dst_ref=d, send_sem=send_sems.at[k], recv_sem=recv_sems.at[k],
                                         device_id=dev, device_id_type=MESH).start()
        token[...] = jnp.zeros_like(token)

    outs = pl.pallas_call(
        body, name=name,
        out_shape=(pltpu.SemaphoreType.DMA((ncopy,)), pltpu.SemaphoreType.DMA((ncopy,)),
                   *[pltpu.HBM(a.shape, a.dtype) for a in arrs], _sds((8, 128), F32)),
        in_specs=[_HBM] * na + dep_specs,
        out_specs=(_SEM, _SEM, *[_HBM] * na, pl.BlockSpec(memory_space=pltpu.VMEM)),
        input_output_aliases={i: 2 + i for i in range(na)},
        compiler_params=pltpu.CompilerParams(has_side_effects=_EFFECT),
    )(*[pltpu.with_memory_space_constraint(a, pltpu.HBM) for a in arrs], *dep_args)
    return (outs[0], outs[1]), list(outs[2:2 + ns]), list(outs[2 + ns:2 + na]), outs[-1]


def _split_wait(name, sems, srcs, lands, plan, after):
    arrs = list(srcs) + list(lands)
    na, ns = len(arrs), len(srcs)
    after = list(after) if isinstance(after, (list, tuple)) else [after]

    def body(*refs):
        send_sems, recv_sems = refs[na], refs[na + 1]
        for k, (s, d, dev) in enumerate(plan(refs[:ns], refs[ns:na])):
            cp = pltpu.make_async_remote_copy(src_ref=s, dst_ref=d, send_sem=send_sems.at[k], recv_sem=recv_sems.at[k],
                                              device_id=dev, device_id_type=MESH)
            cp.wait_send()
            cp.wait_recv()

    outs = pl.pallas_call(
        body, name=name, out_shape=[pltpu.HBM(a.shape, a.dtype) for a in arrs],
        in_specs=[_HBM] * na + [_SEM, _SEM] + [pl.BlockSpec(memory_space=pl.ANY)] * len(after),
        out_specs=[_HBM] * na,
        input_output_aliases={i: i for i in range(na)},
        compiler_params=pltpu.CompilerParams(has_side_effects=_EFFECT),
    )(*arrs, sems[0], sems[1], *after)
    return list(outs[:ns]), list(outs[ns:])


def _gather_chips_plan(layout):
    def plan(src, land):
        x, y, c = _my_place()
        me = 4 * x + 2 * y + c
        mine = [_block_of(land[p], layout[p][0], me, layout[p][1]) for p in range(len(land))]
        return [(mine[p], mine[p], (cx, cy, c)) for p in range(len(land)) for cx, cy in [(1 - x, y), (x, 1 - y)]]
    return plan


def _pair_plan(layout):
    def plan(src, land):
        x, y, c = _my_place()
        n = len(layout)
        return [(_block_of(src[p], layout[p][0], 2 * q + (1 - c), layout[p][1]), land[p].at[q], (x, y, 1 - c))
                for p in range(n) for q in range(4)] + [(src[p], land[p], (x, y, 1 - c)) for p in range(n, len(src))]
    return plan


def _chips_plan(n):
    def plan(src, land):
        x, y, c = _my_place()
        chips = [(1 - x, y), (x, 1 - y), (1 - x, 1 - y)]
        return [(src[p].at[2 * cx + cy], land[p].at[2 * x + y], (cx, cy, c)) for p in range(n) for cx, cy in chips] + \
               [(src[p], land[p].at[2 * x + y], (cx, cy, c)) for p in range(n, len(src)) for cx, cy in chips]
    return plan


def _pair_share(gathered):
    n = len(gathered)

    def body(*refs):
        land = refs[n:2 * n]
        send_sems, recv_sems, fwd_send, fwd_recv = refs[2 * n:]
        x, y, c = _my_place()
        k_dg = 3 - (2 * x + y)
        south = c == 0
        pass_on = 4 * jnp.where(south, 1 - x, x) + 2 * jnp.where(south, y, 1 - y) + c
        to = (jnp.where(south, x, 1 - x), jnp.where(south, 1 - y, y), c)

        def forward(p):
            return pltpu.make_async_remote_copy(
                src_ref=land[p].at[pass_on], dst_ref=land[p].at[pass_on], send_sem=fwd_send.at[p],
                recv_sem=fwd_recv.at[p], device_id=to, device_id_type=MESH)

        def to_sibling(p, k):
            blk = land[p].at[2 * k + c]
            return pltpu.make_async_remote_copy(
                src_ref=blk, dst_ref=blk, send_sem=send_sems.at[4 * p + k], recv_sem=recv_sems.at[4 * p + k],
                device_id=(x, y, 1 - c), device_id_type=MESH)

        for p in range(n):
            forward(p).start()
        for p in range(n):
            for k in range(4):
                @pl.when(k != k_dg)
                def _():
                    to_sibling(p, k).start()
        for p in range(n):
            dg = land[p].at[2 * k_dg + c]
            pltpu.make_async_remote_copy(src_ref=dg, dst_ref=dg, send_sem=fwd_send.at[p], recv_sem=fwd_recv.at[p],
                                         device_id=to, device_id_type=MESH).wait_recv()
            for k in range(4):
                @pl.when(k == k_dg)
                def _():
                    to_sibling(p, k).start()
        for p in range(n):
            forward(p).wait_send()
            for k in range(4):
                to_sibling(p, k).wait_send()
                blk = land[p].at[2 * k + 1 - c]
                pltpu.make_async_remote_copy(
                    src_ref=blk, dst_ref=blk, send_sem=send_sems.at[4 * p + k], recv_sem=recv_sems.at[4 * p + k],
                    device_id=(x, y, 1 - c), device_id_type=MESH).wait_recv()

    return pl.pallas_call(
        body, name="gather_pair_share", in_specs=[_HBM] * n, out_specs=[_HBM] * n,
        out_shape=[_sds(g.shape, g.dtype) for g in gathered],
        input_output_aliases={p: p for p in range(n)},
        scratch_shapes=[pltpu.SemaphoreType.DMA((4 * n,)), pltpu.SemaphoreType.DMA((4 * n,)),
                        pltpu.SemaphoreType.DMA((n,)), pltpu.SemaphoreType.DMA((n,))],
    )(*gathered)


def _place_slot(arrs, slot, n_slots, name, out_dtype=None):
    n = len(arrs)

    def body(s_ref, *refs):
        for p in range(n):
            refs[n + p][...] = refs[p][...].astype(refs[n + p].dtype)

    def spec(a):
        tail = a.shape[-2:]
        return pl.BlockSpec((None,) + tail, lambda i, s_ref: (s_ref[0], 0, 0))

    in_specs = [spec(a) if a.ndim == 3 else pl.BlockSpec(a.shape, lambda i, s_ref: (0, 0)) for a in arrs]
    return pl.pallas_call(
        body, name=name,
        grid_spec=pltpu.PrefetchScalarGridSpec(num_scalar_prefetch=1, grid=(1,), in_specs=in_specs,
                                               out_specs=[spec(a) for a in arrs]),
        out_shape=[_in_hbm((n_slots,) + a.shape[-2:], out_dtype or a.dtype) for a in arrs],
        compiler_params=_cp(("arbitrary",)),
    )(slot, *[_pin(a) for a in arrs])


def _cols_from_blocks(w8, name):
    _, R, C = w8.shape

    def body(w_ref, o_ref):
        for e in range(8):
            o_ref[:, C * e:C * (e + 1)] = w_ref[e]

    return pl.pallas_call(body, name=name, out_shape=_sds((R, 8 * C), w8.dtype), compiler_params=_cp())(w8)


def _all_gather_small(srcs, name):
    n = len(srcs)
    vmem = pl.BlockSpec(memory_space=pltpu.VMEM)

    def body(*refs):
        src, out = refs[:n], refs[n:2 * n]
        send_sems, recv_sems = refs[2 * n:]
        x, y, c = _my_place()
        me = 4 * x + 2 * y + c
        flip = lambda v, f: 1 - v if f else v
        peers = [(flip(x, fx), flip(y, fy), flip(c, fc)) for fx in (0, 1) for fy in (0, 1) for fc in (0, 1)][1:]
        cps = []
        for p in range(n):
            out[p][me] = src[p][...]
            for k, peer in enumerate(peers):
                cps.append(pltpu.make_async_remote_copy(
                    src_ref=src[p], dst_ref=out[p].at[me], send_sem=send_sems.at[7 * p + k],
                    recv_sem=recv_sems.at[7 * p + k], device_id=peer, device_id_type=MESH))
        for cp in cps:
            cp.start()
        for cp in cps:
            cp.wait()

    return pl.pallas_call(
        body, name=name, in_specs=[vmem] * n, out_specs=[vmem] * n,
        out_shape=[_sds((8,) + a.shape, a.dtype) for a in srcs],
        scratch_shapes=[pltpu.SemaphoreType.DMA((7 * n,)), pltpu.SemaphoreType.DMA((7 * n,))],
    )(*srcs)


def _adaln_exchange(c, w_ada, b_part):
    vmem = pl.BlockSpec(memory_space=pltpu.VMEM)

    def body(c_ref, w_ref, b_ref, call_ref, mod_ref, part_ref, send_sems, recv_sems):
        x, y, c = _my_place()
        me = 4 * x + 2 * y + c
        flip = lambda v, f: 1 - v if f else v
        peers = [(flip(x, fx), flip(y, fy), flip(c, fc)) for fx in (0, 1) for fy in (0, 1) for fc in (0, 1)][1:]
        call_ref[me] = c_ref[...]
        first = [pltpu.make_async_remote_copy(src_ref=c_ref, dst_ref=call_ref.at[me], send_sem=send_sems.at[k],
                                              recv_sem=recv_sems.at[k], device_id=peer, device_id_type=MESH)
                 for k, peer in enumerate(peers)]
        for cp in first:
            cp.start()
        for cp in first:
            cp.wait()
        c_all = jnp.concatenate([call_ref[j] for j in range(8)], axis=0)
        part_ref[...] = _dot(c_all.astype(CDT), w_ref[...].astype(CDT), NN) + b_ref[...]
        mod_ref[me] = part_ref[pl.ds(me, 1), :]
        second = [pltpu.make_async_remote_copy(
            src_ref=part_ref.at[pl.ds(4 * px + 2 * py + pc, 1)], dst_ref=mod_ref.at[me], send_sem=send_sems.at[7 + k],
            recv_sem=recv_sems.at[7 + k], device_id=(px, py, pc), device_id_type=MESH)
            for k, (px, py, pc) in enumerate(peers)]
        for cp in second:
            cp.start()
        for cp in second:
            cp.wait()

    return pl.pallas_call(
        body, name="adaln_exchange", in_specs=[vmem] * 3, out_specs=[vmem] * 2,
        out_shape=[_sds((8, 1, D), F32), _sds((8, 1, w_ada.shape[1]), F32)],
        scratch_shapes=[pltpu.VMEM((8, w_ada.shape[1]), F32), pltpu.SemaphoreType.DMA((14,)),
                        pltpu.SemaphoreType.DMA((14,))],
        compiler_params=_cp(),
    )(c, w_ada, b_part)


def _pair_add(dw, recv, axis, width, cidx, name):
    if axis == 0:
        tr = min(width, 256)
        C = dw.shape[1]
        grid = (4, width // tr)
        nb = width // tr
        own = pl.BlockSpec((tr, C), lambda q, i, c_ref: ((2 * q + c_ref[0]) * nb + i, 0))
        oth = pl.BlockSpec((None, tr, C), lambda q, i, c_ref: (q, i, 0))
    else:
        R = dw.shape[0]
        tr = min(R, 1024)
        grid = (4, R // tr)
        own = pl.BlockSpec((tr, width), lambda q, i, c_ref: (i, 2 * q + c_ref[0]))
        oth = pl.BlockSpec((None, tr, width), lambda q, i, c_ref: (q, i, 0))

    def body(c_ref, a_ref, b_ref, o_ref):
        o_ref[...] = (a_ref[...] + b_ref[...].astype(F32)).astype(o_ref.dtype)

    return pl.pallas_call(
        body, name=name,
        grid_spec=pltpu.PrefetchScalarGridSpec(num_scalar_prefetch=1, grid=grid, in_specs=[own, oth], out_specs=oth),
        out_shape=(_in_hbm if _is_big(dw) else _sds)(recv.shape, CDT), compiler_params=_cp(("parallel", "parallel")),
    )(cidx, *[_pin(a) if _is_big(dw) else a for a in (dw, recv)])


def _add_pairs(xs, ys, name):
    n = len(xs)

    def body(*refs):
        for p in range(n):
            refs[2 * n + p][...] = refs[p][...] + refs[n + p][...]

    return pl.pallas_call(body, name=name, out_shape=[_sds(a.shape, F32) for a in xs])(*xs, *ys)


def _adam_math(w, g, m, v):
    m = ADAM_B1 * m + (1.0 - ADAM_B1) * g
    v = ADAM_B2 * v + (1.0 - ADAM_B2) * (g * g)
    m_hat = m / (1.0 - ADAM_B1 ** ADAM_STEP)
    v_hat = v / (1.0 - ADAM_B2 ** ADAM_STEP)
    delta = -ADAM_LR * (m_hat / (jnp.sqrt(v_hat) + ADAM_EPS) + ADAM_WD * w)
    return delta, m, v


def _adam(gparts, w, m, v, name):
    P, R, C = gparts.shape
    tr = R if R <= 256 else 256

    def body(g_ref, w_ref, m_ref, v_ref, go_ref, d_ref, mo_ref, vo_ref):
        g = g_ref[0].astype(F32)
        for i in range(1, P):
            g = g + g_ref[i].astype(F32)
        d, mn, vn = _adam_math(w_ref[...], g, m_ref[...], v_ref[...])
        go_ref[...] = g
        d_ref[...] = d
        mo_ref[...] = mn
        vo_ref[...] = vn

    t = pl.BlockSpec((tr, C), lambda i: (i, 0))
    return pl.pallas_call(
        body, name=name, grid=(R // tr,),
        in_specs=[pl.BlockSpec((P, tr, C), lambda i: (0, i, 0)), t, t, t], out_specs=[t, t, t, t],
        out_shape=[(_in_hbm if _is_big(w) else _sds)((R, C), F32)] * 4, compiler_params=_cp(("parallel",)),
    )(*[_pin(a) if _is_big(w) else a for a in (gparts, w, m, v)])


def _sum_parts(gparts, name):
    P, R, C = gparts.shape

    def body(g_ref, o_ref):
        g = g_ref[0]
        for i in range(1, P):
            g = g + g_ref[i]
        o_ref[...] = g

    return pl.pallas_call(body, name=name, out_shape=_sds((R, C), F32))(gparts)


def _adam_w_ada(c_t, dm, w, m, v):
    R, C = w.shape

    def body(c_ref, dm_ref, w_ref, m_ref, v_ref, go_ref, d_ref, mo_ref, vo_ref):
        g = c_ref[:, 0:1] * dm_ref[0:1, :]
        for b in range(1, 8):
            g = g + c_ref[:, b:b + 1] * dm_ref[b:b + 1, :]
        d, mn, vn = _adam_math(w_ref[...], g, m_ref[...], v_ref[...])
        go_ref[...] = g
        d_ref[...] = d
        mo_ref[...] = mn
        vo_ref[...] = vn

    return pl.pallas_call(body, name="adam_w_ada", out_shape=[_sds((R, C), F32)] * 4,
                          compiler_params=_cp())(c_t, dm, w, m, v)


def kernel(x, c, norm_g, w_ada, b_ada, w_in, pool_w, pool_scale, w_attn_br, w_pool_br, w_out, rel_bias, final_g, loss_target, m_norm_g, m_w_ada, m_b_ada, m_w_in, m_pool_w, m_pool_scale, m_w_attn_br, m_w_pool_br, m_w_out, m_rel_bias, m_final_g, v_norm_g, v_w_ada, v_b_ada, v_w_in, v_pool_w, v_pool_scale, v_w_attn_br, v_w_pool_br, v_w_out, v_rel_bias, v_final_g):
    S = x.shape[1]
    px, py, pc = _my_place()
    me = 4 * px + 2 * py + pc
    cidx = jnp.reshape(pc, (1,)).astype(jnp.int32)

    layout = [(1, 1024), (1, 128), (1, 128), (0, 128)]
    final_g2 = final_g.reshape(1, D)
    me1 = jnp.reshape(me, (1,)).astype(jnp.int32)

    b_part = lax.dynamic_slice(b_ada, (0, me * 384), (1, 384))
    c_all, mod_all = _adaln_exchange(c, w_ada[0], b_part)
    c_all = c_all.reshape(8, D)
    mod = mod_all.reshape(1, 3 * D)
    shift, scale, gate = mod[:, :D], mod[:, D:2 * D], mod[:, 2 * D:]

    gplan = _gather_chips_plan([(None, 1)] * 4)
    lands = _place_slot([w_in[0], w_attn_br[0], w_pool_br[0], w_out[0]], me1, 8, "place_own_shards", CDT)
    g_sems, _, lands, tok = _split_start("gather_chips_start", [], lands, 8, gplan, dep=mod_all)
    st = _local_pre(x[0], shift, scale, norm_g, rel_bias, dep=tok)
    _, lands = _split_wait("gather_chips_wait", g_sems, [], lands, gplan,
                           [st["ht"]] + [b for pair in st["biases"] for b in pair])
    w_in_full, wab8, wpb8, wout8 = _pair_share(lands)
    wab, wpb = _cols_from_blocks(wab8, "w_attn_br_cols"), _cols_from_blocks(wpb8, "w_pool_br_cols")
    wout = wout8.reshape(D, D)

    st = _local_grads(st, loss_target[0], gate, w_in_full, pool_w[0], pool_scale, wab, wpb, wout, final_g2)

    early = [st["dpw"].reshape(AW, 128), st["dps"], st["gfg"], jnp.broadcast_to(st["loss"], (1, 128))]
    dws = [st["dw_in"], st["dw_ab"], st["dw_pb"], st["dw_out"]]
    sent = [st["dw_in_low"]] + dws[1:] + early
    pplan = _pair_plan(layout)
    pair_shapes = [(4, w, a.shape[1]) if ax == 0 else (4, a.shape[0], w) for a, (ax, w) in zip(dws, layout)]
    p_sems, sent, recv, tok = _split_start(
        "reduce_pair_start", sent,
        [lax.empty(s, a.dtype) for s, a in zip(pair_shapes + [a.shape for a in early], sent)], 20, pplan)
    d_rel_bias = _local_bias_grads(st, _BIAS_GRADS_EARLY, dep=tok)
    sent, recv = _split_wait("reduce_pair_wait", p_sems, sent, recv, pplan, d_rel_bias)
    parts = [_pair_add(dw, r, ax, w, cidx, f"pair_add{i}")
             for i, (dw, r, (ax, w)) in enumerate(zip([dws[0]] + sent[1:4], recv[:4], layout))]
    parts += _add_pairs(sent[4:], recv[4:], "pair_add_smalls")

    slot1 = jnp.reshape(2 * px + py, (1,)).astype(jnp.int32)
    cplan = _chips_plan(4)
    c_sems, parts, lands, tok = _split_start("reduce_chips_start", parts,
                                             _place_slot(parts, slot1, 4, "place_own_partials"), 24, cplan)
    d_rel_bias = d_rel_bias + _local_bias_grads(st, _BIAS_GRADS_LATE, dep=tok)
    grad_x, dmod, d_norm_g = _local_input_grads(st, w_in_full, norm_g, scale, dep=tok)

    dmod_all, ng_all, rb_all = _all_gather_small([dmod, d_norm_g, d_rel_bias], "gather_late")
    o_norm_g = _adam(ng_all, norm_g, m_norm_g, v_norm_g, "adam_norm_g")
    o_rel_bias = [o.T for o in _adam(rb_all, rel_bias.T, m_rel_bias.T, v_rel_bias.T, "adam_rel_bias")]
    o_b_ada = _adam(dmod_all, b_ada, m_b_ada, v_b_ada, "adam_b_ada")
    dm_mine = lax.dynamic_slice(dmod_all.reshape(8, 3 * D), (0, me * 384), (8, 384))
    o_w_ada = [o[None] for o in _adam_w_ada(c_all.T, dm_mine, w_ada[0], m_w_ada[0], v_w_ada[0])]

    _, (g_in, g_ab, g_pb, g_out, g_pw, g_ps, g_fg, g_loss) = _split_wait(
        "reduce_chips_wait", c_sems, parts, lands, cplan, [o_w_ada[0], o_b_ada[0], o_norm_g[0], o_rel_bias[0]])
    o_w_in = [o[None] for o in _adam(g_in, w_in[0], m_w_in[0], v_w_in[0], "adam_w_in")]
    o_w_ab = [o[None] for o in _adam(g_ab, w_attn_br[0], m_w_attn_br[0], v_w_attn_br[0], "adam_w_attn_br")]
    o_w_pb = [o[None] for o in _adam(g_pb, w_pool_br[0], m_w_pool_br[0], v_w_pool_br[0], "adam_w_pool_br")]
    o_w_out = [o[None] for o in _adam(g_out, w_out[0], m_w_out[0], v_w_out[0], "adam_w_out")]
    o_pool_w = [o.reshape(pool_w.shape) for o in _adam(g_pw, pool_w.reshape(AW, 128), m_pool_w.reshape(AW, 128),
                                                       v_pool_w.reshape(AW, 128), "adam_pool_w")]
    o_pool_scale = _adam(g_ps, pool_scale, m_pool_scale, v_pool_scale, "adam_pool_scale")
    o_final_g = [o.reshape(D) for o in _adam(g_fg, final_g2, m_final_g.reshape(1, D), v_final_g.reshape(1, D),
                                             "adam_final_g")]
    loss = _sum_parts(g_loss, "sum_loss")[0, 0]

    per_w = [o_norm_g, o_w_ada, o_b_ada, o_w_in, o_pool_w, o_pool_scale, o_w_ab, o_w_pb, o_w_out, o_rel_bias, o_final_g]
    outs = [loss, grad_x[None]]
    for k in range(4):
        outs += [o[k] for o in per_w]
    return tuple(outs)
```

```python
import functools
import math

import numpy as np
import jax
import jax.numpy as jnp
from jax import lax
from jax.experimental import pallas as pl
from jax.experimental.pallas import tpu as pltpu

F32 = jnp.float32
CDT = jnp.bfloat16
D = 1024
HD = 64
NH = 8
AW = 512
QKV_W = 4608
REST_W = 3584
R16 = 16
EPS = 1e-6
NEG = -1e30
N_BACK = 128
ADAM_LR, ADAM_B1, ADAM_B2, ADAM_EPS, ADAM_WD, ADAM_STEP = 0.001, 0.9, 0.999, 1e-08, 0.01, 10
MESH = pl.DeviceIdType.MESH
VMEM_MB = 1024 * 1024


def _cp(sem=None, vmem=48, **kw):
    if sem is not None:
        kw["dimension_semantics"] = sem
    return pltpu.CompilerParams(vmem_limit_bytes=vmem * VMEM_MB, **kw)


def _sds(shape, dtype):
    return jax.ShapeDtypeStruct(tuple(shape), dtype)


def _in_hbm(shape, dtype):
    return pltpu.HBM(tuple(shape), dtype)


def _pin(a):
    return pltpu.with_memory_space_constraint(a, pltpu.HBM)


def _is_big(a):
    return a.size * a.dtype.itemsize >= 4 * VMEM_MB


def _rest_blk(jj):
    return jnp.where(jj == 0, 9, jnp.where(jj == 6, 10, jj + 10))


def _sigmoid(z):
    return 0.5 * jnp.tanh(0.5 * z) + 0.5


def _dot(a, b, dims):
    return lax.dot_general(a, b, (dims, ((), ())), preferred_element_type=F32)


NN = ((1,), (0,))
NT = ((1,), (1,))
TN = ((0,), (0,))


def _ld(ref):
    if len(ref.shape) == 2:
        return ref[...]
    return jnp.concatenate([ref[a] for a in range(ref.shape[0])], axis=0)


def _st(ref, val, cols=None):
    val = val.astype(ref.dtype)
    if len(ref.shape) == 2:
        if cols is None:
            ref[...] = val
        else:
            ref[:, cols] = val
        return
    rows = ref.shape[1]
    for a in range(ref.shape[0]):
        if cols is None:
            ref[a] = val[a * rows:(a + 1) * rows]
        else:
            ref[a, :, cols] = val[a * rows:(a + 1) * rows]


REST_COLS = ((9, 10), (11, 16), (10, 11))
_HBM = pl.BlockSpec(memory_space=pltpu.HBM)


def _resident(shape):
    return pl.BlockSpec(shape, lambda *ids: (0,) * len(shape), pipeline_mode=pl.Buffered(1))


def _w512(w_ref, j):
    return w_ref[j // 2, :, 512 * (j % 2):512 * (j % 2 + 1)]


def _proj(h, w):
    S = h.shape[0]
    tm = 512

    def body(a_ref, w_ref, q_ref, r_ref):
        a = a_ref[...]
        for j in range(9):
            q_ref[:, 512 * j:512 * (j + 1)] = _dot(a, _w512(w_ref, j), NN).astype(q_ref.dtype)
        off = 0
        for lo, hi in REST_COLS:
            for j in range(lo, hi):
                r_ref[:, off:off + 512] = _dot(a, _w512(w_ref, j), NN)
                off += 512

    return pl.pallas_call(
        body, name="proj", grid=(S // tm,),
        in_specs=[pl.BlockSpec((tm, D), lambda i: (i, 0)), _resident((8, D, 1024))],
        out_specs=[pl.BlockSpec((tm, QKV_W), lambda i: (i, 0)), pl.BlockSpec((tm, REST_W), lambda i: (i, 0))],
        out_shape=[_sds((S, QKV_W), CDT), _sds((S, REST_W), F32)],
        compiler_params=_cp(("parallel",), vmem=56),
    )(h, w)


def _after(dep):
    return ([], []) if dep is None else ([pl.BlockSpec(memory_space=pl.ANY)], [dep])


def _dh(dqkv, drest, w, dep=None):
    S = dqkv.shape[0]
    tm = 512
    dep_specs, dep_args = _after(dep)

    def body(a_ref, b_ref, w_ref, *rest):
        o_ref = rest[-1]
        acc = _dot(a_ref[:, 0:512], _w512(w_ref, 0), NT)
        for j in range(1, 9):
            acc = acc + _dot(a_ref[:, 512 * j:512 * (j + 1)], _w512(w_ref, j), NT)
        off = 0
        for lo, hi in REST_COLS:
            for j in range(lo, hi):
                acc = acc + _dot(b_ref[:, off:off + 512], _w512(w_ref, j), NT)
                off += 512
        o_ref[...] = acc

    return pl.pallas_call(
        body, name="dh", grid=(S // tm,),
        in_specs=[pl.BlockSpec((tm, QKV_W), lambda i: (i, 0)), pl.BlockSpec((tm, REST_W), lambda i: (i, 0)),
                  _resident((8, D, 1024))] + dep_specs,
        out_specs=pl.BlockSpec((tm, D), lambda i: (i, 0)),
        out_shape=_sds((S, D), F32),
        compiler_params=_cp(("parallel",), vmem=56),
    )(dqkv, drest, w, *dep_args)


def _mm_tn(at, b, *, n_total, out_blk, name, prev=None, tn=512, with_low=False):
    R, S = at.shape
    nb = b.shape[1] // tn
    n_out = 2 if with_low else 1

    def body(*refs):
        res = _dot(refs[0][...], refs[1][...], NN)
        refs[-n_out][...] = res
        if with_low:
            refs[-1][...] = res.astype(CDT)

    in_specs = [_resident((R, S)), pl.BlockSpec((S, tn), lambda j: (0, j))]
    args = [at, b]
    aliases = {}
    if prev is not None:
        prev = list(prev) if with_low else [prev]
        in_specs += [pl.BlockSpec(memory_space=pl.ANY)] * n_out
        args += prev
        aliases = {2 + k: k for k in range(n_out)}
    out_spec = pl.BlockSpec((R, tn), lambda j: (0, out_blk(j)))
    outs = pl.pallas_call(
        body, name=name, grid=(nb,), in_specs=in_specs,
        out_specs=[out_spec] * n_out,
        out_shape=[_sds((R, n_total), F32), _sds((R, n_total), CDT)][:n_out],
        input_output_aliases=aliases,
        compiler_params=_cp(("parallel",)),
    )(*args)
    return tuple(outs) if with_low else outs[0]


def _tcast(a, name):
    M, C = a.shape
    tm = 512

    def body(a_ref, o_ref):
        o_ref[...] = a_ref[...].astype(F32).T.astype(o_ref.dtype)

    return pl.pallas_call(
        body, name=name, grid=(M // tm,),
        in_specs=[pl.BlockSpec((tm, C), lambda i: (i, 0))],
        out_specs=pl.BlockSpec((C, tm), lambda i: (0, i)),
        out_shape=_sds((C, M), CDT), compiler_params=_cp(("parallel",)),
    )(a)


TOK = R16 * R16


def _swap16():
    i = lax.broadcasted_iota(jnp.int32, (TOK, TOK), 0)
    j = lax.broadcasted_iota(jnp.int32, (TOK, TOK), 1)
    return jnp.where(j == jnp.bitwise_and(i, 15) * 16 + jnp.right_shift(i, 4), 1.0, 0.0).astype(CDT)


def _norm_mod(x, norm_g, shift, scale, dep=None):
    S = x.shape[0]
    dep_specs, dep_args = _after(dep)

    def body(x_ref, g_ref, sh_ref, sc_ref, *rest):
        xv = x_ref[...]
        xn = xv * lax.rsqrt(jnp.mean(xv * xv, axis=-1, keepdims=True) + EPS)
        h = (xn * (g_ref[...] * (1.0 + sc_ref[...])) + sh_ref[...]).astype(CDT)
        _st(rest[-1], _dot(_swap16(), h, NN))

    vec = pl.BlockSpec((1, D), lambda i: (0, 0))
    return pl.pallas_call(
        body, name="norm_mod", grid=(S // TOK,),
        in_specs=[pl.BlockSpec((TOK, D), lambda i: (i, 0)), vec, vec, vec] + dep_specs,
        out_specs=pl.BlockSpec((R16, R16, D), lambda i: (0, i, 0)),
        out_shape=_sds((R16, S // R16, D), CDT), compiler_params=_cp(("parallel",)),
    )(x, norm_g, shift, scale, *dep_args)


def _t5_bucket_np(n):
    nf = np.maximum(n, 1).astype(np.float32)
    large = 16 + (np.log(nf / np.float32(16)) / np.float32(math.log(2048 / 16)) * np.float32(16)).astype(np.int32)
    large = np.minimum(large, 31)
    return np.where(n < 16, n, large).astype(np.int32)


class _Geom:
    def __init__(self, gi, LS):
        self.gi, self.LS = gi, LS
        self.dil = (1, 4, 16)[gi]
        if gi == 0:
            self.lead, self.rows, self.n_tiles, self.has_prev = (R16,), 16, LS // 16, True
            a = np.arange(R16)[:, None]
            ll = np.arange(16)[None, :]
            self.pos = (16 * ll + a).reshape(-1)
        elif gi == 1:
            self.lead, self.rows, self.n_tiles, self.has_prev = (4,), 64, LS // 64, True
            a = np.arange(4)[:, None]
            ll = np.arange(64)[None, :]
            self.pos = (4 * ll + a).reshape(-1)
        else:
            self.lead, self.rows, self.n_tiles, self.has_prev = (), LS, 1, False
            self.pos = np.arange(LS)
        self.T = self.pos.shape[0]
        self.Tk = 2 * self.T if self.has_prev else self.T
        self.outer = {0: (), 1: (4,), 2: (R16,)}[gi]

    def view(self, arr):
        S, C = arr.shape
        if self.gi == 1:
            return arr.reshape(4, 4, self.LS, C)
        return arr.reshape(R16, self.LS, C)

    def view_shape(self, C):
        return (4, 4, self.LS, C) if self.gi == 1 else (R16, self.LS, C)

    def unview(self, arr):
        return arr.reshape(R16 * self.LS, arr.shape[-1])

    def spec(self, W, cblk, tile):
        if self.gi == 0:
            return pl.BlockSpec((R16, 16, W), lambda n: (0, tile(n), cblk))
        if self.gi == 1:
            return pl.BlockSpec((4, None, 64, W), lambda r, n: (0, r, tile(n), cblk))
        return pl.BlockSpec((None, self.LS, W), lambda r: (r, 0, cblk))

    def halves(self):
        R = self.rows
        idx = np.arange(self.T).reshape(-1, R)
        return idx[:, :R // 2].reshape(-1), idx[:, R // 2:].reshape(-1)

    def bucket_index(self, sub):
        lo, hi = self.halves()
        pos = self.pos
        if sub == 0:
            pq = pos[lo]
            pk = np.concatenate([pos[hi] - self.T, pos[lo]]) if self.has_prev else np.concatenate([pos[lo], pos[hi]])
        else:
            pq = pos[hi]
            pk = np.concatenate([pos[lo], pos[hi]])
        dist = pq[:, None] - pk[None, :]
        ok = (dist >= 0) & (dist <= N_BACK)
        bucket = _t5_bucket_np(np.clip(dist, 0, N_BACK) * self.dil)
        return np.where(ok, bucket, -1).astype(np.int32)


def _bias_table(idx, buckets, rel_bias, gi, sub, dep=None):
    T, Tk = idx.shape
    tr = 32
    dep_specs, dep_args = _after(dep)

    def body(idx_ref, rb_ref, *rest):
        h = pl.program_id(0)
        ix = idx_ref[...]
        acc = jnp.full(ix.shape, NEG, F32)
        for b in buckets:
            acc = jnp.where(ix == b, rb_ref[b, gi * NH + h], acc)
        rest[-1][...] = acc

    return pl.pallas_call(
        body, name=f"bias_table{gi}{'ab'[sub]}", grid=(NH, T // tr),
        in_specs=[pl.BlockSpec((tr, Tk), lambda h, i: (i, 0)), pl.BlockSpec(memory_space=pltpu.SMEM)] + dep_specs,
        out_specs=pl.BlockSpec((None, tr, Tk), lambda h, i: (h, i, 0)),
        out_shape=_sds((NH, T, Tk), F32), compiler_params=_cp(("parallel", "parallel")),
    )(idx, rel_bias, *dep_args)


def _bias_grad(dsum, idx, buckets, gi, sub, dep=None):
    _, T, Tk = dsum.shape
    dep_specs, dep_args = _after(dep)

    def body(ds_ref, idx_ref, *rest):
        o_ref, r_ref = rest[-2], rest[-1]
        r_ref[...] = jnp.zeros_like(r_ref)
        ix, ds = idx_ref[...], ds_ref[...]
        for b in buckets:
            r_ref[b:b + 1, :] = jnp.sum(jnp.where(ix == b, ds, 0.0), axis=0, keepdims=True)
        o_ref[...] = jnp.broadcast_to(jnp.sum(r_ref[...], axis=1, keepdims=True), (32, 128))

    out = pl.pallas_call(
        body, name=f"bias_grad{gi}{'ab'[sub]}", grid=(NH,),
        in_specs=[pl.BlockSpec((None, T, Tk), lambda h: (h, 0, 0)), pl.BlockSpec((T, Tk), lambda h: (0, 0))] + dep_specs,
        out_specs=pl.BlockSpec((None, 32, 128), lambda h: (h, 0, 0)),
        out_shape=_sds((NH, 32, 128), F32), scratch_shapes=[pltpu.VMEM((32, Tk), F32)],
        compiler_params=_cp(("parallel",)),
    )(dsum, idx, *dep_args)
    return out[:, :, 0]


def _prev_mask(n, T, Tk):
    col = lax.broadcasted_iota(jnp.int32, (1, Tk), 1)
    return jnp.where((col < T) & (n == 0), NEG, 0.0).astype(F32)


def _split_rows(v, R):
    A, half = v.shape[0] // R, R // 2
    via_f32 = v.dtype != F32 and half % 16 != 0
    w = v.astype(F32) if via_f32 else v
    lo = jnp.concatenate([w[a * R:a * R + half] for a in range(A)], axis=0)
    hi = jnp.concatenate([w[a * R + half:(a + 1) * R] for a in range(A)], axis=0)
    return (lo.astype(v.dtype), hi.astype(v.dtype)) if via_f32 else (lo, hi)


def _merge_rows(lo, hi, R):
    half = R // 2
    A = lo.shape[0] // half
    return jnp.concatenate([x[a * half:(a + 1) * half] for a in range(A) for x in (lo, hi)], axis=0)


def _attn_fwd(qkv, biases, geom, others=None):
    S = qkv.shape[0]
    gi, T, R, has_prev = geom.gi, geom.T, geom.rows, geom.has_prev
    H2 = T // 2
    qv = geom.view(qkv)
    cq, ck, cv = 3 * gi, 3 * gi + 1, 3 * gi + 2
    assert others is None or not has_prev

    def body(*refs):
        if has_prev:
            q_ref, kp_ref, kc_ref, vp_ref, vc_ref, ba_ref, bb_ref, o_ref, l_ref = refs
        elif others is not None:
            (q_ref, kc_ref, vc_ref, ba_ref, bb_ref, o0_ref, o1_ref, l0_ref, l1_ref, z_ref,
             attn_ref, L_ref, ag_ref) = refs
        else:
            q_ref, kc_ref, vc_ref, ba_ref, bb_ref, o_ref, l_ref = refs
        q_lo, q_hi = _split_rows(_ld(q_ref), R)
        kc_lo, kc_hi = _split_rows(_ld(kc_ref), R)
        vc_lo, vc_hi = _split_rows(_ld(vc_ref), R)
        k_b = jnp.concatenate([kc_lo, kc_hi], axis=0)
        v_b = jnp.concatenate([vc_lo, vc_hi], axis=0)
        if has_prev:
            n = pl.program_id(len(geom.outer))
            k_a = jnp.concatenate([_split_rows(_ld(kp_ref), R)[1], kc_lo], axis=0)
            v_a = jnp.concatenate([_split_rows(_ld(vp_ref), R)[1], vc_lo], axis=0)
            pm = _prev_mask(n, H2, T)
        else:
            k_a, v_a, pm = k_b, v_b, None
        low = lax.broadcasted_iota(jnp.int32, (1, 2 * HD), 1) < HD
        for j in range(NH // 2):
            cols = slice(2 * HD * j, 2 * HD * (j + 1))
            o_half, l_half = [], []
            for q, k, v, b_ref, mask in ((q_lo, k_a, v_a, ba_ref, pm), (q_hi, k_b, v_b, bb_ref, None)):
                q2 = q[:, cols].astype(F32) * 0.125
                k2, v2 = k[:, cols], v[:, cols]
                qs = jnp.concatenate([jnp.where(low, q2, 0.0), jnp.where(low, 0.0, q2)], axis=0).astype(q.dtype)
                s = _dot(qs, k2, NT) + b_ref[j]
                if mask is not None:
                    s = s + mask
                m = jnp.max(s, axis=1, keepdims=True)
                p = jnp.exp(s - m)
                l = jnp.sum(p, axis=1, keepdims=True)
                pv = _dot(p.astype(v.dtype), v2, NN) / l
                lse = m + jnp.log(l)
                o_half.append(jnp.where(low, pv[:H2], pv[H2:]))
                l_half.append(jnp.where(low, lse[:H2], lse[H2:]))
            o2, lc = _merge_rows(o_half[0], o_half[1], R), _merge_rows(l_half[0], l_half[1], R)
            if others is None:
                _st(o_ref, o2, cols)
                _st(l_ref, lc, cols)
                continue
            la, lb = l0_ref[:, cols], l1_ref[:, cols]
            m = jnp.maximum(jnp.maximum(la, lb), lc)
            ea, eb, ec = jnp.exp(la - m), jnp.exp(lb - m), jnp.exp(lc - m)
            den = ea + eb + ec
            attn = (ea * o0_ref[:, cols] + eb * o1_ref[:, cols] + ec * o2) / den
            attn_ref[:, cols] = attn
            L_ref[:, cols] = m + jnp.log(den)
            z = z_ref[:, cols]
            ag = attn * (z * _sigmoid(z))
            ag_ref[:, cols] = ag.astype(ag_ref.dtype)

    cur = lambda n: n
    prev = lambda n: jnp.maximum(n - 1, 0)
    if has_prev:
        in_specs = [geom.spec(AW, cq, cur), geom.spec(AW, ck, prev), geom.spec(AW, ck, cur),
                    geom.spec(AW, cv, prev), geom.spec(AW, cv, cur)]
        args = [qv] * 5
    else:
        in_specs = [geom.spec(AW, cq, cur), geom.spec(AW, ck, cur), geom.spec(AW, cv, cur)]
        args = [qv] * 3
    nd = len(geom.outer) + (1 if has_prev else 0)
    biases = [b.reshape(NH // 2, 2 * H2, b.shape[2]) for b in biases]
    in_specs += [pl.BlockSpec(b.shape, lambda *ids: (0, 0, 0)) for b in biases]
    grid = geom.outer + ((geom.n_tiles,) if has_prev else ())
    oshape = geom.view_shape(AW)
    tile = geom.spec(AW, 0, cur)
    if others is None:
        o, l = pl.pallas_call(
            body, name=f"attn_fwd{gi}", grid=grid, in_specs=in_specs, out_specs=[tile, tile],
            out_shape=[_sds(oshape, F32), _sds(oshape, F32)],
            compiler_params=_cp(("arbitrary",) * nd),
        )(*args, *biases)
        return geom.unview(o), geom.unview(l)
    os_, ls_, rest = others
    attn, Lb, ag = pl.pallas_call(
        body, name=f"attn_fwd{gi}_merge", grid=grid, in_specs=in_specs + [tile] * 5,
        out_specs=[tile, tile, tile],
        out_shape=[_sds(oshape, F32), _sds(oshape, F32), _sds(oshape, CDT)],
        compiler_params=_cp(("arbitrary",) * nd),
    )(*args, *biases, *[geom.view(a) for a in (*os_, *ls_, rest)])
    return geom.unview(attn), geom.unview(Lb), geom.unview(ag)


def _pool_counts(r, LS, win):
    l = lax.broadcasted_iota(jnp.int32, (LS, 1), 0)
    return jnp.minimum(16 * l + (r + 1), win).astype(F32)


def _pool_fwd(rest, pool_w, pool_scale):
    S = rest.shape[0]
    LS = S // R16
    r3 = rest.reshape(R16, LS, REST_W)
    PG = 128

    def body(u_ref, z_ref, pw_ref, ps_ref, pt_ref, mp_ref, pg_ref, e_ref):
        g = pl.program_id(0)
        win = jnp.left_shift(2, g)
        row = lax.broadcasted_iota(jnp.int32, (LS, PG), 0)
        for r in range(R16):
            e_ref[R16 + r] = u_ref[r]
        for r in range(1, R16):
            e_ref[r] = jnp.where(row == 0, 0.0, pltpu.roll(u_ref[r], 1, 0))

        def double(s):
            for i in range(2 * R16 - 1, 2 * s - 1, -1):
                e_ref[i] = e_ref[i] + e_ref[i - s]

        double(1)
        for k in (1, 2, 3):
            pl.when(g >= k)(functools.partial(double, 2 ** k))
        pw = pw_ref[...].astype(CDT)
        for r in range(R16):
            u = u_ref[r]
            pooled = e_ref[R16 + r] / _pool_counts(r, LS, win) - u
            pc = pooled.astype(CDT)
            mp = _dot(pc, pw, NN)
            z = z_ref[r]
            pg = (mp * ps_ref[...]) * (z * _sigmoid(z))
            mp_ref[r] = mp
            pg_ref[r] = pg.astype(pg_ref.dtype)
            pt_ref[:, r * LS:(r + 1) * LS] = pooled.T.astype(pt_ref.dtype)

    col = lambda off: pl.BlockSpec((R16, LS, PG), lambda g: (0, 0, off + g))
    tsp = pl.BlockSpec((PG, S), lambda g: (g, 0))
    pt, mp, pg = pl.pallas_call(
        body, name="pool_fwd", grid=(4,),
        in_specs=[col(24), col(4), pl.BlockSpec((None, PG, PG), lambda g: (g, 0, 0)),
                  pl.BlockSpec((1, PG), lambda g: (0, g))],
        out_specs=[tsp, col(0), col(0)],
        out_shape=[_sds((AW, S), CDT), _sds((R16, LS, AW), F32), _sds((R16, LS, AW), CDT)],
        scratch_shapes=[pltpu.VMEM((2 * R16, LS, PG), F32)],
        compiler_params=_cp(("parallel",)),
    )(r3, r3, pool_w, pool_scale)
    return pt, mp.reshape(S, AW), pg.reshape(S, AW)


def _branch_merge(ag, pg, wab, wpb, rest):
    S = ag.shape[0]
    tm, tn = 512, 512

    def body(ag_ref, pg_ref, wa_ref, wp_ref, ga_ref, gp_ref, m_ref, mt_ref):
        ya = _dot(ag_ref[...], wa_ref[...], NN)
        yp = _dot(pg_ref[...], wp_ref[...], NN)
        mg = _sigmoid(ga_ref[...]) * ya + _sigmoid(gp_ref[...]) * yp
        m_ref[...] = mg.astype(m_ref.dtype)
        mt_ref[...] = mg.T.astype(mt_ref.dtype)

    a = pl.BlockSpec((tm, AW), lambda i, j: (i, 0))
    w = pl.BlockSpec((AW, tn), lambda i, j: (0, j))
    o = pl.BlockSpec((tm, tn), lambda i, j: (i, j))
    return pl.pallas_call(
        body, name="branch_merge", grid=(S // tm, D // tn),
        in_specs=[a, a, w, w, pl.BlockSpec((tm, tn), lambda i, j: (i, 2 + j)),
                  pl.BlockSpec((tm, tn), lambda i, j: (i, 4 + j))],
        out_specs=[o, pl.BlockSpec((tn, tm), lambda i, j: (j, i))],
        out_shape=[_sds((S, D), CDT), _sds((D, S), CDT)],
        compiler_params=_cp(("parallel", "parallel")),
    )(ag, pg, wab, wpb, rest, rest)


def _merge_out_loss(ag, pg, rest, wab, wpb, wout, x, target, gate, final_g):
    S = x.shape[0]
    LS = S // R16

    def body(ag_ref, pg_ref, ga_ref, gp_ref, wa_ref, wp_ref, w_ref, x_ref, t_ref, gate_ref, fg_ref,
             dx2_ref, dmo_ref, loss_ref, gfg_ref, dgate_ref, dw_ref):
        i = pl.program_id(0)

        @pl.when(i == 0)
        def _():
            loss_ref[...] = jnp.zeros_like(loss_ref)
            gfg_ref[...] = jnp.zeros_like(gfg_ref)
            dgate_ref[...] = jnp.zeros_like(dgate_ref)
            dw_ref[...] = jnp.zeros_like(dw_ref)

        swap = _swap16()
        ya = _dot(_ld(ag_ref), wa_ref[...], NN)
        yp = _dot(_ld(pg_ref), wp_ref[...], NN)
        mg = (_sigmoid(_ld(ga_ref)) * ya + _sigmoid(_ld(gp_ref)) * yp).astype(CDT)
        m_nat = _dot(swap, mg, NN)
        mo = _dot(m_nat.astype(CDT), w_ref[...], NN)
        gate, fg = gate_ref[...], fg_ref[...]
        x2 = x_ref[...] + gate * mo
        r2 = lax.rsqrt(jnp.mean(x2 * x2, axis=-1, keepdims=True) + EPS)
        xn2 = x2 * r2
        diff = xn2 * fg - t_ref[...]
        loss = jnp.sum(jnp.sum(diff * diff, axis=-1, keepdims=True), axis=0, keepdims=True)
        dy = diff * (1.0 / D)
        dxn2 = dy * fg
        dx2 = r2 * (dxn2 - xn2 * jnp.mean(dxn2 * xn2, axis=-1, keepdims=True))
        dx2_ref[...] = dx2
        dmo = (dx2 * gate).astype(CDT)
        _st(dmo_ref, _dot(swap, dmo, NN))
        loss_ref[...] += jnp.broadcast_to(loss * (0.5 / D), loss_ref.shape)
        gfg_ref[...] += jnp.sum(dy * xn2, axis=0, keepdims=True)
        dgate_ref[...] += jnp.sum(dx2 * mo, axis=0, keepdims=True)
        dw_ref[...] += _dot(m_nat.T.astype(CDT), dmo, NN)

    def slab(w, cblk):
        return pl.BlockSpec((R16, R16, w), lambda i: (0, i, cblk))

    nat = pl.BlockSpec((TOK, D), lambda i: (i, 0))
    vec = pl.BlockSpec((1, D), lambda i: (0, 0))
    r3 = rest.reshape(R16, LS, REST_W)
    return pl.pallas_call(
        body, name="merge_out_loss", grid=(S // TOK,),
        in_specs=[slab(AW, 0), slab(AW, 0), slab(D, 1), slab(D, 2), _resident((AW, D)), _resident((AW, D)),
                  _resident((D, D)), nat, nat, vec, vec],
        out_specs=[nat, slab(D, 0), pl.BlockSpec((1, 128), lambda i: (0, 0)), vec, vec,
                   pl.BlockSpec((D, D), lambda i: (0, 0))],
        out_shape=[_sds((S, D), F32), _sds((R16, LS, D), CDT), _sds((1, 128), F32), _sds((1, D), F32),
                   _sds((1, D), F32), _sds((D, D), F32)],
        compiler_params=_cp(("arbitrary",)),
    )(ag.reshape(R16, LS, AW), pg.reshape(R16, LS, AW), r3, r3, wab, wpb, wout, x, target, gate, final_g)


def _out_loss(merged, wout, x, target, gate, final_g):
    S = x.shape[0]

    def body(m_ref, w_ref, x_ref, t_ref, gate_ref, fg_ref, dx2_ref, dmo_ref, loss_ref, gfg_ref, dgate_ref):
        i = pl.program_id(0)

        @pl.when(i == 0)
        def _():
            loss_ref[...] = jnp.zeros_like(loss_ref)
            gfg_ref[...] = jnp.zeros_like(gfg_ref)
            dgate_ref[...] = jnp.zeros_like(dgate_ref)

        swap = _swap16()
        mo = _dot(_dot(swap, _ld(m_ref), NN).astype(CDT), w_ref[...], NN)
        gate, fg = gate_ref[...], fg_ref[...]
        x2 = x_ref[...] + gate * mo
        r2 = lax.rsqrt(jnp.mean(x2 * x2, axis=-1, keepdims=True) + EPS)
        xn2 = x2 * r2
        diff = xn2 * fg - t_ref[...]
        loss = jnp.sum(jnp.sum(diff * diff, axis=-1, keepdims=True), axis=0, keepdims=True)
        dy = diff * (1.0 / D)
        dxn2 = dy * fg
        dx2 = r2 * (dxn2 - xn2 * jnp.mean(dxn2 * xn2, axis=-1, keepdims=True))
        dx2_ref[...] = dx2
        _st(dmo_ref, _dot(swap, (dx2 * gate).astype(CDT), NN))
        loss_ref[...] += jnp.broadcast_to(loss * (0.5 / D), loss_ref.shape)
        gfg_ref[...] += jnp.sum(dy * xn2, axis=0, keepdims=True)
        dgate_ref[...] += jnp.sum(dx2 * mo, axis=0, keepdims=True)

    slab = pl.BlockSpec((R16, R16, D), lambda i: (0, i, 0))
    nat = pl.BlockSpec((TOK, D), lambda i: (i, 0))
    vec = pl.BlockSpec((1, D), lambda i: (0, 0))
    return pl.pallas_call(
        body, name="out_loss", grid=(S // TOK,),
        in_specs=[slab, _resident((D, D)), nat, nat, vec, vec],
        out_specs=[nat, slab, pl.BlockSpec((1, 128), lambda i: (0, 0)), vec, vec],
        out_shape=[_sds((S, D), F32), _sds((R16, S // R16, D), CDT), _sds((1, 128), F32), _sds((1, D), F32),
                   _sds((1, D), F32)],
        compiler_params=_cp(("arbitrary",)),
    )(merged, wout, x, target, gate, final_g)


def _bwd_gates(dmo, wout, wab, wpb, rest, ag, pg, attn, mp, pool_scale):
    S = dmo.shape[0]
    tm = 256

    def body(dmo_ref, wo_ref, wa_ref, wp_ref, za_ref, zp_ref, ga0, ga1, gp0, gp1, ag_ref, pg_ref, at_ref, mp_ref,
             ps_ref, dwa_ref, dwp_ref, dr_ref, dat_ref, dl_ref, dmp_ref, dps_ref):
        i = pl.program_id(0)

        @pl.when(i == 0)
        def _():
            dps_ref[...] = jnp.zeros_like(dps_ref)
            dwa_ref[...] = jnp.zeros_like(dwa_ref)
            dwp_ref[...] = jnp.zeros_like(dwp_ref)

        dm = _dot(dmo_ref[...], wo_ref[...], NT)
        sa = _sigmoid(jnp.concatenate([ga0[...], ga1[...]], axis=1))
        sp = _sigmoid(jnp.concatenate([gp0[...], gp1[...]], axis=1))
        dya = (dm * sa).astype(CDT)
        dyp = (dm * sp).astype(CDT)
        ag, pg = ag_ref[...], pg_ref[...]
        dwa_ref[...] += _dot(ag.astype(F32).T.astype(CDT), dya, NN)
        dwp_ref[...] += _dot(pg.astype(F32).T.astype(CDT), dyp, NN)
        ya = _dot(ag, wa_ref[...], NN)
        yp = _dot(pg, wp_ref[...], NN)
        dr_ref[:, 1024:2048] = (dm * ya * sa * (1.0 - sa)).astype(dr_ref.dtype)
        dr_ref[:, 2048:3072] = (dm * yp * sp * (1.0 - sp)).astype(dr_ref.dtype)
        da = _dot(dya, wa_ref[...], NT)
        dp = _dot(dyp, wp_ref[...], NT)
        za = za_ref[...]
        sga = _sigmoid(za)
        attn = at_ref[...]
        dattn = da * (za * sga)
        dr_ref[:, 0:512] = (da * attn * (sga * (1.0 + za * (1.0 - sga)))).astype(dr_ref.dtype)
        dat_ref[...] = dattn.astype(dat_ref.dtype)
        prod = dattn * attn
        low = lax.broadcasted_iota(jnp.int32, (1, 2 * HD), 1) < HD
        for j in range(NH // 2):
            x2 = prod[:, 2 * HD * j:2 * HD * (j + 1)]
            s0 = jnp.sum(jnp.where(low, x2, 0.0), axis=1, keepdims=True)
            s1 = jnp.sum(jnp.where(low, 0.0, x2), axis=1, keepdims=True)
            dl_ref[:, 2 * HD * j:2 * HD * (j + 1)] = jnp.where(low, s0, s1)
        zp = zp_ref[...]
        sgp = _sigmoid(zp)
        mpre = mp_ref[...]
        ps = ps_ref[...]
        dmixed = dp * (zp * sgp)
        dr_ref[:, 512:1024] = (dp * (mpre * ps) * (sgp * (1.0 + zp * (1.0 - sgp)))).astype(dr_ref.dtype)
        dps_ref[...] += jnp.sum(dmixed * mpre, axis=0, keepdims=True)
        dmp_ref[...] = (dmixed * ps).astype(dmp_ref.dtype)

    full = lambda shape: pl.BlockSpec(shape, lambda i: (0, 0))
    tD = pl.BlockSpec((tm, D), lambda i: (i, 0))
    tA = pl.BlockSpec((tm, AW), lambda i: (i, 0))
    rb = lambda c: pl.BlockSpec((tm, AW), lambda i: (i, c))
    return pl.pallas_call(
        body, name="bwd_gates", grid=(S // tm,),
        in_specs=[tD, full((D, D)), full((AW, D)), full((AW, D)), rb(0), rb(1), rb(2), rb(3), rb(4), rb(5),
                  tA, tA, tA, tA, full((1, AW))],
        out_specs=[full((AW, D)), full((AW, D)), pl.BlockSpec((tm, 3072), lambda i: (i, 0)), tA, tA, tA, full((1, AW))],
        out_shape=[_sds((AW, D), F32), _sds((AW, D), F32), _sds((S, REST_W), CDT), _sds((S, AW), CDT),
                   _sds((S, AW), F32), _sds((S, AW), CDT), _sds((1, AW), F32)],
        compiler_params=_cp(("arbitrary",), vmem=56),
    )(dmo, wout, wab, wpb, rest, rest, rest, rest, rest, rest, ag, pg, attn, mp, pool_scale)


def _pool_bwd(dmp, pt, pool_w, drest):
    S = dmp.shape[0]
    LS = S // R16
    PG = 128
    d3 = dmp.reshape(R16, LS, AW)
    dr3 = drest.reshape(R16, LS, REST_W)

    def body(d_ref, pt_ref, pw_ref, dr_in, du_ref, dpw_ref, g_ref, s_ref):
        del dr_in
        g = pl.program_id(0)
        win = jnp.left_shift(2, g)
        row = lax.broadcasted_iota(jnp.int32, (LS, PG), 0)
        pw = pw_ref[...].astype(CDT)
        dpw = jnp.zeros((PG, PG), F32)
        for r in range(R16):
            dm = d_ref[r]
            dpw = dpw + _dot(pt_ref[:, r * LS:(r + 1) * LS], dm, NN)
            dpooled = _dot(dm, pw, NT)
            g_ref[r] = dpooled
            sc = dpooled / _pool_counts(r, LS, win)
            s_ref[r] = sc
            if r < R16 - 1:
                s_ref[R16 + r] = jnp.where(row == LS - 1, 0.0, pltpu.roll(sc, LS - 1, 0))
        dpw_ref[...] = dpw

        def double(s):
            for i in range(0, 2 * R16 - 2 * s):
                s_ref[i] = s_ref[i] + s_ref[i + s]

        double(1)
        for k in (1, 2, 3):
            pl.when(g >= k)(functools.partial(double, 2 ** k))
        for r in range(R16):
            du_ref[r] = (s_ref[r] - g_ref[r]).astype(du_ref.dtype)

    du, dpw = pl.pallas_call(
        body, name="pool_bwd", grid=(4,),
        in_specs=[pl.BlockSpec((R16, LS, PG), lambda g: (0, 0, g)), pl.BlockSpec((PG, S), lambda g: (g, 0)),
                  pl.BlockSpec((None, PG, PG), lambda g: (g, 0, 0)), pl.BlockSpec(memory_space=pl.ANY)],
        out_specs=[pl.BlockSpec((R16, LS, PG), lambda g: (0, 0, 24 + g)),
                   pl.BlockSpec((None, PG, PG), lambda g: (g, 0, 0))],
        out_shape=[_sds((R16, LS, REST_W), CDT), _sds((4, PG, PG), F32)],
        scratch_shapes=[pltpu.VMEM((R16, LS, PG), F32), pltpu.VMEM((2 * R16, LS, PG), F32)],
        input_output_aliases={3: 0},
        compiler_params=_cp(("parallel",)),
    )(d3, pt, pool_w, dr3)
    return du.reshape(S, REST_W), dpw


def _attn_bwd(qkv, dattn, Lb, dlb, biases, geom, dqkv_prev):
    S = qkv.shape[0]
    gi, T, R, has_prev, N = geom.gi, geom.T, geom.rows, geom.has_prev, geom.n_tiles
    H2 = T // 2
    nd_outer = len(geom.outer)
    qv = geom.view(qkv)
    cq, ck, cv = 3 * gi, 3 * gi + 1, 3 * gi + 2
    W3 = 3 * AW

    def heads(q_ref, kp_ref, kc_ref, vp_ref, vc_ref, do_ref, L_ref, dl_ref, ba_ref, bb_ref, pm, dsa_ref, dsb_ref, emit):
        q_lo, q_hi = _split_rows(_ld(q_ref), R)
        do_lo, do_hi = _split_rows(_ld(do_ref), R)
        L_lo, L_hi = _split_rows(_ld(L_ref), R)
        dl_lo, dl_hi = _split_rows(_ld(dl_ref), R)
        kc_lo, kc_hi = _split_rows(_ld(kc_ref), R)
        vc_lo, vc_hi = _split_rows(_ld(vc_ref), R)
        k_b = jnp.concatenate([kc_lo, kc_hi], axis=0)
        v_b = jnp.concatenate([vc_lo, vc_hi], axis=0)
        if has_prev:
            k_a = jnp.concatenate([_split_rows(_ld(kp_ref), R)[1], kc_lo], axis=0)
            v_a = jnp.concatenate([_split_rows(_ld(vp_ref), R)[1], vc_lo], axis=0)
        else:
            k_a, v_a = k_b, v_b
        low = lax.broadcasted_iota(jnp.int32, (1, 2 * HD), 1) < HD
        for j in range(NH // 2):
            sl = slice(2 * HD * j, 2 * HD * (j + 1))
            res = []
            for q, k, v, do, L, dl, b_ref, mask, ds_ref in (
                    (q_lo, k_a, v_a, do_lo, L_lo, dl_lo, ba_ref, pm, dsa_ref),
                    (q_hi, k_b, v_b, do_hi, L_hi, dl_hi, bb_ref, None, dsb_ref)):
                q2 = q[:, sl].astype(F32) * 0.125
                do2 = do[:, sl].astype(F32)
                k2, v2 = k[:, sl], v[:, sl]
                L2, dl2 = L[:, sl], dl[:, sl]
                qs = jnp.concatenate([jnp.where(low, q2, 0.0), jnp.where(low, 0.0, q2)], axis=0).astype(q.dtype)
                dom = jnp.concatenate([jnp.where(low, do2, 0.0), jnp.where(low, 0.0, do2)], axis=0).astype(q.dtype)
                Lc = jnp.concatenate([L2[:, 0:1], L2[:, HD:HD + 1]], axis=0)
                dlc = jnp.concatenate([dl2[:, 0:1], dl2[:, HD:HD + 1]], axis=0)
                s = _dot(qs, k2, NT) + b_ref[j]
                if mask is not None:
                    s = s + mask
                p = jnp.exp(s - Lc)
                ds = p * (_dot(dom, v2, NT) - dlc)
                ds_ref[j] += ds
                dsc = ds.astype(q.dtype)
                dq_st = _dot(dsc, k2, NN)
                res.append((jnp.where(low, dq_st[:H2], dq_st[H2:]) * 0.125, _dot(dsc, qs, TN),
                            _dot(p.astype(q.dtype), dom, TN)))
            (dq_lo, dk_a, dv_a), (dq_hi, dk_b, dv_b) = res
            dq = _merge_rows(dq_lo, dq_hi, R)
            if has_prev:
                emit(sl, dq, _merge_rows(dk_a[H2:] + dk_b[:H2], dk_b[H2:], R),
                     _merge_rows(dv_a[H2:] + dv_b[:H2], dv_b[H2:], R), dk_a[:H2], dv_a[:H2])
            else:
                emit(sl, dq, _merge_rows(dk_a[:H2] + dk_b[:H2], dk_a[H2:] + dk_b[H2:], R),
                     _merge_rows(dv_a[:H2] + dv_b[:H2], dv_a[H2:] + dv_b[H2:], R), None, None)

    def first_step():
        ok = pl.program_id(nd_outer) == 0 if has_prev else pl.program_id(0) == 0
        for a in range(nd_outer if has_prev else 0):
            ok = ok & (pl.program_id(a) == 0)
        return ok

    if has_prev:
        def body(q_ref, kp_ref, kc_ref, vp_ref, vc_ref, do_ref, L_ref, dl_ref, ba_ref, bb_ref, prev_in, out_ref,
                 dsa_ref, dsb_ref, hold, new, pp):
            del prev_in
            s_id = pl.program_id(nd_outer)

            @pl.when(first_step())
            def _():
                dsa_ref[...] = jnp.zeros_like(dsa_ref)
                dsb_ref[...] = jnp.zeros_like(dsb_ref)

            @pl.when(s_id < N)
            def _():
                def emit(sl, dq, dk, dv, dk_prev, dv_prev):
                    new[:, sl] = dq
                    new[:, slice(AW + sl.start, AW + sl.stop)] = dk
                    new[:, slice(2 * AW + sl.start, 2 * AW + sl.stop)] = dv
                    pp[:, sl] = dk_prev
                    pp[:, slice(AW + sl.start, AW + sl.stop)] = dv_prev

                heads(q_ref, kp_ref, kc_ref, vp_ref, vc_ref, do_ref, L_ref, dl_ref, ba_ref, bb_ref,
                      _prev_mask(s_id, H2, T), dsa_ref, dsb_ref, emit)

            @pl.when((s_id >= 1) & (s_id < N))
            def _():
                _st(out_ref, hold[:, 0:AW], slice(0, AW))
                late = _merge_rows(jnp.zeros((H2, 2 * AW), F32), pp[...], R)
                _st(out_ref, hold[:, AW:W3] + late, slice(AW, W3))

            @pl.when(s_id == N)
            def _():
                _st(out_ref, hold[...])

            @pl.when(s_id < N)
            def _():
                hold[...] = new[...]

        cur = lambda n: jnp.minimum(n, N - 1)
        prev = lambda n: jnp.clip(n - 1, 0, N - 1)
        delayed = lambda n: jnp.maximum(n - 1, 0)
        in_specs = [geom.spec(AW, cq, cur), geom.spec(AW, ck, prev), geom.spec(AW, ck, cur),
                    geom.spec(AW, cv, prev), geom.spec(AW, cv, cur),
                    geom.spec(AW, 0, cur), geom.spec(AW, 0, cur), geom.spec(AW, 0, cur)]
        args = [qv] * 5 + [geom.view(dattn), geom.view(Lb), geom.view(dlb)]
        out_spec = geom.spec(W3, gi, delayed)
        grid = geom.outer + (N + 1,)
        scratch = [pltpu.VMEM((T, W3), F32), pltpu.VMEM((T, W3), F32), pltpu.VMEM((H2, 2 * AW), F32)]
    else:
        def body(q_ref, kc_ref, vc_ref, do_ref, L_ref, dl_ref, ba_ref, bb_ref, prev_in, out_ref, dsa_ref, dsb_ref):
            del prev_in

            @pl.when(first_step())
            def _():
                dsa_ref[...] = jnp.zeros_like(dsa_ref)
                dsb_ref[...] = jnp.zeros_like(dsb_ref)

            def emit(sl, dq, dk, dv, dk_prev, dv_prev):
                _st(out_ref, dq, sl)
                _st(out_ref, dk, slice(AW + sl.start, AW + sl.stop))
                _st(out_ref, dv, slice(2 * AW + sl.start, 2 * AW + sl.stop))

            heads(q_ref, None, kc_ref, None, vc_ref, do_ref, L_ref, dl_ref, ba_ref, bb_ref, None, dsa_ref, dsb_ref,
                  emit)

        cur = lambda n: n
        in_specs = [geom.spec(AW, cq, cur), geom.spec(AW, ck, cur), geom.spec(AW, cv, cur),
                    geom.spec(AW, 0, cur), geom.spec(AW, 0, cur), geom.spec(AW, 0, cur)]
        args = [qv] * 3 + [geom.view(dattn), geom.view(Lb), geom.view(dlb)]
        out_spec = geom.spec(W3, gi, cur)
        grid = geom.outer
        scratch = []

    whole = lambda b: pl.BlockSpec(b.shape, lambda *ids: (0, 0, 0))
    biases = [b.reshape(NH // 2, 2 * H2, b.shape[2]) for b in biases]
    in_specs += [whole(b) for b in biases]
    in_specs.append(pl.BlockSpec(memory_space=pl.ANY))
    if dqkv_prev is None:
        last, aliases = biases[0], {}
    else:
        last, aliases = geom.view(dqkv_prev), {len(in_specs) - 1: 0}
    dq, dsa, dsb = pl.pallas_call(
        body, name=f"attn_bwd{gi}", grid=grid, in_specs=in_specs,
        out_specs=[out_spec] + [whole(b) for b in biases],
        out_shape=[_sds(geom.view_shape(QKV_W), CDT)] + [_sds(b.shape, F32) for b in biases],
        scratch_shapes=scratch,
        input_output_aliases=aliases,
        compiler_params=_cp(("arbitrary",) * len(grid), vmem=56),
    )(*args, *biases, last)
    return geom.unview(dq), [d.reshape(NH, H2, d.shape[2]) for d in (dsa, dsb)]


def _h_bwd(dh, dx2, x, norm_g, scale):
    S = x.shape[0]

    def body(dh_ref, dx2_ref, x_ref, g_ref, sc_ref, gx_ref, dsh_ref, dsc_ref, dng_ref, nat_ref):
        i = pl.program_id(0)

        @pl.when(i == 0)
        def _():
            dsh_ref[...] = jnp.zeros_like(dsh_ref)
            dsc_ref[...] = jnp.zeros_like(dsc_ref)
            dng_ref[...] = jnp.zeros_like(dng_ref)

        for r in range(R16):
            nat_ref[:, r, :] = dh_ref[r]
        dh = jnp.concatenate([nat_ref[l] for l in range(R16)], axis=0)
        g = g_ref[...]
        one_sc = 1.0 + sc_ref[...]
        xv = x_ref[...]
        r1 = lax.rsqrt(jnp.mean(xv * xv, axis=-1, keepdims=True) + EPS)
        xn = xv * r1
        dhx = dh * xn
        dxn = dh * (g * one_sc)
        dx1 = r1 * (dxn - xn * jnp.mean(dxn * xn, axis=-1, keepdims=True))
        gx_ref[...] = dx2_ref[...] + dx1
        dsh_ref[...] += jnp.sum(dh, axis=0, keepdims=True)
        dsc_ref[...] += jnp.sum(dhx * g, axis=0, keepdims=True)
        dng_ref[...] += jnp.sum(dhx * one_sc, axis=0, keepdims=True)

    nat = pl.BlockSpec((TOK, D), lambda i: (i, 0))
    vec = pl.BlockSpec((1, D), lambda i: (0, 0))
    return pl.pallas_call(
        body, name="h_bwd", grid=(S // TOK,),
        in_specs=[pl.BlockSpec((R16, R16, D), lambda i: (0, i, 0)), nat, nat, vec, vec],
        out_specs=[nat, vec, vec, vec],
        out_shape=[_sds((S, D), F32), _sds((1, D), F32), _sds((1, D), F32), _sds((1, D), F32)],
        scratch_shapes=[pltpu.VMEM((R16, R16, D), F32)],
        compiler_params=_cp(("arbitrary",)),
    )(dh, dx2, x, norm_g, scale)


def _local_step(x, target, shift, scale, gate, norm_g, w_in_full, pool_w, pool_scale, wab, wpb, wout, rel_bias,
                final_g):
    st = _local_pre(x, shift, scale, norm_g, rel_bias)
    st = _local_grads(st, target, gate, w_in_full, pool_w, pool_scale, wab, wpb, wout, final_g)
    d_rel_bias = _local_bias_grads(st, _BIAS_GRADS_EARLY + _BIAS_GRADS_LATE)
    gx, dmod, dng = _local_input_grads(st, w_in_full, norm_g, scale)
    return dict(loss=st["loss"], grad_x=gx, dw_in=st["dw_in"], dw_ab=st["dw_ab"], dw_pb=st["dw_pb"],
                dw_out=st["dw_out"], d_norm_g=dng, d_pool_w=st["dpw"], d_pool_scale=st["dps"],
                d_rel_bias=d_rel_bias.T, d_final_g=st["gfg"], dmod=dmod)


def _local_pre(x, shift, scale, norm_g, rel_bias, dep=None):
    S = x.shape[0]
    LS = S // R16
    geoms = [_Geom(gi, LS) for gi in range(3)]
    idx_np = [[g.bucket_index(sub) for sub in range(2)] for g in geoms]
    buckets = [[tuple(int(b) for b in np.unique(ix) if b >= 0) for ix in pair] for pair in idx_np]
    idxs = [[jnp.asarray(ix) for ix in pair] for pair in idx_np]
    biases = [[_bias_table(idxs[gi][sub], buckets[gi][sub], rel_bias, gi, sub, dep) for sub in range(2)]
              for gi in range(3)]
    h = _norm_mod(x, norm_g, shift, scale, dep).reshape(S, D)
    return dict(geoms=geoms, idxs=idxs, buckets=buckets, biases=biases, x=x, h=h, ht=_tcast(h, "h_transpose"))


def _local_grads(st, target, gate, w_in_full, pool_w, pool_scale, wab, wpb, wout, final_g):
    geoms, idxs, biases, x, h, ht = (st[k] for k in ("geoms", "idxs", "biases", "x", "h", "ht"))
    S = h.shape[0]
    LS = S // R16
    qkv, rest = _proj(h, w_in_full)

    os_, ls_ = [], []
    for gi in range(2):
        o, l = _attn_fwd(qkv, biases[gi], geoms[gi])
        os_.append(o)
        ls_.append(l)
    attn, Lb, ag = _attn_fwd(qkv, biases[2], geoms[2], others=(os_, ls_, rest))
    pt, mp, pg = _pool_fwd(rest, pool_w, pool_scale)
    dx2, dmo3, loss, gfg, dgate, dw_out = _merge_out_loss(ag, pg, rest, wab, wpb, wout, x, target, gate, final_g)
    dmo = dmo3.reshape(S, D)

    dw_ab, dw_pb, drest, dattn, dlb, dmp, dps = _bwd_gates(dmo, wout, wab, wpb, rest, ag, pg, attn, mp, pool_scale)
    drest, dpw = _pool_bwd(dmp, pt, pool_w, drest)

    dqkv = None
    dsums = []
    for gi in range(3):
        dqkv, dsum = _attn_bwd(qkv, dattn, Lb, dlb, biases[gi], geoms[gi], dqkv)
        dsums.append(dsum)

    dw_in = _mm_tn(ht, dqkv, n_total=8192, out_blk=lambda j: j, name="dw_in_qkv", with_low=True)
    dw_in, dw_in_low = _mm_tn(ht, drest, n_total=8192, out_blk=_rest_blk, name="dw_in_rest", prev=dw_in, with_low=True)
    return dict(st, loss=loss[0, 0], dw_in=dw_in, dw_in_low=dw_in_low, dw_ab=dw_ab, dw_pb=dw_pb, dw_out=dw_out, dpw=dpw, dps=dps, gfg=gfg,
                dgate=dgate, dsums=dsums, dqkv=dqkv, drest=drest, dx2=dx2)


def _local_bias_grads(st, which, dep=None):
    rows = []
    for gi in range(3):
        got = [_bias_grad(st["dsums"][gi][sub], st["idxs"][gi][sub], st["buckets"][gi][sub], gi, sub, dep)
               for sub in range(2) if (gi, sub) in which]
        rows.append(sum(got) if got else jnp.zeros((NH, 32), F32))
    return jnp.concatenate(rows, axis=0)


_BIAS_GRADS_EARLY = ((0, 0), (0, 1), (1, 0), (2, 0))
_BIAS_GRADS_LATE = ((1, 1), (2, 1))


def _dh_h_bwd(dqkv, drest, w, dx2, x, norm_g, scale, dep=None):
    S = x.shape[0]
    dep_specs, dep_args = _after(dep)

    def body(a_ref, b_ref, w_ref, dx2_ref, x_ref, g_ref, sc_ref, *rest):
        gx_ref, dsh_ref, dsc_ref, dng_ref, nat_ref = rest[-5:]
        i = pl.program_id(0)

        @pl.when(i == 0)
        def _():
            dsh_ref[...] = jnp.zeros_like(dsh_ref)
            dsc_ref[...] = jnp.zeros_like(dsc_ref)
            dng_ref[...] = jnp.zeros_like(dng_ref)

        a, b = _ld(a_ref), _ld(b_ref)
        acc = _dot(a[:, 0:512], _w512(w_ref, 0), NT)
        for j in range(1, 9):
            acc = acc + _dot(a[:, 512 * j:512 * (j + 1)], _w512(w_ref, j), NT)
        off = 0
        for lo, hi in REST_COLS:
            for j in range(lo, hi):
                acc = acc + _dot(b[:, off:off + 512], _w512(w_ref, j), NT)
                off += 512
        for r in range(R16):
            nat_ref[:, r, :] = acc[R16 * r:R16 * (r + 1)]
        dh = jnp.concatenate([nat_ref[l] for l in range(R16)], axis=0)
        g = g_ref[...]
        one_sc = 1.0 + sc_ref[...]
        xv = x_ref[...]
        r1 = lax.rsqrt(jnp.mean(xv * xv, axis=-1, keepdims=True) + EPS)
        xn = xv * r1
        dhx = dh * xn
        dxn = dh * (g * one_sc)
        dx1 = r1 * (dxn - xn * jnp.mean(dxn * xn, axis=-1, keepdims=True))
        gx_ref[...] = dx2_ref[...] + dx1
        dsh_ref[...] += jnp.sum(dh, axis=0, keepdims=True)
        dsc_ref[...] += jnp.sum(dhx * g, axis=0, keepdims=True)
        dng_ref[...] += jnp.sum(dhx * one_sc, axis=0, keepdims=True)

    nat = pl.BlockSpec((TOK, D), lambda i: (i, 0))
    vec = pl.BlockSpec((1, D), lambda i: (0, 0))
    slab = lambda W: pl.BlockSpec((R16, R16, W), lambda i: (0, i, 0))
    return pl.pallas_call(
        body, name="dh_h_bwd", grid=(S // TOK,),
        in_specs=[slab(QKV_W), slab(REST_W), _resident((8, D, 1024)), nat, nat, vec, vec] + dep_specs,
        out_specs=[nat, vec, vec, vec],
        out_shape=[_sds((S, D), F32), _sds((1, D), F32), _sds((1, D), F32), _sds((1, D), F32)],
        scratch_shapes=[pltpu.VMEM((R16, R16, D), F32)],
        compiler_params=_cp(("arbitrary",), vmem=56),
    )(dqkv, drest, w, dx2, x, norm_g, scale, *dep_args)


def _local_input_grads(st, w_in_full, norm_g, scale, dep=None):
    S = st["h"].shape[0]
    LS = S // R16
    gx, dsh, dsc, dng = _dh_h_bwd(st["dqkv"].reshape(R16, LS, QKV_W), st["drest"].reshape(R16, LS, REST_W),
                                  w_in_full, st["dx2"], st["x"], norm_g, scale, dep)
    return gx, jnp.concatenate([dsh, dsc, st["dgate"]], axis=1), dng


def _my_place():
    return lax.axis_index("x"), lax.axis_index("y"), lax.axis_index("c")


def _block_of(ref, axis, idx, width):
    if axis is None:
        return ref.at[idx]
    start = pl.multiple_of(idx * width, width)
    if axis == 0:
        return ref.at[pl.ds(start, width)]
    return ref.at[:, pl.ds(start, width)]


_SEM = pl.BlockSpec(memory_space=pltpu.SEMAPHORE)
_EFFECT = pltpu.SideEffectType.DATAFLOW_SIDE_EFFECTING


def _split_start(name, srcs, lands, ncopy, plan, dep=None):
    arrs = list(srcs) + list(lands)
    na, ns = len(arrs), len(srcs)
    dep_specs, dep_args = _after(dep)

    def body(*refs):
        send_sems, recv_sems, token = refs[na + len(dep_args)], refs[na + len(dep_args) + 1], refs[-1]
        for k, (s, d, dev) in enumerate(plan(refs[:ns], refs[ns:na])):
            pltpu.make_async_remote_copy(src_ref=s, dst_ref=d, send_sem=send_sems.at[k], recv_sem=recv_sems.at[k],
                                         device_id=dev, device_id_type=MESH).start()
        token[...] = jnp.zeros_like(token)

    outs = pl.pallas_call(
        body, name=name,
        out_shape=(pltpu.SemaphoreType.DMA((ncopy,)), pltpu.SemaphoreType.DMA((ncopy,)),
                   *[pltpu.HBM(a.shape, a.dtype) for a in arrs], _sds((8, 128), F32)),
        in_specs=[_HBM] * na + dep_specs,
        out_specs=(_SEM, _SEM, *[_HBM] * na, pl.BlockSpec(memory_space=pltpu.VMEM)),
        input_output_aliases={i: 2 + i for i in range(na)},
        compiler_params=pltpu.CompilerParams(has_side_effects=_EFFECT),
    )(*[pltpu.with_memory_space_constraint(a, pltpu.HBM) for a in arrs], *dep_args)
    return (outs[0], outs[1]), list(outs[2:2 + ns]), list(outs[2 + ns:2 + na]), outs[-1]


def _split_wait(name, sems, srcs, lands, plan, after):
    arrs = list(srcs) + list(lands)
    na, ns = len(arrs), len(srcs)
    after = list(after) if isinstance(after, (list, tuple)) else [after]

    def body(*refs):
        send_sems, recv_sems = refs[na], refs[na + 1]
        for k, (s, d, dev) in enumerate(plan(refs[:ns], refs[ns:na])):
            cp = pltpu.make_async_remote_copy(src_ref=s, dst_ref=d, send_sem=send_sems.at[k], recv_sem=recv_sems.at[k],
                                              device_id=dev, device_id_type=MESH)
            cp.wait_send()
            cp.wait_recv()

    outs = pl.pallas_call(
        body, name=name, out_shape=[pltpu.HBM(a.shape, a.dtype) for a in arrs],
        in_specs=[_HBM] * na + [_SEM, _SEM] + [pl.BlockSpec(memory_space=pl.ANY)] * len(after),
        out_specs=[_HBM] * na,
        input_output_aliases={i: i for i in range(na)},
        compiler_params=pltpu.CompilerParams(has_side_effects=_EFFECT),
    )(*arrs, sems[0], sems[1], *after)
    return list(outs[:ns]), list(outs[ns:])


def _gather_chips_plan(layout):
    def plan(src, land):
        x, y, c = _my_place()
        me = 4 * x + 2 * y + c
        mine = [_block_of(land[p], layout[p][0], me, layout[p][1]) for p in range(len(land))]
        return [(mine[p], mine[p], (cx, cy, c)) for p in range(len(land)) for cx, cy in [(1 - x, y), (x, 1 - y)]]
    return plan


def _pair_plan(layout):
    def plan(src, land):
        x, y, c = _my_place()
        n = len(layout)
        return [(_block_of(src[p], layout[p][0], 2 * q + (1 - c), layout[p][1]), land[p].at[q], (x, y, 1 - c))
                for p in range(n) for q in range(4)] + [(src[p], land[p], (x, y, 1 - c)) for p in range(n, len(src))]
    return plan


def _chips_plan(n):
    def plan(src, land):
        x, y, c = _my_place()
        chips = [(1 - x, y), (x, 1 - y), (1 - x, 1 - y)]
        return [(src[p].at[2 * cx + cy], land[p].at[2 * x + y], (cx, cy, c)) for p in range(n) for cx, cy in chips] + \
               [(src[p], land[p].at[2 * x + y], (cx, cy, c)) for p in range(n, len(src)) for cx, cy in chips]
    return plan


def _pair_share(gathered):
    n = len(gathered)

    def body(*refs):
        land = refs[n:2 * n]
        send_sems, recv_sems, fwd_send, fwd_recv = refs[2 * n:]
        x, y, c = _my_place()
        k_dg = 3 - (2 * x + y)
        south = c == 0
        pass_on = 4 * jnp.where(south, 1 - x, x) + 2 * jnp.where(south, y, 1 - y) + c
        to = (jnp.where(south, x, 1 - x), jnp.where(south, 1 - y, y), c)

        def forward(p):
            return pltpu.make_async_remote_copy(
                src_ref=land[p].at[pass_on], dst_ref=land[p].at[pass_on], send_sem=fwd_send.at[p],
                recv_sem=fwd_recv.at[p], device_id=to, device_id_type=MESH)

        def to_sibling(p, k):
            blk = land[p].at[2 * k + c]
            return pltpu.make_async_remote_copy(
                src_ref=blk, dst_ref=blk, send_sem=send_sems.at[4 * p + k], recv_sem=recv_sems.at[4 * p + k],
                device_id=(x, y, 1 - c), device_id_type=MESH)

        for p in range(n):
            forward(p).start()
        for p in range(n):
            for k in range(4):
                @pl.when(k != k_dg)
                def _():
                    to_sibling(p, k).start()
        for p in range(n):
            dg = land[p].at[2 * k_dg + c]
            pltpu.make_async_remote_copy(src_ref=dg, dst_ref=dg, send_sem=fwd_send.at[p], recv_sem=fwd_recv.at[p],
                                         device_id=to, device_id_type=MESH).wait_recv()
            for k in range(4):
                @pl.when(k == k_dg)
                def _():
                    to_sibling(p, k).start()
        for p in range(n):
            forward(p).wait_send()
            for k in range(4):
                to_sibling(p, k).wait_send()
                blk = land[p].at[2 * k + 1 - c]
                pltpu.make_async_remote_copy(
                    src_ref=blk, dst_ref=blk, send_sem=send_sems.at[4 * p + k], recv_sem=recv_sems.at[4 * p + k],
                    device_id=(x, y, 1 - c), device_id_type=MESH).wait_recv()

    return pl.pallas_call(
        body, name="gather_pair_share", in_specs=[_HBM] * n, out_specs=[_HBM] * n,
        out_shape=[_sds(g.shape, g.dtype) for g in gathered],
        input_output_aliases={p: p for p in range(n)},
        scratch_shapes=[pltpu.SemaphoreType.DMA((4 * n,)), pltpu.SemaphoreType.DMA((4 * n,)),
                        pltpu.SemaphoreType.DMA((n,)), pltpu.SemaphoreType.DMA((n,))],
    )(*gathered)


def _place_slot(arrs, slot, n_slots, name, out_dtype=None):
    n = len(arrs)

    def body(s_ref, *refs):
        for p in range(n):
            refs[n + p][...] = refs[p][...].astype(refs[n + p].dtype)

    def spec(a):
        tail = a.shape[-2:]
        return pl.BlockSpec((None,) + tail, lambda i, s_ref: (s_ref[0], 0, 0))

    in_specs = [spec(a) if a.ndim == 3 else pl.BlockSpec(a.shape, lambda i, s_ref: (0, 0)) for a in arrs]
    return pl.pallas_call(
        body, name=name,
        grid_spec=pltpu.PrefetchScalarGridSpec(num_scalar_prefetch=1, grid=(1,), in_specs=in_specs,
                                               out_specs=[spec(a) for a in arrs]),
        out_shape=[_in_hbm((n_slots,) + a.shape[-2:], out_dtype or a.dtype) for a in arrs],
        compiler_params=_cp(("arbitrary",)),
    )(slot, *[_pin(a) for a in arrs])


def _cols_from_blocks(w8, name):
    _, R, C = w8.shape

    def body(w_ref, o_ref):
        for e in range(8):
            o_ref[:, C * e:C * (e + 1)] = w_ref[e]

    return pl.pallas_call(body, name=name, out_shape=_sds((R, 8 * C), w8.dtype), compiler_params=_cp())(w8)


def _all_gather_small(srcs, name):
    n = len(srcs)
    vmem = pl.BlockSpec(memory_space=pltpu.VMEM)

    def body(*refs):
        src, out = refs[:n], refs[n:2 * n]
        send_sems, recv_sems = refs[2 * n:]
        x, y, c = _my_place()
        me = 4 * x + 2 * y + c
        flip = lambda v, f: 1 - v if f else v
        peers = [(flip(x, fx), flip(y, fy), flip(c, fc)) for fx in (0, 1) for fy in (0, 1) for fc in (0, 1)][1:]
        cps = []
        for p in range(n):
            out[p][me] = src[p][...]
            for k, peer in enumerate(peers):
                cps.append(pltpu.make_async_remote_copy(
                    src_ref=src[p], dst_ref=out[p].at[me], send_sem=send_sems.at[7 * p + k],
                    recv_sem=recv_sems.at[7 * p + k], device_id=peer, device_id_type=MESH))
        for cp in cps:
            cp.start()
        for cp in cps:
            cp.wait()

    return pl.pallas_call(
        body, name=name, in_specs=[vmem] * n, out_specs=[vmem] * n,
        out_shape=[_sds((8,) + a.shape, a.dtype) for a in srcs],
        scratch_shapes=[pltpu.SemaphoreType.DMA((7 * n,)), pltpu.SemaphoreType.DMA((7 * n,))],
    )(*srcs)


def _adaln_exchange(c, w_ada, b_part):
    vmem = pl.BlockSpec(memory_space=pltpu.VMEM)

    def body(c_ref, w_ref, b_ref, call_ref, mod_ref, part_ref, send_sems, recv_sems):
        x, y, c = _my_place()
        me = 4 * x + 2 * y + c
        flip = lambda v, f: 1 - v if f else v
        peers = [(flip(x, fx), flip(y, fy), flip(c, fc)) for fx in (0, 1) for fy in (0, 1) for fc in (0, 1)][1:]
        call_ref[me] = c_ref[...]
        first = [pltpu.make_async_remote_copy(src_ref=c_ref, dst_ref=call_ref.at[me], send_sem=send_sems.at[k],
                                              recv_sem=recv_sems.at[k], device_id=peer, device_id_type=MESH)
                 for k, peer in enumerate(peers)]
        for cp in first:
            cp.start()
        for cp in first:
            cp.wait()
        c_all = jnp.concatenate([call_ref[j] for j in range(8)], axis=0)
        part_ref[...] = _dot(c_all.astype(CDT), w_ref[...].astype(CDT), NN) + b_ref[...]
        mod_ref[me] = part_ref[pl.ds(me, 1), :]
        second = [pltpu.make_async_remote_copy(
            src_ref=part_ref.at[pl.ds(4 * px + 2 * py + pc, 1)], dst_ref=mod_ref.at[me], send_sem=send_sems.at[7 + k],
            recv_sem=recv_sems.at[7 + k], device_id=(px, py, pc), device_id_type=MESH)
            for k, (px, py, pc) in enumerate(peers)]
        for cp in second:
            cp.start()
        for cp in second:
            cp.wait()

    return pl.pallas_call(
        body, name="adaln_exchange", in_specs=[vmem] * 3, out_specs=[vmem] * 2,
        out_shape=[_sds((8, 1, D), F32), _sds((8, 1, w_ada.shape[1]), F32)],
        scratch_shapes=[pltpu.VMEM((8, w_ada.shape[1]), F32), pltpu.SemaphoreType.DMA((14,)),
                        pltpu.SemaphoreType.DMA((14,))],
        compiler_params=_cp(),
    )(c, w_ada, b_part)


def _pair_add(dw, recv, axis, width, cidx, name):
    if axis == 0:
        tr = min(width, 256)
        C = dw.shape[1]
        grid = (4, width // tr)
        nb = width // tr
        own = pl.BlockSpec((tr, C), lambda q, i, c_ref: ((2 * q + c_ref[0]) * nb + i, 0))
        oth = pl.BlockSpec((None, tr, C), lambda q, i, c_ref: (q, i, 0))
    else:
        R = dw.shape[0]
        tr = min(R, 1024)
        grid = (4, R // tr)
        own = pl.BlockSpec((tr, width), lambda q, i, c_ref: (i, 2 * q + c_ref[0]))
        oth = pl.BlockSpec((None, tr, width), lambda q, i, c_ref: (q, i, 0))

    def body(c_ref, a_ref, b_ref, o_ref):
        o_ref[...] = (a_ref[...] + b_ref[...].astype(F32)).astype(o_ref.dtype)

    return pl.pallas_call(
        body, name=name,
        grid_spec=pltpu.PrefetchScalarGridSpec(num_scalar_prefetch=1, grid=grid, in_specs=[own, oth], out_specs=oth),
        out_shape=(_in_hbm if _is_big(dw) else _sds)(recv.shape, CDT), compiler_params=_cp(("parallel", "parallel")),
    )(cidx, *[_pin(a) if _is_big(dw) else a for a in (dw, recv)])


def _add_pairs(xs, ys, name):
    n = len(xs)

    def body(*refs):
        for p in range(n):
            refs[2 * n + p][...] = refs[p][...] + refs[n + p][...]

    return pl.pallas_call(body, name=name, out_shape=[_sds(a.shape, F32) for a in xs])(*xs, *ys)


def _adam_math(w, g, m, v):
    m = ADAM_B1 * m + (1.0 - ADAM_B1) * g
    v = ADAM_B2 * v + (1.0 - ADAM_B2) * (g * g)
    m_hat = m / (1.0 - ADAM_B1 ** ADAM_STEP)
    v_hat = v / (1.0 - ADAM_B2 ** ADAM_STEP)
    delta = -ADAM_LR * (m_hat / (jnp.sqrt(v_hat) + ADAM_EPS) + ADAM_WD * w)
    return delta, m, v


def _adam(gparts, w, m, v, name, fresh=False):
    P, R, C = gparts.shape
    in_hbm = fresh or _is_big(w)
    tr = R if R <= 256 else 256

    def body(g_ref, w_ref, m_ref, v_ref, go_ref, d_ref, mo_ref, vo_ref):
        g = g_ref[0].astype(F32)
        for i in range(1, P):
            g = g + g_ref[i].astype(F32)
        d, mn, vn = _adam_math(w_ref[...], g, m_ref[...], v_ref[...])
        go_ref[...] = g
        d_ref[...] = d
        mo_ref[...] = mn
        vo_ref[...] = vn

    t = pl.BlockSpec((tr, C), lambda i: (i, 0))
    return pl.pallas_call(
        body, name=name, grid=(R // tr,),
        in_specs=[pl.BlockSpec((P, tr, C), lambda i: (0, i, 0)), t, t, t], out_specs=[t, t, t, t],
        out_shape=[(_in_hbm if in_hbm else _sds)((R, C), F32)] * 4, compiler_params=_cp(("parallel",)),
    )(*[_pin(a) if in_hbm else a for a in (gparts, w, m, v)])


def _sum_parts(gparts, name):
    P, R, C = gparts.shape

    def body(g_ref, o_ref):
        g = g_ref[0]
        for i in range(1, P):
            g = g + g_ref[i]
        o_ref[...] = g

    return pl.pallas_call(body, name=name, out_shape=_sds((R, C), F32))(gparts)


def _adam_w_ada(c_t, dm, w, m, v):
    R, C = w.shape

    def body(c_ref, dm_ref, w_ref, m_ref, v_ref, go_ref, d_ref, mo_ref, vo_ref):
        g = c_ref[:, 0:1] * dm_ref[0:1, :]
        for b in range(1, 8):
            g = g + c_ref[:, b:b + 1] * dm_ref[b:b + 1, :]
        d, mn, vn = _adam_math(w_ref[...], g, m_ref[...], v_ref[...])
        go_ref[...] = g
        d_ref[...] = d
        mo_ref[...] = mn
        vo_ref[...] = vn

    return pl.pallas_call(body, name="adam_w_ada", out_shape=[_sds((R, C), F32)] * 4,
                          compiler_params=_cp())(c_t, dm, w, m, v)


def kernel(x, c, norm_g, w_ada, b_ada, w_in, pool_w, pool_scale, w_attn_br, w_pool_br, w_out, rel_bias, final_g, loss_target, m_norm_g, m_w_ada, m_b_ada, m_w_in, m_pool_w, m_pool_scale, m_w_attn_br, m_w_pool_br, m_w_out, m_rel_bias, m_final_g, v_norm_g, v_w_ada, v_b_ada, v_w_in, v_pool_w, v_pool_scale, v_w_attn_br, v_w_pool_br, v_w_out, v_rel_bias, v_final_g):
    S = x.shape[1]
    px, py, pc = _my_place()
    me = 4 * px + 2 * py + pc
    cidx = jnp.reshape(pc, (1,)).astype(jnp.int32)

    layout = [(1, 1024), (1, 128), (1, 128), (0, 128)]
    final_g2 = final_g.reshape(1, D)
    me1 = jnp.reshape(me, (1,)).astype(jnp.int32)

    b_part = lax.dynamic_slice(b_ada, (0, me * 384), (1, 384))
    c_all, mod_all = _adaln_exchange(c, w_ada[0], b_part)
    c_all = c_all.reshape(8, D)
    mod = mod_all.reshape(1, 3 * D)
    shift, scale, gate = mod[:, :D], mod[:, D:2 * D], mod[:, 2 * D:]

    gplan = _gather_chips_plan([(None, 1)] * 4)
    lands = _place_slot([w_in[0], w_attn_br[0], w_pool_br[0], w_out[0]], me1, 8, "place_own_shards", CDT)
    g_sems, _, lands, tok = _split_start("gather_chips_start", [], lands, 8, gplan, dep=mod_all)
    st = _local_pre(x[0], shift, scale, norm_g, rel_bias, dep=tok)
    _, lands = _split_wait("gather_chips_wait", g_sems, [], lands, gplan,
                           [st["ht"]] + [b for pair in st["biases"] for b in pair])
    w_in_full, wab8, wpb8, wout8 = _pair_share(lands)
    wab, wpb = _cols_from_blocks(wab8, "w_attn_br_cols"), _cols_from_blocks(wpb8, "w_pool_br_cols")
    wout = wout8.reshape(D, D)

    st = _local_grads(st, loss_target[0], gate, w_in_full, pool_w[0], pool_scale, wab, wpb, wout, final_g2)

    early = [st["dpw"].reshape(AW, 128), st["dps"], st["gfg"], jnp.broadcast_to(st["loss"], (1, 128))]
    dws = [st["dw_in"], st["dw_ab"], st["dw_pb"], st["dw_out"]]
    sent = [st["dw_in_low"]] + dws[1:] + early
    pplan = _pair_plan(layout)
    pair_shapes = [(4, w, a.shape[1]) if ax == 0 else (4, a.shape[0], w) for a, (ax, w) in zip(dws, layout)]
    p_sems, sent, recv, tok = _split_start(
        "reduce_pair_start", sent,
        [lax.empty(s, a.dtype) for s, a in zip(pair_shapes + [a.shape for a in early], sent)], 20, pplan)
    d_rel_bias = _local_bias_grads(st, _BIAS_GRADS_EARLY, dep=tok)
    sent, recv = _split_wait("reduce_pair_wait", p_sems, sent, recv, pplan, d_rel_bias)
    parts = [_pair_add(dw, r, ax, w, cidx, f"pair_add{i}")
             for i, (dw, r, (ax, w)) in enumerate(zip([dws[0]] + sent[1:4], recv[:4], layout))]
    parts += _add_pairs(sent[4:], recv[4:], "pair_add_smalls")

    slot1 = jnp.reshape(2 * px + py, (1,)).astype(jnp.int32)
    cplan = _chips_plan(4)
    c_sems, parts, lands, tok = _split_start("reduce_chips_start", parts,
                                             _place_slot(parts, slot1, 4, "place_own_partials"), 24, cplan)
    d_rel_bias = d_rel_bias + _local_bias_grads(st, _BIAS_GRADS_LATE, dep=tok)
    grad_x, dmod, d_norm_g = _local_input_grads(st, w_in_full, norm_g, scale, dep=tok)

    dmod_all, ng_all, rb_all = _all_gather_small([dmod, d_norm_g, d_rel_bias], "gather_late")
    o_norm_g = _adam(ng_all, norm_g, m_norm_g, v_norm_g, "adam_norm_g", fresh=True)
    o_rel_bias = [o.T for o in _adam(rb_all, rel_bias.T, m_rel_bias.T, v_rel_bias.T, "adam_rel_bias", fresh=True)]
    o_b_ada = _adam(dmod_all, b_ada, m_b_ada, v_b_ada, "adam_b_ada", fresh=True)
    dm_mine = lax.dynamic_slice(dmod_all.reshape(8, 3 * D), (0, me * 384), (8, 384))
    o_w_ada = [o[None] for o in _adam_w_ada(c_all.T, dm_mine, w_ada[0], m_w_ada[0], v_w_ada[0])]

    _, (g_in, g_ab, g_pb, g_out, g_pw, g_ps, g_fg, g_loss) = _split_wait(
        "reduce_chips_wait", c_sems, parts, lands, cplan, [o_w_ada[0], o_b_ada[0], o_norm_g[0], o_rel_bias[0]])
    o_w_in = [o[None] for o in _adam(g_in, w_in[0], m_w_in[0], v_w_in[0], "adam_w_in")]
    o_w_ab = [o[None] for o in _adam(g_ab, w_attn_br[0], m_w_attn_br[0], v_w_attn_br[0], "adam_w_attn_br")]
    o_w_pb = [o[None] for o in _adam(g_pb, w_pool_br[0], m_w_pool_br[0], v_w_pool_br[0], "adam_w_pool_br")]
    o_w_out = [o[None] for o in _adam(g_out, w_out[0], m_w_out[0], v_w_out[0], "adam_w_out")]
    o_pool_w = [o.reshape(pool_w.shape) for o in _adam(g_pw, pool_w.reshape(AW, 128), m_pool_w.reshape(AW, 128),
                                                       v_pool_w.reshape(AW, 128), "adam_pool_w")]
    o_pool_scale = _adam(g_ps, pool_scale, m_pool_scale, v_pool_scale, "adam_pool_scale")
    o_final_g = [o.reshape(D) for o in _adam(g_fg, final_g2, m_final_g.reshape(1, D), v_final_g.reshape(1, D),
                                             "adam_final_g")]
    loss = _sum_parts(g_loss, "sum_loss")[0, 0]

    per_w = [o_norm_g, o_w_ada, o_b_ada, o_w_in, o_pool_w, o_pool_scale, o_w_ab, o_w_pb, o_w_out, o_rel_bias, o_final_g]
    outs = [loss, grad_x[None]]
    for k in range(4):
        outs += [o[k] for o in per_w]
    return tuple(outs)
```

```python
import functools
import math

import numpy as np
import jax
import jax.numpy as jnp
from jax import lax
from jax.experimental import pallas as pl
from jax.experimental.pallas import tpu as pltpu

F32 = jnp.float32
CDT = jnp.bfloat16
D = 1024
HD = 64
NH = 8
AW = 512
QKV_W = 4608
REST_W = 3584
R16 = 16
EPS = 1e-6
NEG = -1e30
N_BACK = 128
ADAM_LR, ADAM_B1, ADAM_B2, ADAM_EPS, ADAM_WD, ADAM_STEP = 0.001, 0.9, 0.999, 1e-08, 0.01, 10
MESH = pl.DeviceIdType.MESH
VMEM_MB = 1024 * 1024


def _cp(sem=None, vmem=48, **kw):
    if sem is not None:
        kw["dimension_semantics"] = sem
    return pltpu.CompilerParams(vmem_limit_bytes=vmem * VMEM_MB, **kw)


def _sds(shape, dtype):
    return jax.ShapeDtypeStruct(tuple(shape), dtype)


def _in_hbm(shape, dtype):
    return pltpu.HBM(tuple(shape), dtype)


def _pin(a):
    return pltpu.with_memory_space_constraint(a, pltpu.HBM)


def _is_big(a):
    return a.size * a.dtype.itemsize >= 4 * VMEM_MB


def _rest_blk(jj):
    return jnp.where(jj == 0, 9, jnp.where(jj == 6, 10, jj + 10))


def _sigmoid(z):
    return 0.5 * jnp.tanh(0.5 * z) + 0.5


def _dot(a, b, dims):
    return lax.dot_general(a, b, (dims, ((), ())), preferred_element_type=F32)


NN = ((1,), (0,))
NT = ((1,), (1,))
TN = ((0,), (0,))


def _ld(ref):
    if len(ref.shape) == 2:
        return ref[...]
    return jnp.concatenate([ref[a] for a in range(ref.shape[0])], axis=0)


def _st(ref, val, cols=None):
    val = val.astype(ref.dtype)
    if len(ref.shape) == 2:
        if cols is None:
            ref[...] = val
        else:
            ref[:, cols] = val
        return
    rows = ref.shape[1]
    for a in range(ref.shape[0]):
        if cols is None:
            ref[a] = val[a * rows:(a + 1) * rows]
        else:
            ref[a, :, cols] = val[a * rows:(a + 1) * rows]


REST_COLS = ((9, 10), (11, 16), (10, 11))
_HBM = pl.BlockSpec(memory_space=pltpu.HBM)


def _resident(shape):
    return pl.BlockSpec(shape, lambda *ids: (0,) * len(shape), pipeline_mode=pl.Buffered(1))


def _w512(w_ref, j):
    return w_ref[j // 2, :, 512 * (j % 2):512 * (j % 2 + 1)]


def _proj(h, w):
    S = h.shape[0]
    tm = 512

    def body(a_ref, w_ref, q_ref, r_ref):
        a = a_ref[...]
        for j in range(9):
            q_ref[:, 512 * j:512 * (j + 1)] = _dot(a, _w512(w_ref, j), NN).astype(q_ref.dtype)
        off = 0
        for lo, hi in REST_COLS:
            for j in range(lo, hi):
                r_ref[:, off:off + 512] = _dot(a, _w512(w_ref, j), NN)
                off += 512

    return pl.pallas_call(
        body, name="proj", grid=(S // tm,),
        in_specs=[pl.BlockSpec((tm, D), lambda i: (i, 0)), _resident((8, D, 1024))],
        out_specs=[pl.BlockSpec((tm, QKV_W), lambda i: (i, 0)), pl.BlockSpec((tm, REST_W), lambda i: (i, 0))],
        out_shape=[_sds((S, QKV_W), CDT), _sds((S, REST_W), F32)],
        compiler_params=_cp(("parallel",), vmem=56),
    )(h, w)


def _after(dep):
    return ([], []) if dep is None else ([pl.BlockSpec(memory_space=pl.ANY)], [dep])


def _dh(dqkv, drest, w, dep=None):
    S = dqkv.shape[0]
    tm = 512
    dep_specs, dep_args = _after(dep)

    def body(a_ref, b_ref, w_ref, *rest):
        o_ref = rest[-1]
        acc = _dot(a_ref[:, 0:512], _w512(w_ref, 0), NT)
        for j in range(1, 9):
            acc = acc + _dot(a_ref[:, 512 * j:512 * (j + 1)], _w512(w_ref, j), NT)
        off = 0
        for lo, hi in REST_COLS:
            for j in range(lo, hi):
                acc = acc + _dot(b_ref[:, off:off + 512], _w512(w_ref, j), NT)
                off += 512
        o_ref[...] = acc

    return pl.pallas_call(
        body, name="dh", grid=(S // tm,),
        in_specs=[pl.BlockSpec((tm, QKV_W), lambda i: (i, 0)), pl.BlockSpec((tm, REST_W), lambda i: (i, 0)),
                  _resident((8, D, 1024))] + dep_specs,
        out_specs=pl.BlockSpec((tm, D), lambda i: (i, 0)),
        out_shape=_sds((S, D), F32),
        compiler_params=_cp(("parallel",), vmem=56),
    )(dqkv, drest, w, *dep_args)


def _mm_tn(at, b, *, n_total, out_blk, name, prev=None, tn=512, with_low=False):
    R, S = at.shape
    nb = b.shape[1] // tn
    n_out = 2 if with_low else 1

    def body(*refs):
        res = _dot(refs[0][...], refs[1][...], NN)
        refs[-n_out][...] = res
        if with_low:
            refs[-1][...] = res.astype(CDT)

    in_specs = [_resident((R, S)), pl.BlockSpec((S, tn), lambda j: (0, j))]
    args = [at, b]
    aliases = {}
    if prev is not None:
        prev = list(prev) if with_low else [prev]
        in_specs += [pl.BlockSpec(memory_space=pl.ANY)] * n_out
        args += prev
        aliases = {2 + k: k for k in range(n_out)}
    out_spec = pl.BlockSpec((R, tn), lambda j: (0, out_blk(j)))
    outs = pl.pallas_call(
        body, name=name, grid=(nb,), in_specs=in_specs,
        out_specs=[out_spec] * n_out,
        out_shape=[_sds((R, n_total), F32), _sds((R, n_total), CDT)][:n_out],
        input_output_aliases=aliases,
        compiler_params=_cp(("parallel",)),
    )(*args)
    return tuple(outs) if with_low else outs[0]


def _tcast(a, name):
    M, C = a.shape
    tm = 512

    def body(a_ref, o_ref):
        o_ref[...] = a_ref[...].astype(F32).T.astype(o_ref.dtype)

    return pl.pallas_call(
        body, name=name, grid=(M // tm,),
        in_specs=[pl.BlockSpec((tm, C), lambda i: (i, 0))],
        out_specs=pl.BlockSpec((C, tm), lambda i: (0, i)),
        out_shape=_sds((C, M), CDT), compiler_params=_cp(("parallel",)),
    )(a)


TOK = R16 * R16


def _swap16():
    i = lax.broadcasted_iota(jnp.int32, (TOK, TOK), 0)
    j = lax.broadcasted_iota(jnp.int32, (TOK, TOK), 1)
    return jnp.where(j == jnp.bitwise_and(i, 15) * 16 + jnp.right_shift(i, 4), 1.0, 0.0).astype(CDT)


def _norm_mod(x, norm_g, shift, scale, dep=None):
    S = x.shape[0]
    dep_specs, dep_args = _after(dep)

    def body(x_ref, g_ref, sh_ref, sc_ref, *rest):
        xv = x_ref[...]
        xn = xv * lax.rsqrt(jnp.mean(xv * xv, axis=-1, keepdims=True) + EPS)
        h = (xn * (g_ref[...] * (1.0 + sc_ref[...])) + sh_ref[...]).astype(CDT)
        _st(rest[-1], _dot(_swap16(), h, NN))

    vec = pl.BlockSpec((1, D), lambda i: (0, 0))
    return pl.pallas_call(
        body, name="norm_mod", grid=(S // TOK,),
        in_specs=[pl.BlockSpec((TOK, D), lambda i: (i, 0)), vec, vec, vec] + dep_specs,
        out_specs=pl.BlockSpec((R16, R16, D), lambda i: (0, i, 0)),
        out_shape=_sds((R16, S // R16, D), CDT), compiler_params=_cp(("parallel",)),
    )(x, norm_g, shift, scale, *dep_args)


def _t5_bucket_np(n):
    nf = np.maximum(n, 1).astype(np.float32)
    large = 16 + (np.log(nf / np.float32(16)) / np.float32(math.log(2048 / 16)) * np.float32(16)).astype(np.int32)
    large = np.minimum(large, 31)
    return np.where(n < 16, n, large).astype(np.int32)


class _Geom:
    def __init__(self, gi, LS):
        self.gi, self.LS = gi, LS
        self.dil = (1, 4, 16)[gi]
        if gi == 0:
            self.lead, self.rows, self.n_tiles, self.has_prev = (R16,), 16, LS // 16, True
            a = np.arange(R16)[:, None]
            ll = np.arange(16)[None, :]
            self.pos = (16 * ll + a).reshape(-1)
        elif gi == 1:
            self.lead, self.rows, self.n_tiles, self.has_prev = (4,), 64, LS // 64, True
            a = np.arange(4)[:, None]
            ll = np.arange(64)[None, :]
            self.pos = (4 * ll + a).reshape(-1)
        else:
            self.lead, self.rows, self.n_tiles, self.has_prev = (), LS, 1, False
            self.pos = np.arange(LS)
        self.T = self.pos.shape[0]
        self.Tk = 2 * self.T if self.has_prev else self.T
        self.outer = {0: (), 1: (4,), 2: (R16,)}[gi]

    def view(self, arr):
        S, C = arr.shape
        if self.gi == 1:
            return arr.reshape(4, 4, self.LS, C)
        return arr.reshape(R16, self.LS, C)

    def view_shape(self, C):
        return (4, 4, self.LS, C) if self.gi == 1 else (R16, self.LS, C)

    def unview(self, arr):
        return arr.reshape(R16 * self.LS, arr.shape[-1])

    def spec(self, W, cblk, tile):
        if self.gi == 0:
            return pl.BlockSpec((R16, 16, W), lambda n: (0, tile(n), cblk))
        if self.gi == 1:
            return pl.BlockSpec((4, None, 64, W), lambda r, n: (0, r, tile(n), cblk))
        return pl.BlockSpec((None, self.LS, W), lambda r: (r, 0, cblk))

    def halves(self):
        R = self.rows
        idx = np.arange(self.T).reshape(-1, R)
        return idx[:, :R // 2].reshape(-1), idx[:, R // 2:].reshape(-1)

    def bucket_index(self, sub):
        lo, hi = self.halves()
        pos = self.pos
        if sub == 0:
            pq = pos[lo]
            pk = np.concatenate([pos[hi] - self.T, pos[lo]]) if self.has_prev else np.concatenate([pos[lo], pos[hi]])
        else:
            pq = pos[hi]
            pk = np.concatenate([pos[lo], pos[hi]])
        dist = pq[:, None] - pk[None, :]
        ok = (dist >= 0) & (dist <= N_BACK)
        bucket = _t5_bucket_np(np.clip(dist, 0, N_BACK) * self.dil)
        return np.where(ok, bucket, -1).astype(np.int32)


def _bias_table(idx, buckets, rel_bias, gi, sub, dep=None):
    T, Tk = idx.shape
    tr = 32
    dep_specs, dep_args = _after(dep)

    def body(idx_ref, rb_ref, *rest):
        h = pl.program_id(0)
        ix = idx_ref[...]
        acc = jnp.full(ix.shape, NEG, F32)
        for b in buckets:
            acc = jnp.where(ix == b, rb_ref[b, gi * NH + h], acc)
        rest[-1][...] = acc

    return pl.pallas_call(
        body, name=f"bias_table{gi}{'ab'[sub]}", grid=(NH, T // tr),
        in_specs=[pl.BlockSpec((tr, Tk), lambda h, i: (i, 0)), pl.BlockSpec(memory_space=pltpu.SMEM)] + dep_specs,
        out_specs=pl.BlockSpec((None, tr, Tk), lambda h, i: (h, i, 0)),
        out_shape=_sds((NH, T, Tk), F32), compiler_params=_cp(("parallel", "parallel")),
    )(idx, rel_bias, *dep_args)


def _bias_grad(dsum, idx, buckets, gi, sub, dep=None):
    _, T, Tk = dsum.shape
    dep_specs, dep_args = _after(dep)

    def body(ds_ref, idx_ref, *rest):
        o_ref, r_ref = rest[-2], rest[-1]
        r_ref[...] = jnp.zeros_like(r_ref)
        ix, ds = idx_ref[...], ds_ref[...]
        for b in buckets:
            r_ref[b:b + 1, :] = jnp.sum(jnp.where(ix == b, ds, 0.0), axis=0, keepdims=True)
        o_ref[...] = jnp.broadcast_to(jnp.sum(r_ref[...], axis=1, keepdims=True), (32, 128))

    out = pl.pallas_call(
        body, name=f"bias_grad{gi}{'ab'[sub]}", grid=(NH,),
        in_specs=[pl.BlockSpec((None, T, Tk), lambda h: (h, 0, 0)), pl.BlockSpec((T, Tk), lambda h: (0, 0))] + dep_specs,
        out_specs=pl.BlockSpec((None, 32, 128), lambda h: (h, 0, 0)),
        out_shape=_sds((NH, 32, 128), F32), scratch_shapes=[pltpu.VMEM((32, Tk), F32)],
        compiler_params=_cp(("parallel",)),
    )(dsum, idx, *dep_args)
    return out[:, :, 0]


def _prev_mask(n, T, Tk):
    col = lax.broadcasted_iota(jnp.int32, (1, Tk), 1)
    return jnp.where((col < T) & (n == 0), NEG, 0.0).astype(F32)


def _split_rows(v, R):
    A, half = v.shape[0] // R, R // 2
    via_f32 = v.dtype != F32 and half % 16 != 0
    w = v.astype(F32) if via_f32 else v
    lo = jnp.concatenate([w[a * R:a * R + half] for a in range(A)], axis=0)
    hi = jnp.concatenate([w[a * R + half:(a + 1) * R] for a in range(A)], axis=0)
    return (lo.astype(v.dtype), hi.astype(v.dtype)) if via_f32 else (lo, hi)


def _merge_rows(lo, hi, R):
    half = R // 2
    A = lo.shape[0] // half
    return jnp.concatenate([x[a * half:(a + 1) * half] for a in range(A) for x in (lo, hi)], axis=0)


def _attn_fwd(qkv, biases, geom, others=None):
    S = qkv.shape[0]
    gi, T, R, has_prev = geom.gi, geom.T, geom.rows, geom.has_prev
    H2 = T // 2
    qv = geom.view(qkv)
    cq, ck, cv = 3 * gi, 3 * gi + 1, 3 * gi + 2
    assert others is None or not has_prev

    def body(*refs):
        if has_prev:
            q_ref, kp_ref, kc_ref, vp_ref, vc_ref, ba_ref, bb_ref, o_ref, l_ref = refs
        elif others is not None:
            (q_ref, kc_ref, vc_ref, ba_ref, bb_ref, o0_ref, o1_ref, l0_ref, l1_ref, z_ref,
             attn_ref, L_ref, ag_ref) = refs
        else:
            q_ref, kc_ref, vc_ref, ba_ref, bb_ref, o_ref, l_ref = refs
        q_lo, q_hi = _split_rows(_ld(q_ref), R)
        kc_lo, kc_hi = _split_rows(_ld(kc_ref), R)
        vc_lo, vc_hi = _split_rows(_ld(vc_ref), R)
        k_b = jnp.concatenate([kc_lo, kc_hi], axis=0)
        v_b = jnp.concatenate([vc_lo, vc_hi], axis=0)
        if has_prev:
            n = pl.program_id(len(geom.outer))
            k_a = jnp.concatenate([_split_rows(_ld(kp_ref), R)[1], kc_lo], axis=0)
            v_a = jnp.concatenate([_split_rows(_ld(vp_ref), R)[1], vc_lo], axis=0)
            pm = _prev_mask(n, H2, T)
        else:
            k_a, v_a, pm = k_b, v_b, None
        low = lax.broadcasted_iota(jnp.int32, (1, 2 * HD), 1) < HD
        for j in range(NH // 2):
            cols = slice(2 * HD * j, 2 * HD * (j + 1))
            o_half, l_half = [], []
            for q, k, v, b_ref, mask in ((q_lo, k_a, v_a, ba_ref, pm), (q_hi, k_b, v_b, bb_ref, None)):
                q2 = q[:, cols].astype(F32) * 0.125
                k2, v2 = k[:, cols], v[:, cols]
                qs = jnp.concatenate([jnp.where(low, q2, 0.0), jnp.where(low, 0.0, q2)], axis=0).astype(q.dtype)
                s = _dot(qs, k2, NT) + b_ref[j]
                if mask is not None:
                    s = s + mask
                m = jnp.max(s, axis=1, keepdims=True)
                p = jnp.exp(s - m)
                l = jnp.sum(p, axis=1, keepdims=True)
                pv = _dot(p.astype(v.dtype), v2, NN) / l
                lse = m + jnp.log(l)
                o_half.append(jnp.where(low, pv[:H2], pv[H2:]))
                l_half.append(jnp.where(low, lse[:H2], lse[H2:]))
            o2, lc = _merge_rows(o_half[0], o_half[1], R), _merge_rows(l_half[0], l_half[1], R)
            if others is None:
                _st(o_ref, o2, cols)
                _st(l_ref, lc, cols)
                continue
            la, lb = l0_ref[:, cols], l1_ref[:, cols]
            m = jnp.maximum(jnp.maximum(la, lb), lc)
            ea, eb, ec = jnp.exp(la - m), jnp.exp(lb - m), jnp.exp(lc - m)
            den = ea + eb + ec
            attn = (ea * o0_ref[:, cols] + eb * o1_ref[:, cols] + ec * o2) / den
            attn_ref[:, cols] = attn
            L_ref[:, cols] = m + jnp.log(den)
            z = z_ref[:, cols]
            ag = attn * (z * _sigmoid(z))
            ag_ref[:, cols] = ag.astype(ag_ref.dtype)

    cur = lambda n: n
    prev = lambda n: jnp.maximum(n - 1, 0)
    if has_prev:
        in_specs = [geom.spec(AW, cq, cur), geom.spec(AW, ck, prev), geom.spec(AW, ck, cur),
                    geom.spec(AW, cv, prev), geom.spec(AW, cv, cur)]
        args = [qv] * 5
    else:
        in_specs = [geom.spec(AW, cq, cur), geom.spec(AW, ck, cur), geom.spec(AW, cv, cur)]
        args = [qv] * 3
    nd = len(geom.outer) + (1 if has_prev else 0)
    biases = [b.reshape(NH // 2, 2 * H2, b.shape[2]) for b in biases]
    in_specs += [pl.BlockSpec(b.shape, lambda *ids: (0, 0, 0)) for b in biases]
    grid = geom.outer + ((geom.n_tiles,) if has_prev else ())
    oshape = geom.view_shape(AW)
    tile = geom.spec(AW, 0, cur)
    if others is None:
        o, l = pl.pallas_call(
            body, name=f"attn_fwd{gi}", grid=grid, in_specs=in_specs, out_specs=[tile, tile],
            out_shape=[_sds(oshape, F32), _sds(oshape, F32)],
            compiler_params=_cp(("arbitrary",) * nd),
        )(*args, *biases)
        return geom.unview(o), geom.unview(l)
    os_, ls_, rest = others
    attn, Lb, ag = pl.pallas_call(
        body, name=f"attn_fwd{gi}_merge", grid=grid, in_specs=in_specs + [tile] * 5,
        out_specs=[tile, tile, tile],
        out_shape=[_sds(oshape, F32), _sds(oshape, F32), _sds(oshape, CDT)],
        compiler_params=_cp(("arbitrary",) * nd),
    )(*args, *biases, *[geom.view(a) for a in (*os_, *ls_, rest)])
    return geom.unview(attn), geom.unview(Lb), geom.unview(ag)


def _pool_counts(r, LS, win):
    l = lax.broadcasted_iota(jnp.int32, (LS, 1), 0)
    return jnp.minimum(16 * l + (r + 1), win).astype(F32)


def _pool_fwd(rest, pool_w, pool_scale):
    S = rest.shape[0]
    LS = S // R16
    r3 = rest.reshape(R16, LS, REST_W)
    PG = 128

    def body(u_ref, z_ref, pw_ref, ps_ref, pt_ref, mp_ref, pg_ref, e_ref):
        g = pl.program_id(0)
        win = jnp.left_shift(2, g)
        row = lax.broadcasted_iota(jnp.int32, (LS, PG), 0)
        for r in range(R16):
            e_ref[R16 + r] = u_ref[r]
        for r in range(1, R16):
            e_ref[r] = jnp.where(row == 0, 0.0, pltpu.roll(u_ref[r], 1, 0))

        def double(s):
            for i in range(2 * R16 - 1, 2 * s - 1, -1):
                e_ref[i] = e_ref[i] + e_ref[i - s]

        double(1)
        for k in (1, 2, 3):
            pl.when(g >= k)(functools.partial(double, 2 ** k))
        pw = pw_ref[...].astype(CDT)
        for r in range(R16):
            u = u_ref[r]
            pooled = e_ref[R16 + r] / _pool_counts(r, LS, win) - u
            pc = pooled.astype(CDT)
            mp = _dot(pc, pw, NN)
            z = z_ref[r]
            pg = (mp * ps_ref[...]) * (z * _sigmoid(z))
            mp_ref[r] = mp
            pg_ref[r] = pg.astype(pg_ref.dtype)
            pt_ref[:, r * LS:(r + 1) * LS] = pooled.T.astype(pt_ref.dtype)

    col = lambda off: pl.BlockSpec((R16, LS, PG), lambda g: (0, 0, off + g))
    tsp = pl.BlockSpec((PG, S), lambda g: (g, 0))
    pt, mp, pg = pl.pallas_call(
        body, name="pool_fwd", grid=(4,),
        in_specs=[col(24), col(4), pl.BlockSpec((None, PG, PG), lambda g: (g, 0, 0)),
                  pl.BlockSpec((1, PG), lambda g: (0, g))],
        out_specs=[tsp, col(0), col(0)],
        out_shape=[_sds((AW, S), CDT), _sds((R16, LS, AW), F32), _sds((R16, LS, AW), CDT)],
        scratch_shapes=[pltpu.VMEM((2 * R16, LS, PG), F32)],
        compiler_params=_cp(("parallel",)),
    )(r3, r3, pool_w, pool_scale)
    return pt, mp.reshape(S, AW), pg.reshape(S, AW)


def _branch_merge(ag, pg, wab, wpb, rest):
    S = ag.shape[0]
    tm, tn = 512, 512

    def body(ag_ref, pg_ref, wa_ref, wp_ref, ga_ref, gp_ref, m_ref, mt_ref):
        ya = _dot(ag_ref[...], wa_ref[...], NN)
        yp = _dot(pg_ref[...], wp_ref[...], NN)
        mg = _sigmoid(ga_ref[...]) * ya + _sigmoid(gp_ref[...]) * yp
        m_ref[...] = mg.astype(m_ref.dtype)
        mt_ref[...] = mg.T.astype(mt_ref.dtype)

    a = pl.BlockSpec((tm, AW), lambda i, j: (i, 0))
    w = pl.BlockSpec((AW, tn), lambda i, j: (0, j))
    o = pl.BlockSpec((tm, tn), lambda i, j: (i, j))
    return pl.pallas_call(
        body, name="branch_merge", grid=(S // tm, D // tn),
        in_specs=[a, a, w, w, pl.BlockSpec((tm, tn), lambda i, j: (i, 2 + j)),
                  pl.BlockSpec((tm, tn), lambda i, j: (i, 4 + j))],
        out_specs=[o, pl.BlockSpec((tn, tm), lambda i, j: (j, i))],
        out_shape=[_sds((S, D), CDT), _sds((D, S), CDT)],
        compiler_params=_cp(("parallel", "parallel")),
    )(ag, pg, wab, wpb, rest, rest)


def _merge_out_loss(ag, pg, rest, wab, wpb, wout, x, target, gate, final_g):
    S = x.shape[0]
    LS = S // R16

    def body(ag_ref, pg_ref, ga_ref, gp_ref, wa_ref, wp_ref, w_ref, x_ref, t_ref, gate_ref, fg_ref,
             dx2_ref, dmo_ref, loss_ref, gfg_ref, dgate_ref, dw_ref):
        i = pl.program_id(0)

        @pl.when(i == 0)
        def _():
            loss_ref[...] = jnp.zeros_like(loss_ref)
            gfg_ref[...] = jnp.zeros_like(gfg_ref)
            dgate_ref[...] = jnp.zeros_like(dgate_ref)
            dw_ref[...] = jnp.zeros_like(dw_ref)

        swap = _swap16()
        ya = _dot(_ld(ag_ref), wa_ref[...], NN)
        yp = _dot(_ld(pg_ref), wp_ref[...], NN)
        mg = (_sigmoid(_ld(ga_ref)) * ya + _sigmoid(_ld(gp_ref)) * yp).astype(CDT)
        m_nat = _dot(swap, mg, NN)
        mo = _dot(m_nat.astype(CDT), w_ref[...], NN)
        gate, fg = gate_ref[...], fg_ref[...]
        x2 = x_ref[...] + gate * mo
        r2 = lax.rsqrt(jnp.mean(x2 * x2, axis=-1, keepdims=True) + EPS)
        xn2 = x2 * r2
        diff = xn2 * fg - t_ref[...]
        loss = jnp.sum(jnp.sum(diff * diff, axis=-1, keepdims=True), axis=0, keepdims=True)
        dy = diff * (1.0 / D)
        dxn2 = dy * fg
        dx2 = r2 * (dxn2 - xn2 * jnp.mean(dxn2 * xn2, axis=-1, keepdims=True))
        dx2_ref[...] = dx2
        dmo = (dx2 * gate).astype(CDT)
        _st(dmo_ref, _dot(swap, dmo, NN))
        loss_ref[...] += jnp.broadcast_to(loss * (0.5 / D), loss_ref.shape)
        gfg_ref[...] += jnp.sum(dy * xn2, axis=0, keepdims=True)
        dgate_ref[...] += jnp.sum(dx2 * mo, axis=0, keepdims=True)
        dw_ref[...] += _dot(m_nat.T.astype(CDT), dmo, NN)

    def slab(w, cblk):
        return pl.BlockSpec((R16, R16, w), lambda i: (0, i, cblk))

    nat = pl.BlockSpec((TOK, D), lambda i: (i, 0))
    vec = pl.BlockSpec((1, D), lambda i: (0, 0))
    r3 = rest.reshape(R16, LS, REST_W)
    return pl.pallas_call(
        body, name="merge_out_loss", grid=(S // TOK,),
        in_specs=[slab(AW, 0), slab(AW, 0), slab(D, 1), slab(D, 2), _resident((AW, D)), _resident((AW, D)),
                  _resident((D, D)), nat, nat, vec, vec],
        out_specs=[nat, slab(D, 0), pl.BlockSpec((1, 128), lambda i: (0, 0)), vec, vec,
                   pl.BlockSpec((D, D), lambda i: (0, 0))],
        out_shape=[_sds((S, D), F32), _sds((R16, LS, D), CDT), _sds((1, 128), F32), _sds((1, D), F32),
                   _sds((1, D), F32), _sds((D, D), F32)],
        compiler_params=_cp(("arbitrary",)),
    )(ag.reshape(R16, LS, AW), pg.reshape(R16, LS, AW), r3, r3, wab, wpb, wout, x, target, gate, final_g)


def _out_loss(merged, wout, x, target, gate, final_g):
    S = x.shape[0]

    def body(m_ref, w_ref, x_ref, t_ref, gate_ref, fg_ref, dx2_ref, dmo_ref, loss_ref, gfg_ref, dgate_ref):
        i = pl.program_id(0)

        @pl.when(i == 0)
        def _():
            loss_ref[...] = jnp.zeros_like(loss_ref)
            gfg_ref[...] = jnp.zeros_like(gfg_ref)
            dgate_ref[...] = jnp.zeros_like(dgate_ref)

        swap = _swap16()
        mo = _dot(_dot(swap, _ld(m_ref), NN).astype(CDT), w_ref[...], NN)
        gate, fg = gate_ref[...], fg_ref[...]
        x2 = x_ref[...] + gate * mo
        r2 = lax.rsqrt(jnp.mean(x2 * x2, axis=-1, keepdims=True) + EPS)
        xn2 = x2 * r2
        diff = xn2 * fg - t_ref[...]
        loss = jnp.sum(jnp.sum(diff * diff, axis=-1, keepdims=True), axis=0, keepdims=True)
        dy = diff * (1.0 / D)
        dxn2 = dy * fg
        dx2 = r2 * (dxn2 - xn2 * jnp.mean(dxn2 * xn2, axis=-1, keepdims=True))
        dx2_ref[...] = dx2
        _st(dmo_ref, _dot(swap, (dx2 * gate).astype(CDT), NN))
        loss_ref[...] += jnp.broadcast_to(loss * (0.5 / D), loss_ref.shape)
        gfg_ref[...] += jnp.sum(dy * xn2, axis=0, keepdims=True)
        dgate_ref[...] += jnp.sum(dx2 * mo, axis=0, keepdims=True)

    slab = pl.BlockSpec((R16, R16, D), lambda i: (0, i, 0))
    nat = pl.BlockSpec((TOK, D), lambda i: (i, 0))
    vec = pl.BlockSpec((1, D), lambda i: (0, 0))
    return pl.pallas_call(
        body, name="out_loss", grid=(S // TOK,),
        in_specs=[slab, _resident((D, D)), nat, nat, vec, vec],
        out_specs=[nat, slab, pl.BlockSpec((1, 128), lambda i: (0, 0)), vec, vec],
        out_shape=[_sds((S, D), F32), _sds((R16, S // R16, D), CDT), _sds((1, 128), F32), _sds((1, D), F32),
                   _sds((1, D), F32)],
        compiler_params=_cp(("arbitrary",)),
    )(merged, wout, x, target, gate, final_g)


def _bwd_gates(dmo, wout, wab, wpb, rest, ag, pg, attn, mp, pool_scale):
    S = dmo.shape[0]
    tm = 256

    def body(dmo_ref, wo_ref, wa_ref, wp_ref, za_ref, zp_ref, ga0, ga1, gp0, gp1, ag_ref, pg_ref, at_ref, mp_ref,
             ps_ref, dwa_ref, dwp_ref, dr_ref, dat_ref, dl_ref, dmp_ref, dps_ref):
        i = pl.program_id(0)

        @pl.when(i == 0)
        def _():
            dps_ref[...] = jnp.zeros_like(dps_ref)
            dwa_ref[...] = jnp.zeros_like(dwa_ref)
            dwp_ref[...] = jnp.zeros_like(dwp_ref)

        dm = _dot(dmo_ref[...], wo_ref[...], NT)
        sa = _sigmoid(jnp.concatenate([ga0[...], ga1[...]], axis=1))
        sp = _sigmoid(jnp.concatenate([gp0[...], gp1[...]], axis=1))
        dya = (dm * sa).astype(CDT)
        dyp = (dm * sp).astype(CDT)
        ag, pg = ag_ref[...], pg_ref[...]
        dwa_ref[...] += _dot(ag.astype(F32).T.astype(CDT), dya, NN)
        dwp_ref[...] += _dot(pg.astype(F32).T.astype(CDT), dyp, NN)
        ya = _dot(ag, wa_ref[...], NN)
        yp = _dot(pg, wp_ref[...], NN)
        dr_ref[:, 1024:2048] = (dm * ya * sa * (1.0 - sa)).astype(dr_ref.dtype)
        dr_ref[:, 2048:3072] = (dm * yp * sp * (1.0 - sp)).astype(dr_ref.dtype)
        da = _dot(dya, wa_ref[...], NT)
        dp = _dot(dyp, wp_ref[...], NT)
        za = za_ref[...]
        sga = _sigmoid(za)
        attn = at_ref[...]
        dattn = da * (za * sga)
        dr_ref[:, 0:512] = (da * attn * (sga * (1.0 + za * (1.0 - sga)))).astype(dr_ref.dtype)
        dat_ref[...] = dattn.astype(dat_ref.dtype)
        prod = dattn * attn
        low = lax.broadcasted_iota(jnp.int32, (1, 2 * HD), 1) < HD
        for j in range(NH // 2):
            x2 = prod[:, 2 * HD * j:2 * HD * (j + 1)]
            s0 = jnp.sum(jnp.where(low, x2, 0.0), axis=1, keepdims=True)
            s1 = jnp.sum(jnp.where(low, 0.0, x2), axis=1, keepdims=True)
            dl_ref[:, 2 * HD * j:2 * HD * (j + 1)] = jnp.where(low, s0, s1)
        zp = zp_ref[...]
        sgp = _sigmoid(zp)
        mpre = mp_ref[...]
        ps = ps_ref[...]
        dmixed = dp * (zp * sgp)
        dr_ref[:, 512:1024] = (dp * (mpre * ps) * (sgp * (1.0 + zp * (1.0 - sgp)))).astype(dr_ref.dtype)
        dps_ref[...] += jnp.sum(dmixed * mpre, axis=0, keepdims=True)
        dmp_ref[...] = (dmixed * ps).astype(dmp_ref.dtype)

    full = lambda shape: pl.BlockSpec(shape, lambda i: (0, 0))
    tD = pl.BlockSpec((tm, D), lambda i: (i, 0))
    tA = pl.BlockSpec((tm, AW), lambda i: (i, 0))
    rb = lambda c: pl.BlockSpec((tm, AW), lambda i: (i, c))
    return pl.pallas_call(
        body, name="bwd_gates", grid=(S // tm,),
        in_specs=[tD, full((D, D)), full((AW, D)), full((AW, D)), rb(0), rb(1), rb(2), rb(3), rb(4), rb(5),
                  tA, tA, tA, tA, full((1, AW))],
        out_specs=[full((AW, D)), full((AW, D)), pl.BlockSpec((tm, 3072), lambda i: (i, 0)), tA, tA, tA, full((1, AW))],
        out_shape=[_sds((AW, D), F32), _sds((AW, D), F32), _sds((S, REST_W), CDT), _sds((S, AW), CDT),
                   _sds((S, AW), F32), _sds((S, AW), CDT), _sds((1, AW), F32)],
        compiler_params=_cp(("arbitrary",), vmem=56),
    )(dmo, wout, wab, wpb, rest, rest, rest, rest, rest, rest, ag, pg, attn, mp, pool_scale)


def _pool_bwd(dmp, pt, pool_w, drest):
    S = dmp.shape[0]
    LS = S // R16
    PG = 128
    d3 = dmp.reshape(R16, LS, AW)
    dr3 = drest.reshape(R16, LS, REST_W)

    def body(d_ref, pt_ref, pw_ref, dr_in, du_ref, dpw_ref, g_ref, s_ref):
        del dr_in
        g = pl.program_id(0)
        win = jnp.left_shift(2, g)
        row = lax.broadcasted_iota(jnp.int32, (LS, PG), 0)
        pw = pw_ref[...].astype(CDT)
        dpw = jnp.zeros((PG, PG), F32)
        for r in range(R16):
            dm = d_ref[r]
            dpw = dpw + _dot(pt_ref[:, r * LS:(r + 1) * LS], dm, NN)
            dpooled = _dot(dm, pw, NT)
            g_ref[r] = dpooled
            sc = dpooled / _pool_counts(r, LS, win)
            s_ref[r] = sc
            if r < R16 - 1:
                s_ref[R16 + r] = jnp.where(row == LS - 1, 0.0, pltpu.roll(sc, LS - 1, 0))
        dpw_ref[...] = dpw

        def double(s):
            for i in range(0, 2 * R16 - 2 * s):
                s_ref[i] = s_ref[i] + s_ref[i + s]

        double(1)
        for k in (1, 2, 3):
            pl.when(g >= k)(functools.partial(double, 2 ** k))
        for r in range(R16):
            du_ref[r] = (s_ref[r] - g_ref[r]).astype(du_ref.dtype)

    du, dpw = pl.pallas_call(
        body, name="pool_bwd", grid=(4,),
        in_specs=[pl.BlockSpec((R16, LS, PG), lambda g: (0, 0, g)), pl.BlockSpec((PG, S), lambda g: (g, 0)),
                  pl.BlockSpec((None, PG, PG), lambda g: (g, 0, 0)), pl.BlockSpec(memory_space=pl.ANY)],
        out_specs=[pl.BlockSpec((R16, LS, PG), lambda g: (0, 0, 24 + g)),
                   pl.BlockSpec((None, PG, PG), lambda g: (g, 0, 0))],
        out_shape=[_sds((R16, LS, REST_W), CDT), _sds((4, PG, PG), F32)],
        scratch_shapes=[pltpu.VMEM((R16, LS, PG), F32), pltpu.VMEM((2 * R16, LS, PG), F32)],
        input_output_aliases={3: 0},
        compiler_params=_cp(("parallel",)),
    )(d3, pt, pool_w, dr3)
    return du.reshape(S, REST_W), dpw


def _attn_bwd(qkv, dattn, Lb, dlb, biases, geom, dqkv_prev):
    S = qkv.shape[0]
    gi, T, R, has_prev, N = geom.gi, geom.T, geom.rows, geom.has_prev, geom.n_tiles
    H2 = T // 2
    nd_outer = len(geom.outer)
    qv = geom.view(qkv)
    cq, ck, cv = 3 * gi, 3 * gi + 1, 3 * gi + 2
    W3 = 3 * AW

    def heads(q_ref, kp_ref, kc_ref, vp_ref, vc_ref, do_ref, L_ref, dl_ref, ba_ref, bb_ref, pm, dsa_ref, dsb_ref, emit):
        q_lo, q_hi = _split_rows(_ld(q_ref), R)
        do_lo, do_hi = _split_rows(_ld(do_ref), R)
        L_lo, L_hi = _split_rows(_ld(L_ref), R)
        dl_lo, dl_hi = _split_rows(_ld(dl_ref), R)
        kc_lo, kc_hi = _split_rows(_ld(kc_ref), R)
        vc_lo, vc_hi = _split_rows(_ld(vc_ref), R)
        k_b = jnp.concatenate([kc_lo, kc_hi], axis=0)
        v_b = jnp.concatenate([vc_lo, vc_hi], axis=0)
        if has_prev:
            k_a = jnp.concatenate([_split_rows(_ld(kp_ref), R)[1], kc_lo], axis=0)
            v_a = jnp.concatenate([_split_rows(_ld(vp_ref), R)[1], vc_lo], axis=0)
        else:
            k_a, v_a = k_b, v_b
        low = lax.broadcasted_iota(jnp.int32, (1, 2 * HD), 1) < HD
        for j in range(NH // 2):
            sl = slice(2 * HD * j, 2 * HD * (j + 1))
            res = []
            for q, k, v, do, L, dl, b_ref, mask, ds_ref in (
                    (q_lo, k_a, v_a, do_lo, L_lo, dl_lo, ba_ref, pm, dsa_ref),
                    (q_hi, k_b, v_b, do_hi, L_hi, dl_hi, bb_ref, None, dsb_ref)):
                q2 = q[:, sl].astype(F32) * 0.125
                do2 = do[:, sl].astype(F32)
                k2, v2 = k[:, sl], v[:, sl]
                L2, dl2 = L[:, sl], dl[:, sl]
                qs = jnp.concatenate([jnp.where(low, q2, 0.0), jnp.where(low, 0.0, q2)], axis=0).astype(q.dtype)
                dom = jnp.concatenate([jnp.where(low, do2, 0.0), jnp.where(low, 0.0, do2)], axis=0).astype(q.dtype)
                Lc = jnp.concatenate([L2[:, 0:1], L2[:, HD:HD + 1]], axis=0)
                dlc = jnp.concatenate([dl2[:, 0:1], dl2[:, HD:HD + 1]], axis=0)
                s = _dot(qs, k2, NT) + b_ref[j]
                if mask is not None:
                    s = s + mask
                p = jnp.exp(s - Lc)
                ds = p * (_dot(dom, v2, NT) - dlc)
                ds_ref[j] += ds
                dsc = ds.astype(q.dtype)
                dq_st = _dot(dsc, k2, NN)
                res.append((jnp.where(low, dq_st[:H2], dq_st[H2:]) * 0.125, _dot(dsc, qs, TN),
                            _dot(p.astype(q.dtype), dom, TN)))
            (dq_lo, dk_a, dv_a), (dq_hi, dk_b, dv_b) = res
            dq = _merge_rows(dq_lo, dq_hi, R)
            if has_prev:
                emit(sl, dq, _merge_rows(dk_a[H2:] + dk_b[:H2], dk_b[H2:], R),
                     _merge_rows(dv_a[H2:] + dv_b[:H2], dv_b[H2:], R), dk_a[:H2], dv_a[:H2])
            else:
                emit(sl, dq, _merge_rows(dk_a[:H2] + dk_b[:H2], dk_a[H2:] + dk_b[H2:], R),
                     _merge_rows(dv_a[:H2] + dv_b[:H2], dv_a[H2:] + dv_b[H2:], R), None, None)

    def first_step():
        ok = pl.program_id(nd_outer) == 0 if has_prev else pl.program_id(0) == 0
        for a in range(nd_outer if has_prev else 0):
            ok = ok & (pl.program_id(a) == 0)
        return ok

    if has_prev:
        def body(q_ref, kp_ref, kc_ref, vp_ref, vc_ref, do_ref, L_ref, dl_ref, ba_ref, bb_ref, prev_in, out_ref,
                 dsa_ref, dsb_ref, hold, new, pp):
            del prev_in
            s_id = pl.program_id(nd_outer)

            @pl.when(first_step())
            def _():
                dsa_ref[...] = jnp.zeros_like(dsa_ref)
                dsb_ref[...] = jnp.zeros_like(dsb_ref)

            @pl.when(s_id < N)
            def _():
                def emit(sl, dq, dk, dv, dk_prev, dv_prev):
                    new[:, sl] = dq
                    new[:, slice(AW + sl.start, AW + sl.stop)] = dk
                    new[:, slice(2 * AW + sl.start, 2 * AW + sl.stop)] = dv
                    pp[:, sl] = dk_prev
                    pp[:, slice(AW + sl.start, AW + sl.stop)] = dv_prev

                heads(q_ref, kp_ref, kc_ref, vp_ref, vc_ref, do_ref, L_ref, dl_ref, ba_ref, bb_ref,
                      _prev_mask(s_id, H2, T), dsa_ref, dsb_ref, emit)

            @pl.when((s_id >= 1) & (s_id < N))
            def _():
                _st(out_ref, hold[:, 0:AW], slice(0, AW))
                late = _merge_rows(jnp.zeros((H2, 2 * AW), F32), pp[...], R)
                _st(out_ref, hold[:, AW:W3] + late, slice(AW, W3))

            @pl.when(s_id == N)
            def _():
                _st(out_ref, hold[...])

            @pl.when(s_id < N)
            def _():
                hold[...] = new[...]

        cur = lambda n: jnp.minimum(n, N - 1)
        prev = lambda n: jnp.clip(n - 1, 0, N - 1)
        delayed = lambda n: jnp.maximum(n - 1, 0)
        in_specs = [geom.spec(AW, cq, cur), geom.spec(AW, ck, prev), geom.spec(AW, ck, cur),
                    geom.spec(AW, cv, prev), geom.spec(AW, cv, cur),
                    geom.spec(AW, 0, cur), geom.spec(AW, 0, cur), geom.spec(AW, 0, cur)]
        args = [qv] * 5 + [geom.view(dattn), geom.view(Lb), geom.view(dlb)]
        out_spec = geom.spec(W3, gi, delayed)
        grid = geom.outer + (N + 1,)
        scratch = [pltpu.VMEM((T, W3), F32), pltpu.VMEM((T, W3), F32), pltpu.VMEM((H2, 2 * AW), F32)]
    else:
        def body(q_ref, kc_ref, vc_ref, do_ref, L_ref, dl_ref, ba_ref, bb_ref, prev_in, out_ref, dsa_ref, dsb_ref):
            del prev_in

            @pl.when(first_step())
            def _():
                dsa_ref[...] = jnp.zeros_like(dsa_ref)
                dsb_ref[...] = jnp.zeros_like(dsb_ref)

            def emit(sl, dq, dk, dv, dk_prev, dv_prev):
                _st(out_ref, dq, sl)
                _st(out_ref, dk, slice(AW + sl.start, AW + sl.stop))
                _st(out_ref, dv, slice(2 * AW + sl.start, 2 * AW + sl.stop))

            heads(q_ref, None, kc_ref, None, vc_ref, do_ref, L_ref, dl_ref, ba_ref, bb_ref, None, dsa_ref, dsb_ref,
                  emit)

        cur = lambda n: n
        in_specs = [geom.spec(AW, cq, cur), geom.spec(AW, ck, cur), geom.spec(AW, cv, cur),
                    geom.spec(AW, 0, cur), geom.spec(AW, 0, cur), geom.spec(AW, 0, cur)]
        args = [qv] * 3 + [geom.view(dattn), geom.view(Lb), geom.view(dlb)]
        out_spec = geom.spec(W3, gi, cur)
        grid = geom.outer
        scratch = []

    whole = lambda b: pl.BlockSpec(b.shape, lambda *ids: (0, 0, 0))
    biases = [b.reshape(NH // 2, 2 * H2, b.shape[2]) for b in biases]
    in_specs += [whole(b) for b in biases]
    in_specs.append(pl.BlockSpec(memory_space=pl.ANY))
    if dqkv_prev is None:
        last, aliases = biases[0], {}
    else:
        last, aliases = geom.view(dqkv_prev), {len(in_specs) - 1: 0}
    dq, dsa, dsb = pl.pallas_call(
        body, name=f"attn_bwd{gi}", grid=grid, in_specs=in_specs,
        out_specs=[out_spec] + [whole(b) for b in biases],
        out_shape=[_sds(geom.view_shape(QKV_W), CDT)] + [_sds(b.shape, F32) for b in biases],
        scratch_shapes=scratch,
        input_output_aliases=aliases,
        compiler_params=_cp(("arbitrary",) * len(grid), vmem=56),
    )(*args, *biases, last)
    return geom.unview(dq), [d.reshape(NH, H2, d.shape[2]) for d in (dsa, dsb)]


def _h_bwd(dh, dx2, x, norm_g, scale):
    S = x.shape[0]

    def body(dh_ref, dx2_ref, x_ref, g_ref, sc_ref, gx_ref, dsh_ref, dsc_ref, dng_ref, nat_ref):
        i = pl.program_id(0)

        @pl.when(i == 0)
        def _():
            dsh_ref[...] = jnp.zeros_like(dsh_ref)
            dsc_ref[...] = jnp.zeros_like(dsc_ref)
            dng_ref[...] = jnp.zeros_like(dng_ref)

        for r in range(R16):
            nat_ref[:, r, :] = dh_ref[r]
        dh = jnp.concatenate([nat_ref[l] for l in range(R16)], axis=0)
        g = g_ref[...]
        one_sc = 1.0 + sc_ref[...]
        xv = x_ref[...]
        r1 = lax.rsqrt(jnp.mean(xv * xv, axis=-1, keepdims=True) + EPS)
        xn = xv * r1
        dhx = dh * xn
        dxn = dh * (g * one_sc)
        dx1 = r1 * (dxn - xn * jnp.mean(dxn * xn, axis=-1, keepdims=True))
        gx_ref[...] = dx2_ref[...] + dx1
        dsh_ref[...] += jnp.sum(dh, axis=0, keepdims=True)
        dsc_ref[...] += jnp.sum(dhx * g, axis=0, keepdims=True)
        dng_ref[...] += jnp.sum(dhx * one_sc, axis=0, keepdims=True)

    nat = pl.BlockSpec((TOK, D), lambda i: (i, 0))
    vec = pl.BlockSpec((1, D), lambda i: (0, 0))
    return pl.pallas_call(
        body, name="h_bwd", grid=(S // TOK,),
        in_specs=[pl.BlockSpec((R16, R16, D), lambda i: (0, i, 0)), nat, nat, vec, vec],
        out_specs=[nat, vec, vec, vec],
        out_shape=[_sds((S, D), F32), _sds((1, D), F32), _sds((1, D), F32), _sds((1, D), F32)],
        scratch_shapes=[pltpu.VMEM((R16, R16, D), F32)],
        compiler_params=_cp(("arbitrary",)),
    )(dh, dx2, x, norm_g, scale)


def _local_step(x, target, shift, scale, gate, norm_g, w_in_full, pool_w, pool_scale, wab, wpb, wout, rel_bias,
                final_g):
    st = _local_bias_tables(_local_pre(x, shift, scale, norm_g), rel_bias)
    st = _local_grads(st, target, gate, w_in_full, pool_w, pool_scale, wab, wpb, wout, final_g)
    d_rel_bias = _local_bias_grads(st, _BIAS_GRADS_EARLY + _BIAS_GRADS_LATE)
    gx, dmod, dng = _local_input_grads(st, w_in_full, norm_g, scale)
    return dict(loss=st["loss"], grad_x=gx, dw_in=st["dw_in"], dw_ab=st["dw_ab"], dw_pb=st["dw_pb"],
                dw_out=st["dw_out"], d_norm_g=dng, d_pool_w=st["dpw"], d_pool_scale=st["dps"],
                d_rel_bias=d_rel_bias.T, d_final_g=st["gfg"], dmod=dmod)


def _local_pre(x, shift, scale, norm_g, dep=None):
    S = x.shape[0]
    h = _norm_mod(x, norm_g, shift, scale, dep).reshape(S, D)
    return dict(x=x, h=h, ht=_tcast(h, "h_transpose"))


def _local_bias_tables(st, rel_bias, dep=None):
    LS = st["x"].shape[0] // R16
    geoms = [_Geom(gi, LS) for gi in range(3)]
    idx_np = [[g.bucket_index(sub) for sub in range(2)] for g in geoms]
    buckets = [[tuple(int(b) for b in np.unique(ix) if b >= 0) for ix in pair] for pair in idx_np]
    idxs = [[jnp.asarray(ix) for ix in pair] for pair in idx_np]
    biases = [[_bias_table(idxs[gi][sub], buckets[gi][sub], rel_bias, gi, sub, dep) for sub in range(2)]
              for gi in range(3)]
    return dict(st, geoms=geoms, idxs=idxs, buckets=buckets, biases=biases)


def _local_grads(st, target, gate, w_in_full, pool_w, pool_scale, wab, wpb, wout, final_g):
    geoms, idxs, biases, x, h, ht = (st[k] for k in ("geoms", "idxs", "biases", "x", "h", "ht"))
    S = h.shape[0]
    LS = S // R16
    qkv, rest = _proj(h, w_in_full)

    os_, ls_ = [], []
    for gi in range(2):
        o, l = _attn_fwd(qkv, biases[gi], geoms[gi])
        os_.append(o)
        ls_.append(l)
    attn, Lb, ag = _attn_fwd(qkv, biases[2], geoms[2], others=(os_, ls_, rest))
    pt, mp, pg = _pool_fwd(rest, pool_w, pool_scale)
    dx2, dmo3, loss, gfg, dgate, dw_out = _merge_out_loss(ag, pg, rest, wab, wpb, wout, x, target, gate, final_g)
    dmo = dmo3.reshape(S, D)

    dw_ab, dw_pb, drest, dattn, dlb, dmp, dps = _bwd_gates(dmo, wout, wab, wpb, rest, ag, pg, attn, mp, pool_scale)
    drest, dpw = _pool_bwd(dmp, pt, pool_w, drest)

    dqkv = None
    dsums = []
    for gi in range(3):
        dqkv, dsum = _attn_bwd(qkv, dattn, Lb, dlb, biases[gi], geoms[gi], dqkv)
        dsums.append(dsum)

    dw_in = _mm_tn(ht, dqkv, n_total=8192, out_blk=lambda j: j, name="dw_in_qkv", with_low=True)
    dw_in, dw_in_low = _mm_tn(ht, drest, n_total=8192, out_blk=_rest_blk, name="dw_in_rest", prev=dw_in, with_low=True)
    return dict(st, loss=loss[0, 0], dw_in=dw_in, dw_in_low=dw_in_low, dw_ab=dw_ab, dw_pb=dw_pb, dw_out=dw_out, dpw=dpw, dps=dps, gfg=gfg,
                dgate=dgate, dsums=dsums, dqkv=dqkv, drest=drest, dx2=dx2)


def _local_bias_grads(st, which, dep=None):
    rows = []
    for gi in range(3):
        got = [_bias_grad(st["dsums"][gi][sub], st["idxs"][gi][sub], st["buckets"][gi][sub], gi, sub, dep)
               for sub in range(2) if (gi, sub) in which]
        rows.append(sum(got) if got else jnp.zeros((NH, 32), F32))
    return jnp.concatenate(rows, axis=0)


_BIAS_GRADS_EARLY = ((0, 0), (0, 1), (1, 0), (2, 0))
_BIAS_GRADS_LATE = ((1, 1), (2, 1))


def _dh_h_bwd(dqkv, drest, w, dx2, x, norm_g, scale, dep=None):
    S = x.shape[0]
    dep_specs, dep_args = _after(dep)

    def body(a_ref, b_ref, w_ref, dx2_ref, x_ref, g_ref, sc_ref, *rest):
        gx_ref, dsh_ref, dsc_ref, dng_ref, nat_ref = rest[-5:]
        i = pl.program_id(0)

        @pl.when(i == 0)
        def _():
            dsh_ref[...] = jnp.zeros_like(dsh_ref)
            dsc_ref[...] = jnp.zeros_like(dsc_ref)
            dng_ref[...] = jnp.zeros_like(dng_ref)

        a, b = _ld(a_ref), _ld(b_ref)
        acc = _dot(a[:, 0:512], _w512(w_ref, 0), NT)
        for j in range(1, 9):
            acc = acc + _dot(a[:, 512 * j:512 * (j + 1)], _w512(w_ref, j), NT)
        off = 0
        for lo, hi in REST_COLS:
            for j in range(lo, hi):
                acc = acc + _dot(b[:, off:off + 512], _w512(w_ref, j), NT)
                off += 512
        for r in range(R16):
            nat_ref[:, r, :] = acc[R16 * r:R16 * (r + 1)]
        dh = jnp.concatenate([nat_ref[l] for l in range(R16)], axis=0)
        g = g_ref[...]
        one_sc = 1.0 + sc_ref[...]
        xv = x_ref[...]
        r1 = lax.rsqrt(jnp.mean(xv * xv, axis=-1, keepdims=True) + EPS)
        xn = xv * r1
        dhx = dh * xn
        dxn = dh * (g * one_sc)
        dx1 = r1 * (dxn - xn * jnp.mean(dxn * xn, axis=-1, keepdims=True))
        gx_ref[...] = dx2_ref[...] + dx1
        dsh_ref[...] += jnp.sum(dh, axis=0, keepdims=True)
        dsc_ref[...] += jnp.sum(dhx * g, axis=0, keepdims=True)
        dng_ref[...] += jnp.sum(dhx * one_sc, axis=0, keepdims=True)

    nat = pl.BlockSpec((TOK, D), lambda i: (i, 0))
    vec = pl.BlockSpec((1, D), lambda i: (0, 0))
    slab = lambda W: pl.BlockSpec((R16, R16, W), lambda i: (0, i, 0))
    return pl.pallas_call(
        body, name="dh_h_bwd", grid=(S // TOK,),
        in_specs=[slab(QKV_W), slab(REST_W), _resident((8, D, 1024)), nat, nat, vec, vec] + dep_specs,
        out_specs=[nat, vec, vec, vec],
        out_shape=[_sds((S, D), F32), _sds((1, D), F32), _sds((1, D), F32), _sds((1, D), F32)],
        scratch_shapes=[pltpu.VMEM((R16, R16, D), F32)],
        compiler_params=_cp(("arbitrary",), vmem=56),
    )(dqkv, drest, w, dx2, x, norm_g, scale, *dep_args)


def _local_input_grads(st, w_in_full, norm_g, scale, dep=None):
    S = st["h"].shape[0]
    LS = S // R16
    gx, dsh, dsc, dng = _dh_h_bwd(st["dqkv"].reshape(R16, LS, QKV_W), st["drest"].reshape(R16, LS, REST_W),
                                  w_in_full, st["dx2"], st["x"], norm_g, scale, dep)
    return gx, jnp.concatenate([dsh, dsc, st["dgate"]], axis=1), dng


def _my_place():
    return lax.axis_index("x"), lax.axis_index("y"), lax.axis_index("c")


def _block_of(ref, axis, idx, width):
    if axis is None:
        return ref.at[idx]
    start = pl.multiple_of(idx * width, width)
    if axis == 0:
        return ref.at[pl.ds(start, width)]
    return ref.at[:, pl.ds(start, width)]


_SEM = pl.BlockSpec(memory_space=pltpu.SEMAPHORE)
_EFFECT = pltpu.SideEffectType.DATAFLOW_SIDE_EFFECTING


def _split_start(name, srcs, lands, ncopy, plan, dep=None):
    arrs = list(srcs) + list(lands)
    na, ns = len(arrs), len(srcs)
    dep_specs, dep_args = _after(dep)

    def body(*refs):
        send_sems, recv_sems, token = refs[na + len(dep_args)], refs[na + len(dep_args) + 1], refs[-1]
        for k, (s, d, dev) in enumerate(plan(refs[:ns], refs[ns:na])):
            pltpu.make_async_remote_copy(src_ref=s, dst_ref=d, send_sem=send_sems.at[k], recv_sem=recv_sems.at[k],
                                         device_id=dev, device_id_type=MESH).start()
        token[...] = jnp.zeros_like(token)

    outs = pl.pallas_call(
        body, name=name,
        out_shape=(pltpu.SemaphoreType.DMA((ncopy,)), pltpu.SemaphoreType.DMA((ncopy,)),
                   *[pltpu.HBM(a.shape, a.dtype) for a in arrs], _sds((8, 128), F32)),
        in_specs=[_HBM] * na + dep_specs,
        out_specs=(_SEM, _SEM, *[_HBM] * na, pl.BlockSpec(memory_space=pltpu.VMEM)),
        input_output_aliases={i: 2 + i for i in range(na)},
        compiler_params=pltpu.CompilerParams(has_side_effects=_EFFECT),
    )(*[pltpu.with_memory_space_constraint(a, pltpu.HBM) for a in arrs], *dep_args)
    return (outs[0], outs[1]), list(outs[2:2 + ns]), list(outs[2 + ns:2 + na]), outs[-1]


def _split_wait(name, sems, srcs, lands, plan, after):
    arrs = list(srcs) + list(lands)
    na, ns = len(arrs), len(srcs)
    after = list(after) if isinstance(after, (list, tuple)) else [after]

    def body(*refs):
        send_sems, recv_sems = refs[na], refs[na + 1]
        for k, (s, d, dev) in enumerate(plan(refs[:ns], refs[ns:na])):
            cp = pltpu.make_async_remote_copy(src_ref=s, dst_ref=d, send_sem=send_sems.at[k], recv_sem=recv_sems.at[k],
                                              device_id=dev, device_id_type=MESH)
            cp.wait_send()
            cp.wait_recv()

    outs = pl.pallas_call(
        body, name=name, out_shape=[pltpu.HBM(a.shape, a.dtype) for a in arrs],
        in_specs=[_HBM] * na + [_SEM, _SEM] + [pl.BlockSpec(memory_space=pl.ANY)] * len(after),
        out_specs=[_HBM] * na,
        input_output_aliases={i: i for i in range(na)},
        compiler_params=pltpu.CompilerParams(has_side_effects=_EFFECT),
    )(*arrs, sems[0], sems[1], *after)
    return list(outs[:ns]), list(outs[ns:])


def _gather_chips_plan(layout):
    def plan(src, land):
        x, y, c = _my_place()
        me = 4 * x + 2 * y + c
        mine = [_block_of(land[p], layout[p][0], me, layout[p][1]) for p in range(len(land))]
        return [(mine[p], mine[p], (cx, cy, c)) for p in range(len(land)) for cx, cy in [(1 - x, y), (x, 1 - y)]]
    return plan


def _pair_plan(layout):
    def plan(src, land):
        x, y, c = _my_place()
        n = len(layout)
        return [(_block_of(src[p], layout[p][0], 2 * q + (1 - c), layout[p][1]), land[p].at[q], (x, y, 1 - c))
                for p in range(n) for q in range(4)] + [(src[p], land[p], (x, y, 1 - c)) for p in range(n, len(src))]
    return plan


def _chips_plan(n):
    def plan(src, land):
        x, y, c = _my_place()
        chips = [(1 - x, y), (x, 1 - y), (1 - x, 1 - y)]
        return [(src[p].at[2 * cx + cy], land[p].at[2 * x + y], (cx, cy, c)) for p in range(n) for cx, cy in chips] + \
               [(src[p], land[p].at[2 * x + y], (cx, cy, c)) for p in range(n, len(src)) for cx, cy in chips]
    return plan


def _pair_share(gathered, cover):
    n = len(gathered)

    def copies(land, sems):
        send_sems, recv_sems, fwd_send, fwd_recv = sems
        x, y, c = _my_place()
        south = c == 0
        pass_on = 4 * jnp.where(south, 1 - x, x) + 2 * jnp.where(south, y, 1 - y) + c
        to = (jnp.where(south, x, 1 - x), jnp.where(south, 1 - y, y), c)

        def forward(p, blk=pass_on):
            return pltpu.make_async_remote_copy(
                src_ref=land[p].at[blk], dst_ref=land[p].at[blk], send_sem=fwd_send.at[p],
                recv_sem=fwd_recv.at[p], device_id=to, device_id_type=MESH)

        def to_sibling(p, k, who=c):
            blk = land[p].at[2 * k + who]
            return pltpu.make_async_remote_copy(
                src_ref=blk, dst_ref=blk, send_sem=send_sems.at[4 * p + k], recv_sem=recv_sems.at[4 * p + k],
                device_id=(x, y, 1 - c), device_id_type=MESH)

        return forward, to_sibling, 3 - (2 * x + y), c

    def start_body(*refs):
        land, sems, token = refs[n:2 * n], refs[2 * n:2 * n + 4], refs[-1]
        forward, to_sibling, k_dg, _ = copies(land, sems)
        for p in range(n):
            forward(p).start()
        for p in range(n):
            for k in range(4):
                @pl.when(k != k_dg)
                def _():
                    to_sibling(p, k).start()
        token[...] = jnp.zeros_like(token)

    def wait_body(*refs):
        land, sems = refs[:n], refs[n:n + 4]
        forward, to_sibling, k_dg, c = copies(land, sems)
        for p in range(n):
            forward(p, 2 * k_dg + c).wait_recv()
            forward(p).wait_send()
            for k in range(4):
                @pl.when(k != k_dg)
                def _():
                    to_sibling(p, k).wait_send()
                    to_sibling(p, k, 1 - c).wait_recv()

    def diagonal_body(*refs):
        land, send_sems, recv_sems = refs[n:2 * n], refs[2 * n], refs[2 * n + 1]
        x, y, c = _my_place()
        k_dg = 3 - (2 * x + y)

        def to_sibling(p, who):
            blk = land[p].at[2 * k_dg + who]
            return pltpu.make_async_remote_copy(src_ref=blk, dst_ref=blk, send_sem=send_sems.at[p],
                                                recv_sem=recv_sems.at[p], device_id=(x, y, 1 - c), device_id_type=MESH)

        for p in range(n):
            to_sibling(p, c).start()
        for p in range(n):
            to_sibling(p, c).wait_send()
            to_sibling(p, 1 - c).wait_recv()

    sem_shapes = (pltpu.SemaphoreType.DMA((4 * n,)), pltpu.SemaphoreType.DMA((4 * n,)),
                  pltpu.SemaphoreType.DMA((n,)), pltpu.SemaphoreType.DMA((n,)))
    hbm_shapes = [pltpu.HBM(g.shape, g.dtype) for g in gathered]
    outs = pl.pallas_call(
        start_body, name="gather_pair_share_start", in_specs=[_HBM] * n,
        out_specs=(*[_HBM] * n, *[_SEM] * 4, pl.BlockSpec(memory_space=pltpu.VMEM)),
        out_shape=(*hbm_shapes, *sem_shapes, _sds((8, 128), F32)),
        input_output_aliases={p: p for p in range(n)},
        compiler_params=pltpu.CompilerParams(has_side_effects=_EFFECT),
    )(*[_pin(g) for g in gathered])
    lands, sems, token = outs[:n], outs[n:n + 4], outs[-1]
    after = cover(token)
    lands = pl.pallas_call(
        wait_body, name="gather_pair_share_wait",
        in_specs=[_HBM] * n + [_SEM] * 4 + [pl.BlockSpec(memory_space=pl.ANY)] * len(after), out_specs=[_HBM] * n,
        out_shape=hbm_shapes, input_output_aliases={p: p for p in range(n)},
        compiler_params=pltpu.CompilerParams(has_side_effects=_EFFECT),
    )(*lands, *sems, *after)
    return pl.pallas_call(
        diagonal_body, name="gather_pair_share_diagonal", in_specs=[_HBM] * n, out_specs=[_HBM] * n,
        out_shape=[_sds(g.shape, g.dtype) for g in gathered], input_output_aliases={p: p for p in range(n)},
        scratch_shapes=[pltpu.SemaphoreType.DMA((n,)), pltpu.SemaphoreType.DMA((n,))],
    )(*lands)


def _place_slot(arrs, slot, n_slots, name, out_dtype=None):
    n = len(arrs)

    def body(s_ref, *refs):
        for p in range(n):
            refs[n + p][...] = refs[p][...].astype(refs[n + p].dtype)

    def spec(a):
        tail = a.shape[-2:]
        return pl.BlockSpec((None,) + tail, lambda i, s_ref: (s_ref[0], 0, 0))

    in_specs = [spec(a) if a.ndim == 3 else pl.BlockSpec(a.shape, lambda i, s_ref: (0, 0)) for a in arrs]
    return pl.pallas_call(
        body, name=name,
        grid_spec=pltpu.PrefetchScalarGridSpec(num_scalar_prefetch=1, grid=(1,), in_specs=in_specs,
                                               out_specs=[spec(a) for a in arrs]),
        out_shape=[_in_hbm((n_slots,) + a.shape[-2:], out_dtype or a.dtype) for a in arrs],
        compiler_params=_cp(("arbitrary",)),
    )(slot, *[_pin(a) for a in arrs])


def _cols_from_blocks(w8, name):
    _, R, C = w8.shape

    def body(w_ref, o_ref):
        for e in range(8):
            o_ref[:, C * e:C * (e + 1)] = w_ref[e]

    return pl.pallas_call(body, name=name, out_shape=_sds((R, 8 * C), w8.dtype), compiler_params=_cp())(w8)


def _all_gather_small(srcs, name):
    n = len(srcs)
    vmem = pl.BlockSpec(memory_space=pltpu.VMEM)

    def body(*refs):
        src, out = refs[:n], refs[n:2 * n]
        send_sems, recv_sems = refs[2 * n:]
        x, y, c = _my_place()
        me = 4 * x + 2 * y + c
        flip = lambda v, f: 1 - v if f else v
        peers = [(flip(x, fx), flip(y, fy), flip(c, fc)) for fx in (0, 1) for fy in (0, 1) for fc in (0, 1)][1:]
        cps = []
        for p in range(n):
            out[p][me] = src[p][...]
            for k, peer in enumerate(peers):
                cps.append(pltpu.make_async_remote_copy(
                    src_ref=src[p], dst_ref=out[p].at[me], send_sem=send_sems.at[7 * p + k],
                    recv_sem=recv_sems.at[7 * p + k], device_id=peer, device_id_type=MESH))
        for cp in cps:
            cp.start()
        for cp in cps:
            cp.wait()

    return pl.pallas_call(
        body, name=name, in_specs=[vmem] * n, out_specs=[vmem] * n,
        out_shape=[_sds((8,) + a.shape, a.dtype) for a in srcs],
        scratch_shapes=[pltpu.SemaphoreType.DMA((7 * n,)), pltpu.SemaphoreType.DMA((7 * n,))],
    )(*srcs)


def _adaln_exchange(c, w_ada, b_part):
    vmem = pl.BlockSpec(memory_space=pltpu.VMEM)

    def body(c_ref, w_ref, b_ref, call_ref, mod_ref, part_ref, send_sems, recv_sems):
        x, y, c = _my_place()
        me = 4 * x + 2 * y + c
        flip = lambda v, f: 1 - v if f else v
        peers = [(flip(x, fx), flip(y, fy), flip(c, fc)) for fx in (0, 1) for fy in (0, 1) for fc in (0, 1)][1:]
        call_ref[me] = c_ref[...]
        first = [pltpu.make_async_remote_copy(src_ref=c_ref, dst_ref=call_ref.at[me], send_sem=send_sems.at[k],
                                              recv_sem=recv_sems.at[k], device_id=peer, device_id_type=MESH)
                 for k, peer in enumerate(peers)]
        for cp in first:
            cp.start()
        for cp in first:
            cp.wait()
        c_all = jnp.concatenate([call_ref[j] for j in range(8)], axis=0)
        part_ref[...] = _dot(c_all.astype(CDT), w_ref[...].astype(CDT), NN) + b_ref[...]
        mod_ref[me] = part_ref[pl.ds(me, 1), :]
        second = [pltpu.make_async_remote_copy(
            src_ref=part_ref.at[pl.ds(4 * px + 2 * py + pc, 1)], dst_ref=mod_ref.at[me], send_sem=send_sems.at[7 + k],
            recv_sem=recv_sems.at[7 + k], device_id=(px, py, pc), device_id_type=MESH)
            for k, (px, py, pc) in enumerate(peers)]
        for cp in second:
            cp.start()
        for cp in second:
            cp.wait()

    return pl.pallas_call(
        body, name="adaln_exchange", in_specs=[vmem] * 3, out_specs=[vmem] * 2,
        out_shape=[_sds((8, 1, D), F32), _sds((8, 1, w_ada.shape[1]), F32)],
        scratch_shapes=[pltpu.VMEM((8, w_ada.shape[1]), F32), pltpu.SemaphoreType.DMA((14,)),
                        pltpu.SemaphoreType.DMA((14,))],
        compiler_params=_cp(),
    )(c, w_ada, b_part)


def _pair_add(dw, recv, axis, width, cidx, name):
    if axis == 0:
        tr = min(width, 256)
        C = dw.shape[1]
        grid = (4, width // tr)
        nb = width // tr
        own = pl.BlockSpec((tr, C), lambda q, i, c_ref: ((2 * q + c_ref[0]) * nb + i, 0))
        oth = pl.BlockSpec((None, tr, C), lambda q, i, c_ref: (q, i, 0))
    else:
        R = dw.shape[0]
        tr = min(R, 1024)
        grid = (4, R // tr)
        own = pl.BlockSpec((tr, width), lambda q, i, c_ref: (i, 2 * q + c_ref[0]))
        oth = pl.BlockSpec((None, tr, width), lambda q, i, c_ref: (q, i, 0))

    def body(c_ref, a_ref, b_ref, o_ref):
        o_ref[...] = (a_ref[...] + b_ref[...].astype(F32)).astype(o_ref.dtype)

    return pl.pallas_call(
        body, name=name,
        grid_spec=pltpu.PrefetchScalarGridSpec(num_scalar_prefetch=1, grid=grid, in_specs=[own, oth], out_specs=oth),
        out_shape=(_in_hbm if _is_big(dw) else _sds)(recv.shape, CDT), compiler_params=_cp(("parallel", "parallel")),
    )(cidx, *[_pin(a) if _is_big(dw) else a for a in (dw, recv)])


def _add_pairs(xs, ys, name):
    n = len(xs)

    def body(*refs):
        for p in range(n):
            refs[2 * n + p][...] = refs[p][...] + refs[n + p][...]

    return pl.pallas_call(body, name=name, out_shape=[_sds(a.shape, F32) for a in xs])(*xs, *ys)


def _adam_math(w, g, m, v):
    m = ADAM_B1 * m + (1.0 - ADAM_B1) * g
    v = ADAM_B2 * v + (1.0 - ADAM_B2) * (g * g)
    m_hat = m / (1.0 - ADAM_B1 ** ADAM_STEP)
    v_hat = v / (1.0 - ADAM_B2 ** ADAM_STEP)
    delta = -ADAM_LR * (m_hat / (jnp.sqrt(v_hat) + ADAM_EPS) + ADAM_WD * w)
    return delta, m, v


def _adam(gparts, w, m, v, name, fresh=False):
    P, R, C = gparts.shape
    in_hbm = fresh or _is_big(w)
    tr = R if R <= 256 else 256

    def body(g_ref, w_ref, m_ref, v_ref, go_ref, d_ref, mo_ref, vo_ref):
        g = g_ref[0].astype(F32)
        for i in range(1, P):
            g = g + g_ref[i].astype(F32)
        d, mn, vn = _adam_math(w_ref[...], g, m_ref[...], v_ref[...])
        go_ref[...] = g
        d_ref[...] = d
        mo_ref[...] = mn
        vo_ref[...] = vn

    t = pl.BlockSpec((tr, C), lambda i: (i, 0))
    return pl.pallas_call(
        body, name=name, grid=(R // tr,),
        in_specs=[pl.BlockSpec((P, tr, C), lambda i: (0, i, 0)), t, t, t], out_specs=[t, t, t, t],
        out_shape=[(_in_hbm if in_hbm else _sds)((R, C), F32)] * 4, compiler_params=_cp(("parallel",)),
    )(*[_pin(a) if in_hbm else a for a in (gparts, w, m, v)])


def _sum_parts(gparts, name):
    P, R, C = gparts.shape

    def body(g_ref, o_ref):
        g = g_ref[0]
        for i in range(1, P):
            g = g + g_ref[i]
        o_ref[...] = g

    return pl.pallas_call(body, name=name, out_shape=_sds((R, C), F32))(gparts)


def _adam_w_ada(c_t, dm, w, m, v):
    R, C = w.shape

    def body(c_ref, dm_ref, w_ref, m_ref, v_ref, go_ref, d_ref, mo_ref, vo_ref):
        g = c_ref[:, 0:1] * dm_ref[0:1, :]
        for b in range(1, 8):
            g = g + c_ref[:, b:b + 1] * dm_ref[b:b + 1, :]
        d, mn, vn = _adam_math(w_ref[...], g, m_ref[...], v_ref[...])
        go_ref[...] = g
        d_ref[...] = d
        mo_ref[...] = mn
        vo_ref[...] = vn

    return pl.pallas_call(body, name="adam_w_ada", out_shape=[_sds((R, C), F32)] * 4,
                          compiler_params=_cp())(c_t, dm, w, m, v)


def kernel(x, c, norm_g, w_ada, b_ada, w_in, pool_w, pool_scale, w_attn_br, w_pool_br, w_out, rel_bias, final_g, loss_target, m_norm_g, m_w_ada, m_b_ada, m_w_in, m_pool_w, m_pool_scale, m_w_attn_br, m_w_pool_br, m_w_out, m_rel_bias, m_final_g, v_norm_g, v_w_ada, v_b_ada, v_w_in, v_pool_w, v_pool_scale, v_w_attn_br, v_w_pool_br, v_w_out, v_rel_bias, v_final_g):
    S = x.shape[1]
    px, py, pc = _my_place()
    me = 4 * px + 2 * py + pc
    cidx = jnp.reshape(pc, (1,)).astype(jnp.int32)

    layout = [(1, 1024), (1, 128), (1, 128), (0, 128)]
    final_g2 = final_g.reshape(1, D)
    me1 = jnp.reshape(me, (1,)).astype(jnp.int32)

    b_part = lax.dynamic_slice(b_ada, (0, me * 384), (1, 384))
    c_all, mod_all = _adaln_exchange(c, w_ada[0], b_part)
    c_all = c_all.reshape(8, D)
    mod = mod_all.reshape(1, 3 * D)
    shift, scale, gate = mod[:, :D], mod[:, D:2 * D], mod[:, 2 * D:]

    gplan = _gather_chips_plan([(None, 1)] * 4)
    lands = _place_slot([w_in[0], w_attn_br[0], w_pool_br[0], w_out[0]], me1, 8, "place_own_shards", CDT)
    g_sems, _, lands, tok = _split_start("gather_chips_start", [], lands, 8, gplan, dep=mod_all)
    st = _local_pre(x[0], shift, scale, norm_g, dep=tok)
    _, lands = _split_wait("gather_chips_wait", g_sems, [], lands, gplan, [st["ht"]])

    def bias_tables(token):
        st.update(_local_bias_tables(st, rel_bias, dep=token))
        return [b for pair in st["biases"] for b in pair]

    w_in_full, wab8, wpb8, wout8 = _pair_share(lands, bias_tables)
    wab, wpb = _cols_from_blocks(wab8, "w_attn_br_cols"), _cols_from_blocks(wpb8, "w_pool_br_cols")
    wout = wout8.reshape(D, D)

    st = _local_grads(st, loss_target[0], gate, w_in_full, pool_w[0], pool_scale, wab, wpb, wout, final_g2)

    early = [st["dpw"].reshape(AW, 128), st["dps"], st["gfg"], jnp.broadcast_to(st["loss"], (1, 128))]
    dws = [st["dw_in"], st["dw_ab"], st["dw_pb"], st["dw_out"]]
    sent = [st["dw_in_low"]] + dws[1:] + early
    pplan = _pair_plan(layout)
    pair_shapes = [(4, w, a.shape[1]) if ax == 0 else (4, a.shape[0], w) for a, (ax, w) in zip(dws, layout)]
    p_sems, sent, recv, tok = _split_start(
        "reduce_pair_start", sent,
        [lax.empty(s, a.dtype) for s, a in zip(pair_shapes + [a.shape for a in early], sent)], 20, pplan)
    d_rel_bias = _local_bias_grads(st, _BIAS_GRADS_EARLY, dep=tok)
    sent, recv = _split_wait("reduce_pair_wait", p_sems, sent, recv, pplan, d_rel_bias)
    parts = [_pair_add(dw, r, ax, w, cidx, f"pair_add{i}")
             for i, (dw, r, (ax, w)) in enumerate(zip([dws[0]] + sent[1:4], recv[:4], layout))]
    parts += _add_pairs(sent[4:], recv[4:], "pair_add_smalls")

    slot1 = jnp.reshape(2 * px + py, (1,)).astype(jnp.int32)
    cplan = _chips_plan(4)
    c_sems, parts, lands, tok = _split_start("reduce_chips_start", parts,
                                             _place_slot(parts, slot1, 4, "place_own_partials"), 24, cplan)
    d_rel_bias = d_rel_bias + _local_bias_grads(st, _BIAS_GRADS_LATE, dep=tok)
    grad_x, dmod, d_norm_g = _local_input_grads(st, w_in_full, norm_g, scale, dep=tok)

    dmod_all, ng_all, rb_all = _all_gather_small([dmod, d_norm_g, d_rel_bias], "gather_late")
    o_norm_g = _adam(ng_all, norm_g, m_norm_g, v_norm_g, "adam_norm_g", fresh=True)
    o_rel_bias_t = _adam(rb_all, rel_bias.T, m_rel_bias.T, v_rel_bias.T, "adam_rel_bias", fresh=True)
    o_rel_bias = [o.T for o in o_rel_bias_t]
    o_b_ada = _adam(dmod_all, b_ada, m_b_ada, v_b_ada, "adam_b_ada", fresh=True)
    dm_mine = lax.dynamic_slice(dmod_all.reshape(8, 3 * D), (0, me * 384), (8, 384))
    o_w_ada = [o[None] for o in _adam_w_ada(c_all.T, dm_mine, w_ada[0], m_w_ada[0], v_w_ada[0])]

    _, (g_in, g_ab, g_pb, g_out, g_pw, g_ps, g_fg, g_loss) = _split_wait(
        "reduce_chips_wait", c_sems, parts, lands, cplan, [o_w_ada[0], o_b_ada[0], o_norm_g[0], o_rel_bias_t[0]])
    o_w_in = [o[None] for o in _adam(g_in, w_in[0], m_w_in[0], v_w_in[0], "adam_w_in")]
    o_w_ab = [o[None] for o in _adam(g_ab, w_attn_br[0], m_w_attn_br[0], v_w_attn_br[0], "adam_w_attn_br")]
    o_w_pb = [o[None] for o in _adam(g_pb, w_pool_br[0], m_w_pool_br[0], v_w_pool_br[0], "adam_w_pool_br")]
    o_w_out = [o[None] for o in _adam(g_out, w_out[0], m_w_out[0], v_w_out[0], "adam_w_out")]
    o_pool_w = [o.reshape(pool_w.shape) for o in _adam(g_pw, pool_w.reshape(AW, 128), m_pool_w.reshape(AW, 128),
                                                       v_pool_w.reshape(AW, 128), "adam_pool_w")]
    o_pool_scale = _adam(g_ps, pool_scale, m_pool_scale, v_pool_scale, "adam_pool_scale")
    o_final_g = [o.reshape(D) for o in _adam(g_fg, final_g2, m_final_g.reshape(1, D), v_final_g.reshape(1, D),
                                             "adam_final_g")]
    loss = _sum_parts(g_loss, "sum_loss")[0, 0]

    per_w = [o_norm_g, o_w_ada, o_b_ada, o_w_in, o_pool_w, o_pool_scale, o_w_ab, o_w_pb, o_w_out, o_rel_bias, o_final_g]
    outs = [loss, grad_x[None]]
    for k in range(4):
        outs += [o[k] for o in per_w]
    return tuple(outs)
```

```python
import functools
import math

import numpy as np
import jax
import jax.numpy as jnp
from jax import lax
from jax.experimental import pallas as pl
from jax.experimental.pallas import tpu as pltpu

F32 = jnp.float32
CDT = jnp.bfloat16
D = 1024
HD = 64
NH = 8
AW = 512
QKV_W = 4608
REST_W = 3584
R16 = 16
EPS = 1e-6
NEG = -1e30
N_BACK = 128
ADAM_LR, ADAM_B1, ADAM_B2, ADAM_EPS, ADAM_WD, ADAM_STEP = 0.001, 0.9, 0.999, 1e-08, 0.01, 10
MESH = pl.DeviceIdType.MESH
VMEM_MB = 1024 * 1024


def _cp(sem=None, vmem=48, **kw):
    if sem is not None:
        kw["dimension_semantics"] = sem
    return pltpu.CompilerParams(vmem_limit_bytes=vmem * VMEM_MB, **kw)


def _sds(shape, dtype):
    return jax.ShapeDtypeStruct(tuple(shape), dtype)


def _in_hbm(shape, dtype):
    return pltpu.HBM(tuple(shape), dtype)


def _pin(a):
    return pltpu.with_memory_space_constraint(a, pltpu.HBM)


def _is_big(a):
    return a.size * a.dtype.itemsize >= 4 * VMEM_MB


def _rest_blk(jj):
    return jnp.where(jj == 0, 9, jnp.where(jj == 6, 10, jj + 10))


def _sigmoid(z):
    return 0.5 * jnp.tanh(0.5 * z) + 0.5


def _dot(a, b, dims):
    return lax.dot_general(a, b, (dims, ((), ())), preferred_element_type=F32)


NN = ((1,), (0,))
NT = ((1,), (1,))
TN = ((0,), (0,))


def _ld(ref):
    if len(ref.shape) == 2:
        return ref[...]
    return jnp.concatenate([ref[a] for a in range(ref.shape[0])], axis=0)


def _st(ref, val, cols=None):
    val = val.astype(ref.dtype)
    if len(ref.shape) == 2:
        if cols is None:
            ref[...] = val
        else:
            ref[:, cols] = val
        return
    rows = ref.shape[1]
    for a in range(ref.shape[0]):
        if cols is None:
            ref[a] = val[a * rows:(a + 1) * rows]
        else:
            ref[a, :, cols] = val[a * rows:(a + 1) * rows]


REST_COLS = ((9, 10), (11, 16), (10, 11))
_HBM = pl.BlockSpec(memory_space=pltpu.HBM)


def _resident(shape):
    return pl.BlockSpec(shape, lambda *ids: (0,) * len(shape), pipeline_mode=pl.Buffered(1))


def _w512(w_ref, j):
    return w_ref[j // 2, :, 512 * (j % 2):512 * (j % 2 + 1)]


def _proj(h, w):
    S = h.shape[0]
    tm = 512

    def body(a_ref, w_ref, q_ref, r_ref):
        a = a_ref[...]
        for j in range(9):
            q_ref[:, 512 * j:512 * (j + 1)] = _dot(a, _w512(w_ref, j), NN).astype(q_ref.dtype)
        off = 0
        for lo, hi in REST_COLS:
            for j in range(lo, hi):
                r_ref[:, off:off + 512] = _dot(a, _w512(w_ref, j), NN)
                off += 512

    return pl.pallas_call(
        body, name="proj", grid=(S // tm,),
        in_specs=[pl.BlockSpec((tm, D), lambda i: (i, 0)), _resident((8, D, 1024))],
        out_specs=[pl.BlockSpec((tm, QKV_W), lambda i: (i, 0)), pl.BlockSpec((tm, REST_W), lambda i: (i, 0))],
        out_shape=[_sds((S, QKV_W), CDT), _sds((S, REST_W), F32)],
        compiler_params=_cp(("parallel",), vmem=56),
    )(h, w)


def _after(dep):
    return ([], []) if dep is None else ([pl.BlockSpec(memory_space=pl.ANY)], [dep])


def _dh(dqkv, drest, w, dep=None):
    S = dqkv.shape[0]
    tm = 512
    dep_specs, dep_args = _after(dep)

    def body(a_ref, b_ref, w_ref, *rest):
        o_ref = rest[-1]
        acc = _dot(a_ref[:, 0:512], _w512(w_ref, 0), NT)
        for j in range(1, 9):
            acc = acc + _dot(a_ref[:, 512 * j:512 * (j + 1)], _w512(w_ref, j), NT)
        off = 0
        for lo, hi in REST_COLS:
            for j in range(lo, hi):
                acc = acc + _dot(b_ref[:, off:off + 512], _w512(w_ref, j), NT)
                off += 512
        o_ref[...] = acc

    return pl.pallas_call(
        body, name="dh", grid=(S // tm,),
        in_specs=[pl.BlockSpec((tm, QKV_W), lambda i: (i, 0)), pl.BlockSpec((tm, REST_W), lambda i: (i, 0)),
                  _resident((8, D, 1024))] + dep_specs,
        out_specs=pl.BlockSpec((tm, D), lambda i: (i, 0)),
        out_shape=_sds((S, D), F32),
        compiler_params=_cp(("parallel",), vmem=56),
    )(dqkv, drest, w, *dep_args)


def _mm_tn(at, b, *, n_total, out_blk, name, prev=None, tn=512, with_low=False):
    R, S = at.shape
    nb = b.shape[1] // tn
    n_out = 2 if with_low else 1

    def body(*refs):
        res = _dot(refs[0][...], refs[1][...], NN)
        refs[-n_out][...] = res
        if with_low:
            refs[-1][...] = res.astype(CDT)

    in_specs = [_resident((R, S)), pl.BlockSpec((S, tn), lambda j: (0, j))]
    args = [at, b]
    aliases = {}
    if prev is not None:
        prev = list(prev) if with_low else [prev]
        in_specs += [pl.BlockSpec(memory_space=pl.ANY)] * n_out
        args += prev
        aliases = {2 + k: k for k in range(n_out)}
    out_spec = pl.BlockSpec((R, tn), lambda j: (0, out_blk(j)))
    outs = pl.pallas_call(
        body, name=name, grid=(nb,), in_specs=in_specs,
        out_specs=[out_spec] * n_out,
        out_shape=[_sds((R, n_total), F32), _sds((R, n_total), CDT)][:n_out],
        input_output_aliases=aliases,
        compiler_params=_cp(("parallel",)),
    )(*args)
    return tuple(outs) if with_low else outs[0]


def _tcast(a, name):
    M, C = a.shape
    tm = 512

    def body(a_ref, o_ref):
        o_ref[...] = a_ref[...].astype(F32).T.astype(o_ref.dtype)

    return pl.pallas_call(
        body, name=name, grid=(M // tm,),
        in_specs=[pl.BlockSpec((tm, C), lambda i: (i, 0))],
        out_specs=pl.BlockSpec((C, tm), lambda i: (0, i)),
        out_shape=_sds((C, M), CDT), compiler_params=_cp(("parallel",)),
    )(a)


TOK = R16 * R16


def _swap16():
    i = lax.broadcasted_iota(jnp.int32, (TOK, TOK), 0)
    j = lax.broadcasted_iota(jnp.int32, (TOK, TOK), 1)
    return jnp.where(j == jnp.bitwise_and(i, 15) * 16 + jnp.right_shift(i, 4), 1.0, 0.0).astype(CDT)


def _norm_mod(x, norm_g, shift, scale, dep=None):
    S = x.shape[0]
    dep_specs, dep_args = _after(dep)

    def body(x_ref, g_ref, sh_ref, sc_ref, *rest):
        xv = x_ref[...]
        xn = xv * lax.rsqrt(jnp.mean(xv * xv, axis=-1, keepdims=True) + EPS)
        h = (xn * (g_ref[...] * (1.0 + sc_ref[...])) + sh_ref[...]).astype(CDT)
        _st(rest[-1], _dot(_swap16(), h, NN))

    vec = pl.BlockSpec((1, D), lambda i: (0, 0))
    return pl.pallas_call(
        body, name="norm_mod", grid=(S // TOK,),
        in_specs=[pl.BlockSpec((TOK, D), lambda i: (i, 0)), vec, vec, vec] + dep_specs,
        out_specs=pl.BlockSpec((R16, R16, D), lambda i: (0, i, 0)),
        out_shape=_sds((R16, S // R16, D), CDT), compiler_params=_cp(("parallel",)),
    )(x, norm_g, shift, scale, *dep_args)


def _t5_bucket_np(n):
    nf = np.maximum(n, 1).astype(np.float32)
    large = 16 + (np.log(nf / np.float32(16)) / np.float32(math.log(2048 / 16)) * np.float32(16)).astype(np.int32)
    large = np.minimum(large, 31)
    return np.where(n < 16, n, large).astype(np.int32)


class _Geom:
    def __init__(self, gi, LS):
        self.gi, self.LS = gi, LS
        self.dil = (1, 4, 16)[gi]
        if gi == 0:
            self.lead, self.rows, self.n_tiles, self.has_prev = (R16,), 16, LS // 16, True
            a = np.arange(R16)[:, None]
            ll = np.arange(16)[None, :]
            self.pos = (16 * ll + a).reshape(-1)
        elif gi == 1:
            self.lead, self.rows, self.n_tiles, self.has_prev = (4,), 64, LS // 64, True
            a = np.arange(4)[:, None]
            ll = np.arange(64)[None, :]
            self.pos = (4 * ll + a).reshape(-1)
        else:
            self.lead, self.rows, self.n_tiles, self.has_prev = (), LS, 1, False
            self.pos = np.arange(LS)
        self.T = self.pos.shape[0]
        self.Tk = 2 * self.T if self.has_prev else self.T
        self.outer = {0: (), 1: (4,), 2: (R16,)}[gi]

    def view(self, arr):
        S, C = arr.shape
        if self.gi == 1:
            return arr.reshape(4, 4, self.LS, C)
        return arr.reshape(R16, self.LS, C)

    def view_shape(self, C):
        return (4, 4, self.LS, C) if self.gi == 1 else (R16, self.LS, C)

    def unview(self, arr):
        return arr.reshape(R16 * self.LS, arr.shape[-1])

    def spec(self, W, cblk, tile):
        if self.gi == 0:
            return pl.BlockSpec((R16, 16, W), lambda n: (0, tile(n), cblk))
        if self.gi == 1:
            return pl.BlockSpec((4, None, 64, W), lambda r, n: (0, r, tile(n), cblk))
        return pl.BlockSpec((None, self.LS, W), lambda r: (r, 0, cblk))

    def halves(self):
        R = self.rows
        idx = np.arange(self.T).reshape(-1, R)
        return idx[:, :R // 2].reshape(-1), idx[:, R // 2:].reshape(-1)

    def bucket_index(self, sub):
        lo, hi = self.halves()
        pos = self.pos
        if sub == 0:
            pq = pos[lo]
            pk = np.concatenate([pos[hi] - self.T, pos[lo]]) if self.has_prev else np.concatenate([pos[lo], pos[hi]])
        else:
            pq = pos[hi]
            pk = np.concatenate([pos[lo], pos[hi]])
        dist = pq[:, None] - pk[None, :]
        ok = (dist >= 0) & (dist <= N_BACK)
        bucket = _t5_bucket_np(np.clip(dist, 0, N_BACK) * self.dil)
        return np.where(ok, bucket, -1).astype(np.int32)


def _bias_table(idx, buckets, rel_bias, gi, sub, dep=None):
    T, Tk = idx.shape
    tr = 32
    dep_specs, dep_args = _after(dep)

    def body(idx_ref, rb_ref, *rest):
        h = pl.program_id(0)
        ix = idx_ref[...]
        acc = jnp.full(ix.shape, NEG, F32)
        for b in buckets:
            acc = jnp.where(ix == b, rb_ref[b, gi * NH + h], acc)
        rest[-1][...] = acc

    return pl.pallas_call(
        body, name=f"bias_table{gi}{'ab'[sub]}", grid=(NH, T // tr),
        in_specs=[pl.BlockSpec((tr, Tk), lambda h, i: (i, 0)), pl.BlockSpec(memory_space=pltpu.SMEM)] + dep_specs,
        out_specs=pl.BlockSpec((None, tr, Tk), lambda h, i: (h, i, 0)),
        out_shape=_sds((NH, T, Tk), F32), compiler_params=_cp(("parallel", "parallel")),
    )(idx, rel_bias, *dep_args)


def _bias_grad(dsum, idx, buckets, gi, sub, dep=None):
    _, T, Tk = dsum.shape
    dep_specs, dep_args = _after(dep)

    def body(ds_ref, idx_ref, *rest):
        o_ref, r_ref = rest[-2], rest[-1]
        r_ref[...] = jnp.zeros_like(r_ref)
        ix, ds = idx_ref[...], ds_ref[...]
        for b in buckets:
            r_ref[b:b + 1, :] = jnp.sum(jnp.where(ix == b, ds, 0.0), axis=0, keepdims=True)
        o_ref[...] = jnp.broadcast_to(jnp.sum(r_ref[...], axis=1, keepdims=True), (32, 128))

    out = pl.pallas_call(
        body, name=f"bias_grad{gi}{'ab'[sub]}", grid=(NH,),
        in_specs=[pl.BlockSpec((None, T, Tk), lambda h: (h, 0, 0)), pl.BlockSpec((T, Tk), lambda h: (0, 0))] + dep_specs,
        out_specs=pl.BlockSpec((None, 32, 128), lambda h: (h, 0, 0)),
        out_shape=_sds((NH, 32, 128), F32), scratch_shapes=[pltpu.VMEM((32, Tk), F32)],
        compiler_params=_cp(("parallel",)),
    )(dsum, idx, *dep_args)
    return out[:, :, 0]


def _prev_mask(n, T, Tk):
    col = lax.broadcasted_iota(jnp.int32, (1, Tk), 1)
    return jnp.where((col < T) & (n == 0), NEG, 0.0).astype(F32)


def _split_rows(v, R):
    A, half = v.shape[0] // R, R // 2
    via_f32 = v.dtype != F32 and half % 16 != 0
    w = v.astype(F32) if via_f32 else v
    lo = jnp.concatenate([w[a * R:a * R + half] for a in range(A)], axis=0)
    hi = jnp.concatenate([w[a * R + half:(a + 1) * R] for a in range(A)], axis=0)
    return (lo.astype(v.dtype), hi.astype(v.dtype)) if via_f32 else (lo, hi)


def _merge_rows(lo, hi, R):
    half = R // 2
    A = lo.shape[0] // half
    return jnp.concatenate([x[a * half:(a + 1) * half] for a in range(A) for x in (lo, hi)], axis=0)


def _attn_fwd(qkv, biases, geom, others=None):
    S = qkv.shape[0]
    gi, T, R, has_prev = geom.gi, geom.T, geom.rows, geom.has_prev
    H2 = T // 2
    qv = geom.view(qkv)
    cq, ck, cv = 3 * gi, 3 * gi + 1, 3 * gi + 2
    assert others is None or not has_prev

    def body(*refs):
        if has_prev:
            q_ref, kp_ref, kc_ref, vp_ref, vc_ref, ba_ref, bb_ref, o_ref, l_ref = refs
        elif others is not None:
            (q_ref, kc_ref, vc_ref, ba_ref, bb_ref, o0_ref, o1_ref, l0_ref, l1_ref, z_ref,
             attn_ref, L_ref, ag_ref) = refs
        else:
            q_ref, kc_ref, vc_ref, ba_ref, bb_ref, o_ref, l_ref = refs
        q_lo, q_hi = _split_rows(_ld(q_ref), R)
        kc_lo, kc_hi = _split_rows(_ld(kc_ref), R)
        vc_lo, vc_hi = _split_rows(_ld(vc_ref), R)
        k_b = jnp.concatenate([kc_lo, kc_hi], axis=0)
        v_b = jnp.concatenate([vc_lo, vc_hi], axis=0)
        if has_prev:
            n = pl.program_id(len(geom.outer))
            k_a = jnp.concatenate([_split_rows(_ld(kp_ref), R)[1], kc_lo], axis=0)
            v_a = jnp.concatenate([_split_rows(_ld(vp_ref), R)[1], vc_lo], axis=0)
            pm = _prev_mask(n, H2, T)
        else:
            k_a, v_a, pm = k_b, v_b, None
        low = lax.broadcasted_iota(jnp.int32, (1, 2 * HD), 1) < HD
        for j in range(NH // 2):
            cols = slice(2 * HD * j, 2 * HD * (j + 1))
            o_half, l_half = [], []
            for q, k, v, b_ref, mask in ((q_lo, k_a, v_a, ba_ref, pm), (q_hi, k_b, v_b, bb_ref, None)):
                q2 = q[:, cols].astype(F32) * 0.125
                k2, v2 = k[:, cols], v[:, cols]
                qs = jnp.concatenate([jnp.where(low, q2, 0.0), jnp.where(low, 0.0, q2)], axis=0).astype(q.dtype)
                s = _dot(qs, k2, NT) + b_ref[j]
                if mask is not None:
                    s = s + mask
                m = jnp.max(s, axis=1, keepdims=True)
                p = jnp.exp(s - m)
                l = jnp.sum(p, axis=1, keepdims=True)
                pv = _dot(p.astype(v.dtype), v2, NN) / l
                lse = m + jnp.log(l)
                o_half.append(jnp.where(low, pv[:H2], pv[H2:]))
                l_half.append(jnp.where(low, lse[:H2], lse[H2:]))
            o2, lc = _merge_rows(o_half[0], o_half[1], R), _merge_rows(l_half[0], l_half[1], R)
            if others is None:
                _st(o_ref, o2, cols)
                _st(l_ref, lc, cols)
                continue
            la, lb = l0_ref[:, cols], l1_ref[:, cols]
            m = jnp.maximum(jnp.maximum(la, lb), lc)
            ea, eb, ec = jnp.exp(la - m), jnp.exp(lb - m), jnp.exp(lc - m)
            den = ea + eb + ec
            attn = (ea * o0_ref[:, cols] + eb * o1_ref[:, cols] + ec * o2) / den
            attn_ref[:, cols] = attn
            L_ref[:, cols] = m + jnp.log(den)
            z = z_ref[:, cols]
            ag = attn * (z * _sigmoid(z))
            ag_ref[:, cols] = ag.astype(ag_ref.dtype)

    cur = lambda n: n
    prev = lambda n: jnp.maximum(n - 1, 0)
    if has_prev:
        in_specs = [geom.spec(AW, cq, cur), geom.spec(AW, ck, prev), geom.spec(AW, ck, cur),
                    geom.spec(AW, cv, prev), geom.spec(AW, cv, cur)]
        args = [qv] * 5
    else:
        in_specs = [geom.spec(AW, cq, cur), geom.spec(AW, ck, cur), geom.spec(AW, cv, cur)]
        args = [qv] * 3
    nd = len(geom.outer) + (1 if has_prev else 0)
    biases = [b.reshape(NH // 2, 2 * H2, b.shape[2]) for b in biases]
    in_specs += [pl.BlockSpec(b.shape, lambda *ids: (0, 0, 0)) for b in biases]
    grid = geom.outer + ((geom.n_tiles,) if has_prev else ())
    oshape = geom.view_shape(AW)
    tile = geom.spec(AW, 0, cur)
    if others is None:
        o, l = pl.pallas_call(
            body, name=f"attn_fwd{gi}", grid=grid, in_specs=in_specs, out_specs=[tile, tile],
            out_shape=[_sds(oshape, F32), _sds(oshape, F32)],
            compiler_params=_cp(("arbitrary",) * nd),
        )(*args, *biases)
        return geom.unview(o), geom.unview(l)
    os_, ls_, rest = others
    attn, Lb, ag = pl.pallas_call(
        body, name=f"attn_fwd{gi}_merge", grid=grid, in_specs=in_specs + [tile] * 5,
        out_specs=[tile, tile, tile],
        out_shape=[_sds(oshape, F32), _sds(oshape, F32), _sds(oshape, CDT)],
        compiler_params=_cp(("arbitrary",) * nd),
    )(*args, *biases, *[geom.view(a) for a in (*os_, *ls_, rest)])
    return geom.unview(attn), geom.unview(Lb), geom.unview(ag)


def _pool_counts(r, LS, win):
    l = lax.broadcasted_iota(jnp.int32, (LS, 1), 0)
    return jnp.minimum(16 * l + (r + 1), win).astype(F32)


def _pool_fwd(rest, pool_w, pool_scale):
    S = rest.shape[0]
    LS = S // R16
    r3 = rest.reshape(R16, LS, REST_W)
    PG = 128

    def body(u_ref, z_ref, pw_ref, ps_ref, pt_ref, mp_ref, pg_ref, e_ref):
        g = pl.program_id(0)
        win = jnp.left_shift(2, g)
        row = lax.broadcasted_iota(jnp.int32, (LS, PG), 0)
        for r in range(R16):
            e_ref[R16 + r] = u_ref[r]
        for r in range(1, R16):
            e_ref[r] = jnp.where(row == 0, 0.0, pltpu.roll(u_ref[r], 1, 0))

        def double(s):
            for i in range(2 * R16 - 1, 2 * s - 1, -1):
                e_ref[i] = e_ref[i] + e_ref[i - s]

        double(1)
        for k in (1, 2, 3):
            pl.when(g >= k)(functools.partial(double, 2 ** k))
        pw = pw_ref[...].astype(CDT)
        for r in range(R16):
            u = u_ref[r]
            pooled = e_ref[R16 + r] / _pool_counts(r, LS, win) - u
            pc = pooled.astype(CDT)
            mp = _dot(pc, pw, NN)
            z = z_ref[r]
            pg = (mp * ps_ref[...]) * (z * _sigmoid(z))
            mp_ref[r] = mp
            pg_ref[r] = pg.astype(pg_ref.dtype)
            pt_ref[:, r * LS:(r + 1) * LS] = pooled.T.astype(pt_ref.dtype)

    col = lambda off: pl.BlockSpec((R16, LS, PG), lambda g: (0, 0, off + g))
    tsp = pl.BlockSpec((PG, S), lambda g: (g, 0))
    pt, mp, pg = pl.pallas_call(
        body, name="pool_fwd", grid=(4,),
        in_specs=[col(24), col(4), pl.BlockSpec((None, PG, PG), lambda g: (g, 0, 0)),
                  pl.BlockSpec((1, PG), lambda g: (0, g))],
        out_specs=[tsp, col(0), col(0)],
        out_shape=[_sds((AW, S), CDT), _sds((R16, LS, AW), F32), _sds((R16, LS, AW), CDT)],
        scratch_shapes=[pltpu.VMEM((2 * R16, LS, PG), F32)],
        compiler_params=_cp(("parallel",)),
    )(r3, r3, pool_w, pool_scale)
    return pt, mp.reshape(S, AW), pg.reshape(S, AW)


def _branch_merge(ag, pg, wab, wpb, rest):
    S = ag.shape[0]
    tm, tn = 512, 512

    def body(ag_ref, pg_ref, wa_ref, wp_ref, ga_ref, gp_ref, m_ref, mt_ref):
        ya = _dot(ag_ref[...], wa_ref[...], NN)
        yp = _dot(pg_ref[...], wp_ref[...], NN)
        mg = _sigmoid(ga_ref[...]) * ya + _sigmoid(gp_ref[...]) * yp
        m_ref[...] = mg.astype(m_ref.dtype)
        mt_ref[...] = mg.T.astype(mt_ref.dtype)

    a = pl.BlockSpec((tm, AW), lambda i, j: (i, 0))
    w = pl.BlockSpec((AW, tn), lambda i, j: (0, j))
    o = pl.BlockSpec((tm, tn), lambda i, j: (i, j))
    return pl.pallas_call(
        body, name="branch_merge", grid=(S // tm, D // tn),
        in_specs=[a, a, w, w, pl.BlockSpec((tm, tn), lambda i, j: (i, 2 + j)),
                  pl.BlockSpec((tm, tn), lambda i, j: (i, 4 + j))],
        out_specs=[o, pl.BlockSpec((tn, tm), lambda i, j: (j, i))],
        out_shape=[_sds((S, D), CDT), _sds((D, S), CDT)],
        compiler_params=_cp(("parallel", "parallel")),
    )(ag, pg, wab, wpb, rest, rest)


def _merge_out_loss(ag, pg, rest, wab, wpb, wout, x, target, gate, final_g):
    S = x.shape[0]
    LS = S // R16

    def body(ag_ref, pg_ref, ga_ref, gp_ref, wa_ref, wp_ref, w_ref, x_ref, t_ref, gate_ref, fg_ref,
             dx2_ref, dmo_ref, loss_ref, gfg_ref, dgate_ref, dw_ref):
        i = pl.program_id(0)

        @pl.when(i == 0)
        def _():
            loss_ref[...] = jnp.zeros_like(loss_ref)
            gfg_ref[...] = jnp.zeros_like(gfg_ref)
            dgate_ref[...] = jnp.zeros_like(dgate_ref)
            dw_ref[...] = jnp.zeros_like(dw_ref)

        swap = _swap16()
        ya = _dot(_ld(ag_ref), wa_ref[...], NN)
        yp = _dot(_ld(pg_ref), wp_ref[...], NN)
        mg = (_sigmoid(_ld(ga_ref)) * ya + _sigmoid(_ld(gp_ref)) * yp).astype(CDT)
        m_nat = _dot(swap, mg, NN)
        mo = _dot(m_nat.astype(CDT), w_ref[...], NN)
        gate, fg = gate_ref[...], fg_ref[...]
        x2 = x_ref[...] + gate * mo
        r2 = lax.rsqrt(jnp.mean(x2 * x2, axis=-1, keepdims=True) + EPS)
        xn2 = x2 * r2
        diff = xn2 * fg - t_ref[...]
        loss = jnp.sum(jnp.sum(diff * diff, axis=-1, keepdims=True), axis=0, keepdims=True)
        dy = diff * (1.0 / D)
        dxn2 = dy * fg
        dx2 = r2 * (dxn2 - xn2 * jnp.mean(dxn2 * xn2, axis=-1, keepdims=True))
        dx2_ref[...] = dx2
        dmo = (dx2 * gate).astype(CDT)
        _st(dmo_ref, _dot(swap, dmo, NN))
        loss_ref[...] += jnp.broadcast_to(loss * (0.5 / D), loss_ref.shape)
        gfg_ref[...] += jnp.sum(dy * xn2, axis=0, keepdims=True)
        dgate_ref[...] += jnp.sum(dx2 * mo, axis=0, keepdims=True)
        dw_ref[...] += _dot(m_nat.T.astype(CDT), dmo, NN)

    def slab(w, cblk):
        return pl.BlockSpec((R16, R16, w), lambda i: (0, i, cblk))

    nat = pl.BlockSpec((TOK, D), lambda i: (i, 0))
    vec = pl.BlockSpec((1, D), lambda i: (0, 0))
    r3 = rest.reshape(R16, LS, REST_W)
    return pl.pallas_call(
        body, name="merge_out_loss", grid=(S // TOK,),
        in_specs=[slab(AW, 0), slab(AW, 0), slab(D, 1), slab(D, 2), _resident((AW, D)), _resident((AW, D)),
                  _resident((D, D)), nat, nat, vec, vec],
        out_specs=[nat, slab(D, 0), pl.BlockSpec((1, 128), lambda i: (0, 0)), vec, vec,
                   pl.BlockSpec((D, D), lambda i: (0, 0))],
        out_shape=[_sds((S, D), F32), _sds((R16, LS, D), CDT), _sds((1, 128), F32), _sds((1, D), F32),
                   _sds((1, D), F32), _sds((D, D), F32)],
        compiler_params=_cp(("arbitrary",)),
    )(ag.reshape(R16, LS, AW), pg.reshape(R16, LS, AW), r3, r3, wab, wpb, wout, x, target, gate, final_g)


def _out_loss(merged, wout, x, target, gate, final_g):
    S = x.shape[0]

    def body(m_ref, w_ref, x_ref, t_ref, gate_ref, fg_ref, dx2_ref, dmo_ref, loss_ref, gfg_ref, dgate_ref):
        i = pl.program_id(0)

        @pl.when(i == 0)
        def _():
            loss_ref[...] = jnp.zeros_like(loss_ref)
            gfg_ref[...] = jnp.zeros_like(gfg_ref)
            dgate_ref[...] = jnp.zeros_like(dgate_ref)

        swap = _swap16()
        mo = _dot(_dot(swap, _ld(m_ref), NN).astype(CDT), w_ref[...], NN)
        gate, fg = gate_ref[...], fg_ref[...]
        x2 = x_ref[...] + gate * mo
        r2 = lax.rsqrt(jnp.mean(x2 * x2, axis=-1, keepdims=True) + EPS)
        xn2 = x2 * r2
        diff = xn2 * fg - t_ref[...]
        loss = jnp.sum(jnp.sum(diff * diff, axis=-1, keepdims=True), axis=0, keepdims=True)
        dy = diff * (1.0 / D)
        dxn2 = dy * fg
        dx2 = r2 * (dxn2 - xn2 * jnp.mean(dxn2 * xn2, axis=-1, keepdims=True))
        dx2_ref[...] = dx2
        _st(dmo_ref, _dot(swap, (dx2 * gate).astype(CDT), NN))
        loss_ref[...] += jnp.broadcast_to(loss * (0.5 / D), loss_ref.shape)
        gfg_ref[...] += jnp.sum(dy * xn2, axis=0, keepdims=True)
        dgate_ref[...] += jnp.sum(dx2 * mo, axis=0, keepdims=True)

    slab = pl.BlockSpec((R16, R16, D), lambda i: (0, i, 0))
    nat = pl.BlockSpec((TOK, D), lambda i: (i, 0))
    vec = pl.BlockSpec((1, D), lambda i: (0, 0))
    return pl.pallas_call(
        body, name="out_loss", grid=(S // TOK,),
        in_specs=[slab, _resident((D, D)), nat, nat, vec, vec],
        out_specs=[nat, slab, pl.BlockSpec((1, 128), lambda i: (0, 0)), vec, vec],
        out_shape=[_sds((S, D), F32), _sds((R16, S // R16, D), CDT), _sds((1, 128), F32), _sds((1, D), F32),
                   _sds((1, D), F32)],
        compiler_params=_cp(("arbitrary",)),
    )(merged, wout, x, target, gate, final_g)


def _bwd_gates(dmo, wout, wab, wpb, rest, ag, pg, attn, mp, pool_scale):
    S = dmo.shape[0]
    tm = 256

    def body(dmo_ref, wo_ref, wa_ref, wp_ref, za_ref, zp_ref, ga0, ga1, gp0, gp1, ag_ref, pg_ref, at_ref, mp_ref,
             ps_ref, dwa_ref, dwp_ref, dr_ref, dat_ref, dl_ref, dmp_ref, dps_ref):
        i = pl.program_id(0)

        @pl.when(i == 0)
        def _():
            dps_ref[...] = jnp.zeros_like(dps_ref)
            dwa_ref[...] = jnp.zeros_like(dwa_ref)
            dwp_ref[...] = jnp.zeros_like(dwp_ref)

        dm = _dot(dmo_ref[...], wo_ref[...], NT)
        sa = _sigmoid(jnp.concatenate([ga0[...], ga1[...]], axis=1))
        sp = _sigmoid(jnp.concatenate([gp0[...], gp1[...]], axis=1))
        dya = (dm * sa).astype(CDT)
        dyp = (dm * sp).astype(CDT)
        ag, pg = ag_ref[...], pg_ref[...]
        dwa_ref[...] += _dot(ag.astype(F32).T.astype(CDT), dya, NN)
        dwp_ref[...] += _dot(pg.astype(F32).T.astype(CDT), dyp, NN)
        ya = _dot(ag, wa_ref[...], NN)
        yp = _dot(pg, wp_ref[...], NN)
        dr_ref[:, 1024:2048] = (dm * ya * sa * (1.0 - sa)).astype(dr_ref.dtype)
        dr_ref[:, 2048:3072] = (dm * yp * sp * (1.0 - sp)).astype(dr_ref.dtype)
        da = _dot(dya, wa_ref[...], NT)
        dp = _dot(dyp, wp_ref[...], NT)
        za = za_ref[...]
        sga = _sigmoid(za)
        attn = at_ref[...]
        dattn = da * (za * sga)
        dr_ref[:, 0:512] = (da * attn * (sga * (1.0 + za * (1.0 - sga)))).astype(dr_ref.dtype)
        dat_ref[...] = dattn.astype(dat_ref.dtype)
        prod = dattn * attn
        low = lax.broadcasted_iota(jnp.int32, (1, 2 * HD), 1) < HD
        for j in range(NH // 2):
            x2 = prod[:, 2 * HD * j:2 * HD * (j + 1)]
            s0 = jnp.sum(jnp.where(low, x2, 0.0), axis=1, keepdims=True)
            s1 = jnp.sum(jnp.where(low, 0.0, x2), axis=1, keepdims=True)
            dl_ref[:, 2 * HD * j:2 * HD * (j + 1)] = jnp.where(low, s0, s1)
        zp = zp_ref[...]
        sgp = _sigmoid(zp)
        mpre = mp_ref[...]
        ps = ps_ref[...]
        dmixed = dp * (zp * sgp)
        dr_ref[:, 512:1024] = (dp * (mpre * ps) * (sgp * (1.0 + zp * (1.0 - sgp)))).astype(dr_ref.dtype)
        dps_ref[...] += jnp.sum(dmixed * mpre, axis=0, keepdims=True)
        dmp_ref[...] = (dmixed * ps).astype(dmp_ref.dtype)

    full = lambda shape: pl.BlockSpec(shape, lambda i: (0, 0))
    tD = pl.BlockSpec((tm, D), lambda i: (i, 0))
    tA = pl.BlockSpec((tm, AW), lambda i: (i, 0))
    rb = lambda c: pl.BlockSpec((tm, AW), lambda i: (i, c))
    return pl.pallas_call(
        body, name="bwd_gates", grid=(S // tm,),
        in_specs=[tD, full((D, D)), full((AW, D)), full((AW, D)), rb(0), rb(1), rb(2), rb(3), rb(4), rb(5),
                  tA, tA, tA, tA, full((1, AW))],
        out_specs=[full((AW, D)), full((AW, D)), pl.BlockSpec((tm, 3072), lambda i: (i, 0)), tA, tA, tA, full((1, AW))],
        out_shape=[_sds((AW, D), F32), _sds((AW, D), F32), _sds((S, REST_W), CDT), _sds((S, AW), CDT),
                   _sds((S, AW), F32), _sds((S, AW), CDT), _sds((1, AW), F32)],
        compiler_params=_cp(("arbitrary",), vmem=56),
    )(dmo, wout, wab, wpb, rest, rest, rest, rest, rest, rest, ag, pg, attn, mp, pool_scale)


def _pool_bwd(dmp, pt, pool_w, drest):
    S = dmp.shape[0]
    LS = S // R16
    PG = 128
    d3 = dmp.reshape(R16, LS, AW)
    dr3 = drest.reshape(R16, LS, REST_W)

    def body(d_ref, pt_ref, pw_ref, dr_in, du_ref, dpw_ref, g_ref, s_ref):
        del dr_in
        g = pl.program_id(0)
        win = jnp.left_shift(2, g)
        row = lax.broadcasted_iota(jnp.int32, (LS, PG), 0)
        pw = pw_ref[...].astype(CDT)
        dpw = jnp.zeros((PG, PG), F32)
        for r in range(R16):
            dm = d_ref[r]
            dpw = dpw + _dot(pt_ref[:, r * LS:(r + 1) * LS], dm, NN)
            dpooled = _dot(dm, pw, NT)
            g_ref[r] = dpooled
            sc = dpooled / _pool_counts(r, LS, win)
            s_ref[r] = sc
            if r < R16 - 1:
                s_ref[R16 + r] = jnp.where(row == LS - 1, 0.0, pltpu.roll(sc, LS - 1, 0))
        dpw_ref[...] = dpw

        def double(s):
            for i in range(0, 2 * R16 - 2 * s):
                s_ref[i] = s_ref[i] + s_ref[i + s]

        double(1)
        for k in (1, 2, 3):
            pl.when(g >= k)(functools.partial(double, 2 ** k))
        for r in range(R16):
            du_ref[r] = (s_ref[r] - g_ref[r]).astype(du_ref.dtype)

    du, dpw = pl.pallas_call(
        body, name="pool_bwd", grid=(4,),
        in_specs=[pl.BlockSpec((R16, LS, PG), lambda g: (0, 0, g)), pl.BlockSpec((PG, S), lambda g: (g, 0)),
                  pl.BlockSpec((None, PG, PG), lambda g: (g, 0, 0)), pl.BlockSpec(memory_space=pl.ANY)],
        out_specs=[pl.BlockSpec((R16, LS, PG), lambda g: (0, 0, 24 + g)),
                   pl.BlockSpec((None, PG, PG), lambda g: (g, 0, 0))],
        out_shape=[_sds((R16, LS, REST_W), CDT), _sds((4, PG, PG), F32)],
        scratch_shapes=[pltpu.VMEM((R16, LS, PG), F32), pltpu.VMEM((2 * R16, LS, PG), F32)],
        input_output_aliases={3: 0},
        compiler_params=_cp(("parallel",)),
    )(d3, pt, pool_w, dr3)
    return du.reshape(S, REST_W), dpw


def _attn_bwd(qkv, dattn, Lb, dlb, biases, geom, dqkv_prev):
    S = qkv.shape[0]
    gi, T, R, has_prev, N = geom.gi, geom.T, geom.rows, geom.has_prev, geom.n_tiles
    H2 = T // 2
    nd_outer = len(geom.outer)
    qv = geom.view(qkv)
    cq, ck, cv = 3 * gi, 3 * gi + 1, 3 * gi + 2
    W3 = 3 * AW

    def heads(q_ref, kp_ref, kc_ref, vp_ref, vc_ref, do_ref, L_ref, dl_ref, ba_ref, bb_ref, pm, dsa_ref, dsb_ref, emit):
        q_lo, q_hi = _split_rows(_ld(q_ref), R)
        do_lo, do_hi = _split_rows(_ld(do_ref), R)
        L_lo, L_hi = _split_rows(_ld(L_ref), R)
        dl_lo, dl_hi = _split_rows(_ld(dl_ref), R)
        kc_lo, kc_hi = _split_rows(_ld(kc_ref), R)
        vc_lo, vc_hi = _split_rows(_ld(vc_ref), R)
        k_b = jnp.concatenate([kc_lo, kc_hi], axis=0)
        v_b = jnp.concatenate([vc_lo, vc_hi], axis=0)
        if has_prev:
            k_a = jnp.concatenate([_split_rows(_ld(kp_ref), R)[1], kc_lo], axis=0)
            v_a = jnp.concatenate([_split_rows(_ld(vp_ref), R)[1], vc_lo], axis=0)
        else:
            k_a, v_a = k_b, v_b
        low = lax.broadcasted_iota(jnp.int32, (1, 2 * HD), 1) < HD
        for j in range(NH // 2):
            sl = slice(2 * HD * j, 2 * HD * (j + 1))
            res = []
            for q, k, v, do, L, dl, b_ref, mask, ds_ref in (
                    (q_lo, k_a, v_a, do_lo, L_lo, dl_lo, ba_ref, pm, dsa_ref),
                    (q_hi, k_b, v_b, do_hi, L_hi, dl_hi, bb_ref, None, dsb_ref)):
                q2 = q[:, sl].astype(F32) * 0.125
                do2 = do[:, sl].astype(F32)
                k2, v2 = k[:, sl], v[:, sl]
                L2, dl2 = L[:, sl], dl[:, sl]
                qs = jnp.concatenate([jnp.where(low, q2, 0.0), jnp.where(low, 0.0, q2)], axis=0).astype(q.dtype)
                dom = jnp.concatenate([jnp.where(low, do2, 0.0), jnp.where(low, 0.0, do2)], axis=0).astype(q.dtype)
                Lc = jnp.concatenate([L2[:, 0:1], L2[:, HD:HD + 1]], axis=0)
                dlc = jnp.concatenate([dl2[:, 0:1], dl2[:, HD:HD + 1]], axis=0)
                s = _dot(qs, k2, NT) + b_ref[j]
                if mask is not None:
                    s = s + mask
                p = jnp.exp(s - Lc)
                ds = p * (_dot(dom, v2, NT) - dlc)
                ds_ref[j] += ds
                dsc = ds.astype(q.dtype)
                dq_st = _dot(dsc, k2, NN)
                res.append((jnp.where(low, dq_st[:H2], dq_st[H2:]) * 0.125, _dot(dsc, qs, TN),
                            _dot(p.astype(q.dtype), dom, TN)))
            (dq_lo, dk_a, dv_a), (dq_hi, dk_b, dv_b) = res
            dq = _merge_rows(dq_lo, dq_hi, R)
            if has_prev:
                emit(sl, dq, _merge_rows(dk_a[H2:] + dk_b[:H2], dk_b[H2:], R),
                     _merge_rows(dv_a[H2:] + dv_b[:H2], dv_b[H2:], R), dk_a[:H2], dv_a[:H2])
            else:
                emit(sl, dq, _merge_rows(dk_a[:H2] + dk_b[:H2], dk_a[H2:] + dk_b[H2:], R),
                     _merge_rows(dv_a[:H2] + dv_b[:H2], dv_a[H2:] + dv_b[H2:], R), None, None)

    def first_step():
        ok = pl.program_id(nd_outer) == 0 if has_prev else pl.program_id(0) == 0
        for a in range(nd_outer if has_prev else 0):
            ok = ok & (pl.program_id(a) == 0)
        return ok

    if has_prev:
        def body(q_ref, kp_ref, kc_ref, vp_ref, vc_ref, do_ref, L_ref, dl_ref, ba_ref, bb_ref, prev_in, out_ref,
                 dsa_ref, dsb_ref, hold, new, pp):
            del prev_in
            s_id = pl.program_id(nd_outer)

            @pl.when(first_step())
            def _():
                dsa_ref[...] = jnp.zeros_like(dsa_ref)
                dsb_ref[...] = jnp.zeros_like(dsb_ref)

            @pl.when(s_id < N)
            def _():
                def emit(sl, dq, dk, dv, dk_prev, dv_prev):
                    new[:, sl] = dq
                    new[:, slice(AW + sl.start, AW + sl.stop)] = dk
                    new[:, slice(2 * AW + sl.start, 2 * AW + sl.stop)] = dv
                    pp[:, sl] = dk_prev
                    pp[:, slice(AW + sl.start, AW + sl.stop)] = dv_prev

                heads(q_ref, kp_ref, kc_ref, vp_ref, vc_ref, do_ref, L_ref, dl_ref, ba_ref, bb_ref,
                      _prev_mask(s_id, H2, T), dsa_ref, dsb_ref, emit)

            @pl.when((s_id >= 1) & (s_id < N))
            def _():
                _st(out_ref, hold[:, 0:AW], slice(0, AW))
                late = _merge_rows(jnp.zeros((H2, 2 * AW), F32), pp[...], R)
                _st(out_ref, hold[:, AW:W3] + late, slice(AW, W3))

            @pl.when(s_id == N)
            def _():
                _st(out_ref, hold[...])

            @pl.when(s_id < N)
            def _():
                hold[...] = new[...]

        cur = lambda n: jnp.minimum(n, N - 1)
        prev = lambda n: jnp.clip(n - 1, 0, N - 1)
        delayed = lambda n: jnp.maximum(n - 1, 0)
        in_specs = [geom.spec(AW, cq, cur), geom.spec(AW, ck, prev), geom.spec(AW, ck, cur),
                    geom.spec(AW, cv, prev), geom.spec(AW, cv, cur),
                    geom.spec(AW, 0, cur), geom.spec(AW, 0, cur), geom.spec(AW, 0, cur)]
        args = [qv] * 5 + [geom.view(dattn), geom.view(Lb), geom.view(dlb)]
        out_spec = geom.spec(W3, gi, delayed)
        grid = geom.outer + (N + 1,)
        scratch = [pltpu.VMEM((T, W3), F32), pltpu.VMEM((T, W3), F32), pltpu.VMEM((H2, 2 * AW), F32)]
    else:
        def body(q_ref, kc_ref, vc_ref, do_ref, L_ref, dl_ref, ba_ref, bb_ref, prev_in, out_ref, dsa_ref, dsb_ref):
            del prev_in

            @pl.when(first_step())
            def _():
                dsa_ref[...] = jnp.zeros_like(dsa_ref)
                dsb_ref[...] = jnp.zeros_like(dsb_ref)

            def emit(sl, dq, dk, dv, dk_prev, dv_prev):
                _st(out_ref, dq, sl)
                _st(out_ref, dk, slice(AW + sl.start, AW + sl.stop))
                _st(out_ref, dv, slice(2 * AW + sl.start, 2 * AW + sl.stop))

            heads(q_ref, None, kc_ref, None, vc_ref, do_ref, L_ref, dl_ref, ba_ref, bb_ref, None, dsa_ref, dsb_ref,
                  emit)

        cur = lambda n: n
        in_specs = [geom.spec(AW, cq, cur), geom.spec(AW, ck, cur), geom.spec(AW, cv, cur),
                    geom.spec(AW, 0, cur), geom.spec(AW, 0, cur), geom.spec(AW, 0, cur)]
        args = [qv] * 3 + [geom.view(dattn), geom.view(Lb), geom.view(dlb)]
        out_spec = geom.spec(W3, gi, cur)
        grid = geom.outer
        scratch = []

    whole = lambda b: pl.BlockSpec(b.shape, lambda *ids: (0, 0, 0))
    biases = [b.reshape(NH // 2, 2 * H2, b.shape[2]) for b in biases]
    in_specs += [whole(b) for b in biases]
    in_specs.append(pl.BlockSpec(memory_space=pl.ANY))
    if dqkv_prev is None:
        last, aliases = biases[0], {}
    else:
        last, aliases = geom.view(dqkv_prev), {len(in_specs) - 1: 0}
    dq, dsa, dsb = pl.pallas_call(
        body, name=f"attn_bwd{gi}", grid=grid, in_specs=in_specs,
        out_specs=[out_spec] + [whole(b) for b in biases],
        out_shape=[_sds(geom.view_shape(QKV_W), CDT)] + [_sds(b.shape, F32) for b in biases],
        scratch_shapes=scratch,
        input_output_aliases=aliases,
        compiler_params=_cp(("arbitrary",) * len(grid), vmem=56),
    )(*args, *biases, last)
    return geom.unview(dq), [d.reshape(NH, H2, d.shape[2]) for d in (dsa, dsb)]


def _h_bwd(dh, dx2, x, norm_g, scale):
    S = x.shape[0]

    def body(dh_ref, dx2_ref, x_ref, g_ref, sc_ref, gx_ref, dsh_ref, dsc_ref, dng_ref, nat_ref):
        i = pl.program_id(0)

        @pl.when(i == 0)
        def _():
            dsh_ref[...] = jnp.zeros_like(dsh_ref)
            dsc_ref[...] = jnp.zeros_like(dsc_ref)
            dng_ref[...] = jnp.zeros_like(dng_ref)

        for r in range(R16):
            nat_ref[:, r, :] = dh_ref[r]
        dh = jnp.concatenate([nat_ref[l] for l in range(R16)], axis=0)
        g = g_ref[...]
        one_sc = 1.0 + sc_ref[...]
        xv = x_ref[...]
        r1 = lax.rsqrt(jnp.mean(xv * xv, axis=-1, keepdims=True) + EPS)
        xn = xv * r1
        dhx = dh * xn
        dxn = dh * (g * one_sc)
        dx1 = r1 * (dxn - xn * jnp.mean(dxn * xn, axis=-1, keepdims=True))
        gx_ref[...] = dx2_ref[...] + dx1
        dsh_ref[...] += jnp.sum(dh, axis=0, keepdims=True)
        dsc_ref[...] += jnp.sum(dhx * g, axis=0, keepdims=True)
        dng_ref[...] += jnp.sum(dhx * one_sc, axis=0, keepdims=True)

    nat = pl.BlockSpec((TOK, D), lambda i: (i, 0))
    vec = pl.BlockSpec((1, D), lambda i: (0, 0))
    return pl.pallas_call(
        body, name="h_bwd", grid=(S // TOK,),
        in_specs=[pl.BlockSpec((R16, R16, D), lambda i: (0, i, 0)), nat, nat, vec, vec],
        out_specs=[nat, vec, vec, vec],
        out_shape=[_sds((S, D), F32), _sds((1, D), F32), _sds((1, D), F32), _sds((1, D), F32)],
        scratch_shapes=[pltpu.VMEM((R16, R16, D), F32)],
        compiler_params=_cp(("arbitrary",)),
    )(dh, dx2, x, norm_g, scale)


def _local_step(x, target, shift, scale, gate, norm_g, w_in_full, pool_w, pool_scale, wab, wpb, wout, rel_bias,
                final_g):
    st = _local_bias_tables(_local_pre(x, shift, scale, norm_g), rel_bias)
    st = _local_grads(st, target, gate, w_in_full, pool_w, pool_scale, wab, wpb, wout, final_g)
    d_rel_bias = _local_bias_grads(st, _BIAS_GRADS_EARLY + _BIAS_GRADS_LATE)
    gx, dmod, dng = _local_input_grads(st, w_in_full, norm_g, scale)
    return dict(loss=st["loss"], grad_x=gx, dw_in=st["dw_in"], dw_ab=st["dw_ab"], dw_pb=st["dw_pb"],
                dw_out=st["dw_out"], d_norm_g=dng, d_pool_w=st["dpw"], d_pool_scale=st["dps"],
                d_rel_bias=d_rel_bias.T, d_final_g=st["gfg"], dmod=dmod)


def _local_pre(x, shift, scale, norm_g, dep=None):
    S = x.shape[0]
    h = _norm_mod(x, norm_g, shift, scale, dep).reshape(S, D)
    return dict(x=x, h=h, ht=_tcast(h, "h_transpose"))


def _local_bias_tables(st, rel_bias, dep=None):
    LS = st["x"].shape[0] // R16
    geoms = [_Geom(gi, LS) for gi in range(3)]
    idx_np = [[g.bucket_index(sub) for sub in range(2)] for g in geoms]
    buckets = [[tuple(int(b) for b in np.unique(ix) if b >= 0) for ix in pair] for pair in idx_np]
    idxs = [[jnp.asarray(ix) for ix in pair] for pair in idx_np]
    biases = [[_bias_table(idxs[gi][sub], buckets[gi][sub], rel_bias, gi, sub, dep) for sub in range(2)]
              for gi in range(3)]
    return dict(st, geoms=geoms, idxs=idxs, buckets=buckets, biases=biases)


def _local_grads(st, target, gate, w_in_full, pool_w, pool_scale, wab, wpb, wout, final_g):
    geoms, idxs, biases, x, h, ht = (st[k] for k in ("geoms", "idxs", "biases", "x", "h", "ht"))
    S = h.shape[0]
    LS = S // R16
    qkv, rest = _proj(h, w_in_full)

    os_, ls_ = [], []
    for gi in range(2):
        o, l = _attn_fwd(qkv, biases[gi], geoms[gi])
        os_.append(o)
        ls_.append(l)
    attn, Lb, ag = _attn_fwd(qkv, biases[2], geoms[2], others=(os_, ls_, rest))
    pt, mp, pg = _pool_fwd(rest, pool_w, pool_scale)
    dx2, dmo3, loss, gfg, dgate, dw_out = _merge_out_loss(ag, pg, rest, wab, wpb, wout, x, target, gate, final_g)
    dmo = dmo3.reshape(S, D)

    dw_ab, dw_pb, drest, dattn, dlb, dmp, dps = _bwd_gates(dmo, wout, wab, wpb, rest, ag, pg, attn, mp, pool_scale)
    drest, dpw = _pool_bwd(dmp, pt, pool_w, drest)

    dqkv = None
    dsums = []
    for gi in range(3):
        dqkv, dsum = _attn_bwd(qkv, dattn, Lb, dlb, biases[gi], geoms[gi], dqkv)
        dsums.append(dsum)

    dw_in = _mm_tn(ht, dqkv, n_total=8192, out_blk=lambda j: j, name="dw_in_qkv", with_low=True)
    dw_in, dw_in_low = _mm_tn(ht, drest, n_total=8192, out_blk=_rest_blk, name="dw_in_rest", prev=dw_in, with_low=True)
    return dict(st, loss=loss[0, 0], dw_in=dw_in, dw_in_low=dw_in_low, dw_ab=dw_ab, dw_pb=dw_pb, dw_out=dw_out, dpw=dpw, dps=dps, gfg=gfg,
                dgate=dgate, dsums=dsums, dqkv=dqkv, drest=drest, dx2=dx2)


def _local_bias_grads(st, which, dep=None):
    rows = []
    for gi in range(3):
        got = [_bias_grad(st["dsums"][gi][sub], st["idxs"][gi][sub], st["buckets"][gi][sub], gi, sub, dep)
               for sub in range(2) if (gi, sub) in which]
        rows.append(sum(got) if got else jnp.zeros((NH, 32), F32))
    return jnp.concatenate(rows, axis=0)


_BIAS_GRADS_EARLY = ((0, 0), (0, 1), (1, 0), (2, 0))
_BIAS_GRADS_LATE = ((1, 1), (2, 1))


def _dh_h_bwd(dqkv, drest, w, dx2, x, norm_g, scale, dep=None):
    S = x.shape[0]
    dep_specs, dep_args = _after(dep)

    def body(a_ref, b_ref, w_ref, dx2_ref, x_ref, g_ref, sc_ref, *rest):
        gx_ref, dsh_ref, dsc_ref, dng_ref, nat_ref = rest[-5:]
        i = pl.program_id(0)

        @pl.when(i == 0)
        def _():
            dsh_ref[...] = jnp.zeros_like(dsh_ref)
            dsc_ref[...] = jnp.zeros_like(dsc_ref)
            dng_ref[...] = jnp.zeros_like(dng_ref)

        a, b = _ld(a_ref), _ld(b_ref)
        acc = _dot(a[:, 0:512], _w512(w_ref, 0), NT)
        for j in range(1, 9):
            acc = acc + _dot(a[:, 512 * j:512 * (j + 1)], _w512(w_ref, j), NT)
        off = 0
        for lo, hi in REST_COLS:
            for j in range(lo, hi):
                acc = acc + _dot(b[:, off:off + 512], _w512(w_ref, j), NT)
                off += 512
        for r in range(R16):
            nat_ref[:, r, :] = acc[R16 * r:R16 * (r + 1)]
        dh = jnp.concatenate([nat_ref[l] for l in range(R16)], axis=0)
        g = g_ref[...]
        one_sc = 1.0 + sc_ref[...]
        xv = x_ref[...]
        r1 = lax.rsqrt(jnp.mean(xv * xv, axis=-1, keepdims=True) + EPS)
        xn = xv * r1
        dhx = dh * xn
        dxn = dh * (g * one_sc)
        dx1 = r1 * (dxn - xn * jnp.mean(dxn * xn, axis=-1, keepdims=True))
        gx_ref[...] = dx2_ref[...] + dx1
        dsh_ref[...] += jnp.sum(dh, axis=0, keepdims=True)
        dsc_ref[...] += jnp.sum(dhx * g, axis=0, keepdims=True)
        dng_ref[...] += jnp.sum(dhx * one_sc, axis=0, keepdims=True)

    nat = pl.BlockSpec((TOK, D), lambda i: (i, 0))
    vec = pl.BlockSpec((1, D), lambda i: (0, 0))
    slab = lambda W: pl.BlockSpec((R16, R16, W), lambda i: (0, i, 0))
    return pl.pallas_call(
        body, name="dh_h_bwd", grid=(S // TOK,),
        in_specs=[slab(QKV_W), slab(REST_W), _resident((8, D, 1024)), nat, nat, vec, vec] + dep_specs,
        out_specs=[nat, vec, vec, vec],
        out_shape=[_sds((S, D), F32), _sds((1, D), F32), _sds((1, D), F32), _sds((1, D), F32)],
        scratch_shapes=[pltpu.VMEM((R16, R16, D), F32)],
        compiler_params=_cp(("arbitrary",), vmem=56),
    )(dqkv, drest, w, dx2, x, norm_g, scale, *dep_args)


def _local_input_grads(st, w_in_full, norm_g, scale, dep=None):
    S = st["h"].shape[0]
    LS = S // R16
    gx, dsh, dsc, dng = _dh_h_bwd(st["dqkv"].reshape(R16, LS, QKV_W), st["drest"].reshape(R16, LS, REST_W),
                                  w_in_full, st["dx2"], st["x"], norm_g, scale, dep)
    return gx, jnp.concatenate([dsh, dsc, st["dgate"]], axis=1), dng


def _my_place():
    return lax.axis_index("x"), lax.axis_index("y"), lax.axis_index("c")


def _block_of(ref, axis, idx, width):
    if axis is None:
        return ref.at[idx]
    start = pl.multiple_of(idx * width, width)
    if axis == 0:
        return ref.at[pl.ds(start, width)]
    return ref.at[:, pl.ds(start, width)]


_SEM = pl.BlockSpec(memory_space=pltpu.SEMAPHORE)
_EFFECT = pltpu.SideEffectType.DATAFLOW_SIDE_EFFECTING


def _split_start(name, srcs, lands, ncopy, plan, dep=None):
    arrs = list(srcs) + list(lands)
    na, ns = len(arrs), len(srcs)
    dep_specs, dep_args = _after(dep)

    def body(*refs):
        send_sems, recv_sems, token = refs[na + len(dep_args)], refs[na + len(dep_args) + 1], refs[-1]
        for k, (s, d, dev) in enumerate(plan(refs[:ns], refs[ns:na])):
            pltpu.make_async_remote_copy(src_ref=s, dst_ref=d, send_sem=send_sems.at[k], recv_sem=recv_sems.at[k],
                                         device_id=dev, device_id_type=MESH).start()
        token[...] = jnp.zeros_like(token)

    outs = pl.pallas_call(
        body, name=name,
        out_shape=(pltpu.SemaphoreType.DMA((ncopy,)), pltpu.SemaphoreType.DMA((ncopy,)),
                   *[pltpu.HBM(a.shape, a.dtype) for a in arrs], _sds((8, 128), F32)),
        in_specs=[_HBM] * na + dep_specs,
        out_specs=(_SEM, _SEM, *[_HBM] * na, pl.BlockSpec(memory_space=pltpu.VMEM)),
        input_output_aliases={i: 2 + i for i in range(na)},
        compiler_params=pltpu.CompilerParams(has_side_effects=_EFFECT),
    )(*[pltpu.with_memory_space_constraint(a, pltpu.HBM) for a in arrs], *dep_args)
    return (outs[0], outs[1]), list(outs[2:2 + ns]), list(outs[2 + ns:2 + na]), outs[-1]


def _split_wait(name, sems, srcs, lands, plan, after):
    arrs = list(srcs) + list(lands)
    na, ns = len(arrs), len(srcs)
    after = list(after) if isinstance(after, (list, tuple)) else [after]

    def body(*refs):
        send_sems, recv_sems = refs[na], refs[na + 1]
        for k, (s, d, dev) in enumerate(plan(refs[:ns], refs[ns:na])):
            cp = pltpu.make_async_remote_copy(src_ref=s, dst_ref=d, send_sem=send_sems.at[k], recv_sem=recv_sems.at[k],
                                              device_id=dev, device_id_type=MESH)
            cp.wait_send()
            cp.wait_recv()

    outs = pl.pallas_call(
        body, name=name, out_shape=[pltpu.HBM(a.shape, a.dtype) for a in arrs],
        in_specs=[_HBM] * na + [_SEM, _SEM] + [pl.BlockSpec(memory_space=pl.ANY)] * len(after),
        out_specs=[_HBM] * na,
        input_output_aliases={i: i for i in range(na)},
        compiler_params=pltpu.CompilerParams(has_side_effects=_EFFECT),
    )(*arrs, sems[0], sems[1], *after)
    return list(outs[:ns]), list(outs[ns:])


def _gather_chips_plan(layout):
    def plan(src, land):
        x, y, c = _my_place()
        me = 4 * x + 2 * y + c
        mine = [_block_of(land[p], layout[p][0], me, layout[p][1]) for p in range(len(land))]
        return [(mine[p], mine[p], (cx, cy, c)) for p in range(len(land)) for cx, cy in [(1 - x, y), (x, 1 - y)]]
    return plan


def _pair_plan(layout):
    def plan(src, land):
        x, y, c = _my_place()
        n = len(layout)
        return [(_block_of(src[p], layout[p][0], 2 * q + (1 - c), layout[p][1]), land[p].at[q], (x, y, 1 - c))
                for p in range(n) for q in range(4)] + [(src[p], land[p], (x, y, 1 - c)) for p in range(n, len(src))]
    return plan


def _chips_plan(n):
    def plan(src, land):
        x, y, c = _my_place()
        chips = [(1 - x, y), (x, 1 - y), (1 - x, 1 - y)]
        return [(src[p].at[2 * cx + cy], land[p].at[2 * x + y], (cx, cy, c)) for p in range(n) for cx, cy in chips] + \
               [(src[p], land[p].at[2 * x + y], (cx, cy, c)) for p in range(n, len(src)) for cx, cy in chips]
    return plan


def _pair_share(gathered):
    n = len(gathered)

    def body(*refs):
        land = refs[n:2 * n]
        send_sems, recv_sems, fwd_send, fwd_recv = refs[2 * n:]
        x, y, c = _my_place()
        k_dg = 3 - (2 * x + y)
        south = c == 0
        pass_on = 4 * jnp.where(south, 1 - x, x) + 2 * jnp.where(south, y, 1 - y) + c
        to = (jnp.where(south, x, 1 - x), jnp.where(south, 1 - y, y), c)

        def forward(p):
            return pltpu.make_async_remote_copy(
                src_ref=land[p].at[pass_on], dst_ref=land[p].at[pass_on], send_sem=fwd_send.at[p],
                recv_sem=fwd_recv.at[p], device_id=to, device_id_type=MESH)

        def to_sibling(p, k):
            blk = land[p].at[2 * k + c]
            return pltpu.make_async_remote_copy(
                src_ref=blk, dst_ref=blk, send_sem=send_sems.at[4 * p + k], recv_sem=recv_sems.at[4 * p + k],
                device_id=(x, y, 1 - c), device_id_type=MESH)

        for p in range(n):
            forward(p).start()
        for p in range(n):
            for k in range(4):
                @pl.when(k != k_dg)
                def _():
                    to_sibling(p, k).start()
        for p in range(n):
            dg = land[p].at[2 * k_dg + c]
            pltpu.make_async_remote_copy(src_ref=dg, dst_ref=dg, send_sem=fwd_send.at[p], recv_sem=fwd_recv.at[p],
                                         device_id=to, device_id_type=MESH).wait_recv()
            for k in range(4):
                @pl.when(k == k_dg)
                def _():
                    to_sibling(p, k).start()
        for p in range(n):
            forward(p).wait_send()
            for k in range(4):
                to_sibling(p, k).wait_send()
                blk = land[p].at[2 * k + 1 - c]
                pltpu.make_async_remote_copy(
                    src_ref=blk, dst_ref=blk, send_sem=send_sems.at[4 * p + k], recv_sem=recv_sems.at[4 * p + k],
                    device_id=(x, y, 1 - c), device_id_type=MESH).wait_recv()

    return pl.pallas_call(
        body, name="gather_pair_share", in_specs=[_HBM] * n, out_specs=[_HBM] * n,
        out_shape=[_sds(g.shape, g.dtype) for g in gathered],
        input_output_aliases={p: p for p in range(n)},
        scratch_shapes=[pltpu.SemaphoreType.DMA((4 * n,)), pltpu.SemaphoreType.DMA((4 * n,)),
                        pltpu.SemaphoreType.DMA((n,)), pltpu.SemaphoreType.DMA((n,))],
    )(*gathered)


def _place_slot(arrs, slot, n_slots, name, out_dtype=None):
    n = len(arrs)

    def body(s_ref, *refs):
        for p in range(n):
            refs[n + p][...] = refs[p][...].astype(refs[n + p].dtype)

    def spec(a):
        tail = a.shape[-2:]
        return pl.BlockSpec((None,) + tail, lambda i, s_ref: (s_ref[0], 0, 0))

    in_specs = [spec(a) if a.ndim == 3 else pl.BlockSpec(a.shape, lambda i, s_ref: (0, 0)) for a in arrs]
    return pl.pallas_call(
        body, name=name,
        grid_spec=pltpu.PrefetchScalarGridSpec(num_scalar_prefetch=1, grid=(1,), in_specs=in_specs,
                                               out_specs=[spec(a) for a in arrs]),
        out_shape=[_in_hbm((n_slots,) + a.shape[-2:], out_dtype or a.dtype) for a in arrs],
        compiler_params=_cp(("arbitrary",)),
    )(slot, *[_pin(a) for a in arrs])


def _cols_from_blocks(w8, name):
    _, R, C = w8.shape

    def body(w_ref, o_ref):
        for e in range(8):
            o_ref[:, C * e:C * (e + 1)] = w_ref[e]

    return pl.pallas_call(body, name=name, out_shape=_sds((R, 8 * C), w8.dtype), compiler_params=_cp())(w8)


def _all_gather_small(srcs, name):
    n = len(srcs)
    vmem = pl.BlockSpec(memory_space=pltpu.VMEM)

    def body(*refs):
        src, out = refs[:n], refs[n:2 * n]
        send_sems, recv_sems = refs[2 * n:]
        x, y, c = _my_place()
        me = 4 * x + 2 * y + c
        flip = lambda v, f: 1 - v if f else v
        peers = [(flip(x, fx), flip(y, fy), flip(c, fc)) for fx in (0, 1) for fy in (0, 1) for fc in (0, 1)][1:]
        cps = []
        for p in range(n):
            out[p][me] = src[p][...]
            for k, peer in enumerate(peers):
                cps.append(pltpu.make_async_remote_copy(
                    src_ref=src[p], dst_ref=out[p].at[me], send_sem=send_sems.at[7 * p + k],
                    recv_sem=recv_sems.at[7 * p + k], device_id=peer, device_id_type=MESH))
        for cp in cps:
            cp.start()
        for cp in cps:
            cp.wait()

    return pl.pallas_call(
        body, name=name, in_specs=[vmem] * n, out_specs=[vmem] * n,
        out_shape=[_sds((8,) + a.shape, a.dtype) for a in srcs],
        scratch_shapes=[pltpu.SemaphoreType.DMA((7 * n,)), pltpu.SemaphoreType.DMA((7 * n,))],
    )(*srcs)


def _adaln_exchange(c, w_ada, b_part):
    vmem = pl.BlockSpec(memory_space=pltpu.VMEM)

    def body(c_ref, w_ref, b_ref, call_ref, mod_ref, part_ref, send_sems, recv_sems):
        x, y, c = _my_place()
        me = 4 * x + 2 * y + c
        flip = lambda v, f: 1 - v if f else v
        peers = [(flip(x, fx), flip(y, fy), flip(c, fc)) for fx in (0, 1) for fy in (0, 1) for fc in (0, 1)][1:]
        call_ref[me] = c_ref[...]
        first = [pltpu.make_async_remote_copy(src_ref=c_ref, dst_ref=call_ref.at[me], send_sem=send_sems.at[k],
                                              recv_sem=recv_sems.at[k], device_id=peer, device_id_type=MESH)
                 for k, peer in enumerate(peers)]
        for cp in first:
            cp.start()
        for cp in first:
            cp.wait()
        c_all = jnp.concatenate([call_ref[j] for j in range(8)], axis=0)
        part_ref[...] = _dot(c_all.astype(CDT), w_ref[...].astype(CDT), NN) + b_ref[...]
        mod_ref[me] = part_ref[pl.ds(me, 1), :]
        second = [pltpu.make_async_remote_copy(
            src_ref=part_ref.at[pl.ds(4 * px + 2 * py + pc, 1)], dst_ref=mod_ref.at[me], send_sem=send_sems.at[7 + k],
            recv_sem=recv_sems.at[7 + k], device_id=(px, py, pc), device_id_type=MESH)
            for k, (px, py, pc) in enumerate(peers)]
        for cp in second:
            cp.start()
        for cp in second:
            cp.wait()

    return pl.pallas_call(
        body, name="adaln_exchange", in_specs=[vmem] * 3, out_specs=[vmem] * 2,
        out_shape=[_sds((8, 1, D), F32), _sds((8, 1, w_ada.shape[1]), F32)],
        scratch_shapes=[pltpu.VMEM((8, w_ada.shape[1]), F32), pltpu.SemaphoreType.DMA((14,)),
                        pltpu.SemaphoreType.DMA((14,))],
        compiler_params=_cp(),
    )(c, w_ada, b_part)


def _pair_add(dw, recv, axis, width, cidx, name, dep=None):
    dep_specs, dep_args = _after(dep)
    if axis == 0:
        tr = min(width, 256)
        C = dw.shape[1]
        grid = (4, width // tr)
        nb = width // tr
        own = pl.BlockSpec((tr, C), lambda q, i, c_ref: ((2 * q + c_ref[0]) * nb + i, 0))
        oth = pl.BlockSpec((None, tr, C), lambda q, i, c_ref: (q, i, 0))
    else:
        R = dw.shape[0]
        tr = min(R, 1024)
        grid = (4, R // tr)
        own = pl.BlockSpec((tr, width), lambda q, i, c_ref: (i, 2 * q + c_ref[0]))
        oth = pl.BlockSpec((None, tr, width), lambda q, i, c_ref: (q, i, 0))

    def body(c_ref, a_ref, b_ref, *rest):
        o_ref = rest[-1]
        o_ref[...] = (a_ref[...] + b_ref[...].astype(F32)).astype(o_ref.dtype)

    return pl.pallas_call(
        body, name=name,
        grid_spec=pltpu.PrefetchScalarGridSpec(num_scalar_prefetch=1, grid=grid, in_specs=[own, oth] + dep_specs,
                                               out_specs=oth),
        out_shape=(_in_hbm if _is_big(dw) else _sds)(recv.shape, CDT), compiler_params=_cp(("parallel", "parallel")),
    )(cidx, *[_pin(a) if _is_big(dw) else a for a in (dw, recv)], *dep_args)


def _add_pairs(xs, ys, name):
    n = len(xs)

    def body(*refs):
        for p in range(n):
            refs[2 * n + p][...] = refs[p][...] + refs[n + p][...]

    return pl.pallas_call(body, name=name, out_shape=[_sds(a.shape, F32) for a in xs])(*xs, *ys)


def _adam_math(w, g, m, v):
    m = ADAM_B1 * m + (1.0 - ADAM_B1) * g
    v = ADAM_B2 * v + (1.0 - ADAM_B2) * (g * g)
    m_hat = m / (1.0 - ADAM_B1 ** ADAM_STEP)
    v_hat = v / (1.0 - ADAM_B2 ** ADAM_STEP)
    delta = -ADAM_LR * (m_hat / (jnp.sqrt(v_hat) + ADAM_EPS) + ADAM_WD * w)
    return delta, m, v


def _adam(gparts, w, m, v, name, fresh=False):
    P, R, C = gparts.shape
    in_hbm = fresh or _is_big(w)
    tr = R if R <= 256 else 256

    def body(g_ref, w_ref, m_ref, v_ref, go_ref, d_ref, mo_ref, vo_ref):
        g = g_ref[0].astype(F32)
        for i in range(1, P):
            g = g + g_ref[i].astype(F32)
        d, mn, vn = _adam_math(w_ref[...], g, m_ref[...], v_ref[...])
        go_ref[...] = g
        d_ref[...] = d
        mo_ref[...] = mn
        vo_ref[...] = vn

    t = pl.BlockSpec((tr, C), lambda i: (i, 0))
    return pl.pallas_call(
        body, name=name, grid=(R // tr,),
        in_specs=[pl.BlockSpec((P, tr, C), lambda i: (0, i, 0)), t, t, t], out_specs=[t, t, t, t],
        out_shape=[(_in_hbm if in_hbm else _sds)((R, C), F32)] * 4, compiler_params=_cp(("parallel",)),
    )(*[_pin(a) if in_hbm else a for a in (gparts, w, m, v)])


def _sum_parts(gparts, name):
    P, R, C = gparts.shape

    def body(g_ref, o_ref):
        g = g_ref[0]
        for i in range(1, P):
            g = g + g_ref[i]
        o_ref[...] = g

    return pl.pallas_call(body, name=name, out_shape=_sds((R, C), F32))(gparts)


def _adam_w_ada(c_t, dm, w, m, v):
    R, C = w.shape

    def body(c_ref, dm_ref, w_ref, m_ref, v_ref, go_ref, d_ref, mo_ref, vo_ref):
        g = c_ref[:, 0:1] * dm_ref[0:1, :]
        for b in range(1, 8):
            g = g + c_ref[:, b:b + 1] * dm_ref[b:b + 1, :]
        d, mn, vn = _adam_math(w_ref[...], g, m_ref[...], v_ref[...])
        go_ref[...] = g
        d_ref[...] = d
        mo_ref[...] = mn
        vo_ref[...] = vn

    return pl.pallas_call(body, name="adam_w_ada", out_shape=[_sds((R, C), F32)] * 4,
                          compiler_params=_cp())(c_t, dm, w, m, v)


def kernel(x, c, norm_g, w_ada, b_ada, w_in, pool_w, pool_scale, w_attn_br, w_pool_br, w_out, rel_bias, final_g, loss_target, m_norm_g, m_w_ada, m_b_ada, m_w_in, m_pool_w, m_pool_scale, m_w_attn_br, m_w_pool_br, m_w_out, m_rel_bias, m_final_g, v_norm_g, v_w_ada, v_b_ada, v_w_in, v_pool_w, v_pool_scale, v_w_attn_br, v_w_pool_br, v_w_out, v_rel_bias, v_final_g):
    S = x.shape[1]
    px, py, pc = _my_place()
    me = 4 * px + 2 * py + pc
    cidx = jnp.reshape(pc, (1,)).astype(jnp.int32)

    layout = [(1, 1024), (1, 128), (1, 128), (0, 128)]
    final_g2 = final_g.reshape(1, D)
    me1 = jnp.reshape(me, (1,)).astype(jnp.int32)

    b_part = lax.dynamic_slice(b_ada, (0, me * 384), (1, 384))
    c_all, mod_all = _adaln_exchange(c, w_ada[0], b_part)
    c_all = c_all.reshape(8, D)
    mod = mod_all.reshape(1, 3 * D)
    shift, scale, gate = mod[:, :D], mod[:, D:2 * D], mod[:, 2 * D:]

    gplan = _gather_chips_plan([(None, 1)] * 4)
    lands = _place_slot([w_in[0], w_attn_br[0], w_pool_br[0], w_out[0]], me1, 8, "place_own_shards", CDT)
    g_sems, _, lands, tok = _split_start("gather_chips_start", [], lands, 8, gplan, dep=mod_all)
    st = _local_bias_tables(_local_pre(x[0], shift, scale, norm_g, dep=tok), rel_bias, dep=tok)
    _, lands = _split_wait("gather_chips_wait", g_sems, [], lands, gplan,
                           [st["ht"]] + [b for pair in st["biases"] for b in pair])
    w_in_full, wab8, wpb8, wout8 = _pair_share(lands)
    wab, wpb = _cols_from_blocks(wab8, "w_attn_br_cols"), _cols_from_blocks(wpb8, "w_pool_br_cols")
    wout = wout8.reshape(D, D)

    st = _local_grads(st, loss_target[0], gate, w_in_full, pool_w[0], pool_scale, wab, wpb, wout, final_g2)

    early = [st["dpw"].reshape(AW, 128), st["dps"], st["gfg"], jnp.broadcast_to(st["loss"], (1, 128))]
    dws = [st["dw_in"], st["dw_ab"], st["dw_pb"], st["dw_out"]]
    sent = [st["dw_in_low"]] + dws[1:] + early
    pplan = _pair_plan(layout)
    pair_shapes = [(4, w, a.shape[1]) if ax == 0 else (4, a.shape[0], w) for a, (ax, w) in zip(dws, layout)]
    p_sems, sent, recv, tok = _split_start(
        "reduce_pair_start", sent,
        [lax.empty(s, a.dtype) for s, a in zip(pair_shapes + [a.shape for a in early], sent)], 20, pplan)
    d_rel_bias = _local_bias_grads(st, _BIAS_GRADS_EARLY, dep=tok)
    sent, recv = _split_wait("reduce_pair_wait", p_sems, sent, recv, pplan, d_rel_bias)
    slot1 = jnp.reshape(2 * px + py, (1,)).astype(jnp.int32)
    part_in = [_pair_add(dws[0], recv[0], *layout[0], cidx, "pair_add0")]
    wplan, cplan = _chips_plan(1), _chips_plan(3)
    w_sems, part_in, land_in, tok = _split_start("reduce_chips_start_w_in", part_in,
                                                 _place_slot(part_in, slot1, 4, "place_own_partial_w_in"), 3, wplan)
    parts = [_pair_add(dw, r, ax, w, cidx, f"pair_add{i}", dep=tok)
             for i, dw, r, (ax, w) in zip((1, 2, 3), sent[1:4], recv[1:4], layout[1:])]
    parts += _add_pairs(sent[4:], recv[4:], "pair_add_smalls")
    c_sems, parts, lands, tok = _split_start("reduce_chips_start", parts,
                                             _place_slot(parts, slot1, 4, "place_own_partials"), 21, cplan)
    d_rel_bias = d_rel_bias + _local_bias_grads(st, _BIAS_GRADS_LATE, dep=tok)
    grad_x, dmod, d_norm_g = _local_input_grads(st, w_in_full, norm_g, scale, dep=tok)

    dmod_all, ng_all, rb_all = _all_gather_small([dmod, d_norm_g, d_rel_bias], "gather_late")
    o_norm_g = _adam(ng_all, norm_g, m_norm_g, v_norm_g, "adam_norm_g", fresh=True)
    o_rel_bias_t = _adam(rb_all, rel_bias.T, m_rel_bias.T, v_rel_bias.T, "adam_rel_bias", fresh=True)
    o_rel_bias = [o.T for o in o_rel_bias_t]
    o_b_ada = _adam(dmod_all, b_ada, m_b_ada, v_b_ada, "adam_b_ada", fresh=True)
    dm_mine = lax.dynamic_slice(dmod_all.reshape(8, 3 * D), (0, me * 384), (8, 384))
    o_w_ada = [o[None] for o in _adam_w_ada(c_all.T, dm_mine, w_ada[0], m_w_ada[0], v_w_ada[0])]

    late = [o_w_ada[0], o_b_ada[0], o_norm_g[0], o_rel_bias_t[0]]
    _, (g_in,) = _split_wait("reduce_chips_wait_w_in", w_sems, part_in, land_in, wplan, late)
    o_w_in = [o[None] for o in _adam(g_in, w_in[0], m_w_in[0], v_w_in[0], "adam_w_in")]
    _, (g_ab, g_pb, g_out, g_pw, g_ps, g_fg, g_loss) = _split_wait(
        "reduce_chips_wait", c_sems, parts, lands, cplan, late + [o_w_in[0]])
    o_w_ab = [o[None] for o in _adam(g_ab, w_attn_br[0], m_w_attn_br[0], v_w_attn_br[0], "adam_w_attn_br")]
    o_w_pb = [o[None] for o in _adam(g_pb, w_pool_br[0], m_w_pool_br[0], v_w_pool_br[0], "adam_w_pool_br")]
    o_w_out = [o[None] for o in _adam(g_out, w_out[0], m_w_out[0], v_w_out[0], "adam_w_out")]
    o_pool_w = [o.reshape(pool_w.shape) for o in _adam(g_pw, pool_w.reshape(AW, 128), m_pool_w.reshape(AW, 128),
                                                       v_pool_w.reshape(AW, 128), "adam_pool_w")]
    o_pool_scale = _adam(g_ps, pool_scale, m_pool_scale, v_pool_scale, "adam_pool_scale")
    o_final_g = [o.reshape(D) for o in _adam(g_fg, final_g2, m_final_g.reshape(1, D), v_final_g.reshape(1, D),
                                             "adam_final_g")]
    loss = _sum_parts(g_loss, "sum_loss")[0, 0]

    per_w = [o_norm_g, o_w_ada, o_b_ada, o_w_in, o_pool_w, o_pool_scale, o_w_ab, o_w_pb, o_w_out, o_rel_bias, o_final_g]
    outs = [loss, grad_x[None]]
    for k in range(4):
        outs += [o[k] for o in per_w]
    return tuple(outs)
```

```python
import functools
import math

import numpy as np
import jax
import jax.numpy as jnp
from jax import lax
from jax.experimental import pallas as pl
from jax.experimental.pallas import tpu as pltpu

F32 = jnp.float32
CDT = jnp.bfloat16
D = 1024
HD = 64
NH = 8
AW = 512
QKV_W = 4608
REST_W = 3584
R16 = 16
EPS = 1e-6
NEG = -1e30
N_BACK = 128
ADAM_LR, ADAM_B1, ADAM_B2, ADAM_EPS, ADAM_WD, ADAM_STEP = 0.001, 0.9, 0.999, 1e-08, 0.01, 10
MESH = pl.DeviceIdType.MESH
VMEM_MB = 1024 * 1024


def _cp(sem=None, vmem=48, **kw):
    if sem is not None:
        kw["dimension_semantics"] = sem
    return pltpu.CompilerParams(vmem_limit_bytes=vmem * VMEM_MB, **kw)


def _sds(shape, dtype):
    return jax.ShapeDtypeStruct(tuple(shape), dtype)


def _in_hbm(shape, dtype):
    return pltpu.HBM(tuple(shape), dtype)


def _pin(a):
    return pltpu.with_memory_space_constraint(a, pltpu.HBM)


def _is_big(a):
    return a.size * a.dtype.itemsize >= 4 * VMEM_MB


def _rest_blk(jj):
    return jnp.where(jj == 0, 9, jnp.where(jj == 6, 10, jj + 10))


def _sigmoid(z):
    return 0.5 * jnp.tanh(0.5 * z) + 0.5


def _dot(a, b, dims):
    return lax.dot_general(a, b, (dims, ((), ())), preferred_element_type=F32)


NN = ((1,), (0,))
NT = ((1,), (1,))
TN = ((0,), (0,))


def _ld(ref):
    if len(ref.shape) == 2:
        return ref[...]
    return jnp.concatenate([ref[a] for a in range(ref.shape[0])], axis=0)


def _st(ref, val, cols=None):
    val = val.astype(ref.dtype)
    if len(ref.shape) == 2:
        if cols is None:
            ref[...] = val
        else:
            ref[:, cols] = val
        return
    rows = ref.shape[1]
    for a in range(ref.shape[0]):
        if cols is None:
            ref[a] = val[a * rows:(a + 1) * rows]
        else:
            ref[a, :, cols] = val[a * rows:(a + 1) * rows]


REST_COLS = ((9, 10), (11, 16), (10, 11))
_HBM = pl.BlockSpec(memory_space=pltpu.HBM)


def _resident(shape):
    return pl.BlockSpec(shape, lambda *ids: (0,) * len(shape), pipeline_mode=pl.Buffered(1))


def _w512(w_ref, j):
    return w_ref[j // 2, :, 512 * (j % 2):512 * (j % 2 + 1)]


def _proj(h, w):
    S = h.shape[0]
    tm = 512

    def body(a_ref, w_ref, q_ref, r_ref):
        a = a_ref[...]
        for j in range(9):
            q_ref[:, 512 * j:512 * (j + 1)] = _dot(a, _w512(w_ref, j), NN).astype(q_ref.dtype)
        off = 0
        for lo, hi in REST_COLS:
            for j in range(lo, hi):
                r_ref[:, off:off + 512] = _dot(a, _w512(w_ref, j), NN)
                off += 512

    return pl.pallas_call(
        body, name="proj", grid=(S // tm,),
        in_specs=[pl.BlockSpec((tm, D), lambda i: (i, 0)), _resident((8, D, 1024))],
        out_specs=[pl.BlockSpec((tm, QKV_W), lambda i: (i, 0)), pl.BlockSpec((tm, REST_W), lambda i: (i, 0))],
        out_shape=[_sds((S, QKV_W), CDT), _sds((S, REST_W), F32)],
        compiler_params=_cp(("parallel",), vmem=56),
    )(h, w)


def _after(dep):
    return ([], []) if dep is None else ([pl.BlockSpec(memory_space=pl.ANY)], [dep])


def _dh(dqkv, drest, w, dep=None):
    S = dqkv.shape[0]
    tm = 512
    dep_specs, dep_args = _after(dep)

    def body(a_ref, b_ref, w_ref, *rest):
        o_ref = rest[-1]
        acc = _dot(a_ref[:, 0:512], _w512(w_ref, 0), NT)
        for j in range(1, 9):
            acc = acc + _dot(a_ref[:, 512 * j:512 * (j + 1)], _w512(w_ref, j), NT)
        off = 0
        for lo, hi in REST_COLS:
            for j in range(lo, hi):
                acc = acc + _dot(b_ref[:, off:off + 512], _w512(w_ref, j), NT)
                off += 512
        o_ref[...] = acc

    return pl.pallas_call(
        body, name="dh", grid=(S // tm,),
        in_specs=[pl.BlockSpec((tm, QKV_W), lambda i: (i, 0)), pl.BlockSpec((tm, REST_W), lambda i: (i, 0)),
                  _resident((8, D, 1024))] + dep_specs,
        out_specs=pl.BlockSpec((tm, D), lambda i: (i, 0)),
        out_shape=_sds((S, D), F32),
        compiler_params=_cp(("parallel",), vmem=56),
    )(dqkv, drest, w, *dep_args)


def _mm_tn(at, b, *, n_total, out_blk, name, prev=None, tn=512, with_low=False):
    R, S = at.shape
    nb = b.shape[1] // tn
    n_out = 2 if with_low else 1

    def body(*refs):
        res = _dot(refs[0][...], refs[1][...], NN)
        refs[-n_out][...] = res
        if with_low:
            refs[-1][...] = res.astype(CDT)

    in_specs = [_resident((R, S)), pl.BlockSpec((S, tn), lambda j: (0, j))]
    args = [at, b]
    aliases = {}
    if prev is not None:
        prev = list(prev) if with_low else [prev]
        in_specs += [pl.BlockSpec(memory_space=pl.ANY)] * n_out
        args += prev
        aliases = {2 + k: k for k in range(n_out)}
    out_spec = pl.BlockSpec((R, tn), lambda j: (0, out_blk(j)))
    outs = pl.pallas_call(
        body, name=name, grid=(nb,), in_specs=in_specs,
        out_specs=[out_spec] * n_out,
        out_shape=[_sds((R, n_total), F32), _sds((R, n_total), CDT)][:n_out],
        input_output_aliases=aliases,
        compiler_params=_cp(("parallel",)),
    )(*args)
    return tuple(outs) if with_low else outs[0]


def _tcast(a, name):
    M, C = a.shape
    tm = 512

    def body(a_ref, o_ref):
        o_ref[...] = a_ref[...].astype(F32).T.astype(o_ref.dtype)

    return pl.pallas_call(
        body, name=name, grid=(M // tm,),
        in_specs=[pl.BlockSpec((tm, C), lambda i: (i, 0))],
        out_specs=pl.BlockSpec((C, tm), lambda i: (0, i)),
        out_shape=_sds((C, M), CDT), compiler_params=_cp(("parallel",)),
    )(a)


TOK = R16 * R16


def _swap16():
    i = lax.broadcasted_iota(jnp.int32, (TOK, TOK), 0)
    j = lax.broadcasted_iota(jnp.int32, (TOK, TOK), 1)
    return jnp.where(j == jnp.bitwise_and(i, 15) * 16 + jnp.right_shift(i, 4), 1.0, 0.0).astype(CDT)


def _norm_mod(x, norm_g, shift, scale, dep=None):
    S = x.shape[0]
    dep_specs, dep_args = _after(dep)

    def body(x_ref, g_ref, sh_ref, sc_ref, *rest):
        xv = x_ref[...]
        xn = xv * lax.rsqrt(jnp.mean(xv * xv, axis=-1, keepdims=True) + EPS)
        h = (xn * (g_ref[...] * (1.0 + sc_ref[...])) + sh_ref[...]).astype(CDT)
        _st(rest[-1], _dot(_swap16(), h, NN))

    vec = pl.BlockSpec((1, D), lambda i: (0, 0))
    return pl.pallas_call(
        body, name="norm_mod", grid=(S // TOK,),
        in_specs=[pl.BlockSpec((TOK, D), lambda i: (i, 0)), vec, vec, vec] + dep_specs,
        out_specs=pl.BlockSpec((R16, R16, D), lambda i: (0, i, 0)),
        out_shape=_sds((R16, S // R16, D), CDT), compiler_params=_cp(("parallel",)),
    )(x, norm_g, shift, scale, *dep_args)


def _t5_bucket_np(n):
    nf = np.maximum(n, 1).astype(np.float32)
    large = 16 + (np.log(nf / np.float32(16)) / np.float32(math.log(2048 / 16)) * np.float32(16)).astype(np.int32)
    large = np.minimum(large, 31)
    return np.where(n < 16, n, large).astype(np.int32)


class _Geom:
    def __init__(self, gi, LS):
        self.gi, self.LS = gi, LS
        self.dil = (1, 4, 16)[gi]
        if gi == 0:
            self.lead, self.rows, self.n_tiles, self.has_prev = (R16,), 16, LS // 16, True
            a = np.arange(R16)[:, None]
            ll = np.arange(16)[None, :]
            self.pos = (16 * ll + a).reshape(-1)
        elif gi == 1:
            self.lead, self.rows, self.n_tiles, self.has_prev = (4,), 64, LS // 64, True
            a = np.arange(4)[:, None]
            ll = np.arange(64)[None, :]
            self.pos = (4 * ll + a).reshape(-1)
        else:
            self.lead, self.rows, self.n_tiles, self.has_prev = (), LS, 1, False
            self.pos = np.arange(LS)
        self.T = self.pos.shape[0]
        self.Tk = 2 * self.T if self.has_prev else self.T
        self.outer = {0: (), 1: (4,), 2: (R16,)}[gi]

    def view(self, arr):
        S, C = arr.shape
        if self.gi == 1:
            return arr.reshape(4, 4, self.LS, C)
        return arr.reshape(R16, self.LS, C)

    def view_shape(self, C):
        return (4, 4, self.LS, C) if self.gi == 1 else (R16, self.LS, C)

    def unview(self, arr):
        return arr.reshape(R16 * self.LS, arr.shape[-1])

    def spec(self, W, cblk, tile):
        if self.gi == 0:
            return pl.BlockSpec((R16, 16, W), lambda n: (0, tile(n), cblk))
        if self.gi == 1:
            return pl.BlockSpec((4, None, 64, W), lambda r, n: (0, r, tile(n), cblk))
        return pl.BlockSpec((None, self.LS, W), lambda r: (r, 0, cblk))

    def halves(self):
        R = self.rows
        idx = np.arange(self.T).reshape(-1, R)
        return idx[:, :R // 2].reshape(-1), idx[:, R // 2:].reshape(-1)

    def bucket_index(self, sub):
        lo, hi = self.halves()
        pos = self.pos
        if sub == 0:
            pq = pos[lo]
            pk = np.concatenate([pos[hi] - self.T, pos[lo]]) if self.has_prev else np.concatenate([pos[lo], pos[hi]])
        else:
            pq = pos[hi]
            pk = np.concatenate([pos[lo], pos[hi]])
        dist = pq[:, None] - pk[None, :]
        ok = (dist >= 0) & (dist <= N_BACK)
        bucket = _t5_bucket_np(np.clip(dist, 0, N_BACK) * self.dil)
        return np.where(ok, bucket, -1).astype(np.int32)


def _bias_table(idx, buckets, rel_bias, gi, sub, dep=None):
    T, Tk = idx.shape
    tr = 32
    dep_specs, dep_args = _after(dep)

    def body(idx_ref, rb_ref, *rest):
        h = pl.program_id(0)
        ix = idx_ref[...]
        acc = jnp.full(ix.shape, NEG, F32)
        for b in buckets:
            acc = jnp.where(ix == b, rb_ref[b, gi * NH + h], acc)
        rest[-1][...] = acc

    return pl.pallas_call(
        body, name=f"bias_table{gi}{'ab'[sub]}", grid=(NH, T // tr),
        in_specs=[pl.BlockSpec((tr, Tk), lambda h, i: (i, 0)), pl.BlockSpec(memory_space=pltpu.SMEM)] + dep_specs,
        out_specs=pl.BlockSpec((None, tr, Tk), lambda h, i: (h, i, 0)),
        out_shape=_sds((NH, T, Tk), F32), compiler_params=_cp(("parallel", "parallel")),
    )(idx, rel_bias, *dep_args)


def _bias_grad(dsum, idx, buckets, gi, sub, dep=None):
    _, T, Tk = dsum.shape
    dep_specs, dep_args = _after(dep)

    def body(ds_ref, idx_ref, *rest):
        o_ref, r_ref = rest[-2], rest[-1]
        r_ref[...] = jnp.zeros_like(r_ref)
        ix, ds = idx_ref[...], ds_ref[...]
        for b in buckets:
            r_ref[b:b + 1, :] = jnp.sum(jnp.where(ix == b, ds, 0.0), axis=0, keepdims=True)
        o_ref[...] = jnp.broadcast_to(jnp.sum(r_ref[...], axis=1, keepdims=True), (32, 128))

    out = pl.pallas_call(
        body, name=f"bias_grad{gi}{'ab'[sub]}", grid=(NH,),
        in_specs=[pl.BlockSpec((None, T, Tk), lambda h: (h, 0, 0)), pl.BlockSpec((T, Tk), lambda h: (0, 0))] + dep_specs,
        out_specs=pl.BlockSpec((None, 32, 128), lambda h: (h, 0, 0)),
        out_shape=_sds((NH, 32, 128), F32), scratch_shapes=[pltpu.VMEM((32, Tk), F32)],
        compiler_params=_cp(("parallel",)),
    )(dsum, idx, *dep_args)
    return out[:, :, 0]


def _prev_mask(n, T, Tk):
    col = lax.broadcasted_iota(jnp.int32, (1, Tk), 1)
    return jnp.where((col < T) & (n == 0), NEG, 0.0).astype(F32)


def _split_rows(v, R):
    A, half = v.shape[0] // R, R // 2
    via_f32 = v.dtype != F32 and half % 16 != 0
    w = v.astype(F32) if via_f32 else v
    lo = jnp.concatenate([w[a * R:a * R + half] for a in range(A)], axis=0)
    hi = jnp.concatenate([w[a * R + half:(a + 1) * R] for a in range(A)], axis=0)
    return (lo.astype(v.dtype), hi.astype(v.dtype)) if via_f32 else (lo, hi)


def _merge_rows(lo, hi, R):
    half = R // 2
    A = lo.shape[0] // half
    return jnp.concatenate([x[a * half:(a + 1) * half] for a in range(A) for x in (lo, hi)], axis=0)


def _attn_fwd(qkv, biases, geom, others=None):
    S = qkv.shape[0]
    gi, T, R, has_prev = geom.gi, geom.T, geom.rows, geom.has_prev
    H2 = T // 2
    qv = geom.view(qkv)
    cq, ck, cv = 3 * gi, 3 * gi + 1, 3 * gi + 2
    assert others is None or not has_prev

    def body(*refs):
        if has_prev:
            q_ref, kp_ref, kc_ref, vp_ref, vc_ref, ba_ref, bb_ref, o_ref, l_ref = refs
        elif others is not None:
            (q_ref, kc_ref, vc_ref, ba_ref, bb_ref, o0_ref, o1_ref, l0_ref, l1_ref, z_ref,
             attn_ref, L_ref, ag_ref) = refs
        else:
            q_ref, kc_ref, vc_ref, ba_ref, bb_ref, o_ref, l_ref = refs
        q_lo, q_hi = _split_rows(_ld(q_ref), R)
        kc_lo, kc_hi = _split_rows(_ld(kc_ref), R)
        vc_lo, vc_hi = _split_rows(_ld(vc_ref), R)
        k_b = jnp.concatenate([kc_lo, kc_hi], axis=0)
        v_b = jnp.concatenate([vc_lo, vc_hi], axis=0)
        if has_prev:
            n = pl.program_id(len(geom.outer))
            k_a = jnp.concatenate([_split_rows(_ld(kp_ref), R)[1], kc_lo], axis=0)
            v_a = jnp.concatenate([_split_rows(_ld(vp_ref), R)[1], vc_lo], axis=0)
            pm = _prev_mask(n, H2, T)
        else:
            k_a, v_a, pm = k_b, v_b, None
        low = lax.broadcasted_iota(jnp.int32, (1, 2 * HD), 1) < HD
        for j in range(NH // 2):
            cols = slice(2 * HD * j, 2 * HD * (j + 1))
            o_half, l_half = [], []
            for q, k, v, b_ref, mask in ((q_lo, k_a, v_a, ba_ref, pm), (q_hi, k_b, v_b, bb_ref, None)):
                q2 = q[:, cols].astype(F32) * 0.125
                k2, v2 = k[:, cols], v[:, cols]
                qs = jnp.concatenate([jnp.where(low, q2, 0.0), jnp.where(low, 0.0, q2)], axis=0).astype(q.dtype)
                s = _dot(qs, k2, NT) + b_ref[j]
                if mask is not None:
                    s = s + mask
                m = jnp.max(s, axis=1, keepdims=True)
                p = jnp.exp(s - m)
                l = jnp.sum(p, axis=1, keepdims=True)
                pv = _dot(p.astype(v.dtype), v2, NN) / l
                lse = m + jnp.log(l)
                o_half.append(jnp.where(low, pv[:H2], pv[H2:]))
                l_half.append(jnp.where(low, lse[:H2], lse[H2:]))
            o2, lc = _merge_rows(o_half[0], o_half[1], R), _merge_rows(l_half[0], l_half[1], R)
            if others is None:
                _st(o_ref, o2, cols)
                _st(l_ref, lc, cols)
                continue
            la, lb = l0_ref[:, cols], l1_ref[:, cols]
            m = jnp.maximum(jnp.maximum(la, lb), lc)
            ea, eb, ec = jnp.exp(la - m), jnp.exp(lb - m), jnp.exp(lc - m)
            den = ea + eb + ec
            attn = (ea * o0_ref[:, cols] + eb * o1_ref[:, cols] + ec * o2) / den
            attn_ref[:, cols] = attn
            L_ref[:, cols] = m + jnp.log(den)
            z = z_ref[:, cols]
            ag = attn * (z * _sigmoid(z))
            ag_ref[:, cols] = ag.astype(ag_ref.dtype)

    cur = lambda n: n
    prev = lambda n: jnp.maximum(n - 1, 0)
    if has_prev:
        in_specs = [geom.spec(AW, cq, cur), geom.spec(AW, ck, prev), geom.spec(AW, ck, cur),
                    geom.spec(AW, cv, prev), geom.spec(AW, cv, cur)]
        args = [qv] * 5
    else:
        in_specs = [geom.spec(AW, cq, cur), geom.spec(AW, ck, cur), geom.spec(AW, cv, cur)]
        args = [qv] * 3
    nd = len(geom.outer) + (1 if has_prev else 0)
    biases = [b.reshape(NH // 2, 2 * H2, b.shape[2]) for b in biases]
    in_specs += [pl.BlockSpec(b.shape, lambda *ids: (0, 0, 0)) for b in biases]
    grid = geom.outer + ((geom.n_tiles,) if has_prev else ())
    oshape = geom.view_shape(AW)
    tile = geom.spec(AW, 0, cur)
    if others is None:
        o, l = pl.pallas_call(
            body, name=f"attn_fwd{gi}", grid=grid, in_specs=in_specs, out_specs=[tile, tile],
            out_shape=[_sds(oshape, F32), _sds(oshape, F32)],
            compiler_params=_cp(("arbitrary",) * nd),
        )(*args, *biases)
        return geom.unview(o), geom.unview(l)
    os_, ls_, rest = others
    attn, Lb, ag = pl.pallas_call(
        body, name=f"attn_fwd{gi}_merge", grid=grid, in_specs=in_specs + [tile] * 5,
        out_specs=[tile, tile, tile],
        out_shape=[_sds(oshape, F32), _sds(oshape, F32), _sds(oshape, CDT)],
        compiler_params=_cp(("arbitrary",) * nd),
    )(*args, *biases, *[geom.view(a) for a in (*os_, *ls_, rest)])
    return geom.unview(attn), geom.unview(Lb), geom.unview(ag)


def _pool_counts(r, LS, win):
    l = lax.broadcasted_iota(jnp.int32, (LS, 1), 0)
    return jnp.minimum(16 * l + (r + 1), win).astype(F32)


def _pool_fwd(rest, pool_w, pool_scale):
    S = rest.shape[0]
    LS = S // R16
    r3 = rest.reshape(R16, LS, REST_W)
    PG = 128

    def body(u_ref, z_ref, pw_ref, ps_ref, pt_ref, mp_ref, pg_ref, e_ref):
        g = pl.program_id(0)
        win = jnp.left_shift(2, g)
        row = lax.broadcasted_iota(jnp.int32, (LS, PG), 0)
        for r in range(R16):
            e_ref[R16 + r] = u_ref[r]
        for r in range(1, R16):
            e_ref[r] = jnp.where(row == 0, 0.0, pltpu.roll(u_ref[r], 1, 0))

        def double(s):
            for i in range(2 * R16 - 1, 2 * s - 1, -1):
                e_ref[i] = e_ref[i] + e_ref[i - s]

        double(1)
        for k in (1, 2, 3):
            pl.when(g >= k)(functools.partial(double, 2 ** k))
        pw = pw_ref[...].astype(CDT)
        for r in range(R16):
            u = u_ref[r]
            pooled = e_ref[R16 + r] / _pool_counts(r, LS, win) - u
            pc = pooled.astype(CDT)
            mp = _dot(pc, pw, NN)
            z = z_ref[r]
            pg = (mp * ps_ref[...]) * (z * _sigmoid(z))
            mp_ref[r] = mp
            pg_ref[r] = pg.astype(pg_ref.dtype)
            pt_ref[:, r * LS:(r + 1) * LS] = pooled.T.astype(pt_ref.dtype)

    col = lambda off: pl.BlockSpec((R16, LS, PG), lambda g: (0, 0, off + g))
    tsp = pl.BlockSpec((PG, S), lambda g: (g, 0))
    pt, mp, pg = pl.pallas_call(
        body, name="pool_fwd", grid=(4,),
        in_specs=[col(24), col(4), pl.BlockSpec((None, PG, PG), lambda g: (g, 0, 0)),
                  pl.BlockSpec((1, PG), lambda g: (0, g))],
        out_specs=[tsp, col(0), col(0)],
        out_shape=[_sds((AW, S), CDT), _sds((R16, LS, AW), F32), _sds((R16, LS, AW), CDT)],
        scratch_shapes=[pltpu.VMEM((2 * R16, LS, PG), F32)],
        compiler_params=_cp(("parallel",)),
    )(r3, r3, pool_w, pool_scale)
    return pt, mp.reshape(S, AW), pg.reshape(S, AW)


def _branch_merge(ag, pg, wab, wpb, rest):
    S = ag.shape[0]
    tm, tn = 512, 512

    def body(ag_ref, pg_ref, wa_ref, wp_ref, ga_ref, gp_ref, m_ref, mt_ref):
        ya = _dot(ag_ref[...], wa_ref[...], NN)
        yp = _dot(pg_ref[...], wp_ref[...], NN)
        mg = _sigmoid(ga_ref[...]) * ya + _sigmoid(gp_ref[...]) * yp
        m_ref[...] = mg.astype(m_ref.dtype)
        mt_ref[...] = mg.T.astype(mt_ref.dtype)

    a = pl.BlockSpec((tm, AW), lambda i, j: (i, 0))
    w = pl.BlockSpec((AW, tn), lambda i, j: (0, j))
    o = pl.BlockSpec((tm, tn), lambda i, j: (i, j))
    return pl.pallas_call(
        body, name="branch_merge", grid=(S // tm, D // tn),
        in_specs=[a, a, w, w, pl.BlockSpec((tm, tn), lambda i, j: (i, 2 + j)),
                  pl.BlockSpec((tm, tn), lambda i, j: (i, 4 + j))],
        out_specs=[o, pl.BlockSpec((tn, tm), lambda i, j: (j, i))],
        out_shape=[_sds((S, D), CDT), _sds((D, S), CDT)],
        compiler_params=_cp(("parallel", "parallel")),
    )(ag, pg, wab, wpb, rest, rest)


def _merge_out_loss(ag, pg, rest, wab, wpb, wout, x, target, gate, final_g):
    S = x.shape[0]
    LS = S // R16

    def body(ag_ref, pg_ref, ga_ref, gp_ref, wa_ref, wp_ref, w_ref, x_ref, t_ref, gate_ref, fg_ref,
             dx2_ref, dmo_ref, loss_ref, gfg_ref, dgate_ref, dw_ref):
        i = pl.program_id(0)

        @pl.when(i == 0)
        def _():
            loss_ref[...] = jnp.zeros_like(loss_ref)
            gfg_ref[...] = jnp.zeros_like(gfg_ref)
            dgate_ref[...] = jnp.zeros_like(dgate_ref)
            dw_ref[...] = jnp.zeros_like(dw_ref)

        swap = _swap16()
        ya = _dot(_ld(ag_ref), wa_ref[...], NN)
        yp = _dot(_ld(pg_ref), wp_ref[...], NN)
        mg = (_sigmoid(_ld(ga_ref)) * ya + _sigmoid(_ld(gp_ref)) * yp).astype(CDT)
        m_nat = _dot(swap, mg, NN)
        mo = _dot(m_nat.astype(CDT), w_ref[...], NN)
        gate, fg = gate_ref[...], fg_ref[...]
        x2 = x_ref[...] + gate * mo
        r2 = lax.rsqrt(jnp.mean(x2 * x2, axis=-1, keepdims=True) + EPS)
        xn2 = x2 * r2
        diff = xn2 * fg - t_ref[...]
        loss = jnp.sum(jnp.sum(diff * diff, axis=-1, keepdims=True), axis=0, keepdims=True)
        dy = diff * (1.0 / D)
        dxn2 = dy * fg
        dx2 = r2 * (dxn2 - xn2 * jnp.mean(dxn2 * xn2, axis=-1, keepdims=True))
        dx2_ref[...] = dx2
        dmo = (dx2 * gate).astype(CDT)
        _st(dmo_ref, _dot(swap, dmo, NN))
        loss_ref[...] += jnp.broadcast_to(loss * (0.5 / D), loss_ref.shape)
        gfg_ref[...] += jnp.sum(dy * xn2, axis=0, keepdims=True)
        dgate_ref[...] += jnp.sum(dx2 * mo, axis=0, keepdims=True)
        dw_ref[...] += _dot(m_nat.T.astype(CDT), dmo, NN)

    def slab(w, cblk):
        return pl.BlockSpec((R16, R16, w), lambda i: (0, i, cblk))

    nat = pl.BlockSpec((TOK, D), lambda i: (i, 0))
    vec = pl.BlockSpec((1, D), lambda i: (0, 0))
    r3 = rest.reshape(R16, LS, REST_W)
    return pl.pallas_call(
        body, name="merge_out_loss", grid=(S // TOK,),
        in_specs=[slab(AW, 0), slab(AW, 0), slab(D, 1), slab(D, 2), _resident((AW, D)), _resident((AW, D)),
                  _resident((D, D)), nat, nat, vec, vec],
        out_specs=[nat, slab(D, 0), pl.BlockSpec((1, 128), lambda i: (0, 0)), vec, vec,
                   pl.BlockSpec((D, D), lambda i: (0, 0))],
        out_shape=[_sds((S, D), F32), _sds((R16, LS, D), CDT), _sds((1, 128), F32), _sds((1, D), F32),
                   _sds((1, D), F32), _sds((D, D), F32)],
        compiler_params=_cp(("arbitrary",)),
    )(ag.reshape(R16, LS, AW), pg.reshape(R16, LS, AW), r3, r3, wab, wpb, wout, x, target, gate, final_g)


def _out_loss(merged, wout, x, target, gate, final_g):
    S = x.shape[0]

    def body(m_ref, w_ref, x_ref, t_ref, gate_ref, fg_ref, dx2_ref, dmo_ref, loss_ref, gfg_ref, dgate_ref):
        i = pl.program_id(0)

        @pl.when(i == 0)
        def _():
            loss_ref[...] = jnp.zeros_like(loss_ref)
            gfg_ref[...] = jnp.zeros_like(gfg_ref)
            dgate_ref[...] = jnp.zeros_like(dgate_ref)

        swap = _swap16()
        mo = _dot(_dot(swap, _ld(m_ref), NN).astype(CDT), w_ref[...], NN)
        gate, fg = gate_ref[...], fg_ref[...]
        x2 = x_ref[...] + gate * mo
        r2 = lax.rsqrt(jnp.mean(x2 * x2, axis=-1, keepdims=True) + EPS)
        xn2 = x2 * r2
        diff = xn2 * fg - t_ref[...]
        loss = jnp.sum(jnp.sum(diff * diff, axis=-1, keepdims=True), axis=0, keepdims=True)
        dy = diff * (1.0 / D)
        dxn2 = dy * fg
        dx2 = r2 * (dxn2 - xn2 * jnp.mean(dxn2 * xn2, axis=-1, keepdims=True))
        dx2_ref[...] = dx2
        _st(dmo_ref, _dot(swap, (dx2 * gate).astype(CDT), NN))
        loss_ref[...] += jnp.broadcast_to(loss * (0.5 / D), loss_ref.shape)
        gfg_ref[...] += jnp.sum(dy * xn2, axis=0, keepdims=True)
        dgate_ref[...] += jnp.sum(dx2 * mo, axis=0, keepdims=True)

    slab = pl.BlockSpec((R16, R16, D), lambda i: (0, i, 0))
    nat = pl.BlockSpec((TOK, D), lambda i: (i, 0))
    vec = pl.BlockSpec((1, D), lambda i: (0, 0))
    return pl.pallas_call(
        body, name="out_loss", grid=(S // TOK,),
        in_specs=[slab, _resident((D, D)), nat, nat, vec, vec],
        out_specs=[nat, slab, pl.BlockSpec((1, 128), lambda i: (0, 0)), vec, vec],
        out_shape=[_sds((S, D), F32), _sds((R16, S // R16, D), CDT), _sds((1, 128), F32), _sds((1, D), F32),
                   _sds((1, D), F32)],
        compiler_params=_cp(("arbitrary",)),
    )(merged, wout, x, target, gate, final_g)


def _bwd_gates(dmo, wout, wab, wpb, rest, ag, pg, attn, mp, pool_scale):
    S = dmo.shape[0]
    tm = 256

    def body(dmo_ref, wo_ref, wa_ref, wp_ref, za_ref, zp_ref, ga0, ga1, gp0, gp1, ag_ref, pg_ref, at_ref, mp_ref,
             ps_ref, dwa_ref, dwp_ref, dr_ref, dat_ref, dl_ref, dmp_ref, dps_ref):
        i = pl.program_id(0)

        @pl.when(i == 0)
        def _():
            dps_ref[...] = jnp.zeros_like(dps_ref)
            dwa_ref[...] = jnp.zeros_like(dwa_ref)
            dwp_ref[...] = jnp.zeros_like(dwp_ref)

        dm = _dot(dmo_ref[...], wo_ref[...], NT)
        sa = _sigmoid(jnp.concatenate([ga0[...], ga1[...]], axis=1))
        sp = _sigmoid(jnp.concatenate([gp0[...], gp1[...]], axis=1))
        dya = (dm * sa).astype(CDT)
        dyp = (dm * sp).astype(CDT)
        ag, pg = ag_ref[...], pg_ref[...]
        dwa_ref[...] += _dot(ag.astype(F32).T.astype(CDT), dya, NN)
        dwp_ref[...] += _dot(pg.astype(F32).T.astype(CDT), dyp, NN)
        ya = _dot(ag, wa_ref[...], NN)
        yp = _dot(pg, wp_ref[...], NN)
        dr_ref[:, 1024:2048] = (dm * ya * sa * (1.0 - sa)).astype(dr_ref.dtype)
        dr_ref[:, 2048:3072] = (dm * yp * sp * (1.0 - sp)).astype(dr_ref.dtype)
        da = _dot(dya, wa_ref[...], NT)
        dp = _dot(dyp, wp_ref[...], NT)
        za = za_ref[...]
        sga = _sigmoid(za)
        attn = at_ref[...]
        dattn = da * (za * sga)
        dr_ref[:, 0:512] = (da * attn * (sga * (1.0 + za * (1.0 - sga)))).astype(dr_ref.dtype)
        dat_ref[...] = dattn.astype(dat_ref.dtype)
        prod = dattn * attn
        low = lax.broadcasted_iota(jnp.int32, (1, 2 * HD), 1) < HD
        for j in range(NH // 2):
            x2 = prod[:, 2 * HD * j:2 * HD * (j + 1)]
            s0 = jnp.sum(jnp.where(low, x2, 0.0), axis=1, keepdims=True)
            s1 = jnp.sum(jnp.where(low, 0.0, x2), axis=1, keepdims=True)
            dl_ref[:, 2 * HD * j:2 * HD * (j + 1)] = jnp.where(low, s0, s1)
        zp = zp_ref[...]
        sgp = _sigmoid(zp)
        mpre = mp_ref[...]
        ps = ps_ref[...]
        dmixed = dp * (zp * sgp)
        dr_ref[:, 512:1024] = (dp * (mpre * ps) * (sgp * (1.0 + zp * (1.0 - sgp)))).astype(dr_ref.dtype)
        dps_ref[...] += jnp.sum(dmixed * mpre, axis=0, keepdims=True)
        dmp_ref[...] = (dmixed * ps).astype(dmp_ref.dtype)

    full = lambda shape: pl.BlockSpec(shape, lambda i: (0, 0))
    tD = pl.BlockSpec((tm, D), lambda i: (i, 0))
    tA = pl.BlockSpec((tm, AW), lambda i: (i, 0))
    rb = lambda c: pl.BlockSpec((tm, AW), lambda i: (i, c))
    return pl.pallas_call(
        body, name="bwd_gates", grid=(S // tm,),
        in_specs=[tD, full((D, D)), full((AW, D)), full((AW, D)), rb(0), rb(1), rb(2), rb(3), rb(4), rb(5),
                  tA, tA, tA, tA, full((1, AW))],
        out_specs=[full((AW, D)), full((AW, D)), pl.BlockSpec((tm, 3072), lambda i: (i, 0)), tA, tA, tA, full((1, AW))],
        out_shape=[_sds((AW, D), F32), _sds((AW, D), F32), _sds((S, REST_W), CDT), _sds((S, AW), CDT),
                   _sds((S, AW), F32), _sds((S, AW), CDT), _sds((1, AW), F32)],
        compiler_params=_cp(("arbitrary",), vmem=56),
    )(dmo, wout, wab, wpb, rest, rest, rest, rest, rest, rest, ag, pg, attn, mp, pool_scale)


def _pool_bwd(dmp, pt, pool_w, drest):
    S = dmp.shape[0]
    LS = S // R16
    PG = 128
    d3 = dmp.reshape(R16, LS, AW)
    dr3 = drest.reshape(R16, LS, REST_W)

    def body(d_ref, pt_ref, pw_ref, dr_in, du_ref, dpw_ref, g_ref, s_ref):
        del dr_in
        g = pl.program_id(0)
        win = jnp.left_shift(2, g)
        row = lax.broadcasted_iota(jnp.int32, (LS, PG), 0)
        pw = pw_ref[...].astype(CDT)
        dpw = jnp.zeros((PG, PG), F32)
        for r in range(R16):
            dm = d_ref[r]
            dpw = dpw + _dot(pt_ref[:, r * LS:(r + 1) * LS], dm, NN)
            dpooled = _dot(dm, pw, NT)
            g_ref[r] = dpooled
            sc = dpooled / _pool_counts(r, LS, win)
            s_ref[r] = sc
            if r < R16 - 1:
                s_ref[R16 + r] = jnp.where(row == LS - 1, 0.0, pltpu.roll(sc, LS - 1, 0))
        dpw_ref[...] = dpw

        def double(s):
            for i in range(0, 2 * R16 - 2 * s):
                s_ref[i] = s_ref[i] + s_ref[i + s]

        double(1)
        for k in (1, 2, 3):
            pl.when(g >= k)(functools.partial(double, 2 ** k))
        for r in range(R16):
            du_ref[r] = (s_ref[r] - g_ref[r]).astype(du_ref.dtype)

    du, dpw = pl.pallas_call(
        body, name="pool_bwd", grid=(4,),
        in_specs=[pl.BlockSpec((R16, LS, PG), lambda g: (0, 0, g)), pl.BlockSpec((PG, S), lambda g: (g, 0)),
                  pl.BlockSpec((None, PG, PG), lambda g: (g, 0, 0)), pl.BlockSpec(memory_space=pl.ANY)],
        out_specs=[pl.BlockSpec((R16, LS, PG), lambda g: (0, 0, 24 + g)),
                   pl.BlockSpec((None, PG, PG), lambda g: (g, 0, 0))],
        out_shape=[_sds((R16, LS, REST_W), CDT), _sds((4, PG, PG), F32)],
        scratch_shapes=[pltpu.VMEM((R16, LS, PG), F32), pltpu.VMEM((2 * R16, LS, PG), F32)],
        input_output_aliases={3: 0},
        compiler_params=_cp(("parallel",)),
    )(d3, pt, pool_w, dr3)
    return du.reshape(S, REST_W), dpw


def _attn_bwd(qkv, dattn, Lb, dlb, biases, geom, dqkv_prev):
    S = qkv.shape[0]
    gi, T, R, has_prev, N = geom.gi, geom.T, geom.rows, geom.has_prev, geom.n_tiles
    H2 = T // 2
    nd_outer = len(geom.outer)
    qv = geom.view(qkv)
    cq, ck, cv = 3 * gi, 3 * gi + 1, 3 * gi + 2
    W3 = 3 * AW

    def heads(q_ref, kp_ref, kc_ref, vp_ref, vc_ref, do_ref, L_ref, dl_ref, ba_ref, bb_ref, pm, dsa_ref, dsb_ref, emit):
        q_lo, q_hi = _split_rows(_ld(q_ref), R)
        do_lo, do_hi = _split_rows(_ld(do_ref), R)
        L_lo, L_hi = _split_rows(_ld(L_ref), R)
        dl_lo, dl_hi = _split_rows(_ld(dl_ref), R)
        kc_lo, kc_hi = _split_rows(_ld(kc_ref), R)
        vc_lo, vc_hi = _split_rows(_ld(vc_ref), R)
        k_b = jnp.concatenate([kc_lo, kc_hi], axis=0)
        v_b = jnp.concatenate([vc_lo, vc_hi], axis=0)
        if has_prev:
            k_a = jnp.concatenate([_split_rows(_ld(kp_ref), R)[1], kc_lo], axis=0)
            v_a = jnp.concatenate([_split_rows(_ld(vp_ref), R)[1], vc_lo], axis=0)
        else:
            k_a, v_a = k_b, v_b
        low = lax.broadcasted_iota(jnp.int32, (1, 2 * HD), 1) < HD
        for j in range(NH // 2):
            sl = slice(2 * HD * j, 2 * HD * (j + 1))
            res = []
            for q, k, v, do, L, dl, b_ref, mask, ds_ref in (
                    (q_lo, k_a, v_a, do_lo, L_lo, dl_lo, ba_ref, pm, dsa_ref),
                    (q_hi, k_b, v_b, do_hi, L_hi, dl_hi, bb_ref, None, dsb_ref)):
                q2 = q[:, sl].astype(F32) * 0.125
                do2 = do[:, sl].astype(F32)
                k2, v2 = k[:, sl], v[:, sl]
                L2, dl2 = L[:, sl], dl[:, sl]
                qs = jnp.concatenate([jnp.where(low, q2, 0.0), jnp.where(low, 0.0, q2)], axis=0).astype(q.dtype)
                dom = jnp.concatenate([jnp.where(low, do2, 0.0), jnp.where(low, 0.0, do2)], axis=0).astype(q.dtype)
                Lc = jnp.concatenate([L2[:, 0:1], L2[:, HD:HD + 1]], axis=0)
                dlc = jnp.concatenate([dl2[:, 0:1], dl2[:, HD:HD + 1]], axis=0)
                s = _dot(qs, k2, NT) + b_ref[j]
                if mask is not None:
                    s = s + mask
                p = jnp.exp(s - Lc)
                ds = p * (_dot(dom, v2, NT) - dlc)
                ds_ref[j] += ds
                dsc = ds.astype(q.dtype)
                dq_st = _dot(dsc, k2, NN)
                res.append((jnp.where(low, dq_st[:H2], dq_st[H2:]) * 0.125, _dot(dsc, qs, TN),
                            _dot(p.astype(q.dtype), dom, TN)))
            (dq_lo, dk_a, dv_a), (dq_hi, dk_b, dv_b) = res
            dq = _merge_rows(dq_lo, dq_hi, R)
            if has_prev:
                emit(sl, dq, _merge_rows(dk_a[H2:] + dk_b[:H2], dk_b[H2:], R),
                     _merge_rows(dv_a[H2:] + dv_b[:H2], dv_b[H2:], R), dk_a[:H2], dv_a[:H2])
            else:
                emit(sl, dq, _merge_rows(dk_a[:H2] + dk_b[:H2], dk_a[H2:] + dk_b[H2:], R),
                     _merge_rows(dv_a[:H2] + dv_b[:H2], dv_a[H2:] + dv_b[H2:], R), None, None)

    def first_step():
        ok = pl.program_id(nd_outer) == 0 if has_prev else pl.program_id(0) == 0
        for a in range(nd_outer if has_prev else 0):
            ok = ok & (pl.program_id(a) == 0)
        return ok

    if has_prev:
        def body(q_ref, kp_ref, kc_ref, vp_ref, vc_ref, do_ref, L_ref, dl_ref, ba_ref, bb_ref, prev_in, out_ref,
                 dsa_ref, dsb_ref, hold, new, pp):
            del prev_in
            s_id = pl.program_id(nd_outer)

            @pl.when(first_step())
            def _():
                dsa_ref[...] = jnp.zeros_like(dsa_ref)
                dsb_ref[...] = jnp.zeros_like(dsb_ref)

            @pl.when(s_id < N)
            def _():
                def emit(sl, dq, dk, dv, dk_prev, dv_prev):
                    new[:, sl] = dq
                    new[:, slice(AW + sl.start, AW + sl.stop)] = dk
                    new[:, slice(2 * AW + sl.start, 2 * AW + sl.stop)] = dv
                    pp[:, sl] = dk_prev
                    pp[:, slice(AW + sl.start, AW + sl.stop)] = dv_prev

                heads(q_ref, kp_ref, kc_ref, vp_ref, vc_ref, do_ref, L_ref, dl_ref, ba_ref, bb_ref,
                      _prev_mask(s_id, H2, T), dsa_ref, dsb_ref, emit)

            @pl.when((s_id >= 1) & (s_id < N))
            def _():
                _st(out_ref, hold[:, 0:AW], slice(0, AW))
                late = _merge_rows(jnp.zeros((H2, 2 * AW), F32), pp[...], R)
                _st(out_ref, hold[:, AW:W3] + late, slice(AW, W3))

            @pl.when(s_id == N)
            def _():
                _st(out_ref, hold[...])

            @pl.when(s_id < N)
            def _():
                hold[...] = new[...]

        cur = lambda n: jnp.minimum(n, N - 1)
        prev = lambda n: jnp.clip(n - 1, 0, N - 1)
        delayed = lambda n: jnp.maximum(n - 1, 0)
        in_specs = [geom.spec(AW, cq, cur), geom.spec(AW, ck, prev), geom.spec(AW, ck, cur),
                    geom.spec(AW, cv, prev), geom.spec(AW, cv, cur),
                    geom.spec(AW, 0, cur), geom.spec(AW, 0, cur), geom.spec(AW, 0, cur)]
        args = [qv] * 5 + [geom.view(dattn), geom.view(Lb), geom.view(dlb)]
        out_spec = geom.spec(W3, gi, delayed)
        grid = geom.outer + (N + 1,)
        scratch = [pltpu.VMEM((T, W3), F32), pltpu.VMEM((T, W3), F32), pltpu.VMEM((H2, 2 * AW), F32)]
    else:
        def body(q_ref, kc_ref, vc_ref, do_ref, L_ref, dl_ref, ba_ref, bb_ref, prev_in, out_ref, dsa_ref, dsb_ref):
            del prev_in

            @pl.when(first_step())
            def _():
                dsa_ref[...] = jnp.zeros_like(dsa_ref)
                dsb_ref[...] = jnp.zeros_like(dsb_ref)

            def emit(sl, dq, dk, dv, dk_prev, dv_prev):
                _st(out_ref, dq, sl)
                _st(out_ref, dk, slice(AW + sl.start, AW + sl.stop))
                _st(out_ref, dv, slice(2 * AW + sl.start, 2 * AW + sl.stop))

            heads(q_ref, None, kc_ref, None, vc_ref, do_ref, L_ref, dl_ref, ba_ref, bb_ref, None, dsa_ref, dsb_ref,
                  emit)

        cur = lambda n: n
        in_specs = [geom.spec(AW, cq, cur), geom.spec(AW, ck, cur), geom.spec(AW, cv, cur),
                    geom.spec(AW, 0, cur), geom.spec(AW, 0, cur), geom.spec(AW, 0, cur)]
        args = [qv] * 3 + [geom.view(dattn), geom.view(Lb), geom.view(dlb)]
        out_spec = geom.spec(W3, gi, cur)
        grid = geom.outer
        scratch = []

    whole = lambda b: pl.BlockSpec(b.shape, lambda *ids: (0, 0, 0))
    biases = [b.reshape(NH // 2, 2 * H2, b.shape[2]) for b in biases]
    in_specs += [whole(b) for b in biases]
    in_specs.append(pl.BlockSpec(memory_space=pl.ANY))
    if dqkv_prev is None:
        last, aliases = biases[0], {}
    else:
        last, aliases = geom.view(dqkv_prev), {len(in_specs) - 1: 0}
    dq, dsa, dsb = pl.pallas_call(
        body, name=f"attn_bwd{gi}", grid=grid, in_specs=in_specs,
        out_specs=[out_spec] + [whole(b) for b in biases],
        out_shape=[_sds(geom.view_shape(QKV_W), CDT)] + [_sds(b.shape, F32) for b in biases],
        scratch_shapes=scratch,
        input_output_aliases=aliases,
        compiler_params=_cp(("arbitrary",) * len(grid), vmem=56),
    )(*args, *biases, last)
    return geom.unview(dq), [d.reshape(NH, H2, d.shape[2]) for d in (dsa, dsb)]


def _h_bwd(dh, dx2, x, norm_g, scale):
    S = x.shape[0]

    def body(dh_ref, dx2_ref, x_ref, g_ref, sc_ref, gx_ref, dsh_ref, dsc_ref, dng_ref, nat_ref):
        i = pl.program_id(0)

        @pl.when(i == 0)
        def _():
            dsh_ref[...] = jnp.zeros_like(dsh_ref)
            dsc_ref[...] = jnp.zeros_like(dsc_ref)
            dng_ref[...] = jnp.zeros_like(dng_ref)

        for r in range(R16):
            nat_ref[:, r, :] = dh_ref[r]
        dh = jnp.concatenate([nat_ref[l] for l in range(R16)], axis=0)
        g = g_ref[...]
        one_sc = 1.0 + sc_ref[...]
        xv = x_ref[...]
        r1 = lax.rsqrt(jnp.mean(xv * xv, axis=-1, keepdims=True) + EPS)
        xn = xv * r1
        dhx = dh * xn
        dxn = dh * (g * one_sc)
        dx1 = r1 * (dxn - xn * jnp.mean(dxn * xn, axis=-1, keepdims=True))
        gx_ref[...] = dx2_ref[...] + dx1
        dsh_ref[...] += jnp.sum(dh, axis=0, keepdims=True)
        dsc_ref[...] += jnp.sum(dhx * g, axis=0, keepdims=True)
        dng_ref[...] += jnp.sum(dhx * one_sc, axis=0, keepdims=True)

    nat = pl.BlockSpec((TOK, D), lambda i: (i, 0))
    vec = pl.BlockSpec((1, D), lambda i: (0, 0))
    return pl.pallas_call(
        body, name="h_bwd", grid=(S // TOK,),
        in_specs=[pl.BlockSpec((R16, R16, D), lambda i: (0, i, 0)), nat, nat, vec, vec],
        out_specs=[nat, vec, vec, vec],
        out_shape=[_sds((S, D), F32), _sds((1, D), F32), _sds((1, D), F32), _sds((1, D), F32)],
        scratch_shapes=[pltpu.VMEM((R16, R16, D), F32)],
        compiler_params=_cp(("arbitrary",)),
    )(dh, dx2, x, norm_g, scale)


def _local_step(x, target, shift, scale, gate, norm_g, w_in_full, pool_w, pool_scale, wab, wpb, wout, rel_bias,
                final_g):
    st = _local_bias_tables(_local_pre(x, shift, scale, norm_g), rel_bias)
    st = _local_grads(st, target, gate, w_in_full, pool_w, pool_scale, wab, wpb, wout, final_g)
    d_rel_bias = _local_bias_grads(st, _BIAS_GRADS_EARLY + _BIAS_GRADS_LATE)
    gx, dmod, dng = _local_input_grads(st, w_in_full, norm_g, scale)
    return dict(loss=st["loss"], grad_x=gx, dw_in=st["dw_in"], dw_ab=st["dw_ab"], dw_pb=st["dw_pb"],
                dw_out=st["dw_out"], d_norm_g=dng, d_pool_w=st["dpw"], d_pool_scale=st["dps"],
                d_rel_bias=d_rel_bias.T, d_final_g=st["gfg"], dmod=dmod)


def _local_pre(x, shift, scale, norm_g, dep=None):
    S = x.shape[0]
    h = _norm_mod(x, norm_g, shift, scale, dep).reshape(S, D)
    return dict(x=x, h=h, ht=_tcast(h, "h_transpose"))


def _local_bias_tables(st, rel_bias, dep=None):
    LS = st["x"].shape[0] // R16
    geoms = [_Geom(gi, LS) for gi in range(3)]
    idx_np = [[g.bucket_index(sub) for sub in range(2)] for g in geoms]
    buckets = [[tuple(int(b) for b in np.unique(ix) if b >= 0) for ix in pair] for pair in idx_np]
    idxs = [[jnp.asarray(ix) for ix in pair] for pair in idx_np]
    biases = [[_bias_table(idxs[gi][sub], buckets[gi][sub], rel_bias, gi, sub, dep) for sub in range(2)]
              for gi in range(3)]
    return dict(st, geoms=geoms, idxs=idxs, buckets=buckets, biases=biases)


def _local_grads(st, target, gate, w_in_full, pool_w, pool_scale, wab, wpb, wout, final_g):
    geoms, idxs, biases, x, h, ht = (st[k] for k in ("geoms", "idxs", "biases", "x", "h", "ht"))
    S = h.shape[0]
    LS = S // R16
    qkv, rest = _proj(h, w_in_full)

    os_, ls_ = [], []
    for gi in range(2):
        o, l = _attn_fwd(qkv, biases[gi], geoms[gi])
        os_.append(o)
        ls_.append(l)
    attn, Lb, ag = _attn_fwd(qkv, biases[2], geoms[2], others=(os_, ls_, rest))
    pt, mp, pg = _pool_fwd(rest, pool_w, pool_scale)
    dx2, dmo3, loss, gfg, dgate, dw_out = _merge_out_loss(ag, pg, rest, wab, wpb, wout, x, target, gate, final_g)
    dmo = dmo3.reshape(S, D)

    dw_ab, dw_pb, drest, dattn, dlb, dmp, dps = _bwd_gates(dmo, wout, wab, wpb, rest, ag, pg, attn, mp, pool_scale)
    drest, dpw = _pool_bwd(dmp, pt, pool_w, drest)

    dqkv = None
    dsums = []
    for gi in range(3):
        dqkv, dsum = _attn_bwd(qkv, dattn, Lb, dlb, biases[gi], geoms[gi], dqkv)
        dsums.append(dsum)

    dw_in = _mm_tn(ht, dqkv, n_total=8192, out_blk=lambda j: j, name="dw_in_qkv", with_low=True)
    dw_in, dw_in_low = _mm_tn(ht, drest, n_total=8192, out_blk=_rest_blk, name="dw_in_rest", prev=dw_in, with_low=True)
    return dict(st, loss=loss[0, 0], dw_in=dw_in, dw_in_low=dw_in_low, dw_ab=dw_ab, dw_pb=dw_pb, dw_out=dw_out, dpw=dpw, dps=dps, gfg=gfg,
                dgate=dgate, dsums=dsums, dqkv=dqkv, drest=drest, dx2=dx2)


def _local_bias_grads(st, which, dep=None):
    rows = []
    for gi in range(3):
        got = [_bias_grad(st["dsums"][gi][sub], st["idxs"][gi][sub], st["buckets"][gi][sub], gi, sub, dep)
               for sub in range(2) if (gi, sub) in which]
        rows.append(sum(got) if got else jnp.zeros((NH, 32), F32))
    return jnp.concatenate(rows, axis=0)


_BIAS_GRADS_EARLY = ((0, 0), (0, 1), (1, 0), (2, 0))
_BIAS_GRADS_LATE = ((1, 1), (2, 1))


def _dh_h_bwd(dqkv, drest, w, dx2, x, norm_g, scale, dep=None):
    S = x.shape[0]
    dep_specs, dep_args = _after(dep)

    def body(a_ref, b_ref, w_ref, dx2_ref, x_ref, g_ref, sc_ref, *rest):
        gx_ref, dsh_ref, dsc_ref, dng_ref, nat_ref = rest[-5:]
        i = pl.program_id(0)

        @pl.when(i == 0)
        def _():
            dsh_ref[...] = jnp.zeros_like(dsh_ref)
            dsc_ref[...] = jnp.zeros_like(dsc_ref)
            dng_ref[...] = jnp.zeros_like(dng_ref)

        a, b = _ld(a_ref), _ld(b_ref)
        acc = _dot(a[:, 0:512], _w512(w_ref, 0), NT)
        for j in range(1, 9):
            acc = acc + _dot(a[:, 512 * j:512 * (j + 1)], _w512(w_ref, j), NT)
        off = 0
        for lo, hi in REST_COLS:
            for j in range(lo, hi):
                acc = acc + _dot(b[:, off:off + 512], _w512(w_ref, j), NT)
                off += 512
        for r in range(R16):
            nat_ref[:, r, :] = acc[R16 * r:R16 * (r + 1)]
        dh = jnp.concatenate([nat_ref[l] for l in range(R16)], axis=0)
        g = g_ref[...]
        one_sc = 1.0 + sc_ref[...]
        xv = x_ref[...]
        r1 = lax.rsqrt(jnp.mean(xv * xv, axis=-1, keepdims=True) + EPS)
        xn = xv * r1
        dhx = dh * xn
        dxn = dh * (g * one_sc)
        dx1 = r1 * (dxn - xn * jnp.mean(dxn * xn, axis=-1, keepdims=True))
        gx_ref[...] = dx2_ref[...] + dx1
        dsh_ref[...] += jnp.sum(dh, axis=0, keepdims=True)
        dsc_ref[...] += jnp.sum(dhx * g, axis=0, keepdims=True)
        dng_ref[...] += jnp.sum(dhx * one_sc, axis=0, keepdims=True)

    nat = pl.BlockSpec((TOK, D), lambda i: (i, 0))
    vec = pl.BlockSpec((1, D), lambda i: (0, 0))
    slab = lambda W: pl.BlockSpec((R16, R16, W), lambda i: (0, i, 0))
    return pl.pallas_call(
        body, name="dh_h_bwd", grid=(S // TOK,),
        in_specs=[slab(QKV_W), slab(REST_W), _resident((8, D, 1024)), nat, nat, vec, vec] + dep_specs,
        out_specs=[nat, vec, vec, vec],
        out_shape=[_sds((S, D), F32), _sds((1, D), F32), _sds((1, D), F32), _sds((1, D), F32)],
        scratch_shapes=[pltpu.VMEM((R16, R16, D), F32)],
        compiler_params=_cp(("arbitrary",), vmem=56),
    )(dqkv, drest, w, dx2, x, norm_g, scale, *dep_args)


def _local_input_grads(st, w_in_full, norm_g, scale, dep=None):
    S = st["h"].shape[0]
    LS = S // R16
    gx, dsh, dsc, dng = _dh_h_bwd(st["dqkv"].reshape(R16, LS, QKV_W), st["drest"].reshape(R16, LS, REST_W),
                                  w_in_full, st["dx2"], st["x"], norm_g, scale, dep)
    return gx, jnp.concatenate([dsh, dsc, st["dgate"]], axis=1), dng


def _my_place():
    return lax.axis_index("x"), lax.axis_index("y"), lax.axis_index("c")


def _block_of(ref, axis, idx, width):
    if axis is None:
        return ref.at[idx]
    start = pl.multiple_of(idx * width, width)
    if axis == 0:
        return ref.at[pl.ds(start, width)]
    return ref.at[:, pl.ds(start, width)]


_SEM = pl.BlockSpec(memory_space=pltpu.SEMAPHORE)
_EFFECT = pltpu.SideEffectType.DATAFLOW_SIDE_EFFECTING


def _split_start(name, srcs, lands, ncopy, plan, dep=None):
    arrs = list(srcs) + list(lands)
    na, ns = len(arrs), len(srcs)
    dep_specs, dep_args = _after(dep)

    def body(*refs):
        send_sems, recv_sems, token = refs[na + len(dep_args)], refs[na + len(dep_args) + 1], refs[-1]
        for k, (s, d, dev) in enumerate(plan(refs[:ns], refs[ns:na])):
            pltpu.make_async_remote_copy(src_ref=s, dst_ref=d, send_sem=send_sems.at[k], recv_sem=recv_sems.at[k],
                                         device_id=dev, device_id_type=MESH).start()
        token[...] = jnp.zeros_like(token)

    outs = pl.pallas_call(
        body, name=name,
        out_shape=(pltpu.SemaphoreType.DMA((ncopy,)), pltpu.SemaphoreType.DMA((ncopy,)),
                   *[pltpu.HBM(a.shape, a.dtype) for a in arrs], _sds((8, 128), F32)),
        in_specs=[_HBM] * na + dep_specs,
        out_specs=(_SEM, _SEM, *[_HBM] * na, pl.BlockSpec(memory_space=pltpu.VMEM)),
        input_output_aliases={i: 2 + i for i in range(na)},
        compiler_params=pltpu.CompilerParams(has_side_effects=_EFFECT),
    )(*[pltpu.with_memory_space_constraint(a, pltpu.HBM) for a in arrs], *dep_args)
    return (outs[0], outs[1]), list(outs[2:2 + ns]), list(outs[2 + ns:2 + na]), outs[-1]


def _split_wait(name, sems, srcs, lands, plan, after):
    arrs = list(srcs) + list(lands)
    na, ns = len(arrs), len(srcs)
    after = list(after) if isinstance(after, (list, tuple)) else [after]

    def body(*refs):
        send_sems, recv_sems = refs[na], refs[na + 1]
        for k, (s, d, dev) in enumerate(plan(refs[:ns], refs[ns:na])):
            cp = pltpu.make_async_remote_copy(src_ref=s, dst_ref=d, send_sem=send_sems.at[k], recv_sem=recv_sems.at[k],
                                              device_id=dev, device_id_type=MESH)
            cp.wait_send()
            cp.wait_recv()

    outs = pl.pallas_call(
        body, name=name, out_shape=[pltpu.HBM(a.shape, a.dtype) for a in arrs],
        in_specs=[_HBM] * na + [_SEM, _SEM] + [pl.BlockSpec(memory_space=pl.ANY)] * len(after),
        out_specs=[_HBM] * na,
        input_output_aliases={i: i for i in range(na)},
        compiler_params=pltpu.CompilerParams(has_side_effects=_EFFECT),
    )(*arrs, sems[0], sems[1], *after)
    return list(outs[:ns]), list(outs[ns:])


def _gather_chips_plan(layout):
    def plan(src, land):
        x, y, c = _my_place()
        me = 4 * x + 2 * y + c
        mine = [_block_of(land[p], layout[p][0], me, layout[p][1]) for p in range(len(land))]
        return [(mine[p], mine[p], (cx, cy, c)) for p in range(len(land)) for cx, cy in [(1 - x, y), (x, 1 - y)]]
    return plan


def _pair_plan(layout):
    def plan(src, land):
        x, y, c = _my_place()
        n = len(layout)
        return [(_block_of(src[p], layout[p][0], 2 * q + (1 - c), layout[p][1]), land[p].at[q], (x, y, 1 - c))
                for p in range(n) for q in range(4)] + [(src[p], land[p], (x, y, 1 - c)) for p in range(n, len(src))]
    return plan


def _chips_plan(n):
    def plan(src, land):
        x, y, c = _my_place()
        chips = [(1 - x, y), (x, 1 - y), (1 - x, 1 - y)]
        return [(src[p].at[2 * cx + cy], land[p].at[2 * x + y], (cx, cy, c)) for p in range(n) for cx, cy in chips] + \
               [(src[p], land[p].at[2 * x + y], (cx, cy, c)) for p in range(n, len(src)) for cx, cy in chips]
    return plan


def _pair_share(gathered):
    n = len(gathered)

    def body(*refs):
        land = refs[n:2 * n]
        send_sems, recv_sems, fwd_send, fwd_recv = refs[2 * n:]
        x, y, c = _my_place()
        k_dg = 3 - (2 * x + y)
        south = c == 0
        pass_on = 4 * jnp.where(south, 1 - x, x) + 2 * jnp.where(south, y, 1 - y) + c
        to = (jnp.where(south, x, 1 - x), jnp.where(south, 1 - y, y), c)

        def forward(p):
            return pltpu.make_async_remote_copy(
                src_ref=land[p].at[pass_on], dst_ref=land[p].at[pass_on], send_sem=fwd_send.at[p],
                recv_sem=fwd_recv.at[p], device_id=to, device_id_type=MESH)

        def to_sibling(p, k):
            blk = land[p].at[2 * k + c]
            return pltpu.make_async_remote_copy(
                src_ref=blk, dst_ref=blk, send_sem=send_sems.at[4 * p + k], recv_sem=recv_sems.at[4 * p + k],
                device_id=(x, y, 1 - c), device_id_type=MESH)

        for p in range(n):
            forward(p).start()
        for p in range(n):
            for k in range(4):
                @pl.when(k != k_dg)
                def _():
                    to_sibling(p, k).start()
        for p in range(n):
            dg = land[p].at[2 * k_dg + c]
            pltpu.make_async_remote_copy(src_ref=dg, dst_ref=dg, send_sem=fwd_send.at[p], recv_sem=fwd_recv.at[p],
                                         device_id=to, device_id_type=MESH).wait_recv()
            for k in range(4):
                @pl.when(k == k_dg)
                def _():
                    to_sibling(p, k).start()
        for p in range(n):
            forward(p).wait_send()
            for k in range(4):
                to_sibling(p, k).wait_send()
                blk = land[p].at[2 * k + 1 - c]
                pltpu.make_async_remote_copy(
                    src_ref=blk, dst_ref=blk, send_sem=send_sems.at[4 * p + k], recv_sem=recv_sems.at[4 * p + k],
                    device_id=(x, y, 1 - c), device_id_type=MESH).wait_recv()

    return pl.pallas_call(
        body, name="gather_pair_share", in_specs=[_HBM] * n, out_specs=[_HBM] * n,
        out_shape=[_sds(g.shape, g.dtype) for g in gathered],
        input_output_aliases={p: p for p in range(n)},
        scratch_shapes=[pltpu.SemaphoreType.DMA((4 * n,)), pltpu.SemaphoreType.DMA((4 * n,)),
                        pltpu.SemaphoreType.DMA((n,)), pltpu.SemaphoreType.DMA((n,))],
    )(*gathered)


def _place_slot(arrs, slot, n_slots, name, out_dtype=None):
    n = len(arrs)

    def body(s_ref, *refs):
        for p in range(n):
            refs[n + p][...] = refs[p][...].astype(refs[n + p].dtype)

    def spec(a):
        tail = a.shape[-2:]
        return pl.BlockSpec((None,) + tail, lambda i, s_ref: (s_ref[0], 0, 0))

    in_specs = [spec(a) if a.ndim == 3 else pl.BlockSpec(a.shape, lambda i, s_ref: (0, 0)) for a in arrs]
    return pl.pallas_call(
        body, name=name,
        grid_spec=pltpu.PrefetchScalarGridSpec(num_scalar_prefetch=1, grid=(1,), in_specs=in_specs,
                                               out_specs=[spec(a) for a in arrs]),
        out_shape=[_in_hbm((n_slots,) + a.shape[-2:], out_dtype or a.dtype) for a in arrs],
        compiler_params=_cp(("arbitrary",)),
    )(slot, *[_pin(a) for a in arrs])


def _cols_from_blocks(w8, name):
    _, R, C = w8.shape

    def body(w_ref, o_ref):
        for e in range(8):
            o_ref[:, C * e:C * (e + 1)] = w_ref[e]

    return pl.pallas_call(body, name=name, out_shape=_sds((R, 8 * C), w8.dtype), compiler_params=_cp())(w8)


def _all_gather_small(srcs, name):
    n = len(srcs)
    vmem = pl.BlockSpec(memory_space=pltpu.VMEM)

    def body(*refs):
        src, out = refs[:n], refs[n:2 * n]
        send_sems, recv_sems = refs[2 * n:]
        x, y, c = _my_place()
        me = 4 * x + 2 * y + c
        flip = lambda v, f: 1 - v if f else v
        peers = [(flip(x, fx), flip(y, fy), flip(c, fc)) for fx in (0, 1) for fy in (0, 1) for fc in (0, 1)][1:]
        cps = []
        for p in range(n):
            out[p][me] = src[p][...]
            for k, peer in enumerate(peers):
                cps.append(pltpu.make_async_remote_copy(
                    src_ref=src[p], dst_ref=out[p].at[me], send_sem=send_sems.at[7 * p + k],
                    recv_sem=recv_sems.at[7 * p + k], device_id=peer, device_id_type=MESH))
        for cp in cps:
            cp.start()
        for cp in cps:
            cp.wait()

    return pl.pallas_call(
        body, name=name, in_specs=[vmem] * n, out_specs=[vmem] * n,
        out_shape=[_sds((8,) + a.shape, a.dtype) for a in srcs],
        scratch_shapes=[pltpu.SemaphoreType.DMA((7 * n,)), pltpu.SemaphoreType.DMA((7 * n,))],
    )(*srcs)


def _adaln_exchange(c, w_ada, b_part):
    vmem = pl.BlockSpec(memory_space=pltpu.VMEM)

    def body(c_ref, w_ref, b_ref, call_ref, mod_ref, part_ref, send_sems, recv_sems):
        x, y, c = _my_place()
        me = 4 * x + 2 * y + c
        flip = lambda v, f: 1 - v if f else v
        peers = [(flip(x, fx), flip(y, fy), flip(c, fc)) for fx in (0, 1) for fy in (0, 1) for fc in (0, 1)][1:]
        call_ref[me] = c_ref[...]
        first = [pltpu.make_async_remote_copy(src_ref=c_ref, dst_ref=call_ref.at[me], send_sem=send_sems.at[k],
                                              recv_sem=recv_sems.at[k], device_id=peer, device_id_type=MESH)
                 for k, peer in enumerate(peers)]
        for cp in first:
            cp.start()
        for cp in first:
            cp.wait()
        c_all = jnp.concatenate([call_ref[j] for j in range(8)], axis=0)
        part_ref[...] = _dot(c_all.astype(CDT), w_ref[...].astype(CDT), NN) + b_ref[...]
        mod_ref[me] = part_ref[pl.ds(me, 1), :]
        second = [pltpu.make_async_remote_copy(
            src_ref=part_ref.at[pl.ds(4 * px + 2 * py + pc, 1)], dst_ref=mod_ref.at[me], send_sem=send_sems.at[7 + k],
            recv_sem=recv_sems.at[7 + k], device_id=(px, py, pc), device_id_type=MESH)
            for k, (px, py, pc) in enumerate(peers)]
        for cp in second:
            cp.start()
        for cp in second:
            cp.wait()

    return pl.pallas_call(
        body, name="adaln_exchange", in_specs=[vmem] * 3, out_specs=[vmem] * 2,
        out_shape=[_sds((8, 1, D), F32), _sds((8, 1, w_ada.shape[1]), F32)],
        scratch_shapes=[pltpu.VMEM((8, w_ada.shape[1]), F32), pltpu.SemaphoreType.DMA((14,)),
                        pltpu.SemaphoreType.DMA((14,))],
        compiler_params=_cp(),
    )(c, w_ada, b_part)


def _pair_add(dw, recv, axis, width, cidx, name, dep=None):
    dep_specs, dep_args = _after(dep)
    if axis == 0:
        tr = min(width, 256)
        C = dw.shape[1]
        grid = (4, width // tr)
        nb = width // tr
        own = pl.BlockSpec((tr, C), lambda q, i, c_ref: ((2 * q + c_ref[0]) * nb + i, 0))
        oth = pl.BlockSpec((None, tr, C), lambda q, i, c_ref: (q, i, 0))
    else:
        R = dw.shape[0]
        tr = min(R, 1024)
        grid = (4, R // tr)
        own = pl.BlockSpec((tr, width), lambda q, i, c_ref: (i, 2 * q + c_ref[0]))
        oth = pl.BlockSpec((None, tr, width), lambda q, i, c_ref: (q, i, 0))

    def body(c_ref, a_ref, b_ref, *rest):
        o_ref = rest[-1]
        o_ref[...] = (a_ref[...] + b_ref[...].astype(F32)).astype(o_ref.dtype)

    return pl.pallas_call(
        body, name=name,
        grid_spec=pltpu.PrefetchScalarGridSpec(num_scalar_prefetch=1, grid=grid, in_specs=[own, oth] + dep_specs,
                                               out_specs=oth),
        out_shape=(_in_hbm if _is_big(dw) else _sds)(recv.shape, CDT), compiler_params=_cp(("parallel", "parallel")),
    )(cidx, *[_pin(a) if _is_big(dw) else a for a in (dw, recv)], *dep_args)


def _add_pairs(xs, ys, name):
    n = len(xs)

    def body(*refs):
        for p in range(n):
            refs[2 * n + p][...] = refs[p][...] + refs[n + p][...]

    return pl.pallas_call(body, name=name, out_shape=[_sds(a.shape, F32) for a in xs])(*xs, *ys)


def _adam_math(w, g, m, v):
    m = ADAM_B1 * m + (1.0 - ADAM_B1) * g
    v = ADAM_B2 * v + (1.0 - ADAM_B2) * (g * g)
    m_hat = m / (1.0 - ADAM_B1 ** ADAM_STEP)
    v_hat = v / (1.0 - ADAM_B2 ** ADAM_STEP)
    delta = -ADAM_LR * (m_hat / (jnp.sqrt(v_hat) + ADAM_EPS) + ADAM_WD * w)
    return delta, m, v


def _adam(gparts, w, m, v, name, fresh=False):
    P, R, C = gparts.shape
    in_hbm = fresh or _is_big(w)
    tr = R if R <= 256 else 256

    def body(g_ref, w_ref, m_ref, v_ref, go_ref, d_ref, mo_ref, vo_ref):
        g = g_ref[0].astype(F32)
        for i in range(1, P):
            g = g + g_ref[i].astype(F32)
        d, mn, vn = _adam_math(w_ref[...], g, m_ref[...], v_ref[...])
        go_ref[...] = g
        d_ref[...] = d
        mo_ref[...] = mn
        vo_ref[...] = vn

    t = pl.BlockSpec((tr, C), lambda i: (i, 0))
    return pl.pallas_call(
        body, name=name, grid=(R // tr,),
        in_specs=[pl.BlockSpec((P, tr, C), lambda i: (0, i, 0)), t, t, t], out_specs=[t, t, t, t],
        out_shape=[(_in_hbm if in_hbm else _sds)((R, C), F32)] * 4, compiler_params=_cp(("parallel",)),
    )(*[_pin(a) if in_hbm else a for a in (gparts, w, m, v)])


def _sum_parts(gparts, name):
    P, R, C = gparts.shape

    def body(g_ref, o_ref):
        g = g_ref[0]
        for i in range(1, P):
            g = g + g_ref[i]
        o_ref[...] = g

    return pl.pallas_call(body, name=name, out_shape=_sds((R, C), F32))(gparts)


def _adam_w_ada(c_t, dm, w, m, v):
    R, C = w.shape

    def body(c_ref, dm_ref, w_ref, m_ref, v_ref, go_ref, d_ref, mo_ref, vo_ref):
        g = c_ref[:, 0:1] * dm_ref[0:1, :]
        for b in range(1, 8):
            g = g + c_ref[:, b:b + 1] * dm_ref[b:b + 1, :]
        d, mn, vn = _adam_math(w_ref[...], g, m_ref[...], v_ref[...])
        go_ref[...] = g
        d_ref[...] = d
        mo_ref[...] = mn
        vo_ref[...] = vn

    return pl.pallas_call(body, name="adam_w_ada", out_shape=[_sds((R, C), F32)] * 4,
                          compiler_params=_cp())(c_t, dm, w, m, v)


def kernel(x, c, norm_g, w_ada, b_ada, w_in, pool_w, pool_scale, w_attn_br, w_pool_br, w_out, rel_bias, final_g, loss_target, m_norm_g, m_w_ada, m_b_ada, m_w_in, m_pool_w, m_pool_scale, m_w_attn_br, m_w_pool_br, m_w_out, m_rel_bias, m_final_g, v_norm_g, v_w_ada, v_b_ada, v_w_in, v_pool_w, v_pool_scale, v_w_attn_br, v_w_pool_br, v_w_out, v_rel_bias, v_final_g):
    S = x.shape[1]
    px, py, pc = _my_place()
    me = 4 * px + 2 * py + pc
    cidx = jnp.reshape(pc, (1,)).astype(jnp.int32)

    layout = [(1, 1024), (1, 128), (1, 128), (0, 128)]
    final_g2 = final_g.reshape(1, D)
    me1 = jnp.reshape(me, (1,)).astype(jnp.int32)

    b_part = lax.dynamic_slice(b_ada, (0, me * 384), (1, 384))
    c_all, mod_all = _adaln_exchange(c, w_ada[0], b_part)
    c_all = c_all.reshape(8, D)
    mod = mod_all.reshape(1, 3 * D)
    shift, scale, gate = mod[:, :D], mod[:, D:2 * D], mod[:, 2 * D:]

    gplan = _gather_chips_plan([(None, 1)] * 4)
    lands = _place_slot([w_in[0], w_attn_br[0], w_pool_br[0], w_out[0]], me1, 8, "place_own_shards", CDT)
    g_sems, _, lands, tok = _split_start("gather_chips_start", [], lands, 8, gplan, dep=mod_all)
    st = _local_bias_tables(_local_pre(x[0], shift, scale, norm_g, dep=tok), rel_bias, dep=tok)
    _, lands = _split_wait("gather_chips_wait", g_sems, [], lands, gplan,
                           [st["ht"]] + [b for pair in st["biases"] for b in pair])
    w_in_full, wab8, wpb8, wout8 = _pair_share(lands)
    wab, wpb = _cols_from_blocks(wab8, "w_attn_br_cols"), _cols_from_blocks(wpb8, "w_pool_br_cols")
    wout = wout8.reshape(D, D)

    st = _local_grads(st, loss_target[0], gate, w_in_full, pool_w[0], pool_scale, wab, wpb, wout, final_g2)

    early = [st["dpw"].reshape(AW, 128), st["dps"], st["gfg"], jnp.broadcast_to(st["loss"], (1, 128))]
    dws = [st["dw_in"], st["dw_ab"], st["dw_pb"], st["dw_out"]]
    sent = [st["dw_in_low"]] + dws[1:] + early
    pplan = _pair_plan(layout)
    pair_shapes = [(4, w, a.shape[1]) if ax == 0 else (4, a.shape[0], w) for a, (ax, w) in zip(dws, layout)]
    p_sems, sent, recv, tok = _split_start(
        "reduce_pair_start", sent,
        [lax.empty(s, a.dtype) for s, a in zip(pair_shapes + [a.shape for a in early], sent)], 20, pplan)
    d_rel_bias = _local_bias_grads(st, _BIAS_GRADS_EARLY, dep=tok)
    sent, recv = _split_wait("reduce_pair_wait", p_sems, sent, recv, pplan, d_rel_bias)
    slot1 = jnp.reshape(2 * px + py, (1,)).astype(jnp.int32)
    part_in = [_pair_add(dws[0], recv[0], *layout[0], cidx, "pair_add0")]
    wplan, cplan = _chips_plan(1), _chips_plan(3)
    w_sems, part_in, land_in, tok = _split_start("reduce_chips_start_w_in", part_in,
                                                 _place_slot(part_in, slot1, 4, "place_own_partial_w_in"), 3, wplan)
    parts = [_pair_add(dw, r, ax, w, cidx, f"pair_add{i}", dep=tok)
             for i, dw, r, (ax, w) in zip((1, 2, 3), sent[1:4], recv[1:4], layout[1:])]
    parts += _add_pairs(sent[4:], recv[4:], "pair_add_smalls")
    c_sems, parts, lands, tok = _split_start("reduce_chips_start", parts,
                                             _place_slot(parts, slot1, 4, "place_own_partials"), 21, cplan)
    d_rel_bias = d_rel_bias + _local_bias_grads(st, _BIAS_GRADS_LATE, dep=tok)
    grad_x, dmod, d_norm_g = _local_input_grads(st, w_in_full, norm_g, scale, dep=tok)

    dmod_all, ng_all, rb_all = _all_gather_small([dmod, d_norm_g, d_rel_bias], "gather_late")
    o_norm_g = _adam(ng_all, norm_g, m_norm_g, v_norm_g, "adam_norm_g", fresh=True)
    o_rel_bias_t = _adam(rb_all, rel_bias.T, m_rel_bias.T, v_rel_bias.T, "adam_rel_bias", fresh=True)
    o_rel_bias = [o.T for o in o_rel_bias_t]
    o_b_ada = _adam(dmod_all, b_ada, m_b_ada, v_b_ada, "adam_b_ada", fresh=True)
    dm_mine = lax.dynamic_slice(dmod_all.reshape(8, 3 * D), (0, me * 384), (8, 384))
    o_w_ada = [o[None] for o in _adam_w_ada(c_all.T, dm_mine, w_ada[0], m_w_ada[0], v_w_ada[0])]

    late = [o_w_ada[0], o_b_ada[0], o_norm_g[0], o_rel_bias_t[0]]
    _, (g_in,) = _split_wait("reduce_chips_wait_w_in", w_sems, part_in, land_in, wplan, late)
    _, (g_ab, g_pb, g_out, g_pw, g_ps, g_fg, g_loss) = _split_wait("reduce_chips_wait", c_sems, parts, lands, cplan, late)
    o_w_in = [o[None] for o in _adam(g_in, w_in[0], m_w_in[0], v_w_in[0], "adam_w_in")]
    o_w_ab = [o[None] for o in _adam(g_ab, w_attn_br[0], m_w_attn_br[0], v_w_attn_br[0], "adam_w_attn_br")]
    o_w_pb = [o[None] for o in _adam(g_pb, w_pool_br[0], m_w_pool_br[0], v_w_pool_br[0], "adam_w_pool_br")]
    o_w_out = [o[None] for o in _adam(g_out, w_out[0], m_w_out[0], v_w_out[0], "adam_w_out")]
    o_pool_w = [o.reshape(pool_w.shape) for o in _adam(g_pw, pool_w.reshape(AW, 128), m_pool_w.reshape(AW, 128),
                                                       v_pool_w.reshape(AW, 128), "adam_pool_w")]
    o_pool_scale = _adam(g_ps, pool_scale, m_pool_scale, v_pool_scale, "adam_pool_scale")
    o_final_g = [o.reshape(D) for o in _adam(g_fg, final_g2, m_final_g.reshape(1, D), v_final_g.reshape(1, D),
                                             "adam_final_g")]
    loss = _sum_parts(g_loss, "sum_loss")[0, 0]

    per_w = [o_norm_g, o_w_ada, o_b_ada, o_w_in, o_pool_w, o_pool_scale, o_w_ab, o_w_pb, o_w_out, o_rel_bias, o_final_g]
    outs = [loss, grad_x[None]]
    for k in range(4):
        outs += [o[k] for o in per_w]
    return tuple(outs)
```

```python
import functools
import math

import numpy as np
import jax
import jax.numpy as jnp
from jax import lax
from jax.experimental import pallas as pl
from jax.experimental.pallas import tpu as pltpu

F32 = jnp.float32
CDT = jnp.bfloat16
D = 1024
HD = 64
NH = 8
AW = 512
QKV_W = 4608
REST_W = 3584
R16 = 16
EPS = 1e-6
NEG = -1e30
N_BACK = 128
ADAM_LR, ADAM_B1, ADAM_B2, ADAM_EPS, ADAM_WD, ADAM_STEP = 0.001, 0.9, 0.999, 1e-08, 0.01, 10
MESH = pl.DeviceIdType.MESH
VMEM_MB = 1024 * 1024


def _cp(sem=None, vmem=48, **kw):
    if sem is not None:
        kw["dimension_semantics"] = sem
    return pltpu.CompilerParams(vmem_limit_bytes=vmem * VMEM_MB, **kw)


def _sds(shape, dtype):
    return jax.ShapeDtypeStruct(tuple(shape), dtype)


def _in_hbm(shape, dtype):
    return pltpu.HBM(tuple(shape), dtype)


def _pin(a):
    return pltpu.with_memory_space_constraint(a, pltpu.HBM)


def _is_big(a):
    return a.size * a.dtype.itemsize >= 4 * VMEM_MB


def _rest_blk(jj):
    return jnp.where(jj == 0, 9, jnp.where(jj == 6, 10, jj + 10))


def _sigmoid(z):
    return 0.5 * jnp.tanh(0.5 * z) + 0.5


def _dot(a, b, dims):
    return lax.dot_general(a, b, (dims, ((), ())), preferred_element_type=F32)


NN = ((1,), (0,))
NT = ((1,), (1,))
TN = ((0,), (0,))


def _ld(ref):
    if len(ref.shape) == 2:
        return ref[...]
    return jnp.concatenate([ref[a] for a in range(ref.shape[0])], axis=0)


def _st(ref, val, cols=None):
    val = val.astype(ref.dtype)
    if len(ref.shape) == 2:
        if cols is None:
            ref[...] = val
        else:
            ref[:, cols] = val
        return
    rows = ref.shape[1]
    for a in range(ref.shape[0]):
        if cols is None:
            ref[a] = val[a * rows:(a + 1) * rows]
        else:
            ref[a, :, cols] = val[a * rows:(a + 1) * rows]


REST_COLS = ((9, 10), (11, 16), (10, 11))
_HBM = pl.BlockSpec(memory_space=pltpu.HBM)


def _resident(shape):
    return pl.BlockSpec(shape, lambda *ids: (0,) * len(shape), pipeline_mode=pl.Buffered(1))


def _w512(w_ref, j):
    return w_ref[j // 2, :, 512 * (j % 2):512 * (j % 2 + 1)]


def _proj(h, w):
    S = h.shape[0]
    tm = 512

    def body(a_ref, w_ref, q_ref, r_ref):
        a = a_ref[...]
        for j in range(9):
            q_ref[:, 512 * j:512 * (j + 1)] = _dot(a, _w512(w_ref, j), NN).astype(q_ref.dtype)
        off = 0
        for lo, hi in REST_COLS:
            for j in range(lo, hi):
                r_ref[:, off:off + 512] = _dot(a, _w512(w_ref, j), NN)
                off += 512

    return pl.pallas_call(
        body, name="proj", grid=(S // tm,),
        in_specs=[pl.BlockSpec((tm, D), lambda i: (i, 0)), _resident((8, D, 1024))],
        out_specs=[pl.BlockSpec((tm, QKV_W), lambda i: (i, 0)), pl.BlockSpec((tm, REST_W), lambda i: (i, 0))],
        out_shape=[_sds((S, QKV_W), CDT), _sds((S, REST_W), F32)],
        compiler_params=_cp(("parallel",), vmem=56),
    )(h, w)


def _after(dep):
    return ([], []) if dep is None else ([pl.BlockSpec(memory_space=pl.ANY)], [dep])


def _mm_tn(at, b, *, n_total, out_blk, name, prev=None, tn=512, with_low=False):
    R, S = at.shape
    nb = b.shape[1] // tn
    n_out = 2 if with_low else 1

    def body(*refs):
        res = _dot(refs[0][...], refs[1][...], NN)
        refs[-n_out][...] = res
        if with_low:
            refs[-1][...] = res.astype(CDT)

    in_specs = [_resident((R, S)), pl.BlockSpec((S, tn), lambda j: (0, j))]
    args = [at, b]
    aliases = {}
    if prev is not None:
        prev = list(prev) if with_low else [prev]
        in_specs += [pl.BlockSpec(memory_space=pl.ANY)] * n_out
        args += prev
        aliases = {2 + k: k for k in range(n_out)}
    out_spec = pl.BlockSpec((R, tn), lambda j: (0, out_blk(j)))
    outs = pl.pallas_call(
        body, name=name, grid=(nb,), in_specs=in_specs,
        out_specs=[out_spec] * n_out,
        out_shape=[_sds((R, n_total), F32), _sds((R, n_total), CDT)][:n_out],
        input_output_aliases=aliases,
        compiler_params=_cp(("parallel",)),
    )(*args)
    return tuple(outs) if with_low else outs[0]


def _tcast(a, name):
    M, C = a.shape
    tm = 512

    def body(a_ref, o_ref):
        o_ref[...] = a_ref[...].astype(F32).T.astype(o_ref.dtype)

    return pl.pallas_call(
        body, name=name, grid=(M // tm,),
        in_specs=[pl.BlockSpec((tm, C), lambda i: (i, 0))],
        out_specs=pl.BlockSpec((C, tm), lambda i: (0, i)),
        out_shape=_sds((C, M), CDT), compiler_params=_cp(("parallel",)),
    )(a)


TOK = R16 * R16


def _swap16():
    i = lax.broadcasted_iota(jnp.int32, (TOK, TOK), 0)
    j = lax.broadcasted_iota(jnp.int32, (TOK, TOK), 1)
    return jnp.where(j == jnp.bitwise_and(i, 15) * 16 + jnp.right_shift(i, 4), 1.0, 0.0).astype(CDT)


def _norm_mod(x, norm_g, shift, scale, dep=None):
    S = x.shape[0]
    dep_specs, dep_args = _after(dep)

    def body(x_ref, g_ref, sh_ref, sc_ref, *rest):
        xv = x_ref[...]
        xn = xv * lax.rsqrt(jnp.mean(xv * xv, axis=-1, keepdims=True) + EPS)
        h = (xn * (g_ref[...] * (1.0 + sc_ref[...])) + sh_ref[...]).astype(CDT)
        _st(rest[-1], _dot(_swap16(), h, NN))

    vec = pl.BlockSpec((1, D), lambda i: (0, 0))
    return pl.pallas_call(
        body, name="norm_mod", grid=(S // TOK,),
        in_specs=[pl.BlockSpec((TOK, D), lambda i: (i, 0)), vec, vec, vec] + dep_specs,
        out_specs=pl.BlockSpec((R16, R16, D), lambda i: (0, i, 0)),
        out_shape=_sds((R16, S // R16, D), CDT), compiler_params=_cp(("parallel",)),
    )(x, norm_g, shift, scale, *dep_args)


def _t5_bucket_np(n):
    nf = np.maximum(n, 1).astype(np.float32)
    large = 16 + (np.log(nf / np.float32(16)) / np.float32(math.log(2048 / 16)) * np.float32(16)).astype(np.int32)
    large = np.minimum(large, 31)
    return np.where(n < 16, n, large).astype(np.int32)


class _Geom:
    def __init__(self, gi, LS):
        self.gi, self.LS = gi, LS
        self.dil = (1, 4, 16)[gi]
        if gi == 0:
            self.lead, self.rows, self.n_tiles, self.has_prev = (R16,), 16, LS // 16, True
            a = np.arange(R16)[:, None]
            ll = np.arange(16)[None, :]
            self.pos = (16 * ll + a).reshape(-1)
        elif gi == 1:
            self.lead, self.rows, self.n_tiles, self.has_prev = (4,), 64, LS // 64, True
            a = np.arange(4)[:, None]
            ll = np.arange(64)[None, :]
            self.pos = (4 * ll + a).reshape(-1)
        else:
            self.lead, self.rows, self.n_tiles, self.has_prev = (), LS, 1, False
            self.pos = np.arange(LS)
        self.T = self.pos.shape[0]
        self.Tk = 2 * self.T if self.has_prev else self.T
        self.outer = {0: (), 1: (4,), 2: (R16,)}[gi]

    def view(self, arr):
        S, C = arr.shape
        if self.gi == 1:
            return arr.reshape(4, 4, self.LS, C)
        return arr.reshape(R16, self.LS, C)

    def view_shape(self, C):
        return (4, 4, self.LS, C) if self.gi == 1 else (R16, self.LS, C)

    def unview(self, arr):
        return arr.reshape(R16 * self.LS, arr.shape[-1])

    def spec(self, W, cblk, tile):
        if self.gi == 0:
            return pl.BlockSpec((R16, 16, W), lambda n: (0, tile(n), cblk))
        if self.gi == 1:
            return pl.BlockSpec((4, None, 64, W), lambda r, n: (0, r, tile(n), cblk))
        return pl.BlockSpec((None, self.LS, W), lambda r: (r, 0, cblk))

    def halves(self):
        R = self.rows
        idx = np.arange(self.T).reshape(-1, R)
        return idx[:, :R // 2].reshape(-1), idx[:, R // 2:].reshape(-1)

    def bucket_index(self, sub):
        lo, hi = self.halves()
        pos = self.pos
        if sub == 0:
            pq = pos[lo]
            pk = np.concatenate([pos[hi] - self.T, pos[lo]]) if self.has_prev else np.concatenate([pos[lo], pos[hi]])
        else:
            pq = pos[hi]
            pk = np.concatenate([pos[lo], pos[hi]])
        dist = pq[:, None] - pk[None, :]
        ok = (dist >= 0) & (dist <= N_BACK)
        bucket = _t5_bucket_np(np.clip(dist, 0, N_BACK) * self.dil)
        return np.where(ok, bucket, -1).astype(np.int32)


def _bias_table(idx, buckets, rel_bias, gi, sub, dep=None):
    T, Tk = idx.shape
    tr = 32
    dep_specs, dep_args = _after(dep)

    def body(idx_ref, rb_ref, *rest):
        h = pl.program_id(0)
        ix = idx_ref[...]
        acc = jnp.full(ix.shape, NEG, F32)
        for b in buckets:
            acc = jnp.where(ix == b, rb_ref[b, gi * NH + h], acc)
        rest[-1][...] = acc

    return pl.pallas_call(
        body, name=f"bias_table{gi}{'ab'[sub]}", grid=(NH, T // tr),
        in_specs=[pl.BlockSpec((tr, Tk), lambda h, i: (i, 0)), pl.BlockSpec(memory_space=pltpu.SMEM)] + dep_specs,
        out_specs=pl.BlockSpec((None, tr, Tk), lambda h, i: (h, i, 0)),
        out_shape=_sds((NH, T, Tk), F32), compiler_params=_cp(("parallel", "parallel")),
    )(idx, rel_bias, *dep_args)


def _bias_grad(dsum, idx, buckets, gi, sub, dep=None):
    _, T, Tk = dsum.shape
    dep_specs, dep_args = _after(dep)

    def body(ds_ref, idx_ref, *rest):
        o_ref, r_ref = rest[-2], rest[-1]
        r_ref[...] = jnp.zeros_like(r_ref)
        ix, ds = idx_ref[...], ds_ref[...]
        for b in buckets:
            r_ref[b:b + 1, :] = jnp.sum(jnp.where(ix == b, ds, 0.0), axis=0, keepdims=True)
        o_ref[...] = jnp.broadcast_to(jnp.sum(r_ref[...], axis=1, keepdims=True), (32, 128))

    out = pl.pallas_call(
        body, name=f"bias_grad{gi}{'ab'[sub]}", grid=(NH,),
        in_specs=[pl.BlockSpec((None, T, Tk), lambda h: (h, 0, 0)), pl.BlockSpec((T, Tk), lambda h: (0, 0))] + dep_specs,
        out_specs=pl.BlockSpec((None, 32, 128), lambda h: (h, 0, 0)),
        out_shape=_sds((NH, 32, 128), F32), scratch_shapes=[pltpu.VMEM((32, Tk), F32)],
        compiler_params=_cp(("parallel",)),
    )(dsum, idx, *dep_args)
    return out[:, :, 0]


def _prev_mask(n, T, Tk):
    col = lax.broadcasted_iota(jnp.int32, (1, Tk), 1)
    return jnp.where((col < T) & (n == 0), NEG, 0.0).astype(F32)


def _split_rows(v, R):
    A, half = v.shape[0] // R, R // 2
    via_f32 = v.dtype != F32 and half % 16 != 0
    w = v.astype(F32) if via_f32 else v
    lo = jnp.concatenate([w[a * R:a * R + half] for a in range(A)], axis=0)
    hi = jnp.concatenate([w[a * R + half:(a + 1) * R] for a in range(A)], axis=0)
    return (lo.astype(v.dtype), hi.astype(v.dtype)) if via_f32 else (lo, hi)


def _merge_rows(lo, hi, R):
    half = R // 2
    A = lo.shape[0] // half
    return jnp.concatenate([x[a * half:(a + 1) * half] for a in range(A) for x in (lo, hi)], axis=0)


def _attn_fwd(qkv, biases, geom, others=None):
    S = qkv.shape[0]
    gi, T, R, has_prev = geom.gi, geom.T, geom.rows, geom.has_prev
    H2 = T // 2
    qv = geom.view(qkv)
    cq, ck, cv = 3 * gi, 3 * gi + 1, 3 * gi + 2
    assert others is None or not has_prev

    def body(*refs):
        if has_prev:
            q_ref, kp_ref, kc_ref, vp_ref, vc_ref, ba_ref, bb_ref, o_ref, l_ref = refs
        elif others is not None:
            (q_ref, kc_ref, vc_ref, ba_ref, bb_ref, o0_ref, o1_ref, l0_ref, l1_ref, z_ref,
             attn_ref, L_ref, ag_ref) = refs
        else:
            q_ref, kc_ref, vc_ref, ba_ref, bb_ref, o_ref, l_ref = refs
        q_lo, q_hi = _split_rows(_ld(q_ref), R)
        kc_lo, kc_hi = _split_rows(_ld(kc_ref), R)
        vc_lo, vc_hi = _split_rows(_ld(vc_ref), R)
        k_b = jnp.concatenate([kc_lo, kc_hi], axis=0)
        v_b = jnp.concatenate([vc_lo, vc_hi], axis=0)
        if has_prev:
            n = pl.program_id(len(geom.outer))
            k_a = jnp.concatenate([_split_rows(_ld(kp_ref), R)[1], kc_lo], axis=0)
            v_a = jnp.concatenate([_split_rows(_ld(vp_ref), R)[1], vc_lo], axis=0)
            pm = _prev_mask(n, H2, T)
        else:
            k_a, v_a, pm = k_b, v_b, None
        low = lax.broadcasted_iota(jnp.int32, (1, 2 * HD), 1) < HD
        for j in range(NH // 2):
            cols = slice(2 * HD * j, 2 * HD * (j + 1))
            o_half, l_half = [], []
            for q, k, v, b_ref, mask in ((q_lo, k_a, v_a, ba_ref, pm), (q_hi, k_b, v_b, bb_ref, None)):
                q2 = q[:, cols].astype(F32) * 0.125
                k2, v2 = k[:, cols], v[:, cols]
                qs = jnp.concatenate([jnp.where(low, q2, 0.0), jnp.where(low, 0.0, q2)], axis=0).astype(q.dtype)
                s = _dot(qs, k2, NT) + b_ref[j]
                if mask is not None:
                    s = s + mask
                m = jnp.max(s, axis=1, keepdims=True)
                p = jnp.exp(s - m)
                l = jnp.sum(p, axis=1, keepdims=True)
                pv = _dot(p.astype(v.dtype), v2, NN) / l
                lse = m + jnp.log(l)
                o_half.append(jnp.where(low, pv[:H2], pv[H2:]))
                l_half.append(jnp.where(low, lse[:H2], lse[H2:]))
            o2, lc = _merge_rows(o_half[0], o_half[1], R), _merge_rows(l_half[0], l_half[1], R)
            if others is None:
                _st(o_ref, o2, cols)
                _st(l_ref, lc, cols)
                continue
            la, lb = l0_ref[:, cols], l1_ref[:, cols]
            m = jnp.maximum(jnp.maximum(la, lb), lc)
            ea, eb, ec = jnp.exp(la - m), jnp.exp(lb - m), jnp.exp(lc - m)
            den = ea + eb + ec
            attn = (ea * o0_ref[:, cols] + eb * o1_ref[:, cols] + ec * o2) / den
            attn_ref[:, cols] = attn
            L_ref[:, cols] = m + jnp.log(den)
            z = z_ref[:, cols]
            ag = attn * (z * _sigmoid(z))
            ag_ref[:, cols] = ag.astype(ag_ref.dtype)

    cur = lambda n: n
    prev = lambda n: jnp.maximum(n - 1, 0)
    if has_prev:
        in_specs = [geom.spec(AW, cq, cur), geom.spec(AW, ck, prev), geom.spec(AW, ck, cur),
                    geom.spec(AW, cv, prev), geom.spec(AW, cv, cur)]
        args = [qv] * 5
    else:
        in_specs = [geom.spec(AW, cq, cur), geom.spec(AW, ck, cur), geom.spec(AW, cv, cur)]
        args = [qv] * 3
    nd = len(geom.outer) + (1 if has_prev else 0)
    biases = [b.reshape(NH // 2, 2 * H2, b.shape[2]) for b in biases]
    in_specs += [pl.BlockSpec(b.shape, lambda *ids: (0, 0, 0)) for b in biases]
    grid = geom.outer + ((geom.n_tiles,) if has_prev else ())
    oshape = geom.view_shape(AW)
    tile = geom.spec(AW, 0, cur)
    if others is None:
        o, l = pl.pallas_call(
            body, name=f"attn_fwd{gi}", grid=grid, in_specs=in_specs, out_specs=[tile, tile],
            out_shape=[_sds(oshape, F32), _sds(oshape, F32)],
            compiler_params=_cp(("arbitrary",) * nd),
        )(*args, *biases)
        return geom.unview(o), geom.unview(l)
    os_, ls_, rest = others
    attn, Lb, ag = pl.pallas_call(
        body, name=f"attn_fwd{gi}_merge", grid=grid, in_specs=in_specs + [tile] * 5,
        out_specs=[tile, tile, tile],
        out_shape=[_sds(oshape, F32), _sds(oshape, F32), _sds(oshape, CDT)],
        compiler_params=_cp(("arbitrary",) * nd),
    )(*args, *biases, *[geom.view(a) for a in (*os_, *ls_, rest)])
    return geom.unview(attn), geom.unview(Lb), geom.unview(ag)


def _pool_counts(r, LS, win):
    l = lax.broadcasted_iota(jnp.int32, (LS, 1), 0)
    return jnp.minimum(16 * l + (r + 1), win).astype(F32)


def _pool_fwd(rest, pool_w, pool_scale):
    S = rest.shape[0]
    LS = S // R16
    r3 = rest.reshape(R16, LS, REST_W)
    PG = 128

    def body(u_ref, z_ref, pw_ref, ps_ref, pt_ref, mp_ref, pg_ref, e_ref):
        g = pl.program_id(0)
        win = jnp.left_shift(2, g)
        row = lax.broadcasted_iota(jnp.int32, (LS, PG), 0)
        for r in range(R16):
            e_ref[R16 + r] = u_ref[r]
        for r in range(1, R16):
            e_ref[r] = jnp.where(row == 0, 0.0, pltpu.roll(u_ref[r], 1, 0))

        def double(s):
            for i in range(2 * R16 - 1, 2 * s - 1, -1):
                e_ref[i] = e_ref[i] + e_ref[i - s]

        double(1)
        for k in (1, 2, 3):
            pl.when(g >= k)(functools.partial(double, 2 ** k))
        pw = pw_ref[...].astype(CDT)
        for r in range(R16):
            u = u_ref[r]
            pooled = e_ref[R16 + r] / _pool_counts(r, LS, win) - u
            pc = pooled.astype(CDT)
            mp = _dot(pc, pw, NN)
            z = z_ref[r]
            pg = (mp * ps_ref[...]) * (z * _sigmoid(z))
            mp_ref[r] = mp
            pg_ref[r] = pg.astype(pg_ref.dtype)
            pt_ref[:, r * LS:(r + 1) * LS] = pooled.T.astype(pt_ref.dtype)

    col = lambda off: pl.BlockSpec((R16, LS, PG), lambda g: (0, 0, off + g))
    tsp = pl.BlockSpec((PG, S), lambda g: (g, 0))
    pt, mp, pg = pl.pallas_call(
        body, name="pool_fwd", grid=(4,),
        in_specs=[col(24), col(4), pl.BlockSpec((None, PG, PG), lambda g: (g, 0, 0)),
                  pl.BlockSpec((1, PG), lambda g: (0, g))],
        out_specs=[tsp, col(0), col(0)],
        out_shape=[_sds((AW, S), CDT), _sds((R16, LS, AW), F32), _sds((R16, LS, AW), CDT)],
        scratch_shapes=[pltpu.VMEM((2 * R16, LS, PG), F32)],
        compiler_params=_cp(("parallel",)),
    )(r3, r3, pool_w, pool_scale)
    return pt, mp.reshape(S, AW), pg.reshape(S, AW)


def _merge_out_loss(ag, pg, rest, wab, wpb, wout, x, target, gate, final_g):
    S = x.shape[0]
    LS = S // R16

    def body(ag_ref, pg_ref, ga_ref, gp_ref, wa_ref, wp_ref, w_ref, x_ref, t_ref, gate_ref, fg_ref,
             dx2_ref, dmo_ref, loss_ref, gfg_ref, dgate_ref, dw_ref):
        i = pl.program_id(0)

        @pl.when(i == 0)
        def _():
            loss_ref[...] = jnp.zeros_like(loss_ref)
            gfg_ref[...] = jnp.zeros_like(gfg_ref)
            dgate_ref[...] = jnp.zeros_like(dgate_ref)
            dw_ref[...] = jnp.zeros_like(dw_ref)

        swap = _swap16()
        ya = _dot(_ld(ag_ref), wa_ref[...], NN)
        yp = _dot(_ld(pg_ref), wp_ref[...], NN)
        mg = (_sigmoid(_ld(ga_ref)) * ya + _sigmoid(_ld(gp_ref)) * yp).astype(CDT)
        m_nat = _dot(swap, mg, NN)
        mo = _dot(m_nat.astype(CDT), w_ref[...], NN)
        gate, fg = gate_ref[...], fg_ref[...]
        x2 = x_ref[...] + gate * mo
        r2 = lax.rsqrt(jnp.mean(x2 * x2, axis=-1, keepdims=True) + EPS)
        xn2 = x2 * r2
        diff = xn2 * fg - t_ref[...]
        loss = jnp.sum(jnp.sum(diff * diff, axis=-1, keepdims=True), axis=0, keepdims=True)
        dy = diff * (1.0 / D)
        dxn2 = dy * fg
        dx2 = r2 * (dxn2 - xn2 * jnp.mean(dxn2 * xn2, axis=-1, keepdims=True))
        dx2_ref[...] = dx2
        dmo = (dx2 * gate).astype(CDT)
        _st(dmo_ref, _dot(swap, dmo, NN))
        loss_ref[...] += jnp.broadcast_to(loss * (0.5 / D), loss_ref.shape)
        gfg_ref[...] += jnp.sum(dy * xn2, axis=0, keepdims=True)
        dgate_ref[...] += jnp.sum(dx2 * mo, axis=0, keepdims=True)
        dw_ref[...] += _dot(m_nat.T.astype(CDT), dmo, NN)

    def slab(w, cblk):
        return pl.BlockSpec((R16, R16, w), lambda i: (0, i, cblk))

    nat = pl.BlockSpec((TOK, D), lambda i: (i, 0))
    vec = pl.BlockSpec((1, D), lambda i: (0, 0))
    r3 = rest.reshape(R16, LS, REST_W)
    return pl.pallas_call(
        body, name="merge_out_loss", grid=(S // TOK,),
        in_specs=[slab(AW, 0), slab(AW, 0), slab(D, 1), slab(D, 2), _resident((AW, D)), _resident((AW, D)),
                  _resident((D, D)), nat, nat, vec, vec],
        out_specs=[nat, slab(D, 0), pl.BlockSpec((1, 128), lambda i: (0, 0)), vec, vec,
                   pl.BlockSpec((D, D), lambda i: (0, 0))],
        out_shape=[_sds((S, D), F32), _sds((R16, LS, D), CDT), _sds((1, 128), F32), _sds((1, D), F32),
                   _sds((1, D), F32), _sds((D, D), F32)],
        compiler_params=_cp(("arbitrary",)),
    )(ag.reshape(R16, LS, AW), pg.reshape(R16, LS, AW), r3, r3, wab, wpb, wout, x, target, gate, final_g)


def _bwd_gates(dmo, wout, wab, wpb, rest, ag, pg, attn, mp, pool_scale):
    S = dmo.shape[0]
    tm = 256

    def body(dmo_ref, wo_ref, wa_ref, wp_ref, za_ref, zp_ref, ga0, ga1, gp0, gp1, ag_ref, pg_ref, at_ref, mp_ref,
             ps_ref, dwa_ref, dwp_ref, dr_ref, dat_ref, dl_ref, dmp_ref, dps_ref):
        i = pl.program_id(0)

        @pl.when(i == 0)
        def _():
            dps_ref[...] = jnp.zeros_like(dps_ref)
            dwa_ref[...] = jnp.zeros_like(dwa_ref)
            dwp_ref[...] = jnp.zeros_like(dwp_ref)

        dm = _dot(dmo_ref[...], wo_ref[...], NT)
        sa = _sigmoid(jnp.concatenate([ga0[...], ga1[...]], axis=1))
        sp = _sigmoid(jnp.concatenate([gp0[...], gp1[...]], axis=1))
        dya = (dm * sa).astype(CDT)
        dyp = (dm * sp).astype(CDT)
        ag, pg = ag_ref[...], pg_ref[...]
        dwa_ref[...] += _dot(ag.astype(F32).T.astype(CDT), dya, NN)
        dwp_ref[...] += _dot(pg.astype(F32).T.astype(CDT), dyp, NN)
        ya = _dot(ag, wa_ref[...], NN)
        yp = _dot(pg, wp_ref[...], NN)
        dr_ref[:, 1024:2048] = (dm * ya * sa * (1.0 - sa)).astype(dr_ref.dtype)
        dr_ref[:, 2048:3072] = (dm * yp * sp * (1.0 - sp)).astype(dr_ref.dtype)
        da = _dot(dya, wa_ref[...], NT)
        dp = _dot(dyp, wp_ref[...], NT)
        za = za_ref[...]
        sga = _sigmoid(za)
        attn = at_ref[...]
        dattn = da * (za * sga)
        dr_ref[:, 0:512] = (da * attn * (sga * (1.0 + za * (1.0 - sga)))).astype(dr_ref.dtype)
        dat_ref[...] = dattn.astype(dat_ref.dtype)
        prod = dattn * attn
        low = lax.broadcasted_iota(jnp.int32, (1, 2 * HD), 1) < HD
        for j in range(NH // 2):
            x2 = prod[:, 2 * HD * j:2 * HD * (j + 1)]
            s0 = jnp.sum(jnp.where(low, x2, 0.0), axis=1, keepdims=True)
            s1 = jnp.sum(jnp.where(low, 0.0, x2), axis=1, keepdims=True)
            dl_ref[:, 2 * HD * j:2 * HD * (j + 1)] = jnp.where(low, s0, s1)
        zp = zp_ref[...]
        sgp = _sigmoid(zp)
        mpre = mp_ref[...]
        ps = ps_ref[...]
        dmixed = dp * (zp * sgp)
        dr_ref[:, 512:1024] = (dp * (mpre * ps) * (sgp * (1.0 + zp * (1.0 - sgp)))).astype(dr_ref.dtype)
        dps_ref[...] += jnp.sum(dmixed * mpre, axis=0, keepdims=True)
        dmp_ref[...] = (dmixed * ps).astype(dmp_ref.dtype)

    full = lambda shape: pl.BlockSpec(shape, lambda i: (0, 0))
    tD = pl.BlockSpec((tm, D), lambda i: (i, 0))
    tA = pl.BlockSpec((tm, AW), lambda i: (i, 0))
    rb = lambda c: pl.BlockSpec((tm, AW), lambda i: (i, c))
    return pl.pallas_call(
        body, name="bwd_gates", grid=(S // tm,),
        in_specs=[tD, full((D, D)), full((AW, D)), full((AW, D)), rb(0), rb(1), rb(2), rb(3), rb(4), rb(5),
                  tA, tA, tA, tA, full((1, AW))],
        out_specs=[full((AW, D)), full((AW, D)), pl.BlockSpec((tm, 3072), lambda i: (i, 0)), tA, tA, tA, full((1, AW))],
        out_shape=[_sds((AW, D), F32), _sds((AW, D), F32), _sds((S, REST_W), CDT), _sds((S, AW), CDT),
                   _sds((S, AW), F32), _sds((S, AW), CDT), _sds((1, AW), F32)],
        compiler_params=_cp(("arbitrary",), vmem=56),
    )(dmo, wout, wab, wpb, rest, rest, rest, rest, rest, rest, ag, pg, attn, mp, pool_scale)


def _pool_bwd(dmp, pt, pool_w, drest):
    S = dmp.shape[0]
    LS = S // R16
    PG = 128
    d3 = dmp.reshape(R16, LS, AW)
    dr3 = drest.reshape(R16, LS, REST_W)

    def body(d_ref, pt_ref, pw_ref, dr_in, du_ref, dpw_ref, g_ref, s_ref):
        del dr_in
        g = pl.program_id(0)
        win = jnp.left_shift(2, g)
        row = lax.broadcasted_iota(jnp.int32, (LS, PG), 0)
        pw = pw_ref[...].astype(CDT)
        dpw = jnp.zeros((PG, PG), F32)
        for r in range(R16):
            dm = d_ref[r]
            dpw = dpw + _dot(pt_ref[:, r * LS:(r + 1) * LS], dm, NN)
            dpooled = _dot(dm, pw, NT)
            g_ref[r] = dpooled
            sc = dpooled / _pool_counts(r, LS, win)
            s_ref[r] = sc
            if r < R16 - 1:
                s_ref[R16 + r] = jnp.where(row == LS - 1, 0.0, pltpu.roll(sc, LS - 1, 0))
        dpw_ref[...] = dpw

        def double(s):
            for i in range(0, 2 * R16 - 2 * s):
                s_ref[i] = s_ref[i] + s_ref[i + s]

        double(1)
        for k in (1, 2, 3):
            pl.when(g >= k)(functools.partial(double, 2 ** k))
        for r in range(R16):
            du_ref[r] = (s_ref[r] - g_ref[r]).astype(du_ref.dtype)

    du, dpw = pl.pallas_call(
        body, name="pool_bwd", grid=(4,),
        in_specs=[pl.BlockSpec((R16, LS, PG), lambda g: (0, 0, g)), pl.BlockSpec((PG, S), lambda g: (g, 0)),
                  pl.BlockSpec((None, PG, PG), lambda g: (g, 0, 0)), pl.BlockSpec(memory_space=pl.ANY)],
        out_specs=[pl.BlockSpec((R16, LS, PG), lambda g: (0, 0, 24 + g)),
                   pl.BlockSpec((None, PG, PG), lambda g: (g, 0, 0))],
        out_shape=[_sds((R16, LS, REST_W), CDT), _sds((4, PG, PG), F32)],
        scratch_shapes=[pltpu.VMEM((R16, LS, PG), F32), pltpu.VMEM((2 * R16, LS, PG), F32)],
        input_output_aliases={3: 0},
        compiler_params=_cp(("parallel",)),
    )(d3, pt, pool_w, dr3)
    return du.reshape(S, REST_W), dpw


def _attn_bwd(qkv, dattn, Lb, dlb, biases, geom, dqkv_prev):
    S = qkv.shape[0]
    gi, T, R, has_prev, N = geom.gi, geom.T, geom.rows, geom.has_prev, geom.n_tiles
    H2 = T // 2
    nd_outer = len(geom.outer)
    qv = geom.view(qkv)
    cq, ck, cv = 3 * gi, 3 * gi + 1, 3 * gi + 2
    W3 = 3 * AW

    def heads(q_ref, kp_ref, kc_ref, vp_ref, vc_ref, do_ref, L_ref, dl_ref, ba_ref, bb_ref, pm, dsa_ref, dsb_ref, emit):
        q_lo, q_hi = _split_rows(_ld(q_ref), R)
        do_lo, do_hi = _split_rows(_ld(do_ref), R)
        L_lo, L_hi = _split_rows(_ld(L_ref), R)
        dl_lo, dl_hi = _split_rows(_ld(dl_ref), R)
        kc_lo, kc_hi = _split_rows(_ld(kc_ref), R)
        vc_lo, vc_hi = _split_rows(_ld(vc_ref), R)
        k_b = jnp.concatenate([kc_lo, kc_hi], axis=0)
        v_b = jnp.concatenate([vc_lo, vc_hi], axis=0)
        if has_prev:
            k_a = jnp.concatenate([_split_rows(_ld(kp_ref), R)[1], kc_lo], axis=0)
            v_a = jnp.concatenate([_split_rows(_ld(vp_ref), R)[1], vc_lo], axis=0)
        else:
            k_a, v_a = k_b, v_b
        low = lax.broadcasted_iota(jnp.int32, (1, 2 * HD), 1) < HD
        for j in range(NH // 2):
            sl = slice(2 * HD * j, 2 * HD * (j + 1))
            res = []
            for q, k, v, do, L, dl, b_ref, mask, ds_ref in (
                    (q_lo, k_a, v_a, do_lo, L_lo, dl_lo, ba_ref, pm, dsa_ref),
                    (q_hi, k_b, v_b, do_hi, L_hi, dl_hi, bb_ref, None, dsb_ref)):
                q2 = q[:, sl].astype(F32) * 0.125
                do2 = do[:, sl].astype(F32)
                k2, v2 = k[:, sl], v[:, sl]
                L2, dl2 = L[:, sl], dl[:, sl]
                qs = jnp.concatenate([jnp.where(low, q2, 0.0), jnp.where(low, 0.0, q2)], axis=0).astype(q.dtype)
                dom = jnp.concatenate([jnp.where(low, do2, 0.0), jnp.where(low, 0.0, do2)], axis=0).astype(q.dtype)
                Lc = jnp.concatenate([L2[:, 0:1], L2[:, HD:HD + 1]], axis=0)
                dlc = jnp.concatenate([dl2[:, 0:1], dl2[:, HD:HD + 1]], axis=0)
                s = _dot(qs, k2, NT) + b_ref[j]
                if mask is not None:
                    s = s + mask
                p = jnp.exp(s - Lc)
                ds = p * (_dot(dom, v2, NT) - dlc)
                ds_ref[j] += ds
                dsc = ds.astype(q.dtype)
                dq_st = _dot(dsc, k2, NN)
                res.append((jnp.where(low, dq_st[:H2], dq_st[H2:]) * 0.125, _dot(dsc, qs, TN),
                            _dot(p.astype(q.dtype), dom, TN)))
            (dq_lo, dk_a, dv_a), (dq_hi, dk_b, dv_b) = res
            dq = _merge_rows(dq_lo, dq_hi, R)
            if has_prev:
                emit(sl, dq, _merge_rows(dk_a[H2:] + dk_b[:H2], dk_b[H2:], R),
                     _merge_rows(dv_a[H2:] + dv_b[:H2], dv_b[H2:], R), dk_a[:H2], dv_a[:H2])
            else:
                emit(sl, dq, _merge_rows(dk_a[:H2] + dk_b[:H2], dk_a[H2:] + dk_b[H2:], R),
                     _merge_rows(dv_a[:H2] + dv_b[:H2], dv_a[H2:] + dv_b[H2:], R), None, None)

    def first_step():
        ok = pl.program_id(nd_outer) == 0 if has_prev else pl.program_id(0) == 0
        for a in range(nd_outer if has_prev else 0):
            ok = ok & (pl.program_id(a) == 0)
        return ok

    if has_prev:
        def body(q_ref, kp_ref, kc_ref, vp_ref, vc_ref, do_ref, L_ref, dl_ref, ba_ref, bb_ref, prev_in, out_ref,
                 dsa_ref, dsb_ref, hold, new, pp):
            del prev_in
            s_id = pl.program_id(nd_outer)

            @pl.when(first_step())
            def _():
                dsa_ref[...] = jnp.zeros_like(dsa_ref)
                dsb_ref[...] = jnp.zeros_like(dsb_ref)

            @pl.when(s_id < N)
            def _():
                def emit(sl, dq, dk, dv, dk_prev, dv_prev):
                    new[:, sl] = dq
                    new[:, slice(AW + sl.start, AW + sl.stop)] = dk
                    new[:, slice(2 * AW + sl.start, 2 * AW + sl.stop)] = dv
                    pp[:, sl] = dk_prev
                    pp[:, slice(AW + sl.start, AW + sl.stop)] = dv_prev

                heads(q_ref, kp_ref, kc_ref, vp_ref, vc_ref, do_ref, L_ref, dl_ref, ba_ref, bb_ref,
                      _prev_mask(s_id, H2, T), dsa_ref, dsb_ref, emit)

            @pl.when((s_id >= 1) & (s_id < N))
            def _():
                _st(out_ref, hold[:, 0:AW], slice(0, AW))
                late = _merge_rows(jnp.zeros((H2, 2 * AW), F32), pp[...], R)
                _st(out_ref, hold[:, AW:W3] + late, slice(AW, W3))

            @pl.when(s_id == N)
            def _():
                _st(out_ref, hold[...])

            @pl.when(s_id < N)
            def _():
                hold[...] = new[...]

        cur = lambda n: jnp.minimum(n, N - 1)
        prev = lambda n: jnp.clip(n - 1, 0, N - 1)
        delayed = lambda n: jnp.maximum(n - 1, 0)
        in_specs = [geom.spec(AW, cq, cur), geom.spec(AW, ck, prev), geom.spec(AW, ck, cur),
                    geom.spec(AW, cv, prev), geom.spec(AW, cv, cur),
                    geom.spec(AW, 0, cur), geom.spec(AW, 0, cur), geom.spec(AW, 0, cur)]
        args = [qv] * 5 + [geom.view(dattn), geom.view(Lb), geom.view(dlb)]
        out_spec = geom.spec(W3, gi, delayed)
        grid = geom.outer + (N + 1,)
        scratch = [pltpu.VMEM((T, W3), F32), pltpu.VMEM((T, W3), F32), pltpu.VMEM((H2, 2 * AW), F32)]
    else:
        def body(q_ref, kc_ref, vc_ref, do_ref, L_ref, dl_ref, ba_ref, bb_ref, prev_in, out_ref, dsa_ref, dsb_ref):
            del prev_in

            @pl.when(first_step())
            def _():
                dsa_ref[...] = jnp.zeros_like(dsa_ref)
                dsb_ref[...] = jnp.zeros_like(dsb_ref)

            def emit(sl, dq, dk, dv, dk_prev, dv_prev):
                _st(out_ref, dq, sl)
                _st(out_ref, dk, slice(AW + sl.start, AW + sl.stop))
                _st(out_ref, dv, slice(2 * AW + sl.start, 2 * AW + sl.stop))

            heads(q_ref, None, kc_ref, None, vc_ref, do_ref, L_ref, dl_ref, ba_ref, bb_ref, None, dsa_ref, dsb_ref,
                  emit)

        cur = lambda n: n
        in_specs = [geom.spec(AW, cq, cur), geom.spec(AW, ck, cur), geom.spec(AW, cv, cur),
                    geom.spec(AW, 0, cur), geom.spec(AW, 0, cur), geom.spec(AW, 0, cur)]
        args = [qv] * 3 + [geom.view(dattn), geom.view(Lb), geom.view(dlb)]
        out_spec = geom.spec(W3, gi, cur)
        grid = geom.outer
        scratch = []

    whole = lambda b: pl.BlockSpec(b.shape, lambda *ids: (0, 0, 0))
    biases = [b.reshape(NH // 2, 2 * H2, b.shape[2]) for b in biases]
    in_specs += [whole(b) for b in biases]
    in_specs.append(pl.BlockSpec(memory_space=pl.ANY))
    if dqkv_prev is None:
        last, aliases = biases[0], {}
    else:
        last, aliases = geom.view(dqkv_prev), {len(in_specs) - 1: 0}
    dq, dsa, dsb = pl.pallas_call(
        body, name=f"attn_bwd{gi}", grid=grid, in_specs=in_specs,
        out_specs=[out_spec] + [whole(b) for b in biases],
        out_shape=[_sds(geom.view_shape(QKV_W), CDT)] + [_sds(b.shape, F32) for b in biases],
        scratch_shapes=scratch,
        input_output_aliases=aliases,
        compiler_params=_cp(("arbitrary",) * len(grid), vmem=56),
    )(*args, *biases, last)
    return geom.unview(dq), [d.reshape(NH, H2, d.shape[2]) for d in (dsa, dsb)]


def _local_step(x, target, shift, scale, gate, norm_g, w_in_full, pool_w, pool_scale, wab, wpb, wout, rel_bias,
                final_g):
    st = _local_bias_tables(_local_pre(x, shift, scale, norm_g), rel_bias)
    st = _local_grads(st, target, gate, w_in_full, pool_w, pool_scale, wab, wpb, wout, final_g)
    d_rel_bias = _local_bias_grads(st, _BIAS_GRADS_EARLY + _BIAS_GRADS_LATE)
    gx, dmod, dng = _local_input_grads(st, w_in_full, norm_g, scale)
    return dict(loss=st["loss"], grad_x=gx, dw_in=st["dw_in"], dw_ab=st["dw_ab"], dw_pb=st["dw_pb"],
                dw_out=st["dw_out"], d_norm_g=dng, d_pool_w=st["dpw"], d_pool_scale=st["dps"],
                d_rel_bias=d_rel_bias.T, d_final_g=st["gfg"], dmod=dmod)


def _local_pre(x, shift, scale, norm_g, dep=None):
    S = x.shape[0]
    h = _norm_mod(x, norm_g, shift, scale, dep).reshape(S, D)
    return dict(x=x, h=h, ht=_tcast(h, "h_transpose"))


def _local_bias_tables(st, rel_bias, dep=None):
    LS = st["x"].shape[0] // R16
    geoms = [_Geom(gi, LS) for gi in range(3)]
    idx_np = [[g.bucket_index(sub) for sub in range(2)] for g in geoms]
    buckets = [[tuple(int(b) for b in np.unique(ix) if b >= 0) for ix in pair] for pair in idx_np]
    idxs = [[jnp.asarray(ix) for ix in pair] for pair in idx_np]
    biases = [[_bias_table(idxs[gi][sub], buckets[gi][sub], rel_bias, gi, sub, dep) for sub in range(2)]
              for gi in range(3)]
    return dict(st, geoms=geoms, idxs=idxs, buckets=buckets, biases=biases)


def _local_grads(st, target, gate, w_in_full, pool_w, pool_scale, wab, wpb, wout, final_g):
    geoms, idxs, biases, x, h, ht = (st[k] for k in ("geoms", "idxs", "biases", "x", "h", "ht"))
    S = h.shape[0]
    LS = S // R16
    qkv, rest = _proj(h, w_in_full)

    os_, ls_ = [], []
    for gi in range(2):
        o, l = _attn_fwd(qkv, biases[gi], geoms[gi])
        os_.append(o)
        ls_.append(l)
    attn, Lb, ag = _attn_fwd(qkv, biases[2], geoms[2], others=(os_, ls_, rest))
    pt, mp, pg = _pool_fwd(rest, pool_w, pool_scale)
    dx2, dmo3, loss, gfg, dgate, dw_out = _merge_out_loss(ag, pg, rest, wab, wpb, wout, x, target, gate, final_g)
    dmo = dmo3.reshape(S, D)

    dw_ab, dw_pb, drest, dattn, dlb, dmp, dps = _bwd_gates(dmo, wout, wab, wpb, rest, ag, pg, attn, mp, pool_scale)
    drest, dpw = _pool_bwd(dmp, pt, pool_w, drest)

    dqkv = None
    dsums = []
    for gi in range(3):
        dqkv, dsum = _attn_bwd(qkv, dattn, Lb, dlb, biases[gi], geoms[gi], dqkv)
        dsums.append(dsum)

    dw_in = _mm_tn(ht, dqkv, n_total=8192, out_blk=lambda j: j, name="dw_in_qkv", with_low=True)
    dw_in, dw_in_low = _mm_tn(ht, drest, n_total=8192, out_blk=_rest_blk, name="dw_in_rest", prev=dw_in, with_low=True)
    return dict(st, loss=loss[0, 0], dw_in=dw_in, dw_in_low=dw_in_low, dw_ab=dw_ab, dw_pb=dw_pb, dw_out=dw_out, dpw=dpw, dps=dps, gfg=gfg,
                dgate=dgate, dsums=dsums, dqkv=dqkv, drest=drest, dx2=dx2)


def _local_bias_grads(st, which, dep=None):
    rows = []
    for gi in range(3):
        got = [_bias_grad(st["dsums"][gi][sub], st["idxs"][gi][sub], st["buckets"][gi][sub], gi, sub, dep)
               for sub in range(2) if (gi, sub) in which]
        rows.append(sum(got) if got else jnp.zeros((NH, 32), F32))
    return jnp.concatenate(rows, axis=0)


_BIAS_GRADS_EARLY = ((0, 0), (0, 1), (1, 0), (2, 0))
_BIAS_GRADS_LATE = ((1, 1), (2, 1))


def _dh_h_bwd(dqkv, drest, w, dx2, x, norm_g, scale, dep=None):
    S = x.shape[0]
    dep_specs, dep_args = _after(dep)

    def body(a_ref, b_ref, w_ref, dx2_ref, x_ref, g_ref, sc_ref, *rest):
        gx_ref, dsh_ref, dsc_ref, dng_ref, nat_ref = rest[-5:]
        i = pl.program_id(0)

        @pl.when(i == 0)
        def _():
            dsh_ref[...] = jnp.zeros_like(dsh_ref)
            dsc_ref[...] = jnp.zeros_like(dsc_ref)
            dng_ref[...] = jnp.zeros_like(dng_ref)

        a, b = _ld(a_ref), _ld(b_ref)
        acc = _dot(a[:, 0:512], _w512(w_ref, 0), NT)
        for j in range(1, 9):
            acc = acc + _dot(a[:, 512 * j:512 * (j + 1)], _w512(w_ref, j), NT)
        off = 0
        for lo, hi in REST_COLS:
            for j in range(lo, hi):
                acc = acc + _dot(b[:, off:off + 512], _w512(w_ref, j), NT)
                off += 512
        for r in range(R16):
            nat_ref[:, r, :] = acc[R16 * r:R16 * (r + 1)]
        dh = jnp.concatenate([nat_ref[l] for l in range(R16)], axis=0)
        g = g_ref[...]
        one_sc = 1.0 + sc_ref[...]
        xv = x_ref[...]
        r1 = lax.rsqrt(jnp.mean(xv * xv, axis=-1, keepdims=True) + EPS)
        xn = xv * r1
        dhx = dh * xn
        dxn = dh * (g * one_sc)
        dx1 = r1 * (dxn - xn * jnp.mean(dxn * xn, axis=-1, keepdims=True))
        gx_ref[...] = dx2_ref[...] + dx1
        dsh_ref[...] += jnp.sum(dh, axis=0, keepdims=True)
        dsc_ref[...] += jnp.sum(dhx * g, axis=0, keepdims=True)
        dng_ref[...] += jnp.sum(dhx * one_sc, axis=0, keepdims=True)

    nat = pl.BlockSpec((TOK, D), lambda i: (i, 0))
    vec = pl.BlockSpec((1, D), lambda i: (0, 0))
    slab = lambda W: pl.BlockSpec((R16, R16, W), lambda i: (0, i, 0))
    return pl.pallas_call(
        body, name="dh_h_bwd", grid=(S // TOK,),
        in_specs=[slab(QKV_W), slab(REST_W), _resident((8, D, 1024)), nat, nat, vec, vec] + dep_specs,
        out_specs=[nat, vec, vec, vec],
        out_shape=[_sds((S, D), F32), _sds((1, D), F32), _sds((1, D), F32), _sds((1, D), F32)],
        scratch_shapes=[pltpu.VMEM((R16, R16, D), F32)],
        compiler_params=_cp(("arbitrary",), vmem=56),
    )(dqkv, drest, w, dx2, x, norm_g, scale, *dep_args)


def _local_input_grads(st, w_in_full, norm_g, scale, dep=None):
    S = st["h"].shape[0]
    LS = S // R16
    gx, dsh, dsc, dng = _dh_h_bwd(st["dqkv"].reshape(R16, LS, QKV_W), st["drest"].reshape(R16, LS, REST_W),
                                  w_in_full, st["dx2"], st["x"], norm_g, scale, dep)
    return gx, jnp.concatenate([dsh, dsc, st["dgate"]], axis=1), dng


def _my_place():
    return lax.axis_index("x"), lax.axis_index("y"), lax.axis_index("c")


def _block_of(ref, axis, idx, width):
    if axis is None:
        return ref.at[idx]
    start = pl.multiple_of(idx * width, width)
    if axis == 0:
        return ref.at[pl.ds(start, width)]
    return ref.at[:, pl.ds(start, width)]


_SEM = pl.BlockSpec(memory_space=pltpu.SEMAPHORE)
_EFFECT = pltpu.SideEffectType.DATAFLOW_SIDE_EFFECTING


def _split_start(name, srcs, lands, ncopy, plan, dep=None):
    arrs = list(srcs) + list(lands)
    na, ns = len(arrs), len(srcs)
    dep_specs, dep_args = _after(dep)

    def body(*refs):
        send_sems, recv_sems, token = refs[na + len(dep_args)], refs[na + len(dep_args) + 1], refs[-1]
        for k, (s, d, dev) in enumerate(plan(refs[:ns], refs[ns:na])):
            pltpu.make_async_remote_copy(src_ref=s, dst_ref=d, send_sem=send_sems.at[k], recv_sem=recv_sems.at[k],
                                         device_id=dev, device_id_type=MESH).start()
        token[...] = jnp.zeros_like(token)

    outs = pl.pallas_call(
        body, name=name,
        out_shape=(pltpu.SemaphoreType.DMA((ncopy,)), pltpu.SemaphoreType.DMA((ncopy,)),
                   *[pltpu.HBM(a.shape, a.dtype) for a in arrs], _sds((8, 128), F32)),
        in_specs=[_HBM] * na + dep_specs,
        out_specs=(_SEM, _SEM, *[_HBM] * na, pl.BlockSpec(memory_space=pltpu.VMEM)),
        input_output_aliases={i: 2 + i for i in range(na)},
        compiler_params=pltpu.CompilerParams(has_side_effects=_EFFECT),
    )(*[pltpu.with_memory_space_constraint(a, pltpu.HBM) for a in arrs], *dep_args)
    return (outs[0], outs[1]), list(outs[2:2 + ns]), list(outs[2 + ns:2 + na]), outs[-1]


def _split_wait(name, sems, srcs, lands, plan, after):
    arrs = list(srcs) + list(lands)
    na, ns = len(arrs), len(srcs)
    after = list(after) if isinstance(after, (list, tuple)) else [after]

    def body(*refs):
        send_sems, recv_sems = refs[na], refs[na + 1]
        for k, (s, d, dev) in enumerate(plan(refs[:ns], refs[ns:na])):
            cp = pltpu.make_async_remote_copy(src_ref=s, dst_ref=d, send_sem=send_sems.at[k], recv_sem=recv_sems.at[k],
                                              device_id=dev, device_id_type=MESH)
            cp.wait_send()
            cp.wait_recv()

    outs = pl.pallas_call(
        body, name=name, out_shape=[pltpu.HBM(a.shape, a.dtype) for a in arrs],
        in_specs=[_HBM] * na + [_SEM, _SEM] + [pl.BlockSpec(memory_space=pl.ANY)] * len(after),
        out_specs=[_HBM] * na,
        input_output_aliases={i: i for i in range(na)},
        compiler_params=pltpu.CompilerParams(has_side_effects=_EFFECT),
    )(*arrs, sems[0], sems[1], *after)
    return list(outs[:ns]), list(outs[ns:])


def _gather_chips_plan(layout):
    def plan(src, land):
        x, y, c = _my_place()
        me = 4 * x + 2 * y + c
        mine = [_block_of(land[p], layout[p][0], me, layout[p][1]) for p in range(len(land))]
        return [(mine[p], mine[p], (cx, cy, c)) for p in range(len(land)) for cx, cy in [(1 - x, y), (x, 1 - y)]]
    return plan


def _pair_plan(layout):
    def plan(src, land):
        x, y, c = _my_place()
        n = len(layout)
        return [(_block_of(src[p], layout[p][0], 2 * q + (1 - c), layout[p][1]), land[p].at[q], (x, y, 1 - c))
                for p in range(n) for q in range(4)] + [(src[p], land[p], (x, y, 1 - c)) for p in range(n, len(src))]
    return plan


def _chips_plan(n):
    def plan(src, land):
        x, y, c = _my_place()
        chips = [(1 - x, y), (x, 1 - y), (1 - x, 1 - y)]
        return [(src[p].at[2 * cx + cy], land[p].at[2 * x + y], (cx, cy, c)) for p in range(n) for cx, cy in chips] + \
               [(src[p], land[p].at[2 * x + y], (cx, cy, c)) for p in range(n, len(src)) for cx, cy in chips]
    return plan


def _pair_share(gathered):
    n = len(gathered)

    def body(*refs):
        land = refs[n:2 * n]
        send_sems, recv_sems, fwd_send, fwd_recv = refs[2 * n:]
        x, y, c = _my_place()
        k_dg = 3 - (2 * x + y)
        south = c == 0
        pass_on = 4 * jnp.where(south, 1 - x, x) + 2 * jnp.where(south, y, 1 - y) + c
        to = (jnp.where(south, x, 1 - x), jnp.where(south, 1 - y, y), c)

        def forward(p):
            return pltpu.make_async_remote_copy(
                src_ref=land[p].at[pass_on], dst_ref=land[p].at[pass_on], send_sem=fwd_send.at[p],
                recv_sem=fwd_recv.at[p], device_id=to, device_id_type=MESH)

        def to_sibling(p, k):
            blk = land[p].at[2 * k + c]
            return pltpu.make_async_remote_copy(
                src_ref=blk, dst_ref=blk, send_sem=send_sems.at[4 * p + k], recv_sem=recv_sems.at[4 * p + k],
                device_id=(x, y, 1 - c), device_id_type=MESH)

        for p in range(n):
            forward(p).start()
        for p in range(n):
            for k in range(4):
                @pl.when(k != k_dg)
                def _():
                    to_sibling(p, k).start()
        for p in range(n):
            dg = land[p].at[2 * k_dg + c]
            pltpu.make_async_remote_copy(src_ref=dg, dst_ref=dg, send_sem=fwd_send.at[p], recv_sem=fwd_recv.at[p],
                                         device_id=to, device_id_type=MESH).wait_recv()
            for k in range(4):
                @pl.when(k == k_dg)
                def _():
                    to_sibling(p, k).start()
        for p in range(n):
            forward(p).wait_send()
            for k in range(4):
                to_sibling(p, k).wait_send()
                blk = land[p].at[2 * k + 1 - c]
                pltpu.make_async_remote_copy(
                    src_ref=blk, dst_ref=blk, send_sem=send_sems.at[4 * p + k], recv_sem=recv_sems.at[4 * p + k],
                    device_id=(x, y, 1 - c), device_id_type=MESH).wait_recv()

    return pl.pallas_call(
        body, name="gather_pair_share", in_specs=[_HBM] * n, out_specs=[_HBM] * n,
        out_shape=[_sds(g.shape, g.dtype) for g in gathered],
        input_output_aliases={p: p for p in range(n)},
        scratch_shapes=[pltpu.SemaphoreType.DMA((4 * n,)), pltpu.SemaphoreType.DMA((4 * n,)),
                        pltpu.SemaphoreType.DMA((n,)), pltpu.SemaphoreType.DMA((n,))],
    )(*gathered)


def _place_slot(arrs, slot, n_slots, name, out_dtype=None):
    n = len(arrs)

    def body(s_ref, *refs):
        for p in range(n):
            refs[n + p][...] = refs[p][...].astype(refs[n + p].dtype)

    def spec(a):
        tail = a.shape[-2:]
        return pl.BlockSpec((None,) + tail, lambda i, s_ref: (s_ref[0], 0, 0))

    in_specs = [spec(a) if a.ndim == 3 else pl.BlockSpec(a.shape, lambda i, s_ref: (0, 0)) for a in arrs]
    return pl.pallas_call(
        body, name=name,
        grid_spec=pltpu.PrefetchScalarGridSpec(num_scalar_prefetch=1, grid=(1,), in_specs=in_specs,
                                               out_specs=[spec(a) for a in arrs]),
        out_shape=[_in_hbm((n_slots,) + a.shape[-2:], out_dtype or a.dtype) for a in arrs],
        compiler_params=_cp(("arbitrary",)),
    )(slot, *[_pin(a) for a in arrs])


def _cols_from_blocks(w8, name):
    _, R, C = w8.shape

    def body(w_ref, o_ref):
        for e in range(8):
            o_ref[:, C * e:C * (e + 1)] = w_ref[e]

    return pl.pallas_call(body, name=name, out_shape=_sds((R, 8 * C), w8.dtype), compiler_params=_cp())(w8)


def _all_gather_small(srcs, name):
    n = len(srcs)
    vmem = pl.BlockSpec(memory_space=pltpu.VMEM)

    def body(*refs):
        src, out = refs[:n], refs[n:2 * n]
        send_sems, recv_sems = refs[2 * n:]
        x, y, c = _my_place()
        me = 4 * x + 2 * y + c
        flip = lambda v, f: 1 - v if f else v
        peers = [(flip(x, fx), flip(y, fy), flip(c, fc)) for fx in (0, 1) for fy in (0, 1) for fc in (0, 1)][1:]
        cps = []
        for p in range(n):
            out[p][me] = src[p][...]
            for k, peer in enumerate(peers):
                cps.append(pltpu.make_async_remote_copy(
                    src_ref=src[p], dst_ref=out[p].at[me], send_sem=send_sems.at[7 * p + k],
                    recv_sem=recv_sems.at[7 * p + k], device_id=peer, device_id_type=MESH))
        for cp in cps:
            cp.start()
        for cp in cps:
            cp.wait()

    return pl.pallas_call(
        body, name=name, in_specs=[vmem] * n, out_specs=[vmem] * n,
        out_shape=[_sds((8,) + a.shape, a.dtype) for a in srcs],
        scratch_shapes=[pltpu.SemaphoreType.DMA((7 * n,)), pltpu.SemaphoreType.DMA((7 * n,))],
    )(*srcs)


def _adaln_exchange(c, w_ada, b_part):
    vmem = pl.BlockSpec(memory_space=pltpu.VMEM)

    def body(c_ref, w_ref, b_ref, call_ref, mod_ref, part_ref, send_sems, recv_sems):
        x, y, c = _my_place()
        me = 4 * x + 2 * y + c
        flip = lambda v, f: 1 - v if f else v
        peers = [(flip(x, fx), flip(y, fy), flip(c, fc)) for fx in (0, 1) for fy in (0, 1) for fc in (0, 1)][1:]
        call_ref[me] = c_ref[...]
        first = [pltpu.make_async_remote_copy(src_ref=c_ref, dst_ref=call_ref.at[me], send_sem=send_sems.at[k],
                                              recv_sem=recv_sems.at[k], device_id=peer, device_id_type=MESH)
                 for k, peer in enumerate(peers)]
        for cp in first:
            cp.start()
        for cp in first:
            cp.wait()
        c_all = jnp.concatenate([call_ref[j] for j in range(8)], axis=0)
        part_ref[...] = _dot(c_all.astype(CDT), w_ref[...].astype(CDT), NN) + b_ref[...]
        mod_ref[me] = part_ref[pl.ds(me, 1), :]
        second = [pltpu.make_async_remote_copy(
            src_ref=part_ref.at[pl.ds(4 * px + 2 * py + pc, 1)], dst_ref=mod_ref.at[me], send_sem=send_sems.at[7 + k],
            recv_sem=recv_sems.at[7 + k], device_id=(px, py, pc), device_id_type=MESH)
            for k, (px, py, pc) in enumerate(peers)]
        for cp in second:
            cp.start()
        for cp in second:
            cp.wait()

    return pl.pallas_call(
        body, name="adaln_exchange", in_specs=[vmem] * 3, out_specs=[vmem] * 2,
        out_shape=[_sds((8, 1, D), F32), _sds((8, 1, w_ada.shape[1]), F32)],
        scratch_shapes=[pltpu.VMEM((8, w_ada.shape[1]), F32), pltpu.SemaphoreType.DMA((14,)),
                        pltpu.SemaphoreType.DMA((14,))],
        compiler_params=_cp(),
    )(c, w_ada, b_part)


def _pair_add(dw, recv, axis, width, cidx, name):
    if axis == 0:
        tr = min(width, 256)
        C = dw.shape[1]
        grid = (4, width // tr)
        nb = width // tr
        own = pl.BlockSpec((tr, C), lambda q, i, c_ref: ((2 * q + c_ref[0]) * nb + i, 0))
        oth = pl.BlockSpec((None, tr, C), lambda q, i, c_ref: (q, i, 0))
    else:
        R = dw.shape[0]
        tr = min(R, 1024)
        grid = (4, R // tr)
        own = pl.BlockSpec((tr, width), lambda q, i, c_ref: (i, 2 * q + c_ref[0]))
        oth = pl.BlockSpec((None, tr, width), lambda q, i, c_ref: (q, i, 0))

    def body(c_ref, a_ref, b_ref, o_ref):
        o_ref[...] = (a_ref[...] + b_ref[...].astype(F32)).astype(o_ref.dtype)

    return pl.pallas_call(
        body, name=name,
        grid_spec=pltpu.PrefetchScalarGridSpec(num_scalar_prefetch=1, grid=grid, in_specs=[own, oth], out_specs=oth),
        out_shape=(_in_hbm if _is_big(dw) else _sds)(recv.shape, CDT), compiler_params=_cp(("parallel", "parallel")),
    )(cidx, *[_pin(a) if _is_big(dw) else a for a in (dw, recv)])


def _add_pairs(xs, ys, name):
    n = len(xs)

    def body(*refs):
        for p in range(n):
            refs[2 * n + p][...] = refs[p][...] + refs[n + p][...]

    return pl.pallas_call(body, name=name, out_shape=[_sds(a.shape, F32) for a in xs])(*xs, *ys)


def _adam_math(w, g, m, v):
    m = ADAM_B1 * m + (1.0 - ADAM_B1) * g
    v = ADAM_B2 * v + (1.0 - ADAM_B2) * (g * g)
    m_hat = m / (1.0 - ADAM_B1 ** ADAM_STEP)
    v_hat = v / (1.0 - ADAM_B2 ** ADAM_STEP)
    delta = -ADAM_LR * (m_hat / (jnp.sqrt(v_hat) + ADAM_EPS) + ADAM_WD * w)
    return delta, m, v


def _adam(gparts, w, m, v, name, fresh=False):
    P, R, C = gparts.shape
    in_hbm = fresh or _is_big(w)
    tr = R if R <= 256 else 256

    def body(g_ref, w_ref, m_ref, v_ref, go_ref, d_ref, mo_ref, vo_ref):
        g = g_ref[0].astype(F32)
        for i in range(1, P):
            g = g + g_ref[i].astype(F32)
        d, mn, vn = _adam_math(w_ref[...], g, m_ref[...], v_ref[...])
        go_ref[...] = g
        d_ref[...] = d
        mo_ref[...] = mn
        vo_ref[...] = vn

    t = pl.BlockSpec((tr, C), lambda i: (i, 0))
    return pl.pallas_call(
        body, name=name, grid=(R // tr,),
        in_specs=[pl.BlockSpec((P, tr, C), lambda i: (0, i, 0)), t, t, t], out_specs=[t, t, t, t],
        out_shape=[(_in_hbm if in_hbm else _sds)((R, C), F32)] * 4, compiler_params=_cp(("parallel",)),
    )(*[_pin(a) if in_hbm else a for a in (gparts, w, m, v)])


def _sum_parts(gparts, name):
    P, R, C = gparts.shape

    def body(g_ref, o_ref):
        g = g_ref[0]
        for i in range(1, P):
            g = g + g_ref[i]
        o_ref[...] = g

    return pl.pallas_call(body, name=name, out_shape=_sds((R, C), F32))(gparts)


def _adam_w_ada(c_t, dm, w, m, v):
    R, C = w.shape

    def body(c_ref, dm_ref, w_ref, m_ref, v_ref, go_ref, d_ref, mo_ref, vo_ref):
        g = c_ref[:, 0:1] * dm_ref[0:1, :]
        for b in range(1, 8):
            g = g + c_ref[:, b:b + 1] * dm_ref[b:b + 1, :]
        d, mn, vn = _adam_math(w_ref[...], g, m_ref[...], v_ref[...])
        go_ref[...] = g
        d_ref[...] = d
        mo_ref[...] = mn
        vo_ref[...] = vn

    return pl.pallas_call(body, name="adam_w_ada", out_shape=[_sds((R, C), F32)] * 4,
                          compiler_params=_cp())(c_t, dm, w, m, v)


def kernel(x, c, norm_g, w_ada, b_ada, w_in, pool_w, pool_scale, w_attn_br, w_pool_br, w_out, rel_bias, final_g, loss_target, m_norm_g, m_w_ada, m_b_ada, m_w_in, m_pool_w, m_pool_scale, m_w_attn_br, m_w_pool_br, m_w_out, m_rel_bias, m_final_g, v_norm_g, v_w_ada, v_b_ada, v_w_in, v_pool_w, v_pool_scale, v_w_attn_br, v_w_pool_br, v_w_out, v_rel_bias, v_final_g):
    S = x.shape[1]
    px, py, pc = _my_place()
    me = 4 * px + 2 * py + pc
    cidx = jnp.reshape(pc, (1,)).astype(jnp.int32)

    layout = [(1, 1024), (1, 128), (1, 128), (0, 128)]
    final_g2 = final_g.reshape(1, D)
    me1 = jnp.reshape(me, (1,)).astype(jnp.int32)

    b_part = lax.dynamic_slice(b_ada, (0, me * 384), (1, 384))
    c_all, mod_all = _adaln_exchange(c, w_ada[0], b_part)
    c_all = c_all.reshape(8, D)
    mod = mod_all.reshape(1, 3 * D)
    shift, scale, gate = mod[:, :D], mod[:, D:2 * D], mod[:, 2 * D:]

    gplan = _gather_chips_plan([(None, 1)] * 4)
    lands = _place_slot([w_in[0], w_attn_br[0], w_pool_br[0], w_out[0]], me1, 8, "place_own_shards", CDT)
    g_sems, _, lands, tok = _split_start("gather_chips_start", [], lands, 8, gplan, dep=mod_all)
    st = _local_bias_tables(_local_pre(x[0], shift, scale, norm_g, dep=tok), rel_bias, dep=tok)
    _, lands = _split_wait("gather_chips_wait", g_sems, [], lands, gplan,
                           [st["ht"]] + [b for pair in st["biases"] for b in pair])
    w_in_full, wab8, wpb8, wout8 = _pair_share(lands)
    wab, wpb = _cols_from_blocks(wab8, "w_attn_br_cols"), _cols_from_blocks(wpb8, "w_pool_br_cols")
    wout = wout8.reshape(D, D)

    st = _local_grads(st, loss_target[0], gate, w_in_full, pool_w[0], pool_scale, wab, wpb, wout, final_g2)

    early = [st["dpw"].reshape(AW, 128), st["dps"], st["gfg"], jnp.broadcast_to(st["loss"], (1, 128))]
    dws = [st["dw_in"], st["dw_ab"], st["dw_pb"], st["dw_out"]]
    sent = [st["dw_in_low"]] + dws[1:] + early
    pplan = _pair_plan(layout)
    pair_shapes = [(4, w, a.shape[1]) if ax == 0 else (4, a.shape[0], w) for a, (ax, w) in zip(dws, layout)]
    p_sems, sent, recv, tok = _split_start(
        "reduce_pair_start", sent,
        [lax.empty(s, a.dtype) for s, a in zip(pair_shapes + [a.shape for a in early], sent)], 20, pplan)
    d_rel_bias = _local_bias_grads(st, _BIAS_GRADS_EARLY, dep=tok)
    sent, recv = _split_wait("reduce_pair_wait", p_sems, sent, recv, pplan, d_rel_bias)
    parts = [_pair_add(dw, r, ax, w, cidx, f"pair_add{i}")
             for i, (dw, r, (ax, w)) in enumerate(zip([dws[0]] + sent[1:4], recv[:4], layout))]
    parts += _add_pairs(sent[4:], recv[4:], "pair_add_smalls")

    slot1 = jnp.reshape(2 * px + py, (1,)).astype(jnp.int32)
    cplan = _chips_plan(4)
    c_sems, parts, lands, tok = _split_start("reduce_chips_start", parts,
                                             _place_slot(parts, slot1, 4, "place_own_partials"), 24, cplan)
    d_rel_bias = d_rel_bias + _local_bias_grads(st, _BIAS_GRADS_LATE, dep=tok)
    grad_x, dmod, d_norm_g = _local_input_grads(st, w_in_full, norm_g, scale, dep=tok)

    dmod_all, ng_all, rb_all = _all_gather_small([dmod, d_norm_g, d_rel_bias], "gather_late")
    o_norm_g = _adam(ng_all, norm_g, m_norm_g, v_norm_g, "adam_norm_g", fresh=True)
    o_rel_bias_t = _adam(rb_all, rel_bias.T, m_rel_bias.T, v_rel_bias.T, "adam_rel_bias", fresh=True)
    o_rel_bias = [o.T for o in o_rel_bias_t]
    o_b_ada = _adam(dmod_all, b_ada, m_b_ada, v_b_ada, "adam_b_ada", fresh=True)
    dm_mine = lax.dynamic_slice(dmod_all.reshape(8, 3 * D), (0, me * 384), (8, 384))
    o_w_ada = [o[None] for o in _adam_w_ada(c_all.T, dm_mine, w_ada[0], m_w_ada[0], v_w_ada[0])]

    _, (g_in, g_ab, g_pb, g_out, g_pw, g_ps, g_fg, g_loss) = _split_wait(
        "reduce_chips_wait", c_sems, parts, lands, cplan, [o_w_ada[0], o_b_ada[0], o_norm_g[0], o_rel_bias_t[0]])
    o_w_in = [o[None] for o in _adam(g_in, w_in[0], m_w_in[0], v_w_in[0], "adam_w_in")]
    o_w_ab = [o[None] for o in _adam(g_ab, w_attn_br[0], m_w_attn_br[0], v_w_attn_br[0], "adam_w_attn_br")]
    o_w_pb = [o[None] for o in _adam(g_pb, w_pool_br[0], m_w_pool_br[0], v_w_pool_br[0], "adam_w_pool_br")]
    o_w_out = [o[None] for o in _adam(g_out, w_out[0], m_w_out[0], v_w_out[0], "adam_w_out")]
    o_pool_w = [o.reshape(pool_w.shape) for o in _adam(g_pw, pool_w.reshape(AW, 128), m_pool_w.reshape(AW, 128),
                                                       v_pool_w.reshape(AW, 128), "adam_pool_w")]
    o_pool_scale = _adam(g_ps, pool_scale, m_pool_scale, v_pool_scale, "adam_pool_scale")
    o_final_g = [o.reshape(D) for o in _adam(g_fg, final_g2, m_final_g.reshape(1, D), v_final_g.reshape(1, D),
                                             "adam_final_g")]
    loss = _sum_parts(g_loss, "sum_loss")[0, 0]

    per_w = [o_norm_g, o_w_ada, o_b_ada, o_w_in, o_pool_w, o_pool_scale, o_w_ab, o_w_pb, o_w_out, o_rel_bias, o_final_g]
    outs = [loss, grad_x[None]]
    for k in range(4):
        outs += [o[k] for o in per_w]
    return tuple(outs)
```

```python
import functools
import math

import numpy as np
import jax
import jax.numpy as jnp
from jax import lax
from jax.experimental import pallas as pl
from jax.experimental.pallas import tpu as pltpu

F32 = jnp.float32
CDT = jnp.bfloat16
D = 1024
HD = 64
NH = 8
AW = 512
QKV_W = 4608
REST_W = 3584
R16 = 16
EPS = 1e-6
NEG = -1e30
N_BACK = 128
ADAM_LR, ADAM_B1, ADAM_B2, ADAM_EPS, ADAM_WD, ADAM_STEP = 0.001, 0.9, 0.999, 1e-08, 0.01, 10
MESH = pl.DeviceIdType.MESH
VMEM_MB = 1024 * 1024


def _cp(sem=None, vmem=48, **kw):
    if sem is not None:
        kw["dimension_semantics"] = sem
    return pltpu.CompilerParams(vmem_limit_bytes=vmem * VMEM_MB, **kw)


def _sds(shape, dtype):
    return jax.ShapeDtypeStruct(tuple(shape), dtype)


def _in_hbm(shape, dtype):
    return pltpu.HBM(tuple(shape), dtype)


def _pin(a):
    return pltpu.with_memory_space_constraint(a, pltpu.HBM)


def _is_big(a):
    return a.size * a.dtype.itemsize >= 4 * VMEM_MB


def _rest_blk(jj):
    return jnp.where(jj == 0, 9, jnp.where(jj == 6, 10, jj + 10))


def _sigmoid(z):
    return 0.5 * jnp.tanh(0.5 * z) + 0.5


def _dot(a, b, dims):
    return lax.dot_general(a, b, (dims, ((), ())), preferred_element_type=F32)


NN = ((1,), (0,))
NT = ((1,), (1,))
TN = ((0,), (0,))


def _ld(ref):
    if len(ref.shape) == 2:
        return ref[...]
    return jnp.concatenate([ref[a] for a in range(ref.shape[0])], axis=0)


def _st(ref, val, cols=None):
    val = val.astype(ref.dtype)
    if len(ref.shape) == 2:
        if cols is None:
            ref[...] = val
        else:
            ref[:, cols] = val
        return
    rows = ref.shape[1]
    for a in range(ref.shape[0]):
        if cols is None:
            ref[a] = val[a * rows:(a + 1) * rows]
        else:
            ref[a, :, cols] = val[a * rows:(a + 1) * rows]


REST_COLS = ((9, 10), (11, 16), (10, 11))
_HBM = pl.BlockSpec(memory_space=pltpu.HBM)


def _resident(shape):
    return pl.BlockSpec(shape, lambda *ids: (0,) * len(shape), pipeline_mode=pl.Buffered(1))


def _w512(w_ref, j):
    return w_ref[j // 2, :, 512 * (j % 2):512 * (j % 2 + 1)]


def _proj(h, w):
    S = h.shape[0]
    tm = 512

    def body(a_ref, w_ref, q_ref, r_ref):
        a = a_ref[...]
        for j in range(9):
            q_ref[:, 512 * j:512 * (j + 1)] = _dot(a, _w512(w_ref, j), NN).astype(q_ref.dtype)
        off = 0
        for lo, hi in REST_COLS:
            for j in range(lo, hi):
                r_ref[:, off:off + 512] = _dot(a, _w512(w_ref, j), NN)
                off += 512

    return pl.pallas_call(
        body, name="proj", grid=(S // tm,),
        in_specs=[pl.BlockSpec((tm, D), lambda i: (i, 0)), _resident((8, D, 1024))],
        out_specs=[pl.BlockSpec((tm, QKV_W), lambda i: (i, 0)), pl.BlockSpec((tm, REST_W), lambda i: (i, 0))],
        out_shape=[_sds((S, QKV_W), CDT), _sds((S, REST_W), F32)],
        compiler_params=_cp(("parallel",), vmem=56),
    )(h, w)


def _after(dep):
    return ([], []) if dep is None else ([pl.BlockSpec(memory_space=pl.ANY)], [dep])


def _mm_tn(at, b, *, n_total, out_blk, name, prev=None, tn=512, with_low=False):
    R, S = at.shape
    nb = b.shape[1] // tn
    n_out = 2 if with_low else 1

    def body(*refs):
        res = _dot(refs[0][...], refs[1][...], NN)
        refs[-n_out][...] = res
        if with_low:
            refs[-1][...] = res.astype(CDT)

    in_specs = [_resident((R, S)), pl.BlockSpec((S, tn), lambda j: (0, j))]
    args = [at, b]
    aliases = {}
    if prev is not None:
        prev = list(prev) if with_low else [prev]
        in_specs += [pl.BlockSpec(memory_space=pl.ANY)] * n_out
        args += prev
        aliases = {2 + k: k for k in range(n_out)}
    out_spec = pl.BlockSpec((R, tn), lambda j: (0, out_blk(j)))
    outs = pl.pallas_call(
        body, name=name, grid=(nb,), in_specs=in_specs,
        out_specs=[out_spec] * n_out,
        out_shape=[_sds((R, n_total), F32), _sds((R, n_total), CDT)][:n_out],
        input_output_aliases=aliases,
        compiler_params=_cp(("parallel",)),
    )(*args)
    return tuple(outs) if with_low else outs[0]


def _tcast(a, name):
    M, C = a.shape
    tm = 512

    def body(a_ref, o_ref):
        o_ref[...] = a_ref[...].astype(F32).T.astype(o_ref.dtype)

    return pl.pallas_call(
        body, name=name, grid=(M // tm,),
        in_specs=[pl.BlockSpec((tm, C), lambda i: (i, 0))],
        out_specs=pl.BlockSpec((C, tm), lambda i: (0, i)),
        out_shape=_sds((C, M), CDT), compiler_params=_cp(("parallel",)),
    )(a)


TOK = R16 * R16


def _swap16():
    i = lax.broadcasted_iota(jnp.int32, (TOK, TOK), 0)
    j = lax.broadcasted_iota(jnp.int32, (TOK, TOK), 1)
    return jnp.where(j == jnp.bitwise_and(i, 15) * 16 + jnp.right_shift(i, 4), 1.0, 0.0).astype(CDT)


def _norm_mod(x, norm_g, shift, scale, dep=None):
    S = x.shape[0]
    dep_specs, dep_args = _after(dep)

    def body(x_ref, g_ref, sh_ref, sc_ref, *rest):
        xv = x_ref[...]
        xn = xv * lax.rsqrt(jnp.mean(xv * xv, axis=-1, keepdims=True) + EPS)
        h = (xn * (g_ref[...] * (1.0 + sc_ref[...])) + sh_ref[...]).astype(CDT)
        _st(rest[-1], _dot(_swap16(), h, NN))

    vec = pl.BlockSpec((1, D), lambda i: (0, 0))
    return pl.pallas_call(
        body, name="norm_mod", grid=(S // TOK,),
        in_specs=[pl.BlockSpec((TOK, D), lambda i: (i, 0)), vec, vec, vec] + dep_specs,
        out_specs=pl.BlockSpec((R16, R16, D), lambda i: (0, i, 0)),
        out_shape=_sds((R16, S // R16, D), CDT), compiler_params=_cp(("parallel",)),
    )(x, norm_g, shift, scale, *dep_args)


def _t5_bucket_np(n):
    nf = np.maximum(n, 1).astype(np.float32)
    large = 16 + (np.log(nf / np.float32(16)) / np.float32(math.log(2048 / 16)) * np.float32(16)).astype(np.int32)
    large = np.minimum(large, 31)
    return np.where(n < 16, n, large).astype(np.int32)


class _Geom:
    def __init__(self, gi, LS):
        self.gi, self.LS = gi, LS
        self.dil = (1, 4, 16)[gi]
        if gi == 0:
            self.lead, self.rows, self.n_tiles, self.has_prev = (R16,), 16, LS // 16, True
            a = np.arange(R16)[:, None]
            ll = np.arange(16)[None, :]
            self.pos = (16 * ll + a).reshape(-1)
        elif gi == 1:
            self.lead, self.rows, self.n_tiles, self.has_prev = (4,), 64, LS // 64, True
            a = np.arange(4)[:, None]
            ll = np.arange(64)[None, :]
            self.pos = (4 * ll + a).reshape(-1)
        else:
            self.lead, self.rows, self.n_tiles, self.has_prev = (), LS, 1, False
            self.pos = np.arange(LS)
        self.T = self.pos.shape[0]
        self.Tk = 2 * self.T if self.has_prev else self.T
        self.outer = {0: (), 1: (4,), 2: (R16,)}[gi]

    def view(self, arr):
        S, C = arr.shape
        if self.gi == 1:
            return arr.reshape(4, 4, self.LS, C)
        return arr.reshape(R16, self.LS, C)

    def view_shape(self, C):
        return (4, 4, self.LS, C) if self.gi == 1 else (R16, self.LS, C)

    def unview(self, arr):
        return arr.reshape(R16 * self.LS, arr.shape[-1])

    def spec(self, W, cblk, tile):
        if self.gi == 0:
            return pl.BlockSpec((R16, 16, W), lambda n: (0, tile(n), cblk))
        if self.gi == 1:
            return pl.BlockSpec((4, None, 64, W), lambda r, n: (0, r, tile(n), cblk))
        return pl.BlockSpec((None, self.LS, W), lambda r: (r, 0, cblk))

    def halves(self):
        R = self.rows
        idx = np.arange(self.T).reshape(-1, R)
        return idx[:, :R // 2].reshape(-1), idx[:, R // 2:].reshape(-1)

    def bucket_index(self, sub):
        lo, hi = self.halves()
        pos = self.pos
        if sub == 0:
            pq = pos[lo]
            pk = np.concatenate([pos[hi] - self.T, pos[lo]]) if self.has_prev else np.concatenate([pos[lo], pos[hi]])
        else:
            pq = pos[hi]
            pk = np.concatenate([pos[lo], pos[hi]])
        dist = pq[:, None] - pk[None, :]
        ok = (dist >= 0) & (dist <= N_BACK)
        bucket = _t5_bucket_np(np.clip(dist, 0, N_BACK) * self.dil)
        return np.where(ok, bucket, -1).astype(np.int32)


def _bias_table(idx, buckets, rel_bias, gi, sub, dep=None):
    T, Tk = idx.shape
    tr = 32
    dep_specs, dep_args = _after(dep)

    def body(idx_ref, rb_ref, *rest):
        h = pl.program_id(0)
        ix = idx_ref[...]
        acc = jnp.full(ix.shape, NEG, F32)
        for b in buckets:
            acc = jnp.where(ix == b, rb_ref[b, gi * NH + h], acc)
        rest[-1][...] = acc

    return pl.pallas_call(
        body, name=f"bias_table{gi}{'ab'[sub]}", grid=(NH, T // tr),
        in_specs=[pl.BlockSpec((tr, Tk), lambda h, i: (i, 0)), pl.BlockSpec(memory_space=pltpu.SMEM)] + dep_specs,
        out_specs=pl.BlockSpec((None, tr, Tk), lambda h, i: (h, i, 0)),
        out_shape=_sds((NH, T, Tk), F32), compiler_params=_cp(("parallel", "parallel")),
    )(idx, rel_bias, *dep_args)


def _bias_grad(dsum, idx, buckets, gi, sub, dep=None):
    _, T, Tk = dsum.shape
    dep_specs, dep_args = _after(dep)

    def body(ds_ref, idx_ref, *rest):
        o_ref, r_ref = rest[-2], rest[-1]
        r_ref[...] = jnp.zeros_like(r_ref)
        ix, ds = idx_ref[...], ds_ref[...]
        for b in buckets:
            r_ref[b:b + 1, :] = jnp.sum(jnp.where(ix == b, ds, 0.0), axis=0, keepdims=True)
        o_ref[...] = jnp.broadcast_to(jnp.sum(r_ref[...], axis=1, keepdims=True), (32, 128))

    out = pl.pallas_call(
        body, name=f"bias_grad{gi}{'ab'[sub]}", grid=(NH,),
        in_specs=[pl.BlockSpec((None, T, Tk), lambda h: (h, 0, 0)), pl.BlockSpec((T, Tk), lambda h: (0, 0))] + dep_specs,
        out_specs=pl.BlockSpec((None, 32, 128), lambda h: (h, 0, 0)),
        out_shape=_sds((NH, 32, 128), F32), scratch_shapes=[pltpu.VMEM((32, Tk), F32)],
        compiler_params=_cp(("parallel",)),
    )(dsum, idx, *dep_args)
    return out[:, :, 0]


def _prev_mask(n, T, Tk):
    col = lax.broadcasted_iota(jnp.int32, (1, Tk), 1)
    return jnp.where((col < T) & (n == 0), NEG, 0.0).astype(F32)


def _split_rows(v, R):
    A, half = v.shape[0] // R, R // 2
    via_f32 = v.dtype != F32 and half % 16 != 0
    w = v.astype(F32) if via_f32 else v
    lo = jnp.concatenate([w[a * R:a * R + half] for a in range(A)], axis=0)
    hi = jnp.concatenate([w[a * R + half:(a + 1) * R] for a in range(A)], axis=0)
    return (lo.astype(v.dtype), hi.astype(v.dtype)) if via_f32 else (lo, hi)


def _merge_rows(lo, hi, R):
    half = R // 2
    A = lo.shape[0] // half
    return jnp.concatenate([x[a * half:(a + 1) * half] for a in range(A) for x in (lo, hi)], axis=0)


def _attn_fwd(qkv, biases, geom, others=None):
    S = qkv.shape[0]
    gi, T, R, has_prev = geom.gi, geom.T, geom.rows, geom.has_prev
    H2 = T // 2
    qv = geom.view(qkv)
    cq, ck, cv = 3 * gi, 3 * gi + 1, 3 * gi + 2
    assert others is None or not has_prev

    def body(*refs):
        if has_prev:
            q_ref, kp_ref, kc_ref, vp_ref, vc_ref, ba_ref, bb_ref, o_ref, l_ref = refs
        elif others is not None:
            (q_ref, kc_ref, vc_ref, ba_ref, bb_ref, o0_ref, o1_ref, l0_ref, l1_ref, z_ref,
             attn_ref, L_ref, ag_ref) = refs
        else:
            q_ref, kc_ref, vc_ref, ba_ref, bb_ref, o_ref, l_ref = refs
        q_lo, q_hi = _split_rows(_ld(q_ref), R)
        kc_lo, kc_hi = _split_rows(_ld(kc_ref), R)
        vc_lo, vc_hi = _split_rows(_ld(vc_ref), R)
        k_b = jnp.concatenate([kc_lo, kc_hi], axis=0)
        v_b = jnp.concatenate([vc_lo, vc_hi], axis=0)
        if has_prev:
            n = pl.program_id(len(geom.outer))
            k_a = jnp.concatenate([_split_rows(_ld(kp_ref), R)[1], kc_lo], axis=0)
            v_a = jnp.concatenate([_split_rows(_ld(vp_ref), R)[1], vc_lo], axis=0)
            pm = _prev_mask(n, H2, T)
        else:
            k_a, v_a, pm = k_b, v_b, None
        low = lax.broadcasted_iota(jnp.int32, (1, 2 * HD), 1) < HD
        for j in range(NH // 2):
            cols = slice(2 * HD * j, 2 * HD * (j + 1))
            o_half, l_half = [], []
            for q, k, v, b_ref, mask in ((q_lo, k_a, v_a, ba_ref, pm), (q_hi, k_b, v_b, bb_ref, None)):
                q2 = q[:, cols].astype(F32) * 0.125
                k2, v2 = k[:, cols], v[:, cols]
                qs = jnp.concatenate([jnp.where(low, q2, 0.0), jnp.where(low, 0.0, q2)], axis=0).astype(q.dtype)
                s = _dot(qs, k2, NT) + b_ref[j]
                if mask is not None:
                    s = s + mask
                m = jnp.max(s, axis=1, keepdims=True)
                p = jnp.exp(s - m)
                l = jnp.sum(p, axis=1, keepdims=True)
                pv = _dot(p.astype(v.dtype), v2, NN) / l
                lse = m + jnp.log(l)
                o_half.append(jnp.where(low, pv[:H2], pv[H2:]))
                l_half.append(jnp.where(low, lse[:H2], lse[H2:]))
            o2, lc = _merge_rows(o_half[0], o_half[1], R), _merge_rows(l_half[0], l_half[1], R)
            if others is None:
                _st(o_ref, o2, cols)
                _st(l_ref, lc, cols)
                continue
            la, lb = l0_ref[:, cols], l1_ref[:, cols]
            m = jnp.maximum(jnp.maximum(la, lb), lc)
            ea, eb, ec = jnp.exp(la - m), jnp.exp(lb - m), jnp.exp(lc - m)
            den = ea + eb + ec
            attn = (ea * o0_ref[:, cols] + eb * o1_ref[:, cols] + ec * o2) / den
            attn_ref[:, cols] = attn
            L_ref[:, cols] = m + jnp.log(den)
            z = z_ref[:, cols]
            ag = attn * (z * _sigmoid(z))
            ag_ref[:, cols] = ag.astype(ag_ref.dtype)

    cur = lambda n: n
    prev = lambda n: jnp.maximum(n - 1, 0)
    if has_prev:
        in_specs = [geom.spec(AW, cq, cur), geom.spec(AW, ck, prev), geom.spec(AW, ck, cur),
                    geom.spec(AW, cv, prev), geom.spec(AW, cv, cur)]
        args = [qv] * 5
    else:
        in_specs = [geom.spec(AW, cq, cur), geom.spec(AW, ck, cur), geom.spec(AW, cv, cur)]
        args = [qv] * 3
    nd = len(geom.outer) + (1 if has_prev else 0)
    biases = [b.reshape(NH // 2, 2 * H2, b.shape[2]) for b in biases]
    in_specs += [pl.BlockSpec(b.shape, lambda *ids: (0, 0, 0)) for b in biases]
    grid = geom.outer + ((geom.n_tiles,) if has_prev else ())
    oshape = geom.view_shape(AW)
    tile = geom.spec(AW, 0, cur)
    if others is None:
        o, l = pl.pallas_call(
            body, name=f"attn_fwd{gi}", grid=grid, in_specs=in_specs, out_specs=[tile, tile],
            out_shape=[_sds(oshape, F32), _sds(oshape, F32)],
            compiler_params=_cp(("arbitrary",) * nd),
        )(*args, *biases)
        return geom.unview(o), geom.unview(l)
    os_, ls_, rest = others
    attn, Lb, ag = pl.pallas_call(
        body, name=f"attn_fwd{gi}_merge", grid=grid, in_specs=in_specs + [tile] * 5,
        out_specs=[tile, tile, tile],
        out_shape=[_sds(oshape, F32), _sds(oshape, F32), _sds(oshape, CDT)],
        compiler_params=_cp(("arbitrary",) * nd),
    )(*args, *biases, *[geom.view(a) for a in (*os_, *ls_, rest)])
    return geom.unview(attn), geom.unview(Lb), geom.unview(ag)


def _pool_counts(r, LS, win):
    l = lax.broadcasted_iota(jnp.int32, (LS, 1), 0)
    return jnp.minimum(16 * l + (r + 1), win).astype(F32)


def _pool_fwd(rest, pool_w, pool_scale):
    S = rest.shape[0]
    LS = S // R16
    r3 = rest.reshape(R16, LS, REST_W)
    PG = 128

    def body(u_ref, z_ref, pw_ref, ps_ref, pt_ref, mp_ref, pg_ref, e_ref):
        g = pl.program_id(0)
        win = jnp.left_shift(2, g)
        row = lax.broadcasted_iota(jnp.int32, (LS, PG), 0)
        for r in range(R16):
            e_ref[R16 + r] = u_ref[r]
        for r in range(1, R16):
            e_ref[r] = jnp.where(row == 0, 0.0, pltpu.roll(u_ref[r], 1, 0))

        def double(s):
            for i in range(2 * R16 - 1, 2 * s - 1, -1):
                e_ref[i] = e_ref[i] + e_ref[i - s]

        double(1)
        for k in (1, 2, 3):
            pl.when(g >= k)(functools.partial(double, 2 ** k))
        pw = pw_ref[...].astype(CDT)
        for r in range(R16):
            u = u_ref[r]
            pooled = e_ref[R16 + r] / _pool_counts(r, LS, win) - u
            pc = pooled.astype(CDT)
            mp = _dot(pc, pw, NN)
            z = z_ref[r]
            pg = (mp * ps_ref[...]) * (z * _sigmoid(z))
            mp_ref[r] = mp
            pg_ref[r] = pg.astype(pg_ref.dtype)
            pt_ref[:, r * LS:(r + 1) * LS] = pooled.T.astype(pt_ref.dtype)

    col = lambda off: pl.BlockSpec((R16, LS, PG), lambda g: (0, 0, off + g))
    tsp = pl.BlockSpec((PG, S), lambda g: (g, 0))
    pt, mp, pg = pl.pallas_call(
        body, name="pool_fwd", grid=(4,),
        in_specs=[col(24), col(4), pl.BlockSpec((None, PG, PG), lambda g: (g, 0, 0)),
                  pl.BlockSpec((1, PG), lambda g: (0, g))],
        out_specs=[tsp, col(0), col(0)],
        out_shape=[_in_hbm((AW, S), CDT), _sds((R16, LS, AW), F32), _sds((R16, LS, AW), CDT)],
        scratch_shapes=[pltpu.VMEM((2 * R16, LS, PG), F32)],
        compiler_params=_cp(("parallel",)),
    )(r3, r3, pool_w, pool_scale)
    return pt, mp.reshape(S, AW), pg.reshape(S, AW)


def _merge_out_loss(ag, pg, rest, wab, wpb, wout, x, target, gate, final_g):
    S = x.shape[0]
    LS = S // R16

    def body(ag_ref, pg_ref, ga_ref, gp_ref, wa_ref, wp_ref, w_ref, x_ref, t_ref, gate_ref, fg_ref,
             dx2_ref, dmo_ref, loss_ref, gfg_ref, dgate_ref, dw_ref):
        i = pl.program_id(0)

        @pl.when(i == 0)
        def _():
            loss_ref[...] = jnp.zeros_like(loss_ref)
            gfg_ref[...] = jnp.zeros_like(gfg_ref)
            dgate_ref[...] = jnp.zeros_like(dgate_ref)
            dw_ref[...] = jnp.zeros_like(dw_ref)

        swap = _swap16()
        ya = _dot(_ld(ag_ref), wa_ref[...], NN)
        yp = _dot(_ld(pg_ref), wp_ref[...], NN)
        mg = (_sigmoid(_ld(ga_ref)) * ya + _sigmoid(_ld(gp_ref)) * yp).astype(CDT)
        m_nat = _dot(swap, mg, NN)
        mo = _dot(m_nat.astype(CDT), w_ref[...], NN)
        gate, fg = gate_ref[...], fg_ref[...]
        x2 = x_ref[...] + gate * mo
        r2 = lax.rsqrt(jnp.mean(x2 * x2, axis=-1, keepdims=True) + EPS)
        xn2 = x2 * r2
        diff = xn2 * fg - t_ref[...]
        loss = jnp.sum(jnp.sum(diff * diff, axis=-1, keepdims=True), axis=0, keepdims=True)
        dy = diff * (1.0 / D)
        dxn2 = dy * fg
        dx2 = r2 * (dxn2 - xn2 * jnp.mean(dxn2 * xn2, axis=-1, keepdims=True))
        dx2_ref[...] = dx2
        dmo = (dx2 * gate).astype(CDT)
        _st(dmo_ref, _dot(swap, dmo, NN))
        loss_ref[...] += jnp.broadcast_to(loss * (0.5 / D), loss_ref.shape)
        gfg_ref[...] += jnp.sum(dy * xn2, axis=0, keepdims=True)
        dgate_ref[...] += jnp.sum(dx2 * mo, axis=0, keepdims=True)
        dw_ref[...] += _dot(m_nat.T.astype(CDT), dmo, NN)

    def slab(w, cblk):
        return pl.BlockSpec((R16, R16, w), lambda i: (0, i, cblk))

    nat = pl.BlockSpec((TOK, D), lambda i: (i, 0))
    vec = pl.BlockSpec((1, D), lambda i: (0, 0))
    r3 = rest.reshape(R16, LS, REST_W)
    return pl.pallas_call(
        body, name="merge_out_loss", grid=(S // TOK,),
        in_specs=[slab(AW, 0), slab(AW, 0), slab(D, 1), slab(D, 2), _resident((AW, D)), _resident((AW, D)),
                  _resident((D, D)), nat, nat, vec, vec],
        out_specs=[nat, slab(D, 0), pl.BlockSpec((1, 128), lambda i: (0, 0)), vec, vec,
                   pl.BlockSpec((D, D), lambda i: (0, 0))],
        out_shape=[_sds((S, D), F32), _sds((R16, LS, D), CDT), _in_hbm((1, 128), F32), _in_hbm((1, D), F32),
                   _in_hbm((1, D), F32), _sds((D, D), F32)],
        compiler_params=_cp(("arbitrary",)),
    )(ag.reshape(R16, LS, AW), pg.reshape(R16, LS, AW), r3, r3, wab, wpb, wout, x, target, gate, final_g)


def _bwd_gates(dmo, wout, wab, wpb, rest, ag, pg, attn, mp, pool_scale):
    S = dmo.shape[0]
    tm = 256

    def body(dmo_ref, wo_ref, wa_ref, wp_ref, za_ref, zp_ref, ga0, ga1, gp0, gp1, ag_ref, pg_ref, at_ref, mp_ref,
             ps_ref, dwa_ref, dwp_ref, dr_ref, dat_ref, dl_ref, dmp_ref, dps_ref):
        i = pl.program_id(0)

        @pl.when(i == 0)
        def _():
            dps_ref[...] = jnp.zeros_like(dps_ref)
            dwa_ref[...] = jnp.zeros_like(dwa_ref)
            dwp_ref[...] = jnp.zeros_like(dwp_ref)

        dm = _dot(dmo_ref[...], wo_ref[...], NT)
        sa = _sigmoid(jnp.concatenate([ga0[...], ga1[...]], axis=1))
        sp = _sigmoid(jnp.concatenate([gp0[...], gp1[...]], axis=1))
        dya = (dm * sa).astype(CDT)
        dyp = (dm * sp).astype(CDT)
        ag, pg = ag_ref[...], pg_ref[...]
        dwa_ref[...] += _dot(ag.astype(F32).T.astype(CDT), dya, NN)
        dwp_ref[...] += _dot(pg.astype(F32).T.astype(CDT), dyp, NN)
        ya = _dot(ag, wa_ref[...], NN)
        yp = _dot(pg, wp_ref[...], NN)
        dr_ref[:, 1024:2048] = (dm * ya * sa * (1.0 - sa)).astype(dr_ref.dtype)
        dr_ref[:, 2048:3072] = (dm * yp * sp * (1.0 - sp)).astype(dr_ref.dtype)
        da = _dot(dya, wa_ref[...], NT)
        dp = _dot(dyp, wp_ref[...], NT)
        za = za_ref[...]
        sga = _sigmoid(za)
        attn = at_ref[...]
        dattn = da * (za * sga)
        dr_ref[:, 0:512] = (da * attn * (sga * (1.0 + za * (1.0 - sga)))).astype(dr_ref.dtype)
        dat_ref[...] = dattn.astype(dat_ref.dtype)
        prod = dattn * attn
        low = lax.broadcasted_iota(jnp.int32, (1, 2 * HD), 1) < HD
        for j in range(NH // 2):
            x2 = prod[:, 2 * HD * j:2 * HD * (j + 1)]
            s0 = jnp.sum(jnp.where(low, x2, 0.0), axis=1, keepdims=True)
            s1 = jnp.sum(jnp.where(low, 0.0, x2), axis=1, keepdims=True)
            dl_ref[:, 2 * HD * j:2 * HD * (j + 1)] = jnp.where(low, s0, s1)
        zp = zp_ref[...]
        sgp = _sigmoid(zp)
        mpre = mp_ref[...]
        ps = ps_ref[...]
        dmixed = dp * (zp * sgp)
        dr_ref[:, 512:1024] = (dp * (mpre * ps) * (sgp * (1.0 + zp * (1.0 - sgp)))).astype(dr_ref.dtype)
        dps_ref[...] += jnp.sum(dmixed * mpre, axis=0, keepdims=True)
        dmp_ref[...] = (dmixed * ps).astype(dmp_ref.dtype)

    full = lambda shape: pl.BlockSpec(shape, lambda i: (0, 0))
    tD = pl.BlockSpec((tm, D), lambda i: (i, 0))
    tA = pl.BlockSpec((tm, AW), lambda i: (i, 0))
    rb = lambda c: pl.BlockSpec((tm, AW), lambda i: (i, c))
    return pl.pallas_call(
        body, name="bwd_gates", grid=(S // tm,),
        in_specs=[tD, full((D, D)), full((AW, D)), full((AW, D)), rb(0), rb(1), rb(2), rb(3), rb(4), rb(5),
                  tA, tA, tA, tA, full((1, AW))],
        out_specs=[full((AW, D)), full((AW, D)), pl.BlockSpec((tm, 3072), lambda i: (i, 0)), tA, tA, tA, full((1, AW))],
        out_shape=[_sds((AW, D), F32), _sds((AW, D), F32), _sds((S, REST_W), CDT), _sds((S, AW), CDT),
                   _sds((S, AW), F32), _sds((S, AW), CDT), _sds((1, AW), F32)],
        compiler_params=_cp(("arbitrary",), vmem=56),
    )(dmo, wout, wab, wpb, rest, rest, rest, rest, rest, rest, ag, pg, attn, mp, pool_scale)


def _pool_bwd(dmp, pt, pool_w, drest):
    S = dmp.shape[0]
    LS = S // R16
    PG = 128
    d3 = dmp.reshape(R16, LS, AW)
    dr3 = drest.reshape(R16, LS, REST_W)

    def body(d_ref, pt_ref, pw_ref, dr_in, du_ref, dpw_ref, g_ref, s_ref):
        del dr_in
        g = pl.program_id(0)
        win = jnp.left_shift(2, g)
        row = lax.broadcasted_iota(jnp.int32, (LS, PG), 0)
        pw = pw_ref[...].astype(CDT)
        dpw = jnp.zeros((PG, PG), F32)
        for r in range(R16):
            dm = d_ref[r]
            dpw = dpw + _dot(pt_ref[:, r * LS:(r + 1) * LS], dm, NN)
            dpooled = _dot(dm, pw, NT)
            g_ref[r] = dpooled
            sc = dpooled / _pool_counts(r, LS, win)
            s_ref[r] = sc
            if r < R16 - 1:
                s_ref[R16 + r] = jnp.where(row == LS - 1, 0.0, pltpu.roll(sc, LS - 1, 0))
        dpw_ref[...] = dpw

        def double(s):
            for i in range(0, 2 * R16 - 2 * s):
                s_ref[i] = s_ref[i] + s_ref[i + s]

        double(1)
        for k in (1, 2, 3):
            pl.when(g >= k)(functools.partial(double, 2 ** k))
        for r in range(R16):
            du_ref[r] = (s_ref[r] - g_ref[r]).astype(du_ref.dtype)

    du, dpw = pl.pallas_call(
        body, name="pool_bwd", grid=(4,),
        in_specs=[pl.BlockSpec((R16, LS, PG), lambda g: (0, 0, g)), pl.BlockSpec((PG, S), lambda g: (g, 0)),
                  pl.BlockSpec((None, PG, PG), lambda g: (g, 0, 0)), pl.BlockSpec(memory_space=pl.ANY)],
        out_specs=[pl.BlockSpec((R16, LS, PG), lambda g: (0, 0, 24 + g)),
                   pl.BlockSpec((None, PG, PG), lambda g: (g, 0, 0))],
        out_shape=[_sds((R16, LS, REST_W), CDT), _sds((4, PG, PG), F32)],
        scratch_shapes=[pltpu.VMEM((R16, LS, PG), F32), pltpu.VMEM((2 * R16, LS, PG), F32)],
        input_output_aliases={3: 0},
        compiler_params=_cp(("parallel",)),
    )(d3, pt, pool_w, dr3)
    return du.reshape(S, REST_W), dpw


def _attn_bwd(qkv, dattn, Lb, dlb, biases, geom, dqkv_prev):
    S = qkv.shape[0]
    gi, T, R, has_prev, N = geom.gi, geom.T, geom.rows, geom.has_prev, geom.n_tiles
    H2 = T // 2
    nd_outer = len(geom.outer)
    qv = geom.view(qkv)
    cq, ck, cv = 3 * gi, 3 * gi + 1, 3 * gi + 2
    W3 = 3 * AW

    def heads(q_ref, kp_ref, kc_ref, vp_ref, vc_ref, do_ref, L_ref, dl_ref, ba_ref, bb_ref, pm, dsa_ref, dsb_ref, emit):
        q_lo, q_hi = _split_rows(_ld(q_ref), R)
        do_lo, do_hi = _split_rows(_ld(do_ref), R)
        L_lo, L_hi = _split_rows(_ld(L_ref), R)
        dl_lo, dl_hi = _split_rows(_ld(dl_ref), R)
        kc_lo, kc_hi = _split_rows(_ld(kc_ref), R)
        vc_lo, vc_hi = _split_rows(_ld(vc_ref), R)
        k_b = jnp.concatenate([kc_lo, kc_hi], axis=0)
        v_b = jnp.concatenate([vc_lo, vc_hi], axis=0)
        if has_prev:
            k_a = jnp.concatenate([_split_rows(_ld(kp_ref), R)[1], kc_lo], axis=0)
            v_a = jnp.concatenate([_split_rows(_ld(vp_ref), R)[1], vc_lo], axis=0)
        else:
            k_a, v_a = k_b, v_b
        low = lax.broadcasted_iota(jnp.int32, (1, 2 * HD), 1) < HD
        for j in range(NH // 2):
            sl = slice(2 * HD * j, 2 * HD * (j + 1))
            res = []
            for q, k, v, do, L, dl, b_ref, mask, ds_ref in (
                    (q_lo, k_a, v_a, do_lo, L_lo, dl_lo, ba_ref, pm, dsa_ref),
                    (q_hi, k_b, v_b, do_hi, L_hi, dl_hi, bb_ref, None, dsb_ref)):
                q2 = q[:, sl].astype(F32) * 0.125
                do2 = do[:, sl].astype(F32)
                k2, v2 = k[:, sl], v[:, sl]
                L2, dl2 = L[:, sl], dl[:, sl]
                qs = jnp.concatenate([jnp.where(low, q2, 0.0), jnp.where(low, 0.0, q2)], axis=0).astype(q.dtype)
                dom = jnp.concatenate([jnp.where(low, do2, 0.0), jnp.where(low, 0.0, do2)], axis=0).astype(q.dtype)
                Lc = jnp.concatenate([L2[:, 0:1], L2[:, HD:HD + 1]], axis=0)
                dlc = jnp.concatenate([dl2[:, 0:1], dl2[:, HD:HD + 1]], axis=0)
                s = _dot(qs, k2, NT) + b_ref[j]
                if mask is not None:
                    s = s + mask
                p = jnp.exp(s - Lc)
                ds = p * (_dot(dom, v2, NT) - dlc)
                ds_ref[j] += ds
                dsc = ds.astype(q.dtype)
                dq_st = _dot(dsc, k2, NN)
                res.append((jnp.where(low, dq_st[:H2], dq_st[H2:]) * 0.125, _dot(dsc, qs, TN),
                            _dot(p.astype(q.dtype), dom, TN)))
            (dq_lo, dk_a, dv_a), (dq_hi, dk_b, dv_b) = res
            dq = _merge_rows(dq_lo, dq_hi, R)
            if has_prev:
                emit(sl, dq, _merge_rows(dk_a[H2:] + dk_b[:H2], dk_b[H2:], R),
                     _merge_rows(dv_a[H2:] + dv_b[:H2], dv_b[H2:], R), dk_a[:H2], dv_a[:H2])
            else:
                emit(sl, dq, _merge_rows(dk_a[:H2] + dk_b[:H2], dk_a[H2:] + dk_b[H2:], R),
                     _merge_rows(dv_a[:H2] + dv_b[:H2], dv_a[H2:] + dv_b[H2:], R), None, None)

    def first_step():
        ok = pl.program_id(nd_outer) == 0 if has_prev else pl.program_id(0) == 0
        for a in range(nd_outer if has_prev else 0):
            ok = ok & (pl.program_id(a) == 0)
        return ok

    if has_prev:
        def body(q_ref, kp_ref, kc_ref, vp_ref, vc_ref, do_ref, L_ref, dl_ref, ba_ref, bb_ref, prev_in, out_ref,
                 dsa_ref, dsb_ref, hold, new, pp):
            del prev_in
            s_id = pl.program_id(nd_outer)

            @pl.when(first_step())
            def _():
                dsa_ref[...] = jnp.zeros_like(dsa_ref)
                dsb_ref[...] = jnp.zeros_like(dsb_ref)

            @pl.when(s_id < N)
            def _():
                def emit(sl, dq, dk, dv, dk_prev, dv_prev):
                    new[:, sl] = dq
                    new[:, slice(AW + sl.start, AW + sl.stop)] = dk
                    new[:, slice(2 * AW + sl.start, 2 * AW + sl.stop)] = dv
                    pp[:, sl] = dk_prev
                    pp[:, slice(AW + sl.start, AW + sl.stop)] = dv_prev

                heads(q_ref, kp_ref, kc_ref, vp_ref, vc_ref, do_ref, L_ref, dl_ref, ba_ref, bb_ref,
                      _prev_mask(s_id, H2, T), dsa_ref, dsb_ref, emit)

            @pl.when((s_id >= 1) & (s_id < N))
            def _():
                _st(out_ref, hold[:, 0:AW], slice(0, AW))
                late = _merge_rows(jnp.zeros((H2, 2 * AW), F32), pp[...], R)
                _st(out_ref, hold[:, AW:W3] + late, slice(AW, W3))

            @pl.when(s_id == N)
            def _():
                _st(out_ref, hold[...])

            @pl.when(s_id < N)
            def _():
                hold[...] = new[...]

        cur = lambda n: jnp.minimum(n, N - 1)
        prev = lambda n: jnp.clip(n - 1, 0, N - 1)
        delayed = lambda n: jnp.maximum(n - 1, 0)
        in_specs = [geom.spec(AW, cq, cur), geom.spec(AW, ck, prev), geom.spec(AW, ck, cur),
                    geom.spec(AW, cv, prev), geom.spec(AW, cv, cur),
                    geom.spec(AW, 0, cur), geom.spec(AW, 0, cur), geom.spec(AW, 0, cur)]
        args = [qv] * 5 + [geom.view(dattn), geom.view(Lb), geom.view(dlb)]
        out_spec = geom.spec(W3, gi, delayed)
        grid = geom.outer + (N + 1,)
        scratch = [pltpu.VMEM((T, W3), F32), pltpu.VMEM((T, W3), F32), pltpu.VMEM((H2, 2 * AW), F32)]
    else:
        def body(q_ref, kc_ref, vc_ref, do_ref, L_ref, dl_ref, ba_ref, bb_ref, prev_in, out_ref, dsa_ref, dsb_ref):
            del prev_in

            @pl.when(first_step())
            def _():
                dsa_ref[...] = jnp.zeros_like(dsa_ref)
                dsb_ref[...] = jnp.zeros_like(dsb_ref)

            def emit(sl, dq, dk, dv, dk_prev, dv_prev):
                _st(out_ref, dq, sl)
                _st(out_ref, dk, slice(AW + sl.start, AW + sl.stop))
                _st(out_ref, dv, slice(2 * AW + sl.start, 2 * AW + sl.stop))

            heads(q_ref, None, kc_ref, None, vc_ref, do_ref, L_ref, dl_ref, ba_ref, bb_ref, None, dsa_ref, dsb_ref,
                  emit)

        cur = lambda n: n
        in_specs = [geom.spec(AW, cq, cur), geom.spec(AW, ck, cur), geom.spec(AW, cv, cur),
                    geom.spec(AW, 0, cur), geom.spec(AW, 0, cur), geom.spec(AW, 0, cur)]
        args = [qv] * 3 + [geom.view(dattn), geom.view(Lb), geom.view(dlb)]
        out_spec = geom.spec(W3, gi, cur)
        grid = geom.outer
        scratch = []

    whole = lambda b: pl.BlockSpec(b.shape, lambda *ids: (0, 0, 0))
    biases = [b.reshape(NH // 2, 2 * H2, b.shape[2]) for b in biases]
    in_specs += [whole(b) for b in biases]
    in_specs.append(pl.BlockSpec(memory_space=pl.ANY))
    if dqkv_prev is None:
        last, aliases = biases[0], {}
    else:
        last, aliases = geom.view(dqkv_prev), {len(in_specs) - 1: 0}
    dq, dsa, dsb = pl.pallas_call(
        body, name=f"attn_bwd{gi}", grid=grid, in_specs=in_specs,
        out_specs=[out_spec] + [whole(b) for b in biases],
        out_shape=[_sds(geom.view_shape(QKV_W), CDT)] + [_sds(b.shape, F32) for b in biases],
        scratch_shapes=scratch,
        input_output_aliases=aliases,
        compiler_params=_cp(("arbitrary",) * len(grid), vmem=56),
    )(*args, *biases, last)
    return geom.unview(dq), [d.reshape(NH, H2, d.shape[2]) for d in (dsa, dsb)]


def _local_step(x, target, shift, scale, gate, norm_g, w_in_full, pool_w, pool_scale, wab, wpb, wout, rel_bias,
                final_g):
    st = _local_bias_tables(_local_pre(x, shift, scale, norm_g), rel_bias)
    st = _local_grads(st, target, gate, w_in_full, pool_w, pool_scale, wab, wpb, wout, final_g)
    d_rel_bias = _local_bias_grads(st, _BIAS_GRADS_EARLY + _BIAS_GRADS_LATE)
    gx, dmod, dng = _local_input_grads(st, w_in_full, norm_g, scale)
    return dict(loss=st["loss"], grad_x=gx, dw_in=st["dw_in"], dw_ab=st["dw_ab"], dw_pb=st["dw_pb"],
                dw_out=st["dw_out"], d_norm_g=dng, d_pool_w=st["dpw"], d_pool_scale=st["dps"],
                d_rel_bias=d_rel_bias.T, d_final_g=st["gfg"], dmod=dmod)


def _local_pre(x, shift, scale, norm_g, dep=None):
    S = x.shape[0]
    h = _norm_mod(x, norm_g, shift, scale, dep).reshape(S, D)
    return dict(x=x, h=h, ht=_tcast(h, "h_transpose"))


def _local_bias_tables(st, rel_bias, dep=None):
    LS = st["x"].shape[0] // R16
    geoms = [_Geom(gi, LS) for gi in range(3)]
    idx_np = [[g.bucket_index(sub) for sub in range(2)] for g in geoms]
    buckets = [[tuple(int(b) for b in np.unique(ix) if b >= 0) for ix in pair] for pair in idx_np]
    idxs = [[jnp.asarray(ix) for ix in pair] for pair in idx_np]
    biases = [[_bias_table(idxs[gi][sub], buckets[gi][sub], rel_bias, gi, sub, dep) for sub in range(2)]
              for gi in range(3)]
    return dict(st, geoms=geoms, idxs=idxs, buckets=buckets, biases=biases)


def _local_grads(st, target, gate, w_in_full, pool_w, pool_scale, wab, wpb, wout, final_g):
    geoms, idxs, biases, x, h, ht = (st[k] for k in ("geoms", "idxs", "biases", "x", "h", "ht"))
    S = h.shape[0]
    LS = S // R16
    qkv, rest = _proj(h, w_in_full)

    os_, ls_ = [], []
    for gi in range(2):
        o, l = _attn_fwd(qkv, biases[gi], geoms[gi])
        os_.append(o)
        ls_.append(l)
    attn, Lb, ag = _attn_fwd(qkv, biases[2], geoms[2], others=(os_, ls_, rest))
    pt, mp, pg = _pool_fwd(rest, pool_w, pool_scale)
    dx2, dmo3, loss, gfg, dgate, dw_out = _merge_out_loss(ag, pg, rest, wab, wpb, wout, x, target, gate, final_g)
    dmo = dmo3.reshape(S, D)

    dw_ab, dw_pb, drest, dattn, dlb, dmp, dps = _bwd_gates(dmo, wout, wab, wpb, rest, ag, pg, attn, mp, pool_scale)
    drest, dpw = _pool_bwd(dmp, pt, pool_w, drest)

    dqkv = None
    dsums = []
    for gi in range(3):
        dqkv, dsum = _attn_bwd(qkv, dattn, Lb, dlb, biases[gi], geoms[gi], dqkv)
        dsums.append(dsum)

    dw_in = _mm_tn(ht, dqkv, n_total=8192, out_blk=lambda j: j, name="dw_in_qkv", with_low=True)
    dw_in, dw_in_low = _mm_tn(ht, drest, n_total=8192, out_blk=_rest_blk, name="dw_in_rest", prev=dw_in, with_low=True)
    return dict(st, loss=loss[0, 0], dw_in=dw_in, dw_in_low=dw_in_low, dw_ab=dw_ab, dw_pb=dw_pb, dw_out=dw_out, dpw=dpw, dps=dps, gfg=gfg,
                dgate=dgate, dsums=dsums, dqkv=dqkv, drest=drest, dx2=dx2)


def _local_bias_grads(st, which, dep=None):
    rows = []
    for gi in range(3):
        got = [_bias_grad(st["dsums"][gi][sub], st["idxs"][gi][sub], st["buckets"][gi][sub], gi, sub, dep)
               for sub in range(2) if (gi, sub) in which]
        rows.append(sum(got) if got else jnp.zeros((NH, 32), F32))
    return jnp.concatenate(rows, axis=0)


_BIAS_GRADS_EARLY = ((0, 0), (0, 1), (1, 0), (2, 0))
_BIAS_GRADS_LATE = ((1, 1), (2, 1))


def _dh_h_bwd(dqkv, drest, w, dx2, x, norm_g, scale, dep=None):
    S = x.shape[0]
    dep_specs, dep_args = _after(dep)

    def body(a_ref, b_ref, w_ref, dx2_ref, x_ref, g_ref, sc_ref, *rest):
        gx_ref, dsh_ref, dsc_ref, dng_ref, nat_ref = rest[-5:]
        i = pl.program_id(0)

        @pl.when(i == 0)
        def _():
            dsh_ref[...] = jnp.zeros_like(dsh_ref)
            dsc_ref[...] = jnp.zeros_like(dsc_ref)
            dng_ref[...] = jnp.zeros_like(dng_ref)

        a, b = _ld(a_ref), _ld(b_ref)
        acc = _dot(a[:, 0:512], _w512(w_ref, 0), NT)
        for j in range(1, 9):
            acc = acc + _dot(a[:, 512 * j:512 * (j + 1)], _w512(w_ref, j), NT)
        off = 0
        for lo, hi in REST_COLS:
            for j in range(lo, hi):
                acc = acc + _dot(b[:, off:off + 512], _w512(w_ref, j), NT)
                off += 512
        for r in range(R16):
            nat_ref[:, r, :] = acc[R16 * r:R16 * (r + 1)]
        dh = jnp.concatenate([nat_ref[l] for l in range(R16)], axis=0)
        g = g_ref[...]
        one_sc = 1.0 + sc_ref[...]
        xv = x_ref[...]
        r1 = lax.rsqrt(jnp.mean(xv * xv, axis=-1, keepdims=True) + EPS)
        xn = xv * r1
        dhx = dh * xn
        dxn = dh * (g * one_sc)
        dx1 = r1 * (dxn - xn * jnp.mean(dxn * xn, axis=-1, keepdims=True))
        gx_ref[...] = dx2_ref[...] + dx1
        dsh_ref[...] += jnp.sum(dh, axis=0, keepdims=True)
        dsc_ref[...] += jnp.sum(dhx * g, axis=0, keepdims=True)
        dng_ref[...] += jnp.sum(dhx * one_sc, axis=0, keepdims=True)

    nat = pl.BlockSpec((TOK, D), lambda i: (i, 0))
    vec = pl.BlockSpec((1, D), lambda i: (0, 0))
    slab = lambda W: pl.BlockSpec((R16, R16, W), lambda i: (0, i, 0))
    return pl.pallas_call(
        body, name="dh_h_bwd", grid=(S // TOK,),
        in_specs=[slab(QKV_W), slab(REST_W), _resident((8, D, 1024)), nat, nat, vec, vec] + dep_specs,
        out_specs=[nat, vec, vec, vec],
        out_shape=[_sds((S, D), F32), _sds((1, D), F32), _sds((1, D), F32), _sds((1, D), F32)],
        scratch_shapes=[pltpu.VMEM((R16, R16, D), F32)],
        compiler_params=_cp(("arbitrary",), vmem=56),
    )(dqkv, drest, w, dx2, x, norm_g, scale, *dep_args)


def _local_input_grads(st, w_in_full, norm_g, scale, dep=None):
    S = st["h"].shape[0]
    LS = S // R16
    gx, dsh, dsc, dng = _dh_h_bwd(st["dqkv"].reshape(R16, LS, QKV_W), st["drest"].reshape(R16, LS, REST_W),
                                  w_in_full, st["dx2"], st["x"], norm_g, scale, dep)
    return gx, jnp.concatenate([dsh, dsc, st["dgate"]], axis=1), dng


def _my_place():
    return lax.axis_index("x"), lax.axis_index("y"), lax.axis_index("c")


def _block_of(ref, axis, idx, width):
    if axis is None:
        return ref.at[idx]
    start = pl.multiple_of(idx * width, width)
    if axis == 0:
        return ref.at[pl.ds(start, width)]
    return ref.at[:, pl.ds(start, width)]


_SEM = pl.BlockSpec(memory_space=pltpu.SEMAPHORE)
_EFFECT = pltpu.SideEffectType.DATAFLOW_SIDE_EFFECTING


def _split_start(name, srcs, lands, ncopy, plan, dep=None):
    arrs = list(srcs) + list(lands)
    na, ns = len(arrs), len(srcs)
    dep_specs, dep_args = _after(dep)

    def body(*refs):
        send_sems, recv_sems, token = refs[na + len(dep_args)], refs[na + len(dep_args) + 1], refs[-1]
        for k, (s, d, dev) in enumerate(plan(refs[:ns], refs[ns:na])):
            pltpu.make_async_remote_copy(src_ref=s, dst_ref=d, send_sem=send_sems.at[k], recv_sem=recv_sems.at[k],
                                         device_id=dev, device_id_type=MESH).start()
        token[...] = jnp.zeros_like(token)

    outs = pl.pallas_call(
        body, name=name,
        out_shape=(pltpu.SemaphoreType.DMA((ncopy,)), pltpu.SemaphoreType.DMA((ncopy,)),
                   *[pltpu.HBM(a.shape, a.dtype) for a in arrs], _sds((8, 128), F32)),
        in_specs=[_HBM] * na + dep_specs,
        out_specs=(_SEM, _SEM, *[_HBM] * na, pl.BlockSpec(memory_space=pltpu.VMEM)),
        input_output_aliases={i: 2 + i for i in range(na)},
        compiler_params=pltpu.CompilerParams(has_side_effects=_EFFECT),
    )(*[pltpu.with_memory_space_constraint(a, pltpu.HBM) for a in arrs], *dep_args)
    return (outs[0], outs[1]), list(outs[2:2 + ns]), list(outs[2 + ns:2 + na]), outs[-1]


def _split_wait(name, sems, srcs, lands, plan, after):
    arrs = list(srcs) + list(lands)
    na, ns = len(arrs), len(srcs)
    after = list(after) if isinstance(after, (list, tuple)) else [after]

    def body(*refs):
        send_sems, recv_sems = refs[na], refs[na + 1]
        for k, (s, d, dev) in enumerate(plan(refs[:ns], refs[ns:na])):
            cp = pltpu.make_async_remote_copy(src_ref=s, dst_ref=d, send_sem=send_sems.at[k], recv_sem=recv_sems.at[k],
                                              device_id=dev, device_id_type=MESH)
            cp.wait_send()
            cp.wait_recv()

    outs = pl.pallas_call(
        body, name=name, out_shape=[pltpu.HBM(a.shape, a.dtype) for a in arrs],
        in_specs=[_HBM] * na + [_SEM, _SEM] + [pl.BlockSpec(memory_space=pl.ANY)] * len(after),
        out_specs=[_HBM] * na,
        input_output_aliases={i: i for i in range(na)},
        compiler_params=pltpu.CompilerParams(has_side_effects=_EFFECT),
    )(*arrs, sems[0], sems[1], *after)
    return list(outs[:ns]), list(outs[ns:])


def _gather_chips_plan(layout):
    def plan(src, land):
        x, y, c = _my_place()
        me = 4 * x + 2 * y + c
        mine = [_block_of(land[p], layout[p][0], me, layout[p][1]) for p in range(len(land))]
        return [(mine[p], mine[p], (cx, cy, c)) for p in range(len(land)) for cx, cy in [(1 - x, y), (x, 1 - y)]]
    return plan


def _pair_plan(layout):
    def plan(src, land):
        x, y, c = _my_place()
        n = len(layout)
        return [(_block_of(src[p], layout[p][0], 2 * q + (1 - c), layout[p][1]), land[p].at[q], (x, y, 1 - c))
                for p in range(n) for q in range(4)] + [(src[p], land[p], (x, y, 1 - c)) for p in range(n, len(src))]
    return plan


def _chips_plan(n):
    def plan(src, land):
        x, y, c = _my_place()
        chips = [(1 - x, y), (x, 1 - y), (1 - x, 1 - y)]
        return [(src[p].at[2 * cx + cy], land[p].at[2 * x + y], (cx, cy, c)) for p in range(n) for cx, cy in chips] + \
               [(src[p], land[p].at[2 * x + y], (cx, cy, c)) for p in range(n, len(src)) for cx, cy in chips]
    return plan


def _pair_share(gathered):
    n = len(gathered)

    def body(*refs):
        land = refs[n:2 * n]
        send_sems, recv_sems, fwd_send, fwd_recv = refs[2 * n:]
        x, y, c = _my_place()
        k_dg = 3 - (2 * x + y)
        south = c == 0
        pass_on = 4 * jnp.where(south, 1 - x, x) + 2 * jnp.where(south, y, 1 - y) + c
        to = (jnp.where(south, x, 1 - x), jnp.where(south, 1 - y, y), c)

        def forward(p):
            return pltpu.make_async_remote_copy(
                src_ref=land[p].at[pass_on], dst_ref=land[p].at[pass_on], send_sem=fwd_send.at[p],
                recv_sem=fwd_recv.at[p], device_id=to, device_id_type=MESH)

        def to_sibling(p, k):
            blk = land[p].at[2 * k + c]
            return pltpu.make_async_remote_copy(
                src_ref=blk, dst_ref=blk, send_sem=send_sems.at[4 * p + k], recv_sem=recv_sems.at[4 * p + k],
                device_id=(x, y, 1 - c), device_id_type=MESH)

        for p in range(n):
            forward(p).start()
        for p in range(n):
            for k in range(4):
                @pl.when(k != k_dg)
                def _():
                    to_sibling(p, k).start()
        for p in range(n):
            dg = land[p].at[2 * k_dg + c]
            pltpu.make_async_remote_copy(src_ref=dg, dst_ref=dg, send_sem=fwd_send.at[p], recv_sem=fwd_recv.at[p],
                                         device_id=to, device_id_type=MESH).wait_recv()
            for k in range(4):
                @pl.when(k == k_dg)
                def _():
                    to_sibling(p, k).start()
        for p in range(n):
            forward(p).wait_send()
            for k in range(4):
                to_sibling(p, k).wait_send()
                blk = land[p].at[2 * k + 1 - c]
                pltpu.make_async_remote_copy(
                    src_ref=blk, dst_ref=blk, send_sem=send_sems.at[4 * p + k], recv_sem=recv_sems.at[4 * p + k],
                    device_id=(x, y, 1 - c), device_id_type=MESH).wait_recv()

    return pl.pallas_call(
        body, name="gather_pair_share", in_specs=[_HBM] * n, out_specs=[_HBM] * n,
        out_shape=[_sds(g.shape, g.dtype) for g in gathered],
        input_output_aliases={p: p for p in range(n)},
        scratch_shapes=[pltpu.SemaphoreType.DMA((4 * n,)), pltpu.SemaphoreType.DMA((4 * n,)),
                        pltpu.SemaphoreType.DMA((n,)), pltpu.SemaphoreType.DMA((n,))],
    )(*gathered)


def _place_slot(arrs, slot, n_slots, name, out_dtype=None):
    n = len(arrs)

    def body(s_ref, *refs):
        for p in range(n):
            refs[n + p][...] = refs[p][...].astype(refs[n + p].dtype)

    def spec(a):
        tail = a.shape[-2:]
        return pl.BlockSpec((None,) + tail, lambda i, s_ref: (s_ref[0], 0, 0))

    in_specs = [spec(a) if a.ndim == 3 else pl.BlockSpec(a.shape, lambda i, s_ref: (0, 0)) for a in arrs]
    return pl.pallas_call(
        body, name=name,
        grid_spec=pltpu.PrefetchScalarGridSpec(num_scalar_prefetch=1, grid=(1,), in_specs=in_specs,
                                               out_specs=[spec(a) for a in arrs]),
        out_shape=[_in_hbm((n_slots,) + a.shape[-2:], out_dtype or a.dtype) for a in arrs],
        compiler_params=_cp(("arbitrary",)),
    )(slot, *[_pin(a) for a in arrs])


def _cols_from_blocks(w8, name):
    _, R, C = w8.shape

    def body(w_ref, o_ref):
        for e in range(8):
            o_ref[:, C * e:C * (e + 1)] = w_ref[e]

    return pl.pallas_call(body, name=name, out_shape=_sds((R, 8 * C), w8.dtype), compiler_params=_cp())(w8)


def _all_gather_small(srcs, name):
    n = len(srcs)
    vmem = pl.BlockSpec(memory_space=pltpu.VMEM)

    def body(*refs):
        src, out = refs[:n], refs[n:2 * n]
        send_sems, recv_sems = refs[2 * n:]
        x, y, c = _my_place()
        me = 4 * x + 2 * y + c
        flip = lambda v, f: 1 - v if f else v
        peers = [(flip(x, fx), flip(y, fy), flip(c, fc)) for fx in (0, 1) for fy in (0, 1) for fc in (0, 1)][1:]
        cps = []
        for p in range(n):
            out[p][me] = src[p][...]
            for k, peer in enumerate(peers):
                cps.append(pltpu.make_async_remote_copy(
                    src_ref=src[p], dst_ref=out[p].at[me], send_sem=send_sems.at[7 * p + k],
                    recv_sem=recv_sems.at[7 * p + k], device_id=peer, device_id_type=MESH))
        for cp in cps:
            cp.start()
        for cp in cps:
            cp.wait()

    return pl.pallas_call(
        body, name=name, in_specs=[vmem] * n, out_specs=[vmem] * n,
        out_shape=[_sds((8,) + a.shape, a.dtype) for a in srcs],
        scratch_shapes=[pltpu.SemaphoreType.DMA((7 * n,)), pltpu.SemaphoreType.DMA((7 * n,))],
    )(*srcs)


def _adaln_exchange(c, w_ada, b_part):
    vmem = pl.BlockSpec(memory_space=pltpu.VMEM)

    def body(c_ref, w_ref, b_ref, call_ref, mod_ref, part_ref, send_sems, recv_sems):
        x, y, c = _my_place()
        me = 4 * x + 2 * y + c
        flip = lambda v, f: 1 - v if f else v
        peers = [(flip(x, fx), flip(y, fy), flip(c, fc)) for fx in (0, 1) for fy in (0, 1) for fc in (0, 1)][1:]
        call_ref[me] = c_ref[...]
        first = [pltpu.make_async_remote_copy(src_ref=c_ref, dst_ref=call_ref.at[me], send_sem=send_sems.at[k],
                                              recv_sem=recv_sems.at[k], device_id=peer, device_id_type=MESH)
                 for k, peer in enumerate(peers)]
        for cp in first:
            cp.start()
        for cp in first:
            cp.wait()
        c_all = jnp.concatenate([call_ref[j] for j in range(8)], axis=0)
        part_ref[...] = _dot(c_all.astype(CDT), w_ref[...].astype(CDT), NN) + b_ref[...]
        mod_ref[me] = part_ref[pl.ds(me, 1), :]
        second = [pltpu.make_async_remote_copy(
            src_ref=part_ref.at[pl.ds(4 * px + 2 * py + pc, 1)], dst_ref=mod_ref.at[me], send_sem=send_sems.at[7 + k],
            recv_sem=recv_sems.at[7 + k], device_id=(px, py, pc), device_id_type=MESH)
            for k, (px, py, pc) in enumerate(peers)]
        for cp in second:
            cp.start()
        for cp in second:
            cp.wait()

    return pl.pallas_call(
        body, name="adaln_exchange", in_specs=[vmem] * 3, out_specs=[vmem] * 2,
        out_shape=[_sds((8, 1, D), F32), _sds((8, 1, w_ada.shape[1]), F32)],
        scratch_shapes=[pltpu.VMEM((8, w_ada.shape[1]), F32), pltpu.SemaphoreType.DMA((14,)),
                        pltpu.SemaphoreType.DMA((14,))],
        compiler_params=_cp(),
    )(c, w_ada, b_part)


def _pair_add(dw, recv, axis, width, cidx, name):
    if axis == 0:
        tr = min(width, 256)
        C = dw.shape[1]
        grid = (4, width // tr)
        nb = width // tr
        own = pl.BlockSpec((tr, C), lambda q, i, c_ref: ((2 * q + c_ref[0]) * nb + i, 0))
        oth = pl.BlockSpec((None, tr, C), lambda q, i, c_ref: (q, i, 0))
    else:
        R = dw.shape[0]
        tr = min(R, 1024)
        grid = (4, R // tr)
        own = pl.BlockSpec((tr, width), lambda q, i, c_ref: (i, 2 * q + c_ref[0]))
        oth = pl.BlockSpec((None, tr, width), lambda q, i, c_ref: (q, i, 0))

    def body(c_ref, a_ref, b_ref, o_ref):
        o_ref[...] = (a_ref[...] + b_ref[...].astype(F32)).astype(o_ref.dtype)

    return pl.pallas_call(
        body, name=name,
        grid_spec=pltpu.PrefetchScalarGridSpec(num_scalar_prefetch=1, grid=grid, in_specs=[own, oth], out_specs=oth),
        out_shape=(_in_hbm if _is_big(dw) else _sds)(recv.shape, CDT), compiler_params=_cp(("parallel", "parallel")),
    )(cidx, *[_pin(a) if _is_big(dw) else a for a in (dw, recv)])


def _add_pairs(xs, ys, name):
    n = len(xs)

    def body(*refs):
        for p in range(n):
            refs[2 * n + p][...] = refs[p][...] + refs[n + p][...]

    return pl.pallas_call(body, name=name, out_shape=[_sds(a.shape, F32) for a in xs])(*xs, *ys)


def _adam_math(w, g, m, v):
    m = ADAM_B1 * m + (1.0 - ADAM_B1) * g
    v = ADAM_B2 * v + (1.0 - ADAM_B2) * (g * g)
    m_hat = m / (1.0 - ADAM_B1 ** ADAM_STEP)
    v_hat = v / (1.0 - ADAM_B2 ** ADAM_STEP)
    delta = -ADAM_LR * (m_hat / (jnp.sqrt(v_hat) + ADAM_EPS) + ADAM_WD * w)
    return delta, m, v


def _adam(gparts, w, m, v, name, fresh=False):
    P, R, C = gparts.shape
    in_hbm = fresh or _is_big(w)
    tr = R if R <= 256 else 256

    def body(g_ref, w_ref, m_ref, v_ref, go_ref, d_ref, mo_ref, vo_ref):
        g = g_ref[0].astype(F32)
        for i in range(1, P):
            g = g + g_ref[i].astype(F32)
        d, mn, vn = _adam_math(w_ref[...], g, m_ref[...], v_ref[...])
        go_ref[...] = g
        d_ref[...] = d
        mo_ref[...] = mn
        vo_ref[...] = vn

    t = pl.BlockSpec((tr, C), lambda i: (i, 0))
    return pl.pallas_call(
        body, name=name, grid=(R // tr,),
        in_specs=[pl.BlockSpec((P, tr, C), lambda i: (0, i, 0)), t, t, t], out_specs=[t, t, t, t],
        out_shape=[(_in_hbm if in_hbm else _sds)((R, C), F32)] * 4, compiler_params=_cp(("parallel",)),
    )(*[_pin(a) if in_hbm else a for a in (gparts, w, m, v)])


def _sum_parts(gparts, name):
    P, R, C = gparts.shape

    def body(g_ref, o_ref):
        g = g_ref[0]
        for i in range(1, P):
            g = g + g_ref[i]
        o_ref[...] = g

    return pl.pallas_call(body, name=name, out_shape=_sds((R, C), F32))(gparts)


def _adam_w_ada(c_t, dm, w, m, v):
    R, C = w.shape

    def body(c_ref, dm_ref, w_ref, m_ref, v_ref, go_ref, d_ref, mo_ref, vo_ref):
        g = c_ref[:, 0:1] * dm_ref[0:1, :]
        for b in range(1, 8):
            g = g + c_ref[:, b:b + 1] * dm_ref[b:b + 1, :]
        d, mn, vn = _adam_math(w_ref[...], g, m_ref[...], v_ref[...])
        go_ref[...] = g
        d_ref[...] = d
        mo_ref[...] = mn
        vo_ref[...] = vn

    return pl.pallas_call(body, name="adam_w_ada", out_shape=[_sds((R, C), F32)] * 4,
                          compiler_params=_cp())(c_t, dm, w, m, v)


def kernel(x, c, norm_g, w_ada, b_ada, w_in, pool_w, pool_scale, w_attn_br, w_pool_br, w_out, rel_bias, final_g, loss_target, m_norm_g, m_w_ada, m_b_ada, m_w_in, m_pool_w, m_pool_scale, m_w_attn_br, m_w_pool_br, m_w_out, m_rel_bias, m_final_g, v_norm_g, v_w_ada, v_b_ada, v_w_in, v_pool_w, v_pool_scale, v_w_attn_br, v_w_pool_br, v_w_out, v_rel_bias, v_final_g):
    S = x.shape[1]
    px, py, pc = _my_place()
    me = 4 * px + 2 * py + pc
    cidx = jnp.reshape(pc, (1,)).astype(jnp.int32)

    layout = [(1, 1024), (1, 128), (1, 128), (0, 128)]
    final_g2 = final_g.reshape(1, D)
    me1 = jnp.reshape(me, (1,)).astype(jnp.int32)

    b_part = lax.dynamic_slice(b_ada, (0, me * 384), (1, 384))
    c_all, mod_all = _adaln_exchange(c, w_ada[0], b_part)
    c_all = c_all.reshape(8, D)
    mod = mod_all.reshape(1, 3 * D)
    shift, scale, gate = mod[:, :D], mod[:, D:2 * D], mod[:, 2 * D:]

    gplan = _gather_chips_plan([(None, 1)] * 4)
    lands = _place_slot([w_in[0], w_attn_br[0], w_pool_br[0], w_out[0]], me1, 8, "place_own_shards", CDT)
    g_sems, _, lands, tok = _split_start("gather_chips_start", [], lands, 8, gplan, dep=mod_all)
    st = _local_bias_tables(_local_pre(x[0], shift, scale, norm_g, dep=tok), rel_bias, dep=tok)
    _, lands = _split_wait("gather_chips_wait", g_sems, [], lands, gplan,
                           [st["ht"]] + [b for pair in st["biases"] for b in pair])
    w_in_full, wab8, wpb8, wout8 = _pair_share(lands)
    wab, wpb = _cols_from_blocks(wab8, "w_attn_br_cols"), _cols_from_blocks(wpb8, "w_pool_br_cols")
    wout = wout8.reshape(D, D)

    st = _local_grads(st, loss_target[0], gate, w_in_full, pool_w[0], pool_scale, wab, wpb, wout, final_g2)

    early = [st["dpw"].reshape(AW, 128), st["dps"], st["gfg"], jnp.broadcast_to(st["loss"], (1, 128))]
    dws = [st["dw_in"], st["dw_ab"], st["dw_pb"], st["dw_out"]]
    sent = [st["dw_in_low"]] + dws[1:] + early
    pplan = _pair_plan(layout)
    pair_shapes = [(4, w, a.shape[1]) if ax == 0 else (4, a.shape[0], w) for a, (ax, w) in zip(dws, layout)]
    p_sems, sent, recv, tok = _split_start(
        "reduce_pair_start", sent,
        [lax.empty(s, a.dtype) for s, a in zip(pair_shapes + [a.shape for a in early], sent)], 20, pplan)
    d_rel_bias = _local_bias_grads(st, _BIAS_GRADS_EARLY, dep=tok)
    sent, recv = _split_wait("reduce_pair_wait", p_sems, sent, recv, pplan, d_rel_bias)
    parts = [_pair_add(dw, r, ax, w, cidx, f"pair_add{i}")
             for i, (dw, r, (ax, w)) in enumerate(zip([dws[0]] + sent[1:4], recv[:4], layout))]
    parts += _add_pairs(sent[4:], recv[4:], "pair_add_smalls")

    slot1 = jnp.reshape(2 * px + py, (1,)).astype(jnp.int32)
    cplan = _chips_plan(4)
    c_sems, parts, lands, tok = _split_start("reduce_chips_start", parts,
                                             _place_slot(parts, slot1, 4, "place_own_partials"), 24, cplan)
    d_rel_bias = d_rel_bias + _local_bias_grads(st, _BIAS_GRADS_LATE, dep=tok)
    grad_x, dmod, d_norm_g = _local_input_grads(st, w_in_full, norm_g, scale, dep=tok)

    dmod_all, ng_all, rb_all = _all_gather_small([dmod, d_norm_g, d_rel_bias], "gather_late")
    o_norm_g = _adam(ng_all, norm_g, m_norm_g, v_norm_g, "adam_norm_g", fresh=True)
    o_rel_bias_t = _adam(rb_all, rel_bias.T, m_rel_bias.T, v_rel_bias.T, "adam_rel_bias", fresh=True)
    o_rel_bias = [o.T for o in o_rel_bias_t]
    o_b_ada = _adam(dmod_all, b_ada, m_b_ada, v_b_ada, "adam_b_ada", fresh=True)
    dm_mine = lax.dynamic_slice(dmod_all.reshape(8, 3 * D), (0, me * 384), (8, 384))
    o_w_ada = [o[None] for o in _adam_w_ada(c_all.T, dm_mine, w_ada[0], m_w_ada[0], v_w_ada[0])]

    _, (g_in, g_ab, g_pb, g_out, g_pw, g_ps, g_fg, g_loss) = _split_wait(
        "reduce_chips_wait", c_sems, parts, lands, cplan, [o_w_ada[0], o_b_ada[0], o_norm_g[0], o_rel_bias_t[0]])
    o_w_in = [o[None] for o in _adam(g_in, w_in[0], m_w_in[0], v_w_in[0], "adam_w_in")]
    o_w_ab = [o[None] for o in _adam(g_ab, w_attn_br[0], m_w_attn_br[0], v_w_attn_br[0], "adam_w_attn_br")]
    o_w_pb = [o[None] for o in _adam(g_pb, w_pool_br[0], m_w_pool_br[0], v_w_pool_br[0], "adam_w_pool_br")]
    o_w_out = [o[None] for o in _adam(g_out, w_out[0], m_w_out[0], v_w_out[0], "adam_w_out")]
    o_pool_w = [o.reshape(pool_w.shape) for o in _adam(g_pw, pool_w.reshape(AW, 128), m_pool_w.reshape(AW, 128),
                                                       v_pool_w.reshape(AW, 128), "adam_pool_w")]
    o_pool_scale = _adam(g_ps, pool_scale, m_pool_scale, v_pool_scale, "adam_pool_scale")
    o_final_g = [o.reshape(D) for o in _adam(g_fg, final_g2, m_final_g.reshape(1, D), v_final_g.reshape(1, D),
                                             "adam_final_g")]
    loss = _sum_parts(g_loss, "sum_loss")[0, 0]

    per_w = [o_norm_g, o_w_ada, o_b_ada, o_w_in, o_pool_w, o_pool_scale, o_w_ab, o_w_pb, o_w_out, o_rel_bias, o_final_g]
    outs = [loss, grad_x[None]]
    for k in range(4):
        outs += [o[k] for o in per_w]
    return tuple(outs)
```

```python
import functools
import math

import numpy as np
import jax
import jax.numpy as jnp
from jax import lax
from jax.experimental import pallas as pl
from jax.experimental.pallas import tpu as pltpu

F32 = jnp.float32
CDT = jnp.bfloat16
D = 1024
HD = 64
NH = 8
AW = 512
QKV_W = 4608
REST_W = 3584
R16 = 16
EPS = 1e-6
NEG = -1e30
N_BACK = 128
ADAM_LR, ADAM_B1, ADAM_B2, ADAM_EPS, ADAM_WD, ADAM_STEP = 0.001, 0.9, 0.999, 1e-08, 0.01, 10
MESH = pl.DeviceIdType.MESH
VMEM_MB = 1024 * 1024


def _cp(sem=None, vmem=48, **kw):
    if sem is not None:
        kw["dimension_semantics"] = sem
    return pltpu.CompilerParams(vmem_limit_bytes=vmem * VMEM_MB, **kw)


def _sds(shape, dtype):
    return jax.ShapeDtypeStruct(tuple(shape), dtype)


def _in_hbm(shape, dtype):
    return pltpu.HBM(tuple(shape), dtype)


def _pin(a):
    return pltpu.with_memory_space_constraint(a, pltpu.HBM)


def _is_big(a):
    return a.size * a.dtype.itemsize >= 4 * VMEM_MB


def _rest_blk(jj):
    return jnp.where(jj == 0, 9, jnp.where(jj == 6, 10, jj + 10))


def _sigmoid(z):
    return 0.5 * jnp.tanh(0.5 * z) + 0.5


def _dot(a, b, dims):
    return lax.dot_general(a, b, (dims, ((), ())), preferred_element_type=F32)


NN = ((1,), (0,))
NT = ((1,), (1,))
TN = ((0,), (0,))


def _ld(ref):
    if len(ref.shape) == 2:
        return ref[...]
    return jnp.concatenate([ref[a] for a in range(ref.shape[0])], axis=0)


def _st(ref, val, cols=None):
    val = val.astype(ref.dtype)
    if len(ref.shape) == 2:
        if cols is None:
            ref[...] = val
        else:
            ref[:, cols] = val
        return
    rows = ref.shape[1]
    for a in range(ref.shape[0]):
        if cols is None:
            ref[a] = val[a * rows:(a + 1) * rows]
        else:
            ref[a, :, cols] = val[a * rows:(a + 1) * rows]


REST_COLS = ((9, 10), (11, 16), (10, 11))
_HBM = pl.BlockSpec(memory_space=pltpu.HBM)


def _resident(shape):
    return pl.BlockSpec(shape, lambda *ids: (0,) * len(shape), pipeline_mode=pl.Buffered(1))


def _w512(w_ref, j):
    return w_ref[j // 2, :, 512 * (j % 2):512 * (j % 2 + 1)]


def _proj(h, w):
    S = h.shape[0]
    tm = 512

    def body(a_ref, w_ref, q_ref, r_ref):
        a = a_ref[...]
        for j in range(9):
            q_ref[:, 512 * j:512 * (j + 1)] = _dot(a, _w512(w_ref, j), NN).astype(q_ref.dtype)
        off = 0
        for lo, hi in REST_COLS:
            for j in range(lo, hi):
                r_ref[:, off:off + 512] = _dot(a, _w512(w_ref, j), NN)
                off += 512

    return pl.pallas_call(
        body, name="proj", grid=(S // tm,),
        in_specs=[pl.BlockSpec((tm, D), lambda i: (i, 0)), _resident((8, D, 1024))],
        out_specs=[pl.BlockSpec((tm, QKV_W), lambda i: (i, 0)), pl.BlockSpec((tm, REST_W), lambda i: (i, 0))],
        out_shape=[_sds((S, QKV_W), CDT), _sds((S, REST_W), F32)],
        compiler_params=_cp(("parallel",), vmem=56),
    )(h, w)


def _after(dep):
    return ([], []) if dep is None else ([pl.BlockSpec(memory_space=pl.ANY)], [dep])


def _mm_tn(at, b, *, n_total, out_blk, name, prev=None, tn=512, with_low=False):
    R, S = at.shape
    nb = b.shape[1] // tn
    n_out = 2 if with_low else 1

    def body(*refs):
        res = _dot(refs[0][...], refs[1][...], NN)
        refs[-n_out][...] = res
        if with_low:
            refs[-1][...] = res.astype(CDT)

    in_specs = [_resident((R, S)), pl.BlockSpec((S, tn), lambda j: (0, j))]
    args = [at, b]
    aliases = {}
    if prev is not None:
        prev = list(prev) if with_low else [prev]
        in_specs += [pl.BlockSpec(memory_space=pl.ANY)] * n_out
        args += prev
        aliases = {2 + k: k for k in range(n_out)}
    out_spec = pl.BlockSpec((R, tn), lambda j: (0, out_blk(j)))
    outs = pl.pallas_call(
        body, name=name, grid=(nb,), in_specs=in_specs,
        out_specs=[out_spec] * n_out,
        out_shape=[_sds((R, n_total), F32), _sds((R, n_total), CDT)][:n_out],
        input_output_aliases=aliases,
        compiler_params=_cp(("parallel",)),
    )(*args)
    return tuple(outs) if with_low else outs[0]


def _tcast(a, name):
    M, C = a.shape
    tm = 512

    def body(a_ref, o_ref):
        o_ref[...] = a_ref[...].astype(F32).T.astype(o_ref.dtype)

    return pl.pallas_call(
        body, name=name, grid=(M // tm,),
        in_specs=[pl.BlockSpec((tm, C), lambda i: (i, 0))],
        out_specs=pl.BlockSpec((C, tm), lambda i: (0, i)),
        out_shape=_sds((C, M), CDT), compiler_params=_cp(("parallel",)),
    )(a)


TOK = R16 * R16


def _swap16():
    i = lax.broadcasted_iota(jnp.int32, (TOK, TOK), 0)
    j = lax.broadcasted_iota(jnp.int32, (TOK, TOK), 1)
    return jnp.where(j == jnp.bitwise_and(i, 15) * 16 + jnp.right_shift(i, 4), 1.0, 0.0).astype(CDT)


def _norm_mod(x, norm_g, shift, scale, dep=None):
    S = x.shape[0]
    dep_specs, dep_args = _after(dep)

    def body(x_ref, g_ref, sh_ref, sc_ref, *rest):
        xv = x_ref[...]
        xn = xv * lax.rsqrt(jnp.mean(xv * xv, axis=-1, keepdims=True) + EPS)
        h = (xn * (g_ref[...] * (1.0 + sc_ref[...])) + sh_ref[...]).astype(CDT)
        _st(rest[-1], _dot(_swap16(), h, NN))

    vec = pl.BlockSpec((1, D), lambda i: (0, 0))
    return pl.pallas_call(
        body, name="norm_mod", grid=(S // TOK,),
        in_specs=[pl.BlockSpec((TOK, D), lambda i: (i, 0)), vec, vec, vec] + dep_specs,
        out_specs=pl.BlockSpec((R16, R16, D), lambda i: (0, i, 0)),
        out_shape=_sds((R16, S // R16, D), CDT), compiler_params=_cp(("parallel",)),
    )(x, norm_g, shift, scale, *dep_args)


def _t5_bucket_np(n):
    nf = np.maximum(n, 1).astype(np.float32)
    large = 16 + (np.log(nf / np.float32(16)) / np.float32(math.log(2048 / 16)) * np.float32(16)).astype(np.int32)
    large = np.minimum(large, 31)
    return np.where(n < 16, n, large).astype(np.int32)


class _Geom:
    def __init__(self, gi, LS):
        self.gi, self.LS = gi, LS
        self.dil = (1, 4, 16)[gi]
        if gi == 0:
            self.lead, self.rows, self.n_tiles, self.has_prev = (R16,), 16, LS // 16, True
            a = np.arange(R16)[:, None]
            ll = np.arange(16)[None, :]
            self.pos = (16 * ll + a).reshape(-1)
        elif gi == 1:
            self.lead, self.rows, self.n_tiles, self.has_prev = (4,), 64, LS // 64, True
            a = np.arange(4)[:, None]
            ll = np.arange(64)[None, :]
            self.pos = (4 * ll + a).reshape(-1)
        else:
            self.lead, self.rows, self.n_tiles, self.has_prev = (), LS, 1, False
            self.pos = np.arange(LS)
        self.T = self.pos.shape[0]
        self.Tk = 2 * self.T if self.has_prev else self.T
        self.outer = {0: (), 1: (4,), 2: (R16,)}[gi]

    def view(self, arr):
        S, C = arr.shape
        if self.gi == 1:
            return arr.reshape(4, 4, self.LS, C)
        return arr.reshape(R16, self.LS, C)

    def view_shape(self, C):
        return (4, 4, self.LS, C) if self.gi == 1 else (R16, self.LS, C)

    def unview(self, arr):
        return arr.reshape(R16 * self.LS, arr.shape[-1])

    def spec(self, W, cblk, tile):
        if self.gi == 0:
            return pl.BlockSpec((R16, 16, W), lambda n: (0, tile(n), cblk))
        if self.gi == 1:
            return pl.BlockSpec((4, None, 64, W), lambda r, n: (0, r, tile(n), cblk))
        return pl.BlockSpec((None, self.LS, W), lambda r: (r, 0, cblk))

    def halves(self):
        R = self.rows
        idx = np.arange(self.T).reshape(-1, R)
        return idx[:, :R // 2].reshape(-1), idx[:, R // 2:].reshape(-1)

    def bucket_index(self, sub):
        lo, hi = self.halves()
        pos = self.pos
        if sub == 0:
            pq = pos[lo]
            pk = np.concatenate([pos[hi] - self.T, pos[lo]]) if self.has_prev else np.concatenate([pos[lo], pos[hi]])
        else:
            pq = pos[hi]
            pk = np.concatenate([pos[lo], pos[hi]])
        dist = pq[:, None] - pk[None, :]
        ok = (dist >= 0) & (dist <= N_BACK)
        bucket = _t5_bucket_np(np.clip(dist, 0, N_BACK) * self.dil)
        return np.where(ok, bucket, -1).astype(np.int32)


def _bias_table(idx, buckets, rel_bias, gi, sub, dep=None):
    T, Tk = idx.shape
    tr = 32
    dep_specs, dep_args = _after(dep)

    def body(idx_ref, rb_ref, *rest):
        h = pl.program_id(0)
        ix = idx_ref[...]
        acc = jnp.full(ix.shape, NEG, F32)
        for b in buckets:
            acc = jnp.where(ix == b, rb_ref[b, gi * NH + h], acc)
        rest[-1][...] = acc

    return pl.pallas_call(
        body, name=f"bias_table{gi}{'ab'[sub]}", grid=(NH, T // tr),
        in_specs=[pl.BlockSpec((tr, Tk), lambda h, i: (i, 0)), pl.BlockSpec(memory_space=pltpu.SMEM)] + dep_specs,
        out_specs=pl.BlockSpec((None, tr, Tk), lambda h, i: (h, i, 0)),
        out_shape=_sds((NH, T, Tk), F32), compiler_params=_cp(("parallel", "parallel")),
    )(idx, rel_bias, *dep_args)


def _bias_grad(dsum, idx, buckets, gi, sub, dep=None):
    _, T, Tk = dsum.shape
    dep_specs, dep_args = _after(dep)

    def body(ds_ref, idx_ref, *rest):
        o_ref, r_ref = rest[-2], rest[-1]
        r_ref[...] = jnp.zeros_like(r_ref)
        ix, ds = idx_ref[...], ds_ref[...]
        for b in buckets:
            r_ref[b:b + 1, :] = jnp.sum(jnp.where(ix == b, ds, 0.0), axis=0, keepdims=True)
        o_ref[...] = jnp.broadcast_to(jnp.sum(r_ref[...], axis=1, keepdims=True), (32, 128))

    out = pl.pallas_call(
        body, name=f"bias_grad{gi}{'ab'[sub]}", grid=(NH,),
        in_specs=[pl.BlockSpec((None, T, Tk), lambda h: (h, 0, 0)), pl.BlockSpec((T, Tk), lambda h: (0, 0))] + dep_specs,
        out_specs=pl.BlockSpec((None, 32, 128), lambda h: (h, 0, 0)),
        out_shape=_sds((NH, 32, 128), F32), scratch_shapes=[pltpu.VMEM((32, Tk), F32)],
        compiler_params=_cp(("parallel",)),
    )(dsum, idx, *dep_args)
    return out[:, :, 0]


def _prev_mask(n, T, Tk):
    col = lax.broadcasted_iota(jnp.int32, (1, Tk), 1)
    return jnp.where((col < T) & (n == 0), NEG, 0.0).astype(F32)


def _split_rows(v, R):
    A, half = v.shape[0] // R, R // 2
    via_f32 = v.dtype != F32 and half % 16 != 0
    w = v.astype(F32) if via_f32 else v
    lo = jnp.concatenate([w[a * R:a * R + half] for a in range(A)], axis=0)
    hi = jnp.concatenate([w[a * R + half:(a + 1) * R] for a in range(A)], axis=0)
    return (lo.astype(v.dtype), hi.astype(v.dtype)) if via_f32 else (lo, hi)


def _merge_rows(lo, hi, R):
    half = R // 2
    A = lo.shape[0] // half
    return jnp.concatenate([x[a * half:(a + 1) * half] for a in range(A) for x in (lo, hi)], axis=0)


def _attn_fwd(qkv, biases, geom, others=None):
    S = qkv.shape[0]
    gi, T, R, has_prev = geom.gi, geom.T, geom.rows, geom.has_prev
    H2 = T // 2
    qv = geom.view(qkv)
    cq, ck, cv = 3 * gi, 3 * gi + 1, 3 * gi + 2
    assert others is None or not has_prev

    def body(*refs):
        if has_prev:
            q_ref, kp_ref, kc_ref, vp_ref, vc_ref, ba_ref, bb_ref, o_ref, l_ref = refs
        elif others is not None:
            (q_ref, kc_ref, vc_ref, ba_ref, bb_ref, o0_ref, o1_ref, l0_ref, l1_ref, z_ref,
             attn_ref, L_ref, ag_ref) = refs
        else:
            q_ref, kc_ref, vc_ref, ba_ref, bb_ref, o_ref, l_ref = refs
        q_lo, q_hi = _split_rows(_ld(q_ref), R)
        kc_lo, kc_hi = _split_rows(_ld(kc_ref), R)
        vc_lo, vc_hi = _split_rows(_ld(vc_ref), R)
        k_b = jnp.concatenate([kc_lo, kc_hi], axis=0)
        v_b = jnp.concatenate([vc_lo, vc_hi], axis=0)
        if has_prev:
            n = pl.program_id(len(geom.outer))
            k_a = jnp.concatenate([_split_rows(_ld(kp_ref), R)[1], kc_lo], axis=0)
            v_a = jnp.concatenate([_split_rows(_ld(vp_ref), R)[1], vc_lo], axis=0)
            pm = _prev_mask(n, H2, T)
        else:
            k_a, v_a, pm = k_b, v_b, None
        low = lax.broadcasted_iota(jnp.int32, (1, 2 * HD), 1) < HD
        for j in range(NH // 2):
            cols = slice(2 * HD * j, 2 * HD * (j + 1))
            o_half, l_half = [], []
            for q, k, v, b_ref, mask in ((q_lo, k_a, v_a, ba_ref, pm), (q_hi, k_b, v_b, bb_ref, None)):
                q2 = q[:, cols].astype(F32) * 0.125
                k2, v2 = k[:, cols], v[:, cols]
                qs = jnp.concatenate([jnp.where(low, q2, 0.0), jnp.where(low, 0.0, q2)], axis=0).astype(q.dtype)
                s = _dot(qs, k2, NT) + b_ref[j]
                if mask is not None:
                    s = s + mask
                m = jnp.max(s, axis=1, keepdims=True)
                p = jnp.exp(s - m)
                l = jnp.sum(p, axis=1, keepdims=True)
                pv = _dot(p.astype(v.dtype), v2, NN) / l
                lse = m + jnp.log(l)
                o_half.append(jnp.where(low, pv[:H2], pv[H2:]))
                l_half.append(jnp.where(low, lse[:H2], lse[H2:]))
            o2, lc = _merge_rows(o_half[0], o_half[1], R), _merge_rows(l_half[0], l_half[1], R)
            if others is None:
                _st(o_ref, o2, cols)
                _st(l_ref, lc, cols)
                continue
            la, lb = l0_ref[:, cols], l1_ref[:, cols]
            m = jnp.maximum(jnp.maximum(la, lb), lc)
            ea, eb, ec = jnp.exp(la - m), jnp.exp(lb - m), jnp.exp(lc - m)
            den = ea + eb + ec
            attn = (ea * o0_ref[:, cols] + eb * o1_ref[:, cols] + ec * o2) / den
            attn_ref[:, cols] = attn
            L_ref[:, cols] = m + jnp.log(den)
            z = z_ref[:, cols]
            ag = attn * (z * _sigmoid(z))
            ag_ref[:, cols] = ag.astype(ag_ref.dtype)

    cur = lambda n: n
    prev = lambda n: jnp.maximum(n - 1, 0)
    if has_prev:
        in_specs = [geom.spec(AW, cq, cur), geom.spec(AW, ck, prev), geom.spec(AW, ck, cur),
                    geom.spec(AW, cv, prev), geom.spec(AW, cv, cur)]
        args = [qv] * 5
    else:
        in_specs = [geom.spec(AW, cq, cur), geom.spec(AW, ck, cur), geom.spec(AW, cv, cur)]
        args = [qv] * 3
    nd = len(geom.outer) + (1 if has_prev else 0)
    biases = [b.reshape(NH // 2, 2 * H2, b.shape[2]) for b in biases]
    in_specs += [pl.BlockSpec(b.shape, lambda *ids: (0, 0, 0)) for b in biases]
    grid = geom.outer + ((geom.n_tiles,) if has_prev else ())
    oshape = geom.view_shape(AW)
    tile = geom.spec(AW, 0, cur)
    if others is None:
        o, l = pl.pallas_call(
            body, name=f"attn_fwd{gi}", grid=grid, in_specs=in_specs, out_specs=[tile, tile],
            out_shape=[_sds(oshape, F32), _sds(oshape, F32)],
            compiler_params=_cp(("arbitrary",) * nd),
        )(*args, *biases)
        return geom.unview(o), geom.unview(l)
    os_, ls_, rest = others
    attn, Lb, ag = pl.pallas_call(
        body, name=f"attn_fwd{gi}_merge", grid=grid, in_specs=in_specs + [tile] * 5,
        out_specs=[tile, tile, tile],
        out_shape=[_sds(oshape, F32), _sds(oshape, F32), _sds(oshape, CDT)],
        compiler_params=_cp(("arbitrary",) * nd),
    )(*args, *biases, *[geom.view(a) for a in (*os_, *ls_, rest)])
    return geom.unview(attn), geom.unview(Lb), geom.unview(ag)


def _pool_counts(r, LS, win):
    l = lax.broadcasted_iota(jnp.int32, (LS, 1), 0)
    return jnp.minimum(16 * l + (r + 1), win).astype(F32)


def _pool_fwd(rest, pool_w, pool_scale):
    S = rest.shape[0]
    LS = S // R16
    r3 = rest.reshape(R16, LS, REST_W)
    PG = 128

    def body(u_ref, z_ref, pw_ref, ps_ref, pt_ref, mp_ref, pg_ref, e_ref):
        g = pl.program_id(0)
        win = jnp.left_shift(2, g)
        row = lax.broadcasted_iota(jnp.int32, (LS, PG), 0)
        for r in range(R16):
            e_ref[R16 + r] = u_ref[r]
        for r in range(1, R16):
            e_ref[r] = jnp.where(row == 0, 0.0, pltpu.roll(u_ref[r], 1, 0))

        def double(s):
            for i in range(2 * R16 - 1, 2 * s - 1, -1):
                e_ref[i] = e_ref[i] + e_ref[i - s]

        double(1)
        for k in (1, 2, 3):
            pl.when(g >= k)(functools.partial(double, 2 ** k))
        pw = pw_ref[...].astype(CDT)
        for r in range(R16):
            u = u_ref[r]
            pooled = e_ref[R16 + r] / _pool_counts(r, LS, win) - u
            pc = pooled.astype(CDT)
            mp = _dot(pc, pw, NN)
            z = z_ref[r]
            pg = (mp * ps_ref[...]) * (z * _sigmoid(z))
            mp_ref[r] = mp
            pg_ref[r] = pg.astype(pg_ref.dtype)
            pt_ref[:, r * LS:(r + 1) * LS] = pooled.T.astype(pt_ref.dtype)

    col = lambda off: pl.BlockSpec((R16, LS, PG), lambda g: (0, 0, off + g))
    tsp = pl.BlockSpec((PG, S), lambda g: (g, 0))
    pt, mp, pg = pl.pallas_call(
        body, name="pool_fwd", grid=(4,),
        in_specs=[col(24), col(4), pl.BlockSpec((None, PG, PG), lambda g: (g, 0, 0)),
                  pl.BlockSpec((1, PG), lambda g: (0, g))],
        out_specs=[tsp, col(0), col(0)],
        out_shape=[_in_hbm((AW, S), CDT), _sds((R16, LS, AW), F32), _in_hbm((R16, LS, AW), CDT)],
        scratch_shapes=[pltpu.VMEM((2 * R16, LS, PG), F32)],
        compiler_params=_cp(("parallel",)),
    )(r3, r3, pool_w, pool_scale)
    return pt, mp.reshape(S, AW), pg.reshape(S, AW)


def _merge_out_loss(ag, pg, rest, wab, wpb, wout, x, target, gate, final_g):
    S = x.shape[0]
    LS = S // R16

    def body(ag_ref, pg_ref, ga_ref, gp_ref, wa_ref, wp_ref, w_ref, x_ref, t_ref, gate_ref, fg_ref,
             dx2_ref, dmo_ref, loss_ref, gfg_ref, dgate_ref, dw_ref):
        i = pl.program_id(0)

        @pl.when(i == 0)
        def _():
            loss_ref[...] = jnp.zeros_like(loss_ref)
            gfg_ref[...] = jnp.zeros_like(gfg_ref)
            dgate_ref[...] = jnp.zeros_like(dgate_ref)
            dw_ref[...] = jnp.zeros_like(dw_ref)

        swap = _swap16()
        ya = _dot(_ld(ag_ref), wa_ref[...], NN)
        yp = _dot(_ld(pg_ref), wp_ref[...], NN)
        mg = (_sigmoid(_ld(ga_ref)) * ya + _sigmoid(_ld(gp_ref)) * yp).astype(CDT)
        m_nat = _dot(swap, mg, NN)
        mo = _dot(m_nat.astype(CDT), w_ref[...], NN)
        gate, fg = gate_ref[...], fg_ref[...]
        x2 = x_ref[...] + gate * mo
        r2 = lax.rsqrt(jnp.mean(x2 * x2, axis=-1, keepdims=True) + EPS)
        xn2 = x2 * r2
        diff = xn2 * fg - t_ref[...]
        loss = jnp.sum(jnp.sum(diff * diff, axis=-1, keepdims=True), axis=0, keepdims=True)
        dy = diff * (1.0 / D)
        dxn2 = dy * fg
        dx2 = r2 * (dxn2 - xn2 * jnp.mean(dxn2 * xn2, axis=-1, keepdims=True))
        dx2_ref[...] = dx2
        dmo = (dx2 * gate).astype(CDT)
        _st(dmo_ref, _dot(swap, dmo, NN))
        loss_ref[...] += jnp.broadcast_to(loss * (0.5 / D), loss_ref.shape)
        gfg_ref[...] += jnp.sum(dy * xn2, axis=0, keepdims=True)
        dgate_ref[...] += jnp.sum(dx2 * mo, axis=0, keepdims=True)
        dw_ref[...] += _dot(m_nat.T.astype(CDT), dmo, NN)

    def slab(w, cblk):
        return pl.BlockSpec((R16, R16, w), lambda i: (0, i, cblk))

    nat = pl.BlockSpec((TOK, D), lambda i: (i, 0))
    vec = pl.BlockSpec((1, D), lambda i: (0, 0))
    r3 = rest.reshape(R16, LS, REST_W)
    return pl.pallas_call(
        body, name="merge_out_loss", grid=(S // TOK,),
        in_specs=[slab(AW, 0), slab(AW, 0), slab(D, 1), slab(D, 2), _resident((AW, D)), _resident((AW, D)),
                  _resident((D, D)), nat, nat, vec, vec],
        out_specs=[nat, slab(D, 0), pl.BlockSpec((1, 128), lambda i: (0, 0)), vec, vec,
                   pl.BlockSpec((D, D), lambda i: (0, 0))],
        out_shape=[_sds((S, D), F32), _sds((R16, LS, D), CDT), _in_hbm((1, 128), F32), _in_hbm((1, D), F32),
                   _in_hbm((1, D), F32), _sds((D, D), F32)],
        compiler_params=_cp(("arbitrary",)),
    )(ag.reshape(R16, LS, AW), pg.reshape(R16, LS, AW), r3, r3, wab, wpb, wout, x, target, gate, final_g)


def _bwd_gates(dmo, wout, wab, wpb, rest, ag, pg, attn, mp, pool_scale):
    S = dmo.shape[0]
    tm = 256

    def body(dmo_ref, wo_ref, wa_ref, wp_ref, za_ref, zp_ref, ga0, ga1, gp0, gp1, ag_ref, pg_ref, at_ref, mp_ref,
             ps_ref, dwa_ref, dwp_ref, dr_ref, dat_ref, dl_ref, dmp_ref, dps_ref):
        i = pl.program_id(0)

        @pl.when(i == 0)
        def _():
            dps_ref[...] = jnp.zeros_like(dps_ref)
            dwa_ref[...] = jnp.zeros_like(dwa_ref)
            dwp_ref[...] = jnp.zeros_like(dwp_ref)

        dm = _dot(dmo_ref[...], wo_ref[...], NT)
        sa = _sigmoid(jnp.concatenate([ga0[...], ga1[...]], axis=1))
        sp = _sigmoid(jnp.concatenate([gp0[...], gp1[...]], axis=1))
        dya = (dm * sa).astype(CDT)
        dyp = (dm * sp).astype(CDT)
        ag, pg = ag_ref[...], pg_ref[...]
        dwa_ref[...] += _dot(ag.astype(F32).T.astype(CDT), dya, NN)
        dwp_ref[...] += _dot(pg.astype(F32).T.astype(CDT), dyp, NN)
        ya = _dot(ag, wa_ref[...], NN)
        yp = _dot(pg, wp_ref[...], NN)
        dr_ref[:, 1024:2048] = (dm * ya * sa * (1.0 - sa)).astype(dr_ref.dtype)
        dr_ref[:, 2048:3072] = (dm * yp * sp * (1.0 - sp)).astype(dr_ref.dtype)
        da = _dot(dya, wa_ref[...], NT)
        dp = _dot(dyp, wp_ref[...], NT)
        za = za_ref[...]
        sga = _sigmoid(za)
        attn = at_ref[...]
        dattn = da * (za * sga)
        dr_ref[:, 0:512] = (da * attn * (sga * (1.0 + za * (1.0 - sga)))).astype(dr_ref.dtype)
        dat_ref[...] = dattn.astype(dat_ref.dtype)
        prod = dattn * attn
        low = lax.broadcasted_iota(jnp.int32, (1, 2 * HD), 1) < HD
        for j in range(NH // 2):
            x2 = prod[:, 2 * HD * j:2 * HD * (j + 1)]
            s0 = jnp.sum(jnp.where(low, x2, 0.0), axis=1, keepdims=True)
            s1 = jnp.sum(jnp.where(low, 0.0, x2), axis=1, keepdims=True)
            dl_ref[:, 2 * HD * j:2 * HD * (j + 1)] = jnp.where(low, s0, s1)
        zp = zp_ref[...]
        sgp = _sigmoid(zp)
        mpre = mp_ref[...]
        ps = ps_ref[...]
        dmixed = dp * (zp * sgp)
        dr_ref[:, 512:1024] = (dp * (mpre * ps) * (sgp * (1.0 + zp * (1.0 - sgp)))).astype(dr_ref.dtype)
        dps_ref[...] += jnp.sum(dmixed * mpre, axis=0, keepdims=True)
        dmp_ref[...] = (dmixed * ps).astype(dmp_ref.dtype)

    full = lambda shape: pl.BlockSpec(shape, lambda i: (0, 0))
    tD = pl.BlockSpec((tm, D), lambda i: (i, 0))
    tA = pl.BlockSpec((tm, AW), lambda i: (i, 0))
    rb = lambda c: pl.BlockSpec((tm, AW), lambda i: (i, c))
    return pl.pallas_call(
        body, name="bwd_gates", grid=(S // tm,),
        in_specs=[tD, full((D, D)), full((AW, D)), full((AW, D)), rb(0), rb(1), rb(2), rb(3), rb(4), rb(5),
                  tA, tA, tA, tA, full((1, AW))],
        out_specs=[full((AW, D)), full((AW, D)), pl.BlockSpec((tm, 3072), lambda i: (i, 0)), tA, tA, tA, full((1, AW))],
        out_shape=[_sds((AW, D), F32), _sds((AW, D), F32), _sds((S, REST_W), CDT), _sds((S, AW), CDT),
                   _sds((S, AW), F32), _sds((S, AW), CDT), _sds((1, AW), F32)],
        compiler_params=_cp(("arbitrary",), vmem=56),
    )(dmo, wout, wab, wpb, rest, rest, rest, rest, rest, rest, ag, pg, attn, mp, pool_scale)


def _pool_bwd(dmp, pt, pool_w, drest):
    S = dmp.shape[0]
    LS = S // R16
    PG = 128
    d3 = dmp.reshape(R16, LS, AW)
    dr3 = drest.reshape(R16, LS, REST_W)

    def body(d_ref, pt_ref, pw_ref, dr_in, du_ref, dpw_ref, g_ref, s_ref):
        del dr_in
        g = pl.program_id(0)
        win = jnp.left_shift(2, g)
        row = lax.broadcasted_iota(jnp.int32, (LS, PG), 0)
        pw = pw_ref[...].astype(CDT)
        dpw = jnp.zeros((PG, PG), F32)
        for r in range(R16):
            dm = d_ref[r]
            dpw = dpw + _dot(pt_ref[:, r * LS:(r + 1) * LS], dm, NN)
            dpooled = _dot(dm, pw, NT)
            g_ref[r] = dpooled
            sc = dpooled / _pool_counts(r, LS, win)
            s_ref[r] = sc
            if r < R16 - 1:
                s_ref[R16 + r] = jnp.where(row == LS - 1, 0.0, pltpu.roll(sc, LS - 1, 0))
        dpw_ref[...] = dpw

        def double(s):
            for i in range(0, 2 * R16 - 2 * s):
                s_ref[i] = s_ref[i] + s_ref[i + s]

        double(1)
        for k in (1, 2, 3):
            pl.when(g >= k)(functools.partial(double, 2 ** k))
        for r in range(R16):
            du_ref[r] = (s_ref[r] - g_ref[r]).astype(du_ref.dtype)

    du, dpw = pl.pallas_call(
        body, name="pool_bwd", grid=(4,),
        in_specs=[pl.BlockSpec((R16, LS, PG), lambda g: (0, 0, g)), pl.BlockSpec((PG, S), lambda g: (g, 0)),
                  pl.BlockSpec((None, PG, PG), lambda g: (g, 0, 0)), pl.BlockSpec(memory_space=pl.ANY)],
        out_specs=[pl.BlockSpec((R16, LS, PG), lambda g: (0, 0, 24 + g)),
                   pl.BlockSpec((None, PG, PG), lambda g: (g, 0, 0))],
        out_shape=[_sds((R16, LS, REST_W), CDT), _sds((4, PG, PG), F32)],
        scratch_shapes=[pltpu.VMEM((R16, LS, PG), F32), pltpu.VMEM((2 * R16, LS, PG), F32)],
        input_output_aliases={3: 0},
        compiler_params=_cp(("parallel",)),
    )(d3, pt, pool_w, dr3)
    return du.reshape(S, REST_W), dpw


def _attn_bwd(qkv, dattn, Lb, dlb, biases, geom, dqkv_prev):
    S = qkv.shape[0]
    gi, T, R, has_prev, N = geom.gi, geom.T, geom.rows, geom.has_prev, geom.n_tiles
    H2 = T // 2
    nd_outer = len(geom.outer)
    qv = geom.view(qkv)
    cq, ck, cv = 3 * gi, 3 * gi + 1, 3 * gi + 2
    W3 = 3 * AW

    def heads(q_ref, kp_ref, kc_ref, vp_ref, vc_ref, do_ref, L_ref, dl_ref, ba_ref, bb_ref, pm, dsa_ref, dsb_ref, emit):
        q_lo, q_hi = _split_rows(_ld(q_ref), R)
        do_lo, do_hi = _split_rows(_ld(do_ref), R)
        L_lo, L_hi = _split_rows(_ld(L_ref), R)
        dl_lo, dl_hi = _split_rows(_ld(dl_ref), R)
        kc_lo, kc_hi = _split_rows(_ld(kc_ref), R)
        vc_lo, vc_hi = _split_rows(_ld(vc_ref), R)
        k_b = jnp.concatenate([kc_lo, kc_hi], axis=0)
        v_b = jnp.concatenate([vc_lo, vc_hi], axis=0)
        if has_prev:
            k_a = jnp.concatenate([_split_rows(_ld(kp_ref), R)[1], kc_lo], axis=0)
            v_a = jnp.concatenate([_split_rows(_ld(vp_ref), R)[1], vc_lo], axis=0)
        else:
            k_a, v_a = k_b, v_b
        low = lax.broadcasted_iota(jnp.int32, (1, 2 * HD), 1) < HD
        for j in range(NH // 2):
            sl = slice(2 * HD * j, 2 * HD * (j + 1))
            res = []
            for q, k, v, do, L, dl, b_ref, mask, ds_ref in (
                    (q_lo, k_a, v_a, do_lo, L_lo, dl_lo, ba_ref, pm, dsa_ref),
                    (q_hi, k_b, v_b, do_hi, L_hi, dl_hi, bb_ref, None, dsb_ref)):
                q2 = q[:, sl].astype(F32) * 0.125
                do2 = do[:, sl].astype(F32)
                k2, v2 = k[:, sl], v[:, sl]
                L2, dl2 = L[:, sl], dl[:, sl]
                qs = jnp.concatenate([jnp.where(low, q2, 0.0), jnp.where(low, 0.0, q2)], axis=0).astype(q.dtype)
                dom = jnp.concatenate([jnp.where(low, do2, 0.0), jnp.where(low, 0.0, do2)], axis=0).astype(q.dtype)
                Lc = jnp.concatenate([L2[:, 0:1], L2[:, HD:HD + 1]], axis=0)
                dlc = jnp.concatenate([dl2[:, 0:1], dl2[:, HD:HD + 1]], axis=0)
                s = _dot(qs, k2, NT) + b_ref[j]
                if mask is not None:
                    s = s + mask
                p = jnp.exp(s - Lc)
                ds = p * (_dot(dom, v2, NT) - dlc)
                ds_ref[j] += ds
                dsc = ds.astype(q.dtype)
                dq_st = _dot(dsc, k2, NN)
                res.append((jnp.where(low, dq_st[:H2], dq_st[H2:]) * 0.125, _dot(dsc, qs, TN),
                            _dot(p.astype(q.dtype), dom, TN)))
            (dq_lo, dk_a, dv_a), (dq_hi, dk_b, dv_b) = res
            dq = _merge_rows(dq_lo, dq_hi, R)
            if has_prev:
                emit(sl, dq, _merge_rows(dk_a[H2:] + dk_b[:H2], dk_b[H2:], R),
                     _merge_rows(dv_a[H2:] + dv_b[:H2], dv_b[H2:], R), dk_a[:H2], dv_a[:H2])
            else:
                emit(sl, dq, _merge_rows(dk_a[:H2] + dk_b[:H2], dk_a[H2:] + dk_b[H2:], R),
                     _merge_rows(dv_a[:H2] + dv_b[:H2], dv_a[H2:] + dv_b[H2:], R), None, None)

    def first_step():
        ok = pl.program_id(nd_outer) == 0 if has_prev else pl.program_id(0) == 0
        for a in range(nd_outer if has_prev else 0):
            ok = ok & (pl.program_id(a) == 0)
        return ok

    if has_prev:
        def body(q_ref, kp_ref, kc_ref, vp_ref, vc_ref, do_ref, L_ref, dl_ref, ba_ref, bb_ref, prev_in, out_ref,
                 dsa_ref, dsb_ref, hold, new, pp):
            del prev_in
            s_id = pl.program_id(nd_outer)

            @pl.when(first_step())
            def _():
                dsa_ref[...] = jnp.zeros_like(dsa_ref)
                dsb_ref[...] = jnp.zeros_like(dsb_ref)

            @pl.when(s_id < N)
            def _():
                def emit(sl, dq, dk, dv, dk_prev, dv_prev):
                    new[:, sl] = dq
                    new[:, slice(AW + sl.start, AW + sl.stop)] = dk
                    new[:, slice(2 * AW + sl.start, 2 * AW + sl.stop)] = dv
                    pp[:, sl] = dk_prev
                    pp[:, slice(AW + sl.start, AW + sl.stop)] = dv_prev

                heads(q_ref, kp_ref, kc_ref, vp_ref, vc_ref, do_ref, L_ref, dl_ref, ba_ref, bb_ref,
                      _prev_mask(s_id, H2, T), dsa_ref, dsb_ref, emit)

            @pl.when((s_id >= 1) & (s_id < N))
            def _():
                _st(out_ref, hold[:, 0:AW], slice(0, AW))
                late = _merge_rows(jnp.zeros((H2, 2 * AW), F32), pp[...], R)
                _st(out_ref, hold[:, AW:W3] + late, slice(AW, W3))

            @pl.when(s_id == N)
            def _():
                _st(out_ref, hold[...])

            @pl.when(s_id < N)
            def _():
                hold[...] = new[...]

        cur = lambda n: jnp.minimum(n, N - 1)
        prev = lambda n: jnp.clip(n - 1, 0, N - 1)
        delayed = lambda n: jnp.maximum(n - 1, 0)
        in_specs = [geom.spec(AW, cq, cur), geom.spec(AW, ck, prev), geom.spec(AW, ck, cur),
                    geom.spec(AW, cv, prev), geom.spec(AW, cv, cur),
                    geom.spec(AW, 0, cur), geom.spec(AW, 0, cur), geom.spec(AW, 0, cur)]
        args = [qv] * 5 + [geom.view(dattn), geom.view(Lb), geom.view(dlb)]
        out_spec = geom.spec(W3, gi, delayed)
        grid = geom.outer + (N + 1,)
        scratch = [pltpu.VMEM((T, W3), F32), pltpu.VMEM((T, W3), F32), pltpu.VMEM((H2, 2 * AW), F32)]
    else:
        def body(q_ref, kc_ref, vc_ref, do_ref, L_ref, dl_ref, ba_ref, bb_ref, prev_in, out_ref, dsa_ref, dsb_ref):
            del prev_in

            @pl.when(first_step())
            def _():
                dsa_ref[...] = jnp.zeros_like(dsa_ref)
                dsb_ref[...] = jnp.zeros_like(dsb_ref)

            def emit(sl, dq, dk, dv, dk_prev, dv_prev):
                _st(out_ref, dq, sl)
                _st(out_ref, dk, slice(AW + sl.start, AW + sl.stop))
                _st(out_ref, dv, slice(2 * AW + sl.start, 2 * AW + sl.stop))

            heads(q_ref, None, kc_ref, None, vc_ref, do_ref, L_ref, dl_ref, ba_ref, bb_ref, None, dsa_ref, dsb_ref,
                  emit)

        cur = lambda n: n
        in_specs = [geom.spec(AW, cq, cur), geom.spec(AW, ck, cur), geom.spec(AW, cv, cur),
                    geom.spec(AW, 0, cur), geom.spec(AW, 0, cur), geom.spec(AW, 0, cur)]
        args = [qv] * 3 + [geom.view(dattn), geom.view(Lb), geom.view(dlb)]
        out_spec = geom.spec(W3, gi, cur)
        grid = geom.outer
        scratch = []

    whole = lambda b: pl.BlockSpec(b.shape, lambda *ids: (0, 0, 0))
    biases = [b.reshape(NH // 2, 2 * H2, b.shape[2]) for b in biases]
    in_specs += [whole(b) for b in biases]
    in_specs.append(pl.BlockSpec(memory_space=pl.ANY))
    if dqkv_prev is None:
        last, aliases = biases[0], {}
    else:
        last, aliases = geom.view(dqkv_prev), {len(in_specs) - 1: 0}
    dq, dsa, dsb = pl.pallas_call(
        body, name=f"attn_bwd{gi}", grid=grid, in_specs=in_specs,
        out_specs=[out_spec] + [whole(b) for b in biases],
        out_shape=[_sds(geom.view_shape(QKV_W), CDT)] + [_sds(b.shape, F32) for b in biases],
        scratch_shapes=scratch,
        input_output_aliases=aliases,
        compiler_params=_cp(("arbitrary",) * len(grid), vmem=56),
    )(*args, *biases, last)
    return geom.unview(dq), [d.reshape(NH, H2, d.shape[2]) for d in (dsa, dsb)]


def _local_step(x, target, shift, scale, gate, norm_g, w_in_full, pool_w, pool_scale, wab, wpb, wout, rel_bias,
                final_g):
    st = _local_bias_tables(_local_pre(x, shift, scale, norm_g), rel_bias)
    st = _local_grads(st, target, gate, w_in_full, pool_w, pool_scale, wab, wpb, wout, final_g)
    d_rel_bias = _local_bias_grads(st, _BIAS_GRADS_EARLY + _BIAS_GRADS_LATE)
    gx, dmod, dng = _local_input_grads(st, w_in_full, norm_g, scale)
    return dict(loss=st["loss"], grad_x=gx, dw_in=st["dw_in"], dw_ab=st["dw_ab"], dw_pb=st["dw_pb"],
                dw_out=st["dw_out"], d_norm_g=dng, d_pool_w=st["dpw"], d_pool_scale=st["dps"],
                d_rel_bias=d_rel_bias.T, d_final_g=st["gfg"], dmod=dmod)


def _local_pre(x, shift, scale, norm_g, dep=None):
    S = x.shape[0]
    h = _norm_mod(x, norm_g, shift, scale, dep).reshape(S, D)
    return dict(x=x, h=h, ht=_tcast(h, "h_transpose"))


def _local_bias_tables(st, rel_bias, dep=None):
    LS = st["x"].shape[0] // R16
    geoms = [_Geom(gi, LS) for gi in range(3)]
    idx_np = [[g.bucket_index(sub) for sub in range(2)] for g in geoms]
    buckets = [[tuple(int(b) for b in np.unique(ix) if b >= 0) for ix in pair] for pair in idx_np]
    idxs = [[jnp.asarray(ix) for ix in pair] for pair in idx_np]
    biases = [[_bias_table(idxs[gi][sub], buckets[gi][sub], rel_bias, gi, sub, dep) for sub in range(2)]
              for gi in range(3)]
    return dict(st, geoms=geoms, idxs=idxs, buckets=buckets, biases=biases)


def _local_grads(st, target, gate, w_in_full, pool_w, pool_scale, wab, wpb, wout, final_g):
    geoms, idxs, biases, x, h, ht = (st[k] for k in ("geoms", "idxs", "biases", "x", "h", "ht"))
    S = h.shape[0]
    LS = S // R16
    qkv, rest = _proj(h, w_in_full)

    os_, ls_ = [], []
    for gi in range(2):
        o, l = _attn_fwd(qkv, biases[gi], geoms[gi])
        os_.append(o)
        ls_.append(l)
    attn, Lb, ag = _attn_fwd(qkv, biases[2], geoms[2], others=(os_, ls_, rest))
    pt, mp, pg = _pool_fwd(rest, pool_w, pool_scale)
    dx2, dmo3, loss, gfg, dgate, dw_out = _merge_out_loss(ag, pg, rest, wab, wpb, wout, x, target, gate, final_g)
    dmo = dmo3.reshape(S, D)

    dw_ab, dw_pb, drest, dattn, dlb, dmp, dps = _bwd_gates(dmo, wout, wab, wpb, rest, ag, pg, attn, mp, pool_scale)
    drest, dpw = _pool_bwd(dmp, pt, pool_w, drest)

    dqkv = None
    dsums = []
    for gi in range(3):
        dqkv, dsum = _attn_bwd(qkv, dattn, Lb, dlb, biases[gi], geoms[gi], dqkv)
        dsums.append(dsum)

    dw_in = _mm_tn(ht, dqkv, n_total=8192, out_blk=lambda j: j, name="dw_in_qkv", with_low=True)
    dw_in, dw_in_low = _mm_tn(ht, drest, n_total=8192, out_blk=_rest_blk, name="dw_in_rest", prev=dw_in, with_low=True)
    return dict(st, loss=loss[0, 0], dw_in=dw_in, dw_in_low=dw_in_low, dw_ab=dw_ab, dw_pb=dw_pb, dw_out=dw_out, dpw=dpw, dps=dps, gfg=gfg,
                dgate=dgate, dsums=dsums, dqkv=dqkv, drest=drest, dx2=dx2)


def _local_bias_grads(st, which, dep=None):
    rows = []
    for gi in range(3):
        got = [_bias_grad(st["dsums"][gi][sub], st["idxs"][gi][sub], st["buckets"][gi][sub], gi, sub, dep)
               for sub in range(2) if (gi, sub) in which]
        rows.append(sum(got) if got else jnp.zeros((NH, 32), F32))
    return jnp.concatenate(rows, axis=0)


_BIAS_GRADS_EARLY = ((0, 0), (0, 1), (1, 0), (2, 0))
_BIAS_GRADS_LATE = ((1, 1), (2, 1))


def _dh_h_bwd(dqkv, drest, w, dx2, x, norm_g, scale, dep=None):
    S = x.shape[0]
    dep_specs, dep_args = _after(dep)

    def body(a_ref, b_ref, w_ref, dx2_ref, x_ref, g_ref, sc_ref, *rest):
        gx_ref, dsh_ref, dsc_ref, dng_ref, nat_ref = rest[-5:]
        i = pl.program_id(0)

        @pl.when(i == 0)
        def _():
            dsh_ref[...] = jnp.zeros_like(dsh_ref)
            dsc_ref[...] = jnp.zeros_like(dsc_ref)
            dng_ref[...] = jnp.zeros_like(dng_ref)

        a, b = _ld(a_ref), _ld(b_ref)
        acc = _dot(a[:, 0:512], _w512(w_ref, 0), NT)
        for j in range(1, 9):
            acc = acc + _dot(a[:, 512 * j:512 * (j + 1)], _w512(w_ref, j), NT)
        off = 0
        for lo, hi in REST_COLS:
            for j in range(lo, hi):
                acc = acc + _dot(b[:, off:off + 512], _w512(w_ref, j), NT)
                off += 512
        for r in range(R16):
            nat_ref[:, r, :] = acc[R16 * r:R16 * (r + 1)]
        dh = jnp.concatenate([nat_ref[l] for l in range(R16)], axis=0)
        g = g_ref[...]
        one_sc = 1.0 + sc_ref[...]
        xv = x_ref[...]
        r1 = lax.rsqrt(jnp.mean(xv * xv, axis=-1, keepdims=True) + EPS)
        xn = xv * r1
        dhx = dh * xn
        dxn = dh * (g * one_sc)
        dx1 = r1 * (dxn - xn * jnp.mean(dxn * xn, axis=-1, keepdims=True))
        gx_ref[...] = dx2_ref[...] + dx1
        dsh_ref[...] += jnp.sum(dh, axis=0, keepdims=True)
        dsc_ref[...] += jnp.sum(dhx * g, axis=0, keepdims=True)
        dng_ref[...] += jnp.sum(dhx * one_sc, axis=0, keepdims=True)

    nat = pl.BlockSpec((TOK, D), lambda i: (i, 0))
    vec = pl.BlockSpec((1, D), lambda i: (0, 0))
    slab = lambda W: pl.BlockSpec((R16, R16, W), lambda i: (0, i, 0))
    return pl.pallas_call(
        body, name="dh_h_bwd", grid=(S // TOK,),
        in_specs=[slab(QKV_W), slab(REST_W), _resident((8, D, 1024)), nat, nat, vec, vec] + dep_specs,
        out_specs=[nat, vec, vec, vec],
        out_shape=[_sds((S, D), F32), _sds((1, D), F32), _sds((1, D), F32), _sds((1, D), F32)],
        scratch_shapes=[pltpu.VMEM((R16, R16, D), F32)],
        compiler_params=_cp(("arbitrary",), vmem=56),
    )(dqkv, drest, w, dx2, x, norm_g, scale, *dep_args)


def _local_input_grads(st, w_in_full, norm_g, scale, dep=None):
    S = st["h"].shape[0]
    LS = S // R16
    gx, dsh, dsc, dng = _dh_h_bwd(st["dqkv"].reshape(R16, LS, QKV_W), st["drest"].reshape(R16, LS, REST_W),
                                  w_in_full, st["dx2"], st["x"], norm_g, scale, dep)
    return gx, jnp.concatenate([dsh, dsc, st["dgate"]], axis=1), dng


def _my_place():
    return lax.axis_index("x"), lax.axis_index("y"), lax.axis_index("c")


def _block_of(ref, axis, idx, width):
    if axis is None:
        return ref.at[idx]
    start = pl.multiple_of(idx * width, width)
    if axis == 0:
        return ref.at[pl.ds(start, width)]
    return ref.at[:, pl.ds(start, width)]


_SEM = pl.BlockSpec(memory_space=pltpu.SEMAPHORE)
_EFFECT = pltpu.SideEffectType.DATAFLOW_SIDE_EFFECTING


def _split_start(name, srcs, lands, ncopy, plan, dep=None):
    arrs = list(srcs) + list(lands)
    na, ns = len(arrs), len(srcs)
    dep_specs, dep_args = _after(dep)

    def body(*refs):
        send_sems, recv_sems, token = refs[na + len(dep_args)], refs[na + len(dep_args) + 1], refs[-1]
        for k, (s, d, dev) in enumerate(plan(refs[:ns], refs[ns:na])):
            pltpu.make_async_remote_copy(src_ref=s, dst_ref=d, send_sem=send_sems.at[k], recv_sem=recv_sems.at[k],
                                         device_id=dev, device_id_type=MESH).start()
        token[...] = jnp.zeros_like(token)

    outs = pl.pallas_call(
        body, name=name,
        out_shape=(pltpu.SemaphoreType.DMA((ncopy,)), pltpu.SemaphoreType.DMA((ncopy,)),
                   *[pltpu.HBM(a.shape, a.dtype) for a in arrs], _sds((8, 128), F32)),
        in_specs=[_HBM] * na + dep_specs,
        out_specs=(_SEM, _SEM, *[_HBM] * na, pl.BlockSpec(memory_space=pltpu.VMEM)),
        input_output_aliases={i: 2 + i for i in range(na)},
        compiler_params=pltpu.CompilerParams(has_side_effects=_EFFECT),
    )(*[pltpu.with_memory_space_constraint(a, pltpu.HBM) for a in arrs], *dep_args)
    return (outs[0], outs[1]), list(outs[2:2 + ns]), list(outs[2 + ns:2 + na]), outs[-1]


def _split_wait(name, sems, srcs, lands, plan, after):
    arrs = list(srcs) + list(lands)
    na, ns = len(arrs), len(srcs)
    after = list(after) if isinstance(after, (list, tuple)) else [after]

    def body(*refs):
        send_sems, recv_sems = refs[na], refs[na + 1]
        for k, (s, d, dev) in enumerate(plan(refs[:ns], refs[ns:na])):
            cp = pltpu.make_async_remote_copy(src_ref=s, dst_ref=d, send_sem=send_sems.at[k], recv_sem=recv_sems.at[k],
                                              device_id=dev, device_id_type=MESH)
            cp.wait_send()
            cp.wait_recv()

    outs = pl.pallas_call(
        body, name=name, out_shape=[pltpu.HBM(a.shape, a.dtype) for a in arrs],
        in_specs=[_HBM] * na + [_SEM, _SEM] + [pl.BlockSpec(memory_space=pl.ANY)] * len(after),
        out_specs=[_HBM] * na,
        input_output_aliases={i: i for i in range(na)},
        compiler_params=pltpu.CompilerParams(has_side_effects=_EFFECT),
    )(*arrs, sems[0], sems[1], *after)
    return list(outs[:ns]), list(outs[ns:])


def _gather_chips_plan(layout):
    def plan(src, land):
        x, y, c = _my_place()
        me = 4 * x + 2 * y + c
        mine = [_block_of(land[p], layout[p][0], me, layout[p][1]) for p in range(len(land))]
        return [(mine[p], mine[p], (cx, cy, c)) for p in range(len(land)) for cx, cy in [(1 - x, y), (x, 1 - y)]]
    return plan


def _pair_plan(layout):
    def plan(src, land):
        x, y, c = _my_place()
        n = len(layout)
        return [(_block_of(src[p], layout[p][0], 2 * q + (1 - c), layout[p][1]), land[p].at[q], (x, y, 1 - c))
                for p in range(n) for q in range(4)] + [(src[p], land[p], (x, y, 1 - c)) for p in range(n, len(src))]
    return plan


def _chips_plan(n):
    def plan(src, land):
        x, y, c = _my_place()
        chips = [(1 - x, y), (x, 1 - y), (1 - x, 1 - y)]
        return [(src[p].at[2 * cx + cy], land[p].at[2 * x + y], (cx, cy, c)) for p in range(n) for cx, cy in chips] + \
               [(src[p], land[p].at[2 * x + y], (cx, cy, c)) for p in range(n, len(src)) for cx, cy in chips]
    return plan


def _pair_share(gathered):
    n = len(gathered)

    def body(*refs):
        land = refs[n:2 * n]
        send_sems, recv_sems, fwd_send, fwd_recv = refs[2 * n:]
        x, y, c = _my_place()
        k_dg = 3 - (2 * x + y)
        south = c == 0
        pass_on = 4 * jnp.where(south, 1 - x, x) + 2 * jnp.where(south, y, 1 - y) + c
        to = (jnp.where(south, x, 1 - x), jnp.where(south, 1 - y, y), c)

        def forward(p):
            return pltpu.make_async_remote_copy(
                src_ref=land[p].at[pass_on], dst_ref=land[p].at[pass_on], send_sem=fwd_send.at[p],
                recv_sem=fwd_recv.at[p], device_id=to, device_id_type=MESH)

        def to_sibling(p, k):
            blk = land[p].at[2 * k + c]
            return pltpu.make_async_remote_copy(
                src_ref=blk, dst_ref=blk, send_sem=send_sems.at[4 * p + k], recv_sem=recv_sems.at[4 * p + k],
                device_id=(x, y, 1 - c), device_id_type=MESH)

        for p in range(n):
            forward(p).start()
        for p in range(n):
            for k in range(4):
                @pl.when(k != k_dg)
                def _():
                    to_sibling(p, k).start()
        for p in range(n):
            dg = land[p].at[2 * k_dg + c]
            pltpu.make_async_remote_copy(src_ref=dg, dst_ref=dg, send_sem=fwd_send.at[p], recv_sem=fwd_recv.at[p],
                                         device_id=to, device_id_type=MESH).wait_recv()
            for k in range(4):
                @pl.when(k == k_dg)
                def _():
                    to_sibling(p, k).start()
        for p in range(n):
            forward(p).wait_send()
            for k in range(4):
                to_sibling(p, k).wait_send()
                blk = land[p].at[2 * k + 1 - c]
                pltpu.make_async_remote_copy(
                    src_ref=blk, dst_ref=blk, send_sem=send_sems.at[4 * p + k], recv_sem=recv_sems.at[4 * p + k],
                    device_id=(x, y, 1 - c), device_id_type=MESH).wait_recv()

    return pl.pallas_call(
        body, name="gather_pair_share", in_specs=[_HBM] * n, out_specs=[_HBM] * n,
        out_shape=[_sds(g.shape, g.dtype) for g in gathered],
        input_output_aliases={p: p for p in range(n)},
        scratch_shapes=[pltpu.SemaphoreType.DMA((4 * n,)), pltpu.SemaphoreType.DMA((4 * n,)),
                        pltpu.SemaphoreType.DMA((n,)), pltpu.SemaphoreType.DMA((n,))],
    )(*gathered)


def _place_slot(arrs, slot, n_slots, name, out_dtype=None):
    n = len(arrs)

    def body(s_ref, *refs):
        for p in range(n):
            refs[n + p][...] = refs[p][...].astype(refs[n + p].dtype)

    def spec(a):
        tail = a.shape[-2:]
        return pl.BlockSpec((None,) + tail, lambda i, s_ref: (s_ref[0], 0, 0))

    in_specs = [spec(a) if a.ndim == 3 else pl.BlockSpec(a.shape, lambda i, s_ref: (0, 0)) for a in arrs]
    return pl.pallas_call(
        body, name=name,
        grid_spec=pltpu.PrefetchScalarGridSpec(num_scalar_prefetch=1, grid=(1,), in_specs=in_specs,
                                               out_specs=[spec(a) for a in arrs]),
        out_shape=[_in_hbm((n_slots,) + a.shape[-2:], out_dtype or a.dtype) for a in arrs],
        compiler_params=_cp(("arbitrary",)),
    )(slot, *[_pin(a) for a in arrs])


def _cols_from_blocks(w8, name):
    _, R, C = w8.shape

    def body(w_ref, o_ref):
        for e in range(8):
            o_ref[:, C * e:C * (e + 1)] = w_ref[e]

    return pl.pallas_call(body, name=name, out_shape=_sds((R, 8 * C), w8.dtype), compiler_params=_cp())(w8)


def _all_gather_small(srcs, name):
    n = len(srcs)
    vmem = pl.BlockSpec(memory_space=pltpu.VMEM)

    def body(*refs):
        src, out = refs[:n], refs[n:2 * n]
        send_sems, recv_sems = refs[2 * n:]
        x, y, c = _my_place()
        me = 4 * x + 2 * y + c
        flip = lambda v, f: 1 - v if f else v
        peers = [(flip(x, fx), flip(y, fy), flip(c, fc)) for fx in (0, 1) for fy in (0, 1) for fc in (0, 1)][1:]
        cps = []
        for p in range(n):
            out[p][me] = src[p][...]
            for k, peer in enumerate(peers):
                cps.append(pltpu.make_async_remote_copy(
                    src_ref=src[p], dst_ref=out[p].at[me], send_sem=send_sems.at[7 * p + k],
                    recv_sem=recv_sems.at[7 * p + k], device_id=peer, device_id_type=MESH))
        for cp in cps:
            cp.start()
        for cp in cps:
            cp.wait()

    return pl.pallas_call(
        body, name=name, in_specs=[vmem] * n, out_specs=[vmem] * n,
        out_shape=[_sds((8,) + a.shape, a.dtype) for a in srcs],
        scratch_shapes=[pltpu.SemaphoreType.DMA((7 * n,)), pltpu.SemaphoreType.DMA((7 * n,))],
    )(*srcs)


def _adaln_exchange(c, w_ada, b_part):
    vmem = pl.BlockSpec(memory_space=pltpu.VMEM)

    def body(c_ref, w_ref, b_ref, call_ref, mod_ref, part_ref, send_sems, recv_sems):
        x, y, c = _my_place()
        me = 4 * x + 2 * y + c
        flip = lambda v, f: 1 - v if f else v
        peers = [(flip(x, fx), flip(y, fy), flip(c, fc)) for fx in (0, 1) for fy in (0, 1) for fc in (0, 1)][1:]
        call_ref[me] = c_ref[...]
        first = [pltpu.make_async_remote_copy(src_ref=c_ref, dst_ref=call_ref.at[me], send_sem=send_sems.at[k],
                                              recv_sem=recv_sems.at[k], device_id=peer, device_id_type=MESH)
                 for k, peer in enumerate(peers)]
        for cp in first:
            cp.start()
        for cp in first:
            cp.wait()
        c_all = jnp.concatenate([call_ref[j] for j in range(8)], axis=0)
        part_ref[...] = _dot(c_all.astype(CDT), w_ref[...].astype(CDT), NN) + b_ref[...]
        mod_ref[me] = part_ref[pl.ds(me, 1), :]
        second = [pltpu.make_async_remote_copy(
            src_ref=part_ref.at[pl.ds(4 * px + 2 * py + pc, 1)], dst_ref=mod_ref.at[me], send_sem=send_sems.at[7 + k],
            recv_sem=recv_sems.at[7 + k], device_id=(px, py, pc), device_id_type=MESH)
            for k, (px, py, pc) in enumerate(peers)]
        for cp in second:
            cp.start()
        for cp in second:
            cp.wait()

    return pl.pallas_call(
        body, name="adaln_exchange", in_specs=[vmem] * 3, out_specs=[vmem] * 2,
        out_shape=[_sds((8, 1, D), F32), _sds((8, 1, w_ada.shape[1]), F32)],
        scratch_shapes=[pltpu.VMEM((8, w_ada.shape[1]), F32), pltpu.SemaphoreType.DMA((14,)),
                        pltpu.SemaphoreType.DMA((14,))],
        compiler_params=_cp(),
    )(c, w_ada, b_part)


def _pair_add(dw, recv, axis, width, cidx, name):
    if axis == 0:
        tr = min(width, 256)
        C = dw.shape[1]
        grid = (4, width // tr)
        nb = width // tr
        own = pl.BlockSpec((tr, C), lambda q, i, c_ref: ((2 * q + c_ref[0]) * nb + i, 0))
        oth = pl.BlockSpec((None, tr, C), lambda q, i, c_ref: (q, i, 0))
    else:
        R = dw.shape[0]
        tr = min(R, 1024)
        grid = (4, R // tr)
        own = pl.BlockSpec((tr, width), lambda q, i, c_ref: (i, 2 * q + c_ref[0]))
        oth = pl.BlockSpec((None, tr, width), lambda q, i, c_ref: (q, i, 0))

    def body(c_ref, a_ref, b_ref, o_ref):
        o_ref[...] = (a_ref[...] + b_ref[...].astype(F32)).astype(o_ref.dtype)

    return pl.pallas_call(
        body, name=name,
        grid_spec=pltpu.PrefetchScalarGridSpec(num_scalar_prefetch=1, grid=grid, in_specs=[own, oth], out_specs=oth),
        out_shape=(_in_hbm if _is_big(dw) else _sds)(recv.shape, CDT), compiler_params=_cp(("parallel", "parallel")),
    )(cidx, *[_pin(a) if _is_big(dw) else a for a in (dw, recv)])


def _add_pairs(xs, ys, name):
    n = len(xs)

    def body(*refs):
        for p in range(n):
            refs[2 * n + p][...] = refs[p][...] + refs[n + p][...]

    return pl.pallas_call(body, name=name, out_shape=[_sds(a.shape, F32) for a in xs])(*xs, *ys)


def _adam_math(w, g, m, v):
    m = ADAM_B1 * m + (1.0 - ADAM_B1) * g
    v = ADAM_B2 * v + (1.0 - ADAM_B2) * (g * g)
    m_hat = m / (1.0 - ADAM_B1 ** ADAM_STEP)
    v_hat = v / (1.0 - ADAM_B2 ** ADAM_STEP)
    delta = -ADAM_LR * (m_hat / (jnp.sqrt(v_hat) + ADAM_EPS) + ADAM_WD * w)
    return delta, m, v


def _adam(gparts, w, m, v, name, fresh=False):
    P, R, C = gparts.shape
    in_hbm = fresh or _is_big(w)
    tr = R if R <= 256 else 256

    def body(g_ref, w_ref, m_ref, v_ref, go_ref, d_ref, mo_ref, vo_ref):
        g = g_ref[0].astype(F32)
        for i in range(1, P):
            g = g + g_ref[i].astype(F32)
        d, mn, vn = _adam_math(w_ref[...], g, m_ref[...], v_ref[...])
        go_ref[...] = g
        d_ref[...] = d
        mo_ref[...] = mn
        vo_ref[...] = vn

    t = pl.BlockSpec((tr, C), lambda i: (i, 0))
    return pl.pallas_call(
        body, name=name, grid=(R // tr,),
        in_specs=[pl.BlockSpec((P, tr, C), lambda i: (0, i, 0)), t, t, t], out_specs=[t, t, t, t],
        out_shape=[(_in_hbm if in_hbm else _sds)((R, C), F32)] * 4, compiler_params=_cp(("parallel",)),
    )(*[_pin(a) if in_hbm else a for a in (gparts, w, m, v)])


def _sum_parts(gparts, name):
    P, R, C = gparts.shape

    def body(g_ref, o_ref):
        g = g_ref[0]
        for i in range(1, P):
            g = g + g_ref[i]
        o_ref[...] = g

    return pl.pallas_call(body, name=name, out_shape=_sds((R, C), F32))(gparts)


def _adam_w_ada(c_t, dm, w, m, v):
    R, C = w.shape

    def body(c_ref, dm_ref, w_ref, m_ref, v_ref, go_ref, d_ref, mo_ref, vo_ref):
        g = c_ref[:, 0:1] * dm_ref[0:1, :]
        for b in range(1, 8):
            g = g + c_ref[:, b:b + 1] * dm_ref[b:b + 1, :]
        d, mn, vn = _adam_math(w_ref[...], g, m_ref[...], v_ref[...])
        go_ref[...] = g
        d_ref[...] = d
        mo_ref[...] = mn
        vo_ref[...] = vn

    return pl.pallas_call(body, name="adam_w_ada", out_shape=[_sds((R, C), F32)] * 4,
                          compiler_params=_cp())(c_t, dm, w, m, v)


def kernel(x, c, norm_g, w_ada, b_ada, w_in, pool_w, pool_scale, w_attn_br, w_pool_br, w_out, rel_bias, final_g, loss_target, m_norm_g, m_w_ada, m_b_ada, m_w_in, m_pool_w, m_pool_scale, m_w_attn_br, m_w_pool_br, m_w_out, m_rel_bias, m_final_g, v_norm_g, v_w_ada, v_b_ada, v_w_in, v_pool_w, v_pool_scale, v_w_attn_br, v_w_pool_br, v_w_out, v_rel_bias, v_final_g):
    S = x.shape[1]
    px, py, pc = _my_place()
    me = 4 * px + 2 * py + pc
    cidx = jnp.reshape(pc, (1,)).astype(jnp.int32)

    layout = [(1, 1024), (1, 128), (1, 128), (0, 128)]
    final_g2 = final_g.reshape(1, D)
    me1 = jnp.reshape(me, (1,)).astype(jnp.int32)

    b_part = lax.dynamic_slice(b_ada, (0, me * 384), (1, 384))
    c_all, mod_all = _adaln_exchange(c, w_ada[0], b_part)
    c_all = c_all.reshape(8, D)
    mod = mod_all.reshape(1, 3 * D)
    shift, scale, gate = mod[:, :D], mod[:, D:2 * D], mod[:, 2 * D:]

    gplan = _gather_chips_plan([(None, 1)] * 4)
    lands = _place_slot([w_in[0], w_attn_br[0], w_pool_br[0], w_out[0]], me1, 8, "place_own_shards", CDT)
    g_sems, _, lands, tok = _split_start("gather_chips_start", [], lands, 8, gplan, dep=mod_all)
    st = _local_bias_tables(_local_pre(x[0], shift, scale, norm_g, dep=tok), rel_bias, dep=tok)
    _, lands = _split_wait("gather_chips_wait", g_sems, [], lands, gplan,
                           [st["ht"]] + [b for pair in st["biases"] for b in pair])
    w_in_full, wab8, wpb8, wout8 = _pair_share(lands)
    wab, wpb = _cols_from_blocks(wab8, "w_attn_br_cols"), _cols_from_blocks(wpb8, "w_pool_br_cols")
    wout = wout8.reshape(D, D)

    st = _local_grads(st, loss_target[0], gate, w_in_full, pool_w[0], pool_scale, wab, wpb, wout, final_g2)

    early = [st["dpw"].reshape(AW, 128), st["dps"], st["gfg"], jnp.broadcast_to(st["loss"], (1, 128))]
    dws = [st["dw_in"], st["dw_ab"], st["dw_pb"], st["dw_out"]]
    sent = [st["dw_in_low"]] + dws[1:] + early
    pplan = _pair_plan(layout)
    pair_shapes = [(4, w, a.shape[1]) if ax == 0 else (4, a.shape[0], w) for a, (ax, w) in zip(dws, layout)]
    p_sems, sent, recv, tok = _split_start(
        "reduce_pair_start", sent,
        [lax.empty(s, a.dtype) for s, a in zip(pair_shapes + [a.shape for a in early], sent)], 20, pplan)
    d_rel_bias = _local_bias_grads(st, _BIAS_GRADS_EARLY, dep=tok)
    sent, recv = _split_wait("reduce_pair_wait", p_sems, sent, recv, pplan, d_rel_bias)
    parts = [_pair_add(dw, r, ax, w, cidx, f"pair_add{i}")
             for i, (dw, r, (ax, w)) in enumerate(zip([dws[0]] + sent[1:4], recv[:4], layout))]
    parts += _add_pairs(sent[4:], recv[4:], "pair_add_smalls")

    slot1 = jnp.reshape(2 * px + py, (1,)).astype(jnp.int32)
    cplan = _chips_plan(4)
    c_sems, parts, lands, tok = _split_start("reduce_chips_start", parts,
                                             _place_slot(parts, slot1, 4, "place_own_partials"), 24, cplan)
    d_rel_bias = d_rel_bias + _local_bias_grads(st, _BIAS_GRADS_LATE, dep=tok)
    grad_x, dmod, d_norm_g = _local_input_grads(st, w_in_full, norm_g, scale, dep=tok)

    dmod_all, ng_all, rb_all = _all_gather_small([dmod, d_norm_g, d_rel_bias], "gather_late")
    o_norm_g = _adam(ng_all, norm_g, m_norm_g, v_norm_g, "adam_norm_g", fresh=True)
    o_rel_bias_t = _adam(rb_all, rel_bias.T, m_rel_bias.T, v_rel_bias.T, "adam_rel_bias", fresh=True)
    o_rel_bias = [o.T for o in o_rel_bias_t]
    o_b_ada = _adam(dmod_all, b_ada, m_b_ada, v_b_ada, "adam_b_ada", fresh=True)
    dm_mine = lax.dynamic_slice(dmod_all.reshape(8, 3 * D), (0, me * 384), (8, 384))
    o_w_ada = [o[None] for o in _adam_w_ada(c_all.T, dm_mine, w_ada[0], m_w_ada[0], v_w_ada[0])]

    _, (g_in, g_ab, g_pb, g_out, g_pw, g_ps, g_fg, g_loss) = _split_wait(
        "reduce_chips_wait", c_sems, parts, lands, cplan, [o_w_ada[0], o_b_ada[0], o_norm_g[0], o_rel_bias_t[0]])
    o_w_in = [o[None] for o in _adam(g_in, w_in[0], m_w_in[0], v_w_in[0], "adam_w_in")]
    o_w_ab = [o[None] for o in _adam(g_ab, w_attn_br[0], m_w_attn_br[0], v_w_attn_br[0], "adam_w_attn_br")]
    o_w_pb = [o[None] for o in _adam(g_pb, w_pool_br[0], m_w_pool_br[0], v_w_pool_br[0], "adam_w_pool_br")]
    o_w_out = [o[None] for o in _adam(g_out, w_out[0], m_w_out[0], v_w_out[0], "adam_w_out")]
    o_pool_w = [o.reshape(pool_w.shape) for o in _adam(g_pw, pool_w.reshape(AW, 128), m_pool_w.reshape(AW, 128),
                                                       v_pool_w.reshape(AW, 128), "adam_pool_w")]
    o_pool_scale = _adam(g_ps, pool_scale, m_pool_scale, v_pool_scale, "adam_pool_scale")
    o_final_g = [o.reshape(D) for o in _adam(g_fg, final_g2, m_final_g.reshape(1, D), v_final_g.reshape(1, D),
                                             "adam_final_g")]
    loss = _sum_parts(g_loss, "sum_loss")[0, 0]

    per_w = [o_norm_g, o_w_ada, o_b_ada, o_w_in, o_pool_w, o_pool_scale, o_w_ab, o_w_pb, o_w_out, o_rel_bias, o_final_g]
    outs = [loss, grad_x[None]]
    for k in range(4):
        outs += [o[k] for o in per_w]
    return tuple(outs)
```

```python
import functools
import math

import numpy as np
import jax
import jax.numpy as jnp
from jax import lax
from jax.experimental import pallas as pl
from jax.experimental.pallas import tpu as pltpu

F32 = jnp.float32
CDT = jnp.bfloat16
D = 1024
HD = 64
NH = 8
AW = 512
QKV_W = 4608
REST_W = 3584
R16 = 16
EPS = 1e-6
NEG = -1e30
N_BACK = 128
ADAM_LR, ADAM_B1, ADAM_B2, ADAM_EPS, ADAM_WD, ADAM_STEP = 0.001, 0.9, 0.999, 1e-08, 0.01, 10
MESH = pl.DeviceIdType.MESH
VMEM_MB = 1024 * 1024


def _cp(sem=None, vmem=48, **kw):
    if sem is not None:
        kw["dimension_semantics"] = sem
    return pltpu.CompilerParams(vmem_limit_bytes=vmem * VMEM_MB, **kw)


def _sds(shape, dtype):
    return jax.ShapeDtypeStruct(tuple(shape), dtype)


def _in_hbm(shape, dtype):
    return pltpu.HBM(tuple(shape), dtype)


def _pin(a):
    return pltpu.with_memory_space_constraint(a, pltpu.HBM)


def _is_big(a):
    return a.size * a.dtype.itemsize >= 4 * VMEM_MB


def _rest_blk(jj):
    return jnp.where(jj == 0, 9, jnp.where(jj == 6, 10, jj + 10))


def _sigmoid(z):
    return 0.5 * jnp.tanh(0.5 * z) + 0.5


def _dot(a, b, dims):
    return lax.dot_general(a, b, (dims, ((), ())), preferred_element_type=F32)


NN = ((1,), (0,))
NT = ((1,), (1,))
TN = ((0,), (0,))


def _ld(ref):
    if len(ref.shape) == 2:
        return ref[...]
    return jnp.concatenate([ref[a] for a in range(ref.shape[0])], axis=0)


def _st(ref, val, cols=None):
    val = val.astype(ref.dtype)
    if len(ref.shape) == 2:
        if cols is None:
            ref[...] = val
        else:
            ref[:, cols] = val
        return
    rows = ref.shape[1]
    for a in range(ref.shape[0]):
        if cols is None:
            ref[a] = val[a * rows:(a + 1) * rows]
        else:
            ref[a, :, cols] = val[a * rows:(a + 1) * rows]


REST_COLS = ((9, 10), (11, 16), (10, 11))
_HBM = pl.BlockSpec(memory_space=pltpu.HBM)


def _resident(shape):
    return pl.BlockSpec(shape, lambda *ids: (0,) * len(shape), pipeline_mode=pl.Buffered(1))


def _w512(w_ref, j):
    return w_ref[j // 2, :, 512 * (j % 2):512 * (j % 2 + 1)]


def _proj(h, w):
    S = h.shape[0]
    tm = 512

    def body(a_ref, w_ref, q_ref, r_ref):
        a = a_ref[...]
        for j in range(9):
            q_ref[:, 512 * j:512 * (j + 1)] = _dot(a, _w512(w_ref, j), NN).astype(q_ref.dtype)
        off = 0
        for lo, hi in REST_COLS:
            for j in range(lo, hi):
                r_ref[:, off:off + 512] = _dot(a, _w512(w_ref, j), NN)
                off += 512

    return pl.pallas_call(
        body, name="proj", grid=(S // tm,),
        in_specs=[pl.BlockSpec((tm, D), lambda i: (i, 0)), _resident((8, D, 1024))],
        out_specs=[pl.BlockSpec((tm, QKV_W), lambda i: (i, 0)), pl.BlockSpec((tm, REST_W), lambda i: (i, 0))],
        out_shape=[_sds((S, QKV_W), CDT), _sds((S, REST_W), F32)],
        compiler_params=_cp(("parallel",), vmem=56),
    )(h, w)


def _after(dep):
    return ([], []) if dep is None else ([pl.BlockSpec(memory_space=pl.ANY)], [dep])


def _mm_tn(at, b, *, n_total, out_blk, name, prev=None, tn=512, with_low=False):
    R, S = at.shape
    nb = b.shape[1] // tn
    n_out = 2 if with_low else 1

    def body(*refs):
        res = _dot(refs[0][...], refs[1][...], NN)
        refs[-n_out][...] = res
        if with_low:
            refs[-1][...] = res.astype(CDT)

    in_specs = [_resident((R, S)), pl.BlockSpec((S, tn), lambda j: (0, j))]
    args = [at, b]
    aliases = {}
    if prev is not None:
        prev = list(prev) if with_low else [prev]
        in_specs += [pl.BlockSpec(memory_space=pl.ANY)] * n_out
        args += prev
        aliases = {2 + k: k for k in range(n_out)}
    out_spec = pl.BlockSpec((R, tn), lambda j: (0, out_blk(j)))
    outs = pl.pallas_call(
        body, name=name, grid=(nb,), in_specs=in_specs,
        out_specs=[out_spec] * n_out,
        out_shape=[_sds((R, n_total), F32), _sds((R, n_total), CDT)][:n_out],
        input_output_aliases=aliases,
        compiler_params=_cp(("parallel",)),
    )(*args)
    return tuple(outs) if with_low else outs[0]


def _tcast(a, name):
    M, C = a.shape
    tm = 512

    def body(a_ref, o_ref):
        o_ref[...] = a_ref[...].astype(F32).T.astype(o_ref.dtype)

    return pl.pallas_call(
        body, name=name, grid=(M // tm,),
        in_specs=[pl.BlockSpec((tm, C), lambda i: (i, 0))],
        out_specs=pl.BlockSpec((C, tm), lambda i: (0, i)),
        out_shape=_sds((C, M), CDT), compiler_params=_cp(("parallel",)),
    )(a)


TOK = R16 * R16


def _swap16():
    i = lax.broadcasted_iota(jnp.int32, (TOK, TOK), 0)
    j = lax.broadcasted_iota(jnp.int32, (TOK, TOK), 1)
    return jnp.where(j == jnp.bitwise_and(i, 15) * 16 + jnp.right_shift(i, 4), 1.0, 0.0).astype(CDT)


def _norm_mod(x, norm_g, shift, scale, dep=None):
    S = x.shape[0]
    dep_specs, dep_args = _after(dep)

    def body(x_ref, g_ref, sh_ref, sc_ref, *rest):
        xv = x_ref[...]
        xn = xv * lax.rsqrt(jnp.mean(xv * xv, axis=-1, keepdims=True) + EPS)
        h = (xn * (g_ref[...] * (1.0 + sc_ref[...])) + sh_ref[...]).astype(CDT)
        _st(rest[-1], _dot(_swap16(), h, NN))

    vec = pl.BlockSpec((1, D), lambda i: (0, 0))
    return pl.pallas_call(
        body, name="norm_mod", grid=(S // TOK,),
        in_specs=[pl.BlockSpec((TOK, D), lambda i: (i, 0)), vec, vec, vec] + dep_specs,
        out_specs=pl.BlockSpec((R16, R16, D), lambda i: (0, i, 0)),
        out_shape=_sds((R16, S // R16, D), CDT), compiler_params=_cp(("parallel",)),
    )(x, norm_g, shift, scale, *dep_args)


def _t5_bucket_np(n):
    nf = np.maximum(n, 1).astype(np.float32)
    large = 16 + (np.log(nf / np.float32(16)) / np.float32(math.log(2048 / 16)) * np.float32(16)).astype(np.int32)
    large = np.minimum(large, 31)
    return np.where(n < 16, n, large).astype(np.int32)


class _Geom:
    def __init__(self, gi, LS):
        self.gi, self.LS = gi, LS
        self.dil = (1, 4, 16)[gi]
        if gi == 0:
            self.lead, self.rows, self.n_tiles, self.has_prev = (R16,), 16, LS // 16, True
            a = np.arange(R16)[:, None]
            ll = np.arange(16)[None, :]
            self.pos = (16 * ll + a).reshape(-1)
        elif gi == 1:
            self.lead, self.rows, self.n_tiles, self.has_prev = (4,), 64, LS // 64, True
            a = np.arange(4)[:, None]
            ll = np.arange(64)[None, :]
            self.pos = (4 * ll + a).reshape(-1)
        else:
            self.lead, self.rows, self.n_tiles, self.has_prev = (), LS, 1, False
            self.pos = np.arange(LS)
        self.T = self.pos.shape[0]
        self.Tk = 2 * self.T if self.has_prev else self.T
        self.outer = {0: (), 1: (4,), 2: (R16,)}[gi]

    def view(self, arr):
        S, C = arr.shape
        if self.gi == 1:
            return arr.reshape(4, 4, self.LS, C)
        return arr.reshape(R16, self.LS, C)

    def view_shape(self, C):
        return (4, 4, self.LS, C) if self.gi == 1 else (R16, self.LS, C)

    def unview(self, arr):
        return arr.reshape(R16 * self.LS, arr.shape[-1])

    def spec(self, W, cblk, tile):
        if self.gi == 0:
            return pl.BlockSpec((R16, 16, W), lambda n: (0, tile(n), cblk))
        if self.gi == 1:
            return pl.BlockSpec((4, None, 64, W), lambda r, n: (0, r, tile(n), cblk))
        return pl.BlockSpec((None, self.LS, W), lambda r: (r, 0, cblk))

    def halves(self):
        R = self.rows
        idx = np.arange(self.T).reshape(-1, R)
        return idx[:, :R // 2].reshape(-1), idx[:, R // 2:].reshape(-1)

    def bucket_index(self, sub):
        lo, hi = self.halves()
        pos = self.pos
        if sub == 0:
            pq = pos[lo]
            pk = np.concatenate([pos[hi] - self.T, pos[lo]]) if self.has_prev else np.concatenate([pos[lo], pos[hi]])
        else:
            pq = pos[hi]
            pk = np.concatenate([pos[lo], pos[hi]])
        dist = pq[:, None] - pk[None, :]
        ok = (dist >= 0) & (dist <= N_BACK)
        bucket = _t5_bucket_np(np.clip(dist, 0, N_BACK) * self.dil)
        return np.where(ok, bucket, -1).astype(np.int32)


def _bias_table(idx, buckets, rel_bias, gi, sub, dep=None):
    T, Tk = idx.shape
    tr = 32
    dep_specs, dep_args = _after(dep)

    def body(idx_ref, rb_ref, *rest):
        h = pl.program_id(0)
        ix = idx_ref[...]
        acc = jnp.full(ix.shape, NEG, F32)
        for b in buckets:
            acc = jnp.where(ix == b, rb_ref[b, gi * NH + h], acc)
        rest[-1][...] = acc

    return pl.pallas_call(
        body, name=f"bias_table{gi}{'ab'[sub]}", grid=(NH, T // tr),
        in_specs=[pl.BlockSpec((tr, Tk), lambda h, i: (i, 0)), pl.BlockSpec(memory_space=pltpu.SMEM)] + dep_specs,
        out_specs=pl.BlockSpec((None, tr, Tk), lambda h, i: (h, i, 0)),
        out_shape=_sds((NH, T, Tk), F32), compiler_params=_cp(("parallel", "parallel")),
    )(idx, rel_bias, *dep_args)


def _bias_grad(dsum, idx, buckets, gi, sub, dep=None):
    _, T, Tk = dsum.shape
    dep_specs, dep_args = _after(dep)

    def body(ds_ref, idx_ref, *rest):
        o_ref, r_ref = rest[-2], rest[-1]
        r_ref[...] = jnp.zeros_like(r_ref)
        ix, ds = idx_ref[...], ds_ref[...]
        for b in buckets:
            r_ref[b:b + 1, :] = jnp.sum(jnp.where(ix == b, ds, 0.0), axis=0, keepdims=True)
        o_ref[...] = jnp.broadcast_to(jnp.sum(r_ref[...], axis=1, keepdims=True), (32, 128))

    out = pl.pallas_call(
        body, name=f"bias_grad{gi}{'ab'[sub]}", grid=(NH,),
        in_specs=[pl.BlockSpec((None, T, Tk), lambda h: (h, 0, 0)), pl.BlockSpec((T, Tk), lambda h: (0, 0))] + dep_specs,
        out_specs=pl.BlockSpec((None, 32, 128), lambda h: (h, 0, 0)),
        out_shape=_sds((NH, 32, 128), F32), scratch_shapes=[pltpu.VMEM((32, Tk), F32)],
        compiler_params=_cp(("parallel",)),
    )(dsum, idx, *dep_args)
    return out[:, :, 0]


def _prev_mask(n, T, Tk):
    col = lax.broadcasted_iota(jnp.int32, (1, Tk), 1)
    return jnp.where((col < T) & (n == 0), NEG, 0.0).astype(F32)


def _split_rows(v, R):
    A, half = v.shape[0] // R, R // 2
    via_f32 = v.dtype != F32 and half % 16 != 0
    w = v.astype(F32) if via_f32 else v
    lo = jnp.concatenate([w[a * R:a * R + half] for a in range(A)], axis=0)
    hi = jnp.concatenate([w[a * R + half:(a + 1) * R] for a in range(A)], axis=0)
    return (lo.astype(v.dtype), hi.astype(v.dtype)) if via_f32 else (lo, hi)


def _merge_rows(lo, hi, R):
    half = R // 2
    A = lo.shape[0] // half
    return jnp.concatenate([x[a * half:(a + 1) * half] for a in range(A) for x in (lo, hi)], axis=0)


def _attn_fwd(qkv, biases, geom, others=None):
    S = qkv.shape[0]
    gi, T, R, has_prev = geom.gi, geom.T, geom.rows, geom.has_prev
    H2 = T // 2
    qv = geom.view(qkv)
    cq, ck, cv = 3 * gi, 3 * gi + 1, 3 * gi + 2
    assert others is None or not has_prev

    def body(*refs):
        if has_prev:
            q_ref, kp_ref, kc_ref, vp_ref, vc_ref, ba_ref, bb_ref, o_ref, l_ref = refs
        elif others is not None:
            (q_ref, kc_ref, vc_ref, ba_ref, bb_ref, o0_ref, o1_ref, l0_ref, l1_ref, z_ref,
             attn_ref, L_ref, ag_ref) = refs
        else:
            q_ref, kc_ref, vc_ref, ba_ref, bb_ref, o_ref, l_ref = refs
        q_lo, q_hi = _split_rows(_ld(q_ref), R)
        kc_lo, kc_hi = _split_rows(_ld(kc_ref), R)
        vc_lo, vc_hi = _split_rows(_ld(vc_ref), R)
        k_b = jnp.concatenate([kc_lo, kc_hi], axis=0)
        v_b = jnp.concatenate([vc_lo, vc_hi], axis=0)
        if has_prev:
            n = pl.program_id(len(geom.outer))
            k_a = jnp.concatenate([_split_rows(_ld(kp_ref), R)[1], kc_lo], axis=0)
            v_a = jnp.concatenate([_split_rows(_ld(vp_ref), R)[1], vc_lo], axis=0)
            pm = _prev_mask(n, H2, T)
        else:
            k_a, v_a, pm = k_b, v_b, None
        low = lax.broadcasted_iota(jnp.int32, (1, 2 * HD), 1) < HD
        for j in range(NH // 2):
            cols = slice(2 * HD * j, 2 * HD * (j + 1))
            o_half, l_half = [], []
            for q, k, v, b_ref, mask in ((q_lo, k_a, v_a, ba_ref, pm), (q_hi, k_b, v_b, bb_ref, None)):
                q2 = q[:, cols].astype(F32) * 0.125
                k2, v2 = k[:, cols], v[:, cols]
                qs = jnp.concatenate([jnp.where(low, q2, 0.0), jnp.where(low, 0.0, q2)], axis=0).astype(q.dtype)
                s = _dot(qs, k2, NT) + b_ref[j]
                if mask is not None:
                    s = s + mask
                m = jnp.max(s, axis=1, keepdims=True)
                p = jnp.exp(s - m)
                l = jnp.sum(p, axis=1, keepdims=True)
                pv = _dot(p.astype(v.dtype), v2, NN) / l
                lse = m + jnp.log(l)
                o_half.append(jnp.where(low, pv[:H2], pv[H2:]))
                l_half.append(jnp.where(low, lse[:H2], lse[H2:]))
            o2, lc = _merge_rows(o_half[0], o_half[1], R), _merge_rows(l_half[0], l_half[1], R)
            if others is None:
                _st(o_ref, o2, cols)
                _st(l_ref, lc, cols)
                continue
            la, lb = l0_ref[:, cols], l1_ref[:, cols]
            m = jnp.maximum(jnp.maximum(la, lb), lc)
            ea, eb, ec = jnp.exp(la - m), jnp.exp(lb - m), jnp.exp(lc - m)
            den = ea + eb + ec
            attn = (ea * o0_ref[:, cols] + eb * o1_ref[:, cols] + ec * o2) / den
            attn_ref[:, cols] = attn
            L_ref[:, cols] = m + jnp.log(den)
            z = z_ref[:, cols]
            ag = attn * (z * _sigmoid(z))
            ag_ref[:, cols] = ag.astype(ag_ref.dtype)

    cur = lambda n: n
    prev = lambda n: jnp.maximum(n - 1, 0)
    if has_prev:
        in_specs = [geom.spec(AW, cq, cur), geom.spec(AW, ck, prev), geom.spec(AW, ck, cur),
                    geom.spec(AW, cv, prev), geom.spec(AW, cv, cur)]
        args = [qv] * 5
    else:
        in_specs = [geom.spec(AW, cq, cur), geom.spec(AW, ck, cur), geom.spec(AW, cv, cur)]
        args = [qv] * 3
    nd = len(geom.outer) + (1 if has_prev else 0)
    biases = [b.reshape(NH // 2, 2 * H2, b.shape[2]) for b in biases]
    in_specs += [pl.BlockSpec(b.shape, lambda *ids: (0, 0, 0)) for b in biases]
    grid = geom.outer + ((geom.n_tiles,) if has_prev else ())
    oshape = geom.view_shape(AW)
    tile = geom.spec(AW, 0, cur)
    if others is None:
        o, l = pl.pallas_call(
            body, name=f"attn_fwd{gi}", grid=grid, in_specs=in_specs, out_specs=[tile, tile],
            out_shape=[_sds(oshape, F32), _sds(oshape, F32)],
            compiler_params=_cp(("arbitrary",) * nd),
        )(*args, *biases)
        return geom.unview(o), geom.unview(l)
    os_, ls_, rest = others
    attn, Lb, ag = pl.pallas_call(
        body, name=f"attn_fwd{gi}_merge", grid=grid, in_specs=in_specs + [tile] * 5,
        out_specs=[tile, tile, tile],
        out_shape=[_sds(oshape, F32), _sds(oshape, F32), _sds(oshape, CDT)],
        compiler_params=_cp(("arbitrary",) * nd),
    )(*args, *biases, *[geom.view(a) for a in (*os_, *ls_, rest)])
    return geom.unview(attn), geom.unview(Lb), geom.unview(ag)


def _pool_counts(r, LS, win):
    l = lax.broadcasted_iota(jnp.int32, (LS, 1), 0)
    return jnp.minimum(16 * l + (r + 1), win).astype(F32)


def _pool_fwd(rest, pool_w, pool_scale):
    S = rest.shape[0]
    LS = S // R16
    r3 = rest.reshape(R16, LS, REST_W)
    PG = 128

    def body(u_ref, z_ref, pw_ref, ps_ref, pt_ref, mp_ref, pg_ref, e_ref):
        g = pl.program_id(0)
        win = jnp.left_shift(2, g)
        row = lax.broadcasted_iota(jnp.int32, (LS, PG), 0)
        for r in range(R16):
            e_ref[R16 + r] = u_ref[r]
        for r in range(1, R16):
            e_ref[r] = jnp.where(row == 0, 0.0, pltpu.roll(u_ref[r], 1, 0))

        def double(s):
            for i in range(2 * R16 - 1, 2 * s - 1, -1):
                e_ref[i] = e_ref[i] + e_ref[i - s]

        double(1)
        for k in (1, 2, 3):
            pl.when(g >= k)(functools.partial(double, 2 ** k))
        pw = pw_ref[...].astype(CDT)
        for r in range(R16):
            u = u_ref[r]
            pooled = e_ref[R16 + r] / _pool_counts(r, LS, win) - u
            pc = pooled.astype(CDT)
            mp = _dot(pc, pw, NN)
            z = z_ref[r]
            pg = (mp * ps_ref[...]) * (z * _sigmoid(z))
            mp_ref[r] = mp
            pg_ref[r] = pg.astype(pg_ref.dtype)
            pt_ref[:, r * LS:(r + 1) * LS] = pooled.T.astype(pt_ref.dtype)

    col = lambda off: pl.BlockSpec((R16, LS, PG), lambda g: (0, 0, off + g))
    tsp = pl.BlockSpec((PG, S), lambda g: (g, 0))
    pt, mp, pg = pl.pallas_call(
        body, name="pool_fwd", grid=(4,),
        in_specs=[col(24), col(4), pl.BlockSpec((None, PG, PG), lambda g: (g, 0, 0)),
                  pl.BlockSpec((1, PG), lambda g: (0, g))],
        out_specs=[tsp, col(0), col(0)],
        out_shape=[_in_hbm((AW, S), CDT), _sds((R16, LS, AW), F32), _in_hbm((R16, LS, AW), CDT)],
        scratch_shapes=[pltpu.VMEM((2 * R16, LS, PG), F32)],
        compiler_params=_cp(("parallel",)),
    )(r3, r3, pool_w, pool_scale)
    return pt, mp.reshape(S, AW), pg.reshape(S, AW)


def _merge_out_loss(ag, pg, rest, wab, wpb, wout, x, target, gate, final_g):
    S = x.shape[0]
    LS = S // R16

    def body(ag_ref, pg_ref, ga_ref, gp_ref, wa_ref, wp_ref, w_ref, x_ref, t_ref, gate_ref, fg_ref,
             dx2_ref, dmo_ref, loss_ref, gfg_ref, dgate_ref, dw_ref):
        i = pl.program_id(0)

        @pl.when(i == 0)
        def _():
            loss_ref[...] = jnp.zeros_like(loss_ref)
            gfg_ref[...] = jnp.zeros_like(gfg_ref)
            dgate_ref[...] = jnp.zeros_like(dgate_ref)
            dw_ref[...] = jnp.zeros_like(dw_ref)

        swap = _swap16()
        ya = _dot(_ld(ag_ref), wa_ref[...], NN)
        yp = _dot(_ld(pg_ref), wp_ref[...], NN)
        mg = (_sigmoid(_ld(ga_ref)) * ya + _sigmoid(_ld(gp_ref)) * yp).astype(CDT)
        m_nat = _dot(swap, mg, NN)
        mo = _dot(m_nat.astype(CDT), w_ref[...], NN)
        gate, fg = gate_ref[...], fg_ref[...]
        x2 = x_ref[...] + gate * mo
        r2 = lax.rsqrt(jnp.mean(x2 * x2, axis=-1, keepdims=True) + EPS)
        xn2 = x2 * r2
        diff = xn2 * fg - t_ref[...]
        loss = jnp.sum(jnp.sum(diff * diff, axis=-1, keepdims=True), axis=0, keepdims=True)
        dy = diff * (1.0 / D)
        dxn2 = dy * fg
        dx2 = r2 * (dxn2 - xn2 * jnp.mean(dxn2 * xn2, axis=-1, keepdims=True))
        dx2_ref[...] = dx2
        dmo = (dx2 * gate).astype(CDT)
        _st(dmo_ref, _dot(swap, dmo, NN))
        loss_ref[...] += jnp.broadcast_to(loss * (0.5 / D), loss_ref.shape)
        gfg_ref[...] += jnp.sum(dy * xn2, axis=0, keepdims=True)
        dgate_ref[...] += jnp.sum(dx2 * mo, axis=0, keepdims=True)
        dw_ref[...] += _dot(m_nat.T.astype(CDT), dmo, NN)

    def slab(w, cblk):
        return pl.BlockSpec((R16, R16, w), lambda i: (0, i, cblk))

    nat = pl.BlockSpec((TOK, D), lambda i: (i, 0))
    vec = pl.BlockSpec((1, D), lambda i: (0, 0))
    r3 = rest.reshape(R16, LS, REST_W)
    return pl.pallas_call(
        body, name="merge_out_loss", grid=(S // TOK,),
        in_specs=[slab(AW, 0), slab(AW, 0), slab(D, 1), slab(D, 2), _resident((AW, D)), _resident((AW, D)),
                  _resident((D, D)), nat, nat, vec, vec],
        out_specs=[nat, slab(D, 0), pl.BlockSpec((1, 128), lambda i: (0, 0)), vec, vec,
                   pl.BlockSpec((D, D), lambda i: (0, 0))],
        out_shape=[_sds((S, D), F32), _sds((R16, LS, D), CDT), _in_hbm((1, 128), F32), _in_hbm((1, D), F32),
                   _in_hbm((1, D), F32), _sds((D, D), F32)],
        compiler_params=_cp(("arbitrary",)),
    )(ag.reshape(R16, LS, AW), pg.reshape(R16, LS, AW), r3, r3, wab, wpb, wout, x, target, gate, final_g)


def _bwd_gates(dmo, wout, wab, wpb, rest, ag, pg, attn, mp, pool_scale):
    S = dmo.shape[0]
    tm = 256

    def body(dmo_ref, wo_ref, wa_ref, wp_ref, za_ref, zp_ref, ga0, ga1, gp0, gp1, ag_ref, pg_ref, at_ref, mp_ref,
             ps_ref, dwa_ref, dwp_ref, dr_ref, dat_ref, dl_ref, dmp_ref, dps_ref):
        i = pl.program_id(0)

        @pl.when(i == 0)
        def _():
            dps_ref[...] = jnp.zeros_like(dps_ref)
            dwa_ref[...] = jnp.zeros_like(dwa_ref)
            dwp_ref[...] = jnp.zeros_like(dwp_ref)

        dm = _dot(dmo_ref[...], wo_ref[...], NT)
        sa = _sigmoid(jnp.concatenate([ga0[...], ga1[...]], axis=1))
        sp = _sigmoid(jnp.concatenate([gp0[...], gp1[...]], axis=1))
        dya = (dm * sa).astype(CDT)
        dyp = (dm * sp).astype(CDT)
        ag, pg = ag_ref[...], pg_ref[...]
        dwa_ref[...] += _dot(ag.astype(F32).T.astype(CDT), dya, NN)
        dwp_ref[...] += _dot(pg.astype(F32).T.astype(CDT), dyp, NN)
        ya = _dot(ag, wa_ref[...], NN)
        yp = _dot(pg, wp_ref[...], NN)
        dr_ref[:, 1024:2048] = (dm * ya * sa * (1.0 - sa)).astype(dr_ref.dtype)
        dr_ref[:, 2048:3072] = (dm * yp * sp * (1.0 - sp)).astype(dr_ref.dtype)
        da = _dot(dya, wa_ref[...], NT)
        dp = _dot(dyp, wp_ref[...], NT)
        za = za_ref[...]
        sga = _sigmoid(za)
        attn = at_ref[...]
        dattn = da * (za * sga)
        dr_ref[:, 0:512] = (da * attn * (sga * (1.0 + za * (1.0 - sga)))).astype(dr_ref.dtype)
        dat_ref[...] = dattn.astype(dat_ref.dtype)
        prod = dattn * attn
        low = lax.broadcasted_iota(jnp.int32, (1, 2 * HD), 1) < HD
        for j in range(NH // 2):
            x2 = prod[:, 2 * HD * j:2 * HD * (j + 1)]
            s0 = jnp.sum(jnp.where(low, x2, 0.0), axis=1, keepdims=True)
            s1 = jnp.sum(jnp.where(low, 0.0, x2), axis=1, keepdims=True)
            dl_ref[:, 2 * HD * j:2 * HD * (j + 1)] = jnp.where(low, s0, s1)
        zp = zp_ref[...]
        sgp = _sigmoid(zp)
        mpre = mp_ref[...]
        ps = ps_ref[...]
        dmixed = dp * (zp * sgp)
        dr_ref[:, 512:1024] = (dp * (mpre * ps) * (sgp * (1.0 + zp * (1.0 - sgp)))).astype(dr_ref.dtype)
        dps_ref[...] += jnp.sum(dmixed * mpre, axis=0, keepdims=True)
        dmp_ref[...] = (dmixed * ps).astype(dmp_ref.dtype)

    full = lambda shape: pl.BlockSpec(shape, lambda i: (0, 0))
    tD = pl.BlockSpec((tm, D), lambda i: (i, 0))
    tA = pl.BlockSpec((tm, AW), lambda i: (i, 0))
    rb = lambda c: pl.BlockSpec((tm, AW), lambda i: (i, c))
    return pl.pallas_call(
        body, name="bwd_gates", grid=(S // tm,),
        in_specs=[tD, full((D, D)), full((AW, D)), full((AW, D)), rb(0), rb(1), rb(2), rb(3), rb(4), rb(5),
                  tA, tA, tA, tA, full((1, AW))],
        out_specs=[full((AW, D)), full((AW, D)), pl.BlockSpec((tm, 3072), lambda i: (i, 0)), tA, tA, tA, full((1, AW))],
        out_shape=[_sds((AW, D), F32), _sds((AW, D), F32), _sds((S, REST_W), CDT), _sds((S, AW), CDT),
                   _sds((S, AW), F32), _sds((S, AW), CDT), _sds((1, AW), F32)],
        compiler_params=_cp(("arbitrary",), vmem=56),
    )(dmo, wout, wab, wpb, rest, rest, rest, rest, rest, rest, ag, pg, attn, mp, pool_scale)


def _pool_bwd(dmp, pt, pool_w, drest):
    S = dmp.shape[0]
    LS = S // R16
    PG = 128
    d3 = dmp.reshape(R16, LS, AW)
    dr3 = drest.reshape(R16, LS, REST_W)

    def body(d_ref, pt_ref, pw_ref, dr_in, du_ref, dpw_ref, g_ref, s_ref):
        del dr_in
        g = pl.program_id(0)
        win = jnp.left_shift(2, g)
        row = lax.broadcasted_iota(jnp.int32, (LS, PG), 0)
        pw = pw_ref[...].astype(CDT)
        dpw = jnp.zeros((PG, PG), F32)
        for r in range(R16):
            dm = d_ref[r]
            dpw = dpw + _dot(pt_ref[:, r * LS:(r + 1) * LS], dm, NN)
            dpooled = _dot(dm, pw, NT)
            g_ref[r] = dpooled
            sc = dpooled / _pool_counts(r, LS, win)
            s_ref[r] = sc
            if r < R16 - 1:
                s_ref[R16 + r] = jnp.where(row == LS - 1, 0.0, pltpu.roll(sc, LS - 1, 0))
        dpw_ref[...] = dpw

        def double(s):
            for i in range(0, 2 * R16 - 2 * s):
                s_ref[i] = s_ref[i] + s_ref[i + s]

        double(1)
        for k in (1, 2, 3):
            pl.when(g >= k)(functools.partial(double, 2 ** k))
        for r in range(R16):
            du_ref[r] = (s_ref[r] - g_ref[r]).astype(du_ref.dtype)

    du, dpw = pl.pallas_call(
        body, name="pool_bwd", grid=(4,),
        in_specs=[pl.BlockSpec((R16, LS, PG), lambda g: (0, 0, g)), pl.BlockSpec((PG, S), lambda g: (g, 0)),
                  pl.BlockSpec((None, PG, PG), lambda g: (g, 0, 0)), pl.BlockSpec(memory_space=pl.ANY)],
        out_specs=[pl.BlockSpec((R16, LS, PG), lambda g: (0, 0, 24 + g)),
                   pl.BlockSpec((None, PG, PG), lambda g: (g, 0, 0))],
        out_shape=[_sds((R16, LS, REST_W), CDT), _sds((4, PG, PG), F32)],
        scratch_shapes=[pltpu.VMEM((R16, LS, PG), F32), pltpu.VMEM((2 * R16, LS, PG), F32)],
        input_output_aliases={3: 0},
        compiler_params=_cp(("parallel",)),
    )(d3, pt, pool_w, dr3)
    return du.reshape(S, REST_W), dpw


def _attn_bwd(qkv, dattn, Lb, dlb, biases, geom, dqkv_prev):
    S = qkv.shape[0]
    gi, T, R, has_prev, N = geom.gi, geom.T, geom.rows, geom.has_prev, geom.n_tiles
    H2 = T // 2
    nd_outer = len(geom.outer)
    qv = geom.view(qkv)
    cq, ck, cv = 3 * gi, 3 * gi + 1, 3 * gi + 2
    W3 = 3 * AW

    def heads(q_ref, kp_ref, kc_ref, vp_ref, vc_ref, do_ref, L_ref, dl_ref, ba_ref, bb_ref, pm, dsa_ref, dsb_ref, emit):
        q_lo, q_hi = _split_rows(_ld(q_ref), R)
        do_lo, do_hi = _split_rows(_ld(do_ref), R)
        L_lo, L_hi = _split_rows(_ld(L_ref), R)
        dl_lo, dl_hi = _split_rows(_ld(dl_ref), R)
        kc_lo, kc_hi = _split_rows(_ld(kc_ref), R)
        vc_lo, vc_hi = _split_rows(_ld(vc_ref), R)
        k_b = jnp.concatenate([kc_lo, kc_hi], axis=0)
        v_b = jnp.concatenate([vc_lo, vc_hi], axis=0)
        if has_prev:
            k_a = jnp.concatenate([_split_rows(_ld(kp_ref), R)[1], kc_lo], axis=0)
            v_a = jnp.concatenate([_split_rows(_ld(vp_ref), R)[1], vc_lo], axis=0)
        else:
            k_a, v_a = k_b, v_b
        low = lax.broadcasted_iota(jnp.int32, (1, 2 * HD), 1) < HD
        for j in range(NH // 2):
            sl = slice(2 * HD * j, 2 * HD * (j + 1))
            res = []
            for q, k, v, do, L, dl, b_ref, mask, ds_ref in (
                    (q_lo, k_a, v_a, do_lo, L_lo, dl_lo, ba_ref, pm, dsa_ref),
                    (q_hi, k_b, v_b, do_hi, L_hi, dl_hi, bb_ref, None, dsb_ref)):
                q2 = q[:, sl].astype(F32) * 0.125
                do2 = do[:, sl].astype(F32)
                k2, v2 = k[:, sl], v[:, sl]
                L2, dl2 = L[:, sl], dl[:, sl]
                qs = jnp.concatenate([jnp.where(low, q2, 0.0), jnp.where(low, 0.0, q2)], axis=0).astype(q.dtype)
                dom = jnp.concatenate([jnp.where(low, do2, 0.0), jnp.where(low, 0.0, do2)], axis=0).astype(q.dtype)
                Lc = jnp.concatenate([L2[:, 0:1], L2[:, HD:HD + 1]], axis=0)
                dlc = jnp.concatenate([dl2[:, 0:1], dl2[:, HD:HD + 1]], axis=0)
                s = _dot(qs, k2, NT) + b_ref[j]
                if mask is not None:
                    s = s + mask
                p = jnp.exp(s - Lc)
                ds = p * (_dot(dom, v2, NT) - dlc)
                ds_ref[j] += ds
                dsc = ds.astype(q.dtype)
                dq_st = _dot(dsc, k2, NN)
                res.append((jnp.where(low, dq_st[:H2], dq_st[H2:]) * 0.125, _dot(dsc, qs, TN),
                            _dot(p.astype(q.dtype), dom, TN)))
            (dq_lo, dk_a, dv_a), (dq_hi, dk_b, dv_b) = res
            dq = _merge_rows(dq_lo, dq_hi, R)
            if has_prev:
                emit(sl, dq, _merge_rows(dk_a[H2:] + dk_b[:H2], dk_b[H2:], R),
                     _merge_rows(dv_a[H2:] + dv_b[:H2], dv_b[H2:], R), dk_a[:H2], dv_a[:H2])
            else:
                emit(sl, dq, _merge_rows(dk_a[:H2] + dk_b[:H2], dk_a[H2:] + dk_b[H2:], R),
                     _merge_rows(dv_a[:H2] + dv_b[:H2], dv_a[H2:] + dv_b[H2:], R), None, None)

    def first_step():
        ok = pl.program_id(nd_outer) == 0 if has_prev else pl.program_id(0) == 0
        for a in range(nd_outer if has_prev else 0):
            ok = ok & (pl.program_id(a) == 0)
        return ok

    if has_prev:
        def body(q_ref, kp_ref, kc_ref, vp_ref, vc_ref, do_ref, L_ref, dl_ref, ba_ref, bb_ref, prev_in, out_ref,
                 dsa_ref, dsb_ref, hold, new, pp):
            del prev_in
            s_id = pl.program_id(nd_outer)

            @pl.when(first_step())
            def _():
                dsa_ref[...] = jnp.zeros_like(dsa_ref)
                dsb_ref[...] = jnp.zeros_like(dsb_ref)

            @pl.when(s_id < N)
            def _():
                def emit(sl, dq, dk, dv, dk_prev, dv_prev):
                    new[:, sl] = dq
                    new[:, slice(AW + sl.start, AW + sl.stop)] = dk
                    new[:, slice(2 * AW + sl.start, 2 * AW + sl.stop)] = dv
                    pp[:, sl] = dk_prev
                    pp[:, slice(AW + sl.start, AW + sl.stop)] = dv_prev

                heads(q_ref, kp_ref, kc_ref, vp_ref, vc_ref, do_ref, L_ref, dl_ref, ba_ref, bb_ref,
                      _prev_mask(s_id, H2, T), dsa_ref, dsb_ref, emit)

            @pl.when((s_id >= 1) & (s_id < N))
            def _():
                _st(out_ref, hold[:, 0:AW], slice(0, AW))
                late = _merge_rows(jnp.zeros((H2, 2 * AW), F32), pp[...], R)
                _st(out_ref, hold[:, AW:W3] + late, slice(AW, W3))

            @pl.when(s_id == N)
            def _():
                _st(out_ref, hold[...])

            @pl.when(s_id < N)
            def _():
                hold[...] = new[...]

        cur = lambda n: jnp.minimum(n, N - 1)
        prev = lambda n: jnp.clip(n - 1, 0, N - 1)
        delayed = lambda n: jnp.maximum(n - 1, 0)
        in_specs = [geom.spec(AW, cq, cur), geom.spec(AW, ck, prev), geom.spec(AW, ck, cur),
                    geom.spec(AW, cv, prev), geom.spec(AW, cv, cur),
                    geom.spec(AW, 0, cur), geom.spec(AW, 0, cur), geom.spec(AW, 0, cur)]
        args = [qv] * 5 + [geom.view(dattn), geom.view(Lb), geom.view(dlb)]
        out_spec = geom.spec(W3, gi, delayed)
        grid = geom.outer + (N + 1,)
        scratch = [pltpu.VMEM((T, W3), F32), pltpu.VMEM((T, W3), F32), pltpu.VMEM((H2, 2 * AW), F32)]
    else:
        def body(q_ref, kc_ref, vc_ref, do_ref, L_ref, dl_ref, ba_ref, bb_ref, prev_in, out_ref, dsa_ref, dsb_ref):
            del prev_in

            @pl.when(first_step())
            def _():
                dsa_ref[...] = jnp.zeros_like(dsa_ref)
                dsb_ref[...] = jnp.zeros_like(dsb_ref)

            def emit(sl, dq, dk, dv, dk_prev, dv_prev):
                _st(out_ref, dq, sl)
                _st(out_ref, dk, slice(AW + sl.start, AW + sl.stop))
                _st(out_ref, dv, slice(2 * AW + sl.start, 2 * AW + sl.stop))

            heads(q_ref, None, kc_ref, None, vc_ref, do_ref, L_ref, dl_ref, ba_ref, bb_ref, None, dsa_ref, dsb_ref,
                  emit)

        cur = lambda n: n
        in_specs = [geom.spec(AW, cq, cur), geom.spec(AW, ck, cur), geom.spec(AW, cv, cur),
                    geom.spec(AW, 0, cur), geom.spec(AW, 0, cur), geom.spec(AW, 0, cur)]
        args = [qv] * 3 + [geom.view(dattn), geom.view(Lb), geom.view(dlb)]
        out_spec = geom.spec(W3, gi, cur)
        grid = geom.outer
        scratch = []

    whole = lambda b: pl.BlockSpec(b.shape, lambda *ids: (0, 0, 0))
    biases = [b.reshape(NH // 2, 2 * H2, b.shape[2]) for b in biases]
    in_specs += [whole(b) for b in biases]
    in_specs.append(pl.BlockSpec(memory_space=pl.ANY))
    if dqkv_prev is None:
        last, aliases = biases[0], {}
    else:
        last, aliases = geom.view(dqkv_prev), {len(in_specs) - 1: 0}
    dq, dsa, dsb = pl.pallas_call(
        body, name=f"attn_bwd{gi}", grid=grid, in_specs=in_specs,
        out_specs=[out_spec] + [whole(b) for b in biases],
        out_shape=[_sds(geom.view_shape(QKV_W), CDT)] + [_sds(b.shape, F32) for b in biases],
        scratch_shapes=scratch,
        input_output_aliases=aliases,
        compiler_params=_cp(("arbitrary",) * len(grid), vmem=56),
    )(*args, *biases, last)
    return geom.unview(dq), [d.reshape(NH, H2, d.shape[2]) for d in (dsa, dsb)]


def _local_step(x, target, shift, scale, gate, norm_g, w_in_full, pool_w, pool_scale, wab, wpb, wout, rel_bias,
                final_g):
    st = _local_bias_tables(_local_pre(x, shift, scale, norm_g), rel_bias)
    st = _local_grads(st, target, gate, w_in_full, pool_w, pool_scale, wab, wpb, wout, final_g)
    d_rel_bias = _local_bias_grads(st, _BIAS_GRADS_EARLY + _BIAS_GRADS_LATE)
    gx, dmod, dng = _local_input_grads(st, w_in_full, norm_g, scale)
    return dict(loss=st["loss"], grad_x=gx, dw_in=st["dw_in"], dw_ab=st["dw_ab"], dw_pb=st["dw_pb"],
                dw_out=st["dw_out"], d_norm_g=dng, d_pool_w=st["dpw"], d_pool_scale=st["dps"],
                d_rel_bias=d_rel_bias.T, d_final_g=st["gfg"], dmod=dmod)


def _local_pre(x, shift, scale, norm_g, dep=None):
    S = x.shape[0]
    h = _norm_mod(x, norm_g, shift, scale, dep).reshape(S, D)
    return dict(x=x, h=h, ht=_tcast(h, "h_transpose"))


def _local_bias_tables(st, rel_bias, dep=None):
    LS = st["x"].shape[0] // R16
    geoms = [_Geom(gi, LS) for gi in range(3)]
    idx_np = [[g.bucket_index(sub) for sub in range(2)] for g in geoms]
    buckets = [[tuple(int(b) for b in np.unique(ix) if b >= 0) for ix in pair] for pair in idx_np]
    idxs = [[jnp.asarray(ix) for ix in pair] for pair in idx_np]
    biases = [[_bias_table(idxs[gi][sub], buckets[gi][sub], rel_bias, gi, sub, dep) for sub in range(2)]
              for gi in range(3)]
    return dict(st, geoms=geoms, idxs=idxs, buckets=buckets, biases=biases)


def _local_grads(st, target, gate, w_in_full, pool_w, pool_scale, wab, wpb, wout, final_g):
    geoms, idxs, biases, x, h, ht = (st[k] for k in ("geoms", "idxs", "biases", "x", "h", "ht"))
    S = h.shape[0]
    LS = S // R16
    qkv, rest = _proj(h, w_in_full)

    os_, ls_ = [], []
    for gi in range(2):
        o, l = _attn_fwd(qkv, biases[gi], geoms[gi])
        os_.append(o)
        ls_.append(l)
    attn, Lb, ag = _attn_fwd(qkv, biases[2], geoms[2], others=(os_, ls_, rest))
    pt, mp, pg = _pool_fwd(rest, pool_w, pool_scale)
    dx2, dmo3, loss, gfg, dgate, dw_out = _merge_out_loss(ag, pg, rest, wab, wpb, wout, x, target, gate, final_g)
    dmo = dmo3.reshape(S, D)

    dw_ab, dw_pb, drest, dattn, dlb, dmp, dps = _bwd_gates(dmo, wout, wab, wpb, rest, ag, pg, attn, mp, pool_scale)
    drest, dpw = _pool_bwd(dmp, pt, pool_w, drest)

    dqkv = None
    dsums = []
    for gi in range(3):
        dqkv, dsum = _attn_bwd(qkv, dattn, Lb, dlb, biases[gi], geoms[gi], dqkv)
        dsums.append(dsum)

    dw_in = _mm_tn(ht, dqkv, n_total=8192, out_blk=lambda j: j, name="dw_in_qkv", with_low=True)
    dw_in, dw_in_low = _mm_tn(ht, drest, n_total=8192, out_blk=_rest_blk, name="dw_in_rest", prev=dw_in, with_low=True)
    return dict(st, loss=loss[0, 0], dw_in=dw_in, dw_in_low=dw_in_low, dw_ab=dw_ab, dw_pb=dw_pb, dw_out=dw_out, dpw=dpw, dps=dps, gfg=gfg,
                dgate=dgate, dsums=dsums, dqkv=dqkv, drest=drest, dx2=dx2)


def _local_bias_grads(st, which, dep=None):
    rows = []
    for gi in range(3):
        got = [_bias_grad(st["dsums"][gi][sub], st["idxs"][gi][sub], st["buckets"][gi][sub], gi, sub, dep)
               for sub in range(2) if (gi, sub) in which]
        rows.append(sum(got) if got else jnp.zeros((NH, 32), F32))
    return jnp.concatenate(rows, axis=0)


_BIAS_GRADS_EARLY = ((0, 0), (0, 1), (1, 0), (2, 0))
_BIAS_GRADS_LATE = ((1, 1), (2, 1))


def _dh_h_bwd(dqkv, drest, w, dx2, x, norm_g, scale, dep=None):
    S = x.shape[0]
    dep_specs, dep_args = _after(dep)

    def body(a_ref, b_ref, w_ref, dx2_ref, x_ref, g_ref, sc_ref, *rest):
        gx_ref, dsh_ref, dsc_ref, dng_ref, nat_ref = rest[-5:]
        i = pl.program_id(0)

        @pl.when(i == 0)
        def _():
            dsh_ref[...] = jnp.zeros_like(dsh_ref)
            dsc_ref[...] = jnp.zeros_like(dsc_ref)
            dng_ref[...] = jnp.zeros_like(dng_ref)

        a, b = _ld(a_ref), _ld(b_ref)
        acc = _dot(a[:, 0:512], _w512(w_ref, 0), NT)
        for j in range(1, 9):
            acc = acc + _dot(a[:, 512 * j:512 * (j + 1)], _w512(w_ref, j), NT)
        off = 0
        for lo, hi in REST_COLS:
            for j in range(lo, hi):
                acc = acc + _dot(b[:, off:off + 512], _w512(w_ref, j), NT)
                off += 512
        for r in range(R16):
            nat_ref[:, r, :] = acc[R16 * r:R16 * (r + 1)]
        dh = jnp.concatenate([nat_ref[l] for l in range(R16)], axis=0)
        g = g_ref[...]
        one_sc = 1.0 + sc_ref[...]
        xv = x_ref[...]
        r1 = lax.rsqrt(jnp.mean(xv * xv, axis=-1, keepdims=True) + EPS)
        xn = xv * r1
        dhx = dh * xn
        dxn = dh * (g * one_sc)
        dx1 = r1 * (dxn - xn * jnp.mean(dxn * xn, axis=-1, keepdims=True))
        gx_ref[...] = dx2_ref[...] + dx1
        dsh_ref[...] += jnp.sum(dh, axis=0, keepdims=True)
        dsc_ref[...] += jnp.sum(dhx * g, axis=0, keepdims=True)
        dng_ref[...] += jnp.sum(dhx * one_sc, axis=0, keepdims=True)

    nat = pl.BlockSpec((TOK, D), lambda i: (i, 0))
    vec = pl.BlockSpec((1, D), lambda i: (0, 0))
    slab = lambda W: pl.BlockSpec((R16, R16, W), lambda i: (0, i, 0))
    return pl.pallas_call(
        body, name="dh_h_bwd", grid=(S // TOK,),
        in_specs=[slab(QKV_W), slab(REST_W), _resident((8, D, 1024)), nat, nat, vec, vec] + dep_specs,
        out_specs=[nat, vec, vec, vec],
        out_shape=[_sds((S, D), F32), _sds((1, D), F32), _sds((1, D), F32), _sds((1, D), F32)],
        scratch_shapes=[pltpu.VMEM((R16, R16, D), F32)],
        compiler_params=_cp(("arbitrary",), vmem=56),
    )(dqkv, drest, w, dx2, x, norm_g, scale, *dep_args)


def _local_input_grads(st, w_in_full, norm_g, scale, dep=None):
    S = st["h"].shape[0]
    LS = S // R16
    gx, dsh, dsc, dng = _dh_h_bwd(st["dqkv"].reshape(R16, LS, QKV_W), st["drest"].reshape(R16, LS, REST_W),
                                  w_in_full, st["dx2"], st["x"], norm_g, scale, dep)
    return gx, jnp.concatenate([dsh, dsc, st["dgate"]], axis=1), dng


def _my_place():
    return lax.axis_index("x"), lax.axis_index("y"), lax.axis_index("c")


def _block_of(ref, axis, idx, width):
    if axis is None:
        return ref.at[idx]
    start = pl.multiple_of(idx * width, width)
    if axis == 0:
        return ref.at[pl.ds(start, width)]
    return ref.at[:, pl.ds(start, width)]


_SEM = pl.BlockSpec(memory_space=pltpu.SEMAPHORE)
_EFFECT = pltpu.SideEffectType.DATAFLOW_SIDE_EFFECTING


def _split_start(name, srcs, lands, ncopy, plan, dep=None):
    arrs = list(srcs) + list(lands)
    na, ns = len(arrs), len(srcs)
    dep_specs, dep_args = _after(dep)

    def body(*refs):
        send_sems, recv_sems, token = refs[na + len(dep_args)], refs[na + len(dep_args) + 1], refs[-1]
        for k, (s, d, dev) in enumerate(plan(refs[:ns], refs[ns:na])):
            pltpu.make_async_remote_copy(src_ref=s, dst_ref=d, send_sem=send_sems.at[k], recv_sem=recv_sems.at[k],
                                         device_id=dev, device_id_type=MESH).start()
        token[...] = jnp.zeros_like(token)

    outs = pl.pallas_call(
        body, name=name,
        out_shape=(pltpu.SemaphoreType.DMA((ncopy,)), pltpu.SemaphoreType.DMA((ncopy,)),
                   *[pltpu.HBM(a.shape, a.dtype) for a in arrs], _sds((8, 128), F32)),
        in_specs=[_HBM] * na + dep_specs,
        out_specs=(_SEM, _SEM, *[_HBM] * na, pl.BlockSpec(memory_space=pltpu.VMEM)),
        input_output_aliases={i: 2 + i for i in range(na)},
        compiler_params=pltpu.CompilerParams(has_side_effects=_EFFECT),
    )(*[pltpu.with_memory_space_constraint(a, pltpu.HBM) for a in arrs], *dep_args)
    return (outs[0], outs[1]), list(outs[2:2 + ns]), list(outs[2 + ns:2 + na]), outs[-1]


def _split_wait(name, sems, srcs, lands, plan, after):
    arrs = list(srcs) + list(lands)
    na, ns = len(arrs), len(srcs)
    after = list(after) if isinstance(after, (list, tuple)) else [after]

    def body(*refs):
        send_sems, recv_sems = refs[na], refs[na + 1]
        for k, (s, d, dev) in enumerate(plan(refs[:ns], refs[ns:na])):
            cp = pltpu.make_async_remote_copy(src_ref=s, dst_ref=d, send_sem=send_sems.at[k], recv_sem=recv_sems.at[k],
                                              device_id=dev, device_id_type=MESH)
            cp.wait_send()
            cp.wait_recv()

    outs = pl.pallas_call(
        body, name=name, out_shape=[pltpu.HBM(a.shape, a.dtype) for a in arrs],
        in_specs=[_HBM] * na + [_SEM, _SEM] + [pl.BlockSpec(memory_space=pl.ANY)] * len(after),
        out_specs=[_HBM] * na,
        input_output_aliases={i: i for i in range(na)},
        compiler_params=pltpu.CompilerParams(has_side_effects=_EFFECT),
    )(*arrs, sems[0], sems[1], *after)
    return list(outs[:ns]), list(outs[ns:])


def _gather_chips_plan(layout):
    def plan(src, land):
        x, y, c = _my_place()
        me = 4 * x + 2 * y + c
        mine = [_block_of(land[p], layout[p][0], me, layout[p][1]) for p in range(len(land))]
        return [(mine[p], mine[p], (cx, cy, c)) for p in range(len(land)) for cx, cy in [(1 - x, y), (x, 1 - y)]]
    return plan


def _pair_plan(layout):
    def plan(src, land):
        x, y, c = _my_place()
        n = len(layout)
        return [(_block_of(src[p], layout[p][0], 2 * q + (1 - c), layout[p][1]), land[p].at[q], (x, y, 1 - c))
                for p in range(n) for q in range(4)] + [(src[p], land[p], (x, y, 1 - c)) for p in range(n, len(src))]
    return plan


def _chips_plan(n):
    def plan(src, land):
        x, y, c = _my_place()
        chips = [(1 - x, y), (x, 1 - y), (1 - x, 1 - y)]
        return [(src[p].at[2 * cx + cy], land[p].at[2 * x + y], (cx, cy, c)) for p in range(n) for cx, cy in chips] + \
               [(src[p], land[p].at[2 * x + y], (cx, cy, c)) for p in range(n, len(src)) for cx, cy in chips]
    return plan


def _pair_share(gathered):
    n = len(gathered)

    def body(*refs):
        land = refs[n:2 * n]
        send_sems, recv_sems, fwd_send, fwd_recv = refs[2 * n:]
        x, y, c = _my_place()
        k_dg = 3 - (2 * x + y)
        south = c == 0
        pass_on = 4 * jnp.where(south, 1 - x, x) + 2 * jnp.where(south, y, 1 - y) + c
        to = (jnp.where(south, x, 1 - x), jnp.where(south, 1 - y, y), c)

        def forward(p):
            return pltpu.make_async_remote_copy(
                src_ref=land[p].at[pass_on], dst_ref=land[p].at[pass_on], send_sem=fwd_send.at[p],
                recv_sem=fwd_recv.at[p], device_id=to, device_id_type=MESH)

        def to_sibling(p, k):
            blk = land[p].at[2 * k + c]
            return pltpu.make_async_remote_copy(
                src_ref=blk, dst_ref=blk, send_sem=send_sems.at[4 * p + k], recv_sem=recv_sems.at[4 * p + k],
                device_id=(x, y, 1 - c), device_id_type=MESH)

        for p in range(n):
            forward(p).start()
        for p in range(n):
            for k in range(4):
                @pl.when(k != k_dg)
                def _():
                    to_sibling(p, k).start()
        for p in range(n):
            dg = land[p].at[2 * k_dg + c]
            pltpu.make_async_remote_copy(src_ref=dg, dst_ref=dg, send_sem=fwd_send.at[p], recv_sem=fwd_recv.at[p],
                                         device_id=to, device_id_type=MESH).wait_recv()
            for k in range(4):
                @pl.when(k == k_dg)
                def _():
                    to_sibling(p, k).start()
        for p in range(n):
            forward(p).wait_send()
            for k in range(4):
                to_sibling(p, k).wait_send()
                blk = land[p].at[2 * k + 1 - c]
                pltpu.make_async_remote_copy(
                    src_ref=blk, dst_ref=blk, send_sem=send_sems.at[4 * p + k], recv_sem=recv_sems.at[4 * p + k],
                    device_id=(x, y, 1 - c), device_id_type=MESH).wait_recv()

    return pl.pallas_call(
        body, name="gather_pair_share", in_specs=[_HBM] * n, out_specs=[_HBM] * n,
        out_shape=[_sds(g.shape, g.dtype) for g in gathered],
        input_output_aliases={p: p for p in range(n)},
        scratch_shapes=[pltpu.SemaphoreType.DMA((4 * n,)), pltpu.SemaphoreType.DMA((4 * n,)),
                        pltpu.SemaphoreType.DMA((n,)), pltpu.SemaphoreType.DMA((n,))],
    )(*gathered)


def _place_slot(arrs, slot, n_slots, name, out_dtype=None):
    n = len(arrs)

    def body(s_ref, *refs):
        for p in range(n):
            refs[n + p][...] = refs[p][...].astype(refs[n + p].dtype)

    def spec(a):
        tail = a.shape[-2:]
        return pl.BlockSpec((None,) + tail, lambda i, s_ref: (s_ref[0], 0, 0))

    in_specs = [spec(a) if a.ndim == 3 else pl.BlockSpec(a.shape, lambda i, s_ref: (0, 0)) for a in arrs]
    return pl.pallas_call(
        body, name=name,
        grid_spec=pltpu.PrefetchScalarGridSpec(num_scalar_prefetch=1, grid=(1,), in_specs=in_specs,
                                               out_specs=[spec(a) for a in arrs]),
        out_shape=[_in_hbm((n_slots,) + a.shape[-2:], out_dtype or a.dtype) for a in arrs],
        compiler_params=_cp(("arbitrary",)),
    )(slot, *[_pin(a) for a in arrs])


def _cols_from_blocks(w8, name):
    _, R, C = w8.shape

    def body(w_ref, o_ref):
        for e in range(8):
            o_ref[:, C * e:C * (e + 1)] = w_ref[e]

    return pl.pallas_call(body, name=name, out_shape=_sds((R, 8 * C), w8.dtype), compiler_params=_cp())(w8)


def _all_gather_small(srcs, name):
    n = len(srcs)
    vmem = pl.BlockSpec(memory_space=pltpu.VMEM)

    def body(*refs):
        src, out = refs[:n], refs[n:2 * n]
        send_sems, recv_sems = refs[2 * n:]
        x, y, c = _my_place()
        me = 4 * x + 2 * y + c
        flip = lambda v, f: 1 - v if f else v
        peers = [(flip(x, fx), flip(y, fy), flip(c, fc)) for fx in (0, 1) for fy in (0, 1) for fc in (0, 1)][1:]
        cps = []
        for p in range(n):
            out[p][me] = src[p][...]
            for k, peer in enumerate(peers):
                cps.append(pltpu.make_async_remote_copy(
                    src_ref=src[p], dst_ref=out[p].at[me], send_sem=send_sems.at[7 * p + k],
                    recv_sem=recv_sems.at[7 * p + k], device_id=peer, device_id_type=MESH))
        for cp in cps:
            cp.start()
        for cp in cps:
            cp.wait()

    return pl.pallas_call(
        body, name=name, in_specs=[vmem] * n, out_specs=[vmem] * n,
        out_shape=[_sds((8,) + a.shape, a.dtype) for a in srcs],
        scratch_shapes=[pltpu.SemaphoreType.DMA((7 * n,)), pltpu.SemaphoreType.DMA((7 * n,))],
    )(*srcs)


def _adaln_exchange(c, w_ada, b_part):
    vmem = pl.BlockSpec(memory_space=pltpu.VMEM)

    def body(c_ref, w_hbm, b_ref, call_ref, mod_ref, part_ref, w_ref, w_sem, send_sems, recv_sems):
        load_w = pltpu.make_async_copy(w_hbm, w_ref, w_sem.at[0])
        load_w.start()
        x, y, c = _my_place()
        me = 4 * x + 2 * y + c
        flip = lambda v, f: 1 - v if f else v
        peers = [(flip(x, fx), flip(y, fy), flip(c, fc)) for fx in (0, 1) for fy in (0, 1) for fc in (0, 1)][1:]
        call_ref[me] = c_ref[...]
        first = [pltpu.make_async_remote_copy(src_ref=c_ref, dst_ref=call_ref.at[me], send_sem=send_sems.at[k],
                                              recv_sem=recv_sems.at[k], device_id=peer, device_id_type=MESH)
                 for k, peer in enumerate(peers)]
        for cp in first:
            cp.start()
        for cp in first:
            cp.wait()
        c_all = jnp.concatenate([call_ref[j] for j in range(8)], axis=0)
        load_w.wait()
        part_ref[...] = _dot(c_all.astype(CDT), w_ref[...].astype(CDT), NN) + b_ref[...]
        mod_ref[me] = part_ref[pl.ds(me, 1), :]
        second = [pltpu.make_async_remote_copy(
            src_ref=part_ref.at[pl.ds(4 * px + 2 * py + pc, 1)], dst_ref=mod_ref.at[me], send_sem=send_sems.at[7 + k],
            recv_sem=recv_sems.at[7 + k], device_id=(px, py, pc), device_id_type=MESH)
            for k, (px, py, pc) in enumerate(peers)]
        for cp in second:
            cp.start()
        for cp in second:
            cp.wait()

    return pl.pallas_call(
        body, name="adaln_exchange", in_specs=[vmem, _HBM, vmem], out_specs=[vmem] * 2,
        out_shape=[_sds((8, 1, D), F32), _sds((8, 1, w_ada.shape[1]), F32)],
        scratch_shapes=[pltpu.VMEM((8, w_ada.shape[1]), F32), pltpu.VMEM(w_ada.shape, F32),
                        pltpu.SemaphoreType.DMA((1,)), pltpu.SemaphoreType.DMA((14,)), pltpu.SemaphoreType.DMA((14,))],
        compiler_params=_cp(),
    )(c, _pin(w_ada), b_part)


def _pair_add(dw, recv, axis, width, cidx, name):
    if axis == 0:
        tr = min(width, 256)
        C = dw.shape[1]
        grid = (4, width // tr)
        nb = width // tr
        own = pl.BlockSpec((tr, C), lambda q, i, c_ref: ((2 * q + c_ref[0]) * nb + i, 0))
        oth = pl.BlockSpec((None, tr, C), lambda q, i, c_ref: (q, i, 0))
    else:
        R = dw.shape[0]
        tr = min(R, 1024)
        grid = (4, R // tr)
        own = pl.BlockSpec((tr, width), lambda q, i, c_ref: (i, 2 * q + c_ref[0]))
        oth = pl.BlockSpec((None, tr, width), lambda q, i, c_ref: (q, i, 0))

    def body(c_ref, a_ref, b_ref, o_ref):
        o_ref[...] = (a_ref[...] + b_ref[...].astype(F32)).astype(o_ref.dtype)

    return pl.pallas_call(
        body, name=name,
        grid_spec=pltpu.PrefetchScalarGridSpec(num_scalar_prefetch=1, grid=grid, in_specs=[own, oth], out_specs=oth),
        out_shape=(_in_hbm if _is_big(dw) else _sds)(recv.shape, CDT), compiler_params=_cp(("parallel", "parallel")),
    )(cidx, *[_pin(a) if _is_big(dw) else a for a in (dw, recv)])


def _add_pairs(xs, ys, name):
    n = len(xs)

    def body(*refs):
        for p in range(n):
            refs[2 * n + p][...] = refs[p][...] + refs[n + p][...]

    return pl.pallas_call(body, name=name, out_shape=[_sds(a.shape, F32) for a in xs])(*xs, *ys)


def _adam_math(w, g, m, v):
    m = ADAM_B1 * m + (1.0 - ADAM_B1) * g
    v = ADAM_B2 * v + (1.0 - ADAM_B2) * (g * g)
    m_hat = m / (1.0 - ADAM_B1 ** ADAM_STEP)
    v_hat = v / (1.0 - ADAM_B2 ** ADAM_STEP)
    delta = -ADAM_LR * (m_hat / (jnp.sqrt(v_hat) + ADAM_EPS) + ADAM_WD * w)
    return delta, m, v


def _adam(gparts, w, m, v, name, fresh=False):
    P, R, C = gparts.shape
    in_hbm = fresh or _is_big(w)
    tr = R if R <= 256 else 256

    def body(g_ref, w_ref, m_ref, v_ref, go_ref, d_ref, mo_ref, vo_ref):
        g = g_ref[0].astype(F32)
        for i in range(1, P):
            g = g + g_ref[i].astype(F32)
        d, mn, vn = _adam_math(w_ref[...], g, m_ref[...], v_ref[...])
        go_ref[...] = g
        d_ref[...] = d
        mo_ref[...] = mn
        vo_ref[...] = vn

    t = pl.BlockSpec((tr, C), lambda i: (i, 0))
    return pl.pallas_call(
        body, name=name, grid=(R // tr,),
        in_specs=[pl.BlockSpec((P, tr, C), lambda i: (0, i, 0)), t, t, t], out_specs=[t, t, t, t],
        out_shape=[(_in_hbm if in_hbm else _sds)((R, C), F32)] * 4, compiler_params=_cp(("parallel",)),
    )(*[_pin(a) if in_hbm else a for a in (gparts, w, m, v)])


def _sum_parts(gparts, name):
    P, R, C = gparts.shape

    def body(g_ref, o_ref):
        g = g_ref[0]
        for i in range(1, P):
            g = g + g_ref[i]
        o_ref[...] = g

    return pl.pallas_call(body, name=name, out_shape=_sds((R, C), F32))(gparts)


def _adam_w_ada(c_t, dm, w, m, v):
    R, C = w.shape

    def body(c_ref, dm_ref, w_ref, m_ref, v_ref, go_ref, d_ref, mo_ref, vo_ref):
        g = c_ref[:, 0:1] * dm_ref[0:1, :]
        for b in range(1, 8):
            g = g + c_ref[:, b:b + 1] * dm_ref[b:b + 1, :]
        d, mn, vn = _adam_math(w_ref[...], g, m_ref[...], v_ref[...])
        go_ref[...] = g
        d_ref[...] = d
        mo_ref[...] = mn
        vo_ref[...] = vn

    return pl.pallas_call(body, name="adam_w_ada", out_shape=[_sds((R, C), F32)] * 4,
                          compiler_params=_cp())(c_t, dm, w, m, v)


def kernel(x, c, norm_g, w_ada, b_ada, w_in, pool_w, pool_scale, w_attn_br, w_pool_br, w_out, rel_bias, final_g, loss_target, m_norm_g, m_w_ada, m_b_ada, m_w_in, m_pool_w, m_pool_scale, m_w_attn_br, m_w_pool_br, m_w_out, m_rel_bias, m_final_g, v_norm_g, v_w_ada, v_b_ada, v_w_in, v_pool_w, v_pool_scale, v_w_attn_br, v_w_pool_br, v_w_out, v_rel_bias, v_final_g):
    S = x.shape[1]
    px, py, pc = _my_place()
    me = 4 * px + 2 * py + pc
    cidx = jnp.reshape(pc, (1,)).astype(jnp.int32)

    layout = [(1, 1024), (1, 128), (1, 128), (0, 128)]
    final_g2 = final_g.reshape(1, D)
    me1 = jnp.reshape(me, (1,)).astype(jnp.int32)

    b_part = lax.dynamic_slice(b_ada, (0, me * 384), (1, 384))
    c_all, mod_all = _adaln_exchange(c, w_ada[0], b_part)
    c_all = c_all.reshape(8, D)
    mod = mod_all.reshape(1, 3 * D)
    shift, scale, gate = mod[:, :D], mod[:, D:2 * D], mod[:, 2 * D:]

    gplan = _gather_chips_plan([(None, 1)] * 4)
    lands = _place_slot([w_in[0], w_attn_br[0], w_pool_br[0], w_out[0]], me1, 8, "place_own_shards", CDT)
    g_sems, _, lands, tok = _split_start("gather_chips_start", [], lands, 8, gplan, dep=mod_all)
    st = _local_bias_tables(_local_pre(x[0], shift, scale, norm_g, dep=tok), rel_bias, dep=tok)
    _, lands = _split_wait("gather_chips_wait", g_sems, [], lands, gplan,
                           [st["ht"]] + [b for pair in st["biases"] for b in pair])
    w_in_full, wab8, wpb8, wout8 = _pair_share(lands)
    wab, wpb = _cols_from_blocks(wab8, "w_attn_br_cols"), _cols_from_blocks(wpb8, "w_pool_br_cols")
    wout = wout8.reshape(D, D)

    st = _local_grads(st, loss_target[0], gate, w_in_full, pool_w[0], pool_scale, wab, wpb, wout, final_g2)

    early = [st["dpw"].reshape(AW, 128), st["dps"], st["gfg"], jnp.broadcast_to(st["loss"], (1, 128))]
    dws = [st["dw_in"], st["dw_ab"], st["dw_pb"], st["dw_out"]]
    sent = [st["dw_in_low"]] + dws[1:] + early
    pplan = _pair_plan(layout)
    pair_shapes = [(4, w, a.shape[1]) if ax == 0 else (4, a.shape[0], w) for a, (ax, w) in zip(dws, layout)]
    p_sems, sent, recv, tok = _split_start(
        "reduce_pair_start", sent,
        [lax.empty(s, a.dtype) for s, a in zip(pair_shapes + [a.shape for a in early], sent)], 20, pplan)
    d_rel_bias = _local_bias_grads(st, _BIAS_GRADS_EARLY, dep=tok)
    sent, recv = _split_wait("reduce_pair_wait", p_sems, sent, recv, pplan, d_rel_bias)
    parts = [_pair_add(dw, r, ax, w, cidx, f"pair_add{i}")
             for i, (dw, r, (ax, w)) in enumerate(zip([dws[0]] + sent[1:4], recv[:4], layout))]
    parts += _add_pairs(sent[4:], recv[4:], "pair_add_smalls")

    slot1 = jnp.reshape(2 * px + py, (1,)).astype(jnp.int32)
    cplan = _chips_plan(4)
    c_sems, parts, lands, tok = _split_start("reduce_chips_start", parts,
                                             _place_slot(parts, slot1, 4, "place_own_partials"), 24, cplan)
    d_rel_bias = d_rel_bias + _local_bias_grads(st, _BIAS_GRADS_LATE, dep=tok)
    grad_x, dmod, d_norm_g = _local_input_grads(st, w_in_full, norm_g, scale, dep=tok)

    dmod_all, ng_all, rb_all = _all_gather_small([dmod, d_norm_g, d_rel_bias], "gather_late")
    o_norm_g = _adam(ng_all, norm_g, m_norm_g, v_norm_g, "adam_norm_g", fresh=True)
    o_rel_bias_t = _adam(rb_all, rel_bias.T, m_rel_bias.T, v_rel_bias.T, "adam_rel_bias", fresh=True)
    o_rel_bias = [o.T for o in o_rel_bias_t]
    o_b_ada = _adam(dmod_all, b_ada, m_b_ada, v_b_ada, "adam_b_ada", fresh=True)
    dm_mine = lax.dynamic_slice(dmod_all.reshape(8, 3 * D), (0, me * 384), (8, 384))
    o_w_ada = [o[None] for o in _adam_w_ada(c_all.T, dm_mine, w_ada[0], m_w_ada[0], v_w_ada[0])]

    _, (g_in, g_ab, g_pb, g_out, g_pw, g_ps, g_fg, g_loss) = _split_wait(
        "reduce_chips_wait", c_sems, parts, lands, cplan, [o_w_ada[0], o_b_ada[0], o_norm_g[0], o_rel_bias_t[0]])
    o_w_in = [o[None] for o in _adam(g_in, w_in[0], m_w_in[0], v_w_in[0], "adam_w_in")]
    o_w_ab = [o[None] for o in _adam(g_ab, w_attn_br[0], m_w_attn_br[0], v_w_attn_br[0], "adam_w_attn_br")]
    o_w_pb = [o[None] for o in _adam(g_pb, w_pool_br[0], m_w_pool_br[0], v_w_pool_br[0], "adam_w_pool_br")]
    o_w_out = [o[None] for o in _adam(g_out, w_out[0], m_w_out[0], v_w_out[0], "adam_w_out")]
    o_pool_w = [o.reshape(pool_w.shape) for o in _adam(g_pw, pool_w.reshape(AW, 128), m_pool_w.reshape(AW, 128),
                                                       v_pool_w.reshape(AW, 128), "adam_pool_w")]
    o_pool_scale = _adam(g_ps, pool_scale, m_pool_scale, v_pool_scale, "adam_pool_scale")
    o_final_g = [o.reshape(D) for o in _adam(g_fg, final_g2, m_final_g.reshape(1, D), v_final_g.reshape(1, D),
                                             "adam_final_g")]
    loss = _sum_parts(g_loss, "sum_loss")[0, 0]

    per_w = [o_norm_g, o_w_ada, o_b_ada, o_w_in, o_pool_w, o_pool_scale, o_w_ab, o_w_pb, o_w_out, o_rel_bias, o_final_g]
    outs = [loss, grad_x[None]]
    for k in range(4):
        outs += [o[k] for o in per_w]
    return tuple(outs)
```

```python
import functools
import math

import numpy as np
import jax
import jax.numpy as jnp
from jax import lax
from jax.experimental import pallas as pl
from jax.experimental.pallas import tpu as pltpu

F32 = jnp.float32
CDT = jnp.bfloat16
D = 1024
HD = 64
NH = 8
AW = 512
QKV_W = 4608
REST_W = 3584
R16 = 16
EPS = 1e-6
NEG = -1e30
N_BACK = 128
ADAM_LR, ADAM_B1, ADAM_B2, ADAM_EPS, ADAM_WD, ADAM_STEP = 0.001, 0.9, 0.999, 1e-08, 0.01, 10
MESH = pl.DeviceIdType.MESH
VMEM_MB = 1024 * 1024


def _cp(sem=None, vmem=48, **kw):
    if sem is not None:
        kw["dimension_semantics"] = sem
    return pltpu.CompilerParams(vmem_limit_bytes=vmem * VMEM_MB, **kw)


def _sds(shape, dtype):
    return jax.ShapeDtypeStruct(tuple(shape), dtype)


def _in_hbm(shape, dtype):
    return pltpu.HBM(tuple(shape), dtype)


def _pin(a):
    return pltpu.with_memory_space_constraint(a, pltpu.HBM)


def _is_big(a):
    return a.size * a.dtype.itemsize >= 4 * VMEM_MB


def _rest_blk(jj):
    return jnp.where(jj == 0, 9, jnp.where(jj == 6, 10, jj + 10))


def _sigmoid(z):
    return 0.5 * jnp.tanh(0.5 * z) + 0.5


def _dot(a, b, dims):
    return lax.dot_general(a, b, (dims, ((), ())), preferred_element_type=F32)


NN = ((1,), (0,))
NT = ((1,), (1,))
TN = ((0,), (0,))


def _ld(ref):
    if len(ref.shape) == 2:
        return ref[...]
    return jnp.concatenate([ref[a] for a in range(ref.shape[0])], axis=0)


def _st(ref, val, cols=None):
    val = val.astype(ref.dtype)
    if len(ref.shape) == 2:
        if cols is None:
            ref[...] = val
        else:
            ref[:, cols] = val
        return
    rows = ref.shape[1]
    for a in range(ref.shape[0]):
        if cols is None:
            ref[a] = val[a * rows:(a + 1) * rows]
        else:
            ref[a, :, cols] = val[a * rows:(a + 1) * rows]


REST_COLS = ((9, 10), (11, 16), (10, 11))
_HBM = pl.BlockSpec(memory_space=pltpu.HBM)


def _resident(shape):
    return pl.BlockSpec(shape, lambda *ids: (0,) * len(shape), pipeline_mode=pl.Buffered(1))


def _w512(w_ref, j):
    return w_ref[j // 2, :, 512 * (j % 2):512 * (j % 2 + 1)]


def _proj(h, w):
    S = h.shape[0]
    tm = 512

    def body(a_ref, w_ref, q_ref, r_ref):
        a = a_ref[...]
        for j in range(9):
            q_ref[:, 512 * j:512 * (j + 1)] = _dot(a, _w512(w_ref, j), NN).astype(q_ref.dtype)
        off = 0
        for lo, hi in REST_COLS:
            for j in range(lo, hi):
                r_ref[:, off:off + 512] = _dot(a, _w512(w_ref, j), NN)
                off += 512

    return pl.pallas_call(
        body, name="proj", grid=(S // tm,),
        in_specs=[pl.BlockSpec((tm, D), lambda i: (i, 0)), _resident((8, D, 1024))],
        out_specs=[pl.BlockSpec((tm, QKV_W), lambda i: (i, 0)), pl.BlockSpec((tm, REST_W), lambda i: (i, 0))],
        out_shape=[_sds((S, QKV_W), CDT), _sds((S, REST_W), F32)],
        compiler_params=_cp(("parallel",), vmem=56),
    )(h, w)


def _after(dep):
    return ([], []) if dep is None else ([pl.BlockSpec(memory_space=pl.ANY)], [dep])


def _mm_tn(at, b, *, n_total, out_blk, name, prev=None, tn=512, with_low=False):
    R, S = at.shape
    nb = b.shape[1] // tn
    n_out = 2 if with_low else 1

    def body(*refs):
        res = _dot(refs[0][...], refs[1][...], NN)
        refs[-n_out][...] = res
        if with_low:
            refs[-1][...] = res.astype(CDT)

    in_specs = [_resident((R, S)), pl.BlockSpec((S, tn), lambda j: (0, j))]
    args = [at, b]
    aliases = {}
    if prev is not None:
        prev = list(prev) if with_low else [prev]
        in_specs += [pl.BlockSpec(memory_space=pl.ANY)] * n_out
        args += prev
        aliases = {2 + k: k for k in range(n_out)}
    out_spec = pl.BlockSpec((R, tn), lambda j: (0, out_blk(j)))
    outs = pl.pallas_call(
        body, name=name, grid=(nb,), in_specs=in_specs,
        out_specs=[out_spec] * n_out,
        out_shape=[_sds((R, n_total), F32), _sds((R, n_total), CDT)][:n_out],
        input_output_aliases=aliases,
        compiler_params=_cp(("parallel",)),
    )(*args)
    return tuple(outs) if with_low else outs[0]


def _tcast(a, name):
    M, C = a.shape
    tm = 512

    def body(a_ref, o_ref):
        o_ref[...] = a_ref[...].astype(F32).T.astype(o_ref.dtype)

    return pl.pallas_call(
        body, name=name, grid=(M // tm,),
        in_specs=[pl.BlockSpec((tm, C), lambda i: (i, 0))],
        out_specs=pl.BlockSpec((C, tm), lambda i: (0, i)),
        out_shape=_sds((C, M), CDT), compiler_params=_cp(("parallel",)),
    )(a)


TOK = R16 * R16
TILES = 2


def _swap16():
    i = lax.broadcasted_iota(jnp.int32, (TOK, TOK), 0)
    j = lax.broadcasted_iota(jnp.int32, (TOK, TOK), 1)
    return jnp.where(j == jnp.bitwise_and(i, 15) * 16 + jnp.right_shift(i, 4), 1.0, 0.0).astype(CDT)


def _norm_mod(x, norm_g, shift, scale, dep=None):
    S = x.shape[0]
    dep_specs, dep_args = _after(dep)

    def body(x_ref, g_ref, sh_ref, sc_ref, *rest):
        xv = x_ref[...]
        xn = xv * lax.rsqrt(jnp.mean(xv * xv, axis=-1, keepdims=True) + EPS)
        h = (xn * (g_ref[...] * (1.0 + sc_ref[...])) + sh_ref[...]).astype(CDT)
        _st(rest[-1], _dot(_swap16(), h, NN))

    vec = pl.BlockSpec((1, D), lambda i: (0, 0))
    return pl.pallas_call(
        body, name="norm_mod", grid=(S // TOK,),
        in_specs=[pl.BlockSpec((TOK, D), lambda i: (i, 0)), vec, vec, vec] + dep_specs,
        out_specs=pl.BlockSpec((R16, R16, D), lambda i: (0, i, 0)),
        out_shape=_sds((R16, S // R16, D), CDT), compiler_params=_cp(("parallel",)),
    )(x, norm_g, shift, scale, *dep_args)


def _t5_bucket_np(n):
    nf = np.maximum(n, 1).astype(np.float32)
    large = 16 + (np.log(nf / np.float32(16)) / np.float32(math.log(2048 / 16)) * np.float32(16)).astype(np.int32)
    large = np.minimum(large, 31)
    return np.where(n < 16, n, large).astype(np.int32)


class _Geom:
    def __init__(self, gi, LS):
        self.gi, self.LS = gi, LS
        self.dil = (1, 4, 16)[gi]
        if gi == 0:
            self.lead, self.rows, self.n_tiles, self.has_prev = (R16,), 16, LS // 16, True
            a = np.arange(R16)[:, None]
            ll = np.arange(16)[None, :]
            self.pos = (16 * ll + a).reshape(-1)
        elif gi == 1:
            self.lead, self.rows, self.n_tiles, self.has_prev = (4,), 64, LS // 64, True
            a = np.arange(4)[:, None]
            ll = np.arange(64)[None, :]
            self.pos = (4 * ll + a).reshape(-1)
        else:
            self.lead, self.rows, self.n_tiles, self.has_prev = (), LS, 1, False
            self.pos = np.arange(LS)
        self.T = self.pos.shape[0]
        self.Tk = 2 * self.T if self.has_prev else self.T
        self.outer = {0: (), 1: (4,), 2: (R16,)}[gi]

    def view(self, arr):
        S, C = arr.shape
        if self.gi == 1:
            return arr.reshape(4, 4, self.LS, C)
        return arr.reshape(R16, self.LS, C)

    def view_shape(self, C):
        return (4, 4, self.LS, C) if self.gi == 1 else (R16, self.LS, C)

    def unview(self, arr):
        return arr.reshape(R16 * self.LS, arr.shape[-1])

    def spec(self, W, cblk, tile):
        if self.gi == 0:
            return pl.BlockSpec((R16, 16, W), lambda n: (0, tile(n), cblk))
        if self.gi == 1:
            return pl.BlockSpec((4, None, 64, W), lambda r, n: (0, r, tile(n), cblk))
        return pl.BlockSpec((None, self.LS, W), lambda r: (r, 0, cblk))

    def halves(self):
        R = self.rows
        idx = np.arange(self.T).reshape(-1, R)
        return idx[:, :R // 2].reshape(-1), idx[:, R // 2:].reshape(-1)

    def bucket_index(self, sub):
        lo, hi = self.halves()
        pos = self.pos
        if sub == 0:
            pq = pos[lo]
            pk = np.concatenate([pos[hi] - self.T, pos[lo]]) if self.has_prev else np.concatenate([pos[lo], pos[hi]])
        else:
            pq = pos[hi]
            pk = np.concatenate([pos[lo], pos[hi]])
        dist = pq[:, None] - pk[None, :]
        ok = (dist >= 0) & (dist <= N_BACK)
        bucket = _t5_bucket_np(np.clip(dist, 0, N_BACK) * self.dil)
        return np.where(ok, bucket, -1).astype(np.int32)


def _bias_table(idx, buckets, rel_bias, gi, sub, dep=None):
    T, Tk = idx.shape
    tr = 32
    dep_specs, dep_args = _after(dep)

    def body(idx_ref, rb_ref, *rest):
        h = pl.program_id(0)
        ix = idx_ref[...]
        acc = jnp.full(ix.shape, NEG, F32)
        for b in buckets:
            acc = jnp.where(ix == b, rb_ref[b, gi * NH + h], acc)
        rest[-1][...] = acc

    return pl.pallas_call(
        body, name=f"bias_table{gi}{'ab'[sub]}", grid=(NH, T // tr),
        in_specs=[pl.BlockSpec((tr, Tk), lambda h, i: (i, 0)), pl.BlockSpec(memory_space=pltpu.SMEM)] + dep_specs,
        out_specs=pl.BlockSpec((None, tr, Tk), lambda h, i: (h, i, 0)),
        out_shape=_sds((NH, T, Tk), F32), compiler_params=_cp(("parallel", "parallel")),
    )(idx, rel_bias, *dep_args)


def _bias_grad(dsum, idx, buckets, gi, sub, dep=None):
    _, T, Tk = dsum.shape
    dep_specs, dep_args = _after(dep)

    def body(ds_ref, idx_ref, *rest):
        o_ref, r_ref = rest[-2], rest[-1]
        r_ref[...] = jnp.zeros_like(r_ref)
        ix, ds = idx_ref[...], ds_ref[...]
        for b in buckets:
            r_ref[b:b + 1, :] = jnp.sum(jnp.where(ix == b, ds, 0.0), axis=0, keepdims=True)
        o_ref[...] = jnp.broadcast_to(jnp.sum(r_ref[...], axis=1, keepdims=True), (32, 128))

    out = pl.pallas_call(
        body, name=f"bias_grad{gi}{'ab'[sub]}", grid=(NH,),
        in_specs=[pl.BlockSpec((None, T, Tk), lambda h: (h, 0, 0)), pl.BlockSpec((T, Tk), lambda h: (0, 0))] + dep_specs,
        out_specs=pl.BlockSpec((None, 32, 128), lambda h: (h, 0, 0)),
        out_shape=_sds((NH, 32, 128), F32), scratch_shapes=[pltpu.VMEM((32, Tk), F32)],
        compiler_params=_cp(("parallel",)),
    )(dsum, idx, *dep_args)
    return out[:, :, 0]


def _prev_mask(n, T, Tk):
    col = lax.broadcasted_iota(jnp.int32, (1, Tk), 1)
    return jnp.where((col < T) & (n == 0), NEG, 0.0).astype(F32)


def _split_rows(v, R):
    A, half = v.shape[0] // R, R // 2
    via_f32 = v.dtype != F32 and half % 16 != 0
    w = v.astype(F32) if via_f32 else v
    lo = jnp.concatenate([w[a * R:a * R + half] for a in range(A)], axis=0)
    hi = jnp.concatenate([w[a * R + half:(a + 1) * R] for a in range(A)], axis=0)
    return (lo.astype(v.dtype), hi.astype(v.dtype)) if via_f32 else (lo, hi)


def _merge_rows(lo, hi, R):
    half = R // 2
    A = lo.shape[0] // half
    return jnp.concatenate([x[a * half:(a + 1) * half] for a in range(A) for x in (lo, hi)], axis=0)


def _attn_fwd(qkv, biases, geom, others=None):
    S = qkv.shape[0]
    gi, T, R, has_prev = geom.gi, geom.T, geom.rows, geom.has_prev
    H2 = T // 2
    qv = geom.view(qkv)
    cq, ck, cv = 3 * gi, 3 * gi + 1, 3 * gi + 2
    assert others is None or not has_prev

    def body(*refs):
        if has_prev:
            q_ref, kp_ref, kc_ref, vp_ref, vc_ref, ba_ref, bb_ref, o_ref, l_ref = refs
        elif others is not None:
            (q_ref, kc_ref, vc_ref, ba_ref, bb_ref, o0_ref, o1_ref, l0_ref, l1_ref, z_ref,
             attn_ref, L_ref, ag_ref) = refs
        else:
            q_ref, kc_ref, vc_ref, ba_ref, bb_ref, o_ref, l_ref = refs
        q_lo, q_hi = _split_rows(_ld(q_ref), R)
        kc_lo, kc_hi = _split_rows(_ld(kc_ref), R)
        vc_lo, vc_hi = _split_rows(_ld(vc_ref), R)
        k_b = jnp.concatenate([kc_lo, kc_hi], axis=0)
        v_b = jnp.concatenate([vc_lo, vc_hi], axis=0)
        if has_prev:
            n = pl.program_id(len(geom.outer))
            k_a = jnp.concatenate([_split_rows(_ld(kp_ref), R)[1], kc_lo], axis=0)
            v_a = jnp.concatenate([_split_rows(_ld(vp_ref), R)[1], vc_lo], axis=0)
            pm = _prev_mask(n, H2, T)
        else:
            k_a, v_a, pm = k_b, v_b, None
        low = lax.broadcasted_iota(jnp.int32, (1, 2 * HD), 1) < HD
        for j in range(NH // 2):
            cols = slice(2 * HD * j, 2 * HD * (j + 1))
            o_half, l_half = [], []
            for q, k, v, b_ref, mask in ((q_lo, k_a, v_a, ba_ref, pm), (q_hi, k_b, v_b, bb_ref, None)):
                q2 = q[:, cols].astype(F32) * 0.125
                k2, v2 = k[:, cols], v[:, cols]
                qs = jnp.concatenate([jnp.where(low, q2, 0.0), jnp.where(low, 0.0, q2)], axis=0).astype(q.dtype)
                s = _dot(qs, k2, NT) + b_ref[j]
                if mask is not None:
                    s = s + mask
                m = jnp.max(s, axis=1, keepdims=True)
                p = jnp.exp(s - m)
                l = jnp.sum(p, axis=1, keepdims=True)
                pv = _dot(p.astype(v.dtype), v2, NN) / l
                lse = m + jnp.log(l)
                o_half.append(jnp.where(low, pv[:H2], pv[H2:]))
                l_half.append(jnp.where(low, lse[:H2], lse[H2:]))
            o2, lc = _merge_rows(o_half[0], o_half[1], R), _merge_rows(l_half[0], l_half[1], R)
            if others is None:
                _st(o_ref, o2, cols)
                _st(l_ref, lc, cols)
                continue
            la, lb = l0_ref[:, cols], l1_ref[:, cols]
            m = jnp.maximum(jnp.maximum(la, lb), lc)
            ea, eb, ec = jnp.exp(la - m), jnp.exp(lb - m), jnp.exp(lc - m)
            den = ea + eb + ec
            attn = (ea * o0_ref[:, cols] + eb * o1_ref[:, cols] + ec * o2) / den
            attn_ref[:, cols] = attn
            L_ref[:, cols] = m + jnp.log(den)
            z = z_ref[:, cols]
            ag = attn * (z * _sigmoid(z))
            ag_ref[:, cols] = ag.astype(ag_ref.dtype)

    cur = lambda n: n
    prev = lambda n: jnp.maximum(n - 1, 0)
    if has_prev:
        in_specs = [geom.spec(AW, cq, cur), geom.spec(AW, ck, prev), geom.spec(AW, ck, cur),
                    geom.spec(AW, cv, prev), geom.spec(AW, cv, cur)]
        args = [qv] * 5
    else:
        in_specs = [geom.spec(AW, cq, cur), geom.spec(AW, ck, cur), geom.spec(AW, cv, cur)]
        args = [qv] * 3
    nd = len(geom.outer) + (1 if has_prev else 0)
    biases = [b.reshape(NH // 2, 2 * H2, b.shape[2]) for b in biases]
    in_specs += [pl.BlockSpec(b.shape, lambda *ids: (0, 0, 0)) for b in biases]
    grid = geom.outer + ((geom.n_tiles,) if has_prev else ())
    oshape = geom.view_shape(AW)
    tile = geom.spec(AW, 0, cur)
    if others is None:
        o, l = pl.pallas_call(
            body, name=f"attn_fwd{gi}", grid=grid, in_specs=in_specs, out_specs=[tile, tile],
            out_shape=[_sds(oshape, F32), _sds(oshape, F32)],
            compiler_params=_cp(("arbitrary",) * nd),
        )(*args, *biases)
        return geom.unview(o), geom.unview(l)
    os_, ls_, rest = others
    attn, Lb, ag = pl.pallas_call(
        body, name=f"attn_fwd{gi}_merge", grid=grid, in_specs=in_specs + [tile] * 5,
        out_specs=[tile, tile, tile],
        out_shape=[_sds(oshape, F32), _sds(oshape, F32), _sds(oshape, CDT)],
        compiler_params=_cp(("arbitrary",) * nd),
    )(*args, *biases, *[geom.view(a) for a in (*os_, *ls_, rest)])
    return geom.unview(attn), geom.unview(Lb), geom.unview(ag)


def _pool_counts(r, LS, win):
    l = lax.broadcasted_iota(jnp.int32, (LS, 1), 0)
    return jnp.minimum(16 * l + (r + 1), win).astype(F32)


def _pool_fwd(rest, pool_w, pool_scale):
    S = rest.shape[0]
    LS = S // R16
    r3 = rest.reshape(R16, LS, REST_W)
    PG = 128

    def body(u_ref, z_ref, pw_ref, ps_ref, pt_ref, mp_ref, pg_ref, e_ref):
        g = pl.program_id(0)
        win = jnp.left_shift(2, g)
        row = lax.broadcasted_iota(jnp.int32, (LS, PG), 0)
        for r in range(R16):
            e_ref[R16 + r] = u_ref[r]
        for r in range(1, R16):
            e_ref[r] = jnp.where(row == 0, 0.0, pltpu.roll(u_ref[r], 1, 0))

        def double(s):
            for i in range(2 * R16 - 1, 2 * s - 1, -1):
                e_ref[i] = e_ref[i] + e_ref[i - s]

        double(1)
        for k in (1, 2, 3):
            pl.when(g >= k)(functools.partial(double, 2 ** k))
        pw = pw_ref[...].astype(CDT)
        for r in range(R16):
            u = u_ref[r]
            pooled = e_ref[R16 + r] / _pool_counts(r, LS, win) - u
            pc = pooled.astype(CDT)
            mp = _dot(pc, pw, NN)
            z = z_ref[r]
            pg = (mp * ps_ref[...]) * (z * _sigmoid(z))
            mp_ref[r] = mp
            pg_ref[r] = pg.astype(pg_ref.dtype)
            pt_ref[:, r * LS:(r + 1) * LS] = pooled.T.astype(pt_ref.dtype)

    col = lambda off: pl.BlockSpec((R16, LS, PG), lambda g: (0, 0, off + g))
    tsp = pl.BlockSpec((PG, S), lambda g: (g, 0))
    pt, mp, pg = pl.pallas_call(
        body, name="pool_fwd", grid=(4,),
        in_specs=[col(24), col(4), pl.BlockSpec((None, PG, PG), lambda g: (g, 0, 0)),
                  pl.BlockSpec((1, PG), lambda g: (0, g))],
        out_specs=[tsp, col(0), col(0)],
        out_shape=[_in_hbm((AW, S), CDT), _sds((R16, LS, AW), F32), _in_hbm((R16, LS, AW), CDT)],
        scratch_shapes=[pltpu.VMEM((2 * R16, LS, PG), F32)],
        compiler_params=_cp(("parallel",)),
    )(r3, r3, pool_w, pool_scale)
    return pt, mp.reshape(S, AW), pg.reshape(S, AW)


def _merge_out_loss(ag, pg, rest, wab, wpb, wout, x, target, gate, final_g):
    S = x.shape[0]
    LS = S // R16

    def body(ag_ref, pg_ref, ga_ref, gp_ref, wa_ref, wp_ref, w_ref, x_ref, t_ref, gate_ref, fg_ref,
             dx2_ref, dmo_ref, loss_ref, gfg_ref, dgate_ref, dw_ref):
        i = pl.program_id(0)

        @pl.when(i == 0)
        def _():
            loss_ref[...] = jnp.zeros_like(loss_ref)
            gfg_ref[...] = jnp.zeros_like(gfg_ref)
            dgate_ref[...] = jnp.zeros_like(dgate_ref)
            dw_ref[...] = jnp.zeros_like(dw_ref)

        swap = _swap16()
        gate, fg = gate_ref[...], fg_ref[...]
        loss, gfg, dgate, dw = 0.0, 0.0, 0.0, 0.0
        for s in range(TILES):
            sl = lambda ref: jnp.concatenate([ref[a, R16 * s:R16 * (s + 1), :] for a in range(R16)], axis=0)
            rows = slice(TOK * s, TOK * (s + 1))
            ya = _dot(sl(ag_ref), wa_ref[...], NN)
            yp = _dot(sl(pg_ref), wp_ref[...], NN)
            mg = (_sigmoid(sl(ga_ref)) * ya + _sigmoid(sl(gp_ref)) * yp).astype(CDT)
            m_nat = _dot(swap, mg, NN)
            mo = _dot(m_nat.astype(CDT), w_ref[...], NN)
            x2 = x_ref[rows, :] + gate * mo
            r2 = lax.rsqrt(jnp.mean(x2 * x2, axis=-1, keepdims=True) + EPS)
            xn2 = x2 * r2
            diff = xn2 * fg - t_ref[rows, :]
            loss = loss + jnp.sum(jnp.sum(diff * diff, axis=-1, keepdims=True), axis=0, keepdims=True)
            dy = diff * (1.0 / D)
            dxn2 = dy * fg
            dx2 = r2 * (dxn2 - xn2 * jnp.mean(dxn2 * xn2, axis=-1, keepdims=True))
            dx2_ref[rows, :] = dx2
            dmo = (dx2 * gate).astype(CDT)
            dmo_slab = _dot(swap, dmo, NN).astype(CDT)
            for a in range(R16):
                dmo_ref[a, R16 * s:R16 * (s + 1), :] = dmo_slab[R16 * a:R16 * (a + 1)]
            gfg = gfg + jnp.sum(dy * xn2, axis=0, keepdims=True)
            dgate = dgate + jnp.sum(dx2 * mo, axis=0, keepdims=True)
            dw = dw + _dot(m_nat.T.astype(CDT), dmo, NN)
        loss_ref[...] += jnp.broadcast_to(loss * (0.5 / D), loss_ref.shape)
        gfg_ref[...] += gfg
        dgate_ref[...] += dgate
        dw_ref[...] += dw

    def slab(w, cblk):
        return pl.BlockSpec((R16, TILES * R16, w), lambda i: (0, i, cblk))

    nat = pl.BlockSpec((TILES * TOK, D), lambda i: (i, 0))
    vec = pl.BlockSpec((1, D), lambda i: (0, 0))
    r3 = rest.reshape(R16, LS, REST_W)
    return pl.pallas_call(
        body, name="merge_out_loss", grid=(S // (TILES * TOK),),
        in_specs=[slab(AW, 0), slab(AW, 0), slab(D, 1), slab(D, 2), _resident((AW, D)), _resident((AW, D)),
                  _resident((D, D)), nat, nat, vec, vec],
        out_specs=[nat, slab(D, 0), pl.BlockSpec((1, 128), lambda i: (0, 0)), vec, vec,
                   pl.BlockSpec((D, D), lambda i: (0, 0))],
        out_shape=[_sds((S, D), F32), _sds((R16, LS, D), CDT), _in_hbm((1, 128), F32), _in_hbm((1, D), F32),
                   _in_hbm((1, D), F32), _sds((D, D), F32)],
        compiler_params=_cp(("arbitrary",)),
    )(ag.reshape(R16, LS, AW), pg.reshape(R16, LS, AW), r3, r3, wab, wpb, wout, x, target, gate, final_g)


def _bwd_gates(dmo, wout, wab, wpb, rest, ag, pg, attn, mp, pool_scale):
    S = dmo.shape[0]
    tm = 256

    def body(dmo_ref, wo_ref, wa_ref, wp_ref, za_ref, zp_ref, ga0, ga1, gp0, gp1, ag_ref, pg_ref, at_ref, mp_ref,
             ps_ref, dwa_ref, dwp_ref, dr_ref, dat_ref, dl_ref, dmp_ref, dps_ref):
        i = pl.program_id(0)

        @pl.when(i == 0)
        def _():
            dps_ref[...] = jnp.zeros_like(dps_ref)
            dwa_ref[...] = jnp.zeros_like(dwa_ref)
            dwp_ref[...] = jnp.zeros_like(dwp_ref)

        dm = _dot(dmo_ref[...], wo_ref[...], NT)
        sa = _sigmoid(jnp.concatenate([ga0[...], ga1[...]], axis=1))
        sp = _sigmoid(jnp.concatenate([gp0[...], gp1[...]], axis=1))
        dya = (dm * sa).astype(CDT)
        dyp = (dm * sp).astype(CDT)
        ag, pg = ag_ref[...], pg_ref[...]
        dwa_ref[...] += _dot(ag.astype(F32).T.astype(CDT), dya, NN)
        dwp_ref[...] += _dot(pg.astype(F32).T.astype(CDT), dyp, NN)
        ya = _dot(ag, wa_ref[...], NN)
        yp = _dot(pg, wp_ref[...], NN)
        dr_ref[:, 1024:2048] = (dm * ya * sa * (1.0 - sa)).astype(dr_ref.dtype)
        dr_ref[:, 2048:3072] = (dm * yp * sp * (1.0 - sp)).astype(dr_ref.dtype)
        da = _dot(dya, wa_ref[...], NT)
        dp = _dot(dyp, wp_ref[...], NT)
        za = za_ref[...]
        sga = _sigmoid(za)
        attn = at_ref[...]
        dattn = da * (za * sga)
        dr_ref[:, 0:512] = (da * attn * (sga * (1.0 + za * (1.0 - sga)))).astype(dr_ref.dtype)
        dat_ref[...] = dattn.astype(dat_ref.dtype)
        prod = dattn * attn
        low = lax.broadcasted_iota(jnp.int32, (1, 2 * HD), 1) < HD
        for j in range(NH // 2):
            x2 = prod[:, 2 * HD * j:2 * HD * (j + 1)]
            s0 = jnp.sum(jnp.where(low, x2, 0.0), axis=1, keepdims=True)
            s1 = jnp.sum(jnp.where(low, 0.0, x2), axis=1, keepdims=True)
            dl_ref[:, 2 * HD * j:2 * HD * (j + 1)] = jnp.where(low, s0, s1)
        zp = zp_ref[...]
        sgp = _sigmoid(zp)
        mpre = mp_ref[...]
        ps = ps_ref[...]
        dmixed = dp * (zp * sgp)
        dr_ref[:, 512:1024] = (dp * (mpre * ps) * (sgp * (1.0 + zp * (1.0 - sgp)))).astype(dr_ref.dtype)
        dps_ref[...] += jnp.sum(dmixed * mpre, axis=0, keepdims=True)
        dmp_ref[...] = (dmixed * ps).astype(dmp_ref.dtype)

    full = lambda shape: pl.BlockSpec(shape, lambda i: (0, 0))
    tD = pl.BlockSpec((tm, D), lambda i: (i, 0))
    tA = pl.BlockSpec((tm, AW), lambda i: (i, 0))
    rb = lambda c: pl.BlockSpec((tm, AW), lambda i: (i, c))
    return pl.pallas_call(
        body, name="bwd_gates", grid=(S // tm,),
        in_specs=[tD, full((D, D)), full((AW, D)), full((AW, D)), rb(0), rb(1), rb(2), rb(3), rb(4), rb(5),
                  tA, tA, tA, tA, full((1, AW))],
        out_specs=[full((AW, D)), full((AW, D)), pl.BlockSpec((tm, 3072), lambda i: (i, 0)), tA, tA, tA, full((1, AW))],
        out_shape=[_sds((AW, D), F32), _sds((AW, D), F32), _sds((S, REST_W), CDT), _sds((S, AW), CDT),
                   _sds((S, AW), F32), _sds((S, AW), CDT), _sds((1, AW), F32)],
        compiler_params=_cp(("arbitrary",), vmem=56),
    )(dmo, wout, wab, wpb, rest, rest, rest, rest, rest, rest, ag, pg, attn, mp, pool_scale)


def _pool_bwd(dmp, pt, pool_w, drest):
    S = dmp.shape[0]
    LS = S // R16
    PG = 128
    d3 = dmp.reshape(R16, LS, AW)
    dr3 = drest.reshape(R16, LS, REST_W)

    def body(d_ref, pt_ref, pw_ref, dr_in, du_ref, dpw_ref, g_ref, s_ref):
        del dr_in
        g = pl.program_id(0)
        win = jnp.left_shift(2, g)
        row = lax.broadcasted_iota(jnp.int32, (LS, PG), 0)
        pw = pw_ref[...].astype(CDT)
        dpw = jnp.zeros((PG, PG), F32)
        for r in range(R16):
            dm = d_ref[r]
            dpw = dpw + _dot(pt_ref[:, r * LS:(r + 1) * LS], dm, NN)
            dpooled = _dot(dm, pw, NT)
            g_ref[r] = dpooled
            sc = dpooled / _pool_counts(r, LS, win)
            s_ref[r] = sc
            if r < R16 - 1:
                s_ref[R16 + r] = jnp.where(row == LS - 1, 0.0, pltpu.roll(sc, LS - 1, 0))
        dpw_ref[...] = dpw

        def double(s):
            for i in range(0, 2 * R16 - 2 * s):
                s_ref[i] = s_ref[i] + s_ref[i + s]

        double(1)
        for k in (1, 2, 3):
            pl.when(g >= k)(functools.partial(double, 2 ** k))
        for r in range(R16):
            du_ref[r] = (s_ref[r] - g_ref[r]).astype(du_ref.dtype)

    du, dpw = pl.pallas_call(
        body, name="pool_bwd", grid=(4,),
        in_specs=[pl.BlockSpec((R16, LS, PG), lambda g: (0, 0, g)), pl.BlockSpec((PG, S), lambda g: (g, 0)),
                  pl.BlockSpec((None, PG, PG), lambda g: (g, 0, 0)), pl.BlockSpec(memory_space=pl.ANY)],
        out_specs=[pl.BlockSpec((R16, LS, PG), lambda g: (0, 0, 24 + g)),
                   pl.BlockSpec((None, PG, PG), lambda g: (g, 0, 0))],
        out_shape=[_sds((R16, LS, REST_W), CDT), _sds((4, PG, PG), F32)],
        scratch_shapes=[pltpu.VMEM((R16, LS, PG), F32), pltpu.VMEM((2 * R16, LS, PG), F32)],
        input_output_aliases={3: 0},
        compiler_params=_cp(("parallel",)),
    )(d3, pt, pool_w, dr3)
    return du.reshape(S, REST_W), dpw


def _attn_bwd(qkv, dattn, Lb, dlb, biases, geom, dqkv_prev):
    S = qkv.shape[0]
    gi, T, R, has_prev, N = geom.gi, geom.T, geom.rows, geom.has_prev, geom.n_tiles
    H2 = T // 2
    nd_outer = len(geom.outer)
    qv = geom.view(qkv)
    cq, ck, cv = 3 * gi, 3 * gi + 1, 3 * gi + 2
    W3 = 3 * AW

    def heads(q_ref, kp_ref, kc_ref, vp_ref, vc_ref, do_ref, L_ref, dl_ref, ba_ref, bb_ref, pm, dsa_ref, dsb_ref, emit):
        q_lo, q_hi = _split_rows(_ld(q_ref), R)
        do_lo, do_hi = _split_rows(_ld(do_ref), R)
        L_lo, L_hi = _split_rows(_ld(L_ref), R)
        dl_lo, dl_hi = _split_rows(_ld(dl_ref), R)
        kc_lo, kc_hi = _split_rows(_ld(kc_ref), R)
        vc_lo, vc_hi = _split_rows(_ld(vc_ref), R)
        k_b = jnp.concatenate([kc_lo, kc_hi], axis=0)
        v_b = jnp.concatenate([vc_lo, vc_hi], axis=0)
        if has_prev:
            k_a = jnp.concatenate([_split_rows(_ld(kp_ref), R)[1], kc_lo], axis=0)
            v_a = jnp.concatenate([_split_rows(_ld(vp_ref), R)[1], vc_lo], axis=0)
        else:
            k_a, v_a = k_b, v_b
        low = lax.broadcasted_iota(jnp.int32, (1, 2 * HD), 1) < HD
        for j in range(NH // 2):
            sl = slice(2 * HD * j, 2 * HD * (j + 1))
            res = []
            for q, k, v, do, L, dl, b_ref, mask, ds_ref in (
                    (q_lo, k_a, v_a, do_lo, L_lo, dl_lo, ba_ref, pm, dsa_ref),
                    (q_hi, k_b, v_b, do_hi, L_hi, dl_hi, bb_ref, None, dsb_ref)):
                q2 = q[:, sl].astype(F32) * 0.125
                do2 = do[:, sl].astype(F32)
                k2, v2 = k[:, sl], v[:, sl]
                L2, dl2 = L[:, sl], dl[:, sl]
                qs = jnp.concatenate([jnp.where(low, q2, 0.0), jnp.where(low, 0.0, q2)], axis=0).astype(q.dtype)
                dom = jnp.concatenate([jnp.where(low, do2, 0.0), jnp.where(low, 0.0, do2)], axis=0).astype(q.dtype)
                Lc = jnp.concatenate([L2[:, 0:1], L2[:, HD:HD + 1]], axis=0)
                dlc = jnp.concatenate([dl2[:, 0:1], dl2[:, HD:HD + 1]], axis=0)
                s = _dot(qs, k2, NT) + b_ref[j]
                if mask is not None:
                    s = s + mask
                p = jnp.exp(s - Lc)
                ds = p * (_dot(dom, v2, NT) - dlc)
                ds_ref[j] += ds
                dsc = ds.astype(q.dtype)
                dq_st = _dot(dsc, k2, NN)
                res.append((jnp.where(low, dq_st[:H2], dq_st[H2:]) * 0.125, _dot(dsc, qs, TN),
                            _dot(p.astype(q.dtype), dom, TN)))
            (dq_lo, dk_a, dv_a), (dq_hi, dk_b, dv_b) = res
            dq = _merge_rows(dq_lo, dq_hi, R)
            if has_prev:
                emit(sl, dq, _merge_rows(dk_a[H2:] + dk_b[:H2], dk_b[H2:], R),
                     _merge_rows(dv_a[H2:] + dv_b[:H2], dv_b[H2:], R), dk_a[:H2], dv_a[:H2])
            else:
                emit(sl, dq, _merge_rows(dk_a[:H2] + dk_b[:H2], dk_a[H2:] + dk_b[H2:], R),
                     _merge_rows(dv_a[:H2] + dv_b[:H2], dv_a[H2:] + dv_b[H2:], R), None, None)

    def first_step():
        ok = pl.program_id(nd_outer) == 0 if has_prev else pl.program_id(0) == 0
        for a in range(nd_outer if has_prev else 0):
            ok = ok & (pl.program_id(a) == 0)
        return ok

    if has_prev:
        def body(q_ref, kp_ref, kc_ref, vp_ref, vc_ref, do_ref, L_ref, dl_ref, ba_ref, bb_ref, prev_in, out_ref,
                 dsa_ref, dsb_ref, hold, new, pp):
            del prev_in
            s_id = pl.program_id(nd_outer)

            @pl.when(first_step())
            def _():
                dsa_ref[...] = jnp.zeros_like(dsa_ref)
                dsb_ref[...] = jnp.zeros_like(dsb_ref)

            @pl.when(s_id < N)
            def _():
                def emit(sl, dq, dk, dv, dk_prev, dv_prev):
                    new[:, sl] = dq
                    new[:, slice(AW + sl.start, AW + sl.stop)] = dk
                    new[:, slice(2 * AW + sl.start, 2 * AW + sl.stop)] = dv
                    pp[:, sl] = dk_prev
                    pp[:, slice(AW + sl.start, AW + sl.stop)] = dv_prev

                heads(q_ref, kp_ref, kc_ref, vp_ref, vc_ref, do_ref, L_ref, dl_ref, ba_ref, bb_ref,
                      _prev_mask(s_id, H2, T), dsa_ref, dsb_ref, emit)

            @pl.when((s_id >= 1) & (s_id < N))
            def _():
                _st(out_ref, hold[:, 0:AW], slice(0, AW))
                late = _merge_rows(jnp.zeros((H2, 2 * AW), F32), pp[...], R)
                _st(out_ref, hold[:, AW:W3] + late, slice(AW, W3))

            @pl.when(s_id == N)
            def _():
                _st(out_ref, hold[...])

            @pl.when(s_id < N)
            def _():
                hold[...] = new[...]

        cur = lambda n: jnp.minimum(n, N - 1)
        prev = lambda n: jnp.clip(n - 1, 0, N - 1)
        delayed = lambda n: jnp.maximum(n - 1, 0)
        in_specs = [geom.spec(AW, cq, cur), geom.spec(AW, ck, prev), geom.spec(AW, ck, cur),
                    geom.spec(AW, cv, prev), geom.spec(AW, cv, cur),
                    geom.spec(AW, 0, cur), geom.spec(AW, 0, cur), geom.spec(AW, 0, cur)]
        args = [qv] * 5 + [geom.view(dattn), geom.view(Lb), geom.view(dlb)]
        out_spec = geom.spec(W3, gi, delayed)
        grid = geom.outer + (N + 1,)
        scratch = [pltpu.VMEM((T, W3), F32), pltpu.VMEM((T, W3), F32), pltpu.VMEM((H2, 2 * AW), F32)]
    else:
        def body(q_ref, kc_ref, vc_ref, do_ref, L_ref, dl_ref, ba_ref, bb_ref, prev_in, out_ref, dsa_ref, dsb_ref):
            del prev_in

            @pl.when(first_step())
            def _():
                dsa_ref[...] = jnp.zeros_like(dsa_ref)
                dsb_ref[...] = jnp.zeros_like(dsb_ref)

            def emit(sl, dq, dk, dv, dk_prev, dv_prev):
                _st(out_ref, dq, sl)
                _st(out_ref, dk, slice(AW + sl.start, AW + sl.stop))
                _st(out_ref, dv, slice(2 * AW + sl.start, 2 * AW + sl.stop))

            heads(q_ref, None, kc_ref, None, vc_ref, do_ref, L_ref, dl_ref, ba_ref, bb_ref, None, dsa_ref, dsb_ref,
                  emit)

        cur = lambda n: n
        in_specs = [geom.spec(AW, cq, cur), geom.spec(AW, ck, cur), geom.spec(AW, cv, cur),
                    geom.spec(AW, 0, cur), geom.spec(AW, 0, cur), geom.spec(AW, 0, cur)]
        args = [qv] * 3 + [geom.view(dattn), geom.view(Lb), geom.view(dlb)]
        out_spec = geom.spec(W3, gi, cur)
        grid = geom.outer
        scratch = []

    whole = lambda b: pl.BlockSpec(b.shape, lambda *ids: (0, 0, 0))
    biases = [b.reshape(NH // 2, 2 * H2, b.shape[2]) for b in biases]
    in_specs += [whole(b) for b in biases]
    in_specs.append(pl.BlockSpec(memory_space=pl.ANY))
    if dqkv_prev is None:
        last, aliases = biases[0], {}
    else:
        last, aliases = geom.view(dqkv_prev), {len(in_specs) - 1: 0}
    dq, dsa, dsb = pl.pallas_call(
        body, name=f"attn_bwd{gi}", grid=grid, in_specs=in_specs,
        out_specs=[out_spec] + [whole(b) for b in biases],
        out_shape=[_sds(geom.view_shape(QKV_W), CDT)] + [_sds(b.shape, F32) for b in biases],
        scratch_shapes=scratch,
        input_output_aliases=aliases,
        compiler_params=_cp(("arbitrary",) * len(grid), vmem=56),
    )(*args, *biases, last)
    return geom.unview(dq), [d.reshape(NH, H2, d.shape[2]) for d in (dsa, dsb)]


def _local_step(x, target, shift, scale, gate, norm_g, w_in_full, pool_w, pool_scale, wab, wpb, wout, rel_bias,
                final_g):
    st = _local_bias_tables(_local_pre(x, shift, scale, norm_g), rel_bias)
    st = _local_grads(st, target, gate, w_in_full, pool_w, pool_scale, wab, wpb, wout, final_g)
    d_rel_bias = _local_bias_grads(st, _BIAS_GRADS_EARLY + _BIAS_GRADS_LATE)
    gx, dmod, dng = _local_input_grads(st, w_in_full, norm_g, scale)
    return dict(loss=st["loss"], grad_x=gx, dw_in=st["dw_in"], dw_ab=st["dw_ab"], dw_pb=st["dw_pb"],
                dw_out=st["dw_out"], d_norm_g=dng, d_pool_w=st["dpw"], d_pool_scale=st["dps"],
                d_rel_bias=d_rel_bias.T, d_final_g=st["gfg"], dmod=dmod)


def _local_pre(x, shift, scale, norm_g, dep=None):
    S = x.shape[0]
    h = _norm_mod(x, norm_g, shift, scale, dep).reshape(S, D)
    return dict(x=x, h=h, ht=_tcast(h, "h_transpose"))


def _local_bias_tables(st, rel_bias, dep=None):
    LS = st["x"].shape[0] // R16
    geoms = [_Geom(gi, LS) for gi in range(3)]
    idx_np = [[g.bucket_index(sub) for sub in range(2)] for g in geoms]
    buckets = [[tuple(int(b) for b in np.unique(ix) if b >= 0) for ix in pair] for pair in idx_np]
    idxs = [[jnp.asarray(ix) for ix in pair] for pair in idx_np]
    biases = [[_bias_table(idxs[gi][sub], buckets[gi][sub], rel_bias, gi, sub, dep) for sub in range(2)]
              for gi in range(3)]
    return dict(st, geoms=geoms, idxs=idxs, buckets=buckets, biases=biases)


def _local_grads(st, target, gate, w_in_full, pool_w, pool_scale, wab, wpb, wout, final_g):
    geoms, idxs, biases, x, h, ht = (st[k] for k in ("geoms", "idxs", "biases", "x", "h", "ht"))
    S = h.shape[0]
    LS = S // R16
    qkv, rest = _proj(h, w_in_full)

    os_, ls_ = [], []
    for gi in range(2):
        o, l = _attn_fwd(qkv, biases[gi], geoms[gi])
        os_.append(o)
        ls_.append(l)
    attn, Lb, ag = _attn_fwd(qkv, biases[2], geoms[2], others=(os_, ls_, rest))
    pt, mp, pg = _pool_fwd(rest, pool_w, pool_scale)
    dx2, dmo3, loss, gfg, dgate, dw_out = _merge_out_loss(ag, pg, rest, wab, wpb, wout, x, target, gate, final_g)
    dmo = dmo3.reshape(S, D)

    dw_ab, dw_pb, drest, dattn, dlb, dmp, dps = _bwd_gates(dmo, wout, wab, wpb, rest, ag, pg, attn, mp, pool_scale)
    drest, dpw = _pool_bwd(dmp, pt, pool_w, drest)

    dqkv = None
    dsums = []
    for gi in range(3):
        dqkv, dsum = _attn_bwd(qkv, dattn, Lb, dlb, biases[gi], geoms[gi], dqkv)
        dsums.append(dsum)

    dw_in = _mm_tn(ht, dqkv, n_total=8192, out_blk=lambda j: j, name="dw_in_qkv", with_low=True)
    dw_in, dw_in_low = _mm_tn(ht, drest, n_total=8192, out_blk=_rest_blk, name="dw_in_rest", prev=dw_in, with_low=True)
    return dict(st, loss=loss[0, 0], dw_in=dw_in, dw_in_low=dw_in_low, dw_ab=dw_ab, dw_pb=dw_pb, dw_out=dw_out, dpw=dpw, dps=dps, gfg=gfg,
                dgate=dgate, dsums=dsums, dqkv=dqkv, drest=drest, dx2=dx2)


def _local_bias_grads(st, which, dep=None):
    rows = []
    for gi in range(3):
        got = [_bias_grad(st["dsums"][gi][sub], st["idxs"][gi][sub], st["buckets"][gi][sub], gi, sub, dep)
               for sub in range(2) if (gi, sub) in which]
        rows.append(sum(got) if got else jnp.zeros((NH, 32), F32))
    return jnp.concatenate(rows, axis=0)


_BIAS_GRADS_EARLY = ((0, 0), (0, 1), (1, 0), (2, 0))
_BIAS_GRADS_LATE = ((1, 1), (2, 1))


def _dh_h_bwd(dqkv, drest, w, dx2, x, norm_g, scale, dep=None):
    S = x.shape[0]
    dep_specs, dep_args = _after(dep)

    def body(a_ref, b_ref, w_ref, dx2_ref, x_ref, g_ref, sc_ref, *rest):
        gx_ref, dsh_ref, dsc_ref, dng_ref, nat_ref = rest[-5:]
        i = pl.program_id(0)

        @pl.when(i == 0)
        def _():
            dsh_ref[...] = jnp.zeros_like(dsh_ref)
            dsc_ref[...] = jnp.zeros_like(dsc_ref)
            dng_ref[...] = jnp.zeros_like(dng_ref)

        a, b = _ld(a_ref), _ld(b_ref)
        acc = _dot(a[:, 0:512], _w512(w_ref, 0), NT)
        for j in range(1, 9):
            acc = acc + _dot(a[:, 512 * j:512 * (j + 1)], _w512(w_ref, j), NT)
        off = 0
        for lo, hi in REST_COLS:
            for j in range(lo, hi):
                acc = acc + _dot(b[:, off:off + 512], _w512(w_ref, j), NT)
                off += 512
        for r in range(R16):
            nat_ref[:, r, :] = acc[R16 * r:R16 * (r + 1)]
        dh = jnp.concatenate([nat_ref[l] for l in range(R16)], axis=0)
        g = g_ref[...]
        one_sc = 1.0 + sc_ref[...]
        xv = x_ref[...]
        r1 = lax.rsqrt(jnp.mean(xv * xv, axis=-1, keepdims=True) + EPS)
        xn = xv * r1
        dhx = dh * xn
        dxn = dh * (g * one_sc)
        dx1 = r1 * (dxn - xn * jnp.mean(dxn * xn, axis=-1, keepdims=True))
        gx_ref[...] = dx2_ref[...] + dx1
        dsh_ref[...] += jnp.sum(dh, axis=0, keepdims=True)
        dsc_ref[...] += jnp.sum(dhx * g, axis=0, keepdims=True)
        dng_ref[...] += jnp.sum(dhx * one_sc, axis=0, keepdims=True)

    nat = pl.BlockSpec((TOK, D), lambda i: (i, 0))
    vec = pl.BlockSpec((1, D), lambda i: (0, 0))
    slab = lambda W: pl.BlockSpec((R16, R16, W), lambda i: (0, i, 0))
    return pl.pallas_call(
        body, name="dh_h_bwd", grid=(S // TOK,),
        in_specs=[slab(QKV_W), slab(REST_W), _resident((8, D, 1024)), nat, nat, vec, vec] + dep_specs,
        out_specs=[nat, vec, vec, vec],
        out_shape=[_sds((S, D), F32), _sds((1, D), F32), _sds((1, D), F32), _sds((1, D), F32)],
        scratch_shapes=[pltpu.VMEM((R16, R16, D), F32)],
        compiler_params=_cp(("arbitrary",), vmem=56),
    )(dqkv, drest, w, dx2, x, norm_g, scale, *dep_args)


def _local_input_grads(st, w_in_full, norm_g, scale, dep=None):
    S = st["h"].shape[0]
    LS = S // R16
    gx, dsh, dsc, dng = _dh_h_bwd(st["dqkv"].reshape(R16, LS, QKV_W), st["drest"].reshape(R16, LS, REST_W),
                                  w_in_full, st["dx2"], st["x"], norm_g, scale, dep)
    return gx, jnp.concatenate([dsh, dsc, st["dgate"]], axis=1), dng


def _my_place():
    return lax.axis_index("x"), lax.axis_index("y"), lax.axis_index("c")


def _block_of(ref, axis, idx, width):
    if axis is None:
        return ref.at[idx]
    start = pl.multiple_of(idx * width, width)
    if axis == 0:
        return ref.at[pl.ds(start, width)]
    return ref.at[:, pl.ds(start, width)]


_SEM = pl.BlockSpec(memory_space=pltpu.SEMAPHORE)
_EFFECT = pltpu.SideEffectType.DATAFLOW_SIDE_EFFECTING


def _split_start(name, srcs, lands, ncopy, plan, dep=None):
    arrs = list(srcs) + list(lands)
    na, ns = len(arrs), len(srcs)
    dep_specs, dep_args = _after(dep)

    def body(*refs):
        send_sems, recv_sems, token = refs[na + len(dep_args)], refs[na + len(dep_args) + 1], refs[-1]
        for k, (s, d, dev) in enumerate(plan(refs[:ns], refs[ns:na])):
            pltpu.make_async_remote_copy(src_ref=s, dst_ref=d, send_sem=send_sems.at[k], recv_sem=recv_sems.at[k],
                                         device_id=dev, device_id_type=MESH).start()
        token[...] = jnp.zeros_like(token)

    outs = pl.pallas_call(
        body, name=name,
        out_shape=(pltpu.SemaphoreType.DMA((ncopy,)), pltpu.SemaphoreType.DMA((ncopy,)),
                   *[pltpu.HBM(a.shape, a.dtype) for a in arrs], _sds((8, 128), F32)),
        in_specs=[_HBM] * na + dep_specs,
        out_specs=(_SEM, _SEM, *[_HBM] * na, pl.BlockSpec(memory_space=pltpu.VMEM)),
        input_output_aliases={i: 2 + i for i in range(na)},
        compiler_params=pltpu.CompilerParams(has_side_effects=_EFFECT),
    )(*[pltpu.with_memory_space_constraint(a, pltpu.HBM) for a in arrs], *dep_args)
    return (outs[0], outs[1]), list(outs[2:2 + ns]), list(outs[2 + ns:2 + na]), outs[-1]


def _split_wait(name, sems, srcs, lands, plan, after):
    arrs = list(srcs) + list(lands)
    na, ns = len(arrs), len(srcs)
    after = list(after) if isinstance(after, (list, tuple)) else [after]

    def body(*refs):
        send_sems, recv_sems = refs[na], refs[na + 1]
        for k, (s, d, dev) in enumerate(plan(refs[:ns], refs[ns:na])):
            cp = pltpu.make_async_remote_copy(src_ref=s, dst_ref=d, send_sem=send_sems.at[k], recv_sem=recv_sems.at[k],
                                              device_id=dev, device_id_type=MESH)
            cp.wait_send()
            cp.wait_recv()

    outs = pl.pallas_call(
        body, name=name, out_shape=[pltpu.HBM(a.shape, a.dtype) for a in arrs],
        in_specs=[_HBM] * na + [_SEM, _SEM] + [pl.BlockSpec(memory_space=pl.ANY)] * len(after),
        out_specs=[_HBM] * na,
        input_output_aliases={i: i for i in range(na)},
        compiler_params=pltpu.CompilerParams(has_side_effects=_EFFECT),
    )(*arrs, sems[0], sems[1], *after)
    return list(outs[:ns]), list(outs[ns:])


def _gather_chips_plan(layout):
    def plan(src, land):
        x, y, c = _my_place()
        me = 4 * x + 2 * y + c
        mine = [_block_of(land[p], layout[p][0], me, layout[p][1]) for p in range(len(land))]
        return [(mine[p], mine[p], (cx, cy, c)) for p in range(len(land)) for cx, cy in [(1 - x, y), (x, 1 - y)]]
    return plan


def _pair_plan(layout):
    def plan(src, land):
        x, y, c = _my_place()
        n = len(layout)
        return [(_block_of(src[p], layout[p][0], 2 * q + (1 - c), layout[p][1]), land[p].at[q], (x, y, 1 - c))
                for p in range(n) for q in range(4)] + [(src[p], land[p], (x, y, 1 - c)) for p in range(n, len(src))]
    return plan


def _chips_plan(n):
    def plan(src, land):
        x, y, c = _my_place()
        chips = [(1 - x, y), (x, 1 - y), (1 - x, 1 - y)]
        return [(src[p].at[2 * cx + cy], land[p].at[2 * x + y], (cx, cy, c)) for p in range(n) for cx, cy in chips] + \
               [(src[p], land[p].at[2 * x + y], (cx, cy, c)) for p in range(n, len(src)) for cx, cy in chips]
    return plan


def _pair_share(gathered):
    n = len(gathered)

    def body(*refs):
        land = refs[n:2 * n]
        send_sems, recv_sems, fwd_send, fwd_recv = refs[2 * n:]
        x, y, c = _my_place()
        k_dg = 3 - (2 * x + y)
        south = c == 0
        pass_on = 4 * jnp.where(south, 1 - x, x) + 2 * jnp.where(south, y, 1 - y) + c
        to = (jnp.where(south, x, 1 - x), jnp.where(south, 1 - y, y), c)

        def forward(p):
            return pltpu.make_async_remote_copy(
                src_ref=land[p].at[pass_on], dst_ref=land[p].at[pass_on], send_sem=fwd_send.at[p],
                recv_sem=fwd_recv.at[p], device_id=to, device_id_type=MESH)

        def to_sibling(p, k):
            blk = land[p].at[2 * k + c]
            return pltpu.make_async_remote_copy(
                src_ref=blk, dst_ref=blk, send_sem=send_sems.at[4 * p + k], recv_sem=recv_sems.at[4 * p + k],
                device_id=(x, y, 1 - c), device_id_type=MESH)

        for p in range(n):
            forward(p).start()
        for p in range(n):
            for k in range(4):
                @pl.when(k != k_dg)
                def _():
                    to_sibling(p, k).start()
        for p in range(n):
            dg = land[p].at[2 * k_dg + c]
            pltpu.make_async_remote_copy(src_ref=dg, dst_ref=dg, send_sem=fwd_send.at[p], recv_sem=fwd_recv.at[p],
                                         device_id=to, device_id_type=MESH).wait_recv()
            for k in range(4):
                @pl.when(k == k_dg)
                def _():
                    to_sibling(p, k).start()
        for p in range(n):
            forward(p).wait_send()
            for k in range(4):
                to_sibling(p, k).wait_send()
                blk = land[p].at[2 * k + 1 - c]
                pltpu.make_async_remote_copy(
                    src_ref=blk, dst_ref=blk, send_sem=send_sems.at[4 * p + k], recv_sem=recv_sems.at[4 * p + k],
                    device_id=(x, y, 1 - c), device_id_type=MESH).wait_recv()

    return pl.pallas_call(
        body, name="gather_pair_share", in_specs=[_HBM] * n, out_specs=[_HBM] * n,
        out_shape=[_sds(g.shape, g.dtype) for g in gathered],
        input_output_aliases={p: p for p in range(n)},
        scratch_shapes=[pltpu.SemaphoreType.DMA((4 * n,)), pltpu.SemaphoreType.DMA((4 * n,)),
                        pltpu.SemaphoreType.DMA((n,)), pltpu.SemaphoreType.DMA((n,))],
    )(*gathered)


def _place_slot(arrs, slot, n_slots, name, out_dtype=None):
    n = len(arrs)

    def body(s_ref, *refs):
        for p in range(n):
            refs[n + p][...] = refs[p][...].astype(refs[n + p].dtype)

    def spec(a):
        tail = a.shape[-2:]
        return pl.BlockSpec((None,) + tail, lambda i, s_ref: (s_ref[0], 0, 0))

    in_specs = [spec(a) if a.ndim == 3 else pl.BlockSpec(a.shape, lambda i, s_ref: (0, 0)) for a in arrs]
    return pl.pallas_call(
        body, name=name,
        grid_spec=pltpu.PrefetchScalarGridSpec(num_scalar_prefetch=1, grid=(1,), in_specs=in_specs,
                                               out_specs=[spec(a) for a in arrs]),
        out_shape=[_in_hbm((n_slots,) + a.shape[-2:], out_dtype or a.dtype) for a in arrs],
        compiler_params=_cp(("arbitrary",)),
    )(slot, *[_pin(a) for a in arrs])


def _cols_from_blocks(w8, name):
    _, R, C = w8.shape

    def body(w_ref, o_ref):
        for e in range(8):
            o_ref[:, C * e:C * (e + 1)] = w_ref[e]

    return pl.pallas_call(body, name=name, out_shape=_sds((R, 8 * C), w8.dtype), compiler_params=_cp())(w8)


def _all_gather_small(srcs, name):
    n = len(srcs)
    vmem = pl.BlockSpec(memory_space=pltpu.VMEM)

    def body(*refs):
        src, out = refs[:n], refs[n:2 * n]
        send_sems, recv_sems = refs[2 * n:]
        x, y, c = _my_place()
        me = 4 * x + 2 * y + c
        flip = lambda v, f: 1 - v if f else v
        peers = [(flip(x, fx), flip(y, fy), flip(c, fc)) for fx in (0, 1) for fy in (0, 1) for fc in (0, 1)][1:]
        cps = []
        for p in range(n):
            out[p][me] = src[p][...]
            for k, peer in enumerate(peers):
                cps.append(pltpu.make_async_remote_copy(
                    src_ref=src[p], dst_ref=out[p].at[me], send_sem=send_sems.at[7 * p + k],
                    recv_sem=recv_sems.at[7 * p + k], device_id=peer, device_id_type=MESH))
        for cp in cps:
            cp.start()
        for cp in cps:
            cp.wait()

    return pl.pallas_call(
        body, name=name, in_specs=[vmem] * n, out_specs=[vmem] * n,
        out_shape=[_sds((8,) + a.shape, a.dtype) for a in srcs],
        scratch_shapes=[pltpu.SemaphoreType.DMA((7 * n,)), pltpu.SemaphoreType.DMA((7 * n,))],
    )(*srcs)


def _adaln_exchange(c, w_ada, b_part):
    vmem = pl.BlockSpec(memory_space=pltpu.VMEM)

    def body(c_ref, w_hbm, b_ref, call_ref, mod_ref, part_ref, w_ref, w_sem, send_sems, recv_sems):
        load_w = pltpu.make_async_copy(w_hbm, w_ref, w_sem.at[0])
        load_w.start()
        x, y, c = _my_place()
        me = 4 * x + 2 * y + c
        flip = lambda v, f: 1 - v if f else v
        peers = [(flip(x, fx), flip(y, fy), flip(c, fc)) for fx in (0, 1) for fy in (0, 1) for fc in (0, 1)][1:]
        call_ref[me] = c_ref[...]
        first = [pltpu.make_async_remote_copy(src_ref=c_ref, dst_ref=call_ref.at[me], send_sem=send_sems.at[k],
                                              recv_sem=recv_sems.at[k], device_id=peer, device_id_type=MESH)
                 for k, peer in enumerate(peers)]
        for cp in first:
            cp.start()
        for cp in first:
            cp.wait()
        c_all = jnp.concatenate([call_ref[j] for j in range(8)], axis=0)
        load_w.wait()
        part_ref[...] = _dot(c_all.astype(CDT), w_ref[...].astype(CDT), NN) + b_ref[...]
        mod_ref[me] = part_ref[pl.ds(me, 1), :]
        second = [pltpu.make_async_remote_copy(
            src_ref=part_ref.at[pl.ds(4 * px + 2 * py + pc, 1)], dst_ref=mod_ref.at[me], send_sem=send_sems.at[7 + k],
            recv_sem=recv_sems.at[7 + k], device_id=(px, py, pc), device_id_type=MESH)
            for k, (px, py, pc) in enumerate(peers)]
        for cp in second:
            cp.start()
        for cp in second:
            cp.wait()

    return pl.pallas_call(
        body, name="adaln_exchange", in_specs=[vmem, _HBM, vmem], out_specs=[vmem] * 2,
        out_shape=[_sds((8, 1, D), F32), _sds((8, 1, w_ada.shape[1]), F32)],
        scratch_shapes=[pltpu.VMEM((8, w_ada.shape[1]), F32), pltpu.VMEM(w_ada.shape, F32),
                        pltpu.SemaphoreType.DMA((1,)), pltpu.SemaphoreType.DMA((14,)), pltpu.SemaphoreType.DMA((14,))],
        compiler_params=_cp(),
    )(c, _pin(w_ada), b_part)


def _pair_add(dw, recv, axis, width, cidx, name):
    if axis == 0:
        tr = min(width, 256)
        C = dw.shape[1]
        grid = (4, width // tr)
        nb = width // tr
        own = pl.BlockSpec((tr, C), lambda q, i, c_ref: ((2 * q + c_ref[0]) * nb + i, 0))
        oth = pl.BlockSpec((None, tr, C), lambda q, i, c_ref: (q, i, 0))
    else:
        R = dw.shape[0]
        tr = min(R, 1024)
        grid = (4, R // tr)
        own = pl.BlockSpec((tr, width), lambda q, i, c_ref: (i, 2 * q + c_ref[0]))
        oth = pl.BlockSpec((None, tr, width), lambda q, i, c_ref: (q, i, 0))

    def body(c_ref, a_ref, b_ref, o_ref):
        o_ref[...] = (a_ref[...] + b_ref[...].astype(F32)).astype(o_ref.dtype)

    return pl.pallas_call(
        body, name=name,
        grid_spec=pltpu.PrefetchScalarGridSpec(num_scalar_prefetch=1, grid=grid, in_specs=[own, oth], out_specs=oth),
        out_shape=(_in_hbm if _is_big(dw) else _sds)(recv.shape, CDT), compiler_params=_cp(("parallel", "parallel")),
    )(cidx, *[_pin(a) if _is_big(dw) else a for a in (dw, recv)])


def _add_pairs(xs, ys, name):
    n = len(xs)

    def body(*refs):
        for p in range(n):
            refs[2 * n + p][...] = refs[p][...] + refs[n + p][...]

    return pl.pallas_call(body, name=name, out_shape=[_sds(a.shape, F32) for a in xs])(*xs, *ys)


def _adam_math(w, g, m, v):
    m = ADAM_B1 * m + (1.0 - ADAM_B1) * g
    v = ADAM_B2 * v + (1.0 - ADAM_B2) * (g * g)
    m_hat = m / (1.0 - ADAM_B1 ** ADAM_STEP)
    v_hat = v / (1.0 - ADAM_B2 ** ADAM_STEP)
    delta = -ADAM_LR * (m_hat / (jnp.sqrt(v_hat) + ADAM_EPS) + ADAM_WD * w)
    return delta, m, v


def _adam(gparts, w, m, v, name, fresh=False):
    P, R, C = gparts.shape
    in_hbm = fresh or _is_big(w)
    tr = R if R <= 256 else 256

    def body(g_ref, w_ref, m_ref, v_ref, go_ref, d_ref, mo_ref, vo_ref):
        g = g_ref[0].astype(F32)
        for i in range(1, P):
            g = g + g_ref[i].astype(F32)
        d, mn, vn = _adam_math(w_ref[...], g, m_ref[...], v_ref[...])
        go_ref[...] = g
        d_ref[...] = d
        mo_ref[...] = mn
        vo_ref[...] = vn

    t = pl.BlockSpec((tr, C), lambda i: (i, 0))
    return pl.pallas_call(
        body, name=name, grid=(R // tr,),
        in_specs=[pl.BlockSpec((P, tr, C), lambda i: (0, i, 0)), t, t, t], out_specs=[t, t, t, t],
        out_shape=[(_in_hbm if in_hbm else _sds)((R, C), F32)] * 4, compiler_params=_cp(("parallel",)),
    )(*[_pin(a) if in_hbm else a for a in (gparts, w, m, v)])


def _sum_parts(gparts, name):
    P, R, C = gparts.shape

    def body(g_ref, o_ref):
        g = g_ref[0]
        for i in range(1, P):
            g = g + g_ref[i]
        o_ref[...] = g

    return pl.pallas_call(body, name=name, out_shape=_sds((R, C), F32))(gparts)


def _adam_w_ada(c_t, dm, w, m, v):
    R, C = w.shape

    def body(c_ref, dm_ref, w_ref, m_ref, v_ref, go_ref, d_ref, mo_ref, vo_ref):
        g = c_ref[:, 0:1] * dm_ref[0:1, :]
        for b in range(1, 8):
            g = g + c_ref[:, b:b + 1] * dm_ref[b:b + 1, :]
        d, mn, vn = _adam_math(w_ref[...], g, m_ref[...], v_ref[...])
        go_ref[...] = g
        d_ref[...] = d
        mo_ref[...] = mn
        vo_ref[...] = vn

    return pl.pallas_call(body, name="adam_w_ada", out_shape=[_sds((R, C), F32)] * 4,
                          compiler_params=_cp())(c_t, dm, w, m, v)


def kernel(x, c, norm_g, w_ada, b_ada, w_in, pool_w, pool_scale, w_attn_br, w_pool_br, w_out, rel_bias, final_g, loss_target, m_norm_g, m_w_ada, m_b_ada, m_w_in, m_pool_w, m_pool_scale, m_w_attn_br, m_w_pool_br, m_w_out, m_rel_bias, m_final_g, v_norm_g, v_w_ada, v_b_ada, v_w_in, v_pool_w, v_pool_scale, v_w_attn_br, v_w_pool_br, v_w_out, v_rel_bias, v_final_g):
    S = x.shape[1]
    px, py, pc = _my_place()
    me = 4 * px + 2 * py + pc
    cidx = jnp.reshape(pc, (1,)).astype(jnp.int32)

    layout = [(1, 1024), (1, 128), (1, 128), (0, 128)]
    final_g2 = final_g.reshape(1, D)
    me1 = jnp.reshape(me, (1,)).astype(jnp.int32)

    b_part = lax.dynamic_slice(b_ada, (0, me * 384), (1, 384))
    c_all, mod_all = _adaln_exchange(c, w_ada[0], b_part)
    c_all = c_all.reshape(8, D)
    mod = mod_all.reshape(1, 3 * D)
    shift, scale, gate = mod[:, :D], mod[:, D:2 * D], mod[:, 2 * D:]

    gplan = _gather_chips_plan([(None, 1)] * 4)
    lands = _place_slot([w_in[0], w_attn_br[0], w_pool_br[0], w_out[0]], me1, 8, "place_own_shards", CDT)
    g_sems, _, lands, tok = _split_start("gather_chips_start", [], lands, 8, gplan, dep=mod_all)
    st = _local_bias_tables(_local_pre(x[0], shift, scale, norm_g, dep=tok), rel_bias, dep=tok)
    _, lands = _split_wait("gather_chips_wait", g_sems, [], lands, gplan,
                           [st["ht"]] + [b for pair in st["biases"] for b in pair])
    w_in_full, wab8, wpb8, wout8 = _pair_share(lands)
    wab, wpb = _cols_from_blocks(wab8, "w_attn_br_cols"), _cols_from_blocks(wpb8, "w_pool_br_cols")
    wout = wout8.reshape(D, D)

    st = _local_grads(st, loss_target[0], gate, w_in_full, pool_w[0], pool_scale, wab, wpb, wout, final_g2)

    early = [st["dpw"].reshape(AW, 128), st["dps"], st["gfg"], jnp.broadcast_to(st["loss"], (1, 128))]
    dws = [st["dw_in"], st["dw_ab"], st["dw_pb"], st["dw_out"]]
    sent = [st["dw_in_low"]] + dws[1:] + early
    pplan = _pair_plan(layout)
    pair_shapes = [(4, w, a.shape[1]) if ax == 0 else (4, a.shape[0], w) for a, (ax, w) in zip(dws, layout)]
    p_sems, sent, recv, tok = _split_start(
        "reduce_pair_start", sent,
        [lax.empty(s, a.dtype) for s, a in zip(pair_shapes + [a.shape for a in early], sent)], 20, pplan)
    d_rel_bias = _local_bias_grads(st, _BIAS_GRADS_EARLY, dep=tok)
    sent, recv = _split_wait("reduce_pair_wait", p_sems, sent, recv, pplan, d_rel_bias)
    parts = [_pair_add(dw, r, ax, w, cidx, f"pair_add{i}")
             for i, (dw, r, (ax, w)) in enumerate(zip([dws[0]] + sent[1:4], recv[:4], layout))]
    parts += _add_pairs(sent[4:], recv[4:], "pair_add_smalls")

    slot1 = jnp.reshape(2 * px + py, (1,)).astype(jnp.int32)
    cplan = _chips_plan(4)
    c_sems, parts, lands, tok = _split_start("reduce_chips_start", parts,
                                             _place_slot(parts, slot1, 4, "place_own_partials"), 24, cplan)
    d_rel_bias = d_rel_bias + _local_bias_grads(st, _BIAS_GRADS_LATE, dep=tok)
    grad_x, dmod, d_norm_g = _local_input_grads(st, w_in_full, norm_g, scale, dep=tok)

    dmod_all, ng_all, rb_all = _all_gather_small([dmod, d_norm_g, d_rel_bias], "gather_late")
    o_norm_g = _adam(ng_all, norm_g, m_norm_g, v_norm_g, "adam_norm_g", fresh=True)
    o_rel_bias_t = _adam(rb_all, rel_bias.T, m_rel_bias.T, v_rel_bias.T, "adam_rel_bias", fresh=True)
    o_rel_bias = [o.T for o in o_rel_bias_t]
    o_b_ada = _adam(dmod_all, b_ada, m_b_ada, v_b_ada, "adam_b_ada", fresh=True)
    dm_mine = lax.dynamic_slice(dmod_all.reshape(8, 3 * D), (0, me * 384), (8, 384))
    o_w_ada = [o[None] for o in _adam_w_ada(c_all.T, dm_mine, w_ada[0], m_w_ada[0], v_w_ada[0])]

    _, (g_in, g_ab, g_pb, g_out, g_pw, g_ps, g_fg, g_loss) = _split_wait(
        "reduce_chips_wait", c_sems, parts, lands, cplan, [o_w_ada[0], o_b_ada[0], o_norm_g[0], o_rel_bias_t[0]])
    o_w_in = [o[None] for o in _adam(g_in, w_in[0], m_w_in[0], v_w_in[0], "adam_w_in")]
    o_w_ab = [o[None] for o in _adam(g_ab, w_attn_br[0], m_w_attn_br[0], v_w_attn_br[0], "adam_w_attn_br")]
    o_w_pb = [o[None] for o in _adam(g_pb, w_pool_br[0], m_w_pool_br[0], v_w_pool_br[0], "adam_w_pool_br")]
    o_w_out = [o[None] for o in _adam(g_out, w_out[0], m_w_out[0], v_w_out[0], "adam_w_out")]
    o_pool_w = [o.reshape(pool_w.shape) for o in _adam(g_pw, pool_w.reshape(AW, 128), m_pool_w.reshape(AW, 128),
                                                       v_pool_w.reshape(AW, 128), "adam_pool_w")]
    o_pool_scale = _adam(g_ps, pool_scale, m_pool_scale, v_pool_scale, "adam_pool_scale")
    o_final_g = [o.reshape(D) for o in _adam(g_fg, final_g2, m_final_g.reshape(1, D), v_final_g.reshape(1, D),
                                             "adam_final_g")]
    loss = _sum_parts(g_loss, "sum_loss")[0, 0]

    per_w = [o_norm_g, o_w_ada, o_b_ada, o_w_in, o_pool_w, o_pool_scale, o_w_ab, o_w_pb, o_w_out, o_rel_bias, o_final_g]
    outs = [loss, grad_x[None]]
    for k in range(4):
        outs += [o[k] for o in per_w]
    return tuple(outs)
```
